```python
import jax, jax.numpy as jnp
from jax import lax
import numpy as np

D_MODEL = 1024
BATCH = 8
SEQ = 8192
DEPTH = 2

N_META = 16
CHUNK = 64
PAD = CHUNK - N_META
D_FF = 2816
EPS = 1e-6
N_MIXERS = 2
N_RET = (DEPTH + 1) // 2
N_GLA = DEPTH // 2

RET_HEADS = 4
RET_DK = D_MODEL // RET_HEADS
RET_DV = 2 * D_MODEL // RET_HEADS
RET_IN = 2 * D_MODEL + 4 * D_MODEL
ROPE_BASE = 10000.0

GLA_HEADS = 4
GLA_DK = D_MODEL // 2 // GLA_HEADS
GLA_DV = D_MODEL // GLA_HEADS
GLA_RANK = 16
GLA_TAU = 16.0
GLA_HK = GLA_HEADS * GLA_DK
GLA_HV = GLA_HEADS * GLA_DV
GLA_IN = 2 * GLA_HK + 2 * GLA_HV + GLA_RANK

kernel_name = "hybrid_retnet_gla_macaron_meta"


def rmsnorm(x, g):
    xf = x.astype(jnp.float32)
    y = xf * lax.rsqrt(jnp.mean(xf * xf, axis=-1, keepdims=True) + EPS)
    return (y * g).astype(x.dtype)


def swiglu(x, w_in, w_out):
    gate, up = jnp.split(x @ w_in, 2, axis=-1)
    return (jax.nn.silu(gate) * up) @ w_out


def to_chunks(t):
    b, T, h, d = t.shape
    n = (T + PAD) // CHUNK
    t = jnp.pad(t, ((0, 0), (PAD, 0), (0, 0), (0, 0)))
    return t.reshape(b, n, CHUNK, h, d).transpose(1, 0, 3, 2, 4)


def from_chunks(t):
    n, b, h, c, d = t.shape
    return t.transpose(1, 0, 3, 2, 4).reshape(b, n * c, h, d)[:, PAD:]


def rotary(t, pos):
    half = t.shape[-1] // 2
    inv = 1.0 / (ROPE_BASE ** jnp.linspace(0.0, 1.0, half, dtype=jnp.float32))
    ang = pos.astype(jnp.float32)[:, None] * inv[None, :]
    cos = jnp.cos(ang)[None, :, None, :].astype(t.dtype)
    sin = jnp.sin(ang)[None, :, None, :].astype(t.dtype)
    t1, t2 = t[..., :half], t[..., half:]
    return jnp.concatenate([t1 * cos - t2 * sin, t1 * sin + t2 * cos], axis=-1)


def retention(h, w_in, head_norm, w_out):
    b, T, _ = h.shape
    q, k, v, g = jnp.split(h @ w_in, [D_MODEL, 2 * D_MODEL, 4 * D_MODEL], axis=-1)
    pos = jnp.arange(T)
    q = rotary(q.reshape(b, T, RET_HEADS, RET_DK), pos)
    k = rotary(k.reshape(b, T, RET_HEADS, RET_DK), pos) * (RET_DK ** -0.5)
    v = v.reshape(b, T, RET_HEADS, RET_DV)

    log_gamma = jnp.log1p(-2.0 ** (-5.0 - jnp.arange(RET_HEADS, dtype=jnp.float32)))
    idx = jnp.arange(CHUNK, dtype=jnp.float32)
    rel = idx[:, None] - idx[None, :]
    decay_intra = jnp.where(rel >= 0, jnp.exp(log_gamma[:, None, None] * jnp.maximum(rel, 0.0)), 0.0)
    decay_q = jnp.exp(log_gamma[:, None] * (idx + 1.0))[..., None]
    decay_k = jnp.exp(log_gamma[:, None] * (CHUNK - 1.0 - idx))[..., None]
    decay_chunk = jnp.exp(log_gamma * CHUNK)[:, None, None]

    def step(S, inp):
        qi, ki, vi = inp
        scores = jnp.einsum('bhid,bhjd->bhij', qi, ki) * decay_intra
        o = (jnp.einsum('bhij,bhjv->bhiv', scores, vi)
             + jnp.einsum('bhid,bhdv->bhiv', qi * decay_q, S))
        S = S * decay_chunk + jnp.einsum('bhjd,bhjv->bhdv', ki * decay_k, vi)
        return S, o

    S0 = jnp.zeros((b, RET_HEADS, RET_DK, RET_DV), jnp.float32)
    _, o = lax.scan(step, S0, (to_chunks(q), to_chunks(k), to_chunks(v)))
    o = rmsnorm(from_chunks(o), head_norm)
    o = o.reshape(b, T, RET_HEADS * RET_DV) * jax.nn.silu(g)
    return o @ w_out


def gla(h, w_in, w_gate, b_gate, head_norm, w_out):
    b, T, _ = h.shape
    q, k, v, g, z = jnp.split(h @ w_in, [GLA_HK, 2 * GLA_HK, 2 * GLA_HK + GLA_HV, 2 * GLA_HK + 2 * GLA_HV], axis=-1)
    q = q.reshape(b, T, GLA_HEADS, GLA_DK) * (GLA_DK ** -0.5)
    k = k.reshape(b, T, GLA_HEADS, GLA_DK)
    v = v.reshape(b, T, GLA_HEADS, GLA_DV)
    log_a = jax.nn.log_sigmoid((z @ w_gate + b_gate).astype(jnp.float32)) / GLA_TAU
    log_a = log_a.reshape(b, T, GLA_HEADS, GLA_DK)
    causal = jnp.tril(jnp.ones((CHUNK, CHUNK), dtype=bool))[:, :, None]

    def step(S, inp):
        qi, ki, vi, ai = inp
        bcum = jnp.cumsum(ai, axis=2)
        diff = bcum[:, :, :, None, :] - bcum[:, :, None, :, :]
        dec = jnp.exp(jnp.where(causal, diff, -jnp.inf))
        scores = jnp.einsum('bhid,bhijd,bhjd->bhij', qi, dec, ki)
        o = (jnp.einsum('bhij,bhjv->bhiv', scores, vi)
             + jnp.einsum('bhid,bhdv->bhiv', qi * jnp.exp(bcum), S))
        btot = bcum[:, :, -1:, :]
        S = (S * jnp.exp(btot)[:, :, 0, :, None]
             + jnp.einsum('bhjd,bhjv->bhdv', ki * jnp.exp(btot - bcum), vi))
        return S, o

    S0 = jnp.zeros((b, GLA_HEADS, GLA_DK, GLA_DV), jnp.float32)
    _, o = lax.scan(step, S0, (to_chunks(q), to_chunks(k), to_chunks(v), to_chunks(log_a)))
    o = rmsnorm(from_chunks(o), head_norm)
    o = o.reshape(b, T, GLA_HV) * jax.nn.silu(g)
    return o @ w_out


def _fwd_setup_inputs(seed: int = 0) -> dict:
    key = jax.random.key(seed)
    ks = jax.random.split(key, 20)
    nrm = lambda k, shape, fan_in: jax.random.normal(k, shape, jnp.float32) * (fan_in ** -0.5)
    gain = lambda k, shape: 1.0 + 0.01 * jax.random.normal(k, shape, jnp.float32)
    return {
        "x": jax.random.normal(ks[0], (BATCH, SEQ, D_MODEL), jnp.float32),
        "meta_tokens": jax.random.normal(ks[1], (N_META, D_MODEL), jnp.float32),
        "norm_ffn1": gain(ks[2], (DEPTH, D_MODEL)),
        "ffn1_w_in": nrm(ks[3], (DEPTH, D_MODEL, 2 * D_FF), D_MODEL),
        "ffn1_w_out": nrm(ks[4], (DEPTH, D_FF, D_MODEL), D_FF),
        "norm_mix": gain(ks[5], (DEPTH, D_MODEL)),
        "norm_ffn2": gain(ks[6], (DEPTH, D_MODEL)),
        "ffn2_w_in": nrm(ks[7], (DEPTH, D_MODEL, 2 * D_FF), D_MODEL),
        "ffn2_w_out": nrm(ks[8], (DEPTH, D_FF, D_MODEL), D_FF),
        "ret_w_in": nrm(ks[9], (N_RET, D_MODEL, RET_IN), D_MODEL),
        "ret_head_norm": gain(ks[10], (N_RET, RET_HEADS, RET_DV)),
        "ret_w_out": nrm(ks[11], (N_RET, RET_HEADS * RET_DV, D_MODEL), RET_HEADS * RET_DV),
        "gla_w_in": nrm(ks[12], (N_GLA, D_MODEL, GLA_IN), D_MODEL),
        "gla_w_gate": nrm(ks[13], (N_GLA, GLA_RANK, GLA_HK), GLA_RANK),
        "gla_b_gate": 0.1 * jax.random.normal(ks[14], (N_GLA, GLA_HK), jnp.float32),
        "gla_head_norm": gain(ks[15], (N_GLA, GLA_HEADS, GLA_DV)),
        "gla_w_out": nrm(ks[16], (N_GLA, GLA_HV, D_MODEL), GLA_HV),
        "final_norm": gain(ks[17], (D_MODEL,)),
    }


def _fwd_reference(x, meta_tokens, norm_ffn1, ffn1_w_in, ffn1_w_out, norm_mix, norm_ffn2,
              ffn2_w_in, ffn2_w_out, ret_w_in, ret_head_norm, ret_w_out,
              gla_w_in, gla_w_gate, gla_b_gate, gla_head_norm, gla_w_out, final_norm):
    b = x.shape[0]
    meta = jnp.broadcast_to(meta_tokens[None].astype(x.dtype), (b, N_META, D_MODEL))
    h = jnp.concatenate([meta, x], axis=1)
    for i in range(DEPTH):
        j = i // N_MIXERS
        h = h + 0.5 * swiglu(rmsnorm(h, norm_ffn1[i]), ffn1_w_in[i], ffn1_w_out[i])
        hn = rmsnorm(h, norm_mix[i])
        if i % N_MIXERS == 0:
            mix = retention(hn, ret_w_in[j], ret_head_norm[j], ret_w_out[j])
        else:
            mix = gla(hn, gla_w_in[j], gla_w_gate[j], gla_b_gate[j], gla_head_norm[j], gla_w_out[j])
        h = h + mix
        h = h + 0.5 * swiglu(rmsnorm(h, norm_ffn2[i]), ffn2_w_in[i], ffn2_w_out[i])
    h = rmsnorm(h, final_norm)
    return h[:, N_META:]


import jax as _jax
import jax.numpy as _jnp

TWIN_FORMAT = 'train_step'
FWD_PARAMS = ['x', 'meta_tokens', 'norm_ffn1', 'ffn1_w_in', 'ffn1_w_out', 'norm_mix', 'norm_ffn2', 'ffn2_w_in', 'ffn2_w_out', 'ret_w_in', 'ret_head_norm', 'ret_w_out', 'gla_w_in', 'gla_w_gate', 'gla_b_gate', 'gla_head_norm', 'gla_w_out', 'final_norm']
TWIN_WEIGHTS = ['meta_tokens', 'norm_ffn1', 'ffn1_w_in', 'ffn1_w_out', 'norm_mix', 'norm_ffn2', 'ffn2_w_in', 'ffn2_w_out', 'ret_w_in', 'ret_head_norm', 'ret_w_out', 'gla_w_in', 'gla_w_gate', 'gla_b_gate', 'gla_head_norm', 'gla_w_out', 'final_norm']
TWIN_DIFF_INPUT = 'x'
TWIN_INPUTS = ['x', 'meta_tokens', 'norm_ffn1', 'ffn1_w_in', 'ffn1_w_out', 'norm_mix', 'norm_ffn2', 'ffn2_w_in', 'ffn2_w_out', 'ret_w_in', 'ret_head_norm', 'ret_w_out', 'gla_w_in', 'gla_w_gate', 'gla_b_gate', 'gla_head_norm', 'gla_w_out', 'final_norm', 'loss_target', 'm_meta_tokens', 'm_norm_ffn1', 'm_ffn1_w_in', 'm_ffn1_w_out', 'm_norm_mix', 'm_norm_ffn2', 'm_ffn2_w_in', 'm_ffn2_w_out', 'm_ret_w_in', 'm_ret_head_norm', 'm_ret_w_out', 'm_gla_w_in', 'm_gla_w_gate', 'm_gla_b_gate', 'm_gla_head_norm', 'm_gla_w_out', 'm_final_norm', 'v_meta_tokens', 'v_norm_ffn1', 'v_ffn1_w_in', 'v_ffn1_w_out', 'v_norm_mix', 'v_norm_ffn2', 'v_ffn2_w_in', 'v_ffn2_w_out', 'v_ret_w_in', 'v_ret_head_norm', 'v_ret_w_out', 'v_gla_w_in', 'v_gla_w_gate', 'v_gla_b_gate', 'v_gla_head_norm', 'v_gla_w_out', 'v_final_norm']
TWIN_OUTPUTS = ['loss', 'grad_x', 'grad_meta_tokens', 'grad_norm_ffn1', 'grad_ffn1_w_in', 'grad_ffn1_w_out', 'grad_norm_mix', 'grad_norm_ffn2', 'grad_ffn2_w_in', 'grad_ffn2_w_out', 'grad_ret_w_in', 'grad_ret_head_norm', 'grad_ret_w_out', 'grad_gla_w_in', 'grad_gla_w_gate', 'grad_gla_b_gate', 'grad_gla_head_norm', 'grad_gla_w_out', 'grad_final_norm', 'delta_meta_tokens', 'delta_norm_ffn1', 'delta_ffn1_w_in', 'delta_ffn1_w_out', 'delta_norm_mix', 'delta_norm_ffn2', 'delta_ffn2_w_in', 'delta_ffn2_w_out', 'delta_ret_w_in', 'delta_ret_head_norm', 'delta_ret_w_out', 'delta_gla_w_in', 'delta_gla_w_gate', 'delta_gla_b_gate', 'delta_gla_head_norm', 'delta_gla_w_out', 'delta_final_norm', 'new_m_meta_tokens', 'new_m_norm_ffn1', 'new_m_ffn1_w_in', 'new_m_ffn1_w_out', 'new_m_norm_mix', 'new_m_norm_ffn2', 'new_m_ffn2_w_in', 'new_m_ffn2_w_out', 'new_m_ret_w_in', 'new_m_ret_head_norm', 'new_m_ret_w_out', 'new_m_gla_w_in', 'new_m_gla_w_gate', 'new_m_gla_b_gate', 'new_m_gla_head_norm', 'new_m_gla_w_out', 'new_m_final_norm', 'new_v_meta_tokens', 'new_v_norm_ffn1', 'new_v_ffn1_w_in', 'new_v_ffn1_w_out', 'new_v_norm_mix', 'new_v_norm_ffn2', 'new_v_ffn2_w_in', 'new_v_ffn2_w_out', 'new_v_ret_w_in', 'new_v_ret_head_norm', 'new_v_ret_w_out', 'new_v_gla_w_in', 'new_v_gla_w_gate', 'new_v_gla_b_gate', 'new_v_gla_head_norm', 'new_v_gla_w_out', 'new_v_final_norm']
TWIN_LEAF_KINDS = {'loss': 'loss', 'grad_x': 'grad_x', 'grad_meta_tokens': 'grad_w', 'grad_norm_ffn1': 'grad_w', 'grad_ffn1_w_in': 'grad_w', 'grad_ffn1_w_out': 'grad_w', 'grad_norm_mix': 'grad_w', 'grad_norm_ffn2': 'grad_w', 'grad_ffn2_w_in': 'grad_w', 'grad_ffn2_w_out': 'grad_w', 'grad_ret_w_in': 'grad_w', 'grad_ret_head_norm': 'grad_w', 'grad_ret_w_out': 'grad_w', 'grad_gla_w_in': 'grad_w', 'grad_gla_w_gate': 'grad_w', 'grad_gla_b_gate': 'grad_w', 'grad_gla_head_norm': 'grad_w', 'grad_gla_w_out': 'grad_w', 'grad_final_norm': 'grad_w', 'delta_meta_tokens': 'delta_w', 'delta_norm_ffn1': 'delta_w', 'delta_ffn1_w_in': 'delta_w', 'delta_ffn1_w_out': 'delta_w', 'delta_norm_mix': 'delta_w', 'delta_norm_ffn2': 'delta_w', 'delta_ffn2_w_in': 'delta_w', 'delta_ffn2_w_out': 'delta_w', 'delta_ret_w_in': 'delta_w', 'delta_ret_head_norm': 'delta_w', 'delta_ret_w_out': 'delta_w', 'delta_gla_w_in': 'delta_w', 'delta_gla_w_gate': 'delta_w', 'delta_gla_b_gate': 'delta_w', 'delta_gla_head_norm': 'delta_w', 'delta_gla_w_out': 'delta_w', 'delta_final_norm': 'delta_w', 'new_m_meta_tokens': 'new_m', 'new_m_norm_ffn1': 'new_m', 'new_m_ffn1_w_in': 'new_m', 'new_m_ffn1_w_out': 'new_m', 'new_m_norm_mix': 'new_m', 'new_m_norm_ffn2': 'new_m', 'new_m_ffn2_w_in': 'new_m', 'new_m_ffn2_w_out': 'new_m', 'new_m_ret_w_in': 'new_m', 'new_m_ret_head_norm': 'new_m', 'new_m_ret_w_out': 'new_m', 'new_m_gla_w_in': 'new_m', 'new_m_gla_w_gate': 'new_m', 'new_m_gla_b_gate': 'new_m', 'new_m_gla_head_norm': 'new_m', 'new_m_gla_w_out': 'new_m', 'new_m_final_norm': 'new_m', 'new_v_meta_tokens': 'new_v', 'new_v_norm_ffn1': 'new_v', 'new_v_ffn1_w_in': 'new_v', 'new_v_ffn1_w_out': 'new_v', 'new_v_norm_mix': 'new_v', 'new_v_norm_ffn2': 'new_v', 'new_v_ffn2_w_in': 'new_v', 'new_v_ffn2_w_out': 'new_v', 'new_v_ret_w_in': 'new_v', 'new_v_ret_head_norm': 'new_v', 'new_v_ret_w_out': 'new_v', 'new_v_gla_w_in': 'new_v', 'new_v_gla_w_gate': 'new_v', 'new_v_gla_b_gate': 'new_v', 'new_v_gla_head_norm': 'new_v', 'new_v_gla_w_out': 'new_v', 'new_v_final_norm': 'new_v'}


def _forward(args):
    return _fwd_reference(*[args[k] for k in FWD_PARAMS])


def _output_shape():
    def fwd():
        inp = _fwd_setup_inputs(0)
        return _fwd_reference(*[inp[k] for k in FWD_PARAMS])
    out = _jax.eval_shape(fwd)
    return out.shape, out.dtype

N_MICROBATCH = 1
ADAM_LR = 0.001
ADAM_B1 = 0.9
ADAM_B2 = 0.999
ADAM_EPS = 1e-08
ADAM_WD = 0.01
ADAM_STEP = 10
PER_EXAMPLE_BATCH_AXIS = {'x': 0, 'loss_target': 0}
SHARED_INPUTS = []
_WEIGHT_DTYPES = {'meta_tokens': _jnp.float32, 'norm_ffn1': _jnp.float32, 'ffn1_w_in': _jnp.float32, 'ffn1_w_out': _jnp.float32, 'norm_mix': _jnp.float32, 'norm_ffn2': _jnp.float32, 'ffn2_w_in': _jnp.float32, 'ffn2_w_out': _jnp.float32, 'ret_w_in': _jnp.float32, 'ret_head_norm': _jnp.float32, 'ret_w_out': _jnp.float32, 'gla_w_in': _jnp.float32, 'gla_w_gate': _jnp.float32, 'gla_b_gate': _jnp.float32, 'gla_head_norm': _jnp.float32, 'gla_w_out': _jnp.float32, 'final_norm': _jnp.float32}
MOMENT_SCALE = {'meta_tokens': 2.143368e-02, 'norm_ffn1': 1.538178e-01, 'ffn1_w_in': 6.440216e-02, 'ffn1_w_out': 1.050401e-01, 'norm_mix': 2.954671e-01, 'norm_ffn2': 1.011757e-01, 'ffn2_w_in': 4.112419e-02, 'ffn2_w_out': 6.716155e-02, 'ret_w_in': 1.395036e-01, 'ret_head_norm': 1.185941e-01, 'ret_w_out': 1.686307e-01, 'gla_w_in': 1.282828e-01, 'gla_w_gate': 1.721996e-02, 'gla_b_gate': 7.720213e-02, 'gla_head_norm': 1.085900e-01, 'gla_w_out': 1.087064e-01, 'final_norm': 6.395483e+01}


def _to_microbatches(a, axis):
    t = _jnp.moveaxis(a, axis, 0)
    t = t.reshape((N_MICROBATCH, t.shape[0] // N_MICROBATCH) + t.shape[1:])
    return _jnp.moveaxis(t, 1, axis + 1)


def setup_inputs(seed: int = 0) -> dict:
    inp = _fwd_setup_inputs(seed)
    key = _jax.random.fold_in(_jax.random.key(seed), 7919)
    shape, _ = _output_shape()
    out = dict(inp)
    out["loss_target"] = _jax.random.normal(_jax.random.fold_in(key, 0), shape, _jnp.float32)
    for i, name in enumerate(TWIN_WEIGHTS):
        w = inp[name].astype(_jnp.float32)
        if MOMENT_SCALE is None:
            s = _jnp.sqrt(_jnp.mean(_jnp.square(w)) + 1e-30)
        else:
            s = MOMENT_SCALE[name]
        km, kv = _jax.random.split(_jax.random.fold_in(key, i + 1))
        out[name] = w
        out["m_" + name] = s * _jax.random.normal(km, w.shape, _jnp.float32)
        out["v_" + name] = (s * s) * _jax.random.uniform(kv, w.shape, _jnp.float32, 0.5, 1.5)
    if N_MICROBATCH > 1:
        for name, axis in PER_EXAMPLE_BATCH_AXIS.items():
            out[name] = _to_microbatches(out[name], axis)
    return {'x': out['x'], 'meta_tokens': out['meta_tokens'], 'norm_ffn1': out['norm_ffn1'], 'ffn1_w_in': out['ffn1_w_in'], 'ffn1_w_out': out['ffn1_w_out'], 'norm_mix': out['norm_mix'], 'norm_ffn2': out['norm_ffn2'], 'ffn2_w_in': out['ffn2_w_in'], 'ffn2_w_out': out['ffn2_w_out'], 'ret_w_in': out['ret_w_in'], 'ret_head_norm': out['ret_head_norm'], 'ret_w_out': out['ret_w_out'], 'gla_w_in': out['gla_w_in'], 'gla_w_gate': out['gla_w_gate'], 'gla_b_gate': out['gla_b_gate'], 'gla_head_norm': out['gla_head_norm'], 'gla_w_out': out['gla_w_out'], 'final_norm': out['final_norm'], 'loss_target': out['loss_target'], 'm_meta_tokens': out['m_meta_tokens'], 'm_norm_ffn1': out['m_norm_ffn1'], 'm_ffn1_w_in': out['m_ffn1_w_in'], 'm_ffn1_w_out': out['m_ffn1_w_out'], 'm_norm_mix': out['m_norm_mix'], 'm_norm_ffn2': out['m_norm_ffn2'], 'm_ffn2_w_in': out['m_ffn2_w_in'], 'm_ffn2_w_out': out['m_ffn2_w_out'], 'm_ret_w_in': out['m_ret_w_in'], 'm_ret_head_norm': out['m_ret_head_norm'], 'm_ret_w_out': out['m_ret_w_out'], 'm_gla_w_in': out['m_gla_w_in'], 'm_gla_w_gate': out['m_gla_w_gate'], 'm_gla_b_gate': out['m_gla_b_gate'], 'm_gla_head_norm': out['m_gla_head_norm'], 'm_gla_w_out': out['m_gla_w_out'], 'm_final_norm': out['m_final_norm'], 'v_meta_tokens': out['v_meta_tokens'], 'v_norm_ffn1': out['v_norm_ffn1'], 'v_ffn1_w_in': out['v_ffn1_w_in'], 'v_ffn1_w_out': out['v_ffn1_w_out'], 'v_norm_mix': out['v_norm_mix'], 'v_norm_ffn2': out['v_norm_ffn2'], 'v_ffn2_w_in': out['v_ffn2_w_in'], 'v_ffn2_w_out': out['v_ffn2_w_out'], 'v_ret_w_in': out['v_ret_w_in'], 'v_ret_head_norm': out['v_ret_head_norm'], 'v_ret_w_out': out['v_ret_w_out'], 'v_gla_w_in': out['v_gla_w_in'], 'v_gla_w_gate': out['v_gla_w_gate'], 'v_gla_b_gate': out['v_gla_b_gate'], 'v_gla_head_norm': out['v_gla_head_norm'], 'v_gla_w_out': out['v_gla_w_out'], 'v_final_norm': out['v_final_norm']}


def _loss(weights, diff, rest, loss_target):
    with _jax.named_scope("forward"):
        args = {**rest, TWIN_DIFF_INPUT: diff, **{k: w.astype(_WEIGHT_DTYPES[k]) for k, w in weights.items()}}
        y = _forward(args)
    with _jax.named_scope("loss_head"):
        err = _jnp.square(y.astype(_jnp.float32) - loss_target)
        return 0.5 * _jnp.sum(_jnp.mean(err, axis=-1)) if err.ndim else 0.5 * err


def _adamw(w, g, m, v):
    m = ADAM_B1 * m + (1.0 - ADAM_B1) * g
    v = ADAM_B2 * v + (1.0 - ADAM_B2) * _jnp.square(g)
    m_hat = m / (1.0 - ADAM_B1 ** ADAM_STEP)
    v_hat = v / (1.0 - ADAM_B2 ** ADAM_STEP)
    delta = -ADAM_LR * (m_hat / (_jnp.sqrt(v_hat) + ADAM_EPS) + ADAM_WD * w)
    return delta, m, v


def reference(x, meta_tokens, norm_ffn1, ffn1_w_in, ffn1_w_out, norm_mix, norm_ffn2, ffn2_w_in, ffn2_w_out, ret_w_in, ret_head_norm, ret_w_out, gla_w_in, gla_w_gate, gla_b_gate, gla_head_norm, gla_w_out, final_norm, loss_target, m_meta_tokens, m_norm_ffn1, m_ffn1_w_in, m_ffn1_w_out, m_norm_mix, m_norm_ffn2, m_ffn2_w_in, m_ffn2_w_out, m_ret_w_in, m_ret_head_norm, m_ret_w_out, m_gla_w_in, m_gla_w_gate, m_gla_b_gate, m_gla_head_norm, m_gla_w_out, m_final_norm, v_meta_tokens, v_norm_ffn1, v_ffn1_w_in, v_ffn1_w_out, v_norm_mix, v_norm_ffn2, v_ffn2_w_in, v_ffn2_w_out, v_ret_w_in, v_ret_head_norm, v_ret_w_out, v_gla_w_in, v_gla_w_gate, v_gla_b_gate, v_gla_head_norm, v_gla_w_out, v_final_norm):
    given = dict(x=x, meta_tokens=meta_tokens, norm_ffn1=norm_ffn1, ffn1_w_in=ffn1_w_in, ffn1_w_out=ffn1_w_out, norm_mix=norm_mix, norm_ffn2=norm_ffn2, ffn2_w_in=ffn2_w_in, ffn2_w_out=ffn2_w_out, ret_w_in=ret_w_in, ret_head_norm=ret_head_norm, ret_w_out=ret_w_out, gla_w_in=gla_w_in, gla_w_gate=gla_w_gate, gla_b_gate=gla_b_gate, gla_head_norm=gla_head_norm, gla_w_out=gla_w_out, final_norm=final_norm, loss_target=loss_target, m_meta_tokens=m_meta_tokens, m_norm_ffn1=m_norm_ffn1, m_ffn1_w_in=m_ffn1_w_in, m_ffn1_w_out=m_ffn1_w_out, m_norm_mix=m_norm_mix, m_norm_ffn2=m_norm_ffn2, m_ffn2_w_in=m_ffn2_w_in, m_ffn2_w_out=m_ffn2_w_out, m_ret_w_in=m_ret_w_in, m_ret_head_norm=m_ret_head_norm, m_ret_w_out=m_ret_w_out, m_gla_w_in=m_gla_w_in, m_gla_w_gate=m_gla_w_gate, m_gla_b_gate=m_gla_b_gate, m_gla_head_norm=m_gla_head_norm, m_gla_w_out=m_gla_w_out, m_final_norm=m_final_norm, v_meta_tokens=v_meta_tokens, v_norm_ffn1=v_norm_ffn1, v_ffn1_w_in=v_ffn1_w_in, v_ffn1_w_out=v_ffn1_w_out, v_norm_mix=v_norm_mix, v_norm_ffn2=v_norm_ffn2, v_ffn2_w_in=v_ffn2_w_in, v_ffn2_w_out=v_ffn2_w_out, v_ret_w_in=v_ret_w_in, v_ret_head_norm=v_ret_head_norm, v_ret_w_out=v_ret_w_out, v_gla_w_in=v_gla_w_in, v_gla_w_gate=v_gla_w_gate, v_gla_b_gate=v_gla_b_gate, v_gla_head_norm=v_gla_head_norm, v_gla_w_out=v_gla_w_out, v_final_norm=v_final_norm)
    weights = {n: given[n] for n in TWIN_WEIGHTS}
    shared = {n: given[n] for n in SHARED_INPUTS}
    per_example = {n: given[n] for n in ['x']}
    grad_fn = _jax.value_and_grad(_loss, argnums=(0, 1))

    def one_microbatch(ex, loss_target):
        ex = dict(ex)
        diff = ex.pop(TWIN_DIFF_INPUT)
        return grad_fn(weights, diff, {**shared, **ex}, loss_target)

    if N_MICROBATCH == 1:
        loss, (grad_w, grad_x) = one_microbatch(per_example, given["loss_target"])
    else:
        def body(carry, xs):
            loss_sum, grad_sum = carry
            l_k, (gw_k, gx_k) = one_microbatch(xs[0], xs[1])
            with _jax.named_scope("update"):
                return (loss_sum + l_k, _jax.tree.map(_jnp.add, grad_sum, gw_k)), gx_k

        init = (_jnp.zeros((), _jnp.float32), _jax.tree.map(_jnp.zeros_like, weights))
        (loss, grad_w), grad_x = _jax.lax.scan(body, init, (per_example, given["loss_target"]))
    with _jax.named_scope("update"):
        delta_w, new_m, new_v = {}, {}, {}
        for n in TWIN_WEIGHTS:
            delta_w[n], new_m[n], new_v[n] = _adamw(weights[n], grad_w[n], given["m_" + n], given["v_" + n])
    return (loss, grad_x, *[grad_w[n] for n in TWIN_WEIGHTS], *[delta_w[n] for n in TWIN_WEIGHTS],
            *[new_m[n] for n in TWIN_WEIGHTS], *[new_v[n] for n in TWIN_WEIGHTS])
```

```python
import functools

import jax
import jax.numpy as jnp
from jax import lax
from jax.experimental import pallas as pl
from jax.experimental.pallas import tpu as pltpu

F32, BF16 = jnp.float32, jnp.bfloat16
MESH = pl.DeviceIdType.MESH

D = 1024
N_META = 16
CHUNK = 64
FRONT = 256
D_FF = 2816
EPS = 1e-6
HEADS = 4
RET_DK, RET_DV = 256, 512
GLA_DK, GLA_DV = 128, 256
GLA_RANK = 16
GLA_TAU = 16.0
GLA_U = 3200
ROPE_BASE = 10000.0
N_CHIPS = 4
N_DEV = 8

ADAM_LR, ADAM_B1, ADAM_B2, ADAM_EPS, ADAM_WD, ADAM_STEP = 0.001, 0.9, 0.999, 1e-08, 0.01, 10

VMEM_LIMIT_BYTES = 56 * 1024 * 1024
TM = 768
TM_SMALL = 256


def _cp(n_axes):
    return pltpu.CompilerParams(dimension_semantics=("arbitrary",) * n_axes, vmem_limit_bytes=VMEM_LIMIT_BYTES)


def _dg(a, b, ca, cb):
    return lax.dot_general(a.astype(BF16), b.astype(BF16), (((ca,), (cb,)), ((), ())), preferred_element_type=F32)


@jax.custom_vjp
def _nn(a, b):
    return _dg(a, b, 1, 0)


@jax.custom_vjp
def _nt(a, b):
    return _dg(a, b, 1, 1)


@jax.custom_vjp
def _tn(a, b):
    return _dg(a, b, 0, 0)


_nn.defvjp(lambda a, b: (_nn(a, b), (a, b)), lambda res, g: (_nt(g, res[1]), _tn(res[0], g)))
_nt.defvjp(lambda a, b: (_nt(a, b), (a, b)), lambda res, g: (_nn(g, res[1]), _tn(g, res[0])))
_tn.defvjp(lambda a, b: (_tn(a, b), (a, b)), lambda res, g: (_nt(res[1], g), _nn(res[0], g)))


def _split3_dot(m, a):
    a1 = a.astype(BF16)
    r1 = a - a1.astype(F32)
    a2 = r1.astype(BF16)
    a3 = (r1 - a2.astype(F32)).astype(BF16)
    dot = lambda p: jnp.dot(m, p, preferred_element_type=F32)
    return dot(a1) + dot(a2) + dot(a3)


@jax.custom_vjp
def _cum(m, mt, a):
    return _split3_dot(m, a)


_cum.defvjp(lambda m, mt, a: (_split3_dot(m, a), (m, mt)),
            lambda res, g: (jnp.zeros_like(res[0]), jnp.zeros_like(res[1]), _split3_dot(res[1], g)))


def _sigmoid(x):
    return 1.0 / (1.0 + jnp.exp(-x))


def _rms(x):
    return lax.rsqrt(jnp.mean(x * x, axis=-1, keepdims=True) + EPS)


def _rmsnorm_bwd(dy, x, gain):
    r = _rms(x)
    xhat = x * r
    dxh = dy * gain
    return r * (dxh - xhat * jnp.mean(dxh * xhat, axis=-1, keepdims=True)), xhat


def _norm_proj(h, gain, w, name):
    tp, d = h.shape
    s, _, ns = w.shape

    def body(h_ref, g_ref, w_ref, hn_ref, u_ref):
        @pl.when(pl.program_id(1) == 0)
        def _():
            x = h_ref[...]
            hn_ref[...] = (x * _rms(x) * g_ref[...]).astype(BF16)

        u_ref[...] = jnp.dot(hn_ref[...], w_ref[...], preferred_element_type=F32).astype(BF16)

    return pl.pallas_call(
        body, name=name, grid=(tp // TM, s),
        in_specs=[pl.BlockSpec((TM, d), lambda i, j: (i, 0)), pl.BlockSpec((1, d), lambda i, j: (0, 0)),
                  pl.BlockSpec((None, d, ns), lambda i, j: (j, 0, 0))],
        out_specs=[pl.BlockSpec((TM, d), lambda i, j: (i, 0)), pl.BlockSpec((TM, ns), lambda i, j: (i, j))],
        out_shape=[jax.ShapeDtypeStruct((tp, d), BF16), jax.ShapeDtypeStruct((tp, s * ns), BF16)],
        compiler_params=_cp(2))(h, gain, w)


def _norm_ffn_in(h, gain, w, name):
    tp, d = h.shape
    s, _, ns = w.shape
    half = s // 2

    def body(h_ref, g_ref, wg_ref, wu_ref, hn_ref, ug_ref, uu_ref, act_ref):
        @pl.when(pl.program_id(1) == 0)
        def _():
            x = h_ref[...]
            hn_ref[...] = (x * _rms(x) * g_ref[...]).astype(BF16)

        a = hn_ref[...]
        g = jnp.dot(a, wg_ref[...], preferred_element_type=F32)
        u = jnp.dot(a, wu_ref[...], preferred_element_type=F32)
        ug_ref[...] = g.astype(BF16)
        uu_ref[...] = u.astype(BF16)
        act_ref[...] = (g * _sigmoid(g) * u).astype(BF16)

    wide = jax.ShapeDtypeStruct((tp, half * ns), BF16)
    return pl.pallas_call(
        body, name=name, grid=(tp // TM, half),
        in_specs=[pl.BlockSpec((TM, d), lambda i, j: (i, 0)), pl.BlockSpec((1, d), lambda i, j: (0, 0)),
                  pl.BlockSpec((None, d, ns), lambda i, j: (j, 0, 0)),
                  pl.BlockSpec((None, d, ns), lambda i, j: (j + half, 0, 0))],
        out_specs=[pl.BlockSpec((TM, d), lambda i, j: (i, 0))] + [pl.BlockSpec((TM, ns), lambda i, j: (i, j))] * 3,
        out_shape=[jax.ShapeDtypeStruct((tp, d), BF16), wide, wide, wide],
        compiler_params=_cp(2))(h, gain, w, w)


def _out_proj(a, w, h, scale, name):
    tp, k = a.shape
    d = w.shape[1]

    def body(a_ref, w_ref, h_ref, o_ref):
        o_ref[...] = h_ref[...] + scale * jnp.dot(a_ref[...], w_ref[...], preferred_element_type=F32)

    return pl.pallas_call(
        body, name=name, grid=(tp // TM,),
        in_specs=[pl.BlockSpec((TM, k), lambda i: (i, 0)), pl.BlockSpec((k, d), lambda i: (0, 0)),
                  pl.BlockSpec((TM, d), lambda i: (i, 0))],
        out_specs=pl.BlockSpec((TM, d), lambda i: (i, 0)),
        out_shape=jax.ShapeDtypeStruct((tp, d), F32),
        compiler_params=_cp(1))(a, w, h)


def _ffn_dact(dh, w_out, ug, uu, name):
    tp, d = dh.shape
    ff = w_out.shape[0]
    tm = TM_SMALL

    def body(dh_ref, w_ref, ug_ref, uu_ref, du_ref):
        dy = (0.5 * dh_ref[...]).astype(BF16)
        dact = lax.dot_general(dy, w_ref[...], (((1,), (1,)), ((), ())), preferred_element_type=F32)
        g = ug_ref[...].astype(F32)
        u = uu_ref[...].astype(F32)
        sg = _sigmoid(g)
        du_ref[:, :ff] = (dact * u * (sg * (1.0 + g * (1.0 - sg)))).astype(BF16)
        du_ref[:, ff:] = (dact * (g * sg)).astype(BF16)

    return pl.pallas_call(
        body, name=name, grid=(tp // tm,),
        in_specs=[pl.BlockSpec((tm, d), lambda i: (i, 0)), pl.BlockSpec((ff, d), lambda i: (0, 0)),
                  pl.BlockSpec((tm, ff), lambda i: (i, 0)), pl.BlockSpec((tm, ff), lambda i: (i, 0))],
        out_specs=pl.BlockSpec((tm, 2 * ff), lambda i: (i, 0)),
        out_shape=jax.ShapeDtypeStruct((tp, 2 * ff), BF16),
        compiler_params=_cp(1))(dh, w_out, ug, uu)


def _dgrad(dh, w, name):
    tp, d = dh.shape
    k = w.shape[0]

    def body(dh_ref, w_ref, o_ref):
        o_ref[...] = lax.dot_general(dh_ref[...].astype(BF16), w_ref[...], (((1,), (1,)), ((), ())),
                                     preferred_element_type=F32).astype(BF16)

    return pl.pallas_call(
        body, name=name, grid=(tp // TM,),
        in_specs=[pl.BlockSpec((TM, d), lambda i: (i, 0)), pl.BlockSpec((k, d), lambda i: (0, 0))],
        out_specs=pl.BlockSpec((TM, k), lambda i: (i, 0)),
        out_shape=jax.ShapeDtypeStruct((tp, k), BF16),
        compiler_params=_cp(1))(dh, w)


def _wgrad(a, b, *, bm, bn, scale, sharded, name):
    tp, m = a.shape
    n = b.shape[1]
    nk = tp // TM

    def body(a_ref, b_ref, o_ref, acc_ref):
        k = pl.program_id(2)

        @pl.when(k == 0)
        def _():
            acc_ref[...] = jnp.zeros_like(acc_ref)

        bb = b_ref[...]
        if scale != 1.0:
            bb = scale * bb
        acc_ref[...] += lax.dot_general(a_ref[...], bb.astype(BF16), (((0,), (0,)), ((), ())),
                                        preferred_element_type=F32)

        @pl.when(k == nk - 1)
        def _():
            o_ref[...] = acc_ref[...].astype(BF16)

    if sharded:
        assert m == bm
        out_spec = pl.BlockSpec((None, bm, bn), lambda i, j, k: (j, 0, 0))
        out_shape = jax.ShapeDtypeStruct((n // bn, m, bn), BF16)
    else:
        out_spec = pl.BlockSpec((bm, bn), lambda i, j, k: (i, j))
        out_shape = jax.ShapeDtypeStruct((m, n), BF16)
    return pl.pallas_call(
        body, name=name, grid=(m // bm, n // bn, nk),
        in_specs=[pl.BlockSpec((TM, bm), lambda i, j, k: (k, i)), pl.BlockSpec((TM, bn), lambda i, j, k: (k, j))],
        out_specs=out_spec, out_shape=out_shape,
        scratch_shapes=[pltpu.VMEM((bm, bn), F32)],
        compiler_params=_cp(3))(a, b)


def _dgrad_norm(du, w, h, gain, dh_out, name):
    tp, d = h.shape
    s, _, ns = w.shape

    def body(du_ref, w_ref, h_ref, g_ref, dho_ref, dhi_ref, dg_ref, acc_ref):
        i, k = pl.program_id(0), pl.program_id(1)

        @pl.when(k == 0)
        def _():
            acc_ref[...] = jnp.zeros_like(acc_ref)

        @pl.when((i == 0) & (k == 0))
        def _():
            dg_ref[...] = jnp.zeros_like(dg_ref)

        acc_ref[...] += lax.dot_general(du_ref[...], w_ref[...], (((1,), (1,)), ((), ())),
                                        preferred_element_type=F32)

        @pl.when(k == s - 1)
        def _():
            dhn = acc_ref[...]
            dx, xhat = _rmsnorm_bwd(dhn, h_ref[...], g_ref[...])
            dg_ref[...] += jnp.sum(dhn * xhat, axis=0, keepdims=True)
            dhi_ref[...] = dho_ref[...] + dx

    return pl.pallas_call(
        body, name=name, grid=(tp // TM, s),
        in_specs=[pl.BlockSpec((TM, ns), lambda i, k: (i, k)), pl.BlockSpec((None, d, ns), lambda i, k: (k, 0, 0)),
                  pl.BlockSpec((TM, d), lambda i, k: (i, 0)), pl.BlockSpec((1, d), lambda i, k: (0, 0)),
                  pl.BlockSpec((TM, d), lambda i, k: (i, 0))],
        out_specs=[pl.BlockSpec((TM, d), lambda i, k: (i, 0)), pl.BlockSpec((1, d), lambda i, k: (0, 0))],
        out_shape=[jax.ShapeDtypeStruct((tp, d), F32), jax.ShapeDtypeStruct((1, d), F32)],
        scratch_shapes=[pltpu.VMEM((TM, d), F32)],
        compiler_params=_cp(2))(du, w, h, gain, dh_out)


def _loss_head(h, gain, target, name):
    tp, d = h.shape
    tm = TM_SMALL
    front_tiles = FRONT // tm

    def body(h_ref, g_ref, t_ref, dh_ref, dg_ref, loss_ref):
        i = pl.program_id(0)

        @pl.when(i == 0)
        def _():
            dg_ref[...] = jnp.zeros_like(dg_ref)
            loss_ref[...] = jnp.zeros_like(loss_ref)

        x = h_ref[...]
        gain_v = g_ref[...]
        y = x * _rms(x) * gain_v
        err = jnp.where(i >= front_tiles, y - t_ref[...], 0.0)
        loss_ref[...] += 0.5 * jnp.sum(jnp.mean(err * err, axis=-1, keepdims=True), axis=0, keepdims=True)
        dy = err * (1.0 / d)
        dx, xhat = _rmsnorm_bwd(dy, x, gain_v)
        dg_ref[...] += jnp.sum(dy * xhat, axis=0, keepdims=True)
        dh_ref[...] = dx

    return pl.pallas_call(
        body, name=name, grid=(tp // tm,),
        in_specs=[pl.BlockSpec((tm, d), lambda i: (i, 0)), pl.BlockSpec((1, d), lambda i: (0, 0)),
                  pl.BlockSpec((tm, d), lambda i: (jnp.maximum(i - front_tiles, 0), 0))],
        out_specs=[pl.BlockSpec((tm, d), lambda i: (i, 0)), pl.BlockSpec((1, d), lambda i: (0, 0)),
                   pl.BlockSpec((1, 128), lambda i: (0, 0))],
        out_shape=[jax.ShapeDtypeStruct((tp, d), F32), jax.ShapeDtypeStruct((1, d), F32),
                   jax.ShapeDtypeStruct((1, 128), F32)],
        compiler_params=_cp(1))(h, gain, target)


def _gated_headnorm(o, g, gain):
    return o * _rms(o) * gain * (g * _sigmoid(g))


def _row_mask(chunk):
    rows = chunk * CHUNK + lax.broadcasted_iota(jnp.int32, (CHUNK, 1), 0)
    return (rows >= FRONT - N_META).astype(F32)


def _ret_head(q1, q2, k1, k2, v, g, state, gain, cos, sin, dmat, dq, dk, dc):
    q = jnp.concatenate([q1 * cos - q2 * sin, q1 * sin + q2 * cos], axis=1)
    k = jnp.concatenate([k1 * cos - k2 * sin, k1 * sin + k2 * cos], axis=1) * (RET_DK ** -0.5)
    scores = _nt(q, k) * dmat
    o = _nn(scores, v) + _nn(q * dq, state)
    new_state = state * dc + _tn(k * dk, v)
    return _gated_headnorm(o, g, gain), new_state


def _ret_consts():
    log_gamma = jnp.log1p(-2.0 ** (-5.0 - jnp.arange(HEADS, dtype=F32)))
    idx = jnp.arange(CHUNK, dtype=F32)
    rel = idx[:, None] - idx[None, :]
    dmat = jnp.where(rel >= 0, jnp.exp(log_gamma[:, None, None] * jnp.maximum(rel, 0.0)), 0.0)
    dq = jnp.exp(log_gamma[:, None] * (idx + 1.0))[..., None]
    dk = jnp.exp(log_gamma[:, None] * (CHUNK - 1.0 - idx))[..., None]
    dc = jnp.broadcast_to(jnp.exp(log_gamma * CHUNK)[:, None, None], (HEADS, 1, 128))
    return dmat, dq, dk, dc


def _rope_tables(tp):
    half = RET_DK // 2
    inv = 1.0 / (ROPE_BASE ** jnp.linspace(0.0, 1.0, half, dtype=F32))
    pos = (jnp.arange(tp) - (FRONT - N_META)).astype(F32)
    ang = pos[:, None] * inv[None, :]
    return jnp.cos(ang), jnp.sin(ang)


_RET_V0, _RET_G0 = 2 * D, 4 * D


def _ret_pieces(u_ref, hd):
    f = lambda a, n: u_ref[:, a:a + n].astype(F32)
    hk = RET_DK // 2
    return (f(RET_DK * hd, hk), f(RET_DK * hd + hk, hk), f(D + RET_DK * hd, hk), f(D + RET_DK * hd + hk, hk),
            f(_RET_V0 + RET_DV * hd, RET_DV), f(_RET_G0 + RET_DV * hd, RET_DV))


def _ret_const_specs(rev=None):
    c = (lambda n: (rev(n), 0)) if rev else (lambda n: (n, 0))
    z3 = lambda n: (0, 0, 0)
    return [pl.BlockSpec((CHUNK, RET_DK // 2), c), pl.BlockSpec((CHUNK, RET_DK // 2), c),
            pl.BlockSpec((HEADS, CHUNK, CHUNK), z3), pl.BlockSpec((HEADS, CHUNK, 1), z3),
            pl.BlockSpec((HEADS, CHUNK, 1), z3), pl.BlockSpec((HEADS, 1, 128), z3)]


def _ret_fwd(u, gain, name):
    tp = u.shape[0]
    nch = tp // CHUNK
    cos, sin = _rope_tables(tp)
    dmat, dq, dk, dc = _ret_consts()

    def body(u_ref, gain_ref, cos_ref, sin_ref, dmat_ref, dq_ref, dk_ref, dc_ref, on_ref, st_ref, state_ref):
        @pl.when(pl.program_id(0) == 0)
        def _():
            state_ref[...] = jnp.zeros_like(state_ref)

        cos_v, sin_v = cos_ref[...], sin_ref[...]
        for hd in range(HEADS):
            state = state_ref[hd]
            st_ref[hd] = state.astype(BF16)
            on, new_state = _ret_head(*_ret_pieces(u_ref, hd), state,
                                      gain_ref[:, RET_DV * hd:RET_DV * (hd + 1)], cos_v, sin_v,
                                      dmat_ref[hd], dq_ref[hd], dk_ref[hd], dc_ref[hd][:, :1])
            state_ref[hd] = new_state
            on_ref[:, RET_DV * hd:RET_DV * (hd + 1)] = on.astype(BF16)

    return pl.pallas_call(
        body, name=name, grid=(nch,),
        in_specs=[pl.BlockSpec((CHUNK, 6 * D), lambda n: (n, 0)), pl.BlockSpec((1, HEADS * RET_DV), lambda n: (0, 0))]
                 + _ret_const_specs(),
        out_specs=[pl.BlockSpec((CHUNK, HEADS * RET_DV), lambda n: (n, 0)),
                   pl.BlockSpec((None, HEADS, RET_DK, RET_DV), lambda n: (n, 0, 0, 0))],
        out_shape=[jax.ShapeDtypeStruct((tp, HEADS * RET_DV), BF16),
                   jax.ShapeDtypeStruct((nch, HEADS, RET_DK, RET_DV), BF16)],
        scratch_shapes=[pltpu.VMEM((HEADS, RET_DK, RET_DV), F32)],
        compiler_params=_cp(1))(u, gain, cos, sin, dmat, dq, dk, dc)


def _ret_bwd(u, gain, states, d_on, name):
    tp = u.shape[0]
    nch = tp // CHUNK
    cos, sin = _rope_tables(tp)
    dmat, dq, dk, dc = _ret_consts()
    rev = lambda n: nch - 1 - n
    hk = RET_DK // 2

    def body(u_ref, gain_ref, st_ref, don_ref, cos_ref, sin_ref, dmat_ref, dq_ref, dk_ref, dc_ref,
             du_ref, dgain_ref, dstate_ref):
        @pl.when(pl.program_id(0) == 0)
        def _():
            dstate_ref[...] = jnp.zeros_like(dstate_ref)
            dgain_ref[...] = jnp.zeros_like(dgain_ref)

        cos_v, sin_v = cos_ref[...], sin_ref[...]
        mask = _row_mask(rev(pl.program_id(0)))
        for hd in range(HEADS):
            consts = (cos_v, sin_v, dmat_ref[hd], dq_ref[hd], dk_ref[hd], dc_ref[hd][:, :1])
            cols = slice(RET_DV * hd, RET_DV * (hd + 1))
            _, vjp = jax.vjp(lambda *a: _ret_head(*a, *consts), *_ret_pieces(u_ref, hd),
                             st_ref[hd].astype(F32), gain_ref[:, cols])
            dq1, dq2, dk1, dk2, dv, dg, dstate, dgain = vjp((don_ref[:, cols].astype(F32), dstate_ref[hd]))
            dstate_ref[hd] = dstate
            dgain_ref[:, cols] += dgain
            put = lambda a, t: du_ref.__setitem__((slice(None), slice(a, a + t.shape[1])), (t * mask).astype(BF16))
            put(RET_DK * hd, dq1)
            put(RET_DK * hd + hk, dq2)
            put(D + RET_DK * hd, dk1)
            put(D + RET_DK * hd + hk, dk2)
            put(_RET_V0 + RET_DV * hd, dv)
            put(_RET_G0 + RET_DV * hd, dg)

    return pl.pallas_call(
        body, name=name, grid=(nch,),
        in_specs=[pl.BlockSpec((CHUNK, 6 * D), lambda n: (rev(n), 0)),
                  pl.BlockSpec((1, HEADS * RET_DV), lambda n: (0, 0)),
                  pl.BlockSpec((None, HEADS, RET_DK, RET_DV), lambda n: (rev(n), 0, 0, 0)),
                  pl.BlockSpec((CHUNK, HEADS * RET_DV), lambda n: (rev(n), 0))] + _ret_const_specs(rev),
        out_specs=[pl.BlockSpec((CHUNK, 6 * D), lambda n: (rev(n), 0)),
                   pl.BlockSpec((1, HEADS * RET_DV), lambda n: (0, 0))],
        out_shape=[jax.ShapeDtypeStruct((tp, 6 * D), BF16), jax.ShapeDtypeStruct((1, HEADS * RET_DV), F32)],
        scratch_shapes=[pltpu.VMEM((HEADS, RET_DK, RET_DV), F32)],
        compiler_params=_cp(1))(u, gain, states, d_on, cos, sin, dmat, dq, dk, dc)


_GLA_K0, _GLA_V0, _GLA_G0, _GLA_Z0 = 512, 1024, 2048, 3072


def _gla_head(q, k, v, g, z, state_t, wg, bg, gain, mask, lo, lo_t, loc, loc_t):
    ga = _nn(z, wg) + bg
    log_a = (jnp.minimum(ga, 0.0) - jnp.log(1.0 + jnp.exp(-jnp.abs(ga)))) * (mask * (1.0 / GLA_TAU))
    bcum = _cum(lo, lo_t, log_a)
    bmid = _cum(loc, loc_t, log_a)
    btot = jnp.sum(log_a, axis=0, keepdims=True)
    qs = q * (GLA_DK ** -0.5)
    causal = lax.broadcasted_iota(jnp.int32, (CHUNK, CHUNK), 0) >= lax.broadcasted_iota(jnp.int32, (CHUNK, CHUNK), 1)
    scores = jnp.where(causal, _nt(qs * jnp.exp(bmid), k * jnp.exp(-bmid)), 0.0)
    o = _nn(scores, v) + _nt(qs * jnp.exp(bcum), state_t)
    new_state_t = state_t * jnp.exp(btot) + _tn(v, k * jnp.exp(btot - bcum))
    return _gated_headnorm(o, g, gain), new_state_t


def _cum_mats():
    r = lax.broadcasted_iota(jnp.int32, (CHUNK, CHUNK), 0)
    c = lax.broadcasted_iota(jnp.int32, (CHUNK, CHUNK), 1)
    mid = CHUNK // 2
    low = lambda a, b: (a >= b).astype(F32)
    lo, lo_t = low(r, c), low(c, r)
    loc = lo - (c <= mid).astype(F32)
    loc_t = lo_t - (r <= mid).astype(F32)
    return tuple(m.astype(BF16) for m in (lo, lo_t, loc, loc_t))


def _gla_pieces(u_ref, hd):
    f = lambda a, n: u_ref[:, a:a + n].astype(F32)
    return (f(GLA_DK * hd, GLA_DK), f(_GLA_K0 + GLA_DK * hd, GLA_DK), f(_GLA_V0 + GLA_DV * hd, GLA_DV),
            f(_GLA_G0 + GLA_DV * hd, GLA_DV), f(_GLA_Z0, 128))


def _gla_fwd(u, wg, bg, gain, name):
    tp = u.shape[0]
    nch = tp // CHUNK

    def body(u_ref, wg_ref, bg_ref, gain_ref, on_ref, st_ref, state_ref):
        @pl.when(pl.program_id(0) == 0)
        def _():
            state_ref[...] = jnp.zeros_like(state_ref)

        mask = _row_mask(pl.program_id(0))
        mats = _cum_mats()
        for hd in range(HEADS):
            state = state_ref[hd]
            st_ref[hd] = state.astype(BF16)
            kc = slice(GLA_DK * hd, GLA_DK * (hd + 1))
            vc = slice(GLA_DV * hd, GLA_DV * (hd + 1))
            on, new_state = _gla_head(*_gla_pieces(u_ref, hd), state,
                                      wg_ref[:, kc], bg_ref[:, kc], gain_ref[:, vc], mask, *mats)
            state_ref[hd] = new_state
            on_ref[:, vc] = on.astype(BF16)

    return pl.pallas_call(
        body, name=name, grid=(nch,),
        in_specs=[pl.BlockSpec((CHUNK, GLA_U), lambda n: (n, 0)), pl.BlockSpec((128, HEADS * GLA_DK), lambda n: (0, 0)),
                  pl.BlockSpec((1, HEADS * GLA_DK), lambda n: (0, 0)), pl.BlockSpec((1, HEADS * GLA_DV), lambda n: (0, 0))],
        out_specs=[pl.BlockSpec((CHUNK, HEADS * GLA_DV), lambda n: (n, 0)),
                   pl.BlockSpec((None, HEADS, GLA_DV, GLA_DK), lambda n: (n, 0, 0, 0))],
        out_shape=[jax.ShapeDtypeStruct((tp, HEADS * GLA_DV), BF16),
                   jax.ShapeDtypeStruct((nch, HEADS, GLA_DV, GLA_DK), BF16)],
        scratch_shapes=[pltpu.VMEM((HEADS, GLA_DV, GLA_DK), F32)],
        compiler_params=_cp(1))(u, wg, bg, gain)


def _gla_bwd(u, wg, bg, gain, states, d_on, name):
    tp = u.shape[0]
    nch = tp // CHUNK
    rev = lambda n: nch - 1 - n

    def body(u_ref, wg_ref, bg_ref, gain_ref, st_ref, don_ref, du_ref, dwg_ref, dbg_ref, dgain_ref, dstate_ref):
        @pl.when(pl.program_id(0) == 0)
        def _():
            dstate_ref[...] = jnp.zeros_like(dstate_ref)
            dwg_ref[...] = jnp.zeros_like(dwg_ref)
            dbg_ref[...] = jnp.zeros_like(dbg_ref)
            dgain_ref[...] = jnp.zeros_like(dgain_ref)

        mask = _row_mask(rev(pl.program_id(0)))
        mats = _cum_mats()
        dz_sum = jnp.zeros((CHUNK, 128), F32)
        for hd in range(HEADS):
            kc = slice(GLA_DK * hd, GLA_DK * (hd + 1))
            vc = slice(GLA_DV * hd, GLA_DV * (hd + 1))
            _, vjp = jax.vjp(lambda *a: _gla_head(*a, mask, *mats), *_gla_pieces(u_ref, hd),
                             st_ref[hd].astype(F32), wg_ref[:, kc].astype(F32), bg_ref[:, kc], gain_ref[:, vc])
            dq, dk, dv, dg, dz, dstate, dwg, dbg, dgain = vjp((don_ref[:, vc].astype(F32), dstate_ref[hd]))
            dstate_ref[hd] = dstate
            dwg_ref[:, kc] += dwg
            dbg_ref[:, kc] += dbg
            dgain_ref[:, vc] += dgain
            dz_sum = dz_sum + dz
            put = lambda a, t: du_ref.__setitem__((slice(None), slice(a, a + t.shape[1])), (t * mask).astype(BF16))
            put(GLA_DK * hd, dq)
            put(_GLA_K0 + GLA_DK * hd, dk)
            put(_GLA_V0 + GLA_DV * hd, dv)
            put(_GLA_G0 + GLA_DV * hd, dg)
        du_ref[:, _GLA_Z0:] = dz_sum.astype(BF16)

    full = lambda r, c: pl.BlockSpec((r, c), lambda n: (0, 0))
    return pl.pallas_call(
        body, name=name, grid=(nch,),
        in_specs=[pl.BlockSpec((CHUNK, GLA_U), lambda n: (rev(n), 0)), full(128, HEADS * GLA_DK),
                  full(1, HEADS * GLA_DK), full(1, HEADS * GLA_DV),
                  pl.BlockSpec((None, HEADS, GLA_DV, GLA_DK), lambda n: (rev(n), 0, 0, 0)),
                  pl.BlockSpec((CHUNK, HEADS * GLA_DV), lambda n: (rev(n), 0))],
        out_specs=[pl.BlockSpec((CHUNK, GLA_U), lambda n: (rev(n), 0)), full(128, HEADS * GLA_DK),
                   full(1, HEADS * GLA_DK), full(1, HEADS * GLA_DV)],
        out_shape=[jax.ShapeDtypeStruct((tp, GLA_U), BF16), jax.ShapeDtypeStruct((128, HEADS * GLA_DK), F32),
                   jax.ShapeDtypeStruct((1, HEADS * GLA_DK), F32), jax.ShapeDtypeStruct((1, HEADS * GLA_DV), F32)],
        scratch_shapes=[pltpu.VMEM((HEADS, GLA_DV, GLA_DK), F32)],
        compiler_params=_cp(1))(u, wg, bg, gain, states, d_on)


def _ffn_fwd(h, gain, w_in, w_out, tag):
    hn, ug, uu, act = _norm_ffn_in(h, gain, w_in, f"{tag}_in")
    return _out_proj(act, w_out, h, 0.5, f"{tag}_out"), (h, hn, ug, uu, act)


def _ffn_bwd(dh, saved, gain, w_in, w_out, tag):
    h, hn, ug, uu, act = saved
    du = _ffn_dact(dh, w_out, ug, uu, f"{tag}_dact")
    d_w_out = _wgrad(act, dh, bm=D_FF // 2, bn=D, scale=0.5, sharded=False, name=f"{tag}_dwout")
    d_w_in = _wgrad(hn, du, bm=D, bn=w_in.shape[2], scale=1.0, sharded=True, name=f"{tag}_dwin")
    dh_in, d_gain = _dgrad_norm(du, w_in, h, gain, dh, f"{tag}_dnorm")
    return dh_in, d_gain, d_w_in, d_w_out


def _sequence_grads(x, target, w):
    t = x.shape[0]
    h = jnp.concatenate([jnp.zeros((FRONT - N_META, D), F32), w["meta"], x], axis=0)
    row = lambda v: v.reshape(1, -1)

    saved = []
    for layer in range(2):
        h, s1 = _ffn_fwd(h, row(w["norm_ffn1"][layer]), w["ffn1_in"][layer], w["ffn1_out"][layer], f"l{layer}_ffn1")
        if layer == 0:
            hn, u = _norm_proj(h, row(w["norm_mix"][layer]), w["ret_in"], "ret_in")
            on, states = _ret_fwd(u, w["ret_gain"], "ret_fwd")
            h_mix = _out_proj(on, w["ret_out"], h, 1.0, "ret_out")
        else:
            hn, u = _norm_proj(h, row(w["norm_mix"][layer]), w["gla_in"], "gla_in")
            on, states = _gla_fwd(u, w["gla_wg"], w["gla_bg"], w["gla_gain"], "gla_fwd")
            h_mix = _out_proj(on, w["gla_out"], h, 1.0, "gla_out")
        s2 = (h, hn, u, on, states)
        h, s3 = _ffn_fwd(h_mix, row(w["norm_ffn2"][layer]), w["ffn2_in"][layer], w["ffn2_out"][layer], f"l{layer}_ffn2")
        saved.append((s1, s2, s3))

    dh, d_final, loss = _loss_head(h, row(w["final_norm"]), target, "loss_head")

    g = {"final_norm": d_final, "norm_ffn1": [None, None], "norm_mix": [None, None], "norm_ffn2": [None, None],
         "ffn1_in": [None, None], "ffn1_out": [None, None], "ffn2_in": [None, None], "ffn2_out": [None, None]}
    for layer in (1, 0):
        s1, s2, s3 = saved[layer]
        dh, g["norm_ffn2"][layer], g["ffn2_in"][layer], g["ffn2_out"][layer] = _ffn_bwd(
            dh, s3, row(w["norm_ffn2"][layer]), w["ffn2_in"][layer], w["ffn2_out"][layer], f"l{layer}_ffn2")
        h_in, hn, u, on, states = s2
        if layer == 0:
            d_on = _dgrad(dh, w["ret_out"], "ret_don")
            g["ret_out"] = _wgrad(on, dh, bm=D, bn=D, scale=1.0, sharded=False, name="ret_dwout")
            du, g["ret_gain"] = _ret_bwd(u, w["ret_gain"], states, d_on, "ret_bwd")
            g["ret_in"] = _wgrad(hn, du, bm=D, bn=w["ret_in"].shape[2], scale=1.0, sharded=True, name="ret_dwin")
            dh, g["norm_mix"][layer] = _dgrad_norm(du, w["ret_in"], h_in, row(w["norm_mix"][layer]), dh, "ret_dnorm")
        else:
            d_on = _dgrad(dh, w["gla_out"], "gla_don")
            g["gla_out"] = _wgrad(on, dh, bm=D, bn=D, scale=1.0, sharded=False, name="gla_dwout")
            du, g["gla_wg"], g["gla_bg"], g["gla_gain"] = _gla_bwd(u, w["gla_wg"], w["gla_bg"], w["gla_gain"],
                                                                   states, d_on, "gla_bwd")
            g["gla_in"] = _wgrad(hn, du, bm=D, bn=GLA_U // 5, scale=1.0, sharded=False, name="gla_dwin")
            dh, g["norm_mix"][layer] = _dgrad_norm(du, w["gla_in"], h_in, row(w["norm_mix"][layer]), dh, "gla_dnorm")
        dh, g["norm_ffn1"][layer], g["ffn1_in"][layer], g["ffn1_out"][layer] = _ffn_bwd(
            dh, s1, row(w["norm_ffn1"][layer]), w["ffn1_in"][layer], w["ffn1_out"][layer], f"l{layer}_ffn1")
    return loss, dh[FRONT:], dh[FRONT - N_META:FRONT], g


_HBM = pl.BlockSpec(memory_space=pl.ANY)


def _place():
    return lax.axis_index("x"), lax.axis_index("y"), lax.axis_index("c")


def _flip(v, bit):
    return 1 - v if bit else v


def _gather_chips(shards):
    n = len(shards)
    offsets = [(1, 0), (0, 1), (1, 1)]

    def body(*refs):
        src, dst = refs[:n], refs[n:2 * n]
        send_sems, recv_sems, local_sems = refs[2 * n:]
        x, y, c = _place()
        mine = 2 * x + y
        copies = []
        for t in range(n):
            local = pltpu.make_async_copy(src[t], dst[t].at[mine], local_sems.at[t])
            local.start()
            copies.append(local)
            for j, (fx, fy) in enumerate(offsets):
                cp = pltpu.make_async_remote_copy(src_ref=src[t], dst_ref=dst[t].at[mine],
                                                  send_sem=send_sems.at[t, j], recv_sem=recv_sems.at[t, j],
                                                  device_id=(_flip(x, fx), _flip(y, fy), c), device_id_type=MESH)
                cp.start()
                copies.append(cp)
        for cp in copies:
            cp.wait()

    return pl.pallas_call(
        body, name="gather_chips", in_specs=[_HBM] * n, out_specs=[_HBM] * n,
        out_shape=[jax.ShapeDtypeStruct((N_CHIPS,) + s.shape, s.dtype) for s in shards],
        scratch_shapes=[pltpu.SemaphoreType.DMA((n, 3)), pltpu.SemaphoreType.DMA((n, 3)), pltpu.SemaphoreType.DMA((n,))],
    )(*shards)


_PEER_FLIPS = [(fx, fy, fc) for fx in (0, 1) for fy in (0, 1) for fc in (0, 1)][1:]


def _scatter_devices(pieces, whole):
    n, m = len(pieces), len(whole)

    def body(*refs):
        src, dst = refs[:n + m], refs[n + m:2 * (n + m)]
        send_sems, recv_sems, local_sems = refs[2 * (n + m):]
        x, y, c = _place()
        me = 4 * x + 2 * y + c
        copies = []
        for t in range(n + m):
            part = (lambda px, py, pc: src[t].at[2 * px + py, pc]) if t < n else (lambda px, py, pc: src[t])
            local = pltpu.make_async_copy(part(x, y, c), dst[t].at[me], local_sems.at[t])
            local.start()
            copies.append(local)
            for j, (fx, fy, fc) in enumerate(_PEER_FLIPS):
                peer = (_flip(x, fx), _flip(y, fy), _flip(c, fc))
                cp = pltpu.make_async_remote_copy(src_ref=part(*peer), dst_ref=dst[t].at[me],
                                                  send_sem=send_sems.at[t, j], recv_sem=recv_sems.at[t, j],
                                                  device_id=peer, device_id_type=MESH)
                cp.start()
                copies.append(cp)
        for cp in copies:
            cp.wait()

    arrays = list(pieces) + list(whole)
    out_shape = [jax.ShapeDtypeStruct((N_DEV,) + a.shape[2:], a.dtype) for a in pieces]
    out_shape += [jax.ShapeDtypeStruct((N_DEV,) + a.shape, a.dtype) for a in whole]
    return pl.pallas_call(
        body, name="scatter_devices", in_specs=[_HBM] * (n + m), out_specs=[_HBM] * (n + m), out_shape=out_shape,
        scratch_shapes=[pltpu.SemaphoreType.DMA((n + m, 7)), pltpu.SemaphoreType.DMA((n + m, 7)),
                        pltpu.SemaphoreType.DMA((n + m,))],
    )(*arrays)


def _swap_cores(halves):
    n = len(halves)

    def body(*refs):
        src, dst = refs[:n], refs[n:2 * n]
        send_sems, recv_sems, local_sems = refs[2 * n:]
        x, y, c = _place()
        copies = []
        for t in range(n):
            local = pltpu.make_async_copy(src[t], dst[t].at[c], local_sems.at[t])
            local.start()
            copies.append(local)
            cp = pltpu.make_async_remote_copy(src_ref=src[t], dst_ref=dst[t].at[c], send_sem=send_sems.at[t],
                                              recv_sem=recv_sems.at[t], device_id=(x, y, 1 - c), device_id_type=MESH)
            cp.start()
            copies.append(cp)
        for cp in copies:
            cp.wait()

    return pl.pallas_call(
        body, name="swap_cores", in_specs=[_HBM] * n, out_specs=[_HBM] * n,
        out_shape=[jax.ShapeDtypeStruct((2,) + a.shape, a.dtype) for a in halves],
        scratch_shapes=[pltpu.SemaphoreType.DMA((n,)), pltpu.SemaphoreType.DMA((n,)), pltpu.SemaphoreType.DMA((n,))],
    )(*halves)


def _row_tile(rows, cap):
    fits = [t for t in range(16, cap + 1, 16) if rows % t == 0]
    return fits[-1] if fits else rows


def _sum_slots(a, name):
    _, r, c = a.shape
    tr = _row_tile(r, 384)

    def body(a_ref, o_ref):
        s = a_ref[0].astype(F32)
        for k in range(1, N_DEV):
            s = s + a_ref[k].astype(F32)
        o_ref[...] = s

    return pl.pallas_call(
        body, name=name, grid=(r // tr,),
        in_specs=[pl.BlockSpec((N_DEV, tr, c), lambda i: (0, i, 0))],
        out_specs=pl.BlockSpec((tr, c), lambda i: (i, 0)),
        out_shape=jax.ShapeDtypeStruct((r, c), F32),
        compiler_params=_cp(1))(a)


def _adamw(w, g, m, v, name):
    r, c = w.shape
    tr = _row_tile(r, 256)

    def body(w_ref, g_ref, m_ref, v_ref, d_ref, nm_ref, nv_ref):
        gv = g_ref[...]
        nm = ADAM_B1 * m_ref[...] + (1.0 - ADAM_B1) * gv
        nv = ADAM_B2 * v_ref[...] + (1.0 - ADAM_B2) * (gv * gv)
        m_hat = nm / (1.0 - ADAM_B1 ** ADAM_STEP)
        v_hat = nv / (1.0 - ADAM_B2 ** ADAM_STEP)
        d_ref[...] = -ADAM_LR * (m_hat / (jnp.sqrt(v_hat) + ADAM_EPS) + ADAM_WD * w_ref[...])
        nm_ref[...] = nm
        nv_ref[...] = nv

    spec = pl.BlockSpec((tr, c), lambda i: (i, 0))
    return pl.pallas_call(
        body, name=name, grid=(r // tr,), in_specs=[spec] * 4, out_specs=[spec] * 3,
        out_shape=[jax.ShapeDtypeStruct((r, c), F32)] * 3,
        compiler_params=_cp(1))(w, g, m, v)


_SMALL = ["meta_tokens", "ret_head_norm", "gla_w_gate", "gla_b_gate", "gla_head_norm"]
_LOCAL_SMALL = ["meta_tokens", "norm_ffn1", "norm_mix", "norm_ffn2", "ret_head_norm", "gla_w_gate", "gla_b_gate",
                "gla_head_norm", "final_norm"]
_BIG = ["ffn1_w_in", "ffn1_w_out", "ffn2_w_in", "ffn2_w_out", "ret_w_in", "ret_w_out", "gla_w_in", "gla_w_out"]
_WEIGHTS = ["meta_tokens", "norm_ffn1", "ffn1_w_in", "ffn1_w_out", "norm_mix", "norm_ffn2", "ffn2_w_in", "ffn2_w_out",
            "ret_w_in", "ret_head_norm", "ret_w_out", "gla_w_in", "gla_w_gate", "gla_b_gate", "gla_head_norm",
            "gla_w_out", "final_norm"]


def _pack_rows(arrays, width):
    flat = jnp.concatenate([a.reshape(-1) for a in arrays])
    pad = -flat.shape[0] % (8 * width)
    return jnp.pad(flat, (0, pad)).reshape(-1, width)


def _unpack_rows(packed, shapes):
    flat, out, at = packed.reshape(-1), [], 0
    for s in shapes:
        size = 1
        for dim in s:
            size *= dim
        out.append(flat[at:at + size].reshape(s))
        at += size
    return out


def _gather_all(p):
    b = lambda a: a.astype(BF16)
    shards = []
    for name in ("ffn1_w_in", "ffn2_w_in", "ffn1_w_out", "ffn2_w_out"):
        shards += [b(p[name][0]), b(p[name][1])]
    shards += [b(p["ret_w_in"][0]), b(p["ret_w_out"][0]), b(p["gla_w_in"][0]), b(p["gla_w_out"][0])]
    small_shapes = [p[name].shape for name in _SMALL]
    shards.append(_pack_rows([p[name] for name in _SMALL], 128))
    got = _gather_chips(shards)

    w = {"ffn1_in": got[0:2], "ffn2_in": got[2:4],
         "ffn1_out": [a.reshape(D_FF, D) for a in got[4:6]], "ffn2_out": [a.reshape(D_FF, D) for a in got[6:8]],
         "ret_in": got[8], "ret_out": got[9].reshape(HEADS * RET_DV, D), "gla_out": got[11].reshape(HEADS * GLA_DV, D)}
    gla_in = jnp.moveaxis(got[10], 0, 1).reshape(D, -1)
    w["gla_in"] = jnp.pad(gla_in, ((0, 0), (0, GLA_U - gla_in.shape[1])))[None]
    small = [jnp.stack(parts) for parts in zip(*[_unpack_rows(got[12][s], small_shapes) for s in range(N_CHIPS)])]
    cat = lambda a: jnp.moveaxis(a, 0, -2).reshape(a.shape[1:-1] + (-1,))
    meta, ret_gain, wg, bg, gla_gain = [cat(a) for a in small]
    w["meta"] = meta
    w["ret_gain"] = ret_gain.reshape(1, -1)
    w["gla_wg"] = jnp.pad(wg[0], ((0, 128 - GLA_RANK), (0, 0))).astype(BF16)
    w["gla_bg"] = bg.reshape(1, -1)
    w["gla_gain"] = gla_gain.reshape(1, -1)
    for name in ("norm_ffn1", "norm_mix", "norm_ffn2", "final_norm"):
        w[name] = p[name]
    return w


def _reduce_grads(loss, d_meta, g):
    halves = lambda a: a.reshape(N_CHIPS, 2, -1, a.shape[-1])
    pieces, names = [], []
    for name in ("ffn1_in", "ffn2_in", "ffn1_out", "ffn2_out"):
        for layer in range(2):
            pieces.append(halves(g[name][layer]))
            names.append((name, layer))
    gla_in = jnp.moveaxis(g["gla_in"][:, :4 * 772].reshape(D, N_CHIPS, 772), 1, 0)
    for name, a in (("ret_in", g["ret_in"]), ("ret_out", g["ret_out"]), ("gla_in", gla_in), ("gla_out", g["gla_out"])):
        pieces.append(halves(a))
        names.append((name, 0))
    small = [d_meta, g["norm_ffn1"][0], g["norm_ffn1"][1], g["norm_mix"][0], g["norm_mix"][1], g["norm_ffn2"][0],
             g["norm_ffn2"][1], g["final_norm"], g["ret_gain"], g["gla_wg"][:GLA_RANK], g["gla_bg"], g["gla_gain"],
             loss[:, :1]]
    small_shapes = [a.shape for a in small]
    got = _scatter_devices(pieces, [_pack_rows(small, D)])

    sums = [_sum_slots(a, f"sum_{name}{layer}") for a, (name, layer) in zip(got[:-1], names)]
    swapped = _swap_cores(sums)
    big = {}
    for a, (name, layer) in zip(swapped, names):
        big.setdefault(name, []).append(a.reshape(a.shape[0] * a.shape[1], a.shape[2]))
    small_sum = _unpack_rows(_sum_slots(got[-1], "sum_small"), small_shapes)
    return big, small_sum


def kernel(x, meta_tokens, norm_ffn1, ffn1_w_in, ffn1_w_out, norm_mix, norm_ffn2, ffn2_w_in, ffn2_w_out, ret_w_in, ret_head_norm, ret_w_out, gla_w_in, gla_w_gate, gla_b_gate, gla_head_norm, gla_w_out, final_norm, loss_target, m_meta_tokens, m_norm_ffn1, m_ffn1_w_in, m_ffn1_w_out, m_norm_mix, m_norm_ffn2, m_ffn2_w_in, m_ffn2_w_out, m_ret_w_in, m_ret_head_norm, m_ret_w_out, m_gla_w_in, m_gla_w_gate, m_gla_b_gate, m_gla_head_norm, m_gla_w_out, m_final_norm, v_meta_tokens, v_norm_ffn1, v_ffn1_w_in, v_ffn1_w_out, v_norm_mix, v_norm_ffn2, v_ffn2_w_in, v_ffn2_w_out, v_ret_w_in, v_ret_head_norm, v_ret_w_out, v_gla_w_in, v_gla_w_gate, v_gla_b_gate, v_gla_head_norm, v_gla_w_out, v_final_norm):
    p = dict(meta_tokens=meta_tokens, norm_ffn1=norm_ffn1, ffn1_w_in=ffn1_w_in, ffn1_w_out=ffn1_w_out, norm_mix=norm_mix,
             norm_ffn2=norm_ffn2, ffn2_w_in=ffn2_w_in, ffn2_w_out=ffn2_w_out, ret_w_in=ret_w_in,
             ret_head_norm=ret_head_norm, ret_w_out=ret_w_out, gla_w_in=gla_w_in, gla_w_gate=gla_w_gate,
             gla_b_gate=gla_b_gate, gla_head_norm=gla_head_norm, gla_w_out=gla_w_out, final_norm=final_norm)
    m = dict(zip(_WEIGHTS, (m_meta_tokens, m_norm_ffn1, m_ffn1_w_in, m_ffn1_w_out, m_norm_mix, m_norm_ffn2, m_ffn2_w_in,
                            m_ffn2_w_out, m_ret_w_in, m_ret_head_norm, m_ret_w_out, m_gla_w_in, m_gla_w_gate,
                            m_gla_b_gate, m_gla_head_norm, m_gla_w_out, m_final_norm)))
    v = dict(zip(_WEIGHTS, (v_meta_tokens, v_norm_ffn1, v_ffn1_w_in, v_ffn1_w_out, v_norm_mix, v_norm_ffn2, v_ffn2_w_in,
                            v_ffn2_w_out, v_ret_w_in, v_ret_head_norm, v_ret_w_out, v_gla_w_in, v_gla_w_gate,
                            v_gla_b_gate, v_gla_head_norm, v_gla_w_out, v_final_norm)))

    w = _gather_all(p)
    loss, d_x, d_meta, g = _sequence_grads(x[0], loss_target[0], w)
    big, small = _reduce_grads(loss, d_meta, g)

    chip = 2 * lax.axis_index("x") + lax.axis_index("y")
    cols = lambda a, n: lax.dynamic_slice_in_dim(a, chip * n, n, axis=a.ndim - 1)
    (s_meta, s_n1a, s_n1b, s_nma, s_nmb, s_n2a, s_n2b, s_final, s_ret_gain, s_wg, s_bg, s_gla_gain, s_loss) = small
    grads = {
        "meta_tokens": cols(s_meta, 256), "norm_ffn1": jnp.concatenate([s_n1a, s_n1b]),
        "norm_mix": jnp.concatenate([s_nma, s_nmb]), "norm_ffn2": jnp.concatenate([s_n2a, s_n2b]),
        "final_norm": s_final.reshape(D),
        "ret_head_norm": cols(s_ret_gain.reshape(1, HEADS, RET_DV), RET_DV // N_CHIPS),
        "gla_w_gate": cols(s_wg, GLA_DK)[None], "gla_b_gate": cols(s_bg, GLA_DK),
        "gla_head_norm": cols(s_gla_gain.reshape(1, HEADS, GLA_DV), GLA_DV // N_CHIPS),
        "ffn1_w_in": jnp.stack(big["ffn1_in"]), "ffn1_w_out": jnp.stack(big["ffn1_out"]),
        "ffn2_w_in": jnp.stack(big["ffn2_in"]), "ffn2_w_out": jnp.stack(big["ffn2_out"]),
        "ret_w_in": big["ret_in"][0][None], "ret_w_out": big["ret_out"][0][None],
        "gla_w_in": big["gla_in"][0][None], "gla_w_out": big["gla_out"][0][None],
    }

    delta, new_m, new_v = {}, {}, {}
    for name in _BIG:
        shape = p[name].shape
        flat = lambda a: a.reshape(-1, shape[-1])
        out = _adamw(flat(p[name]), flat(grads[name]), flat(m[name]), flat(v[name]), f"adamw_{name}")
        delta[name], new_m[name], new_v[name] = [a.reshape(shape) for a in out]
    packed = [_pack_rows([d[name] for name in _LOCAL_SMALL], 128) for d in (p, grads, m, v)]
    out = _adamw(*packed, "adamw_small")
    shapes = [p[name].shape for name in _LOCAL_SMALL]
    for d, a in zip((delta, new_m, new_v), out):
        d.update(zip(_LOCAL_SMALL, _unpack_rows(a, shapes)))

    return (s_loss.reshape(()), d_x[None], *[grads[n] for n in _WEIGHTS], *[delta[n] for n in _WEIGHTS],
            *[new_m[n] for n in _WEIGHTS], *[new_v[n] for n in _WEIGHTS])
```

```python
import functools

import jax
import jax.numpy as jnp
from jax import lax
from jax.experimental import pallas as pl
from jax.experimental.pallas import tpu as pltpu

F32, BF16 = jnp.float32, jnp.bfloat16
MESH = pl.DeviceIdType.MESH

D = 1024
N_META = 16
CHUNK = 64
FRONT = 256
D_FF = 2816
EPS = 1e-6
HEADS = 4
RET_DK, RET_DV = 256, 512
GLA_DK, GLA_DV = 128, 256
GLA_RANK = 16
GLA_TAU = 16.0
GLA_U = 3200
ROPE_BASE = 10000.0
N_CHIPS = 4
N_DEV = 8

ADAM_LR, ADAM_B1, ADAM_B2, ADAM_EPS, ADAM_WD, ADAM_STEP = 0.001, 0.9, 0.999, 1e-08, 0.01, 10

VMEM_LIMIT_BYTES = 56 * 1024 * 1024
TM = 768
TM_SMALL = 256


def _cp(n_axes):
    return pltpu.CompilerParams(dimension_semantics=("arbitrary",) * n_axes, vmem_limit_bytes=VMEM_LIMIT_BYTES)


def _dg(a, b, ca, cb):
    return lax.dot_general(a.astype(BF16), b.astype(BF16), (((ca,), (cb,)), ((), ())), preferred_element_type=F32)


@jax.custom_vjp
def _nn(a, b):
    return _dg(a, b, 1, 0)


@jax.custom_vjp
def _nt(a, b):
    return _dg(a, b, 1, 1)


@jax.custom_vjp
def _tn(a, b):
    return _dg(a, b, 0, 0)


_nn.defvjp(lambda a, b: (_nn(a, b), (a, b)), lambda res, g: (_nt(g, res[1]), _tn(res[0], g)))
_nt.defvjp(lambda a, b: (_nt(a, b), (a, b)), lambda res, g: (_nn(g, res[1]), _tn(g, res[0])))
_tn.defvjp(lambda a, b: (_tn(a, b), (a, b)), lambda res, g: (_nt(res[1], g), _nn(res[0], g)))


def _split3_dot(m, a):
    a1 = a.astype(BF16)
    r1 = a - a1.astype(F32)
    a2 = r1.astype(BF16)
    a3 = (r1 - a2.astype(F32)).astype(BF16)
    dot = lambda p: jnp.dot(m, p, preferred_element_type=F32)
    return dot(a1) + dot(a2) + dot(a3)


@jax.custom_vjp
def _cum(m, mt, a):
    return _split3_dot(m, a)


_cum.defvjp(lambda m, mt, a: (_split3_dot(m, a), (m, mt)),
            lambda res, g: (jnp.zeros_like(res[0]), jnp.zeros_like(res[1]), _split3_dot(res[1], g)))


def _sigmoid(x):
    return 1.0 / (1.0 + jnp.exp(-x))


def _rms(x):
    return lax.rsqrt(jnp.mean(x * x, axis=-1, keepdims=True) + EPS)


def _rmsnorm_bwd(dy, x, gain):
    r = _rms(x)
    xhat = x * r
    dxh = dy * gain
    return r * (dxh - xhat * jnp.mean(dxh * xhat, axis=-1, keepdims=True)), xhat


def _norm_proj(h, gain, w, name):
    tp, d = h.shape
    s, _, ns = w.shape

    def body(h_ref, g_ref, w_ref, hn_ref, u_ref):
        @pl.when(pl.program_id(1) == 0)
        def _():
            x = h_ref[...]
            hn_ref[...] = (x * _rms(x) * g_ref[...]).astype(BF16)

        u_ref[...] = jnp.dot(hn_ref[...], w_ref[...], preferred_element_type=F32).astype(BF16)

    return pl.pallas_call(
        body, name=name, grid=(tp // TM, s),
        in_specs=[pl.BlockSpec((TM, d), lambda i, j: (i, 0)), pl.BlockSpec((1, d), lambda i, j: (0, 0)),
                  pl.BlockSpec((None, d, ns), lambda i, j: (j, 0, 0))],
        out_specs=[pl.BlockSpec((TM, d), lambda i, j: (i, 0)), pl.BlockSpec((TM, ns), lambda i, j: (i, j))],
        out_shape=[jax.ShapeDtypeStruct((tp, d), BF16), jax.ShapeDtypeStruct((tp, s * ns), BF16)],
        compiler_params=_cp(2))(h, gain, w)


def _norm_ffn_in(h, gain, w, name):
    tp, d = h.shape
    s, _, ns = w.shape
    half = s // 2

    def body(h_ref, g_ref, wg_ref, wu_ref, hn_ref, ug_ref, uu_ref, act_ref):
        @pl.when(pl.program_id(1) == 0)
        def _():
            x = h_ref[...]
            hn_ref[...] = (x * _rms(x) * g_ref[...]).astype(BF16)

        a = hn_ref[...]
        g = jnp.dot(a, wg_ref[...], preferred_element_type=F32)
        u = jnp.dot(a, wu_ref[...], preferred_element_type=F32)
        ug_ref[...] = g.astype(BF16)
        uu_ref[...] = u.astype(BF16)
        act_ref[...] = (g * _sigmoid(g) * u).astype(BF16)

    wide = jax.ShapeDtypeStruct((tp, half * ns), BF16)
    return pl.pallas_call(
        body, name=name, grid=(tp // TM, half),
        in_specs=[pl.BlockSpec((TM, d), lambda i, j: (i, 0)), pl.BlockSpec((1, d), lambda i, j: (0, 0)),
                  pl.BlockSpec((None, d, ns), lambda i, j: (j, 0, 0)),
                  pl.BlockSpec((None, d, ns), lambda i, j: (j + half, 0, 0))],
        out_specs=[pl.BlockSpec((TM, d), lambda i, j: (i, 0))] + [pl.BlockSpec((TM, ns), lambda i, j: (i, j))] * 3,
        out_shape=[jax.ShapeDtypeStruct((tp, d), BF16), wide, wide, wide],
        compiler_params=_cp(2))(h, gain, w, w)


def _out_proj(a, w, h, scale, name):
    tp, k = a.shape
    d = w.shape[1]

    def body(a_ref, w_ref, h_ref, o_ref):
        o_ref[...] = h_ref[...] + scale * jnp.dot(a_ref[...], w_ref[...], preferred_element_type=F32)

    return pl.pallas_call(
        body, name=name, grid=(tp // TM,),
        in_specs=[pl.BlockSpec((TM, k), lambda i: (i, 0)), pl.BlockSpec((k, d), lambda i: (0, 0)),
                  pl.BlockSpec((TM, d), lambda i: (i, 0))],
        out_specs=pl.BlockSpec((TM, d), lambda i: (i, 0)),
        out_shape=jax.ShapeDtypeStruct((tp, d), F32),
        compiler_params=_cp(1))(a, w, h)


def _ffn_dact(dh, w_out, ug, uu, name):
    tp, d = dh.shape
    ff = w_out.shape[0]
    tm = TM_SMALL

    def body(dh_ref, w_ref, ug_ref, uu_ref, du_ref):
        dy = (0.5 * dh_ref[...]).astype(BF16)
        dact = lax.dot_general(dy, w_ref[...], (((1,), (1,)), ((), ())), preferred_element_type=F32)
        g = ug_ref[...].astype(F32)
        u = uu_ref[...].astype(F32)
        sg = _sigmoid(g)
        du_ref[:, :ff] = (dact * u * (sg * (1.0 + g * (1.0 - sg)))).astype(BF16)
        du_ref[:, ff:] = (dact * (g * sg)).astype(BF16)

    return pl.pallas_call(
        body, name=name, grid=(tp // tm,),
        in_specs=[pl.BlockSpec((tm, d), lambda i: (i, 0)), pl.BlockSpec((ff, d), lambda i: (0, 0)),
                  pl.BlockSpec((tm, ff), lambda i: (i, 0)), pl.BlockSpec((tm, ff), lambda i: (i, 0))],
        out_specs=pl.BlockSpec((tm, 2 * ff), lambda i: (i, 0)),
        out_shape=jax.ShapeDtypeStruct((tp, 2 * ff), BF16),
        compiler_params=_cp(1))(dh, w_out, ug, uu)


def _dgrad(dh, w, name):
    tp, d = dh.shape
    k = w.shape[0]

    def body(dh_ref, w_ref, o_ref):
        o_ref[...] = lax.dot_general(dh_ref[...].astype(BF16), w_ref[...], (((1,), (1,)), ((), ())),
                                     preferred_element_type=F32).astype(BF16)

    return pl.pallas_call(
        body, name=name, grid=(tp // TM,),
        in_specs=[pl.BlockSpec((TM, d), lambda i: (i, 0)), pl.BlockSpec((k, d), lambda i: (0, 0))],
        out_specs=pl.BlockSpec((TM, k), lambda i: (i, 0)),
        out_shape=jax.ShapeDtypeStruct((tp, k), BF16),
        compiler_params=_cp(1))(dh, w)


def _wgrad(a, b, *, bm, bn, scale, sharded, name):
    tp, m = a.shape
    n = b.shape[1]
    nk = tp // TM

    def body(a_ref, b_ref, o_ref, acc_ref):
        k = pl.program_id(2)

        @pl.when(k == 0)
        def _():
            acc_ref[...] = jnp.zeros_like(acc_ref)

        bb = b_ref[...]
        if scale != 1.0:
            bb = scale * bb
        acc_ref[...] += lax.dot_general(a_ref[...], bb.astype(BF16), (((0,), (0,)), ((), ())),
                                        preferred_element_type=F32)

        @pl.when(k == nk - 1)
        def _():
            o_ref[...] = acc_ref[...].astype(BF16)

    if sharded:
        assert m == bm
        out_spec = pl.BlockSpec((None, bm, bn), lambda i, j, k: (j, 0, 0))
        out_shape = jax.ShapeDtypeStruct((n // bn, m, bn), BF16)
    else:
        out_spec = pl.BlockSpec((bm, bn), lambda i, j, k: (i, j))
        out_shape = jax.ShapeDtypeStruct((m, n), BF16)
    return pl.pallas_call(
        body, name=name, grid=(m // bm, n // bn, nk),
        in_specs=[pl.BlockSpec((TM, bm), lambda i, j, k: (k, i)), pl.BlockSpec((TM, bn), lambda i, j, k: (k, j))],
        out_specs=out_spec, out_shape=out_shape,
        scratch_shapes=[pltpu.VMEM((bm, bn), F32)],
        compiler_params=_cp(3))(a, b)


def _dgrad_norm(du, w, h, gain, dh_out, name):
    tp, d = h.shape
    s, _, ns = w.shape

    def body(du_ref, w_ref, h_ref, g_ref, dho_ref, dhi_ref, dg_ref, acc_ref):
        i, k = pl.program_id(0), pl.program_id(1)

        @pl.when(k == 0)
        def _():
            acc_ref[...] = jnp.zeros_like(acc_ref)

        @pl.when((i == 0) & (k == 0))
        def _():
            dg_ref[...] = jnp.zeros_like(dg_ref)

        acc_ref[...] += lax.dot_general(du_ref[...], w_ref[...], (((1,), (1,)), ((), ())),
                                        preferred_element_type=F32)

        @pl.when(k == s - 1)
        def _():
            dhn = acc_ref[...]
            dx, xhat = _rmsnorm_bwd(dhn, h_ref[...], g_ref[...])
            dg_ref[...] += jnp.sum(dhn * xhat, axis=0, keepdims=True)
            dhi_ref[...] = dho_ref[...] + dx

    return pl.pallas_call(
        body, name=name, grid=(tp // TM, s),
        in_specs=[pl.BlockSpec((TM, ns), lambda i, k: (i, k)), pl.BlockSpec((None, d, ns), lambda i, k: (k, 0, 0)),
                  pl.BlockSpec((TM, d), lambda i, k: (i, 0)), pl.BlockSpec((1, d), lambda i, k: (0, 0)),
                  pl.BlockSpec((TM, d), lambda i, k: (i, 0))],
        out_specs=[pl.BlockSpec((TM, d), lambda i, k: (i, 0)), pl.BlockSpec((1, d), lambda i, k: (0, 0))],
        out_shape=[jax.ShapeDtypeStruct((tp, d), F32), jax.ShapeDtypeStruct((1, d), F32)],
        scratch_shapes=[pltpu.VMEM((TM, d), F32)],
        compiler_params=_cp(2))(du, w, h, gain, dh_out)


def _loss_head(h, gain, target, name):
    tp, d = h.shape
    tm = TM_SMALL
    front_tiles = FRONT // tm

    def body(h_ref, g_ref, t_ref, dh_ref, dg_ref, loss_ref):
        i = pl.program_id(0)

        @pl.when(i == 0)
        def _():
            dg_ref[...] = jnp.zeros_like(dg_ref)
            loss_ref[...] = jnp.zeros_like(loss_ref)

        x = h_ref[...]
        gain_v = g_ref[...]
        y = x * _rms(x) * gain_v
        err = jnp.where(i >= front_tiles, y - t_ref[...], 0.0)
        loss_ref[...] += 0.5 * jnp.sum(jnp.mean(err * err, axis=-1, keepdims=True), axis=0, keepdims=True)
        dy = err * (1.0 / d)
        dx, xhat = _rmsnorm_bwd(dy, x, gain_v)
        dg_ref[...] += jnp.sum(dy * xhat, axis=0, keepdims=True)
        dh_ref[...] = dx

    return pl.pallas_call(
        body, name=name, grid=(tp // tm,),
        in_specs=[pl.BlockSpec((tm, d), lambda i: (i, 0)), pl.BlockSpec((1, d), lambda i: (0, 0)),
                  pl.BlockSpec((tm, d), lambda i: (jnp.maximum(i - front_tiles, 0), 0))],
        out_specs=[pl.BlockSpec((tm, d), lambda i: (i, 0)), pl.BlockSpec((1, d), lambda i: (0, 0)),
                   pl.BlockSpec((1, 128), lambda i: (0, 0))],
        out_shape=[jax.ShapeDtypeStruct((tp, d), F32), jax.ShapeDtypeStruct((1, d), F32),
                   jax.ShapeDtypeStruct((1, 128), F32)],
        compiler_params=_cp(1))(h, gain, target)


def _gated_headnorm(o, g, gain):
    return o * _rms(o) * gain * (g * _sigmoid(g))


def _row_mask(chunk):
    rows = chunk * CHUNK + lax.broadcasted_iota(jnp.int32, (CHUNK, 1), 0)
    return (rows >= FRONT - N_META).astype(F32)


def _ret_head(q1, q2, k1, k2, v, g, state, gain, cos, sin, dmat, dq, dk, dc):
    q = jnp.concatenate([q1 * cos - q2 * sin, q1 * sin + q2 * cos], axis=1)
    k = jnp.concatenate([k1 * cos - k2 * sin, k1 * sin + k2 * cos], axis=1) * (RET_DK ** -0.5)
    scores = _nt(q, k) * dmat
    o = _nn(scores, v) + _nn(q * dq, state)
    new_state = state * dc + _tn(k * dk, v)
    return _gated_headnorm(o, g, gain), new_state


def _ret_consts():
    log_gamma = jnp.log1p(-2.0 ** (-5.0 - jnp.arange(HEADS, dtype=F32)))
    idx = jnp.arange(CHUNK, dtype=F32)
    rel = idx[:, None] - idx[None, :]
    dmat = jnp.where(rel >= 0, jnp.exp(log_gamma[:, None, None] * jnp.maximum(rel, 0.0)), 0.0)
    dq = jnp.exp(log_gamma[:, None] * (idx + 1.0))[..., None]
    dk = jnp.exp(log_gamma[:, None] * (CHUNK - 1.0 - idx))[..., None]
    dc = jnp.broadcast_to(jnp.exp(log_gamma * CHUNK)[:, None, None], (HEADS, 1, 128))
    return dmat, dq, dk, dc


def _rope_tables(tp):
    half = RET_DK // 2
    inv = 1.0 / (ROPE_BASE ** jnp.linspace(0.0, 1.0, half, dtype=F32))
    pos = (jnp.arange(tp) - (FRONT - N_META)).astype(F32)
    ang = pos[:, None] * inv[None, :]
    return jnp.cos(ang), jnp.sin(ang)


_RET_V0, _RET_G0 = 2 * D, 4 * D


def _ret_pieces(u_ref, hd):
    f = lambda a, n: u_ref[:, a:a + n].astype(F32)
    hk = RET_DK // 2
    return (f(RET_DK * hd, hk), f(RET_DK * hd + hk, hk), f(D + RET_DK * hd, hk), f(D + RET_DK * hd + hk, hk),
            f(_RET_V0 + RET_DV * hd, RET_DV), f(_RET_G0 + RET_DV * hd, RET_DV))


def _ret_const_specs(rev=None):
    c = (lambda n: (rev(n), 0)) if rev else (lambda n: (n, 0))
    z3 = lambda n: (0, 0, 0)
    return [pl.BlockSpec((CHUNK, RET_DK // 2), c), pl.BlockSpec((CHUNK, RET_DK // 2), c),
            pl.BlockSpec((HEADS, CHUNK, CHUNK), z3), pl.BlockSpec((HEADS, CHUNK, 1), z3),
            pl.BlockSpec((HEADS, CHUNK, 1), z3), pl.BlockSpec((HEADS, 1, 128), z3)]


def _ret_fwd(u, gain, name):
    tp = u.shape[0]
    nch = tp // CHUNK
    cos, sin = _rope_tables(tp)
    dmat, dq, dk, dc = _ret_consts()

    def body(u_ref, gain_ref, cos_ref, sin_ref, dmat_ref, dq_ref, dk_ref, dc_ref, on_ref, st_ref, state_ref):
        @pl.when(pl.program_id(0) == 0)
        def _():
            state_ref[...] = jnp.zeros_like(state_ref)

        cos_v, sin_v = cos_ref[...], sin_ref[...]
        for hd in range(HEADS):
            state = state_ref[hd]
            st_ref[hd] = state.astype(BF16)
            on, new_state = _ret_head(*_ret_pieces(u_ref, hd), state,
                                      gain_ref[:, RET_DV * hd:RET_DV * (hd + 1)], cos_v, sin_v,
                                      dmat_ref[hd], dq_ref[hd], dk_ref[hd], dc_ref[hd][:, :1])
            state_ref[hd] = new_state
            on_ref[:, RET_DV * hd:RET_DV * (hd + 1)] = on.astype(BF16)

    return pl.pallas_call(
        body, name=name, grid=(nch,),
        in_specs=[pl.BlockSpec((CHUNK, 6 * D), lambda n: (n, 0)), pl.BlockSpec((1, HEADS * RET_DV), lambda n: (0, 0))]
                 + _ret_const_specs(),
        out_specs=[pl.BlockSpec((CHUNK, HEADS * RET_DV), lambda n: (n, 0)),
                   pl.BlockSpec((None, HEADS, RET_DK, RET_DV), lambda n: (n, 0, 0, 0))],
        out_shape=[jax.ShapeDtypeStruct((tp, HEADS * RET_DV), BF16),
                   jax.ShapeDtypeStruct((nch, HEADS, RET_DK, RET_DV), BF16)],
        scratch_shapes=[pltpu.VMEM((HEADS, RET_DK, RET_DV), F32)],
        compiler_params=_cp(1))(u, gain, cos, sin, dmat, dq, dk, dc)


def _ret_bwd(u, gain, states, d_on, name):
    tp = u.shape[0]
    nch = tp // CHUNK
    cos, sin = _rope_tables(tp)
    dmat, dq, dk, dc = _ret_consts()
    rev = lambda n: nch - 1 - n
    hk = RET_DK // 2

    def body(u_ref, gain_ref, st_ref, don_ref, cos_ref, sin_ref, dmat_ref, dq_ref, dk_ref, dc_ref,
             du_ref, dgain_ref, dstate_ref):
        @pl.when(pl.program_id(0) == 0)
        def _():
            dstate_ref[...] = jnp.zeros_like(dstate_ref)
            dgain_ref[...] = jnp.zeros_like(dgain_ref)

        cos_v, sin_v = cos_ref[...], sin_ref[...]
        mask = _row_mask(rev(pl.program_id(0)))
        for hd in range(HEADS):
            consts = (cos_v, sin_v, dmat_ref[hd], dq_ref[hd], dk_ref[hd], dc_ref[hd][:, :1])
            cols = slice(RET_DV * hd, RET_DV * (hd + 1))
            _, vjp = jax.vjp(lambda *a: _ret_head(*a, *consts), *_ret_pieces(u_ref, hd),
                             st_ref[hd].astype(F32), gain_ref[:, cols])
            dq1, dq2, dk1, dk2, dv, dg, dstate, dgain = vjp((don_ref[:, cols].astype(F32), dstate_ref[hd]))
            dstate_ref[hd] = dstate
            dgain_ref[:, cols] += dgain
            put = lambda a, t: du_ref.__setitem__((slice(None), slice(a, a + t.shape[1])), (t * mask).astype(BF16))
            put(RET_DK * hd, dq1)
            put(RET_DK * hd + hk, dq2)
            put(D + RET_DK * hd, dk1)
            put(D + RET_DK * hd + hk, dk2)
            put(_RET_V0 + RET_DV * hd, dv)
            put(_RET_G0 + RET_DV * hd, dg)

    return pl.pallas_call(
        body, name=name, grid=(nch,),
        in_specs=[pl.BlockSpec((CHUNK, 6 * D), lambda n: (rev(n), 0)),
                  pl.BlockSpec((1, HEADS * RET_DV), lambda n: (0, 0)),
                  pl.BlockSpec((None, HEADS, RET_DK, RET_DV), lambda n: (rev(n), 0, 0, 0)),
                  pl.BlockSpec((CHUNK, HEADS * RET_DV), lambda n: (rev(n), 0))] + _ret_const_specs(rev),
        out_specs=[pl.BlockSpec((CHUNK, 6 * D), lambda n: (rev(n), 0)),
                   pl.BlockSpec((1, HEADS * RET_DV), lambda n: (0, 0))],
        out_shape=[jax.ShapeDtypeStruct((tp, 6 * D), BF16), jax.ShapeDtypeStruct((1, HEADS * RET_DV), F32)],
        scratch_shapes=[pltpu.VMEM((HEADS, RET_DK, RET_DV), F32)],
        compiler_params=_cp(1))(u, gain, states, d_on, cos, sin, dmat, dq, dk, dc)


_GLA_K0, _GLA_V0, _GLA_G0, _GLA_Z0 = 512, 1024, 2048, 3072


def _gla_head(q, k, v, g, z, state_t, wg, bg, gain, mask, lo, lo_t, loc, loc_t):
    ga = _nn(z, wg) + bg
    log_a = (jnp.minimum(ga, 0.0) - jnp.log(1.0 + jnp.exp(-jnp.abs(ga)))) * (mask * (1.0 / GLA_TAU))
    bcum = _cum(lo, lo_t, log_a)
    bmid = _cum(loc, loc_t, log_a)
    btot = jnp.sum(log_a, axis=0, keepdims=True)
    qs = q * (GLA_DK ** -0.5)
    causal = lax.broadcasted_iota(jnp.int32, (CHUNK, CHUNK), 0) >= lax.broadcasted_iota(jnp.int32, (CHUNK, CHUNK), 1)
    scores = jnp.where(causal, _nt(qs * jnp.exp(bmid), k * jnp.exp(-bmid)), 0.0)
    o = _nn(scores, v) + _nt(qs * jnp.exp(bcum), state_t)
    new_state_t = state_t * jnp.exp(btot) + _tn(v, k * jnp.exp(btot - bcum))
    return _gated_headnorm(o, g, gain), new_state_t


def _cum_mats():
    r = lax.broadcasted_iota(jnp.int32, (CHUNK, CHUNK), 0)
    c = lax.broadcasted_iota(jnp.int32, (CHUNK, CHUNK), 1)
    mid = CHUNK // 2
    low = lambda a, b: (a >= b).astype(F32)
    lo, lo_t = low(r, c), low(c, r)
    loc = lo - (c <= mid).astype(F32)
    loc_t = lo_t - (r <= mid).astype(F32)
    return tuple(m.astype(BF16) for m in (lo, lo_t, loc, loc_t))


def _gla_pieces(u_ref, hd):
    f = lambda a, n: u_ref[:, a:a + n].astype(F32)
    return (f(GLA_DK * hd, GLA_DK), f(_GLA_K0 + GLA_DK * hd, GLA_DK), f(_GLA_V0 + GLA_DV * hd, GLA_DV),
            f(_GLA_G0 + GLA_DV * hd, GLA_DV), f(_GLA_Z0, 128))


def _gla_fwd(u, wg, bg, gain, name):
    tp = u.shape[0]
    nch = tp // CHUNK

    def body(u_ref, wg_ref, bg_ref, gain_ref, on_ref, st_ref, state_ref):
        @pl.when(pl.program_id(0) == 0)
        def _():
            state_ref[...] = jnp.zeros_like(state_ref)

        mask = _row_mask(pl.program_id(0))
        mats = _cum_mats()
        for hd in range(HEADS):
            state = state_ref[hd]
            st_ref[hd] = state.astype(BF16)
            kc = slice(GLA_DK * hd, GLA_DK * (hd + 1))
            vc = slice(GLA_DV * hd, GLA_DV * (hd + 1))
            on, new_state = _gla_head(*_gla_pieces(u_ref, hd), state,
                                      wg_ref[:, kc], bg_ref[:, kc], gain_ref[:, vc], mask, *mats)
            state_ref[hd] = new_state
            on_ref[:, vc] = on.astype(BF16)

    return pl.pallas_call(
        body, name=name, grid=(nch,),
        in_specs=[pl.BlockSpec((CHUNK, GLA_U), lambda n: (n, 0)), pl.BlockSpec((128, HEADS * GLA_DK), lambda n: (0, 0)),
                  pl.BlockSpec((1, HEADS * GLA_DK), lambda n: (0, 0)), pl.BlockSpec((1, HEADS * GLA_DV), lambda n: (0, 0))],
        out_specs=[pl.BlockSpec((CHUNK, HEADS * GLA_DV), lambda n: (n, 0)),
                   pl.BlockSpec((None, HEADS, GLA_DV, GLA_DK), lambda n: (n, 0, 0, 0))],
        out_shape=[jax.ShapeDtypeStruct((tp, HEADS * GLA_DV), BF16),
                   jax.ShapeDtypeStruct((nch, HEADS, GLA_DV, GLA_DK), BF16)],
        scratch_shapes=[pltpu.VMEM((HEADS, GLA_DV, GLA_DK), F32)],
        compiler_params=_cp(1))(u, wg, bg, gain)


def _gla_bwd(u, wg, bg, gain, states, d_on, name):
    tp = u.shape[0]
    nch = tp // CHUNK
    rev = lambda n: nch - 1 - n

    def body(u_ref, wg_ref, bg_ref, gain_ref, st_ref, don_ref, du_ref, dwg_ref, dbg_ref, dgain_ref, dstate_ref):
        @pl.when(pl.program_id(0) == 0)
        def _():
            dstate_ref[...] = jnp.zeros_like(dstate_ref)
            dwg_ref[...] = jnp.zeros_like(dwg_ref)
            dbg_ref[...] = jnp.zeros_like(dbg_ref)
            dgain_ref[...] = jnp.zeros_like(dgain_ref)

        mask = _row_mask(rev(pl.program_id(0)))
        mats = _cum_mats()
        dz_sum = jnp.zeros((CHUNK, 128), F32)
        for hd in range(HEADS):
            kc = slice(GLA_DK * hd, GLA_DK * (hd + 1))
            vc = slice(GLA_DV * hd, GLA_DV * (hd + 1))
            _, vjp = jax.vjp(lambda *a: _gla_head(*a, mask, *mats), *_gla_pieces(u_ref, hd),
                             st_ref[hd].astype(F32), wg_ref[:, kc].astype(F32), bg_ref[:, kc], gain_ref[:, vc])
            dq, dk, dv, dg, dz, dstate, dwg, dbg, dgain = vjp((don_ref[:, vc].astype(F32), dstate_ref[hd]))
            dstate_ref[hd] = dstate
            dwg_ref[:, kc] += dwg
            dbg_ref[:, kc] += dbg
            dgain_ref[:, vc] += dgain
            dz_sum = dz_sum + dz
            put = lambda a, t: du_ref.__setitem__((slice(None), slice(a, a + t.shape[1])), (t * mask).astype(BF16))
            put(GLA_DK * hd, dq)
            put(_GLA_K0 + GLA_DK * hd, dk)
            put(_GLA_V0 + GLA_DV * hd, dv)
            put(_GLA_G0 + GLA_DV * hd, dg)
        du_ref[:, _GLA_Z0:] = dz_sum.astype(BF16)

    full = lambda r, c: pl.BlockSpec((r, c), lambda n: (0, 0))
    return pl.pallas_call(
        body, name=name, grid=(nch,),
        in_specs=[pl.BlockSpec((CHUNK, GLA_U), lambda n: (rev(n), 0)), full(128, HEADS * GLA_DK),
                  full(1, HEADS * GLA_DK), full(1, HEADS * GLA_DV),
                  pl.BlockSpec((None, HEADS, GLA_DV, GLA_DK), lambda n: (rev(n), 0, 0, 0)),
                  pl.BlockSpec((CHUNK, HEADS * GLA_DV), lambda n: (rev(n), 0))],
        out_specs=[pl.BlockSpec((CHUNK, GLA_U), lambda n: (rev(n), 0)), full(128, HEADS * GLA_DK),
                   full(1, HEADS * GLA_DK), full(1, HEADS * GLA_DV)],
        out_shape=[jax.ShapeDtypeStruct((tp, GLA_U), BF16), jax.ShapeDtypeStruct((128, HEADS * GLA_DK), F32),
                   jax.ShapeDtypeStruct((1, HEADS * GLA_DK), F32), jax.ShapeDtypeStruct((1, HEADS * GLA_DV), F32)],
        scratch_shapes=[pltpu.VMEM((HEADS, GLA_DV, GLA_DK), F32)],
        compiler_params=_cp(1))(u, wg, bg, gain, states, d_on)


def _ffn_fwd(h, gain, w_in, w_out, tag):
    hn, ug, uu, act = _norm_ffn_in(h, gain, w_in, f"{tag}_in")
    return _out_proj(act, w_out, h, 0.5, f"{tag}_out"), (h, hn, ug, uu, act)


def _ffn_bwd(dh, saved, gain, w_in, w_out, tag):
    h, hn, ug, uu, act = saved
    du = _ffn_dact(dh, w_out, ug, uu, f"{tag}_dact")
    d_w_out = _wgrad(act, dh, bm=D_FF // 2, bn=D, scale=0.5, sharded=False, name=f"{tag}_dwout")
    d_w_in = _wgrad(hn, du, bm=D, bn=w_in.shape[2], scale=1.0, sharded=True, name=f"{tag}_dwin")
    dh_in, d_gain = _dgrad_norm(du, w_in, h, gain, dh, f"{tag}_dnorm")
    return dh_in, d_gain, d_w_in, d_w_out


def _sequence_grads(x, target, w):
    t = x.shape[0]
    h = jnp.concatenate([jnp.zeros((FRONT - N_META, D), F32), w["meta"], x], axis=0)
    row = lambda v: v.reshape(1, -1)

    saved = []
    for layer in range(2):
        h, s1 = _ffn_fwd(h, row(w["norm_ffn1"][layer]), w["ffn1_in"][layer], w["ffn1_out"][layer], f"l{layer}_ffn1")
        if layer == 0:
            hn, u = _norm_proj(h, row(w["norm_mix"][layer]), w["ret_in"], "ret_in")
            on, states = _ret_fwd(u, w["ret_gain"], "ret_fwd")
            h_mix = _out_proj(on, w["ret_out"], h, 1.0, "ret_out")
        else:
            hn, u = _norm_proj(h, row(w["norm_mix"][layer]), w["gla_in"], "gla_in")
            on, states = _gla_fwd(u, w["gla_wg"], w["gla_bg"], w["gla_gain"], "gla_fwd")
            h_mix = _out_proj(on, w["gla_out"], h, 1.0, "gla_out")
        s2 = (h, hn, u, on, states)
        h, s3 = _ffn_fwd(h_mix, row(w["norm_ffn2"][layer]), w["ffn2_in"][layer], w["ffn2_out"][layer], f"l{layer}_ffn2")
        saved.append((s1, s2, s3))

    dh, d_final, loss = _loss_head(h, row(w["final_norm"]), target, "loss_head")

    g = {"final_norm": d_final, "norm_ffn1": [None, None], "norm_mix": [None, None], "norm_ffn2": [None, None],
         "ffn1_in": [None, None], "ffn1_out": [None, None], "ffn2_in": [None, None], "ffn2_out": [None, None]}
    for layer in (1, 0):
        s1, s2, s3 = saved[layer]
        dh, g["norm_ffn2"][layer], g["ffn2_in"][layer], g["ffn2_out"][layer] = _ffn_bwd(
            dh, s3, row(w["norm_ffn2"][layer]), w["ffn2_in"][layer], w["ffn2_out"][layer], f"l{layer}_ffn2")
        h_in, hn, u, on, states = s2
        if layer == 0:
            d_on = _dgrad(dh, w["ret_out"], "ret_don")
            g["ret_out"] = _wgrad(on, dh, bm=D, bn=D, scale=1.0, sharded=False, name="ret_dwout")
            du, g["ret_gain"] = _ret_bwd(u, w["ret_gain"], states, d_on, "ret_bwd")
            g["ret_in"] = _wgrad(hn, du, bm=D, bn=w["ret_in"].shape[2], scale=1.0, sharded=True, name="ret_dwin")
            dh, g["norm_mix"][layer] = _dgrad_norm(du, w["ret_in"], h_in, row(w["norm_mix"][layer]), dh, "ret_dnorm")
        else:
            d_on = _dgrad(dh, w["gla_out"], "gla_don")
            g["gla_out"] = _wgrad(on, dh, bm=D, bn=D, scale=1.0, sharded=False, name="gla_dwout")
            du, g["gla_wg"], g["gla_bg"], g["gla_gain"] = _gla_bwd(u, w["gla_wg"], w["gla_bg"], w["gla_gain"],
                                                                   states, d_on, "gla_bwd")
            g["gla_in"] = _wgrad(hn, du, bm=D, bn=GLA_U // 5, scale=1.0, sharded=False, name="gla_dwin")
            dh, g["norm_mix"][layer] = _dgrad_norm(du, w["gla_in"], h_in, row(w["norm_mix"][layer]), dh, "gla_dnorm")
        dh, g["norm_ffn1"][layer], g["ffn1_in"][layer], g["ffn1_out"][layer] = _ffn_bwd(
            dh, s1, row(w["norm_ffn1"][layer]), w["ffn1_in"][layer], w["ffn1_out"][layer], f"l{layer}_ffn1")
    return loss, dh[FRONT:], dh[FRONT - N_META:FRONT], g


_HBM = pl.BlockSpec(memory_space=pl.ANY)


def _place():
    return lax.axis_index("x"), lax.axis_index("y"), lax.axis_index("c")


def _flip(v, bit):
    return 1 - v if bit else v


DMA_CHUNK_BYTES = 128 * 1024


def _row_chunks(ref):
    rows, cols = ref.shape
    step = _row_tile(rows, max(16, DMA_CHUNK_BYTES // (cols * ref.dtype.itemsize)))
    return [pl.ds(a, step) for a in range(0, rows, step)]


def _send(src, dst, send_sem, recv_sem, peer):
    for rows in _row_chunks(src):
        pltpu.make_async_remote_copy(src_ref=src.at[rows], dst_ref=dst.at[rows], send_sem=send_sem, recv_sem=recv_sem,
                                     device_id=peer, device_id_type=MESH).start()
    return pltpu.make_async_remote_copy(src_ref=src, dst_ref=dst, send_sem=send_sem, recv_sem=recv_sem,
                                        device_id=peer, device_id_type=MESH)


def _gather_chips(shards):
    n = len(shards)
    offsets = [(1, 0), (0, 1), (1, 1)]

    def body(*refs):
        src, dst = refs[:n], refs[n:2 * n]
        send_sems, recv_sems, local_sems = refs[2 * n:]
        x, y, c = _place()
        mine = 2 * x + y
        copies = []
        for t in range(n):
            local = pltpu.make_async_copy(src[t], dst[t].at[mine], local_sems.at[t])
            local.start()
            copies.append(local)
            for j, (fx, fy) in enumerate(offsets):
                copies.append(_send(src[t], dst[t].at[mine], send_sems.at[t, j], recv_sems.at[t, j],
                                    (_flip(x, fx), _flip(y, fy), c)))
        for cp in copies:
            cp.wait()

    return pl.pallas_call(
        body, name="gather_chips", in_specs=[_HBM] * n, out_specs=[_HBM] * n,
        out_shape=[jax.ShapeDtypeStruct((N_CHIPS,) + s.shape, s.dtype) for s in shards],
        scratch_shapes=[pltpu.SemaphoreType.DMA((n, 3)), pltpu.SemaphoreType.DMA((n, 3)), pltpu.SemaphoreType.DMA((n,))],
    )(*shards)


_PEER_FLIPS = [(fx, fy, fc) for fx in (0, 1) for fy in (0, 1) for fc in (0, 1)][1:]


def _scatter_devices(pieces, whole):
    n, m = len(pieces), len(whole)

    def body(*refs):
        src, dst = refs[:n + m], refs[n + m:2 * (n + m)]
        send_sems, recv_sems, local_sems = refs[2 * (n + m):]
        x, y, c = _place()
        me = 4 * x + 2 * y + c
        copies = []
        for t in range(n + m):
            part = (lambda px, py, pc: src[t].at[2 * px + py, pc]) if t < n else (lambda px, py, pc: src[t])
            local = pltpu.make_async_copy(part(x, y, c), dst[t].at[me], local_sems.at[t])
            local.start()
            copies.append(local)
            for j, (fx, fy, fc) in enumerate(_PEER_FLIPS):
                peer = (_flip(x, fx), _flip(y, fy), _flip(c, fc))
                copies.append(_send(part(*peer), dst[t].at[me], send_sems.at[t, j], recv_sems.at[t, j], peer))
        for cp in copies:
            cp.wait()

    arrays = list(pieces) + list(whole)
    out_shape = [jax.ShapeDtypeStruct((N_DEV,) + a.shape[2:], a.dtype) for a in pieces]
    out_shape += [jax.ShapeDtypeStruct((N_DEV,) + a.shape, a.dtype) for a in whole]
    return pl.pallas_call(
        body, name="scatter_devices", in_specs=[_HBM] * (n + m), out_specs=[_HBM] * (n + m), out_shape=out_shape,
        scratch_shapes=[pltpu.SemaphoreType.DMA((n + m, 7)), pltpu.SemaphoreType.DMA((n + m, 7)),
                        pltpu.SemaphoreType.DMA((n + m,))],
    )(*arrays)


def _swap_cores(halves):
    n = len(halves)

    def body(*refs):
        src, dst = refs[:n], refs[n:2 * n]
        send_sems, recv_sems, local_sems = refs[2 * n:]
        x, y, c = _place()
        copies = []
        for t in range(n):
            local = pltpu.make_async_copy(src[t], dst[t].at[c], local_sems.at[t])
            local.start()
            copies.append(local)
            copies.append(_send(src[t], dst[t].at[c], send_sems.at[t], recv_sems.at[t], (x, y, 1 - c)))
        for cp in copies:
            cp.wait()

    return pl.pallas_call(
        body, name="swap_cores", in_specs=[_HBM] * n, out_specs=[_HBM] * n,
        out_shape=[jax.ShapeDtypeStruct((2,) + a.shape, a.dtype) for a in halves],
        scratch_shapes=[pltpu.SemaphoreType.DMA((n,)), pltpu.SemaphoreType.DMA((n,)), pltpu.SemaphoreType.DMA((n,))],
    )(*halves)


def _row_tile(rows, cap):
    fits = [t for t in range(16, cap + 1, 16) if rows % t == 0]
    return fits[-1] if fits else rows


def _sum_slots(a, name):
    _, r, c = a.shape
    tr = _row_tile(r, 384)

    def body(a_ref, o_ref):
        s = a_ref[0].astype(F32)
        for k in range(1, N_DEV):
            s = s + a_ref[k].astype(F32)
        o_ref[...] = s

    return pl.pallas_call(
        body, name=name, grid=(r // tr,),
        in_specs=[pl.BlockSpec((N_DEV, tr, c), lambda i: (0, i, 0))],
        out_specs=pl.BlockSpec((tr, c), lambda i: (i, 0)),
        out_shape=jax.ShapeDtypeStruct((r, c), F32),
        compiler_params=_cp(1))(a)


def _adamw(w, g, m, v, name):
    r, c = w.shape
    tr = _row_tile(r, 256)

    def body(w_ref, g_ref, m_ref, v_ref, d_ref, nm_ref, nv_ref):
        gv = g_ref[...]
        nm = ADAM_B1 * m_ref[...] + (1.0 - ADAM_B1) * gv
        nv = ADAM_B2 * v_ref[...] + (1.0 - ADAM_B2) * (gv * gv)
        m_hat = nm / (1.0 - ADAM_B1 ** ADAM_STEP)
        v_hat = nv / (1.0 - ADAM_B2 ** ADAM_STEP)
        d_ref[...] = -ADAM_LR * (m_hat / (jnp.sqrt(v_hat) + ADAM_EPS) + ADAM_WD * w_ref[...])
        nm_ref[...] = nm
        nv_ref[...] = nv

    spec = pl.BlockSpec((tr, c), lambda i: (i, 0))
    return pl.pallas_call(
        body, name=name, grid=(r // tr,), in_specs=[spec] * 4, out_specs=[spec] * 3,
        out_shape=[jax.ShapeDtypeStruct((r, c), F32)] * 3,
        compiler_params=_cp(1))(w, g, m, v)


_SMALL = ["meta_tokens", "ret_head_norm", "gla_w_gate", "gla_b_gate", "gla_head_norm"]
_LOCAL_SMALL = ["meta_tokens", "norm_ffn1", "norm_mix", "norm_ffn2", "ret_head_norm", "gla_w_gate", "gla_b_gate",
                "gla_head_norm", "final_norm"]
_BIG = ["ffn1_w_in", "ffn1_w_out", "ffn2_w_in", "ffn2_w_out", "ret_w_in", "ret_w_out", "gla_w_in", "gla_w_out"]
_WEIGHTS = ["meta_tokens", "norm_ffn1", "ffn1_w_in", "ffn1_w_out", "norm_mix", "norm_ffn2", "ffn2_w_in", "ffn2_w_out",
            "ret_w_in", "ret_head_norm", "ret_w_out", "gla_w_in", "gla_w_gate", "gla_b_gate", "gla_head_norm",
            "gla_w_out", "final_norm"]


def _pack_rows(arrays, width):
    flat = jnp.concatenate([a.reshape(-1) for a in arrays])
    pad = -flat.shape[0] % (8 * width)
    return jnp.pad(flat, (0, pad)).reshape(-1, width)


def _unpack_rows(packed, shapes):
    flat, out, at = packed.reshape(-1), [], 0
    for s in shapes:
        size = 1
        for dim in s:
            size *= dim
        out.append(flat[at:at + size].reshape(s))
        at += size
    return out


def _gather_all(p):
    b = lambda a: a.astype(BF16)
    shards = []
    for name in ("ffn1_w_in", "ffn2_w_in", "ffn1_w_out", "ffn2_w_out"):
        shards += [b(p[name][0]), b(p[name][1])]
    shards += [b(p["ret_w_in"][0]), b(p["ret_w_out"][0]), b(p["gla_w_in"][0]), b(p["gla_w_out"][0])]
    small_shapes = [p[name].shape for name in _SMALL]
    shards.append(_pack_rows([p[name] for name in _SMALL], 128))
    got = _gather_chips(shards)

    w = {"ffn1_in": got[0:2], "ffn2_in": got[2:4],
         "ffn1_out": [a.reshape(D_FF, D) for a in got[4:6]], "ffn2_out": [a.reshape(D_FF, D) for a in got[6:8]],
         "ret_in": got[8], "ret_out": got[9].reshape(HEADS * RET_DV, D), "gla_out": got[11].reshape(HEADS * GLA_DV, D)}
    gla_in = jnp.moveaxis(got[10], 0, 1).reshape(D, -1)
    w["gla_in"] = jnp.pad(gla_in, ((0, 0), (0, GLA_U - gla_in.shape[1])))[None]
    small = [jnp.stack(parts) for parts in zip(*[_unpack_rows(got[12][s], small_shapes) for s in range(N_CHIPS)])]
    cat = lambda a: jnp.moveaxis(a, 0, -2).reshape(a.shape[1:-1] + (-1,))
    meta, ret_gain, wg, bg, gla_gain = [cat(a) for a in small]
    w["meta"] = meta
    w["ret_gain"] = ret_gain.reshape(1, -1)
    w["gla_wg"] = jnp.pad(wg[0], ((0, 128 - GLA_RANK), (0, 0))).astype(BF16)
    w["gla_bg"] = bg.reshape(1, -1)
    w["gla_gain"] = gla_gain.reshape(1, -1)
    for name in ("norm_ffn1", "norm_mix", "norm_ffn2", "final_norm"):
        w[name] = p[name]
    return w


def _reduce_grads(loss, d_meta, g):
    halves = lambda a: a.reshape(N_CHIPS, 2, -1, a.shape[-1])
    pieces, names = [], []
    for name in ("ffn1_in", "ffn2_in", "ffn1_out", "ffn2_out"):
        for layer in range(2):
            pieces.append(halves(g[name][layer]))
            names.append((name, layer))
    gla_in = jnp.moveaxis(g["gla_in"][:, :4 * 772].reshape(D, N_CHIPS, 772), 1, 0)
    for name, a in (("ret_in", g["ret_in"]), ("ret_out", g["ret_out"]), ("gla_in", gla_in), ("gla_out", g["gla_out"])):
        pieces.append(halves(a))
        names.append((name, 0))
    small = [d_meta, g["norm_ffn1"][0], g["norm_ffn1"][1], g["norm_mix"][0], g["norm_mix"][1], g["norm_ffn2"][0],
             g["norm_ffn2"][1], g["final_norm"], g["ret_gain"], g["gla_wg"][:GLA_RANK], g["gla_bg"], g["gla_gain"],
             loss[:, :1]]
    small_shapes = [a.shape for a in small]
    got = _scatter_devices(pieces, [_pack_rows(small, D)])

    sums = [_sum_slots(a, f"sum_{name}{layer}") for a, (name, layer) in zip(got[:-1], names)]
    swapped = _swap_cores(sums)
    big = {}
    for a, (name, layer) in zip(swapped, names):
        big.setdefault(name, []).append(a.reshape(a.shape[0] * a.shape[1], a.shape[2]))
    small_sum = _unpack_rows(_sum_slots(got[-1], "sum_small"), small_shapes)
    return big, small_sum


def kernel(x, meta_tokens, norm_ffn1, ffn1_w_in, ffn1_w_out, norm_mix, norm_ffn2, ffn2_w_in, ffn2_w_out, ret_w_in, ret_head_norm, ret_w_out, gla_w_in, gla_w_gate, gla_b_gate, gla_head_norm, gla_w_out, final_norm, loss_target, m_meta_tokens, m_norm_ffn1, m_ffn1_w_in, m_ffn1_w_out, m_norm_mix, m_norm_ffn2, m_ffn2_w_in, m_ffn2_w_out, m_ret_w_in, m_ret_head_norm, m_ret_w_out, m_gla_w_in, m_gla_w_gate, m_gla_b_gate, m_gla_head_norm, m_gla_w_out, m_final_norm, v_meta_tokens, v_norm_ffn1, v_ffn1_w_in, v_ffn1_w_out, v_norm_mix, v_norm_ffn2, v_ffn2_w_in, v_ffn2_w_out, v_ret_w_in, v_ret_head_norm, v_ret_w_out, v_gla_w_in, v_gla_w_gate, v_gla_b_gate, v_gla_head_norm, v_gla_w_out, v_final_norm):
    p = dict(meta_tokens=meta_tokens, norm_ffn1=norm_ffn1, ffn1_w_in=ffn1_w_in, ffn1_w_out=ffn1_w_out, norm_mix=norm_mix,
             norm_ffn2=norm_ffn2, ffn2_w_in=ffn2_w_in, ffn2_w_out=ffn2_w_out, ret_w_in=ret_w_in,
             ret_head_norm=ret_head_norm, ret_w_out=ret_w_out, gla_w_in=gla_w_in, gla_w_gate=gla_w_gate,
             gla_b_gate=gla_b_gate, gla_head_norm=gla_head_norm, gla_w_out=gla_w_out, final_norm=final_norm)
    m = dict(zip(_WEIGHTS, (m_meta_tokens, m_norm_ffn1, m_ffn1_w_in, m_ffn1_w_out, m_norm_mix, m_norm_ffn2, m_ffn2_w_in,
                            m_ffn2_w_out, m_ret_w_in, m_ret_head_norm, m_ret_w_out, m_gla_w_in, m_gla_w_gate,
                            m_gla_b_gate, m_gla_head_norm, m_gla_w_out, m_final_norm)))
    v = dict(zip(_WEIGHTS, (v_meta_tokens, v_norm_ffn1, v_ffn1_w_in, v_ffn1_w_out, v_norm_mix, v_norm_ffn2, v_ffn2_w_in,
                            v_ffn2_w_out, v_ret_w_in, v_ret_head_norm, v_ret_w_out, v_gla_w_in, v_gla_w_gate,
                            v_gla_b_gate, v_gla_head_norm, v_gla_w_out, v_final_norm)))

    w = _gather_all(p)
    loss, d_x, d_meta, g = _sequence_grads(x[0], loss_target[0], w)
    big, small = _reduce_grads(loss, d_meta, g)

    chip = 2 * lax.axis_index("x") + lax.axis_index("y")
    cols = lambda a, n: lax.dynamic_slice_in_dim(a, chip * n, n, axis=a.ndim - 1)
    (s_meta, s_n1a, s_n1b, s_nma, s_nmb, s_n2a, s_n2b, s_final, s_ret_gain, s_wg, s_bg, s_gla_gain, s_loss) = small
    grads = {
        "meta_tokens": cols(s_meta, 256), "norm_ffn1": jnp.concatenate([s_n1a, s_n1b]),
        "norm_mix": jnp.concatenate([s_nma, s_nmb]), "norm_ffn2": jnp.concatenate([s_n2a, s_n2b]),
        "final_norm": s_final.reshape(D),
        "ret_head_norm": cols(s_ret_gain.reshape(1, HEADS, RET_DV), RET_DV // N_CHIPS),
        "gla_w_gate": cols(s_wg, GLA_DK)[None], "gla_b_gate": cols(s_bg, GLA_DK),
        "gla_head_norm": cols(s_gla_gain.reshape(1, HEADS, GLA_DV), GLA_DV // N_CHIPS),
        "ffn1_w_in": jnp.stack(big["ffn1_in"]), "ffn1_w_out": jnp.stack(big["ffn1_out"]),
        "ffn2_w_in": jnp.stack(big["ffn2_in"]), "ffn2_w_out": jnp.stack(big["ffn2_out"]),
        "ret_w_in": big["ret_in"][0][None], "ret_w_out": big["ret_out"][0][None],
        "gla_w_in": big["gla_in"][0][None], "gla_w_out": big["gla_out"][0][None],
    }

    delta, new_m, new_v = {}, {}, {}
    for name in _BIG:
        shape = p[name].shape
        flat = lambda a: a.reshape(-1, shape[-1])
        out = _adamw(flat(p[name]), flat(grads[name]), flat(m[name]), flat(v[name]), f"adamw_{name}")
        delta[name], new_m[name], new_v[name] = [a.reshape(shape) for a in out]
    packed = [_pack_rows([d[name] for name in _LOCAL_SMALL], 128) for d in (p, grads, m, v)]
    out = _adamw(*packed, "adamw_small")
    shapes = [p[name].shape for name in _LOCAL_SMALL]
    for d, a in zip((delta, new_m, new_v), out):
        d.update(zip(_LOCAL_SMALL, _unpack_rows(a, shapes)))

    return (s_loss.reshape(()), d_x[None], *[grads[n] for n in _WEIGHTS], *[delta[n] for n in _WEIGHTS],
            *[new_m[n] for n in _WEIGHTS], *[new_v[n] for n in _WEIGHTS])
```

```python
import functools

import jax
import jax.numpy as jnp
from jax import lax
from jax.experimental import pallas as pl
from jax.experimental.pallas import tpu as pltpu

F32, BF16 = jnp.float32, jnp.bfloat16
MESH = pl.DeviceIdType.MESH

D = 1024
N_META = 16
CHUNK = 64
FRONT = 256
D_FF = 2816
EPS = 1e-6
HEADS = 4
RET_DK, RET_DV = 256, 512
GLA_DK, GLA_DV = 128, 256
GLA_RANK = 16
GLA_TAU = 16.0
GLA_U = 3200
ROPE_BASE = 10000.0
N_CHIPS = 4
N_DEV = 8

ADAM_LR, ADAM_B1, ADAM_B2, ADAM_EPS, ADAM_WD, ADAM_STEP = 0.001, 0.9, 0.999, 1e-08, 0.01, 10

VMEM_LIMIT_BYTES = 56 * 1024 * 1024
TM = 768
TM_SMALL = 256


def _cp(n_axes):
    return pltpu.CompilerParams(dimension_semantics=("arbitrary",) * n_axes, vmem_limit_bytes=VMEM_LIMIT_BYTES)


def _dg(a, b, ca, cb):
    return lax.dot_general(a.astype(BF16), b.astype(BF16), (((ca,), (cb,)), ((), ())), preferred_element_type=F32)


@jax.custom_vjp
def _nn(a, b):
    return _dg(a, b, 1, 0)


@jax.custom_vjp
def _nt(a, b):
    return _dg(a, b, 1, 1)


@jax.custom_vjp
def _tn(a, b):
    return _dg(a, b, 0, 0)


_nn.defvjp(lambda a, b: (_nn(a, b), (a, b)), lambda res, g: (_nt(g, res[1]), _tn(res[0], g)))
_nt.defvjp(lambda a, b: (_nt(a, b), (a, b)), lambda res, g: (_nn(g, res[1]), _tn(g, res[0])))
_tn.defvjp(lambda a, b: (_tn(a, b), (a, b)), lambda res, g: (_nt(res[1], g), _nn(res[0], g)))


def _split3_dot(m, a):
    a1 = a.astype(BF16)
    r1 = a - a1.astype(F32)
    a2 = r1.astype(BF16)
    a3 = (r1 - a2.astype(F32)).astype(BF16)
    dot = lambda p: jnp.dot(m, p, preferred_element_type=F32)
    return dot(a1) + dot(a2) + dot(a3)


@jax.custom_vjp
def _cum(m, mt, a):
    return _split3_dot(m, a)


_cum.defvjp(lambda m, mt, a: (_split3_dot(m, a), (m, mt)),
            lambda res, g: (jnp.zeros_like(res[0]), jnp.zeros_like(res[1]), _split3_dot(res[1], g)))


def _sigmoid(x):
    return 1.0 / (1.0 + jnp.exp(-x))


def _rms(x):
    return lax.rsqrt(jnp.mean(x * x, axis=-1, keepdims=True) + EPS)


def _rmsnorm_bwd(dy, x, gain):
    r = _rms(x)
    xhat = x * r
    dxh = dy * gain
    return r * (dxh - xhat * jnp.mean(dxh * xhat, axis=-1, keepdims=True)), xhat


def _norm_proj(h, gain, w, name):
    tp, d = h.shape
    s, _, ns = w.shape

    def body(h_ref, g_ref, w_ref, hn_ref, u_ref):
        @pl.when(pl.program_id(1) == 0)
        def _():
            x = h_ref[...]
            hn_ref[...] = (x * _rms(x) * g_ref[...]).astype(BF16)

        u_ref[...] = jnp.dot(hn_ref[...], w_ref[...], preferred_element_type=F32).astype(BF16)

    return pl.pallas_call(
        body, name=name, grid=(tp // TM, s),
        in_specs=[pl.BlockSpec((TM, d), lambda i, j: (i, 0)), pl.BlockSpec((1, d), lambda i, j: (0, 0)),
                  pl.BlockSpec((None, d, ns), lambda i, j: (j, 0, 0))],
        out_specs=[pl.BlockSpec((TM, d), lambda i, j: (i, 0)), pl.BlockSpec((TM, ns), lambda i, j: (i, j))],
        out_shape=[jax.ShapeDtypeStruct((tp, d), BF16), jax.ShapeDtypeStruct((tp, s * ns), BF16)],
        compiler_params=_cp(2))(h, gain, w)


def _norm_ffn_in(h, gain, w, name):
    tp, d = h.shape
    s, _, ns = w.shape
    half = s // 2

    def body(h_ref, g_ref, wg_ref, wu_ref, hn_ref, ug_ref, uu_ref, act_ref):
        @pl.when(pl.program_id(1) == 0)
        def _():
            x = h_ref[...]
            hn_ref[...] = (x * _rms(x) * g_ref[...]).astype(BF16)

        a = hn_ref[...]
        g = jnp.dot(a, wg_ref[...], preferred_element_type=F32)
        u = jnp.dot(a, wu_ref[...], preferred_element_type=F32)
        ug_ref[...] = g.astype(BF16)
        uu_ref[...] = u.astype(BF16)
        act_ref[...] = (g * _sigmoid(g) * u).astype(BF16)

    wide = jax.ShapeDtypeStruct((tp, half * ns), BF16)
    return pl.pallas_call(
        body, name=name, grid=(tp // TM, half),
        in_specs=[pl.BlockSpec((TM, d), lambda i, j: (i, 0)), pl.BlockSpec((1, d), lambda i, j: (0, 0)),
                  pl.BlockSpec((None, d, ns), lambda i, j: (j, 0, 0)),
                  pl.BlockSpec((None, d, ns), lambda i, j: (j + half, 0, 0))],
        out_specs=[pl.BlockSpec((TM, d), lambda i, j: (i, 0))] + [pl.BlockSpec((TM, ns), lambda i, j: (i, j))] * 3,
        out_shape=[jax.ShapeDtypeStruct((tp, d), BF16), wide, wide, wide],
        compiler_params=_cp(2))(h, gain, w, w)


def _out_proj(a, w, h, scale, name):
    tp, k = a.shape
    d = w.shape[1]

    def body(a_ref, w_ref, h_ref, o_ref):
        o_ref[...] = h_ref[...] + scale * jnp.dot(a_ref[...], w_ref[...], preferred_element_type=F32)

    return pl.pallas_call(
        body, name=name, grid=(tp // TM,),
        in_specs=[pl.BlockSpec((TM, k), lambda i: (i, 0)), pl.BlockSpec((k, d), lambda i: (0, 0)),
                  pl.BlockSpec((TM, d), lambda i: (i, 0))],
        out_specs=pl.BlockSpec((TM, d), lambda i: (i, 0)),
        out_shape=jax.ShapeDtypeStruct((tp, d), F32),
        compiler_params=_cp(1))(a, w, h)


def _ffn_dact(dh, w_out, ug, uu, name):
    tp, d = dh.shape
    ff = w_out.shape[0]
    tm = TM_SMALL

    def body(dh_ref, w_ref, ug_ref, uu_ref, du_ref):
        dy = (0.5 * dh_ref[...]).astype(BF16)
        dact = lax.dot_general(dy, w_ref[...], (((1,), (1,)), ((), ())), preferred_element_type=F32)
        g = ug_ref[...].astype(F32)
        u = uu_ref[...].astype(F32)
        sg = _sigmoid(g)
        du_ref[:, :ff] = (dact * u * (sg * (1.0 + g * (1.0 - sg)))).astype(BF16)
        du_ref[:, ff:] = (dact * (g * sg)).astype(BF16)

    return pl.pallas_call(
        body, name=name, grid=(tp // tm,),
        in_specs=[pl.BlockSpec((tm, d), lambda i: (i, 0)), pl.BlockSpec((ff, d), lambda i: (0, 0)),
                  pl.BlockSpec((tm, ff), lambda i: (i, 0)), pl.BlockSpec((tm, ff), lambda i: (i, 0))],
        out_specs=pl.BlockSpec((tm, 2 * ff), lambda i: (i, 0)),
        out_shape=jax.ShapeDtypeStruct((tp, 2 * ff), BF16),
        compiler_params=_cp(1))(dh, w_out, ug, uu)


def _dgrad(dh, w, name):
    tp, d = dh.shape
    k = w.shape[0]

    def body(dh_ref, w_ref, o_ref):
        o_ref[...] = lax.dot_general(dh_ref[...].astype(BF16), w_ref[...], (((1,), (1,)), ((), ())),
                                     preferred_element_type=F32).astype(BF16)

    return pl.pallas_call(
        body, name=name, grid=(tp // TM,),
        in_specs=[pl.BlockSpec((TM, d), lambda i: (i, 0)), pl.BlockSpec((k, d), lambda i: (0, 0))],
        out_specs=pl.BlockSpec((TM, k), lambda i: (i, 0)),
        out_shape=jax.ShapeDtypeStruct((tp, k), BF16),
        compiler_params=_cp(1))(dh, w)


def _wgrad(a, b, *, bm, bn, scale, sharded, name):
    tp, m = a.shape
    n = b.shape[1]
    nk = tp // TM

    def body(a_ref, b_ref, o_ref, acc_ref):
        k = pl.program_id(2)

        @pl.when(k == 0)
        def _():
            acc_ref[...] = jnp.zeros_like(acc_ref)

        bb = b_ref[...]
        if scale != 1.0:
            bb = scale * bb
        acc_ref[...] += lax.dot_general(a_ref[...], bb.astype(BF16), (((0,), (0,)), ((), ())),
                                        preferred_element_type=F32)

        @pl.when(k == nk - 1)
        def _():
            o_ref[...] = acc_ref[...].astype(BF16)

    if sharded:
        assert m == bm
        out_spec = pl.BlockSpec((None, bm, bn), lambda i, j, k: (j, 0, 0))
        out_shape = jax.ShapeDtypeStruct((n // bn, m, bn), BF16)
    else:
        out_spec = pl.BlockSpec((bm, bn), lambda i, j, k: (i, j))
        out_shape = jax.ShapeDtypeStruct((m, n), BF16)
    return pl.pallas_call(
        body, name=name, grid=(m // bm, n // bn, nk),
        in_specs=[pl.BlockSpec((TM, bm), lambda i, j, k: (k, i)), pl.BlockSpec((TM, bn), lambda i, j, k: (k, j))],
        out_specs=out_spec, out_shape=out_shape,
        scratch_shapes=[pltpu.VMEM((bm, bn), F32)],
        compiler_params=_cp(3))(a, b)


def _dgrad_norm(du, w, h, gain, dh_out, name):
    tp, d = h.shape
    s, _, ns = w.shape

    def body(du_ref, w_ref, h_ref, g_ref, dho_ref, dhi_ref, dg_ref, acc_ref):
        i, k = pl.program_id(0), pl.program_id(1)

        @pl.when(k == 0)
        def _():
            acc_ref[...] = jnp.zeros_like(acc_ref)

        @pl.when((i == 0) & (k == 0))
        def _():
            dg_ref[...] = jnp.zeros_like(dg_ref)

        acc_ref[...] += lax.dot_general(du_ref[...], w_ref[...], (((1,), (1,)), ((), ())),
                                        preferred_element_type=F32)

        @pl.when(k == s - 1)
        def _():
            dhn = acc_ref[...]
            dx, xhat = _rmsnorm_bwd(dhn, h_ref[...], g_ref[...])
            dg_ref[...] += jnp.sum(dhn * xhat, axis=0, keepdims=True)
            dhi_ref[...] = dho_ref[...] + dx

    return pl.pallas_call(
        body, name=name, grid=(tp // TM, s),
        in_specs=[pl.BlockSpec((TM, ns), lambda i, k: (i, k)), pl.BlockSpec((None, d, ns), lambda i, k: (k, 0, 0)),
                  pl.BlockSpec((TM, d), lambda i, k: (i, 0)), pl.BlockSpec((1, d), lambda i, k: (0, 0)),
                  pl.BlockSpec((TM, d), lambda i, k: (i, 0))],
        out_specs=[pl.BlockSpec((TM, d), lambda i, k: (i, 0)), pl.BlockSpec((1, d), lambda i, k: (0, 0))],
        out_shape=[jax.ShapeDtypeStruct((tp, d), F32), jax.ShapeDtypeStruct((1, d), F32)],
        scratch_shapes=[pltpu.VMEM((TM, d), F32)],
        compiler_params=_cp(2))(du, w, h, gain, dh_out)


def _loss_head(h, gain, target, name):
    tp, d = h.shape
    tm = TM_SMALL
    front_tiles = FRONT // tm

    def body(h_ref, g_ref, t_ref, dh_ref, dg_ref, loss_ref):
        i = pl.program_id(0)

        @pl.when(i == 0)
        def _():
            dg_ref[...] = jnp.zeros_like(dg_ref)
            loss_ref[...] = jnp.zeros_like(loss_ref)

        x = h_ref[...]
        gain_v = g_ref[...]
        y = x * _rms(x) * gain_v
        err = jnp.where(i >= front_tiles, y - t_ref[...], 0.0)
        loss_ref[...] += 0.5 * jnp.sum(jnp.mean(err * err, axis=-1, keepdims=True), axis=0, keepdims=True)
        dy = err * (1.0 / d)
        dx, xhat = _rmsnorm_bwd(dy, x, gain_v)
        dg_ref[...] += jnp.sum(dy * xhat, axis=0, keepdims=True)
        dh_ref[...] = dx

    return pl.pallas_call(
        body, name=name, grid=(tp // tm,),
        in_specs=[pl.BlockSpec((tm, d), lambda i: (i, 0)), pl.BlockSpec((1, d), lambda i: (0, 0)),
                  pl.BlockSpec((tm, d), lambda i: (jnp.maximum(i - front_tiles, 0), 0))],
        out_specs=[pl.BlockSpec((tm, d), lambda i: (i, 0)), pl.BlockSpec((1, d), lambda i: (0, 0)),
                   pl.BlockSpec((1, 128), lambda i: (0, 0))],
        out_shape=[jax.ShapeDtypeStruct((tp, d), F32), jax.ShapeDtypeStruct((1, d), F32),
                   jax.ShapeDtypeStruct((1, 128), F32)],
        compiler_params=_cp(1))(h, gain, target)


def _gated_headnorm(o, g, gain):
    return o * _rms(o) * gain * (g * _sigmoid(g))


def _row_mask(chunk):
    rows = chunk * CHUNK + lax.broadcasted_iota(jnp.int32, (CHUNK, 1), 0)
    return (rows >= FRONT - N_META).astype(F32)


def _ret_head(q1, q2, k1, k2, v, g, state, gain, cos, sin, dmat, dq, dk, dc):
    q = jnp.concatenate([q1 * cos - q2 * sin, q1 * sin + q2 * cos], axis=1)
    k = jnp.concatenate([k1 * cos - k2 * sin, k1 * sin + k2 * cos], axis=1) * (RET_DK ** -0.5)
    scores = _nt(q, k) * dmat
    o = _nn(scores, v) + _nn(q * dq, state)
    new_state = state * dc + _tn(k * dk, v)
    return _gated_headnorm(o, g, gain), new_state


def _ret_consts():
    log_gamma = jnp.log1p(-2.0 ** (-5.0 - jnp.arange(HEADS, dtype=F32)))
    idx = jnp.arange(CHUNK, dtype=F32)
    rel = idx[:, None] - idx[None, :]
    dmat = jnp.where(rel >= 0, jnp.exp(log_gamma[:, None, None] * jnp.maximum(rel, 0.0)), 0.0)
    dq = jnp.exp(log_gamma[:, None] * (idx + 1.0))[..., None]
    dk = jnp.exp(log_gamma[:, None] * (CHUNK - 1.0 - idx))[..., None]
    dc = jnp.broadcast_to(jnp.exp(log_gamma * CHUNK)[:, None, None], (HEADS, 1, 128))
    return dmat, dq, dk, dc


def _rope_tables(tp):
    half = RET_DK // 2
    inv = 1.0 / (ROPE_BASE ** jnp.linspace(0.0, 1.0, half, dtype=F32))
    pos = (jnp.arange(tp) - (FRONT - N_META)).astype(F32)
    ang = pos[:, None] * inv[None, :]
    return jnp.cos(ang), jnp.sin(ang)


_RET_V0, _RET_G0 = 2 * D, 4 * D


def _ret_pieces(u_ref, hd):
    f = lambda a, n: u_ref[:, a:a + n].astype(F32)
    hk = RET_DK // 2
    return (f(RET_DK * hd, hk), f(RET_DK * hd + hk, hk), f(D + RET_DK * hd, hk), f(D + RET_DK * hd + hk, hk),
            f(_RET_V0 + RET_DV * hd, RET_DV), f(_RET_G0 + RET_DV * hd, RET_DV))


def _ret_const_specs(rev=None):
    c = (lambda n: (rev(n), 0)) if rev else (lambda n: (n, 0))
    z3 = lambda n: (0, 0, 0)
    return [pl.BlockSpec((CHUNK, RET_DK // 2), c), pl.BlockSpec((CHUNK, RET_DK // 2), c),
            pl.BlockSpec((HEADS, CHUNK, CHUNK), z3), pl.BlockSpec((HEADS, CHUNK, 1), z3),
            pl.BlockSpec((HEADS, CHUNK, 1), z3), pl.BlockSpec((HEADS, 1, 128), z3)]


def _ret_fwd(u, gain, name):
    tp = u.shape[0]
    nch = tp // CHUNK
    cos, sin = _rope_tables(tp)
    dmat, dq, dk, dc = _ret_consts()

    def body(u_ref, gain_ref, cos_ref, sin_ref, dmat_ref, dq_ref, dk_ref, dc_ref, on_ref, st_ref, state_ref):
        @pl.when(pl.program_id(0) == 0)
        def _():
            state_ref[...] = jnp.zeros_like(state_ref)

        cos_v, sin_v = cos_ref[...], sin_ref[...]
        for hd in range(HEADS):
            state = state_ref[hd]
            st_ref[hd] = state.astype(BF16)
            on, new_state = _ret_head(*_ret_pieces(u_ref, hd), state,
                                      gain_ref[:, RET_DV * hd:RET_DV * (hd + 1)], cos_v, sin_v,
                                      dmat_ref[hd], dq_ref[hd], dk_ref[hd], dc_ref[hd][:, :1])
            state_ref[hd] = new_state
            on_ref[:, RET_DV * hd:RET_DV * (hd + 1)] = on.astype(BF16)

    return pl.pallas_call(
        body, name=name, grid=(nch,),
        in_specs=[pl.BlockSpec((CHUNK, 6 * D), lambda n: (n, 0)), pl.BlockSpec((1, HEADS * RET_DV), lambda n: (0, 0))]
                 + _ret_const_specs(),
        out_specs=[pl.BlockSpec((CHUNK, HEADS * RET_DV), lambda n: (n, 0)),
                   pl.BlockSpec((None, HEADS, RET_DK, RET_DV), lambda n: (n, 0, 0, 0))],
        out_shape=[jax.ShapeDtypeStruct((tp, HEADS * RET_DV), BF16),
                   jax.ShapeDtypeStruct((nch, HEADS, RET_DK, RET_DV), BF16)],
        scratch_shapes=[pltpu.VMEM((HEADS, RET_DK, RET_DV), F32)],
        compiler_params=_cp(1))(u, gain, cos, sin, dmat, dq, dk, dc)


def _ret_bwd(u, gain, states, d_on, name):
    tp = u.shape[0]
    nch = tp // CHUNK
    cos, sin = _rope_tables(tp)
    dmat, dq, dk, dc = _ret_consts()
    rev = lambda n: nch - 1 - n
    hk = RET_DK // 2

    def body(u_ref, gain_ref, st_ref, don_ref, cos_ref, sin_ref, dmat_ref, dq_ref, dk_ref, dc_ref,
             du_ref, dgain_ref, dstate_ref):
        @pl.when(pl.program_id(0) == 0)
        def _():
            dstate_ref[...] = jnp.zeros_like(dstate_ref)
            dgain_ref[...] = jnp.zeros_like(dgain_ref)

        cos_v, sin_v = cos_ref[...], sin_ref[...]
        mask = _row_mask(rev(pl.program_id(0)))
        for hd in range(HEADS):
            consts = (cos_v, sin_v, dmat_ref[hd], dq_ref[hd], dk_ref[hd], dc_ref[hd][:, :1])
            cols = slice(RET_DV * hd, RET_DV * (hd + 1))
            _, vjp = jax.vjp(lambda *a: _ret_head(*a, *consts), *_ret_pieces(u_ref, hd),
                             st_ref[hd].astype(F32), gain_ref[:, cols])
            dq1, dq2, dk1, dk2, dv, dg, dstate, dgain = vjp((don_ref[:, cols].astype(F32), dstate_ref[hd]))
            dstate_ref[hd] = dstate
            dgain_ref[:, cols] += dgain
            put = lambda a, t: du_ref.__setitem__((slice(None), slice(a, a + t.shape[1])), (t * mask).astype(BF16))
            put(RET_DK * hd, dq1)
            put(RET_DK * hd + hk, dq2)
            put(D + RET_DK * hd, dk1)
            put(D + RET_DK * hd + hk, dk2)
            put(_RET_V0 + RET_DV * hd, dv)
            put(_RET_G0 + RET_DV * hd, dg)

    return pl.pallas_call(
        body, name=name, grid=(nch,),
        in_specs=[pl.BlockSpec((CHUNK, 6 * D), lambda n: (rev(n), 0)),
                  pl.BlockSpec((1, HEADS * RET_DV), lambda n: (0, 0)),
                  pl.BlockSpec((None, HEADS, RET_DK, RET_DV), lambda n: (rev(n), 0, 0, 0)),
                  pl.BlockSpec((CHUNK, HEADS * RET_DV), lambda n: (rev(n), 0))] + _ret_const_specs(rev),
        out_specs=[pl.BlockSpec((CHUNK, 6 * D), lambda n: (rev(n), 0)),
                   pl.BlockSpec((1, HEADS * RET_DV), lambda n: (0, 0))],
        out_shape=[jax.ShapeDtypeStruct((tp, 6 * D), BF16), jax.ShapeDtypeStruct((1, HEADS * RET_DV), F32)],
        scratch_shapes=[pltpu.VMEM((HEADS, RET_DK, RET_DV), F32)],
        compiler_params=_cp(1))(u, gain, states, d_on, cos, sin, dmat, dq, dk, dc)


_GLA_K0, _GLA_V0, _GLA_G0, _GLA_Z0 = 512, 1024, 2048, 3072


def _gla_head(q, k, v, g, z, state_t, wg, bg, gain, mask, lo, lo_t, loc, loc_t):
    ga = _nn(z, wg) + bg
    log_a = (jnp.minimum(ga, 0.0) - jnp.log(1.0 + jnp.exp(-jnp.abs(ga)))) * (mask * (1.0 / GLA_TAU))
    bcum = _cum(lo, lo_t, log_a)
    bmid = _cum(loc, loc_t, log_a)
    btot = jnp.sum(log_a, axis=0, keepdims=True)
    qs = q * (GLA_DK ** -0.5)
    causal = lax.broadcasted_iota(jnp.int32, (CHUNK, CHUNK), 0) >= lax.broadcasted_iota(jnp.int32, (CHUNK, CHUNK), 1)
    scores = jnp.where(causal, _nt(qs * jnp.exp(bmid), k * jnp.exp(-bmid)), 0.0)
    o = _nn(scores, v) + _nt(qs * jnp.exp(bcum), state_t)
    new_state_t = state_t * jnp.exp(btot) + _tn(v, k * jnp.exp(btot - bcum))
    return _gated_headnorm(o, g, gain), new_state_t


def _cum_mats():
    r = lax.broadcasted_iota(jnp.int32, (CHUNK, CHUNK), 0)
    c = lax.broadcasted_iota(jnp.int32, (CHUNK, CHUNK), 1)
    mid = CHUNK // 2
    low = lambda a, b: (a >= b).astype(F32)
    lo, lo_t = low(r, c), low(c, r)
    loc = lo - (c <= mid).astype(F32)
    loc_t = lo_t - (r <= mid).astype(F32)
    return tuple(m.astype(BF16) for m in (lo, lo_t, loc, loc_t))


def _gla_pieces(u_ref, hd):
    f = lambda a, n: u_ref[:, a:a + n].astype(F32)
    return (f(GLA_DK * hd, GLA_DK), f(_GLA_K0 + GLA_DK * hd, GLA_DK), f(_GLA_V0 + GLA_DV * hd, GLA_DV),
            f(_GLA_G0 + GLA_DV * hd, GLA_DV), f(_GLA_Z0, 128))


def _gla_fwd(u, wg, bg, gain, name):
    tp = u.shape[0]
    nch = tp // CHUNK

    def body(u_ref, wg_ref, bg_ref, gain_ref, on_ref, st_ref, state_ref):
        @pl.when(pl.program_id(0) == 0)
        def _():
            state_ref[...] = jnp.zeros_like(state_ref)

        mask = _row_mask(pl.program_id(0))
        mats = _cum_mats()
        for hd in range(HEADS):
            state = state_ref[hd]
            st_ref[hd] = state.astype(BF16)
            kc = slice(GLA_DK * hd, GLA_DK * (hd + 1))
            vc = slice(GLA_DV * hd, GLA_DV * (hd + 1))
            on, new_state = _gla_head(*_gla_pieces(u_ref, hd), state,
                                      wg_ref[:, kc], bg_ref[:, kc], gain_ref[:, vc], mask, *mats)
            state_ref[hd] = new_state
            on_ref[:, vc] = on.astype(BF16)

    return pl.pallas_call(
        body, name=name, grid=(nch,),
        in_specs=[pl.BlockSpec((CHUNK, GLA_U), lambda n: (n, 0)), pl.BlockSpec((128, HEADS * GLA_DK), lambda n: (0, 0)),
                  pl.BlockSpec((1, HEADS * GLA_DK), lambda n: (0, 0)), pl.BlockSpec((1, HEADS * GLA_DV), lambda n: (0, 0))],
        out_specs=[pl.BlockSpec((CHUNK, HEADS * GLA_DV), lambda n: (n, 0)),
                   pl.BlockSpec((None, HEADS, GLA_DV, GLA_DK), lambda n: (n, 0, 0, 0))],
        out_shape=[jax.ShapeDtypeStruct((tp, HEADS * GLA_DV), BF16),
                   jax.ShapeDtypeStruct((nch, HEADS, GLA_DV, GLA_DK), BF16)],
        scratch_shapes=[pltpu.VMEM((HEADS, GLA_DV, GLA_DK), F32)],
        compiler_params=_cp(1))(u, wg, bg, gain)


def _gla_bwd(u, wg, bg, gain, states, d_on, name):
    tp = u.shape[0]
    nch = tp // CHUNK
    rev = lambda n: nch - 1 - n

    def body(u_ref, wg_ref, bg_ref, gain_ref, st_ref, don_ref, du_ref, dwg_ref, dbg_ref, dgain_ref, dstate_ref):
        @pl.when(pl.program_id(0) == 0)
        def _():
            dstate_ref[...] = jnp.zeros_like(dstate_ref)
            dwg_ref[...] = jnp.zeros_like(dwg_ref)
            dbg_ref[...] = jnp.zeros_like(dbg_ref)
            dgain_ref[...] = jnp.zeros_like(dgain_ref)

        mask = _row_mask(rev(pl.program_id(0)))
        mats = _cum_mats()
        dz_sum = jnp.zeros((CHUNK, 128), F32)
        for hd in range(HEADS):
            kc = slice(GLA_DK * hd, GLA_DK * (hd + 1))
            vc = slice(GLA_DV * hd, GLA_DV * (hd + 1))
            _, vjp = jax.vjp(lambda *a: _gla_head(*a, mask, *mats), *_gla_pieces(u_ref, hd),
                             st_ref[hd].astype(F32), wg_ref[:, kc].astype(F32), bg_ref[:, kc], gain_ref[:, vc])
            dq, dk, dv, dg, dz, dstate, dwg, dbg, dgain = vjp((don_ref[:, vc].astype(F32), dstate_ref[hd]))
            dstate_ref[hd] = dstate
            dwg_ref[:, kc] += dwg
            dbg_ref[:, kc] += dbg
            dgain_ref[:, vc] += dgain
            dz_sum = dz_sum + dz
            put = lambda a, t: du_ref.__setitem__((slice(None), slice(a, a + t.shape[1])), (t * mask).astype(BF16))
            put(GLA_DK * hd, dq)
            put(_GLA_K0 + GLA_DK * hd, dk)
            put(_GLA_V0 + GLA_DV * hd, dv)
            put(_GLA_G0 + GLA_DV * hd, dg)
        du_ref[:, _GLA_Z0:] = dz_sum.astype(BF16)

    full = lambda r, c: pl.BlockSpec((r, c), lambda n: (0, 0))
    return pl.pallas_call(
        body, name=name, grid=(nch,),
        in_specs=[pl.BlockSpec((CHUNK, GLA_U), lambda n: (rev(n), 0)), full(128, HEADS * GLA_DK),
                  full(1, HEADS * GLA_DK), full(1, HEADS * GLA_DV),
                  pl.BlockSpec((None, HEADS, GLA_DV, GLA_DK), lambda n: (rev(n), 0, 0, 0)),
                  pl.BlockSpec((CHUNK, HEADS * GLA_DV), lambda n: (rev(n), 0))],
        out_specs=[pl.BlockSpec((CHUNK, GLA_U), lambda n: (rev(n), 0)), full(128, HEADS * GLA_DK),
                   full(1, HEADS * GLA_DK), full(1, HEADS * GLA_DV)],
        out_shape=[jax.ShapeDtypeStruct((tp, GLA_U), BF16), jax.ShapeDtypeStruct((128, HEADS * GLA_DK), F32),
                   jax.ShapeDtypeStruct((1, HEADS * GLA_DK), F32), jax.ShapeDtypeStruct((1, HEADS * GLA_DV), F32)],
        scratch_shapes=[pltpu.VMEM((HEADS, GLA_DV, GLA_DK), F32)],
        compiler_params=_cp(1))(u, wg, bg, gain, states, d_on)


def _ffn_fwd(h, gain, w_in, w_out, tag):
    hn, ug, uu, act = _norm_ffn_in(h, gain, w_in, f"{tag}_in")
    return _out_proj(act, w_out, h, 0.5, f"{tag}_out"), (h, hn, ug, uu, act)


def _ffn_bwd(dh, saved, gain, w_in, w_out, tag):
    h, hn, ug, uu, act = saved
    du = _ffn_dact(dh, w_out, ug, uu, f"{tag}_dact")
    d_w_out = _wgrad(act, dh, bm=D_FF // 2, bn=D, scale=0.5, sharded=False, name=f"{tag}_dwout")
    d_w_in = _wgrad(hn, du, bm=D, bn=w_in.shape[2], scale=1.0, sharded=True, name=f"{tag}_dwin")
    dh_in, d_gain = _dgrad_norm(du, w_in, h, gain, dh, f"{tag}_dnorm")
    return dh_in, d_gain, d_w_in, d_w_out


def _sequence_grads(x, target, w):
    t = x.shape[0]
    h = jnp.concatenate([jnp.zeros((FRONT - N_META, D), F32), w["meta"], x], axis=0)
    row = lambda v: v.reshape(1, -1)

    saved = []
    for layer in range(2):
        h, s1 = _ffn_fwd(h, row(w["norm_ffn1"][layer]), w["ffn1_in"][layer], w["ffn1_out"][layer], f"l{layer}_ffn1")
        if layer == 0:
            hn, u = _norm_proj(h, row(w["norm_mix"][layer]), w["ret_in"], "ret_in")
            on, states = _ret_fwd(u, w["ret_gain"], "ret_fwd")
            h_mix = _out_proj(on, w["ret_out"], h, 1.0, "ret_out")
        else:
            hn, u = _norm_proj(h, row(w["norm_mix"][layer]), w["gla_in"], "gla_in")
            on, states = _gla_fwd(u, w["gla_wg"], w["gla_bg"], w["gla_gain"], "gla_fwd")
            h_mix = _out_proj(on, w["gla_out"], h, 1.0, "gla_out")
        s2 = (h, hn, u, on, states)
        h, s3 = _ffn_fwd(h_mix, row(w["norm_ffn2"][layer]), w["ffn2_in"][layer], w["ffn2_out"][layer], f"l{layer}_ffn2")
        saved.append((s1, s2, s3))

    dh, d_final, loss = _loss_head(h, row(w["final_norm"]), target, "loss_head")

    g = {"final_norm": d_final, "norm_ffn1": [None, None], "norm_mix": [None, None], "norm_ffn2": [None, None],
         "ffn1_in": [None, None], "ffn1_out": [None, None], "ffn2_in": [None, None], "ffn2_out": [None, None]}
    for layer in (1, 0):
        s1, s2, s3 = saved[layer]
        dh, g["norm_ffn2"][layer], g["ffn2_in"][layer], g["ffn2_out"][layer] = _ffn_bwd(
            dh, s3, row(w["norm_ffn2"][layer]), w["ffn2_in"][layer], w["ffn2_out"][layer], f"l{layer}_ffn2")
        h_in, hn, u, on, states = s2
        if layer == 0:
            d_on = _dgrad(dh, w["ret_out"], "ret_don")
            g["ret_out"] = _wgrad(on, dh, bm=D, bn=D, scale=1.0, sharded=False, name="ret_dwout")
            du, g["ret_gain"] = _ret_bwd(u, w["ret_gain"], states, d_on, "ret_bwd")
            g["ret_in"] = _wgrad(hn, du, bm=D, bn=w["ret_in"].shape[2], scale=1.0, sharded=True, name="ret_dwin")
            dh, g["norm_mix"][layer] = _dgrad_norm(du, w["ret_in"], h_in, row(w["norm_mix"][layer]), dh, "ret_dnorm")
        else:
            d_on = _dgrad(dh, w["gla_out"], "gla_don")
            g["gla_out"] = _wgrad(on, dh, bm=D, bn=D, scale=1.0, sharded=False, name="gla_dwout")
            du, g["gla_wg"], g["gla_bg"], g["gla_gain"] = _gla_bwd(u, w["gla_wg"], w["gla_bg"], w["gla_gain"],
                                                                   states, d_on, "gla_bwd")
            g["gla_in"] = _wgrad(hn, du, bm=D, bn=GLA_U // 5, scale=1.0, sharded=False, name="gla_dwin")
            dh, g["norm_mix"][layer] = _dgrad_norm(du, w["gla_in"], h_in, row(w["norm_mix"][layer]), dh, "gla_dnorm")
        dh, g["norm_ffn1"][layer], g["ffn1_in"][layer], g["ffn1_out"][layer] = _ffn_bwd(
            dh, s1, row(w["norm_ffn1"][layer]), w["ffn1_in"][layer], w["ffn1_out"][layer], f"l{layer}_ffn1")
    return loss, dh[FRONT:], dh[FRONT - N_META:FRONT], g


_HBM = pl.BlockSpec(memory_space=pl.ANY)


def _place():
    return lax.axis_index("x"), lax.axis_index("y"), lax.axis_index("c")


def _flip(v, bit):
    return 1 - v if bit else v


DMA_CHUNK_BYTES = 128 * 1024


def _row_chunks(ref):
    rows, cols = ref.shape
    step = _row_tile(rows, max(16, DMA_CHUNK_BYTES // (cols * ref.dtype.itemsize)))
    return [pl.ds(a, step) for a in range(0, rows, step)]


def _send(src, dst, send_sem, recv_sem, peer):
    for rows in _row_chunks(src):
        pltpu.make_async_remote_copy(src_ref=src.at[rows], dst_ref=dst.at[rows], send_sem=send_sem, recv_sem=recv_sem,
                                     device_id=peer, device_id_type=MESH).start()
    return pltpu.make_async_remote_copy(src_ref=src, dst_ref=dst, send_sem=send_sem, recv_sem=recv_sem,
                                        device_id=peer, device_id_type=MESH)


def _gather_chips(shards):
    n = len(shards)
    offsets = [(1, 0), (0, 1), (1, 1)]

    def body(*refs):
        src, dst = refs[:n], refs[n:2 * n]
        send_sems, recv_sems = refs[2 * n:]
        x, y, c = _place()
        mine = 2 * x + y
        copies = []
        for t in range(n):
            for j, (fx, fy) in enumerate(offsets):
                copies.append(_send(src[t], dst[t].at[mine], send_sems.at[t, j], recv_sems.at[t, j],
                                    (_flip(x, fx), _flip(y, fy), c)))
        for cp in copies:
            cp.wait()

    got = pl.pallas_call(
        body, name="gather_chips", in_specs=[_HBM] * n, out_specs=[_HBM] * n,
        out_shape=[jax.ShapeDtypeStruct((N_CHIPS,) + s.shape, s.dtype) for s in shards],
        scratch_shapes=[pltpu.SemaphoreType.DMA((n, 3)), pltpu.SemaphoreType.DMA((n, 3))],
    )(*shards)
    mine = 2 * lax.axis_index("x") + lax.axis_index("y")
    return [lax.dynamic_update_index_in_dim(g, s, mine, 0) for g, s in zip(got, shards)]


_PEER_FLIPS = [(fx, fy, fc) for fx in (0, 1) for fy in (0, 1) for fc in (0, 1)][1:]


def _scatter_devices(pieces, whole):
    n, m = len(pieces), len(whole)

    def body(*refs):
        src, dst = refs[:n + m], refs[n + m:2 * (n + m)]
        send_sems, recv_sems = refs[2 * (n + m):]
        x, y, c = _place()
        me = 4 * x + 2 * y + c
        copies = []
        for t in range(n + m):
            part = (lambda px, py, pc: src[t].at[2 * px + py, pc]) if t < n else (lambda px, py, pc: src[t])
            for j, (fx, fy, fc) in enumerate(_PEER_FLIPS):
                peer = (_flip(x, fx), _flip(y, fy), _flip(c, fc))
                copies.append(_send(part(*peer), dst[t].at[me], send_sems.at[t, j], recv_sems.at[t, j], peer))
        for cp in copies:
            cp.wait()

    arrays = list(pieces) + list(whole)
    out_shape = [jax.ShapeDtypeStruct((N_DEV,) + a.shape[2:], a.dtype) for a in pieces]
    out_shape += [jax.ShapeDtypeStruct((N_DEV,) + a.shape, a.dtype) for a in whole]
    got = pl.pallas_call(
        body, name="scatter_devices", in_specs=[_HBM] * (n + m), out_specs=[_HBM] * (n + m), out_shape=out_shape,
        scratch_shapes=[pltpu.SemaphoreType.DMA((n + m, 7)), pltpu.SemaphoreType.DMA((n + m, 7))],
    )(*arrays)
    x, y, c = _place()
    own = [a[2 * x + y, c] for a in pieces] + list(whole)
    return [lax.dynamic_update_index_in_dim(g, a, 4 * x + 2 * y + c, 0) for g, a in zip(got, own)]


def _swap_cores(halves):
    n = len(halves)

    def body(*refs):
        src, dst = refs[:n], refs[n:2 * n]
        send_sems, recv_sems = refs[2 * n:]
        x, y, c = _place()
        copies = [_send(src[t], dst[t], send_sems.at[t], recv_sems.at[t], (x, y, 1 - c)) for t in range(n)]
        for cp in copies:
            cp.wait()

    got = pl.pallas_call(
        body, name="swap_cores", in_specs=[_HBM] * n, out_specs=[_HBM] * n,
        out_shape=[jax.ShapeDtypeStruct(a.shape, a.dtype) for a in halves],
        scratch_shapes=[pltpu.SemaphoreType.DMA((n,)), pltpu.SemaphoreType.DMA((n,))],
    )(*halves)
    south = lax.axis_index("c") == 0
    return [jnp.stack([jnp.where(south, a, b), jnp.where(south, b, a)]) for a, b in zip(halves, got)]


def _row_tile(rows, cap):
    fits = [t for t in range(16, cap + 1, 16) if rows % t == 0]
    return fits[-1] if fits else rows


def _sum_slots(a, name):
    _, r, c = a.shape
    tr = _row_tile(r, 384)

    def body(a_ref, o_ref):
        s = a_ref[0].astype(F32)
        for k in range(1, N_DEV):
            s = s + a_ref[k].astype(F32)
        o_ref[...] = s

    return pl.pallas_call(
        body, name=name, grid=(r // tr,),
        in_specs=[pl.BlockSpec((N_DEV, tr, c), lambda i: (0, i, 0))],
        out_specs=pl.BlockSpec((tr, c), lambda i: (i, 0)),
        out_shape=jax.ShapeDtypeStruct((r, c), F32),
        compiler_params=_cp(1))(a)


def _adamw(w, g, m, v, name):
    r, c = w.shape
    tr = _row_tile(r, 256)

    def body(w_ref, g_ref, m_ref, v_ref, d_ref, nm_ref, nv_ref):
        gv = g_ref[...]
        nm = ADAM_B1 * m_ref[...] + (1.0 - ADAM_B1) * gv
        nv = ADAM_B2 * v_ref[...] + (1.0 - ADAM_B2) * (gv * gv)
        m_hat = nm / (1.0 - ADAM_B1 ** ADAM_STEP)
        v_hat = nv / (1.0 - ADAM_B2 ** ADAM_STEP)
        d_ref[...] = -ADAM_LR * (m_hat / (jnp.sqrt(v_hat) + ADAM_EPS) + ADAM_WD * w_ref[...])
        nm_ref[...] = nm
        nv_ref[...] = nv

    spec = pl.BlockSpec((tr, c), lambda i: (i, 0))
    return pl.pallas_call(
        body, name=name, grid=(r // tr,), in_specs=[spec] * 4, out_specs=[spec] * 3,
        out_shape=[jax.ShapeDtypeStruct((r, c), F32)] * 3,
        compiler_params=_cp(1))(w, g, m, v)


_SMALL = ["meta_tokens", "ret_head_norm", "gla_w_gate", "gla_b_gate", "gla_head_norm"]
_LOCAL_SMALL = ["meta_tokens", "norm_ffn1", "norm_mix", "norm_ffn2", "ret_head_norm", "gla_w_gate", "gla_b_gate",
                "gla_head_norm", "final_norm"]
_BIG = ["ffn1_w_in", "ffn1_w_out", "ffn2_w_in", "ffn2_w_out", "ret_w_in", "ret_w_out", "gla_w_in", "gla_w_out"]
_WEIGHTS = ["meta_tokens", "norm_ffn1", "ffn1_w_in", "ffn1_w_out", "norm_mix", "norm_ffn2", "ffn2_w_in", "ffn2_w_out",
            "ret_w_in", "ret_head_norm", "ret_w_out", "gla_w_in", "gla_w_gate", "gla_b_gate", "gla_head_norm",
            "gla_w_out", "final_norm"]


def _pack_rows(arrays, width):
    flat = jnp.concatenate([a.reshape(-1) for a in arrays])
    pad = -flat.shape[0] % (8 * width)
    return jnp.pad(flat, (0, pad)).reshape(-1, width)


def _unpack_rows(packed, shapes):
    flat, out, at = packed.reshape(-1), [], 0
    for s in shapes:
        size = 1
        for dim in s:
            size *= dim
        out.append(flat[at:at + size].reshape(s))
        at += size
    return out


def _gather_all(p):
    b = lambda a: a.astype(BF16)
    shards = []
    for name in ("ffn1_w_in", "ffn2_w_in", "ffn1_w_out", "ffn2_w_out"):
        shards += [b(p[name][0]), b(p[name][1])]
    shards += [b(p["ret_w_in"][0]), b(p["ret_w_out"][0]), b(p["gla_w_in"][0]), b(p["gla_w_out"][0])]
    small_shapes = [p[name].shape for name in _SMALL]
    shards.append(_pack_rows([p[name] for name in _SMALL], 128))
    got = _gather_chips(shards)

    w = {"ffn1_in": got[0:2], "ffn2_in": got[2:4],
         "ffn1_out": [a.reshape(D_FF, D) for a in got[4:6]], "ffn2_out": [a.reshape(D_FF, D) for a in got[6:8]],
         "ret_in": got[8], "ret_out": got[9].reshape(HEADS * RET_DV, D), "gla_out": got[11].reshape(HEADS * GLA_DV, D)}
    gla_in = jnp.moveaxis(got[10], 0, 1).reshape(D, -1)
    w["gla_in"] = jnp.pad(gla_in, ((0, 0), (0, GLA_U - gla_in.shape[1])))[None]
    small = [jnp.stack(parts) for parts in zip(*[_unpack_rows(got[12][s], small_shapes) for s in range(N_CHIPS)])]
    cat = lambda a: jnp.moveaxis(a, 0, -2).reshape(a.shape[1:-1] + (-1,))
    meta, ret_gain, wg, bg, gla_gain = [cat(a) for a in small]
    w["meta"] = meta
    w["ret_gain"] = ret_gain.reshape(1, -1)
    w["gla_wg"] = jnp.pad(wg[0], ((0, 128 - GLA_RANK), (0, 0))).astype(BF16)
    w["gla_bg"] = bg.reshape(1, -1)
    w["gla_gain"] = gla_gain.reshape(1, -1)
    for name in ("norm_ffn1", "norm_mix", "norm_ffn2", "final_norm"):
        w[name] = p[name]
    return w


def _reduce_grads(loss, d_meta, g):
    halves = lambda a: a.reshape(N_CHIPS, 2, -1, a.shape[-1])
    pieces, names = [], []
    for name in ("ffn1_in", "ffn2_in", "ffn1_out", "ffn2_out"):
        for layer in range(2):
            pieces.append(halves(g[name][layer]))
            names.append((name, layer))
    gla_in = jnp.moveaxis(g["gla_in"][:, :4 * 772].reshape(D, N_CHIPS, 772), 1, 0)
    for name, a in (("ret_in", g["ret_in"]), ("ret_out", g["ret_out"]), ("gla_in", gla_in), ("gla_out", g["gla_out"])):
        pieces.append(halves(a))
        names.append((name, 0))
    small = [d_meta, g["norm_ffn1"][0], g["norm_ffn1"][1], g["norm_mix"][0], g["norm_mix"][1], g["norm_ffn2"][0],
             g["norm_ffn2"][1], g["final_norm"], g["ret_gain"], g["gla_wg"][:GLA_RANK], g["gla_bg"], g["gla_gain"],
             loss[:, :1]]
    small_shapes = [a.shape for a in small]
    got = _scatter_devices(pieces, [_pack_rows(small, D)])

    sums = [_sum_slots(a, f"sum_{name}{layer}") for a, (name, layer) in zip(got[:-1], names)]
    swapped = _swap_cores(sums)
    big = {}
    for a, (name, layer) in zip(swapped, names):
        big.setdefault(name, []).append(a.reshape(a.shape[0] * a.shape[1], a.shape[2]))
    small_sum = _unpack_rows(_sum_slots(got[-1], "sum_small"), small_shapes)
    return big, small_sum


def kernel(x, meta_tokens, norm_ffn1, ffn1_w_in, ffn1_w_out, norm_mix, norm_ffn2, ffn2_w_in, ffn2_w_out, ret_w_in, ret_head_norm, ret_w_out, gla_w_in, gla_w_gate, gla_b_gate, gla_head_norm, gla_w_out, final_norm, loss_target, m_meta_tokens, m_norm_ffn1, m_ffn1_w_in, m_ffn1_w_out, m_norm_mix, m_norm_ffn2, m_ffn2_w_in, m_ffn2_w_out, m_ret_w_in, m_ret_head_norm, m_ret_w_out, m_gla_w_in, m_gla_w_gate, m_gla_b_gate, m_gla_head_norm, m_gla_w_out, m_final_norm, v_meta_tokens, v_norm_ffn1, v_ffn1_w_in, v_ffn1_w_out, v_norm_mix, v_norm_ffn2, v_ffn2_w_in, v_ffn2_w_out, v_ret_w_in, v_ret_head_norm, v_ret_w_out, v_gla_w_in, v_gla_w_gate, v_gla_b_gate, v_gla_head_norm, v_gla_w_out, v_final_norm):
    p = dict(meta_tokens=meta_tokens, norm_ffn1=norm_ffn1, ffn1_w_in=ffn1_w_in, ffn1_w_out=ffn1_w_out, norm_mix=norm_mix,
             norm_ffn2=norm_ffn2, ffn2_w_in=ffn2_w_in, ffn2_w_out=ffn2_w_out, ret_w_in=ret_w_in,
             ret_head_norm=ret_head_norm, ret_w_out=ret_w_out, gla_w_in=gla_w_in, gla_w_gate=gla_w_gate,
             gla_b_gate=gla_b_gate, gla_head_norm=gla_head_norm, gla_w_out=gla_w_out, final_norm=final_norm)
    m = dict(zip(_WEIGHTS, (m_meta_tokens, m_norm_ffn1, m_ffn1_w_in, m_ffn1_w_out, m_norm_mix, m_norm_ffn2, m_ffn2_w_in,
                            m_ffn2_w_out, m_ret_w_in, m_ret_head_norm, m_ret_w_out, m_gla_w_in, m_gla_w_gate,
                            m_gla_b_gate, m_gla_head_norm, m_gla_w_out, m_final_norm)))
    v = dict(zip(_WEIGHTS, (v_meta_tokens, v_norm_ffn1, v_ffn1_w_in, v_ffn1_w_out, v_norm_mix, v_norm_ffn2, v_ffn2_w_in,
                            v_ffn2_w_out, v_ret_w_in, v_ret_head_norm, v_ret_w_out, v_gla_w_in, v_gla_w_gate,
                            v_gla_b_gate, v_gla_head_norm, v_gla_w_out, v_final_norm)))

    w = _gather_all(p)
    loss, d_x, d_meta, g = _sequence_grads(x[0], loss_target[0], w)
    big, small = _reduce_grads(loss, d_meta, g)

    chip = 2 * lax.axis_index("x") + lax.axis_index("y")
    cols = lambda a, n: lax.dynamic_slice_in_dim(a, chip * n, n, axis=a.ndim - 1)
    (s_meta, s_n1a, s_n1b, s_nma, s_nmb, s_n2a, s_n2b, s_final, s_ret_gain, s_wg, s_bg, s_gla_gain, s_loss) = small
    grads = {
        "meta_tokens": cols(s_meta, 256), "norm_ffn1": jnp.concatenate([s_n1a, s_n1b]),
        "norm_mix": jnp.concatenate([s_nma, s_nmb]), "norm_ffn2": jnp.concatenate([s_n2a, s_n2b]),
        "final_norm": s_final.reshape(D),
        "ret_head_norm": cols(s_ret_gain.reshape(1, HEADS, RET_DV), RET_DV // N_CHIPS),
        "gla_w_gate": cols(s_wg, GLA_DK)[None], "gla_b_gate": cols(s_bg, GLA_DK),
        "gla_head_norm": cols(s_gla_gain.reshape(1, HEADS, GLA_DV), GLA_DV // N_CHIPS),
        "ffn1_w_in": jnp.stack(big["ffn1_in"]), "ffn1_w_out": jnp.stack(big["ffn1_out"]),
        "ffn2_w_in": jnp.stack(big["ffn2_in"]), "ffn2_w_out": jnp.stack(big["ffn2_out"]),
        "ret_w_in": big["ret_in"][0][None], "ret_w_out": big["ret_out"][0][None],
        "gla_w_in": big["gla_in"][0][None], "gla_w_out": big["gla_out"][0][None],
    }

    delta, new_m, new_v = {}, {}, {}
    for name in _BIG:
        shape = p[name].shape
        flat = lambda a: a.reshape(-1, shape[-1])
        out = _adamw(flat(p[name]), flat(grads[name]), flat(m[name]), flat(v[name]), f"adamw_{name}")
        delta[name], new_m[name], new_v[name] = [a.reshape(shape) for a in out]
    packed = [_pack_rows([d[name] for name in _LOCAL_SMALL], 128) for d in (p, grads, m, v)]
    out = _adamw(*packed, "adamw_small")
    shapes = [p[name].shape for name in _LOCAL_SMALL]
    for d, a in zip((delta, new_m, new_v), out):
        d.update(zip(_LOCAL_SMALL, _unpack_rows(a, shapes)))

    return (s_loss.reshape(()), d_x[None], *[grads[n] for n in _WEIGHTS], *[delta[n] for n in _WEIGHTS],
            *[new_m[n] for n in _WEIGHTS], *[new_v[n] for n in _WEIGHTS])
```

```python
import functools

import jax
import jax.numpy as jnp
from jax import lax
from jax.experimental import pallas as pl
from jax.experimental.pallas import tpu as pltpu

F32, BF16 = jnp.float32, jnp.bfloat16
MESH = pl.DeviceIdType.MESH

D = 1024
N_META = 16
CHUNK = 64
FRONT = 256
D_FF = 2816
EPS = 1e-6
HEADS = 4
RET_DK, RET_DV = 256, 512
GLA_DK, GLA_DV = 128, 256
GLA_RANK = 16
GLA_TAU = 16.0
GLA_IN = 2 * HEADS * GLA_DK + 2 * HEADS * GLA_DV + GLA_RANK
GLA_U = 3200
ROPE_BASE = 10000.0
N_CHIPS = 4
N_DEV = 8

ADAM_LR, ADAM_B1, ADAM_B2, ADAM_EPS, ADAM_WD, ADAM_STEP = 0.001, 0.9, 0.999, 1e-08, 0.01, 10

VMEM_LIMIT_BYTES = 56 * 1024 * 1024
TM = 768
TM_SMALL = 256


def _cp(n_axes):
    return pltpu.CompilerParams(dimension_semantics=("arbitrary",) * n_axes, vmem_limit_bytes=VMEM_LIMIT_BYTES)


def _dg(a, b, ca, cb):
    return lax.dot_general(a.astype(BF16), b.astype(BF16), (((ca,), (cb,)), ((), ())), preferred_element_type=F32)


@jax.custom_vjp
def _nn(a, b):
    return _dg(a, b, 1, 0)


@jax.custom_vjp
def _nt(a, b):
    return _dg(a, b, 1, 1)


@jax.custom_vjp
def _tn(a, b):
    return _dg(a, b, 0, 0)


_nn.defvjp(lambda a, b: (_nn(a, b), (a, b)), lambda res, g: (_nt(g, res[1]), _tn(res[0], g)))
_nt.defvjp(lambda a, b: (_nt(a, b), (a, b)), lambda res, g: (_nn(g, res[1]), _tn(g, res[0])))
_tn.defvjp(lambda a, b: (_tn(a, b), (a, b)), lambda res, g: (_nt(res[1], g), _nn(res[0], g)))


def _split3_dot(m, a):
    a1 = a.astype(BF16)
    r1 = a - a1.astype(F32)
    a2 = r1.astype(BF16)
    a3 = (r1 - a2.astype(F32)).astype(BF16)
    dot = lambda p: jnp.dot(m, p, preferred_element_type=F32)
    return dot(a1) + dot(a2) + dot(a3)


@jax.custom_vjp
def _cum(m, mt, a):
    return _split3_dot(m, a)


_cum.defvjp(lambda m, mt, a: (_split3_dot(m, a), (m, mt)),
            lambda res, g: (jnp.zeros_like(res[0]), jnp.zeros_like(res[1]), _split3_dot(res[1], g)))


def _sigmoid(x):
    return 1.0 / (1.0 + jnp.exp(-x))


def _rms(x):
    return lax.rsqrt(jnp.mean(x * x, axis=-1, keepdims=True) + EPS)


def _rmsnorm_bwd(dy, x, gain):
    r = _rms(x)
    xhat = x * r
    dxh = dy * gain
    return r * (dxh - xhat * jnp.mean(dxh * xhat, axis=-1, keepdims=True)), xhat


def _norm_proj(h, gain, w, name):
    tp, d = h.shape
    s, _, ns = w.shape

    def body(h_ref, g_ref, w_ref, hn_ref, u_ref):
        @pl.when(pl.program_id(1) == 0)
        def _():
            x = h_ref[...]
            hn_ref[...] = (x * _rms(x) * g_ref[...]).astype(BF16)

        u_ref[...] = jnp.dot(hn_ref[...], w_ref[...], preferred_element_type=F32).astype(BF16)

    return pl.pallas_call(
        body, name=name, grid=(tp // TM, s),
        in_specs=[pl.BlockSpec((TM, d), lambda i, j: (i, 0)), pl.BlockSpec((1, d), lambda i, j: (0, 0)),
                  pl.BlockSpec((None, d, ns), lambda i, j: (j, 0, 0))],
        out_specs=[pl.BlockSpec((TM, d), lambda i, j: (i, 0)), pl.BlockSpec((TM, ns), lambda i, j: (i, j))],
        out_shape=[jax.ShapeDtypeStruct((tp, d), BF16), jax.ShapeDtypeStruct((tp, s * ns), BF16)],
        compiler_params=_cp(2))(h, gain, w)


def _norm_ffn_in(h, gain, w, name):
    tp, d = h.shape
    s, _, ns = w.shape
    half = s // 2

    def body(h_ref, g_ref, wg_ref, wu_ref, hn_ref, ug_ref, uu_ref, act_ref):
        @pl.when(pl.program_id(1) == 0)
        def _():
            x = h_ref[...]
            hn_ref[...] = (x * _rms(x) * g_ref[...]).astype(BF16)

        a = hn_ref[...]
        g = jnp.dot(a, wg_ref[...], preferred_element_type=F32)
        u = jnp.dot(a, wu_ref[...], preferred_element_type=F32)
        ug_ref[...] = g.astype(BF16)
        uu_ref[...] = u.astype(BF16)
        act_ref[...] = (g * _sigmoid(g) * u).astype(BF16)

    wide = jax.ShapeDtypeStruct((tp, half * ns), BF16)
    return pl.pallas_call(
        body, name=name, grid=(tp // TM, half),
        in_specs=[pl.BlockSpec((TM, d), lambda i, j: (i, 0)), pl.BlockSpec((1, d), lambda i, j: (0, 0)),
                  pl.BlockSpec((None, d, ns), lambda i, j: (j, 0, 0)),
                  pl.BlockSpec((None, d, ns), lambda i, j: (j + half, 0, 0))],
        out_specs=[pl.BlockSpec((TM, d), lambda i, j: (i, 0))] + [pl.BlockSpec((TM, ns), lambda i, j: (i, j))] * 3,
        out_shape=[jax.ShapeDtypeStruct((tp, d), BF16), wide, wide, wide],
        compiler_params=_cp(2))(h, gain, w, w)


def _out_proj(a, w, h, scale, name):
    tp, k = a.shape
    d = w.shape[1]

    def body(a_ref, w_ref, h_ref, o_ref):
        o_ref[...] = h_ref[...] + scale * jnp.dot(a_ref[...], w_ref[...], preferred_element_type=F32)

    return pl.pallas_call(
        body, name=name, grid=(tp // TM,),
        in_specs=[pl.BlockSpec((TM, k), lambda i: (i, 0)), pl.BlockSpec((k, d), lambda i: (0, 0)),
                  pl.BlockSpec((TM, d), lambda i: (i, 0))],
        out_specs=pl.BlockSpec((TM, d), lambda i: (i, 0)),
        out_shape=jax.ShapeDtypeStruct((tp, d), F32),
        compiler_params=_cp(1))(a, w, h)


def _ffn_dact(dh, w_out, ug, uu, name):
    tp, d = dh.shape
    ff = w_out.shape[0]
    tm = TM_SMALL

    def body(dh_ref, w_ref, ug_ref, uu_ref, du_ref):
        dy = (0.5 * dh_ref[...]).astype(BF16)
        dact = lax.dot_general(dy, w_ref[...], (((1,), (1,)), ((), ())), preferred_element_type=F32)
        g = ug_ref[...].astype(F32)
        u = uu_ref[...].astype(F32)
        sg = _sigmoid(g)
        du_ref[:, :ff] = (dact * u * (sg * (1.0 + g * (1.0 - sg)))).astype(BF16)
        du_ref[:, ff:] = (dact * (g * sg)).astype(BF16)

    return pl.pallas_call(
        body, name=name, grid=(tp // tm,),
        in_specs=[pl.BlockSpec((tm, d), lambda i: (i, 0)), pl.BlockSpec((ff, d), lambda i: (0, 0)),
                  pl.BlockSpec((tm, ff), lambda i: (i, 0)), pl.BlockSpec((tm, ff), lambda i: (i, 0))],
        out_specs=pl.BlockSpec((tm, 2 * ff), lambda i: (i, 0)),
        out_shape=jax.ShapeDtypeStruct((tp, 2 * ff), BF16),
        compiler_params=_cp(1))(dh, w_out, ug, uu)


def _dgrad(dh, w, name):
    tp, d = dh.shape
    k = w.shape[0]

    def body(dh_ref, w_ref, o_ref):
        o_ref[...] = lax.dot_general(dh_ref[...].astype(BF16), w_ref[...], (((1,), (1,)), ((), ())),
                                     preferred_element_type=F32).astype(BF16)

    return pl.pallas_call(
        body, name=name, grid=(tp // TM,),
        in_specs=[pl.BlockSpec((TM, d), lambda i: (i, 0)), pl.BlockSpec((k, d), lambda i: (0, 0))],
        out_specs=pl.BlockSpec((TM, k), lambda i: (i, 0)),
        out_shape=jax.ShapeDtypeStruct((tp, k), BF16),
        compiler_params=_cp(1))(dh, w)


def _wgrad(a, b, *, bm, bn, scale, sharded, name):
    tp, m = a.shape
    n = b.shape[1]
    nk = tp // TM

    def body(a_ref, b_ref, o_ref, acc_ref):
        k = pl.program_id(2)

        @pl.when(k == 0)
        def _():
            acc_ref[...] = jnp.zeros_like(acc_ref)

        bb = b_ref[...]
        if scale != 1.0:
            bb = scale * bb
        acc_ref[...] += lax.dot_general(a_ref[...], bb.astype(BF16), (((0,), (0,)), ((), ())),
                                        preferred_element_type=F32)

        @pl.when(k == nk - 1)
        def _():
            o_ref[...] = acc_ref[...].astype(BF16)

    if sharded:
        assert m == bm
        out_spec = pl.BlockSpec((None, bm, bn), lambda i, j, k: (j, 0, 0))
        out_shape = jax.ShapeDtypeStruct((n // bn, m, bn), BF16)
    else:
        out_spec = pl.BlockSpec((bm, bn), lambda i, j, k: (i, j))
        out_shape = jax.ShapeDtypeStruct((m, n), BF16)
    return pl.pallas_call(
        body, name=name, grid=(m // bm, n // bn, nk),
        in_specs=[pl.BlockSpec((TM, bm), lambda i, j, k: (k, i)), pl.BlockSpec((TM, bn), lambda i, j, k: (k, j))],
        out_specs=out_spec, out_shape=out_shape,
        scratch_shapes=[pltpu.VMEM((bm, bn), F32)],
        compiler_params=_cp(3))(a, b)


def _dgrad_norm(du, w, h, gain, dh_out, name):
    tp, d = h.shape
    s, _, ns = w.shape

    def body(du_ref, w_ref, h_ref, g_ref, dho_ref, dhi_ref, dg_ref, acc_ref):
        i, k = pl.program_id(0), pl.program_id(1)

        @pl.when(k == 0)
        def _():
            acc_ref[...] = jnp.zeros_like(acc_ref)

        @pl.when((i == 0) & (k == 0))
        def _():
            dg_ref[...] = jnp.zeros_like(dg_ref)

        acc_ref[...] += lax.dot_general(du_ref[...], w_ref[...], (((1,), (1,)), ((), ())),
                                        preferred_element_type=F32)

        @pl.when(k == s - 1)
        def _():
            dhn = acc_ref[...]
            dx, xhat = _rmsnorm_bwd(dhn, h_ref[...], g_ref[...])
            dg_ref[...] += jnp.sum(dhn * xhat, axis=0, keepdims=True)
            dhi_ref[...] = dho_ref[...] + dx

    return pl.pallas_call(
        body, name=name, grid=(tp // TM, s),
        in_specs=[pl.BlockSpec((TM, ns), lambda i, k: (i, k)), pl.BlockSpec((None, d, ns), lambda i, k: (k, 0, 0)),
                  pl.BlockSpec((TM, d), lambda i, k: (i, 0)), pl.BlockSpec((1, d), lambda i, k: (0, 0)),
                  pl.BlockSpec((TM, d), lambda i, k: (i, 0))],
        out_specs=[pl.BlockSpec((TM, d), lambda i, k: (i, 0)), pl.BlockSpec((1, d), lambda i, k: (0, 0))],
        out_shape=[jax.ShapeDtypeStruct((tp, d), F32), jax.ShapeDtypeStruct((1, d), F32)],
        scratch_shapes=[pltpu.VMEM((TM, d), F32)],
        compiler_params=_cp(2))(du, w, h, gain, dh_out)


def _loss_head(h, gain, target, name):
    tp, d = h.shape
    tm = TM_SMALL
    front_tiles = FRONT // tm

    def body(h_ref, g_ref, t_ref, dh_ref, dg_ref, loss_ref):
        i = pl.program_id(0)

        @pl.when(i == 0)
        def _():
            dg_ref[...] = jnp.zeros_like(dg_ref)
            loss_ref[...] = jnp.zeros_like(loss_ref)

        x = h_ref[...]
        gain_v = g_ref[...]
        y = x * _rms(x) * gain_v
        err = jnp.where(i >= front_tiles, y - t_ref[...], 0.0)
        loss_ref[...] += 0.5 * jnp.sum(jnp.mean(err * err, axis=-1, keepdims=True), axis=0, keepdims=True)
        dy = err * (1.0 / d)
        dx, xhat = _rmsnorm_bwd(dy, x, gain_v)
        dg_ref[...] += jnp.sum(dy * xhat, axis=0, keepdims=True)
        dh_ref[...] = dx

    return pl.pallas_call(
        body, name=name, grid=(tp // tm,),
        in_specs=[pl.BlockSpec((tm, d), lambda i: (i, 0)), pl.BlockSpec((1, d), lambda i: (0, 0)),
                  pl.BlockSpec((tm, d), lambda i: (jnp.maximum(i - front_tiles, 0), 0))],
        out_specs=[pl.BlockSpec((tm, d), lambda i: (i, 0)), pl.BlockSpec((1, d), lambda i: (0, 0)),
                   pl.BlockSpec((1, 128), lambda i: (0, 0))],
        out_shape=[jax.ShapeDtypeStruct((tp, d), F32), jax.ShapeDtypeStruct((1, d), F32),
                   jax.ShapeDtypeStruct((1, 128), F32)],
        compiler_params=_cp(1))(h, gain, target)


def _gated_headnorm(o, g, gain):
    return o * _rms(o) * gain * (g * _sigmoid(g))


def _row_mask(chunk):
    rows = chunk * CHUNK + lax.broadcasted_iota(jnp.int32, (CHUNK, 1), 0)
    return (rows >= FRONT - N_META).astype(F32)


def _ret_head(q1, q2, k1, k2, v, g, state, gain, cos, sin, dmat, dq, dk, dc):
    q = jnp.concatenate([q1 * cos - q2 * sin, q1 * sin + q2 * cos], axis=1)
    k = jnp.concatenate([k1 * cos - k2 * sin, k1 * sin + k2 * cos], axis=1) * (RET_DK ** -0.5)
    scores = _nt(q, k) * dmat
    o = _nn(scores, v) + _nn(q * dq, state)
    new_state = state * dc + _tn(k * dk, v)
    return _gated_headnorm(o, g, gain), new_state


def _ret_consts():
    log_gamma = jnp.log1p(-2.0 ** (-5.0 - jnp.arange(HEADS, dtype=F32)))
    idx = jnp.arange(CHUNK, dtype=F32)
    rel = idx[:, None] - idx[None, :]
    dmat = jnp.where(rel >= 0, jnp.exp(log_gamma[:, None, None] * jnp.maximum(rel, 0.0)), 0.0)
    dq = jnp.exp(log_gamma[:, None] * (idx + 1.0))[..., None]
    dk = jnp.exp(log_gamma[:, None] * (CHUNK - 1.0 - idx))[..., None]
    dc = jnp.broadcast_to(jnp.exp(log_gamma * CHUNK)[:, None, None], (HEADS, 1, 128))
    return dmat, dq, dk, dc


def _rope_tables(tp):
    half = RET_DK // 2
    inv = 1.0 / (ROPE_BASE ** jnp.linspace(0.0, 1.0, half, dtype=F32))
    pos = (jnp.arange(tp) - (FRONT - N_META)).astype(F32)
    ang = pos[:, None] * inv[None, :]
    return jnp.cos(ang), jnp.sin(ang)


_RET_V0, _RET_G0 = 2 * D, 4 * D


def _ret_pieces(u_ref, hd):
    f = lambda a, n: u_ref[:, a:a + n].astype(F32)
    hk = RET_DK // 2
    return (f(RET_DK * hd, hk), f(RET_DK * hd + hk, hk), f(D + RET_DK * hd, hk), f(D + RET_DK * hd + hk, hk),
            f(_RET_V0 + RET_DV * hd, RET_DV), f(_RET_G0 + RET_DV * hd, RET_DV))


def _ret_const_specs(rev=None):
    c = (lambda n: (rev(n), 0)) if rev else (lambda n: (n, 0))
    z3 = lambda n: (0, 0, 0)
    return [pl.BlockSpec((CHUNK, RET_DK // 2), c), pl.BlockSpec((CHUNK, RET_DK // 2), c),
            pl.BlockSpec((HEADS, CHUNK, CHUNK), z3), pl.BlockSpec((HEADS, CHUNK, 1), z3),
            pl.BlockSpec((HEADS, CHUNK, 1), z3), pl.BlockSpec((HEADS, 1, 128), z3)]


def _ret_fwd(u, gain, name):
    tp = u.shape[0]
    nch = tp // CHUNK
    cos, sin = _rope_tables(tp)
    dmat, dq, dk, dc = _ret_consts()

    def body(u_ref, gain_ref, cos_ref, sin_ref, dmat_ref, dq_ref, dk_ref, dc_ref, on_ref, st_ref, state_ref):
        @pl.when(pl.program_id(0) == 0)
        def _():
            state_ref[...] = jnp.zeros_like(state_ref)

        cos_v, sin_v = cos_ref[...], sin_ref[...]
        for hd in range(HEADS):
            state = state_ref[hd]
            st_ref[hd] = state.astype(BF16)
            on, new_state = _ret_head(*_ret_pieces(u_ref, hd), state,
                                      gain_ref[:, RET_DV * hd:RET_DV * (hd + 1)], cos_v, sin_v,
                                      dmat_ref[hd], dq_ref[hd], dk_ref[hd], dc_ref[hd][:, :1])
            state_ref[hd] = new_state
            on_ref[:, RET_DV * hd:RET_DV * (hd + 1)] = on.astype(BF16)

    return pl.pallas_call(
        body, name=name, grid=(nch,),
        in_specs=[pl.BlockSpec((CHUNK, 6 * D), lambda n: (n, 0)), pl.BlockSpec((1, HEADS * RET_DV), lambda n: (0, 0))]
                 + _ret_const_specs(),
        out_specs=[pl.BlockSpec((CHUNK, HEADS * RET_DV), lambda n: (n, 0)),
                   pl.BlockSpec((None, HEADS, RET_DK, RET_DV), lambda n: (n, 0, 0, 0))],
        out_shape=[jax.ShapeDtypeStruct((tp, HEADS * RET_DV), BF16),
                   jax.ShapeDtypeStruct((nch, HEADS, RET_DK, RET_DV), BF16)],
        scratch_shapes=[pltpu.VMEM((HEADS, RET_DK, RET_DV), F32)],
        compiler_params=_cp(1))(u, gain, cos, sin, dmat, dq, dk, dc)


def _ret_bwd(u, gain, states, d_on, name):
    tp = u.shape[0]
    nch = tp // CHUNK
    cos, sin = _rope_tables(tp)
    dmat, dq, dk, dc = _ret_consts()
    rev = lambda n: nch - 1 - n
    hk = RET_DK // 2

    def body(u_ref, gain_ref, st_ref, don_ref, cos_ref, sin_ref, dmat_ref, dq_ref, dk_ref, dc_ref,
             du_ref, dgain_ref, dstate_ref):
        @pl.when(pl.program_id(0) == 0)
        def _():
            dstate_ref[...] = jnp.zeros_like(dstate_ref)
            dgain_ref[...] = jnp.zeros_like(dgain_ref)

        cos_v, sin_v = cos_ref[...], sin_ref[...]
        mask = _row_mask(rev(pl.program_id(0)))
        for hd in range(HEADS):
            consts = (cos_v, sin_v, dmat_ref[hd], dq_ref[hd], dk_ref[hd], dc_ref[hd][:, :1])
            cols = slice(RET_DV * hd, RET_DV * (hd + 1))
            _, vjp = jax.vjp(lambda *a: _ret_head(*a, *consts), *_ret_pieces(u_ref, hd),
                             st_ref[hd].astype(F32), gain_ref[:, cols])
            dq1, dq2, dk1, dk2, dv, dg, dstate, dgain = vjp((don_ref[:, cols].astype(F32), dstate_ref[hd]))
            dstate_ref[hd] = dstate
            dgain_ref[:, cols] += dgain
            put = lambda a, t: du_ref.__setitem__((slice(None), slice(a, a + t.shape[1])), (t * mask).astype(BF16))
            put(RET_DK * hd, dq1)
            put(RET_DK * hd + hk, dq2)
            put(D + RET_DK * hd, dk1)
            put(D + RET_DK * hd + hk, dk2)
            put(_RET_V0 + RET_DV * hd, dv)
            put(_RET_G0 + RET_DV * hd, dg)

    return pl.pallas_call(
        body, name=name, grid=(nch,),
        in_specs=[pl.BlockSpec((CHUNK, 6 * D), lambda n: (rev(n), 0)),
                  pl.BlockSpec((1, HEADS * RET_DV), lambda n: (0, 0)),
                  pl.BlockSpec((None, HEADS, RET_DK, RET_DV), lambda n: (rev(n), 0, 0, 0)),
                  pl.BlockSpec((CHUNK, HEADS * RET_DV), lambda n: (rev(n), 0))] + _ret_const_specs(rev),
        out_specs=[pl.BlockSpec((CHUNK, 6 * D), lambda n: (rev(n), 0)),
                   pl.BlockSpec((1, HEADS * RET_DV), lambda n: (0, 0))],
        out_shape=[jax.ShapeDtypeStruct((tp, 6 * D), BF16), jax.ShapeDtypeStruct((1, HEADS * RET_DV), F32)],
        scratch_shapes=[pltpu.VMEM((HEADS, RET_DK, RET_DV), F32)],
        compiler_params=_cp(1))(u, gain, states, d_on, cos, sin, dmat, dq, dk, dc)


_GLA_K0, _GLA_V0, _GLA_G0, _GLA_Z0 = 512, 1024, 2048, 3072


def _gla_head(q, k, v, g, z, state_t, wg, bg, gain, mask, lo, lo_t, loc, loc_t):
    ga = _nn(z, wg) + bg
    log_a = (jnp.minimum(ga, 0.0) - jnp.log(1.0 + jnp.exp(-jnp.abs(ga)))) * (mask * (1.0 / GLA_TAU))
    bcum = _cum(lo, lo_t, log_a)
    bmid = _cum(loc, loc_t, log_a)
    btot = jnp.sum(log_a, axis=0, keepdims=True)
    qs = q * (GLA_DK ** -0.5)
    causal = lax.broadcasted_iota(jnp.int32, (CHUNK, CHUNK), 0) >= lax.broadcasted_iota(jnp.int32, (CHUNK, CHUNK), 1)
    scores = jnp.where(causal, _nt(qs * jnp.exp(bmid), k * jnp.exp(-bmid)), 0.0)
    o = _nn(scores, v) + _nt(qs * jnp.exp(bcum), state_t)
    new_state_t = state_t * jnp.exp(btot) + _tn(v, k * jnp.exp(btot - bcum))
    return _gated_headnorm(o, g, gain), new_state_t


def _cum_mats():
    r = lax.broadcasted_iota(jnp.int32, (CHUNK, CHUNK), 0)
    c = lax.broadcasted_iota(jnp.int32, (CHUNK, CHUNK), 1)
    mid = CHUNK // 2
    low = lambda a, b: (a >= b).astype(F32)
    lo, lo_t = low(r, c), low(c, r)
    loc = lo - (c <= mid).astype(F32)
    loc_t = lo_t - (r <= mid).astype(F32)
    return tuple(m.astype(BF16) for m in (lo, lo_t, loc, loc_t))


def _gla_pieces(u_ref, hd):
    f = lambda a, n: u_ref[:, a:a + n].astype(F32)
    return (f(GLA_DK * hd, GLA_DK), f(_GLA_K0 + GLA_DK * hd, GLA_DK), f(_GLA_V0 + GLA_DV * hd, GLA_DV),
            f(_GLA_G0 + GLA_DV * hd, GLA_DV), f(_GLA_Z0, 128))


def _gla_fwd(u, wg, bg, gain, name):
    tp = u.shape[0]
    nch = tp // CHUNK

    def body(u_ref, wg_ref, bg_ref, gain_ref, on_ref, st_ref, state_ref):
        @pl.when(pl.program_id(0) == 0)
        def _():
            state_ref[...] = jnp.zeros_like(state_ref)

        mask = _row_mask(pl.program_id(0))
        mats = _cum_mats()
        for hd in range(HEADS):
            state = state_ref[hd]
            st_ref[hd] = state.astype(BF16)
            kc = slice(GLA_DK * hd, GLA_DK * (hd + 1))
            vc = slice(GLA_DV * hd, GLA_DV * (hd + 1))
            on, new_state = _gla_head(*_gla_pieces(u_ref, hd), state,
                                      wg_ref[:, kc], bg_ref[:, kc], gain_ref[:, vc], mask, *mats)
            state_ref[hd] = new_state
            on_ref[:, vc] = on.astype(BF16)

    return pl.pallas_call(
        body, name=name, grid=(nch,),
        in_specs=[pl.BlockSpec((CHUNK, GLA_U), lambda n: (n, 0)), pl.BlockSpec((128, HEADS * GLA_DK), lambda n: (0, 0)),
                  pl.BlockSpec((1, HEADS * GLA_DK), lambda n: (0, 0)), pl.BlockSpec((1, HEADS * GLA_DV), lambda n: (0, 0))],
        out_specs=[pl.BlockSpec((CHUNK, HEADS * GLA_DV), lambda n: (n, 0)),
                   pl.BlockSpec((None, HEADS, GLA_DV, GLA_DK), lambda n: (n, 0, 0, 0))],
        out_shape=[jax.ShapeDtypeStruct((tp, HEADS * GLA_DV), BF16),
                   jax.ShapeDtypeStruct((nch, HEADS, GLA_DV, GLA_DK), BF16)],
        scratch_shapes=[pltpu.VMEM((HEADS, GLA_DV, GLA_DK), F32)],
        compiler_params=_cp(1))(u, wg, bg, gain)


def _gla_bwd(u, wg, bg, gain, states, d_on, name):
    tp = u.shape[0]
    nch = tp // CHUNK
    rev = lambda n: nch - 1 - n

    def body(u_ref, wg_ref, bg_ref, gain_ref, st_ref, don_ref, du_ref, dwg_ref, dbg_ref, dgain_ref, dstate_ref):
        @pl.when(pl.program_id(0) == 0)
        def _():
            dstate_ref[...] = jnp.zeros_like(dstate_ref)
            dwg_ref[...] = jnp.zeros_like(dwg_ref)
            dbg_ref[...] = jnp.zeros_like(dbg_ref)
            dgain_ref[...] = jnp.zeros_like(dgain_ref)

        mask = _row_mask(rev(pl.program_id(0)))
        mats = _cum_mats()
        dz_sum = jnp.zeros((CHUNK, 128), F32)
        for hd in range(HEADS):
            kc = slice(GLA_DK * hd, GLA_DK * (hd + 1))
            vc = slice(GLA_DV * hd, GLA_DV * (hd + 1))
            _, vjp = jax.vjp(lambda *a: _gla_head(*a, mask, *mats), *_gla_pieces(u_ref, hd),
                             st_ref[hd].astype(F32), wg_ref[:, kc].astype(F32), bg_ref[:, kc], gain_ref[:, vc])
            dq, dk, dv, dg, dz, dstate, dwg, dbg, dgain = vjp((don_ref[:, vc].astype(F32), dstate_ref[hd]))
            dstate_ref[hd] = dstate
            dwg_ref[:, kc] += dwg
            dbg_ref[:, kc] += dbg
            dgain_ref[:, vc] += dgain
            dz_sum = dz_sum + dz
            put = lambda a, t: du_ref.__setitem__((slice(None), slice(a, a + t.shape[1])), (t * mask).astype(BF16))
            put(GLA_DK * hd, dq)
            put(_GLA_K0 + GLA_DK * hd, dk)
            put(_GLA_V0 + GLA_DV * hd, dv)
            put(_GLA_G0 + GLA_DV * hd, dg)
        du_ref[:, _GLA_Z0:] = dz_sum.astype(BF16)

    full = lambda r, c: pl.BlockSpec((r, c), lambda n: (0, 0))
    return pl.pallas_call(
        body, name=name, grid=(nch,),
        in_specs=[pl.BlockSpec((CHUNK, GLA_U), lambda n: (rev(n), 0)), full(128, HEADS * GLA_DK),
                  full(1, HEADS * GLA_DK), full(1, HEADS * GLA_DV),
                  pl.BlockSpec((None, HEADS, GLA_DV, GLA_DK), lambda n: (rev(n), 0, 0, 0)),
                  pl.BlockSpec((CHUNK, HEADS * GLA_DV), lambda n: (rev(n), 0))],
        out_specs=[pl.BlockSpec((CHUNK, GLA_U), lambda n: (rev(n), 0)), full(128, HEADS * GLA_DK),
                   full(1, HEADS * GLA_DK), full(1, HEADS * GLA_DV)],
        out_shape=[jax.ShapeDtypeStruct((tp, GLA_U), BF16), jax.ShapeDtypeStruct((128, HEADS * GLA_DK), F32),
                   jax.ShapeDtypeStruct((1, HEADS * GLA_DK), F32), jax.ShapeDtypeStruct((1, HEADS * GLA_DV), F32)],
        scratch_shapes=[pltpu.VMEM((HEADS, GLA_DV, GLA_DK), F32)],
        compiler_params=_cp(1))(u, wg, bg, gain, states, d_on)


def _ffn_fwd(h, gain, w_in, w_out, tag):
    hn, ug, uu, act = _norm_ffn_in(h, gain, w_in, f"{tag}_in")
    return _out_proj(act, w_out, h, 0.5, f"{tag}_out"), (h, hn, ug, uu, act)


def _ffn_bwd(dh, saved, gain, w_in, w_out, tag):
    h, hn, ug, uu, act = saved
    du = _ffn_dact(dh, w_out, ug, uu, f"{tag}_dact")
    d_w_out = _wgrad(act, dh, bm=D_FF // 2, bn=D, scale=0.5, sharded=False, name=f"{tag}_dwout")
    d_w_in = _wgrad(hn, du, bm=D, bn=w_in.shape[2], scale=1.0, sharded=True, name=f"{tag}_dwin")
    dh_in, d_gain = _dgrad_norm(du, w_in, h, gain, dh, f"{tag}_dnorm")
    return dh_in, d_gain, d_w_in, d_w_out


def _sequence_grads(x, target, p, weights, grads):
    row = lambda v, token: v.reshape(1, -1) + token[0, 0]

    w0, tok = weights.group(0, None)
    h = jnp.concatenate([jnp.zeros((FRONT - N_META, D), F32), w0["meta"], x], axis=0)
    gain_f1 = [row(p["norm_ffn1"][0], tok), None]
    h, s1 = _ffn_fwd(h, gain_f1[0], w0["ffn1_in"], w0["ffn1_out"], "l0_ffn1")
    w1, tok = weights.group(1, h)
    gain_mix = [row(p["norm_mix"][0], tok), None]
    hn, u = _norm_proj(h, gain_mix[0], w1["ret_in"], "ret_in")
    on, states = _ret_fwd(u, w0["ret_gain"], "ret_fwd")
    h_mix = _out_proj(on, w1["ret_out"], h, 1.0, "ret_out")
    s2 = (h, hn, u, on, states)
    gain_f2 = [row(p["norm_ffn2"][0], tok), None]
    h, s3 = _ffn_fwd(h_mix, gain_f2[0], w1["ffn2_in"], w1["ffn2_out"], "l0_ffn2")
    saved = [(s1, s2, s3)]

    w2, tok = weights.group(2, h)
    gain_f1[1] = row(p["norm_ffn1"][1], tok)
    h, s1 = _ffn_fwd(h, gain_f1[1], w2["ffn1_in"], w2["ffn1_out"], "l1_ffn1")
    gain_mix[1] = row(p["norm_mix"][1], tok)
    hn, u = _norm_proj(h, gain_mix[1], w2["gla_in"], "gla_in")
    on, states = _gla_fwd(u, w0["gla_wg"], w0["gla_bg"], w0["gla_gain"], "gla_fwd")
    h_mix = _out_proj(on, w2["gla_out"], h, 1.0, "gla_out")
    s2 = (h, hn, u, on, states)
    gain_f2[1] = row(p["norm_ffn2"][1], tok)
    h, s3 = _ffn_fwd(h_mix, gain_f2[1], w2["ffn2_in"], w2["ffn2_out"], "l1_ffn2")
    saved.append((s1, s2, s3))

    dh, d_final, loss = _loss_head(h, row(p["final_norm"], tok), target, "loss_head")
    small = {"final_norm": d_final, "norm_ffn1": [None, None], "norm_mix": [None, None], "norm_ffn2": [None, None]}

    s1, s2, s3 = saved[1]
    dh, small["norm_ffn2"][1], d_in, d_out = _ffn_bwd(dh, s3, gain_f2[1], w2["ffn2_in"], w2["ffn2_out"], "l1_ffn2")
    tok = grads.push(0, [d_in, d_out])
    h_in, hn, u, on, states = s2
    d_on = _dgrad(dh, w2["gla_out"], "gla_don")
    d_gla_out = _wgrad(on, dh, bm=D, bn=D, scale=1.0, sharded=False, name="gla_dwout")
    du, small["gla_wg"], small["gla_bg"], small["gla_gain"] = _gla_bwd(
        u, w0["gla_wg"], w0["gla_bg"], w0["gla_gain"] + tok[0, 0], states, d_on, "gla_bwd")
    d_gla_in = _wgrad(hn, du, bm=D, bn=GLA_U // 5, scale=1.0, sharded=False, name="gla_dwin")
    dh, small["norm_mix"][1] = _dgrad_norm(du, w2["gla_in"], h_in, gain_mix[1], dh, "gla_dnorm")
    dh, small["norm_ffn1"][1], d_in, d_out = _ffn_bwd(dh, s1, gain_f1[1], w2["ffn1_in"], w2["ffn1_out"], "l1_ffn1")
    d_gla_in = jnp.moveaxis(d_gla_in[:, :GLA_IN].reshape(D, N_CHIPS, -1), 1, 0)
    tok = grads.push(1, [d_gla_in, d_gla_out, d_in, d_out])

    s1, s2, s3 = saved[0]
    dh, small["norm_ffn2"][0], d_in, d_out = _ffn_bwd(dh, s3, gain_f2[0] + tok[0, 0], w1["ffn2_in"], w1["ffn2_out"],
                                                      "l0_ffn2")
    h_in, hn, u, on, states = s2
    d_on = _dgrad(dh, w1["ret_out"], "ret_don")
    d_ret_out = _wgrad(on, dh, bm=D, bn=D, scale=1.0, sharded=False, name="ret_dwout")
    du, small["ret_gain"] = _ret_bwd(u, w0["ret_gain"], states, d_on, "ret_bwd")
    d_ret_in = _wgrad(hn, du, bm=D, bn=w1["ret_in"].shape[2], scale=1.0, sharded=True, name="ret_dwin")
    dh, small["norm_mix"][0] = _dgrad_norm(du, w1["ret_in"], h_in, gain_mix[0], dh, "ret_dnorm")
    tok = grads.push(2, [d_in, d_out, d_ret_in, d_ret_out])
    dh, small["norm_ffn1"][0], d_in, d_out = _ffn_bwd(dh, s1, gain_f1[0] + tok[0, 0], w0["ffn1_in"], w0["ffn1_out"],
                                                      "l0_ffn1")
    grads.push(3, [d_in, d_out], [dh[FRONT - N_META:FRONT], *small["norm_ffn1"], *small["norm_mix"], *small["norm_ffn2"],
                                  small["final_norm"], small["ret_gain"], small["gla_wg"][:GLA_RANK], small["gla_bg"],
                                  small["gla_gain"], loss[:, :1]])
    return dh[FRONT:]


_HBM = pl.BlockSpec(memory_space=pl.ANY)


def _place():
    return lax.axis_index("x"), lax.axis_index("y"), lax.axis_index("c")


def _flip(v, bit):
    return 1 - v if bit else v


DMA_CHUNK_BYTES = 128 * 1024


def _row_chunks(ref):
    rows, cols = ref.shape
    step = _row_tile(rows, max(16, DMA_CHUNK_BYTES // (cols * ref.dtype.itemsize)))
    return [pl.ds(a, step) for a in range(0, rows, step)]


def _whole(src, dst, send_sem, recv_sem, peer):
    return pltpu.make_async_remote_copy(src_ref=src, dst_ref=dst, send_sem=send_sem, recv_sem=recv_sem,
                                        device_id=peer, device_id_type=MESH)


def _send(src, dst, send_sem, recv_sem, peer):
    for rows in _row_chunks(src):
        _whole(src.at[rows], dst.at[rows], send_sem, recv_sem, peer).start()
    return _whole(src, dst, send_sem, recv_sem, peer)


_HBM_ONLY = pl.BlockSpec(memory_space=pltpu.HBM)
_SEMS = pl.BlockSpec(memory_space=pltpu.SEMAPHORE)
_SIDE_EFFECT = pltpu.CompilerParams(has_side_effects=pltpu.SideEffectType.DATAFLOW_SIDE_EFFECTING)
_CHIP_FLIPS = [(1, 0, 0), (0, 1, 0), (1, 1, 0)]
_PEER_FLIPS = [(fx, fy, fc) for fx in (0, 1) for fy in (0, 1) for fc in (0, 1)][1:]


def _zero_token():
    return jnp.zeros((8, 128), F32)


def _exchange_start(srcs, lands, route, flips, after, name):
    n = len(srcs)

    def body(*refs):
        src, land = refs[:n], refs[n:2 * n]
        send_sems, recv_sems, token = refs[2 * n + 1], refs[2 * n + 2], refs[-1]
        me = _place()
        for t in range(n):
            for j, flip in enumerate(flips):
                peer = tuple(_flip(v, f) for v, f in zip(me, flip))
                s, d = route(t, src[t], land[t], me, peer)
                _send(s, d, send_sems.at[t * len(flips) + j], recv_sems.at[t * len(flips) + j], peer)
        token[...] = jnp.zeros_like(token)

    hbm = lambda a: pltpu.HBM(a.shape, a.dtype)
    sems = pltpu.SemaphoreType.DMA((n * len(flips),))
    operands = [pltpu.with_memory_space_constraint(a, pltpu.HBM) for a in list(srcs) + list(lands)]
    out = pl.pallas_call(
        body, name=name, in_specs=[_HBM_ONLY] * (2 * n) + [_HBM],
        out_shape=(sems, sems, *[hbm(a) for a in operands], jax.ShapeDtypeStruct((8, 128), F32)),
        out_specs=(_SEMS, _SEMS, *[_HBM_ONLY] * (2 * n), pl.BlockSpec(memory_space=pltpu.VMEM)),
        input_output_aliases={i: 2 + i for i in range(2 * n)}, compiler_params=_SIDE_EFFECT,
    )(*operands, _zero_token() if after is None else after)
    return (out[0], out[1], out[2:2 + n], out[2 + n:2 + 2 * n]), out[-1]


def _exchange_wait(started, route, flips, after, name):
    send_sems, recv_sems, srcs, lands = started
    n = len(srcs)

    def body(*refs):
        src, land = refs[:n], refs[n:2 * n]
        send_sems, recv_sems = refs[2 * n], refs[2 * n + 1]
        me = _place()
        for t in range(n):
            for j, flip in enumerate(flips):
                peer = tuple(_flip(v, f) for v, f in zip(me, flip))
                s, d = route(t, src[t], land[t], me, peer)
                cp = _whole(s, d, send_sems.at[t * len(flips) + j], recv_sems.at[t * len(flips) + j], peer)
                cp.wait_send()
                cp.wait_recv()

    hbm = lambda a: pltpu.HBM(a.shape, a.dtype)
    out = pl.pallas_call(
        body, name=name, in_specs=[_HBM_ONLY] * (2 * n) + [_SEMS, _SEMS, _HBM],
        out_shape=tuple(hbm(a) for a in list(srcs) + list(lands)), out_specs=tuple([_HBM_ONLY] * (2 * n)),
        input_output_aliases={i: i for i in range(2 * n)}, compiler_params=_SIDE_EFFECT,
    )(*srcs, *lands, send_sems, recv_sems, after)
    return out[:n], out[n:]


def _gather_route(t, src, land, me, peer):
    return src, land.at[2 * me[0] + me[1]]


def _scatter_route(n_pieces):
    def route(t, src, land, me, peer):
        part = src.at[2 * peer[0] + peer[1], peer[2]] if t < n_pieces else src
        return part, land.at[4 * me[0] + 2 * me[1] + me[2]]

    return route


def _swap_cores(halves):
    n = len(halves)

    def body(*refs):
        src, dst = refs[:n], refs[n:2 * n]
        send_sems, recv_sems = refs[2 * n:]
        x, y, c = _place()
        copies = [_send(src[t], dst[t], send_sems.at[t], recv_sems.at[t], (x, y, 1 - c)) for t in range(n)]
        for cp in copies:
            cp.wait()

    got = pl.pallas_call(
        body, name="swap_cores", in_specs=[_HBM] * n, out_specs=[_HBM] * n,
        out_shape=[jax.ShapeDtypeStruct(a.shape, a.dtype) for a in halves],
        scratch_shapes=[pltpu.SemaphoreType.DMA((n,)), pltpu.SemaphoreType.DMA((n,))],
    )(*halves)
    south = lax.axis_index("c") == 0
    return [jnp.stack([jnp.where(south, a, b), jnp.where(south, b, a)]) for a, b in zip(halves, got)]


def _row_tile(rows, cap):
    fits = [t for t in range(16, cap + 1, 16) if rows % t == 0]
    return fits[-1] if fits else rows


def _sum_slots(a, name):
    _, r, c = a.shape
    tr = _row_tile(r, 384)

    def body(a_ref, o_ref):
        s = a_ref[0].astype(F32)
        for k in range(1, N_DEV):
            s = s + a_ref[k].astype(F32)
        o_ref[...] = s

    return pl.pallas_call(
        body, name=name, grid=(r // tr,),
        in_specs=[pl.BlockSpec((N_DEV, tr, c), lambda i: (0, i, 0))],
        out_specs=pl.BlockSpec((tr, c), lambda i: (i, 0)),
        out_shape=jax.ShapeDtypeStruct((r, c), F32),
        compiler_params=_cp(1))(a)


def _adamw(w, g, m, v, name):
    r, c = w.shape
    tr = _row_tile(r, 256)

    def body(w_ref, g_ref, m_ref, v_ref, d_ref, nm_ref, nv_ref):
        gv = g_ref[...]
        nm = ADAM_B1 * m_ref[...] + (1.0 - ADAM_B1) * gv
        nv = ADAM_B2 * v_ref[...] + (1.0 - ADAM_B2) * (gv * gv)
        m_hat = nm / (1.0 - ADAM_B1 ** ADAM_STEP)
        v_hat = nv / (1.0 - ADAM_B2 ** ADAM_STEP)
        d_ref[...] = -ADAM_LR * (m_hat / (jnp.sqrt(v_hat) + ADAM_EPS) + ADAM_WD * w_ref[...])
        nm_ref[...] = nm
        nv_ref[...] = nv

    spec = pl.BlockSpec((tr, c), lambda i: (i, 0))
    return pl.pallas_call(
        body, name=name, grid=(r // tr,), in_specs=[spec] * 4, out_specs=[spec] * 3,
        out_shape=[jax.ShapeDtypeStruct((r, c), F32)] * 3,
        compiler_params=_cp(1))(w, g, m, v)


_SMALL = ["meta_tokens", "ret_head_norm", "gla_w_gate", "gla_b_gate", "gla_head_norm"]
_LOCAL_SMALL = ["meta_tokens", "norm_ffn1", "norm_mix", "norm_ffn2", "ret_head_norm", "gla_w_gate", "gla_b_gate",
                "gla_head_norm", "final_norm"]
_BIG = ["ffn1_w_in", "ffn1_w_out", "ffn2_w_in", "ffn2_w_out", "ret_w_in", "ret_w_out", "gla_w_in", "gla_w_out"]
_WEIGHTS = ["meta_tokens", "norm_ffn1", "ffn1_w_in", "ffn1_w_out", "norm_mix", "norm_ffn2", "ffn2_w_in", "ffn2_w_out",
            "ret_w_in", "ret_head_norm", "ret_w_out", "gla_w_in", "gla_w_gate", "gla_b_gate", "gla_head_norm",
            "gla_w_out", "final_norm"]


def _pack_rows(arrays, width):
    flat = jnp.concatenate([a.reshape(-1) for a in arrays])
    pad = -flat.shape[0] % (8 * width)
    return jnp.pad(flat, (0, pad)).reshape(-1, width)


def _unpack_rows(packed, shapes):
    flat, out, at = packed.reshape(-1), [], 0
    for s in shapes:
        size = 1
        for dim in s:
            size *= dim
        out.append(flat[at:at + size].reshape(s))
        at += size
    return out


class _WeightGather:
    def __init__(self, p):
        b = lambda a: a.astype(BF16)
        ffn = lambda name, layer: [b(p[f"{name}_w_in"][layer]), b(p[f"{name}_w_out"][layer])]
        self.small_shapes = [p[name].shape for name in _SMALL]
        self.shards = [ffn("ffn1", 0) + [_pack_rows([p[name] for name in _SMALL], 128)],
                       [b(p["ret_w_in"][0]), b(p["ret_w_out"][0])] + ffn("ffn2", 0),
                       ffn("ffn1", 1) + [b(p["gla_w_in"][0]), b(p["gla_w_out"][0])] + ffn("ffn2", 1)]
        self.started = None

    def _start(self, k, after):
        lands = [lax.empty((N_CHIPS,) + s.shape, s.dtype) for s in self.shards[k]]
        self.started, token = _exchange_start(self.shards[k], lands, _gather_route, _CHIP_FLIPS, after,
                                              f"gather{k}_start")
        return token

    def group(self, k, after):
        if k == 0:
            after = self._start(0, None)
        shards, got = _exchange_wait(self.started, _gather_route, _CHIP_FLIPS, after, f"gather{k}_wait")
        mine = 2 * lax.axis_index("x") + lax.axis_index("y")
        got = [lax.dynamic_update_index_in_dim(g, s, mine, 0) for g, s in zip(got, shards)]
        token = self._start(k + 1, got[0]) if k + 1 < len(self.shards) else _zero_token()
        rows = lambda a: a.reshape(-1, a.shape[-1])
        if k == 0:
            w = {"ffn1_in": got[0], "ffn1_out": rows(got[1])}
            parts = zip(*[_unpack_rows(got[2][s], self.small_shapes) for s in range(N_CHIPS)])
            cat = lambda a: jnp.moveaxis(a, 0, -2).reshape(a.shape[1:-1] + (-1,))
            meta, ret_gain, wg, bg, gla_gain = [cat(jnp.stack(part)) for part in parts]
            w.update(meta=meta, ret_gain=ret_gain.reshape(1, -1), gla_bg=bg.reshape(1, -1),
                     gla_gain=gla_gain.reshape(1, -1),
                     gla_wg=jnp.pad(wg[0], ((0, 128 - GLA_RANK), (0, 0))).astype(BF16))
        elif k == 1:
            w = {"ret_in": got[0], "ret_out": rows(got[1]), "ffn2_in": got[2], "ffn2_out": rows(got[3])}
        else:
            gla_in = jnp.moveaxis(got[2], 0, 1).reshape(D, -1)
            w = {"ffn1_in": got[0], "ffn1_out": rows(got[1]), "gla_out": rows(got[3]), "ffn2_in": got[4],
                 "ffn2_out": rows(got[5]), "gla_in": jnp.pad(gla_in, ((0, 0), (0, GLA_U - gla_in.shape[1])))[None]}
        return w, token


class _GradExchange:
    def __init__(self):
        self.started = []
        self.token = None
        self.small_shapes = None

    def push(self, k, arrays, small=None):
        srcs = [a.reshape(N_CHIPS, 2, -1, a.shape[-1]) for a in arrays]
        lands = [lax.empty((N_DEV,) + a.shape[2:], a.dtype) for a in srcs]
        if small is not None:
            self.small_shapes = [a.shape for a in small]
            srcs.append(_pack_rows(small, D))
            lands.append(lax.empty((N_DEV,) + srcs[-1].shape, F32))
        started, self.token = _exchange_start(srcs, lands, _scatter_route(len(arrays)), _PEER_FLIPS, None,
                                              f"scatter{k}_start")
        self.started.append((started, len(arrays)))
        return self.token

    def collect(self):
        x, y, c = _place()
        after, sums = self.token, []
        for k, (started, n_pieces) in enumerate(self.started):
            srcs, got = _exchange_wait(started, _scatter_route(n_pieces), _PEER_FLIPS, after, f"scatter{k}_wait")
            own = [a[2 * x + y, c] for a in srcs[:n_pieces]] + list(srcs[n_pieces:])
            got = [lax.dynamic_update_index_in_dim(g, a, 4 * x + 2 * y + c, 0) for g, a in zip(got, own)]
            sums.append([_sum_slots(a, f"sum{k}_{i}") for i, a in enumerate(got)])
            after = sums[-1][0]
        small = _unpack_rows(sums[-1].pop(), self.small_shapes)
        return sums, small


def kernel(x, meta_tokens, norm_ffn1, ffn1_w_in, ffn1_w_out, norm_mix, norm_ffn2, ffn2_w_in, ffn2_w_out, ret_w_in, ret_head_norm, ret_w_out, gla_w_in, gla_w_gate, gla_b_gate, gla_head_norm, gla_w_out, final_norm, loss_target, m_meta_tokens, m_norm_ffn1, m_ffn1_w_in, m_ffn1_w_out, m_norm_mix, m_norm_ffn2, m_ffn2_w_in, m_ffn2_w_out, m_ret_w_in, m_ret_head_norm, m_ret_w_out, m_gla_w_in, m_gla_w_gate, m_gla_b_gate, m_gla_head_norm, m_gla_w_out, m_final_norm, v_meta_tokens, v_norm_ffn1, v_ffn1_w_in, v_ffn1_w_out, v_norm_mix, v_norm_ffn2, v_ffn2_w_in, v_ffn2_w_out, v_ret_w_in, v_ret_head_norm, v_ret_w_out, v_gla_w_in, v_gla_w_gate, v_gla_b_gate, v_gla_head_norm, v_gla_w_out, v_final_norm):
    p = dict(meta_tokens=meta_tokens, norm_ffn1=norm_ffn1, ffn1_w_in=ffn1_w_in, ffn1_w_out=ffn1_w_out, norm_mix=norm_mix,
             norm_ffn2=norm_ffn2, ffn2_w_in=ffn2_w_in, ffn2_w_out=ffn2_w_out, ret_w_in=ret_w_in,
             ret_head_norm=ret_head_norm, ret_w_out=ret_w_out, gla_w_in=gla_w_in, gla_w_gate=gla_w_gate,
             gla_b_gate=gla_b_gate, gla_head_norm=gla_head_norm, gla_w_out=gla_w_out, final_norm=final_norm)
    m = dict(zip(_WEIGHTS, (m_meta_tokens, m_norm_ffn1, m_ffn1_w_in, m_ffn1_w_out, m_norm_mix, m_norm_ffn2, m_ffn2_w_in,
                            m_ffn2_w_out, m_ret_w_in, m_ret_head_norm, m_ret_w_out, m_gla_w_in, m_gla_w_gate,
                            m_gla_b_gate, m_gla_head_norm, m_gla_w_out, m_final_norm)))
    v = dict(zip(_WEIGHTS, (v_meta_tokens, v_norm_ffn1, v_ffn1_w_in, v_ffn1_w_out, v_norm_mix, v_norm_ffn2, v_ffn2_w_in,
                            v_ffn2_w_out, v_ret_w_in, v_ret_head_norm, v_ret_w_out, v_gla_w_in, v_gla_w_gate,
                            v_gla_b_gate, v_gla_head_norm, v_gla_w_out, v_final_norm)))

    exchange = _GradExchange()
    d_x = _sequence_grads(x[0], loss_target[0], p, _WeightGather(p), exchange)
    sums, small = exchange.collect()
    names = [("ffn2_in", 1), ("ffn2_out", 1), ("gla_in", 0), ("gla_out", 0), ("ffn1_in", 1), ("ffn1_out", 1),
             ("ffn2_in", 0), ("ffn2_out", 0), ("ret_in", 0), ("ret_out", 0), ("ffn1_in", 0), ("ffn1_out", 0)]
    swapped = _swap_cores([a for group in sums for a in group])
    shard = {key: a.reshape(-1, a.shape[-1]) for key, a in zip(names, swapped)}
    big = {name: [shard[name, layer] for layer in range(2) if (name, layer) in shard] for name, _ in names}

    chip = 2 * lax.axis_index("x") + lax.axis_index("y")
    cols = lambda a, n: lax.dynamic_slice_in_dim(a, chip * n, n, axis=a.ndim - 1)
    (s_meta, s_n1a, s_n1b, s_nma, s_nmb, s_n2a, s_n2b, s_final, s_ret_gain, s_wg, s_bg, s_gla_gain, s_loss) = small
    grads = {
        "meta_tokens": cols(s_meta, 256), "norm_ffn1": jnp.concatenate([s_n1a, s_n1b]),
        "norm_mix": jnp.concatenate([s_nma, s_nmb]), "norm_ffn2": jnp.concatenate([s_n2a, s_n2b]),
        "final_norm": s_final.reshape(D),
        "ret_head_norm": cols(s_ret_gain.reshape(1, HEADS, RET_DV), RET_DV // N_CHIPS),
        "gla_w_gate": cols(s_wg, GLA_DK)[None], "gla_b_gate": cols(s_bg, GLA_DK),
        "gla_head_norm": cols(s_gla_gain.reshape(1, HEADS, GLA_DV), GLA_DV // N_CHIPS),
        "ffn1_w_in": jnp.stack(big["ffn1_in"]), "ffn1_w_out": jnp.stack(big["ffn1_out"]),
        "ffn2_w_in": jnp.stack(big["ffn2_in"]), "ffn2_w_out": jnp.stack(big["ffn2_out"]),
        "ret_w_in": big["ret_in"][0][None], "ret_w_out": big["ret_out"][0][None],
        "gla_w_in": big["gla_in"][0][None], "gla_w_out": big["gla_out"][0][None],
    }

    delta, new_m, new_v = {}, {}, {}
    for name in _BIG:
        shape = p[name].shape
        flat = lambda a: a.reshape(-1, shape[-1])
        out = _adamw(flat(p[name]), flat(grads[name]), flat(m[name]), flat(v[name]), f"adamw_{name}")
        delta[name], new_m[name], new_v[name] = [a.reshape(shape) for a in out]
    packed = [_pack_rows([d[name] for name in _LOCAL_SMALL], 128) for d in (p, grads, m, v)]
    out = _adamw(*packed, "adamw_small")
    shapes = [p[name].shape for name in _LOCAL_SMALL]
    for d, a in zip((delta, new_m, new_v), out):
        d.update(zip(_LOCAL_SMALL, _unpack_rows(a, shapes)))

    return (s_loss.reshape(()), d_x[None], *[grads[n] for n in _WEIGHTS], *[delta[n] for n in _WEIGHTS],
            *[new_m[n] for n in _WEIGHTS], *[new_v[n] for n in _WEIGHTS])
```

```python
import functools

import jax
import jax.numpy as jnp
from jax import lax
from jax.experimental import pallas as pl
from jax.experimental.pallas import tpu as pltpu

F32, BF16 = jnp.float32, jnp.bfloat16
MESH = pl.DeviceIdType.MESH

D = 1024
N_META = 16
CHUNK = 64
FRONT = 256
D_FF = 2816
EPS = 1e-6
HEADS = 4
RET_DK, RET_DV = 256, 512
GLA_DK, GLA_DV = 128, 256
GLA_RANK = 16
GLA_TAU = 16.0
GLA_IN = 2 * HEADS * GLA_DK + 2 * HEADS * GLA_DV + GLA_RANK
GLA_U = 3200
ROPE_BASE = 10000.0
N_CHIPS = 4
N_DEV = 8

ADAM_LR, ADAM_B1, ADAM_B2, ADAM_EPS, ADAM_WD, ADAM_STEP = 0.001, 0.9, 0.999, 1e-08, 0.01, 10

VMEM_LIMIT_BYTES = 56 * 1024 * 1024
TM = 768
TM_SMALL = 256


def _cp(n_axes):
    return pltpu.CompilerParams(dimension_semantics=("arbitrary",) * n_axes, vmem_limit_bytes=VMEM_LIMIT_BYTES)


def _dg(a, b, ca, cb):
    return lax.dot_general(a.astype(BF16), b.astype(BF16), (((ca,), (cb,)), ((), ())), preferred_element_type=F32)


@jax.custom_vjp
def _nn(a, b):
    return _dg(a, b, 1, 0)


@jax.custom_vjp
def _nt(a, b):
    return _dg(a, b, 1, 1)


@jax.custom_vjp
def _tn(a, b):
    return _dg(a, b, 0, 0)


_nn.defvjp(lambda a, b: (_nn(a, b), (a, b)), lambda res, g: (_nt(g, res[1]), _tn(res[0], g)))
_nt.defvjp(lambda a, b: (_nt(a, b), (a, b)), lambda res, g: (_nn(g, res[1]), _tn(g, res[0])))
_tn.defvjp(lambda a, b: (_tn(a, b), (a, b)), lambda res, g: (_nt(res[1], g), _nn(res[0], g)))


def _split3_dot(m, a):
    a1 = a.astype(BF16)
    r1 = a - a1.astype(F32)
    a2 = r1.astype(BF16)
    a3 = (r1 - a2.astype(F32)).astype(BF16)
    dot = lambda p: jnp.dot(m, p, preferred_element_type=F32)
    return dot(a1) + dot(a2) + dot(a3)


@jax.custom_vjp
def _cum(m, mt, a):
    return _split3_dot(m, a)


_cum.defvjp(lambda m, mt, a: (_split3_dot(m, a), (m, mt)),
            lambda res, g: (jnp.zeros_like(res[0]), jnp.zeros_like(res[1]), _split3_dot(res[1], g)))


def _sigmoid(x):
    return 1.0 / (1.0 + jnp.exp(-x))


def _rms(x):
    return lax.rsqrt(jnp.mean(x * x, axis=-1, keepdims=True) + EPS)


def _rmsnorm_bwd(dy, x, gain):
    r = _rms(x)
    xhat = x * r
    dxh = dy * gain
    return r * (dxh - xhat * jnp.mean(dxh * xhat, axis=-1, keepdims=True)), xhat


def _norm_proj(h, gain, w, name):
    tp, d = h.shape
    s, _, ns = w.shape

    def body(h_ref, g_ref, w_ref, hn_ref, u_ref):
        @pl.when(pl.program_id(1) == 0)
        def _():
            x = h_ref[...]
            hn_ref[...] = (x * _rms(x) * g_ref[...]).astype(BF16)

        u_ref[...] = jnp.dot(hn_ref[...], w_ref[...], preferred_element_type=F32).astype(BF16)

    return pl.pallas_call(
        body, name=name, grid=(tp // TM, s),
        in_specs=[pl.BlockSpec((TM, d), lambda i, j: (i, 0)), pl.BlockSpec((1, d), lambda i, j: (0, 0)),
                  pl.BlockSpec((None, d, ns), lambda i, j: (j, 0, 0))],
        out_specs=[pl.BlockSpec((TM, d), lambda i, j: (i, 0)), pl.BlockSpec((TM, ns), lambda i, j: (i, j))],
        out_shape=[jax.ShapeDtypeStruct((tp, d), BF16), jax.ShapeDtypeStruct((tp, s * ns), BF16)],
        compiler_params=_cp(2))(h, gain, w)


def _norm_ffn_in(h, gain, w, name):
    tp, d = h.shape
    s, _, ns = w.shape
    half = s // 2

    def body(h_ref, g_ref, wg_ref, wu_ref, hn_ref, ug_ref, uu_ref, act_ref):
        @pl.when(pl.program_id(1) == 0)
        def _():
            x = h_ref[...]
            hn_ref[...] = (x * _rms(x) * g_ref[...]).astype(BF16)

        a = hn_ref[...]
        g = jnp.dot(a, wg_ref[...], preferred_element_type=F32)
        u = jnp.dot(a, wu_ref[...], preferred_element_type=F32)
        ug_ref[...] = g.astype(BF16)
        uu_ref[...] = u.astype(BF16)
        act_ref[...] = (g * _sigmoid(g) * u).astype(BF16)

    wide = jax.ShapeDtypeStruct((tp, half * ns), BF16)
    return pl.pallas_call(
        body, name=name, grid=(tp // TM, half),
        in_specs=[pl.BlockSpec((TM, d), lambda i, j: (i, 0)), pl.BlockSpec((1, d), lambda i, j: (0, 0)),
                  pl.BlockSpec((None, d, ns), lambda i, j: (j, 0, 0)),
                  pl.BlockSpec((None, d, ns), lambda i, j: (j + half, 0, 0))],
        out_specs=[pl.BlockSpec((TM, d), lambda i, j: (i, 0))] + [pl.BlockSpec((TM, ns), lambda i, j: (i, j))] * 3,
        out_shape=[jax.ShapeDtypeStruct((tp, d), BF16), wide, wide, wide],
        compiler_params=_cp(2))(h, gain, w, w)


def _out_proj(a, w, h, scale, name):
    tp, k = a.shape
    d = w.shape[1]

    def body(a_ref, w_ref, h_ref, o_ref):
        o_ref[...] = h_ref[...] + scale * jnp.dot(a_ref[...], w_ref[...], preferred_element_type=F32)

    return pl.pallas_call(
        body, name=name, grid=(tp // TM,),
        in_specs=[pl.BlockSpec((TM, k), lambda i: (i, 0)), pl.BlockSpec((k, d), lambda i: (0, 0)),
                  pl.BlockSpec((TM, d), lambda i: (i, 0))],
        out_specs=pl.BlockSpec((TM, d), lambda i: (i, 0)),
        out_shape=jax.ShapeDtypeStruct((tp, d), F32),
        compiler_params=_cp(1))(a, w, h)


def _ffn_dact(dh, w_out, ug, uu, name):
    tp, d = dh.shape
    ff = w_out.shape[0]
    tm = TM_SMALL

    def body(dh_ref, w_ref, ug_ref, uu_ref, du_ref):
        dy = (0.5 * dh_ref[...]).astype(BF16)
        dact = lax.dot_general(dy, w_ref[...], (((1,), (1,)), ((), ())), preferred_element_type=F32)
        g = ug_ref[...].astype(F32)
        u = uu_ref[...].astype(F32)
        sg = _sigmoid(g)
        du_ref[:, :ff] = (dact * u * (sg * (1.0 + g * (1.0 - sg)))).astype(BF16)
        du_ref[:, ff:] = (dact * (g * sg)).astype(BF16)

    return pl.pallas_call(
        body, name=name, grid=(tp // tm,),
        in_specs=[pl.BlockSpec((tm, d), lambda i: (i, 0)), pl.BlockSpec((ff, d), lambda i: (0, 0)),
                  pl.BlockSpec((tm, ff), lambda i: (i, 0)), pl.BlockSpec((tm, ff), lambda i: (i, 0))],
        out_specs=pl.BlockSpec((tm, 2 * ff), lambda i: (i, 0)),
        out_shape=jax.ShapeDtypeStruct((tp, 2 * ff), BF16),
        compiler_params=_cp(1))(dh, w_out, ug, uu)


def _dgrad(dh, w, name):
    tp, d = dh.shape
    k = w.shape[0]

    def body(dh_ref, w_ref, o_ref):
        o_ref[...] = lax.dot_general(dh_ref[...].astype(BF16), w_ref[...], (((1,), (1,)), ((), ())),
                                     preferred_element_type=F32).astype(BF16)

    return pl.pallas_call(
        body, name=name, grid=(tp // TM,),
        in_specs=[pl.BlockSpec((TM, d), lambda i: (i, 0)), pl.BlockSpec((k, d), lambda i: (0, 0))],
        out_specs=pl.BlockSpec((TM, k), lambda i: (i, 0)),
        out_shape=jax.ShapeDtypeStruct((tp, k), BF16),
        compiler_params=_cp(1))(dh, w)


def _wgrad(a, b, *, bm, bn, scale, sharded, name):
    tp, m = a.shape
    n = b.shape[1]
    nk = tp // TM

    def body(a_ref, b_ref, o_ref, acc_ref):
        k = pl.program_id(2)

        @pl.when(k == 0)
        def _():
            acc_ref[...] = jnp.zeros_like(acc_ref)

        bb = b_ref[...]
        if scale != 1.0:
            bb = scale * bb
        acc_ref[...] += lax.dot_general(a_ref[...], bb.astype(BF16), (((0,), (0,)), ((), ())),
                                        preferred_element_type=F32)

        @pl.when(k == nk - 1)
        def _():
            o_ref[...] = acc_ref[...].astype(BF16)

    if sharded:
        assert m == bm
        out_spec = pl.BlockSpec((None, bm, bn), lambda i, j, k: (j, 0, 0))
        out_shape = jax.ShapeDtypeStruct((n // bn, m, bn), BF16)
    else:
        out_spec = pl.BlockSpec((bm, bn), lambda i, j, k: (i, j))
        out_shape = jax.ShapeDtypeStruct((m, n), BF16)
    return pl.pallas_call(
        body, name=name, grid=(m // bm, n // bn, nk),
        in_specs=[pl.BlockSpec((TM, bm), lambda i, j, k: (k, i)), pl.BlockSpec((TM, bn), lambda i, j, k: (k, j))],
        out_specs=out_spec, out_shape=out_shape,
        scratch_shapes=[pltpu.VMEM((bm, bn), F32)],
        compiler_params=_cp(3))(a, b)


def _dgrad_norm(du, w, h, gain, dh_out, name):
    tp, d = h.shape
    s, _, ns = w.shape

    def body(du_ref, w_ref, h_ref, g_ref, dho_ref, dhi_ref, dg_ref, acc_ref):
        i, k = pl.program_id(0), pl.program_id(1)

        @pl.when(k == 0)
        def _():
            acc_ref[...] = jnp.zeros_like(acc_ref)

        @pl.when((i == 0) & (k == 0))
        def _():
            dg_ref[...] = jnp.zeros_like(dg_ref)

        acc_ref[...] += lax.dot_general(du_ref[...], w_ref[...], (((1,), (1,)), ((), ())),
                                        preferred_element_type=F32)

        @pl.when(k == s - 1)
        def _():
            dhn = acc_ref[...]
            dx, xhat = _rmsnorm_bwd(dhn, h_ref[...], g_ref[...])
            dg_ref[...] += jnp.sum(dhn * xhat, axis=0, keepdims=True)
            dhi_ref[...] = dho_ref[...] + dx

    return pl.pallas_call(
        body, name=name, grid=(tp // TM, s),
        in_specs=[pl.BlockSpec((TM, ns), lambda i, k: (i, k)), pl.BlockSpec((None, d, ns), lambda i, k: (k, 0, 0)),
                  pl.BlockSpec((TM, d), lambda i, k: (i, 0)), pl.BlockSpec((1, d), lambda i, k: (0, 0)),
                  pl.BlockSpec((TM, d), lambda i, k: (i, 0))],
        out_specs=[pl.BlockSpec((TM, d), lambda i, k: (i, 0)), pl.BlockSpec((1, d), lambda i, k: (0, 0))],
        out_shape=[jax.ShapeDtypeStruct((tp, d), F32), jax.ShapeDtypeStruct((1, d), F32)],
        scratch_shapes=[pltpu.VMEM((TM, d), F32)],
        compiler_params=_cp(2))(du, w, h, gain, dh_out)


def _loss_head(h, gain, target, name):
    tp, d = h.shape
    tm = TM_SMALL
    front_tiles = FRONT // tm

    def body(h_ref, g_ref, t_ref, dh_ref, dg_ref, loss_ref):
        i = pl.program_id(0)

        @pl.when(i == 0)
        def _():
            dg_ref[...] = jnp.zeros_like(dg_ref)
            loss_ref[...] = jnp.zeros_like(loss_ref)

        x = h_ref[...]
        gain_v = g_ref[...]
        y = x * _rms(x) * gain_v
        err = jnp.where(i >= front_tiles, y - t_ref[...], 0.0)
        loss_ref[...] += 0.5 * jnp.sum(jnp.mean(err * err, axis=-1, keepdims=True), axis=0, keepdims=True)
        dy = err * (1.0 / d)
        dx, xhat = _rmsnorm_bwd(dy, x, gain_v)
        dg_ref[...] += jnp.sum(dy * xhat, axis=0, keepdims=True)
        dh_ref[...] = dx

    return pl.pallas_call(
        body, name=name, grid=(tp // tm,),
        in_specs=[pl.BlockSpec((tm, d), lambda i: (i, 0)), pl.BlockSpec((1, d), lambda i: (0, 0)),
                  pl.BlockSpec((tm, d), lambda i: (jnp.maximum(i - front_tiles, 0), 0))],
        out_specs=[pl.BlockSpec((tm, d), lambda i: (i, 0)), pl.BlockSpec((1, d), lambda i: (0, 0)),
                   pl.BlockSpec((1, 128), lambda i: (0, 0))],
        out_shape=[jax.ShapeDtypeStruct((tp, d), F32), jax.ShapeDtypeStruct((1, d), F32),
                   jax.ShapeDtypeStruct((1, 128), F32)],
        compiler_params=_cp(1))(h, gain, target)


def _gated_headnorm(o, g, gain):
    return o * _rms(o) * gain * (g * _sigmoid(g))


def _row_mask(chunk):
    rows = chunk * CHUNK + lax.broadcasted_iota(jnp.int32, (CHUNK, 1), 0)
    return (rows >= FRONT - N_META).astype(F32)


def _ret_head(q1, q2, k1, k2, v, g, state, gain, cos, sin, dmat, dq, dk, dc):
    q = jnp.concatenate([q1 * cos - q2 * sin, q1 * sin + q2 * cos], axis=1)
    k = jnp.concatenate([k1 * cos - k2 * sin, k1 * sin + k2 * cos], axis=1) * (RET_DK ** -0.5)
    scores = _nt(q, k) * dmat
    o = _nn(scores, v) + _nn(q * dq, state)
    new_state = state * dc + _tn(k * dk, v)
    return _gated_headnorm(o, g, gain), new_state


def _ret_consts():
    log_gamma = jnp.log1p(-2.0 ** (-5.0 - jnp.arange(HEADS, dtype=F32)))
    idx = jnp.arange(CHUNK, dtype=F32)
    rel = idx[:, None] - idx[None, :]
    dmat = jnp.where(rel >= 0, jnp.exp(log_gamma[:, None, None] * jnp.maximum(rel, 0.0)), 0.0)
    dq = jnp.exp(log_gamma[:, None] * (idx + 1.0))[..., None]
    dk = jnp.exp(log_gamma[:, None] * (CHUNK - 1.0 - idx))[..., None]
    dc = jnp.broadcast_to(jnp.exp(log_gamma * CHUNK)[:, None, None], (HEADS, 1, 128))
    return dmat, dq, dk, dc


def _rope_tables(tp):
    half = RET_DK // 2
    inv = 1.0 / (ROPE_BASE ** jnp.linspace(0.0, 1.0, half, dtype=F32))
    pos = (jnp.arange(tp) - (FRONT - N_META)).astype(F32)
    ang = pos[:, None] * inv[None, :]
    return jnp.cos(ang), jnp.sin(ang)


_RET_V0, _RET_G0 = 2 * D, 4 * D


def _ret_pieces(u_ref, hd):
    f = lambda a, n: u_ref[:, a:a + n].astype(F32)
    hk = RET_DK // 2
    return (f(RET_DK * hd, hk), f(RET_DK * hd + hk, hk), f(D + RET_DK * hd, hk), f(D + RET_DK * hd + hk, hk),
            f(_RET_V0 + RET_DV * hd, RET_DV), f(_RET_G0 + RET_DV * hd, RET_DV))


def _ret_const_specs(rev=None):
    c = (lambda n: (rev(n), 0)) if rev else (lambda n: (n, 0))
    z3 = lambda n: (0, 0, 0)
    return [pl.BlockSpec((CHUNK, RET_DK // 2), c), pl.BlockSpec((CHUNK, RET_DK // 2), c),
            pl.BlockSpec((HEADS, CHUNK, CHUNK), z3), pl.BlockSpec((HEADS, CHUNK, 1), z3),
            pl.BlockSpec((HEADS, CHUNK, 1), z3), pl.BlockSpec((HEADS, 1, 128), z3)]


def _ret_fwd(u, gain, name):
    tp = u.shape[0]
    nch = tp // CHUNK
    cos, sin = _rope_tables(tp)
    dmat, dq, dk, dc = _ret_consts()

    def body(u_ref, gain_ref, cos_ref, sin_ref, dmat_ref, dq_ref, dk_ref, dc_ref, on_ref, st_ref, state_ref):
        @pl.when(pl.program_id(0) == 0)
        def _():
            state_ref[...] = jnp.zeros_like(state_ref)

        cos_v, sin_v = cos_ref[...], sin_ref[...]
        for hd in range(HEADS):
            state = state_ref[hd]
            st_ref[hd] = state.astype(BF16)
            on, new_state = _ret_head(*_ret_pieces(u_ref, hd), state,
                                      gain_ref[:, RET_DV * hd:RET_DV * (hd + 1)], cos_v, sin_v,
                                      dmat_ref[hd], dq_ref[hd], dk_ref[hd], dc_ref[hd][:, :1])
            state_ref[hd] = new_state
            on_ref[:, RET_DV * hd:RET_DV * (hd + 1)] = on.astype(BF16)

    return pl.pallas_call(
        body, name=name, grid=(nch,),
        in_specs=[pl.BlockSpec((CHUNK, 6 * D), lambda n: (n, 0)), pl.BlockSpec((1, HEADS * RET_DV), lambda n: (0, 0))]
                 + _ret_const_specs(),
        out_specs=[pl.BlockSpec((CHUNK, HEADS * RET_DV), lambda n: (n, 0)),
                   pl.BlockSpec((None, HEADS, RET_DK, RET_DV), lambda n: (n, 0, 0, 0))],
        out_shape=[jax.ShapeDtypeStruct((tp, HEADS * RET_DV), BF16),
                   jax.ShapeDtypeStruct((nch, HEADS, RET_DK, RET_DV), BF16)],
        scratch_shapes=[pltpu.VMEM((HEADS, RET_DK, RET_DV), F32)],
        compiler_params=_cp(1))(u, gain, cos, sin, dmat, dq, dk, dc)


def _ret_bwd(u, gain, states, d_on, name):
    tp = u.shape[0]
    nch = tp // CHUNK
    cos, sin = _rope_tables(tp)
    dmat, dq, dk, dc = _ret_consts()
    rev = lambda n: nch - 1 - n
    hk = RET_DK // 2

    def body(u_ref, gain_ref, st_ref, don_ref, cos_ref, sin_ref, dmat_ref, dq_ref, dk_ref, dc_ref,
             du_ref, dgain_ref, dstate_ref):
        @pl.when(pl.program_id(0) == 0)
        def _():
            dstate_ref[...] = jnp.zeros_like(dstate_ref)
            dgain_ref[...] = jnp.zeros_like(dgain_ref)

        cos_v, sin_v = cos_ref[...], sin_ref[...]
        mask = _row_mask(rev(pl.program_id(0)))
        for hd in range(HEADS):
            consts = (cos_v, sin_v, dmat_ref[hd], dq_ref[hd], dk_ref[hd], dc_ref[hd][:, :1])
            cols = slice(RET_DV * hd, RET_DV * (hd + 1))
            _, vjp = jax.vjp(lambda *a: _ret_head(*a, *consts), *_ret_pieces(u_ref, hd),
                             st_ref[hd].astype(F32), gain_ref[:, cols])
            dq1, dq2, dk1, dk2, dv, dg, dstate, dgain = vjp((don_ref[:, cols].astype(F32), dstate_ref[hd]))
            dstate_ref[hd] = dstate
            dgain_ref[:, cols] += dgain
            put = lambda a, t: du_ref.__setitem__((slice(None), slice(a, a + t.shape[1])), (t * mask).astype(BF16))
            put(RET_DK * hd, dq1)
            put(RET_DK * hd + hk, dq2)
            put(D + RET_DK * hd, dk1)
            put(D + RET_DK * hd + hk, dk2)
            put(_RET_V0 + RET_DV * hd, dv)
            put(_RET_G0 + RET_DV * hd, dg)

    return pl.pallas_call(
        body, name=name, grid=(nch,),
        in_specs=[pl.BlockSpec((CHUNK, 6 * D), lambda n: (rev(n), 0)),
                  pl.BlockSpec((1, HEADS * RET_DV), lambda n: (0, 0)),
                  pl.BlockSpec((None, HEADS, RET_DK, RET_DV), lambda n: (rev(n), 0, 0, 0)),
                  pl.BlockSpec((CHUNK, HEADS * RET_DV), lambda n: (rev(n), 0))] + _ret_const_specs(rev),
        out_specs=[pl.BlockSpec((CHUNK, 6 * D), lambda n: (rev(n), 0)),
                   pl.BlockSpec((1, HEADS * RET_DV), lambda n: (0, 0))],
        out_shape=[jax.ShapeDtypeStruct((tp, 6 * D), BF16), jax.ShapeDtypeStruct((1, HEADS * RET_DV), F32)],
        scratch_shapes=[pltpu.VMEM((HEADS, RET_DK, RET_DV), F32)],
        compiler_params=_cp(1))(u, gain, states, d_on, cos, sin, dmat, dq, dk, dc)


_GLA_K0, _GLA_V0, _GLA_G0, _GLA_Z0 = 512, 1024, 2048, 3072


def _gla_head(q, k, v, g, z, state_t, wg, bg, gain, mask, lo, lo_t, loc, loc_t):
    ga = _nn(z, wg) + bg
    log_a = (jnp.minimum(ga, 0.0) - jnp.log(1.0 + jnp.exp(-jnp.abs(ga)))) * (mask * (1.0 / GLA_TAU))
    bcum = _cum(lo, lo_t, log_a)
    bmid = _cum(loc, loc_t, log_a)
    btot = jnp.sum(log_a, axis=0, keepdims=True)
    qs = q * (GLA_DK ** -0.5)
    causal = lax.broadcasted_iota(jnp.int32, (CHUNK, CHUNK), 0) >= lax.broadcasted_iota(jnp.int32, (CHUNK, CHUNK), 1)
    scores = jnp.where(causal, _nt(qs * jnp.exp(bmid), k * jnp.exp(-bmid)), 0.0)
    o = _nn(scores, v) + _nt(qs * jnp.exp(bcum), state_t)
    new_state_t = state_t * jnp.exp(btot) + _tn(v, k * jnp.exp(btot - bcum))
    return _gated_headnorm(o, g, gain), new_state_t


def _cum_mats():
    r = lax.broadcasted_iota(jnp.int32, (CHUNK, CHUNK), 0)
    c = lax.broadcasted_iota(jnp.int32, (CHUNK, CHUNK), 1)
    mid = CHUNK // 2
    low = lambda a, b: (a >= b).astype(F32)
    lo, lo_t = low(r, c), low(c, r)
    loc = lo - (c <= mid).astype(F32)
    loc_t = lo_t - (r <= mid).astype(F32)
    return tuple(m.astype(BF16) for m in (lo, lo_t, loc, loc_t))


def _gla_pieces(u_ref, hd):
    f = lambda a, n: u_ref[:, a:a + n].astype(F32)
    return (f(GLA_DK * hd, GLA_DK), f(_GLA_K0 + GLA_DK * hd, GLA_DK), f(_GLA_V0 + GLA_DV * hd, GLA_DV),
            f(_GLA_G0 + GLA_DV * hd, GLA_DV), f(_GLA_Z0, 128))


def _gla_fwd(u, wg, bg, gain, name):
    tp = u.shape[0]
    nch = tp // CHUNK

    def body(u_ref, wg_ref, bg_ref, gain_ref, on_ref, st_ref, state_ref):
        @pl.when(pl.program_id(0) == 0)
        def _():
            state_ref[...] = jnp.zeros_like(state_ref)

        mask = _row_mask(pl.program_id(0))
        mats = _cum_mats()
        for hd in range(HEADS):
            state = state_ref[hd]
            st_ref[hd] = state.astype(BF16)
            kc = slice(GLA_DK * hd, GLA_DK * (hd + 1))
            vc = slice(GLA_DV * hd, GLA_DV * (hd + 1))
            on, new_state = _gla_head(*_gla_pieces(u_ref, hd), state,
                                      wg_ref[:, kc], bg_ref[:, kc], gain_ref[:, vc], mask, *mats)
            state_ref[hd] = new_state
            on_ref[:, vc] = on.astype(BF16)

    return pl.pallas_call(
        body, name=name, grid=(nch,),
        in_specs=[pl.BlockSpec((CHUNK, GLA_U), lambda n: (n, 0)), pl.BlockSpec((128, HEADS * GLA_DK), lambda n: (0, 0)),
                  pl.BlockSpec((1, HEADS * GLA_DK), lambda n: (0, 0)), pl.BlockSpec((1, HEADS * GLA_DV), lambda n: (0, 0))],
        out_specs=[pl.BlockSpec((CHUNK, HEADS * GLA_DV), lambda n: (n, 0)),
                   pl.BlockSpec((None, HEADS, GLA_DV, GLA_DK), lambda n: (n, 0, 0, 0))],
        out_shape=[jax.ShapeDtypeStruct((tp, HEADS * GLA_DV), BF16),
                   jax.ShapeDtypeStruct((nch, HEADS, GLA_DV, GLA_DK), BF16)],
        scratch_shapes=[pltpu.VMEM((HEADS, GLA_DV, GLA_DK), F32)],
        compiler_params=_cp(1))(u, wg, bg, gain)


def _gla_bwd(u, wg, bg, gain, states, d_on, name):
    tp = u.shape[0]
    nch = tp // CHUNK
    rev = lambda n: nch - 1 - n

    def body(u_ref, wg_ref, bg_ref, gain_ref, st_ref, don_ref, du_ref, dwg_ref, dbg_ref, dgain_ref, dstate_ref):
        @pl.when(pl.program_id(0) == 0)
        def _():
            dstate_ref[...] = jnp.zeros_like(dstate_ref)
            dwg_ref[...] = jnp.zeros_like(dwg_ref)
            dbg_ref[...] = jnp.zeros_like(dbg_ref)
            dgain_ref[...] = jnp.zeros_like(dgain_ref)

        mask = _row_mask(rev(pl.program_id(0)))
        mats = _cum_mats()
        dz_sum = jnp.zeros((CHUNK, 128), F32)
        for hd in range(HEADS):
            kc = slice(GLA_DK * hd, GLA_DK * (hd + 1))
            vc = slice(GLA_DV * hd, GLA_DV * (hd + 1))
            _, vjp = jax.vjp(lambda *a: _gla_head(*a, mask, *mats), *_gla_pieces(u_ref, hd),
                             st_ref[hd].astype(F32), wg_ref[:, kc].astype(F32), bg_ref[:, kc], gain_ref[:, vc])
            dq, dk, dv, dg, dz, dstate, dwg, dbg, dgain = vjp((don_ref[:, vc].astype(F32), dstate_ref[hd]))
            dstate_ref[hd] = dstate
            dwg_ref[:, kc] += dwg
            dbg_ref[:, kc] += dbg
            dgain_ref[:, vc] += dgain
            dz_sum = dz_sum + dz
            put = lambda a, t: du_ref.__setitem__((slice(None), slice(a, a + t.shape[1])), (t * mask).astype(BF16))
            put(GLA_DK * hd, dq)
            put(_GLA_K0 + GLA_DK * hd, dk)
            put(_GLA_V0 + GLA_DV * hd, dv)
            put(_GLA_G0 + GLA_DV * hd, dg)
        du_ref[:, _GLA_Z0:] = dz_sum.astype(BF16)

    full = lambda r, c: pl.BlockSpec((r, c), lambda n: (0, 0))
    return pl.pallas_call(
        body, name=name, grid=(nch,),
        in_specs=[pl.BlockSpec((CHUNK, GLA_U), lambda n: (rev(n), 0)), full(128, HEADS * GLA_DK),
                  full(1, HEADS * GLA_DK), full(1, HEADS * GLA_DV),
                  pl.BlockSpec((None, HEADS, GLA_DV, GLA_DK), lambda n: (rev(n), 0, 0, 0)),
                  pl.BlockSpec((CHUNK, HEADS * GLA_DV), lambda n: (rev(n), 0))],
        out_specs=[pl.BlockSpec((CHUNK, GLA_U), lambda n: (rev(n), 0)), full(128, HEADS * GLA_DK),
                   full(1, HEADS * GLA_DK), full(1, HEADS * GLA_DV)],
        out_shape=[jax.ShapeDtypeStruct((tp, GLA_U), BF16), jax.ShapeDtypeStruct((128, HEADS * GLA_DK), F32),
                   jax.ShapeDtypeStruct((1, HEADS * GLA_DK), F32), jax.ShapeDtypeStruct((1, HEADS * GLA_DV), F32)],
        scratch_shapes=[pltpu.VMEM((HEADS, GLA_DV, GLA_DK), F32)],
        compiler_params=_cp(1))(u, wg, bg, gain, states, d_on)


def _ffn_fwd(h, gain, w_in, w_out, tag):
    hn, ug, uu, act = _norm_ffn_in(h, gain, w_in, f"{tag}_in")
    if callable(w_out):
        w_out = w_out(act)
    return _out_proj(act, w_out, h, 0.5, f"{tag}_out"), (h, hn, ug, uu, act), w_out


def _ffn_bwd(dh, saved, gain, w_in, w_out, tag, push):
    h, hn, ug, uu, act = saved
    du = _ffn_dact(dh, w_out, ug, uu, f"{tag}_dact")
    d_w_out = _wgrad(act, dh, bm=D_FF // 2, bn=D, scale=0.5, sharded=False, name=f"{tag}_dwout")
    d_w_in = _wgrad(hn, du, bm=D, bn=w_in.shape[2], scale=1.0, sharded=True, name=f"{tag}_dwin")
    token = push([d_w_in, d_w_out])
    return _dgrad_norm(du, w_in, h, gain + token[0, 0], dh, f"{tag}_dnorm")


def _sequence_grads(x, target, p, weights, grads):
    row = lambda v, token: v.reshape(1, -1) + token[0, 0]
    gains = {}

    tok = weights.start(1, weights.start(0, None))
    w = weights.wait(0, tok)
    tok = weights.start(2, w["l0_ffn1_in"])
    h = jnp.concatenate([jnp.zeros((FRONT - N_META, D), F32), w["meta"], x], axis=0)
    gains["l0_ffn1"] = row(p["norm_ffn1"][0], tok)
    h, s1, w["l0_ffn1_out"] = _ffn_fwd(h, gains["l0_ffn1"], w["l0_ffn1_in"],
                                       lambda act: weights.wait(1, act)["l0_ffn1_out"], "l0_ffn1")
    w.update(weights.wait(2, h))
    tok = weights.start(3, w["ret_in"])
    gains["ret"] = row(p["norm_mix"][0], tok)
    hn, u = _norm_proj(h, gains["ret"], w["ret_in"], "ret_in")
    on, states = _ret_fwd(u, w["ret_gain"], "ret_fwd")
    h_mix = _out_proj(on, w["ret_out"], h, 1.0, "ret_out")
    s2 = (h, hn, u, on, states)
    w.update(weights.wait(3, h_mix))
    tok = weights.start(4, w["l0_ffn2_in"])
    gains["l0_ffn2"] = row(p["norm_ffn2"][0], tok)
    h, s3, _ = _ffn_fwd(h_mix, gains["l0_ffn2"], w["l0_ffn2_in"], w["l0_ffn2_out"], "l0_ffn2")
    saved = [(s1, s2, s3)]

    w.update(weights.wait(4, h))
    tok = weights.start(5, w["l1_ffn1_in"])
    gains["l1_ffn1"] = row(p["norm_ffn1"][1], tok)
    h, s1, _ = _ffn_fwd(h, gains["l1_ffn1"], w["l1_ffn1_in"], w["l1_ffn1_out"], "l1_ffn1")
    w.update(weights.wait(5, h))
    tok = weights.start(6, w["gla_out"])
    gains["gla"] = row(p["norm_mix"][1], tok)
    hn, u = _norm_proj(h, gains["gla"], w["gla_in"], "gla_in")
    on, states = _gla_fwd(u, w["gla_wg"], w["gla_bg"], w["gla_gain"], "gla_fwd")
    h_mix = _out_proj(on, w["gla_out"], h, 1.0, "gla_out")
    s2 = (h, hn, u, on, states)
    w.update(weights.wait(6, h_mix))
    gains["l1_ffn2"] = p["norm_ffn2"][1].reshape(1, -1)
    h, s3, _ = _ffn_fwd(h_mix, gains["l1_ffn2"], w["l1_ffn2_in"], w["l1_ffn2_out"], "l1_ffn2")
    saved.append((s1, s2, s3))

    dh, d_final, loss = _loss_head(h, p["final_norm"].reshape(1, -1), target, "loss_head")
    small = {"final_norm": d_final, "norm_ffn1": [None, None], "norm_mix": [None, None], "norm_ffn2": [None, None]}
    pusher = lambda k: functools.partial(grads.push, k)

    s1, s2, s3 = saved[1]
    dh, small["norm_ffn2"][1] = _ffn_bwd(dh, s3, gains["l1_ffn2"], w["l1_ffn2_in"], w["l1_ffn2_out"], "l1_ffn2",
                                         pusher(0))
    h_in, hn, u, on, states = s2
    d_on = _dgrad(dh, w["gla_out"], "gla_don")
    d_out = _wgrad(on, dh, bm=D, bn=D, scale=1.0, sharded=False, name="gla_dwout")
    du, small["gla_wg"], small["gla_bg"], small["gla_gain"] = _gla_bwd(u, w["gla_wg"], w["gla_bg"], w["gla_gain"],
                                                                       states, d_on, "gla_bwd")
    d_in = _wgrad(hn, du, bm=D, bn=GLA_U // 5, scale=1.0, sharded=False, name="gla_dwin")
    d_in = jnp.moveaxis(d_in[:, :GLA_IN].reshape(D, N_CHIPS, -1), 1, 0)
    tok = grads.push(1, [d_in, d_out])
    dh, small["norm_mix"][1] = _dgrad_norm(du, w["gla_in"], h_in, gains["gla"] + tok[0, 0], dh, "gla_dnorm")
    dh, small["norm_ffn1"][1] = _ffn_bwd(dh, s1, gains["l1_ffn1"], w["l1_ffn1_in"], w["l1_ffn1_out"], "l1_ffn1",
                                         pusher(2))

    s1, s2, s3 = saved[0]
    dh, small["norm_ffn2"][0] = _ffn_bwd(dh, s3, gains["l0_ffn2"], w["l0_ffn2_in"], w["l0_ffn2_out"], "l0_ffn2",
                                         pusher(3))
    h_in, hn, u, on, states = s2
    d_on = _dgrad(dh, w["ret_out"], "ret_don")
    d_out = _wgrad(on, dh, bm=D, bn=D, scale=1.0, sharded=False, name="ret_dwout")
    du, small["ret_gain"] = _ret_bwd(u, w["ret_gain"], states, d_on, "ret_bwd")
    d_in = _wgrad(hn, du, bm=D, bn=w["ret_in"].shape[2], scale=1.0, sharded=True, name="ret_dwin")
    tok = grads.push(4, [d_in, d_out])
    dh, small["norm_mix"][0] = _dgrad_norm(du, w["ret_in"], h_in, gains["ret"] + tok[0, 0], dh, "ret_dnorm")
    dh, small["norm_ffn1"][0] = _ffn_bwd(dh, s1, gains["l0_ffn1"], w["l0_ffn1_in"], w["l0_ffn1_out"], "l0_ffn1",
                                         pusher(5))
    grads.push(6, [], [dh[FRONT - N_META:FRONT], *small["norm_ffn1"], *small["norm_mix"], *small["norm_ffn2"],
                       small["final_norm"], small["ret_gain"], small["gla_wg"][:GLA_RANK], small["gla_bg"],
                       small["gla_gain"], loss[:, :1]])
    return dh[FRONT:]


_HBM = pl.BlockSpec(memory_space=pl.ANY)


def _place():
    return lax.axis_index("x"), lax.axis_index("y"), lax.axis_index("c")


def _flip(v, bit):
    return 1 - v if bit else v


DMA_CHUNK_BYTES = 128 * 1024


def _row_chunks(ref):
    rows, cols = ref.shape
    step = _row_tile(rows, max(16, DMA_CHUNK_BYTES // (cols * ref.dtype.itemsize)))
    return [pl.ds(a, step) for a in range(0, rows, step)]


def _whole(src, dst, send_sem, recv_sem, peer):
    return pltpu.make_async_remote_copy(src_ref=src, dst_ref=dst, send_sem=send_sem, recv_sem=recv_sem,
                                        device_id=peer, device_id_type=MESH)


def _send(src, dst, send_sem, recv_sem, peer):
    for rows in _row_chunks(src):
        _whole(src.at[rows], dst.at[rows], send_sem, recv_sem, peer).start()
    return _whole(src, dst, send_sem, recv_sem, peer)


_HBM_ONLY = pl.BlockSpec(memory_space=pltpu.HBM)
_SEMS = pl.BlockSpec(memory_space=pltpu.SEMAPHORE)
_SIDE_EFFECT = pltpu.CompilerParams(has_side_effects=pltpu.SideEffectType.DATAFLOW_SIDE_EFFECTING)
_CHIP_FLIPS = [(1, 0, 0), (0, 1, 0), (1, 1, 0)]
_PEER_FLIPS = [(fx, fy, fc) for fx in (0, 1) for fy in (0, 1) for fc in (0, 1)][1:]


def _zero_token():
    return jnp.zeros((8, 128), F32)


def _exchange_start(srcs, lands, route, flips, after, name):
    n = len(srcs)

    def body(*refs):
        src, land = refs[:n], refs[n:2 * n]
        send_sems, recv_sems, token = refs[2 * n + 1], refs[2 * n + 2], refs[-1]
        me = _place()
        for t in range(n):
            for j, flip in enumerate(flips):
                peer = tuple(_flip(v, f) for v, f in zip(me, flip))
                s, d = route(t, src[t], land[t], me, peer)
                _send(s, d, send_sems.at[t * len(flips) + j], recv_sems.at[t * len(flips) + j], peer)
        token[...] = jnp.zeros_like(token)

    hbm = lambda a: pltpu.HBM(a.shape, a.dtype)
    sems = pltpu.SemaphoreType.DMA((n * len(flips),))
    operands = [pltpu.with_memory_space_constraint(a, pltpu.HBM) for a in list(srcs) + list(lands)]
    out = pl.pallas_call(
        body, name=name, in_specs=[_HBM_ONLY] * (2 * n) + [_HBM],
        out_shape=(sems, sems, *[hbm(a) for a in operands], jax.ShapeDtypeStruct((8, 128), F32)),
        out_specs=(_SEMS, _SEMS, *[_HBM_ONLY] * (2 * n), pl.BlockSpec(memory_space=pltpu.VMEM)),
        input_output_aliases={i: 2 + i for i in range(2 * n)}, compiler_params=_SIDE_EFFECT,
    )(*operands, _zero_token() if after is None else after)
    return (out[0], out[1], out[2:2 + n], out[2 + n:2 + 2 * n]), out[-1]


def _exchange_wait(started, route, flips, after, name):
    send_sems, recv_sems, srcs, lands = started
    n = len(srcs)

    def body(*refs):
        src, land = refs[:n], refs[n:2 * n]
        send_sems, recv_sems = refs[2 * n], refs[2 * n + 1]
        me = _place()
        for t in range(n):
            for j, flip in enumerate(flips):
                peer = tuple(_flip(v, f) for v, f in zip(me, flip))
                s, d = route(t, src[t], land[t], me, peer)
                cp = _whole(s, d, send_sems.at[t * len(flips) + j], recv_sems.at[t * len(flips) + j], peer)
                cp.wait_send()
                cp.wait_recv()

    hbm = lambda a: pltpu.HBM(a.shape, a.dtype)
    out = pl.pallas_call(
        body, name=name, in_specs=[_HBM_ONLY] * (2 * n) + [_SEMS, _SEMS, _HBM],
        out_shape=tuple(hbm(a) for a in list(srcs) + list(lands)), out_specs=tuple([_HBM_ONLY] * (2 * n)),
        input_output_aliases={i: i for i in range(2 * n)}, compiler_params=_SIDE_EFFECT,
    )(*srcs, *lands, send_sems, recv_sems, after)
    return out[:n], out[n:]


def _gather_route(t, src, land, me, peer):
    return src, land.at[2 * me[0] + me[1]]


def _scatter_route(n_pieces):
    def route(t, src, land, me, peer):
        part = src.at[2 * peer[0] + peer[1], peer[2]] if t < n_pieces else src
        return part, land.at[4 * me[0] + 2 * me[1] + me[2]]

    return route


def _swap_cores(halves):
    n = len(halves)

    def body(*refs):
        src, dst = refs[:n], refs[n:2 * n]
        send_sems, recv_sems = refs[2 * n:]
        x, y, c = _place()
        copies = [_send(src[t], dst[t], send_sems.at[t], recv_sems.at[t], (x, y, 1 - c)) for t in range(n)]
        for cp in copies:
            cp.wait()

    got = pl.pallas_call(
        body, name="swap_cores", in_specs=[_HBM] * n, out_specs=[_HBM] * n,
        out_shape=[jax.ShapeDtypeStruct(a.shape, a.dtype) for a in halves],
        scratch_shapes=[pltpu.SemaphoreType.DMA((n,)), pltpu.SemaphoreType.DMA((n,))],
    )(*halves)
    south = lax.axis_index("c") == 0
    return [jnp.stack([jnp.where(south, a, b), jnp.where(south, b, a)]) for a, b in zip(halves, got)]


def _row_tile(rows, cap):
    fits = [t for t in range(16, cap + 1, 16) if rows % t == 0]
    return fits[-1] if fits else rows


def _sum_slots(a, name):
    _, r, c = a.shape
    tr = _row_tile(r, 384)

    def body(a_ref, o_ref):
        s = a_ref[0].astype(F32)
        for k in range(1, N_DEV):
            s = s + a_ref[k].astype(F32)
        o_ref[...] = s

    return pl.pallas_call(
        body, name=name, grid=(r // tr,),
        in_specs=[pl.BlockSpec((N_DEV, tr, c), lambda i: (0, i, 0))],
        out_specs=pl.BlockSpec((tr, c), lambda i: (i, 0)),
        out_shape=jax.ShapeDtypeStruct((r, c), F32),
        compiler_params=_cp(1))(a)


def _adamw(w, g, m, v, name):
    r, c = w.shape
    tr = _row_tile(r, 256)

    def body(w_ref, g_ref, m_ref, v_ref, d_ref, nm_ref, nv_ref):
        gv = g_ref[...]
        nm = ADAM_B1 * m_ref[...] + (1.0 - ADAM_B1) * gv
        nv = ADAM_B2 * v_ref[...] + (1.0 - ADAM_B2) * (gv * gv)
        m_hat = nm / (1.0 - ADAM_B1 ** ADAM_STEP)
        v_hat = nv / (1.0 - ADAM_B2 ** ADAM_STEP)
        d_ref[...] = -ADAM_LR * (m_hat / (jnp.sqrt(v_hat) + ADAM_EPS) + ADAM_WD * w_ref[...])
        nm_ref[...] = nm
        nv_ref[...] = nv

    spec = pl.BlockSpec((tr, c), lambda i: (i, 0))
    return pl.pallas_call(
        body, name=name, grid=(r // tr,), in_specs=[spec] * 4, out_specs=[spec] * 3,
        out_shape=[jax.ShapeDtypeStruct((r, c), F32)] * 3,
        compiler_params=_cp(1))(w, g, m, v)


_SMALL = ["meta_tokens", "ret_head_norm", "gla_w_gate", "gla_b_gate", "gla_head_norm"]
_LOCAL_SMALL = ["meta_tokens", "norm_ffn1", "norm_mix", "norm_ffn2", "ret_head_norm", "gla_w_gate", "gla_b_gate",
                "gla_head_norm", "final_norm"]
_BIG = ["ffn1_w_in", "ffn1_w_out", "ffn2_w_in", "ffn2_w_out", "ret_w_in", "ret_w_out", "gla_w_in", "gla_w_out"]
_WEIGHTS = ["meta_tokens", "norm_ffn1", "ffn1_w_in", "ffn1_w_out", "norm_mix", "norm_ffn2", "ffn2_w_in", "ffn2_w_out",
            "ret_w_in", "ret_head_norm", "ret_w_out", "gla_w_in", "gla_w_gate", "gla_b_gate", "gla_head_norm",
            "gla_w_out", "final_norm"]


def _pack_rows(arrays, width):
    flat = jnp.concatenate([a.reshape(-1) for a in arrays])
    pad = -flat.shape[0] % (8 * width)
    return jnp.pad(flat, (0, pad)).reshape(-1, width)


def _unpack_rows(packed, shapes):
    flat, out, at = packed.reshape(-1), [], 0
    for s in shapes:
        size = 1
        for dim in s:
            size *= dim
        out.append(flat[at:at + size].reshape(s))
        at += size
    return out


class _WeightGather:
    GROUPS = [("small", "l0_ffn1_in"), ("l0_ffn1_out",), ("ret_in", "ret_out"), ("l0_ffn2_in", "l0_ffn2_out"),
              ("l1_ffn1_in", "l1_ffn1_out"), ("gla_in", "gla_out"), ("l1_ffn2_in", "l1_ffn2_out")]

    def __init__(self, p):
        b = lambda a: a.astype(BF16)
        self.small_shapes = [p[name].shape for name in _SMALL]
        self.shards = {"small": _pack_rows([p[name] for name in _SMALL], 128), "ret_in": b(p["ret_w_in"][0]),
                       "ret_out": b(p["ret_w_out"][0]), "gla_in": b(p["gla_w_in"][0]), "gla_out": b(p["gla_w_out"][0])}
        for layer in range(2):
            for name in ("ffn1", "ffn2"):
                self.shards[f"l{layer}_{name}_in"] = b(p[f"{name}_w_in"][layer])
                self.shards[f"l{layer}_{name}_out"] = b(p[f"{name}_w_out"][layer])
        self.started = {}

    def start(self, k, after):
        shards = [self.shards[name] for name in self.GROUPS[k]]
        lands = [lax.empty((N_CHIPS,) + s.shape, s.dtype) for s in shards]
        self.started[k], token = _exchange_start(shards, lands, _gather_route, _CHIP_FLIPS, after, f"gather{k}_start")
        return token

    def wait(self, k, after):
        shards, got = _exchange_wait(self.started[k], _gather_route, _CHIP_FLIPS, after, f"gather{k}_wait")
        mine = 2 * lax.axis_index("x") + lax.axis_index("y")
        w = {}
        for name, g, s in zip(self.GROUPS[k], got, shards):
            g = lax.dynamic_update_index_in_dim(g, s, mine, 0)
            if name == "small":
                parts = zip(*[_unpack_rows(g[chip], self.small_shapes) for chip in range(N_CHIPS)])
                cat = lambda a: jnp.moveaxis(a, 0, -2).reshape(a.shape[1:-1] + (-1,))
                meta, ret_gain, wg, bg, gla_gain = [cat(jnp.stack(part)) for part in parts]
                w.update(meta=meta, ret_gain=ret_gain.reshape(1, -1), gla_bg=bg.reshape(1, -1),
                         gla_gain=gla_gain.reshape(1, -1),
                         gla_wg=jnp.pad(wg[0], ((0, 128 - GLA_RANK), (0, 0))).astype(BF16))
            elif name == "gla_in":
                full = jnp.moveaxis(g, 0, 1).reshape(D, -1)
                w[name] = jnp.pad(full, ((0, 0), (0, GLA_U - GLA_IN)))[None]
            elif name.endswith("_out"):
                w[name] = g.reshape(-1, g.shape[-1])
            else:
                w[name] = g
        return w


class _GradExchange:
    def __init__(self):
        self.started = []
        self.token = None
        self.small_shapes = None

    def push(self, k, arrays, small=None):
        srcs = [a.reshape(N_CHIPS, 2, -1, a.shape[-1]) for a in arrays]
        lands = [lax.empty((N_DEV,) + a.shape[2:], a.dtype) for a in srcs]
        if small is not None:
            self.small_shapes = [a.shape for a in small]
            srcs.append(_pack_rows(small, D))
            lands.append(lax.empty((N_DEV,) + srcs[-1].shape, F32))
        started, self.token = _exchange_start(srcs, lands, _scatter_route(len(arrays)), _PEER_FLIPS, None,
                                              f"scatter{k}_start")
        self.started.append((started, len(arrays)))
        return self.token

    def collect(self):
        x, y, c = _place()
        after, sums = self.token, []
        for k, (started, n_pieces) in enumerate(self.started):
            srcs, got = _exchange_wait(started, _scatter_route(n_pieces), _PEER_FLIPS, after, f"scatter{k}_wait")
            own = [a[2 * x + y, c] for a in srcs[:n_pieces]] + list(srcs[n_pieces:])
            got = [lax.dynamic_update_index_in_dim(g, a, 4 * x + 2 * y + c, 0) for g, a in zip(got, own)]
            sums.append([_sum_slots(a, f"sum{k}_{i}") for i, a in enumerate(got)])
            after = sums[-1][0]
        small = _unpack_rows(sums[-1].pop(), self.small_shapes)
        return sums, small


def kernel(x, meta_tokens, norm_ffn1, ffn1_w_in, ffn1_w_out, norm_mix, norm_ffn2, ffn2_w_in, ffn2_w_out, ret_w_in, ret_head_norm, ret_w_out, gla_w_in, gla_w_gate, gla_b_gate, gla_head_norm, gla_w_out, final_norm, loss_target, m_meta_tokens, m_norm_ffn1, m_ffn1_w_in, m_ffn1_w_out, m_norm_mix, m_norm_ffn2, m_ffn2_w_in, m_ffn2_w_out, m_ret_w_in, m_ret_head_norm, m_ret_w_out, m_gla_w_in, m_gla_w_gate, m_gla_b_gate, m_gla_head_norm, m_gla_w_out, m_final_norm, v_meta_tokens, v_norm_ffn1, v_ffn1_w_in, v_ffn1_w_out, v_norm_mix, v_norm_ffn2, v_ffn2_w_in, v_ffn2_w_out, v_ret_w_in, v_ret_head_norm, v_ret_w_out, v_gla_w_in, v_gla_w_gate, v_gla_b_gate, v_gla_head_norm, v_gla_w_out, v_final_norm):
    p = dict(meta_tokens=meta_tokens, norm_ffn1=norm_ffn1, ffn1_w_in=ffn1_w_in, ffn1_w_out=ffn1_w_out, norm_mix=norm_mix,
             norm_ffn2=norm_ffn2, ffn2_w_in=ffn2_w_in, ffn2_w_out=ffn2_w_out, ret_w_in=ret_w_in,
             ret_head_norm=ret_head_norm, ret_w_out=ret_w_out, gla_w_in=gla_w_in, gla_w_gate=gla_w_gate,
             gla_b_gate=gla_b_gate, gla_head_norm=gla_head_norm, gla_w_out=gla_w_out, final_norm=final_norm)
    m = dict(zip(_WEIGHTS, (m_meta_tokens, m_norm_ffn1, m_ffn1_w_in, m_ffn1_w_out, m_norm_mix, m_norm_ffn2, m_ffn2_w_in,
                            m_ffn2_w_out, m_ret_w_in, m_ret_head_norm, m_ret_w_out, m_gla_w_in, m_gla_w_gate,
                            m_gla_b_gate, m_gla_head_norm, m_gla_w_out, m_final_norm)))
    v = dict(zip(_WEIGHTS, (v_meta_tokens, v_norm_ffn1, v_ffn1_w_in, v_ffn1_w_out, v_norm_mix, v_norm_ffn2, v_ffn2_w_in,
                            v_ffn2_w_out, v_ret_w_in, v_ret_head_norm, v_ret_w_out, v_gla_w_in, v_gla_w_gate,
                            v_gla_b_gate, v_gla_head_norm, v_gla_w_out, v_final_norm)))

    exchange = _GradExchange()
    d_x = _sequence_grads(x[0], loss_target[0], p, _WeightGather(p), exchange)
    sums, small = exchange.collect()
    names = [("ffn2_in", 1), ("ffn2_out", 1), ("gla_in", 0), ("gla_out", 0), ("ffn1_in", 1), ("ffn1_out", 1),
             ("ffn2_in", 0), ("ffn2_out", 0), ("ret_in", 0), ("ret_out", 0), ("ffn1_in", 0), ("ffn1_out", 0)]
    swapped = _swap_cores([a for group in sums for a in group])
    shard = {key: a.reshape(-1, a.shape[-1]) for key, a in zip(names, swapped)}
    big = {name: [shard[name, layer] for layer in range(2) if (name, layer) in shard] for name, _ in names}

    chip = 2 * lax.axis_index("x") + lax.axis_index("y")
    cols = lambda a, n: lax.dynamic_slice_in_dim(a, chip * n, n, axis=a.ndim - 1)
    (s_meta, s_n1a, s_n1b, s_nma, s_nmb, s_n2a, s_n2b, s_final, s_ret_gain, s_wg, s_bg, s_gla_gain, s_loss) = small
    grads = {
        "meta_tokens": cols(s_meta, 256), "norm_ffn1": jnp.concatenate([s_n1a, s_n1b]),
        "norm_mix": jnp.concatenate([s_nma, s_nmb]), "norm_ffn2": jnp.concatenate([s_n2a, s_n2b]),
        "final_norm": s_final.reshape(D),
        "ret_head_norm": cols(s_ret_gain.reshape(1, HEADS, RET_DV), RET_DV // N_CHIPS),
        "gla_w_gate": cols(s_wg, GLA_DK)[None], "gla_b_gate": cols(s_bg, GLA_DK),
        "gla_head_norm": cols(s_gla_gain.reshape(1, HEADS, GLA_DV), GLA_DV // N_CHIPS),
        "ffn1_w_in": jnp.stack(big["ffn1_in"]), "ffn1_w_out": jnp.stack(big["ffn1_out"]),
        "ffn2_w_in": jnp.stack(big["ffn2_in"]), "ffn2_w_out": jnp.stack(big["ffn2_out"]),
        "ret_w_in": big["ret_in"][0][None], "ret_w_out": big["ret_out"][0][None],
        "gla_w_in": big["gla_in"][0][None], "gla_w_out": big["gla_out"][0][None],
    }

    delta, new_m, new_v = {}, {}, {}
    for name in _BIG:
        shape = p[name].shape
        flat = lambda a: a.reshape(-1, shape[-1])
        out = _adamw(flat(p[name]), flat(grads[name]), flat(m[name]), flat(v[name]), f"adamw_{name}")
        delta[name], new_m[name], new_v[name] = [a.reshape(shape) for a in out]
    packed = [_pack_rows([d[name] for name in _LOCAL_SMALL], 128) for d in (p, grads, m, v)]
    out = _adamw(*packed, "adamw_small")
    shapes = [p[name].shape for name in _LOCAL_SMALL]
    for d, a in zip((delta, new_m, new_v), out):
        d.update(zip(_LOCAL_SMALL, _unpack_rows(a, shapes)))

    return (s_loss.reshape(()), d_x[None], *[grads[n] for n in _WEIGHTS], *[delta[n] for n in _WEIGHTS],
            *[new_m[n] for n in _WEIGHTS], *[new_v[n] for n in _WEIGHTS])
```

```python
import functools

import jax
import jax.numpy as jnp
from jax import lax
from jax.experimental import pallas as pl
from jax.experimental.pallas import tpu as pltpu

F32, BF16 = jnp.float32, jnp.bfloat16
MESH = pl.DeviceIdType.MESH

D = 1024
N_META = 16
CHUNK = 64
FRONT = 256
D_FF = 2816
EPS = 1e-6
HEADS = 4
RET_DK, RET_DV = 256, 512
GLA_DK, GLA_DV = 128, 256
GLA_RANK = 16
GLA_TAU = 16.0
GLA_IN = 2 * HEADS * GLA_DK + 2 * HEADS * GLA_DV + GLA_RANK
GLA_U = 3200
ROPE_BASE = 10000.0
N_CHIPS = 4
N_DEV = 8

ADAM_LR, ADAM_B1, ADAM_B2, ADAM_EPS, ADAM_WD, ADAM_STEP = 0.001, 0.9, 0.999, 1e-08, 0.01, 10

VMEM_LIMIT_BYTES = 56 * 1024 * 1024
TM = 768
TM_SMALL = 256


def _cp(n_axes):
    return pltpu.CompilerParams(dimension_semantics=("arbitrary",) * n_axes, vmem_limit_bytes=VMEM_LIMIT_BYTES)


def _dg(a, b, ca, cb):
    nb = a.ndim - 2
    dims = (((ca + nb,), (cb + nb,)), (tuple(range(nb)), tuple(range(nb))))
    return lax.dot_general(a.astype(BF16), b.astype(BF16), dims, preferred_element_type=F32)


@jax.custom_vjp
def _nn(a, b):
    return _dg(a, b, 1, 0)


@jax.custom_vjp
def _nt(a, b):
    return _dg(a, b, 1, 1)


@jax.custom_vjp
def _tn(a, b):
    return _dg(a, b, 0, 0)


_nn.defvjp(lambda a, b: (_nn(a, b), (a, b)), lambda res, g: (_nt(g, res[1]), _tn(res[0], g)))
_nt.defvjp(lambda a, b: (_nt(a, b), (a, b)), lambda res, g: (_nn(g, res[1]), _tn(g, res[0])))
_tn.defvjp(lambda a, b: (_tn(a, b), (a, b)), lambda res, g: (_nt(res[1], g), _nn(res[0], g)))


def _split3_dot(m, a):
    a1 = a.astype(BF16)
    r1 = a - a1.astype(F32)
    a2 = r1.astype(BF16)
    a3 = (r1 - a2.astype(F32)).astype(BF16)
    mb = jnp.broadcast_to(m, a.shape[:-2] + m.shape)
    return _dg(mb, a1, 1, 0) + _dg(mb, a2, 1, 0) + _dg(mb, a3, 1, 0)


@jax.custom_vjp
def _cum(m, mt, a):
    return _split3_dot(m, a)


_cum.defvjp(lambda m, mt, a: (_split3_dot(m, a), (m, mt)),
            lambda res, g: (jnp.zeros_like(res[0]), jnp.zeros_like(res[1]), _split3_dot(res[1], g)))


def _sigmoid(x):
    return 1.0 / (1.0 + jnp.exp(-x))


def _rms(x):
    return lax.rsqrt(jnp.mean(x * x, axis=-1, keepdims=True) + EPS)


def _rmsnorm_bwd(dy, x, gain):
    r = _rms(x)
    xhat = x * r
    dxh = dy * gain
    return r * (dxh - xhat * jnp.mean(dxh * xhat, axis=-1, keepdims=True)), xhat


def _norm_proj(h, gain, w, name):
    tp, d = h.shape
    s, _, ns = w.shape

    def body(h_ref, g_ref, w_ref, hn_ref, u_ref):
        @pl.when(pl.program_id(1) == 0)
        def _():
            x = h_ref[...]
            hn_ref[...] = (x * _rms(x) * g_ref[...]).astype(BF16)

        u_ref[...] = jnp.dot(hn_ref[...], w_ref[...], preferred_element_type=F32).astype(BF16)

    return pl.pallas_call(
        body, name=name, grid=(tp // TM, s),
        in_specs=[pl.BlockSpec((TM, d), lambda i, j: (i, 0)), pl.BlockSpec((1, d), lambda i, j: (0, 0)),
                  pl.BlockSpec((None, d, ns), lambda i, j: (j, 0, 0))],
        out_specs=[pl.BlockSpec((TM, d), lambda i, j: (i, 0)), pl.BlockSpec((TM, ns), lambda i, j: (i, j))],
        out_shape=[jax.ShapeDtypeStruct((tp, d), BF16), jax.ShapeDtypeStruct((tp, s * ns), BF16)],
        compiler_params=_cp(2))(h, gain, w)


def _norm_ffn_in(h, gain, w, name):
    tp, d = h.shape
    s, _, ns = w.shape
    half = s // 2

    def body(h_ref, g_ref, wg_ref, wu_ref, hn_ref, ug_ref, uu_ref, act_ref):
        @pl.when(pl.program_id(1) == 0)
        def _():
            x = h_ref[...]
            hn_ref[...] = (x * _rms(x) * g_ref[...]).astype(BF16)

        a = hn_ref[...]
        g = jnp.dot(a, wg_ref[...], preferred_element_type=F32)
        u = jnp.dot(a, wu_ref[...], preferred_element_type=F32)
        ug_ref[...] = g.astype(BF16)
        uu_ref[...] = u.astype(BF16)
        act_ref[...] = (g * _sigmoid(g) * u).astype(BF16)

    wide = jax.ShapeDtypeStruct((tp, half * ns), BF16)
    return pl.pallas_call(
        body, name=name, grid=(tp // TM, half),
        in_specs=[pl.BlockSpec((TM, d), lambda i, j: (i, 0)), pl.BlockSpec((1, d), lambda i, j: (0, 0)),
                  pl.BlockSpec((None, d, ns), lambda i, j: (j, 0, 0)),
                  pl.BlockSpec((None, d, ns), lambda i, j: (j + half, 0, 0))],
        out_specs=[pl.BlockSpec((TM, d), lambda i, j: (i, 0))] + [pl.BlockSpec((TM, ns), lambda i, j: (i, j))] * 3,
        out_shape=[jax.ShapeDtypeStruct((tp, d), BF16), wide, wide, wide],
        compiler_params=_cp(2))(h, gain, w, w)


def _out_proj(a, w, h, scale, name):
    tp, k = a.shape
    d = w.shape[1]

    def body(a_ref, w_ref, h_ref, o_ref):
        o_ref[...] = h_ref[...] + scale * jnp.dot(a_ref[...], w_ref[...], preferred_element_type=F32)

    return pl.pallas_call(
        body, name=name, grid=(tp // TM,),
        in_specs=[pl.BlockSpec((TM, k), lambda i: (i, 0)), pl.BlockSpec((k, d), lambda i: (0, 0)),
                  pl.BlockSpec((TM, d), lambda i: (i, 0))],
        out_specs=pl.BlockSpec((TM, d), lambda i: (i, 0)),
        out_shape=jax.ShapeDtypeStruct((tp, d), F32),
        compiler_params=_cp(1))(a, w, h)


def _ffn_dact(dh, w_out, ug, uu, name):
    tp, d = dh.shape
    ff = w_out.shape[0]
    tm = TM_SMALL

    def body(dh_ref, w_ref, ug_ref, uu_ref, du_ref):
        dy = (0.5 * dh_ref[...]).astype(BF16)
        dact = lax.dot_general(dy, w_ref[...], (((1,), (1,)), ((), ())), preferred_element_type=F32)
        g = ug_ref[...].astype(F32)
        u = uu_ref[...].astype(F32)
        sg = _sigmoid(g)
        du_ref[:, :ff] = (dact * u * (sg * (1.0 + g * (1.0 - sg)))).astype(BF16)
        du_ref[:, ff:] = (dact * (g * sg)).astype(BF16)

    return pl.pallas_call(
        body, name=name, grid=(tp // tm,),
        in_specs=[pl.BlockSpec((tm, d), lambda i: (i, 0)), pl.BlockSpec((ff, d), lambda i: (0, 0)),
                  pl.BlockSpec((tm, ff), lambda i: (i, 0)), pl.BlockSpec((tm, ff), lambda i: (i, 0))],
        out_specs=pl.BlockSpec((tm, 2 * ff), lambda i: (i, 0)),
        out_shape=jax.ShapeDtypeStruct((tp, 2 * ff), BF16),
        compiler_params=_cp(1))(dh, w_out, ug, uu)


def _dgrad(dh, w, name):
    tp, d = dh.shape
    k = w.shape[0]

    def body(dh_ref, w_ref, o_ref):
        o_ref[...] = lax.dot_general(dh_ref[...].astype(BF16), w_ref[...], (((1,), (1,)), ((), ())),
                                     preferred_element_type=F32).astype(BF16)

    return pl.pallas_call(
        body, name=name, grid=(tp // TM,),
        in_specs=[pl.BlockSpec((TM, d), lambda i: (i, 0)), pl.BlockSpec((k, d), lambda i: (0, 0))],
        out_specs=pl.BlockSpec((TM, k), lambda i: (i, 0)),
        out_shape=jax.ShapeDtypeStruct((tp, k), BF16),
        compiler_params=_cp(1))(dh, w)


def _wgrad(a, b, *, bm, bn, scale, sharded, name):
    tp, m = a.shape
    n = b.shape[1]
    nk = tp // TM

    def body(a_ref, b_ref, o_ref, acc_ref):
        k = pl.program_id(2)

        @pl.when(k == 0)
        def _():
            acc_ref[...] = jnp.zeros_like(acc_ref)

        bb = b_ref[...]
        if scale != 1.0:
            bb = scale * bb
        acc_ref[...] += lax.dot_general(a_ref[...], bb.astype(BF16), (((0,), (0,)), ((), ())),
                                        preferred_element_type=F32)

        @pl.when(k == nk - 1)
        def _():
            o_ref[...] = acc_ref[...].astype(BF16)

    if sharded:
        assert m == bm
        out_spec = pl.BlockSpec((None, bm, bn), lambda i, j, k: (j, 0, 0))
        out_shape = jax.ShapeDtypeStruct((n // bn, m, bn), BF16)
    else:
        out_spec = pl.BlockSpec((bm, bn), lambda i, j, k: (i, j))
        out_shape = jax.ShapeDtypeStruct((m, n), BF16)
    return pl.pallas_call(
        body, name=name, grid=(m // bm, n // bn, nk),
        in_specs=[pl.BlockSpec((TM, bm), lambda i, j, k: (k, i)), pl.BlockSpec((TM, bn), lambda i, j, k: (k, j))],
        out_specs=out_spec, out_shape=out_shape,
        scratch_shapes=[pltpu.VMEM((bm, bn), F32)],
        compiler_params=_cp(3))(a, b)


def _dgrad_norm(du, w, h, gain, dh_out, name):
    tp, d = h.shape
    s, _, ns = w.shape

    def body(du_ref, w_ref, h_ref, g_ref, dho_ref, dhi_ref, dg_ref, acc_ref):
        i, k = pl.program_id(0), pl.program_id(1)

        @pl.when(k == 0)
        def _():
            acc_ref[...] = jnp.zeros_like(acc_ref)

        @pl.when((i == 0) & (k == 0))
        def _():
            dg_ref[...] = jnp.zeros_like(dg_ref)

        acc_ref[...] += lax.dot_general(du_ref[...], w_ref[...], (((1,), (1,)), ((), ())),
                                        preferred_element_type=F32)

        @pl.when(k == s - 1)
        def _():
            dhn = acc_ref[...]
            dx, xhat = _rmsnorm_bwd(dhn, h_ref[...], g_ref[...])
            dg_ref[...] += jnp.sum(dhn * xhat, axis=0, keepdims=True)
            dhi_ref[...] = dho_ref[...] + dx

    return pl.pallas_call(
        body, name=name, grid=(tp // TM, s),
        in_specs=[pl.BlockSpec((TM, ns), lambda i, k: (i, k)), pl.BlockSpec((None, d, ns), lambda i, k: (k, 0, 0)),
                  pl.BlockSpec((TM, d), lambda i, k: (i, 0)), pl.BlockSpec((1, d), lambda i, k: (0, 0)),
                  pl.BlockSpec((TM, d), lambda i, k: (i, 0))],
        out_specs=[pl.BlockSpec((TM, d), lambda i, k: (i, 0)), pl.BlockSpec((1, d), lambda i, k: (0, 0))],
        out_shape=[jax.ShapeDtypeStruct((tp, d), F32), jax.ShapeDtypeStruct((1, d), F32)],
        scratch_shapes=[pltpu.VMEM((TM, d), F32)],
        compiler_params=_cp(2))(du, w, h, gain, dh_out)


def _loss_head(h, gain, target, name):
    tp, d = h.shape
    tm = TM_SMALL
    front_tiles = FRONT // tm

    def body(h_ref, g_ref, t_ref, dh_ref, dg_ref, loss_ref):
        i = pl.program_id(0)

        @pl.when(i == 0)
        def _():
            dg_ref[...] = jnp.zeros_like(dg_ref)
            loss_ref[...] = jnp.zeros_like(loss_ref)

        x = h_ref[...]
        gain_v = g_ref[...]
        y = x * _rms(x) * gain_v
        err = jnp.where(i >= front_tiles, y - t_ref[...], 0.0)
        loss_ref[...] += 0.5 * jnp.sum(jnp.mean(err * err, axis=-1, keepdims=True), axis=0, keepdims=True)
        dy = err * (1.0 / d)
        dx, xhat = _rmsnorm_bwd(dy, x, gain_v)
        dg_ref[...] += jnp.sum(dy * xhat, axis=0, keepdims=True)
        dh_ref[...] = dx

    return pl.pallas_call(
        body, name=name, grid=(tp // tm,),
        in_specs=[pl.BlockSpec((tm, d), lambda i: (i, 0)), pl.BlockSpec((1, d), lambda i: (0, 0)),
                  pl.BlockSpec((tm, d), lambda i: (jnp.maximum(i - front_tiles, 0), 0))],
        out_specs=[pl.BlockSpec((tm, d), lambda i: (i, 0)), pl.BlockSpec((1, d), lambda i: (0, 0)),
                   pl.BlockSpec((1, 128), lambda i: (0, 0))],
        out_shape=[jax.ShapeDtypeStruct((tp, d), F32), jax.ShapeDtypeStruct((1, d), F32),
                   jax.ShapeDtypeStruct((1, 128), F32)],
        compiler_params=_cp(1))(h, gain, target)


def _gated_headnorm(o, g, gain):
    return o * _rms(o) * gain * (g * _sigmoid(g))


def _row_mask(chunk):
    rows = chunk * CHUNK + lax.broadcasted_iota(jnp.int32, (CHUNK, 1), 0)
    return (rows >= FRONT - N_META).astype(F32)


def _ret_head(q1, q2, k1, k2, v, g, state, gain, cos, sin, dmat, dq, dk, dc):
    q = jnp.concatenate([q1 * cos - q2 * sin, q1 * sin + q2 * cos], axis=-1)
    k = jnp.concatenate([k1 * cos - k2 * sin, k1 * sin + k2 * cos], axis=-1) * (RET_DK ** -0.5)
    scores = _nt(q, k) * dmat
    o = _nn(scores, v) + _nn(q * dq, state)
    new_state = state * dc + _tn(k * dk, v)
    return _gated_headnorm(o, g, gain), new_state


def _ret_consts():
    log_gamma = jnp.log1p(-2.0 ** (-5.0 - jnp.arange(HEADS, dtype=F32)))
    idx = jnp.arange(CHUNK, dtype=F32)
    rel = idx[:, None] - idx[None, :]
    dmat = jnp.where(rel >= 0, jnp.exp(log_gamma[:, None, None] * jnp.maximum(rel, 0.0)), 0.0)
    dq = jnp.exp(log_gamma[:, None] * (idx + 1.0))[..., None]
    dk = jnp.exp(log_gamma[:, None] * (CHUNK - 1.0 - idx))[..., None]
    dc = jnp.broadcast_to(jnp.exp(log_gamma * CHUNK)[:, None, None], (HEADS, 1, 128))
    return dmat, dq, dk, dc


def _rope_tables(tp):
    half = RET_DK // 2
    inv = 1.0 / (ROPE_BASE ** jnp.linspace(0.0, 1.0, half, dtype=F32))
    pos = (jnp.arange(tp) - (FRONT - N_META)).astype(F32)
    ang = pos[:, None] * inv[None, :]
    return jnp.cos(ang), jnp.sin(ang)


_RET_V0, _RET_G0 = 2 * D, 4 * D


def _heads(ref, start, width, stride=None):
    stride = width if stride is None else stride
    return jnp.stack([ref[:, start + stride * h:start + stride * h + width].astype(F32) for h in range(HEADS)])


def _put_heads(ref, start, value, mask, stride=None):
    width = value.shape[-1]
    stride = width if stride is None else stride
    for h in range(HEADS):
        ref[:, start + stride * h:start + stride * h + width] = (value[h] * mask).astype(ref.dtype)


def _ret_pieces(u_ref):
    hk = RET_DK // 2
    return (_heads(u_ref, 0, hk, RET_DK), _heads(u_ref, hk, hk, RET_DK), _heads(u_ref, D, hk, RET_DK),
            _heads(u_ref, D + hk, hk, RET_DK), _heads(u_ref, _RET_V0, RET_DV), _heads(u_ref, _RET_G0, RET_DV))


def _ret_const_specs(rev=None):
    c = (lambda n: (rev(n), 0)) if rev else (lambda n: (n, 0))
    z3 = lambda n: (0, 0, 0)
    return [pl.BlockSpec((CHUNK, RET_DK // 2), c), pl.BlockSpec((CHUNK, RET_DK // 2), c),
            pl.BlockSpec((HEADS, CHUNK, CHUNK), z3), pl.BlockSpec((HEADS, CHUNK, 1), z3),
            pl.BlockSpec((HEADS, CHUNK, 1), z3), pl.BlockSpec((HEADS, 1, 128), z3)]


def _ret_fwd(u, gain, name):
    tp = u.shape[0]
    nch = tp // CHUNK
    cos, sin = _rope_tables(tp)
    dmat, dq, dk, dc = _ret_consts()

    def body(u_ref, gain_ref, cos_ref, sin_ref, dmat_ref, dq_ref, dk_ref, dc_ref, on_ref, st_ref, state_ref):
        @pl.when(pl.program_id(0) == 0)
        def _():
            state_ref[...] = jnp.zeros_like(state_ref)

        state = state_ref[...]
        st_ref[...] = state.astype(BF16)
        on, new_state = _ret_head(*_ret_pieces(u_ref), state, _heads(gain_ref, 0, RET_DV), cos_ref[...], sin_ref[...],
                                  dmat_ref[...], dq_ref[...], dk_ref[...], dc_ref[...][:, :, :1])
        state_ref[...] = new_state
        _put_heads(on_ref, 0, on, 1.0)

    return pl.pallas_call(
        body, name=name, grid=(nch,),
        in_specs=[pl.BlockSpec((CHUNK, 6 * D), lambda n: (n, 0)), pl.BlockSpec((1, HEADS * RET_DV), lambda n: (0, 0))]
                 + _ret_const_specs(),
        out_specs=[pl.BlockSpec((CHUNK, HEADS * RET_DV), lambda n: (n, 0)),
                   pl.BlockSpec((None, HEADS, RET_DK, RET_DV), lambda n: (n, 0, 0, 0))],
        out_shape=[jax.ShapeDtypeStruct((tp, HEADS * RET_DV), BF16),
                   jax.ShapeDtypeStruct((nch, HEADS, RET_DK, RET_DV), BF16)],
        scratch_shapes=[pltpu.VMEM((HEADS, RET_DK, RET_DV), F32)],
        compiler_params=_cp(1))(u, gain, cos, sin, dmat, dq, dk, dc)


def _ret_bwd(u, gain, states, d_on, name):
    tp = u.shape[0]
    nch = tp // CHUNK
    cos, sin = _rope_tables(tp)
    dmat, dq, dk, dc = _ret_consts()
    rev = lambda n: nch - 1 - n
    hk = RET_DK // 2

    def body(u_ref, gain_ref, st_ref, don_ref, cos_ref, sin_ref, dmat_ref, dq_ref, dk_ref, dc_ref,
             du_ref, dgain_ref, dstate_ref):
        @pl.when(pl.program_id(0) == 0)
        def _():
            dstate_ref[...] = jnp.zeros_like(dstate_ref)
            dgain_ref[...] = jnp.zeros_like(dgain_ref)

        mask = _row_mask(rev(pl.program_id(0)))
        consts = (cos_ref[...], sin_ref[...], dmat_ref[...], dq_ref[...], dk_ref[...], dc_ref[...][:, :, :1])
        _, vjp = jax.vjp(lambda *a: _ret_head(*a, *consts), *_ret_pieces(u_ref), st_ref[...].astype(F32),
                         _heads(gain_ref, 0, RET_DV))
        dq1, dq2, dk1, dk2, dv, dg, dstate, dgain = vjp((_heads(don_ref, 0, RET_DV), dstate_ref[...]))
        dstate_ref[...] = dstate
        for hd in range(HEADS):
            dgain_ref[:, RET_DV * hd:RET_DV * (hd + 1)] += dgain[hd]
        _put_heads(du_ref, 0, dq1, mask, RET_DK)
        _put_heads(du_ref, hk, dq2, mask, RET_DK)
        _put_heads(du_ref, D, dk1, mask, RET_DK)
        _put_heads(du_ref, D + hk, dk2, mask, RET_DK)
        _put_heads(du_ref, _RET_V0, dv, mask)
        _put_heads(du_ref, _RET_G0, dg, mask)

    return pl.pallas_call(
        body, name=name, grid=(nch,),
        in_specs=[pl.BlockSpec((CHUNK, 6 * D), lambda n: (rev(n), 0)),
                  pl.BlockSpec((1, HEADS * RET_DV), lambda n: (0, 0)),
                  pl.BlockSpec((None, HEADS, RET_DK, RET_DV), lambda n: (rev(n), 0, 0, 0)),
                  pl.BlockSpec((CHUNK, HEADS * RET_DV), lambda n: (rev(n), 0))] + _ret_const_specs(rev),
        out_specs=[pl.BlockSpec((CHUNK, 6 * D), lambda n: (rev(n), 0)),
                   pl.BlockSpec((1, HEADS * RET_DV), lambda n: (0, 0))],
        out_shape=[jax.ShapeDtypeStruct((tp, 6 * D), BF16), jax.ShapeDtypeStruct((1, HEADS * RET_DV), F32)],
        scratch_shapes=[pltpu.VMEM((HEADS, RET_DK, RET_DV), F32)],
        compiler_params=_cp(1))(u, gain, states, d_on, cos, sin, dmat, dq, dk, dc)


_GLA_K0, _GLA_V0, _GLA_G0, _GLA_Z0 = 512, 1024, 2048, 3072


def _gla_head(q, k, v, g, z, state_t, wg, bg, gain, mask, lo, lo_t, loc, loc_t):
    ga = _nn(jnp.broadcast_to(z, wg.shape[:-2] + z.shape), wg) + bg
    log_a = (jnp.minimum(ga, 0.0) - jnp.log(1.0 + jnp.exp(-jnp.abs(ga)))) * (mask * (1.0 / GLA_TAU))
    bcum = _cum(lo, lo_t, log_a)
    bmid = _cum(loc, loc_t, log_a)
    btot = jnp.sum(log_a, axis=-2, keepdims=True)
    qs = q * (GLA_DK ** -0.5)
    causal = lax.broadcasted_iota(jnp.int32, (CHUNK, CHUNK), 0) >= lax.broadcasted_iota(jnp.int32, (CHUNK, CHUNK), 1)
    scores = jnp.where(causal, _nt(qs * jnp.exp(bmid), k * jnp.exp(-bmid)), 0.0)
    o = _nn(scores, v) + _nt(qs * jnp.exp(bcum), state_t)
    new_state_t = state_t * jnp.exp(btot) + _tn(v, k * jnp.exp(btot - bcum))
    return _gated_headnorm(o, g, gain), new_state_t


def _cum_mats():
    r = lax.broadcasted_iota(jnp.int32, (CHUNK, CHUNK), 0)
    c = lax.broadcasted_iota(jnp.int32, (CHUNK, CHUNK), 1)
    mid = CHUNK // 2
    low = lambda a, b: (a >= b).astype(F32)
    lo, lo_t = low(r, c), low(c, r)
    loc = lo - (c <= mid).astype(F32)
    loc_t = lo_t - (r <= mid).astype(F32)
    return tuple(m.astype(BF16) for m in (lo, lo_t, loc, loc_t))


def _gla_pieces(u_ref):
    return (_heads(u_ref, 0, GLA_DK), _heads(u_ref, _GLA_K0, GLA_DK), _heads(u_ref, _GLA_V0, GLA_DV),
            _heads(u_ref, _GLA_G0, GLA_DV), u_ref[:, _GLA_Z0:].astype(F32))


def _gla_fwd(u, wg, bg, gain, name):
    tp = u.shape[0]
    nch = tp // CHUNK

    def body(u_ref, wg_ref, bg_ref, gain_ref, on_ref, st_ref, state_ref):
        @pl.when(pl.program_id(0) == 0)
        def _():
            state_ref[...] = jnp.zeros_like(state_ref)

        state = state_ref[...]
        st_ref[...] = state.astype(BF16)
        on, new_state = _gla_head(*_gla_pieces(u_ref), state, _heads(wg_ref, 0, GLA_DK), _heads(bg_ref, 0, GLA_DK),
                                  _heads(gain_ref, 0, GLA_DV), _row_mask(pl.program_id(0)), *_cum_mats())
        state_ref[...] = new_state
        _put_heads(on_ref, 0, on, 1.0)

    return pl.pallas_call(
        body, name=name, grid=(nch,),
        in_specs=[pl.BlockSpec((CHUNK, GLA_U), lambda n: (n, 0)), pl.BlockSpec((128, HEADS * GLA_DK), lambda n: (0, 0)),
                  pl.BlockSpec((1, HEADS * GLA_DK), lambda n: (0, 0)), pl.BlockSpec((1, HEADS * GLA_DV), lambda n: (0, 0))],
        out_specs=[pl.BlockSpec((CHUNK, HEADS * GLA_DV), lambda n: (n, 0)),
                   pl.BlockSpec((None, HEADS, GLA_DV, GLA_DK), lambda n: (n, 0, 0, 0))],
        out_shape=[jax.ShapeDtypeStruct((tp, HEADS * GLA_DV), BF16),
                   jax.ShapeDtypeStruct((nch, HEADS, GLA_DV, GLA_DK), BF16)],
        scratch_shapes=[pltpu.VMEM((HEADS, GLA_DV, GLA_DK), F32)],
        compiler_params=_cp(1))(u, wg, bg, gain)


def _gla_bwd(u, wg, bg, gain, states, d_on, name):
    tp = u.shape[0]
    nch = tp // CHUNK
    rev = lambda n: nch - 1 - n

    def body(u_ref, wg_ref, bg_ref, gain_ref, st_ref, don_ref, du_ref, dwg_ref, dbg_ref, dgain_ref, dstate_ref):
        @pl.when(pl.program_id(0) == 0)
        def _():
            dstate_ref[...] = jnp.zeros_like(dstate_ref)
            dwg_ref[...] = jnp.zeros_like(dwg_ref)
            dbg_ref[...] = jnp.zeros_like(dbg_ref)
            dgain_ref[...] = jnp.zeros_like(dgain_ref)

        mask = _row_mask(rev(pl.program_id(0)))
        mats = _cum_mats()
        _, vjp = jax.vjp(lambda *a: _gla_head(*a, mask, *mats), *_gla_pieces(u_ref), st_ref[...].astype(F32),
                         _heads(wg_ref, 0, GLA_DK), _heads(bg_ref, 0, GLA_DK), _heads(gain_ref, 0, GLA_DV))
        dq, dk, dv, dg, dz, dstate, dwg, dbg, dgain = vjp((_heads(don_ref, 0, GLA_DV), dstate_ref[...]))
        dstate_ref[...] = dstate
        for hd in range(HEADS):
            dwg_ref[:, GLA_DK * hd:GLA_DK * (hd + 1)] += dwg[hd]
            dbg_ref[:, GLA_DK * hd:GLA_DK * (hd + 1)] += dbg[hd]
            dgain_ref[:, GLA_DV * hd:GLA_DV * (hd + 1)] += dgain[hd]
        _put_heads(du_ref, 0, dq, mask)
        _put_heads(du_ref, _GLA_K0, dk, mask)
        _put_heads(du_ref, _GLA_V0, dv, mask)
        _put_heads(du_ref, _GLA_G0, dg, mask)
        du_ref[:, _GLA_Z0:] = dz.astype(BF16)

    full = lambda r, c: pl.BlockSpec((r, c), lambda n: (0, 0))
    return pl.pallas_call(
        body, name=name, grid=(nch,),
        in_specs=[pl.BlockSpec((CHUNK, GLA_U), lambda n: (rev(n), 0)), full(128, HEADS * GLA_DK),
                  full(1, HEADS * GLA_DK), full(1, HEADS * GLA_DV),
                  pl.BlockSpec((None, HEADS, GLA_DV, GLA_DK), lambda n: (rev(n), 0, 0, 0)),
                  pl.BlockSpec((CHUNK, HEADS * GLA_DV), lambda n: (rev(n), 0))],
        out_specs=[pl.BlockSpec((CHUNK, GLA_U), lambda n: (rev(n), 0)), full(128, HEADS * GLA_DK),
                   full(1, HEADS * GLA_DK), full(1, HEADS * GLA_DV)],
        out_shape=[jax.ShapeDtypeStruct((tp, GLA_U), BF16), jax.ShapeDtypeStruct((128, HEADS * GLA_DK), F32),
                   jax.ShapeDtypeStruct((1, HEADS * GLA_DK), F32), jax.ShapeDtypeStruct((1, HEADS * GLA_DV), F32)],
        scratch_shapes=[pltpu.VMEM((HEADS, GLA_DV, GLA_DK), F32)],
        compiler_params=_cp(1))(u, wg, bg, gain, states, d_on)


def _ffn_fwd(h, gain, w_in, w_out, tag):
    hn, ug, uu, act = _norm_ffn_in(h, gain, w_in, f"{tag}_in")
    if callable(w_out):
        w_out = w_out(act)
    return _out_proj(act, w_out, h, 0.5, f"{tag}_out"), (h, hn, ug, uu, act), w_out


def _ffn_bwd(dh, saved, gain, w_in, w_out, tag, push):
    h, hn, ug, uu, act = saved
    du = _ffn_dact(dh, w_out, ug, uu, f"{tag}_dact")
    d_w_out = _wgrad(act, dh, bm=D_FF // 2, bn=D, scale=0.5, sharded=False, name=f"{tag}_dwout")
    d_w_in = _wgrad(hn, du, bm=D, bn=w_in.shape[2], scale=1.0, sharded=True, name=f"{tag}_dwin")
    token = push([d_w_in, d_w_out])
    return _dgrad_norm(du, w_in, h, gain + token[0, 0], dh, f"{tag}_dnorm")


def _sequence_grads(x, target, p, weights, grads):
    row = lambda v, token: v.reshape(1, -1) + token[0, 0]
    gains = {}

    tok = weights.start(1, weights.start(0, None))
    w = weights.wait(0, tok)
    tok = weights.start(2, w["l0_ffn1_in"])
    h = jnp.concatenate([jnp.zeros((FRONT - N_META, D), F32), w["meta"], x], axis=0)
    gains["l0_ffn1"] = row(p["norm_ffn1"][0], tok)
    h, s1, w["l0_ffn1_out"] = _ffn_fwd(h, gains["l0_ffn1"], w["l0_ffn1_in"],
                                       lambda act: weights.wait(1, act)["l0_ffn1_out"], "l0_ffn1")
    w.update(weights.wait(2, h))
    tok = weights.start(3, w["ret_in"])
    gains["ret"] = row(p["norm_mix"][0], tok)
    hn, u = _norm_proj(h, gains["ret"], w["ret_in"], "ret_in")
    on, states = _ret_fwd(u, w["ret_gain"], "ret_fwd")
    h_mix = _out_proj(on, w["ret_out"], h, 1.0, "ret_out")
    s2 = (h, hn, u, on, states)
    w.update(weights.wait(3, h_mix))
    tok = weights.start(4, w["l0_ffn2_in"])
    gains["l0_ffn2"] = row(p["norm_ffn2"][0], tok)
    h, s3, _ = _ffn_fwd(h_mix, gains["l0_ffn2"], w["l0_ffn2_in"], w["l0_ffn2_out"], "l0_ffn2")
    saved = [(s1, s2, s3)]

    w.update(weights.wait(4, h))
    tok = weights.start(5, w["l1_ffn1_in"])
    gains["l1_ffn1"] = row(p["norm_ffn1"][1], tok)
    h, s1, _ = _ffn_fwd(h, gains["l1_ffn1"], w["l1_ffn1_in"], w["l1_ffn1_out"], "l1_ffn1")
    w.update(weights.wait(5, h))
    tok = weights.start(6, w["gla_out"])
    gains["gla"] = row(p["norm_mix"][1], tok)
    hn, u = _norm_proj(h, gains["gla"], w["gla_in"], "gla_in")
    on, states = _gla_fwd(u, w["gla_wg"], w["gla_bg"], w["gla_gain"], "gla_fwd")
    h_mix = _out_proj(on, w["gla_out"], h, 1.0, "gla_out")
    s2 = (h, hn, u, on, states)
    w.update(weights.wait(6, h_mix))
    gains["l1_ffn2"] = p["norm_ffn2"][1].reshape(1, -1)
    h, s3, _ = _ffn_fwd(h_mix, gains["l1_ffn2"], w["l1_ffn2_in"], w["l1_ffn2_out"], "l1_ffn2")
    saved.append((s1, s2, s3))

    dh, d_final, loss = _loss_head(h, p["final_norm"].reshape(1, -1), target, "loss_head")
    small = {"final_norm": d_final, "norm_ffn1": [None, None], "norm_mix": [None, None], "norm_ffn2": [None, None]}
    pusher = lambda k: functools.partial(grads.push, k)

    s1, s2, s3 = saved[1]
    dh, small["norm_ffn2"][1] = _ffn_bwd(dh, s3, gains["l1_ffn2"], w["l1_ffn2_in"], w["l1_ffn2_out"], "l1_ffn2",
                                         pusher(0))
    h_in, hn, u, on, states = s2
    d_on = _dgrad(dh, w["gla_out"], "gla_don")
    d_out = _wgrad(on, dh, bm=D, bn=D, scale=1.0, sharded=False, name="gla_dwout")
    du, small["gla_wg"], small["gla_bg"], small["gla_gain"] = _gla_bwd(u, w["gla_wg"], w["gla_bg"], w["gla_gain"],
                                                                       states, d_on, "gla_bwd")
    d_in = _wgrad(hn, du, bm=D, bn=GLA_U // 5, scale=1.0, sharded=False, name="gla_dwin")
    d_in = jnp.moveaxis(d_in[:, :GLA_IN].reshape(D, N_CHIPS, -1), 1, 0)
    tok = grads.push(1, [d_in, d_out])
    dh, small["norm_mix"][1] = _dgrad_norm(du, w["gla_in"], h_in, gains["gla"] + tok[0, 0], dh, "gla_dnorm")
    dh, small["norm_ffn1"][1] = _ffn_bwd(dh, s1, gains["l1_ffn1"], w["l1_ffn1_in"], w["l1_ffn1_out"], "l1_ffn1",
                                         pusher(2))

    s1, s2, s3 = saved[0]
    dh, small["norm_ffn2"][0] = _ffn_bwd(dh, s3, gains["l0_ffn2"], w["l0_ffn2_in"], w["l0_ffn2_out"], "l0_ffn2",
                                         pusher(3))
    h_in, hn, u, on, states = s2
    d_on = _dgrad(dh, w["ret_out"], "ret_don")
    d_out = _wgrad(on, dh, bm=D, bn=D, scale=1.0, sharded=False, name="ret_dwout")
    du, small["ret_gain"] = _ret_bwd(u, w["ret_gain"], states, d_on, "ret_bwd")
    d_in = _wgrad(hn, du, bm=D, bn=w["ret_in"].shape[2], scale=1.0, sharded=True, name="ret_dwin")
    tok = grads.push(4, [d_in, d_out])
    dh, small["norm_mix"][0] = _dgrad_norm(du, w["ret_in"], h_in, gains["ret"] + tok[0, 0], dh, "ret_dnorm")
    dh, small["norm_ffn1"][0] = _ffn_bwd(dh, s1, gains["l0_ffn1"], w["l0_ffn1_in"], w["l0_ffn1_out"], "l0_ffn1",
                                         pusher(5))
    grads.push(6, [], [dh[FRONT - N_META:FRONT], *small["norm_ffn1"], *small["norm_mix"], *small["norm_ffn2"],
                       small["final_norm"], small["ret_gain"], small["gla_wg"][:GLA_RANK], small["gla_bg"],
                       small["gla_gain"], loss[:, :1]])
    return dh[FRONT:]


_HBM = pl.BlockSpec(memory_space=pl.ANY)


def _place():
    return lax.axis_index("x"), lax.axis_index("y"), lax.axis_index("c")


def _flip(v, bit):
    return 1 - v if bit else v


DMA_CHUNK_BYTES = 128 * 1024


def _row_chunks(ref):
    rows, cols = ref.shape
    step = _row_tile(rows, max(16, DMA_CHUNK_BYTES // (cols * ref.dtype.itemsize)))
    return [pl.ds(a, step) for a in range(0, rows, step)]


def _whole(src, dst, send_sem, recv_sem, peer):
    return pltpu.make_async_remote_copy(src_ref=src, dst_ref=dst, send_sem=send_sem, recv_sem=recv_sem,
                                        device_id=peer, device_id_type=MESH)


def _send(src, dst, send_sem, recv_sem, peer):
    for rows in _row_chunks(src):
        _whole(src.at[rows], dst.at[rows], send_sem, recv_sem, peer).start()
    return _whole(src, dst, send_sem, recv_sem, peer)


_HBM_ONLY = pl.BlockSpec(memory_space=pltpu.HBM)
_SEMS = pl.BlockSpec(memory_space=pltpu.SEMAPHORE)
_SIDE_EFFECT = pltpu.CompilerParams(has_side_effects=pltpu.SideEffectType.DATAFLOW_SIDE_EFFECTING)
_CHIP_FLIPS = [(1, 0, 0), (0, 1, 0), (1, 1, 0)]
_PEER_FLIPS = [(fx, fy, fc) for fx in (0, 1) for fy in (0, 1) for fc in (0, 1)][1:]


def _zero_token():
    return jnp.zeros((8, 128), F32)


def _exchange_start(srcs, lands, route, flips, after, name):
    n = len(srcs)

    def body(*refs):
        src, land = refs[:n], refs[n:2 * n]
        send_sems, recv_sems, token = refs[2 * n + 1], refs[2 * n + 2], refs[-1]
        me = _place()
        for t in range(n):
            for j, flip in enumerate(flips):
                peer = tuple(_flip(v, f) for v, f in zip(me, flip))
                s, d = route(t, src[t], land[t], me, peer)
                _send(s, d, send_sems.at[t * len(flips) + j], recv_sems.at[t * len(flips) + j], peer)
        token[...] = jnp.zeros_like(token)

    hbm = lambda a: pltpu.HBM(a.shape, a.dtype)
    sems = pltpu.SemaphoreType.DMA((n * len(flips),))
    operands = [pltpu.with_memory_space_constraint(a, pltpu.HBM) for a in list(srcs) + list(lands)]
    out = pl.pallas_call(
        body, name=name, in_specs=[_HBM_ONLY] * (2 * n) + [_HBM],
        out_shape=(sems, sems, *[hbm(a) for a in operands], jax.ShapeDtypeStruct((8, 128), F32)),
        out_specs=(_SEMS, _SEMS, *[_HBM_ONLY] * (2 * n), pl.BlockSpec(memory_space=pltpu.VMEM)),
        input_output_aliases={i: 2 + i for i in range(2 * n)}, compiler_params=_SIDE_EFFECT,
    )(*operands, _zero_token() if after is None else after)
    return (out[0], out[1], out[2:2 + n], out[2 + n:2 + 2 * n]), out[-1]


def _exchange_wait(started, route, flips, after, name):
    send_sems, recv_sems, srcs, lands = started
    n = len(srcs)

    def body(*refs):
        src, land = refs[:n], refs[n:2 * n]
        send_sems, recv_sems = refs[2 * n], refs[2 * n + 1]
        me = _place()
        for t in range(n):
            for j, flip in enumerate(flips):
                peer = tuple(_flip(v, f) for v, f in zip(me, flip))
                s, d = route(t, src[t], land[t], me, peer)
                cp = _whole(s, d, send_sems.at[t * len(flips) + j], recv_sems.at[t * len(flips) + j], peer)
                cp.wait_send()
                cp.wait_recv()

    hbm = lambda a: pltpu.HBM(a.shape, a.dtype)
    out = pl.pallas_call(
        body, name=name, in_specs=[_HBM_ONLY] * (2 * n) + [_SEMS, _SEMS, _HBM],
        out_shape=tuple(hbm(a) for a in list(srcs) + list(lands)), out_specs=tuple([_HBM_ONLY] * (2 * n)),
        input_output_aliases={i: i for i in range(2 * n)}, compiler_params=_SIDE_EFFECT,
    )(*srcs, *lands, send_sems, recv_sems, after)
    return out[:n], out[n:]


def _gather_route(t, src, land, me, peer):
    return src, land.at[2 * me[0] + me[1]]


def _scatter_route(n_pieces):
    def route(t, src, land, me, peer):
        part = src.at[2 * peer[0] + peer[1], peer[2]] if t < n_pieces else src
        return part, land.at[4 * me[0] + 2 * me[1] + me[2]]

    return route


def _swap_cores(halves):
    n = len(halves)

    def body(*refs):
        src, dst = refs[:n], refs[n:2 * n]
        send_sems, recv_sems = refs[2 * n:]
        x, y, c = _place()
        copies = [_send(src[t], dst[t], send_sems.at[t], recv_sems.at[t], (x, y, 1 - c)) for t in range(n)]
        for cp in copies:
            cp.wait()

    got = pl.pallas_call(
        body, name="swap_cores", in_specs=[_HBM] * n, out_specs=[_HBM] * n,
        out_shape=[jax.ShapeDtypeStruct(a.shape, a.dtype) for a in halves],
        scratch_shapes=[pltpu.SemaphoreType.DMA((n,)), pltpu.SemaphoreType.DMA((n,))],
    )(*halves)
    south = lax.axis_index("c") == 0
    return [jnp.stack([jnp.where(south, a, b), jnp.where(south, b, a)]) for a, b in zip(halves, got)]


def _row_tile(rows, cap):
    fits = [t for t in range(16, cap + 1, 16) if rows % t == 0]
    return fits[-1] if fits else rows


def _sum_slots(a, name):
    _, r, c = a.shape
    tr = _row_tile(r, 384)

    def body(a_ref, o_ref):
        s = a_ref[0].astype(F32)
        for k in range(1, N_DEV):
            s = s + a_ref[k].astype(F32)
        o_ref[...] = s

    return pl.pallas_call(
        body, name=name, grid=(r // tr,),
        in_specs=[pl.BlockSpec((N_DEV, tr, c), lambda i: (0, i, 0))],
        out_specs=pl.BlockSpec((tr, c), lambda i: (i, 0)),
        out_shape=jax.ShapeDtypeStruct((r, c), F32),
        compiler_params=_cp(1))(a)


def _adamw(w, g, m, v, name):
    r, c = w.shape
    tr = _row_tile(r, 256)

    def body(w_ref, g_ref, m_ref, v_ref, d_ref, nm_ref, nv_ref):
        gv = g_ref[...]
        nm = ADAM_B1 * m_ref[...] + (1.0 - ADAM_B1) * gv
        nv = ADAM_B2 * v_ref[...] + (1.0 - ADAM_B2) * (gv * gv)
        m_hat = nm / (1.0 - ADAM_B1 ** ADAM_STEP)
        v_hat = nv / (1.0 - ADAM_B2 ** ADAM_STEP)
        d_ref[...] = -ADAM_LR * (m_hat / (jnp.sqrt(v_hat) + ADAM_EPS) + ADAM_WD * w_ref[...])
        nm_ref[...] = nm
        nv_ref[...] = nv

    spec = pl.BlockSpec((tr, c), lambda i: (i, 0))
    return pl.pallas_call(
        body, name=name, grid=(r // tr,), in_specs=[spec] * 4, out_specs=[spec] * 3,
        out_shape=[jax.ShapeDtypeStruct((r, c), F32)] * 3,
        compiler_params=_cp(1))(w, g, m, v)


_SMALL = ["meta_tokens", "ret_head_norm", "gla_w_gate", "gla_b_gate", "gla_head_norm"]
_LOCAL_SMALL = ["meta_tokens", "norm_ffn1", "norm_mix", "norm_ffn2", "ret_head_norm", "gla_w_gate", "gla_b_gate",
                "gla_head_norm", "final_norm"]
_BIG = ["ffn1_w_in", "ffn1_w_out", "ffn2_w_in", "ffn2_w_out", "ret_w_in", "ret_w_out", "gla_w_in", "gla_w_out"]
_WEIGHTS = ["meta_tokens", "norm_ffn1", "ffn1_w_in", "ffn1_w_out", "norm_mix", "norm_ffn2", "ffn2_w_in", "ffn2_w_out",
            "ret_w_in", "ret_head_norm", "ret_w_out", "gla_w_in", "gla_w_gate", "gla_b_gate", "gla_head_norm",
            "gla_w_out", "final_norm"]


def _pack_rows(arrays, width):
    flat = jnp.concatenate([a.reshape(-1) for a in arrays])
    pad = -flat.shape[0] % (8 * width)
    return jnp.pad(flat, (0, pad)).reshape(-1, width)


def _unpack_rows(packed, shapes):
    flat, out, at = packed.reshape(-1), [], 0
    for s in shapes:
        size = 1
        for dim in s:
            size *= dim
        out.append(flat[at:at + size].reshape(s))
        at += size
    return out


class _WeightGather:
    GROUPS = [("small", "l0_ffn1_in"), ("l0_ffn1_out",), ("ret_in", "ret_out"), ("l0_ffn2_in", "l0_ffn2_out"),
              ("l1_ffn1_in", "l1_ffn1_out"), ("gla_in", "gla_out"), ("l1_ffn2_in", "l1_ffn2_out")]

    def __init__(self, p):
        b = lambda a: a.astype(BF16)
        self.small_shapes = [p[name].shape for name in _SMALL]
        self.shards = {"small": _pack_rows([p[name] for name in _SMALL], 128), "ret_in": b(p["ret_w_in"][0]),
                       "ret_out": b(p["ret_w_out"][0]), "gla_in": b(p["gla_w_in"][0]), "gla_out": b(p["gla_w_out"][0])}
        for layer in range(2):
            for name in ("ffn1", "ffn2"):
                self.shards[f"l{layer}_{name}_in"] = b(p[f"{name}_w_in"][layer])
                self.shards[f"l{layer}_{name}_out"] = b(p[f"{name}_w_out"][layer])
        self.started = {}

    def start(self, k, after):
        shards = [self.shards[name] for name in self.GROUPS[k]]
        lands = [lax.empty((N_CHIPS,) + s.shape, s.dtype) for s in shards]
        self.started[k], token = _exchange_start(shards, lands, _gather_route, _CHIP_FLIPS, after, f"gather{k}_start")
        return token

    def wait(self, k, after):
        shards, got = _exchange_wait(self.started[k], _gather_route, _CHIP_FLIPS, after, f"gather{k}_wait")
        mine = 2 * lax.axis_index("x") + lax.axis_index("y")
        w = {}
        for name, g, s in zip(self.GROUPS[k], got, shards):
            g = lax.dynamic_update_index_in_dim(g, s, mine, 0)
            if name == "small":
                parts = zip(*[_unpack_rows(g[chip], self.small_shapes) for chip in range(N_CHIPS)])
                cat = lambda a: jnp.moveaxis(a, 0, -2).reshape(a.shape[1:-1] + (-1,))
                meta, ret_gain, wg, bg, gla_gain = [cat(jnp.stack(part)) for part in parts]
                w.update(meta=meta, ret_gain=ret_gain.reshape(1, -1), gla_bg=bg.reshape(1, -1),
                         gla_gain=gla_gain.reshape(1, -1),
                         gla_wg=jnp.pad(wg[0], ((0, 128 - GLA_RANK), (0, 0))).astype(BF16))
            elif name == "gla_in":
                full = jnp.moveaxis(g, 0, 1).reshape(D, -1)
                w[name] = jnp.pad(full, ((0, 0), (0, GLA_U - GLA_IN)))[None]
            elif name.endswith("_out"):
                w[name] = g.reshape(-1, g.shape[-1])
            else:
                w[name] = g
        return w


class _GradExchange:
    def __init__(self):
        self.started = []
        self.token = None
        self.small_shapes = None

    def push(self, k, arrays, small=None):
        srcs = [a.reshape(N_CHIPS, 2, -1, a.shape[-1]) for a in arrays]
        lands = [lax.empty((N_DEV,) + a.shape[2:], a.dtype) for a in srcs]
        if small is not None:
            self.small_shapes = [a.shape for a in small]
            srcs.append(_pack_rows(small, D))
            lands.append(lax.empty((N_DEV,) + srcs[-1].shape, F32))
        started, self.token = _exchange_start(srcs, lands, _scatter_route(len(arrays)), _PEER_FLIPS, None,
                                              f"scatter{k}_start")
        self.started.append((started, len(arrays)))
        return self.token

    def collect(self):
        x, y, c = _place()
        after, sums = self.token, []
        for k, (started, n_pieces) in enumerate(self.started):
            srcs, got = _exchange_wait(started, _scatter_route(n_pieces), _PEER_FLIPS, after, f"scatter{k}_wait")
            own = [a[2 * x + y, c] for a in srcs[:n_pieces]] + list(srcs[n_pieces:])
            got = [lax.dynamic_update_index_in_dim(g, a, 4 * x + 2 * y + c, 0) for g, a in zip(got, own)]
            sums.append([_sum_slots(a, f"sum{k}_{i}") for i, a in enumerate(got)])
            after = sums[-1][0]
        small = _unpack_rows(sums[-1].pop(), self.small_shapes)
        return sums, small


def kernel(x, meta_tokens, norm_ffn1, ffn1_w_in, ffn1_w_out, norm_mix, norm_ffn2, ffn2_w_in, ffn2_w_out, ret_w_in, ret_head_norm, ret_w_out, gla_w_in, gla_w_gate, gla_b_gate, gla_head_norm, gla_w_out, final_norm, loss_target, m_meta_tokens, m_norm_ffn1, m_ffn1_w_in, m_ffn1_w_out, m_norm_mix, m_norm_ffn2, m_ffn2_w_in, m_ffn2_w_out, m_ret_w_in, m_ret_head_norm, m_ret_w_out, m_gla_w_in, m_gla_w_gate, m_gla_b_gate, m_gla_head_norm, m_gla_w_out, m_final_norm, v_meta_tokens, v_norm_ffn1, v_ffn1_w_in, v_ffn1_w_out, v_norm_mix, v_norm_ffn2, v_ffn2_w_in, v_ffn2_w_out, v_ret_w_in, v_ret_head_norm, v_ret_w_out, v_gla_w_in, v_gla_w_gate, v_gla_b_gate, v_gla_head_norm, v_gla_w_out, v_final_norm):
    p = dict(meta_tokens=meta_tokens, norm_ffn1=norm_ffn1, ffn1_w_in=ffn1_w_in, ffn1_w_out=ffn1_w_out, norm_mix=norm_mix,
             norm_ffn2=norm_ffn2, ffn2_w_in=ffn2_w_in, ffn2_w_out=ffn2_w_out, ret_w_in=ret_w_in,
             ret_head_norm=ret_head_norm, ret_w_out=ret_w_out, gla_w_in=gla_w_in, gla_w_gate=gla_w_gate,
             gla_b_gate=gla_b_gate, gla_head_norm=gla_head_norm, gla_w_out=gla_w_out, final_norm=final_norm)
    m = dict(zip(_WEIGHTS, (m_meta_tokens, m_norm_ffn1, m_ffn1_w_in, m_ffn1_w_out, m_norm_mix, m_norm_ffn2, m_ffn2_w_in,
                            m_ffn2_w_out, m_ret_w_in, m_ret_head_norm, m_ret_w_out, m_gla_w_in, m_gla_w_gate,
                            m_gla_b_gate, m_gla_head_norm, m_gla_w_out, m_final_norm)))
    v = dict(zip(_WEIGHTS, (v_meta_tokens, v_norm_ffn1, v_ffn1_w_in, v_ffn1_w_out, v_norm_mix, v_norm_ffn2, v_ffn2_w_in,
                            v_ffn2_w_out, v_ret_w_in, v_ret_head_norm, v_ret_w_out, v_gla_w_in, v_gla_w_gate,
                            v_gla_b_gate, v_gla_head_norm, v_gla_w_out, v_final_norm)))

    exchange = _GradExchange()
    d_x = _sequence_grads(x[0], loss_target[0], p, _WeightGather(p), exchange)
    sums, small = exchange.collect()
    names = [("ffn2_in", 1), ("ffn2_out", 1), ("gla_in", 0), ("gla_out", 0), ("ffn1_in", 1), ("ffn1_out", 1),
             ("ffn2_in", 0), ("ffn2_out", 0), ("ret_in", 0), ("ret_out", 0), ("ffn1_in", 0), ("ffn1_out", 0)]
    swapped = _swap_cores([a for group in sums for a in group])
    shard = {key: a.reshape(-1, a.shape[-1]) for key, a in zip(names, swapped)}
    big = {name: [shard[name, layer] for layer in range(2) if (name, layer) in shard] for name, _ in names}

    chip = 2 * lax.axis_index("x") + lax.axis_index("y")
    cols = lambda a, n: lax.dynamic_slice_in_dim(a, chip * n, n, axis=a.ndim - 1)
    (s_meta, s_n1a, s_n1b, s_nma, s_nmb, s_n2a, s_n2b, s_final, s_ret_gain, s_wg, s_bg, s_gla_gain, s_loss) = small
    grads = {
        "meta_tokens": cols(s_meta, 256), "norm_ffn1": jnp.concatenate([s_n1a, s_n1b]),
        "norm_mix": jnp.concatenate([s_nma, s_nmb]), "norm_ffn2": jnp.concatenate([s_n2a, s_n2b]),
        "final_norm": s_final.reshape(D),
        "ret_head_norm": cols(s_ret_gain.reshape(1, HEADS, RET_DV), RET_DV // N_CHIPS),
        "gla_w_gate": cols(s_wg, GLA_DK)[None], "gla_b_gate": cols(s_bg, GLA_DK),
        "gla_head_norm": cols(s_gla_gain.reshape(1, HEADS, GLA_DV), GLA_DV // N_CHIPS),
        "ffn1_w_in": jnp.stack(big["ffn1_in"]), "ffn1_w_out": jnp.stack(big["ffn1_out"]),
        "ffn2_w_in": jnp.stack(big["ffn2_in"]), "ffn2_w_out": jnp.stack(big["ffn2_out"]),
        "ret_w_in": big["ret_in"][0][None], "ret_w_out": big["ret_out"][0][None],
        "gla_w_in": big["gla_in"][0][None], "gla_w_out": big["gla_out"][0][None],
    }

    delta, new_m, new_v = {}, {}, {}
    for name in _BIG:
        shape = p[name].shape
        flat = lambda a: a.reshape(-1, shape[-1])
        out = _adamw(flat(p[name]), flat(grads[name]), flat(m[name]), flat(v[name]), f"adamw_{name}")
        delta[name], new_m[name], new_v[name] = [a.reshape(shape) for a in out]
    packed = [_pack_rows([d[name] for name in _LOCAL_SMALL], 128) for d in (p, grads, m, v)]
    out = _adamw(*packed, "adamw_small")
    shapes = [p[name].shape for name in _LOCAL_SMALL]
    for d, a in zip((delta, new_m, new_v), out):
        d.update(zip(_LOCAL_SMALL, _unpack_rows(a, shapes)))

    return (s_loss.reshape(()), d_x[None], *[grads[n] for n in _WEIGHTS], *[delta[n] for n in _WEIGHTS],
            *[new_m[n] for n in _WEIGHTS], *[new_v[n] for n in _WEIGHTS])
```

```python
import functools

import jax
import jax.numpy as jnp
from jax import lax
from jax.experimental import pallas as pl
from jax.experimental.pallas import tpu as pltpu

F32, BF16 = jnp.float32, jnp.bfloat16
MESH = pl.DeviceIdType.MESH

D = 1024
N_META = 16
CHUNK = 64
FRONT = 256
D_FF = 2816
EPS = 1e-6
HEADS = 4
RET_DK, RET_DV = 256, 512
GLA_DK, GLA_DV = 128, 256
GLA_RANK = 16
GLA_TAU = 16.0
GLA_IN = 2 * HEADS * GLA_DK + 2 * HEADS * GLA_DV + GLA_RANK
GLA_U = 3200
ROPE_BASE = 10000.0
N_CHIPS = 4
N_DEV = 8

ADAM_LR, ADAM_B1, ADAM_B2, ADAM_EPS, ADAM_WD, ADAM_STEP = 0.001, 0.9, 0.999, 1e-08, 0.01, 10

VMEM_LIMIT_BYTES = 56 * 1024 * 1024
TM = 768
TM_SMALL = 256


TM_RESIDENT = 384


def _cp(n_axes):
    return pltpu.CompilerParams(dimension_semantics=("arbitrary",) * n_axes, vmem_limit_bytes=VMEM_LIMIT_BYTES)


def _resident(shape, n_axes):
    zeros = (0,) * len(shape)
    index = (lambda i: zeros) if n_axes == 1 else (lambda i, j: zeros)
    return pl.BlockSpec(shape, index, pipeline_mode=pl.Buffered(1))


def _dg(a, b, ca, cb):
    nb = a.ndim - 2
    dims = (((ca + nb,), (cb + nb,)), (tuple(range(nb)), tuple(range(nb))))
    return lax.dot_general(a.astype(BF16), b.astype(BF16), dims, preferred_element_type=F32)


@jax.custom_vjp
def _nn(a, b):
    return _dg(a, b, 1, 0)


@jax.custom_vjp
def _nt(a, b):
    return _dg(a, b, 1, 1)


@jax.custom_vjp
def _tn(a, b):
    return _dg(a, b, 0, 0)


_nn.defvjp(lambda a, b: (_nn(a, b), (a, b)), lambda res, g: (_nt(g, res[1]), _tn(res[0], g)))
_nt.defvjp(lambda a, b: (_nt(a, b), (a, b)), lambda res, g: (_nn(g, res[1]), _tn(g, res[0])))
_tn.defvjp(lambda a, b: (_tn(a, b), (a, b)), lambda res, g: (_nt(res[1], g), _nn(res[0], g)))


def _split3_dot(m, a):
    a1 = a.astype(BF16)
    r1 = a - a1.astype(F32)
    a2 = r1.astype(BF16)
    a3 = (r1 - a2.astype(F32)).astype(BF16)
    mb = jnp.broadcast_to(m, a.shape[:-2] + m.shape)
    return _dg(mb, a1, 1, 0) + _dg(mb, a2, 1, 0) + _dg(mb, a3, 1, 0)


@jax.custom_vjp
def _cum(m, mt, a):
    return _split3_dot(m, a)


_cum.defvjp(lambda m, mt, a: (_split3_dot(m, a), (m, mt)),
            lambda res, g: (jnp.zeros_like(res[0]), jnp.zeros_like(res[1]), _split3_dot(res[1], g)))


def _sigmoid(x):
    return 1.0 / (1.0 + jnp.exp(-x))


def _rms(x):
    return lax.rsqrt(jnp.mean(x * x, axis=-1, keepdims=True) + EPS)


def _rmsnorm_bwd(dy, x, gain):
    r = _rms(x)
    xhat = x * r
    dxh = dy * gain
    return r * (dxh - xhat * jnp.mean(dxh * xhat, axis=-1, keepdims=True)), xhat


def _norm_proj(h, gain, w, name):
    tp, d = h.shape
    s, _, ns = w.shape

    tm = TM_RESIDENT

    def body(h_ref, g_ref, w_ref, hn_ref, u_ref):
        @pl.when(pl.program_id(1) == 0)
        def _():
            x = h_ref[...]
            hn_ref[...] = (x * _rms(x) * g_ref[...]).astype(BF16)

        u_ref[...] = jnp.dot(hn_ref[...], w_ref[pl.program_id(1)], preferred_element_type=F32).astype(BF16)

    return pl.pallas_call(
        body, name=name, grid=(tp // tm, s),
        in_specs=[pl.BlockSpec((tm, d), lambda i, j: (i, 0)), pl.BlockSpec((1, d), lambda i, j: (0, 0)),
                  _resident(w.shape, 2)],
        out_specs=[pl.BlockSpec((tm, d), lambda i, j: (i, 0)), pl.BlockSpec((tm, ns), lambda i, j: (i, j))],
        out_shape=[jax.ShapeDtypeStruct((tp, d), BF16), jax.ShapeDtypeStruct((tp, s * ns), BF16)],
        compiler_params=_cp(2))(h, gain, w)


def _norm_ffn_in(h, gain, w, name):
    tp, d = h.shape
    s, _, ns = w.shape
    half = s // 2
    tm = TM_RESIDENT

    def body(h_ref, g_ref, w_ref, hn_ref, dg_ref, du_ref, act_ref):
        j = pl.program_id(1)

        @pl.when(j == 0)
        def _():
            x = h_ref[...]
            hn_ref[...] = (x * _rms(x) * g_ref[...]).astype(BF16)

        a = hn_ref[...]
        g = jnp.dot(a, w_ref[j], preferred_element_type=F32)
        u = jnp.dot(a, w_ref[j + half], preferred_element_type=F32)
        sg = _sigmoid(g)
        silu = g * sg
        dg_ref[...] = (u * (sg + silu * (1.0 - sg))).astype(BF16)
        du_ref[...] = silu.astype(BF16)
        act_ref[...] = (silu * u).astype(BF16)

    wide = jax.ShapeDtypeStruct((tp, half * ns), BF16)
    return pl.pallas_call(
        body, name=name, grid=(tp // tm, half),
        in_specs=[pl.BlockSpec((tm, d), lambda i, j: (i, 0)), pl.BlockSpec((1, d), lambda i, j: (0, 0)),
                  _resident(w.shape, 2)],
        out_specs=[pl.BlockSpec((tm, d), lambda i, j: (i, 0))] + [pl.BlockSpec((tm, ns), lambda i, j: (i, j))] * 3,
        out_shape=[jax.ShapeDtypeStruct((tp, d), BF16), wide, wide, wide],
        compiler_params=_cp(2))(h, gain, w)


def _out_proj(a, w, h, scale, name):
    tp, k = a.shape
    d = w.shape[1]

    def body(a_ref, w_ref, h_ref, o_ref):
        o_ref[...] = h_ref[...] + scale * jnp.dot(a_ref[...], w_ref[...], preferred_element_type=F32)

    return pl.pallas_call(
        body, name=name, grid=(tp // TM,),
        in_specs=[pl.BlockSpec((TM, k), lambda i: (i, 0)), pl.BlockSpec((k, d), lambda i: (0, 0)),
                  pl.BlockSpec((TM, d), lambda i: (i, 0))],
        out_specs=pl.BlockSpec((TM, d), lambda i: (i, 0)),
        out_shape=jax.ShapeDtypeStruct((tp, d), F32),
        compiler_params=_cp(1))(a, w, h)


def _ffn_dact(dh, w_out, act_dg, act_du, name):
    tp, d = dh.shape
    ff = w_out.shape[0]
    tm = TM_RESIDENT

    def body(dh_ref, w_ref, dg_ref, du_ref, o_ref):
        dy = (0.5 * dh_ref[...]).astype(BF16)
        dact = lax.dot_general(dy, w_ref[...], (((1,), (1,)), ((), ())), preferred_element_type=F32)
        o_ref[:, :ff] = (dact * dg_ref[...].astype(F32)).astype(BF16)
        o_ref[:, ff:] = (dact * du_ref[...].astype(F32)).astype(BF16)

    return pl.pallas_call(
        body, name=name, grid=(tp // tm,),
        in_specs=[pl.BlockSpec((tm, d), lambda i: (i, 0)), _resident(w_out.shape, 1),
                  pl.BlockSpec((tm, ff), lambda i: (i, 0)), pl.BlockSpec((tm, ff), lambda i: (i, 0))],
        out_specs=pl.BlockSpec((tm, 2 * ff), lambda i: (i, 0)),
        out_shape=jax.ShapeDtypeStruct((tp, 2 * ff), BF16),
        compiler_params=_cp(1))(dh, w_out, act_dg, act_du)


def _dgrad(dh, w, name):
    tp, d = dh.shape
    k = w.shape[0]

    def body(dh_ref, w_ref, o_ref):
        o_ref[...] = lax.dot_general(dh_ref[...].astype(BF16), w_ref[...], (((1,), (1,)), ((), ())),
                                     preferred_element_type=F32).astype(BF16)

    return pl.pallas_call(
        body, name=name, grid=(tp // TM,),
        in_specs=[pl.BlockSpec((TM, d), lambda i: (i, 0)), pl.BlockSpec((k, d), lambda i: (0, 0))],
        out_specs=pl.BlockSpec((TM, k), lambda i: (i, 0)),
        out_shape=jax.ShapeDtypeStruct((tp, k), BF16),
        compiler_params=_cp(1))(dh, w)


def _wgrad(a, b, *, bm, bn, scale, sharded, name):
    tp, m = a.shape
    n = b.shape[1]
    nk = tp // TM

    def body(a_ref, b_ref, o_ref, acc_ref):
        k = pl.program_id(2)

        @pl.when(k == 0)
        def _():
            acc_ref[...] = jnp.zeros_like(acc_ref)

        bb = b_ref[...]
        if scale != 1.0:
            bb = scale * bb
        acc_ref[...] += lax.dot_general(a_ref[...], bb.astype(BF16), (((0,), (0,)), ((), ())),
                                        preferred_element_type=F32)

        @pl.when(k == nk - 1)
        def _():
            o_ref[...] = acc_ref[...].astype(BF16)

    if sharded:
        assert m == bm
        out_spec = pl.BlockSpec((None, bm, bn), lambda i, j, k: (j, 0, 0))
        out_shape = jax.ShapeDtypeStruct((n // bn, m, bn), BF16)
    else:
        out_spec = pl.BlockSpec((bm, bn), lambda i, j, k: (i, j))
        out_shape = jax.ShapeDtypeStruct((m, n), BF16)
    return pl.pallas_call(
        body, name=name, grid=(m // bm, n // bn, nk),
        in_specs=[pl.BlockSpec((TM, bm), lambda i, j, k: (k, i)), pl.BlockSpec((TM, bn), lambda i, j, k: (k, j))],
        out_specs=out_spec, out_shape=out_shape,
        scratch_shapes=[pltpu.VMEM((bm, bn), F32)],
        compiler_params=_cp(3))(a, b)


def _dgrad_norm(du, w, h, gain, dh_out, name):
    tp, d = h.shape
    s, _, ns = w.shape
    tm = TM_RESIDENT

    def body(du_ref, w_ref, h_ref, g_ref, dho_ref, dhi_ref, dg_ref):
        @pl.when(pl.program_id(0) == 0)
        def _():
            dg_ref[...] = jnp.zeros_like(dg_ref)

        dhn = None
        for k in range(s):
            part = lax.dot_general(du_ref[:, ns * k:ns * (k + 1)], w_ref[k], (((1,), (1,)), ((), ())),
                                   preferred_element_type=F32)
            dhn = part if dhn is None else dhn + part
        dx, xhat = _rmsnorm_bwd(dhn, h_ref[...], g_ref[...])
        dg_ref[...] += jnp.sum(dhn * xhat, axis=0, keepdims=True)
        dhi_ref[...] = dho_ref[...] + dx

    return pl.pallas_call(
        body, name=name, grid=(tp // tm,),
        in_specs=[pl.BlockSpec((tm, s * ns), lambda i: (i, 0)), _resident(w.shape, 1),
                  pl.BlockSpec((tm, d), lambda i: (i, 0)), pl.BlockSpec((1, d), lambda i: (0, 0)),
                  pl.BlockSpec((tm, d), lambda i: (i, 0))],
        out_specs=[pl.BlockSpec((tm, d), lambda i: (i, 0)), pl.BlockSpec((1, d), lambda i: (0, 0))],
        out_shape=[jax.ShapeDtypeStruct((tp, d), F32), jax.ShapeDtypeStruct((1, d), F32)],
        compiler_params=_cp(1))(du, w, h, gain, dh_out)


def _loss_head(h, gain, target, name):
    tp, d = h.shape
    tm = TM_SMALL
    front_tiles = FRONT // tm

    def body(h_ref, g_ref, t_ref, dh_ref, dg_ref, loss_ref):
        i = pl.program_id(0)

        @pl.when(i == 0)
        def _():
            dg_ref[...] = jnp.zeros_like(dg_ref)
            loss_ref[...] = jnp.zeros_like(loss_ref)

        x = h_ref[...]
        gain_v = g_ref[...]
        y = x * _rms(x) * gain_v
        err = jnp.where(i >= front_tiles, y - t_ref[...], 0.0)
        loss_ref[...] += 0.5 * jnp.sum(jnp.mean(err * err, axis=-1, keepdims=True), axis=0, keepdims=True)
        dy = err * (1.0 / d)
        dx, xhat = _rmsnorm_bwd(dy, x, gain_v)
        dg_ref[...] += jnp.sum(dy * xhat, axis=0, keepdims=True)
        dh_ref[...] = dx

    return pl.pallas_call(
        body, name=name, grid=(tp // tm,),
        in_specs=[pl.BlockSpec((tm, d), lambda i: (i, 0)), pl.BlockSpec((1, d), lambda i: (0, 0)),
                  pl.BlockSpec((tm, d), lambda i: (jnp.maximum(i - front_tiles, 0), 0))],
        out_specs=[pl.BlockSpec((tm, d), lambda i: (i, 0)), pl.BlockSpec((1, d), lambda i: (0, 0)),
                   pl.BlockSpec((1, 128), lambda i: (0, 0))],
        out_shape=[jax.ShapeDtypeStruct((tp, d), F32), jax.ShapeDtypeStruct((1, d), F32),
                   jax.ShapeDtypeStruct((1, 128), F32)],
        compiler_params=_cp(1))(h, gain, target)


def _gated_headnorm(o, g, gain):
    return o * _rms(o) * gain * (g * _sigmoid(g))


def _row_mask(chunk):
    rows = chunk * CHUNK + lax.broadcasted_iota(jnp.int32, (CHUNK, 1), 0)
    return (rows >= FRONT - N_META).astype(F32)


def _ret_head(q1, q2, k1, k2, v, g, state, gain, cos, sin, dmat, dq, dk, dc):
    q = jnp.concatenate([q1 * cos - q2 * sin, q1 * sin + q2 * cos], axis=-1)
    k = jnp.concatenate([k1 * cos - k2 * sin, k1 * sin + k2 * cos], axis=-1) * (RET_DK ** -0.5)
    scores = _nt(q, k) * dmat
    o = _nn(scores, v) + _nn(q * dq, state)
    new_state = state * dc + _tn(k * dk, v)
    return _gated_headnorm(o, g, gain), new_state


def _ret_consts():
    log_gamma = jnp.log1p(-2.0 ** (-5.0 - jnp.arange(HEADS, dtype=F32)))
    idx = jnp.arange(CHUNK, dtype=F32)
    rel = idx[:, None] - idx[None, :]
    dmat = jnp.where(rel >= 0, jnp.exp(log_gamma[:, None, None] * jnp.maximum(rel, 0.0)), 0.0)
    dq = jnp.exp(log_gamma[:, None] * (idx + 1.0))[..., None]
    dk = jnp.exp(log_gamma[:, None] * (CHUNK - 1.0 - idx))[..., None]
    dc = jnp.broadcast_to(jnp.exp(log_gamma * CHUNK)[:, None, None], (HEADS, 1, 128))
    return dmat, dq, dk, dc


def _rope_tables(tp):
    half = RET_DK // 2
    inv = 1.0 / (ROPE_BASE ** jnp.linspace(0.0, 1.0, half, dtype=F32))
    pos = (jnp.arange(tp) - (FRONT - N_META)).astype(F32)
    ang = pos[:, None] * inv[None, :]
    return jnp.cos(ang), jnp.sin(ang)


_RET_V0, _RET_G0 = 2 * D, 4 * D


def _heads(ref, start, width, stride=None):
    stride = width if stride is None else stride
    return jnp.stack([ref[:, start + stride * h:start + stride * h + width].astype(F32) for h in range(HEADS)])


def _put_heads(ref, start, value, mask, stride=None):
    width = value.shape[-1]
    stride = width if stride is None else stride
    for h in range(HEADS):
        ref[:, start + stride * h:start + stride * h + width] = (value[h] * mask).astype(ref.dtype)


def _ret_pieces(u_ref):
    hk = RET_DK // 2
    return (_heads(u_ref, 0, hk, RET_DK), _heads(u_ref, hk, hk, RET_DK), _heads(u_ref, D, hk, RET_DK),
            _heads(u_ref, D + hk, hk, RET_DK), _heads(u_ref, _RET_V0, RET_DV), _heads(u_ref, _RET_G0, RET_DV))


def _ret_const_specs(rev=None):
    c = (lambda n: (rev(n), 0)) if rev else (lambda n: (n, 0))
    z3 = lambda n: (0, 0, 0)
    return [pl.BlockSpec((CHUNK, RET_DK // 2), c), pl.BlockSpec((CHUNK, RET_DK // 2), c),
            pl.BlockSpec((HEADS, CHUNK, CHUNK), z3), pl.BlockSpec((HEADS, CHUNK, 1), z3),
            pl.BlockSpec((HEADS, CHUNK, 1), z3), pl.BlockSpec((HEADS, 1, 128), z3)]


def _ret_fwd(u, gain, name):
    tp = u.shape[0]
    nch = tp // CHUNK
    cos, sin = _rope_tables(tp)
    dmat, dq, dk, dc = _ret_consts()

    def body(u_ref, gain_ref, cos_ref, sin_ref, dmat_ref, dq_ref, dk_ref, dc_ref, on_ref, st_ref, state_ref):
        @pl.when(pl.program_id(0) == 0)
        def _():
            state_ref[...] = jnp.zeros_like(state_ref)

        state = state_ref[...]
        st_ref[...] = state.astype(BF16)
        on, new_state = _ret_head(*_ret_pieces(u_ref), state, _heads(gain_ref, 0, RET_DV), cos_ref[...], sin_ref[...],
                                  dmat_ref[...], dq_ref[...], dk_ref[...], dc_ref[...][:, :, :1])
        state_ref[...] = new_state
        _put_heads(on_ref, 0, on, 1.0)

    return pl.pallas_call(
        body, name=name, grid=(nch,),
        in_specs=[pl.BlockSpec((CHUNK, 6 * D), lambda n: (n, 0)), pl.BlockSpec((1, HEADS * RET_DV), lambda n: (0, 0))]
                 + _ret_const_specs(),
        out_specs=[pl.BlockSpec((CHUNK, HEADS * RET_DV), lambda n: (n, 0)),
                   pl.BlockSpec((None, HEADS, RET_DK, RET_DV), lambda n: (n, 0, 0, 0))],
        out_shape=[jax.ShapeDtypeStruct((tp, HEADS * RET_DV), BF16),
                   jax.ShapeDtypeStruct((nch, HEADS, RET_DK, RET_DV), BF16)],
        scratch_shapes=[pltpu.VMEM((HEADS, RET_DK, RET_DV), F32)],
        compiler_params=_cp(1))(u, gain, cos, sin, dmat, dq, dk, dc)


def _ret_bwd(u, gain, states, d_on, name):
    tp = u.shape[0]
    nch = tp // CHUNK
    cos, sin = _rope_tables(tp)
    dmat, dq, dk, dc = _ret_consts()
    rev = lambda n: nch - 1 - n
    hk = RET_DK // 2

    def body(u_ref, gain_ref, st_ref, don_ref, cos_ref, sin_ref, dmat_ref, dq_ref, dk_ref, dc_ref,
             du_ref, dgain_ref, dstate_ref):
        @pl.when(pl.program_id(0) == 0)
        def _():
            dstate_ref[...] = jnp.zeros_like(dstate_ref)
            dgain_ref[...] = jnp.zeros_like(dgain_ref)

        mask = _row_mask(rev(pl.program_id(0)))
        consts = (cos_ref[...], sin_ref[...], dmat_ref[...], dq_ref[...], dk_ref[...], dc_ref[...][:, :, :1])
        _, vjp = jax.vjp(lambda *a: _ret_head(*a, *consts), *_ret_pieces(u_ref), st_ref[...].astype(F32),
                         _heads(gain_ref, 0, RET_DV))
        dq1, dq2, dk1, dk2, dv, dg, dstate, dgain = vjp((_heads(don_ref, 0, RET_DV), dstate_ref[...]))
        dstate_ref[...] = dstate
        for hd in range(HEADS):
            dgain_ref[:, RET_DV * hd:RET_DV * (hd + 1)] += dgain[hd]
        _put_heads(du_ref, 0, dq1, mask, RET_DK)
        _put_heads(du_ref, hk, dq2, mask, RET_DK)
        _put_heads(du_ref, D, dk1, mask, RET_DK)
        _put_heads(du_ref, D + hk, dk2, mask, RET_DK)
        _put_heads(du_ref, _RET_V0, dv, mask)
        _put_heads(du_ref, _RET_G0, dg, mask)

    return pl.pallas_call(
        body, name=name, grid=(nch,),
        in_specs=[pl.BlockSpec((CHUNK, 6 * D), lambda n: (rev(n), 0)),
                  pl.BlockSpec((1, HEADS * RET_DV), lambda n: (0, 0)),
                  pl.BlockSpec((None, HEADS, RET_DK, RET_DV), lambda n: (rev(n), 0, 0, 0)),
                  pl.BlockSpec((CHUNK, HEADS * RET_DV), lambda n: (rev(n), 0))] + _ret_const_specs(rev),
        out_specs=[pl.BlockSpec((CHUNK, 6 * D), lambda n: (rev(n), 0)),
                   pl.BlockSpec((1, HEADS * RET_DV), lambda n: (0, 0))],
        out_shape=[jax.ShapeDtypeStruct((tp, 6 * D), BF16), jax.ShapeDtypeStruct((1, HEADS * RET_DV), F32)],
        scratch_shapes=[pltpu.VMEM((HEADS, RET_DK, RET_DV), F32)],
        compiler_params=_cp(1))(u, gain, states, d_on, cos, sin, dmat, dq, dk, dc)


_GLA_K0, _GLA_V0, _GLA_G0, _GLA_Z0 = 512, 1024, 2048, 3072


def _gla_head(q, k, v, g, z, state_t, wg, bg, gain, mask, lo, lo_t, loc, loc_t):
    ga = _nn(jnp.broadcast_to(z, wg.shape[:-2] + z.shape), wg) + bg
    log_a = (jnp.minimum(ga, 0.0) - jnp.log(1.0 + jnp.exp(-jnp.abs(ga)))) * (mask * (1.0 / GLA_TAU))
    bcum = _cum(lo, lo_t, log_a)
    bmid = _cum(loc, loc_t, log_a)
    btot = jnp.sum(log_a, axis=-2, keepdims=True)
    qs = q * (GLA_DK ** -0.5)
    causal = lax.broadcasted_iota(jnp.int32, (CHUNK, CHUNK), 0) >= lax.broadcasted_iota(jnp.int32, (CHUNK, CHUNK), 1)
    scores = jnp.where(causal, _nt(qs * jnp.exp(bmid), k * jnp.exp(-bmid)), 0.0)
    o = _nn(scores, v) + _nt(qs * jnp.exp(bcum), state_t)
    new_state_t = state_t * jnp.exp(btot) + _tn(v, k * jnp.exp(btot - bcum))
    return _gated_headnorm(o, g, gain), new_state_t


def _cum_mats():
    r = lax.broadcasted_iota(jnp.int32, (CHUNK, CHUNK), 0)
    c = lax.broadcasted_iota(jnp.int32, (CHUNK, CHUNK), 1)
    mid = CHUNK // 2
    low = lambda a, b: (a >= b).astype(F32)
    lo, lo_t = low(r, c), low(c, r)
    loc = lo - (c <= mid).astype(F32)
    loc_t = lo_t - (r <= mid).astype(F32)
    return tuple(m.astype(BF16) for m in (lo, lo_t, loc, loc_t))


def _gla_pieces(u_ref):
    return (_heads(u_ref, 0, GLA_DK), _heads(u_ref, _GLA_K0, GLA_DK), _heads(u_ref, _GLA_V0, GLA_DV),
            _heads(u_ref, _GLA_G0, GLA_DV), u_ref[:, _GLA_Z0:].astype(F32))


def _gla_fwd(u, wg, bg, gain, name):
    tp = u.shape[0]
    nch = tp // CHUNK

    def body(u_ref, wg_ref, bg_ref, gain_ref, on_ref, st_ref, state_ref):
        @pl.when(pl.program_id(0) == 0)
        def _():
            state_ref[...] = jnp.zeros_like(state_ref)

        state = state_ref[...]
        st_ref[...] = state.astype(BF16)
        on, new_state = _gla_head(*_gla_pieces(u_ref), state, _heads(wg_ref, 0, GLA_DK), _heads(bg_ref, 0, GLA_DK),
                                  _heads(gain_ref, 0, GLA_DV), _row_mask(pl.program_id(0)), *_cum_mats())
        state_ref[...] = new_state
        _put_heads(on_ref, 0, on, 1.0)

    return pl.pallas_call(
        body, name=name, grid=(nch,),
        in_specs=[pl.BlockSpec((CHUNK, GLA_U), lambda n: (n, 0)), pl.BlockSpec((128, HEADS * GLA_DK), lambda n: (0, 0)),
                  pl.BlockSpec((1, HEADS * GLA_DK), lambda n: (0, 0)), pl.BlockSpec((1, HEADS * GLA_DV), lambda n: (0, 0))],
        out_specs=[pl.BlockSpec((CHUNK, HEADS * GLA_DV), lambda n: (n, 0)),
                   pl.BlockSpec((None, HEADS, GLA_DV, GLA_DK), lambda n: (n, 0, 0, 0))],
        out_shape=[jax.ShapeDtypeStruct((tp, HEADS * GLA_DV), BF16),
                   jax.ShapeDtypeStruct((nch, HEADS, GLA_DV, GLA_DK), BF16)],
        scratch_shapes=[pltpu.VMEM((HEADS, GLA_DV, GLA_DK), F32)],
        compiler_params=_cp(1))(u, wg, bg, gain)


def _gla_bwd(u, wg, bg, gain, states, d_on, name):
    tp = u.shape[0]
    nch = tp // CHUNK
    rev = lambda n: nch - 1 - n

    def body(u_ref, wg_ref, bg_ref, gain_ref, st_ref, don_ref, du_ref, dwg_ref, dbg_ref, dgain_ref, dstate_ref):
        @pl.when(pl.program_id(0) == 0)
        def _():
            dstate_ref[...] = jnp.zeros_like(dstate_ref)
            dwg_ref[...] = jnp.zeros_like(dwg_ref)
            dbg_ref[...] = jnp.zeros_like(dbg_ref)
            dgain_ref[...] = jnp.zeros_like(dgain_ref)

        mask = _row_mask(rev(pl.program_id(0)))
        mats = _cum_mats()
        _, vjp = jax.vjp(lambda *a: _gla_head(*a, mask, *mats), *_gla_pieces(u_ref), st_ref[...].astype(F32),
                         _heads(wg_ref, 0, GLA_DK), _heads(bg_ref, 0, GLA_DK), _heads(gain_ref, 0, GLA_DV))
        dq, dk, dv, dg, dz, dstate, dwg, dbg, dgain = vjp((_heads(don_ref, 0, GLA_DV), dstate_ref[...]))
        dstate_ref[...] = dstate
        for hd in range(HEADS):
            dwg_ref[:, GLA_DK * hd:GLA_DK * (hd + 1)] += dwg[hd]
            dbg_ref[:, GLA_DK * hd:GLA_DK * (hd + 1)] += dbg[hd]
            dgain_ref[:, GLA_DV * hd:GLA_DV * (hd + 1)] += dgain[hd]
        _put_heads(du_ref, 0, dq, mask)
        _put_heads(du_ref, _GLA_K0, dk, mask)
        _put_heads(du_ref, _GLA_V0, dv, mask)
        _put_heads(du_ref, _GLA_G0, dg, mask)
        du_ref[:, _GLA_Z0:] = dz.astype(BF16)

    full = lambda r, c: pl.BlockSpec((r, c), lambda n: (0, 0))
    return pl.pallas_call(
        body, name=name, grid=(nch,),
        in_specs=[pl.BlockSpec((CHUNK, GLA_U), lambda n: (rev(n), 0)), full(128, HEADS * GLA_DK),
                  full(1, HEADS * GLA_DK), full(1, HEADS * GLA_DV),
                  pl.BlockSpec((None, HEADS, GLA_DV, GLA_DK), lambda n: (rev(n), 0, 0, 0)),
                  pl.BlockSpec((CHUNK, HEADS * GLA_DV), lambda n: (rev(n), 0))],
        out_specs=[pl.BlockSpec((CHUNK, GLA_U), lambda n: (rev(n), 0)), full(128, HEADS * GLA_DK),
                   full(1, HEADS * GLA_DK), full(1, HEADS * GLA_DV)],
        out_shape=[jax.ShapeDtypeStruct((tp, GLA_U), BF16), jax.ShapeDtypeStruct((128, HEADS * GLA_DK), F32),
                   jax.ShapeDtypeStruct((1, HEADS * GLA_DK), F32), jax.ShapeDtypeStruct((1, HEADS * GLA_DV), F32)],
        scratch_shapes=[pltpu.VMEM((HEADS, GLA_DV, GLA_DK), F32)],
        compiler_params=_cp(1))(u, wg, bg, gain, states, d_on)


def _ffn_fwd(h, gain, w_in, w_out, tag):
    hn, ug, uu, act = _norm_ffn_in(h, gain, w_in, f"{tag}_in")
    if callable(w_out):
        w_out = w_out(act)
    return _out_proj(act, w_out, h, 0.5, f"{tag}_out"), (h, hn, ug, uu, act), w_out


def _ffn_bwd(dh, saved, gain, w_in, w_out, tag, push):
    h, hn, ug, uu, act = saved
    du = _ffn_dact(dh, w_out, ug, uu, f"{tag}_dact")
    d_w_out = _wgrad(act, dh, bm=D_FF // 2, bn=D, scale=0.5, sharded=False, name=f"{tag}_dwout")
    d_w_in = _wgrad(hn, du, bm=D, bn=w_in.shape[2], scale=1.0, sharded=True, name=f"{tag}_dwin")
    token = push([d_w_in, d_w_out])
    return _dgrad_norm(du, w_in, h, gain + token[0, 0], dh, f"{tag}_dnorm")


def _sequence_grads(x, target, p, weights, grads):
    row = lambda v, token: v.reshape(1, -1) + token[0, 0]
    gains = {}

    tok = weights.start(1, weights.start(0, None))
    w = weights.wait(0, tok)
    tok = weights.start(2, w["l0_ffn1_in"])
    h = jnp.concatenate([jnp.zeros((FRONT - N_META, D), F32), w["meta"], x], axis=0)
    gains["l0_ffn1"] = row(p["norm_ffn1"][0], tok)
    h, s1, w["l0_ffn1_out"] = _ffn_fwd(h, gains["l0_ffn1"], w["l0_ffn1_in"],
                                       lambda act: weights.wait(1, act)["l0_ffn1_out"], "l0_ffn1")
    w.update(weights.wait(2, h))
    tok = weights.start(3, w["ret_in"])
    gains["ret"] = row(p["norm_mix"][0], tok)
    hn, u = _norm_proj(h, gains["ret"], w["ret_in"], "ret_in")
    on, states = _ret_fwd(u, w["ret_gain"], "ret_fwd")
    h_mix = _out_proj(on, w["ret_out"], h, 1.0, "ret_out")
    s2 = (h, hn, u, on, states)
    w.update(weights.wait(3, h_mix))
    tok = weights.start(4, w["l0_ffn2_in"])
    gains["l0_ffn2"] = row(p["norm_ffn2"][0], tok)
    h, s3, _ = _ffn_fwd(h_mix, gains["l0_ffn2"], w["l0_ffn2_in"], w["l0_ffn2_out"], "l0_ffn2")
    saved = [(s1, s2, s3)]

    w.update(weights.wait(4, h))
    tok = weights.start(5, w["l1_ffn1_in"])
    gains["l1_ffn1"] = row(p["norm_ffn1"][1], tok)
    h, s1, _ = _ffn_fwd(h, gains["l1_ffn1"], w["l1_ffn1_in"], w["l1_ffn1_out"], "l1_ffn1")
    w.update(weights.wait(5, h))
    tok = weights.start(6, w["gla_out"])
    gains["gla"] = row(p["norm_mix"][1], tok)
    hn, u = _norm_proj(h, gains["gla"], w["gla_in"], "gla_in")
    on, states = _gla_fwd(u, w["gla_wg"], w["gla_bg"], w["gla_gain"], "gla_fwd")
    h_mix = _out_proj(on, w["gla_out"], h, 1.0, "gla_out")
    s2 = (h, hn, u, on, states)
    w.update(weights.wait(6, h_mix))
    gains["l1_ffn2"] = p["norm_ffn2"][1].reshape(1, -1)
    h, s3, _ = _ffn_fwd(h_mix, gains["l1_ffn2"], w["l1_ffn2_in"], w["l1_ffn2_out"], "l1_ffn2")
    saved.append((s1, s2, s3))

    dh, d_final, loss = _loss_head(h, p["final_norm"].reshape(1, -1), target, "loss_head")
    small = {"final_norm": d_final, "norm_ffn1": [None, None], "norm_mix": [None, None], "norm_ffn2": [None, None]}
    pusher = lambda k: functools.partial(grads.push, k)

    s1, s2, s3 = saved[1]
    dh, small["norm_ffn2"][1] = _ffn_bwd(dh, s3, gains["l1_ffn2"], w["l1_ffn2_in"], w["l1_ffn2_out"], "l1_ffn2",
                                         pusher(0))
    h_in, hn, u, on, states = s2
    d_on = _dgrad(dh, w["gla_out"], "gla_don")
    d_out = _wgrad(on, dh, bm=D, bn=D, scale=1.0, sharded=False, name="gla_dwout")
    du, small["gla_wg"], small["gla_bg"], small["gla_gain"] = _gla_bwd(u, w["gla_wg"], w["gla_bg"], w["gla_gain"],
                                                                       states, d_on, "gla_bwd")
    d_in = _wgrad(hn, du, bm=D, bn=GLA_U // 5, scale=1.0, sharded=False, name="gla_dwin")
    d_in = jnp.moveaxis(d_in[:, :GLA_IN].reshape(D, N_CHIPS, -1), 1, 0)
    tok = grads.push(1, [d_in, d_out])
    dh, small["norm_mix"][1] = _dgrad_norm(du, w["gla_in"], h_in, gains["gla"] + tok[0, 0], dh, "gla_dnorm")
    dh, small["norm_ffn1"][1] = _ffn_bwd(dh, s1, gains["l1_ffn1"], w["l1_ffn1_in"], w["l1_ffn1_out"], "l1_ffn1",
                                         pusher(2))

    s1, s2, s3 = saved[0]
    dh, small["norm_ffn2"][0] = _ffn_bwd(dh, s3, gains["l0_ffn2"], w["l0_ffn2_in"], w["l0_ffn2_out"], "l0_ffn2",
                                         pusher(3))
    h_in, hn, u, on, states = s2
    d_on = _dgrad(dh, w["ret_out"], "ret_don")
    d_out = _wgrad(on, dh, bm=D, bn=D, scale=1.0, sharded=False, name="ret_dwout")
    du, small["ret_gain"] = _ret_bwd(u, w["ret_gain"], states, d_on, "ret_bwd")
    d_in = _wgrad(hn, du, bm=D, bn=w["ret_in"].shape[2], scale=1.0, sharded=True, name="ret_dwin")
    tok = grads.push(4, [d_in, d_out])
    dh, small["norm_mix"][0] = _dgrad_norm(du, w["ret_in"], h_in, gains["ret"] + tok[0, 0], dh, "ret_dnorm")
    dh, small["norm_ffn1"][0] = _ffn_bwd(dh, s1, gains["l0_ffn1"], w["l0_ffn1_in"], w["l0_ffn1_out"], "l0_ffn1",
                                         pusher(5))
    grads.push(6, [], [dh[FRONT - N_META:FRONT], *small["norm_ffn1"], *small["norm_mix"], *small["norm_ffn2"],
                       small["final_norm"], small["ret_gain"], small["gla_wg"][:GLA_RANK], small["gla_bg"],
                       small["gla_gain"], loss[:, :1]])
    return dh[FRONT:]


_HBM = pl.BlockSpec(memory_space=pl.ANY)


def _place():
    return lax.axis_index("x"), lax.axis_index("y"), lax.axis_index("c")


def _flip(v, bit):
    return 1 - v if bit else v


DMA_CHUNK_BYTES = 128 * 1024


def _row_chunks(ref):
    rows, cols = ref.shape
    step = _row_tile(rows, max(16, DMA_CHUNK_BYTES // (cols * ref.dtype.itemsize)))
    return [pl.ds(a, step) for a in range(0, rows, step)]


def _whole(src, dst, send_sem, recv_sem, peer):
    return pltpu.make_async_remote_copy(src_ref=src, dst_ref=dst, send_sem=send_sem, recv_sem=recv_sem,
                                        device_id=peer, device_id_type=MESH)


def _send(src, dst, send_sem, recv_sem, peer):
    for rows in _row_chunks(src):
        _whole(src.at[rows], dst.at[rows], send_sem, recv_sem, peer).start()
    return _whole(src, dst, send_sem, recv_sem, peer)


_HBM_ONLY = pl.BlockSpec(memory_space=pltpu.HBM)
_SEMS = pl.BlockSpec(memory_space=pltpu.SEMAPHORE)
_SIDE_EFFECT = pltpu.CompilerParams(has_side_effects=pltpu.SideEffectType.DATAFLOW_SIDE_EFFECTING)
_CHIP_FLIPS = [(1, 0, 0), (0, 1, 0), (1, 1, 0)]
_PEER_FLIPS = [(fx, fy, fc) for fx in (0, 1) for fy in (0, 1) for fc in (0, 1)][1:]


def _zero_token():
    return jnp.zeros((8, 128), F32)


def _exchange_start(srcs, lands, route, flips, after, name):
    n = len(srcs)

    def body(*refs):
        src, land = refs[:n], refs[n:2 * n]
        send_sems, recv_sems, token = refs[2 * n + 1], refs[2 * n + 2], refs[-1]
        me = _place()
        for t in range(n):
            for j, flip in enumerate(flips):
                peer = tuple(_flip(v, f) for v, f in zip(me, flip))
                s, d = route(t, src[t], land[t], me, peer)
                _send(s, d, send_sems.at[t * len(flips) + j], recv_sems.at[t * len(flips) + j], peer)
        token[...] = jnp.zeros_like(token)

    hbm = lambda a: pltpu.HBM(a.shape, a.dtype)
    sems = pltpu.SemaphoreType.DMA((n * len(flips),))
    operands = [pltpu.with_memory_space_constraint(a, pltpu.HBM) for a in list(srcs) + list(lands)]
    out = pl.pallas_call(
        body, name=name, in_specs=[_HBM_ONLY] * (2 * n) + [_HBM],
        out_shape=(sems, sems, *[hbm(a) for a in operands], jax.ShapeDtypeStruct((8, 128), F32)),
        out_specs=(_SEMS, _SEMS, *[_HBM_ONLY] * (2 * n), pl.BlockSpec(memory_space=pltpu.VMEM)),
        input_output_aliases={i: 2 + i for i in range(2 * n)}, compiler_params=_SIDE_EFFECT,
    )(*operands, _zero_token() if after is None else after)
    return (out[0], out[1], out[2:2 + n], out[2 + n:2 + 2 * n]), out[-1]


def _exchange_wait(started, route, flips, after, name):
    send_sems, recv_sems, srcs, lands = started
    n = len(srcs)

    def body(*refs):
        src, land = refs[:n], refs[n:2 * n]
        send_sems, recv_sems = refs[2 * n], refs[2 * n + 1]
        me = _place()
        for t in range(n):
            for j, flip in enumerate(flips):
                peer = tuple(_flip(v, f) for v, f in zip(me, flip))
                s, d = route(t, src[t], land[t], me, peer)
                cp = _whole(s, d, send_sems.at[t * len(flips) + j], recv_sems.at[t * len(flips) + j], peer)
                cp.wait_send()
                cp.wait_recv()

    hbm = lambda a: pltpu.HBM(a.shape, a.dtype)
    out = pl.pallas_call(
        body, name=name, in_specs=[_HBM_ONLY] * (2 * n) + [_SEMS, _SEMS, _HBM],
        out_shape=tuple(hbm(a) for a in list(srcs) + list(lands)), out_specs=tuple([_HBM_ONLY] * (2 * n)),
        input_output_aliases={i: i for i in range(2 * n)}, compiler_params=_SIDE_EFFECT,
    )(*srcs, *lands, send_sems, recv_sems, after)
    return out[:n], out[n:]


def _gather_route(t, src, land, me, peer):
    return src, land.at[2 * me[0] + me[1]]


def _scatter_route(n_pieces):
    def route(t, src, land, me, peer):
        part = src.at[2 * peer[0] + peer[1], peer[2]] if t < n_pieces else src
        return part, land.at[4 * me[0] + 2 * me[1] + me[2]]

    return route


def _swap_cores(halves):
    n = len(halves)

    def body(*refs):
        src, dst = refs[:n], refs[n:2 * n]
        send_sems, recv_sems = refs[2 * n:]
        x, y, c = _place()
        copies = [_send(src[t], dst[t], send_sems.at[t], recv_sems.at[t], (x, y, 1 - c)) for t in range(n)]
        for cp in copies:
            cp.wait()

    got = pl.pallas_call(
        body, name="swap_cores", in_specs=[_HBM] * n, out_specs=[_HBM] * n,
        out_shape=[jax.ShapeDtypeStruct(a.shape, a.dtype) for a in halves],
        scratch_shapes=[pltpu.SemaphoreType.DMA((n,)), pltpu.SemaphoreType.DMA((n,))],
    )(*halves)
    south = lax.axis_index("c") == 0
    return [jnp.stack([jnp.where(south, a, b), jnp.where(south, b, a)]) for a, b in zip(halves, got)]


def _row_tile(rows, cap):
    fits = [t for t in range(16, cap + 1, 16) if rows % t == 0]
    return fits[-1] if fits else rows


def _sum_slots(a, name):
    _, r, c = a.shape
    tr = _row_tile(r, 384)

    def body(a_ref, o_ref):
        s = a_ref[0].astype(F32)
        for k in range(1, N_DEV):
            s = s + a_ref[k].astype(F32)
        o_ref[...] = s

    return pl.pallas_call(
        body, name=name, grid=(r // tr,),
        in_specs=[pl.BlockSpec((N_DEV, tr, c), lambda i: (0, i, 0))],
        out_specs=pl.BlockSpec((tr, c), lambda i: (i, 0)),
        out_shape=jax.ShapeDtypeStruct((r, c), F32),
        compiler_params=_cp(1))(a)


def _adamw(w, g, m, v, name):
    r, c = w.shape
    tr = _row_tile(r, 256)

    def body(w_ref, g_ref, m_ref, v_ref, d_ref, nm_ref, nv_ref):
        gv = g_ref[...]
        nm = ADAM_B1 * m_ref[...] + (1.0 - ADAM_B1) * gv
        nv = ADAM_B2 * v_ref[...] + (1.0 - ADAM_B2) * (gv * gv)
        m_hat = nm / (1.0 - ADAM_B1 ** ADAM_STEP)
        v_hat = nv / (1.0 - ADAM_B2 ** ADAM_STEP)
        d_ref[...] = -ADAM_LR * (m_hat / (jnp.sqrt(v_hat) + ADAM_EPS) + ADAM_WD * w_ref[...])
        nm_ref[...] = nm
        nv_ref[...] = nv

    spec = pl.BlockSpec((tr, c), lambda i: (i, 0))
    return pl.pallas_call(
        body, name=name, grid=(r // tr,), in_specs=[spec] * 4, out_specs=[spec] * 3,
        out_shape=[jax.ShapeDtypeStruct((r, c), F32)] * 3,
        compiler_params=_cp(1))(w, g, m, v)


_SMALL = ["meta_tokens", "ret_head_norm", "gla_w_gate", "gla_b_gate", "gla_head_norm"]
_LOCAL_SMALL = ["meta_tokens", "norm_ffn1", "norm_mix", "norm_ffn2", "ret_head_norm", "gla_w_gate", "gla_b_gate",
                "gla_head_norm", "final_norm"]
_BIG = ["ffn1_w_in", "ffn1_w_out", "ffn2_w_in", "ffn2_w_out", "ret_w_in", "ret_w_out", "gla_w_in", "gla_w_out"]
_WEIGHTS = ["meta_tokens", "norm_ffn1", "ffn1_w_in", "ffn1_w_out", "norm_mix", "norm_ffn2", "ffn2_w_in", "ffn2_w_out",
            "ret_w_in", "ret_head_norm", "ret_w_out", "gla_w_in", "gla_w_gate", "gla_b_gate", "gla_head_norm",
            "gla_w_out", "final_norm"]


def _pack_rows(arrays, width):
    flat = jnp.concatenate([a.reshape(-1) for a in arrays])
    pad = -flat.shape[0] % (8 * width)
    return jnp.pad(flat, (0, pad)).reshape(-1, width)


def _unpack_rows(packed, shapes):
    flat, out, at = packed.reshape(-1), [], 0
    for s in shapes:
        size = 1
        for dim in s:
            size *= dim
        out.append(flat[at:at + size].reshape(s))
        at += size
    return out


class _WeightGather:
    GROUPS = [("small", "l0_ffn1_in"), ("l0_ffn1_out",), ("ret_in", "ret_out"), ("l0_ffn2_in", "l0_ffn2_out"),
              ("l1_ffn1_in", "l1_ffn1_out"), ("gla_in", "gla_out"), ("l1_ffn2_in", "l1_ffn2_out")]

    def __init__(self, p):
        b = lambda a: a.astype(BF16)
        self.small_shapes = [p[name].shape for name in _SMALL]
        self.shards = {"small": _pack_rows([p[name] for name in _SMALL], 128), "ret_in": b(p["ret_w_in"][0]),
                       "ret_out": b(p["ret_w_out"][0]), "gla_in": b(p["gla_w_in"][0]), "gla_out": b(p["gla_w_out"][0])}
        for layer in range(2):
            for name in ("ffn1", "ffn2"):
                self.shards[f"l{layer}_{name}_in"] = b(p[f"{name}_w_in"][layer])
                self.shards[f"l{layer}_{name}_out"] = b(p[f"{name}_w_out"][layer])
        self.started = {}

    def start(self, k, after):
        shards = [self.shards[name] for name in self.GROUPS[k]]
        lands = [lax.empty((N_CHIPS,) + s.shape, s.dtype) for s in shards]
        self.started[k], token = _exchange_start(shards, lands, _gather_route, _CHIP_FLIPS, after, f"gather{k}_start")
        return token

    def wait(self, k, after):
        shards, got = _exchange_wait(self.started[k], _gather_route, _CHIP_FLIPS, after, f"gather{k}_wait")
        mine = 2 * lax.axis_index("x") + lax.axis_index("y")
        w = {}
        for name, g, s in zip(self.GROUPS[k], got, shards):
            g = lax.dynamic_update_index_in_dim(g, s, mine, 0)
            if name == "small":
                parts = zip(*[_unpack_rows(g[chip], self.small_shapes) for chip in range(N_CHIPS)])
                cat = lambda a: jnp.moveaxis(a, 0, -2).reshape(a.shape[1:-1] + (-1,))
                meta, ret_gain, wg, bg, gla_gain = [cat(jnp.stack(part)) for part in parts]
                w.update(meta=meta, ret_gain=ret_gain.reshape(1, -1), gla_bg=bg.reshape(1, -1),
                         gla_gain=gla_gain.reshape(1, -1),
                         gla_wg=jnp.pad(wg[0], ((0, 128 - GLA_RANK), (0, 0))).astype(BF16))
            elif name == "gla_in":
                full = jnp.moveaxis(g, 0, 1).reshape(D, -1)
                w[name] = jnp.pad(full, ((0, 0), (0, GLA_U - GLA_IN)))[None]
            elif name.endswith("_out"):
                w[name] = g.reshape(-1, g.shape[-1])
            else:
                w[name] = g
        return w


class _GradExchange:
    def __init__(self):
        self.started = []
        self.token = None
        self.small_shapes = None

    def push(self, k, arrays, small=None):
        srcs = [a.reshape(N_CHIPS, 2, -1, a.shape[-1]) for a in arrays]
        lands = [lax.empty((N_DEV,) + a.shape[2:], a.dtype) for a in srcs]
        if small is not None:
            self.small_shapes = [a.shape for a in small]
            srcs.append(_pack_rows(small, D))
            lands.append(lax.empty((N_DEV,) + srcs[-1].shape, F32))
        started, self.token = _exchange_start(srcs, lands, _scatter_route(len(arrays)), _PEER_FLIPS, None,
                                              f"scatter{k}_start")
        self.started.append((started, len(arrays)))
        return self.token

    def collect(self):
        x, y, c = _place()
        after, sums = self.token, []
        for k, (started, n_pieces) in enumerate(self.started):
            srcs, got = _exchange_wait(started, _scatter_route(n_pieces), _PEER_FLIPS, after, f"scatter{k}_wait")
            own = [a[2 * x + y, c] for a in srcs[:n_pieces]] + list(srcs[n_pieces:])
            got = [lax.dynamic_update_index_in_dim(g, a, 4 * x + 2 * y + c, 0) for g, a in zip(got, own)]
            sums.append([_sum_slots(a, f"sum{k}_{i}") for i, a in enumerate(got)])
            after = sums[-1][0]
        small = _unpack_rows(sums[-1].pop(), self.small_shapes)
        return sums, small


def kernel(x, meta_tokens, norm_ffn1, ffn1_w_in, ffn1_w_out, norm_mix, norm_ffn2, ffn2_w_in, ffn2_w_out, ret_w_in, ret_head_norm, ret_w_out, gla_w_in, gla_w_gate, gla_b_gate, gla_head_norm, gla_w_out, final_norm, loss_target, m_meta_tokens, m_norm_ffn1, m_ffn1_w_in, m_ffn1_w_out, m_norm_mix, m_norm_ffn2, m_ffn2_w_in, m_ffn2_w_out, m_ret_w_in, m_ret_head_norm, m_ret_w_out, m_gla_w_in, m_gla_w_gate, m_gla_b_gate, m_gla_head_norm, m_gla_w_out, m_final_norm, v_meta_tokens, v_norm_ffn1, v_ffn1_w_in, v_ffn1_w_out, v_norm_mix, v_norm_ffn2, v_ffn2_w_in, v_ffn2_w_out, v_ret_w_in, v_ret_head_norm, v_ret_w_out, v_gla_w_in, v_gla_w_gate, v_gla_b_gate, v_gla_head_norm, v_gla_w_out, v_final_norm):
    p = dict(meta_tokens=meta_tokens, norm_ffn1=norm_ffn1, ffn1_w_in=ffn1_w_in, ffn1_w_out=ffn1_w_out, norm_mix=norm_mix,
             norm_ffn2=norm_ffn2, ffn2_w_in=ffn2_w_in, ffn2_w_out=ffn2_w_out, ret_w_in=ret_w_in,
             ret_head_norm=ret_head_norm, ret_w_out=ret_w_out, gla_w_in=gla_w_in, gla_w_gate=gla_w_gate,
             gla_b_gate=gla_b_gate, gla_head_norm=gla_head_norm, gla_w_out=gla_w_out, final_norm=final_norm)
    m = dict(zip(_WEIGHTS, (m_meta_tokens, m_norm_ffn1, m_ffn1_w_in, m_ffn1_w_out, m_norm_mix, m_norm_ffn2, m_ffn2_w_in,
                            m_ffn2_w_out, m_ret_w_in, m_ret_head_norm, m_ret_w_out, m_gla_w_in, m_gla_w_gate,
                            m_gla_b_gate, m_gla_head_norm, m_gla_w_out, m_final_norm)))
    v = dict(zip(_WEIGHTS, (v_meta_tokens, v_norm_ffn1, v_ffn1_w_in, v_ffn1_w_out, v_norm_mix, v_norm_ffn2, v_ffn2_w_in,
                            v_ffn2_w_out, v_ret_w_in, v_ret_head_norm, v_ret_w_out, v_gla_w_in, v_gla_w_gate,
                            v_gla_b_gate, v_gla_head_norm, v_gla_w_out, v_final_norm)))

    exchange = _GradExchange()
    d_x = _sequence_grads(x[0], loss_target[0], p, _WeightGather(p), exchange)
    sums, small = exchange.collect()
    names = [("ffn2_in", 1), ("ffn2_out", 1), ("gla_in", 0), ("gla_out", 0), ("ffn1_in", 1), ("ffn1_out", 1),
             ("ffn2_in", 0), ("ffn2_out", 0), ("ret_in", 0), ("ret_out", 0), ("ffn1_in", 0), ("ffn1_out", 0)]
    swapped = _swap_cores([a for group in sums for a in group])
    shard = {key: a.reshape(-1, a.shape[-1]) for key, a in zip(names, swapped)}
    big = {name: [shard[name, layer] for layer in range(2) if (name, layer) in shard] for name, _ in names}

    chip = 2 * lax.axis_index("x") + lax.axis_index("y")
    cols = lambda a, n: lax.dynamic_slice_in_dim(a, chip * n, n, axis=a.ndim - 1)
    (s_meta, s_n1a, s_n1b, s_nma, s_nmb, s_n2a, s_n2b, s_final, s_ret_gain, s_wg, s_bg, s_gla_gain, s_loss) = small
    grads = {
        "meta_tokens": cols(s_meta, 256), "norm_ffn1": jnp.concatenate([s_n1a, s_n1b]),
        "norm_mix": jnp.concatenate([s_nma, s_nmb]), "norm_ffn2": jnp.concatenate([s_n2a, s_n2b]),
        "final_norm": s_final.reshape(D),
        "ret_head_norm": cols(s_ret_gain.reshape(1, HEADS, RET_DV), RET_DV // N_CHIPS),
        "gla_w_gate": cols(s_wg, GLA_DK)[None], "gla_b_gate": cols(s_bg, GLA_DK),
        "gla_head_norm": cols(s_gla_gain.reshape(1, HEADS, GLA_DV), GLA_DV // N_CHIPS),
        "ffn1_w_in": jnp.stack(big["ffn1_in"]), "ffn1_w_out": jnp.stack(big["ffn1_out"]),
        "ffn2_w_in": jnp.stack(big["ffn2_in"]), "ffn2_w_out": jnp.stack(big["ffn2_out"]),
        "ret_w_in": big["ret_in"][0][None], "ret_w_out": big["ret_out"][0][None],
        "gla_w_in": big["gla_in"][0][None], "gla_w_out": big["gla_out"][0][None],
    }

    delta, new_m, new_v = {}, {}, {}
    for name in _BIG:
        shape = p[name].shape
        flat = lambda a: a.reshape(-1, shape[-1])
        out = _adamw(flat(p[name]), flat(grads[name]), flat(m[name]), flat(v[name]), f"adamw_{name}")
        delta[name], new_m[name], new_v[name] = [a.reshape(shape) for a in out]
    packed = [_pack_rows([d[name] for name in _LOCAL_SMALL], 128) for d in (p, grads, m, v)]
    out = _adamw(*packed, "adamw_small")
    shapes = [p[name].shape for name in _LOCAL_SMALL]
    for d, a in zip((delta, new_m, new_v), out):
        d.update(zip(_LOCAL_SMALL, _unpack_rows(a, shapes)))

    return (s_loss.reshape(()), d_x[None], *[grads[n] for n in _WEIGHTS], *[delta[n] for n in _WEIGHTS],
            *[new_m[n] for n in _WEIGHTS], *[new_v[n] for n in _WEIGHTS])
```

```python
import functools

import jax
import jax.numpy as jnp
from jax import lax
from jax.experimental import pallas as pl
from jax.experimental.pallas import tpu as pltpu

F32, BF16 = jnp.float32, jnp.bfloat16
MESH = pl.DeviceIdType.MESH

D = 1024
N_META = 16
CHUNK = 64
RET_CHUNK = 128
FRONT = 256
D_FF = 2816
EPS = 1e-6
HEADS = 4
RET_DK, RET_DV = 256, 512
GLA_DK, GLA_DV = 128, 256
GLA_RANK = 16
GLA_TAU = 16.0
GLA_IN = 2 * HEADS * GLA_DK + 2 * HEADS * GLA_DV + GLA_RANK
GLA_U = 3200
ROPE_BASE = 10000.0
N_CHIPS = 4
N_DEV = 8

ADAM_LR, ADAM_B1, ADAM_B2, ADAM_EPS, ADAM_WD, ADAM_STEP = 0.001, 0.9, 0.999, 1e-08, 0.01, 10

VMEM_LIMIT_BYTES = 56 * 1024 * 1024
TM = 768
TM_SMALL = 256


TM_RESIDENT = 384


def _cp(n_axes):
    return pltpu.CompilerParams(dimension_semantics=("arbitrary",) * n_axes, vmem_limit_bytes=VMEM_LIMIT_BYTES)


def _resident(shape, n_axes):
    zeros = (0,) * len(shape)
    index = (lambda i: zeros) if n_axes == 1 else (lambda i, j: zeros)
    return pl.BlockSpec(shape, index, pipeline_mode=pl.Buffered(1))


def _dg(a, b, ca, cb):
    nb = a.ndim - 2
    dims = (((ca + nb,), (cb + nb,)), (tuple(range(nb)), tuple(range(nb))))
    return lax.dot_general(a.astype(BF16), b.astype(BF16), dims, preferred_element_type=F32)


@jax.custom_vjp
def _nn(a, b):
    return _dg(a, b, 1, 0)


@jax.custom_vjp
def _nt(a, b):
    return _dg(a, b, 1, 1)


@jax.custom_vjp
def _tn(a, b):
    return _dg(a, b, 0, 0)


_nn.defvjp(lambda a, b: (_nn(a, b), (a, b)), lambda res, g: (_nt(g, res[1]), _tn(res[0], g)))
_nt.defvjp(lambda a, b: (_nt(a, b), (a, b)), lambda res, g: (_nn(g, res[1]), _tn(g, res[0])))
_tn.defvjp(lambda a, b: (_tn(a, b), (a, b)), lambda res, g: (_nt(res[1], g), _nn(res[0], g)))


def _split3_dot(m, a):
    a1 = a.astype(BF16)
    r1 = a - a1.astype(F32)
    a2 = r1.astype(BF16)
    a3 = (r1 - a2.astype(F32)).astype(BF16)
    mb = jnp.broadcast_to(m, a.shape[:-2] + m.shape)
    return _dg(mb, a1, 1, 0) + _dg(mb, a2, 1, 0) + _dg(mb, a3, 1, 0)


@jax.custom_vjp
def _cum(m, mt, a):
    return _split3_dot(m, a)


_cum.defvjp(lambda m, mt, a: (_split3_dot(m, a), (m, mt)),
            lambda res, g: (jnp.zeros_like(res[0]), jnp.zeros_like(res[1]), _split3_dot(res[1], g)))


def _sigmoid(x):
    return 1.0 / (1.0 + jnp.exp(-x))


def _rms(x):
    return lax.rsqrt(jnp.mean(x * x, axis=-1, keepdims=True) + EPS)


def _rmsnorm_bwd(dy, x, gain):
    r = _rms(x)
    xhat = x * r
    dxh = dy * gain
    return r * (dxh - xhat * jnp.mean(dxh * xhat, axis=-1, keepdims=True)), xhat


def _norm_proj(h, gain, w, name):
    tp, d = h.shape
    s, _, ns = w.shape

    tm = TM_RESIDENT

    def body(h_ref, g_ref, w_ref, hn_ref, u_ref):
        @pl.when(pl.program_id(1) == 0)
        def _():
            x = h_ref[...]
            hn_ref[...] = (x * _rms(x) * g_ref[...]).astype(BF16)

        u_ref[...] = jnp.dot(hn_ref[...], w_ref[pl.program_id(1)], preferred_element_type=F32).astype(BF16)

    return pl.pallas_call(
        body, name=name, grid=(tp // tm, s),
        in_specs=[pl.BlockSpec((tm, d), lambda i, j: (i, 0)), pl.BlockSpec((1, d), lambda i, j: (0, 0)),
                  _resident(w.shape, 2)],
        out_specs=[pl.BlockSpec((tm, d), lambda i, j: (i, 0)), pl.BlockSpec((tm, ns), lambda i, j: (i, j))],
        out_shape=[jax.ShapeDtypeStruct((tp, d), BF16), jax.ShapeDtypeStruct((tp, s * ns), BF16)],
        compiler_params=_cp(2))(h, gain, w)


def _norm_ffn_in(h, gain, w, name):
    tp, d = h.shape
    s, _, ns = w.shape
    half = s // 2
    tm = TM_RESIDENT

    def body(h_ref, g_ref, w_ref, hn_ref, dg_ref, du_ref, act_ref):
        j = pl.program_id(1)

        @pl.when(j == 0)
        def _():
            x = h_ref[...]
            hn_ref[...] = (x * _rms(x) * g_ref[...]).astype(BF16)

        a = hn_ref[...]
        g = jnp.dot(a, w_ref[j], preferred_element_type=F32)
        u = jnp.dot(a, w_ref[j + half], preferred_element_type=F32)
        sg = _sigmoid(g)
        silu = g * sg
        dg_ref[...] = (u * (sg + silu * (1.0 - sg))).astype(BF16)
        du_ref[...] = silu.astype(BF16)
        act_ref[...] = (silu * u).astype(BF16)

    wide = jax.ShapeDtypeStruct((tp, half * ns), BF16)
    return pl.pallas_call(
        body, name=name, grid=(tp // tm, half),
        in_specs=[pl.BlockSpec((tm, d), lambda i, j: (i, 0)), pl.BlockSpec((1, d), lambda i, j: (0, 0)),
                  _resident(w.shape, 2)],
        out_specs=[pl.BlockSpec((tm, d), lambda i, j: (i, 0))] + [pl.BlockSpec((tm, ns), lambda i, j: (i, j))] * 3,
        out_shape=[jax.ShapeDtypeStruct((tp, d), BF16), wide, wide, wide],
        compiler_params=_cp(2))(h, gain, w)


def _out_proj(a, w, h, scale, name):
    tp, k = a.shape
    d = w.shape[1]

    def body(a_ref, w_ref, h_ref, o_ref):
        o_ref[...] = h_ref[...] + scale * jnp.dot(a_ref[...], w_ref[...], preferred_element_type=F32)

    return pl.pallas_call(
        body, name=name, grid=(tp // TM,),
        in_specs=[pl.BlockSpec((TM, k), lambda i: (i, 0)), pl.BlockSpec((k, d), lambda i: (0, 0)),
                  pl.BlockSpec((TM, d), lambda i: (i, 0))],
        out_specs=pl.BlockSpec((TM, d), lambda i: (i, 0)),
        out_shape=jax.ShapeDtypeStruct((tp, d), F32),
        compiler_params=_cp(1))(a, w, h)


def _ffn_dact(dh, w_out, act_dg, act_du, name):
    tp, d = dh.shape
    ff = w_out.shape[0]
    tm = TM_RESIDENT

    def body(dh_ref, w_ref, dg_ref, du_ref, o_ref):
        dy = (0.5 * dh_ref[...]).astype(BF16)
        dact = lax.dot_general(dy, w_ref[...], (((1,), (1,)), ((), ())), preferred_element_type=F32)
        o_ref[:, :ff] = (dact * dg_ref[...].astype(F32)).astype(BF16)
        o_ref[:, ff:] = (dact * du_ref[...].astype(F32)).astype(BF16)

    return pl.pallas_call(
        body, name=name, grid=(tp // tm,),
        in_specs=[pl.BlockSpec((tm, d), lambda i: (i, 0)), _resident(w_out.shape, 1),
                  pl.BlockSpec((tm, ff), lambda i: (i, 0)), pl.BlockSpec((tm, ff), lambda i: (i, 0))],
        out_specs=pl.BlockSpec((tm, 2 * ff), lambda i: (i, 0)),
        out_shape=jax.ShapeDtypeStruct((tp, 2 * ff), BF16),
        compiler_params=_cp(1))(dh, w_out, act_dg, act_du)


def _dgrad(dh, w, name):
    tp, d = dh.shape
    k = w.shape[0]

    def body(dh_ref, w_ref, o_ref):
        o_ref[...] = lax.dot_general(dh_ref[...].astype(BF16), w_ref[...], (((1,), (1,)), ((), ())),
                                     preferred_element_type=F32).astype(BF16)

    return pl.pallas_call(
        body, name=name, grid=(tp // TM,),
        in_specs=[pl.BlockSpec((TM, d), lambda i: (i, 0)), pl.BlockSpec((k, d), lambda i: (0, 0))],
        out_specs=pl.BlockSpec((TM, k), lambda i: (i, 0)),
        out_shape=jax.ShapeDtypeStruct((tp, k), BF16),
        compiler_params=_cp(1))(dh, w)


def _wgrad(a, b, *, bm, bn, scale, sharded, name):
    tp, m = a.shape
    n = b.shape[1]
    nk = tp // TM

    def body(a_ref, b_ref, o_ref, acc_ref):
        k = pl.program_id(2)

        @pl.when(k == 0)
        def _():
            acc_ref[...] = jnp.zeros_like(acc_ref)

        bb = b_ref[...]
        if scale != 1.0:
            bb = scale * bb
        acc_ref[...] += lax.dot_general(a_ref[...], bb.astype(BF16), (((0,), (0,)), ((), ())),
                                        preferred_element_type=F32)

        @pl.when(k == nk - 1)
        def _():
            o_ref[...] = acc_ref[...].astype(BF16)

    if sharded:
        assert m == bm
        out_spec = pl.BlockSpec((None, bm, bn), lambda i, j, k: (j, 0, 0))
        out_shape = jax.ShapeDtypeStruct((n // bn, m, bn), BF16)
    else:
        out_spec = pl.BlockSpec((bm, bn), lambda i, j, k: (i, j))
        out_shape = jax.ShapeDtypeStruct((m, n), BF16)
    return pl.pallas_call(
        body, name=name, grid=(m // bm, n // bn, nk),
        in_specs=[pl.BlockSpec((TM, bm), lambda i, j, k: (k, i)), pl.BlockSpec((TM, bn), lambda i, j, k: (k, j))],
        out_specs=out_spec, out_shape=out_shape,
        scratch_shapes=[pltpu.VMEM((bm, bn), F32)],
        compiler_params=_cp(3))(a, b)


def _dgrad_norm(du, w, h, gain, dh_out, name):
    tp, d = h.shape
    s, _, ns = w.shape
    tm = TM_RESIDENT

    def body(du_ref, w_ref, h_ref, g_ref, dho_ref, dhi_ref, dg_ref):
        @pl.when(pl.program_id(0) == 0)
        def _():
            dg_ref[...] = jnp.zeros_like(dg_ref)

        dhn = None
        for k in range(s):
            part = lax.dot_general(du_ref[:, ns * k:ns * (k + 1)], w_ref[k], (((1,), (1,)), ((), ())),
                                   preferred_element_type=F32)
            dhn = part if dhn is None else dhn + part
        dx, xhat = _rmsnorm_bwd(dhn, h_ref[...], g_ref[...])
        dg_ref[...] += jnp.sum(dhn * xhat, axis=0, keepdims=True)
        dhi_ref[...] = dho_ref[...] + dx

    return pl.pallas_call(
        body, name=name, grid=(tp // tm,),
        in_specs=[pl.BlockSpec((tm, s * ns), lambda i: (i, 0)), _resident(w.shape, 1),
                  pl.BlockSpec((tm, d), lambda i: (i, 0)), pl.BlockSpec((1, d), lambda i: (0, 0)),
                  pl.BlockSpec((tm, d), lambda i: (i, 0))],
        out_specs=[pl.BlockSpec((tm, d), lambda i: (i, 0)), pl.BlockSpec((1, d), lambda i: (0, 0))],
        out_shape=[jax.ShapeDtypeStruct((tp, d), F32), jax.ShapeDtypeStruct((1, d), F32)],
        compiler_params=_cp(1))(du, w, h, gain, dh_out)


def _loss_head(h, gain, target, name):
    tp, d = h.shape
    tm = TM_SMALL
    front_tiles = FRONT // tm

    def body(h_ref, g_ref, t_ref, dh_ref, dg_ref, loss_ref):
        i = pl.program_id(0)

        @pl.when(i == 0)
        def _():
            dg_ref[...] = jnp.zeros_like(dg_ref)
            loss_ref[...] = jnp.zeros_like(loss_ref)

        x = h_ref[...]
        gain_v = g_ref[...]
        y = x * _rms(x) * gain_v
        err = jnp.where(i >= front_tiles, y - t_ref[...], 0.0)
        loss_ref[...] += 0.5 * jnp.sum(jnp.mean(err * err, axis=-1, keepdims=True), axis=0, keepdims=True)
        dy = err * (1.0 / d)
        dx, xhat = _rmsnorm_bwd(dy, x, gain_v)
        dg_ref[...] += jnp.sum(dy * xhat, axis=0, keepdims=True)
        dh_ref[...] = dx

    return pl.pallas_call(
        body, name=name, grid=(tp // tm,),
        in_specs=[pl.BlockSpec((tm, d), lambda i: (i, 0)), pl.BlockSpec((1, d), lambda i: (0, 0)),
                  pl.BlockSpec((tm, d), lambda i: (jnp.maximum(i - front_tiles, 0), 0))],
        out_specs=[pl.BlockSpec((tm, d), lambda i: (i, 0)), pl.BlockSpec((1, d), lambda i: (0, 0)),
                   pl.BlockSpec((1, 128), lambda i: (0, 0))],
        out_shape=[jax.ShapeDtypeStruct((tp, d), F32), jax.ShapeDtypeStruct((1, d), F32),
                   jax.ShapeDtypeStruct((1, 128), F32)],
        compiler_params=_cp(1))(h, gain, target)


def _gated_headnorm(o, g, gain):
    return o * _rms(o) * gain * (g * _sigmoid(g))


def _row_mask(chunk, size=CHUNK):
    rows = chunk * size + lax.broadcasted_iota(jnp.int32, (size, 1), 0)
    return (rows >= FRONT - N_META).astype(F32)


def _ret_head(q1, q2, k1, k2, v, g, state, gain, cos, sin, dmat, dq, dk, dc):
    q = jnp.concatenate([q1 * cos - q2 * sin, q1 * sin + q2 * cos], axis=-1)
    k = jnp.concatenate([k1 * cos - k2 * sin, k1 * sin + k2 * cos], axis=-1) * (RET_DK ** -0.5)
    scores = _nt(q, k) * dmat
    o = _nn(scores, v) + _nn(q * dq, state)
    new_state = state * dc + _tn(k * dk, v)
    return _gated_headnorm(o, g, gain), new_state


def _ret_consts():
    log_gamma = jnp.log1p(-2.0 ** (-5.0 - jnp.arange(HEADS, dtype=F32)))
    idx = jnp.arange(RET_CHUNK, dtype=F32)
    rel = idx[:, None] - idx[None, :]
    dmat = jnp.where(rel >= 0, jnp.exp(log_gamma[:, None, None] * jnp.maximum(rel, 0.0)), 0.0)
    dq = jnp.exp(log_gamma[:, None] * (idx + 1.0))[..., None]
    dk = jnp.exp(log_gamma[:, None] * (RET_CHUNK - 1.0 - idx))[..., None]
    dc = jnp.broadcast_to(jnp.exp(log_gamma * RET_CHUNK)[:, None, None], (HEADS, 1, 128))
    return dmat, dq, dk, dc


def _rope_tables(tp):
    half = RET_DK // 2
    inv = 1.0 / (ROPE_BASE ** jnp.linspace(0.0, 1.0, half, dtype=F32))
    pos = (jnp.arange(tp) - (FRONT - N_META)).astype(F32)
    ang = pos[:, None] * inv[None, :]
    return jnp.cos(ang), jnp.sin(ang)


_RET_V0, _RET_G0 = 2 * D, 4 * D


def _heads(ref, start, width, stride=None):
    stride = width if stride is None else stride
    return jnp.stack([ref[:, start + stride * h:start + stride * h + width].astype(F32) for h in range(HEADS)])


def _put_heads(ref, start, value, mask, stride=None):
    width = value.shape[-1]
    stride = width if stride is None else stride
    for h in range(HEADS):
        ref[:, start + stride * h:start + stride * h + width] = (value[h] * mask).astype(ref.dtype)


def _ret_pieces(u_ref):
    hk = RET_DK // 2
    return (_heads(u_ref, 0, hk, RET_DK), _heads(u_ref, hk, hk, RET_DK), _heads(u_ref, D, hk, RET_DK),
            _heads(u_ref, D + hk, hk, RET_DK), _heads(u_ref, _RET_V0, RET_DV), _heads(u_ref, _RET_G0, RET_DV))


def _ret_const_specs(rev=None):
    c = (lambda n: (rev(n), 0)) if rev else (lambda n: (n, 0))
    z3 = lambda n: (0, 0, 0)
    return [pl.BlockSpec((RET_CHUNK, RET_DK // 2), c), pl.BlockSpec((RET_CHUNK, RET_DK // 2), c),
            pl.BlockSpec((HEADS, RET_CHUNK, RET_CHUNK), z3), pl.BlockSpec((HEADS, RET_CHUNK, 1), z3),
            pl.BlockSpec((HEADS, RET_CHUNK, 1), z3), pl.BlockSpec((HEADS, 1, 128), z3)]


def _ret_fwd(u, gain, name):
    tp = u.shape[0]
    nch = tp // RET_CHUNK
    cos, sin = _rope_tables(tp)
    dmat, dq, dk, dc = _ret_consts()

    def body(u_ref, gain_ref, cos_ref, sin_ref, dmat_ref, dq_ref, dk_ref, dc_ref, on_ref, st_ref, state_ref):
        @pl.when(pl.program_id(0) == 0)
        def _():
            state_ref[...] = jnp.zeros_like(state_ref)

        state = state_ref[...]
        st_ref[...] = state.astype(BF16)
        on, new_state = _ret_head(*_ret_pieces(u_ref), state, _heads(gain_ref, 0, RET_DV), cos_ref[...], sin_ref[...],
                                  dmat_ref[...], dq_ref[...], dk_ref[...], dc_ref[...][:, :, :1])
        state_ref[...] = new_state
        _put_heads(on_ref, 0, on, 1.0)

    return pl.pallas_call(
        body, name=name, grid=(nch,),
        in_specs=[pl.BlockSpec((RET_CHUNK, 6 * D), lambda n: (n, 0)), pl.BlockSpec((1, HEADS * RET_DV), lambda n: (0, 0))]
                 + _ret_const_specs(),
        out_specs=[pl.BlockSpec((RET_CHUNK, HEADS * RET_DV), lambda n: (n, 0)),
                   pl.BlockSpec((None, HEADS, RET_DK, RET_DV), lambda n: (n, 0, 0, 0))],
        out_shape=[jax.ShapeDtypeStruct((tp, HEADS * RET_DV), BF16),
                   jax.ShapeDtypeStruct((nch, HEADS, RET_DK, RET_DV), BF16)],
        scratch_shapes=[pltpu.VMEM((HEADS, RET_DK, RET_DV), F32)],
        compiler_params=_cp(1))(u, gain, cos, sin, dmat, dq, dk, dc)


def _ret_bwd(u, gain, states, d_on, name):
    tp = u.shape[0]
    nch = tp // RET_CHUNK
    cos, sin = _rope_tables(tp)
    dmat, dq, dk, dc = _ret_consts()
    rev = lambda n: nch - 1 - n
    hk = RET_DK // 2

    def body(u_ref, gain_ref, st_ref, don_ref, cos_ref, sin_ref, dmat_ref, dq_ref, dk_ref, dc_ref,
             du_ref, dgain_ref, dstate_ref):
        @pl.when(pl.program_id(0) == 0)
        def _():
            dstate_ref[...] = jnp.zeros_like(dstate_ref)
            dgain_ref[...] = jnp.zeros_like(dgain_ref)

        mask = _row_mask(rev(pl.program_id(0)), RET_CHUNK)
        consts = (cos_ref[...], sin_ref[...], dmat_ref[...], dq_ref[...], dk_ref[...], dc_ref[...][:, :, :1])
        _, vjp = jax.vjp(lambda *a: _ret_head(*a, *consts), *_ret_pieces(u_ref), st_ref[...].astype(F32),
                         _heads(gain_ref, 0, RET_DV))
        dq1, dq2, dk1, dk2, dv, dg, dstate, dgain = vjp((_heads(don_ref, 0, RET_DV), dstate_ref[...]))
        dstate_ref[...] = dstate
        for hd in range(HEADS):
            dgain_ref[:, RET_DV * hd:RET_DV * (hd + 1)] += dgain[hd]
        _put_heads(du_ref, 0, dq1, mask, RET_DK)
        _put_heads(du_ref, hk, dq2, mask, RET_DK)
        _put_heads(du_ref, D, dk1, mask, RET_DK)
        _put_heads(du_ref, D + hk, dk2, mask, RET_DK)
        _put_heads(du_ref, _RET_V0, dv, mask)
        _put_heads(du_ref, _RET_G0, dg, mask)

    return pl.pallas_call(
        body, name=name, grid=(nch,),
        in_specs=[pl.BlockSpec((RET_CHUNK, 6 * D), lambda n: (rev(n), 0)),
                  pl.BlockSpec((1, HEADS * RET_DV), lambda n: (0, 0)),
                  pl.BlockSpec((None, HEADS, RET_DK, RET_DV), lambda n: (rev(n), 0, 0, 0)),
                  pl.BlockSpec((RET_CHUNK, HEADS * RET_DV), lambda n: (rev(n), 0))] + _ret_const_specs(rev),
        out_specs=[pl.BlockSpec((RET_CHUNK, 6 * D), lambda n: (rev(n), 0)),
                   pl.BlockSpec((1, HEADS * RET_DV), lambda n: (0, 0))],
        out_shape=[jax.ShapeDtypeStruct((tp, 6 * D), BF16), jax.ShapeDtypeStruct((1, HEADS * RET_DV), F32)],
        scratch_shapes=[pltpu.VMEM((HEADS, RET_DK, RET_DV), F32)],
        compiler_params=_cp(1))(u, gain, states, d_on, cos, sin, dmat, dq, dk, dc)


_GLA_K0, _GLA_V0, _GLA_G0, _GLA_Z0 = 512, 1024, 2048, 3072


def _gla_head(q, k, v, g, z, state_t, wg, bg, gain, mask, lo, lo_t, loc, loc_t):
    ga = _nn(jnp.broadcast_to(z, wg.shape[:-2] + z.shape), wg) + bg
    log_a = (jnp.minimum(ga, 0.0) - jnp.log(1.0 + jnp.exp(-jnp.abs(ga)))) * (mask * (1.0 / GLA_TAU))
    bcum = _cum(lo, lo_t, log_a)
    bmid = _cum(loc, loc_t, log_a)
    btot = jnp.sum(log_a, axis=-2, keepdims=True)
    qs = q * (GLA_DK ** -0.5)
    causal = lax.broadcasted_iota(jnp.int32, (CHUNK, CHUNK), 0) >= lax.broadcasted_iota(jnp.int32, (CHUNK, CHUNK), 1)
    scores = jnp.where(causal, _nt(qs * jnp.exp(bmid), k * jnp.exp(-bmid)), 0.0)
    o = _nn(scores, v) + _nt(qs * jnp.exp(bcum), state_t)
    new_state_t = state_t * jnp.exp(btot) + _tn(v, k * jnp.exp(btot - bcum))
    return _gated_headnorm(o, g, gain), new_state_t


def _cum_mats():
    r = lax.broadcasted_iota(jnp.int32, (CHUNK, CHUNK), 0)
    c = lax.broadcasted_iota(jnp.int32, (CHUNK, CHUNK), 1)
    mid = CHUNK // 2
    low = lambda a, b: (a >= b).astype(F32)
    lo, lo_t = low(r, c), low(c, r)
    loc = lo - (c <= mid).astype(F32)
    loc_t = lo_t - (r <= mid).astype(F32)
    return tuple(m.astype(BF16) for m in (lo, lo_t, loc, loc_t))


def _gla_pieces(u_ref):
    return (_heads(u_ref, 0, GLA_DK), _heads(u_ref, _GLA_K0, GLA_DK), _heads(u_ref, _GLA_V0, GLA_DV),
            _heads(u_ref, _GLA_G0, GLA_DV), u_ref[:, _GLA_Z0:].astype(F32))


def _gla_fwd(u, wg, bg, gain, name):
    tp = u.shape[0]
    nch = tp // CHUNK

    def body(u_ref, wg_ref, bg_ref, gain_ref, on_ref, st_ref, state_ref):
        @pl.when(pl.program_id(0) == 0)
        def _():
            state_ref[...] = jnp.zeros_like(state_ref)

        state = state_ref[...]
        st_ref[...] = state.astype(BF16)
        on, new_state = _gla_head(*_gla_pieces(u_ref), state, _heads(wg_ref, 0, GLA_DK), _heads(bg_ref, 0, GLA_DK),
                                  _heads(gain_ref, 0, GLA_DV), _row_mask(pl.program_id(0)), *_cum_mats())
        state_ref[...] = new_state
        _put_heads(on_ref, 0, on, 1.0)

    return pl.pallas_call(
        body, name=name, grid=(nch,),
        in_specs=[pl.BlockSpec((CHUNK, GLA_U), lambda n: (n, 0)), pl.BlockSpec((128, HEADS * GLA_DK), lambda n: (0, 0)),
                  pl.BlockSpec((1, HEADS * GLA_DK), lambda n: (0, 0)), pl.BlockSpec((1, HEADS * GLA_DV), lambda n: (0, 0))],
        out_specs=[pl.BlockSpec((CHUNK, HEADS * GLA_DV), lambda n: (n, 0)),
                   pl.BlockSpec((None, HEADS, GLA_DV, GLA_DK), lambda n: (n, 0, 0, 0))],
        out_shape=[jax.ShapeDtypeStruct((tp, HEADS * GLA_DV), BF16),
                   jax.ShapeDtypeStruct((nch, HEADS, GLA_DV, GLA_DK), BF16)],
        scratch_shapes=[pltpu.VMEM((HEADS, GLA_DV, GLA_DK), F32)],
        compiler_params=_cp(1))(u, wg, bg, gain)


def _gla_bwd(u, wg, bg, gain, states, d_on, name):
    tp = u.shape[0]
    nch = tp // CHUNK
    rev = lambda n: nch - 1 - n

    def body(u_ref, wg_ref, bg_ref, gain_ref, st_ref, don_ref, du_ref, dwg_ref, dbg_ref, dgain_ref, dstate_ref):
        @pl.when(pl.program_id(0) == 0)
        def _():
            dstate_ref[...] = jnp.zeros_like(dstate_ref)
            dwg_ref[...] = jnp.zeros_like(dwg_ref)
            dbg_ref[...] = jnp.zeros_like(dbg_ref)
            dgain_ref[...] = jnp.zeros_like(dgain_ref)

        mask = _row_mask(rev(pl.program_id(0)))
        mats = _cum_mats()
        _, vjp = jax.vjp(lambda *a: _gla_head(*a, mask, *mats), *_gla_pieces(u_ref), st_ref[...].astype(F32),
                         _heads(wg_ref, 0, GLA_DK), _heads(bg_ref, 0, GLA_DK), _heads(gain_ref, 0, GLA_DV))
        dq, dk, dv, dg, dz, dstate, dwg, dbg, dgain = vjp((_heads(don_ref, 0, GLA_DV), dstate_ref[...]))
        dstate_ref[...] = dstate
        for hd in range(HEADS):
            dwg_ref[:, GLA_DK * hd:GLA_DK * (hd + 1)] += dwg[hd]
            dbg_ref[:, GLA_DK * hd:GLA_DK * (hd + 1)] += dbg[hd]
            dgain_ref[:, GLA_DV * hd:GLA_DV * (hd + 1)] += dgain[hd]
        _put_heads(du_ref, 0, dq, mask)
        _put_heads(du_ref, _GLA_K0, dk, mask)
        _put_heads(du_ref, _GLA_V0, dv, mask)
        _put_heads(du_ref, _GLA_G0, dg, mask)
        du_ref[:, _GLA_Z0:] = dz.astype(BF16)

    full = lambda r, c: pl.BlockSpec((r, c), lambda n: (0, 0))
    return pl.pallas_call(
        body, name=name, grid=(nch,),
        in_specs=[pl.BlockSpec((CHUNK, GLA_U), lambda n: (rev(n), 0)), full(128, HEADS * GLA_DK),
                  full(1, HEADS * GLA_DK), full(1, HEADS * GLA_DV),
                  pl.BlockSpec((None, HEADS, GLA_DV, GLA_DK), lambda n: (rev(n), 0, 0, 0)),
                  pl.BlockSpec((CHUNK, HEADS * GLA_DV), lambda n: (rev(n), 0))],
        out_specs=[pl.BlockSpec((CHUNK, GLA_U), lambda n: (rev(n), 0)), full(128, HEADS * GLA_DK),
                   full(1, HEADS * GLA_DK), full(1, HEADS * GLA_DV)],
        out_shape=[jax.ShapeDtypeStruct((tp, GLA_U), BF16), jax.ShapeDtypeStruct((128, HEADS * GLA_DK), F32),
                   jax.ShapeDtypeStruct((1, HEADS * GLA_DK), F32), jax.ShapeDtypeStruct((1, HEADS * GLA_DV), F32)],
        scratch_shapes=[pltpu.VMEM((HEADS, GLA_DV, GLA_DK), F32)],
        compiler_params=_cp(1))(u, wg, bg, gain, states, d_on)


def _ffn_fwd(h, gain, w_in, w_out, tag):
    hn, ug, uu, act = _norm_ffn_in(h, gain, w_in, f"{tag}_in")
    if callable(w_out):
        w_out = w_out(act)
    return _out_proj(act, w_out, h, 0.5, f"{tag}_out"), (h, hn, ug, uu, act), w_out


def _ffn_bwd(dh, saved, gain, w_in, w_out, tag, push):
    h, hn, ug, uu, act = saved
    du = _ffn_dact(dh, w_out, ug, uu, f"{tag}_dact")
    d_w_out = _wgrad(act, dh, bm=D_FF // 2, bn=D, scale=0.5, sharded=False, name=f"{tag}_dwout")
    d_w_in = _wgrad(hn, du, bm=D, bn=w_in.shape[2], scale=1.0, sharded=True, name=f"{tag}_dwin")
    token = push([d_w_in, d_w_out])
    return _dgrad_norm(du, w_in, h, gain + token[0, 0], dh, f"{tag}_dnorm")


def _sequence_grads(x, target, p, weights, grads):
    row = lambda v, token: v.reshape(1, -1) + token[0, 0]
    gains = {}

    tok = weights.start(1, weights.start(0, None))
    weights.pin = tok
    h = jnp.concatenate([jnp.zeros((FRONT, D), F32), x], axis=0) + tok[0, 0]
    w = weights.wait(0, [tok, h, *weights.later_shards(2)])
    tok = weights.start(2, w["l0_ffn1_in"])
    h = lax.dynamic_update_slice(h, w["meta"], (FRONT - N_META, 0))
    gains["l0_ffn1"] = row(p["norm_ffn1"][0], tok)
    h, s1, w["l0_ffn1_out"] = _ffn_fwd(h, gains["l0_ffn1"], w["l0_ffn1_in"],
                                       lambda act: weights.wait(1, act)["l0_ffn1_out"], "l0_ffn1")
    w.update(weights.wait(2, h))
    tok = weights.start(3, w["ret_in"])
    gains["ret"] = row(p["norm_mix"][0], tok)
    hn, u = _norm_proj(h, gains["ret"], w["ret_in"], "ret_in")
    on, states = _ret_fwd(u, w["ret_gain"], "ret_fwd")
    h_mix = _out_proj(on, w["ret_out"], h, 1.0, "ret_out")
    s2 = (h, hn, u, on, states)
    w.update(weights.wait(3, h_mix))
    tok = weights.start(4, w["l0_ffn2_in"])
    gains["l0_ffn2"] = row(p["norm_ffn2"][0], tok)
    h, s3, _ = _ffn_fwd(h_mix, gains["l0_ffn2"], w["l0_ffn2_in"], w["l0_ffn2_out"], "l0_ffn2")
    saved = [(s1, s2, s3)]

    w.update(weights.wait(4, h))
    tok = weights.start(5, w["l1_ffn1_in"])
    gains["l1_ffn1"] = row(p["norm_ffn1"][1], tok)
    h, s1, _ = _ffn_fwd(h, gains["l1_ffn1"], w["l1_ffn1_in"], w["l1_ffn1_out"], "l1_ffn1")
    w.update(weights.wait(5, h))
    tok = weights.start(6, w["gla_out"])
    gains["gla"] = row(p["norm_mix"][1], tok)
    hn, u = _norm_proj(h, gains["gla"], w["gla_in"], "gla_in")
    on, states = _gla_fwd(u, w["gla_wg"], w["gla_bg"], w["gla_gain"], "gla_fwd")
    h_mix = _out_proj(on, w["gla_out"], h, 1.0, "gla_out")
    s2 = (h, hn, u, on, states)
    w.update(weights.wait(6, h_mix))
    gains["l1_ffn2"] = p["norm_ffn2"][1].reshape(1, -1)
    h, s3, _ = _ffn_fwd(h_mix, gains["l1_ffn2"], w["l1_ffn2_in"], w["l1_ffn2_out"], "l1_ffn2")
    saved.append((s1, s2, s3))

    dh, d_final, loss = _loss_head(h, p["final_norm"].reshape(1, -1), target, "loss_head")
    small = {"final_norm": d_final, "norm_ffn1": [None, None], "norm_mix": [None, None], "norm_ffn2": [None, None]}
    pusher = lambda k: functools.partial(grads.push, k)

    s1, s2, s3 = saved[1]
    dh, small["norm_ffn2"][1] = _ffn_bwd(dh, s3, gains["l1_ffn2"], w["l1_ffn2_in"], w["l1_ffn2_out"], "l1_ffn2",
                                         pusher(0))
    h_in, hn, u, on, states = s2
    d_on = _dgrad(dh, w["gla_out"], "gla_don")
    d_out = _wgrad(on, dh, bm=D, bn=D, scale=1.0, sharded=False, name="gla_dwout")
    du, small["gla_wg"], small["gla_bg"], small["gla_gain"] = _gla_bwd(u, w["gla_wg"], w["gla_bg"], w["gla_gain"],
                                                                       states, d_on, "gla_bwd")
    d_in = _wgrad(hn, du, bm=D, bn=GLA_U // 5, scale=1.0, sharded=False, name="gla_dwin")
    d_in = jnp.moveaxis(d_in[:, :GLA_IN].reshape(D, N_CHIPS, -1), 1, 0)
    tok = grads.push(1, [d_in, d_out])
    dh, small["norm_mix"][1] = _dgrad_norm(du, w["gla_in"], h_in, gains["gla"] + tok[0, 0], dh, "gla_dnorm")
    dh, small["norm_ffn1"][1] = _ffn_bwd(dh, s1, gains["l1_ffn1"], w["l1_ffn1_in"], w["l1_ffn1_out"], "l1_ffn1",
                                         pusher(2))

    s1, s2, s3 = saved[0]
    dh, small["norm_ffn2"][0] = _ffn_bwd(dh, s3, gains["l0_ffn2"], w["l0_ffn2_in"], w["l0_ffn2_out"], "l0_ffn2",
                                         pusher(3))
    h_in, hn, u, on, states = s2
    d_on = _dgrad(dh, w["ret_out"], "ret_don")
    d_out = _wgrad(on, dh, bm=D, bn=D, scale=1.0, sharded=False, name="ret_dwout")
    du, small["ret_gain"] = _ret_bwd(u, w["ret_gain"], states, d_on, "ret_bwd")
    d_in = _wgrad(hn, du, bm=D, bn=w["ret_in"].shape[2], scale=1.0, sharded=True, name="ret_dwin")
    tok = grads.push(4, [d_in, d_out])
    dh, small["norm_mix"][0] = _dgrad_norm(du, w["ret_in"], h_in, gains["ret"] + tok[0, 0], dh, "ret_dnorm")
    dh, small["norm_ffn1"][0] = _ffn_bwd(dh, s1, gains["l0_ffn1"], w["l0_ffn1_in"], w["l0_ffn1_out"], "l0_ffn1",
                                         pusher(5))
    grads.push(6, [], [dh[FRONT - N_META:FRONT], *small["norm_ffn1"], *small["norm_mix"], *small["norm_ffn2"],
                       small["final_norm"], small["ret_gain"], small["gla_wg"][:GLA_RANK], small["gla_bg"],
                       small["gla_gain"], loss[:, :1]])
    return dh[FRONT:]


_HBM = pl.BlockSpec(memory_space=pl.ANY)


def _place():
    return lax.axis_index("x"), lax.axis_index("y"), lax.axis_index("c")


def _flip(v, bit):
    return 1 - v if bit else v


DMA_CHUNK_BYTES = 128 * 1024


def _row_chunks(ref):
    rows, cols = ref.shape
    step = _row_tile(rows, max(16, DMA_CHUNK_BYTES // (cols * ref.dtype.itemsize)))
    return [pl.ds(a, step) for a in range(0, rows, step)]


def _whole(src, dst, send_sem, recv_sem, peer):
    return pltpu.make_async_remote_copy(src_ref=src, dst_ref=dst, send_sem=send_sem, recv_sem=recv_sem,
                                        device_id=peer, device_id_type=MESH)


def _send(src, dst, send_sem, recv_sem, peer):
    for rows in _row_chunks(src):
        _whole(src.at[rows], dst.at[rows], send_sem, recv_sem, peer).start()
    return _whole(src, dst, send_sem, recv_sem, peer)


_HBM_ONLY = pl.BlockSpec(memory_space=pltpu.HBM)
_SEMS = pl.BlockSpec(memory_space=pltpu.SEMAPHORE)
_SIDE_EFFECT = pltpu.CompilerParams(has_side_effects=pltpu.SideEffectType.DATAFLOW_SIDE_EFFECTING)
_CHIP_FLIPS = [(1, 0, 0), (0, 1, 0), (1, 1, 0)]
_PEER_FLIPS = [(fx, fy, fc) for fx in (0, 1) for fy in (0, 1) for fc in (0, 1)][1:]


def _zero_token():
    return jnp.zeros((8, 128), F32)


def _exchange_start(srcs, lands, route, flips, after, name):
    n = len(srcs)

    def body(*refs):
        src, land = refs[:n], refs[n:2 * n]
        send_sems, recv_sems, token = refs[2 * n + 1], refs[2 * n + 2], refs[-1]
        me = _place()
        for t in range(n):
            for j, flip in enumerate(flips):
                peer = tuple(_flip(v, f) for v, f in zip(me, flip))
                s, d = route(t, src[t], land[t], me, peer)
                _send(s, d, send_sems.at[t * len(flips) + j], recv_sems.at[t * len(flips) + j], peer)
        token[...] = jnp.zeros_like(token)

    hbm = lambda a: pltpu.HBM(a.shape, a.dtype)
    sems = pltpu.SemaphoreType.DMA((n * len(flips),))
    operands = [pltpu.with_memory_space_constraint(a, pltpu.HBM) for a in list(srcs) + list(lands)]
    out = pl.pallas_call(
        body, name=name, in_specs=[_HBM_ONLY] * (2 * n) + [_HBM],
        out_shape=(sems, sems, *[hbm(a) for a in operands], jax.ShapeDtypeStruct((8, 128), F32)),
        out_specs=(_SEMS, _SEMS, *[_HBM_ONLY] * (2 * n), pl.BlockSpec(memory_space=pltpu.VMEM)),
        input_output_aliases={i: 2 + i for i in range(2 * n)}, compiler_params=_SIDE_EFFECT,
    )(*operands, _zero_token() if after is None else after)
    return (out[0], out[1], out[2:2 + n], out[2 + n:2 + 2 * n]), out[-1]


def _exchange_wait(started, route, flips, after, name):
    send_sems, recv_sems, srcs, lands = started
    n = len(srcs)

    def body(*refs):
        src, land = refs[:n], refs[n:2 * n]
        send_sems, recv_sems = refs[2 * n], refs[2 * n + 1]
        me = _place()
        for t in range(n):
            for j, flip in enumerate(flips):
                peer = tuple(_flip(v, f) for v, f in zip(me, flip))
                s, d = route(t, src[t], land[t], me, peer)
                cp = _whole(s, d, send_sems.at[t * len(flips) + j], recv_sems.at[t * len(flips) + j], peer)
                cp.wait_send()
                cp.wait_recv()

    hbm = lambda a: pltpu.HBM(a.shape, a.dtype)
    after = list(after) if isinstance(after, (list, tuple)) else [after]
    out = pl.pallas_call(
        body, name=name, in_specs=[_HBM_ONLY] * (2 * n) + [_SEMS, _SEMS] + [_HBM] * len(after),
        out_shape=tuple(hbm(a) for a in list(srcs) + list(lands)), out_specs=tuple([_HBM_ONLY] * (2 * n)),
        input_output_aliases={i: i for i in range(2 * n)}, compiler_params=_SIDE_EFFECT,
    )(*srcs, *lands, send_sems, recv_sems, *after)
    return out[:n], out[n:]


def _gather_route(t, src, land, me, peer):
    return src, land.at[2 * me[0] + me[1]]


def _scatter_route(n_pieces):
    def route(t, src, land, me, peer):
        part = src.at[2 * peer[0] + peer[1], peer[2]] if t < n_pieces else src
        return part, land.at[4 * me[0] + 2 * me[1] + me[2]]

    return route


def _swap_cores(halves):
    n = len(halves)

    def body(*refs):
        src, dst = refs[:n], refs[n:2 * n]
        send_sems, recv_sems = refs[2 * n:]
        x, y, c = _place()
        copies = [_send(src[t], dst[t], send_sems.at[t], recv_sems.at[t], (x, y, 1 - c)) for t in range(n)]
        for cp in copies:
            cp.wait()

    got = pl.pallas_call(
        body, name="swap_cores", in_specs=[_HBM] * n, out_specs=[_HBM] * n,
        out_shape=[jax.ShapeDtypeStruct(a.shape, a.dtype) for a in halves],
        scratch_shapes=[pltpu.SemaphoreType.DMA((n,)), pltpu.SemaphoreType.DMA((n,))],
    )(*halves)
    south = lax.axis_index("c") == 0
    return [jnp.stack([jnp.where(south, a, b), jnp.where(south, b, a)]) for a, b in zip(halves, got)]


def _row_tile(rows, cap):
    fits = [t for t in range(16, cap + 1, 16) if rows % t == 0]
    return fits[-1] if fits else rows


def _sum_slots(a, name):
    _, r, c = a.shape
    tr = _row_tile(r, 384)

    def body(a_ref, o_ref):
        s = a_ref[0].astype(F32)
        for k in range(1, N_DEV):
            s = s + a_ref[k].astype(F32)
        o_ref[...] = s

    return pl.pallas_call(
        body, name=name, grid=(r // tr,),
        in_specs=[pl.BlockSpec((N_DEV, tr, c), lambda i: (0, i, 0))],
        out_specs=pl.BlockSpec((tr, c), lambda i: (i, 0)),
        out_shape=jax.ShapeDtypeStruct((r, c), F32),
        compiler_params=_cp(1))(a)


def _adamw(w, g, m, v, name):
    r, c = w.shape
    tr = _row_tile(r, 256)

    def body(w_ref, g_ref, m_ref, v_ref, d_ref, nm_ref, nv_ref):
        gv = g_ref[...]
        nm = ADAM_B1 * m_ref[...] + (1.0 - ADAM_B1) * gv
        nv = ADAM_B2 * v_ref[...] + (1.0 - ADAM_B2) * (gv * gv)
        m_hat = nm / (1.0 - ADAM_B1 ** ADAM_STEP)
        v_hat = nv / (1.0 - ADAM_B2 ** ADAM_STEP)
        d_ref[...] = -ADAM_LR * (m_hat / (jnp.sqrt(v_hat) + ADAM_EPS) + ADAM_WD * w_ref[...])
        nm_ref[...] = nm
        nv_ref[...] = nv

    spec = pl.BlockSpec((tr, c), lambda i: (i, 0))
    return pl.pallas_call(
        body, name=name, grid=(r // tr,), in_specs=[spec] * 4, out_specs=[spec] * 3,
        out_shape=[jax.ShapeDtypeStruct((r, c), F32)] * 3,
        compiler_params=_cp(1))(w, g, m, v)


_SMALL = ["meta_tokens", "ret_head_norm", "gla_w_gate", "gla_b_gate", "gla_head_norm"]
_LOCAL_SMALL = ["meta_tokens", "norm_ffn1", "norm_mix", "norm_ffn2", "ret_head_norm", "gla_w_gate", "gla_b_gate",
                "gla_head_norm", "final_norm"]
_BIG = ["ffn1_w_in", "ffn1_w_out", "ffn2_w_in", "ffn2_w_out", "ret_w_in", "ret_w_out", "gla_w_in", "gla_w_out"]
_WEIGHTS = ["meta_tokens", "norm_ffn1", "ffn1_w_in", "ffn1_w_out", "norm_mix", "norm_ffn2", "ffn2_w_in", "ffn2_w_out",
            "ret_w_in", "ret_head_norm", "ret_w_out", "gla_w_in", "gla_w_gate", "gla_b_gate", "gla_head_norm",
            "gla_w_out", "final_norm"]


def _pack_rows(arrays, width):
    flat = jnp.concatenate([a.reshape(-1) for a in arrays])
    pad = -flat.shape[0] % (8 * width)
    return jnp.pad(flat, (0, pad)).reshape(-1, width)


def _unpack_rows(packed, shapes):
    flat, out, at = packed.reshape(-1), [], 0
    for s in shapes:
        size = 1
        for dim in s:
            size *= dim
        out.append(flat[at:at + size].reshape(s))
        at += size
    return out


class _WeightGather:
    GROUPS = [("small", "l0_ffn1_in"), ("l0_ffn1_out",), ("ret_in", "ret_out"), ("l0_ffn2_in", "l0_ffn2_out"),
              ("l1_ffn1_in", "l1_ffn1_out"), ("gla_in", "gla_out"), ("l1_ffn2_in", "l1_ffn2_out")]

    def __init__(self, p):
        self.small_shapes = [p[name].shape for name in _SMALL]
        self.f32 = {"small": _pack_rows([p[name] for name in _SMALL], 128), "ret_in": p["ret_w_in"][0],
                    "ret_out": p["ret_w_out"][0], "gla_in": p["gla_w_in"][0], "gla_out": p["gla_w_out"][0]}
        for layer in range(2):
            for name in ("ffn1", "ffn2"):
                self.f32[f"l{layer}_{name}_in"] = p[f"{name}_w_in"][layer]
                self.f32[f"l{layer}_{name}_out"] = p[f"{name}_w_out"][layer]
        self.shards = {}
        self.started = {}
        self.pin = None

    def shard(self, name):
        if name not in self.shards:
            a = self.f32[name]
            if name != "small":
                a = (a if self.pin is None else a + self.pin[0, 0]).astype(BF16)
            self.shards[name] = a
        return self.shards[name]

    def later_shards(self, k):
        return [self.shard(name) for group in self.GROUPS[k:] for name in group]

    def start(self, k, after):
        shards = [self.shard(name) for name in self.GROUPS[k]]
        lands = [lax.empty((N_CHIPS,) + s.shape, s.dtype) for s in shards]
        self.started[k], token = _exchange_start(shards, lands, _gather_route, _CHIP_FLIPS, after, f"gather{k}_start")
        return token

    def wait(self, k, after):
        shards, got = _exchange_wait(self.started[k], _gather_route, _CHIP_FLIPS, after, f"gather{k}_wait")
        mine = 2 * lax.axis_index("x") + lax.axis_index("y")
        w = {}
        for name, g, s in zip(self.GROUPS[k], got, shards):
            g = lax.dynamic_update_index_in_dim(g, s, mine, 0)
            if name == "small":
                parts = zip(*[_unpack_rows(g[chip], self.small_shapes) for chip in range(N_CHIPS)])
                cat = lambda a: jnp.moveaxis(a, 0, -2).reshape(a.shape[1:-1] + (-1,))
                meta, ret_gain, wg, bg, gla_gain = [cat(jnp.stack(part)) for part in parts]
                w.update(meta=meta, ret_gain=ret_gain.reshape(1, -1), gla_bg=bg.reshape(1, -1),
                         gla_gain=gla_gain.reshape(1, -1),
                         gla_wg=jnp.pad(wg[0], ((0, 128 - GLA_RANK), (0, 0))).astype(BF16))
            elif name == "gla_in":
                full = jnp.moveaxis(g, 0, 1).reshape(D, -1)
                w[name] = jnp.pad(full, ((0, 0), (0, GLA_U - GLA_IN)))[None]
            elif name.endswith("_out"):
                w[name] = g.reshape(-1, g.shape[-1])
            else:
                w[name] = g
        return w


class _GradExchange:
    def __init__(self):
        self.started = []
        self.token = None
        self.small_shapes = None

    def push(self, k, arrays, small=None):
        srcs = [a.reshape(N_CHIPS, 2, -1, a.shape[-1]) for a in arrays]
        lands = [lax.empty((N_DEV,) + a.shape[2:], a.dtype) for a in srcs]
        if small is not None:
            self.small_shapes = [a.shape for a in small]
            srcs.append(_pack_rows(small, D))
            lands.append(lax.empty((N_DEV,) + srcs[-1].shape, F32))
        started, self.token = _exchange_start(srcs, lands, _scatter_route(len(arrays)), _PEER_FLIPS, None,
                                              f"scatter{k}_start")
        self.started.append((started, len(arrays)))
        return self.token

    def collect(self):
        x, y, c = _place()
        after, sums = self.token, []
        for k, (started, n_pieces) in enumerate(self.started):
            srcs, got = _exchange_wait(started, _scatter_route(n_pieces), _PEER_FLIPS, after, f"scatter{k}_wait")
            own = [a[2 * x + y, c] for a in srcs[:n_pieces]] + list(srcs[n_pieces:])
            got = [lax.dynamic_update_index_in_dim(g, a, 4 * x + 2 * y + c, 0) for g, a in zip(got, own)]
            sums.append([_sum_slots(a, f"sum{k}_{i}") for i, a in enumerate(got)])
            after = sums[-1][0]
        small = _unpack_rows(sums[-1].pop(), self.small_shapes)
        return sums, small


def kernel(x, meta_tokens, norm_ffn1, ffn1_w_in, ffn1_w_out, norm_mix, norm_ffn2, ffn2_w_in, ffn2_w_out, ret_w_in, ret_head_norm, ret_w_out, gla_w_in, gla_w_gate, gla_b_gate, gla_head_norm, gla_w_out, final_norm, loss_target, m_meta_tokens, m_norm_ffn1, m_ffn1_w_in, m_ffn1_w_out, m_norm_mix, m_norm_ffn2, m_ffn2_w_in, m_ffn2_w_out, m_ret_w_in, m_ret_head_norm, m_ret_w_out, m_gla_w_in, m_gla_w_gate, m_gla_b_gate, m_gla_head_norm, m_gla_w_out, m_final_norm, v_meta_tokens, v_norm_ffn1, v_ffn1_w_in, v_ffn1_w_out, v_norm_mix, v_norm_ffn2, v_ffn2_w_in, v_ffn2_w_out, v_ret_w_in, v_ret_head_norm, v_ret_w_out, v_gla_w_in, v_gla_w_gate, v_gla_b_gate, v_gla_head_norm, v_gla_w_out, v_final_norm):
    p = dict(meta_tokens=meta_tokens, norm_ffn1=norm_ffn1, ffn1_w_in=ffn1_w_in, ffn1_w_out=ffn1_w_out, norm_mix=norm_mix,
             norm_ffn2=norm_ffn2, ffn2_w_in=ffn2_w_in, ffn2_w_out=ffn2_w_out, ret_w_in=ret_w_in,
             ret_head_norm=ret_head_norm, ret_w_out=ret_w_out, gla_w_in=gla_w_in, gla_w_gate=gla_w_gate,
             gla_b_gate=gla_b_gate, gla_head_norm=gla_head_norm, gla_w_out=gla_w_out, final_norm=final_norm)
    m = dict(zip(_WEIGHTS, (m_meta_tokens, m_norm_ffn1, m_ffn1_w_in, m_ffn1_w_out, m_norm_mix, m_norm_ffn2, m_ffn2_w_in,
                            m_ffn2_w_out, m_ret_w_in, m_ret_head_norm, m_ret_w_out, m_gla_w_in, m_gla_w_gate,
                            m_gla_b_gate, m_gla_head_norm, m_gla_w_out, m_final_norm)))
    v = dict(zip(_WEIGHTS, (v_meta_tokens, v_norm_ffn1, v_ffn1_w_in, v_ffn1_w_out, v_norm_mix, v_norm_ffn2, v_ffn2_w_in,
                            v_ffn2_w_out, v_ret_w_in, v_ret_head_norm, v_ret_w_out, v_gla_w_in, v_gla_w_gate,
                            v_gla_b_gate, v_gla_head_norm, v_gla_w_out, v_final_norm)))

    exchange = _GradExchange()
    d_x = _sequence_grads(x[0], loss_target[0], p, _WeightGather(p), exchange)
    sums, small = exchange.collect()
    names = [("ffn2_in", 1), ("ffn2_out", 1), ("gla_in", 0), ("gla_out", 0), ("ffn1_in", 1), ("ffn1_out", 1),
             ("ffn2_in", 0), ("ffn2_out", 0), ("ret_in", 0), ("ret_out", 0), ("ffn1_in", 0), ("ffn1_out", 0)]
    swapped = _swap_cores([a for group in sums for a in group])
    shard = {key: a.reshape(-1, a.shape[-1]) for key, a in zip(names, swapped)}
    big = {name: [shard[name, layer] for layer in range(2) if (name, layer) in shard] for name, _ in names}

    chip = 2 * lax.axis_index("x") + lax.axis_index("y")
    cols = lambda a, n: lax.dynamic_slice_in_dim(a, chip * n, n, axis=a.ndim - 1)
    (s_meta, s_n1a, s_n1b, s_nma, s_nmb, s_n2a, s_n2b, s_final, s_ret_gain, s_wg, s_bg, s_gla_gain, s_loss) = small
    grads = {
        "meta_tokens": cols(s_meta, 256), "norm_ffn1": jnp.concatenate([s_n1a, s_n1b]),
        "norm_mix": jnp.concatenate([s_nma, s_nmb]), "norm_ffn2": jnp.concatenate([s_n2a, s_n2b]),
        "final_norm": s_final.reshape(D),
        "ret_head_norm": cols(s_ret_gain.reshape(1, HEADS, RET_DV), RET_DV // N_CHIPS),
        "gla_w_gate": cols(s_wg, GLA_DK)[None], "gla_b_gate": cols(s_bg, GLA_DK),
        "gla_head_norm": cols(s_gla_gain.reshape(1, HEADS, GLA_DV), GLA_DV // N_CHIPS),
        "ffn1_w_in": jnp.stack(big["ffn1_in"]), "ffn1_w_out": jnp.stack(big["ffn1_out"]),
        "ffn2_w_in": jnp.stack(big["ffn2_in"]), "ffn2_w_out": jnp.stack(big["ffn2_out"]),
        "ret_w_in": big["ret_in"][0][None], "ret_w_out": big["ret_out"][0][None],
        "gla_w_in": big["gla_in"][0][None], "gla_w_out": big["gla_out"][0][None],
    }

    delta, new_m, new_v = {}, {}, {}
    for name in _BIG:
        shape = p[name].shape
        flat = lambda a: a.reshape(-1, shape[-1])
        out = _adamw(flat(p[name]), flat(grads[name]), flat(m[name]), flat(v[name]), f"adamw_{name}")
        delta[name], new_m[name], new_v[name] = [a.reshape(shape) for a in out]
    packed = [_pack_rows([d[name] for name in _LOCAL_SMALL], 128) for d in (p, grads, m, v)]
    out = _adamw(*packed, "adamw_small")
    shapes = [p[name].shape for name in _LOCAL_SMALL]
    for d, a in zip((delta, new_m, new_v), out):
        d.update(zip(_LOCAL_SMALL, _unpack_rows(a, shapes)))

    return (s_loss.reshape(()), d_x[None], *[grads[n] for n in _WEIGHTS], *[delta[n] for n in _WEIGHTS],
            *[new_m[n] for n in _WEIGHTS], *[new_v[n] for n in _WEIGHTS])
```

```python
import functools

import jax
import jax.numpy as jnp
from jax import lax
from jax.experimental import pallas as pl
from jax.experimental.pallas import tpu as pltpu

F32, BF16 = jnp.float32, jnp.bfloat16
MESH = pl.DeviceIdType.MESH

D = 1024
N_META = 16
CHUNK = 64
RET_CHUNK = 128
FRONT = 256
D_FF = 2816
EPS = 1e-6
HEADS = 4
RET_DK, RET_DV = 256, 512
GLA_DK, GLA_DV = 128, 256
GLA_RANK = 16
GLA_TAU = 16.0
GLA_IN = 2 * HEADS * GLA_DK + 2 * HEADS * GLA_DV + GLA_RANK
GLA_U = 3200
ROPE_BASE = 10000.0
N_CHIPS = 4
N_DEV = 8

ADAM_LR, ADAM_B1, ADAM_B2, ADAM_EPS, ADAM_WD, ADAM_STEP = 0.001, 0.9, 0.999, 1e-08, 0.01, 10

VMEM_LIMIT_BYTES = 56 * 1024 * 1024
TM = 768
TM_SMALL = 256


TM_RESIDENT = 384


def _cp(n_axes):
    return pltpu.CompilerParams(dimension_semantics=("arbitrary",) * n_axes, vmem_limit_bytes=VMEM_LIMIT_BYTES)


def _resident(shape, n_axes):
    zeros = (0,) * len(shape)
    index = (lambda i: zeros) if n_axes == 1 else (lambda i, j: zeros)
    return pl.BlockSpec(shape, index, pipeline_mode=pl.Buffered(1))


def _dg(a, b, ca, cb):
    nb = a.ndim - 2
    dims = (((ca + nb,), (cb + nb,)), (tuple(range(nb)), tuple(range(nb))))
    return lax.dot_general(a.astype(BF16), b.astype(BF16), dims, preferred_element_type=F32)


@jax.custom_vjp
def _nn(a, b):
    return _dg(a, b, 1, 0)


@jax.custom_vjp
def _nt(a, b):
    return _dg(a, b, 1, 1)


@jax.custom_vjp
def _tn(a, b):
    return _dg(a, b, 0, 0)


_nn.defvjp(lambda a, b: (_nn(a, b), (a, b)), lambda res, g: (_nt(g, res[1]), _tn(res[0], g)))
_nt.defvjp(lambda a, b: (_nt(a, b), (a, b)), lambda res, g: (_nn(g, res[1]), _tn(g, res[0])))
_tn.defvjp(lambda a, b: (_tn(a, b), (a, b)), lambda res, g: (_nt(res[1], g), _nn(res[0], g)))


def _split3_dot(m, a):
    a1 = a.astype(BF16)
    r1 = a - a1.astype(F32)
    a2 = r1.astype(BF16)
    a3 = (r1 - a2.astype(F32)).astype(BF16)
    mb = jnp.broadcast_to(m, a.shape[:-2] + m.shape)
    return _dg(mb, a1, 1, 0) + _dg(mb, a2, 1, 0) + _dg(mb, a3, 1, 0)


@jax.custom_vjp
def _cum(m, mt, a):
    return _split3_dot(m, a)


_cum.defvjp(lambda m, mt, a: (_split3_dot(m, a), (m, mt)),
            lambda res, g: (jnp.zeros_like(res[0]), jnp.zeros_like(res[1]), _split3_dot(res[1], g)))


def _sigmoid(x):
    return 1.0 / (1.0 + jnp.exp(-x))


def _rms(x):
    return lax.rsqrt(jnp.mean(x * x, axis=-1, keepdims=True) + EPS)


def _rmsnorm_bwd(dy, x, gain):
    r = _rms(x)
    xhat = x * r
    dxh = dy * gain
    return r * (dxh - xhat * jnp.mean(dxh * xhat, axis=-1, keepdims=True)), xhat


def _norm_proj(h, gain, w, name):
    tp, d = h.shape
    s, _, ns = w.shape

    tm = TM_RESIDENT

    def body(h_ref, g_ref, w_ref, hn_ref, u_ref):
        @pl.when(pl.program_id(1) == 0)
        def _():
            x = h_ref[...]
            hn_ref[...] = (x * _rms(x) * g_ref[...]).astype(BF16)

        u_ref[...] = jnp.dot(hn_ref[...], w_ref[pl.program_id(1)], preferred_element_type=F32).astype(BF16)

    return pl.pallas_call(
        body, name=name, grid=(tp // tm, s),
        in_specs=[pl.BlockSpec((tm, d), lambda i, j: (i, 0)), pl.BlockSpec((1, d), lambda i, j: (0, 0)),
                  _resident(w.shape, 2)],
        out_specs=[pl.BlockSpec((tm, d), lambda i, j: (i, 0)), pl.BlockSpec((tm, ns), lambda i, j: (i, j))],
        out_shape=[jax.ShapeDtypeStruct((tp, d), BF16), jax.ShapeDtypeStruct((tp, s * ns), BF16)],
        compiler_params=_cp(2))(h, gain, w)


def _norm_ffn_in(h, gain, w, name):
    tp, d = h.shape
    s, _, ns = w.shape
    half = s // 2
    tm = TM_RESIDENT

    def body(h_ref, g_ref, w_ref, hn_ref, dg_ref, du_ref, act_ref):
        j = pl.program_id(1)

        @pl.when(j == 0)
        def _():
            x = h_ref[...]
            hn_ref[...] = (x * _rms(x) * g_ref[...]).astype(BF16)

        a = hn_ref[...]
        g = jnp.dot(a, w_ref[j], preferred_element_type=F32)
        u = jnp.dot(a, w_ref[j + half], preferred_element_type=F32)
        sg = _sigmoid(g)
        silu = g * sg
        dg_ref[...] = (u * (sg + silu * (1.0 - sg))).astype(BF16)
        du_ref[...] = silu.astype(BF16)
        act_ref[...] = (silu * u).astype(BF16)

    wide = jax.ShapeDtypeStruct((tp, half * ns), BF16)
    return pl.pallas_call(
        body, name=name, grid=(tp // tm, half),
        in_specs=[pl.BlockSpec((tm, d), lambda i, j: (i, 0)), pl.BlockSpec((1, d), lambda i, j: (0, 0)),
                  _resident(w.shape, 2)],
        out_specs=[pl.BlockSpec((tm, d), lambda i, j: (i, 0))] + [pl.BlockSpec((tm, ns), lambda i, j: (i, j))] * 3,
        out_shape=[jax.ShapeDtypeStruct((tp, d), BF16), wide, wide, wide],
        compiler_params=_cp(2))(h, gain, w)


def _out_proj(a, w, h, scale, name):
    tp, k = a.shape
    d = w.shape[1]

    def body(a_ref, w_ref, h_ref, o_ref):
        o_ref[...] = h_ref[...] + scale * jnp.dot(a_ref[...], w_ref[...], preferred_element_type=F32)

    return pl.pallas_call(
        body, name=name, grid=(tp // TM,),
        in_specs=[pl.BlockSpec((TM, k), lambda i: (i, 0)), pl.BlockSpec((k, d), lambda i: (0, 0)),
                  pl.BlockSpec((TM, d), lambda i: (i, 0))],
        out_specs=pl.BlockSpec((TM, d), lambda i: (i, 0)),
        out_shape=jax.ShapeDtypeStruct((tp, d), F32),
        compiler_params=_cp(1))(a, w, h)


def _ffn_dact(dh, w_out, act_dg, act_du, name):
    tp, d = dh.shape
    ff = w_out.shape[0]
    tm = TM_RESIDENT

    def body(dh_ref, w_ref, dg_ref, du_ref, o_ref):
        dy = (0.5 * dh_ref[...]).astype(BF16)
        dact = lax.dot_general(dy, w_ref[...], (((1,), (1,)), ((), ())), preferred_element_type=F32)
        o_ref[:, :ff] = (dact * dg_ref[...].astype(F32)).astype(BF16)
        o_ref[:, ff:] = (dact * du_ref[...].astype(F32)).astype(BF16)

    return pl.pallas_call(
        body, name=name, grid=(tp // tm,),
        in_specs=[pl.BlockSpec((tm, d), lambda i: (i, 0)), _resident(w_out.shape, 1),
                  pl.BlockSpec((tm, ff), lambda i: (i, 0)), pl.BlockSpec((tm, ff), lambda i: (i, 0))],
        out_specs=pl.BlockSpec((tm, 2 * ff), lambda i: (i, 0)),
        out_shape=jax.ShapeDtypeStruct((tp, 2 * ff), BF16),
        compiler_params=_cp(1))(dh, w_out, act_dg, act_du)


def _dgrad(dh, w, name):
    tp, d = dh.shape
    k = w.shape[0]

    def body(dh_ref, w_ref, o_ref):
        o_ref[...] = lax.dot_general(dh_ref[...].astype(BF16), w_ref[...], (((1,), (1,)), ((), ())),
                                     preferred_element_type=F32).astype(BF16)

    return pl.pallas_call(
        body, name=name, grid=(tp // TM,),
        in_specs=[pl.BlockSpec((TM, d), lambda i: (i, 0)), pl.BlockSpec((k, d), lambda i: (0, 0))],
        out_specs=pl.BlockSpec((TM, k), lambda i: (i, 0)),
        out_shape=jax.ShapeDtypeStruct((tp, k), BF16),
        compiler_params=_cp(1))(dh, w)


def _wgrad(a, b, *, bm, bn, scale, sharded, name):
    tp, m = a.shape
    n = b.shape[1]
    nk = tp // TM

    def body(a_ref, b_ref, o_ref, acc_ref):
        k = pl.program_id(2)

        @pl.when(k == 0)
        def _():
            acc_ref[...] = jnp.zeros_like(acc_ref)

        bb = b_ref[...]
        if scale != 1.0:
            bb = scale * bb
        acc_ref[...] += lax.dot_general(a_ref[...], bb.astype(BF16), (((0,), (0,)), ((), ())),
                                        preferred_element_type=F32)

        @pl.when(k == nk - 1)
        def _():
            o_ref[...] = acc_ref[...].astype(BF16)

    if sharded:
        assert m == bm
        out_spec = pl.BlockSpec((None, bm, bn), lambda i, j, k: (j, 0, 0))
        out_shape = jax.ShapeDtypeStruct((n // bn, m, bn), BF16)
    else:
        out_spec = pl.BlockSpec((bm, bn), lambda i, j, k: (i, j))
        out_shape = jax.ShapeDtypeStruct((m, n), BF16)
    return pl.pallas_call(
        body, name=name, grid=(m // bm, n // bn, nk),
        in_specs=[pl.BlockSpec((TM, bm), lambda i, j, k: (k, i)), pl.BlockSpec((TM, bn), lambda i, j, k: (k, j))],
        out_specs=out_spec, out_shape=out_shape,
        scratch_shapes=[pltpu.VMEM((bm, bn), F32)],
        compiler_params=_cp(3))(a, b)


def _dgrad_norm(du, w, h, gain, dh_out, name):
    tp, d = h.shape
    s, _, ns = w.shape
    tm = TM_RESIDENT

    def body(du_ref, w_ref, h_ref, g_ref, dho_ref, dhi_ref, dg_ref):
        @pl.when(pl.program_id(0) == 0)
        def _():
            dg_ref[...] = jnp.zeros_like(dg_ref)

        dhn = None
        for k in range(s):
            part = lax.dot_general(du_ref[:, ns * k:ns * (k + 1)], w_ref[k], (((1,), (1,)), ((), ())),
                                   preferred_element_type=F32)
            dhn = part if dhn is None else dhn + part
        dx, xhat = _rmsnorm_bwd(dhn, h_ref[...], g_ref[...])
        dg_ref[...] += jnp.sum(dhn * xhat, axis=0, keepdims=True)
        dhi_ref[...] = dho_ref[...] + dx

    return pl.pallas_call(
        body, name=name, grid=(tp // tm,),
        in_specs=[pl.BlockSpec((tm, s * ns), lambda i: (i, 0)), _resident(w.shape, 1),
                  pl.BlockSpec((tm, d), lambda i: (i, 0)), pl.BlockSpec((1, d), lambda i: (0, 0)),
                  pl.BlockSpec((tm, d), lambda i: (i, 0))],
        out_specs=[pl.BlockSpec((tm, d), lambda i: (i, 0)), pl.BlockSpec((1, d), lambda i: (0, 0))],
        out_shape=[jax.ShapeDtypeStruct((tp, d), F32), jax.ShapeDtypeStruct((1, d), F32)],
        compiler_params=_cp(1))(du, w, h, gain, dh_out)


def _loss_head(h, gain, target, name):
    tp, d = h.shape
    tm = TM_SMALL
    front_tiles = FRONT // tm

    def body(h_ref, g_ref, t_ref, dh_ref, dg_ref, loss_ref):
        i = pl.program_id(0)

        @pl.when(i == 0)
        def _():
            dg_ref[...] = jnp.zeros_like(dg_ref)
            loss_ref[...] = jnp.zeros_like(loss_ref)

        x = h_ref[...]
        gain_v = g_ref[...]
        y = x * _rms(x) * gain_v
        err = jnp.where(i >= front_tiles, y - t_ref[...], 0.0)
        loss_ref[...] += 0.5 * jnp.sum(jnp.mean(err * err, axis=-1, keepdims=True), axis=0, keepdims=True)
        dy = err * (1.0 / d)
        dx, xhat = _rmsnorm_bwd(dy, x, gain_v)
        dg_ref[...] += jnp.sum(dy * xhat, axis=0, keepdims=True)
        dh_ref[...] = dx

    return pl.pallas_call(
        body, name=name, grid=(tp // tm,),
        in_specs=[pl.BlockSpec((tm, d), lambda i: (i, 0)), pl.BlockSpec((1, d), lambda i: (0, 0)),
                  pl.BlockSpec((tm, d), lambda i: (jnp.maximum(i - front_tiles, 0), 0))],
        out_specs=[pl.BlockSpec((tm, d), lambda i: (i, 0)), pl.BlockSpec((1, d), lambda i: (0, 0)),
                   pl.BlockSpec((1, 128), lambda i: (0, 0))],
        out_shape=[jax.ShapeDtypeStruct((tp, d), F32), jax.ShapeDtypeStruct((1, d), F32),
                   jax.ShapeDtypeStruct((1, 128), F32)],
        compiler_params=_cp(1))(h, gain, target)


def _gated_headnorm(o, g, gain):
    return o * _rms(o) * gain * (g * _sigmoid(g))


def _row_mask(chunk, size=CHUNK):
    rows = chunk * size + lax.broadcasted_iota(jnp.int32, (size, 1), 0)
    return (rows >= FRONT - N_META).astype(F32)


def _ret_head(q1, q2, k1, k2, v, g, state, gain, cos, sin, dmat, dq, dk, dc):
    q = jnp.concatenate([q1 * cos - q2 * sin, q1 * sin + q2 * cos], axis=-1)
    k = jnp.concatenate([k1 * cos - k2 * sin, k1 * sin + k2 * cos], axis=-1) * (RET_DK ** -0.5)
    scores = _nt(q, k) * dmat
    o = _nn(scores, v) + _nn(q * dq, state)
    new_state = state * dc + _tn(k * dk, v)
    return _gated_headnorm(o, g, gain), new_state


def _ret_consts():
    log_gamma = jnp.log1p(-2.0 ** (-5.0 - jnp.arange(HEADS, dtype=F32)))
    idx = jnp.arange(RET_CHUNK, dtype=F32)
    rel = idx[:, None] - idx[None, :]
    dmat = jnp.where(rel >= 0, jnp.exp(log_gamma[:, None, None] * jnp.maximum(rel, 0.0)), 0.0)
    dq = jnp.exp(log_gamma[:, None] * (idx + 1.0))[..., None]
    dk = jnp.exp(log_gamma[:, None] * (RET_CHUNK - 1.0 - idx))[..., None]
    dc = jnp.broadcast_to(jnp.exp(log_gamma * RET_CHUNK)[:, None, None], (HEADS, 1, 128))
    return dmat, dq, dk, dc


def _rope_tables(tp):
    half = RET_DK // 2
    inv = 1.0 / (ROPE_BASE ** jnp.linspace(0.0, 1.0, half, dtype=F32))
    pos = (jnp.arange(tp) - (FRONT - N_META)).astype(F32)
    ang = pos[:, None] * inv[None, :]
    return jnp.cos(ang), jnp.sin(ang)


_RET_V0, _RET_G0 = 2 * D, 4 * D


def _heads(ref, start, width, stride=None):
    stride = width if stride is None else stride
    return jnp.stack([ref[:, start + stride * h:start + stride * h + width].astype(F32) for h in range(HEADS)])


def _put_heads(ref, start, value, mask, stride=None):
    width = value.shape[-1]
    stride = width if stride is None else stride
    for h in range(HEADS):
        ref[:, start + stride * h:start + stride * h + width] = (value[h] * mask).astype(ref.dtype)


def _ret_pieces(u_ref):
    hk = RET_DK // 2
    return (_heads(u_ref, 0, hk, RET_DK), _heads(u_ref, hk, hk, RET_DK), _heads(u_ref, D, hk, RET_DK),
            _heads(u_ref, D + hk, hk, RET_DK), _heads(u_ref, _RET_V0, RET_DV), _heads(u_ref, _RET_G0, RET_DV))


def _ret_const_specs(rev=None):
    c = (lambda n: (rev(n), 0)) if rev else (lambda n: (n, 0))
    z3 = lambda n: (0, 0, 0)
    return [pl.BlockSpec((RET_CHUNK, RET_DK // 2), c), pl.BlockSpec((RET_CHUNK, RET_DK // 2), c),
            pl.BlockSpec((HEADS, RET_CHUNK, RET_CHUNK), z3), pl.BlockSpec((HEADS, RET_CHUNK, 1), z3),
            pl.BlockSpec((HEADS, RET_CHUNK, 1), z3), pl.BlockSpec((HEADS, 1, 128), z3)]


def _ret_fwd(u, gain, rope, name):
    tp = u.shape[0]
    nch = tp // RET_CHUNK
    cos, sin = rope
    dmat, dq, dk, dc = _ret_consts()

    def body(u_ref, gain_ref, cos_ref, sin_ref, dmat_ref, dq_ref, dk_ref, dc_ref, on_ref, st_ref, state_ref):
        @pl.when(pl.program_id(0) == 0)
        def _():
            state_ref[...] = jnp.zeros_like(state_ref)

        state = state_ref[...]
        st_ref[...] = state.astype(BF16)
        on, new_state = _ret_head(*_ret_pieces(u_ref), state, _heads(gain_ref, 0, RET_DV), cos_ref[...], sin_ref[...],
                                  dmat_ref[...], dq_ref[...], dk_ref[...], dc_ref[...][:, :, :1])
        state_ref[...] = new_state
        _put_heads(on_ref, 0, on, 1.0)

    return pl.pallas_call(
        body, name=name, grid=(nch,),
        in_specs=[pl.BlockSpec((RET_CHUNK, 6 * D), lambda n: (n, 0)), pl.BlockSpec((1, HEADS * RET_DV), lambda n: (0, 0))]
                 + _ret_const_specs(),
        out_specs=[pl.BlockSpec((RET_CHUNK, HEADS * RET_DV), lambda n: (n, 0)),
                   pl.BlockSpec((None, HEADS, RET_DK, RET_DV), lambda n: (n, 0, 0, 0))],
        out_shape=[jax.ShapeDtypeStruct((tp, HEADS * RET_DV), BF16),
                   jax.ShapeDtypeStruct((nch, HEADS, RET_DK, RET_DV), BF16)],
        scratch_shapes=[pltpu.VMEM((HEADS, RET_DK, RET_DV), F32)],
        compiler_params=_cp(1))(u, gain, cos, sin, dmat, dq, dk, dc)


def _ret_bwd(u, gain, rope, states, d_on, name):
    tp = u.shape[0]
    nch = tp // RET_CHUNK
    cos, sin = rope
    dmat, dq, dk, dc = _ret_consts()
    rev = lambda n: nch - 1 - n
    hk = RET_DK // 2

    def body(u_ref, gain_ref, st_ref, don_ref, cos_ref, sin_ref, dmat_ref, dq_ref, dk_ref, dc_ref,
             du_ref, dgain_ref, dstate_ref):
        @pl.when(pl.program_id(0) == 0)
        def _():
            dstate_ref[...] = jnp.zeros_like(dstate_ref)
            dgain_ref[...] = jnp.zeros_like(dgain_ref)

        mask = _row_mask(rev(pl.program_id(0)), RET_CHUNK)
        consts = (cos_ref[...], sin_ref[...], dmat_ref[...], dq_ref[...], dk_ref[...], dc_ref[...][:, :, :1])
        _, vjp = jax.vjp(lambda *a: _ret_head(*a, *consts), *_ret_pieces(u_ref), st_ref[...].astype(F32),
                         _heads(gain_ref, 0, RET_DV))
        dq1, dq2, dk1, dk2, dv, dg, dstate, dgain = vjp((_heads(don_ref, 0, RET_DV), dstate_ref[...]))
        dstate_ref[...] = dstate
        for hd in range(HEADS):
            dgain_ref[:, RET_DV * hd:RET_DV * (hd + 1)] += dgain[hd]
        _put_heads(du_ref, 0, dq1, mask, RET_DK)
        _put_heads(du_ref, hk, dq2, mask, RET_DK)
        _put_heads(du_ref, D, dk1, mask, RET_DK)
        _put_heads(du_ref, D + hk, dk2, mask, RET_DK)
        _put_heads(du_ref, _RET_V0, dv, mask)
        _put_heads(du_ref, _RET_G0, dg, mask)

    return pl.pallas_call(
        body, name=name, grid=(nch,),
        in_specs=[pl.BlockSpec((RET_CHUNK, 6 * D), lambda n: (rev(n), 0)),
                  pl.BlockSpec((1, HEADS * RET_DV), lambda n: (0, 0)),
                  pl.BlockSpec((None, HEADS, RET_DK, RET_DV), lambda n: (rev(n), 0, 0, 0)),
                  pl.BlockSpec((RET_CHUNK, HEADS * RET_DV), lambda n: (rev(n), 0))] + _ret_const_specs(rev),
        out_specs=[pl.BlockSpec((RET_CHUNK, 6 * D), lambda n: (rev(n), 0)),
                   pl.BlockSpec((1, HEADS * RET_DV), lambda n: (0, 0))],
        out_shape=[jax.ShapeDtypeStruct((tp, 6 * D), BF16), jax.ShapeDtypeStruct((1, HEADS * RET_DV), F32)],
        scratch_shapes=[pltpu.VMEM((HEADS, RET_DK, RET_DV), F32)],
        compiler_params=_cp(1))(u, gain, states, d_on, cos, sin, dmat, dq, dk, dc)


_GLA_K0, _GLA_V0, _GLA_G0, _GLA_Z0 = 512, 1024, 2048, 3072


def _gla_head(q, k, v, g, z, state_t, wg, bg, gain, mask, lo, lo_t, loc, loc_t):
    ga = _nn(jnp.broadcast_to(z, wg.shape[:-2] + z.shape), wg) + bg
    log_a = (jnp.minimum(ga, 0.0) - jnp.log(1.0 + jnp.exp(-jnp.abs(ga)))) * (mask * (1.0 / GLA_TAU))
    bcum = _cum(lo, lo_t, log_a)
    bmid = _cum(loc, loc_t, log_a)
    btot = jnp.sum(log_a, axis=-2, keepdims=True)
    qs = q * (GLA_DK ** -0.5)
    causal = lax.broadcasted_iota(jnp.int32, (CHUNK, CHUNK), 0) >= lax.broadcasted_iota(jnp.int32, (CHUNK, CHUNK), 1)
    scores = jnp.where(causal, _nt(qs * jnp.exp(bmid), k * jnp.exp(-bmid)), 0.0)
    o = _nn(scores, v) + _nt(qs * jnp.exp(bcum), state_t)
    new_state_t = state_t * jnp.exp(btot) + _tn(v, k * jnp.exp(btot - bcum))
    return _gated_headnorm(o, g, gain), new_state_t


def _cum_mats():
    r = lax.broadcasted_iota(jnp.int32, (CHUNK, CHUNK), 0)
    c = lax.broadcasted_iota(jnp.int32, (CHUNK, CHUNK), 1)
    mid = CHUNK // 2
    low = lambda a, b: (a >= b).astype(F32)
    lo, lo_t = low(r, c), low(c, r)
    loc = lo - (c <= mid).astype(F32)
    loc_t = lo_t - (r <= mid).astype(F32)
    return tuple(m.astype(BF16) for m in (lo, lo_t, loc, loc_t))


def _gla_pieces(u_ref):
    return (_heads(u_ref, 0, GLA_DK), _heads(u_ref, _GLA_K0, GLA_DK), _heads(u_ref, _GLA_V0, GLA_DV),
            _heads(u_ref, _GLA_G0, GLA_DV), u_ref[:, _GLA_Z0:].astype(F32))


def _gla_fwd(u, wg, bg, gain, name):
    tp = u.shape[0]
    nch = tp // CHUNK

    def body(u_ref, wg_ref, bg_ref, gain_ref, on_ref, st_ref, state_ref):
        @pl.when(pl.program_id(0) == 0)
        def _():
            state_ref[...] = jnp.zeros_like(state_ref)

        state = state_ref[...]
        st_ref[...] = state.astype(BF16)
        on, new_state = _gla_head(*_gla_pieces(u_ref), state, _heads(wg_ref, 0, GLA_DK), _heads(bg_ref, 0, GLA_DK),
                                  _heads(gain_ref, 0, GLA_DV), _row_mask(pl.program_id(0)), *_cum_mats())
        state_ref[...] = new_state
        _put_heads(on_ref, 0, on, 1.0)

    return pl.pallas_call(
        body, name=name, grid=(nch,),
        in_specs=[pl.BlockSpec((CHUNK, GLA_U), lambda n: (n, 0)), pl.BlockSpec((128, HEADS * GLA_DK), lambda n: (0, 0)),
                  pl.BlockSpec((1, HEADS * GLA_DK), lambda n: (0, 0)), pl.BlockSpec((1, HEADS * GLA_DV), lambda n: (0, 0))],
        out_specs=[pl.BlockSpec((CHUNK, HEADS * GLA_DV), lambda n: (n, 0)),
                   pl.BlockSpec((None, HEADS, GLA_DV, GLA_DK), lambda n: (n, 0, 0, 0))],
        out_shape=[jax.ShapeDtypeStruct((tp, HEADS * GLA_DV), BF16),
                   jax.ShapeDtypeStruct((nch, HEADS, GLA_DV, GLA_DK), BF16)],
        scratch_shapes=[pltpu.VMEM((HEADS, GLA_DV, GLA_DK), F32)],
        compiler_params=_cp(1))(u, wg, bg, gain)


def _gla_bwd(u, wg, bg, gain, states, d_on, name):
    tp = u.shape[0]
    nch = tp // CHUNK
    rev = lambda n: nch - 1 - n

    def body(u_ref, wg_ref, bg_ref, gain_ref, st_ref, don_ref, du_ref, dwg_ref, dbg_ref, dgain_ref, dstate_ref):
        @pl.when(pl.program_id(0) == 0)
        def _():
            dstate_ref[...] = jnp.zeros_like(dstate_ref)
            dwg_ref[...] = jnp.zeros_like(dwg_ref)
            dbg_ref[...] = jnp.zeros_like(dbg_ref)
            dgain_ref[...] = jnp.zeros_like(dgain_ref)

        mask = _row_mask(rev(pl.program_id(0)))
        mats = _cum_mats()
        _, vjp = jax.vjp(lambda *a: _gla_head(*a, mask, *mats), *_gla_pieces(u_ref), st_ref[...].astype(F32),
                         _heads(wg_ref, 0, GLA_DK), _heads(bg_ref, 0, GLA_DK), _heads(gain_ref, 0, GLA_DV))
        dq, dk, dv, dg, dz, dstate, dwg, dbg, dgain = vjp((_heads(don_ref, 0, GLA_DV), dstate_ref[...]))
        dstate_ref[...] = dstate
        for hd in range(HEADS):
            dwg_ref[:, GLA_DK * hd:GLA_DK * (hd + 1)] += dwg[hd]
            dbg_ref[:, GLA_DK * hd:GLA_DK * (hd + 1)] += dbg[hd]
            dgain_ref[:, GLA_DV * hd:GLA_DV * (hd + 1)] += dgain[hd]
        _put_heads(du_ref, 0, dq, mask)
        _put_heads(du_ref, _GLA_K0, dk, mask)
        _put_heads(du_ref, _GLA_V0, dv, mask)
        _put_heads(du_ref, _GLA_G0, dg, mask)
        du_ref[:, _GLA_Z0:] = dz.astype(BF16)

    full = lambda r, c: pl.BlockSpec((r, c), lambda n: (0, 0))
    return pl.pallas_call(
        body, name=name, grid=(nch,),
        in_specs=[pl.BlockSpec((CHUNK, GLA_U), lambda n: (rev(n), 0)), full(128, HEADS * GLA_DK),
                  full(1, HEADS * GLA_DK), full(1, HEADS * GLA_DV),
                  pl.BlockSpec((None, HEADS, GLA_DV, GLA_DK), lambda n: (rev(n), 0, 0, 0)),
                  pl.BlockSpec((CHUNK, HEADS * GLA_DV), lambda n: (rev(n), 0))],
        out_specs=[pl.BlockSpec((CHUNK, GLA_U), lambda n: (rev(n), 0)), full(128, HEADS * GLA_DK),
                   full(1, HEADS * GLA_DK), full(1, HEADS * GLA_DV)],
        out_shape=[jax.ShapeDtypeStruct((tp, GLA_U), BF16), jax.ShapeDtypeStruct((128, HEADS * GLA_DK), F32),
                   jax.ShapeDtypeStruct((1, HEADS * GLA_DK), F32), jax.ShapeDtypeStruct((1, HEADS * GLA_DV), F32)],
        scratch_shapes=[pltpu.VMEM((HEADS, GLA_DV, GLA_DK), F32)],
        compiler_params=_cp(1))(u, wg, bg, gain, states, d_on)


def _ffn_fwd(h, gain, w_in, w_out, tag):
    hn, ug, uu, act = _norm_ffn_in(h, gain, w_in, f"{tag}_in")
    if callable(w_out):
        w_out = w_out(act)
    return _out_proj(act, w_out, h, 0.5, f"{tag}_out"), (h, hn, ug, uu, act), w_out


def _ffn_bwd(dh, saved, gain, w_in, w_out, tag, push):
    h, hn, ug, uu, act = saved
    du = _ffn_dact(dh, w_out, ug, uu, f"{tag}_dact")
    d_w_out = _wgrad(act, dh, bm=D_FF // 2, bn=D, scale=0.5, sharded=False, name=f"{tag}_dwout")
    d_w_in = _wgrad(hn, du, bm=D, bn=w_in.shape[2], scale=1.0, sharded=True, name=f"{tag}_dwin")
    token = push([d_w_in, d_w_out])
    return _dgrad_norm(du, w_in, h, gain + token[0, 0], dh, f"{tag}_dnorm")


def _sequence_grads(x, target, p, weights, grads):
    row = lambda v, token: v.reshape(1, -1) + token[0, 0]
    gains = {}

    tok = weights.start(1, weights.start(0, None))
    weights.pin = tok
    h = jnp.concatenate([jnp.zeros((FRONT, D), F32), x], axis=0) + tok[0, 0]
    rope = _rope_tables(h.shape[0])
    w = weights.wait(0, [tok, h, *rope, *weights.later_shards(2)])
    tok = weights.start(2, w["l0_ffn1_in"])
    h = lax.dynamic_update_slice(h, w["meta"], (FRONT - N_META, 0))
    gains["l0_ffn1"] = row(p["norm_ffn1"][0], tok)
    h, s1, w["l0_ffn1_out"] = _ffn_fwd(h, gains["l0_ffn1"], w["l0_ffn1_in"],
                                       lambda act: weights.wait(1, act)["l0_ffn1_out"], "l0_ffn1")
    w.update(weights.wait(2, h))
    tok = weights.start(3, w["ret_in"])
    gains["ret"] = row(p["norm_mix"][0], tok)
    hn, u = _norm_proj(h, gains["ret"], w["ret_in"], "ret_in")
    on, states = _ret_fwd(u, w["ret_gain"], rope, "ret_fwd")
    h_mix = _out_proj(on, w["ret_out"], h, 1.0, "ret_out")
    s2 = (h, hn, u, on, states)
    w.update(weights.wait(3, h_mix))
    tok = weights.start(4, w["l0_ffn2_in"])
    gains["l0_ffn2"] = row(p["norm_ffn2"][0], tok)
    h, s3, _ = _ffn_fwd(h_mix, gains["l0_ffn2"], w["l0_ffn2_in"], w["l0_ffn2_out"], "l0_ffn2")
    saved = [(s1, s2, s3)]

    w.update(weights.wait(4, h))
    tok = weights.start(5, w["l1_ffn1_in"])
    gains["l1_ffn1"] = row(p["norm_ffn1"][1], tok)
    h, s1, _ = _ffn_fwd(h, gains["l1_ffn1"], w["l1_ffn1_in"], w["l1_ffn1_out"], "l1_ffn1")
    w.update(weights.wait(5, h))
    tok = weights.start(6, w["gla_out"])
    gains["gla"] = row(p["norm_mix"][1], tok)
    hn, u = _norm_proj(h, gains["gla"], w["gla_in"], "gla_in")
    on, states = _gla_fwd(u, w["gla_wg"], w["gla_bg"], w["gla_gain"], "gla_fwd")
    h_mix = _out_proj(on, w["gla_out"], h, 1.0, "gla_out")
    s2 = (h, hn, u, on, states)
    w.update(weights.wait(6, h_mix))
    gains["l1_ffn2"] = p["norm_ffn2"][1].reshape(1, -1)
    h, s3, _ = _ffn_fwd(h_mix, gains["l1_ffn2"], w["l1_ffn2_in"], w["l1_ffn2_out"], "l1_ffn2")
    saved.append((s1, s2, s3))

    dh, d_final, loss = _loss_head(h, p["final_norm"].reshape(1, -1), target, "loss_head")
    small = {"final_norm": d_final, "norm_ffn1": [None, None], "norm_mix": [None, None], "norm_ffn2": [None, None]}
    pusher = lambda k: functools.partial(grads.push, k)

    s1, s2, s3 = saved[1]
    dh, small["norm_ffn2"][1] = _ffn_bwd(dh, s3, gains["l1_ffn2"], w["l1_ffn2_in"], w["l1_ffn2_out"], "l1_ffn2",
                                         pusher(0))
    h_in, hn, u, on, states = s2
    d_on = _dgrad(dh, w["gla_out"], "gla_don")
    d_out = _wgrad(on, dh, bm=D, bn=D, scale=1.0, sharded=False, name="gla_dwout")
    du, small["gla_wg"], small["gla_bg"], small["gla_gain"] = _gla_bwd(u, w["gla_wg"], w["gla_bg"], w["gla_gain"],
                                                                       states, d_on, "gla_bwd")
    d_in = _wgrad(hn, du, bm=D, bn=GLA_U // 5, scale=1.0, sharded=False, name="gla_dwin")
    d_in = jnp.moveaxis(d_in[:, :GLA_IN].reshape(D, N_CHIPS, -1), 1, 0)
    tok = grads.push(1, [d_in, d_out])
    dh, small["norm_mix"][1] = _dgrad_norm(du, w["gla_in"], h_in, gains["gla"] + tok[0, 0], dh, "gla_dnorm")
    dh, small["norm_ffn1"][1] = _ffn_bwd(dh, s1, gains["l1_ffn1"], w["l1_ffn1_in"], w["l1_ffn1_out"], "l1_ffn1",
                                         pusher(2))

    s1, s2, s3 = saved[0]
    dh, small["norm_ffn2"][0] = _ffn_bwd(dh, s3, gains["l0_ffn2"], w["l0_ffn2_in"], w["l0_ffn2_out"], "l0_ffn2",
                                         pusher(3))
    h_in, hn, u, on, states = s2
    d_on = _dgrad(dh, w["ret_out"], "ret_don")
    d_out = _wgrad(on, dh, bm=D, bn=D, scale=1.0, sharded=False, name="ret_dwout")
    du, small["ret_gain"] = _ret_bwd(u, w["ret_gain"], rope, states, d_on, "ret_bwd")
    d_in = _wgrad(hn, du, bm=D, bn=w["ret_in"].shape[2], scale=1.0, sharded=True, name="ret_dwin")
    tok = grads.push(4, [d_in, d_out])
    dh, small["norm_mix"][0] = _dgrad_norm(du, w["ret_in"], h_in, gains["ret"] + tok[0, 0], dh, "ret_dnorm")
    dh, small["norm_ffn1"][0] = _ffn_bwd(dh, s1, gains["l0_ffn1"], w["l0_ffn1_in"], w["l0_ffn1_out"], "l0_ffn1",
                                         pusher(5))
    grads.push(6, [], [dh[FRONT - N_META:FRONT], *small["norm_ffn1"], *small["norm_mix"], *small["norm_ffn2"],
                       small["final_norm"], small["ret_gain"], small["gla_wg"][:GLA_RANK], small["gla_bg"],
                       small["gla_gain"], loss[:, :1]])
    return dh[FRONT:]


_HBM = pl.BlockSpec(memory_space=pl.ANY)


def _place():
    return lax.axis_index("x"), lax.axis_index("y"), lax.axis_index("c")


def _flip(v, bit):
    return 1 - v if bit else v


DMA_CHUNK_BYTES = 128 * 1024


def _row_chunks(ref):
    rows, cols = ref.shape
    step = _row_tile(rows, max(16, DMA_CHUNK_BYTES // (cols * ref.dtype.itemsize)))
    return [pl.ds(a, step) for a in range(0, rows, step)]


def _whole(src, dst, send_sem, recv_sem, peer):
    return pltpu.make_async_remote_copy(src_ref=src, dst_ref=dst, send_sem=send_sem, recv_sem=recv_sem,
                                        device_id=peer, device_id_type=MESH)


def _send(src, dst, send_sem, recv_sem, peer):
    for rows in _row_chunks(src):
        _whole(src.at[rows], dst.at[rows], send_sem, recv_sem, peer).start()
    return _whole(src, dst, send_sem, recv_sem, peer)


_HBM_ONLY = pl.BlockSpec(memory_space=pltpu.HBM)
_SEMS = pl.BlockSpec(memory_space=pltpu.SEMAPHORE)
_SIDE_EFFECT = pltpu.CompilerParams(has_side_effects=pltpu.SideEffectType.DATAFLOW_SIDE_EFFECTING)
_CHIP_FLIPS = [(1, 0, 0), (0, 1, 0), (1, 1, 0)]
_PEER_FLIPS = [(fx, fy, fc) for fx in (0, 1) for fy in (0, 1) for fc in (0, 1)][1:]


def _zero_token():
    return jnp.zeros((8, 128), F32)


def _exchange_start(srcs, lands, route, flips, after, name):
    n = len(srcs)

    def body(*refs):
        src, land = refs[:n], refs[n:2 * n]
        send_sems, recv_sems, token = refs[2 * n + 1], refs[2 * n + 2], refs[-1]
        me = _place()
        for t in range(n):
            for j, flip in enumerate(flips):
                peer = tuple(_flip(v, f) for v, f in zip(me, flip))
                s, d = route(t, src[t], land[t], me, peer)
                _send(s, d, send_sems.at[t * len(flips) + j], recv_sems.at[t * len(flips) + j], peer)
        token[...] = jnp.zeros_like(token)

    hbm = lambda a: pltpu.HBM(a.shape, a.dtype)
    sems = pltpu.SemaphoreType.DMA((n * len(flips),))
    operands = [pltpu.with_memory_space_constraint(a, pltpu.HBM) for a in list(srcs) + list(lands)]
    out = pl.pallas_call(
        body, name=name, in_specs=[_HBM_ONLY] * (2 * n) + [_HBM],
        out_shape=(sems, sems, *[hbm(a) for a in operands], jax.ShapeDtypeStruct((8, 128), F32)),
        out_specs=(_SEMS, _SEMS, *[_HBM_ONLY] * (2 * n), pl.BlockSpec(memory_space=pltpu.VMEM)),
        input_output_aliases={i: 2 + i for i in range(2 * n)}, compiler_params=_SIDE_EFFECT,
    )(*operands, _zero_token() if after is None else after)
    return (out[0], out[1], out[2:2 + n], out[2 + n:2 + 2 * n]), out[-1]


def _exchange_wait(started, route, flips, after, name):
    send_sems, recv_sems, srcs, lands = started
    n = len(srcs)

    def body(*refs):
        src, land = refs[:n], refs[n:2 * n]
        send_sems, recv_sems = refs[2 * n], refs[2 * n + 1]
        me = _place()
        for t in range(n):
            for j, flip in enumerate(flips):
                peer = tuple(_flip(v, f) for v, f in zip(me, flip))
                s, d = route(t, src[t], land[t], me, peer)
                cp = _whole(s, d, send_sems.at[t * len(flips) + j], recv_sems.at[t * len(flips) + j], peer)
                cp.wait_send()
                cp.wait_recv()

    hbm = lambda a: pltpu.HBM(a.shape, a.dtype)
    after = list(after) if isinstance(after, (list, tuple)) else [after]
    out = pl.pallas_call(
        body, name=name, in_specs=[_HBM_ONLY] * (2 * n) + [_SEMS, _SEMS] + [_HBM] * len(after),
        out_shape=tuple(hbm(a) for a in list(srcs) + list(lands)), out_specs=tuple([_HBM_ONLY] * (2 * n)),
        input_output_aliases={i: i for i in range(2 * n)}, compiler_params=_SIDE_EFFECT,
    )(*srcs, *lands, send_sems, recv_sems, *after)
    return out[:n], out[n:]


def _gather_route(t, src, land, me, peer):
    return src, land.at[2 * me[0] + me[1]]


def _scatter_route(n_pieces):
    def route(t, src, land, me, peer):
        part = src.at[2 * peer[0] + peer[1], peer[2]] if t < n_pieces else src
        return part, land.at[4 * me[0] + 2 * me[1] + me[2]]

    return route


def _swap_cores(halves):
    n = len(halves)

    def body(*refs):
        src, dst = refs[:n], refs[n:2 * n]
        send_sems, recv_sems = refs[2 * n:]
        x, y, c = _place()
        copies = [_send(src[t], dst[t], send_sems.at[t], recv_sems.at[t], (x, y, 1 - c)) for t in range(n)]
        for cp in copies:
            cp.wait()

    got = pl.pallas_call(
        body, name="swap_cores", in_specs=[_HBM] * n, out_specs=[_HBM] * n,
        out_shape=[jax.ShapeDtypeStruct(a.shape, a.dtype) for a in halves],
        scratch_shapes=[pltpu.SemaphoreType.DMA((n,)), pltpu.SemaphoreType.DMA((n,))],
    )(*halves)
    south = lax.axis_index("c") == 0
    return [jnp.stack([jnp.where(south, a, b), jnp.where(south, b, a)]) for a, b in zip(halves, got)]


def _row_tile(rows, cap):
    fits = [t for t in range(16, cap + 1, 16) if rows % t == 0]
    return fits[-1] if fits else rows


def _sum_slots(a, name):
    _, r, c = a.shape
    tr = _row_tile(r, 384)

    def body(a_ref, o_ref):
        s = a_ref[0].astype(F32)
        for k in range(1, N_DEV):
            s = s + a_ref[k].astype(F32)
        o_ref[...] = s

    return pl.pallas_call(
        body, name=name, grid=(r // tr,),
        in_specs=[pl.BlockSpec((N_DEV, tr, c), lambda i: (0, i, 0))],
        out_specs=pl.BlockSpec((tr, c), lambda i: (i, 0)),
        out_shape=jax.ShapeDtypeStruct((r, c), F32),
        compiler_params=_cp(1))(a)


def _adamw(w, g, m, v, name):
    r, c = w.shape
    tr = _row_tile(r, 256)

    def body(w_ref, g_ref, m_ref, v_ref, d_ref, nm_ref, nv_ref):
        gv = g_ref[...]
        nm = ADAM_B1 * m_ref[...] + (1.0 - ADAM_B1) * gv
        nv = ADAM_B2 * v_ref[...] + (1.0 - ADAM_B2) * (gv * gv)
        m_hat = nm / (1.0 - ADAM_B1 ** ADAM_STEP)
        v_hat = nv / (1.0 - ADAM_B2 ** ADAM_STEP)
        d_ref[...] = -ADAM_LR * (m_hat / (jnp.sqrt(v_hat) + ADAM_EPS) + ADAM_WD * w_ref[...])
        nm_ref[...] = nm
        nv_ref[...] = nv

    spec = pl.BlockSpec((tr, c), lambda i: (i, 0))
    return pl.pallas_call(
        body, name=name, grid=(r // tr,), in_specs=[spec] * 4, out_specs=[spec] * 3,
        out_shape=[jax.ShapeDtypeStruct((r, c), F32)] * 3,
        compiler_params=_cp(1))(w, g, m, v)


_SMALL = ["meta_tokens", "ret_head_norm", "gla_w_gate", "gla_b_gate", "gla_head_norm"]
_LOCAL_SMALL = ["meta_tokens", "norm_ffn1", "norm_mix", "norm_ffn2", "ret_head_norm", "gla_w_gate", "gla_b_gate",
                "gla_head_norm", "final_norm"]
_BIG = ["ffn1_w_in", "ffn1_w_out", "ffn2_w_in", "ffn2_w_out", "ret_w_in", "ret_w_out", "gla_w_in", "gla_w_out"]
_WEIGHTS = ["meta_tokens", "norm_ffn1", "ffn1_w_in", "ffn1_w_out", "norm_mix", "norm_ffn2", "ffn2_w_in", "ffn2_w_out",
            "ret_w_in", "ret_head_norm", "ret_w_out", "gla_w_in", "gla_w_gate", "gla_b_gate", "gla_head_norm",
            "gla_w_out", "final_norm"]


def _pack_rows(arrays, width):
    flat = jnp.concatenate([a.reshape(-1) for a in arrays])
    pad = -flat.shape[0] % (8 * width)
    return jnp.pad(flat, (0, pad)).reshape(-1, width)


def _unpack_rows(packed, shapes):
    flat, out, at = packed.reshape(-1), [], 0
    for s in shapes:
        size = 1
        for dim in s:
            size *= dim
        out.append(flat[at:at + size].reshape(s))
        at += size
    return out


class _WeightGather:
    GROUPS = [("small", "l0_ffn1_in"), ("l0_ffn1_out",), ("ret_in", "ret_out"), ("l0_ffn2_in", "l0_ffn2_out"),
              ("l1_ffn1_in", "l1_ffn1_out"), ("gla_in", "gla_out"), ("l1_ffn2_in", "l1_ffn2_out")]

    def __init__(self, p):
        self.small_shapes = [p[name].shape for name in _SMALL]
        self.f32 = {"small": _pack_rows([p[name] for name in _SMALL], 128), "ret_in": p["ret_w_in"][0],
                    "ret_out": p["ret_w_out"][0], "gla_in": p["gla_w_in"][0], "gla_out": p["gla_w_out"][0]}
        for layer in range(2):
            for name in ("ffn1", "ffn2"):
                self.f32[f"l{layer}_{name}_in"] = p[f"{name}_w_in"][layer]
                self.f32[f"l{layer}_{name}_out"] = p[f"{name}_w_out"][layer]
        self.shards = {}
        self.started = {}
        self.pin = None

    def shard(self, name):
        if name not in self.shards:
            a = self.f32[name]
            if name != "small":
                a = (a if self.pin is None else a + self.pin[0, 0]).astype(BF16)
            self.shards[name] = a
        return self.shards[name]

    def later_shards(self, k):
        return [self.shard(name) for group in self.GROUPS[k:] for name in group]

    def start(self, k, after):
        shards = [self.shard(name) for name in self.GROUPS[k]]
        mine = 2 * lax.axis_index("x") + lax.axis_index("y")
        lands = [lax.dynamic_update_index_in_dim(lax.empty((N_CHIPS,) + s.shape, s.dtype), s, mine, 0) for s in shards]
        self.started[k], token = _exchange_start(shards, lands, _gather_route, _CHIP_FLIPS, after, f"gather{k}_start")
        return token

    def wait(self, k, after):
        _, got = _exchange_wait(self.started[k], _gather_route, _CHIP_FLIPS, after, f"gather{k}_wait")
        w = {}
        for name, g in zip(self.GROUPS[k], got):
            if name == "small":
                parts = zip(*[_unpack_rows(g[chip], self.small_shapes) for chip in range(N_CHIPS)])
                cat = lambda a: jnp.moveaxis(a, 0, -2).reshape(a.shape[1:-1] + (-1,))
                meta, ret_gain, wg, bg, gla_gain = [cat(jnp.stack(part)) for part in parts]
                w.update(meta=meta, ret_gain=ret_gain.reshape(1, -1), gla_bg=bg.reshape(1, -1),
                         gla_gain=gla_gain.reshape(1, -1),
                         gla_wg=jnp.pad(wg[0], ((0, 128 - GLA_RANK), (0, 0))).astype(BF16))
            elif name == "gla_in":
                full = jnp.moveaxis(g, 0, 1).reshape(D, -1)
                w[name] = jnp.pad(full, ((0, 0), (0, GLA_U - GLA_IN)))[None]
            elif name.endswith("_out"):
                w[name] = g.reshape(-1, g.shape[-1])
            else:
                w[name] = g
        return w


class _GradExchange:
    def __init__(self):
        self.started = []
        self.token = None
        self.small_shapes = None

    def push(self, k, arrays, small=None):
        srcs = [a.reshape(N_CHIPS, 2, -1, a.shape[-1]) for a in arrays]
        lands = [lax.empty((N_DEV,) + a.shape[2:], a.dtype) for a in srcs]
        if small is not None:
            self.small_shapes = [a.shape for a in small]
            srcs.append(_pack_rows(small, D))
            lands.append(lax.empty((N_DEV,) + srcs[-1].shape, F32))
        started, self.token = _exchange_start(srcs, lands, _scatter_route(len(arrays)), _PEER_FLIPS, None,
                                              f"scatter{k}_start")
        self.started.append((started, len(arrays)))
        return self.token

    def collect(self):
        x, y, c = _place()
        after, sums = self.token, []
        for k, (started, n_pieces) in enumerate(self.started):
            srcs, got = _exchange_wait(started, _scatter_route(n_pieces), _PEER_FLIPS, after, f"scatter{k}_wait")
            own = [a[2 * x + y, c] for a in srcs[:n_pieces]] + list(srcs[n_pieces:])
            got = [lax.dynamic_update_index_in_dim(g, a, 4 * x + 2 * y + c, 0) for g, a in zip(got, own)]
            sums.append([_sum_slots(a, f"sum{k}_{i}") for i, a in enumerate(got)])
            after = sums[-1][0]
        small = _unpack_rows(sums[-1].pop(), self.small_shapes)
        return sums, small


def kernel(x, meta_tokens, norm_ffn1, ffn1_w_in, ffn1_w_out, norm_mix, norm_ffn2, ffn2_w_in, ffn2_w_out, ret_w_in, ret_head_norm, ret_w_out, gla_w_in, gla_w_gate, gla_b_gate, gla_head_norm, gla_w_out, final_norm, loss_target, m_meta_tokens, m_norm_ffn1, m_ffn1_w_in, m_ffn1_w_out, m_norm_mix, m_norm_ffn2, m_ffn2_w_in, m_ffn2_w_out, m_ret_w_in, m_ret_head_norm, m_ret_w_out, m_gla_w_in, m_gla_w_gate, m_gla_b_gate, m_gla_head_norm, m_gla_w_out, m_final_norm, v_meta_tokens, v_norm_ffn1, v_ffn1_w_in, v_ffn1_w_out, v_norm_mix, v_norm_ffn2, v_ffn2_w_in, v_ffn2_w_out, v_ret_w_in, v_ret_head_norm, v_ret_w_out, v_gla_w_in, v_gla_w_gate, v_gla_b_gate, v_gla_head_norm, v_gla_w_out, v_final_norm):
    p = dict(meta_tokens=meta_tokens, norm_ffn1=norm_ffn1, ffn1_w_in=ffn1_w_in, ffn1_w_out=ffn1_w_out, norm_mix=norm_mix,
             norm_ffn2=norm_ffn2, ffn2_w_in=ffn2_w_in, ffn2_w_out=ffn2_w_out, ret_w_in=ret_w_in,
             ret_head_norm=ret_head_norm, ret_w_out=ret_w_out, gla_w_in=gla_w_in, gla_w_gate=gla_w_gate,
             gla_b_gate=gla_b_gate, gla_head_norm=gla_head_norm, gla_w_out=gla_w_out, final_norm=final_norm)
    m = dict(zip(_WEIGHTS, (m_meta_tokens, m_norm_ffn1, m_ffn1_w_in, m_ffn1_w_out, m_norm_mix, m_norm_ffn2, m_ffn2_w_in,
                            m_ffn2_w_out, m_ret_w_in, m_ret_head_norm, m_ret_w_out, m_gla_w_in, m_gla_w_gate,
                            m_gla_b_gate, m_gla_head_norm, m_gla_w_out, m_final_norm)))
    v = dict(zip(_WEIGHTS, (v_meta_tokens, v_norm_ffn1, v_ffn1_w_in, v_ffn1_w_out, v_norm_mix, v_norm_ffn2, v_ffn2_w_in,
                            v_ffn2_w_out, v_ret_w_in, v_ret_head_norm, v_ret_w_out, v_gla_w_in, v_gla_w_gate,
                            v_gla_b_gate, v_gla_head_norm, v_gla_w_out, v_final_norm)))

    exchange = _GradExchange()
    d_x = _sequence_grads(x[0], loss_target[0], p, _WeightGather(p), exchange)
    sums, small = exchange.collect()
    names = [("ffn2_in", 1), ("ffn2_out", 1), ("gla_in", 0), ("gla_out", 0), ("ffn1_in", 1), ("ffn1_out", 1),
             ("ffn2_in", 0), ("ffn2_out", 0), ("ret_in", 0), ("ret_out", 0), ("ffn1_in", 0), ("ffn1_out", 0)]
    swapped = _swap_cores([a for group in sums for a in group])
    shard = {key: a.reshape(-1, a.shape[-1]) for key, a in zip(names, swapped)}
    big = {name: [shard[name, layer] for layer in range(2) if (name, layer) in shard] for name, _ in names}

    chip = 2 * lax.axis_index("x") + lax.axis_index("y")
    cols = lambda a, n: lax.dynamic_slice_in_dim(a, chip * n, n, axis=a.ndim - 1)
    (s_meta, s_n1a, s_n1b, s_nma, s_nmb, s_n2a, s_n2b, s_final, s_ret_gain, s_wg, s_bg, s_gla_gain, s_loss) = small
    grads = {
        "meta_tokens": cols(s_meta, 256), "norm_ffn1": jnp.concatenate([s_n1a, s_n1b]),
        "norm_mix": jnp.concatenate([s_nma, s_nmb]), "norm_ffn2": jnp.concatenate([s_n2a, s_n2b]),
        "final_norm": s_final.reshape(D),
        "ret_head_norm": cols(s_ret_gain.reshape(1, HEADS, RET_DV), RET_DV // N_CHIPS),
        "gla_w_gate": cols(s_wg, GLA_DK)[None], "gla_b_gate": cols(s_bg, GLA_DK),
        "gla_head_norm": cols(s_gla_gain.reshape(1, HEADS, GLA_DV), GLA_DV // N_CHIPS),
        "ffn1_w_in": jnp.stack(big["ffn1_in"]), "ffn1_w_out": jnp.stack(big["ffn1_out"]),
        "ffn2_w_in": jnp.stack(big["ffn2_in"]), "ffn2_w_out": jnp.stack(big["ffn2_out"]),
        "ret_w_in": big["ret_in"][0][None], "ret_w_out": big["ret_out"][0][None],
        "gla_w_in": big["gla_in"][0][None], "gla_w_out": big["gla_out"][0][None],
    }

    delta, new_m, new_v = {}, {}, {}
    for name in _BIG:
        shape = p[name].shape
        flat = lambda a: a.reshape(-1, shape[-1])
        out = _adamw(flat(p[name]), flat(grads[name]), flat(m[name]), flat(v[name]), f"adamw_{name}")
        delta[name], new_m[name], new_v[name] = [a.reshape(shape) for a in out]
    packed = [_pack_rows([d[name] for name in _LOCAL_SMALL], 128) for d in (p, grads, m, v)]
    out = _adamw(*packed, "adamw_small")
    shapes = [p[name].shape for name in _LOCAL_SMALL]
    for d, a in zip((delta, new_m, new_v), out):
        d.update(zip(_LOCAL_SMALL, _unpack_rows(a, shapes)))

    return (s_loss.reshape(()), d_x[None], *[grads[n] for n in _WEIGHTS], *[delta[n] for n in _WEIGHTS],
            *[new_m[n] for n in _WEIGHTS], *[new_v[n] for n in _WEIGHTS])
```

```python
import functools

import jax
import jax.numpy as jnp
from jax import lax
from jax.experimental import pallas as pl
from jax.experimental.pallas import tpu as pltpu

F32, BF16 = jnp.float32, jnp.bfloat16
MESH = pl.DeviceIdType.MESH

D = 1024
N_META = 16
CHUNK = 64
RET_CHUNK = 128
FRONT = 256
D_FF = 2816
EPS = 1e-6
HEADS = 4
RET_DK, RET_DV = 256, 512
GLA_DK, GLA_DV = 128, 256
GLA_RANK = 16
GLA_TAU = 16.0
GLA_IN = 2 * HEADS * GLA_DK + 2 * HEADS * GLA_DV + GLA_RANK
GLA_U = 3200
ROPE_BASE = 10000.0
N_CHIPS = 4
N_DEV = 8

ADAM_LR, ADAM_B1, ADAM_B2, ADAM_EPS, ADAM_WD, ADAM_STEP = 0.001, 0.9, 0.999, 1e-08, 0.01, 10

VMEM_LIMIT_BYTES = 56 * 1024 * 1024
TM = 768
TM_SMALL = 256


TM_RESIDENT = 384
MXU_TILE = 256


def _cp(n_axes):
    return pltpu.CompilerParams(dimension_semantics=("arbitrary",) * n_axes, vmem_limit_bytes=VMEM_LIMIT_BYTES)


def _resident(shape, n_axes):
    zeros = (0,) * len(shape)
    index = (lambda i: zeros) if n_axes == 1 else (lambda i, j: zeros)
    return pl.BlockSpec(shape, index, pipeline_mode=pl.Buffered(1))


def _dg(a, b, ca, cb):
    nb = a.ndim - 2
    dims = (((ca + nb,), (cb + nb,)), (tuple(range(nb)), tuple(range(nb))))
    return lax.dot_general(a.astype(BF16), b.astype(BF16), dims, preferred_element_type=F32)


@jax.custom_vjp
def _nn(a, b):
    return _dg(a, b, 1, 0)


@jax.custom_vjp
def _nt(a, b):
    return _dg(a, b, 1, 1)


@jax.custom_vjp
def _tn(a, b):
    return _dg(a, b, 0, 0)


_nn.defvjp(lambda a, b: (_nn(a, b), (a, b)), lambda res, g: (_nt(g, res[1]), _tn(res[0], g)))
_nt.defvjp(lambda a, b: (_nt(a, b), (a, b)), lambda res, g: (_nn(g, res[1]), _tn(g, res[0])))
_tn.defvjp(lambda a, b: (_tn(a, b), (a, b)), lambda res, g: (_nt(res[1], g), _nn(res[0], g)))


def _split3_dot(m, a):
    a1 = a.astype(BF16)
    r1 = a - a1.astype(F32)
    a2 = r1.astype(BF16)
    a3 = (r1 - a2.astype(F32)).astype(BF16)
    mb = jnp.broadcast_to(m, a.shape[:-2] + m.shape)
    return _dg(mb, a1, 1, 0) + _dg(mb, a2, 1, 0) + _dg(mb, a3, 1, 0)


@jax.custom_vjp
def _cum(m, mt, a):
    return _split3_dot(m, a)


_cum.defvjp(lambda m, mt, a: (_split3_dot(m, a), (m, mt)),
            lambda res, g: (jnp.zeros_like(res[0]), jnp.zeros_like(res[1]), _split3_dot(res[1], g)))


def _sigmoid(x):
    return 1.0 / (1.0 + jnp.exp(-x))


def _rms(x):
    return lax.rsqrt(jnp.mean(x * x, axis=-1, keepdims=True) + EPS)


def _rmsnorm_bwd(dy, x, gain):
    r = _rms(x)
    xhat = x * r
    dxh = dy * gain
    return r * (dxh - xhat * jnp.mean(dxh * xhat, axis=-1, keepdims=True)), xhat


def _norm_proj(h, gain, w, name):
    tp, d = h.shape
    s, _, ns = w.shape

    tm = TM_RESIDENT

    def body(h_ref, g_ref, w_ref, hn_ref, u_ref):
        @pl.when(pl.program_id(1) == 0)
        def _():
            x = h_ref[...]
            hn_ref[...] = (x * _rms(x) * g_ref[...]).astype(BF16)

        u_ref[...] = jnp.dot(hn_ref[...], w_ref[pl.program_id(1)], preferred_element_type=F32).astype(BF16)

    return pl.pallas_call(
        body, name=name, grid=(tp // tm, s),
        in_specs=[pl.BlockSpec((tm, d), lambda i, j: (i, 0)), pl.BlockSpec((1, d), lambda i, j: (0, 0)),
                  _resident(w.shape, 2)],
        out_specs=[pl.BlockSpec((tm, d), lambda i, j: (i, 0)), pl.BlockSpec((tm, ns), lambda i, j: (i, j))],
        out_shape=[jax.ShapeDtypeStruct((tp, d), BF16), jax.ShapeDtypeStruct((tp, s * ns), BF16)],
        compiler_params=_cp(2))(h, gain, w)


def _norm_ffn_in(h, gain, w, name):
    tp, d = h.shape
    ff = w.shape[1] // 2
    tm = TM_RESIDENT
    blocks = [(c, min(c + 6 * MXU_TILE, ff)) for c in range(0, ff, 6 * MXU_TILE)]

    def body(h_ref, g_ref, w_ref, hn_ref, dg_ref, du_ref, act_ref):
        x = h_ref[...]
        a = (x * _rms(x) * g_ref[...]).astype(BF16)
        hn_ref[...] = a
        for c0, c1 in blocks:
            g = jnp.dot(a, w_ref[:, c0:c1], preferred_element_type=F32)
            u = jnp.dot(a, w_ref[:, ff + c0:ff + c1], preferred_element_type=F32)
            sg = _sigmoid(g)
            silu = g * sg
            dg_ref[:, c0:c1] = (u * (sg + silu * (1.0 - sg))).astype(BF16)
            du_ref[:, c0:c1] = silu.astype(BF16)
            act_ref[:, c0:c1] = (silu * u).astype(BF16)

    wide = jax.ShapeDtypeStruct((tp, ff), BF16)
    return pl.pallas_call(
        body, name=name, grid=(tp // tm,),
        in_specs=[pl.BlockSpec((tm, d), lambda i: (i, 0)), pl.BlockSpec((1, d), lambda i: (0, 0)),
                  _resident(w.shape, 1)],
        out_specs=[pl.BlockSpec((tm, d), lambda i: (i, 0))] + [pl.BlockSpec((tm, ff), lambda i: (i, 0))] * 3,
        out_shape=[jax.ShapeDtypeStruct((tp, d), BF16), wide, wide, wide],
        compiler_params=_cp(1))(h, gain, w)


def _out_proj(a, w, h, scale, name):
    tp, k = a.shape
    d = w.shape[1]

    def body(a_ref, w_ref, h_ref, o_ref):
        o_ref[...] = h_ref[...] + scale * jnp.dot(a_ref[...], w_ref[...], preferred_element_type=F32)

    return pl.pallas_call(
        body, name=name, grid=(tp // TM,),
        in_specs=[pl.BlockSpec((TM, k), lambda i: (i, 0)), pl.BlockSpec((k, d), lambda i: (0, 0)),
                  pl.BlockSpec((TM, d), lambda i: (i, 0))],
        out_specs=pl.BlockSpec((TM, d), lambda i: (i, 0)),
        out_shape=jax.ShapeDtypeStruct((tp, d), F32),
        compiler_params=_cp(1))(a, w, h)


def _ffn_dact(dh, w_out, act_dg, act_du, name):
    tp, d = dh.shape
    ff = w_out.shape[0]
    tm = TM_RESIDENT

    def body(dh_ref, w_ref, dg_ref, du_ref, o_ref):
        dy = (0.5 * dh_ref[...]).astype(BF16)
        dact = lax.dot_general(dy, w_ref[...], (((1,), (1,)), ((), ())), preferred_element_type=F32)
        o_ref[:, :ff] = (dact * dg_ref[...].astype(F32)).astype(BF16)
        o_ref[:, ff:] = (dact * du_ref[...].astype(F32)).astype(BF16)

    return pl.pallas_call(
        body, name=name, grid=(tp // tm,),
        in_specs=[pl.BlockSpec((tm, d), lambda i: (i, 0)), _resident(w_out.shape, 1),
                  pl.BlockSpec((tm, ff), lambda i: (i, 0)), pl.BlockSpec((tm, ff), lambda i: (i, 0))],
        out_specs=pl.BlockSpec((tm, 2 * ff), lambda i: (i, 0)),
        out_shape=jax.ShapeDtypeStruct((tp, 2 * ff), BF16),
        compiler_params=_cp(1))(dh, w_out, act_dg, act_du)


def _dgrad(dh, w, name):
    tp, d = dh.shape
    k = w.shape[0]

    def body(dh_ref, w_ref, o_ref):
        o_ref[...] = lax.dot_general(dh_ref[...].astype(BF16), w_ref[...], (((1,), (1,)), ((), ())),
                                     preferred_element_type=F32).astype(BF16)

    return pl.pallas_call(
        body, name=name, grid=(tp // TM,),
        in_specs=[pl.BlockSpec((TM, d), lambda i: (i, 0)), pl.BlockSpec((k, d), lambda i: (0, 0))],
        out_specs=pl.BlockSpec((TM, k), lambda i: (i, 0)),
        out_shape=jax.ShapeDtypeStruct((tp, k), BF16),
        compiler_params=_cp(1))(dh, w)


def _wgrad(a, b, *, bm, bn, scale, sharded, name):
    tp, m = a.shape
    n = b.shape[1]
    nk = tp // TM

    def body(a_ref, b_ref, o_ref, acc_ref):
        k = pl.program_id(2)

        @pl.when(k == 0)
        def _():
            acc_ref[...] = jnp.zeros_like(acc_ref)

        bb = b_ref[...]
        if scale != 1.0:
            bb = scale * bb
        acc_ref[...] += lax.dot_general(a_ref[...], bb.astype(BF16), (((0,), (0,)), ((), ())),
                                        preferred_element_type=F32)

        @pl.when(k == nk - 1)
        def _():
            o_ref[...] = acc_ref[...].astype(BF16)

    if sharded:
        assert m == bm
        out_spec = pl.BlockSpec((None, bm, bn), lambda i, j, k: (j, 0, 0))
        out_shape = jax.ShapeDtypeStruct((n // bn, m, bn), BF16)
    else:
        out_spec = pl.BlockSpec((bm, bn), lambda i, j, k: (i, j))
        out_shape = jax.ShapeDtypeStruct((m, n), BF16)
    return pl.pallas_call(
        body, name=name, grid=(m // bm, n // bn, nk),
        in_specs=[pl.BlockSpec((TM, bm), lambda i, j, k: (k, i)), pl.BlockSpec((TM, bn), lambda i, j, k: (k, j))],
        out_specs=out_spec, out_shape=out_shape,
        scratch_shapes=[pltpu.VMEM((bm, bn), F32)],
        compiler_params=_cp(3))(a, b)


def _dgrad_norm(du, w, h, gain, dh_out, name):
    tp, d = h.shape
    s, _, ns = w.shape
    tm = TM_RESIDENT

    def body(du_ref, w_ref, h_ref, g_ref, dho_ref, dhi_ref, dg_ref):
        @pl.when(pl.program_id(0) == 0)
        def _():
            dg_ref[...] = jnp.zeros_like(dg_ref)

        dhn = None
        for k in range(s):
            part = lax.dot_general(du_ref[:, ns * k:ns * (k + 1)], w_ref[k], (((1,), (1,)), ((), ())),
                                   preferred_element_type=F32)
            dhn = part if dhn is None else dhn + part
        dx, xhat = _rmsnorm_bwd(dhn, h_ref[...], g_ref[...])
        dg_ref[...] += jnp.sum(dhn * xhat, axis=0, keepdims=True)
        dhi_ref[...] = dho_ref[...] + dx

    return pl.pallas_call(
        body, name=name, grid=(tp // tm,),
        in_specs=[pl.BlockSpec((tm, s * ns), lambda i: (i, 0)), _resident(w.shape, 1),
                  pl.BlockSpec((tm, d), lambda i: (i, 0)), pl.BlockSpec((1, d), lambda i: (0, 0)),
                  pl.BlockSpec((tm, d), lambda i: (i, 0))],
        out_specs=[pl.BlockSpec((tm, d), lambda i: (i, 0)), pl.BlockSpec((1, d), lambda i: (0, 0))],
        out_shape=[jax.ShapeDtypeStruct((tp, d), F32), jax.ShapeDtypeStruct((1, d), F32)],
        compiler_params=_cp(1))(du, w, h, gain, dh_out)


def _loss_head(h, gain, target, name):
    tp, d = h.shape
    tm = TM_SMALL
    front_tiles = FRONT // tm

    def body(h_ref, g_ref, t_ref, dh_ref, dg_ref, loss_ref):
        i = pl.program_id(0)

        @pl.when(i == 0)
        def _():
            dg_ref[...] = jnp.zeros_like(dg_ref)
            loss_ref[...] = jnp.zeros_like(loss_ref)

        x = h_ref[...]
        gain_v = g_ref[...]
        y = x * _rms(x) * gain_v
        err = jnp.where(i >= front_tiles, y - t_ref[...], 0.0)
        loss_ref[...] += 0.5 * jnp.sum(jnp.mean(err * err, axis=-1, keepdims=True), axis=0, keepdims=True)
        dy = err * (1.0 / d)
        dx, xhat = _rmsnorm_bwd(dy, x, gain_v)
        dg_ref[...] += jnp.sum(dy * xhat, axis=0, keepdims=True)
        dh_ref[...] = dx

    return pl.pallas_call(
        body, name=name, grid=(tp // tm,),
        in_specs=[pl.BlockSpec((tm, d), lambda i: (i, 0)), pl.BlockSpec((1, d), lambda i: (0, 0)),
                  pl.BlockSpec((tm, d), lambda i: (jnp.maximum(i - front_tiles, 0), 0))],
        out_specs=[pl.BlockSpec((tm, d), lambda i: (i, 0)), pl.BlockSpec((1, d), lambda i: (0, 0)),
                   pl.BlockSpec((1, 128), lambda i: (0, 0))],
        out_shape=[jax.ShapeDtypeStruct((tp, d), F32), jax.ShapeDtypeStruct((1, d), F32),
                   jax.ShapeDtypeStruct((1, 128), F32)],
        compiler_params=_cp(1))(h, gain, target)


def _gated_headnorm(o, g, gain):
    return o * _rms(o) * gain * (g * _sigmoid(g))


def _row_mask(chunk, size=CHUNK):
    rows = chunk * size + lax.broadcasted_iota(jnp.int32, (size, 1), 0)
    return (rows >= FRONT - N_META).astype(F32)


def _ret_head(q1, q2, k1, k2, v, g, state, gain, cos, sin, dmat, dq, dk, dc):
    q = jnp.concatenate([q1 * cos - q2 * sin, q1 * sin + q2 * cos], axis=-1)
    k = jnp.concatenate([k1 * cos - k2 * sin, k1 * sin + k2 * cos], axis=-1) * (RET_DK ** -0.5)
    scores = _nt(q, k) * dmat
    o = _nn(scores, v) + _nn(q * dq, state)
    new_state = state * dc + _tn(k * dk, v)
    return _gated_headnorm(o, g, gain), new_state


def _ret_consts():
    log_gamma = jnp.log1p(-2.0 ** (-5.0 - jnp.arange(HEADS, dtype=F32)))
    idx = jnp.arange(RET_CHUNK, dtype=F32)
    rel = idx[:, None] - idx[None, :]
    dmat = jnp.where(rel >= 0, jnp.exp(log_gamma[:, None, None] * jnp.maximum(rel, 0.0)), 0.0)
    dq = jnp.exp(log_gamma[:, None] * (idx + 1.0))[..., None]
    dk = jnp.exp(log_gamma[:, None] * (RET_CHUNK - 1.0 - idx))[..., None]
    dc = jnp.broadcast_to(jnp.exp(log_gamma * RET_CHUNK)[:, None, None], (HEADS, 1, 128))
    return dmat, dq, dk, dc


def _rope_tables(tp):
    half = RET_DK // 2
    inv = 1.0 / (ROPE_BASE ** jnp.linspace(0.0, 1.0, half, dtype=F32))
    pos = (jnp.arange(tp) - (FRONT - N_META)).astype(F32)
    ang = pos[:, None] * inv[None, :]
    return jnp.cos(ang), jnp.sin(ang)


_RET_V0, _RET_G0 = 2 * D, 4 * D


def _heads(ref, start, width, stride=None):
    stride = width if stride is None else stride
    return jnp.stack([ref[:, start + stride * h:start + stride * h + width].astype(F32) for h in range(HEADS)])


def _put_heads(ref, start, value, mask, stride=None):
    width = value.shape[-1]
    stride = width if stride is None else stride
    for h in range(HEADS):
        ref[:, start + stride * h:start + stride * h + width] = (value[h] * mask).astype(ref.dtype)


def _ret_pieces(u_ref):
    hk = RET_DK // 2
    return (_heads(u_ref, 0, hk, RET_DK), _heads(u_ref, hk, hk, RET_DK), _heads(u_ref, D, hk, RET_DK),
            _heads(u_ref, D + hk, hk, RET_DK), _heads(u_ref, _RET_V0, RET_DV), _heads(u_ref, _RET_G0, RET_DV))


def _ret_const_specs(rev=None):
    c = (lambda n: (rev(n), 0)) if rev else (lambda n: (n, 0))
    z3 = lambda n: (0, 0, 0)
    return [pl.BlockSpec((RET_CHUNK, RET_DK // 2), c), pl.BlockSpec((RET_CHUNK, RET_DK // 2), c),
            pl.BlockSpec((HEADS, RET_CHUNK, RET_CHUNK), z3), pl.BlockSpec((HEADS, RET_CHUNK, 1), z3),
            pl.BlockSpec((HEADS, RET_CHUNK, 1), z3), pl.BlockSpec((HEADS, 1, 128), z3)]


def _ret_fwd(u, gain, rope, name):
    tp = u.shape[0]
    nch = tp // RET_CHUNK
    cos, sin = rope
    dmat, dq, dk, dc = _ret_consts()

    def body(u_ref, gain_ref, cos_ref, sin_ref, dmat_ref, dq_ref, dk_ref, dc_ref, on_ref, st_ref, state_ref):
        @pl.when(pl.program_id(0) == 0)
        def _():
            state_ref[...] = jnp.zeros_like(state_ref)

        state = state_ref[...]
        st_ref[...] = state.astype(BF16)
        on, new_state = _ret_head(*_ret_pieces(u_ref), state, _heads(gain_ref, 0, RET_DV), cos_ref[...], sin_ref[...],
                                  dmat_ref[...], dq_ref[...], dk_ref[...], dc_ref[...][:, :, :1])
        state_ref[...] = new_state
        _put_heads(on_ref, 0, on, 1.0)

    return pl.pallas_call(
        body, name=name, grid=(nch,),
        in_specs=[pl.BlockSpec((RET_CHUNK, 6 * D), lambda n: (n, 0)), pl.BlockSpec((1, HEADS * RET_DV), lambda n: (0, 0))]
                 + _ret_const_specs(),
        out_specs=[pl.BlockSpec((RET_CHUNK, HEADS * RET_DV), lambda n: (n, 0)),
                   pl.BlockSpec((None, HEADS, RET_DK, RET_DV), lambda n: (n, 0, 0, 0))],
        out_shape=[jax.ShapeDtypeStruct((tp, HEADS * RET_DV), BF16),
                   jax.ShapeDtypeStruct((nch, HEADS, RET_DK, RET_DV), BF16)],
        scratch_shapes=[pltpu.VMEM((HEADS, RET_DK, RET_DV), F32)],
        compiler_params=_cp(1))(u, gain, cos, sin, dmat, dq, dk, dc)


def _ret_bwd(u, gain, rope, states, d_on, name):
    tp = u.shape[0]
    nch = tp // RET_CHUNK
    cos, sin = rope
    dmat, dq, dk, dc = _ret_consts()
    rev = lambda n: nch - 1 - n
    hk = RET_DK // 2

    def body(u_ref, gain_ref, st_ref, don_ref, cos_ref, sin_ref, dmat_ref, dq_ref, dk_ref, dc_ref,
             du_ref, dgain_ref, dstate_ref):
        @pl.when(pl.program_id(0) == 0)
        def _():
            dstate_ref[...] = jnp.zeros_like(dstate_ref)
            dgain_ref[...] = jnp.zeros_like(dgain_ref)

        mask = _row_mask(rev(pl.program_id(0)), RET_CHUNK)
        consts = (cos_ref[...], sin_ref[...], dmat_ref[...], dq_ref[...], dk_ref[...], dc_ref[...][:, :, :1])
        _, vjp = jax.vjp(lambda *a: _ret_head(*a, *consts), *_ret_pieces(u_ref), st_ref[...].astype(F32),
                         _heads(gain_ref, 0, RET_DV))
        dq1, dq2, dk1, dk2, dv, dg, dstate, dgain = vjp((_heads(don_ref, 0, RET_DV), dstate_ref[...]))
        dstate_ref[...] = dstate
        for hd in range(HEADS):
            dgain_ref[:, RET_DV * hd:RET_DV * (hd + 1)] += dgain[hd]
        _put_heads(du_ref, 0, dq1, mask, RET_DK)
        _put_heads(du_ref, hk, dq2, mask, RET_DK)
        _put_heads(du_ref, D, dk1, mask, RET_DK)
        _put_heads(du_ref, D + hk, dk2, mask, RET_DK)
        _put_heads(du_ref, _RET_V0, dv, mask)
        _put_heads(du_ref, _RET_G0, dg, mask)

    return pl.pallas_call(
        body, name=name, grid=(nch,),
        in_specs=[pl.BlockSpec((RET_CHUNK, 6 * D), lambda n: (rev(n), 0)),
                  pl.BlockSpec((1, HEADS * RET_DV), lambda n: (0, 0)),
                  pl.BlockSpec((None, HEADS, RET_DK, RET_DV), lambda n: (rev(n), 0, 0, 0)),
                  pl.BlockSpec((RET_CHUNK, HEADS * RET_DV), lambda n: (rev(n), 0))] + _ret_const_specs(rev),
        out_specs=[pl.BlockSpec((RET_CHUNK, 6 * D), lambda n: (rev(n), 0)),
                   pl.BlockSpec((1, HEADS * RET_DV), lambda n: (0, 0))],
        out_shape=[jax.ShapeDtypeStruct((tp, 6 * D), BF16), jax.ShapeDtypeStruct((1, HEADS * RET_DV), F32)],
        scratch_shapes=[pltpu.VMEM((HEADS, RET_DK, RET_DV), F32)],
        compiler_params=_cp(1))(u, gain, states, d_on, cos, sin, dmat, dq, dk, dc)


_GLA_K0, _GLA_V0, _GLA_G0, _GLA_Z0 = 512, 1024, 2048, 3072


def _gla_head(q, k, v, g, z, state_t, wg, bg, gain, mask, lo, lo_t, loc, loc_t):
    ga = _nn(jnp.broadcast_to(z, wg.shape[:-2] + z.shape), wg) + bg
    log_a = (jnp.minimum(ga, 0.0) - jnp.log(1.0 + jnp.exp(-jnp.abs(ga)))) * (mask * (1.0 / GLA_TAU))
    bcum = _cum(lo, lo_t, log_a)
    bmid = _cum(loc, loc_t, log_a)
    btot = jnp.sum(log_a, axis=-2, keepdims=True)
    qs = q * (GLA_DK ** -0.5)
    causal = lax.broadcasted_iota(jnp.int32, (CHUNK, CHUNK), 0) >= lax.broadcasted_iota(jnp.int32, (CHUNK, CHUNK), 1)
    scores = jnp.where(causal, _nt(qs * jnp.exp(bmid), k * jnp.exp(-bmid)), 0.0)
    o = _nn(scores, v) + _nt(qs * jnp.exp(bcum), state_t)
    new_state_t = state_t * jnp.exp(btot) + _tn(v, k * jnp.exp(btot - bcum))
    return _gated_headnorm(o, g, gain), new_state_t


def _cum_mats():
    r = lax.broadcasted_iota(jnp.int32, (CHUNK, CHUNK), 0)
    c = lax.broadcasted_iota(jnp.int32, (CHUNK, CHUNK), 1)
    mid = CHUNK // 2
    low = lambda a, b: (a >= b).astype(F32)
    lo, lo_t = low(r, c), low(c, r)
    loc = lo - (c <= mid).astype(F32)
    loc_t = lo_t - (r <= mid).astype(F32)
    return tuple(m.astype(BF16) for m in (lo, lo_t, loc, loc_t))


def _gla_pieces(u_ref):
    return (_heads(u_ref, 0, GLA_DK), _heads(u_ref, _GLA_K0, GLA_DK), _heads(u_ref, _GLA_V0, GLA_DV),
            _heads(u_ref, _GLA_G0, GLA_DV), u_ref[:, _GLA_Z0:].astype(F32))


def _gla_fwd(u, wg, bg, gain, name):
    tp = u.shape[0]
    nch = tp // CHUNK

    def body(u_ref, wg_ref, bg_ref, gain_ref, on_ref, st_ref, state_ref):
        @pl.when(pl.program_id(0) == 0)
        def _():
            state_ref[...] = jnp.zeros_like(state_ref)

        state = state_ref[...]
        st_ref[...] = state.astype(BF16)
        on, new_state = _gla_head(*_gla_pieces(u_ref), state, _heads(wg_ref, 0, GLA_DK), _heads(bg_ref, 0, GLA_DK),
                                  _heads(gain_ref, 0, GLA_DV), _row_mask(pl.program_id(0)), *_cum_mats())
        state_ref[...] = new_state
        _put_heads(on_ref, 0, on, 1.0)

    return pl.pallas_call(
        body, name=name, grid=(nch,),
        in_specs=[pl.BlockSpec((CHUNK, GLA_U), lambda n: (n, 0)), pl.BlockSpec((128, HEADS * GLA_DK), lambda n: (0, 0)),
                  pl.BlockSpec((1, HEADS * GLA_DK), lambda n: (0, 0)), pl.BlockSpec((1, HEADS * GLA_DV), lambda n: (0, 0))],
        out_specs=[pl.BlockSpec((CHUNK, HEADS * GLA_DV), lambda n: (n, 0)),
                   pl.BlockSpec((None, HEADS, GLA_DV, GLA_DK), lambda n: (n, 0, 0, 0))],
        out_shape=[jax.ShapeDtypeStruct((tp, HEADS * GLA_DV), BF16),
                   jax.ShapeDtypeStruct((nch, HEADS, GLA_DV, GLA_DK), BF16)],
        scratch_shapes=[pltpu.VMEM((HEADS, GLA_DV, GLA_DK), F32)],
        compiler_params=_cp(1))(u, wg, bg, gain)


def _gla_bwd(u, wg, bg, gain, states, d_on, name):
    tp = u.shape[0]
    nch = tp // CHUNK
    rev = lambda n: nch - 1 - n

    def body(u_ref, wg_ref, bg_ref, gain_ref, st_ref, don_ref, du_ref, dwg_ref, dbg_ref, dgain_ref, dstate_ref):
        @pl.when(pl.program_id(0) == 0)
        def _():
            dstate_ref[...] = jnp.zeros_like(dstate_ref)
            dwg_ref[...] = jnp.zeros_like(dwg_ref)
            dbg_ref[...] = jnp.zeros_like(dbg_ref)
            dgain_ref[...] = jnp.zeros_like(dgain_ref)

        mask = _row_mask(rev(pl.program_id(0)))
        mats = _cum_mats()
        _, vjp = jax.vjp(lambda *a: _gla_head(*a, mask, *mats), *_gla_pieces(u_ref), st_ref[...].astype(F32),
                         _heads(wg_ref, 0, GLA_DK), _heads(bg_ref, 0, GLA_DK), _heads(gain_ref, 0, GLA_DV))
        dq, dk, dv, dg, dz, dstate, dwg, dbg, dgain = vjp((_heads(don_ref, 0, GLA_DV), dstate_ref[...]))
        dstate_ref[...] = dstate
        for hd in range(HEADS):
            dwg_ref[:, GLA_DK * hd:GLA_DK * (hd + 1)] += dwg[hd]
            dbg_ref[:, GLA_DK * hd:GLA_DK * (hd + 1)] += dbg[hd]
            dgain_ref[:, GLA_DV * hd:GLA_DV * (hd + 1)] += dgain[hd]
        _put_heads(du_ref, 0, dq, mask)
        _put_heads(du_ref, _GLA_K0, dk, mask)
        _put_heads(du_ref, _GLA_V0, dv, mask)
        _put_heads(du_ref, _GLA_G0, dg, mask)
        du_ref[:, _GLA_Z0:] = dz.astype(BF16)

    full = lambda r, c: pl.BlockSpec((r, c), lambda n: (0, 0))
    return pl.pallas_call(
        body, name=name, grid=(nch,),
        in_specs=[pl.BlockSpec((CHUNK, GLA_U), lambda n: (rev(n), 0)), full(128, HEADS * GLA_DK),
                  full(1, HEADS * GLA_DK), full(1, HEADS * GLA_DV),
                  pl.BlockSpec((None, HEADS, GLA_DV, GLA_DK), lambda n: (rev(n), 0, 0, 0)),
                  pl.BlockSpec((CHUNK, HEADS * GLA_DV), lambda n: (rev(n), 0))],
        out_specs=[pl.BlockSpec((CHUNK, GLA_U), lambda n: (rev(n), 0)), full(128, HEADS * GLA_DK),
                   full(1, HEADS * GLA_DK), full(1, HEADS * GLA_DV)],
        out_shape=[jax.ShapeDtypeStruct((tp, GLA_U), BF16), jax.ShapeDtypeStruct((128, HEADS * GLA_DK), F32),
                   jax.ShapeDtypeStruct((1, HEADS * GLA_DK), F32), jax.ShapeDtypeStruct((1, HEADS * GLA_DV), F32)],
        scratch_shapes=[pltpu.VMEM((HEADS, GLA_DV, GLA_DK), F32)],
        compiler_params=_cp(1))(u, wg, bg, gain, states, d_on)


def _ffn_fwd(h, gain, w_in, w_out, tag):
    hn, ug, uu, act = _norm_ffn_in(h, gain, w_in, f"{tag}_in")
    if callable(w_out):
        w_out = w_out(act)
    return _out_proj(act, w_out, h, 0.5, f"{tag}_out"), (h, hn, ug, uu, act), w_out


def _ffn_bwd(dh, saved, gain, w_in, w_out, tag, push):
    h, hn, ug, uu, act = saved
    du = _ffn_dact(dh, w_out, ug, uu, f"{tag}_dact")
    d_w_out = _wgrad(act, dh, bm=D_FF // 2, bn=D, scale=0.5, sharded=False, name=f"{tag}_dwout")
    d_w_in = _wgrad(hn, du, bm=D, bn=D_FF, scale=1.0, sharded=False, name=f"{tag}_dwin")
    token = push([("cols", d_w_in), d_w_out])
    return _dgrad_norm(du, w_in[None], h, gain + token[0, 0], dh, f"{tag}_dnorm")


def _sequence_grads(x, target, p, weights, grads):
    row = lambda v, token: v.reshape(1, -1) + token[0, 0]
    gains = {}

    tok = weights.start(1, weights.start(0, None))
    weights.pin = tok
    h = jnp.concatenate([jnp.zeros((FRONT, D), F32), x], axis=0) + tok[0, 0]
    rope = _rope_tables(h.shape[0])
    w = weights.wait(0, [tok, h, *rope, *weights.later_shards(2)])
    tok = weights.start(2, w["l0_ffn1_in"])
    h = lax.dynamic_update_slice(h, w["meta"], (FRONT - N_META, 0))
    gains["l0_ffn1"] = row(p["norm_ffn1"][0], tok)
    h, s1, w["l0_ffn1_out"] = _ffn_fwd(h, gains["l0_ffn1"], w["l0_ffn1_in"],
                                       lambda act: weights.wait(1, act)["l0_ffn1_out"], "l0_ffn1")
    w.update(weights.wait(2, h))
    tok = weights.start(3, w["ret_in"])
    gains["ret"] = row(p["norm_mix"][0], tok)
    hn, u = _norm_proj(h, gains["ret"], w["ret_in"], "ret_in")
    on, states = _ret_fwd(u, w["ret_gain"], rope, "ret_fwd")
    h_mix = _out_proj(on, w["ret_out"], h, 1.0, "ret_out")
    s2 = (h, hn, u, on, states)
    w.update(weights.wait(3, h_mix))
    tok = weights.start(4, w["l0_ffn2_in"])
    gains["l0_ffn2"] = row(p["norm_ffn2"][0], tok)
    h, s3, _ = _ffn_fwd(h_mix, gains["l0_ffn2"], w["l0_ffn2_in"], w["l0_ffn2_out"], "l0_ffn2")
    saved = [(s1, s2, s3)]

    w.update(weights.wait(4, h))
    tok = weights.start(5, w["l1_ffn1_in"])
    gains["l1_ffn1"] = row(p["norm_ffn1"][1], tok)
    h, s1, _ = _ffn_fwd(h, gains["l1_ffn1"], w["l1_ffn1_in"], w["l1_ffn1_out"], "l1_ffn1")
    w.update(weights.wait(5, h))
    tok = weights.start(6, w["gla_out"])
    gains["gla"] = row(p["norm_mix"][1], tok)
    hn, u = _norm_proj(h, gains["gla"], w["gla_in"], "gla_in")
    on, states = _gla_fwd(u, w["gla_wg"], w["gla_bg"], w["gla_gain"], "gla_fwd")
    h_mix = _out_proj(on, w["gla_out"], h, 1.0, "gla_out")
    s2 = (h, hn, u, on, states)
    w.update(weights.wait(6, h_mix))
    gains["l1_ffn2"] = p["norm_ffn2"][1].reshape(1, -1)
    h, s3, _ = _ffn_fwd(h_mix, gains["l1_ffn2"], w["l1_ffn2_in"], w["l1_ffn2_out"], "l1_ffn2")
    saved.append((s1, s2, s3))

    dh, d_final, loss = _loss_head(h, p["final_norm"].reshape(1, -1), target, "loss_head")
    small = {"final_norm": d_final, "norm_ffn1": [None, None], "norm_mix": [None, None], "norm_ffn2": [None, None]}
    pusher = lambda k: functools.partial(grads.push, k)

    s1, s2, s3 = saved[1]
    dh, small["norm_ffn2"][1] = _ffn_bwd(dh, s3, gains["l1_ffn2"], w["l1_ffn2_in"], w["l1_ffn2_out"], "l1_ffn2",
                                         pusher(0))
    h_in, hn, u, on, states = s2
    d_on = _dgrad(dh, w["gla_out"], "gla_don")
    d_out = _wgrad(on, dh, bm=D, bn=D, scale=1.0, sharded=False, name="gla_dwout")
    du, small["gla_wg"], small["gla_bg"], small["gla_gain"] = _gla_bwd(u, w["gla_wg"], w["gla_bg"], w["gla_gain"],
                                                                       states, d_on, "gla_bwd")
    d_in = _wgrad(hn, du, bm=D, bn=GLA_U // 5, scale=1.0, sharded=False, name="gla_dwin")
    d_in = jnp.moveaxis(d_in[:, :GLA_IN].reshape(D, N_CHIPS, -1), 1, 0)
    tok = grads.push(1, [d_in, d_out])
    dh, small["norm_mix"][1] = _dgrad_norm(du, w["gla_in"], h_in, gains["gla"] + tok[0, 0], dh, "gla_dnorm")
    dh, small["norm_ffn1"][1] = _ffn_bwd(dh, s1, gains["l1_ffn1"], w["l1_ffn1_in"], w["l1_ffn1_out"], "l1_ffn1",
                                         pusher(2))

    s1, s2, s3 = saved[0]
    dh, small["norm_ffn2"][0] = _ffn_bwd(dh, s3, gains["l0_ffn2"], w["l0_ffn2_in"], w["l0_ffn2_out"], "l0_ffn2",
                                         pusher(3))
    h_in, hn, u, on, states = s2
    d_on = _dgrad(dh, w["ret_out"], "ret_don")
    d_out = _wgrad(on, dh, bm=D, bn=D, scale=1.0, sharded=False, name="ret_dwout")
    du, small["ret_gain"] = _ret_bwd(u, w["ret_gain"], rope, states, d_on, "ret_bwd")
    d_in = _wgrad(hn, du, bm=D, bn=w["ret_in"].shape[2], scale=1.0, sharded=True, name="ret_dwin")
    tok = grads.push(4, [d_in, d_out])
    dh, small["norm_mix"][0] = _dgrad_norm(du, w["ret_in"], h_in, gains["ret"] + tok[0, 0], dh, "ret_dnorm")
    dh, small["norm_ffn1"][0] = _ffn_bwd(dh, s1, gains["l0_ffn1"], w["l0_ffn1_in"], w["l0_ffn1_out"], "l0_ffn1",
                                         pusher(5))
    grads.push(6, [], [dh[FRONT - N_META:FRONT], *small["norm_ffn1"], *small["norm_mix"], *small["norm_ffn2"],
                       small["final_norm"], small["ret_gain"], small["gla_wg"][:GLA_RANK], small["gla_bg"],
                       small["gla_gain"], loss[:, :1]])
    return dh[FRONT:]


_HBM = pl.BlockSpec(memory_space=pl.ANY)


def _place():
    return lax.axis_index("x"), lax.axis_index("y"), lax.axis_index("c")


def _flip(v, bit):
    return 1 - v if bit else v


DMA_CHUNK_BYTES = 128 * 1024


def _row_chunks(ref):
    rows, cols = ref.shape
    step = _row_tile(rows, max(16, DMA_CHUNK_BYTES // (cols * ref.dtype.itemsize)))
    return [pl.ds(a, step) for a in range(0, rows, step)]


def _whole(src, dst, send_sem, recv_sem, peer):
    return pltpu.make_async_remote_copy(src_ref=src, dst_ref=dst, send_sem=send_sem, recv_sem=recv_sem,
                                        device_id=peer, device_id_type=MESH)


def _send(src, dst, send_sem, recv_sem, peer):
    for rows in _row_chunks(src):
        _whole(src.at[rows], dst.at[rows], send_sem, recv_sem, peer).start()
    return _whole(src, dst, send_sem, recv_sem, peer)


_HBM_ONLY = pl.BlockSpec(memory_space=pltpu.HBM)
_SEMS = pl.BlockSpec(memory_space=pltpu.SEMAPHORE)
_SIDE_EFFECT = pltpu.CompilerParams(has_side_effects=pltpu.SideEffectType.DATAFLOW_SIDE_EFFECTING)
_CHIP_FLIPS = [(1, 0, 0), (0, 1, 0), (1, 1, 0)]
_PEER_FLIPS = [(fx, fy, fc) for fx in (0, 1) for fy in (0, 1) for fc in (0, 1)][1:]


def _zero_token():
    return jnp.zeros((8, 128), F32)


def _exchange_start(srcs, lands, route, flips, after, name):
    n = len(srcs)

    def body(*refs):
        src, land = refs[:n], refs[n:2 * n]
        send_sems, recv_sems, token = refs[2 * n + 1], refs[2 * n + 2], refs[-1]
        me = _place()
        for t in range(n):
            for j, flip in enumerate(flips):
                peer = tuple(_flip(v, f) for v, f in zip(me, flip))
                s, d = route(t, src[t], land[t], me, peer)
                _send(s, d, send_sems.at[t * len(flips) + j], recv_sems.at[t * len(flips) + j], peer)
        token[...] = jnp.zeros_like(token)

    hbm = lambda a: pltpu.HBM(a.shape, a.dtype)
    sems = pltpu.SemaphoreType.DMA((n * len(flips),))
    operands = [pltpu.with_memory_space_constraint(a, pltpu.HBM) for a in list(srcs) + list(lands)]
    out = pl.pallas_call(
        body, name=name, in_specs=[_HBM_ONLY] * (2 * n) + [_HBM],
        out_shape=(sems, sems, *[hbm(a) for a in operands], jax.ShapeDtypeStruct((8, 128), F32)),
        out_specs=(_SEMS, _SEMS, *[_HBM_ONLY] * (2 * n), pl.BlockSpec(memory_space=pltpu.VMEM)),
        input_output_aliases={i: 2 + i for i in range(2 * n)}, compiler_params=_SIDE_EFFECT,
    )(*operands, _zero_token() if after is None else after)
    return (out[0], out[1], out[2:2 + n], out[2 + n:2 + 2 * n]), out[-1]


def _exchange_wait(started, route, flips, after, name):
    send_sems, recv_sems, srcs, lands = started
    n = len(srcs)

    def body(*refs):
        src, land = refs[:n], refs[n:2 * n]
        send_sems, recv_sems = refs[2 * n], refs[2 * n + 1]
        me = _place()
        for t in range(n):
            for j, flip in enumerate(flips):
                peer = tuple(_flip(v, f) for v, f in zip(me, flip))
                s, d = route(t, src[t], land[t], me, peer)
                cp = _whole(s, d, send_sems.at[t * len(flips) + j], recv_sems.at[t * len(flips) + j], peer)
                cp.wait_send()
                cp.wait_recv()

    hbm = lambda a: pltpu.HBM(a.shape, a.dtype)
    after = list(after) if isinstance(after, (list, tuple)) else [after]
    out = pl.pallas_call(
        body, name=name, in_specs=[_HBM_ONLY] * (2 * n) + [_SEMS, _SEMS] + [_HBM] * len(after),
        out_shape=tuple(hbm(a) for a in list(srcs) + list(lands)), out_specs=tuple([_HBM_ONLY] * (2 * n)),
        input_output_aliases={i: i for i in range(2 * n)}, compiler_params=_SIDE_EFFECT,
    )(*srcs, *lands, send_sems, recv_sems, *after)
    return out[:n], out[n:]


def _gather_route(t, src, land, me, peer):
    mine = 2 * me[0] + me[1]
    if land.ndim == 3:
        return src, land.at[mine]
    cols = src.shape[1]
    return src, land.at[:, pl.ds(pl.multiple_of(mine * cols, 128), cols)]


def _scatter_route(n_pieces):
    def route(t, src, land, me, peer):
        chip = 2 * peer[0] + peer[1]
        if t >= n_pieces:
            part = src
        elif src.ndim == 4:
            part = src.at[chip, peer[2]]
        else:
            rows, cols = land.shape[1:]
            part = src.at[pl.ds(pl.multiple_of(peer[2] * rows, 16), rows), pl.ds(pl.multiple_of(chip * cols, 128), cols)]
        return part, land.at[4 * me[0] + 2 * me[1] + me[2]]

    return route


def _swap_cores(halves):
    n = len(halves)

    def body(*refs):
        src, dst = refs[:n], refs[n:2 * n]
        send_sems, recv_sems = refs[2 * n:]
        x, y, c = _place()
        copies = [_send(src[t], dst[t], send_sems.at[t], recv_sems.at[t], (x, y, 1 - c)) for t in range(n)]
        for cp in copies:
            cp.wait()

    got = pl.pallas_call(
        body, name="swap_cores", in_specs=[_HBM] * n, out_specs=[_HBM] * n,
        out_shape=[jax.ShapeDtypeStruct(a.shape, a.dtype) for a in halves],
        scratch_shapes=[pltpu.SemaphoreType.DMA((n,)), pltpu.SemaphoreType.DMA((n,))],
    )(*halves)
    south = lax.axis_index("c") == 0
    return [jnp.stack([jnp.where(south, a, b), jnp.where(south, b, a)]) for a, b in zip(halves, got)]


def _row_tile(rows, cap):
    fits = [t for t in range(16, cap + 1, 16) if rows % t == 0]
    return fits[-1] if fits else rows


def _sum_slots(a, name):
    _, r, c = a.shape
    tr = _row_tile(r, 384)

    def body(a_ref, o_ref):
        s = a_ref[0].astype(F32)
        for k in range(1, N_DEV):
            s = s + a_ref[k].astype(F32)
        o_ref[...] = s

    return pl.pallas_call(
        body, name=name, grid=(r // tr,),
        in_specs=[pl.BlockSpec((N_DEV, tr, c), lambda i: (0, i, 0))],
        out_specs=pl.BlockSpec((tr, c), lambda i: (i, 0)),
        out_shape=jax.ShapeDtypeStruct((r, c), F32),
        compiler_params=_cp(1))(a)


def _adamw(w, g, m, v, name):
    layers, r, c = w.shape
    tr = _row_tile(r, 256)

    def body(w_ref, g_ref, m_ref, v_ref, d_ref, nm_ref, nv_ref):
        gv = g_ref[...]
        nm = ADAM_B1 * m_ref[...] + (1.0 - ADAM_B1) * gv
        nv = ADAM_B2 * v_ref[...] + (1.0 - ADAM_B2) * (gv * gv)
        m_hat = nm / (1.0 - ADAM_B1 ** ADAM_STEP)
        v_hat = nv / (1.0 - ADAM_B2 ** ADAM_STEP)
        d_ref[...] = -ADAM_LR * (m_hat / (jnp.sqrt(v_hat) + ADAM_EPS) + ADAM_WD * w_ref[...])
        nm_ref[...] = nm
        nv_ref[...] = nv

    spec = pl.BlockSpec((None, tr, c), lambda a, i: (a, i, 0))
    return pl.pallas_call(
        body, name=name, grid=(layers, r // tr), in_specs=[spec] * 4, out_specs=[spec] * 3,
        out_shape=[jax.ShapeDtypeStruct((layers, r, c), F32)] * 3,
        compiler_params=_cp(2))(w, g, m, v)


_SMALL = ["meta_tokens", "ret_head_norm", "gla_w_gate", "gla_b_gate", "gla_head_norm"]
_LOCAL_SMALL = ["meta_tokens", "norm_ffn1", "norm_mix", "norm_ffn2", "ret_head_norm", "gla_w_gate", "gla_b_gate",
                "gla_head_norm", "final_norm"]
_BIG = ["ffn1_w_in", "ffn1_w_out", "ffn2_w_in", "ffn2_w_out", "ret_w_in", "ret_w_out", "gla_w_in", "gla_w_out"]
_WEIGHTS = ["meta_tokens", "norm_ffn1", "ffn1_w_in", "ffn1_w_out", "norm_mix", "norm_ffn2", "ffn2_w_in", "ffn2_w_out",
            "ret_w_in", "ret_head_norm", "ret_w_out", "gla_w_in", "gla_w_gate", "gla_b_gate", "gla_head_norm",
            "gla_w_out", "final_norm"]


def _pack_rows(arrays, width):
    flat = jnp.concatenate([a.reshape(-1) for a in arrays])
    pad = -flat.shape[0] % (8 * width)
    return jnp.pad(flat, (0, pad)).reshape(-1, width)


def _unpack_rows(packed, shapes):
    flat, out, at = packed.reshape(-1), [], 0
    for s in shapes:
        size = 1
        for dim in s:
            size *= dim
        out.append(flat[at:at + size].reshape(s))
        at += size
    return out


class _WeightGather:
    GROUPS = [("small", "l0_ffn1_in"), ("l0_ffn1_out",), ("ret_in", "ret_out"), ("l0_ffn2_in", "l0_ffn2_out"),
              ("l1_ffn1_in", "l1_ffn1_out"), ("gla_in", "gla_out"), ("l1_ffn2_in", "l1_ffn2_out")]

    def __init__(self, p):
        self.small_shapes = [p[name].shape for name in _SMALL]
        self.f32 = {"small": _pack_rows([p[name] for name in _SMALL], 128), "ret_in": p["ret_w_in"][0],
                    "ret_out": p["ret_w_out"][0], "gla_in": p["gla_w_in"][0], "gla_out": p["gla_w_out"][0]}
        for layer in range(2):
            for name in ("ffn1", "ffn2"):
                self.f32[f"l{layer}_{name}_in"] = p[f"{name}_w_in"][layer]
                self.f32[f"l{layer}_{name}_out"] = p[f"{name}_w_out"][layer]
        self.shards = {}
        self.started = {}
        self.pin = None

    def shard(self, name):
        if name not in self.shards:
            a = self.f32[name]
            if name != "small":
                a = (a if self.pin is None else a + self.pin[0, 0]).astype(BF16)
            self.shards[name] = a
        return self.shards[name]

    def later_shards(self, k):
        return [self.shard(name) for group in self.GROUPS[k:] for name in group]

    def start(self, k, after):
        shards = [self.shard(name) for name in self.GROUPS[k]]
        mine = 2 * lax.axis_index("x") + lax.axis_index("y")
        lands = []
        for name, s in zip(self.GROUPS[k], shards):
            if "ffn" in name and name.endswith("_in"):
                rows, cols = s.shape
                lands.append(lax.dynamic_update_slice_in_dim(lax.empty((rows, N_CHIPS * cols), s.dtype), s,
                                                             mine * cols, axis=1))
            else:
                lands.append(lax.dynamic_update_index_in_dim(lax.empty((N_CHIPS,) + s.shape, s.dtype), s, mine, 0))
        self.started[k], token = _exchange_start(shards, lands, _gather_route, _CHIP_FLIPS, after, f"gather{k}_start")
        return token

    def wait(self, k, after):
        _, got = _exchange_wait(self.started[k], _gather_route, _CHIP_FLIPS, after, f"gather{k}_wait")
        w = {}
        for name, g in zip(self.GROUPS[k], got):
            if name == "small":
                parts = zip(*[_unpack_rows(g[chip], self.small_shapes) for chip in range(N_CHIPS)])
                cat = lambda a: jnp.moveaxis(a, 0, -2).reshape(a.shape[1:-1] + (-1,))
                meta, ret_gain, wg, bg, gla_gain = [cat(jnp.stack(part)) for part in parts]
                w.update(meta=meta, ret_gain=ret_gain.reshape(1, -1), gla_bg=bg.reshape(1, -1),
                         gla_gain=gla_gain.reshape(1, -1),
                         gla_wg=jnp.pad(wg[0], ((0, 128 - GLA_RANK), (0, 0))).astype(BF16))
            elif name == "gla_in":
                full = jnp.moveaxis(g, 0, 1).reshape(D, -1)
                w[name] = jnp.pad(full, ((0, 0), (0, GLA_U - GLA_IN)))[None]
            elif name.endswith("_out"):
                w[name] = g.reshape(-1, g.shape[-1])
            else:
                w[name] = g
        return w


class _GradExchange:
    def __init__(self):
        self.started = []
        self.token = None
        self.small_shapes = None

    def push(self, k, arrays, small=None):
        srcs, lands = [], []
        for a in arrays:
            if isinstance(a, tuple):
                a = a[1]
                piece = (a.shape[0] // 2, a.shape[1] // N_CHIPS)
            else:
                a = a.reshape(N_CHIPS, 2, -1, a.shape[-1])
                piece = a.shape[2:]
            srcs.append(a)
            lands.append(lax.empty((N_DEV,) + piece, a.dtype))
        if small is not None:
            self.small_shapes = [a.shape for a in small]
            srcs.append(_pack_rows(small, D))
            lands.append(lax.empty((N_DEV,) + srcs[-1].shape, F32))
        started, self.token = _exchange_start(srcs, lands, _scatter_route(len(arrays)), _PEER_FLIPS, None,
                                              f"scatter{k}_start")
        self.started.append((started, len(arrays)))
        return self.token

    def collect(self):
        x, y, c = _place()
        after, sums = self.token, []
        for k, (started, n_pieces) in enumerate(self.started):
            srcs, got = _exchange_wait(started, _scatter_route(n_pieces), _PEER_FLIPS, after, f"scatter{k}_wait")
            own = []
            for t, (a, g) in enumerate(zip(srcs, got)):
                if t >= n_pieces:
                    own.append(a)
                elif a.ndim == 4:
                    own.append(a[2 * x + y, c])
                else:
                    rows, cols = g.shape[1:]
                    own.append(lax.dynamic_slice(a, (c * rows, (2 * x + y) * cols), (rows, cols)))
            got = [lax.dynamic_update_index_in_dim(g, a, 4 * x + 2 * y + c, 0) for g, a in zip(got, own)]
            sums.append([_sum_slots(a, f"sum{k}_{i}") for i, a in enumerate(got)])
            after = sums[-1][0]
        small = _unpack_rows(sums[-1].pop(), self.small_shapes)
        return sums, small


def kernel(x, meta_tokens, norm_ffn1, ffn1_w_in, ffn1_w_out, norm_mix, norm_ffn2, ffn2_w_in, ffn2_w_out, ret_w_in, ret_head_norm, ret_w_out, gla_w_in, gla_w_gate, gla_b_gate, gla_head_norm, gla_w_out, final_norm, loss_target, m_meta_tokens, m_norm_ffn1, m_ffn1_w_in, m_ffn1_w_out, m_norm_mix, m_norm_ffn2, m_ffn2_w_in, m_ffn2_w_out, m_ret_w_in, m_ret_head_norm, m_ret_w_out, m_gla_w_in, m_gla_w_gate, m_gla_b_gate, m_gla_head_norm, m_gla_w_out, m_final_norm, v_meta_tokens, v_norm_ffn1, v_ffn1_w_in, v_ffn1_w_out, v_norm_mix, v_norm_ffn2, v_ffn2_w_in, v_ffn2_w_out, v_ret_w_in, v_ret_head_norm, v_ret_w_out, v_gla_w_in, v_gla_w_gate, v_gla_b_gate, v_gla_head_norm, v_gla_w_out, v_final_norm):
    p = dict(meta_tokens=meta_tokens, norm_ffn1=norm_ffn1, ffn1_w_in=ffn1_w_in, ffn1_w_out=ffn1_w_out, norm_mix=norm_mix,
             norm_ffn2=norm_ffn2, ffn2_w_in=ffn2_w_in, ffn2_w_out=ffn2_w_out, ret_w_in=ret_w_in,
             ret_head_norm=ret_head_norm, ret_w_out=ret_w_out, gla_w_in=gla_w_in, gla_w_gate=gla_w_gate,
             gla_b_gate=gla_b_gate, gla_head_norm=gla_head_norm, gla_w_out=gla_w_out, final_norm=final_norm)
    m = dict(zip(_WEIGHTS, (m_meta_tokens, m_norm_ffn1, m_ffn1_w_in, m_ffn1_w_out, m_norm_mix, m_norm_ffn2, m_ffn2_w_in,
                            m_ffn2_w_out, m_ret_w_in, m_ret_head_norm, m_ret_w_out, m_gla_w_in, m_gla_w_gate,
                            m_gla_b_gate, m_gla_head_norm, m_gla_w_out, m_final_norm)))
    v = dict(zip(_WEIGHTS, (v_meta_tokens, v_norm_ffn1, v_ffn1_w_in, v_ffn1_w_out, v_norm_mix, v_norm_ffn2, v_ffn2_w_in,
                            v_ffn2_w_out, v_ret_w_in, v_ret_head_norm, v_ret_w_out, v_gla_w_in, v_gla_w_gate,
                            v_gla_b_gate, v_gla_head_norm, v_gla_w_out, v_final_norm)))

    exchange = _GradExchange()
    d_x = _sequence_grads(x[0], loss_target[0], p, _WeightGather(p), exchange)
    sums, small = exchange.collect()
    names = [("ffn2_in", 1), ("ffn2_out", 1), ("gla_in", 0), ("gla_out", 0), ("ffn1_in", 1), ("ffn1_out", 1),
             ("ffn2_in", 0), ("ffn2_out", 0), ("ret_in", 0), ("ret_out", 0), ("ffn1_in", 0), ("ffn1_out", 0)]
    swapped = _swap_cores([a for group in sums for a in group])
    shard = {key: a.reshape(-1, a.shape[-1]) for key, a in zip(names, swapped)}
    big = {name: [shard[name, layer] for layer in range(2) if (name, layer) in shard] for name, _ in names}

    chip = 2 * lax.axis_index("x") + lax.axis_index("y")
    cols = lambda a, n: lax.dynamic_slice_in_dim(a, chip * n, n, axis=a.ndim - 1)
    (s_meta, s_n1a, s_n1b, s_nma, s_nmb, s_n2a, s_n2b, s_final, s_ret_gain, s_wg, s_bg, s_gla_gain, s_loss) = small
    grads = {
        "meta_tokens": cols(s_meta, 256), "norm_ffn1": jnp.concatenate([s_n1a, s_n1b]),
        "norm_mix": jnp.concatenate([s_nma, s_nmb]), "norm_ffn2": jnp.concatenate([s_n2a, s_n2b]),
        "final_norm": s_final.reshape(D),
        "ret_head_norm": cols(s_ret_gain.reshape(1, HEADS, RET_DV), RET_DV // N_CHIPS),
        "gla_w_gate": cols(s_wg, GLA_DK)[None], "gla_b_gate": cols(s_bg, GLA_DK),
        "gla_head_norm": cols(s_gla_gain.reshape(1, HEADS, GLA_DV), GLA_DV // N_CHIPS),
        "ffn1_w_in": jnp.stack(big["ffn1_in"]), "ffn1_w_out": jnp.stack(big["ffn1_out"]),
        "ffn2_w_in": jnp.stack(big["ffn2_in"]), "ffn2_w_out": jnp.stack(big["ffn2_out"]),
        "ret_w_in": big["ret_in"][0][None], "ret_w_out": big["ret_out"][0][None],
        "gla_w_in": big["gla_in"][0][None], "gla_w_out": big["gla_out"][0][None],
    }

    delta, new_m, new_v = {}, {}, {}
    for name in _BIG:
        delta[name], new_m[name], new_v[name] = _adamw(p[name], grads[name], m[name], v[name], f"adamw_{name}")
    packed = [_pack_rows([d[name] for name in _LOCAL_SMALL], 128)[None] for d in (p, grads, m, v)]
    out = _adamw(*packed, "adamw_small")
    shapes = [p[name].shape for name in _LOCAL_SMALL]
    for d, a in zip((delta, new_m, new_v), out):
        d.update(zip(_LOCAL_SMALL, _unpack_rows(a, shapes)))

    return (s_loss.reshape(()), d_x[None], *[grads[n] for n in _WEIGHTS], *[delta[n] for n in _WEIGHTS],
            *[new_m[n] for n in _WEIGHTS], *[new_v[n] for n in _WEIGHTS])
```

```python
import functools

import jax
import jax.numpy as jnp
from jax import lax
from jax.experimental import pallas as pl
from jax.experimental.pallas import tpu as pltpu

F32, BF16 = jnp.float32, jnp.bfloat16
MESH = pl.DeviceIdType.MESH

D = 1024
N_META = 16
CHUNK = 64
RET_CHUNK = 128
FRONT = 256
D_FF = 2816
EPS = 1e-6
HEADS = 4
RET_DK, RET_DV = 256, 512
GLA_DK, GLA_DV = 128, 256
GLA_RANK = 16
GLA_TAU = 16.0
GLA_IN = 2 * HEADS * GLA_DK + 2 * HEADS * GLA_DV + GLA_RANK
GLA_U = 3200
ROPE_BASE = 10000.0
N_CHIPS = 4
N_DEV = 8

ADAM_LR, ADAM_B1, ADAM_B2, ADAM_EPS, ADAM_WD, ADAM_STEP = 0.001, 0.9, 0.999, 1e-08, 0.01, 10

VMEM_LIMIT_BYTES = 56 * 1024 * 1024
TM = 768
TM_SMALL = 256


TM_RESIDENT = 384
MXU_TILE = 256


def _cp(n_axes):
    return pltpu.CompilerParams(dimension_semantics=("arbitrary",) * n_axes, vmem_limit_bytes=VMEM_LIMIT_BYTES)


def _resident(shape, n_axes):
    zeros = (0,) * len(shape)
    index = (lambda i: zeros) if n_axes == 1 else (lambda i, j: zeros)
    return pl.BlockSpec(shape, index, pipeline_mode=pl.Buffered(1))


def _dg(a, b, ca, cb):
    nb = a.ndim - 2
    dims = (((ca + nb,), (cb + nb,)), (tuple(range(nb)), tuple(range(nb))))
    return lax.dot_general(a.astype(BF16), b.astype(BF16), dims, preferred_element_type=F32)


@jax.custom_vjp
def _nn(a, b):
    return _dg(a, b, 1, 0)


@jax.custom_vjp
def _nt(a, b):
    return _dg(a, b, 1, 1)


@jax.custom_vjp
def _tn(a, b):
    return _dg(a, b, 0, 0)


_nn.defvjp(lambda a, b: (_nn(a, b), (a, b)), lambda res, g: (_nt(g, res[1]), _tn(res[0], g)))
_nt.defvjp(lambda a, b: (_nt(a, b), (a, b)), lambda res, g: (_nn(g, res[1]), _tn(g, res[0])))
_tn.defvjp(lambda a, b: (_tn(a, b), (a, b)), lambda res, g: (_nt(res[1], g), _nn(res[0], g)))


def _split3_dot(m, a):
    a1 = a.astype(BF16)
    r1 = a - a1.astype(F32)
    a2 = r1.astype(BF16)
    a3 = (r1 - a2.astype(F32)).astype(BF16)
    mb = jnp.broadcast_to(m, a.shape[:-2] + m.shape)
    return _dg(mb, a1, 1, 0) + _dg(mb, a2, 1, 0) + _dg(mb, a3, 1, 0)


@jax.custom_vjp
def _cum(m, mt, a):
    return _split3_dot(m, a)


_cum.defvjp(lambda m, mt, a: (_split3_dot(m, a), (m, mt)),
            lambda res, g: (jnp.zeros_like(res[0]), jnp.zeros_like(res[1]), _split3_dot(res[1], g)))


def _sigmoid(x):
    return 1.0 / (1.0 + jnp.exp(-x))


def _rms(x):
    return lax.rsqrt(jnp.mean(x * x, axis=-1, keepdims=True) + EPS)


def _rmsnorm_bwd(dy, x, gain):
    r = _rms(x)
    xhat = x * r
    dxh = dy * gain
    return r * (dxh - xhat * jnp.mean(dxh * xhat, axis=-1, keepdims=True)), xhat


def _norm_proj(h, gain, w, name):
    tp, d = h.shape
    s, _, ns = w.shape

    tm = TM_RESIDENT

    def body(h_ref, g_ref, w_ref, hn_ref, u_ref):
        @pl.when(pl.program_id(1) == 0)
        def _():
            x = h_ref[...]
            hn_ref[...] = (x * _rms(x) * g_ref[...]).astype(BF16)

        u_ref[...] = jnp.dot(hn_ref[...], w_ref[pl.program_id(1)], preferred_element_type=F32).astype(BF16)

    return pl.pallas_call(
        body, name=name, grid=(tp // tm, s),
        in_specs=[pl.BlockSpec((tm, d), lambda i, j: (i, 0)), pl.BlockSpec((1, d), lambda i, j: (0, 0)),
                  _resident(w.shape, 2)],
        out_specs=[pl.BlockSpec((tm, d), lambda i, j: (i, 0)), pl.BlockSpec((tm, ns), lambda i, j: (i, j))],
        out_shape=[jax.ShapeDtypeStruct((tp, d), BF16), jax.ShapeDtypeStruct((tp, s * ns), BF16)],
        compiler_params=_cp(2))(h, gain, w)


def _norm_ffn_in(h, gain, w, name):
    tp, d = h.shape
    ff = w.shape[1] // 2
    tm = TM_RESIDENT
    blocks = [(c, min(c + 6 * MXU_TILE, ff)) for c in range(0, ff, 6 * MXU_TILE)]

    def body(h_ref, g_ref, w_ref, hn_ref, dg_ref, du_ref, act_ref):
        x = h_ref[...]
        a = (x * _rms(x) * g_ref[...]).astype(BF16)
        hn_ref[...] = a
        for c0, c1 in blocks:
            g = jnp.dot(a, w_ref[:, c0:c1], preferred_element_type=F32)
            u = jnp.dot(a, w_ref[:, ff + c0:ff + c1], preferred_element_type=F32)
            sg = _sigmoid(g)
            silu = g * sg
            dg_ref[:, c0:c1] = (u * (sg + silu * (1.0 - sg))).astype(BF16)
            du_ref[:, c0:c1] = silu.astype(BF16)
            act_ref[:, c0:c1] = (silu * u).astype(BF16)

    wide = jax.ShapeDtypeStruct((tp, ff), BF16)
    return pl.pallas_call(
        body, name=name, grid=(tp // tm,),
        in_specs=[pl.BlockSpec((tm, d), lambda i: (i, 0)), pl.BlockSpec((1, d), lambda i: (0, 0)),
                  _resident(w.shape, 1)],
        out_specs=[pl.BlockSpec((tm, d), lambda i: (i, 0))] + [pl.BlockSpec((tm, ff), lambda i: (i, 0))] * 3,
        out_shape=[jax.ShapeDtypeStruct((tp, d), BF16), wide, wide, wide],
        compiler_params=_cp(1))(h, gain, w)


def _out_proj(a, w, h, scale, name):
    tp, k = a.shape
    d = w.shape[1]

    def body(a_ref, w_ref, h_ref, o_ref):
        o_ref[...] = h_ref[...] + scale * jnp.dot(a_ref[...], w_ref[...], preferred_element_type=F32)

    return pl.pallas_call(
        body, name=name, grid=(tp // TM,),
        in_specs=[pl.BlockSpec((TM, k), lambda i: (i, 0)), pl.BlockSpec((k, d), lambda i: (0, 0)),
                  pl.BlockSpec((TM, d), lambda i: (i, 0))],
        out_specs=pl.BlockSpec((TM, d), lambda i: (i, 0)),
        out_shape=jax.ShapeDtypeStruct((tp, d), F32),
        compiler_params=_cp(1))(a, w, h)


def _ffn_dact(dh, w_out, act_dg, act_du, name):
    tp, d = dh.shape
    ff = w_out.shape[0]
    tm = TM_RESIDENT

    def body(dh_ref, w_ref, dg_ref, du_ref, o_ref):
        dy = (0.5 * dh_ref[...]).astype(BF16)
        dact = lax.dot_general(dy, w_ref[...], (((1,), (1,)), ((), ())), preferred_element_type=F32)
        o_ref[:, :ff] = (dact * dg_ref[...].astype(F32)).astype(BF16)
        o_ref[:, ff:] = (dact * du_ref[...].astype(F32)).astype(BF16)

    return pl.pallas_call(
        body, name=name, grid=(tp // tm,),
        in_specs=[pl.BlockSpec((tm, d), lambda i: (i, 0)), _resident(w_out.shape, 1),
                  pl.BlockSpec((tm, ff), lambda i: (i, 0)), pl.BlockSpec((tm, ff), lambda i: (i, 0))],
        out_specs=pl.BlockSpec((tm, 2 * ff), lambda i: (i, 0)),
        out_shape=jax.ShapeDtypeStruct((tp, 2 * ff), BF16),
        compiler_params=_cp(1))(dh, w_out, act_dg, act_du)


def _dgrad(dh, w, name):
    tp, d = dh.shape
    k = w.shape[0]

    def body(dh_ref, w_ref, o_ref):
        o_ref[...] = lax.dot_general(dh_ref[...].astype(BF16), w_ref[...], (((1,), (1,)), ((), ())),
                                     preferred_element_type=F32).astype(BF16)

    return pl.pallas_call(
        body, name=name, grid=(tp // TM,),
        in_specs=[pl.BlockSpec((TM, d), lambda i: (i, 0)), pl.BlockSpec((k, d), lambda i: (0, 0))],
        out_specs=pl.BlockSpec((TM, k), lambda i: (i, 0)),
        out_shape=jax.ShapeDtypeStruct((tp, k), BF16),
        compiler_params=_cp(1))(dh, w)


def _wgrad(a, b, *, bm, bn, scale, sharded, name):
    tp, m = a.shape
    n = b.shape[1]
    nk = tp // TM

    def body(a_ref, b_ref, o_ref, acc_ref):
        k = pl.program_id(2)

        @pl.when(k == 0)
        def _():
            acc_ref[...] = jnp.zeros_like(acc_ref)

        bb = b_ref[...]
        if scale != 1.0:
            bb = scale * bb
        acc_ref[...] += lax.dot_general(a_ref[...], bb.astype(BF16), (((0,), (0,)), ((), ())),
                                        preferred_element_type=F32)

        @pl.when(k == nk - 1)
        def _():
            o_ref[...] = acc_ref[...].astype(BF16)

    if sharded:
        assert m == bm
        out_spec = pl.BlockSpec((None, bm, bn), lambda i, j, k: (j, 0, 0))
        out_shape = jax.ShapeDtypeStruct((n // bn, m, bn), BF16)
    else:
        out_spec = pl.BlockSpec((bm, bn), lambda i, j, k: (i, j))
        out_shape = jax.ShapeDtypeStruct((m, n), BF16)
    return pl.pallas_call(
        body, name=name, grid=(m // bm, n // bn, nk),
        in_specs=[pl.BlockSpec((TM, bm), lambda i, j, k: (k, i)), pl.BlockSpec((TM, bn), lambda i, j, k: (k, j))],
        out_specs=out_spec, out_shape=out_shape,
        scratch_shapes=[pltpu.VMEM((bm, bn), F32)],
        compiler_params=_cp(3))(a, b)


def _dgrad_norm(du, w, h, gain, dh_out, name):
    tp, d = h.shape
    s, _, ns = w.shape
    tm = TM_RESIDENT

    def body(du_ref, w_ref, h_ref, g_ref, dho_ref, dhi_ref, dg_ref):
        @pl.when(pl.program_id(0) == 0)
        def _():
            dg_ref[...] = jnp.zeros_like(dg_ref)

        dhn = None
        for k in range(s):
            part = lax.dot_general(du_ref[:, ns * k:ns * (k + 1)], w_ref[k], (((1,), (1,)), ((), ())),
                                   preferred_element_type=F32)
            dhn = part if dhn is None else dhn + part
        dx, xhat = _rmsnorm_bwd(dhn, h_ref[...], g_ref[...])
        dg_ref[...] += jnp.sum(dhn * xhat, axis=0, keepdims=True)
        dhi_ref[...] = dho_ref[...] + dx

    return pl.pallas_call(
        body, name=name, grid=(tp // tm,),
        in_specs=[pl.BlockSpec((tm, s * ns), lambda i: (i, 0)), _resident(w.shape, 1),
                  pl.BlockSpec((tm, d), lambda i: (i, 0)), pl.BlockSpec((1, d), lambda i: (0, 0)),
                  pl.BlockSpec((tm, d), lambda i: (i, 0))],
        out_specs=[pl.BlockSpec((tm, d), lambda i: (i, 0)), pl.BlockSpec((1, d), lambda i: (0, 0))],
        out_shape=[jax.ShapeDtypeStruct((tp, d), F32), jax.ShapeDtypeStruct((1, d), F32)],
        compiler_params=_cp(1))(du, w, h, gain, dh_out)


def _loss_head(h, gain, target, name):
    tp, d = h.shape
    tm = TM_SMALL
    front_tiles = FRONT // tm

    def body(h_ref, g_ref, t_ref, dh_ref, dg_ref, loss_ref):
        i = pl.program_id(0)

        @pl.when(i == 0)
        def _():
            dg_ref[...] = jnp.zeros_like(dg_ref)
            loss_ref[...] = jnp.zeros_like(loss_ref)

        x = h_ref[...]
        gain_v = g_ref[...]
        y = x * _rms(x) * gain_v
        err = jnp.where(i >= front_tiles, y - t_ref[...], 0.0)
        loss_ref[...] += 0.5 * jnp.sum(jnp.mean(err * err, axis=-1, keepdims=True), axis=0, keepdims=True)
        dy = err * (1.0 / d)
        dx, xhat = _rmsnorm_bwd(dy, x, gain_v)
        dg_ref[...] += jnp.sum(dy * xhat, axis=0, keepdims=True)
        dh_ref[...] = dx

    return pl.pallas_call(
        body, name=name, grid=(tp // tm,),
        in_specs=[pl.BlockSpec((tm, d), lambda i: (i, 0)), pl.BlockSpec((1, d), lambda i: (0, 0)),
                  pl.BlockSpec((tm, d), lambda i: (jnp.maximum(i - front_tiles, 0), 0))],
        out_specs=[pl.BlockSpec((tm, d), lambda i: (i, 0)), pl.BlockSpec((1, d), lambda i: (0, 0)),
                   pl.BlockSpec((1, 128), lambda i: (0, 0))],
        out_shape=[jax.ShapeDtypeStruct((tp, d), F32), jax.ShapeDtypeStruct((1, d), F32),
                   jax.ShapeDtypeStruct((1, 128), F32)],
        compiler_params=_cp(1))(h, gain, target)


def _gated_headnorm(o, g, gain):
    return o * _rms(o) * gain * (g * _sigmoid(g))


def _row_mask(chunk, size=CHUNK):
    rows = chunk * size + lax.broadcasted_iota(jnp.int32, (size, 1), 0)
    return (rows >= FRONT - N_META).astype(F32)


def _ret_head(q1, q2, k1, k2, v, g, state, gain, cos, sin, dmat, dq, dk, dc):
    q = jnp.concatenate([q1 * cos - q2 * sin, q1 * sin + q2 * cos], axis=-1)
    k = jnp.concatenate([k1 * cos - k2 * sin, k1 * sin + k2 * cos], axis=-1) * (RET_DK ** -0.5)
    scores = _nt(q, k) * dmat
    o = _nn(scores, v) + _nn(q * dq, state)
    new_state = state * dc + _tn(k * dk, v)
    return _gated_headnorm(o, g, gain), new_state


def _ret_consts():
    log_gamma = jnp.log1p(-2.0 ** (-5.0 - jnp.arange(HEADS, dtype=F32)))
    idx = jnp.arange(RET_CHUNK, dtype=F32)
    rel = idx[:, None] - idx[None, :]
    dmat = jnp.where(rel >= 0, jnp.exp(log_gamma[:, None, None] * jnp.maximum(rel, 0.0)), 0.0)
    dq = jnp.exp(log_gamma[:, None] * (idx + 1.0))[..., None]
    dk = jnp.exp(log_gamma[:, None] * (RET_CHUNK - 1.0 - idx))[..., None]
    dc = jnp.broadcast_to(jnp.exp(log_gamma * RET_CHUNK)[:, None, None], (HEADS, 1, 128))
    return dmat, dq, dk, dc


def _rope_tables(tp):
    half = RET_DK // 2
    inv = 1.0 / (ROPE_BASE ** jnp.linspace(0.0, 1.0, half, dtype=F32))
    pos = (jnp.arange(tp) - (FRONT - N_META)).astype(F32)
    ang = pos[:, None] * inv[None, :]
    return jnp.cos(ang), jnp.sin(ang)


_RET_V0, _RET_G0 = 2 * D, 4 * D


def _heads(ref, start, width, stride=None):
    stride = width if stride is None else stride
    return jnp.stack([ref[:, start + stride * h:start + stride * h + width].astype(F32) for h in range(HEADS)])


def _put_heads(ref, start, value, mask, stride=None):
    width = value.shape[-1]
    stride = width if stride is None else stride
    for h in range(HEADS):
        ref[:, start + stride * h:start + stride * h + width] = (value[h] * mask).astype(ref.dtype)


def _ret_pieces(u_ref):
    hk = RET_DK // 2
    return (_heads(u_ref, 0, hk, RET_DK), _heads(u_ref, hk, hk, RET_DK), _heads(u_ref, D, hk, RET_DK),
            _heads(u_ref, D + hk, hk, RET_DK), _heads(u_ref, _RET_V0, RET_DV), _heads(u_ref, _RET_G0, RET_DV))


def _ret_const_specs(rev=None):
    c = (lambda n: (rev(n), 0)) if rev else (lambda n: (n, 0))
    z3 = lambda n: (0, 0, 0)
    return [pl.BlockSpec((RET_CHUNK, RET_DK // 2), c), pl.BlockSpec((RET_CHUNK, RET_DK // 2), c),
            pl.BlockSpec((HEADS, RET_CHUNK, RET_CHUNK), z3), pl.BlockSpec((HEADS, RET_CHUNK, 1), z3),
            pl.BlockSpec((HEADS, RET_CHUNK, 1), z3), pl.BlockSpec((HEADS, 1, 128), z3)]


def _ret_fwd(u, gain, rope, name):
    tp = u.shape[0]
    nch = tp // RET_CHUNK
    cos, sin = rope
    dmat, dq, dk, dc = _ret_consts()

    def body(u_ref, gain_ref, cos_ref, sin_ref, dmat_ref, dq_ref, dk_ref, dc_ref, on_ref, st_ref, state_ref):
        @pl.when(pl.program_id(0) == 0)
        def _():
            state_ref[...] = jnp.zeros_like(state_ref)

        state = state_ref[...]
        st_ref[...] = state.astype(BF16)
        on, new_state = _ret_head(*_ret_pieces(u_ref), state, _heads(gain_ref, 0, RET_DV), cos_ref[...], sin_ref[...],
                                  dmat_ref[...], dq_ref[...], dk_ref[...], dc_ref[...][:, :, :1])
        state_ref[...] = new_state
        _put_heads(on_ref, 0, on, 1.0)

    return pl.pallas_call(
        body, name=name, grid=(nch,),
        in_specs=[pl.BlockSpec((RET_CHUNK, 6 * D), lambda n: (n, 0)), pl.BlockSpec((1, HEADS * RET_DV), lambda n: (0, 0))]
                 + _ret_const_specs(),
        out_specs=[pl.BlockSpec((RET_CHUNK, HEADS * RET_DV), lambda n: (n, 0)),
                   pl.BlockSpec((None, HEADS, RET_DK, RET_DV), lambda n: (n, 0, 0, 0))],
        out_shape=[jax.ShapeDtypeStruct((tp, HEADS * RET_DV), BF16),
                   jax.ShapeDtypeStruct((nch, HEADS, RET_DK, RET_DV), BF16)],
        scratch_shapes=[pltpu.VMEM((HEADS, RET_DK, RET_DV), F32)],
        compiler_params=_cp(1))(u, gain, cos, sin, dmat, dq, dk, dc)


def _ret_bwd(u, gain, rope, states, d_on, name):
    tp = u.shape[0]
    nch = tp // RET_CHUNK
    cos, sin = rope
    dmat, dq, dk, dc = _ret_consts()
    rev = lambda n: nch - 1 - n
    hk = RET_DK // 2

    def body(u_ref, gain_ref, st_ref, don_ref, cos_ref, sin_ref, dmat_ref, dq_ref, dk_ref, dc_ref,
             du_ref, dgain_ref, dstate_ref):
        @pl.when(pl.program_id(0) == 0)
        def _():
            dstate_ref[...] = jnp.zeros_like(dstate_ref)
            dgain_ref[...] = jnp.zeros_like(dgain_ref)

        mask = _row_mask(rev(pl.program_id(0)), RET_CHUNK)
        consts = (cos_ref[...], sin_ref[...], dmat_ref[...], dq_ref[...], dk_ref[...], dc_ref[...][:, :, :1])
        _, vjp = jax.vjp(lambda *a: _ret_head(*a, *consts), *_ret_pieces(u_ref), st_ref[...].astype(F32),
                         _heads(gain_ref, 0, RET_DV))
        dq1, dq2, dk1, dk2, dv, dg, dstate, dgain = vjp((_heads(don_ref, 0, RET_DV), dstate_ref[...]))
        dstate_ref[...] = dstate
        for hd in range(HEADS):
            dgain_ref[:, RET_DV * hd:RET_DV * (hd + 1)] += dgain[hd]
        _put_heads(du_ref, 0, dq1, mask, RET_DK)
        _put_heads(du_ref, hk, dq2, mask, RET_DK)
        _put_heads(du_ref, D, dk1, mask, RET_DK)
        _put_heads(du_ref, D + hk, dk2, mask, RET_DK)
        _put_heads(du_ref, _RET_V0, dv, mask)
        _put_heads(du_ref, _RET_G0, dg, mask)

    return pl.pallas_call(
        body, name=name, grid=(nch,),
        in_specs=[pl.BlockSpec((RET_CHUNK, 6 * D), lambda n: (rev(n), 0)),
                  pl.BlockSpec((1, HEADS * RET_DV), lambda n: (0, 0)),
                  pl.BlockSpec((None, HEADS, RET_DK, RET_DV), lambda n: (rev(n), 0, 0, 0)),
                  pl.BlockSpec((RET_CHUNK, HEADS * RET_DV), lambda n: (rev(n), 0))] + _ret_const_specs(rev),
        out_specs=[pl.BlockSpec((RET_CHUNK, 6 * D), lambda n: (rev(n), 0)),
                   pl.BlockSpec((1, HEADS * RET_DV), lambda n: (0, 0))],
        out_shape=[jax.ShapeDtypeStruct((tp, 6 * D), BF16), jax.ShapeDtypeStruct((1, HEADS * RET_DV), F32)],
        scratch_shapes=[pltpu.VMEM((HEADS, RET_DK, RET_DV), F32)],
        compiler_params=_cp(1))(u, gain, states, d_on, cos, sin, dmat, dq, dk, dc)


_GLA_K0, _GLA_V0, _GLA_G0, _GLA_Z0 = 512, 1024, 2048, 3072


def _gla_head(q, k, v, g, z, state_t, wg, bg, gain, mask, lo, lo_t, loc, loc_t):
    ga = _nn(jnp.broadcast_to(z, wg.shape[:-2] + z.shape), wg) + bg
    log_a = (jnp.minimum(ga, 0.0) - jnp.log(1.0 + jnp.exp(-jnp.abs(ga)))) * (mask * (1.0 / GLA_TAU))
    bcum = _cum(lo, lo_t, log_a)
    bmid = _cum(loc, loc_t, log_a)
    btot = jnp.sum(log_a, axis=-2, keepdims=True)
    qs = q * (GLA_DK ** -0.5)
    causal = lax.broadcasted_iota(jnp.int32, (CHUNK, CHUNK), 0) >= lax.broadcasted_iota(jnp.int32, (CHUNK, CHUNK), 1)
    scores = jnp.where(causal, _nt(qs * jnp.exp(bmid), k * jnp.exp(-bmid)), 0.0)
    o = _nn(scores, v) + _nt(qs * jnp.exp(bcum), state_t)
    new_state_t = state_t * jnp.exp(btot) + _tn(v, k * jnp.exp(btot - bcum))
    return _gated_headnorm(o, g, gain), new_state_t


def _cum_mats():
    r = lax.broadcasted_iota(jnp.int32, (CHUNK, CHUNK), 0)
    c = lax.broadcasted_iota(jnp.int32, (CHUNK, CHUNK), 1)
    mid = CHUNK // 2
    low = lambda a, b: (a >= b).astype(F32)
    lo, lo_t = low(r, c), low(c, r)
    loc = lo - (c <= mid).astype(F32)
    loc_t = lo_t - (r <= mid).astype(F32)
    return tuple(m.astype(BF16) for m in (lo, lo_t, loc, loc_t))


def _gla_pieces(u_ref):
    return (_heads(u_ref, 0, GLA_DK), _heads(u_ref, _GLA_K0, GLA_DK), _heads(u_ref, _GLA_V0, GLA_DV),
            _heads(u_ref, _GLA_G0, GLA_DV), u_ref[:, _GLA_Z0:].astype(F32))


def _gla_fwd(u, wg, bg, gain, name):
    tp = u.shape[0]
    nch = tp // CHUNK

    def body(u_ref, wg_ref, bg_ref, gain_ref, on_ref, st_ref, state_ref):
        @pl.when(pl.program_id(0) == 0)
        def _():
            state_ref[...] = jnp.zeros_like(state_ref)

        state = state_ref[...]
        st_ref[...] = state.astype(BF16)
        on, new_state = _gla_head(*_gla_pieces(u_ref), state, _heads(wg_ref, 0, GLA_DK), _heads(bg_ref, 0, GLA_DK),
                                  _heads(gain_ref, 0, GLA_DV), _row_mask(pl.program_id(0)), *_cum_mats())
        state_ref[...] = new_state
        _put_heads(on_ref, 0, on, 1.0)

    return pl.pallas_call(
        body, name=name, grid=(nch,),
        in_specs=[pl.BlockSpec((CHUNK, GLA_U), lambda n: (n, 0)), pl.BlockSpec((128, HEADS * GLA_DK), lambda n: (0, 0)),
                  pl.BlockSpec((1, HEADS * GLA_DK), lambda n: (0, 0)), pl.BlockSpec((1, HEADS * GLA_DV), lambda n: (0, 0))],
        out_specs=[pl.BlockSpec((CHUNK, HEADS * GLA_DV), lambda n: (n, 0)),
                   pl.BlockSpec((None, HEADS, GLA_DV, GLA_DK), lambda n: (n, 0, 0, 0))],
        out_shape=[jax.ShapeDtypeStruct((tp, HEADS * GLA_DV), BF16),
                   jax.ShapeDtypeStruct((nch, HEADS, GLA_DV, GLA_DK), BF16)],
        scratch_shapes=[pltpu.VMEM((HEADS, GLA_DV, GLA_DK), F32)],
        compiler_params=_cp(1))(u, wg, bg, gain)


def _gla_bwd(u, wg, bg, gain, states, d_on, name):
    tp = u.shape[0]
    nch = tp // CHUNK
    rev = lambda n: nch - 1 - n

    def body(u_ref, wg_ref, bg_ref, gain_ref, st_ref, don_ref, du_ref, dwg_ref, dbg_ref, dgain_ref, dstate_ref):
        @pl.when(pl.program_id(0) == 0)
        def _():
            dstate_ref[...] = jnp.zeros_like(dstate_ref)
            dwg_ref[...] = jnp.zeros_like(dwg_ref)
            dbg_ref[...] = jnp.zeros_like(dbg_ref)
            dgain_ref[...] = jnp.zeros_like(dgain_ref)

        mask = _row_mask(rev(pl.program_id(0)))
        mats = _cum_mats()
        _, vjp = jax.vjp(lambda *a: _gla_head(*a, mask, *mats), *_gla_pieces(u_ref), st_ref[...].astype(F32),
                         _heads(wg_ref, 0, GLA_DK), _heads(bg_ref, 0, GLA_DK), _heads(gain_ref, 0, GLA_DV))
        dq, dk, dv, dg, dz, dstate, dwg, dbg, dgain = vjp((_heads(don_ref, 0, GLA_DV), dstate_ref[...]))
        dstate_ref[...] = dstate
        for hd in range(HEADS):
            dwg_ref[:, GLA_DK * hd:GLA_DK * (hd + 1)] += dwg[hd]
            dbg_ref[:, GLA_DK * hd:GLA_DK * (hd + 1)] += dbg[hd]
            dgain_ref[:, GLA_DV * hd:GLA_DV * (hd + 1)] += dgain[hd]
        _put_heads(du_ref, 0, dq, mask)
        _put_heads(du_ref, _GLA_K0, dk, mask)
        _put_heads(du_ref, _GLA_V0, dv, mask)
        _put_heads(du_ref, _GLA_G0, dg, mask)
        du_ref[:, _GLA_Z0:] = dz.astype(BF16)

    full = lambda r, c: pl.BlockSpec((r, c), lambda n: (0, 0))
    return pl.pallas_call(
        body, name=name, grid=(nch,),
        in_specs=[pl.BlockSpec((CHUNK, GLA_U), lambda n: (rev(n), 0)), full(128, HEADS * GLA_DK),
                  full(1, HEADS * GLA_DK), full(1, HEADS * GLA_DV),
                  pl.BlockSpec((None, HEADS, GLA_DV, GLA_DK), lambda n: (rev(n), 0, 0, 0)),
                  pl.BlockSpec((CHUNK, HEADS * GLA_DV), lambda n: (rev(n), 0))],
        out_specs=[pl.BlockSpec((CHUNK, GLA_U), lambda n: (rev(n), 0)), full(128, HEADS * GLA_DK),
                   full(1, HEADS * GLA_DK), full(1, HEADS * GLA_DV)],
        out_shape=[jax.ShapeDtypeStruct((tp, GLA_U), BF16), jax.ShapeDtypeStruct((128, HEADS * GLA_DK), F32),
                   jax.ShapeDtypeStruct((1, HEADS * GLA_DK), F32), jax.ShapeDtypeStruct((1, HEADS * GLA_DV), F32)],
        scratch_shapes=[pltpu.VMEM((HEADS, GLA_DV, GLA_DK), F32)],
        compiler_params=_cp(1))(u, wg, bg, gain, states, d_on)


def _ffn_fwd(h, gain, w_in, w_out, tag):
    hn, ug, uu, act = _norm_ffn_in(h, gain, w_in, f"{tag}_in")
    if callable(w_out):
        w_out = w_out(act)
    return _out_proj(act, w_out, h, 0.5, f"{tag}_out"), (h, hn, ug, uu, act), w_out


def _ffn_bwd(dh, saved, gain, w_in, w_out, tag, push):
    h, hn, ug, uu, act = saved
    du = _ffn_dact(dh, w_out, ug, uu, f"{tag}_dact")
    d_w_out = _wgrad(act, dh, bm=D_FF // 2, bn=D, scale=0.5, sharded=False, name=f"{tag}_dwout")
    d_w_in = _wgrad(hn, du, bm=D, bn=D_FF, scale=1.0, sharded=False, name=f"{tag}_dwin")
    token = push([("cols", d_w_in), d_w_out])
    return _dgrad_norm(du, w_in[None], h, gain + token[0, 0], dh, f"{tag}_dnorm")


def _sequence_grads(x, target, p, weights, grads):
    row = lambda v, token: v.reshape(1, -1) + token[0, 0]
    gains = {}

    tok = weights.start(1, weights.start(0, None))
    weights.pin = tok
    h = jnp.concatenate([jnp.zeros((FRONT, D), F32), x], axis=0) + tok[0, 0]
    rope = _rope_tables(h.shape[0])
    w = weights.wait(0, [tok, h, *rope, *weights.later_shards(2)])
    tok = weights.start(2, w["l0_ffn1_in"])
    h = lax.dynamic_update_slice(h, w["meta"], (FRONT - N_META, 0))
    gains["l0_ffn1"] = row(p["norm_ffn1"][0], tok)
    h, s1, w["l0_ffn1_out"] = _ffn_fwd(h, gains["l0_ffn1"], w["l0_ffn1_in"],
                                       lambda act: weights.wait(1, act)["l0_ffn1_out"], "l0_ffn1")
    w.update(weights.wait(2, h))
    tok = weights.start(3, w["ret_in"])
    gains["ret"] = row(p["norm_mix"][0], tok)
    hn, u = _norm_proj(h, gains["ret"], w["ret_in"], "ret_in")
    on, states = _ret_fwd(u, w["ret_gain"], rope, "ret_fwd")
    h_mix = _out_proj(on, w["ret_out"], h, 1.0, "ret_out")
    s2 = (h, hn, u, on, states)
    w.update(weights.wait(3, h_mix))
    tok = weights.start(4, w["l0_ffn2_in"])
    gains["l0_ffn2"] = row(p["norm_ffn2"][0], tok)
    h, s3, _ = _ffn_fwd(h_mix, gains["l0_ffn2"], w["l0_ffn2_in"], w["l0_ffn2_out"], "l0_ffn2")
    saved = [(s1, s2, s3)]

    w.update(weights.wait(4, h))
    tok = weights.start(5, w["l1_ffn1_in"])
    gains["l1_ffn1"] = row(p["norm_ffn1"][1], tok)
    h, s1, _ = _ffn_fwd(h, gains["l1_ffn1"], w["l1_ffn1_in"], w["l1_ffn1_out"], "l1_ffn1")
    w.update(weights.wait(5, h))
    tok = weights.start(6, w["gla_out"])
    gains["gla"] = row(p["norm_mix"][1], tok)
    hn, u = _norm_proj(h, gains["gla"], w["gla_in"], "gla_in")
    on, states = _gla_fwd(u, w["gla_wg"], w["gla_bg"], w["gla_gain"], "gla_fwd")
    h_mix = _out_proj(on, w["gla_out"], h, 1.0, "gla_out")
    s2 = (h, hn, u, on, states)
    w.update(weights.wait(6, h_mix))
    gains["l1_ffn2"] = p["norm_ffn2"][1].reshape(1, -1)
    h, s3, _ = _ffn_fwd(h_mix, gains["l1_ffn2"], w["l1_ffn2_in"], w["l1_ffn2_out"], "l1_ffn2")
    saved.append((s1, s2, s3))

    dh, d_final, loss = _loss_head(h, p["final_norm"].reshape(1, -1), target, "loss_head")
    small = {"final_norm": d_final, "norm_ffn1": [None, None], "norm_mix": [None, None], "norm_ffn2": [None, None]}
    pusher = lambda k: functools.partial(grads.push, k)

    s1, s2, s3 = saved[1]
    dh, small["norm_ffn2"][1] = _ffn_bwd(dh, s3, gains["l1_ffn2"], w["l1_ffn2_in"], w["l1_ffn2_out"], "l1_ffn2",
                                         pusher(0))
    h_in, hn, u, on, states = s2
    d_on = _dgrad(dh, w["gla_out"], "gla_don")
    d_out = _wgrad(on, dh, bm=D, bn=D, scale=1.0, sharded=False, name="gla_dwout")
    du, small["gla_wg"], small["gla_bg"], small["gla_gain"] = _gla_bwd(u, w["gla_wg"], w["gla_bg"], w["gla_gain"],
                                                                       states, d_on, "gla_bwd")
    d_in = _wgrad(hn, du, bm=D, bn=GLA_U // 5, scale=1.0, sharded=False, name="gla_dwin")
    d_in = jnp.moveaxis(d_in[:, :GLA_IN].reshape(D, N_CHIPS, -1), 1, 0)
    tok = grads.push(1, [d_in, d_out])
    dh, small["norm_mix"][1] = _dgrad_norm(du, w["gla_in"], h_in, gains["gla"] + tok[0, 0], dh, "gla_dnorm")
    dh, small["norm_ffn1"][1] = _ffn_bwd(dh, s1, gains["l1_ffn1"], w["l1_ffn1_in"], w["l1_ffn1_out"], "l1_ffn1",
                                         pusher(2))

    s1, s2, s3 = saved[0]
    dh, small["norm_ffn2"][0] = _ffn_bwd(dh, s3, gains["l0_ffn2"], w["l0_ffn2_in"], w["l0_ffn2_out"], "l0_ffn2",
                                         pusher(3))
    h_in, hn, u, on, states = s2
    d_on = _dgrad(dh, w["ret_out"], "ret_don")
    d_out = _wgrad(on, dh, bm=D, bn=D, scale=1.0, sharded=False, name="ret_dwout")
    du, small["ret_gain"] = _ret_bwd(u, w["ret_gain"], rope, states, d_on, "ret_bwd")
    d_in = _wgrad(hn, du, bm=D, bn=w["ret_in"].shape[2], scale=1.0, sharded=True, name="ret_dwin")
    tok = grads.push(4, [d_in, d_out])
    dh, small["norm_mix"][0] = _dgrad_norm(du, w["ret_in"], h_in, gains["ret"] + tok[0, 0], dh, "ret_dnorm")
    dh, small["norm_ffn1"][0] = _ffn_bwd(dh, s1, gains["l0_ffn1"], w["l0_ffn1_in"], w["l0_ffn1_out"], "l0_ffn1",
                                         pusher(5))
    grads.push(6, [], [dh[FRONT - N_META:FRONT], *small["norm_ffn1"], *small["norm_mix"], *small["norm_ffn2"],
                       small["final_norm"], small["ret_gain"], small["gla_wg"][:GLA_RANK], small["gla_bg"],
                       small["gla_gain"], loss[:, :1]])
    return dh[FRONT:]


_HBM = pl.BlockSpec(memory_space=pl.ANY)


def _place():
    return lax.axis_index("x"), lax.axis_index("y"), lax.axis_index("c")


def _flip(v, bit):
    return 1 - v if bit else v


DMA_CHUNK_BYTES = 128 * 1024


def _row_chunks(ref):
    rows, cols = ref.shape
    step = _row_tile(rows, max(16, DMA_CHUNK_BYTES // (cols * ref.dtype.itemsize)))
    return [pl.ds(a, step) for a in range(0, rows, step)]


def _whole(src, dst, send_sem, recv_sem, peer):
    return pltpu.make_async_remote_copy(src_ref=src, dst_ref=dst, send_sem=send_sem, recv_sem=recv_sem,
                                        device_id=peer, device_id_type=MESH)


def _send(src, dst, send_sem, recv_sem, peer):
    for rows in _row_chunks(src):
        _whole(src.at[rows], dst.at[rows], send_sem, recv_sem, peer).start()
    return _whole(src, dst, send_sem, recv_sem, peer)


_HBM_ONLY = pl.BlockSpec(memory_space=pltpu.HBM)
_SEMS = pl.BlockSpec(memory_space=pltpu.SEMAPHORE)
_SIDE_EFFECT = pltpu.CompilerParams(has_side_effects=pltpu.SideEffectType.DATAFLOW_SIDE_EFFECTING)
_GATHER_FLIPS = [(1, 0, 0), (0, 1, 0), (1, 1, 0), (0, 0, 1)]
_PEER_FLIPS = [(fx, fy, fc) for fx in (0, 1) for fy in (0, 1) for fc in (0, 1)][1:]


def _zero_token():
    return jnp.zeros((8, 128), F32)


def _exchange_start(srcs, lands, route, flips, after, name):
    n = len(srcs)

    def body(*refs):
        src, land = refs[:n], refs[n:2 * n]
        send_sems, recv_sems, token = refs[2 * n + 1], refs[2 * n + 2], refs[-1]
        me = _place()
        for t in range(n):
            for j, flip in enumerate(flips):
                peer = tuple(_flip(v, f) for v, f in zip(me, flip))
                s, d = route(t, src[t], land[t], me, peer)
                _send(s, d, send_sems.at[t * len(flips) + j], recv_sems.at[t * len(flips) + j], peer)
        token[...] = jnp.zeros_like(token)

    hbm = lambda a: pltpu.HBM(a.shape, a.dtype)
    sems = pltpu.SemaphoreType.DMA((n * len(flips),))
    operands = [pltpu.with_memory_space_constraint(a, pltpu.HBM) for a in list(srcs) + list(lands)]
    out = pl.pallas_call(
        body, name=name, in_specs=[_HBM_ONLY] * (2 * n) + [_HBM],
        out_shape=(sems, sems, *[hbm(a) for a in operands], jax.ShapeDtypeStruct((8, 128), F32)),
        out_specs=(_SEMS, _SEMS, *[_HBM_ONLY] * (2 * n), pl.BlockSpec(memory_space=pltpu.VMEM)),
        input_output_aliases={i: 2 + i for i in range(2 * n)}, compiler_params=_SIDE_EFFECT,
    )(*operands, _zero_token() if after is None else after)
    return (out[0], out[1], out[2:2 + n], out[2 + n:2 + 2 * n]), out[-1]


def _exchange_wait(started, route, flips, after, name):
    send_sems, recv_sems, srcs, lands = started
    n = len(srcs)

    def body(*refs):
        src, land = refs[:n], refs[n:2 * n]
        send_sems, recv_sems = refs[2 * n], refs[2 * n + 1]
        me = _place()
        for t in range(n):
            for j, flip in enumerate(flips):
                peer = tuple(_flip(v, f) for v, f in zip(me, flip))
                s, d = route(t, src[t], land[t], me, peer)
                cp = _whole(s, d, send_sems.at[t * len(flips) + j], recv_sems.at[t * len(flips) + j], peer)
                cp.wait_send()
                cp.wait_recv()

    hbm = lambda a: pltpu.HBM(a.shape, a.dtype)
    after = list(after) if isinstance(after, (list, tuple)) else [after]
    out = pl.pallas_call(
        body, name=name, in_specs=[_HBM_ONLY] * (2 * n) + [_SEMS, _SEMS] + [_HBM] * len(after),
        out_shape=tuple(hbm(a) for a in list(srcs) + list(lands)), out_specs=tuple([_HBM_ONLY] * (2 * n)),
        input_output_aliases={i: i for i in range(2 * n)}, compiler_params=_SIDE_EFFECT,
    )(*srcs, *lands, send_sems, recv_sems, *after)
    return out[:n], out[n:]


def _gather_route(t, src, land, me, peer):
    mine = 2 * me[0] + me[1]
    if land.ndim == 3:
        return src, land.at[mine]
    cols = src.shape[1]
    return src, land.at[:, pl.ds(pl.multiple_of(mine * cols, 128), cols)]


def _scatter_route(n_pieces):
    def route(t, src, land, me, peer):
        chip = 2 * peer[0] + peer[1]
        if t >= n_pieces:
            part = src
        elif src.ndim == 4:
            part = src.at[chip, peer[2]]
        else:
            rows, cols = land.shape[1:]
            part = src.at[pl.ds(pl.multiple_of(peer[2] * rows, 16), rows), pl.ds(pl.multiple_of(chip * cols, 128), cols)]
        return part, land.at[4 * me[0] + 2 * me[1] + me[2]]

    return route


def _swap_cores(halves):
    n = len(halves)

    def body(*refs):
        src, dst = refs[:n], refs[n:2 * n]
        send_sems, recv_sems = refs[2 * n:]
        x, y, c = _place()
        copies = [_send(src[t], dst[t], send_sems.at[t], recv_sems.at[t], (x, y, 1 - c)) for t in range(n)]
        for cp in copies:
            cp.wait()

    got = pl.pallas_call(
        body, name="swap_cores", in_specs=[_HBM] * n, out_specs=[_HBM] * n,
        out_shape=[jax.ShapeDtypeStruct(a.shape, a.dtype) for a in halves],
        scratch_shapes=[pltpu.SemaphoreType.DMA((n,)), pltpu.SemaphoreType.DMA((n,))],
    )(*halves)
    south = lax.axis_index("c") == 0
    return [jnp.stack([jnp.where(south, a, b), jnp.where(south, b, a)]) for a, b in zip(halves, got)]


def _row_tile(rows, cap):
    fits = [t for t in range(16, cap + 1, 16) if rows % t == 0]
    return fits[-1] if fits else rows


def _sum_slots(a, name):
    _, r, c = a.shape
    tr = _row_tile(r, 384)

    def body(a_ref, o_ref):
        s = a_ref[0].astype(F32)
        for k in range(1, N_DEV):
            s = s + a_ref[k].astype(F32)
        o_ref[...] = s

    return pl.pallas_call(
        body, name=name, grid=(r // tr,),
        in_specs=[pl.BlockSpec((N_DEV, tr, c), lambda i: (0, i, 0))],
        out_specs=pl.BlockSpec((tr, c), lambda i: (i, 0)),
        out_shape=jax.ShapeDtypeStruct((r, c), F32),
        compiler_params=_cp(1))(a)


def _adamw(w, g, m, v, name):
    layers, r, c = w.shape
    tr = _row_tile(r, 256)

    def body(w_ref, g_ref, m_ref, v_ref, d_ref, nm_ref, nv_ref):
        gv = g_ref[...]
        nm = ADAM_B1 * m_ref[...] + (1.0 - ADAM_B1) * gv
        nv = ADAM_B2 * v_ref[...] + (1.0 - ADAM_B2) * (gv * gv)
        m_hat = nm / (1.0 - ADAM_B1 ** ADAM_STEP)
        v_hat = nv / (1.0 - ADAM_B2 ** ADAM_STEP)
        d_ref[...] = -ADAM_LR * (m_hat / (jnp.sqrt(v_hat) + ADAM_EPS) + ADAM_WD * w_ref[...])
        nm_ref[...] = nm
        nv_ref[...] = nv

    spec = pl.BlockSpec((None, tr, c), lambda a, i: (a, i, 0))
    return pl.pallas_call(
        body, name=name, grid=(layers, r // tr), in_specs=[spec] * 4, out_specs=[spec] * 3,
        out_shape=[jax.ShapeDtypeStruct((layers, r, c), F32)] * 3,
        compiler_params=_cp(2))(w, g, m, v)


_SMALL = ["meta_tokens", "ret_head_norm", "gla_w_gate", "gla_b_gate", "gla_head_norm"]
_LOCAL_SMALL = ["meta_tokens", "norm_ffn1", "norm_mix", "norm_ffn2", "ret_head_norm", "gla_w_gate", "gla_b_gate",
                "gla_head_norm", "final_norm"]
_BIG = ["ffn1_w_in", "ffn1_w_out", "ffn2_w_in", "ffn2_w_out", "ret_w_in", "ret_w_out", "gla_w_in", "gla_w_out"]
_WEIGHTS = ["meta_tokens", "norm_ffn1", "ffn1_w_in", "ffn1_w_out", "norm_mix", "norm_ffn2", "ffn2_w_in", "ffn2_w_out",
            "ret_w_in", "ret_head_norm", "ret_w_out", "gla_w_in", "gla_w_gate", "gla_b_gate", "gla_head_norm",
            "gla_w_out", "final_norm"]


def _pack_rows(arrays, width):
    flat = jnp.concatenate([a.reshape(-1) for a in arrays])
    pad = -flat.shape[0] % (8 * width)
    return jnp.pad(flat, (0, pad)).reshape(-1, width)


def _unpack_rows(packed, shapes):
    flat, out, at = packed.reshape(-1), [], 0
    for s in shapes:
        size = 1
        for dim in s:
            size *= dim
        out.append(flat[at:at + size].reshape(s))
        at += size
    return out


class _WeightGather:
    GROUPS = [("small", "l0_ffn1_in"), ("l0_ffn1_out",), ("ret_in", "ret_out"), ("l0_ffn2_in", "l0_ffn2_out"),
              ("l1_ffn1_in", "l1_ffn1_out"), ("gla_in", "gla_out"), ("l1_ffn2_in", "l1_ffn2_out")]

    def __init__(self, p):
        self.small_shapes = [p[name].shape for name in _SMALL]
        self.f32 = {"small": _pack_rows([p[name] for name in _SMALL], 128), "ret_in": p["ret_w_in"][0],
                    "ret_out": p["ret_w_out"][0], "gla_in": p["gla_w_in"][0], "gla_out": p["gla_w_out"][0]}
        for layer in range(2):
            for name in ("ffn1", "ffn2"):
                self.f32[f"l{layer}_{name}_in"] = p[f"{name}_w_in"][layer]
                self.f32[f"l{layer}_{name}_out"] = p[f"{name}_w_out"][layer]
        self.shards = {}
        self.started = {}
        self.pin = None

    def shard(self, name):
        if name not in self.shards:
            a = self.f32[name]
            if name != "small":
                a = (a if self.pin is None else a + self.pin[0, 0]).astype(BF16)
            self.shards[name] = a
        return self.shards[name]

    def later_shards(self, k):
        return [self.shard(name) for group in self.GROUPS[k:] for name in group]

    def start(self, k, after):
        shards = [self.shard(name) for name in self.GROUPS[k]]
        lands = []
        for name, s in zip(self.GROUPS[k], shards):
            if "ffn" in name and name.endswith("_in"):
                lands.append(lax.empty((s.shape[0], N_CHIPS * s.shape[1]), s.dtype))
            else:
                lands.append(lax.empty((N_CHIPS,) + s.shape, s.dtype))
        self.started[k], token = _exchange_start(shards, lands, _gather_route, _GATHER_FLIPS, after, f"gather{k}_start")
        return token

    def wait(self, k, after):
        _, got = _exchange_wait(self.started[k], _gather_route, _GATHER_FLIPS, after, f"gather{k}_wait")
        w = {}
        for name, g in zip(self.GROUPS[k], got):
            if name == "small":
                parts = zip(*[_unpack_rows(g[chip], self.small_shapes) for chip in range(N_CHIPS)])
                cat = lambda a: jnp.moveaxis(a, 0, -2).reshape(a.shape[1:-1] + (-1,))
                meta, ret_gain, wg, bg, gla_gain = [cat(jnp.stack(part)) for part in parts]
                w.update(meta=meta, ret_gain=ret_gain.reshape(1, -1), gla_bg=bg.reshape(1, -1),
                         gla_gain=gla_gain.reshape(1, -1),
                         gla_wg=jnp.pad(wg[0], ((0, 128 - GLA_RANK), (0, 0))).astype(BF16))
            elif name == "gla_in":
                full = jnp.moveaxis(g, 0, 1).reshape(D, -1)
                w[name] = jnp.pad(full, ((0, 0), (0, GLA_U - GLA_IN)))[None]
            elif name.endswith("_out"):
                w[name] = g.reshape(-1, g.shape[-1])
            else:
                w[name] = g
        return w


class _GradExchange:
    def __init__(self):
        self.started = []
        self.token = None
        self.small_shapes = None

    def push(self, k, arrays, small=None):
        srcs, lands = [], []
        for a in arrays:
            if isinstance(a, tuple):
                a = a[1]
                piece = (a.shape[0] // 2, a.shape[1] // N_CHIPS)
            else:
                a = a.reshape(N_CHIPS, 2, -1, a.shape[-1])
                piece = a.shape[2:]
            srcs.append(a)
            lands.append(lax.empty((N_DEV,) + piece, a.dtype))
        if small is not None:
            self.small_shapes = [a.shape for a in small]
            srcs.append(_pack_rows(small, D))
            lands.append(lax.empty((N_DEV,) + srcs[-1].shape, F32))
        started, self.token = _exchange_start(srcs, lands, _scatter_route(len(arrays)), _PEER_FLIPS, None,
                                              f"scatter{k}_start")
        self.started.append((started, len(arrays)))
        return self.token

    def collect(self):
        x, y, c = _place()
        after, sums = self.token, []
        for k, (started, n_pieces) in enumerate(self.started):
            srcs, got = _exchange_wait(started, _scatter_route(n_pieces), _PEER_FLIPS, after, f"scatter{k}_wait")
            own = []
            for t, (a, g) in enumerate(zip(srcs, got)):
                if t >= n_pieces:
                    own.append(a)
                elif a.ndim == 4:
                    own.append(a[2 * x + y, c])
                else:
                    rows, cols = g.shape[1:]
                    own.append(lax.dynamic_slice(a, (c * rows, (2 * x + y) * cols), (rows, cols)))
            got = [lax.dynamic_update_index_in_dim(g, a, 4 * x + 2 * y + c, 0) for g, a in zip(got, own)]
            sums.append([_sum_slots(a, f"sum{k}_{i}") for i, a in enumerate(got)])
            after = sums[-1][0]
        small = _unpack_rows(sums[-1].pop(), self.small_shapes)
        return sums, small


def kernel(x, meta_tokens, norm_ffn1, ffn1_w_in, ffn1_w_out, norm_mix, norm_ffn2, ffn2_w_in, ffn2_w_out, ret_w_in, ret_head_norm, ret_w_out, gla_w_in, gla_w_gate, gla_b_gate, gla_head_norm, gla_w_out, final_norm, loss_target, m_meta_tokens, m_norm_ffn1, m_ffn1_w_in, m_ffn1_w_out, m_norm_mix, m_norm_ffn2, m_ffn2_w_in, m_ffn2_w_out, m_ret_w_in, m_ret_head_norm, m_ret_w_out, m_gla_w_in, m_gla_w_gate, m_gla_b_gate, m_gla_head_norm, m_gla_w_out, m_final_norm, v_meta_tokens, v_norm_ffn1, v_ffn1_w_in, v_ffn1_w_out, v_norm_mix, v_norm_ffn2, v_ffn2_w_in, v_ffn2_w_out, v_ret_w_in, v_ret_head_norm, v_ret_w_out, v_gla_w_in, v_gla_w_gate, v_gla_b_gate, v_gla_head_norm, v_gla_w_out, v_final_norm):
    p = dict(meta_tokens=meta_tokens, norm_ffn1=norm_ffn1, ffn1_w_in=ffn1_w_in, ffn1_w_out=ffn1_w_out, norm_mix=norm_mix,
             norm_ffn2=norm_ffn2, ffn2_w_in=ffn2_w_in, ffn2_w_out=ffn2_w_out, ret_w_in=ret_w_in,
             ret_head_norm=ret_head_norm, ret_w_out=ret_w_out, gla_w_in=gla_w_in, gla_w_gate=gla_w_gate,
             gla_b_gate=gla_b_gate, gla_head_norm=gla_head_norm, gla_w_out=gla_w_out, final_norm=final_norm)
    m = dict(zip(_WEIGHTS, (m_meta_tokens, m_norm_ffn1, m_ffn1_w_in, m_ffn1_w_out, m_norm_mix, m_norm_ffn2, m_ffn2_w_in,
                            m_ffn2_w_out, m_ret_w_in, m_ret_head_norm, m_ret_w_out, m_gla_w_in, m_gla_w_gate,
                            m_gla_b_gate, m_gla_head_norm, m_gla_w_out, m_final_norm)))
    v = dict(zip(_WEIGHTS, (v_meta_tokens, v_norm_ffn1, v_ffn1_w_in, v_ffn1_w_out, v_norm_mix, v_norm_ffn2, v_ffn2_w_in,
                            v_ffn2_w_out, v_ret_w_in, v_ret_head_norm, v_ret_w_out, v_gla_w_in, v_gla_w_gate,
                            v_gla_b_gate, v_gla_head_norm, v_gla_w_out, v_final_norm)))

    exchange = _GradExchange()
    d_x = _sequence_grads(x[0], loss_target[0], p, _WeightGather(p), exchange)
    sums, small = exchange.collect()
    names = [("ffn2_in", 1), ("ffn2_out", 1), ("gla_in", 0), ("gla_out", 0), ("ffn1_in", 1), ("ffn1_out", 1),
             ("ffn2_in", 0), ("ffn2_out", 0), ("ret_in", 0), ("ret_out", 0), ("ffn1_in", 0), ("ffn1_out", 0)]
    swapped = _swap_cores([a for group in sums for a in group])
    shard = {key: a.reshape(-1, a.shape[-1]) for key, a in zip(names, swapped)}
    big = {name: [shard[name, layer] for layer in range(2) if (name, layer) in shard] for name, _ in names}

    chip = 2 * lax.axis_index("x") + lax.axis_index("y")
    cols = lambda a, n: lax.dynamic_slice_in_dim(a, chip * n, n, axis=a.ndim - 1)
    (s_meta, s_n1a, s_n1b, s_nma, s_nmb, s_n2a, s_n2b, s_final, s_ret_gain, s_wg, s_bg, s_gla_gain, s_loss) = small
    grads = {
        "meta_tokens": cols(s_meta, 256), "norm_ffn1": jnp.concatenate([s_n1a, s_n1b]),
        "norm_mix": jnp.concatenate([s_nma, s_nmb]), "norm_ffn2": jnp.concatenate([s_n2a, s_n2b]),
        "final_norm": s_final.reshape(D),
        "ret_head_norm": cols(s_ret_gain.reshape(1, HEADS, RET_DV), RET_DV // N_CHIPS),
        "gla_w_gate": cols(s_wg, GLA_DK)[None], "gla_b_gate": cols(s_bg, GLA_DK),
        "gla_head_norm": cols(s_gla_gain.reshape(1, HEADS, GLA_DV), GLA_DV // N_CHIPS),
        "ffn1_w_in": jnp.stack(big["ffn1_in"]), "ffn1_w_out": jnp.stack(big["ffn1_out"]),
        "ffn2_w_in": jnp.stack(big["ffn2_in"]), "ffn2_w_out": jnp.stack(big["ffn2_out"]),
        "ret_w_in": big["ret_in"][0][None], "ret_w_out": big["ret_out"][0][None],
        "gla_w_in": big["gla_in"][0][None], "gla_w_out": big["gla_out"][0][None],
    }

    delta, new_m, new_v = {}, {}, {}
    for name in _BIG:
        delta[name], new_m[name], new_v[name] = _adamw(p[name], grads[name], m[name], v[name], f"adamw_{name}")
    packed = [_pack_rows([d[name] for name in _LOCAL_SMALL], 128)[None] for d in (p, grads, m, v)]
    out = _adamw(*packed, "adamw_small")
    shapes = [p[name].shape for name in _LOCAL_SMALL]
    for d, a in zip((delta, new_m, new_v), out):
        d.update(zip(_LOCAL_SMALL, _unpack_rows(a, shapes)))

    return (s_loss.reshape(()), d_x[None], *[grads[n] for n in _WEIGHTS], *[delta[n] for n in _WEIGHTS],
            *[new_m[n] for n in _WEIGHTS], *[new_v[n] for n in _WEIGHTS])
```

```python
import functools

import jax
import jax.numpy as jnp
from jax import lax
from jax.experimental import pallas as pl
from jax.experimental.pallas import tpu as pltpu

F32, BF16 = jnp.float32, jnp.bfloat16
MESH = pl.DeviceIdType.MESH

D = 1024
N_META = 16
CHUNK = 64
RET_CHUNK = 128
FRONT = 256
D_FF = 2816
EPS = 1e-6
HEADS = 4
RET_DK, RET_DV = 256, 512
GLA_DK, GLA_DV = 128, 256
GLA_RANK = 16
GLA_TAU = 16.0
GLA_IN = 2 * HEADS * GLA_DK + 2 * HEADS * GLA_DV + GLA_RANK
GLA_U = 3200
ROPE_BASE = 10000.0
N_CHIPS = 4
N_DEV = 8

ADAM_LR, ADAM_B1, ADAM_B2, ADAM_EPS, ADAM_WD, ADAM_STEP = 0.001, 0.9, 0.999, 1e-08, 0.01, 10

VMEM_LIMIT_BYTES = 56 * 1024 * 1024
TM = 768
TM_SMALL = 256


TM_RESIDENT = 384
MXU_TILE = 256


def _cp(n_axes):
    return pltpu.CompilerParams(dimension_semantics=("arbitrary",) * n_axes, vmem_limit_bytes=VMEM_LIMIT_BYTES)


def _resident(shape, n_axes):
    zeros = (0,) * len(shape)
    index = (lambda i: zeros) if n_axes == 1 else (lambda i, j: zeros)
    return pl.BlockSpec(shape, index, pipeline_mode=pl.Buffered(1))


def _dg(a, b, ca, cb):
    nb = a.ndim - 2
    dims = (((ca + nb,), (cb + nb,)), (tuple(range(nb)), tuple(range(nb))))
    return lax.dot_general(a.astype(BF16), b.astype(BF16), dims, preferred_element_type=F32)


@jax.custom_vjp
def _nn(a, b):
    return _dg(a, b, 1, 0)


@jax.custom_vjp
def _nt(a, b):
    return _dg(a, b, 1, 1)


@jax.custom_vjp
def _tn(a, b):
    return _dg(a, b, 0, 0)


def _dot_vjp(fn, ca, cb, da, db):
    def fwd(a, b):
        a, b = a.astype(BF16), b.astype(BF16)
        return _dg(a, b, ca, cb), (a, b)

    def bwd(res, g):
        a, b = res
        g = g.astype(BF16)
        grad = lambda other, dims, g_first: _dg(g, other, *dims) if g_first else _dg(other, g, *dims)
        return grad(b, *da), grad(a, *db)

    fn.defvjp(fwd, bwd)


_dot_vjp(_nn, 1, 0, ((1, 1), True), ((0, 0), False))
_dot_vjp(_nt, 1, 1, ((1, 0), True), ((0, 0), True))
_dot_vjp(_tn, 0, 0, ((1, 1), False), ((1, 0), False))


def _split3_dot(m, a):
    a1 = a.astype(BF16)
    r1 = a - a1.astype(F32)
    a2 = r1.astype(BF16)
    a3 = (r1 - a2.astype(F32)).astype(BF16)
    mb = jnp.broadcast_to(m, a.shape[:-2] + m.shape)
    return _dg(mb, a1, 1, 0) + _dg(mb, a2, 1, 0) + _dg(mb, a3, 1, 0)


@jax.custom_vjp
def _cum(m, mt, a):
    return _split3_dot(m, a)


_cum.defvjp(lambda m, mt, a: (_split3_dot(m, a), (m, mt)),
            lambda res, g: (jnp.zeros_like(res[0]), jnp.zeros_like(res[1]), _split3_dot(res[1], g)))


def _sigmoid(x):
    return 1.0 / (1.0 + jnp.exp(-x))


def _rms(x):
    return lax.rsqrt(jnp.mean(x * x, axis=-1, keepdims=True) + EPS)


def _rmsnorm_bwd(dy, x, gain):
    r = _rms(x)
    xhat = x * r
    dxh = dy * gain
    return r * (dxh - xhat * jnp.mean(dxh * xhat, axis=-1, keepdims=True)), xhat


def _norm_proj(h, gain, w, name):
    tp, d = h.shape
    s, _, ns = w.shape

    tm = TM_RESIDENT

    def body(h_ref, g_ref, w_ref, hn_ref, u_ref):
        @pl.when(pl.program_id(1) == 0)
        def _():
            x = h_ref[...]
            hn_ref[...] = (x * _rms(x) * g_ref[...]).astype(BF16)

        u_ref[...] = jnp.dot(hn_ref[...], w_ref[pl.program_id(1)], preferred_element_type=F32).astype(BF16)

    return pl.pallas_call(
        body, name=name, grid=(tp // tm, s),
        in_specs=[pl.BlockSpec((tm, d), lambda i, j: (i, 0)), pl.BlockSpec((1, d), lambda i, j: (0, 0)),
                  _resident(w.shape, 2)],
        out_specs=[pl.BlockSpec((tm, d), lambda i, j: (i, 0)), pl.BlockSpec((tm, ns), lambda i, j: (i, j))],
        out_shape=[jax.ShapeDtypeStruct((tp, d), BF16), jax.ShapeDtypeStruct((tp, s * ns), BF16)],
        compiler_params=_cp(2))(h, gain, w)


def _norm_ffn_in(h, gain, w, name):
    tp, d = h.shape
    ff = w.shape[1] // 2
    tm = TM_RESIDENT
    blocks = [(c, min(c + 6 * MXU_TILE, ff)) for c in range(0, ff, 6 * MXU_TILE)]

    def body(h_ref, g_ref, w_ref, hn_ref, dg_ref, du_ref, act_ref):
        x = h_ref[...]
        a = (x * _rms(x) * g_ref[...]).astype(BF16)
        hn_ref[...] = a
        for c0, c1 in blocks:
            g = jnp.dot(a, w_ref[:, c0:c1], preferred_element_type=F32)
            u = jnp.dot(a, w_ref[:, ff + c0:ff + c1], preferred_element_type=F32)
            sg = _sigmoid(g)
            silu = g * sg
            dg_ref[:, c0:c1] = (u * (sg + silu * (1.0 - sg))).astype(BF16)
            du_ref[:, c0:c1] = silu.astype(BF16)
            act_ref[:, c0:c1] = (silu * u).astype(BF16)

    wide = jax.ShapeDtypeStruct((tp, ff), BF16)
    return pl.pallas_call(
        body, name=name, grid=(tp // tm,),
        in_specs=[pl.BlockSpec((tm, d), lambda i: (i, 0)), pl.BlockSpec((1, d), lambda i: (0, 0)),
                  _resident(w.shape, 1)],
        out_specs=[pl.BlockSpec((tm, d), lambda i: (i, 0))] + [pl.BlockSpec((tm, ff), lambda i: (i, 0))] * 3,
        out_shape=[jax.ShapeDtypeStruct((tp, d), BF16), wide, wide, wide],
        compiler_params=_cp(1))(h, gain, w)


def _out_proj(a, w, h, scale, name):
    tp, k = a.shape
    d = w.shape[1]

    def body(a_ref, w_ref, h_ref, o_ref):
        o_ref[...] = h_ref[...] + scale * jnp.dot(a_ref[...], w_ref[...], preferred_element_type=F32)

    return pl.pallas_call(
        body, name=name, grid=(tp // TM,),
        in_specs=[pl.BlockSpec((TM, k), lambda i: (i, 0)), pl.BlockSpec((k, d), lambda i: (0, 0)),
                  pl.BlockSpec((TM, d), lambda i: (i, 0))],
        out_specs=pl.BlockSpec((TM, d), lambda i: (i, 0)),
        out_shape=jax.ShapeDtypeStruct((tp, d), F32),
        compiler_params=_cp(1))(a, w, h)


def _dgrad(dh, w, name):
    tp, d = dh.shape
    k = w.shape[0]

    def body(dh_ref, w_ref, o_ref):
        o_ref[...] = lax.dot_general(dh_ref[...].astype(BF16), w_ref[...], (((1,), (1,)), ((), ())),
                                     preferred_element_type=F32).astype(BF16)

    return pl.pallas_call(
        body, name=name, grid=(tp // TM,),
        in_specs=[pl.BlockSpec((TM, d), lambda i: (i, 0)), pl.BlockSpec((k, d), lambda i: (0, 0))],
        out_specs=pl.BlockSpec((TM, k), lambda i: (i, 0)),
        out_shape=jax.ShapeDtypeStruct((tp, k), BF16),
        compiler_params=_cp(1))(dh, w)


def _wgrad(a, b, *, bm, bn, scale, sharded, name):
    tp, m = a.shape
    n = b.shape[1]
    nk = tp // TM

    def body(a_ref, b_ref, o_ref, acc_ref):
        k = pl.program_id(2)

        @pl.when(k == 0)
        def _():
            acc_ref[...] = jnp.zeros_like(acc_ref)

        bb = b_ref[...]
        if scale != 1.0:
            bb = scale * bb
        acc_ref[...] += lax.dot_general(a_ref[...], bb.astype(BF16), (((0,), (0,)), ((), ())),
                                        preferred_element_type=F32)

        @pl.when(k == nk - 1)
        def _():
            o_ref[...] = acc_ref[...].astype(BF16)

    if sharded:
        assert m == bm
        out_spec = pl.BlockSpec((None, bm, bn), lambda i, j, k: (j, 0, 0))
        out_shape = jax.ShapeDtypeStruct((n // bn, m, bn), BF16)
    else:
        out_spec = pl.BlockSpec((bm, bn), lambda i, j, k: (i, j))
        out_shape = jax.ShapeDtypeStruct((m, n), BF16)
    return pl.pallas_call(
        body, name=name, grid=(m // bm, n // bn, nk),
        in_specs=[pl.BlockSpec((TM, bm), lambda i, j, k: (k, i)), pl.BlockSpec((TM, bn), lambda i, j, k: (k, j))],
        out_specs=out_spec, out_shape=out_shape,
        scratch_shapes=[pltpu.VMEM((bm, bn), F32)],
        compiler_params=_cp(3))(a, b)


def _dgrad_norm(du, w, h, gain, dh_out, name):
    tp, d = h.shape
    s, _, ns = w.shape
    tm = TM_RESIDENT

    def body(du_ref, w_ref, h_ref, g_ref, dho_ref, dhi_ref, dg_ref):
        @pl.when(pl.program_id(0) == 0)
        def _():
            dg_ref[...] = jnp.zeros_like(dg_ref)

        dhn = None
        for k in range(s):
            part = lax.dot_general(du_ref[:, ns * k:ns * (k + 1)], w_ref[k], (((1,), (1,)), ((), ())),
                                   preferred_element_type=F32)
            dhn = part if dhn is None else dhn + part
        dx, xhat = _rmsnorm_bwd(dhn, h_ref[...], g_ref[...])
        dg_ref[...] += jnp.sum(dhn * xhat, axis=0, keepdims=True)
        dhi_ref[...] = dho_ref[...] + dx

    return pl.pallas_call(
        body, name=name, grid=(tp // tm,),
        in_specs=[pl.BlockSpec((tm, s * ns), lambda i: (i, 0)), _resident(w.shape, 1),
                  pl.BlockSpec((tm, d), lambda i: (i, 0)), pl.BlockSpec((1, d), lambda i: (0, 0)),
                  pl.BlockSpec((tm, d), lambda i: (i, 0))],
        out_specs=[pl.BlockSpec((tm, d), lambda i: (i, 0)), pl.BlockSpec((1, d), lambda i: (0, 0))],
        out_shape=[jax.ShapeDtypeStruct((tp, d), F32), jax.ShapeDtypeStruct((1, d), F32)],
        compiler_params=_cp(1))(du, w, h, gain, dh_out)


def _loss_head(h, gain, target, name):
    tp, d = h.shape
    tm = TM_SMALL
    front_tiles = FRONT // tm

    def body(h_ref, g_ref, t_ref, dh_ref, dg_ref, loss_ref):
        i = pl.program_id(0)

        @pl.when(i == 0)
        def _():
            dg_ref[...] = jnp.zeros_like(dg_ref)
            loss_ref[...] = jnp.zeros_like(loss_ref)

        x = h_ref[...]
        gain_v = g_ref[...]
        y = x * _rms(x) * gain_v
        err = jnp.where(i >= front_tiles, y - t_ref[...], 0.0)
        loss_ref[...] += 0.5 * jnp.sum(jnp.mean(err * err, axis=-1, keepdims=True), axis=0, keepdims=True)
        dy = err * (1.0 / d)
        dx, xhat = _rmsnorm_bwd(dy, x, gain_v)
        dg_ref[...] += jnp.sum(dy * xhat, axis=0, keepdims=True)
        dh_ref[...] = dx

    return pl.pallas_call(
        body, name=name, grid=(tp // tm,),
        in_specs=[pl.BlockSpec((tm, d), lambda i: (i, 0)), pl.BlockSpec((1, d), lambda i: (0, 0)),
                  pl.BlockSpec((tm, d), lambda i: (jnp.maximum(i - front_tiles, 0), 0))],
        out_specs=[pl.BlockSpec((tm, d), lambda i: (i, 0)), pl.BlockSpec((1, d), lambda i: (0, 0)),
                   pl.BlockSpec((1, 128), lambda i: (0, 0))],
        out_shape=[jax.ShapeDtypeStruct((tp, d), F32), jax.ShapeDtypeStruct((1, d), F32),
                   jax.ShapeDtypeStruct((1, 128), F32)],
        compiler_params=_cp(1))(h, gain, target)


def _gated_headnorm(o, g, gain):
    return o * _rms(o) * gain * (g * _sigmoid(g))


def _row_mask(chunk, size=CHUNK):
    rows = chunk * size + lax.broadcasted_iota(jnp.int32, (size, 1), 0)
    return (rows >= FRONT - N_META).astype(F32)


def _ret_head(q1, q2, k1, k2, v, g, state, gain, cos, sin, dmat, dq, dk, dc):
    q = jnp.concatenate([q1 * cos - q2 * sin, q1 * sin + q2 * cos], axis=-1)
    k = jnp.concatenate([k1 * cos - k2 * sin, k1 * sin + k2 * cos], axis=-1) * (RET_DK ** -0.5)
    scores = _nt(q, k) * dmat
    o = _nn(scores, v) + _nn(q * dq, state)
    new_state = state * dc + _tn(k * dk, v)
    return _gated_headnorm(o, g, gain), new_state


def _ret_consts():
    log_gamma = jnp.log1p(-2.0 ** (-5.0 - jnp.arange(HEADS, dtype=F32)))
    idx = jnp.arange(RET_CHUNK, dtype=F32)
    rel = idx[:, None] - idx[None, :]
    dmat = jnp.where(rel >= 0, jnp.exp(log_gamma[:, None, None] * jnp.maximum(rel, 0.0)), 0.0)
    dq = jnp.exp(log_gamma[:, None] * (idx + 1.0))[..., None]
    dk = jnp.exp(log_gamma[:, None] * (RET_CHUNK - 1.0 - idx))[..., None]
    dc = jnp.broadcast_to(jnp.exp(log_gamma * RET_CHUNK)[:, None, None], (HEADS, 1, 128))
    return dmat, dq, dk, dc


def _rope_tables(tp):
    half = RET_DK // 2
    inv = 1.0 / (ROPE_BASE ** jnp.linspace(0.0, 1.0, half, dtype=F32))
    pos = (jnp.arange(tp) - (FRONT - N_META)).astype(F32)
    ang = pos[:, None] * inv[None, :]
    return jnp.cos(ang), jnp.sin(ang)


_RET_V0, _RET_G0 = 2 * D, 4 * D


def _heads(ref, start, width, stride=None):
    stride = width if stride is None else stride
    return jnp.stack([ref[:, start + stride * h:start + stride * h + width].astype(F32) for h in range(HEADS)])


def _put_heads(ref, start, value, mask, stride=None):
    width = value.shape[-1]
    stride = width if stride is None else stride
    for h in range(HEADS):
        ref[:, start + stride * h:start + stride * h + width] = (value[h] * mask).astype(ref.dtype)


def _ret_pieces(u_ref):
    hk = RET_DK // 2
    return (_heads(u_ref, 0, hk, RET_DK), _heads(u_ref, hk, hk, RET_DK), _heads(u_ref, D, hk, RET_DK),
            _heads(u_ref, D + hk, hk, RET_DK), _heads(u_ref, _RET_V0, RET_DV), _heads(u_ref, _RET_G0, RET_DV))


def _ret_const_specs(rev=None):
    c = (lambda n: (rev(n), 0)) if rev else (lambda n: (n, 0))
    z3 = lambda n: (0, 0, 0)
    return [pl.BlockSpec((RET_CHUNK, RET_DK // 2), c), pl.BlockSpec((RET_CHUNK, RET_DK // 2), c),
            pl.BlockSpec((HEADS, RET_CHUNK, RET_CHUNK), z3), pl.BlockSpec((HEADS, RET_CHUNK, 1), z3),
            pl.BlockSpec((HEADS, RET_CHUNK, 1), z3), pl.BlockSpec((HEADS, 1, 128), z3)]


def _ret_fwd(u, gain, rope, name):
    tp = u.shape[0]
    nch = tp // RET_CHUNK
    cos, sin = rope
    dmat, dq, dk, dc = _ret_consts()

    def body(u_ref, gain_ref, cos_ref, sin_ref, dmat_ref, dq_ref, dk_ref, dc_ref, on_ref, st_ref, state_ref):
        @pl.when(pl.program_id(0) == 0)
        def _():
            state_ref[...] = jnp.zeros_like(state_ref)

        state = state_ref[...]
        st_ref[...] = state.astype(BF16)
        on, new_state = _ret_head(*_ret_pieces(u_ref), state, _heads(gain_ref, 0, RET_DV), cos_ref[...], sin_ref[...],
                                  dmat_ref[...], dq_ref[...], dk_ref[...], dc_ref[...][:, :, :1])
        state_ref[...] = new_state
        _put_heads(on_ref, 0, on, 1.0)

    return pl.pallas_call(
        body, name=name, grid=(nch,),
        in_specs=[pl.BlockSpec((RET_CHUNK, 6 * D), lambda n: (n, 0)), pl.BlockSpec((1, HEADS * RET_DV), lambda n: (0, 0))]
                 + _ret_const_specs(),
        out_specs=[pl.BlockSpec((RET_CHUNK, HEADS * RET_DV), lambda n: (n, 0)),
                   pl.BlockSpec((None, HEADS, RET_DK, RET_DV), lambda n: (n, 0, 0, 0))],
        out_shape=[jax.ShapeDtypeStruct((tp, HEADS * RET_DV), BF16),
                   jax.ShapeDtypeStruct((nch, HEADS, RET_DK, RET_DV), BF16)],
        scratch_shapes=[pltpu.VMEM((HEADS, RET_DK, RET_DV), F32)],
        compiler_params=_cp(1))(u, gain, cos, sin, dmat, dq, dk, dc)


def _ret_bwd(u, gain, rope, states, d_on, name):
    tp = u.shape[0]
    nch = tp // RET_CHUNK
    cos, sin = rope
    dmat, dq, dk, dc = _ret_consts()
    rev = lambda n: nch - 1 - n
    hk = RET_DK // 2

    def body(u_ref, gain_ref, st_ref, don_ref, cos_ref, sin_ref, dmat_ref, dq_ref, dk_ref, dc_ref,
             du_ref, dgain_ref, dstate_ref):
        @pl.when(pl.program_id(0) == 0)
        def _():
            dstate_ref[...] = jnp.zeros_like(dstate_ref)
            dgain_ref[...] = jnp.zeros_like(dgain_ref)

        mask = _row_mask(rev(pl.program_id(0)), RET_CHUNK)
        consts = (cos_ref[...], sin_ref[...], dmat_ref[...], dq_ref[...], dk_ref[...], dc_ref[...][:, :, :1])
        _, vjp = jax.vjp(lambda *a: _ret_head(*a, *consts), *_ret_pieces(u_ref), st_ref[...].astype(F32),
                         _heads(gain_ref, 0, RET_DV))
        dq1, dq2, dk1, dk2, dv, dg, dstate, dgain = vjp((_heads(don_ref, 0, RET_DV), dstate_ref[...]))
        dstate_ref[...] = dstate
        for hd in range(HEADS):
            dgain_ref[:, RET_DV * hd:RET_DV * (hd + 1)] += dgain[hd]
        _put_heads(du_ref, 0, dq1, mask, RET_DK)
        _put_heads(du_ref, hk, dq2, mask, RET_DK)
        _put_heads(du_ref, D, dk1, mask, RET_DK)
        _put_heads(du_ref, D + hk, dk2, mask, RET_DK)
        _put_heads(du_ref, _RET_V0, dv, mask)
        _put_heads(du_ref, _RET_G0, dg, mask)

    return pl.pallas_call(
        body, name=name, grid=(nch,),
        in_specs=[pl.BlockSpec((RET_CHUNK, 6 * D), lambda n: (rev(n), 0)),
                  pl.BlockSpec((1, HEADS * RET_DV), lambda n: (0, 0)),
                  pl.BlockSpec((None, HEADS, RET_DK, RET_DV), lambda n: (rev(n), 0, 0, 0)),
                  pl.BlockSpec((RET_CHUNK, HEADS * RET_DV), lambda n: (rev(n), 0))] + _ret_const_specs(rev),
        out_specs=[pl.BlockSpec((RET_CHUNK, 6 * D), lambda n: (rev(n), 0)),
                   pl.BlockSpec((1, HEADS * RET_DV), lambda n: (0, 0))],
        out_shape=[jax.ShapeDtypeStruct((tp, 6 * D), BF16), jax.ShapeDtypeStruct((1, HEADS * RET_DV), F32)],
        scratch_shapes=[pltpu.VMEM((HEADS, RET_DK, RET_DV), F32)],
        compiler_params=_cp(1))(u, gain, states, d_on, cos, sin, dmat, dq, dk, dc)


_GLA_K0, _GLA_V0, _GLA_G0, _GLA_Z0 = 512, 1024, 2048, 3072


def _gla_head(q, k, v, g, z, state_t, wg, bg, gain, mask, lo, lo_t, loc, loc_t):
    ga = _nn(jnp.broadcast_to(z, wg.shape[:-2] + z.shape), wg) + bg
    log_a = (jnp.minimum(ga, 0.0) - jnp.log(1.0 + jnp.exp(-jnp.abs(ga)))) * (mask * (1.0 / GLA_TAU))
    bcum = _cum(lo, lo_t, log_a)
    bmid = _cum(loc, loc_t, log_a)
    btot = jnp.sum(log_a, axis=-2, keepdims=True)
    qs = q * (GLA_DK ** -0.5)
    causal = lax.broadcasted_iota(jnp.int32, (CHUNK, CHUNK), 0) >= lax.broadcasted_iota(jnp.int32, (CHUNK, CHUNK), 1)
    scores = jnp.where(causal, _nt(qs * jnp.exp(bmid), k * jnp.exp(-bmid)), 0.0)
    o = _nn(scores, v) + _nt(qs * jnp.exp(bcum), state_t)
    new_state_t = state_t * jnp.exp(btot) + _tn(v, k * jnp.exp(btot - bcum))
    return _gated_headnorm(o, g, gain), new_state_t


def _cum_mats():
    r = lax.broadcasted_iota(jnp.int32, (CHUNK, CHUNK), 0)
    c = lax.broadcasted_iota(jnp.int32, (CHUNK, CHUNK), 1)
    mid = CHUNK // 2
    low = lambda a, b: (a >= b).astype(F32)
    lo, lo_t = low(r, c), low(c, r)
    loc = lo - (c <= mid).astype(F32)
    loc_t = lo_t - (r <= mid).astype(F32)
    return tuple(m.astype(BF16) for m in (lo, lo_t, loc, loc_t))


def _gla_pieces(u_ref):
    return (_heads(u_ref, 0, GLA_DK), _heads(u_ref, _GLA_K0, GLA_DK), _heads(u_ref, _GLA_V0, GLA_DV),
            _heads(u_ref, _GLA_G0, GLA_DV), u_ref[:, _GLA_Z0:].astype(F32))


def _gla_fwd(u, wg, bg, gain, name):
    tp = u.shape[0]
    nch = tp // CHUNK

    def body(u_ref, wg_ref, bg_ref, gain_ref, on_ref, st_ref, state_ref):
        @pl.when(pl.program_id(0) == 0)
        def _():
            state_ref[...] = jnp.zeros_like(state_ref)

        state = state_ref[...]
        st_ref[...] = state.astype(BF16)
        on, new_state = _gla_head(*_gla_pieces(u_ref), state, _heads(wg_ref, 0, GLA_DK), _heads(bg_ref, 0, GLA_DK),
                                  _heads(gain_ref, 0, GLA_DV), _row_mask(pl.program_id(0)), *_cum_mats())
        state_ref[...] = new_state
        _put_heads(on_ref, 0, on, 1.0)

    return pl.pallas_call(
        body, name=name, grid=(nch,),
        in_specs=[pl.BlockSpec((CHUNK, GLA_U), lambda n: (n, 0)), pl.BlockSpec((128, HEADS * GLA_DK), lambda n: (0, 0)),
                  pl.BlockSpec((1, HEADS * GLA_DK), lambda n: (0, 0)), pl.BlockSpec((1, HEADS * GLA_DV), lambda n: (0, 0))],
        out_specs=[pl.BlockSpec((CHUNK, HEADS * GLA_DV), lambda n: (n, 0)),
                   pl.BlockSpec((None, HEADS, GLA_DV, GLA_DK), lambda n: (n, 0, 0, 0))],
        out_shape=[jax.ShapeDtypeStruct((tp, HEADS * GLA_DV), BF16),
                   jax.ShapeDtypeStruct((nch, HEADS, GLA_DV, GLA_DK), BF16)],
        scratch_shapes=[pltpu.VMEM((HEADS, GLA_DV, GLA_DK), F32)],
        compiler_params=_cp(1))(u, wg, bg, gain)


def _gla_bwd(u, wg, bg, gain, states, d_on, name):
    tp = u.shape[0]
    nch = tp // CHUNK
    rev = lambda n: nch - 1 - n

    def body(u_ref, wg_ref, bg_ref, gain_ref, st_ref, don_ref, du_ref, dwg_ref, dbg_ref, dgain_ref, dstate_ref):
        @pl.when(pl.program_id(0) == 0)
        def _():
            dstate_ref[...] = jnp.zeros_like(dstate_ref)
            dwg_ref[...] = jnp.zeros_like(dwg_ref)
            dbg_ref[...] = jnp.zeros_like(dbg_ref)
            dgain_ref[...] = jnp.zeros_like(dgain_ref)

        mask = _row_mask(rev(pl.program_id(0)))
        mats = _cum_mats()
        _, vjp = jax.vjp(lambda *a: _gla_head(*a, mask, *mats), *_gla_pieces(u_ref), st_ref[...].astype(F32),
                         _heads(wg_ref, 0, GLA_DK), _heads(bg_ref, 0, GLA_DK), _heads(gain_ref, 0, GLA_DV))
        dq, dk, dv, dg, dz, dstate, dwg, dbg, dgain = vjp((_heads(don_ref, 0, GLA_DV), dstate_ref[...]))
        dstate_ref[...] = dstate
        for hd in range(HEADS):
            dwg_ref[:, GLA_DK * hd:GLA_DK * (hd + 1)] += dwg[hd]
            dbg_ref[:, GLA_DK * hd:GLA_DK * (hd + 1)] += dbg[hd]
            dgain_ref[:, GLA_DV * hd:GLA_DV * (hd + 1)] += dgain[hd]
        _put_heads(du_ref, 0, dq, mask)
        _put_heads(du_ref, _GLA_K0, dk, mask)
        _put_heads(du_ref, _GLA_V0, dv, mask)
        _put_heads(du_ref, _GLA_G0, dg, mask)
        du_ref[:, _GLA_Z0:] = dz.astype(BF16)

    full = lambda r, c: pl.BlockSpec((r, c), lambda n: (0, 0))
    return pl.pallas_call(
        body, name=name, grid=(nch,),
        in_specs=[pl.BlockSpec((CHUNK, GLA_U), lambda n: (rev(n), 0)), full(128, HEADS * GLA_DK),
                  full(1, HEADS * GLA_DK), full(1, HEADS * GLA_DV),
                  pl.BlockSpec((None, HEADS, GLA_DV, GLA_DK), lambda n: (rev(n), 0, 0, 0)),
                  pl.BlockSpec((CHUNK, HEADS * GLA_DV), lambda n: (rev(n), 0))],
        out_specs=[pl.BlockSpec((CHUNK, GLA_U), lambda n: (rev(n), 0)), full(128, HEADS * GLA_DK),
                   full(1, HEADS * GLA_DK), full(1, HEADS * GLA_DV)],
        out_shape=[jax.ShapeDtypeStruct((tp, GLA_U), BF16), jax.ShapeDtypeStruct((128, HEADS * GLA_DK), F32),
                   jax.ShapeDtypeStruct((1, HEADS * GLA_DK), F32), jax.ShapeDtypeStruct((1, HEADS * GLA_DV), F32)],
        scratch_shapes=[pltpu.VMEM((HEADS, GLA_DV, GLA_DK), F32)],
        compiler_params=_cp(1))(u, wg, bg, gain, states, d_on)


def _ffn_fwd(h, gain, w_in, w_out, tag):
    hn, ug, uu, act = _norm_ffn_in(h, gain, w_in, f"{tag}_in")
    if callable(w_out):
        w_out = w_out(act)
    return _out_proj(act, w_out, h, 0.5, f"{tag}_out"), (h, hn, ug, uu, act), w_out


def _ffn_dgrad(dh, w_out, w_in, act_dg, act_du, h, gain, name):
    tp, d = dh.shape
    ff = w_out.shape[0]
    tm = TM_SMALL
    nt = (((1,), (1,)), ((), ()))

    def body(dh_ref, wo_ref, wi_ref, dg_ref, du_ref, h_ref, g_ref, o_ref, dhi_ref, dgain_ref):
        @pl.when(pl.program_id(0) == 0)
        def _():
            dgain_ref[...] = jnp.zeros_like(dgain_ref)

        dho = dh_ref[...]
        dact = lax.dot_general((0.5 * dho).astype(BF16), wo_ref[...], nt, preferred_element_type=F32)
        d_gate = (dact * dg_ref[...].astype(F32)).astype(BF16)
        d_up = (dact * du_ref[...].astype(F32)).astype(BF16)
        o_ref[:, :ff] = d_gate
        o_ref[:, ff:] = d_up
        dhn = (lax.dot_general(d_gate, wi_ref[:, :ff], nt, preferred_element_type=F32)
               + lax.dot_general(d_up, wi_ref[:, ff:], nt, preferred_element_type=F32))
        dx, xhat = _rmsnorm_bwd(dhn, h_ref[...], g_ref[...])
        dgain_ref[...] += jnp.sum(dhn * xhat, axis=0, keepdims=True)
        dhi_ref[...] = dho + dx

    rows = lambda width: pl.BlockSpec((tm, width), lambda i: (i, 0))
    return pl.pallas_call(
        body, name=name, grid=(tp // tm,),
        in_specs=[rows(d), _resident(w_out.shape, 1), _resident(w_in.shape, 1), rows(ff), rows(ff), rows(d),
                  pl.BlockSpec((1, d), lambda i: (0, 0))],
        out_specs=[rows(2 * ff), rows(d), pl.BlockSpec((1, d), lambda i: (0, 0))],
        out_shape=[jax.ShapeDtypeStruct((tp, 2 * ff), BF16), jax.ShapeDtypeStruct((tp, d), F32),
                   jax.ShapeDtypeStruct((1, d), F32)],
        compiler_params=_cp(1))(dh, w_out, w_in, act_dg, act_du, h, gain)


def _ffn_bwd(dh, saved, gain, w_in, w_out, tag, push):
    h, hn, act_dg, act_du, act = saved
    du, dh_in, d_gain = _ffn_dgrad(dh, w_out, w_in, act_dg, act_du, h, gain, f"{tag}_dgrad")
    d_w_out = _wgrad(act, dh, bm=D_FF // 2, bn=D, scale=0.5, sharded=False, name=f"{tag}_dwout")
    d_w_in = _wgrad(hn, du, bm=D, bn=D_FF, scale=1.0, sharded=False, name=f"{tag}_dwin")
    return dh_in, d_gain, push([("cols", d_w_in), d_w_out])


def _sequence_grads(x, target, p, weights, grads):
    row = lambda v, token: v.reshape(1, -1) + token[0, 0]
    gains = {}

    tok = weights.start(1, weights.start(0, None))
    weights.pin = tok
    h = jnp.concatenate([jnp.zeros((FRONT, D), F32), x], axis=0) + tok[0, 0]
    rope = _rope_tables(h.shape[0])
    w = weights.wait(0, [tok, h, *rope, *weights.later_shards(2)])
    tok = weights.start(2, w["l0_ffn1_in"])
    h = lax.dynamic_update_slice(h, w["meta"], (FRONT - N_META, 0))
    gains["l0_ffn1"] = row(p["norm_ffn1"][0], tok)
    h, s1, w["l0_ffn1_out"] = _ffn_fwd(h, gains["l0_ffn1"], w["l0_ffn1_in"],
                                       lambda act: weights.wait(1, act)["l0_ffn1_out"], "l0_ffn1")
    w.update(weights.wait(2, h))
    tok = weights.start(3, w["ret_in"])
    gains["ret"] = row(p["norm_mix"][0], tok)
    hn, u = _norm_proj(h, gains["ret"], w["ret_in"], "ret_in")
    on, states = _ret_fwd(u, w["ret_gain"], rope, "ret_fwd")
    w.update(weights.wait(3, on))
    tok = weights.start(4, w["ret_out"])
    h_mix = _out_proj(on, w["ret_out"], h, 1.0, "ret_out")
    s2 = (h, hn, u, on, states)
    gains["l0_ffn2"] = row(p["norm_ffn2"][0], tok)
    h, s3, _ = _ffn_fwd(h_mix, gains["l0_ffn2"], w["l0_ffn2_in"], w["l0_ffn2_out"], "l0_ffn2")
    saved = [(s1, s2, s3)]

    w.update(weights.wait(4, h))
    tok = weights.start(5, w["l1_ffn1_in"])
    gains["l1_ffn1"] = row(p["norm_ffn1"][1], tok)
    h, s1, _ = _ffn_fwd(h, gains["l1_ffn1"], w["l1_ffn1_in"], w["l1_ffn1_out"], "l1_ffn1")
    w.update(weights.wait(5, h))
    tok = weights.start(6, w["gla_out"])
    gains["gla"] = row(p["norm_mix"][1], tok)
    hn, u = _norm_proj(h, gains["gla"], w["gla_in"], "gla_in")
    on, states = _gla_fwd(u, w["gla_wg"], w["gla_bg"], w["gla_gain"], "gla_fwd")
    h_mix = _out_proj(on, w["gla_out"], h, 1.0, "gla_out")
    s2 = (h, hn, u, on, states)
    w.update(weights.wait(6, h_mix))
    gains["l1_ffn2"] = p["norm_ffn2"][1].reshape(1, -1)
    h, s3, _ = _ffn_fwd(h_mix, gains["l1_ffn2"], w["l1_ffn2_in"], w["l1_ffn2_out"], "l1_ffn2")
    saved.append((s1, s2, s3))

    dh, d_final, loss = _loss_head(h, p["final_norm"].reshape(1, -1), target, "loss_head")
    small = {"final_norm": d_final, "norm_ffn1": [None, None], "norm_mix": [None, None], "norm_ffn2": [None, None]}
    pusher = lambda k: functools.partial(grads.push, k)

    s1, s2, s3 = saved[1]
    dh, small["norm_ffn2"][1], tok = _ffn_bwd(dh, s3, gains["l1_ffn2"], w["l1_ffn2_in"], w["l1_ffn2_out"], "l1_ffn2",
                                              pusher(0))
    h_in, hn, u, on, states = s2
    d_on = _dgrad(dh, w["gla_out"], "gla_don")
    d_out = _wgrad(on, dh, bm=D, bn=D, scale=1.0, sharded=False, name="gla_dwout")
    du, small["gla_wg"], small["gla_bg"], small["gla_gain"] = _gla_bwd(
        u, w["gla_wg"], w["gla_bg"], w["gla_gain"] + tok[0, 0], states, d_on, "gla_bwd")
    d_in = _wgrad(hn, du, bm=D, bn=GLA_U // 5, scale=1.0, sharded=False, name="gla_dwin")
    d_in = jnp.moveaxis(d_in[:, :GLA_IN].reshape(D, N_CHIPS, -1), 1, 0)
    tok = grads.push(1, [d_in, d_out])
    dh, small["norm_mix"][1] = _dgrad_norm(du, w["gla_in"], h_in, gains["gla"] + tok[0, 0], dh, "gla_dnorm")
    dh, small["norm_ffn1"][1], tok = _ffn_bwd(dh, s1, gains["l1_ffn1"], w["l1_ffn1_in"], w["l1_ffn1_out"], "l1_ffn1",
                                              pusher(2))

    s1, s2, s3 = saved[0]
    dh, small["norm_ffn2"][0], tok = _ffn_bwd(dh, s3, gains["l0_ffn2"] + tok[0, 0], w["l0_ffn2_in"],
                                              w["l0_ffn2_out"], "l0_ffn2", pusher(3))
    h_in, hn, u, on, states = s2
    d_on = _dgrad(dh, w["ret_out"], "ret_don")
    d_out = _wgrad(on, dh, bm=D, bn=D, scale=1.0, sharded=False, name="ret_dwout")
    du, small["ret_gain"] = _ret_bwd(u, w["ret_gain"] + tok[0, 0], rope, states, d_on, "ret_bwd")
    d_in = _wgrad(hn, du, bm=D, bn=w["ret_in"].shape[2], scale=1.0, sharded=True, name="ret_dwin")
    tok = grads.push(4, [d_in, d_out])
    dh, small["norm_mix"][0] = _dgrad_norm(du, w["ret_in"], h_in, gains["ret"] + tok[0, 0], dh, "ret_dnorm")
    dh, small["norm_ffn1"][0], _ = _ffn_bwd(dh, s1, gains["l0_ffn1"], w["l0_ffn1_in"], w["l0_ffn1_out"], "l0_ffn1",
                                            pusher(5))
    grads.push(6, [], [dh[FRONT - N_META:FRONT], *small["norm_ffn1"], *small["norm_mix"], *small["norm_ffn2"],
                       small["final_norm"], small["ret_gain"], small["gla_wg"][:GLA_RANK], small["gla_bg"],
                       small["gla_gain"], loss[:, :1]])
    return dh[FRONT:]


_HBM = pl.BlockSpec(memory_space=pl.ANY)


def _place():
    return lax.axis_index("x"), lax.axis_index("y"), lax.axis_index("c")


def _flip(v, bit):
    return 1 - v if bit else v


DMA_CHUNK_BYTES = 128 * 1024


def _row_chunks(ref):
    rows, cols = ref.shape
    step = _row_tile(rows, max(16, DMA_CHUNK_BYTES // (cols * ref.dtype.itemsize)))
    return [pl.ds(a, step) for a in range(0, rows, step)]


def _whole(src, dst, send_sem, recv_sem, peer):
    return pltpu.make_async_remote_copy(src_ref=src, dst_ref=dst, send_sem=send_sem, recv_sem=recv_sem,
                                        device_id=peer, device_id_type=MESH)


def _send(src, dst, send_sem, recv_sem, peer):
    for rows in _row_chunks(src):
        _whole(src.at[rows], dst.at[rows], send_sem, recv_sem, peer).start()
    return _whole(src, dst, send_sem, recv_sem, peer)


_HBM_ONLY = pl.BlockSpec(memory_space=pltpu.HBM)
_SEMS = pl.BlockSpec(memory_space=pltpu.SEMAPHORE)
_SIDE_EFFECT = pltpu.CompilerParams(has_side_effects=pltpu.SideEffectType.DATAFLOW_SIDE_EFFECTING)
_GATHER_FLIPS = [(1, 0, 0), (0, 1, 0), (1, 1, 0), (0, 0, 1)]
_PEER_FLIPS = [(fx, fy, fc) for fx in (0, 1) for fy in (0, 1) for fc in (0, 1)][1:]


def _zero_token():
    return jnp.zeros((8, 128), F32)


def _exchange_start(srcs, lands, route, flips, after, name):
    n = len(srcs)

    def body(*refs):
        src, land = refs[:n], refs[n:2 * n]
        send_sems, recv_sems, token = refs[2 * n + 1], refs[2 * n + 2], refs[-1]
        me = _place()
        for t in range(n):
            for j, flip in enumerate(flips):
                peer = tuple(_flip(v, f) for v, f in zip(me, flip))
                s, d = route(t, src[t], land[t], me, peer)
                _send(s, d, send_sems.at[t * len(flips) + j], recv_sems.at[t * len(flips) + j], peer)
        token[...] = jnp.zeros_like(token)

    hbm = lambda a: pltpu.HBM(a.shape, a.dtype)
    sems = pltpu.SemaphoreType.DMA((n * len(flips),))
    operands = [pltpu.with_memory_space_constraint(a, pltpu.HBM) for a in list(srcs) + list(lands)]
    out = pl.pallas_call(
        body, name=name, in_specs=[_HBM_ONLY] * (2 * n) + [_HBM],
        out_shape=(sems, sems, *[hbm(a) for a in operands], jax.ShapeDtypeStruct((8, 128), F32)),
        out_specs=(_SEMS, _SEMS, *[_HBM_ONLY] * (2 * n), pl.BlockSpec(memory_space=pltpu.VMEM)),
        input_output_aliases={i: 2 + i for i in range(2 * n)}, compiler_params=_SIDE_EFFECT,
    )(*operands, _zero_token() if after is None else after)
    return (out[0], out[1], out[2:2 + n], out[2 + n:2 + 2 * n]), out[-1]


def _exchange_wait(started, route, flips, after, name):
    send_sems, recv_sems, srcs, lands = started
    n = len(srcs)

    def body(*refs):
        src, land = refs[:n], refs[n:2 * n]
        send_sems, recv_sems = refs[2 * n], refs[2 * n + 1]
        me = _place()
        for t in range(n):
            for j, flip in enumerate(flips):
                peer = tuple(_flip(v, f) for v, f in zip(me, flip))
                s, d = route(t, src[t], land[t], me, peer)
                cp = _whole(s, d, send_sems.at[t * len(flips) + j], recv_sems.at[t * len(flips) + j], peer)
                cp.wait_send()
                cp.wait_recv()

    hbm = lambda a: pltpu.HBM(a.shape, a.dtype)
    after = list(after) if isinstance(after, (list, tuple)) else [after]
    out = pl.pallas_call(
        body, name=name, in_specs=[_HBM_ONLY] * (2 * n) + [_SEMS, _SEMS] + [_HBM] * len(after),
        out_shape=tuple(hbm(a) for a in list(srcs) + list(lands)), out_specs=tuple([_HBM_ONLY] * (2 * n)),
        input_output_aliases={i: i for i in range(2 * n)}, compiler_params=_SIDE_EFFECT,
    )(*srcs, *lands, send_sems, recv_sems, *after)
    return out[:n], out[n:]


def _gather_route(t, src, land, me, peer):
    mine = 2 * me[0] + me[1]
    if land.ndim == 3:
        return src, land.at[mine]
    cols = src.shape[1]
    return src, land.at[:, pl.ds(pl.multiple_of(mine * cols, 128), cols)]


def _scatter_route(n_pieces):
    def route(t, src, land, me, peer):
        chip = 2 * peer[0] + peer[1]
        if t >= n_pieces:
            part = src
        elif src.ndim == 4:
            part = src.at[chip, peer[2]]
        else:
            rows, cols = land.shape[1:]
            part = src.at[pl.ds(pl.multiple_of(peer[2] * rows, 16), rows), pl.ds(pl.multiple_of(chip * cols, 128), cols)]
        return part, land.at[4 * me[0] + 2 * me[1] + me[2]]

    return route


def _swap_cores(halves):
    n = len(halves)

    def body(*refs):
        src, dst = refs[:n], refs[n:2 * n]
        send_sems, recv_sems = refs[2 * n:]
        x, y, c = _place()
        copies = [_send(src[t], dst[t], send_sems.at[t], recv_sems.at[t], (x, y, 1 - c)) for t in range(n)]
        for cp in copies:
            cp.wait()

    got = pl.pallas_call(
        body, name="swap_cores", in_specs=[_HBM] * n, out_specs=[_HBM] * n,
        out_shape=[jax.ShapeDtypeStruct(a.shape, a.dtype) for a in halves],
        scratch_shapes=[pltpu.SemaphoreType.DMA((n,)), pltpu.SemaphoreType.DMA((n,))],
    )(*halves)
    south = lax.axis_index("c") == 0
    return [jnp.stack([jnp.where(south, a, b), jnp.where(south, b, a)]) for a, b in zip(halves, got)]


def _row_tile(rows, cap):
    fits = [t for t in range(16, cap + 1, 16) if rows % t == 0]
    return fits[-1] if fits else rows


def _sum_slots(a, name):
    _, r, c = a.shape
    tr = _row_tile(r, 384)

    def body(a_ref, o_ref):
        s = a_ref[0].astype(F32)
        for k in range(1, N_DEV):
            s = s + a_ref[k].astype(F32)
        o_ref[...] = s

    return pl.pallas_call(
        body, name=name, grid=(r // tr,),
        in_specs=[pl.BlockSpec((N_DEV, tr, c), lambda i: (0, i, 0))],
        out_specs=pl.BlockSpec((tr, c), lambda i: (i, 0)),
        out_shape=jax.ShapeDtypeStruct((r, c), F32),
        compiler_params=_cp(1))(a)


def _adamw(w, g, m, v, name):
    layers, r, c = w.shape
    tr = _row_tile(r, 256)

    def body(w_ref, g_ref, m_ref, v_ref, d_ref, nm_ref, nv_ref):
        gv = g_ref[...]
        nm = ADAM_B1 * m_ref[...] + (1.0 - ADAM_B1) * gv
        nv = ADAM_B2 * v_ref[...] + (1.0 - ADAM_B2) * (gv * gv)
        m_hat = nm / (1.0 - ADAM_B1 ** ADAM_STEP)
        v_hat = nv / (1.0 - ADAM_B2 ** ADAM_STEP)
        d_ref[...] = -ADAM_LR * (m_hat / (jnp.sqrt(v_hat) + ADAM_EPS) + ADAM_WD * w_ref[...])
        nm_ref[...] = nm
        nv_ref[...] = nv

    spec = pl.BlockSpec((None, tr, c), lambda a, i: (a, i, 0))
    return pl.pallas_call(
        body, name=name, grid=(layers, r // tr), in_specs=[spec] * 4, out_specs=[spec] * 3,
        out_shape=[jax.ShapeDtypeStruct((layers, r, c), F32)] * 3,
        compiler_params=_cp(2))(w, g, m, v)


_SMALL = ["meta_tokens", "ret_head_norm", "gla_w_gate", "gla_b_gate", "gla_head_norm"]
_LOCAL_SMALL = ["meta_tokens", "norm_ffn1", "norm_mix", "norm_ffn2", "ret_head_norm", "gla_w_gate", "gla_b_gate",
                "gla_head_norm", "final_norm"]
_BIG = ["ffn1_w_in", "ffn1_w_out", "ffn2_w_in", "ffn2_w_out", "ret_w_in", "ret_w_out", "gla_w_in", "gla_w_out"]
_WEIGHTS = ["meta_tokens", "norm_ffn1", "ffn1_w_in", "ffn1_w_out", "norm_mix", "norm_ffn2", "ffn2_w_in", "ffn2_w_out",
            "ret_w_in", "ret_head_norm", "ret_w_out", "gla_w_in", "gla_w_gate", "gla_b_gate", "gla_head_norm",
            "gla_w_out", "final_norm"]


def _pack_rows(arrays, width):
    flat = jnp.concatenate([a.reshape(-1) for a in arrays])
    pad = -flat.shape[0] % (8 * width)
    return jnp.pad(flat, (0, pad)).reshape(-1, width)


def _unpack_rows(packed, shapes):
    flat, out, at = packed.reshape(-1), [], 0
    for s in shapes:
        size = 1
        for dim in s:
            size *= dim
        out.append(flat[at:at + size].reshape(s))
        at += size
    return out


class _WeightGather:
    GROUPS = [("small", "l0_ffn1_in"), ("l0_ffn1_out",), ("ret_in",), ("ret_out", "l0_ffn2_in", "l0_ffn2_out"),
              ("l1_ffn1_in", "l1_ffn1_out"), ("gla_in", "gla_out"), ("l1_ffn2_in", "l1_ffn2_out")]

    def __init__(self, p):
        self.small_shapes = [p[name].shape for name in _SMALL]
        self.f32 = {"small": _pack_rows([p[name] for name in _SMALL], 128), "ret_in": p["ret_w_in"][0],
                    "ret_out": p["ret_w_out"][0], "gla_in": p["gla_w_in"][0], "gla_out": p["gla_w_out"][0]}
        for layer in range(2):
            for name in ("ffn1", "ffn2"):
                self.f32[f"l{layer}_{name}_in"] = p[f"{name}_w_in"][layer]
                self.f32[f"l{layer}_{name}_out"] = p[f"{name}_w_out"][layer]
        self.shards = {}
        self.started = {}
        self.pin = None

    def shard(self, name):
        if name not in self.shards:
            a = self.f32[name]
            if name != "small":
                a = (a if self.pin is None else a + self.pin[0, 0]).astype(BF16)
            self.shards[name] = a
        return self.shards[name]

    def later_shards(self, k):
        return [self.shard(name) for group in self.GROUPS[k:] for name in group]

    def start(self, k, after):
        shards = [self.shard(name) for name in self.GROUPS[k]]
        lands = []
        for name, s in zip(self.GROUPS[k], shards):
            if "ffn" in name and name.endswith("_in"):
                lands.append(lax.empty((s.shape[0], N_CHIPS * s.shape[1]), s.dtype))
            else:
                lands.append(lax.empty((N_CHIPS,) + s.shape, s.dtype))
        self.started[k], token = _exchange_start(shards, lands, _gather_route, _GATHER_FLIPS, after, f"gather{k}_start")
        return token

    def wait(self, k, after):
        _, got = _exchange_wait(self.started[k], _gather_route, _GATHER_FLIPS, after, f"gather{k}_wait")
        w = {}
        for name, g in zip(self.GROUPS[k], got):
            if name == "small":
                parts = zip(*[_unpack_rows(g[chip], self.small_shapes) for chip in range(N_CHIPS)])
                cat = lambda a: jnp.moveaxis(a, 0, -2).reshape(a.shape[1:-1] + (-1,))
                meta, ret_gain, wg, bg, gla_gain = [cat(jnp.stack(part)) for part in parts]
                w.update(meta=meta, ret_gain=ret_gain.reshape(1, -1), gla_bg=bg.reshape(1, -1),
                         gla_gain=gla_gain.reshape(1, -1),
                         gla_wg=jnp.pad(wg[0], ((0, 128 - GLA_RANK), (0, 0))).astype(BF16))
            elif name == "gla_in":
                full = jnp.moveaxis(g, 0, 1).reshape(D, -1)
                w[name] = jnp.pad(full, ((0, 0), (0, GLA_U - GLA_IN)))[None]
            elif name.endswith("_out"):
                w[name] = g.reshape(-1, g.shape[-1])
            else:
                w[name] = g
        return w


class _GradExchange:
    def __init__(self):
        self.started = []
        self.token = None
        self.small_shapes = None

    def push(self, k, arrays, small=None):
        srcs, lands = [], []
        for a in arrays:
            if isinstance(a, tuple):
                a = a[1]
                piece = (a.shape[0] // 2, a.shape[1] // N_CHIPS)
            else:
                a = a.reshape(N_CHIPS, 2, -1, a.shape[-1])
                piece = a.shape[2:]
            srcs.append(a)
            lands.append(lax.empty((N_DEV,) + piece, a.dtype))
        if small is not None:
            self.small_shapes = [a.shape for a in small]
            srcs.append(_pack_rows(small, D))
            lands.append(lax.empty((N_DEV,) + srcs[-1].shape, F32))
        started, self.token = _exchange_start(srcs, lands, _scatter_route(len(arrays)), _PEER_FLIPS, None,
                                              f"scatter{k}_start")
        self.started.append((started, len(arrays)))
        return self.token

    def collect(self):
        x, y, c = _place()
        after, sums = self.token, []
        for k, (started, n_pieces) in enumerate(self.started):
            srcs, got = _exchange_wait(started, _scatter_route(n_pieces), _PEER_FLIPS, after, f"scatter{k}_wait")
            own = []
            for t, (a, g) in enumerate(zip(srcs, got)):
                if t >= n_pieces:
                    own.append(a)
                elif a.ndim == 4:
                    own.append(a[2 * x + y, c])
                else:
                    rows, cols = g.shape[1:]
                    own.append(lax.dynamic_slice(a, (c * rows, (2 * x + y) * cols), (rows, cols)))
            got = [lax.dynamic_update_index_in_dim(g, a, 4 * x + 2 * y + c, 0) for g, a in zip(got, own)]
            sums.append([_sum_slots(a, f"sum{k}_{i}") for i, a in enumerate(got)])
            after = sums[-1][0]
        small = _unpack_rows(sums[-1].pop(), self.small_shapes)
        return sums, small


def kernel(x, meta_tokens, norm_ffn1, ffn1_w_in, ffn1_w_out, norm_mix, norm_ffn2, ffn2_w_in, ffn2_w_out, ret_w_in, ret_head_norm, ret_w_out, gla_w_in, gla_w_gate, gla_b_gate, gla_head_norm, gla_w_out, final_norm, loss_target, m_meta_tokens, m_norm_ffn1, m_ffn1_w_in, m_ffn1_w_out, m_norm_mix, m_norm_ffn2, m_ffn2_w_in, m_ffn2_w_out, m_ret_w_in, m_ret_head_norm, m_ret_w_out, m_gla_w_in, m_gla_w_gate, m_gla_b_gate, m_gla_head_norm, m_gla_w_out, m_final_norm, v_meta_tokens, v_norm_ffn1, v_ffn1_w_in, v_ffn1_w_out, v_norm_mix, v_norm_ffn2, v_ffn2_w_in, v_ffn2_w_out, v_ret_w_in, v_ret_head_norm, v_ret_w_out, v_gla_w_in, v_gla_w_gate, v_gla_b_gate, v_gla_head_norm, v_gla_w_out, v_final_norm):
    p = dict(meta_tokens=meta_tokens, norm_ffn1=norm_ffn1, ffn1_w_in=ffn1_w_in, ffn1_w_out=ffn1_w_out, norm_mix=norm_mix,
             norm_ffn2=norm_ffn2, ffn2_w_in=ffn2_w_in, ffn2_w_out=ffn2_w_out, ret_w_in=ret_w_in,
             ret_head_norm=ret_head_norm, ret_w_out=ret_w_out, gla_w_in=gla_w_in, gla_w_gate=gla_w_gate,
             gla_b_gate=gla_b_gate, gla_head_norm=gla_head_norm, gla_w_out=gla_w_out, final_norm=final_norm)
    m = dict(zip(_WEIGHTS, (m_meta_tokens, m_norm_ffn1, m_ffn1_w_in, m_ffn1_w_out, m_norm_mix, m_norm_ffn2, m_ffn2_w_in,
                            m_ffn2_w_out, m_ret_w_in, m_ret_head_norm, m_ret_w_out, m_gla_w_in, m_gla_w_gate,
                            m_gla_b_gate, m_gla_head_norm, m_gla_w_out, m_final_norm)))
    v = dict(zip(_WEIGHTS, (v_meta_tokens, v_norm_ffn1, v_ffn1_w_in, v_ffn1_w_out, v_norm_mix, v_norm_ffn2, v_ffn2_w_in,
                            v_ffn2_w_out, v_ret_w_in, v_ret_head_norm, v_ret_w_out, v_gla_w_in, v_gla_w_gate,
                            v_gla_b_gate, v_gla_head_norm, v_gla_w_out, v_final_norm)))

    exchange = _GradExchange()
    d_x = _sequence_grads(x[0], loss_target[0], p, _WeightGather(p), exchange)
    sums, small = exchange.collect()
    names = [("ffn2_in", 1), ("ffn2_out", 1), ("gla_in", 0), ("gla_out", 0), ("ffn1_in", 1), ("ffn1_out", 1),
             ("ffn2_in", 0), ("ffn2_out", 0), ("ret_in", 0), ("ret_out", 0), ("ffn1_in", 0), ("ffn1_out", 0)]
    swapped = _swap_cores([a for group in sums for a in group])
    shard = {key: a.reshape(-1, a.shape[-1]) for key, a in zip(names, swapped)}
    big = {name: [shard[name, layer] for layer in range(2) if (name, layer) in shard] for name, _ in names}

    chip = 2 * lax.axis_index("x") + lax.axis_index("y")
    cols = lambda a, n: lax.dynamic_slice_in_dim(a, chip * n, n, axis=a.ndim - 1)
    (s_meta, s_n1a, s_n1b, s_nma, s_nmb, s_n2a, s_n2b, s_final, s_ret_gain, s_wg, s_bg, s_gla_gain, s_loss) = small
    grads = {
        "meta_tokens": cols(s_meta, 256), "norm_ffn1": jnp.concatenate([s_n1a, s_n1b]),
        "norm_mix": jnp.concatenate([s_nma, s_nmb]), "norm_ffn2": jnp.concatenate([s_n2a, s_n2b]),
        "final_norm": s_final.reshape(D),
        "ret_head_norm": cols(s_ret_gain.reshape(1, HEADS, RET_DV), RET_DV // N_CHIPS),
        "gla_w_gate": cols(s_wg, GLA_DK)[None], "gla_b_gate": cols(s_bg, GLA_DK),
        "gla_head_norm": cols(s_gla_gain.reshape(1, HEADS, GLA_DV), GLA_DV // N_CHIPS),
        "ffn1_w_in": jnp.stack(big["ffn1_in"]), "ffn1_w_out": jnp.stack(big["ffn1_out"]),
        "ffn2_w_in": jnp.stack(big["ffn2_in"]), "ffn2_w_out": jnp.stack(big["ffn2_out"]),
        "ret_w_in": big["ret_in"][0][None], "ret_w_out": big["ret_out"][0][None],
        "gla_w_in": big["gla_in"][0][None], "gla_w_out": big["gla_out"][0][None],
    }

    delta, new_m, new_v = {}, {}, {}
    for name in _BIG:
        delta[name], new_m[name], new_v[name] = _adamw(p[name], grads[name], m[name], v[name], f"adamw_{name}")
    packed = [_pack_rows([d[name] for name in _LOCAL_SMALL], 128)[None] for d in (p, grads, m, v)]
    out = _adamw(*packed, "adamw_small")
    shapes = [p[name].shape for name in _LOCAL_SMALL]
    for d, a in zip((delta, new_m, new_v), out):
        d.update(zip(_LOCAL_SMALL, _unpack_rows(a, shapes)))

    return (s_loss.reshape(()), d_x[None], *[grads[n] for n in _WEIGHTS], *[delta[n] for n in _WEIGHTS],
            *[new_m[n] for n in _WEIGHTS], *[new_v[n] for n in _WEIGHTS])
```

```python
import functools

import jax
import jax.numpy as jnp
from jax import lax
from jax.experimental import pallas as pl
from jax.experimental.pallas import tpu as pltpu

F32, BF16 = jnp.float32, jnp.bfloat16
MESH = pl.DeviceIdType.MESH

D = 1024
N_META = 16
CHUNK = 64
RET_CHUNK = 128
FRONT = 256
D_FF = 2816
EPS = 1e-6
HEADS = 4
RET_DK, RET_DV = 256, 512
GLA_DK, GLA_DV = 128, 256
GLA_RANK = 16
GLA_TAU = 16.0
GLA_IN = 2 * HEADS * GLA_DK + 2 * HEADS * GLA_DV + GLA_RANK
GLA_U = 3200
ROPE_BASE = 10000.0
N_CHIPS = 4
N_DEV = 8

ADAM_LR, ADAM_B1, ADAM_B2, ADAM_EPS, ADAM_WD, ADAM_STEP = 0.001, 0.9, 0.999, 1e-08, 0.01, 10

VMEM_LIMIT_BYTES = 56 * 1024 * 1024
TM = 768
TM_SMALL = 256


TM_RESIDENT = 384
MXU_TILE = 256


def _cp(n_axes):
    return pltpu.CompilerParams(dimension_semantics=("arbitrary",) * n_axes, vmem_limit_bytes=VMEM_LIMIT_BYTES)


def _resident(shape, n_axes):
    zeros = (0,) * len(shape)
    index = (lambda i: zeros) if n_axes == 1 else (lambda i, j: zeros)
    return pl.BlockSpec(shape, index, pipeline_mode=pl.Buffered(1))


def _dg(a, b, ca, cb):
    nb = a.ndim - 2
    dims = (((ca + nb,), (cb + nb,)), (tuple(range(nb)), tuple(range(nb))))
    return lax.dot_general(a.astype(BF16), b.astype(BF16), dims, preferred_element_type=F32)


@jax.custom_vjp
def _nn(a, b):
    return _dg(a, b, 1, 0)


@jax.custom_vjp
def _nt(a, b):
    return _dg(a, b, 1, 1)


@jax.custom_vjp
def _tn(a, b):
    return _dg(a, b, 0, 0)


def _dot_vjp(fn, ca, cb, da, db):
    def fwd(a, b):
        a, b = a.astype(BF16), b.astype(BF16)
        return _dg(a, b, ca, cb), (a, b)

    def bwd(res, g):
        a, b = res
        g = g.astype(BF16)
        grad = lambda other, dims, g_first: _dg(g, other, *dims) if g_first else _dg(other, g, *dims)
        return grad(b, *da), grad(a, *db)

    fn.defvjp(fwd, bwd)


_dot_vjp(_nn, 1, 0, ((1, 1), True), ((0, 0), False))
_dot_vjp(_nt, 1, 1, ((1, 0), True), ((0, 0), True))
_dot_vjp(_tn, 0, 0, ((1, 1), False), ((1, 0), False))


def _split3_dot(m, a):
    a1 = a.astype(BF16)
    r1 = a - a1.astype(F32)
    a2 = r1.astype(BF16)
    a3 = (r1 - a2.astype(F32)).astype(BF16)
    mb = jnp.broadcast_to(m, a.shape[:-2] + m.shape)
    return _dg(mb, a1, 1, 0) + _dg(mb, a2, 1, 0) + _dg(mb, a3, 1, 0)


@jax.custom_vjp
def _cum(m, mt, a):
    return _split3_dot(m, a)


_cum.defvjp(lambda m, mt, a: (_split3_dot(m, a), (m, mt)),
            lambda res, g: (jnp.zeros_like(res[0]), jnp.zeros_like(res[1]), _split3_dot(res[1], g)))


def _sigmoid(x):
    return 1.0 / (1.0 + jnp.exp(-x))


def _rms(x):
    return lax.rsqrt(jnp.mean(x * x, axis=-1, keepdims=True) + EPS)


def _rmsnorm_bwd(dy, x, gain):
    r = _rms(x)
    xhat = x * r
    dxh = dy * gain
    return r * (dxh - xhat * jnp.mean(dxh * xhat, axis=-1, keepdims=True)), xhat


def _norm_proj(h, gain, w, name):
    tp, d = h.shape
    s, _, ns = w.shape

    tm = TM_RESIDENT

    def body(h_ref, g_ref, w_ref, hn_ref, u_ref):
        @pl.when(pl.program_id(1) == 0)
        def _():
            x = h_ref[...]
            hn_ref[...] = (x * _rms(x) * g_ref[...]).astype(BF16)

        u_ref[...] = jnp.dot(hn_ref[...], w_ref[pl.program_id(1)], preferred_element_type=F32).astype(BF16)

    return pl.pallas_call(
        body, name=name, grid=(tp // tm, s),
        in_specs=[pl.BlockSpec((tm, d), lambda i, j: (i, 0)), pl.BlockSpec((1, d), lambda i, j: (0, 0)),
                  _resident(w.shape, 2)],
        out_specs=[pl.BlockSpec((tm, d), lambda i, j: (i, 0)), pl.BlockSpec((tm, ns), lambda i, j: (i, j))],
        out_shape=[jax.ShapeDtypeStruct((tp, d), BF16), jax.ShapeDtypeStruct((tp, s * ns), BF16)],
        compiler_params=_cp(2))(h, gain, w)


def _norm_ffn_in(h, gain, w, name):
    tp, d = h.shape
    ff = w.shape[1] // 2
    tm = TM_RESIDENT
    blocks = [(c, min(c + 6 * MXU_TILE, ff)) for c in range(0, ff, 6 * MXU_TILE)]

    def body(h_ref, g_ref, w_ref, hn_ref, dg_ref, du_ref, act_ref):
        x = h_ref[...]
        a = (x * _rms(x) * g_ref[...]).astype(BF16)
        hn_ref[...] = a
        for c0, c1 in blocks:
            g = jnp.dot(a, w_ref[:, c0:c1], preferred_element_type=F32)
            u = jnp.dot(a, w_ref[:, ff + c0:ff + c1], preferred_element_type=F32)
            sg = _sigmoid(g)
            silu = g * sg
            dg_ref[:, c0:c1] = (u * (sg + silu * (1.0 - sg))).astype(BF16)
            du_ref[:, c0:c1] = silu.astype(BF16)
            act_ref[:, c0:c1] = (silu * u).astype(BF16)

    wide = jax.ShapeDtypeStruct((tp, ff), BF16)
    return pl.pallas_call(
        body, name=name, grid=(tp // tm,),
        in_specs=[pl.BlockSpec((tm, d), lambda i: (i, 0)), pl.BlockSpec((1, d), lambda i: (0, 0)),
                  _resident(w.shape, 1)],
        out_specs=[pl.BlockSpec((tm, d), lambda i: (i, 0))] + [pl.BlockSpec((tm, ff), lambda i: (i, 0))] * 3,
        out_shape=[jax.ShapeDtypeStruct((tp, d), BF16), wide, wide, wide],
        compiler_params=_cp(1))(h, gain, w)


def _out_proj(a, w, h, scale, name):
    tp, k = a.shape
    d = w.shape[1]

    def body(a_ref, w_ref, h_ref, o_ref):
        o_ref[...] = h_ref[...] + scale * jnp.dot(a_ref[...], w_ref[...], preferred_element_type=F32)

    return pl.pallas_call(
        body, name=name, grid=(tp // TM,),
        in_specs=[pl.BlockSpec((TM, k), lambda i: (i, 0)), pl.BlockSpec((k, d), lambda i: (0, 0)),
                  pl.BlockSpec((TM, d), lambda i: (i, 0))],
        out_specs=pl.BlockSpec((TM, d), lambda i: (i, 0)),
        out_shape=jax.ShapeDtypeStruct((tp, d), F32),
        compiler_params=_cp(1))(a, w, h)


def _dgrad(dh, w, name):
    tp, d = dh.shape
    k = w.shape[0]

    def body(dh_ref, w_ref, o_ref):
        o_ref[...] = lax.dot_general(dh_ref[...].astype(BF16), w_ref[...], (((1,), (1,)), ((), ())),
                                     preferred_element_type=F32).astype(BF16)

    return pl.pallas_call(
        body, name=name, grid=(tp // TM,),
        in_specs=[pl.BlockSpec((TM, d), lambda i: (i, 0)), pl.BlockSpec((k, d), lambda i: (0, 0))],
        out_specs=pl.BlockSpec((TM, k), lambda i: (i, 0)),
        out_shape=jax.ShapeDtypeStruct((tp, k), BF16),
        compiler_params=_cp(1))(dh, w)


def _wgrad(a, b, *, bm, bn, scale, sharded, name):
    tp, m = a.shape
    n = b.shape[1]
    nk = tp // TM

    def body(a_ref, b_ref, o_ref, acc_ref):
        k = pl.program_id(2)

        @pl.when(k == 0)
        def _():
            acc_ref[...] = jnp.zeros_like(acc_ref)

        bb = b_ref[...]
        if scale != 1.0:
            bb = scale * bb
        acc_ref[...] += lax.dot_general(a_ref[...], bb.astype(BF16), (((0,), (0,)), ((), ())),
                                        preferred_element_type=F32)

        @pl.when(k == nk - 1)
        def _():
            o_ref[...] = acc_ref[...].astype(BF16)

    if sharded:
        assert m == bm
        out_spec = pl.BlockSpec((None, bm, bn), lambda i, j, k: (j, 0, 0))
        out_shape = jax.ShapeDtypeStruct((n // bn, m, bn), BF16)
    else:
        out_spec = pl.BlockSpec((bm, bn), lambda i, j, k: (i, j))
        out_shape = jax.ShapeDtypeStruct((m, n), BF16)
    return pl.pallas_call(
        body, name=name, grid=(m // bm, n // bn, nk),
        in_specs=[pl.BlockSpec((TM, bm), lambda i, j, k: (k, i)), pl.BlockSpec((TM, bn), lambda i, j, k: (k, j))],
        out_specs=out_spec, out_shape=out_shape,
        scratch_shapes=[pltpu.VMEM((bm, bn), F32)],
        compiler_params=_cp(3))(a, b)


def _dgrad_norm(du, w, h, gain, dh_out, name):
    tp, d = h.shape
    s, _, ns = w.shape
    tm = TM_RESIDENT

    def body(du_ref, w_ref, h_ref, g_ref, dho_ref, dhi_ref, dg_ref):
        @pl.when(pl.program_id(0) == 0)
        def _():
            dg_ref[...] = jnp.zeros_like(dg_ref)

        dhn = None
        for k in range(s):
            part = lax.dot_general(du_ref[:, ns * k:ns * (k + 1)], w_ref[k], (((1,), (1,)), ((), ())),
                                   preferred_element_type=F32)
            dhn = part if dhn is None else dhn + part
        dx, xhat = _rmsnorm_bwd(dhn, h_ref[...], g_ref[...])
        dg_ref[...] += jnp.sum(dhn * xhat, axis=0, keepdims=True)
        dhi_ref[...] = dho_ref[...] + dx

    return pl.pallas_call(
        body, name=name, grid=(tp // tm,),
        in_specs=[pl.BlockSpec((tm, s * ns), lambda i: (i, 0)), _resident(w.shape, 1),
                  pl.BlockSpec((tm, d), lambda i: (i, 0)), pl.BlockSpec((1, d), lambda i: (0, 0)),
                  pl.BlockSpec((tm, d), lambda i: (i, 0))],
        out_specs=[pl.BlockSpec((tm, d), lambda i: (i, 0)), pl.BlockSpec((1, d), lambda i: (0, 0))],
        out_shape=[jax.ShapeDtypeStruct((tp, d), F32), jax.ShapeDtypeStruct((1, d), F32)],
        compiler_params=_cp(1))(du, w, h, gain, dh_out)


def _loss_head(h, gain, target, name):
    tp, d = h.shape
    tm = TM_SMALL
    front_tiles = FRONT // tm

    def body(h_ref, g_ref, t_ref, dh_ref, dg_ref, loss_ref):
        i = pl.program_id(0)

        @pl.when(i == 0)
        def _():
            dg_ref[...] = jnp.zeros_like(dg_ref)
            loss_ref[...] = jnp.zeros_like(loss_ref)

        x = h_ref[...]
        gain_v = g_ref[...]
        y = x * _rms(x) * gain_v
        err = jnp.where(i >= front_tiles, y - t_ref[...], 0.0)
        loss_ref[...] += 0.5 * jnp.sum(jnp.mean(err * err, axis=-1, keepdims=True), axis=0, keepdims=True)
        dy = err * (1.0 / d)
        dx, xhat = _rmsnorm_bwd(dy, x, gain_v)
        dg_ref[...] += jnp.sum(dy * xhat, axis=0, keepdims=True)
        dh_ref[...] = dx

    return pl.pallas_call(
        body, name=name, grid=(tp // tm,),
        in_specs=[pl.BlockSpec((tm, d), lambda i: (i, 0)), pl.BlockSpec((1, d), lambda i: (0, 0)),
                  pl.BlockSpec((tm, d), lambda i: (jnp.maximum(i - front_tiles, 0), 0))],
        out_specs=[pl.BlockSpec((tm, d), lambda i: (i, 0)), pl.BlockSpec((1, d), lambda i: (0, 0)),
                   pl.BlockSpec((1, 128), lambda i: (0, 0))],
        out_shape=[jax.ShapeDtypeStruct((tp, d), F32), jax.ShapeDtypeStruct((1, d), F32),
                   jax.ShapeDtypeStruct((1, 128), F32)],
        compiler_params=_cp(1))(h, gain, target)


def _gated_headnorm(o, g, gain):
    return o * _rms(o) * gain * (g * _sigmoid(g))


def _row_mask(chunk, size=CHUNK):
    rows = chunk * size + lax.broadcasted_iota(jnp.int32, (size, 1), 0)
    return (rows >= FRONT - N_META).astype(F32)


def _ret_head(q1, q2, k1, k2, v, g, state, gain, cos, sin, dmat, dq, dk, dc):
    q = jnp.concatenate([q1 * cos - q2 * sin, q1 * sin + q2 * cos], axis=-1)
    k = jnp.concatenate([k1 * cos - k2 * sin, k1 * sin + k2 * cos], axis=-1) * (RET_DK ** -0.5)
    scores = _nt(q, k) * dmat
    o = _nn(scores, v) + _nn(q * dq, state)
    new_state = state * dc + _tn(k * dk, v)
    return _gated_headnorm(o, g, gain), new_state


def _ret_consts():
    log_gamma = jnp.log1p(-2.0 ** (-5.0 - jnp.arange(HEADS, dtype=F32)))
    idx = jnp.arange(RET_CHUNK, dtype=F32)
    rel = idx[:, None] - idx[None, :]
    dmat = jnp.where(rel >= 0, jnp.exp(log_gamma[:, None, None] * jnp.maximum(rel, 0.0)), 0.0)
    dq = jnp.exp(log_gamma[:, None] * (idx + 1.0))[..., None]
    dk = jnp.exp(log_gamma[:, None] * (RET_CHUNK - 1.0 - idx))[..., None]
    dc = jnp.broadcast_to(jnp.exp(log_gamma * RET_CHUNK)[:, None, None], (HEADS, 1, 128))
    return dmat, dq, dk, dc


def _rope_tables(tp):
    half = RET_DK // 2
    inv = 1.0 / (ROPE_BASE ** jnp.linspace(0.0, 1.0, half, dtype=F32))
    pos = (jnp.arange(tp) - (FRONT - N_META)).astype(F32)
    ang = pos[:, None] * inv[None, :]
    return jnp.cos(ang), jnp.sin(ang)


_RET_V0, _RET_G0 = 2 * D, 4 * D


def _heads(ref, start, width, stride=None):
    stride = width if stride is None else stride
    return jnp.stack([ref[:, start + stride * h:start + stride * h + width].astype(F32) for h in range(HEADS)])


def _put_heads(ref, start, value, mask, stride=None):
    width = value.shape[-1]
    stride = width if stride is None else stride
    for h in range(HEADS):
        ref[:, start + stride * h:start + stride * h + width] = (value[h] * mask).astype(ref.dtype)


def _ret_pieces(u_ref):
    hk = RET_DK // 2
    return (_heads(u_ref, 0, hk, RET_DK), _heads(u_ref, hk, hk, RET_DK), _heads(u_ref, D, hk, RET_DK),
            _heads(u_ref, D + hk, hk, RET_DK), _heads(u_ref, _RET_V0, RET_DV), _heads(u_ref, _RET_G0, RET_DV))


def _ret_const_specs(rev=None):
    c = (lambda n: (rev(n), 0)) if rev else (lambda n: (n, 0))
    z3 = lambda n: (0, 0, 0)
    return [pl.BlockSpec((RET_CHUNK, RET_DK // 2), c), pl.BlockSpec((RET_CHUNK, RET_DK // 2), c),
            pl.BlockSpec((HEADS, RET_CHUNK, RET_CHUNK), z3), pl.BlockSpec((HEADS, RET_CHUNK, 1), z3),
            pl.BlockSpec((HEADS, RET_CHUNK, 1), z3), pl.BlockSpec((HEADS, 1, 128), z3)]


def _ret_fwd(u, gain, rope, name):
    tp = u.shape[0]
    nch = tp // RET_CHUNK
    cos, sin = rope
    dmat, dq, dk, dc = _ret_consts()

    def body(u_ref, gain_ref, cos_ref, sin_ref, dmat_ref, dq_ref, dk_ref, dc_ref, on_ref, st_ref, state_ref):
        @pl.when(pl.program_id(0) == 0)
        def _():
            state_ref[...] = jnp.zeros_like(state_ref)

        state = state_ref[...]
        st_ref[...] = state.astype(BF16)
        on, new_state = _ret_head(*_ret_pieces(u_ref), state, _heads(gain_ref, 0, RET_DV), cos_ref[...], sin_ref[...],
                                  dmat_ref[...], dq_ref[...], dk_ref[...], dc_ref[...][:, :, :1])
        state_ref[...] = new_state
        _put_heads(on_ref, 0, on, 1.0)

    return pl.pallas_call(
        body, name=name, grid=(nch,),
        in_specs=[pl.BlockSpec((RET_CHUNK, 6 * D), lambda n: (n, 0)), pl.BlockSpec((1, HEADS * RET_DV), lambda n: (0, 0))]
                 + _ret_const_specs(),
        out_specs=[pl.BlockSpec((RET_CHUNK, HEADS * RET_DV), lambda n: (n, 0)),
                   pl.BlockSpec((None, HEADS, RET_DK, RET_DV), lambda n: (n, 0, 0, 0))],
        out_shape=[jax.ShapeDtypeStruct((tp, HEADS * RET_DV), BF16),
                   jax.ShapeDtypeStruct((nch, HEADS, RET_DK, RET_DV), BF16)],
        scratch_shapes=[pltpu.VMEM((HEADS, RET_DK, RET_DV), F32)],
        compiler_params=_cp(1))(u, gain, cos, sin, dmat, dq, dk, dc)


def _ret_bwd(u, gain, rope, states, d_on, name):
    tp = u.shape[0]
    nch = tp // RET_CHUNK
    cos, sin = rope
    dmat, dq, dk, dc = _ret_consts()
    rev = lambda n: nch - 1 - n
    hk = RET_DK // 2

    def body(u_ref, gain_ref, st_ref, don_ref, cos_ref, sin_ref, dmat_ref, dq_ref, dk_ref, dc_ref,
             du_ref, dgain_ref, dstate_ref):
        @pl.when(pl.program_id(0) == 0)
        def _():
            dstate_ref[...] = jnp.zeros_like(dstate_ref)
            dgain_ref[...] = jnp.zeros_like(dgain_ref)

        mask = _row_mask(rev(pl.program_id(0)), RET_CHUNK)
        consts = (cos_ref[...], sin_ref[...], dmat_ref[...], dq_ref[...], dk_ref[...], dc_ref[...][:, :, :1])
        _, vjp = jax.vjp(lambda *a: _ret_head(*a, *consts), *_ret_pieces(u_ref), st_ref[...].astype(F32),
                         _heads(gain_ref, 0, RET_DV))
        dq1, dq2, dk1, dk2, dv, dg, dstate, dgain = vjp((_heads(don_ref, 0, RET_DV), dstate_ref[...]))
        dstate_ref[...] = dstate
        for hd in range(HEADS):
            dgain_ref[:, RET_DV * hd:RET_DV * (hd + 1)] += dgain[hd]
        _put_heads(du_ref, 0, dq1, mask, RET_DK)
        _put_heads(du_ref, hk, dq2, mask, RET_DK)
        _put_heads(du_ref, D, dk1, mask, RET_DK)
        _put_heads(du_ref, D + hk, dk2, mask, RET_DK)
        _put_heads(du_ref, _RET_V0, dv, mask)
        _put_heads(du_ref, _RET_G0, dg, mask)

    return pl.pallas_call(
        body, name=name, grid=(nch,),
        in_specs=[pl.BlockSpec((RET_CHUNK, 6 * D), lambda n: (rev(n), 0)),
                  pl.BlockSpec((1, HEADS * RET_DV), lambda n: (0, 0)),
                  pl.BlockSpec((None, HEADS, RET_DK, RET_DV), lambda n: (rev(n), 0, 0, 0)),
                  pl.BlockSpec((RET_CHUNK, HEADS * RET_DV), lambda n: (rev(n), 0))] + _ret_const_specs(rev),
        out_specs=[pl.BlockSpec((RET_CHUNK, 6 * D), lambda n: (rev(n), 0)),
                   pl.BlockSpec((1, HEADS * RET_DV), lambda n: (0, 0))],
        out_shape=[jax.ShapeDtypeStruct((tp, 6 * D), BF16), jax.ShapeDtypeStruct((1, HEADS * RET_DV), F32)],
        scratch_shapes=[pltpu.VMEM((HEADS, RET_DK, RET_DV), F32)],
        compiler_params=_cp(1))(u, gain, states, d_on, cos, sin, dmat, dq, dk, dc)


_GLA_K0, _GLA_V0, _GLA_G0, _GLA_Z0 = 512, 1024, 2048, 3072


def _gla_head(q, k, v, g, z, state_t, wg, bg, gain, mask, lo, lo_t, loc, loc_t):
    ga = _nn(jnp.broadcast_to(z, wg.shape[:-2] + z.shape), wg) + bg
    log_a = (jnp.minimum(ga, 0.0) - jnp.log(1.0 + jnp.exp(-jnp.abs(ga)))) * (mask * (1.0 / GLA_TAU))
    bcum = _cum(lo, lo_t, log_a)
    bmid = _cum(loc, loc_t, log_a)
    btot = jnp.sum(log_a, axis=-2, keepdims=True)
    qs = q * (GLA_DK ** -0.5)
    causal = lax.broadcasted_iota(jnp.int32, (CHUNK, CHUNK), 0) >= lax.broadcasted_iota(jnp.int32, (CHUNK, CHUNK), 1)
    scores = jnp.where(causal, _nt(qs * jnp.exp(bmid), k * jnp.exp(-bmid)), 0.0)
    o = _nn(scores, v) + _nt(qs * jnp.exp(bcum), state_t)
    new_state_t = state_t * jnp.exp(btot) + _tn(v, k * jnp.exp(btot - bcum))
    return _gated_headnorm(o, g, gain), new_state_t


def _cum_mats():
    r = lax.broadcasted_iota(jnp.int32, (CHUNK, CHUNK), 0)
    c = lax.broadcasted_iota(jnp.int32, (CHUNK, CHUNK), 1)
    mid = CHUNK // 2
    low = lambda a, b: (a >= b).astype(F32)
    lo, lo_t = low(r, c), low(c, r)
    loc = lo - (c <= mid).astype(F32)
    loc_t = lo_t - (r <= mid).astype(F32)
    return tuple(m.astype(BF16) for m in (lo, lo_t, loc, loc_t))


def _gla_pieces(u_ref):
    return (_heads(u_ref, 0, GLA_DK), _heads(u_ref, _GLA_K0, GLA_DK), _heads(u_ref, _GLA_V0, GLA_DV),
            _heads(u_ref, _GLA_G0, GLA_DV), u_ref[:, _GLA_Z0:].astype(F32))


def _gla_fwd(u, wg, bg, gain, name):
    tp = u.shape[0]
    nch = tp // CHUNK

    def body(u_ref, wg_ref, bg_ref, gain_ref, on_ref, st_ref, state_ref):
        @pl.when(pl.program_id(0) == 0)
        def _():
            state_ref[...] = jnp.zeros_like(state_ref)

        state = state_ref[...]
        st_ref[...] = state.astype(BF16)
        on, new_state = _gla_head(*_gla_pieces(u_ref), state, _heads(wg_ref, 0, GLA_DK), _heads(bg_ref, 0, GLA_DK),
                                  _heads(gain_ref, 0, GLA_DV), _row_mask(pl.program_id(0)), *_cum_mats())
        state_ref[...] = new_state
        _put_heads(on_ref, 0, on, 1.0)

    return pl.pallas_call(
        body, name=name, grid=(nch,),
        in_specs=[pl.BlockSpec((CHUNK, GLA_U), lambda n: (n, 0)), pl.BlockSpec((128, HEADS * GLA_DK), lambda n: (0, 0)),
                  pl.BlockSpec((1, HEADS * GLA_DK), lambda n: (0, 0)), pl.BlockSpec((1, HEADS * GLA_DV), lambda n: (0, 0))],
        out_specs=[pl.BlockSpec((CHUNK, HEADS * GLA_DV), lambda n: (n, 0)),
                   pl.BlockSpec((None, HEADS, GLA_DV, GLA_DK), lambda n: (n, 0, 0, 0))],
        out_shape=[jax.ShapeDtypeStruct((tp, HEADS * GLA_DV), BF16),
                   jax.ShapeDtypeStruct((nch, HEADS, GLA_DV, GLA_DK), BF16)],
        scratch_shapes=[pltpu.VMEM((HEADS, GLA_DV, GLA_DK), F32)],
        compiler_params=_cp(1))(u, wg, bg, gain)


def _gla_bwd(u, wg, bg, gain, states, d_on, name):
    tp = u.shape[0]
    nch = tp // CHUNK
    rev = lambda n: nch - 1 - n

    def body(u_ref, wg_ref, bg_ref, gain_ref, st_ref, don_ref, du_ref, dwg_ref, dbg_ref, dgain_ref, dstate_ref):
        @pl.when(pl.program_id(0) == 0)
        def _():
            dstate_ref[...] = jnp.zeros_like(dstate_ref)
            dwg_ref[...] = jnp.zeros_like(dwg_ref)
            dbg_ref[...] = jnp.zeros_like(dbg_ref)
            dgain_ref[...] = jnp.zeros_like(dgain_ref)

        mask = _row_mask(rev(pl.program_id(0)))
        mats = _cum_mats()
        _, vjp = jax.vjp(lambda *a: _gla_head(*a, mask, *mats), *_gla_pieces(u_ref), st_ref[...].astype(F32),
                         _heads(wg_ref, 0, GLA_DK), _heads(bg_ref, 0, GLA_DK), _heads(gain_ref, 0, GLA_DV))
        dq, dk, dv, dg, dz, dstate, dwg, dbg, dgain = vjp((_heads(don_ref, 0, GLA_DV), dstate_ref[...]))
        dstate_ref[...] = dstate
        for hd in range(HEADS):
            dwg_ref[:, GLA_DK * hd:GLA_DK * (hd + 1)] += dwg[hd]
            dbg_ref[:, GLA_DK * hd:GLA_DK * (hd + 1)] += dbg[hd]
            dgain_ref[:, GLA_DV * hd:GLA_DV * (hd + 1)] += dgain[hd]
        _put_heads(du_ref, 0, dq, mask)
        _put_heads(du_ref, _GLA_K0, dk, mask)
        _put_heads(du_ref, _GLA_V0, dv, mask)
        _put_heads(du_ref, _GLA_G0, dg, mask)
        du_ref[:, _GLA_Z0:] = dz.astype(BF16)

    full = lambda r, c: pl.BlockSpec((r, c), lambda n: (0, 0))
    return pl.pallas_call(
        body, name=name, grid=(nch,),
        in_specs=[pl.BlockSpec((CHUNK, GLA_U), lambda n: (rev(n), 0)), full(128, HEADS * GLA_DK),
                  full(1, HEADS * GLA_DK), full(1, HEADS * GLA_DV),
                  pl.BlockSpec((None, HEADS, GLA_DV, GLA_DK), lambda n: (rev(n), 0, 0, 0)),
                  pl.BlockSpec((CHUNK, HEADS * GLA_DV), lambda n: (rev(n), 0))],
        out_specs=[pl.BlockSpec((CHUNK, GLA_U), lambda n: (rev(n), 0)), full(128, HEADS * GLA_DK),
                   full(1, HEADS * GLA_DK), full(1, HEADS * GLA_DV)],
        out_shape=[jax.ShapeDtypeStruct((tp, GLA_U), BF16), jax.ShapeDtypeStruct((128, HEADS * GLA_DK), F32),
                   jax.ShapeDtypeStruct((1, HEADS * GLA_DK), F32), jax.ShapeDtypeStruct((1, HEADS * GLA_DV), F32)],
        scratch_shapes=[pltpu.VMEM((HEADS, GLA_DV, GLA_DK), F32)],
        compiler_params=_cp(1))(u, wg, bg, gain, states, d_on)


def _ffn_fwd(h, gain, w_in, w_out, tag):
    hn, ug, uu, act = _norm_ffn_in(h, gain, w_in, f"{tag}_in")
    if callable(w_out):
        w_out = w_out(act)
    return _out_proj(act, w_out, h, 0.5, f"{tag}_out"), (h, hn, ug, uu, act), w_out


def _ffn_dgrad(dh, w_out, w_in, act_dg, act_du, h, gain, name):
    tp, d = dh.shape
    ff = w_out.shape[0]
    tm = TM_SMALL
    nt = (((1,), (1,)), ((), ()))

    def body(dh_ref, wo_ref, wi_ref, dg_ref, du_ref, h_ref, g_ref, o_ref, dhi_ref, dgain_ref):
        @pl.when(pl.program_id(0) == 0)
        def _():
            dgain_ref[...] = jnp.zeros_like(dgain_ref)

        dho = dh_ref[...]
        dact = lax.dot_general((0.5 * dho).astype(BF16), wo_ref[...], nt, preferred_element_type=F32)
        d_gate = (dact * dg_ref[...].astype(F32)).astype(BF16)
        d_up = (dact * du_ref[...].astype(F32)).astype(BF16)
        o_ref[:, :ff] = d_gate
        o_ref[:, ff:] = d_up
        dhn = (lax.dot_general(d_gate, wi_ref[:, :ff], nt, preferred_element_type=F32)
               + lax.dot_general(d_up, wi_ref[:, ff:], nt, preferred_element_type=F32))
        dx, xhat = _rmsnorm_bwd(dhn, h_ref[...], g_ref[...])
        dgain_ref[...] += jnp.sum(dhn * xhat, axis=0, keepdims=True)
        dhi_ref[...] = dho + dx

    rows = lambda width: pl.BlockSpec((tm, width), lambda i: (i, 0))
    return pl.pallas_call(
        body, name=name, grid=(tp // tm,),
        in_specs=[rows(d), _resident(w_out.shape, 1), _resident(w_in.shape, 1), rows(ff), rows(ff), rows(d),
                  pl.BlockSpec((1, d), lambda i: (0, 0))],
        out_specs=[rows(2 * ff), rows(d), pl.BlockSpec((1, d), lambda i: (0, 0))],
        out_shape=[jax.ShapeDtypeStruct((tp, 2 * ff), BF16), jax.ShapeDtypeStruct((tp, d), F32),
                   jax.ShapeDtypeStruct((1, d), F32)],
        compiler_params=_cp(1))(dh, w_out, w_in, act_dg, act_du, h, gain)


def _ffn_bwd(dh, saved, gain, w_in, w_out, tag, push):
    h, hn, act_dg, act_du, act = saved
    du, dh_in, d_gain = _ffn_dgrad(dh, w_out, w_in, act_dg, act_du, h, gain, f"{tag}_dgrad")
    d_w_out = _wgrad(act, dh, bm=D_FF // 2, bn=D, scale=0.5, sharded=False, name=f"{tag}_dwout")
    d_w_in = _wgrad(hn, du, bm=D, bn=D_FF, scale=1.0, sharded=False, name=f"{tag}_dwin")
    return dh_in, d_gain, push([("cols", d_w_in), d_w_out])


def _sequence_grads(x, target, p, weights, grads):
    row = lambda v, token: v.reshape(1, -1) + token[0, 0]
    gains = {}

    tok = weights.start(1, weights.start(0, None))
    weights.pin = tok
    h = jnp.concatenate([jnp.zeros((FRONT, D), F32), x], axis=0) + tok[0, 0]
    rope = _rope_tables(h.shape[0])
    w = weights.wait(0, [tok, h, *rope, *weights.later_shards(2)])
    tok = weights.start(2, w["l0_ffn1_in"])
    h = lax.dynamic_update_slice(h, w["meta"], (FRONT - N_META, 0))
    gains["l0_ffn1"] = row(p["norm_ffn1"][0], tok)
    h, s1, w["l0_ffn1_out"] = _ffn_fwd(h, gains["l0_ffn1"], w["l0_ffn1_in"],
                                       lambda act: weights.wait(1, act)["l0_ffn1_out"], "l0_ffn1")
    w.update(weights.wait(2, h))
    tok = weights.start(3, w["ret_in"])
    gains["ret"] = row(p["norm_mix"][0], tok)
    hn, u = _norm_proj(h, gains["ret"], w["ret_in"], "ret_in")
    on, states = _ret_fwd(u, w["ret_gain"], rope, "ret_fwd")
    w.update(weights.wait(3, on))
    tok = weights.start(4, w["ret_out"])
    h_mix = _out_proj(on, w["ret_out"], h, 1.0, "ret_out")
    s2 = (h, hn, u, on, states)
    gains["l0_ffn2"] = row(p["norm_ffn2"][0], tok)
    h, s3, _ = _ffn_fwd(h_mix, gains["l0_ffn2"], w["l0_ffn2_in"], w["l0_ffn2_out"], "l0_ffn2")
    saved = [(s1, s2, s3)]

    w.update(weights.wait(4, h))
    tok = weights.start(5, w["l1_ffn1_in"])
    gains["l1_ffn1"] = row(p["norm_ffn1"][1], tok)
    h, s1, _ = _ffn_fwd(h, gains["l1_ffn1"], w["l1_ffn1_in"], w["l1_ffn1_out"], "l1_ffn1")
    w.update(weights.wait(5, h))
    tok = weights.start(6, w["gla_out"])
    gains["gla"] = row(p["norm_mix"][1], tok)
    hn, u = _norm_proj(h, gains["gla"], w["gla_in"], "gla_in")
    on, states = _gla_fwd(u, w["gla_wg"], w["gla_bg"], w["gla_gain"], "gla_fwd")
    h_mix = _out_proj(on, w["gla_out"], h, 1.0, "gla_out")
    s2 = (h, hn, u, on, states)
    w.update(weights.wait(6, h_mix))
    gains["l1_ffn2"] = p["norm_ffn2"][1].reshape(1, -1)
    h, s3, _ = _ffn_fwd(h_mix, gains["l1_ffn2"], w["l1_ffn2_in"], w["l1_ffn2_out"], "l1_ffn2")
    saved.append((s1, s2, s3))

    dh, d_final, loss = _loss_head(h, p["final_norm"].reshape(1, -1), target, "loss_head")
    small = {"final_norm": d_final, "norm_ffn1": [None, None], "norm_mix": [None, None], "norm_ffn2": [None, None]}
    pusher = lambda k: functools.partial(grads.push, k)

    s1, s2, s3 = saved[1]
    dh, small["norm_ffn2"][1], tok = _ffn_bwd(dh, s3, gains["l1_ffn2"], w["l1_ffn2_in"], w["l1_ffn2_out"], "l1_ffn2",
                                              pusher(0))
    h_in, hn, u, on, states = s2
    d_on = _dgrad(dh, w["gla_out"], "gla_don")
    d_out = _wgrad(on, dh, bm=D, bn=D, scale=1.0, sharded=False, name="gla_dwout")
    du, small["gla_wg"], small["gla_bg"], small["gla_gain"] = _gla_bwd(
        u, w["gla_wg"], w["gla_bg"], w["gla_gain"] + tok[0, 0], states, d_on, "gla_bwd")
    d_in = _wgrad(hn, du, bm=D, bn=GLA_U // 5, scale=1.0, sharded=False, name="gla_dwin")
    d_in = jnp.moveaxis(d_in[:, :GLA_IN].reshape(D, N_CHIPS, -1), 1, 0)
    tok = grads.push(1, [d_in, d_out])
    dh, small["norm_mix"][1] = _dgrad_norm(du, w["gla_in"], h_in, gains["gla"] + tok[0, 0], dh, "gla_dnorm")
    dh, small["norm_ffn1"][1], tok = _ffn_bwd(dh, s1, gains["l1_ffn1"], w["l1_ffn1_in"], w["l1_ffn1_out"], "l1_ffn1",
                                              pusher(2))

    s1, s2, s3 = saved[0]
    dh, small["norm_ffn2"][0], tok = _ffn_bwd(dh, s3, gains["l0_ffn2"] + tok[0, 0], w["l0_ffn2_in"],
                                              w["l0_ffn2_out"], "l0_ffn2", pusher(3))
    h_in, hn, u, on, states = s2
    d_on = _dgrad(dh, w["ret_out"], "ret_don")
    d_out = _wgrad(on, dh, bm=D, bn=D, scale=1.0, sharded=False, name="ret_dwout")
    du, small["ret_gain"] = _ret_bwd(u, w["ret_gain"] + tok[0, 0], rope, states, d_on, "ret_bwd")
    d_in = _wgrad(hn, du, bm=D, bn=w["ret_in"].shape[2], scale=1.0, sharded=True, name="ret_dwin")
    tok = grads.push(4, [d_in, d_out])
    dh, small["norm_mix"][0] = _dgrad_norm(du, w["ret_in"], h_in, gains["ret"] + tok[0, 0], dh, "ret_dnorm")
    dh, small["norm_ffn1"][0], _ = _ffn_bwd(dh, s1, gains["l0_ffn1"], w["l0_ffn1_in"], w["l0_ffn1_out"], "l0_ffn1",
                                            pusher(5))
    grads.push(6, [], [dh[FRONT - N_META:FRONT], *small["norm_ffn1"], *small["norm_mix"], *small["norm_ffn2"],
                       small["final_norm"], small["ret_gain"], small["gla_wg"][:GLA_RANK], small["gla_bg"],
                       small["gla_gain"], loss[:, :1]])
    return dh[FRONT:]


_HBM = pl.BlockSpec(memory_space=pl.ANY)


def _place():
    return lax.axis_index("x"), lax.axis_index("y"), lax.axis_index("c")


def _flip(v, bit):
    return 1 - v if bit else v


DMA_CHUNK_BYTES = 128 * 1024


def _row_chunks(ref):
    rows, cols = ref.shape
    step = _row_tile(rows, max(16, DMA_CHUNK_BYTES // (cols * ref.dtype.itemsize)))
    return [pl.ds(a, step) for a in range(0, rows, step)]


def _whole(src, dst, send_sem, recv_sem, peer):
    return pltpu.make_async_remote_copy(src_ref=src, dst_ref=dst, send_sem=send_sem, recv_sem=recv_sem,
                                        device_id=peer, device_id_type=MESH)


def _send(src, dst, send_sem, recv_sem, peer):
    for rows in _row_chunks(src):
        _whole(src.at[rows], dst.at[rows], send_sem, recv_sem, peer).start()
    return _whole(src, dst, send_sem, recv_sem, peer)


_HBM_ONLY = pl.BlockSpec(memory_space=pltpu.HBM)
_SEMS = pl.BlockSpec(memory_space=pltpu.SEMAPHORE)
_SIDE_EFFECT = pltpu.CompilerParams(has_side_effects=pltpu.SideEffectType.DATAFLOW_SIDE_EFFECTING)
_GATHER_FLIPS = [(1, 0, 0), (0, 1, 0), (1, 1, 0), (0, 0, 1)]
_PEER_FLIPS = [(fx, fy, fc) for fx in (0, 1) for fy in (0, 1) for fc in (0, 1)][1:]


def _zero_token():
    return jnp.zeros((8, 128), F32)


def _exchange_start(srcs, lands, route, flips, after, name):
    n = len(srcs)

    def body(*refs):
        src, land = refs[:n], refs[n:2 * n]
        send_sems, recv_sems, token = refs[2 * n + 1], refs[2 * n + 2], refs[-1]
        me = _place()
        for t in range(n):
            for j, flip in enumerate(flips):
                peer = tuple(_flip(v, f) for v, f in zip(me, flip))
                s, d = route(t, src[t], land[t], me, peer)
                _send(s, d, send_sems.at[t * len(flips) + j], recv_sems.at[t * len(flips) + j], peer)
        token[...] = jnp.zeros_like(token)

    hbm = lambda a: pltpu.HBM(a.shape, a.dtype)
    sems = pltpu.SemaphoreType.DMA((n * len(flips),))
    operands = [pltpu.with_memory_space_constraint(a, pltpu.HBM) for a in list(srcs) + list(lands)]
    out = pl.pallas_call(
        body, name=name, in_specs=[_HBM_ONLY] * (2 * n) + [_HBM],
        out_shape=(sems, sems, *[hbm(a) for a in operands], jax.ShapeDtypeStruct((8, 128), F32)),
        out_specs=(_SEMS, _SEMS, *[_HBM_ONLY] * (2 * n), pl.BlockSpec(memory_space=pltpu.VMEM)),
        input_output_aliases={i: 2 + i for i in range(2 * n)}, compiler_params=_SIDE_EFFECT,
    )(*operands, _zero_token() if after is None else after)
    return (out[0], out[1], out[2:2 + n], out[2 + n:2 + 2 * n]), out[-1]


def _exchange_wait(started, route, flips, after, name):
    send_sems, recv_sems, srcs, lands = started
    n = len(srcs)

    def body(*refs):
        src, land = refs[:n], refs[n:2 * n]
        send_sems, recv_sems = refs[2 * n], refs[2 * n + 1]
        me = _place()
        for t in range(n):
            for j, flip in enumerate(flips):
                peer = tuple(_flip(v, f) for v, f in zip(me, flip))
                s, d = route(t, src[t], land[t], me, peer)
                cp = _whole(s, d, send_sems.at[t * len(flips) + j], recv_sems.at[t * len(flips) + j], peer)
                cp.wait_send()
                cp.wait_recv()

    hbm = lambda a: pltpu.HBM(a.shape, a.dtype)
    after = list(after) if isinstance(after, (list, tuple)) else [after]
    out = pl.pallas_call(
        body, name=name, in_specs=[_HBM_ONLY] * (2 * n) + [_SEMS, _SEMS] + [_HBM] * len(after),
        out_shape=tuple(hbm(a) for a in list(srcs) + list(lands)), out_specs=tuple([_HBM_ONLY] * (2 * n)),
        input_output_aliases={i: i for i in range(2 * n)}, compiler_params=_SIDE_EFFECT,
    )(*srcs, *lands, send_sems, recv_sems, *after)
    return out[:n], out[n:]


def _gather_route(t, src, land, me, peer):
    mine = 2 * me[0] + me[1]
    if land.ndim == 3:
        return src, land.at[mine]
    cols = src.shape[1]
    return src, land.at[:, pl.ds(pl.multiple_of(mine * cols, 128), cols)]


def _scatter_route(n_pieces):
    def route(t, src, land, me, peer):
        chip = 2 * peer[0] + peer[1]
        if t >= n_pieces:
            part = src
        elif src.ndim == 4:
            part = src.at[chip, peer[2]]
        else:
            rows, cols = land.shape[1:]
            part = src.at[pl.ds(pl.multiple_of(peer[2] * rows, 16), rows), pl.ds(pl.multiple_of(chip * cols, 128), cols)]
        return part, land.at[4 * me[0] + 2 * me[1] + me[2]]

    return route


def _swap_cores(halves, name):
    n = len(halves)

    def body(*refs):
        src, dst = refs[:n], refs[n:2 * n]
        send_sems, recv_sems = refs[2 * n:]
        x, y, c = _place()
        copies = [_send(src[t], dst[t], send_sems.at[t], recv_sems.at[t], (x, y, 1 - c)) for t in range(n)]
        for cp in copies:
            cp.wait()

    got = pl.pallas_call(
        body, name=name, in_specs=[_HBM] * n, out_specs=[_HBM] * n,
        out_shape=[jax.ShapeDtypeStruct(a.shape, a.dtype) for a in halves],
        scratch_shapes=[pltpu.SemaphoreType.DMA((n,)), pltpu.SemaphoreType.DMA((n,))],
    )(*halves)
    south = lax.axis_index("c") == 0
    return [jnp.stack([jnp.where(south, a, b), jnp.where(south, b, a)]) for a, b in zip(halves, got)]


def _row_tile(rows, cap):
    fits = [t for t in range(16, cap + 1, 16) if rows % t == 0]
    return fits[-1] if fits else rows


def _sum_slots(a, name):
    _, r, c = a.shape
    tr = _row_tile(r, 384)

    def body(a_ref, o_ref):
        s = a_ref[0].astype(F32)
        for k in range(1, N_DEV):
            s = s + a_ref[k].astype(F32)
        o_ref[...] = s

    return pl.pallas_call(
        body, name=name, grid=(r // tr,),
        in_specs=[pl.BlockSpec((N_DEV, tr, c), lambda i: (0, i, 0))],
        out_specs=pl.BlockSpec((tr, c), lambda i: (i, 0)),
        out_shape=jax.ShapeDtypeStruct((r, c), F32),
        compiler_params=_cp(1))(a)


def _adamw(w, g, m, v, name):
    layers, r, c = w.shape
    tr = _row_tile(r, 256)

    def body(w_ref, g_ref, m_ref, v_ref, d_ref, nm_ref, nv_ref):
        gv = g_ref[...]
        nm = ADAM_B1 * m_ref[...] + (1.0 - ADAM_B1) * gv
        nv = ADAM_B2 * v_ref[...] + (1.0 - ADAM_B2) * (gv * gv)
        m_hat = nm / (1.0 - ADAM_B1 ** ADAM_STEP)
        v_hat = nv / (1.0 - ADAM_B2 ** ADAM_STEP)
        d_ref[...] = -ADAM_LR * (m_hat / (jnp.sqrt(v_hat) + ADAM_EPS) + ADAM_WD * w_ref[...])
        nm_ref[...] = nm
        nv_ref[...] = nv

    spec = pl.BlockSpec((None, tr, c), lambda a, i: (a, i, 0))
    return pl.pallas_call(
        body, name=name, grid=(layers, r // tr), in_specs=[spec] * 4, out_specs=[spec] * 3,
        out_shape=[jax.ShapeDtypeStruct((layers, r, c), F32)] * 3,
        compiler_params=_cp(2))(w, g, m, v)


_SMALL = ["meta_tokens", "ret_head_norm", "gla_w_gate", "gla_b_gate", "gla_head_norm"]
_LOCAL_SMALL = ["meta_tokens", "norm_ffn1", "norm_mix", "norm_ffn2", "ret_head_norm", "gla_w_gate", "gla_b_gate",
                "gla_head_norm", "final_norm"]
_BIG = ["ffn1_w_in", "ffn1_w_out", "ffn2_w_in", "ffn2_w_out", "ret_w_in", "ret_w_out", "gla_w_in", "gla_w_out"]
_WEIGHTS = ["meta_tokens", "norm_ffn1", "ffn1_w_in", "ffn1_w_out", "norm_mix", "norm_ffn2", "ffn2_w_in", "ffn2_w_out",
            "ret_w_in", "ret_head_norm", "ret_w_out", "gla_w_in", "gla_w_gate", "gla_b_gate", "gla_head_norm",
            "gla_w_out", "final_norm"]


def _pack_rows(arrays, width):
    flat = jnp.concatenate([a.reshape(-1) for a in arrays])
    pad = -flat.shape[0] % (8 * width)
    return jnp.pad(flat, (0, pad)).reshape(-1, width)


def _unpack_rows(packed, shapes):
    flat, out, at = packed.reshape(-1), [], 0
    for s in shapes:
        size = 1
        for dim in s:
            size *= dim
        out.append(flat[at:at + size].reshape(s))
        at += size
    return out


class _WeightGather:
    GROUPS = [("small", "l0_ffn1_in"), ("l0_ffn1_out",), ("ret_in",), ("ret_out", "l0_ffn2_in", "l0_ffn2_out"),
              ("l1_ffn1_in", "l1_ffn1_out"), ("gla_in", "gla_out"), ("l1_ffn2_in", "l1_ffn2_out")]

    def __init__(self, p):
        self.small_shapes = [p[name].shape for name in _SMALL]
        self.f32 = {"small": _pack_rows([p[name] for name in _SMALL], 128), "ret_in": p["ret_w_in"][0],
                    "ret_out": p["ret_w_out"][0], "gla_in": p["gla_w_in"][0], "gla_out": p["gla_w_out"][0]}
        for layer in range(2):
            for name in ("ffn1", "ffn2"):
                self.f32[f"l{layer}_{name}_in"] = p[f"{name}_w_in"][layer]
                self.f32[f"l{layer}_{name}_out"] = p[f"{name}_w_out"][layer]
        self.shards = {}
        self.started = {}
        self.pin = None

    def shard(self, name):
        if name not in self.shards:
            a = self.f32[name]
            if name != "small":
                a = (a if self.pin is None else a + self.pin[0, 0]).astype(BF16)
            self.shards[name] = a
        return self.shards[name]

    def later_shards(self, k):
        return [self.shard(name) for group in self.GROUPS[k:] for name in group]

    def start(self, k, after):
        shards = [self.shard(name) for name in self.GROUPS[k]]
        lands = []
        for name, s in zip(self.GROUPS[k], shards):
            if "ffn" in name and name.endswith("_in"):
                lands.append(lax.empty((s.shape[0], N_CHIPS * s.shape[1]), s.dtype))
            else:
                lands.append(lax.empty((N_CHIPS,) + s.shape, s.dtype))
        self.started[k], token = _exchange_start(shards, lands, _gather_route, _GATHER_FLIPS, after, f"gather{k}_start")
        return token

    def wait(self, k, after):
        _, got = _exchange_wait(self.started[k], _gather_route, _GATHER_FLIPS, after, f"gather{k}_wait")
        w = {}
        for name, g in zip(self.GROUPS[k], got):
            if name == "small":
                parts = zip(*[_unpack_rows(g[chip], self.small_shapes) for chip in range(N_CHIPS)])
                cat = lambda a: jnp.moveaxis(a, 0, -2).reshape(a.shape[1:-1] + (-1,))
                meta, ret_gain, wg, bg, gla_gain = [cat(jnp.stack(part)) for part in parts]
                w.update(meta=meta, ret_gain=ret_gain.reshape(1, -1), gla_bg=bg.reshape(1, -1),
                         gla_gain=gla_gain.reshape(1, -1),
                         gla_wg=jnp.pad(wg[0], ((0, 128 - GLA_RANK), (0, 0))).astype(BF16))
            elif name == "gla_in":
                full = jnp.moveaxis(g, 0, 1).reshape(D, -1)
                w[name] = jnp.pad(full, ((0, 0), (0, GLA_U - GLA_IN)))[None]
            elif name.endswith("_out"):
                w[name] = g.reshape(-1, g.shape[-1])
            else:
                w[name] = g
        return w


class _GradExchange:
    def __init__(self):
        self.started = []
        self.token = None
        self.small_shapes = None

    def push(self, k, arrays, small=None):
        srcs, lands = [], []
        for a in arrays:
            if isinstance(a, tuple):
                a = a[1]
                piece = (a.shape[0] // 2, a.shape[1] // N_CHIPS)
            else:
                a = a.reshape(N_CHIPS, 2, -1, a.shape[-1])
                piece = a.shape[2:]
            srcs.append(a)
            lands.append(lax.empty((N_DEV,) + piece, a.dtype))
        if small is not None:
            self.small_shapes = [a.shape for a in small]
            srcs.append(_pack_rows(small, D))
            lands.append(lax.empty((N_DEV,) + srcs[-1].shape, F32))
        started, self.token = _exchange_start(srcs, lands, _scatter_route(len(arrays)), _PEER_FLIPS, None,
                                              f"scatter{k}_start")
        self.started.append((started, len(arrays)))
        return self.token

    def collect(self, groups, after=None):
        x, y, c = _place()
        after, sums = self.token if after is None else after, []
        for k in groups:
            started, n_pieces = self.started[k]
            srcs, got = _exchange_wait(started, _scatter_route(n_pieces), _PEER_FLIPS, after, f"scatter{k}_wait")
            own = []
            for t, (a, g) in enumerate(zip(srcs, got)):
                if t >= n_pieces:
                    own.append(a)
                elif a.ndim == 4:
                    own.append(a[2 * x + y, c])
                else:
                    rows, cols = g.shape[1:]
                    own.append(lax.dynamic_slice(a, (c * rows, (2 * x + y) * cols), (rows, cols)))
            got = [lax.dynamic_update_index_in_dim(g, a, 4 * x + 2 * y + c, 0) for g, a in zip(got, own)]
            sums.append([_sum_slots(a, f"sum{k}_{i}") for i, a in enumerate(got)])
            after = sums[-1][0]
        return sums


def kernel(x, meta_tokens, norm_ffn1, ffn1_w_in, ffn1_w_out, norm_mix, norm_ffn2, ffn2_w_in, ffn2_w_out, ret_w_in, ret_head_norm, ret_w_out, gla_w_in, gla_w_gate, gla_b_gate, gla_head_norm, gla_w_out, final_norm, loss_target, m_meta_tokens, m_norm_ffn1, m_ffn1_w_in, m_ffn1_w_out, m_norm_mix, m_norm_ffn2, m_ffn2_w_in, m_ffn2_w_out, m_ret_w_in, m_ret_head_norm, m_ret_w_out, m_gla_w_in, m_gla_w_gate, m_gla_b_gate, m_gla_head_norm, m_gla_w_out, m_final_norm, v_meta_tokens, v_norm_ffn1, v_ffn1_w_in, v_ffn1_w_out, v_norm_mix, v_norm_ffn2, v_ffn2_w_in, v_ffn2_w_out, v_ret_w_in, v_ret_head_norm, v_ret_w_out, v_gla_w_in, v_gla_w_gate, v_gla_b_gate, v_gla_head_norm, v_gla_w_out, v_final_norm):
    p = dict(meta_tokens=meta_tokens, norm_ffn1=norm_ffn1, ffn1_w_in=ffn1_w_in, ffn1_w_out=ffn1_w_out, norm_mix=norm_mix,
             norm_ffn2=norm_ffn2, ffn2_w_in=ffn2_w_in, ffn2_w_out=ffn2_w_out, ret_w_in=ret_w_in,
             ret_head_norm=ret_head_norm, ret_w_out=ret_w_out, gla_w_in=gla_w_in, gla_w_gate=gla_w_gate,
             gla_b_gate=gla_b_gate, gla_head_norm=gla_head_norm, gla_w_out=gla_w_out, final_norm=final_norm)
    m = dict(zip(_WEIGHTS, (m_meta_tokens, m_norm_ffn1, m_ffn1_w_in, m_ffn1_w_out, m_norm_mix, m_norm_ffn2, m_ffn2_w_in,
                            m_ffn2_w_out, m_ret_w_in, m_ret_head_norm, m_ret_w_out, m_gla_w_in, m_gla_w_gate,
                            m_gla_b_gate, m_gla_head_norm, m_gla_w_out, m_final_norm)))
    v = dict(zip(_WEIGHTS, (v_meta_tokens, v_norm_ffn1, v_ffn1_w_in, v_ffn1_w_out, v_norm_mix, v_norm_ffn2, v_ffn2_w_in,
                            v_ffn2_w_out, v_ret_w_in, v_ret_head_norm, v_ret_w_out, v_gla_w_in, v_gla_w_gate,
                            v_gla_b_gate, v_gla_head_norm, v_gla_w_out, v_final_norm)))

    exchange = _GradExchange()
    d_x = _sequence_grads(x[0], loss_target[0], p, _WeightGather(p), exchange)
    names = [("ffn2_w_in", 1), ("ffn2_w_out", 1), ("gla_w_in", 0), ("gla_w_out", 0), ("ffn1_w_in", 1), ("ffn1_w_out", 1),
             ("ffn2_w_in", 0), ("ffn2_w_out", 0), ("ret_w_in", 0), ("ret_w_out", 0), ("ffn1_w_in", 0), ("ffn1_w_out", 0)]
    shard, grads, delta, new_m, new_v = {}, {}, {}, {}, {}

    def swap(sums, keys, name):
        for key, a in zip(keys, _swap_cores(sums, name)):
            shard[key] = a.reshape(-1, a.shape[-1])

    def update(name):
        layers = p[name].shape[0]
        grads[name] = jnp.stack([shard[name, layer] for layer in range(layers)])
        delta[name], new_m[name], new_v[name] = _adamw(p[name], grads[name], m[name], v[name], f"adamw_{name}")

    swap([a for group in exchange.collect(range(5)) for a in group], names[:10], "swap_first")
    for name in ("ffn2_w_in", "ffn2_w_out", "ret_w_in", "ret_w_out", "gla_w_in", "gla_w_out"):
        update(name)
    last, (small_sum,) = exchange.collect([5, 6], after=delta["gla_w_out"])
    swap(last, names[10:], "swap_last")
    for name in ("ffn1_w_in", "ffn1_w_out"):
        update(name)

    chip = 2 * lax.axis_index("x") + lax.axis_index("y")
    cols = lambda a, n: lax.dynamic_slice_in_dim(a, chip * n, n, axis=a.ndim - 1)
    (s_meta, s_n1a, s_n1b, s_nma, s_nmb, s_n2a, s_n2b, s_final, s_ret_gain, s_wg, s_bg, s_gla_gain,
     s_loss) = _unpack_rows(small_sum, exchange.small_shapes)
    grads.update({
        "meta_tokens": cols(s_meta, 256), "norm_ffn1": jnp.concatenate([s_n1a, s_n1b]),
        "norm_mix": jnp.concatenate([s_nma, s_nmb]), "norm_ffn2": jnp.concatenate([s_n2a, s_n2b]),
        "final_norm": s_final.reshape(D),
        "ret_head_norm": cols(s_ret_gain.reshape(1, HEADS, RET_DV), RET_DV // N_CHIPS),
        "gla_w_gate": cols(s_wg, GLA_DK)[None], "gla_b_gate": cols(s_bg, GLA_DK),
        "gla_head_norm": cols(s_gla_gain.reshape(1, HEADS, GLA_DV), GLA_DV // N_CHIPS),
    })
    packed = [_pack_rows([d[name] for name in _LOCAL_SMALL], 128)[None] for d in (p, grads, m, v)]
    out = _adamw(*packed, "adamw_small")
    shapes = [p[name].shape for name in _LOCAL_SMALL]
    for d, a in zip((delta, new_m, new_v), out):
        d.update(zip(_LOCAL_SMALL, _unpack_rows(a, shapes)))

    return (s_loss.reshape(()), d_x[None], *[grads[n] for n in _WEIGHTS], *[delta[n] for n in _WEIGHTS],
            *[new_m[n] for n in _WEIGHTS], *[new_v[n] for n in _WEIGHTS])
```

```python
import functools

import jax
import jax.numpy as jnp
from jax import lax
from jax.experimental import pallas as pl
from jax.experimental.pallas import tpu as pltpu

F32, BF16 = jnp.float32, jnp.bfloat16
MESH = pl.DeviceIdType.MESH

D = 1024
N_META = 16
CHUNK = 64
RET_CHUNK = 128
FRONT = 256
D_FF = 2816
EPS = 1e-6
HEADS = 4
RET_DK, RET_DV = 256, 512
GLA_DK, GLA_DV = 128, 256
GLA_RANK = 16
GLA_TAU = 16.0
GLA_IN = 2 * HEADS * GLA_DK + 2 * HEADS * GLA_DV + GLA_RANK
GLA_U = 3200
ROPE_BASE = 10000.0
N_CHIPS = 4
N_DEV = 8

ADAM_LR, ADAM_B1, ADAM_B2, ADAM_EPS, ADAM_WD, ADAM_STEP = 0.001, 0.9, 0.999, 1e-08, 0.01, 10

VMEM_LIMIT_BYTES = 56 * 1024 * 1024
TM = 768
TM_SMALL = 256


TM_RESIDENT = 384
MXU_TILE = 256


def _cp(n_axes):
    return pltpu.CompilerParams(dimension_semantics=("arbitrary",) * n_axes, vmem_limit_bytes=VMEM_LIMIT_BYTES)


def _resident(shape, n_axes):
    zeros = (0,) * len(shape)
    index = (lambda i: zeros) if n_axes == 1 else (lambda i, j: zeros)
    return pl.BlockSpec(shape, index, pipeline_mode=pl.Buffered(1))


def _dg(a, b, ca, cb):
    nb = a.ndim - 2
    dims = (((ca + nb,), (cb + nb,)), (tuple(range(nb)), tuple(range(nb))))
    return lax.dot_general(a.astype(BF16), b.astype(BF16), dims, preferred_element_type=F32)


@jax.custom_vjp
def _nn(a, b):
    return _dg(a, b, 1, 0)


@jax.custom_vjp
def _nt(a, b):
    return _dg(a, b, 1, 1)


@jax.custom_vjp
def _tn(a, b):
    return _dg(a, b, 0, 0)


def _dot_vjp(fn, ca, cb, da, db):
    def fwd(a, b):
        a, b = a.astype(BF16), b.astype(BF16)
        return _dg(a, b, ca, cb), (a, b)

    def bwd(res, g):
        a, b = res
        g = g.astype(BF16)
        grad = lambda other, dims, g_first: _dg(g, other, *dims) if g_first else _dg(other, g, *dims)
        return grad(b, *da), grad(a, *db)

    fn.defvjp(fwd, bwd)


_dot_vjp(_nn, 1, 0, ((1, 1), True), ((0, 0), False))
_dot_vjp(_nt, 1, 1, ((1, 0), True), ((0, 0), True))
_dot_vjp(_tn, 0, 0, ((1, 1), False), ((1, 0), False))


def _split3_dot(m, a):
    a1 = a.astype(BF16)
    r1 = a - a1.astype(F32)
    a2 = r1.astype(BF16)
    a3 = (r1 - a2.astype(F32)).astype(BF16)
    mb = jnp.broadcast_to(m, a.shape[:-2] + m.shape)
    return _dg(mb, a1, 1, 0) + _dg(mb, a2, 1, 0) + _dg(mb, a3, 1, 0)


@jax.custom_vjp
def _cum(m, mt, a):
    return _split3_dot(m, a)


_cum.defvjp(lambda m, mt, a: (_split3_dot(m, a), (m, mt)),
            lambda res, g: (jnp.zeros_like(res[0]), jnp.zeros_like(res[1]), _split3_dot(res[1], g)))


def _sigmoid(x):
    return 1.0 / (1.0 + jnp.exp(-x))


def _rms(x):
    return lax.rsqrt(jnp.mean(x * x, axis=-1, keepdims=True) + EPS)


def _rmsnorm_bwd(dy, x, gain):
    r = _rms(x)
    xhat = x * r
    dxh = dy * gain
    return r * (dxh - xhat * jnp.mean(dxh * xhat, axis=-1, keepdims=True)), xhat


def _norm_proj(h, gain, w, name):
    tp, d = h.shape
    s, _, ns = w.shape

    tm = TM_RESIDENT

    def body(h_ref, g_ref, w_ref, hn_ref, u_ref):
        @pl.when(pl.program_id(1) == 0)
        def _():
            x = h_ref[...]
            hn_ref[...] = (x * _rms(x) * g_ref[...]).astype(BF16)

        u_ref[...] = jnp.dot(hn_ref[...], w_ref[pl.program_id(1)], preferred_element_type=F32).astype(BF16)

    return pl.pallas_call(
        body, name=name, grid=(tp // tm, s),
        in_specs=[pl.BlockSpec((tm, d), lambda i, j: (i, 0)), pl.BlockSpec((1, d), lambda i, j: (0, 0)),
                  _resident(w.shape, 2)],
        out_specs=[pl.BlockSpec((tm, d), lambda i, j: (i, 0)), pl.BlockSpec((tm, ns), lambda i, j: (i, j))],
        out_shape=[jax.ShapeDtypeStruct((tp, d), BF16), jax.ShapeDtypeStruct((tp, s * ns), BF16)],
        compiler_params=_cp(2))(h, gain, w)


def _norm_ffn_in(h, gain, w, name):
    tp, d = h.shape
    ff = w.shape[1] // 2
    tm = TM_RESIDENT
    blocks = [(c, min(c + 6 * MXU_TILE, ff)) for c in range(0, ff, 6 * MXU_TILE)]

    def body(h_ref, g_ref, w_ref, hn_ref, dg_ref, du_ref, act_ref):
        x = h_ref[...]
        a = (x * _rms(x) * g_ref[...]).astype(BF16)
        hn_ref[...] = a
        for c0, c1 in blocks:
            g = jnp.dot(a, w_ref[:, c0:c1], preferred_element_type=F32)
            u = jnp.dot(a, w_ref[:, ff + c0:ff + c1], preferred_element_type=F32)
            sg = _sigmoid(g)
            silu = g * sg
            dg_ref[:, c0:c1] = (u * (sg + silu * (1.0 - sg))).astype(BF16)
            du_ref[:, c0:c1] = silu.astype(BF16)
            act_ref[:, c0:c1] = (silu * u).astype(BF16)

    wide = jax.ShapeDtypeStruct((tp, ff), BF16)
    return pl.pallas_call(
        body, name=name, grid=(tp // tm,),
        in_specs=[pl.BlockSpec((tm, d), lambda i: (i, 0)), pl.BlockSpec((1, d), lambda i: (0, 0)),
                  _resident(w.shape, 1)],
        out_specs=[pl.BlockSpec((tm, d), lambda i: (i, 0))] + [pl.BlockSpec((tm, ff), lambda i: (i, 0))] * 3,
        out_shape=[jax.ShapeDtypeStruct((tp, d), BF16), wide, wide, wide],
        compiler_params=_cp(1))(h, gain, w)


def _out_proj(a, w, h, scale, name):
    tp, k = a.shape
    d = w.shape[1]

    def body(a_ref, w_ref, h_ref, o_ref):
        o_ref[...] = h_ref[...] + scale * jnp.dot(a_ref[...], w_ref[...], preferred_element_type=F32)

    return pl.pallas_call(
        body, name=name, grid=(tp // TM,),
        in_specs=[pl.BlockSpec((TM, k), lambda i: (i, 0)), pl.BlockSpec((k, d), lambda i: (0, 0)),
                  pl.BlockSpec((TM, d), lambda i: (i, 0))],
        out_specs=pl.BlockSpec((TM, d), lambda i: (i, 0)),
        out_shape=jax.ShapeDtypeStruct((tp, d), F32),
        compiler_params=_cp(1))(a, w, h)


def _dgrad(dh, w, name):
    tp, d = dh.shape
    k = w.shape[0]

    def body(dh_ref, w_ref, o_ref):
        o_ref[...] = lax.dot_general(dh_ref[...].astype(BF16), w_ref[...], (((1,), (1,)), ((), ())),
                                     preferred_element_type=F32).astype(BF16)

    return pl.pallas_call(
        body, name=name, grid=(tp // TM,),
        in_specs=[pl.BlockSpec((TM, d), lambda i: (i, 0)), pl.BlockSpec((k, d), lambda i: (0, 0))],
        out_specs=pl.BlockSpec((TM, k), lambda i: (i, 0)),
        out_shape=jax.ShapeDtypeStruct((tp, k), BF16),
        compiler_params=_cp(1))(dh, w)


def _wgrad(a, b, *, bm, bn, scale, sharded, name):
    tp, m = a.shape
    n = b.shape[1]
    nk = tp // TM

    def body(a_ref, b_ref, o_ref, acc_ref):
        k = pl.program_id(2)

        @pl.when(k == 0)
        def _():
            acc_ref[...] = jnp.zeros_like(acc_ref)

        bb = b_ref[...]
        if scale != 1.0:
            bb = scale * bb
        acc_ref[...] += lax.dot_general(a_ref[...], bb.astype(BF16), (((0,), (0,)), ((), ())),
                                        preferred_element_type=F32)

        @pl.when(k == nk - 1)
        def _():
            o_ref[...] = acc_ref[...].astype(BF16)

    if sharded:
        assert m == bm
        out_spec = pl.BlockSpec((None, bm, bn), lambda i, j, k: (j, 0, 0))
        out_shape = jax.ShapeDtypeStruct((n // bn, m, bn), BF16)
    else:
        out_spec = pl.BlockSpec((bm, bn), lambda i, j, k: (i, j))
        out_shape = jax.ShapeDtypeStruct((m, n), BF16)
    return pl.pallas_call(
        body, name=name, grid=(m // bm, n // bn, nk),
        in_specs=[pl.BlockSpec((TM, bm), lambda i, j, k: (k, i)), pl.BlockSpec((TM, bn), lambda i, j, k: (k, j))],
        out_specs=out_spec, out_shape=out_shape,
        scratch_shapes=[pltpu.VMEM((bm, bn), F32)],
        compiler_params=_cp(3))(a, b)


def _dgrad_norm(du, w, h, gain, dh_out, name):
    tp, d = h.shape
    s, _, ns = w.shape
    tm = TM_RESIDENT

    def body(du_ref, w_ref, h_ref, g_ref, dho_ref, dhi_ref, dg_ref):
        @pl.when(pl.program_id(0) == 0)
        def _():
            dg_ref[...] = jnp.zeros_like(dg_ref)

        dhn = None
        for k in range(s):
            part = lax.dot_general(du_ref[:, ns * k:ns * (k + 1)], w_ref[k], (((1,), (1,)), ((), ())),
                                   preferred_element_type=F32)
            dhn = part if dhn is None else dhn + part
        dx, xhat = _rmsnorm_bwd(dhn, h_ref[...], g_ref[...])
        dg_ref[...] += jnp.sum(dhn * xhat, axis=0, keepdims=True)
        dhi_ref[...] = dho_ref[...] + dx

    return pl.pallas_call(
        body, name=name, grid=(tp // tm,),
        in_specs=[pl.BlockSpec((tm, s * ns), lambda i: (i, 0)), _resident(w.shape, 1),
                  pl.BlockSpec((tm, d), lambda i: (i, 0)), pl.BlockSpec((1, d), lambda i: (0, 0)),
                  pl.BlockSpec((tm, d), lambda i: (i, 0))],
        out_specs=[pl.BlockSpec((tm, d), lambda i: (i, 0)), pl.BlockSpec((1, d), lambda i: (0, 0))],
        out_shape=[jax.ShapeDtypeStruct((tp, d), F32), jax.ShapeDtypeStruct((1, d), F32)],
        compiler_params=_cp(1))(du, w, h, gain, dh_out)


def _loss_head(h, gain, target, name):
    tp, d = h.shape
    tm = TM_SMALL
    front_tiles = FRONT // tm

    def body(h_ref, g_ref, t_ref, dh_ref, dg_ref, loss_ref):
        i = pl.program_id(0)

        @pl.when(i == 0)
        def _():
            dg_ref[...] = jnp.zeros_like(dg_ref)
            loss_ref[...] = jnp.zeros_like(loss_ref)

        x = h_ref[...]
        gain_v = g_ref[...]
        y = x * _rms(x) * gain_v
        err = jnp.where(i >= front_tiles, y - t_ref[...], 0.0)
        loss_ref[...] += 0.5 * jnp.sum(jnp.mean(err * err, axis=-1, keepdims=True), axis=0, keepdims=True)
        dy = err * (1.0 / d)
        dx, xhat = _rmsnorm_bwd(dy, x, gain_v)
        dg_ref[...] += jnp.sum(dy * xhat, axis=0, keepdims=True)
        dh_ref[...] = dx

    return pl.pallas_call(
        body, name=name, grid=(tp // tm,),
        in_specs=[pl.BlockSpec((tm, d), lambda i: (i, 0)), pl.BlockSpec((1, d), lambda i: (0, 0)),
                  pl.BlockSpec((tm, d), lambda i: (jnp.maximum(i - front_tiles, 0), 0))],
        out_specs=[pl.BlockSpec((tm, d), lambda i: (i, 0)), pl.BlockSpec((1, d), lambda i: (0, 0)),
                   pl.BlockSpec((1, 128), lambda i: (0, 0))],
        out_shape=[jax.ShapeDtypeStruct((tp, d), F32), jax.ShapeDtypeStruct((1, d), F32),
                   jax.ShapeDtypeStruct((1, 128), F32)],
        compiler_params=_cp(1))(h, gain, target)


def _gated_headnorm(o, g, gain):
    return o * _rms(o) * gain * (g * _sigmoid(g))


def _row_mask(chunk, size=CHUNK):
    rows = chunk * size + lax.broadcasted_iota(jnp.int32, (size, 1), 0)
    return (rows >= FRONT - N_META).astype(F32)


def _ret_head(q1, q2, k1, k2, v, g, state, gain, cos, sin, dmat, dq, dk, dc):
    q = jnp.concatenate([q1 * cos - q2 * sin, q1 * sin + q2 * cos], axis=-1)
    k = jnp.concatenate([k1 * cos - k2 * sin, k1 * sin + k2 * cos], axis=-1) * (RET_DK ** -0.5)
    scores = _nt(q, k) * dmat
    o = _nn(scores, v) + _nn(q * dq, state)
    new_state = state * dc + _tn(k * dk, v)
    return _gated_headnorm(o, g, gain), new_state


def _ret_consts():
    log_gamma = jnp.log1p(-2.0 ** (-5.0 - jnp.arange(HEADS, dtype=F32)))
    idx = jnp.arange(RET_CHUNK, dtype=F32)
    rel = idx[:, None] - idx[None, :]
    dmat = jnp.where(rel >= 0, jnp.exp(log_gamma[:, None, None] * jnp.maximum(rel, 0.0)), 0.0)
    dq = jnp.exp(log_gamma[:, None] * (idx + 1.0))[..., None]
    dk = jnp.exp(log_gamma[:, None] * (RET_CHUNK - 1.0 - idx))[..., None]
    dc = jnp.broadcast_to(jnp.exp(log_gamma * RET_CHUNK)[:, None, None], (HEADS, 1, 128))
    return dmat, dq, dk, dc


def _rope_tables(tp):
    half = RET_DK // 2
    inv = 1.0 / (ROPE_BASE ** jnp.linspace(0.0, 1.0, half, dtype=F32))
    pos = (jnp.arange(tp) - (FRONT - N_META)).astype(F32)
    ang = pos[:, None] * inv[None, :]
    return jnp.cos(ang), jnp.sin(ang)


_RET_V0, _RET_G0 = 2 * D, 4 * D


def _heads(ref, start, width, stride=None):
    stride = width if stride is None else stride
    return jnp.stack([ref[:, start + stride * h:start + stride * h + width].astype(F32) for h in range(HEADS)])


def _put_heads(ref, start, value, mask, stride=None):
    width = value.shape[-1]
    stride = width if stride is None else stride
    for h in range(HEADS):
        ref[:, start + stride * h:start + stride * h + width] = (value[h] * mask).astype(ref.dtype)


def _ret_pieces(u_ref):
    hk = RET_DK // 2
    return (_heads(u_ref, 0, hk, RET_DK), _heads(u_ref, hk, hk, RET_DK), _heads(u_ref, D, hk, RET_DK),
            _heads(u_ref, D + hk, hk, RET_DK), _heads(u_ref, _RET_V0, RET_DV), _heads(u_ref, _RET_G0, RET_DV))


def _ret_const_specs(rev=None):
    c = (lambda n: (rev(n), 0)) if rev else (lambda n: (n, 0))
    z3 = lambda n: (0, 0, 0)
    return [pl.BlockSpec((RET_CHUNK, RET_DK // 2), c), pl.BlockSpec((RET_CHUNK, RET_DK // 2), c),
            pl.BlockSpec((HEADS, RET_CHUNK, RET_CHUNK), z3), pl.BlockSpec((HEADS, RET_CHUNK, 1), z3),
            pl.BlockSpec((HEADS, RET_CHUNK, 1), z3), pl.BlockSpec((HEADS, 1, 128), z3)]


def _ret_fwd(u, gain, rope, name):
    tp = u.shape[0]
    nch = tp // RET_CHUNK
    cos, sin = rope
    dmat, dq, dk, dc = _ret_consts()

    def body(u_ref, gain_ref, cos_ref, sin_ref, dmat_ref, dq_ref, dk_ref, dc_ref, on_ref, st_ref, state_ref):
        @pl.when(pl.program_id(0) == 0)
        def _():
            state_ref[...] = jnp.zeros_like(state_ref)

        state = state_ref[...]
        st_ref[...] = state.astype(BF16)
        on, new_state = _ret_head(*_ret_pieces(u_ref), state, _heads(gain_ref, 0, RET_DV), cos_ref[...], sin_ref[...],
                                  dmat_ref[...], dq_ref[...], dk_ref[...], dc_ref[...][:, :, :1])
        state_ref[...] = new_state
        _put_heads(on_ref, 0, on, 1.0)

    return pl.pallas_call(
        body, name=name, grid=(nch,),
        in_specs=[pl.BlockSpec((RET_CHUNK, 6 * D), lambda n: (n, 0)), pl.BlockSpec((1, HEADS * RET_DV), lambda n: (0, 0))]
                 + _ret_const_specs(),
        out_specs=[pl.BlockSpec((RET_CHUNK, HEADS * RET_DV), lambda n: (n, 0)),
                   pl.BlockSpec((None, HEADS, RET_DK, RET_DV), lambda n: (n, 0, 0, 0))],
        out_shape=[jax.ShapeDtypeStruct((tp, HEADS * RET_DV), BF16),
                   jax.ShapeDtypeStruct((nch, HEADS, RET_DK, RET_DV), BF16)],
        scratch_shapes=[pltpu.VMEM((HEADS, RET_DK, RET_DV), F32)],
        compiler_params=_cp(1))(u, gain, cos, sin, dmat, dq, dk, dc)


def _ret_bwd(u, gain, rope, states, d_on, name):
    tp = u.shape[0]
    nch = tp // RET_CHUNK
    cos, sin = rope
    dmat, dq, dk, dc = _ret_consts()
    rev = lambda n: nch - 1 - n
    hk = RET_DK // 2

    def body(u_ref, gain_ref, st_ref, don_ref, cos_ref, sin_ref, dmat_ref, dq_ref, dk_ref, dc_ref,
             du_ref, dgain_ref, dstate_ref):
        @pl.when(pl.program_id(0) == 0)
        def _():
            dstate_ref[...] = jnp.zeros_like(dstate_ref)
            dgain_ref[...] = jnp.zeros_like(dgain_ref)

        mask = _row_mask(rev(pl.program_id(0)), RET_CHUNK)
        consts = (cos_ref[...], sin_ref[...], dmat_ref[...], dq_ref[...], dk_ref[...], dc_ref[...][:, :, :1])
        _, vjp = jax.vjp(lambda *a: _ret_head(*a, *consts), *_ret_pieces(u_ref), st_ref[...].astype(F32),
                         _heads(gain_ref, 0, RET_DV))
        dq1, dq2, dk1, dk2, dv, dg, dstate, dgain = vjp((_heads(don_ref, 0, RET_DV), dstate_ref[...]))
        dstate_ref[...] = dstate
        for hd in range(HEADS):
            dgain_ref[:, RET_DV * hd:RET_DV * (hd + 1)] += dgain[hd]
        _put_heads(du_ref, 0, dq1, mask, RET_DK)
        _put_heads(du_ref, hk, dq2, mask, RET_DK)
        _put_heads(du_ref, D, dk1, mask, RET_DK)
        _put_heads(du_ref, D + hk, dk2, mask, RET_DK)
        _put_heads(du_ref, _RET_V0, dv, mask)
        _put_heads(du_ref, _RET_G0, dg, mask)

    return pl.pallas_call(
        body, name=name, grid=(nch,),
        in_specs=[pl.BlockSpec((RET_CHUNK, 6 * D), lambda n: (rev(n), 0)),
                  pl.BlockSpec((1, HEADS * RET_DV), lambda n: (0, 0)),
                  pl.BlockSpec((None, HEADS, RET_DK, RET_DV), lambda n: (rev(n), 0, 0, 0)),
                  pl.BlockSpec((RET_CHUNK, HEADS * RET_DV), lambda n: (rev(n), 0))] + _ret_const_specs(rev),
        out_specs=[pl.BlockSpec((RET_CHUNK, 6 * D), lambda n: (rev(n), 0)),
                   pl.BlockSpec((1, HEADS * RET_DV), lambda n: (0, 0))],
        out_shape=[jax.ShapeDtypeStruct((tp, 6 * D), BF16), jax.ShapeDtypeStruct((1, HEADS * RET_DV), F32)],
        scratch_shapes=[pltpu.VMEM((HEADS, RET_DK, RET_DV), F32)],
        compiler_params=_cp(1))(u, gain, states, d_on, cos, sin, dmat, dq, dk, dc)


_GLA_K0, _GLA_V0, _GLA_G0, _GLA_Z0 = 512, 1024, 2048, 3072


def _gla_head(q, k, v, g, z, state_t, wg, bg, gain, mask, lo, lo_t, loc, loc_t):
    ga = _nn(jnp.broadcast_to(z, wg.shape[:-2] + z.shape), wg) + bg
    log_a = (jnp.minimum(ga, 0.0) - jnp.log(1.0 + jnp.exp(-jnp.abs(ga)))) * (mask * (1.0 / GLA_TAU))
    bcum = _cum(lo, lo_t, log_a)
    bmid = _cum(loc, loc_t, log_a)
    btot = jnp.sum(log_a, axis=-2, keepdims=True)
    qs = q * (GLA_DK ** -0.5)
    causal = lax.broadcasted_iota(jnp.int32, (CHUNK, CHUNK), 0) >= lax.broadcasted_iota(jnp.int32, (CHUNK, CHUNK), 1)
    scores = jnp.where(causal, _nt(qs * jnp.exp(bmid), k * jnp.exp(-bmid)), 0.0)
    o = _nn(scores, v) + _nt(qs * jnp.exp(bcum), state_t)
    new_state_t = state_t * jnp.exp(btot) + _tn(v, k * jnp.exp(btot - bcum))
    return _gated_headnorm(o, g, gain), new_state_t


def _cum_mats():
    r = lax.broadcasted_iota(jnp.int32, (CHUNK, CHUNK), 0)
    c = lax.broadcasted_iota(jnp.int32, (CHUNK, CHUNK), 1)
    mid = CHUNK // 2
    low = lambda a, b: (a >= b).astype(F32)
    lo, lo_t = low(r, c), low(c, r)
    loc = lo - (c <= mid).astype(F32)
    loc_t = lo_t - (r <= mid).astype(F32)
    return tuple(m.astype(BF16) for m in (lo, lo_t, loc, loc_t))


def _gla_pieces(u_ref):
    return (_heads(u_ref, 0, GLA_DK), _heads(u_ref, _GLA_K0, GLA_DK), _heads(u_ref, _GLA_V0, GLA_DV),
            _heads(u_ref, _GLA_G0, GLA_DV), u_ref[:, _GLA_Z0:].astype(F32))


def _gla_fwd(u, wg, bg, gain, name):
    tp = u.shape[0]
    nch = tp // CHUNK

    def body(u_ref, wg_ref, bg_ref, gain_ref, on_ref, st_ref, state_ref):
        @pl.when(pl.program_id(0) == 0)
        def _():
            state_ref[...] = jnp.zeros_like(state_ref)

        state = state_ref[...]
        st_ref[...] = state.astype(BF16)
        on, new_state = _gla_head(*_gla_pieces(u_ref), state, _heads(wg_ref, 0, GLA_DK), _heads(bg_ref, 0, GLA_DK),
                                  _heads(gain_ref, 0, GLA_DV), _row_mask(pl.program_id(0)), *_cum_mats())
        state_ref[...] = new_state
        _put_heads(on_ref, 0, on, 1.0)

    return pl.pallas_call(
        body, name=name, grid=(nch,),
        in_specs=[pl.BlockSpec((CHUNK, GLA_U), lambda n: (n, 0)), pl.BlockSpec((128, HEADS * GLA_DK), lambda n: (0, 0)),
                  pl.BlockSpec((1, HEADS * GLA_DK), lambda n: (0, 0)), pl.BlockSpec((1, HEADS * GLA_DV), lambda n: (0, 0))],
        out_specs=[pl.BlockSpec((CHUNK, HEADS * GLA_DV), lambda n: (n, 0)),
                   pl.BlockSpec((None, HEADS, GLA_DV, GLA_DK), lambda n: (n, 0, 0, 0))],
        out_shape=[jax.ShapeDtypeStruct((tp, HEADS * GLA_DV), BF16),
                   jax.ShapeDtypeStruct((nch, HEADS, GLA_DV, GLA_DK), BF16)],
        scratch_shapes=[pltpu.VMEM((HEADS, GLA_DV, GLA_DK), F32)],
        compiler_params=_cp(1))(u, wg, bg, gain)


def _gla_bwd(u, wg, bg, gain, states, d_on, name):
    tp = u.shape[0]
    nch = tp // CHUNK
    rev = lambda n: nch - 1 - n

    def body(u_ref, wg_ref, bg_ref, gain_ref, st_ref, don_ref, du_ref, dwg_ref, dbg_ref, dgain_ref, dstate_ref):
        @pl.when(pl.program_id(0) == 0)
        def _():
            dstate_ref[...] = jnp.zeros_like(dstate_ref)
            dwg_ref[...] = jnp.zeros_like(dwg_ref)
            dbg_ref[...] = jnp.zeros_like(dbg_ref)
            dgain_ref[...] = jnp.zeros_like(dgain_ref)

        mask = _row_mask(rev(pl.program_id(0)))
        mats = _cum_mats()
        _, vjp = jax.vjp(lambda *a: _gla_head(*a, mask, *mats), *_gla_pieces(u_ref), st_ref[...].astype(F32),
                         _heads(wg_ref, 0, GLA_DK), _heads(bg_ref, 0, GLA_DK), _heads(gain_ref, 0, GLA_DV))
        dq, dk, dv, dg, dz, dstate, dwg, dbg, dgain = vjp((_heads(don_ref, 0, GLA_DV), dstate_ref[...]))
        dstate_ref[...] = dstate
        for hd in range(HEADS):
            dwg_ref[:, GLA_DK * hd:GLA_DK * (hd + 1)] += dwg[hd]
            dbg_ref[:, GLA_DK * hd:GLA_DK * (hd + 1)] += dbg[hd]
            dgain_ref[:, GLA_DV * hd:GLA_DV * (hd + 1)] += dgain[hd]
        _put_heads(du_ref, 0, dq, mask)
        _put_heads(du_ref, _GLA_K0, dk, mask)
        _put_heads(du_ref, _GLA_V0, dv, mask)
        _put_heads(du_ref, _GLA_G0, dg, mask)
        du_ref[:, _GLA_Z0:] = dz.astype(BF16)

    full = lambda r, c: pl.BlockSpec((r, c), lambda n: (0, 0))
    return pl.pallas_call(
        body, name=name, grid=(nch,),
        in_specs=[pl.BlockSpec((CHUNK, GLA_U), lambda n: (rev(n), 0)), full(128, HEADS * GLA_DK),
                  full(1, HEADS * GLA_DK), full(1, HEADS * GLA_DV),
                  pl.BlockSpec((None, HEADS, GLA_DV, GLA_DK), lambda n: (rev(n), 0, 0, 0)),
                  pl.BlockSpec((CHUNK, HEADS * GLA_DV), lambda n: (rev(n), 0))],
        out_specs=[pl.BlockSpec((CHUNK, GLA_U), lambda n: (rev(n), 0)), full(128, HEADS * GLA_DK),
                   full(1, HEADS * GLA_DK), full(1, HEADS * GLA_DV)],
        out_shape=[jax.ShapeDtypeStruct((tp, GLA_U), BF16), jax.ShapeDtypeStruct((128, HEADS * GLA_DK), F32),
                   jax.ShapeDtypeStruct((1, HEADS * GLA_DK), F32), jax.ShapeDtypeStruct((1, HEADS * GLA_DV), F32)],
        scratch_shapes=[pltpu.VMEM((HEADS, GLA_DV, GLA_DK), F32)],
        compiler_params=_cp(1))(u, wg, bg, gain, states, d_on)


def _ffn_fwd(h, gain, w_in, w_out, tag):
    hn, ug, uu, act = _norm_ffn_in(h, gain, w_in, f"{tag}_in")
    if callable(w_out):
        w_out = w_out(act)
    return _out_proj(act, w_out, h, 0.5, f"{tag}_out"), (h, hn, ug, uu, act), w_out


def _ffn_dgrad(dh, w_out, w_in, act_dg, act_du, h, gain, name):
    tp, d = dh.shape
    ff = w_out.shape[0]
    tm = TM_SMALL
    nt = (((1,), (1,)), ((), ()))

    def body(dh_ref, wo_ref, wi_ref, dg_ref, du_ref, h_ref, g_ref, o_ref, dhi_ref, dgain_ref):
        @pl.when(pl.program_id(0) == 0)
        def _():
            dgain_ref[...] = jnp.zeros_like(dgain_ref)

        dho = dh_ref[...]
        dact = lax.dot_general((0.5 * dho).astype(BF16), wo_ref[...], nt, preferred_element_type=F32)
        d_gate = (dact * dg_ref[...].astype(F32)).astype(BF16)
        d_up = (dact * du_ref[...].astype(F32)).astype(BF16)
        o_ref[:, :ff] = d_gate
        o_ref[:, ff:] = d_up
        dhn = (lax.dot_general(d_gate, wi_ref[:, :ff], nt, preferred_element_type=F32)
               + lax.dot_general(d_up, wi_ref[:, ff:], nt, preferred_element_type=F32))
        dx, xhat = _rmsnorm_bwd(dhn, h_ref[...], g_ref[...])
        dgain_ref[...] += jnp.sum(dhn * xhat, axis=0, keepdims=True)
        dhi_ref[...] = dho + dx

    rows = lambda width: pl.BlockSpec((tm, width), lambda i: (i, 0))
    return pl.pallas_call(
        body, name=name, grid=(tp // tm,),
        in_specs=[rows(d), _resident(w_out.shape, 1), _resident(w_in.shape, 1), rows(ff), rows(ff), rows(d),
                  pl.BlockSpec((1, d), lambda i: (0, 0))],
        out_specs=[rows(2 * ff), rows(d), pl.BlockSpec((1, d), lambda i: (0, 0))],
        out_shape=[jax.ShapeDtypeStruct((tp, 2 * ff), BF16), jax.ShapeDtypeStruct((tp, d), F32),
                   jax.ShapeDtypeStruct((1, d), F32)],
        compiler_params=_cp(1))(dh, w_out, w_in, act_dg, act_du, h, gain)


def _ffn_bwd(dh, saved, gain, w_in, w_out, tag, push):
    h, hn, act_dg, act_du, act = saved
    du, dh_in, d_gain = _ffn_dgrad(dh, w_out, w_in, act_dg, act_du, h, gain, f"{tag}_dgrad")
    d_w_out = _wgrad(act, dh, bm=D_FF // 2, bn=D, scale=0.5, sharded=False, name=f"{tag}_dwout")
    d_w_in = _wgrad(hn, du, bm=D, bn=D_FF, scale=1.0, sharded=False, name=f"{tag}_dwin")
    return dh_in, d_gain, push([("cols", d_w_in), d_w_out])


def _sequence_grads(x, target, p, weights, grads):
    row = lambda v, token: v.reshape(1, -1) + token[0, 0]
    gains = {}

    tok = weights.start(1, weights.start(0, None))
    weights.pin = tok
    h = jnp.concatenate([jnp.zeros((FRONT, D), F32), x], axis=0) + tok[0, 0]
    rope = _rope_tables(h.shape[0])
    w = weights.wait(0, [tok, h, *rope, *weights.later_shards(2)])
    tok = weights.start(2, w["l0_ffn1_in"])
    h = lax.dynamic_update_slice(h, w["meta"], (FRONT - N_META, 0))
    gains["l0_ffn1"] = row(p["norm_ffn1"][0], tok)
    h, s1, w["l0_ffn1_out"] = _ffn_fwd(h, gains["l0_ffn1"], w["l0_ffn1_in"],
                                       lambda act: weights.wait(1, act)["l0_ffn1_out"], "l0_ffn1")
    w.update(weights.wait(2, h))
    tok = weights.start(3, w["ret_in"])
    gains["ret"] = row(p["norm_mix"][0], tok)
    hn, u = _norm_proj(h, gains["ret"], w["ret_in"], "ret_in")
    on, states = _ret_fwd(u, w["ret_gain"], rope, "ret_fwd")
    w.update(weights.wait(3, on))
    tok = weights.start(4, w["ret_out"])
    h_mix = _out_proj(on, w["ret_out"], h, 1.0, "ret_out")
    s2 = (h, hn, u, on, states)
    gains["l0_ffn2"] = row(p["norm_ffn2"][0], tok)
    h, s3, _ = _ffn_fwd(h_mix, gains["l0_ffn2"], w["l0_ffn2_in"], w["l0_ffn2_out"], "l0_ffn2")
    saved = [(s1, s2, s3)]

    w.update(weights.wait(4, h))
    tok = weights.start(5, w["l1_ffn1_in"])
    gains["l1_ffn1"] = row(p["norm_ffn1"][1], tok)
    h, s1, _ = _ffn_fwd(h, gains["l1_ffn1"], w["l1_ffn1_in"], w["l1_ffn1_out"], "l1_ffn1")
    w.update(weights.wait(5, h))
    tok = weights.start(6, w["gla_out"])
    gains["gla"] = row(p["norm_mix"][1], tok)
    hn, u = _norm_proj(h, gains["gla"], w["gla_in"], "gla_in")
    on, states = _gla_fwd(u, w["gla_wg"], w["gla_bg"], w["gla_gain"], "gla_fwd")
    h_mix = _out_proj(on, w["gla_out"], h, 1.0, "gla_out")
    s2 = (h, hn, u, on, states)
    w.update(weights.wait(6, h_mix))
    gains["l1_ffn2"] = p["norm_ffn2"][1].reshape(1, -1)
    h, s3, _ = _ffn_fwd(h_mix, gains["l1_ffn2"], w["l1_ffn2_in"], w["l1_ffn2_out"], "l1_ffn2")
    saved.append((s1, s2, s3))

    dh, d_final, loss = _loss_head(h, p["final_norm"].reshape(1, -1), target, "loss_head")
    small = {"final_norm": d_final, "norm_ffn1": [None, None], "norm_mix": [None, None], "norm_ffn2": [None, None]}
    pusher = lambda k: functools.partial(grads.push, k)

    s1, s2, s3 = saved[1]
    dh, small["norm_ffn2"][1], tok = _ffn_bwd(dh, s3, gains["l1_ffn2"], w["l1_ffn2_in"], w["l1_ffn2_out"], "l1_ffn2",
                                              pusher(0))
    h_in, hn, u, on, states = s2
    d_on = _dgrad(dh, w["gla_out"], "gla_don")
    d_out = _wgrad(on, dh, bm=D, bn=D, scale=1.0, sharded=False, name="gla_dwout")
    du, small["gla_wg"], small["gla_bg"], small["gla_gain"] = _gla_bwd(
        u, w["gla_wg"], w["gla_bg"], w["gla_gain"] + tok[0, 0], states, d_on, "gla_bwd")
    d_in = _wgrad(hn, du, bm=D, bn=GLA_U // 5, scale=1.0, sharded=False, name="gla_dwin")
    d_in = jnp.moveaxis(d_in[:, :GLA_IN].reshape(D, N_CHIPS, -1), 1, 0)
    tok = grads.push(1, [d_in, d_out])
    dh, small["norm_mix"][1] = _dgrad_norm(du, w["gla_in"], h_in, gains["gla"] + tok[0, 0], dh, "gla_dnorm")
    dh, small["norm_ffn1"][1], tok = _ffn_bwd(dh, s1, gains["l1_ffn1"], w["l1_ffn1_in"], w["l1_ffn1_out"], "l1_ffn1",
                                              pusher(2))

    s1, s2, s3 = saved[0]
    dh, small["norm_ffn2"][0], tok = _ffn_bwd(dh, s3, gains["l0_ffn2"] + tok[0, 0], w["l0_ffn2_in"],
                                              w["l0_ffn2_out"], "l0_ffn2", pusher(3))
    h_in, hn, u, on, states = s2
    d_on = _dgrad(dh, w["ret_out"], "ret_don")
    d_out = _wgrad(on, dh, bm=D, bn=D, scale=1.0, sharded=False, name="ret_dwout")
    du, small["ret_gain"] = _ret_bwd(u, w["ret_gain"] + tok[0, 0], rope, states, d_on, "ret_bwd")
    d_in = _wgrad(hn, du, bm=D, bn=w["ret_in"].shape[2], scale=1.0, sharded=True, name="ret_dwin")
    tok = grads.push(4, [d_in, d_out])
    dh, small["norm_mix"][0] = _dgrad_norm(du, w["ret_in"], h_in, gains["ret"] + tok[0, 0], dh, "ret_dnorm")
    dh, small["norm_ffn1"][0], tok = _ffn_bwd(dh, s1, gains["l0_ffn1"], w["l0_ffn1_in"], w["l0_ffn1_out"], "l0_ffn1",
                                              pusher(5))
    grads.push(6, [], [dh[FRONT - N_META:FRONT], *small["norm_ffn1"], *small["norm_mix"], *small["norm_ffn2"],
                       small["final_norm"], small["ret_gain"], small["gla_wg"][:GLA_RANK], small["gla_bg"],
                       small["gla_gain"], loss[:, :1] + tok[0, 0]])
    return dh[FRONT:]


_HBM = pl.BlockSpec(memory_space=pl.ANY)


def _place():
    return lax.axis_index("x"), lax.axis_index("y"), lax.axis_index("c")


def _flip(v, bit):
    return 1 - v if bit else v


DMA_CHUNK_BYTES = 128 * 1024


def _row_chunks(ref):
    rows, cols = ref.shape
    step = _row_tile(rows, max(16, DMA_CHUNK_BYTES // (cols * ref.dtype.itemsize)))
    return [pl.ds(a, step) for a in range(0, rows, step)]


def _whole(src, dst, send_sem, recv_sem, peer):
    return pltpu.make_async_remote_copy(src_ref=src, dst_ref=dst, send_sem=send_sem, recv_sem=recv_sem,
                                        device_id=peer, device_id_type=MESH)


def _send(src, dst, send_sem, recv_sem, peer):
    for rows in _row_chunks(src):
        _whole(src.at[rows], dst.at[rows], send_sem, recv_sem, peer).start()
    return _whole(src, dst, send_sem, recv_sem, peer)


_HBM_ONLY = pl.BlockSpec(memory_space=pltpu.HBM)
_SEMS = pl.BlockSpec(memory_space=pltpu.SEMAPHORE)
_SIDE_EFFECT = pltpu.CompilerParams(has_side_effects=pltpu.SideEffectType.DATAFLOW_SIDE_EFFECTING)
_GATHER_FLIPS = [(1, 0, 0), (0, 1, 0), (1, 1, 0), (0, 0, 1)]
_PEER_FLIPS = [(fx, fy, fc) for fx in (0, 1) for fy in (0, 1) for fc in (0, 1)][1:]


def _zero_token():
    return jnp.zeros((8, 128), F32)


def _exchange_start(srcs, lands, route, flips, after, name):
    n = len(srcs)

    def body(*refs):
        src, land = refs[:n], refs[n:2 * n]
        send_sems, recv_sems, token = refs[2 * n + 1], refs[2 * n + 2], refs[-1]
        me = _place()
        for t in range(n):
            for j, flip in enumerate(flips):
                peer = tuple(_flip(v, f) for v, f in zip(me, flip))
                s, d = route(t, src[t], land[t], me, peer)
                _send(s, d, send_sems.at[t * len(flips) + j], recv_sems.at[t * len(flips) + j], peer)
        token[...] = jnp.zeros_like(token)

    hbm = lambda a: pltpu.HBM(a.shape, a.dtype)
    sems = pltpu.SemaphoreType.DMA((n * len(flips),))
    operands = [pltpu.with_memory_space_constraint(a, pltpu.HBM) for a in list(srcs) + list(lands)]
    out = pl.pallas_call(
        body, name=name, in_specs=[_HBM_ONLY] * (2 * n) + [_HBM],
        out_shape=(sems, sems, *[hbm(a) for a in operands], jax.ShapeDtypeStruct((8, 128), F32)),
        out_specs=(_SEMS, _SEMS, *[_HBM_ONLY] * (2 * n), pl.BlockSpec(memory_space=pltpu.VMEM)),
        input_output_aliases={i: 2 + i for i in range(2 * n)}, compiler_params=_SIDE_EFFECT,
    )(*operands, _zero_token() if after is None else after)
    return (out[0], out[1], out[2:2 + n], out[2 + n:2 + 2 * n]), out[-1]


def _exchange_wait(started, route, flips, after, name):
    send_sems, recv_sems, srcs, lands = started
    n = len(srcs)

    def body(*refs):
        src, land = refs[:n], refs[n:2 * n]
        send_sems, recv_sems = refs[2 * n], refs[2 * n + 1]
        me = _place()
        for t in range(n):
            for j, flip in enumerate(flips):
                peer = tuple(_flip(v, f) for v, f in zip(me, flip))
                s, d = route(t, src[t], land[t], me, peer)
                cp = _whole(s, d, send_sems.at[t * len(flips) + j], recv_sems.at[t * len(flips) + j], peer)
                cp.wait_send()
                cp.wait_recv()

    hbm = lambda a: pltpu.HBM(a.shape, a.dtype)
    after = list(after) if isinstance(after, (list, tuple)) else [after]
    out = pl.pallas_call(
        body, name=name, in_specs=[_HBM_ONLY] * (2 * n) + [_SEMS, _SEMS] + [_HBM] * len(after),
        out_shape=tuple(hbm(a) for a in list(srcs) + list(lands)), out_specs=tuple([_HBM_ONLY] * (2 * n)),
        input_output_aliases={i: i for i in range(2 * n)}, compiler_params=_SIDE_EFFECT,
    )(*srcs, *lands, send_sems, recv_sems, *after)
    return out[:n], out[n:]


def _gather_route(t, src, land, me, peer):
    mine = 2 * me[0] + me[1]
    if land.ndim == 3:
        return src, land.at[mine]
    cols = src.shape[1]
    return src, land.at[:, pl.ds(pl.multiple_of(mine * cols, 128), cols)]


def _scatter_route(n_pieces):
    def route(t, src, land, me, peer):
        chip = 2 * peer[0] + peer[1]
        if t >= n_pieces:
            part = src
        elif src.ndim == 4:
            part = src.at[chip, peer[2]]
        else:
            rows, cols = land.shape[1:]
            part = src.at[pl.ds(pl.multiple_of(peer[2] * rows, 16), rows), pl.ds(pl.multiple_of(chip * cols, 128), cols)]
        return part, land.at[4 * me[0] + 2 * me[1] + me[2]]

    return route


def _swap_cores(halves, name):
    n = len(halves)

    def body(*refs):
        src, dst = refs[:n], refs[n:2 * n]
        send_sems, recv_sems = refs[2 * n:]
        x, y, c = _place()
        copies = [_send(src[t], dst[t], send_sems.at[t], recv_sems.at[t], (x, y, 1 - c)) for t in range(n)]
        for cp in copies:
            cp.wait()

    got = pl.pallas_call(
        body, name=name, in_specs=[_HBM] * n, out_specs=[_HBM] * n,
        out_shape=[jax.ShapeDtypeStruct(a.shape, a.dtype) for a in halves],
        scratch_shapes=[pltpu.SemaphoreType.DMA((n,)), pltpu.SemaphoreType.DMA((n,))],
    )(*halves)
    south = lax.axis_index("c") == 0
    return [jnp.stack([jnp.where(south, a, b), jnp.where(south, b, a)]) for a, b in zip(halves, got)]


def _row_tile(rows, cap):
    fits = [t for t in range(16, cap + 1, 16) if rows % t == 0]
    return fits[-1] if fits else rows


def _sum_slots(a, name):
    _, r, c = a.shape
    tr = _row_tile(r, 384)

    def body(a_ref, o_ref):
        s = a_ref[0].astype(F32)
        for k in range(1, N_DEV):
            s = s + a_ref[k].astype(F32)
        o_ref[...] = s

    return pl.pallas_call(
        body, name=name, grid=(r // tr,),
        in_specs=[pl.BlockSpec((N_DEV, tr, c), lambda i: (0, i, 0))],
        out_specs=pl.BlockSpec((tr, c), lambda i: (i, 0)),
        out_shape=jax.ShapeDtypeStruct((r, c), F32),
        compiler_params=_cp(1))(a)


def _adamw(w, g, m, v, name):
    layers, r, c = w.shape
    tr = _row_tile(r, 256)

    def body(w_ref, g_ref, m_ref, v_ref, d_ref, nm_ref, nv_ref):
        gv = g_ref[...]
        nm = ADAM_B1 * m_ref[...] + (1.0 - ADAM_B1) * gv
        nv = ADAM_B2 * v_ref[...] + (1.0 - ADAM_B2) * (gv * gv)
        m_hat = nm / (1.0 - ADAM_B1 ** ADAM_STEP)
        v_hat = nv / (1.0 - ADAM_B2 ** ADAM_STEP)
        d_ref[...] = -ADAM_LR * (m_hat / (jnp.sqrt(v_hat) + ADAM_EPS) + ADAM_WD * w_ref[...])
        nm_ref[...] = nm
        nv_ref[...] = nv

    spec = pl.BlockSpec((None, tr, c), lambda a, i: (a, i, 0))
    return pl.pallas_call(
        body, name=name, grid=(layers, r // tr), in_specs=[spec] * 4, out_specs=[spec] * 3,
        out_shape=[jax.ShapeDtypeStruct((layers, r, c), F32)] * 3,
        compiler_params=_cp(2))(w, g, m, v)


_SMALL = ["meta_tokens", "ret_head_norm", "gla_w_gate", "gla_b_gate", "gla_head_norm"]
_LOCAL_SMALL = ["meta_tokens", "norm_ffn1", "norm_mix", "norm_ffn2", "ret_head_norm", "gla_w_gate", "gla_b_gate",
                "gla_head_norm", "final_norm"]
_BIG = ["ffn1_w_in", "ffn1_w_out", "ffn2_w_in", "ffn2_w_out", "ret_w_in", "ret_w_out", "gla_w_in", "gla_w_out"]
_WEIGHTS = ["meta_tokens", "norm_ffn1", "ffn1_w_in", "ffn1_w_out", "norm_mix", "norm_ffn2", "ffn2_w_in", "ffn2_w_out",
            "ret_w_in", "ret_head_norm", "ret_w_out", "gla_w_in", "gla_w_gate", "gla_b_gate", "gla_head_norm",
            "gla_w_out", "final_norm"]


def _pack_rows(arrays, width):
    flat = jnp.concatenate([a.reshape(-1) for a in arrays])
    pad = -flat.shape[0] % (8 * width)
    return jnp.pad(flat, (0, pad)).reshape(-1, width)


def _unpack_rows(packed, shapes):
    flat, out, at = packed.reshape(-1), [], 0
    for s in shapes:
        size = 1
        for dim in s:
            size *= dim
        out.append(flat[at:at + size].reshape(s))
        at += size
    return out


class _WeightGather:
    GROUPS = [("small", "l0_ffn1_in"), ("l0_ffn1_out",), ("ret_in",), ("ret_out", "l0_ffn2_in", "l0_ffn2_out"),
              ("l1_ffn1_in", "l1_ffn1_out"), ("gla_in", "gla_out"), ("l1_ffn2_in", "l1_ffn2_out")]

    def __init__(self, p):
        self.small_shapes = [p[name].shape for name in _SMALL]
        self.f32 = {"small": _pack_rows([p[name] for name in _SMALL], 128), "ret_in": p["ret_w_in"][0],
                    "ret_out": p["ret_w_out"][0], "gla_in": p["gla_w_in"][0], "gla_out": p["gla_w_out"][0]}
        for layer in range(2):
            for name in ("ffn1", "ffn2"):
                self.f32[f"l{layer}_{name}_in"] = p[f"{name}_w_in"][layer]
                self.f32[f"l{layer}_{name}_out"] = p[f"{name}_w_out"][layer]
        self.shards = {}
        self.started = {}
        self.pin = None

    def shard(self, name):
        if name not in self.shards:
            a = self.f32[name]
            if name != "small":
                a = (a if self.pin is None else a + self.pin[0, 0]).astype(BF16)
            self.shards[name] = a
        return self.shards[name]

    def later_shards(self, k):
        return [self.shard(name) for group in self.GROUPS[k:] for name in group]

    def start(self, k, after):
        shards = [self.shard(name) for name in self.GROUPS[k]]
        lands = []
        for name, s in zip(self.GROUPS[k], shards):
            if "ffn" in name and name.endswith("_in"):
                lands.append(lax.empty((s.shape[0], N_CHIPS * s.shape[1]), s.dtype))
            else:
                lands.append(lax.empty((N_CHIPS,) + s.shape, s.dtype))
        self.started[k], token = _exchange_start(shards, lands, _gather_route, _GATHER_FLIPS, after, f"gather{k}_start")
        return token

    def wait(self, k, after):
        _, got = _exchange_wait(self.started[k], _gather_route, _GATHER_FLIPS, after, f"gather{k}_wait")
        w = {}
        for name, g in zip(self.GROUPS[k], got):
            if name == "small":
                parts = zip(*[_unpack_rows(g[chip], self.small_shapes) for chip in range(N_CHIPS)])
                cat = lambda a: jnp.moveaxis(a, 0, -2).reshape(a.shape[1:-1] + (-1,))
                meta, ret_gain, wg, bg, gla_gain = [cat(jnp.stack(part)) for part in parts]
                w.update(meta=meta, ret_gain=ret_gain.reshape(1, -1), gla_bg=bg.reshape(1, -1),
                         gla_gain=gla_gain.reshape(1, -1),
                         gla_wg=jnp.pad(wg[0], ((0, 128 - GLA_RANK), (0, 0))).astype(BF16))
            elif name == "gla_in":
                full = jnp.moveaxis(g, 0, 1).reshape(D, -1)
                w[name] = jnp.pad(full, ((0, 0), (0, GLA_U - GLA_IN)))[None]
            elif name.endswith("_out"):
                w[name] = g.reshape(-1, g.shape[-1])
            else:
                w[name] = g
        return w


class _GradExchange:
    def __init__(self):
        self.started = []
        self.token = None
        self.small_shapes = None

    def push(self, k, arrays, small=None):
        srcs, lands = [], []
        for a in arrays:
            if isinstance(a, tuple):
                a = a[1]
                piece = (a.shape[0] // 2, a.shape[1] // N_CHIPS)
            else:
                a = a.reshape(N_CHIPS, 2, -1, a.shape[-1])
                piece = a.shape[2:]
            srcs.append(a)
            lands.append(lax.empty((N_DEV,) + piece, a.dtype))
        if small is not None:
            self.small_shapes = [a.shape for a in small]
            srcs.append(_pack_rows(small, D))
            lands.append(lax.empty((N_DEV,) + srcs[-1].shape, F32))
        started, self.token = _exchange_start(srcs, lands, _scatter_route(len(arrays)), _PEER_FLIPS, None,
                                              f"scatter{k}_start")
        self.started.append((started, len(arrays)))
        return self.token

    def collect(self, groups, after=None):
        x, y, c = _place()
        after, sums = self.token if after is None else after, []
        for k in groups:
            started, n_pieces = self.started[k]
            srcs, got = _exchange_wait(started, _scatter_route(n_pieces), _PEER_FLIPS, after, f"scatter{k}_wait")
            own = []
            for t, (a, g) in enumerate(zip(srcs, got)):
                if t >= n_pieces:
                    own.append(a)
                elif a.ndim == 4:
                    own.append(a[2 * x + y, c])
                else:
                    rows, cols = g.shape[1:]
                    own.append(lax.dynamic_slice(a, (c * rows, (2 * x + y) * cols), (rows, cols)))
            got = [lax.dynamic_update_index_in_dim(g, a, 4 * x + 2 * y + c, 0) for g, a in zip(got, own)]
            sums.append([_sum_slots(a, f"sum{k}_{i}") for i, a in enumerate(got)])
            after = sums[-1][0]
        return sums


def kernel(x, meta_tokens, norm_ffn1, ffn1_w_in, ffn1_w_out, norm_mix, norm_ffn2, ffn2_w_in, ffn2_w_out, ret_w_in, ret_head_norm, ret_w_out, gla_w_in, gla_w_gate, gla_b_gate, gla_head_norm, gla_w_out, final_norm, loss_target, m_meta_tokens, m_norm_ffn1, m_ffn1_w_in, m_ffn1_w_out, m_norm_mix, m_norm_ffn2, m_ffn2_w_in, m_ffn2_w_out, m_ret_w_in, m_ret_head_norm, m_ret_w_out, m_gla_w_in, m_gla_w_gate, m_gla_b_gate, m_gla_head_norm, m_gla_w_out, m_final_norm, v_meta_tokens, v_norm_ffn1, v_ffn1_w_in, v_ffn1_w_out, v_norm_mix, v_norm_ffn2, v_ffn2_w_in, v_ffn2_w_out, v_ret_w_in, v_ret_head_norm, v_ret_w_out, v_gla_w_in, v_gla_w_gate, v_gla_b_gate, v_gla_head_norm, v_gla_w_out, v_final_norm):
    p = dict(meta_tokens=meta_tokens, norm_ffn1=norm_ffn1, ffn1_w_in=ffn1_w_in, ffn1_w_out=ffn1_w_out, norm_mix=norm_mix,
             norm_ffn2=norm_ffn2, ffn2_w_in=ffn2_w_in, ffn2_w_out=ffn2_w_out, ret_w_in=ret_w_in,
             ret_head_norm=ret_head_norm, ret_w_out=ret_w_out, gla_w_in=gla_w_in, gla_w_gate=gla_w_gate,
             gla_b_gate=gla_b_gate, gla_head_norm=gla_head_norm, gla_w_out=gla_w_out, final_norm=final_norm)
    m = dict(zip(_WEIGHTS, (m_meta_tokens, m_norm_ffn1, m_ffn1_w_in, m_ffn1_w_out, m_norm_mix, m_norm_ffn2, m_ffn2_w_in,
                            m_ffn2_w_out, m_ret_w_in, m_ret_head_norm, m_ret_w_out, m_gla_w_in, m_gla_w_gate,
                            m_gla_b_gate, m_gla_head_norm, m_gla_w_out, m_final_norm)))
    v = dict(zip(_WEIGHTS, (v_meta_tokens, v_norm_ffn1, v_ffn1_w_in, v_ffn1_w_out, v_norm_mix, v_norm_ffn2, v_ffn2_w_in,
                            v_ffn2_w_out, v_ret_w_in, v_ret_head_norm, v_ret_w_out, v_gla_w_in, v_gla_w_gate,
                            v_gla_b_gate, v_gla_head_norm, v_gla_w_out, v_final_norm)))

    exchange = _GradExchange()
    d_x = _sequence_grads(x[0], loss_target[0], p, _WeightGather(p), exchange)
    names = [("ffn2_w_in", 1), ("ffn2_w_out", 1), ("gla_w_in", 0), ("gla_w_out", 0), ("ffn1_w_in", 1), ("ffn1_w_out", 1),
             ("ffn2_w_in", 0), ("ffn2_w_out", 0), ("ret_w_in", 0), ("ret_w_out", 0), ("ffn1_w_in", 0), ("ffn1_w_out", 0)]
    shard, grads, delta, new_m, new_v = {}, {}, {}, {}, {}

    def swap(sums, keys, name):
        for key, a in zip(keys, _swap_cores(sums, name)):
            shard[key] = a.reshape(-1, a.shape[-1])

    def update(name):
        layers = p[name].shape[0]
        grads[name] = jnp.stack([shard[name, layer] for layer in range(layers)])
        delta[name], new_m[name], new_v[name] = _adamw(p[name], grads[name], m[name], v[name], f"adamw_{name}")

    swap([a for group in exchange.collect(range(5)) for a in group], names[:10], "swap_first")
    for name in ("ffn2_w_in", "ffn2_w_out", "ret_w_in", "ret_w_out", "gla_w_in", "gla_w_out"):
        update(name)
    last, (small_sum,) = exchange.collect([5, 6], after=list(delta.values()))
    swap(last, names[10:], "swap_last")
    for name in ("ffn1_w_in", "ffn1_w_out"):
        update(name)

    chip = 2 * lax.axis_index("x") + lax.axis_index("y")
    cols = lambda a, n: lax.dynamic_slice_in_dim(a, chip * n, n, axis=a.ndim - 1)
    (s_meta, s_n1a, s_n1b, s_nma, s_nmb, s_n2a, s_n2b, s_final, s_ret_gain, s_wg, s_bg, s_gla_gain,
     s_loss) = _unpack_rows(small_sum, exchange.small_shapes)
    grads.update({
        "meta_tokens": cols(s_meta, 256), "norm_ffn1": jnp.concatenate([s_n1a, s_n1b]),
        "norm_mix": jnp.concatenate([s_nma, s_nmb]), "norm_ffn2": jnp.concatenate([s_n2a, s_n2b]),
        "final_norm": s_final.reshape(D),
        "ret_head_norm": cols(s_ret_gain.reshape(1, HEADS, RET_DV), RET_DV // N_CHIPS),
        "gla_w_gate": cols(s_wg, GLA_DK)[None], "gla_b_gate": cols(s_bg, GLA_DK),
        "gla_head_norm": cols(s_gla_gain.reshape(1, HEADS, GLA_DV), GLA_DV // N_CHIPS),
    })
    packed = [_pack_rows([d[name] for name in _LOCAL_SMALL], 128)[None] for d in (p, grads, m, v)]
    out = _adamw(*packed, "adamw_small")
    shapes = [p[name].shape for name in _LOCAL_SMALL]
    for d, a in zip((delta, new_m, new_v), out):
        d.update(zip(_LOCAL_SMALL, _unpack_rows(a, shapes)))

    return (s_loss.reshape(()), d_x[None], *[grads[n] for n in _WEIGHTS], *[delta[n] for n in _WEIGHTS],
            *[new_m[n] for n in _WEIGHTS], *[new_v[n] for n in _WEIGHTS])
```

```python
import functools

import jax
import jax.numpy as jnp
from jax import lax
from jax.experimental import pallas as pl
from jax.experimental.pallas import tpu as pltpu

F32, BF16 = jnp.float32, jnp.bfloat16
MESH = pl.DeviceIdType.MESH

D = 1024
N_META = 16
CHUNK = 64
RET_CHUNK = 128
FRONT = 256
D_FF = 2816
EPS = 1e-6
HEADS = 4
RET_DK, RET_DV = 256, 512
GLA_DK, GLA_DV = 128, 256
GLA_RANK = 16
GLA_TAU = 16.0
GLA_IN = 2 * HEADS * GLA_DK + 2 * HEADS * GLA_DV + GLA_RANK
GLA_U = 3328
ROPE_BASE = 10000.0
N_CHIPS = 4
N_DEV = 8

ADAM_LR, ADAM_B1, ADAM_B2, ADAM_EPS, ADAM_WD, ADAM_STEP = 0.001, 0.9, 0.999, 1e-08, 0.01, 10

VMEM_LIMIT_BYTES = 56 * 1024 * 1024
TM = 768
TM_SMALL = 256


TM_RESIDENT = 384
MXU_TILE = 256


def _cp(n_axes):
    return pltpu.CompilerParams(dimension_semantics=("arbitrary",) * n_axes, vmem_limit_bytes=VMEM_LIMIT_BYTES)


def _resident(shape, n_axes):
    zeros = (0,) * len(shape)
    index = (lambda i: zeros) if n_axes == 1 else (lambda i, j: zeros)
    return pl.BlockSpec(shape, index, pipeline_mode=pl.Buffered(1))


def _dg(a, b, ca, cb):
    nb = a.ndim - 2
    dims = (((ca + nb,), (cb + nb,)), (tuple(range(nb)), tuple(range(nb))))
    return lax.dot_general(a.astype(BF16), b.astype(BF16), dims, preferred_element_type=F32)


@jax.custom_vjp
def _nn(a, b):
    return _dg(a, b, 1, 0)


@jax.custom_vjp
def _nt(a, b):
    return _dg(a, b, 1, 1)


@jax.custom_vjp
def _tn(a, b):
    return _dg(a, b, 0, 0)


def _dot_vjp(fn, ca, cb, da, db):
    def fwd(a, b):
        a, b = a.astype(BF16), b.astype(BF16)
        return _dg(a, b, ca, cb), (a, b)

    def bwd(res, g):
        a, b = res
        g = g.astype(BF16)
        grad = lambda other, dims, g_first: _dg(g, other, *dims) if g_first else _dg(other, g, *dims)
        return grad(b, *da), grad(a, *db)

    fn.defvjp(fwd, bwd)


_dot_vjp(_nn, 1, 0, ((1, 1), True), ((0, 0), False))
_dot_vjp(_nt, 1, 1, ((1, 0), True), ((0, 0), True))
_dot_vjp(_tn, 0, 0, ((1, 1), False), ((1, 0), False))


def _split3_dot(m, a):
    a1 = a.astype(BF16)
    r1 = a - a1.astype(F32)
    a2 = r1.astype(BF16)
    a3 = (r1 - a2.astype(F32)).astype(BF16)
    mb = jnp.broadcast_to(m, a.shape[:-2] + m.shape)
    return _dg(mb, a1, 1, 0) + _dg(mb, a2, 1, 0) + _dg(mb, a3, 1, 0)


@jax.custom_vjp
def _cum(m, mt, a):
    return _split3_dot(m, a)


_cum.defvjp(lambda m, mt, a: (_split3_dot(m, a), (m, mt)),
            lambda res, g: (jnp.zeros_like(res[0]), jnp.zeros_like(res[1]), _split3_dot(res[1], g)))


def _sigmoid(x):
    return 1.0 / (1.0 + jnp.exp(-x))


def _rms(x):
    return lax.rsqrt(jnp.mean(x * x, axis=-1, keepdims=True) + EPS)


def _rmsnorm_bwd(dy, x, gain):
    r = _rms(x)
    xhat = x * r
    dxh = dy * gain
    return r * (dxh - xhat * jnp.mean(dxh * xhat, axis=-1, keepdims=True)), xhat


def _norm_proj(h, gain, w, name):
    tp, d = h.shape
    s, _, ns = w.shape

    tm = TM_RESIDENT

    def body(h_ref, g_ref, w_ref, hn_ref, u_ref):
        x = h_ref[...]
        a = (x * _rms(x) * g_ref[...]).astype(BF16)
        hn_ref[...] = a
        for k in range(s):
            u_ref[:, ns * k:ns * (k + 1)] = jnp.dot(a, w_ref[k], preferred_element_type=F32).astype(BF16)

    return pl.pallas_call(
        body, name=name, grid=(tp // tm,),
        in_specs=[pl.BlockSpec((tm, d), lambda i: (i, 0)), pl.BlockSpec((1, d), lambda i: (0, 0)), _resident(w.shape, 1)],
        out_specs=[pl.BlockSpec((tm, d), lambda i: (i, 0)), pl.BlockSpec((tm, s * ns), lambda i: (i, 0))],
        out_shape=[jax.ShapeDtypeStruct((tp, d), BF16), jax.ShapeDtypeStruct((tp, s * ns), BF16)],
        compiler_params=_cp(1))(h, gain, w)


def _norm_ffn_in(h, gain, w, name):
    tp, d = h.shape
    ff = w.shape[1] // 2
    tm = TM_RESIDENT
    blocks = [(c, min(c + 6 * MXU_TILE, ff)) for c in range(0, ff, 6 * MXU_TILE)]

    def body(h_ref, g_ref, w_ref, hn_ref, dg_ref, du_ref, act_ref):
        x = h_ref[...]
        a = (x * _rms(x) * g_ref[...]).astype(BF16)
        hn_ref[...] = a
        for c0, c1 in blocks:
            g = jnp.dot(a, w_ref[:, c0:c1], preferred_element_type=F32)
            u = jnp.dot(a, w_ref[:, ff + c0:ff + c1], preferred_element_type=F32)
            sg = _sigmoid(g)
            silu = g * sg
            dg_ref[:, c0:c1] = (u * (sg + silu * (1.0 - sg))).astype(BF16)
            du_ref[:, c0:c1] = silu.astype(BF16)
            act_ref[:, c0:c1] = (silu * u).astype(BF16)

    wide = jax.ShapeDtypeStruct((tp, ff), BF16)
    return pl.pallas_call(
        body, name=name, grid=(tp // tm,),
        in_specs=[pl.BlockSpec((tm, d), lambda i: (i, 0)), pl.BlockSpec((1, d), lambda i: (0, 0)),
                  _resident(w.shape, 1)],
        out_specs=[pl.BlockSpec((tm, d), lambda i: (i, 0))] + [pl.BlockSpec((tm, ff), lambda i: (i, 0))] * 3,
        out_shape=[jax.ShapeDtypeStruct((tp, d), BF16), wide, wide, wide],
        compiler_params=_cp(1))(h, gain, w)


def _out_proj(a, w, h, scale, name):
    tp, k = a.shape
    d = w.shape[1]

    def body(a_ref, w_ref, h_ref, o_ref):
        o_ref[...] = h_ref[...] + scale * jnp.dot(a_ref[...], w_ref[...], preferred_element_type=F32)

    return pl.pallas_call(
        body, name=name, grid=(tp // TM,),
        in_specs=[pl.BlockSpec((TM, k), lambda i: (i, 0)), pl.BlockSpec((k, d), lambda i: (0, 0)),
                  pl.BlockSpec((TM, d), lambda i: (i, 0))],
        out_specs=pl.BlockSpec((TM, d), lambda i: (i, 0)),
        out_shape=jax.ShapeDtypeStruct((tp, d), F32),
        compiler_params=_cp(1))(a, w, h)


def _wgrad(a, b, *, bm, bn, scale, sharded, name):
    tp, m = a.shape
    n = b.shape[1]
    nk = tp // TM

    def body(a_ref, b_ref, o_ref, acc_ref):
        k = pl.program_id(2)

        @pl.when(k == 0)
        def _():
            acc_ref[...] = jnp.zeros_like(acc_ref)

        bb = b_ref[...]
        if scale != 1.0:
            bb = scale * bb
        acc_ref[...] += lax.dot_general(a_ref[...], bb.astype(BF16), (((0,), (0,)), ((), ())),
                                        preferred_element_type=F32)

        @pl.when(k == nk - 1)
        def _():
            o_ref[...] = acc_ref[...].astype(BF16)

    if sharded:
        assert m == bm
        out_spec = pl.BlockSpec((None, bm, bn), lambda i, j, k: (j, 0, 0))
        out_shape = jax.ShapeDtypeStruct((n // bn, m, bn), BF16)
    else:
        out_spec = pl.BlockSpec((bm, bn), lambda i, j, k: (i, j))
        out_shape = jax.ShapeDtypeStruct((m, n), BF16)
    return pl.pallas_call(
        body, name=name, grid=(m // bm, n // bn, nk),
        in_specs=[pl.BlockSpec((TM, bm), lambda i, j, k: (k, i)), pl.BlockSpec((TM, bn), lambda i, j, k: (k, j))],
        out_specs=out_spec, out_shape=out_shape,
        scratch_shapes=[pltpu.VMEM((bm, bn), F32)],
        compiler_params=_cp(3))(a, b)


def _dgrad_norm(du, w, h, gain, dh_out, name):
    tp, d = h.shape
    s, _, ns = w.shape
    tm = TM_RESIDENT

    def body(du_ref, w_ref, h_ref, g_ref, dho_ref, dhi_ref, dg_ref):
        @pl.when(pl.program_id(0) == 0)
        def _():
            dg_ref[...] = jnp.zeros_like(dg_ref)

        dhn = None
        for k in range(s):
            part = lax.dot_general(du_ref[:, ns * k:ns * (k + 1)], w_ref[k], (((1,), (1,)), ((), ())),
                                   preferred_element_type=F32)
            dhn = part if dhn is None else dhn + part
        dx, xhat = _rmsnorm_bwd(dhn, h_ref[...], g_ref[...])
        dg_ref[...] += jnp.sum(dhn * xhat, axis=0, keepdims=True)
        dhi_ref[...] = dho_ref[...] + dx

    return pl.pallas_call(
        body, name=name, grid=(tp // tm,),
        in_specs=[pl.BlockSpec((tm, s * ns), lambda i: (i, 0)), _resident(w.shape, 1),
                  pl.BlockSpec((tm, d), lambda i: (i, 0)), pl.BlockSpec((1, d), lambda i: (0, 0)),
                  pl.BlockSpec((tm, d), lambda i: (i, 0))],
        out_specs=[pl.BlockSpec((tm, d), lambda i: (i, 0)), pl.BlockSpec((1, d), lambda i: (0, 0))],
        out_shape=[jax.ShapeDtypeStruct((tp, d), F32), jax.ShapeDtypeStruct((1, d), F32)],
        compiler_params=_cp(1))(du, w, h, gain, dh_out)


def _loss_head(h, gain, target, name):
    tp, d = h.shape
    tm = TM_SMALL
    front_tiles = FRONT // tm

    def body(h_ref, g_ref, t_ref, dh_ref, dg_ref, loss_ref):
        i = pl.program_id(0)

        @pl.when(i == 0)
        def _():
            dg_ref[...] = jnp.zeros_like(dg_ref)
            loss_ref[...] = jnp.zeros_like(loss_ref)

        x = h_ref[...]
        gain_v = g_ref[...]
        y = x * _rms(x) * gain_v
        err = jnp.where(i >= front_tiles, y - t_ref[...], 0.0)
        loss_ref[...] += 0.5 * jnp.sum(jnp.mean(err * err, axis=-1, keepdims=True), axis=0, keepdims=True)
        dy = err * (1.0 / d)
        dx, xhat = _rmsnorm_bwd(dy, x, gain_v)
        dg_ref[...] += jnp.sum(dy * xhat, axis=0, keepdims=True)
        dh_ref[...] = dx

    return pl.pallas_call(
        body, name=name, grid=(tp // tm,),
        in_specs=[pl.BlockSpec((tm, d), lambda i: (i, 0)), pl.BlockSpec((1, d), lambda i: (0, 0)),
                  pl.BlockSpec((tm, d), lambda i: (jnp.maximum(i - front_tiles, 0), 0))],
        out_specs=[pl.BlockSpec((tm, d), lambda i: (i, 0)), pl.BlockSpec((1, d), lambda i: (0, 0)),
                   pl.BlockSpec((1, 128), lambda i: (0, 0))],
        out_shape=[jax.ShapeDtypeStruct((tp, d), F32), jax.ShapeDtypeStruct((1, d), F32),
                   jax.ShapeDtypeStruct((1, 128), F32)],
        compiler_params=_cp(1))(h, gain, target)


def _gated_headnorm(o, g, gain):
    return o * _rms(o) * gain * (g * _sigmoid(g))


def _row_mask(chunk, size=CHUNK):
    rows = chunk * size + lax.broadcasted_iota(jnp.int32, (size, 1), 0)
    return (rows >= FRONT - N_META).astype(F32)


def _ret_head(q1, q2, k1, k2, v, g, state, gain, cos, sin, dmat, dq, dk, dc):
    q = jnp.concatenate([q1 * cos - q2 * sin, q1 * sin + q2 * cos], axis=-1)
    k = jnp.concatenate([k1 * cos - k2 * sin, k1 * sin + k2 * cos], axis=-1) * (RET_DK ** -0.5)
    scores = _nt(q, k) * dmat
    o = _nn(scores, v) + _nn(q * dq, state)
    new_state = state * dc + _tn(k * dk, v)
    return _gated_headnorm(o, g, gain), new_state


def _ret_consts():
    log_gamma = jnp.log1p(-2.0 ** (-5.0 - jnp.arange(HEADS, dtype=F32)))
    idx = jnp.arange(RET_CHUNK, dtype=F32)
    rel = idx[:, None] - idx[None, :]
    dmat = jnp.where(rel >= 0, jnp.exp(log_gamma[:, None, None] * jnp.maximum(rel, 0.0)), 0.0)
    dq = jnp.exp(log_gamma[:, None] * (idx + 1.0))[..., None]
    dk = jnp.exp(log_gamma[:, None] * (RET_CHUNK - 1.0 - idx))[..., None]
    dc = jnp.broadcast_to(jnp.exp(log_gamma * RET_CHUNK)[:, None, None], (HEADS, 1, 128))
    return dmat, dq, dk, dc


def _rope_tables(tp):
    half = RET_DK // 2
    inv = 1.0 / (ROPE_BASE ** jnp.linspace(0.0, 1.0, half, dtype=F32))
    pos = (jnp.arange(tp) - (FRONT - N_META)).astype(F32)
    ang = pos[:, None] * inv[None, :]
    return jnp.cos(ang), jnp.sin(ang)


_RET_V0, _RET_G0 = 2 * D, 4 * D


def _heads(ref, start, width, stride=None):
    stride = width if stride is None else stride
    return jnp.stack([ref[:, start + stride * h:start + stride * h + width].astype(F32) for h in range(HEADS)])


def _put_heads(ref, start, value, mask, stride=None):
    width = value.shape[-1]
    stride = width if stride is None else stride
    for h in range(HEADS):
        ref[:, start + stride * h:start + stride * h + width] = (value[h] * mask).astype(ref.dtype)


def _d_on(dh_ref, w_ref, width):
    d_on = lax.dot_general(dh_ref[...].astype(BF16), w_ref[...], (((1,), (1,)), ((), ())), preferred_element_type=F32)
    return jnp.stack([d_on[:, width * h:width * (h + 1)] for h in range(HEADS)])


def _ret_pieces(u_ref):
    hk = RET_DK // 2
    return (_heads(u_ref, 0, hk, RET_DK), _heads(u_ref, hk, hk, RET_DK), _heads(u_ref, D, hk, RET_DK),
            _heads(u_ref, D + hk, hk, RET_DK), _heads(u_ref, _RET_V0, RET_DV), _heads(u_ref, _RET_G0, RET_DV))


def _ret_const_specs(rev=None):
    c = (lambda n: (rev(n), 0)) if rev else (lambda n: (n, 0))
    z3 = lambda n: (0, 0, 0)
    return [pl.BlockSpec((RET_CHUNK, RET_DK // 2), c), pl.BlockSpec((RET_CHUNK, RET_DK // 2), c),
            pl.BlockSpec((HEADS, RET_CHUNK, RET_CHUNK), z3), pl.BlockSpec((HEADS, RET_CHUNK, 1), z3),
            pl.BlockSpec((HEADS, RET_CHUNK, 1), z3), pl.BlockSpec((HEADS, 1, 128), z3)]


def _ret_fwd(u, gain, rope, h, w_out, name):
    tp = u.shape[0]
    nch = tp // RET_CHUNK
    cos, sin = rope
    dmat, dq, dk, dc = _ret_consts()

    def body(u_ref, gain_ref, h_ref, w_ref, cos_ref, sin_ref, dmat_ref, dq_ref, dk_ref, dc_ref,
             on_ref, st_ref, hmix_ref, state_ref):
        @pl.when(pl.program_id(0) == 0)
        def _():
            state_ref[...] = jnp.zeros_like(state_ref)

        state = state_ref[...]
        st_ref[...] = state.astype(BF16)
        on, new_state = _ret_head(*_ret_pieces(u_ref), state, _heads(gain_ref, 0, RET_DV), cos_ref[...], sin_ref[...],
                                  dmat_ref[...], dq_ref[...], dk_ref[...], dc_ref[...][:, :, :1])
        state_ref[...] = new_state
        _put_heads(on_ref, 0, on, 1.0)
        hmix_ref[...] = h_ref[...] + jnp.dot(on_ref[...], w_ref[...], preferred_element_type=F32)

    rows = lambda width: pl.BlockSpec((RET_CHUNK, width), lambda n: (n, 0))
    return pl.pallas_call(
        body, name=name, grid=(nch,),
        in_specs=[rows(6 * D), pl.BlockSpec((1, HEADS * RET_DV), lambda n: (0, 0)), rows(D),
                  _resident(w_out.shape, 1)] + _ret_const_specs(),
        out_specs=[rows(HEADS * RET_DV), pl.BlockSpec((None, HEADS, RET_DK, RET_DV), lambda n: (n, 0, 0, 0)), rows(D)],
        out_shape=[jax.ShapeDtypeStruct((tp, HEADS * RET_DV), BF16),
                   jax.ShapeDtypeStruct((nch, HEADS, RET_DK, RET_DV), BF16), jax.ShapeDtypeStruct((tp, D), F32)],
        scratch_shapes=[pltpu.VMEM((HEADS, RET_DK, RET_DV), F32)],
        compiler_params=_cp(1))(u, gain, h, w_out, cos, sin, dmat, dq, dk, dc)


def _ret_bwd(u, gain, rope, states, dh, w_out, name):
    tp = u.shape[0]
    nch = tp // RET_CHUNK
    cos, sin = rope
    dmat, dq, dk, dc = _ret_consts()
    rev = lambda n: nch - 1 - n
    hk = RET_DK // 2

    def body(u_ref, gain_ref, st_ref, dh_ref, w_ref, cos_ref, sin_ref, dmat_ref, dq_ref, dk_ref, dc_ref,
             du_ref, dgain_ref, dstate_ref):
        @pl.when(pl.program_id(0) == 0)
        def _():
            dstate_ref[...] = jnp.zeros_like(dstate_ref)
            dgain_ref[...] = jnp.zeros_like(dgain_ref)

        mask = _row_mask(rev(pl.program_id(0)), RET_CHUNK)
        consts = (cos_ref[...], sin_ref[...], dmat_ref[...], dq_ref[...], dk_ref[...], dc_ref[...][:, :, :1])
        _, vjp = jax.vjp(lambda *a: _ret_head(*a, *consts), *_ret_pieces(u_ref), st_ref[...].astype(F32),
                         _heads(gain_ref, 0, RET_DV))
        dq1, dq2, dk1, dk2, dv, dg, dstate, dgain = vjp((_d_on(dh_ref, w_ref, RET_DV), dstate_ref[...]))
        dstate_ref[...] = dstate
        for hd in range(HEADS):
            dgain_ref[:, RET_DV * hd:RET_DV * (hd + 1)] += dgain[hd]
        _put_heads(du_ref, 0, dq1, mask, RET_DK)
        _put_heads(du_ref, hk, dq2, mask, RET_DK)
        _put_heads(du_ref, D, dk1, mask, RET_DK)
        _put_heads(du_ref, D + hk, dk2, mask, RET_DK)
        _put_heads(du_ref, _RET_V0, dv, mask)
        _put_heads(du_ref, _RET_G0, dg, mask)

    return pl.pallas_call(
        body, name=name, grid=(nch,),
        in_specs=[pl.BlockSpec((RET_CHUNK, 6 * D), lambda n: (rev(n), 0)),
                  pl.BlockSpec((1, HEADS * RET_DV), lambda n: (0, 0)),
                  pl.BlockSpec((None, HEADS, RET_DK, RET_DV), lambda n: (rev(n), 0, 0, 0)),
                  pl.BlockSpec((RET_CHUNK, D), lambda n: (rev(n), 0)), _resident(w_out.shape, 1)]
                 + _ret_const_specs(rev),
        out_specs=[pl.BlockSpec((RET_CHUNK, 6 * D), lambda n: (rev(n), 0)),
                   pl.BlockSpec((1, HEADS * RET_DV), lambda n: (0, 0))],
        out_shape=[jax.ShapeDtypeStruct((tp, 6 * D), BF16), jax.ShapeDtypeStruct((1, HEADS * RET_DV), F32)],
        scratch_shapes=[pltpu.VMEM((HEADS, RET_DK, RET_DV), F32)],
        compiler_params=_cp(1))(u, gain, states, dh, w_out, cos, sin, dmat, dq, dk, dc)


_GLA_K0, _GLA_V0, _GLA_G0, _GLA_Z0 = 512, 1024, 2048, 3072


def _gla_head(q, k, v, g, z, state_t, wg, bg, gain, mask, lo, lo_t, loc, loc_t):
    ga = _nn(jnp.broadcast_to(z, wg.shape[:-2] + z.shape), wg) + bg
    log_a = (jnp.minimum(ga, 0.0) - jnp.log(1.0 + jnp.exp(-jnp.abs(ga)))) * (mask * (1.0 / GLA_TAU))
    bcum = _cum(lo, lo_t, log_a)
    bmid = _cum(loc, loc_t, log_a)
    btot = jnp.sum(log_a, axis=-2, keepdims=True)
    qs = q * (GLA_DK ** -0.5)
    causal = lax.broadcasted_iota(jnp.int32, (CHUNK, CHUNK), 0) >= lax.broadcasted_iota(jnp.int32, (CHUNK, CHUNK), 1)
    scores = jnp.where(causal, _nt(qs * jnp.exp(bmid), k * jnp.exp(-bmid)), 0.0)
    o = _nn(scores, v) + _nt(qs * jnp.exp(bcum), state_t)
    new_state_t = state_t * jnp.exp(btot) + _tn(v, k * jnp.exp(btot - bcum))
    return _gated_headnorm(o, g, gain), new_state_t


def _cum_mats():
    r = lax.broadcasted_iota(jnp.int32, (CHUNK, CHUNK), 0)
    c = lax.broadcasted_iota(jnp.int32, (CHUNK, CHUNK), 1)
    mid = CHUNK // 2
    low = lambda a, b: (a >= b).astype(F32)
    lo, lo_t = low(r, c), low(c, r)
    loc = lo - (c <= mid).astype(F32)
    loc_t = lo_t - (r <= mid).astype(F32)
    return tuple(m.astype(BF16) for m in (lo, lo_t, loc, loc_t))


def _gla_pieces(u_ref):
    return (_heads(u_ref, 0, GLA_DK), _heads(u_ref, _GLA_K0, GLA_DK), _heads(u_ref, _GLA_V0, GLA_DV),
            _heads(u_ref, _GLA_G0, GLA_DV), u_ref[:, _GLA_Z0:_GLA_Z0 + 128].astype(F32))


def _gla_fwd(u, wg, bg, gain, h, w_out, name):
    tp = u.shape[0]
    nch = tp // CHUNK

    def body(u_ref, wg_ref, bg_ref, gain_ref, h_ref, w_ref, on_ref, st_ref, hmix_ref, state_ref):
        @pl.when(pl.program_id(0) == 0)
        def _():
            state_ref[...] = jnp.zeros_like(state_ref)

        state = state_ref[...]
        st_ref[...] = state.astype(BF16)
        on, new_state = _gla_head(*_gla_pieces(u_ref), state, _heads(wg_ref, 0, GLA_DK), _heads(bg_ref, 0, GLA_DK),
                                  _heads(gain_ref, 0, GLA_DV), _row_mask(pl.program_id(0)), *_cum_mats())
        state_ref[...] = new_state
        _put_heads(on_ref, 0, on, 1.0)
        hmix_ref[...] = h_ref[...] + jnp.dot(on_ref[...], w_ref[...], preferred_element_type=F32)

    rows = lambda width: pl.BlockSpec((CHUNK, width), lambda n: (n, 0))
    full = lambda r, c: pl.BlockSpec((r, c), lambda n: (0, 0))
    return pl.pallas_call(
        body, name=name, grid=(nch,),
        in_specs=[rows(GLA_U), full(128, HEADS * GLA_DK), full(1, HEADS * GLA_DK), full(1, HEADS * GLA_DV), rows(D),
                  _resident(w_out.shape, 1)],
        out_specs=[rows(HEADS * GLA_DV), pl.BlockSpec((None, HEADS, GLA_DV, GLA_DK), lambda n: (n, 0, 0, 0)), rows(D)],
        out_shape=[jax.ShapeDtypeStruct((tp, HEADS * GLA_DV), BF16),
                   jax.ShapeDtypeStruct((nch, HEADS, GLA_DV, GLA_DK), BF16), jax.ShapeDtypeStruct((tp, D), F32)],
        scratch_shapes=[pltpu.VMEM((HEADS, GLA_DV, GLA_DK), F32)],
        compiler_params=_cp(1))(u, wg, bg, gain, h, w_out)


def _gla_bwd(u, wg, bg, gain, states, dh, w_out, name):
    tp = u.shape[0]
    nch = tp // CHUNK
    rev = lambda n: nch - 1 - n

    def body(u_ref, wg_ref, bg_ref, gain_ref, st_ref, dh_ref, w_ref, du_ref, dwg_ref, dbg_ref, dgain_ref, dstate_ref):
        @pl.when(pl.program_id(0) == 0)
        def _():
            dstate_ref[...] = jnp.zeros_like(dstate_ref)
            dwg_ref[...] = jnp.zeros_like(dwg_ref)
            dbg_ref[...] = jnp.zeros_like(dbg_ref)
            dgain_ref[...] = jnp.zeros_like(dgain_ref)

        mask = _row_mask(rev(pl.program_id(0)))
        mats = _cum_mats()
        _, vjp = jax.vjp(lambda *a: _gla_head(*a, mask, *mats), *_gla_pieces(u_ref), st_ref[...].astype(F32),
                         _heads(wg_ref, 0, GLA_DK), _heads(bg_ref, 0, GLA_DK), _heads(gain_ref, 0, GLA_DV))
        dq, dk, dv, dg, dz, dstate, dwg, dbg, dgain = vjp((_d_on(dh_ref, w_ref, GLA_DV), dstate_ref[...]))
        dstate_ref[...] = dstate
        for hd in range(HEADS):
            dwg_ref[:, GLA_DK * hd:GLA_DK * (hd + 1)] += dwg[hd]
            dbg_ref[:, GLA_DK * hd:GLA_DK * (hd + 1)] += dbg[hd]
            dgain_ref[:, GLA_DV * hd:GLA_DV * (hd + 1)] += dgain[hd]
        _put_heads(du_ref, 0, dq, mask)
        _put_heads(du_ref, _GLA_K0, dk, mask)
        _put_heads(du_ref, _GLA_V0, dv, mask)
        _put_heads(du_ref, _GLA_G0, dg, mask)
        du_ref[:, _GLA_Z0:_GLA_Z0 + 128] = dz.astype(BF16)
        du_ref[:, _GLA_Z0 + 128:] = jnp.zeros((CHUNK, GLA_U - _GLA_Z0 - 128), BF16)

    full = lambda r, c: pl.BlockSpec((r, c), lambda n: (0, 0))
    return pl.pallas_call(
        body, name=name, grid=(nch,),
        in_specs=[pl.BlockSpec((CHUNK, GLA_U), lambda n: (rev(n), 0)), full(128, HEADS * GLA_DK),
                  full(1, HEADS * GLA_DK), full(1, HEADS * GLA_DV),
                  pl.BlockSpec((None, HEADS, GLA_DV, GLA_DK), lambda n: (rev(n), 0, 0, 0)),
                  pl.BlockSpec((CHUNK, D), lambda n: (rev(n), 0)), _resident(w_out.shape, 1)],
        out_specs=[pl.BlockSpec((CHUNK, GLA_U), lambda n: (rev(n), 0)), full(128, HEADS * GLA_DK),
                   full(1, HEADS * GLA_DK), full(1, HEADS * GLA_DV)],
        out_shape=[jax.ShapeDtypeStruct((tp, GLA_U), BF16), jax.ShapeDtypeStruct((128, HEADS * GLA_DK), F32),
                   jax.ShapeDtypeStruct((1, HEADS * GLA_DK), F32), jax.ShapeDtypeStruct((1, HEADS * GLA_DV), F32)],
        scratch_shapes=[pltpu.VMEM((HEADS, GLA_DV, GLA_DK), F32)],
        compiler_params=_cp(1))(u, wg, bg, gain, states, dh, w_out)


def _ffn_fwd(h, gain, w_in, w_out, tag):
    hn, ug, uu, act = _norm_ffn_in(h, gain, w_in, f"{tag}_in")
    if callable(w_out):
        w_out = w_out(act)
    return _out_proj(act, w_out, h, 0.5, f"{tag}_out"), (h, hn, ug, uu, act), w_out


def _ffn_dgrad(dh, w_out, w_in, act_dg, act_du, h, gain, name):
    tp, d = dh.shape
    ff = w_out.shape[0]
    tm = TM_SMALL
    nt = (((1,), (1,)), ((), ()))

    def body(dh_ref, wo_ref, wi_ref, dg_ref, du_ref, h_ref, g_ref, o_ref, dhi_ref, dgain_ref):
        @pl.when(pl.program_id(0) == 0)
        def _():
            dgain_ref[...] = jnp.zeros_like(dgain_ref)

        dho = dh_ref[...]
        dact = lax.dot_general((0.5 * dho).astype(BF16), wo_ref[...], nt, preferred_element_type=F32)
        d_gate = (dact * dg_ref[...].astype(F32)).astype(BF16)
        d_up = (dact * du_ref[...].astype(F32)).astype(BF16)
        o_ref[:, :ff] = d_gate
        o_ref[:, ff:] = d_up
        dhn = (lax.dot_general(d_gate, wi_ref[:, :ff], nt, preferred_element_type=F32)
               + lax.dot_general(d_up, wi_ref[:, ff:], nt, preferred_element_type=F32))
        dx, xhat = _rmsnorm_bwd(dhn, h_ref[...], g_ref[...])
        dgain_ref[...] += jnp.sum(dhn * xhat, axis=0, keepdims=True)
        dhi_ref[...] = dho + dx

    rows = lambda width: pl.BlockSpec((tm, width), lambda i: (i, 0))
    return pl.pallas_call(
        body, name=name, grid=(tp // tm,),
        in_specs=[rows(d), _resident(w_out.shape, 1), _resident(w_in.shape, 1), rows(ff), rows(ff), rows(d),
                  pl.BlockSpec((1, d), lambda i: (0, 0))],
        out_specs=[rows(2 * ff), rows(d), pl.BlockSpec((1, d), lambda i: (0, 0))],
        out_shape=[jax.ShapeDtypeStruct((tp, 2 * ff), BF16), jax.ShapeDtypeStruct((tp, d), F32),
                   jax.ShapeDtypeStruct((1, d), F32)],
        compiler_params=_cp(1))(dh, w_out, w_in, act_dg, act_du, h, gain)


def _ffn_bwd(dh, saved, gain, w_in, w_out, tag, push):
    h, hn, act_dg, act_du, act = saved
    du, dh_in, d_gain = _ffn_dgrad(dh, w_out, w_in, act_dg, act_du, h, gain, f"{tag}_dgrad")
    d_w_out = _wgrad(act, dh, bm=D_FF // 2, bn=D, scale=0.5, sharded=False, name=f"{tag}_dwout")
    d_w_in = _wgrad(hn, du, bm=D, bn=D_FF, scale=1.0, sharded=False, name=f"{tag}_dwin")
    return dh_in, d_gain, push([("cols", d_w_in), d_w_out])


def _sequence_grads(x, target, p, weights, grads):
    row = lambda v, token: v.reshape(1, -1) + token[0, 0]
    gains = {}

    tok = weights.start(1, weights.start(0, None))
    weights.pin = tok
    h = jnp.concatenate([jnp.zeros((FRONT, D), F32), x], axis=0) + tok[0, 0]
    rope = _rope_tables(h.shape[0])
    w = weights.wait(0, [tok, h, *rope, *weights.later_shards(2)])
    tok = weights.start(2, w["l0_ffn1_in"])
    h = lax.dynamic_update_slice(h, w["meta"], (FRONT - N_META, 0))
    gains["l0_ffn1"] = row(p["norm_ffn1"][0], tok)
    h, s1, w["l0_ffn1_out"] = _ffn_fwd(h, gains["l0_ffn1"], w["l0_ffn1_in"],
                                       lambda act: weights.wait(1, act)["l0_ffn1_out"], "l0_ffn1")
    w.update(weights.wait(2, h))
    tok = weights.start(4, weights.start(3, w["ret_in"]))
    gains["ret"] = row(p["norm_mix"][0], tok)
    hn, u = _norm_proj(h, gains["ret"], w["ret_in"], "ret_in")
    w.update(weights.wait(3, u))
    on, states, h_mix = _ret_fwd(u, w["ret_gain"], rope, h, w["ret_out"], "ret_fwd")
    s2 = (h, hn, u, on, states)
    w.update(weights.wait(4, h_mix))
    tok = weights.start(5, w["l0_ffn2_in"])
    gains["l0_ffn2"] = row(p["norm_ffn2"][0], tok)
    h, s3, _ = _ffn_fwd(h_mix, gains["l0_ffn2"], w["l0_ffn2_in"], w["l0_ffn2_out"], "l0_ffn2")
    saved = [(s1, s2, s3)]

    w.update(weights.wait(5, h))
    tok = weights.start(6, w["l1_ffn1_in"])
    gains["l1_ffn1"] = row(p["norm_ffn1"][1], tok)
    h, s1, _ = _ffn_fwd(h, gains["l1_ffn1"], w["l1_ffn1_in"], w["l1_ffn1_out"], "l1_ffn1")
    w.update(weights.wait(6, h))
    tok = weights.start(7, w["gla_out"])
    gains["gla"] = row(p["norm_mix"][1], tok)
    hn, u = _norm_proj(h, gains["gla"], w["gla_in"], "gla_in")
    on, states, h_mix = _gla_fwd(u, w["gla_wg"], w["gla_bg"], w["gla_gain"], h, w["gla_out"], "gla_fwd")
    s2 = (h, hn, u, on, states)
    w.update(weights.wait(7, h_mix))
    gains["l1_ffn2"] = p["norm_ffn2"][1].reshape(1, -1)
    h, s3, _ = _ffn_fwd(h_mix, gains["l1_ffn2"], w["l1_ffn2_in"], w["l1_ffn2_out"], "l1_ffn2")
    saved.append((s1, s2, s3))

    dh, d_final, loss = _loss_head(h, p["final_norm"].reshape(1, -1), target, "loss_head")
    small = {"final_norm": d_final, "norm_ffn1": [None, None], "norm_mix": [None, None], "norm_ffn2": [None, None]}
    pusher = lambda k: functools.partial(grads.push, k)

    s1, s2, s3 = saved[1]
    dh, small["norm_ffn2"][1], tok = _ffn_bwd(dh, s3, gains["l1_ffn2"], w["l1_ffn2_in"], w["l1_ffn2_out"], "l1_ffn2",
                                              pusher(0))
    h_in, hn, u, on, states = s2
    d_out = _wgrad(on, dh, bm=D, bn=D, scale=1.0, sharded=False, name="gla_dwout")
    du, small["gla_wg"], small["gla_bg"], small["gla_gain"] = _gla_bwd(
        u, w["gla_wg"], w["gla_bg"], w["gla_gain"] + tok[0, 0], states, dh, w["gla_out"], "gla_bwd")
    d_in = _wgrad(hn, du, bm=D, bn=GLA_U, scale=1.0, sharded=False, name="gla_dwin")
    d_in = jnp.moveaxis(d_in[:, :GLA_IN].reshape(D, N_CHIPS, -1), 1, 0)
    tok = grads.push(1, [d_in, d_out])
    dh, small["norm_mix"][1] = _dgrad_norm(du, w["gla_in"], h_in, gains["gla"] + tok[0, 0], dh, "gla_dnorm")
    dh, small["norm_ffn1"][1], tok = _ffn_bwd(dh, s1, gains["l1_ffn1"], w["l1_ffn1_in"], w["l1_ffn1_out"], "l1_ffn1",
                                              pusher(2))

    s1, s2, s3 = saved[0]
    dh, small["norm_ffn2"][0], tok = _ffn_bwd(dh, s3, gains["l0_ffn2"] + tok[0, 0], w["l0_ffn2_in"],
                                              w["l0_ffn2_out"], "l0_ffn2", pusher(3))
    h_in, hn, u, on, states = s2
    d_out = _wgrad(on, dh, bm=D, bn=D, scale=1.0, sharded=False, name="ret_dwout")
    du, small["ret_gain"] = _ret_bwd(u, w["ret_gain"] + tok[0, 0], rope, states, dh, w["ret_out"], "ret_bwd")
    d_in = _wgrad(hn, du, bm=D, bn=w["ret_in"].shape[2], scale=1.0, sharded=True, name="ret_dwin")
    tok = grads.push(4, [d_in, d_out])
    dh, small["norm_mix"][0] = _dgrad_norm(du, w["ret_in"], h_in, gains["ret"] + tok[0, 0], dh, "ret_dnorm")
    dh, small["norm_ffn1"][0], tok = _ffn_bwd(dh, s1, gains["l0_ffn1"], w["l0_ffn1_in"], w["l0_ffn1_out"], "l0_ffn1",
                                              pusher(5))
    grads.push(6, [], [dh[FRONT - N_META:FRONT], *small["norm_ffn1"], *small["norm_mix"], *small["norm_ffn2"],
                       small["final_norm"], small["ret_gain"], small["gla_wg"][:GLA_RANK], small["gla_bg"],
                       small["gla_gain"], loss[:, :1] + tok[0, 0]])
    return dh[FRONT:]


_HBM = pl.BlockSpec(memory_space=pl.ANY)


def _place():
    return lax.axis_index("x"), lax.axis_index("y"), lax.axis_index("c")


def _flip(v, bit):
    return 1 - v if bit else v


DMA_CHUNK_BYTES = 128 * 1024


def _row_chunks(ref):
    rows, cols = ref.shape
    step = _row_tile(rows, max(16, DMA_CHUNK_BYTES // (cols * ref.dtype.itemsize)))
    return [pl.ds(a, step) for a in range(0, rows, step)]


def _whole(src, dst, send_sem, recv_sem, peer):
    return pltpu.make_async_remote_copy(src_ref=src, dst_ref=dst, send_sem=send_sem, recv_sem=recv_sem,
                                        device_id=peer, device_id_type=MESH)


def _send(src, dst, send_sem, recv_sem, peer):
    for rows in _row_chunks(src):
        _whole(src.at[rows], dst.at[rows], send_sem, recv_sem, peer).start()
    return _whole(src, dst, send_sem, recv_sem, peer)


_HBM_ONLY = pl.BlockSpec(memory_space=pltpu.HBM)
_SEMS = pl.BlockSpec(memory_space=pltpu.SEMAPHORE)
_SIDE_EFFECT = pltpu.CompilerParams(has_side_effects=pltpu.SideEffectType.DATAFLOW_SIDE_EFFECTING)
_GATHER_FLIPS = [(1, 0, 0), (0, 1, 0), (1, 1, 0), (0, 0, 1)]
_PEER_FLIPS = [(fx, fy, fc) for fx in (0, 1) for fy in (0, 1) for fc in (0, 1)][1:]


def _zero_token():
    return jnp.zeros((8, 128), F32)


def _exchange_start(srcs, lands, route, flips, after, name):
    n = len(srcs)

    def body(*refs):
        src, land = refs[:n], refs[n:2 * n]
        send_sems, recv_sems, token = refs[2 * n + 1], refs[2 * n + 2], refs[-1]
        me = _place()
        for t in range(n):
            for j, flip in enumerate(flips):
                peer = tuple(_flip(v, f) for v, f in zip(me, flip))
                s, d = route(t, src[t], land[t], me, peer)
                _send(s, d, send_sems.at[t * len(flips) + j], recv_sems.at[t * len(flips) + j], peer)
        token[...] = jnp.zeros_like(token)

    hbm = lambda a: pltpu.HBM(a.shape, a.dtype)
    sems = pltpu.SemaphoreType.DMA((n * len(flips),))
    operands = [pltpu.with_memory_space_constraint(a, pltpu.HBM) for a in list(srcs) + list(lands)]
    out = pl.pallas_call(
        body, name=name, in_specs=[_HBM_ONLY] * (2 * n) + [_HBM],
        out_shape=(sems, sems, *[hbm(a) for a in operands], jax.ShapeDtypeStruct((8, 128), F32)),
        out_specs=(_SEMS, _SEMS, *[_HBM_ONLY] * (2 * n), pl.BlockSpec(memory_space=pltpu.VMEM)),
        input_output_aliases={i: 2 + i for i in range(2 * n)}, compiler_params=_SIDE_EFFECT,
    )(*operands, _zero_token() if after is None else after)
    return (out[0], out[1], out[2:2 + n], out[2 + n:2 + 2 * n]), out[-1]


def _exchange_wait(started, route, flips, after, name):
    send_sems, recv_sems, srcs, lands = started
    n = len(srcs)

    def body(*refs):
        src, land = refs[:n], refs[n:2 * n]
        send_sems, recv_sems = refs[2 * n], refs[2 * n + 1]
        me = _place()
        for t in range(n):
            for j, flip in enumerate(flips):
                peer = tuple(_flip(v, f) for v, f in zip(me, flip))
                s, d = route(t, src[t], land[t], me, peer)
                cp = _whole(s, d, send_sems.at[t * len(flips) + j], recv_sems.at[t * len(flips) + j], peer)
                cp.wait_send()
                cp.wait_recv()

    hbm = lambda a: pltpu.HBM(a.shape, a.dtype)
    after = list(after) if isinstance(after, (list, tuple)) else [after]
    out = pl.pallas_call(
        body, name=name, in_specs=[_HBM_ONLY] * (2 * n) + [_SEMS, _SEMS] + [_HBM] * len(after),
        out_shape=tuple(hbm(a) for a in list(srcs) + list(lands)), out_specs=tuple([_HBM_ONLY] * (2 * n)),
        input_output_aliases={i: i for i in range(2 * n)}, compiler_params=_SIDE_EFFECT,
    )(*srcs, *lands, send_sems, recv_sems, *after)
    return out[:n], out[n:]


def _gather_route(t, src, land, me, peer):
    mine = 2 * me[0] + me[1]
    if land.ndim == 3:
        return src, land.at[mine]
    cols = src.shape[1]
    return src, land.at[:, pl.ds(pl.multiple_of(mine * cols, 128), cols)]


def _scatter_route(n_pieces):
    def route(t, src, land, me, peer):
        chip = 2 * peer[0] + peer[1]
        if t >= n_pieces:
            part = src
        elif src.ndim == 4:
            part = src.at[chip, peer[2]]
        else:
            rows, cols = land.shape[1:]
            part = src.at[pl.ds(pl.multiple_of(peer[2] * rows, 16), rows), pl.ds(pl.multiple_of(chip * cols, 128), cols)]
        return part, land.at[4 * me[0] + 2 * me[1] + me[2]]

    return route


def _swap_cores(halves, name):
    n = len(halves)

    def body(*refs):
        src, dst = refs[:n], refs[n:2 * n]
        send_sems, recv_sems = refs[2 * n:]
        x, y, c = _place()
        copies = [_send(src[t], dst[t], send_sems.at[t], recv_sems.at[t], (x, y, 1 - c)) for t in range(n)]
        for cp in copies:
            cp.wait()

    got = pl.pallas_call(
        body, name=name, in_specs=[_HBM] * n, out_specs=[_HBM] * n,
        out_shape=[jax.ShapeDtypeStruct(a.shape, a.dtype) for a in halves],
        scratch_shapes=[pltpu.SemaphoreType.DMA((n,)), pltpu.SemaphoreType.DMA((n,))],
    )(*halves)
    south = lax.axis_index("c") == 0
    return [jnp.stack([jnp.where(south, a, b), jnp.where(south, b, a)]) for a, b in zip(halves, got)]


def _row_tile(rows, cap):
    fits = [t for t in range(16, cap + 1, 16) if rows % t == 0]
    return fits[-1] if fits else rows


def _sum_slots(a, name):
    _, r, c = a.shape
    tr = _row_tile(r, 384)

    def body(a_ref, o_ref):
        s = a_ref[0].astype(F32)
        for k in range(1, N_DEV):
            s = s + a_ref[k].astype(F32)
        o_ref[...] = s

    return pl.pallas_call(
        body, name=name, grid=(r // tr,),
        in_specs=[pl.BlockSpec((N_DEV, tr, c), lambda i: (0, i, 0))],
        out_specs=pl.BlockSpec((tr, c), lambda i: (i, 0)),
        out_shape=jax.ShapeDtypeStruct((r, c), F32),
        compiler_params=_cp(1))(a)


def _adamw(w, g, m, v, name):
    layers, r, c = w.shape
    tr = _row_tile(r, 256)

    def body(w_ref, g_ref, m_ref, v_ref, d_ref, nm_ref, nv_ref):
        gv = g_ref[...]
        nm = ADAM_B1 * m_ref[...] + (1.0 - ADAM_B1) * gv
        nv = ADAM_B2 * v_ref[...] + (1.0 - ADAM_B2) * (gv * gv)
        m_hat = nm / (1.0 - ADAM_B1 ** ADAM_STEP)
        v_hat = nv / (1.0 - ADAM_B2 ** ADAM_STEP)
        d_ref[...] = -ADAM_LR * (m_hat / (jnp.sqrt(v_hat) + ADAM_EPS) + ADAM_WD * w_ref[...])
        nm_ref[...] = nm
        nv_ref[...] = nv

    spec = pl.BlockSpec((None, tr, c), lambda a, i: (a, i, 0))
    return pl.pallas_call(
        body, name=name, grid=(layers, r // tr), in_specs=[spec] * 4, out_specs=[spec] * 3,
        out_shape=[jax.ShapeDtypeStruct((layers, r, c), F32)] * 3,
        compiler_params=_cp(2))(w, g, m, v)


_SMALL = ["meta_tokens", "ret_head_norm", "gla_w_gate", "gla_b_gate", "gla_head_norm"]
_LOCAL_SMALL = ["meta_tokens", "norm_ffn1", "norm_mix", "norm_ffn2", "ret_head_norm", "gla_w_gate", "gla_b_gate",
                "gla_head_norm", "final_norm"]
_BIG = ["ffn1_w_in", "ffn1_w_out", "ffn2_w_in", "ffn2_w_out", "ret_w_in", "ret_w_out", "gla_w_in", "gla_w_out"]
_WEIGHTS = ["meta_tokens", "norm_ffn1", "ffn1_w_in", "ffn1_w_out", "norm_mix", "norm_ffn2", "ffn2_w_in", "ffn2_w_out",
            "ret_w_in", "ret_head_norm", "ret_w_out", "gla_w_in", "gla_w_gate", "gla_b_gate", "gla_head_norm",
            "gla_w_out", "final_norm"]


def _pack_rows(arrays, width):
    flat = jnp.concatenate([a.reshape(-1) for a in arrays])
    pad = -flat.shape[0] % (8 * width)
    return jnp.pad(flat, (0, pad)).reshape(-1, width)


def _unpack_rows(packed, shapes):
    flat, out, at = packed.reshape(-1), [], 0
    for s in shapes:
        size = 1
        for dim in s:
            size *= dim
        out.append(flat[at:at + size].reshape(s))
        at += size
    return out


class _WeightGather:
    GROUPS = [("small", "l0_ffn1_in"), ("l0_ffn1_out",), ("ret_in",), ("ret_out",), ("l0_ffn2_in", "l0_ffn2_out"),
              ("l1_ffn1_in", "l1_ffn1_out"), ("gla_in", "gla_out"), ("l1_ffn2_in", "l1_ffn2_out")]

    def __init__(self, p):
        self.small_shapes = [p[name].shape for name in _SMALL]
        self.f32 = {"small": _pack_rows([p[name] for name in _SMALL], 128), "ret_in": p["ret_w_in"][0],
                    "ret_out": p["ret_w_out"][0], "gla_in": p["gla_w_in"][0], "gla_out": p["gla_w_out"][0]}
        for layer in range(2):
            for name in ("ffn1", "ffn2"):
                self.f32[f"l{layer}_{name}_in"] = p[f"{name}_w_in"][layer]
                self.f32[f"l{layer}_{name}_out"] = p[f"{name}_w_out"][layer]
        self.shards = {}
        self.started = {}
        self.pin = None

    def shard(self, name):
        if name not in self.shards:
            a = self.f32[name]
            if name != "small":
                a = (a if self.pin is None else a + self.pin[0, 0]).astype(BF16)
            self.shards[name] = a
        return self.shards[name]

    def later_shards(self, k):
        return [self.shard(name) for group in self.GROUPS[k:] for name in group]

    def start(self, k, after):
        shards = [self.shard(name) for name in self.GROUPS[k]]
        lands = []
        for name, s in zip(self.GROUPS[k], shards):
            if "ffn" in name and name.endswith("_in"):
                lands.append(lax.empty((s.shape[0], N_CHIPS * s.shape[1]), s.dtype))
            else:
                lands.append(lax.empty((N_CHIPS,) + s.shape, s.dtype))
        self.started[k], token = _exchange_start(shards, lands, _gather_route, _GATHER_FLIPS, after, f"gather{k}_start")
        return token

    def wait(self, k, after):
        _, got = _exchange_wait(self.started[k], _gather_route, _GATHER_FLIPS, after, f"gather{k}_wait")
        w = {}
        for name, g in zip(self.GROUPS[k], got):
            if name == "small":
                parts = zip(*[_unpack_rows(g[chip], self.small_shapes) for chip in range(N_CHIPS)])
                cat = lambda a: jnp.moveaxis(a, 0, -2).reshape(a.shape[1:-1] + (-1,))
                meta, ret_gain, wg, bg, gla_gain = [cat(jnp.stack(part)) for part in parts]
                w.update(meta=meta, ret_gain=ret_gain.reshape(1, -1), gla_bg=bg.reshape(1, -1),
                         gla_gain=gla_gain.reshape(1, -1),
                         gla_wg=jnp.pad(wg[0], ((0, 128 - GLA_RANK), (0, 0))).astype(BF16))
            elif name == "gla_in":
                full = jnp.moveaxis(g, 0, 1).reshape(D, -1)
                w[name] = jnp.pad(full, ((0, 0), (0, GLA_U - GLA_IN)))[None]
            elif name.endswith("_out"):
                w[name] = g.reshape(-1, g.shape[-1])
            else:
                w[name] = g
        return w


class _GradExchange:
    def __init__(self):
        self.started = []
        self.token = None
        self.small_shapes = None

    def push(self, k, arrays, small=None):
        srcs, lands = [], []
        for a in arrays:
            if isinstance(a, tuple):
                a = a[1]
                piece = (a.shape[0] // 2, a.shape[1] // N_CHIPS)
            else:
                a = a.reshape(N_CHIPS, 2, -1, a.shape[-1])
                piece = a.shape[2:]
            srcs.append(a)
            lands.append(lax.empty((N_DEV,) + piece, a.dtype))
        if small is not None:
            self.small_shapes = [a.shape for a in small]
            srcs.append(_pack_rows(small, D))
            lands.append(lax.empty((N_DEV,) + srcs[-1].shape, F32))
        started, self.token = _exchange_start(srcs, lands, _scatter_route(len(arrays)), _PEER_FLIPS, None,
                                              f"scatter{k}_start")
        self.started.append((started, len(arrays)))
        return self.token

    def collect(self, groups, after=None):
        x, y, c = _place()
        after, sums = self.token if after is None else after, []
        for k in groups:
            started, n_pieces = self.started[k]
            srcs, got = _exchange_wait(started, _scatter_route(n_pieces), _PEER_FLIPS, after, f"scatter{k}_wait")
            own = []
            for t, (a, g) in enumerate(zip(srcs, got)):
                if t >= n_pieces:
                    own.append(a)
                elif a.ndim == 4:
                    own.append(a[2 * x + y, c])
                else:
                    rows, cols = g.shape[1:]
                    own.append(lax.dynamic_slice(a, (c * rows, (2 * x + y) * cols), (rows, cols)))
            got = [lax.dynamic_update_index_in_dim(g, a, 4 * x + 2 * y + c, 0) for g, a in zip(got, own)]
            sums.append([_sum_slots(a, f"sum{k}_{i}") for i, a in enumerate(got)])
            after = sums[-1][0]
        return sums


def kernel(x, meta_tokens, norm_ffn1, ffn1_w_in, ffn1_w_out, norm_mix, norm_ffn2, ffn2_w_in, ffn2_w_out, ret_w_in, ret_head_norm, ret_w_out, gla_w_in, gla_w_gate, gla_b_gate, gla_head_norm, gla_w_out, final_norm, loss_target, m_meta_tokens, m_norm_ffn1, m_ffn1_w_in, m_ffn1_w_out, m_norm_mix, m_norm_ffn2, m_ffn2_w_in, m_ffn2_w_out, m_ret_w_in, m_ret_head_norm, m_ret_w_out, m_gla_w_in, m_gla_w_gate, m_gla_b_gate, m_gla_head_norm, m_gla_w_out, m_final_norm, v_meta_tokens, v_norm_ffn1, v_ffn1_w_in, v_ffn1_w_out, v_norm_mix, v_norm_ffn2, v_ffn2_w_in, v_ffn2_w_out, v_ret_w_in, v_ret_head_norm, v_ret_w_out, v_gla_w_in, v_gla_w_gate, v_gla_b_gate, v_gla_head_norm, v_gla_w_out, v_final_norm):
    p = dict(meta_tokens=meta_tokens, norm_ffn1=norm_ffn1, ffn1_w_in=ffn1_w_in, ffn1_w_out=ffn1_w_out, norm_mix=norm_mix,
             norm_ffn2=norm_ffn2, ffn2_w_in=ffn2_w_in, ffn2_w_out=ffn2_w_out, ret_w_in=ret_w_in,
             ret_head_norm=ret_head_norm, ret_w_out=ret_w_out, gla_w_in=gla_w_in, gla_w_gate=gla_w_gate,
             gla_b_gate=gla_b_gate, gla_head_norm=gla_head_norm, gla_w_out=gla_w_out, final_norm=final_norm)
    m = dict(zip(_WEIGHTS, (m_meta_tokens, m_norm_ffn1, m_ffn1_w_in, m_ffn1_w_out, m_norm_mix, m_norm_ffn2, m_ffn2_w_in,
                            m_ffn2_w_out, m_ret_w_in, m_ret_head_norm, m_ret_w_out, m_gla_w_in, m_gla_w_gate,
                            m_gla_b_gate, m_gla_head_norm, m_gla_w_out, m_final_norm)))
    v = dict(zip(_WEIGHTS, (v_meta_tokens, v_norm_ffn1, v_ffn1_w_in, v_ffn1_w_out, v_norm_mix, v_norm_ffn2, v_ffn2_w_in,
                            v_ffn2_w_out, v_ret_w_in, v_ret_head_norm, v_ret_w_out, v_gla_w_in, v_gla_w_gate,
                            v_gla_b_gate, v_gla_head_norm, v_gla_w_out, v_final_norm)))

    exchange = _GradExchange()
    d_x = _sequence_grads(x[0], loss_target[0], p, _WeightGather(p), exchange)
    names = [("ffn2_w_in", 1), ("ffn2_w_out", 1), ("gla_w_in", 0), ("gla_w_out", 0), ("ffn1_w_in", 1), ("ffn1_w_out", 1),
             ("ffn2_w_in", 0), ("ffn2_w_out", 0), ("ret_w_in", 0), ("ret_w_out", 0), ("ffn1_w_in", 0), ("ffn1_w_out", 0)]
    shard, grads, delta, new_m, new_v = {}, {}, {}, {}, {}

    def swap(sums, keys, name):
        for key, a in zip(keys, _swap_cores(sums, name)):
            shard[key] = a.reshape(-1, a.shape[-1])

    def update(name):
        layers = p[name].shape[0]
        grads[name] = jnp.stack([shard[name, layer] for layer in range(layers)])
        delta[name], new_m[name], new_v[name] = _adamw(p[name], grads[name], m[name], v[name], f"adamw_{name}")

    swap([a for group in exchange.collect(range(5)) for a in group], names[:10], "swap_first")
    for name in ("ffn2_w_in", "ffn2_w_out", "ret_w_in", "ret_w_out", "gla_w_in", "gla_w_out"):
        update(name)
    last, (small_sum,) = exchange.collect([5, 6], after=list(delta.values()))
    swap(last, names[10:], "swap_last")
    for name in ("ffn1_w_in", "ffn1_w_out"):
        update(name)

    chip = 2 * lax.axis_index("x") + lax.axis_index("y")
    cols = lambda a, n: lax.dynamic_slice_in_dim(a, chip * n, n, axis=a.ndim - 1)
    (s_meta, s_n1a, s_n1b, s_nma, s_nmb, s_n2a, s_n2b, s_final, s_ret_gain, s_wg, s_bg, s_gla_gain,
     s_loss) = _unpack_rows(small_sum, exchange.small_shapes)
    grads.update({
        "meta_tokens": cols(s_meta, 256), "norm_ffn1": jnp.concatenate([s_n1a, s_n1b]),
        "norm_mix": jnp.concatenate([s_nma, s_nmb]), "norm_ffn2": jnp.concatenate([s_n2a, s_n2b]),
        "final_norm": s_final.reshape(D),
        "ret_head_norm": cols(s_ret_gain.reshape(1, HEADS, RET_DV), RET_DV // N_CHIPS),
        "gla_w_gate": cols(s_wg, GLA_DK)[None], "gla_b_gate": cols(s_bg, GLA_DK),
        "gla_head_norm": cols(s_gla_gain.reshape(1, HEADS, GLA_DV), GLA_DV // N_CHIPS),
    })
    packed = [_pack_rows([d[name] for name in _LOCAL_SMALL], 128)[None] for d in (p, grads, m, v)]
    out = _adamw(*packed, "adamw_small")
    shapes = [p[name].shape for name in _LOCAL_SMALL]
    for d, a in zip((delta, new_m, new_v), out):
        d.update(zip(_LOCAL_SMALL, _unpack_rows(a, shapes)))

    return (s_loss.reshape(()), d_x[None], *[grads[n] for n in _WEIGHTS], *[delta[n] for n in _WEIGHTS],
            *[new_m[n] for n in _WEIGHTS], *[new_v[n] for n in _WEIGHTS])
```

```python
import functools

import jax
import jax.numpy as jnp
from jax import lax
from jax.experimental import pallas as pl
from jax.experimental.pallas import tpu as pltpu

F32, BF16 = jnp.float32, jnp.bfloat16
MESH = pl.DeviceIdType.MESH

D = 1024
N_META = 16
CHUNK = 64
RET_CHUNK = 128
FRONT = 256
D_FF = 2816
EPS = 1e-6
HEADS = 4
RET_DK, RET_DV = 256, 512
GLA_DK, GLA_DV = 128, 256
GLA_RANK = 16
GLA_TAU = 16.0
GLA_IN = 2 * HEADS * GLA_DK + 2 * HEADS * GLA_DV + GLA_RANK
GLA_U = 3328
ROPE_BASE = 10000.0
N_CHIPS = 4
N_DEV = 8

ADAM_LR, ADAM_B1, ADAM_B2, ADAM_EPS, ADAM_WD, ADAM_STEP = 0.001, 0.9, 0.999, 1e-08, 0.01, 10

VMEM_LIMIT_BYTES = 56 * 1024 * 1024
TM = 768
TM_SMALL = 256


TM_RESIDENT = 384
MXU_TILE = 256


def _cp(n_axes):
    return pltpu.CompilerParams(dimension_semantics=("arbitrary",) * n_axes, vmem_limit_bytes=VMEM_LIMIT_BYTES)


def _resident(shape, n_axes):
    zeros = (0,) * len(shape)
    index = (lambda i: zeros) if n_axes == 1 else (lambda i, j: zeros)
    return pl.BlockSpec(shape, index, pipeline_mode=pl.Buffered(1))


def _dg(a, b, ca, cb):
    nb = a.ndim - 2
    dims = (((ca + nb,), (cb + nb,)), (tuple(range(nb)), tuple(range(nb))))
    return lax.dot_general(a.astype(BF16), b.astype(BF16), dims, preferred_element_type=F32)


@jax.custom_vjp
def _nn(a, b):
    return _dg(a, b, 1, 0)


@jax.custom_vjp
def _nt(a, b):
    return _dg(a, b, 1, 1)


@jax.custom_vjp
def _tn(a, b):
    return _dg(a, b, 0, 0)


def _dot_vjp(fn, ca, cb, da, db):
    def fwd(a, b):
        a, b = a.astype(BF16), b.astype(BF16)
        return _dg(a, b, ca, cb), (a, b)

    def bwd(res, g):
        a, b = res
        g = g.astype(BF16)
        grad = lambda other, dims, g_first: _dg(g, other, *dims) if g_first else _dg(other, g, *dims)
        return grad(b, *da), grad(a, *db)

    fn.defvjp(fwd, bwd)


_dot_vjp(_nn, 1, 0, ((1, 1), True), ((0, 0), False))
_dot_vjp(_nt, 1, 1, ((1, 0), True), ((0, 0), True))
_dot_vjp(_tn, 0, 0, ((1, 1), False), ((1, 0), False))


def _split3_dot(m, a):
    a1 = a.astype(BF16)
    r1 = a - a1.astype(F32)
    a2 = r1.astype(BF16)
    a3 = (r1 - a2.astype(F32)).astype(BF16)
    mb = jnp.broadcast_to(m, a.shape[:-2] + m.shape)
    return _dg(mb, a1, 1, 0) + _dg(mb, a2, 1, 0) + _dg(mb, a3, 1, 0)


@jax.custom_vjp
def _cum(m, mt, a):
    return _split3_dot(m, a)


_cum.defvjp(lambda m, mt, a: (_split3_dot(m, a), (m, mt)),
            lambda res, g: (jnp.zeros_like(res[0]), jnp.zeros_like(res[1]), _split3_dot(res[1], g)))


def _sigmoid(x):
    return 1.0 / (1.0 + jnp.exp(-x))


def _rms(x):
    return lax.rsqrt(jnp.mean(x * x, axis=-1, keepdims=True) + EPS)


def _rmsnorm_bwd(dy, x, gain):
    r = _rms(x)
    xhat = x * r
    dxh = dy * gain
    return r * (dxh - xhat * jnp.mean(dxh * xhat, axis=-1, keepdims=True)), xhat


def _norm_proj(h, gain, w, name):
    tp, d = h.shape
    s, _, ns = w.shape

    tm = TM_RESIDENT

    def body(h_ref, g_ref, w_ref, hn_ref, u_ref):
        x = h_ref[...]
        a = (x * _rms(x) * g_ref[...]).astype(BF16)
        hn_ref[...] = a
        for k in range(s):
            u_ref[:, ns * k:ns * (k + 1)] = jnp.dot(a, w_ref[k], preferred_element_type=F32).astype(BF16)

    return pl.pallas_call(
        body, name=name, grid=(tp // tm,),
        in_specs=[pl.BlockSpec((tm, d), lambda i: (i, 0)), pl.BlockSpec((1, d), lambda i: (0, 0)), _resident(w.shape, 1)],
        out_specs=[pl.BlockSpec((tm, d), lambda i: (i, 0)), pl.BlockSpec((tm, s * ns), lambda i: (i, 0))],
        out_shape=[jax.ShapeDtypeStruct((tp, d), BF16), jax.ShapeDtypeStruct((tp, s * ns), BF16)],
        compiler_params=_cp(1))(h, gain, w)


def _norm_ffn_in(h, gain, w, name):
    tp, d = h.shape
    ff = w.shape[1] // 2
    tm = TM_RESIDENT
    blocks = [(c, min(c + 6 * MXU_TILE, ff)) for c in range(0, ff, 6 * MXU_TILE)]

    def body(h_ref, g_ref, w_ref, hn_ref, dg_ref, du_ref, act_ref):
        x = h_ref[...]
        a = (x * _rms(x) * g_ref[...]).astype(BF16)
        hn_ref[...] = a
        for c0, c1 in blocks:
            g = jnp.dot(a, w_ref[:, c0:c1], preferred_element_type=F32)
            u = jnp.dot(a, w_ref[:, ff + c0:ff + c1], preferred_element_type=F32)
            sg = _sigmoid(g)
            silu = g * sg
            dg_ref[:, c0:c1] = (u * (sg + silu * (1.0 - sg))).astype(BF16)
            du_ref[:, c0:c1] = silu.astype(BF16)
            act_ref[:, c0:c1] = (silu * u).astype(BF16)

    wide = jax.ShapeDtypeStruct((tp, ff), BF16)
    return pl.pallas_call(
        body, name=name, grid=(tp // tm,),
        in_specs=[pl.BlockSpec((tm, d), lambda i: (i, 0)), pl.BlockSpec((1, d), lambda i: (0, 0)),
                  _resident(w.shape, 1)],
        out_specs=[pl.BlockSpec((tm, d), lambda i: (i, 0))] + [pl.BlockSpec((tm, ff), lambda i: (i, 0))] * 3,
        out_shape=[jax.ShapeDtypeStruct((tp, d), BF16), wide, wide, wide],
        compiler_params=_cp(1))(h, gain, w)


def _out_proj(a, w, h, scale, name):
    tp, k = a.shape
    d = w.shape[1]

    def body(a_ref, w_ref, h_ref, o_ref):
        o_ref[...] = h_ref[...] + scale * jnp.dot(a_ref[...], w_ref[...], preferred_element_type=F32)

    return pl.pallas_call(
        body, name=name, grid=(tp // TM,),
        in_specs=[pl.BlockSpec((TM, k), lambda i: (i, 0)), pl.BlockSpec((k, d), lambda i: (0, 0)),
                  pl.BlockSpec((TM, d), lambda i: (i, 0))],
        out_specs=pl.BlockSpec((TM, d), lambda i: (i, 0)),
        out_shape=jax.ShapeDtypeStruct((tp, d), F32),
        compiler_params=_cp(1))(a, w, h)


def _dgrad(dh, w, name):
    tp, d = dh.shape
    k = w.shape[0]

    def body(dh_ref, w_ref, o_ref):
        o_ref[...] = lax.dot_general(dh_ref[...].astype(BF16), w_ref[...], (((1,), (1,)), ((), ())),
                                     preferred_element_type=F32).astype(BF16)

    return pl.pallas_call(
        body, name=name, grid=(tp // TM,),
        in_specs=[pl.BlockSpec((TM, d), lambda i: (i, 0)), pl.BlockSpec((k, d), lambda i: (0, 0))],
        out_specs=pl.BlockSpec((TM, k), lambda i: (i, 0)),
        out_shape=jax.ShapeDtypeStruct((tp, k), BF16),
        compiler_params=_cp(1))(dh, w)


def _wgrad(a, b, *, bm, bn, scale, sharded, name):
    tp, m = a.shape
    n = b.shape[1]
    nk = tp // TM

    def body(a_ref, b_ref, o_ref, acc_ref):
        k = pl.program_id(2)

        @pl.when(k == 0)
        def _():
            acc_ref[...] = jnp.zeros_like(acc_ref)

        bb = b_ref[...]
        if scale != 1.0:
            bb = scale * bb
        acc_ref[...] += lax.dot_general(a_ref[...], bb.astype(BF16), (((0,), (0,)), ((), ())),
                                        preferred_element_type=F32)

        @pl.when(k == nk - 1)
        def _():
            o_ref[...] = acc_ref[...].astype(BF16)

    if sharded:
        assert m == bm
        out_spec = pl.BlockSpec((None, bm, bn), lambda i, j, k: (j, 0, 0))
        out_shape = jax.ShapeDtypeStruct((n // bn, m, bn), BF16)
    else:
        out_spec = pl.BlockSpec((bm, bn), lambda i, j, k: (i, j))
        out_shape = jax.ShapeDtypeStruct((m, n), BF16)
    return pl.pallas_call(
        body, name=name, grid=(m // bm, n // bn, nk),
        in_specs=[pl.BlockSpec((TM, bm), lambda i, j, k: (k, i)), pl.BlockSpec((TM, bn), lambda i, j, k: (k, j))],
        out_specs=out_spec, out_shape=out_shape,
        scratch_shapes=[pltpu.VMEM((bm, bn), F32)],
        compiler_params=_cp(3))(a, b)


def _dgrad_norm(du, w, h, gain, dh_out, name):
    tp, d = h.shape
    s, _, ns = w.shape
    tm = TM_RESIDENT

    def body(du_ref, w_ref, h_ref, g_ref, dho_ref, dhi_ref, dg_ref):
        @pl.when(pl.program_id(0) == 0)
        def _():
            dg_ref[...] = jnp.zeros_like(dg_ref)

        dhn = None
        for k in range(s):
            part = lax.dot_general(du_ref[:, ns * k:ns * (k + 1)], w_ref[k], (((1,), (1,)), ((), ())),
                                   preferred_element_type=F32)
            dhn = part if dhn is None else dhn + part
        dx, xhat = _rmsnorm_bwd(dhn, h_ref[...], g_ref[...])
        dg_ref[...] += jnp.sum(dhn * xhat, axis=0, keepdims=True)
        dhi_ref[...] = dho_ref[...] + dx

    return pl.pallas_call(
        body, name=name, grid=(tp // tm,),
        in_specs=[pl.BlockSpec((tm, s * ns), lambda i: (i, 0)), _resident(w.shape, 1),
                  pl.BlockSpec((tm, d), lambda i: (i, 0)), pl.BlockSpec((1, d), lambda i: (0, 0)),
                  pl.BlockSpec((tm, d), lambda i: (i, 0))],
        out_specs=[pl.BlockSpec((tm, d), lambda i: (i, 0)), pl.BlockSpec((1, d), lambda i: (0, 0))],
        out_shape=[jax.ShapeDtypeStruct((tp, d), F32), jax.ShapeDtypeStruct((1, d), F32)],
        compiler_params=_cp(1))(du, w, h, gain, dh_out)


def _loss_head(h, gain, target, name):
    tp, d = h.shape
    tm = TM_SMALL
    front_tiles = FRONT // tm

    def body(h_ref, g_ref, t_ref, dh_ref, dg_ref, loss_ref):
        i = pl.program_id(0)

        @pl.when(i == 0)
        def _():
            dg_ref[...] = jnp.zeros_like(dg_ref)
            loss_ref[...] = jnp.zeros_like(loss_ref)

        x = h_ref[...]
        gain_v = g_ref[...]
        y = x * _rms(x) * gain_v
        err = jnp.where(i >= front_tiles, y - t_ref[...], 0.0)
        loss_ref[...] += 0.5 * jnp.sum(jnp.mean(err * err, axis=-1, keepdims=True), axis=0, keepdims=True)
        dy = err * (1.0 / d)
        dx, xhat = _rmsnorm_bwd(dy, x, gain_v)
        dg_ref[...] += jnp.sum(dy * xhat, axis=0, keepdims=True)
        dh_ref[...] = dx

    return pl.pallas_call(
        body, name=name, grid=(tp // tm,),
        in_specs=[pl.BlockSpec((tm, d), lambda i: (i, 0)), pl.BlockSpec((1, d), lambda i: (0, 0)),
                  pl.BlockSpec((tm, d), lambda i: (jnp.maximum(i - front_tiles, 0), 0))],
        out_specs=[pl.BlockSpec((tm, d), lambda i: (i, 0)), pl.BlockSpec((1, d), lambda i: (0, 0)),
                   pl.BlockSpec((1, 128), lambda i: (0, 0))],
        out_shape=[jax.ShapeDtypeStruct((tp, d), F32), jax.ShapeDtypeStruct((1, d), F32),
                   jax.ShapeDtypeStruct((1, 128), F32)],
        compiler_params=_cp(1))(h, gain, target)


def _gated_headnorm(o, g, gain):
    return o * _rms(o) * gain * (g * _sigmoid(g))


def _row_mask(chunk, size=CHUNK):
    rows = chunk * size + lax.broadcasted_iota(jnp.int32, (size, 1), 0)
    return (rows >= FRONT - N_META).astype(F32)


def _ret_head(q1, q2, k1, k2, v, g, state, gain, cos, sin, dmat, dq, dk, dc):
    q = jnp.concatenate([q1 * cos - q2 * sin, q1 * sin + q2 * cos], axis=-1)
    k = jnp.concatenate([k1 * cos - k2 * sin, k1 * sin + k2 * cos], axis=-1) * (RET_DK ** -0.5)
    scores = _nt(q, k) * dmat
    o = _nn(scores, v) + _nn(q * dq, state)
    new_state = state * dc + _tn(k * dk, v)
    return _gated_headnorm(o, g, gain), new_state


def _ret_consts():
    log_gamma = jnp.log1p(-2.0 ** (-5.0 - jnp.arange(HEADS, dtype=F32)))
    idx = jnp.arange(RET_CHUNK, dtype=F32)
    rel = idx[:, None] - idx[None, :]
    dmat = jnp.where(rel >= 0, jnp.exp(log_gamma[:, None, None] * jnp.maximum(rel, 0.0)), 0.0)
    dq = jnp.exp(log_gamma[:, None] * (idx + 1.0))[..., None]
    dk = jnp.exp(log_gamma[:, None] * (RET_CHUNK - 1.0 - idx))[..., None]
    dc = jnp.broadcast_to(jnp.exp(log_gamma * RET_CHUNK)[:, None, None], (HEADS, 1, 128))
    return dmat, dq, dk, dc


def _rope_tables(tp):
    half = RET_DK // 2
    inv = 1.0 / (ROPE_BASE ** jnp.linspace(0.0, 1.0, half, dtype=F32))
    pos = (jnp.arange(tp) - (FRONT - N_META)).astype(F32)
    ang = pos[:, None] * inv[None, :]
    return jnp.cos(ang), jnp.sin(ang)


_RET_V0, _RET_G0 = 2 * D, 4 * D


def _heads(ref, start, width, stride=None):
    stride = width if stride is None else stride
    return jnp.stack([ref[:, start + stride * h:start + stride * h + width].astype(F32) for h in range(HEADS)])


def _put_heads(ref, start, value, mask, stride=None):
    width = value.shape[-1]
    stride = width if stride is None else stride
    for h in range(HEADS):
        ref[:, start + stride * h:start + stride * h + width] = (value[h] * mask).astype(ref.dtype)


def _ret_pieces(u_ref):
    hk = RET_DK // 2
    return (_heads(u_ref, 0, hk, RET_DK), _heads(u_ref, hk, hk, RET_DK), _heads(u_ref, D, hk, RET_DK),
            _heads(u_ref, D + hk, hk, RET_DK), _heads(u_ref, _RET_V0, RET_DV), _heads(u_ref, _RET_G0, RET_DV))


def _ret_const_specs(rev=None):
    c = (lambda n: (rev(n), 0)) if rev else (lambda n: (n, 0))
    z3 = lambda n: (0, 0, 0)
    return [pl.BlockSpec((RET_CHUNK, RET_DK // 2), c), pl.BlockSpec((RET_CHUNK, RET_DK // 2), c),
            pl.BlockSpec((HEADS, RET_CHUNK, RET_CHUNK), z3), pl.BlockSpec((HEADS, RET_CHUNK, 1), z3),
            pl.BlockSpec((HEADS, RET_CHUNK, 1), z3), pl.BlockSpec((HEADS, 1, 128), z3)]


def _ret_fwd(u, gain, rope, h, w_out, name):
    tp = u.shape[0]
    nch = tp // RET_CHUNK
    cos, sin = rope
    dmat, dq, dk, dc = _ret_consts()

    def body(u_ref, gain_ref, h_ref, w_ref, cos_ref, sin_ref, dmat_ref, dq_ref, dk_ref, dc_ref,
             on_ref, st_ref, hmix_ref, state_ref):
        @pl.when(pl.program_id(0) == 0)
        def _():
            state_ref[...] = jnp.zeros_like(state_ref)

        state = state_ref[...]
        st_ref[...] = state.astype(BF16)
        on, new_state = _ret_head(*_ret_pieces(u_ref), state, _heads(gain_ref, 0, RET_DV), cos_ref[...], sin_ref[...],
                                  dmat_ref[...], dq_ref[...], dk_ref[...], dc_ref[...][:, :, :1])
        state_ref[...] = new_state
        _put_heads(on_ref, 0, on, 1.0)
        hmix_ref[...] = h_ref[...] + jnp.dot(on_ref[...], w_ref[...], preferred_element_type=F32)

    rows = lambda width: pl.BlockSpec((RET_CHUNK, width), lambda n: (n, 0))
    return pl.pallas_call(
        body, name=name, grid=(nch,),
        in_specs=[rows(6 * D), pl.BlockSpec((1, HEADS * RET_DV), lambda n: (0, 0)), rows(D),
                  _resident(w_out.shape, 1)] + _ret_const_specs(),
        out_specs=[rows(HEADS * RET_DV), pl.BlockSpec((None, HEADS, RET_DK, RET_DV), lambda n: (n, 0, 0, 0)), rows(D)],
        out_shape=[jax.ShapeDtypeStruct((tp, HEADS * RET_DV), BF16),
                   jax.ShapeDtypeStruct((nch, HEADS, RET_DK, RET_DV), BF16), jax.ShapeDtypeStruct((tp, D), F32)],
        scratch_shapes=[pltpu.VMEM((HEADS, RET_DK, RET_DV), F32)],
        compiler_params=_cp(1))(u, gain, h, w_out, cos, sin, dmat, dq, dk, dc)


def _ret_bwd(u, gain, rope, states, d_on, name):
    tp = u.shape[0]
    nch = tp // RET_CHUNK
    cos, sin = rope
    dmat, dq, dk, dc = _ret_consts()
    rev = lambda n: nch - 1 - n
    hk = RET_DK // 2

    def body(u_ref, gain_ref, st_ref, don_ref, cos_ref, sin_ref, dmat_ref, dq_ref, dk_ref, dc_ref,
             du_ref, dgain_ref, dstate_ref):
        @pl.when(pl.program_id(0) == 0)
        def _():
            dstate_ref[...] = jnp.zeros_like(dstate_ref)
            dgain_ref[...] = jnp.zeros_like(dgain_ref)

        mask = _row_mask(rev(pl.program_id(0)), RET_CHUNK)
        consts = (cos_ref[...], sin_ref[...], dmat_ref[...], dq_ref[...], dk_ref[...], dc_ref[...][:, :, :1])
        _, vjp = jax.vjp(lambda *a: _ret_head(*a, *consts), *_ret_pieces(u_ref), st_ref[...].astype(F32),
                         _heads(gain_ref, 0, RET_DV))
        dq1, dq2, dk1, dk2, dv, dg, dstate, dgain = vjp((_heads(don_ref, 0, RET_DV), dstate_ref[...]))
        dstate_ref[...] = dstate
        for hd in range(HEADS):
            dgain_ref[:, RET_DV * hd:RET_DV * (hd + 1)] += dgain[hd]
        _put_heads(du_ref, 0, dq1, mask, RET_DK)
        _put_heads(du_ref, hk, dq2, mask, RET_DK)
        _put_heads(du_ref, D, dk1, mask, RET_DK)
        _put_heads(du_ref, D + hk, dk2, mask, RET_DK)
        _put_heads(du_ref, _RET_V0, dv, mask)
        _put_heads(du_ref, _RET_G0, dg, mask)

    return pl.pallas_call(
        body, name=name, grid=(nch,),
        in_specs=[pl.BlockSpec((RET_CHUNK, 6 * D), lambda n: (rev(n), 0)),
                  pl.BlockSpec((1, HEADS * RET_DV), lambda n: (0, 0)),
                  pl.BlockSpec((None, HEADS, RET_DK, RET_DV), lambda n: (rev(n), 0, 0, 0)),
                  pl.BlockSpec((RET_CHUNK, HEADS * RET_DV), lambda n: (rev(n), 0))] + _ret_const_specs(rev),
        out_specs=[pl.BlockSpec((RET_CHUNK, 6 * D), lambda n: (rev(n), 0)),
                   pl.BlockSpec((1, HEADS * RET_DV), lambda n: (0, 0))],
        out_shape=[jax.ShapeDtypeStruct((tp, 6 * D), BF16), jax.ShapeDtypeStruct((1, HEADS * RET_DV), F32)],
        scratch_shapes=[pltpu.VMEM((HEADS, RET_DK, RET_DV), F32)],
        compiler_params=_cp(1))(u, gain, states, d_on, cos, sin, dmat, dq, dk, dc)


_GLA_K0, _GLA_V0, _GLA_G0, _GLA_Z0 = 512, 1024, 2048, 3072


def _gla_head(q, k, v, g, z, state_t, wg, bg, gain, mask, lo, lo_t, loc, loc_t):
    ga = _nn(jnp.broadcast_to(z, wg.shape[:-2] + z.shape), wg) + bg
    log_a = (jnp.minimum(ga, 0.0) - jnp.log(1.0 + jnp.exp(-jnp.abs(ga)))) * (mask * (1.0 / GLA_TAU))
    bcum = _cum(lo, lo_t, log_a)
    bmid = _cum(loc, loc_t, log_a)
    btot = jnp.sum(log_a, axis=-2, keepdims=True)
    qs = q * (GLA_DK ** -0.5)
    causal = lax.broadcasted_iota(jnp.int32, (CHUNK, CHUNK), 0) >= lax.broadcasted_iota(jnp.int32, (CHUNK, CHUNK), 1)
    scores = jnp.where(causal, _nt(qs * jnp.exp(bmid), k * jnp.exp(-bmid)), 0.0)
    o = _nn(scores, v) + _nt(qs * jnp.exp(bcum), state_t)
    new_state_t = state_t * jnp.exp(btot) + _tn(v, k * jnp.exp(btot - bcum))
    return _gated_headnorm(o, g, gain), new_state_t


def _cum_mats():
    r = lax.broadcasted_iota(jnp.int32, (CHUNK, CHUNK), 0)
    c = lax.broadcasted_iota(jnp.int32, (CHUNK, CHUNK), 1)
    mid = CHUNK // 2
    low = lambda a, b: (a >= b).astype(F32)
    lo, lo_t = low(r, c), low(c, r)
    loc = lo - (c <= mid).astype(F32)
    loc_t = lo_t - (r <= mid).astype(F32)
    return tuple(m.astype(BF16) for m in (lo, lo_t, loc, loc_t))


def _gla_pieces(u_ref):
    return (_heads(u_ref, 0, GLA_DK), _heads(u_ref, _GLA_K0, GLA_DK), _heads(u_ref, _GLA_V0, GLA_DV),
            _heads(u_ref, _GLA_G0, GLA_DV), u_ref[:, _GLA_Z0:_GLA_Z0 + 128].astype(F32))


def _gla_fwd(u, wg, bg, gain, name):
    tp = u.shape[0]
    nch = tp // CHUNK

    def body(u_ref, wg_ref, bg_ref, gain_ref, on_ref, st_ref, state_ref):
        @pl.when(pl.program_id(0) == 0)
        def _():
            state_ref[...] = jnp.zeros_like(state_ref)

        state = state_ref[...]
        st_ref[...] = state.astype(BF16)
        on, new_state = _gla_head(*_gla_pieces(u_ref), state, _heads(wg_ref, 0, GLA_DK), _heads(bg_ref, 0, GLA_DK),
                                  _heads(gain_ref, 0, GLA_DV), _row_mask(pl.program_id(0)), *_cum_mats())
        state_ref[...] = new_state
        _put_heads(on_ref, 0, on, 1.0)

    rows = lambda width: pl.BlockSpec((CHUNK, width), lambda n: (n, 0))
    full = lambda r, c: pl.BlockSpec((r, c), lambda n: (0, 0))
    return pl.pallas_call(
        body, name=name, grid=(nch,),
        in_specs=[rows(GLA_U), full(128, HEADS * GLA_DK), full(1, HEADS * GLA_DK), full(1, HEADS * GLA_DV)],
        out_specs=[rows(HEADS * GLA_DV), pl.BlockSpec((None, HEADS, GLA_DV, GLA_DK), lambda n: (n, 0, 0, 0))],
        out_shape=[jax.ShapeDtypeStruct((tp, HEADS * GLA_DV), BF16),
                   jax.ShapeDtypeStruct((nch, HEADS, GLA_DV, GLA_DK), BF16)],
        scratch_shapes=[pltpu.VMEM((HEADS, GLA_DV, GLA_DK), F32)],
        compiler_params=_cp(1))(u, wg, bg, gain)


def _gla_bwd(u, wg, bg, gain, states, d_on, name):
    tp = u.shape[0]
    nch = tp // CHUNK
    rev = lambda n: nch - 1 - n

    def body(u_ref, wg_ref, bg_ref, gain_ref, st_ref, don_ref, du_ref, dwg_ref, dbg_ref, dgain_ref, dstate_ref):
        @pl.when(pl.program_id(0) == 0)
        def _():
            dstate_ref[...] = jnp.zeros_like(dstate_ref)
            dwg_ref[...] = jnp.zeros_like(dwg_ref)
            dbg_ref[...] = jnp.zeros_like(dbg_ref)
            dgain_ref[...] = jnp.zeros_like(dgain_ref)

        mask = _row_mask(rev(pl.program_id(0)))
        mats = _cum_mats()
        _, vjp = jax.vjp(lambda *a: _gla_head(*a, mask, *mats), *_gla_pieces(u_ref), st_ref[...].astype(F32),
                         _heads(wg_ref, 0, GLA_DK), _heads(bg_ref, 0, GLA_DK), _heads(gain_ref, 0, GLA_DV))
        dq, dk, dv, dg, dz, dstate, dwg, dbg, dgain = vjp((_heads(don_ref, 0, GLA_DV), dstate_ref[...]))
        dstate_ref[...] = dstate
        for hd in range(HEADS):
            dwg_ref[:, GLA_DK * hd:GLA_DK * (hd + 1)] += dwg[hd]
            dbg_ref[:, GLA_DK * hd:GLA_DK * (hd + 1)] += dbg[hd]
            dgain_ref[:, GLA_DV * hd:GLA_DV * (hd + 1)] += dgain[hd]
        _put_heads(du_ref, 0, dq, mask)
        _put_heads(du_ref, _GLA_K0, dk, mask)
        _put_heads(du_ref, _GLA_V0, dv, mask)
        _put_heads(du_ref, _GLA_G0, dg, mask)
        du_ref[:, _GLA_Z0:_GLA_Z0 + 128] = dz.astype(BF16)
        du_ref[:, _GLA_Z0 + 128:] = jnp.zeros((CHUNK, GLA_U - _GLA_Z0 - 128), BF16)

    full = lambda r, c: pl.BlockSpec((r, c), lambda n: (0, 0))
    return pl.pallas_call(
        body, name=name, grid=(nch,),
        in_specs=[pl.BlockSpec((CHUNK, GLA_U), lambda n: (rev(n), 0)), full(128, HEADS * GLA_DK),
                  full(1, HEADS * GLA_DK), full(1, HEADS * GLA_DV),
                  pl.BlockSpec((None, HEADS, GLA_DV, GLA_DK), lambda n: (rev(n), 0, 0, 0)),
                  pl.BlockSpec((CHUNK, HEADS * GLA_DV), lambda n: (rev(n), 0))],
        out_specs=[pl.BlockSpec((CHUNK, GLA_U), lambda n: (rev(n), 0)), full(128, HEADS * GLA_DK),
                   full(1, HEADS * GLA_DK), full(1, HEADS * GLA_DV)],
        out_shape=[jax.ShapeDtypeStruct((tp, GLA_U), BF16), jax.ShapeDtypeStruct((128, HEADS * GLA_DK), F32),
                   jax.ShapeDtypeStruct((1, HEADS * GLA_DK), F32), jax.ShapeDtypeStruct((1, HEADS * GLA_DV), F32)],
        scratch_shapes=[pltpu.VMEM((HEADS, GLA_DV, GLA_DK), F32)],
        compiler_params=_cp(1))(u, wg, bg, gain, states, d_on)


def _ffn_fwd(h, gain, w_in, w_out, tag):
    hn, ug, uu, act = _norm_ffn_in(h, gain, w_in, f"{tag}_in")
    if callable(w_out):
        w_out = w_out(act)
    return _out_proj(act, w_out, h, 0.5, f"{tag}_out"), (h, hn, ug, uu, act), w_out


def _ffn_dgrad(dh, w_out, w_in, act_dg, act_du, h, gain, name):
    tp, d = dh.shape
    ff = w_out.shape[0]
    tm = TM_SMALL
    nt = (((1,), (1,)), ((), ()))

    def body(dh_ref, wo_ref, wi_ref, dg_ref, du_ref, h_ref, g_ref, o_ref, dhi_ref, dgain_ref):
        @pl.when(pl.program_id(0) == 0)
        def _():
            dgain_ref[...] = jnp.zeros_like(dgain_ref)

        dho = dh_ref[...]
        dact = lax.dot_general((0.5 * dho).astype(BF16), wo_ref[...], nt, preferred_element_type=F32)
        d_gate = (dact * dg_ref[...].astype(F32)).astype(BF16)
        d_up = (dact * du_ref[...].astype(F32)).astype(BF16)
        o_ref[:, :ff] = d_gate
        o_ref[:, ff:] = d_up
        dhn = (lax.dot_general(d_gate, wi_ref[:, :ff], nt, preferred_element_type=F32)
               + lax.dot_general(d_up, wi_ref[:, ff:], nt, preferred_element_type=F32))
        dx, xhat = _rmsnorm_bwd(dhn, h_ref[...], g_ref[...])
        dgain_ref[...] += jnp.sum(dhn * xhat, axis=0, keepdims=True)
        dhi_ref[...] = dho + dx

    rows = lambda width: pl.BlockSpec((tm, width), lambda i: (i, 0))
    return pl.pallas_call(
        body, name=name, grid=(tp // tm,),
        in_specs=[rows(d), _resident(w_out.shape, 1), _resident(w_in.shape, 1), rows(ff), rows(ff), rows(d),
                  pl.BlockSpec((1, d), lambda i: (0, 0))],
        out_specs=[rows(2 * ff), rows(d), pl.BlockSpec((1, d), lambda i: (0, 0))],
        out_shape=[jax.ShapeDtypeStruct((tp, 2 * ff), BF16), jax.ShapeDtypeStruct((tp, d), F32),
                   jax.ShapeDtypeStruct((1, d), F32)],
        compiler_params=_cp(1))(dh, w_out, w_in, act_dg, act_du, h, gain)


def _ffn_bwd(dh, saved, gain, w_in, w_out, tag, push):
    h, hn, act_dg, act_du, act = saved
    du, dh_in, d_gain = _ffn_dgrad(dh, w_out, w_in, act_dg, act_du, h, gain, f"{tag}_dgrad")
    d_w_out = _wgrad(act, dh, bm=D_FF // 2, bn=D, scale=0.5, sharded=False, name=f"{tag}_dwout")
    d_w_in = _wgrad(hn, du, bm=D, bn=D_FF, scale=1.0, sharded=False, name=f"{tag}_dwin")
    return dh_in, d_gain, push([("cols", d_w_in), d_w_out])


def _sequence_grads(x, target, p, weights, grads):
    row = lambda v, token: v.reshape(1, -1) + token[0, 0]
    gains = {}

    tok = weights.start(1, weights.start(0, None))
    weights.pin = tok
    h = jnp.concatenate([jnp.zeros((FRONT, D), F32), x], axis=0) + tok[0, 0]
    rope = _rope_tables(h.shape[0])
    w = weights.wait(0, [tok, h, *rope, *weights.later_shards(2)])
    tok = weights.start(2, w["l0_ffn1_in"])
    h = lax.dynamic_update_slice(h, w["meta"], (FRONT - N_META, 0))
    gains["l0_ffn1"] = row(p["norm_ffn1"][0], tok)
    h, s1, w["l0_ffn1_out"] = _ffn_fwd(h, gains["l0_ffn1"], w["l0_ffn1_in"],
                                       lambda act: weights.wait(1, act)["l0_ffn1_out"], "l0_ffn1")
    w.update(weights.wait(2, h))
    tok = weights.start(4, weights.start(3, w["ret_in"]))
    gains["ret"] = row(p["norm_mix"][0], tok)
    hn, u = _norm_proj(h, gains["ret"], w["ret_in"], "ret_in")
    w.update(weights.wait(3, u))
    on, states, h_mix = _ret_fwd(u, w["ret_gain"], rope, h, w["ret_out"], "ret_fwd")
    s2 = (h, hn, u, on, states)
    w.update(weights.wait(4, h_mix))
    tok = weights.start(5, w["l0_ffn2_in"])
    gains["l0_ffn2"] = row(p["norm_ffn2"][0], tok)
    h, s3, _ = _ffn_fwd(h_mix, gains["l0_ffn2"], w["l0_ffn2_in"], w["l0_ffn2_out"], "l0_ffn2")
    saved = [(s1, s2, s3)]

    w.update(weights.wait(5, h))
    tok = weights.start(6, w["l1_ffn1_in"])
    gains["l1_ffn1"] = row(p["norm_ffn1"][1], tok)
    h, s1, _ = _ffn_fwd(h, gains["l1_ffn1"], w["l1_ffn1_in"], w["l1_ffn1_out"], "l1_ffn1")
    w.update(weights.wait(6, h))
    tok = weights.start(7, w["gla_out"])
    gains["gla"] = row(p["norm_mix"][1], tok)
    hn, u = _norm_proj(h, gains["gla"], w["gla_in"], "gla_in")
    on, states = _gla_fwd(u, w["gla_wg"], w["gla_bg"], w["gla_gain"], "gla_fwd")
    h_mix = _out_proj(on, w["gla_out"], h, 1.0, "gla_out")
    s2 = (h, hn, u, on, states)
    w.update(weights.wait(7, h_mix))
    gains["l1_ffn2"] = p["norm_ffn2"][1].reshape(1, -1)
    h, s3, _ = _ffn_fwd(h_mix, gains["l1_ffn2"], w["l1_ffn2_in"], w["l1_ffn2_out"], "l1_ffn2")
    saved.append((s1, s2, s3))

    dh, d_final, loss = _loss_head(h, p["final_norm"].reshape(1, -1), target, "loss_head")
    small = {"final_norm": d_final, "norm_ffn1": [None, None], "norm_mix": [None, None], "norm_ffn2": [None, None]}
    pusher = lambda k: functools.partial(grads.push, k)

    s1, s2, s3 = saved[1]
    dh, small["norm_ffn2"][1], tok = _ffn_bwd(dh, s3, gains["l1_ffn2"], w["l1_ffn2_in"], w["l1_ffn2_out"], "l1_ffn2",
                                              pusher(0))
    h_in, hn, u, on, states = s2
    d_on = _dgrad(dh, w["gla_out"], "gla_don")
    d_out = _wgrad(on, dh, bm=D, bn=D, scale=1.0, sharded=False, name="gla_dwout")
    du, small["gla_wg"], small["gla_bg"], small["gla_gain"] = _gla_bwd(
        u, w["gla_wg"], w["gla_bg"], w["gla_gain"] + tok[0, 0], states, d_on, "gla_bwd")
    d_in = _wgrad(hn, du, bm=D, bn=GLA_U, scale=1.0, sharded=False, name="gla_dwin")
    d_in = jnp.moveaxis(d_in[:, :GLA_IN].reshape(D, N_CHIPS, -1), 1, 0)
    tok = grads.push(1, [d_in, d_out])
    dh, small["norm_mix"][1] = _dgrad_norm(du, w["gla_in"], h_in, gains["gla"] + tok[0, 0], dh, "gla_dnorm")
    dh, small["norm_ffn1"][1], tok = _ffn_bwd(dh, s1, gains["l1_ffn1"], w["l1_ffn1_in"], w["l1_ffn1_out"], "l1_ffn1",
                                              pusher(2))

    s1, s2, s3 = saved[0]
    dh, small["norm_ffn2"][0], tok = _ffn_bwd(dh, s3, gains["l0_ffn2"] + tok[0, 0], w["l0_ffn2_in"],
                                              w["l0_ffn2_out"], "l0_ffn2", pusher(3))
    h_in, hn, u, on, states = s2
    d_on = _dgrad(dh, w["ret_out"], "ret_don")
    d_out = _wgrad(on, dh, bm=D, bn=D, scale=1.0, sharded=False, name="ret_dwout")
    du, small["ret_gain"] = _ret_bwd(u, w["ret_gain"] + tok[0, 0], rope, states, d_on, "ret_bwd")
    d_in = _wgrad(hn, du, bm=D, bn=w["ret_in"].shape[2], scale=1.0, sharded=True, name="ret_dwin")
    tok = grads.push(4, [d_in, d_out])
    dh, small["norm_mix"][0] = _dgrad_norm(du, w["ret_in"], h_in, gains["ret"] + tok[0, 0], dh, "ret_dnorm")
    dh, small["norm_ffn1"][0], tok = _ffn_bwd(dh, s1, gains["l0_ffn1"], w["l0_ffn1_in"], w["l0_ffn1_out"], "l0_ffn1",
                                              pusher(5))
    grads.push(6, [], [dh[FRONT - N_META:FRONT], *small["norm_ffn1"], *small["norm_mix"], *small["norm_ffn2"],
                       small["final_norm"], small["ret_gain"], small["gla_wg"][:GLA_RANK], small["gla_bg"],
                       small["gla_gain"], loss[:, :1] + tok[0, 0]])
    return dh[FRONT:]


_HBM = pl.BlockSpec(memory_space=pl.ANY)


def _place():
    return lax.axis_index("x"), lax.axis_index("y"), lax.axis_index("c")


def _flip(v, bit):
    return 1 - v if bit else v


DMA_CHUNK_BYTES = 128 * 1024


def _row_chunks(ref):
    rows, cols = ref.shape
    step = _row_tile(rows, max(16, DMA_CHUNK_BYTES // (cols * ref.dtype.itemsize)))
    return [pl.ds(a, step) for a in range(0, rows, step)]


def _whole(src, dst, send_sem, recv_sem, peer):
    return pltpu.make_async_remote_copy(src_ref=src, dst_ref=dst, send_sem=send_sem, recv_sem=recv_sem,
                                        device_id=peer, device_id_type=MESH)


def _send(src, dst, send_sem, recv_sem, peer):
    for rows in _row_chunks(src):
        _whole(src.at[rows], dst.at[rows], send_sem, recv_sem, peer).start()
    return _whole(src, dst, send_sem, recv_sem, peer)


_HBM_ONLY = pl.BlockSpec(memory_space=pltpu.HBM)
_SEMS = pl.BlockSpec(memory_space=pltpu.SEMAPHORE)
_SIDE_EFFECT = pltpu.CompilerParams(has_side_effects=pltpu.SideEffectType.DATAFLOW_SIDE_EFFECTING)
_GATHER_FLIPS = [(1, 0, 0), (0, 1, 0), (1, 1, 0), (0, 0, 1)]
_PEER_FLIPS = [(fx, fy, fc) for fx in (0, 1) for fy in (0, 1) for fc in (0, 1)][1:]


def _zero_token():
    return jnp.zeros((8, 128), F32)


def _exchange_start(srcs, lands, route, flips, after, name):
    n = len(srcs)

    def body(*refs):
        src, land = refs[:n], refs[n:2 * n]
        send_sems, recv_sems, token = refs[2 * n + 1], refs[2 * n + 2], refs[-1]
        me = _place()
        for t in range(n):
            for j, flip in enumerate(flips):
                peer = tuple(_flip(v, f) for v, f in zip(me, flip))
                s, d = route(t, src[t], land[t], me, peer)
                _send(s, d, send_sems.at[t * len(flips) + j], recv_sems.at[t * len(flips) + j], peer)
        token[...] = jnp.zeros_like(token)

    hbm = lambda a: pltpu.HBM(a.shape, a.dtype)
    sems = pltpu.SemaphoreType.DMA((n * len(flips),))
    operands = [pltpu.with_memory_space_constraint(a, pltpu.HBM) for a in list(srcs) + list(lands)]
    out = pl.pallas_call(
        body, name=name, in_specs=[_HBM_ONLY] * (2 * n) + [_HBM],
        out_shape=(sems, sems, *[hbm(a) for a in operands], jax.ShapeDtypeStruct((8, 128), F32)),
        out_specs=(_SEMS, _SEMS, *[_HBM_ONLY] * (2 * n), pl.BlockSpec(memory_space=pltpu.VMEM)),
        input_output_aliases={i: 2 + i for i in range(2 * n)}, compiler_params=_SIDE_EFFECT,
    )(*operands, _zero_token() if after is None else after)
    return (out[0], out[1], out[2:2 + n], out[2 + n:2 + 2 * n]), out[-1]


def _exchange_wait(started, route, flips, after, name):
    send_sems, recv_sems, srcs, lands = started
    n = len(srcs)

    def body(*refs):
        src, land = refs[:n], refs[n:2 * n]
        send_sems, recv_sems = refs[2 * n], refs[2 * n + 1]
        me = _place()
        for t in range(n):
            for j, flip in enumerate(flips):
                peer = tuple(_flip(v, f) for v, f in zip(me, flip))
                s, d = route(t, src[t], land[t], me, peer)
                cp = _whole(s, d, send_sems.at[t * len(flips) + j], recv_sems.at[t * len(flips) + j], peer)
                cp.wait_send()
                cp.wait_recv()

    hbm = lambda a: pltpu.HBM(a.shape, a.dtype)
    after = list(after) if isinstance(after, (list, tuple)) else [after]
    out = pl.pallas_call(
        body, name=name, in_specs=[_HBM_ONLY] * (2 * n) + [_SEMS, _SEMS] + [_HBM] * len(after),
        out_shape=tuple(hbm(a) for a in list(srcs) + list(lands)), out_specs=tuple([_HBM_ONLY] * (2 * n)),
        input_output_aliases={i: i for i in range(2 * n)}, compiler_params=_SIDE_EFFECT,
    )(*srcs, *lands, send_sems, recv_sems, *after)
    return out[:n], out[n:]


def _gather_route(t, src, land, me, peer):
    mine = 2 * me[0] + me[1]
    if land.ndim == 3:
        return src, land.at[mine]
    cols = src.shape[1]
    return src, land.at[:, pl.ds(pl.multiple_of(mine * cols, 128), cols)]


def _scatter_route(n_pieces):
    def route(t, src, land, me, peer):
        chip = 2 * peer[0] + peer[1]
        if t >= n_pieces:
            part = src
        elif src.ndim == 4:
            part = src.at[chip, peer[2]]
        else:
            rows, cols = land.shape[1:]
            part = src.at[pl.ds(pl.multiple_of(peer[2] * rows, 16), rows), pl.ds(pl.multiple_of(chip * cols, 128), cols)]
        return part, land.at[4 * me[0] + 2 * me[1] + me[2]]

    return route


def _swap_cores(halves, name):
    n = len(halves)

    def body(*refs):
        src, dst = refs[:n], refs[n:2 * n]
        send_sems, recv_sems = refs[2 * n:]
        x, y, c = _place()
        copies = [_send(src[t], dst[t], send_sems.at[t], recv_sems.at[t], (x, y, 1 - c)) for t in range(n)]
        for cp in copies:
            cp.wait()

    got = pl.pallas_call(
        body, name=name, in_specs=[_HBM] * n, out_specs=[_HBM] * n,
        out_shape=[jax.ShapeDtypeStruct(a.shape, a.dtype) for a in halves],
        scratch_shapes=[pltpu.SemaphoreType.DMA((n,)), pltpu.SemaphoreType.DMA((n,))],
    )(*halves)
    south = lax.axis_index("c") == 0
    return [jnp.stack([jnp.where(south, a, b), jnp.where(south, b, a)]) for a, b in zip(halves, got)]


def _row_tile(rows, cap):
    fits = [t for t in range(16, cap + 1, 16) if rows % t == 0]
    return fits[-1] if fits else rows


def _sum_slots(a, name):
    _, r, c = a.shape
    tr = _row_tile(r, 384)

    def body(a_ref, o_ref):
        s = a_ref[0].astype(F32)
        for k in range(1, N_DEV):
            s = s + a_ref[k].astype(F32)
        o_ref[...] = s

    return pl.pallas_call(
        body, name=name, grid=(r // tr,),
        in_specs=[pl.BlockSpec((N_DEV, tr, c), lambda i: (0, i, 0))],
        out_specs=pl.BlockSpec((tr, c), lambda i: (i, 0)),
        out_shape=jax.ShapeDtypeStruct((r, c), F32),
        compiler_params=_cp(1))(a)


def _adamw(w, g, m, v, name):
    layers, r, c = w.shape
    tr = _row_tile(r, 256)

    def body(w_ref, g_ref, m_ref, v_ref, d_ref, nm_ref, nv_ref):
        gv = g_ref[...]
        nm = ADAM_B1 * m_ref[...] + (1.0 - ADAM_B1) * gv
        nv = ADAM_B2 * v_ref[...] + (1.0 - ADAM_B2) * (gv * gv)
        m_hat = nm / (1.0 - ADAM_B1 ** ADAM_STEP)
        v_hat = nv / (1.0 - ADAM_B2 ** ADAM_STEP)
        d_ref[...] = -ADAM_LR * (m_hat / (jnp.sqrt(v_hat) + ADAM_EPS) + ADAM_WD * w_ref[...])
        nm_ref[...] = nm
        nv_ref[...] = nv

    spec = pl.BlockSpec((None, tr, c), lambda a, i: (a, i, 0))
    return pl.pallas_call(
        body, name=name, grid=(layers, r // tr), in_specs=[spec] * 4, out_specs=[spec] * 3,
        out_shape=[jax.ShapeDtypeStruct((layers, r, c), F32)] * 3,
        compiler_params=_cp(2))(w, g, m, v)


_SMALL = ["meta_tokens", "ret_head_norm", "gla_w_gate", "gla_b_gate", "gla_head_norm"]
_LOCAL_SMALL = ["meta_tokens", "norm_ffn1", "norm_mix", "norm_ffn2", "ret_head_norm", "gla_w_gate", "gla_b_gate",
                "gla_head_norm", "final_norm"]
_BIG = ["ffn1_w_in", "ffn1_w_out", "ffn2_w_in", "ffn2_w_out", "ret_w_in", "ret_w_out", "gla_w_in", "gla_w_out"]
_WEIGHTS = ["meta_tokens", "norm_ffn1", "ffn1_w_in", "ffn1_w_out", "norm_mix", "norm_ffn2", "ffn2_w_in", "ffn2_w_out",
            "ret_w_in", "ret_head_norm", "ret_w_out", "gla_w_in", "gla_w_gate", "gla_b_gate", "gla_head_norm",
            "gla_w_out", "final_norm"]


def _pack_rows(arrays, width):
    flat = jnp.concatenate([a.reshape(-1) for a in arrays])
    pad = -flat.shape[0] % (8 * width)
    return jnp.pad(flat, (0, pad)).reshape(-1, width)


def _unpack_rows(packed, shapes):
    flat, out, at = packed.reshape(-1), [], 0
    for s in shapes:
        size = 1
        for dim in s:
            size *= dim
        out.append(flat[at:at + size].reshape(s))
        at += size
    return out


class _WeightGather:
    GROUPS = [("small", "l0_ffn1_in"), ("l0_ffn1_out",), ("ret_in",), ("ret_out",), ("l0_ffn2_in", "l0_ffn2_out"),
              ("l1_ffn1_in", "l1_ffn1_out"), ("gla_in", "gla_out"), ("l1_ffn2_in", "l1_ffn2_out")]

    def __init__(self, p):
        self.small_shapes = [p[name].shape for name in _SMALL]
        self.f32 = {"small": _pack_rows([p[name] for name in _SMALL], 128), "ret_in": p["ret_w_in"][0],
                    "ret_out": p["ret_w_out"][0], "gla_in": p["gla_w_in"][0], "gla_out": p["gla_w_out"][0]}
        for layer in range(2):
            for name in ("ffn1", "ffn2"):
                self.f32[f"l{layer}_{name}_in"] = p[f"{name}_w_in"][layer]
                self.f32[f"l{layer}_{name}_out"] = p[f"{name}_w_out"][layer]
        self.shards = {}
        self.started = {}
        self.pin = None

    def shard(self, name):
        if name not in self.shards:
            a = self.f32[name]
            if name != "small":
                a = (a if self.pin is None else a + self.pin[0, 0]).astype(BF16)
            self.shards[name] = a
        return self.shards[name]

    def later_shards(self, k):
        return [self.shard(name) for group in self.GROUPS[k:] for name in group]

    def start(self, k, after):
        shards = [self.shard(name) for name in self.GROUPS[k]]
        lands = []
        for name, s in zip(self.GROUPS[k], shards):
            if "ffn" in name and name.endswith("_in"):
                lands.append(lax.empty((s.shape[0], N_CHIPS * s.shape[1]), s.dtype))
            else:
                lands.append(lax.empty((N_CHIPS,) + s.shape, s.dtype))
        self.started[k], token = _exchange_start(shards, lands, _gather_route, _GATHER_FLIPS, after, f"gather{k}_start")
        return token

    def wait(self, k, after):
        _, got = _exchange_wait(self.started[k], _gather_route, _GATHER_FLIPS, after, f"gather{k}_wait")
        w = {}
        for name, g in zip(self.GROUPS[k], got):
            if name == "small":
                parts = zip(*[_unpack_rows(g[chip], self.small_shapes) for chip in range(N_CHIPS)])
                cat = lambda a: jnp.moveaxis(a, 0, -2).reshape(a.shape[1:-1] + (-1,))
                meta, ret_gain, wg, bg, gla_gain = [cat(jnp.stack(part)) for part in parts]
                w.update(meta=meta, ret_gain=ret_gain.reshape(1, -1), gla_bg=bg.reshape(1, -1),
                         gla_gain=gla_gain.reshape(1, -1),
                         gla_wg=jnp.pad(wg[0], ((0, 128 - GLA_RANK), (0, 0))).astype(BF16))
            elif name == "gla_in":
                full = jnp.moveaxis(g, 0, 1).reshape(D, -1)
                w[name] = jnp.pad(full, ((0, 0), (0, GLA_U - GLA_IN)))[None]
            elif name.endswith("_out"):
                w[name] = g.reshape(-1, g.shape[-1])
            else:
                w[name] = g
        return w


class _GradExchange:
    def __init__(self):
        self.started = []
        self.token = None
        self.small_shapes = None

    def push(self, k, arrays, small=None):
        srcs, lands = [], []
        for a in arrays:
            if isinstance(a, tuple):
                a = a[1]
                piece = (a.shape[0] // 2, a.shape[1] // N_CHIPS)
            else:
                a = a.reshape(N_CHIPS, 2, -1, a.shape[-1])
                piece = a.shape[2:]
            srcs.append(a)
            lands.append(lax.empty((N_DEV,) + piece, a.dtype))
        if small is not None:
            self.small_shapes = [a.shape for a in small]
            srcs.append(_pack_rows(small, D))
            lands.append(lax.empty((N_DEV,) + srcs[-1].shape, F32))
        started, self.token = _exchange_start(srcs, lands, _scatter_route(len(arrays)), _PEER_FLIPS, None,
                                              f"scatter{k}_start")
        self.started.append((started, len(arrays)))
        return self.token

    def collect(self, groups, after=None):
        x, y, c = _place()
        after, sums = self.token if after is None else after, []
        for k in groups:
            started, n_pieces = self.started[k]
            srcs, got = _exchange_wait(started, _scatter_route(n_pieces), _PEER_FLIPS, after, f"scatter{k}_wait")
            own = []
            for t, (a, g) in enumerate(zip(srcs, got)):
                if t >= n_pieces:
                    own.append(a)
                elif a.ndim == 4:
                    own.append(a[2 * x + y, c])
                else:
                    rows, cols = g.shape[1:]
                    own.append(lax.dynamic_slice(a, (c * rows, (2 * x + y) * cols), (rows, cols)))
            got = [lax.dynamic_update_index_in_dim(g, a, 4 * x + 2 * y + c, 0) for g, a in zip(got, own)]
            sums.append([_sum_slots(a, f"sum{k}_{i}") for i, a in enumerate(got)])
            after = sums[-1][0]
        return sums


def kernel(x, meta_tokens, norm_ffn1, ffn1_w_in, ffn1_w_out, norm_mix, norm_ffn2, ffn2_w_in, ffn2_w_out, ret_w_in, ret_head_norm, ret_w_out, gla_w_in, gla_w_gate, gla_b_gate, gla_head_norm, gla_w_out, final_norm, loss_target, m_meta_tokens, m_norm_ffn1, m_ffn1_w_in, m_ffn1_w_out, m_norm_mix, m_norm_ffn2, m_ffn2_w_in, m_ffn2_w_out, m_ret_w_in, m_ret_head_norm, m_ret_w_out, m_gla_w_in, m_gla_w_gate, m_gla_b_gate, m_gla_head_norm, m_gla_w_out, m_final_norm, v_meta_tokens, v_norm_ffn1, v_ffn1_w_in, v_ffn1_w_out, v_norm_mix, v_norm_ffn2, v_ffn2_w_in, v_ffn2_w_out, v_ret_w_in, v_ret_head_norm, v_ret_w_out, v_gla_w_in, v_gla_w_gate, v_gla_b_gate, v_gla_head_norm, v_gla_w_out, v_final_norm):
    p = dict(meta_tokens=meta_tokens, norm_ffn1=norm_ffn1, ffn1_w_in=ffn1_w_in, ffn1_w_out=ffn1_w_out, norm_mix=norm_mix,
             norm_ffn2=norm_ffn2, ffn2_w_in=ffn2_w_in, ffn2_w_out=ffn2_w_out, ret_w_in=ret_w_in,
             ret_head_norm=ret_head_norm, ret_w_out=ret_w_out, gla_w_in=gla_w_in, gla_w_gate=gla_w_gate,
             gla_b_gate=gla_b_gate, gla_head_norm=gla_head_norm, gla_w_out=gla_w_out, final_norm=final_norm)
    m = dict(zip(_WEIGHTS, (m_meta_tokens, m_norm_ffn1, m_ffn1_w_in, m_ffn1_w_out, m_norm_mix, m_norm_ffn2, m_ffn2_w_in,
                            m_ffn2_w_out, m_ret_w_in, m_ret_head_norm, m_ret_w_out, m_gla_w_in, m_gla_w_gate,
                            m_gla_b_gate, m_gla_head_norm, m_gla_w_out, m_final_norm)))
    v = dict(zip(_WEIGHTS, (v_meta_tokens, v_norm_ffn1, v_ffn1_w_in, v_ffn1_w_out, v_norm_mix, v_norm_ffn2, v_ffn2_w_in,
                            v_ffn2_w_out, v_ret_w_in, v_ret_head_norm, v_ret_w_out, v_gla_w_in, v_gla_w_gate,
                            v_gla_b_gate, v_gla_head_norm, v_gla_w_out, v_final_norm)))

    exchange = _GradExchange()
    d_x = _sequence_grads(x[0], loss_target[0], p, _WeightGather(p), exchange)
    names = [("ffn2_w_in", 1), ("ffn2_w_out", 1), ("gla_w_in", 0), ("gla_w_out", 0), ("ffn1_w_in", 1), ("ffn1_w_out", 1),
             ("ffn2_w_in", 0), ("ffn2_w_out", 0), ("ret_w_in", 0), ("ret_w_out", 0), ("ffn1_w_in", 0), ("ffn1_w_out", 0)]
    shard, grads, delta, new_m, new_v = {}, {}, {}, {}, {}

    def swap(sums, keys, name):
        for key, a in zip(keys, _swap_cores(sums, name)):
            shard[key] = a.reshape(-1, a.shape[-1])

    def update(name):
        layers = p[name].shape[0]
        grads[name] = jnp.stack([shard[name, layer] for layer in range(layers)])
        delta[name], new_m[name], new_v[name] = _adamw(p[name], grads[name], m[name], v[name], f"adamw_{name}")

    swap([a for group in exchange.collect(range(5)) for a in group], names[:10], "swap_first")
    for name in ("ffn2_w_in", "ffn2_w_out", "ret_w_in", "ret_w_out", "gla_w_in", "gla_w_out"):
        update(name)
    last, (small_sum,) = exchange.collect([5, 6], after=list(delta.values()))
    swap(last, names[10:], "swap_last")
    for name in ("ffn1_w_in", "ffn1_w_out"):
        update(name)

    chip = 2 * lax.axis_index("x") + lax.axis_index("y")
    cols = lambda a, n: lax.dynamic_slice_in_dim(a, chip * n, n, axis=a.ndim - 1)
    (s_meta, s_n1a, s_n1b, s_nma, s_nmb, s_n2a, s_n2b, s_final, s_ret_gain, s_wg, s_bg, s_gla_gain,
     s_loss) = _unpack_rows(small_sum, exchange.small_shapes)
    grads.update({
        "meta_tokens": cols(s_meta, 256), "norm_ffn1": jnp.concatenate([s_n1a, s_n1b]),
        "norm_mix": jnp.concatenate([s_nma, s_nmb]), "norm_ffn2": jnp.concatenate([s_n2a, s_n2b]),
        "final_norm": s_final.reshape(D),
        "ret_head_norm": cols(s_ret_gain.reshape(1, HEADS, RET_DV), RET_DV // N_CHIPS),
        "gla_w_gate": cols(s_wg, GLA_DK)[None], "gla_b_gate": cols(s_bg, GLA_DK),
        "gla_head_norm": cols(s_gla_gain.reshape(1, HEADS, GLA_DV), GLA_DV // N_CHIPS),
    })
    packed = [_pack_rows([d[name] for name in _LOCAL_SMALL], 128)[None] for d in (p, grads, m, v)]
    out = _adamw(*packed, "adamw_small")
    shapes = [p[name].shape for name in _LOCAL_SMALL]
    for d, a in zip((delta, new_m, new_v), out):
        d.update(zip(_LOCAL_SMALL, _unpack_rows(a, shapes)))

    return (s_loss.reshape(()), d_x[None], *[grads[n] for n in _WEIGHTS], *[delta[n] for n in _WEIGHTS],
            *[new_m[n] for n in _WEIGHTS], *[new_v[n] for n in _WEIGHTS])
```

```python
import functools

import jax
import jax.numpy as jnp
from jax import lax
from jax.experimental import pallas as pl
from jax.experimental.pallas import tpu as pltpu

F32, BF16 = jnp.float32, jnp.bfloat16
MESH = pl.DeviceIdType.MESH

D = 1024
N_META = 16
CHUNK = 64
RET_CHUNK = 256
FRONT = 256
D_FF = 2816
EPS = 1e-6
HEADS = 4
RET_DK, RET_DV = 256, 512
GLA_DK, GLA_DV = 128, 256
GLA_RANK = 16
GLA_TAU = 16.0
GLA_IN = 2 * HEADS * GLA_DK + 2 * HEADS * GLA_DV + GLA_RANK
GLA_U = 3328
ROPE_BASE = 10000.0
N_CHIPS = 4
N_DEV = 8

ADAM_LR, ADAM_B1, ADAM_B2, ADAM_EPS, ADAM_WD, ADAM_STEP = 0.001, 0.9, 0.999, 1e-08, 0.01, 10

VMEM_LIMIT_BYTES = 56 * 1024 * 1024
TM = 768
TM_SMALL = 256


TM_RESIDENT = 384
MXU_TILE = 256


def _cp(n_axes):
    return pltpu.CompilerParams(dimension_semantics=("arbitrary",) * n_axes, vmem_limit_bytes=VMEM_LIMIT_BYTES)


def _resident(shape, n_axes):
    zeros = (0,) * len(shape)
    index = (lambda i: zeros) if n_axes == 1 else (lambda i, j: zeros)
    return pl.BlockSpec(shape, index, pipeline_mode=pl.Buffered(1))


def _dg(a, b, ca, cb):
    nb = a.ndim - 2
    dims = (((ca + nb,), (cb + nb,)), (tuple(range(nb)), tuple(range(nb))))
    return lax.dot_general(a.astype(BF16), b.astype(BF16), dims, preferred_element_type=F32)


@jax.custom_vjp
def _nn(a, b):
    return _dg(a, b, 1, 0)


@jax.custom_vjp
def _nt(a, b):
    return _dg(a, b, 1, 1)


@jax.custom_vjp
def _tn(a, b):
    return _dg(a, b, 0, 0)


def _dot_vjp(fn, ca, cb, da, db):
    def fwd(a, b):
        a, b = a.astype(BF16), b.astype(BF16)
        return _dg(a, b, ca, cb), (a, b)

    def bwd(res, g):
        a, b = res
        g = g.astype(BF16)
        grad = lambda other, dims, g_first: _dg(g, other, *dims) if g_first else _dg(other, g, *dims)
        return grad(b, *da), grad(a, *db)

    fn.defvjp(fwd, bwd)


_dot_vjp(_nn, 1, 0, ((1, 1), True), ((0, 0), False))
_dot_vjp(_nt, 1, 1, ((1, 0), True), ((0, 0), True))
_dot_vjp(_tn, 0, 0, ((1, 1), False), ((1, 0), False))


def _split3_dot(m, a):
    a1 = a.astype(BF16)
    r1 = a - a1.astype(F32)
    a2 = r1.astype(BF16)
    a3 = (r1 - a2.astype(F32)).astype(BF16)
    mb = jnp.broadcast_to(m, a.shape[:-2] + m.shape)
    return _dg(mb, a1, 1, 0) + _dg(mb, a2, 1, 0) + _dg(mb, a3, 1, 0)


@jax.custom_vjp
def _cum(m, mt, a):
    return _split3_dot(m, a)


_cum.defvjp(lambda m, mt, a: (_split3_dot(m, a), (m, mt)),
            lambda res, g: (jnp.zeros_like(res[0]), jnp.zeros_like(res[1]), _split3_dot(res[1], g)))


def _sigmoid(x):
    return 1.0 / (1.0 + jnp.exp(-x))


def _rms(x):
    return lax.rsqrt(jnp.mean(x * x, axis=-1, keepdims=True) + EPS)


def _rmsnorm_bwd(dy, x, gain):
    r = _rms(x)
    xhat = x * r
    dxh = dy * gain
    return r * (dxh - xhat * jnp.mean(dxh * xhat, axis=-1, keepdims=True)), xhat


def _norm_proj(h, gain, w, name):
    tp, d = h.shape
    s, _, ns = w.shape

    tm = TM_RESIDENT

    def body(h_ref, g_ref, w_ref, hn_ref, u_ref):
        x = h_ref[...]
        a = (x * _rms(x) * g_ref[...]).astype(BF16)
        hn_ref[...] = a
        for k in range(s):
            u_ref[:, ns * k:ns * (k + 1)] = jnp.dot(a, w_ref[k], preferred_element_type=F32).astype(BF16)

    return pl.pallas_call(
        body, name=name, grid=(tp // tm,),
        in_specs=[pl.BlockSpec((tm, d), lambda i: (i, 0)), pl.BlockSpec((1, d), lambda i: (0, 0)), _resident(w.shape, 1)],
        out_specs=[pl.BlockSpec((tm, d), lambda i: (i, 0)), pl.BlockSpec((tm, s * ns), lambda i: (i, 0))],
        out_shape=[jax.ShapeDtypeStruct((tp, d), BF16), jax.ShapeDtypeStruct((tp, s * ns), BF16)],
        compiler_params=_cp(1))(h, gain, w)


def _norm_ffn_in(h, gain, w, name):
    tp, d = h.shape
    ff = w.shape[1] // 2
    tm = TM_RESIDENT
    blocks = [(c, min(c + 6 * MXU_TILE, ff)) for c in range(0, ff, 6 * MXU_TILE)]

    def body(h_ref, g_ref, w_ref, hn_ref, dg_ref, du_ref, act_ref):
        x = h_ref[...]
        a = (x * _rms(x) * g_ref[...]).astype(BF16)
        hn_ref[...] = a
        for c0, c1 in blocks:
            g = jnp.dot(a, w_ref[:, c0:c1], preferred_element_type=F32)
            u = jnp.dot(a, w_ref[:, ff + c0:ff + c1], preferred_element_type=F32)
            sg = _sigmoid(g)
            silu = g * sg
            dg_ref[:, c0:c1] = (u * (sg + silu * (1.0 - sg))).astype(BF16)
            du_ref[:, c0:c1] = silu.astype(BF16)
            act_ref[:, c0:c1] = (silu * u).astype(BF16)

    wide = jax.ShapeDtypeStruct((tp, ff), BF16)
    return pl.pallas_call(
        body, name=name, grid=(tp // tm,),
        in_specs=[pl.BlockSpec((tm, d), lambda i: (i, 0)), pl.BlockSpec((1, d), lambda i: (0, 0)),
                  _resident(w.shape, 1)],
        out_specs=[pl.BlockSpec((tm, d), lambda i: (i, 0))] + [pl.BlockSpec((tm, ff), lambda i: (i, 0))] * 3,
        out_shape=[jax.ShapeDtypeStruct((tp, d), BF16), wide, wide, wide],
        compiler_params=_cp(1))(h, gain, w)


def _out_proj(a, w, h, scale, name):
    tp, k = a.shape
    d = w.shape[1]

    def body(a_ref, w_ref, h_ref, o_ref):
        o_ref[...] = h_ref[...] + scale * jnp.dot(a_ref[...], w_ref[...], preferred_element_type=F32)

    return pl.pallas_call(
        body, name=name, grid=(tp // TM,),
        in_specs=[pl.BlockSpec((TM, k), lambda i: (i, 0)), pl.BlockSpec((k, d), lambda i: (0, 0)),
                  pl.BlockSpec((TM, d), lambda i: (i, 0))],
        out_specs=pl.BlockSpec((TM, d), lambda i: (i, 0)),
        out_shape=jax.ShapeDtypeStruct((tp, d), F32),
        compiler_params=_cp(1))(a, w, h)


def _dgrad(dh, w, name):
    tp, d = dh.shape
    k = w.shape[0]

    def body(dh_ref, w_ref, o_ref):
        o_ref[...] = lax.dot_general(dh_ref[...].astype(BF16), w_ref[...], (((1,), (1,)), ((), ())),
                                     preferred_element_type=F32).astype(BF16)

    return pl.pallas_call(
        body, name=name, grid=(tp // TM,),
        in_specs=[pl.BlockSpec((TM, d), lambda i: (i, 0)), pl.BlockSpec((k, d), lambda i: (0, 0))],
        out_specs=pl.BlockSpec((TM, k), lambda i: (i, 0)),
        out_shape=jax.ShapeDtypeStruct((tp, k), BF16),
        compiler_params=_cp(1))(dh, w)


def _wgrad(a, b, *, bm, bn, scale, sharded, name):
    tp, m = a.shape
    n = b.shape[1]
    nk = tp // TM

    def body(a_ref, b_ref, o_ref, acc_ref):
        k = pl.program_id(2)

        @pl.when(k == 0)
        def _():
            acc_ref[...] = jnp.zeros_like(acc_ref)

        bb = b_ref[...]
        if scale != 1.0:
            bb = scale * bb
        acc_ref[...] += lax.dot_general(a_ref[...], bb.astype(BF16), (((0,), (0,)), ((), ())),
                                        preferred_element_type=F32)

        @pl.when(k == nk - 1)
        def _():
            o_ref[...] = acc_ref[...].astype(BF16)

    if sharded:
        assert m == bm
        out_spec = pl.BlockSpec((None, bm, bn), lambda i, j, k: (j, 0, 0))
        out_shape = jax.ShapeDtypeStruct((n // bn, m, bn), BF16)
    else:
        out_spec = pl.BlockSpec((bm, bn), lambda i, j, k: (i, j))
        out_shape = jax.ShapeDtypeStruct((m, n), BF16)
    return pl.pallas_call(
        body, name=name, grid=(m // bm, n // bn, nk),
        in_specs=[pl.BlockSpec((TM, bm), lambda i, j, k: (k, i)), pl.BlockSpec((TM, bn), lambda i, j, k: (k, j))],
        out_specs=out_spec, out_shape=out_shape,
        scratch_shapes=[pltpu.VMEM((bm, bn), F32)],
        compiler_params=_cp(3))(a, b)


def _dgrad_norm(du, w, h, gain, dh_out, name):
    tp, d = h.shape
    s, _, ns = w.shape
    tm = TM_RESIDENT

    def body(du_ref, w_ref, h_ref, g_ref, dho_ref, dhi_ref, dg_ref):
        @pl.when(pl.program_id(0) == 0)
        def _():
            dg_ref[...] = jnp.zeros_like(dg_ref)

        dhn = None
        for k in range(s):
            part = lax.dot_general(du_ref[:, ns * k:ns * (k + 1)], w_ref[k], (((1,), (1,)), ((), ())),
                                   preferred_element_type=F32)
            dhn = part if dhn is None else dhn + part
        dx, xhat = _rmsnorm_bwd(dhn, h_ref[...], g_ref[...])
        dg_ref[...] += jnp.sum(dhn * xhat, axis=0, keepdims=True)
        dhi_ref[...] = dho_ref[...] + dx

    return pl.pallas_call(
        body, name=name, grid=(tp // tm,),
        in_specs=[pl.BlockSpec((tm, s * ns), lambda i: (i, 0)), _resident(w.shape, 1),
                  pl.BlockSpec((tm, d), lambda i: (i, 0)), pl.BlockSpec((1, d), lambda i: (0, 0)),
                  pl.BlockSpec((tm, d), lambda i: (i, 0))],
        out_specs=[pl.BlockSpec((tm, d), lambda i: (i, 0)), pl.BlockSpec((1, d), lambda i: (0, 0))],
        out_shape=[jax.ShapeDtypeStruct((tp, d), F32), jax.ShapeDtypeStruct((1, d), F32)],
        compiler_params=_cp(1))(du, w, h, gain, dh_out)


def _loss_head(h, gain, target, name):
    tp, d = h.shape
    tm = TM_SMALL
    front_tiles = FRONT // tm

    def body(h_ref, g_ref, t_ref, dh_ref, dg_ref, loss_ref):
        i = pl.program_id(0)

        @pl.when(i == 0)
        def _():
            dg_ref[...] = jnp.zeros_like(dg_ref)
            loss_ref[...] = jnp.zeros_like(loss_ref)

        x = h_ref[...]
        gain_v = g_ref[...]
        y = x * _rms(x) * gain_v
        err = jnp.where(i >= front_tiles, y - t_ref[...], 0.0)
        loss_ref[...] += 0.5 * jnp.sum(jnp.mean(err * err, axis=-1, keepdims=True), axis=0, keepdims=True)
        dy = err * (1.0 / d)
        dx, xhat = _rmsnorm_bwd(dy, x, gain_v)
        dg_ref[...] += jnp.sum(dy * xhat, axis=0, keepdims=True)
        dh_ref[...] = dx

    return pl.pallas_call(
        body, name=name, grid=(tp // tm,),
        in_specs=[pl.BlockSpec((tm, d), lambda i: (i, 0)), pl.BlockSpec((1, d), lambda i: (0, 0)),
                  pl.BlockSpec((tm, d), lambda i: (jnp.maximum(i - front_tiles, 0), 0))],
        out_specs=[pl.BlockSpec((tm, d), lambda i: (i, 0)), pl.BlockSpec((1, d), lambda i: (0, 0)),
                   pl.BlockSpec((1, 128), lambda i: (0, 0))],
        out_shape=[jax.ShapeDtypeStruct((tp, d), F32), jax.ShapeDtypeStruct((1, d), F32),
                   jax.ShapeDtypeStruct((1, 128), F32)],
        compiler_params=_cp(1))(h, gain, target)


def _gated_headnorm(o, g, gain):
    return o * _rms(o) * gain * (g * _sigmoid(g))


def _row_mask(chunk, size=CHUNK):
    rows = chunk * size + lax.broadcasted_iota(jnp.int32, (size, 1), 0)
    return (rows >= FRONT - N_META).astype(F32)


def _ret_head(q1, q2, k1, k2, v, g, state, gain, cos, sin, dmat, dq, dk, dc):
    q = jnp.concatenate([q1 * cos - q2 * sin, q1 * sin + q2 * cos], axis=-1)
    k = jnp.concatenate([k1 * cos - k2 * sin, k1 * sin + k2 * cos], axis=-1) * (RET_DK ** -0.5)
    scores = _nt(q, k) * dmat
    o = _nn(scores, v) + _nn(q * dq, state)
    new_state = state * dc + _tn(k * dk, v)
    return _gated_headnorm(o, g, gain), new_state


def _ret_consts():
    log_gamma = jnp.log1p(-2.0 ** (-5.0 - jnp.arange(HEADS, dtype=F32)))
    idx = jnp.arange(RET_CHUNK, dtype=F32)
    rel = idx[:, None] - idx[None, :]
    dmat = jnp.where(rel >= 0, jnp.exp(log_gamma[:, None, None] * jnp.maximum(rel, 0.0)), 0.0)
    dq = jnp.exp(log_gamma[:, None] * (idx + 1.0))[..., None]
    dk = jnp.exp(log_gamma[:, None] * (RET_CHUNK - 1.0 - idx))[..., None]
    dc = jnp.broadcast_to(jnp.exp(log_gamma * RET_CHUNK)[:, None, None], (HEADS, 1, 128))
    return dmat, dq, dk, dc


def _rope_tables(tp):
    half = RET_DK // 2
    inv = 1.0 / (ROPE_BASE ** jnp.linspace(0.0, 1.0, half, dtype=F32))
    pos = (jnp.arange(tp) - (FRONT - N_META)).astype(F32)
    ang = pos[:, None] * inv[None, :]
    return jnp.cos(ang), jnp.sin(ang)


_RET_V0, _RET_G0 = 2 * D, 4 * D


def _heads(ref, start, width, stride=None):
    stride = width if stride is None else stride
    return jnp.stack([ref[:, start + stride * h:start + stride * h + width].astype(F32) for h in range(HEADS)])


def _put_heads(ref, start, value, mask, stride=None):
    width = value.shape[-1]
    stride = width if stride is None else stride
    for h in range(HEADS):
        ref[:, start + stride * h:start + stride * h + width] = (value[h] * mask).astype(ref.dtype)


def _ret_pieces(u_ref):
    hk = RET_DK // 2
    return (_heads(u_ref, 0, hk, RET_DK), _heads(u_ref, hk, hk, RET_DK), _heads(u_ref, D, hk, RET_DK),
            _heads(u_ref, D + hk, hk, RET_DK), _heads(u_ref, _RET_V0, RET_DV), _heads(u_ref, _RET_G0, RET_DV))


def _ret_const_specs(rev=None):
    c = (lambda n: (rev(n), 0)) if rev else (lambda n: (n, 0))
    z3 = lambda n: (0, 0, 0)
    return [pl.BlockSpec((RET_CHUNK, RET_DK // 2), c), pl.BlockSpec((RET_CHUNK, RET_DK // 2), c),
            pl.BlockSpec((HEADS, RET_CHUNK, RET_CHUNK), z3), pl.BlockSpec((HEADS, RET_CHUNK, 1), z3),
            pl.BlockSpec((HEADS, RET_CHUNK, 1), z3), pl.BlockSpec((HEADS, 1, 128), z3)]


def _ret_fwd(u, gain, rope, h, w_out, name):
    tp = u.shape[0]
    nch = tp // RET_CHUNK
    cos, sin = rope
    dmat, dq, dk, dc = _ret_consts()

    def body(u_ref, gain_ref, h_ref, w_ref, cos_ref, sin_ref, dmat_ref, dq_ref, dk_ref, dc_ref,
             on_ref, st_ref, hmix_ref, state_ref):
        @pl.when(pl.program_id(0) == 0)
        def _():
            state_ref[...] = jnp.zeros_like(state_ref)

        state = state_ref[...]
        st_ref[...] = state.astype(BF16)
        on, new_state = _ret_head(*_ret_pieces(u_ref), state, _heads(gain_ref, 0, RET_DV), cos_ref[...], sin_ref[...],
                                  dmat_ref[...], dq_ref[...], dk_ref[...], dc_ref[...][:, :, :1])
        state_ref[...] = new_state
        _put_heads(on_ref, 0, on, 1.0)
        hmix_ref[...] = h_ref[...] + jnp.dot(on_ref[...], w_ref[...], preferred_element_type=F32)

    rows = lambda width: pl.BlockSpec((RET_CHUNK, width), lambda n: (n, 0))
    return pl.pallas_call(
        body, name=name, grid=(nch,),
        in_specs=[rows(6 * D), pl.BlockSpec((1, HEADS * RET_DV), lambda n: (0, 0)), rows(D),
                  _resident(w_out.shape, 1)] + _ret_const_specs(),
        out_specs=[rows(HEADS * RET_DV), pl.BlockSpec((None, HEADS, RET_DK, RET_DV), lambda n: (n, 0, 0, 0)), rows(D)],
        out_shape=[jax.ShapeDtypeStruct((tp, HEADS * RET_DV), BF16),
                   jax.ShapeDtypeStruct((nch, HEADS, RET_DK, RET_DV), BF16), jax.ShapeDtypeStruct((tp, D), F32)],
        scratch_shapes=[pltpu.VMEM((HEADS, RET_DK, RET_DV), F32)],
        compiler_params=_cp(1))(u, gain, h, w_out, cos, sin, dmat, dq, dk, dc)


def _ret_bwd(u, gain, rope, states, d_on, name):
    tp = u.shape[0]
    nch = tp // RET_CHUNK
    cos, sin = rope
    dmat, dq, dk, dc = _ret_consts()
    rev = lambda n: nch - 1 - n
    hk = RET_DK // 2

    def body(u_ref, gain_ref, st_ref, don_ref, cos_ref, sin_ref, dmat_ref, dq_ref, dk_ref, dc_ref,
             du_ref, dgain_ref, dstate_ref):
        @pl.when(pl.program_id(0) == 0)
        def _():
            dstate_ref[...] = jnp.zeros_like(dstate_ref)
            dgain_ref[...] = jnp.zeros_like(dgain_ref)

        mask = _row_mask(rev(pl.program_id(0)), RET_CHUNK)
        consts = (cos_ref[...], sin_ref[...], dmat_ref[...], dq_ref[...], dk_ref[...], dc_ref[...][:, :, :1])
        _, vjp = jax.vjp(lambda *a: _ret_head(*a, *consts), *_ret_pieces(u_ref), st_ref[...].astype(F32),
                         _heads(gain_ref, 0, RET_DV))
        dq1, dq2, dk1, dk2, dv, dg, dstate, dgain = vjp((_heads(don_ref, 0, RET_DV), dstate_ref[...]))
        dstate_ref[...] = dstate
        for hd in range(HEADS):
            dgain_ref[:, RET_DV * hd:RET_DV * (hd + 1)] += dgain[hd]
        _put_heads(du_ref, 0, dq1, mask, RET_DK)
        _put_heads(du_ref, hk, dq2, mask, RET_DK)
        _put_heads(du_ref, D, dk1, mask, RET_DK)
        _put_heads(du_ref, D + hk, dk2, mask, RET_DK)
        _put_heads(du_ref, _RET_V0, dv, mask)
        _put_heads(du_ref, _RET_G0, dg, mask)

    return pl.pallas_call(
        body, name=name, grid=(nch,),
        in_specs=[pl.BlockSpec((RET_CHUNK, 6 * D), lambda n: (rev(n), 0)),
                  pl.BlockSpec((1, HEADS * RET_DV), lambda n: (0, 0)),
                  pl.BlockSpec((None, HEADS, RET_DK, RET_DV), lambda n: (rev(n), 0, 0, 0)),
                  pl.BlockSpec((RET_CHUNK, HEADS * RET_DV), lambda n: (rev(n), 0))] + _ret_const_specs(rev),
        out_specs=[pl.BlockSpec((RET_CHUNK, 6 * D), lambda n: (rev(n), 0)),
                   pl.BlockSpec((1, HEADS * RET_DV), lambda n: (0, 0))],
        out_shape=[jax.ShapeDtypeStruct((tp, 6 * D), BF16), jax.ShapeDtypeStruct((1, HEADS * RET_DV), F32)],
        scratch_shapes=[pltpu.VMEM((HEADS, RET_DK, RET_DV), F32)],
        compiler_params=_cp(1))(u, gain, states, d_on, cos, sin, dmat, dq, dk, dc)


_GLA_K0, _GLA_V0, _GLA_G0, _GLA_Z0 = 512, 1024, 2048, 3072


def _gla_head(q, k, v, g, z, state_t, wg, bg, gain, mask, lo, lo_t, loc, loc_t):
    ga = _nn(jnp.broadcast_to(z, wg.shape[:-2] + z.shape), wg) + bg
    log_a = (jnp.minimum(ga, 0.0) - jnp.log(1.0 + jnp.exp(-jnp.abs(ga)))) * (mask * (1.0 / GLA_TAU))
    bcum = _cum(lo, lo_t, log_a)
    bmid = _cum(loc, loc_t, log_a)
    btot = jnp.sum(log_a, axis=-2, keepdims=True)
    qs = q * (GLA_DK ** -0.5)
    causal = lax.broadcasted_iota(jnp.int32, (CHUNK, CHUNK), 0) >= lax.broadcasted_iota(jnp.int32, (CHUNK, CHUNK), 1)
    scores = jnp.where(causal, _nt(qs * jnp.exp(bmid), k * jnp.exp(-bmid)), 0.0)
    o = _nn(scores, v) + _nt(qs * jnp.exp(bcum), state_t)
    new_state_t = state_t * jnp.exp(btot) + _tn(v, k * jnp.exp(btot - bcum))
    return _gated_headnorm(o, g, gain), new_state_t


def _cum_mats():
    r = lax.broadcasted_iota(jnp.int32, (CHUNK, CHUNK), 0)
    c = lax.broadcasted_iota(jnp.int32, (CHUNK, CHUNK), 1)
    mid = CHUNK // 2
    low = lambda a, b: (a >= b).astype(F32)
    lo, lo_t = low(r, c), low(c, r)
    loc = lo - (c <= mid).astype(F32)
    loc_t = lo_t - (r <= mid).astype(F32)
    return tuple(m.astype(BF16) for m in (lo, lo_t, loc, loc_t))


def _gla_pieces(u_ref):
    return (_heads(u_ref, 0, GLA_DK), _heads(u_ref, _GLA_K0, GLA_DK), _heads(u_ref, _GLA_V0, GLA_DV),
            _heads(u_ref, _GLA_G0, GLA_DV), u_ref[:, _GLA_Z0:_GLA_Z0 + 128].astype(F32))


def _gla_fwd(u, wg, bg, gain, name):
    tp = u.shape[0]
    nch = tp // CHUNK

    def body(u_ref, wg_ref, bg_ref, gain_ref, on_ref, st_ref, state_ref):
        @pl.when(pl.program_id(0) == 0)
        def _():
            state_ref[...] = jnp.zeros_like(state_ref)

        state = state_ref[...]
        st_ref[...] = state.astype(BF16)
        on, new_state = _gla_head(*_gla_pieces(u_ref), state, _heads(wg_ref, 0, GLA_DK), _heads(bg_ref, 0, GLA_DK),
                                  _heads(gain_ref, 0, GLA_DV), _row_mask(pl.program_id(0)), *_cum_mats())
        state_ref[...] = new_state
        _put_heads(on_ref, 0, on, 1.0)

    rows = lambda width: pl.BlockSpec((CHUNK, width), lambda n: (n, 0))
    full = lambda r, c: pl.BlockSpec((r, c), lambda n: (0, 0))
    return pl.pallas_call(
        body, name=name, grid=(nch,),
        in_specs=[rows(GLA_U), full(128, HEADS * GLA_DK), full(1, HEADS * GLA_DK), full(1, HEADS * GLA_DV)],
        out_specs=[rows(HEADS * GLA_DV), pl.BlockSpec((None, HEADS, GLA_DV, GLA_DK), lambda n: (n, 0, 0, 0))],
        out_shape=[jax.ShapeDtypeStruct((tp, HEADS * GLA_DV), BF16),
                   jax.ShapeDtypeStruct((nch, HEADS, GLA_DV, GLA_DK), BF16)],
        scratch_shapes=[pltpu.VMEM((HEADS, GLA_DV, GLA_DK), F32)],
        compiler_params=_cp(1))(u, wg, bg, gain)


def _gla_bwd(u, wg, bg, gain, states, d_on, name):
    tp = u.shape[0]
    nch = tp // CHUNK
    rev = lambda n: nch - 1 - n

    def body(u_ref, wg_ref, bg_ref, gain_ref, st_ref, don_ref, du_ref, dwg_ref, dbg_ref, dgain_ref, dstate_ref):
        @pl.when(pl.program_id(0) == 0)
        def _():
            dstate_ref[...] = jnp.zeros_like(dstate_ref)
            dwg_ref[...] = jnp.zeros_like(dwg_ref)
            dbg_ref[...] = jnp.zeros_like(dbg_ref)
            dgain_ref[...] = jnp.zeros_like(dgain_ref)

        mask = _row_mask(rev(pl.program_id(0)))
        mats = _cum_mats()
        _, vjp = jax.vjp(lambda *a: _gla_head(*a, mask, *mats), *_gla_pieces(u_ref), st_ref[...].astype(F32),
                         _heads(wg_ref, 0, GLA_DK), _heads(bg_ref, 0, GLA_DK), _heads(gain_ref, 0, GLA_DV))
        dq, dk, dv, dg, dz, dstate, dwg, dbg, dgain = vjp((_heads(don_ref, 0, GLA_DV), dstate_ref[...]))
        dstate_ref[...] = dstate
        for hd in range(HEADS):
            dwg_ref[:, GLA_DK * hd:GLA_DK * (hd + 1)] += dwg[hd]
            dbg_ref[:, GLA_DK * hd:GLA_DK * (hd + 1)] += dbg[hd]
            dgain_ref[:, GLA_DV * hd:GLA_DV * (hd + 1)] += dgain[hd]
        _put_heads(du_ref, 0, dq, mask)
        _put_heads(du_ref, _GLA_K0, dk, mask)
        _put_heads(du_ref, _GLA_V0, dv, mask)
        _put_heads(du_ref, _GLA_G0, dg, mask)
        du_ref[:, _GLA_Z0:_GLA_Z0 + 128] = dz.astype(BF16)
        du_ref[:, _GLA_Z0 + 128:] = jnp.zeros((CHUNK, GLA_U - _GLA_Z0 - 128), BF16)

    full = lambda r, c: pl.BlockSpec((r, c), lambda n: (0, 0))
    return pl.pallas_call(
        body, name=name, grid=(nch,),
        in_specs=[pl.BlockSpec((CHUNK, GLA_U), lambda n: (rev(n), 0)), full(128, HEADS * GLA_DK),
                  full(1, HEADS * GLA_DK), full(1, HEADS * GLA_DV),
                  pl.BlockSpec((None, HEADS, GLA_DV, GLA_DK), lambda n: (rev(n), 0, 0, 0)),
                  pl.BlockSpec((CHUNK, HEADS * GLA_DV), lambda n: (rev(n), 0))],
        out_specs=[pl.BlockSpec((CHUNK, GLA_U), lambda n: (rev(n), 0)), full(128, HEADS * GLA_DK),
                   full(1, HEADS * GLA_DK), full(1, HEADS * GLA_DV)],
        out_shape=[jax.ShapeDtypeStruct((tp, GLA_U), BF16), jax.ShapeDtypeStruct((128, HEADS * GLA_DK), F32),
                   jax.ShapeDtypeStruct((1, HEADS * GLA_DK), F32), jax.ShapeDtypeStruct((1, HEADS * GLA_DV), F32)],
        scratch_shapes=[pltpu.VMEM((HEADS, GLA_DV, GLA_DK), F32)],
        compiler_params=_cp(1))(u, wg, bg, gain, states, d_on)


def _ffn_fwd(h, gain, w_in, w_out, tag):
    hn, ug, uu, act = _norm_ffn_in(h, gain, w_in, f"{tag}_in")
    if callable(w_out):
        w_out = w_out(act)
    return _out_proj(act, w_out, h, 0.5, f"{tag}_out"), (h, hn, ug, uu, act), w_out


def _ffn_dgrad(dh, w_out, w_in, act_dg, act_du, h, gain, name, split_front=False):
    tp, d = dh.shape
    ff = w_out.shape[0]
    tm = TM_SMALL
    nt = (((1,), (1,)), ((), ()))

    def body(dh_ref, wo_ref, wi_ref, dg_ref, du_ref, h_ref, g_ref, o_ref, *out_refs):
        dhi_ref, dgain_ref = out_refs[-2:]

        @pl.when(pl.program_id(0) == 0)
        def _():
            dgain_ref[...] = jnp.zeros_like(dgain_ref)

        dho = dh_ref[...]
        dact = lax.dot_general((0.5 * dho).astype(BF16), wo_ref[...], nt, preferred_element_type=F32)
        d_gate = (dact * dg_ref[...].astype(F32)).astype(BF16)
        d_up = (dact * du_ref[...].astype(F32)).astype(BF16)
        o_ref[:, :ff] = d_gate
        o_ref[:, ff:] = d_up
        dhn = (lax.dot_general(d_gate, wi_ref[:, :ff], nt, preferred_element_type=F32)
               + lax.dot_general(d_up, wi_ref[:, ff:], nt, preferred_element_type=F32))
        dx, xhat = _rmsnorm_bwd(dhn, h_ref[...], g_ref[...])
        dgain_ref[...] += jnp.sum(dhn * xhat, axis=0, keepdims=True)
        dhi_ref[...] = dho + dx
        if split_front:
            @pl.when(pl.program_id(0) == 0)
            def _():
                out_refs[0][...] = dho + dx

    rows = lambda width: pl.BlockSpec((tm, width), lambda i: (i, 0))
    if split_front:
        assert tm == FRONT
        dhi_specs = [pl.BlockSpec((tm, d), lambda i: (0, 0)), pl.BlockSpec((tm, d), lambda i: (jnp.maximum(i - 1, 0), 0))]
        dhi_shapes = [jax.ShapeDtypeStruct((FRONT, d), F32), jax.ShapeDtypeStruct((tp - FRONT, d), F32)]
    else:
        dhi_specs, dhi_shapes = [rows(d)], [jax.ShapeDtypeStruct((tp, d), F32)]
    out = pl.pallas_call(
        body, name=name, grid=(tp // tm,),
        in_specs=[rows(d), _resident(w_out.shape, 1), _resident(w_in.shape, 1), rows(ff), rows(ff), rows(d),
                  pl.BlockSpec((1, d), lambda i: (0, 0))],
        out_specs=[rows(2 * ff), *dhi_specs, pl.BlockSpec((1, d), lambda i: (0, 0))],
        out_shape=[jax.ShapeDtypeStruct((tp, 2 * ff), BF16), *dhi_shapes, jax.ShapeDtypeStruct((1, d), F32)],
        compiler_params=_cp(1))(dh, w_out, w_in, act_dg, act_du, h, gain)
    return (out[0], tuple(out[1:3]), out[3]) if split_front else tuple(out)


def _ffn_bwd(dh, saved, gain, w_in, w_out, tag, push, split_front=False):
    h, hn, act_dg, act_du, act = saved
    du, dh_in, d_gain = _ffn_dgrad(dh, w_out, w_in, act_dg, act_du, h, gain, f"{tag}_dgrad", split_front)
    d_w_out = _wgrad(act, dh, bm=D_FF // 2, bn=D, scale=0.5, sharded=False, name=f"{tag}_dwout")
    d_w_in = _wgrad(hn, du, bm=D, bn=D_FF, scale=1.0, sharded=False, name=f"{tag}_dwin")
    return dh_in, d_gain, push([("cols", d_w_in), d_w_out])


def _sequence_grads(x, target, p, weights, grads):
    row = lambda v, token: v.reshape(1, -1) + token[0, 0]
    gains = {}

    tok = weights.start(1, weights.start(0, None))
    weights.pin = tok
    h = jnp.concatenate([jnp.zeros((FRONT, D), F32), x], axis=0) + tok[0, 0]
    rope = _rope_tables(h.shape[0])
    w = weights.wait(0, [tok, h, *rope, *weights.later_shards(2)])
    tok = weights.start(2, w["l0_ffn1_in"])
    h = lax.dynamic_update_slice(h, w["meta"], (FRONT - N_META, 0))
    gains["l0_ffn1"] = row(p["norm_ffn1"][0], tok)
    h, s1, w["l0_ffn1_out"] = _ffn_fwd(h, gains["l0_ffn1"], w["l0_ffn1_in"],
                                       lambda act: weights.wait(1, act)["l0_ffn1_out"], "l0_ffn1")
    w.update(weights.wait(2, h))
    tok = weights.start(4, weights.start(3, w["ret_in"]))
    gains["ret"] = row(p["norm_mix"][0], tok)
    hn, u = _norm_proj(h, gains["ret"], w["ret_in"], "ret_in")
    w.update(weights.wait(3, u))
    on, states, h_mix = _ret_fwd(u, w["ret_gain"], rope, h, w["ret_out"], "ret_fwd")
    s2 = (h, hn, u, on, states)
    w.update(weights.wait(4, h_mix))
    tok = weights.start(5, w["l0_ffn2_in"])
    gains["l0_ffn2"] = row(p["norm_ffn2"][0], tok)
    h, s3, _ = _ffn_fwd(h_mix, gains["l0_ffn2"], w["l0_ffn2_in"], w["l0_ffn2_out"], "l0_ffn2")
    saved = [(s1, s2, s3)]

    w.update(weights.wait(5, h))
    tok = weights.start(6, w["l1_ffn1_in"])
    gains["l1_ffn1"] = row(p["norm_ffn1"][1], tok)
    h, s1, _ = _ffn_fwd(h, gains["l1_ffn1"], w["l1_ffn1_in"], w["l1_ffn1_out"], "l1_ffn1")
    w.update(weights.wait(6, h))
    tok = weights.start(7, w["gla_out"])
    gains["gla"] = row(p["norm_mix"][1], tok)
    hn, u = _norm_proj(h, gains["gla"], w["gla_in"], "gla_in")
    on, states = _gla_fwd(u, w["gla_wg"], w["gla_bg"], w["gla_gain"], "gla_fwd")
    h_mix = _out_proj(on, w["gla_out"], h, 1.0, "gla_out")
    s2 = (h, hn, u, on, states)
    w.update(weights.wait(7, h_mix))
    gains["l1_ffn2"] = p["norm_ffn2"][1].reshape(1, -1)
    h, s3, _ = _ffn_fwd(h_mix, gains["l1_ffn2"], w["l1_ffn2_in"], w["l1_ffn2_out"], "l1_ffn2")
    saved.append((s1, s2, s3))

    dh, d_final, loss = _loss_head(h, p["final_norm"].reshape(1, -1), target, "loss_head")
    small = {"final_norm": d_final, "norm_ffn1": [None, None], "norm_mix": [None, None], "norm_ffn2": [None, None]}
    pusher = lambda k: functools.partial(grads.push, k)

    s1, s2, s3 = saved[1]
    dh, small["norm_ffn2"][1], tok = _ffn_bwd(dh, s3, gains["l1_ffn2"], w["l1_ffn2_in"], w["l1_ffn2_out"], "l1_ffn2",
                                              pusher(0))
    h_in, hn, u, on, states = s2
    d_on = _dgrad(dh, w["gla_out"], "gla_don")
    d_out = _wgrad(on, dh, bm=D, bn=D, scale=1.0, sharded=False, name="gla_dwout")
    du, small["gla_wg"], small["gla_bg"], small["gla_gain"] = _gla_bwd(
        u, w["gla_wg"], w["gla_bg"], w["gla_gain"] + tok[0, 0], states, d_on, "gla_bwd")
    d_in = _wgrad(hn, du, bm=D, bn=GLA_U, scale=1.0, sharded=False, name="gla_dwin")
    d_in = jnp.moveaxis(d_in[:, :GLA_IN].reshape(D, N_CHIPS, -1), 1, 0)
    tok = grads.push(1, [d_in, d_out])
    dh, small["norm_mix"][1] = _dgrad_norm(du, w["gla_in"], h_in, gains["gla"] + tok[0, 0], dh, "gla_dnorm")
    dh, small["norm_ffn1"][1], tok = _ffn_bwd(dh, s1, gains["l1_ffn1"], w["l1_ffn1_in"], w["l1_ffn1_out"], "l1_ffn1",
                                              pusher(2))

    s1, s2, s3 = saved[0]
    dh, small["norm_ffn2"][0], tok = _ffn_bwd(dh, s3, gains["l0_ffn2"] + tok[0, 0], w["l0_ffn2_in"],
                                              w["l0_ffn2_out"], "l0_ffn2", pusher(3))
    h_in, hn, u, on, states = s2
    d_on = _dgrad(dh, w["ret_out"], "ret_don")
    d_out = _wgrad(on, dh, bm=D, bn=D, scale=1.0, sharded=False, name="ret_dwout")
    du, small["ret_gain"] = _ret_bwd(u, w["ret_gain"] + tok[0, 0], rope, states, d_on, "ret_bwd")
    d_in = _wgrad(hn, du, bm=D, bn=w["ret_in"].shape[2], scale=1.0, sharded=True, name="ret_dwin")
    tok = grads.push(4, [d_in, d_out])
    dh, small["norm_mix"][0] = _dgrad_norm(du, w["ret_in"], h_in, gains["ret"] + tok[0, 0], dh, "ret_dnorm")
    (d_front, d_x), small["norm_ffn1"][0], tok = _ffn_bwd(dh, s1, gains["l0_ffn1"], w["l0_ffn1_in"], w["l0_ffn1_out"],
                                                          "l0_ffn1", pusher(5), split_front=True)
    grads.push(6, [], [d_front[FRONT - N_META:], *small["norm_ffn1"], *small["norm_mix"], *small["norm_ffn2"],
                       small["final_norm"], small["ret_gain"], small["gla_wg"][:GLA_RANK], small["gla_bg"],
                       small["gla_gain"], loss[:, :1] + tok[0, 0]])
    return d_x


_HBM = pl.BlockSpec(memory_space=pl.ANY)


def _place():
    return lax.axis_index("x"), lax.axis_index("y"), lax.axis_index("c")


def _flip(v, bit):
    return 1 - v if bit else v


DMA_CHUNK_BYTES = 128 * 1024


def _row_chunks(ref):
    rows, cols = ref.shape
    step = _row_tile(rows, max(16, DMA_CHUNK_BYTES // (cols * ref.dtype.itemsize)))
    return [pl.ds(a, step) for a in range(0, rows, step)]


def _whole(src, dst, send_sem, recv_sem, peer):
    return pltpu.make_async_remote_copy(src_ref=src, dst_ref=dst, send_sem=send_sem, recv_sem=recv_sem,
                                        device_id=peer, device_id_type=MESH)


def _send(src, dst, send_sem, recv_sem, peer):
    for rows in _row_chunks(src):
        _whole(src.at[rows], dst.at[rows], send_sem, recv_sem, peer).start()
    return _whole(src, dst, send_sem, recv_sem, peer)


_HBM_ONLY = pl.BlockSpec(memory_space=pltpu.HBM)
_SEMS = pl.BlockSpec(memory_space=pltpu.SEMAPHORE)
_SIDE_EFFECT = pltpu.CompilerParams(has_side_effects=pltpu.SideEffectType.DATAFLOW_SIDE_EFFECTING)
_GATHER_FLIPS = [(1, 0, 0), (0, 1, 0), (1, 1, 0), (0, 0, 1)]
_PEER_FLIPS = [(fx, fy, fc) for fx in (0, 1) for fy in (0, 1) for fc in (0, 1)][1:]


def _zero_token():
    return jnp.zeros((8, 128), F32)


def _exchange_start(srcs, lands, route, flips, after, name):
    n = len(srcs)

    def body(*refs):
        src, land = refs[:n], refs[n:2 * n]
        send_sems, recv_sems, token = refs[2 * n + 1], refs[2 * n + 2], refs[-1]
        me = _place()
        for t in range(n):
            for j, flip in enumerate(flips):
                peer = tuple(_flip(v, f) for v, f in zip(me, flip))
                s, d = route(t, src[t], land[t], me, peer)
                _send(s, d, send_sems.at[t * len(flips) + j], recv_sems.at[t * len(flips) + j], peer)
        token[...] = jnp.zeros_like(token)

    hbm = lambda a: pltpu.HBM(a.shape, a.dtype)
    sems = pltpu.SemaphoreType.DMA((n * len(flips),))
    operands = [pltpu.with_memory_space_constraint(a, pltpu.HBM) for a in list(srcs) + list(lands)]
    out = pl.pallas_call(
        body, name=name, in_specs=[_HBM_ONLY] * (2 * n) + [_HBM],
        out_shape=(sems, sems, *[hbm(a) for a in operands], jax.ShapeDtypeStruct((8, 128), F32)),
        out_specs=(_SEMS, _SEMS, *[_HBM_ONLY] * (2 * n), pl.BlockSpec(memory_space=pltpu.VMEM)),
        input_output_aliases={i: 2 + i for i in range(2 * n)}, compiler_params=_SIDE_EFFECT,
    )(*operands, _zero_token() if after is None else after)
    return (out[0], out[1], out[2:2 + n], out[2 + n:2 + 2 * n]), out[-1]


def _exchange_wait(started, route, flips, after, name):
    send_sems, recv_sems, srcs, lands = started
    n = len(srcs)

    def body(*refs):
        src, land = refs[:n], refs[n:2 * n]
        send_sems, recv_sems = refs[2 * n], refs[2 * n + 1]
        me = _place()
        for t in range(n):
            for j, flip in enumerate(flips):
                peer = tuple(_flip(v, f) for v, f in zip(me, flip))
                s, d = route(t, src[t], land[t], me, peer)
                cp = _whole(s, d, send_sems.at[t * len(flips) + j], recv_sems.at[t * len(flips) + j], peer)
                cp.wait_send()
                cp.wait_recv()

    hbm = lambda a: pltpu.HBM(a.shape, a.dtype)
    after = list(after) if isinstance(after, (list, tuple)) else [after]
    out = pl.pallas_call(
        body, name=name, in_specs=[_HBM_ONLY] * (2 * n) + [_SEMS, _SEMS] + [_HBM] * len(after),
        out_shape=tuple(hbm(a) for a in list(srcs) + list(lands)), out_specs=tuple([_HBM_ONLY] * (2 * n)),
        input_output_aliases={i: i for i in range(2 * n)}, compiler_params=_SIDE_EFFECT,
    )(*srcs, *lands, send_sems, recv_sems, *after)
    return out[:n], out[n:]


def _gather_route(t, src, land, me, peer):
    mine = 2 * me[0] + me[1]
    if land.ndim == 3:
        return src, land.at[mine]
    cols = src.shape[1]
    return src, land.at[:, pl.ds(pl.multiple_of(mine * cols, 128), cols)]


def _scatter_route(n_pieces):
    def route(t, src, land, me, peer):
        chip = 2 * peer[0] + peer[1]
        if t >= n_pieces:
            part = src
        elif src.ndim == 4:
            part = src.at[chip, peer[2]]
        else:
            rows, cols = land.shape[1:]
            part = src.at[pl.ds(pl.multiple_of(peer[2] * rows, 16), rows), pl.ds(pl.multiple_of(chip * cols, 128), cols)]
        return part, land.at[4 * me[0] + 2 * me[1] + me[2]]

    return route


def _swap_cores(halves, name):
    n = len(halves)

    def body(*refs):
        src, dst = refs[:n], refs[n:2 * n]
        send_sems, recv_sems = refs[2 * n:]
        x, y, c = _place()
        copies = [_send(src[t], dst[t], send_sems.at[t], recv_sems.at[t], (x, y, 1 - c)) for t in range(n)]
        for cp in copies:
            cp.wait()

    got = pl.pallas_call(
        body, name=name, in_specs=[_HBM] * n, out_specs=[_HBM] * n,
        out_shape=[jax.ShapeDtypeStruct(a.shape, a.dtype) for a in halves],
        scratch_shapes=[pltpu.SemaphoreType.DMA((n,)), pltpu.SemaphoreType.DMA((n,))],
    )(*halves)
    south = lax.axis_index("c") == 0
    return [jnp.stack([jnp.where(south, a, b), jnp.where(south, b, a)]) for a, b in zip(halves, got)]


def _row_tile(rows, cap):
    fits = [t for t in range(16, cap + 1, 16) if rows % t == 0]
    return fits[-1] if fits else rows


def _sum_slots(a, name):
    _, r, c = a.shape
    tr = _row_tile(r, 384)

    def body(a_ref, o_ref):
        s = a_ref[0].astype(F32)
        for k in range(1, N_DEV):
            s = s + a_ref[k].astype(F32)
        o_ref[...] = s

    return pl.pallas_call(
        body, name=name, grid=(r // tr,),
        in_specs=[pl.BlockSpec((N_DEV, tr, c), lambda i: (0, i, 0))],
        out_specs=pl.BlockSpec((tr, c), lambda i: (i, 0)),
        out_shape=jax.ShapeDtypeStruct((r, c), F32),
        compiler_params=_cp(1))(a)


def _adamw(w, g, m, v, name):
    layers, r, c = w.shape
    tr = _row_tile(r, 256)

    def body(w_ref, g_ref, m_ref, v_ref, d_ref, nm_ref, nv_ref):
        gv = g_ref[...]
        nm = ADAM_B1 * m_ref[...] + (1.0 - ADAM_B1) * gv
        nv = ADAM_B2 * v_ref[...] + (1.0 - ADAM_B2) * (gv * gv)
        m_hat = nm / (1.0 - ADAM_B1 ** ADAM_STEP)
        v_hat = nv / (1.0 - ADAM_B2 ** ADAM_STEP)
        d_ref[...] = -ADAM_LR * (m_hat / (jnp.sqrt(v_hat) + ADAM_EPS) + ADAM_WD * w_ref[...])
        nm_ref[...] = nm
        nv_ref[...] = nv

    spec = pl.BlockSpec((None, tr, c), lambda a, i: (a, i, 0))
    return pl.pallas_call(
        body, name=name, grid=(layers, r // tr), in_specs=[spec] * 4, out_specs=[spec] * 3,
        out_shape=[jax.ShapeDtypeStruct((layers, r, c), F32)] * 3,
        compiler_params=_cp(2))(w, g, m, v)


_SMALL = ["meta_tokens", "ret_head_norm", "gla_w_gate", "gla_b_gate", "gla_head_norm"]
_LOCAL_SMALL = ["meta_tokens", "norm_ffn1", "norm_mix", "norm_ffn2", "ret_head_norm", "gla_w_gate", "gla_b_gate",
                "gla_head_norm", "final_norm"]
_WEIGHTS = ["meta_tokens", "norm_ffn1", "ffn1_w_in", "ffn1_w_out", "norm_mix", "norm_ffn2", "ffn2_w_in", "ffn2_w_out",
            "ret_w_in", "ret_head_norm", "ret_w_out", "gla_w_in", "gla_w_gate", "gla_b_gate", "gla_head_norm",
            "gla_w_out", "final_norm"]


def _pack_rows(arrays, width):
    flat = jnp.concatenate([a.reshape(-1) for a in arrays])
    pad = -flat.shape[0] % (8 * width)
    return jnp.pad(flat, (0, pad)).reshape(-1, width)


def _unpack_rows(packed, shapes):
    flat, out, at = packed.reshape(-1), [], 0
    for s in shapes:
        size = 1
        for dim in s:
            size *= dim
        out.append(flat[at:at + size].reshape(s))
        at += size
    return out


class _WeightGather:
    GROUPS = [("small", "l0_ffn1_in"), ("l0_ffn1_out",), ("ret_in",), ("ret_out",), ("l0_ffn2_in", "l0_ffn2_out"),
              ("l1_ffn1_in", "l1_ffn1_out"), ("gla_in", "gla_out"), ("l1_ffn2_in", "l1_ffn2_out")]

    def __init__(self, p):
        self.small_shapes = [p[name].shape for name in _SMALL]
        self.f32 = {"small": _pack_rows([p[name] for name in _SMALL], 128), "ret_in": p["ret_w_in"][0],
                    "ret_out": p["ret_w_out"][0], "gla_in": p["gla_w_in"][0], "gla_out": p["gla_w_out"][0]}
        for layer in range(2):
            for name in ("ffn1", "ffn2"):
                self.f32[f"l{layer}_{name}_in"] = p[f"{name}_w_in"][layer]
                self.f32[f"l{layer}_{name}_out"] = p[f"{name}_w_out"][layer]
        self.shards = {}
        self.started = {}
        self.pin = None

    def shard(self, name):
        if name not in self.shards:
            a = self.f32[name]
            if name != "small":
                a = (a if self.pin is None else a + self.pin[0, 0]).astype(BF16)
            self.shards[name] = a
        return self.shards[name]

    def later_shards(self, k):
        return [self.shard(name) for group in self.GROUPS[k:] for name in group]

    def start(self, k, after):
        shards = [self.shard(name) for name in self.GROUPS[k]]
        lands = []
        for name, s in zip(self.GROUPS[k], shards):
            if "ffn" in name and name.endswith("_in"):
                lands.append(lax.empty((s.shape[0], N_CHIPS * s.shape[1]), s.dtype))
            else:
                lands.append(lax.empty((N_CHIPS,) + s.shape, s.dtype))
        self.started[k], token = _exchange_start(shards, lands, _gather_route, _GATHER_FLIPS, after, f"gather{k}_start")
        return token

    def wait(self, k, after):
        _, got = _exchange_wait(self.started[k], _gather_route, _GATHER_FLIPS, after, f"gather{k}_wait")
        w = {}
        for name, g in zip(self.GROUPS[k], got):
            if name == "small":
                parts = zip(*[_unpack_rows(g[chip], self.small_shapes) for chip in range(N_CHIPS)])
                cat = lambda a: jnp.moveaxis(a, 0, -2).reshape(a.shape[1:-1] + (-1,))
                meta, ret_gain, wg, bg, gla_gain = [cat(jnp.stack(part)) for part in parts]
                w.update(meta=meta, ret_gain=ret_gain.reshape(1, -1), gla_bg=bg.reshape(1, -1),
                         gla_gain=gla_gain.reshape(1, -1),
                         gla_wg=jnp.pad(wg[0], ((0, 128 - GLA_RANK), (0, 0))).astype(BF16))
            elif name == "gla_in":
                full = jnp.moveaxis(g, 0, 1).reshape(D, -1)
                w[name] = jnp.pad(full, ((0, 0), (0, GLA_U - GLA_IN)))[None]
            elif name.endswith("_out"):
                w[name] = g.reshape(-1, g.shape[-1])
            else:
                w[name] = g
        return w


class _GradExchange:
    def __init__(self):
        self.started = []
        self.token = None
        self.small_shapes = None

    def push(self, k, arrays, small=None):
        srcs, lands = [], []
        for a in arrays:
            if isinstance(a, tuple):
                a = a[1]
                piece = (a.shape[0] // 2, a.shape[1] // N_CHIPS)
            else:
                a = a.reshape(N_CHIPS, 2, -1, a.shape[-1])
                piece = a.shape[2:]
            srcs.append(a)
            lands.append(lax.empty((N_DEV,) + piece, a.dtype))
        if small is not None:
            self.small_shapes = [a.shape for a in small]
            srcs.append(_pack_rows(small, D))
            lands.append(lax.empty((N_DEV,) + srcs[-1].shape, F32))
        started, self.token = _exchange_start(srcs, lands, _scatter_route(len(arrays)), _PEER_FLIPS, None,
                                              f"scatter{k}_start")
        self.started.append((started, len(arrays)))
        return self.token

    def collect(self, groups, after=None):
        x, y, c = _place()
        after, sums = self.token if after is None else after, []
        for k in groups:
            started, n_pieces = self.started[k]
            srcs, got = _exchange_wait(started, _scatter_route(n_pieces), _PEER_FLIPS, after, f"scatter{k}_wait")
            own = []
            for t, (a, g) in enumerate(zip(srcs, got)):
                if t >= n_pieces:
                    own.append(a)
                elif a.ndim == 4:
                    own.append(a[2 * x + y, c])
                else:
                    rows, cols = g.shape[1:]
                    own.append(lax.dynamic_slice(a, (c * rows, (2 * x + y) * cols), (rows, cols)))
            got = [lax.dynamic_update_index_in_dim(g, a, 4 * x + 2 * y + c, 0) for g, a in zip(got, own)]
            sums.append([_sum_slots(a, f"sum{k}_{i}") for i, a in enumerate(got)])
            after = sums[-1][0]
        return sums


def kernel(x, meta_tokens, norm_ffn1, ffn1_w_in, ffn1_w_out, norm_mix, norm_ffn2, ffn2_w_in, ffn2_w_out, ret_w_in, ret_head_norm, ret_w_out, gla_w_in, gla_w_gate, gla_b_gate, gla_head_norm, gla_w_out, final_norm, loss_target, m_meta_tokens, m_norm_ffn1, m_ffn1_w_in, m_ffn1_w_out, m_norm_mix, m_norm_ffn2, m_ffn2_w_in, m_ffn2_w_out, m_ret_w_in, m_ret_head_norm, m_ret_w_out, m_gla_w_in, m_gla_w_gate, m_gla_b_gate, m_gla_head_norm, m_gla_w_out, m_final_norm, v_meta_tokens, v_norm_ffn1, v_ffn1_w_in, v_ffn1_w_out, v_norm_mix, v_norm_ffn2, v_ffn2_w_in, v_ffn2_w_out, v_ret_w_in, v_ret_head_norm, v_ret_w_out, v_gla_w_in, v_gla_w_gate, v_gla_b_gate, v_gla_head_norm, v_gla_w_out, v_final_norm):
    p = dict(meta_tokens=meta_tokens, norm_ffn1=norm_ffn1, ffn1_w_in=ffn1_w_in, ffn1_w_out=ffn1_w_out, norm_mix=norm_mix,
             norm_ffn2=norm_ffn2, ffn2_w_in=ffn2_w_in, ffn2_w_out=ffn2_w_out, ret_w_in=ret_w_in,
             ret_head_norm=ret_head_norm, ret_w_out=ret_w_out, gla_w_in=gla_w_in, gla_w_gate=gla_w_gate,
             gla_b_gate=gla_b_gate, gla_head_norm=gla_head_norm, gla_w_out=gla_w_out, final_norm=final_norm)
    m = dict(zip(_WEIGHTS, (m_meta_tokens, m_norm_ffn1, m_ffn1_w_in, m_ffn1_w_out, m_norm_mix, m_norm_ffn2, m_ffn2_w_in,
                            m_ffn2_w_out, m_ret_w_in, m_ret_head_norm, m_ret_w_out, m_gla_w_in, m_gla_w_gate,
                            m_gla_b_gate, m_gla_head_norm, m_gla_w_out, m_final_norm)))
    v = dict(zip(_WEIGHTS, (v_meta_tokens, v_norm_ffn1, v_ffn1_w_in, v_ffn1_w_out, v_norm_mix, v_norm_ffn2, v_ffn2_w_in,
                            v_ffn2_w_out, v_ret_w_in, v_ret_head_norm, v_ret_w_out, v_gla_w_in, v_gla_w_gate,
                            v_gla_b_gate, v_gla_head_norm, v_gla_w_out, v_final_norm)))

    exchange = _GradExchange()
    d_x = _sequence_grads(x[0], loss_target[0], p, _WeightGather(p), exchange)
    names = [("ffn2_w_in", 1), ("ffn2_w_out", 1), ("gla_w_in", 0), ("gla_w_out", 0), ("ffn1_w_in", 1), ("ffn1_w_out", 1),
             ("ffn2_w_in", 0), ("ffn2_w_out", 0), ("ret_w_in", 0), ("ret_w_out", 0), ("ffn1_w_in", 0), ("ffn1_w_out", 0)]
    shard, grads, delta, new_m, new_v = {}, {}, {}, {}, {}

    def swap(sums, keys, name):
        for key, a in zip(keys, _swap_cores(sums, name)):
            shard[key] = a.reshape(-1, a.shape[-1])

    def update(name):
        layers = p[name].shape[0]
        grads[name] = jnp.stack([shard[name, layer] for layer in range(layers)])
        delta[name], new_m[name], new_v[name] = _adamw(p[name], grads[name], m[name], v[name], f"adamw_{name}")

    swap([a for group in exchange.collect(range(5)) for a in group], names[:10], "swap_first")
    for name in ("ffn2_w_in", "ffn2_w_out", "ret_w_in", "ret_w_out", "gla_w_in", "gla_w_out"):
        update(name)
    last, (small_sum,) = exchange.collect([5, 6], after=list(delta.values()))
    swap(last, names[10:], "swap_last")
    for name in ("ffn1_w_in", "ffn1_w_out"):
        update(name)

    chip = 2 * lax.axis_index("x") + lax.axis_index("y")
    cols = lambda a, n: lax.dynamic_slice_in_dim(a, chip * n, n, axis=a.ndim - 1)
    (s_meta, s_n1a, s_n1b, s_nma, s_nmb, s_n2a, s_n2b, s_final, s_ret_gain, s_wg, s_bg, s_gla_gain,
     s_loss) = _unpack_rows(small_sum, exchange.small_shapes)
    grads.update({
        "meta_tokens": cols(s_meta, 256), "norm_ffn1": jnp.concatenate([s_n1a, s_n1b]),
        "norm_mix": jnp.concatenate([s_nma, s_nmb]), "norm_ffn2": jnp.concatenate([s_n2a, s_n2b]),
        "final_norm": s_final.reshape(D),
        "ret_head_norm": cols(s_ret_gain.reshape(1, HEADS, RET_DV), RET_DV // N_CHIPS),
        "gla_w_gate": cols(s_wg, GLA_DK)[None], "gla_b_gate": cols(s_bg, GLA_DK),
        "gla_head_norm": cols(s_gla_gain.reshape(1, HEADS, GLA_DV), GLA_DV // N_CHIPS),
    })
    packed = [_pack_rows([d[name] for name in _LOCAL_SMALL], 128)[None] for d in (p, grads, m, v)]
    out = _adamw(*packed, "adamw_small")
    shapes = [p[name].shape for name in _LOCAL_SMALL]
    for d, a in zip((delta, new_m, new_v), out):
        d.update(zip(_LOCAL_SMALL, _unpack_rows(a, shapes)))

    return (s_loss.reshape(()), d_x[None], *[grads[n] for n in _WEIGHTS], *[delta[n] for n in _WEIGHTS],
            *[new_m[n] for n in _WEIGHTS], *[new_v[n] for n in _WEIGHTS])
```

```python
import functools

import jax
import jax.numpy as jnp
from jax import lax
from jax.experimental import pallas as pl
from jax.experimental.pallas import tpu as pltpu

F32, BF16 = jnp.float32, jnp.bfloat16
MESH = pl.DeviceIdType.MESH

D = 1024
N_META = 16
CHUNK = 64
RET_CHUNK = 256
FRONT = 256
D_FF = 2816
EPS = 1e-6
HEADS = 4
RET_DK, RET_DV = 256, 512
GLA_DK, GLA_DV = 128, 256
GLA_RANK = 16
GLA_TAU = 16.0
GLA_IN = 2 * HEADS * GLA_DK + 2 * HEADS * GLA_DV + GLA_RANK
GLA_U = 3328
ROPE_BASE = 10000.0
N_CHIPS = 4
N_DEV = 8

ADAM_LR, ADAM_B1, ADAM_B2, ADAM_EPS, ADAM_WD, ADAM_STEP = 0.001, 0.9, 0.999, 1e-08, 0.01, 10

VMEM_LIMIT_BYTES = 56 * 1024 * 1024
TM = 768
TM_SMALL = 256


TM_RESIDENT = 384
MXU_TILE = 256


def _cp(n_axes):
    return pltpu.CompilerParams(dimension_semantics=("arbitrary",) * n_axes, vmem_limit_bytes=VMEM_LIMIT_BYTES)


def _resident(shape, n_axes):
    zeros = (0,) * len(shape)
    index = (lambda i: zeros) if n_axes == 1 else (lambda i, j: zeros)
    return pl.BlockSpec(shape, index, pipeline_mode=pl.Buffered(1))


def _dg(a, b, ca, cb):
    nb = a.ndim - 2
    dims = (((ca + nb,), (cb + nb,)), (tuple(range(nb)), tuple(range(nb))))
    return lax.dot_general(a.astype(BF16), b.astype(BF16), dims, preferred_element_type=F32)


@jax.custom_vjp
def _nn(a, b):
    return _dg(a, b, 1, 0)


@jax.custom_vjp
def _nt(a, b):
    return _dg(a, b, 1, 1)


@jax.custom_vjp
def _tn(a, b):
    return _dg(a, b, 0, 0)


def _dot_vjp(fn, ca, cb, da, db):
    def fwd(a, b):
        a, b = a.astype(BF16), b.astype(BF16)
        return _dg(a, b, ca, cb), (a, b)

    def bwd(res, g):
        a, b = res
        g = g.astype(BF16)
        grad = lambda other, dims, g_first: _dg(g, other, *dims) if g_first else _dg(other, g, *dims)
        return grad(b, *da), grad(a, *db)

    fn.defvjp(fwd, bwd)


_dot_vjp(_nn, 1, 0, ((1, 1), True), ((0, 0), False))
_dot_vjp(_nt, 1, 1, ((1, 0), True), ((0, 0), True))
_dot_vjp(_tn, 0, 0, ((1, 1), False), ((1, 0), False))


def _split3_dot(m, a):
    a1 = a.astype(BF16)
    r1 = a - a1.astype(F32)
    a2 = r1.astype(BF16)
    a3 = (r1 - a2.astype(F32)).astype(BF16)
    mb = jnp.broadcast_to(m, a.shape[:-2] + m.shape)
    return _dg(mb, a1, 1, 0) + _dg(mb, a2, 1, 0) + _dg(mb, a3, 1, 0)


@jax.custom_vjp
def _cum(m, mt, a):
    return _split3_dot(m, a)


_cum.defvjp(lambda m, mt, a: (_split3_dot(m, a), (m, mt)),
            lambda res, g: (jnp.zeros_like(res[0]), jnp.zeros_like(res[1]), _split3_dot(res[1], g)))


def _sigmoid(x):
    return 1.0 / (1.0 + jnp.exp(-x))


def _rms(x):
    return lax.rsqrt(jnp.mean(x * x, axis=-1, keepdims=True) + EPS)


def _rmsnorm_bwd(dy, x, gain):
    r = _rms(x)
    xhat = x * r
    dxh = dy * gain
    return r * (dxh - xhat * jnp.mean(dxh * xhat, axis=-1, keepdims=True)), xhat


def _norm_proj(h, gain, w, name):
    tp, d = h.shape
    s, _, ns = w.shape

    tm = TM_RESIDENT

    def body(h_ref, g_ref, w_ref, hn_ref, u_ref):
        x = h_ref[...]
        a = (x * _rms(x) * g_ref[...]).astype(BF16)
        hn_ref[...] = a
        for k in range(s):
            u_ref[:, ns * k:ns * (k + 1)] = jnp.dot(a, w_ref[k], preferred_element_type=F32).astype(BF16)

    return pl.pallas_call(
        body, name=name, grid=(tp // tm,),
        in_specs=[pl.BlockSpec((tm, d), lambda i: (i, 0)), pl.BlockSpec((1, d), lambda i: (0, 0)), _resident(w.shape, 1)],
        out_specs=[pl.BlockSpec((tm, d), lambda i: (i, 0)), pl.BlockSpec((tm, s * ns), lambda i: (i, 0))],
        out_shape=[jax.ShapeDtypeStruct((tp, d), BF16), jax.ShapeDtypeStruct((tp, s * ns), BF16)],
        compiler_params=_cp(1))(h, gain, w)


def _norm_ffn_in(h, gain, w, name):
    tp, d = h.shape
    ff = w.shape[1] // 2
    tm = TM_RESIDENT
    blocks = [(c, min(c + 6 * MXU_TILE, ff)) for c in range(0, ff, 6 * MXU_TILE)]

    def body(h_ref, g_ref, w_ref, hn_ref, dg_ref, du_ref, act_ref):
        x = h_ref[...]
        a = (x * _rms(x) * g_ref[...]).astype(BF16)
        hn_ref[...] = a
        for c0, c1 in blocks:
            g = jnp.dot(a, w_ref[:, c0:c1], preferred_element_type=F32)
            u = jnp.dot(a, w_ref[:, ff + c0:ff + c1], preferred_element_type=F32)
            sg = _sigmoid(g)
            silu = g * sg
            dg_ref[:, c0:c1] = (u * (sg + silu * (1.0 - sg))).astype(BF16)
            du_ref[:, c0:c1] = silu.astype(BF16)
            act_ref[:, c0:c1] = (silu * u).astype(BF16)

    wide = jax.ShapeDtypeStruct((tp, ff), BF16)
    return pl.pallas_call(
        body, name=name, grid=(tp // tm,),
        in_specs=[pl.BlockSpec((tm, d), lambda i: (i, 0)), pl.BlockSpec((1, d), lambda i: (0, 0)),
                  _resident(w.shape, 1)],
        out_specs=[pl.BlockSpec((tm, d), lambda i: (i, 0))] + [pl.BlockSpec((tm, ff), lambda i: (i, 0))] * 3,
        out_shape=[jax.ShapeDtypeStruct((tp, d), BF16), wide, wide, wide],
        compiler_params=_cp(1))(h, gain, w)


def _out_proj(a, w, h, scale, name):
    tp, k = a.shape
    d = w.shape[1]

    def body(a_ref, w_ref, h_ref, o_ref):
        o_ref[...] = h_ref[...] + scale * jnp.dot(a_ref[...], w_ref[...], preferred_element_type=F32)

    return pl.pallas_call(
        body, name=name, grid=(tp // TM,),
        in_specs=[pl.BlockSpec((TM, k), lambda i: (i, 0)), pl.BlockSpec((k, d), lambda i: (0, 0)),
                  pl.BlockSpec((TM, d), lambda i: (i, 0))],
        out_specs=pl.BlockSpec((TM, d), lambda i: (i, 0)),
        out_shape=jax.ShapeDtypeStruct((tp, d), F32),
        compiler_params=_cp(1))(a, w, h)


def _dgrad(dh, w, name):
    tp, d = dh.shape
    k = w.shape[0]

    def body(dh_ref, w_ref, o_ref):
        o_ref[...] = lax.dot_general(dh_ref[...].astype(BF16), w_ref[...], (((1,), (1,)), ((), ())),
                                     preferred_element_type=F32).astype(BF16)

    return pl.pallas_call(
        body, name=name, grid=(tp // TM,),
        in_specs=[pl.BlockSpec((TM, d), lambda i: (i, 0)), pl.BlockSpec((k, d), lambda i: (0, 0))],
        out_specs=pl.BlockSpec((TM, k), lambda i: (i, 0)),
        out_shape=jax.ShapeDtypeStruct((tp, k), BF16),
        compiler_params=_cp(1))(dh, w)


def _wgrad(a, b, *, bm, bn, scale, sharded, name):
    tp, m = a.shape
    n = b.shape[1]
    nk = tp // TM

    def body(a_ref, b_ref, o_ref, acc_ref):
        k = pl.program_id(2)

        @pl.when(k == 0)
        def _():
            acc_ref[...] = jnp.zeros_like(acc_ref)

        bb = b_ref[...]
        if scale != 1.0:
            bb = scale * bb
        acc_ref[...] += lax.dot_general(a_ref[...], bb.astype(BF16), (((0,), (0,)), ((), ())),
                                        preferred_element_type=F32)

        @pl.when(k == nk - 1)
        def _():
            o_ref[...] = acc_ref[...].astype(BF16)

    if sharded:
        assert m == bm
        out_spec = pl.BlockSpec((None, bm, bn), lambda i, j, k: (j, 0, 0))
        out_shape = jax.ShapeDtypeStruct((n // bn, m, bn), BF16)
    else:
        out_spec = pl.BlockSpec((bm, bn), lambda i, j, k: (i, j))
        out_shape = jax.ShapeDtypeStruct((m, n), BF16)
    return pl.pallas_call(
        body, name=name, grid=(m // bm, n // bn, nk),
        in_specs=[pl.BlockSpec((TM, bm), lambda i, j, k: (k, i)), pl.BlockSpec((TM, bn), lambda i, j, k: (k, j))],
        out_specs=out_spec, out_shape=out_shape,
        scratch_shapes=[pltpu.VMEM((bm, bn), F32)],
        compiler_params=_cp(3))(a, b)


def _dgrad_norm(du, w, h, gain, dh_out, name):
    tp, d = h.shape
    s, _, ns = w.shape
    tm = TM_RESIDENT

    def body(du_ref, w_ref, h_ref, g_ref, dho_ref, dhi_ref, dg_ref):
        @pl.when(pl.program_id(0) == 0)
        def _():
            dg_ref[...] = jnp.zeros_like(dg_ref)

        dhn = None
        for k in range(s):
            part = lax.dot_general(du_ref[:, ns * k:ns * (k + 1)], w_ref[k], (((1,), (1,)), ((), ())),
                                   preferred_element_type=F32)
            dhn = part if dhn is None else dhn + part
        dx, xhat = _rmsnorm_bwd(dhn, h_ref[...], g_ref[...])
        dg_ref[...] += jnp.sum(dhn * xhat, axis=0, keepdims=True)
        dhi_ref[...] = dho_ref[...] + dx

    return pl.pallas_call(
        body, name=name, grid=(tp // tm,),
        in_specs=[pl.BlockSpec((tm, s * ns), lambda i: (i, 0)), _resident(w.shape, 1),
                  pl.BlockSpec((tm, d), lambda i: (i, 0)), pl.BlockSpec((1, d), lambda i: (0, 0)),
                  pl.BlockSpec((tm, d), lambda i: (i, 0))],
        out_specs=[pl.BlockSpec((tm, d), lambda i: (i, 0)), pl.BlockSpec((1, d), lambda i: (0, 0))],
        out_shape=[jax.ShapeDtypeStruct((tp, d), F32), jax.ShapeDtypeStruct((1, d), F32)],
        compiler_params=_cp(1))(du, w, h, gain, dh_out)


def _loss_head(h, gain, target, name):
    tp, d = h.shape
    tm = TM_SMALL
    front_tiles = FRONT // tm

    def body(h_ref, g_ref, t_ref, dh_ref, dg_ref, loss_ref):
        i = pl.program_id(0)

        @pl.when(i == 0)
        def _():
            dg_ref[...] = jnp.zeros_like(dg_ref)
            loss_ref[...] = jnp.zeros_like(loss_ref)

        x = h_ref[...]
        gain_v = g_ref[...]
        y = x * _rms(x) * gain_v
        err = jnp.where(i >= front_tiles, y - t_ref[...], 0.0)
        loss_ref[...] += 0.5 * jnp.sum(jnp.mean(err * err, axis=-1, keepdims=True), axis=0, keepdims=True)
        dy = err * (1.0 / d)
        dx, xhat = _rmsnorm_bwd(dy, x, gain_v)
        dg_ref[...] += jnp.sum(dy * xhat, axis=0, keepdims=True)
        dh_ref[...] = dx

    return pl.pallas_call(
        body, name=name, grid=(tp // tm,),
        in_specs=[pl.BlockSpec((tm, d), lambda i: (i, 0)), pl.BlockSpec((1, d), lambda i: (0, 0)),
                  pl.BlockSpec((tm, d), lambda i: (jnp.maximum(i - front_tiles, 0), 0))],
        out_specs=[pl.BlockSpec((tm, d), lambda i: (i, 0)), pl.BlockSpec((1, d), lambda i: (0, 0)),
                   pl.BlockSpec((1, 128), lambda i: (0, 0))],
        out_shape=[jax.ShapeDtypeStruct((tp, d), F32), jax.ShapeDtypeStruct((1, d), F32),
                   jax.ShapeDtypeStruct((1, 128), F32)],
        compiler_params=_cp(1))(h, gain, target)


def _gated_headnorm(o, g, gain):
    return o * _rms(o) * gain * (g * _sigmoid(g))


def _row_mask(chunk, size=CHUNK):
    rows = chunk * size + lax.broadcasted_iota(jnp.int32, (size, 1), 0)
    return (rows >= FRONT - N_META).astype(F32)


def _ret_head(q1, q2, k1, k2, v, g, state, gain, cos, sin, dmat, dq, dk, dc):
    q = jnp.concatenate([q1 * cos - q2 * sin, q1 * sin + q2 * cos], axis=-1)
    k = jnp.concatenate([k1 * cos - k2 * sin, k1 * sin + k2 * cos], axis=-1) * (RET_DK ** -0.5)
    scores = _nt(q, k) * dmat
    o = _nn(scores, v) + _nn(q * dq, state)
    new_state = state * dc + _tn(k * dk, v)
    return _gated_headnorm(o, g, gain), new_state


def _ret_consts():
    log_gamma = jnp.log1p(-2.0 ** (-5.0 - jnp.arange(HEADS, dtype=F32)))
    idx = jnp.arange(RET_CHUNK, dtype=F32)
    rel = idx[:, None] - idx[None, :]
    dmat = jnp.where(rel >= 0, jnp.exp(log_gamma[:, None, None] * jnp.maximum(rel, 0.0)), 0.0)
    dq = jnp.exp(log_gamma[:, None] * (idx + 1.0))[..., None]
    dk = jnp.exp(log_gamma[:, None] * (RET_CHUNK - 1.0 - idx))[..., None]
    dc = jnp.broadcast_to(jnp.exp(log_gamma * RET_CHUNK)[:, None, None], (HEADS, 1, 128))
    return dmat, dq, dk, dc


def _rope_tables(tp):
    half = RET_DK // 2
    inv = 1.0 / (ROPE_BASE ** jnp.linspace(0.0, 1.0, half, dtype=F32))
    pos = (jnp.arange(tp) - (FRONT - N_META)).astype(F32)
    ang = pos[:, None] * inv[None, :]
    return jnp.cos(ang), jnp.sin(ang)


_RET_V0, _RET_G0 = 2 * D, 4 * D


def _heads(ref, start, width, stride=None):
    stride = width if stride is None else stride
    return jnp.stack([ref[:, start + stride * h:start + stride * h + width].astype(F32) for h in range(HEADS)])


def _put_heads(ref, start, value, mask, stride=None):
    width = value.shape[-1]
    stride = width if stride is None else stride
    for h in range(HEADS):
        ref[:, start + stride * h:start + stride * h + width] = (value[h] * mask).astype(ref.dtype)


def _ret_pieces(u_ref):
    hk = RET_DK // 2
    return (_heads(u_ref, 0, hk, RET_DK), _heads(u_ref, hk, hk, RET_DK), _heads(u_ref, D, hk, RET_DK),
            _heads(u_ref, D + hk, hk, RET_DK), _heads(u_ref, _RET_V0, RET_DV), _heads(u_ref, _RET_G0, RET_DV))


def _ret_const_specs(rev=None):
    c = (lambda n: (rev(n), 0)) if rev else (lambda n: (n, 0))
    z3 = lambda n: (0, 0, 0)
    return [pl.BlockSpec((RET_CHUNK, RET_DK // 2), c), pl.BlockSpec((RET_CHUNK, RET_DK // 2), c),
            pl.BlockSpec((HEADS, RET_CHUNK, RET_CHUNK), z3), pl.BlockSpec((HEADS, RET_CHUNK, 1), z3),
            pl.BlockSpec((HEADS, RET_CHUNK, 1), z3), pl.BlockSpec((HEADS, 1, 128), z3)]


def _ret_fwd(u, gain, rope, h, w_out, name):
    tp = u.shape[0]
    nch = tp // RET_CHUNK
    cos, sin = rope
    dmat, dq, dk, dc = _ret_consts()

    def body(u_ref, gain_ref, h_ref, w_ref, cos_ref, sin_ref, dmat_ref, dq_ref, dk_ref, dc_ref,
             on_ref, st_ref, hmix_ref, state_ref):
        @pl.when(pl.program_id(0) == 0)
        def _():
            state_ref[...] = jnp.zeros_like(state_ref)

        state = state_ref[...]
        st_ref[...] = state.astype(BF16)
        on, new_state = _ret_head(*_ret_pieces(u_ref), state, _heads(gain_ref, 0, RET_DV), cos_ref[...], sin_ref[...],
                                  dmat_ref[...], dq_ref[...], dk_ref[...], dc_ref[...][:, :, :1])
        state_ref[...] = new_state
        _put_heads(on_ref, 0, on, 1.0)
        hmix_ref[...] = h_ref[...] + jnp.dot(on_ref[...], w_ref[...], preferred_element_type=F32)

    rows = lambda width: pl.BlockSpec((RET_CHUNK, width), lambda n: (n, 0))
    return pl.pallas_call(
        body, name=name, grid=(nch,),
        in_specs=[rows(6 * D), pl.BlockSpec((1, HEADS * RET_DV), lambda n: (0, 0)), rows(D),
                  _resident(w_out.shape, 1)] + _ret_const_specs(),
        out_specs=[rows(HEADS * RET_DV), pl.BlockSpec((None, HEADS, RET_DK, RET_DV), lambda n: (n, 0, 0, 0)), rows(D)],
        out_shape=[jax.ShapeDtypeStruct((tp, HEADS * RET_DV), BF16),
                   jax.ShapeDtypeStruct((nch, HEADS, RET_DK, RET_DV), BF16), jax.ShapeDtypeStruct((tp, D), F32)],
        scratch_shapes=[pltpu.VMEM((HEADS, RET_DK, RET_DV), F32)],
        compiler_params=_cp(1))(u, gain, h, w_out, cos, sin, dmat, dq, dk, dc)


def _ret_bwd(u, gain, rope, states, d_on, name):
    tp = u.shape[0]
    nch = tp // RET_CHUNK
    cos, sin = rope
    dmat, dq, dk, dc = _ret_consts()
    rev = lambda n: nch - 1 - n
    hk = RET_DK // 2

    def body(u_ref, gain_ref, st_ref, don_ref, cos_ref, sin_ref, dmat_ref, dq_ref, dk_ref, dc_ref,
             du_ref, dgain_ref, dstate_ref):
        @pl.when(pl.program_id(0) == 0)
        def _():
            dstate_ref[...] = jnp.zeros_like(dstate_ref)
            dgain_ref[...] = jnp.zeros_like(dgain_ref)

        mask = _row_mask(rev(pl.program_id(0)), RET_CHUNK)
        consts = (cos_ref[...], sin_ref[...], dmat_ref[...], dq_ref[...], dk_ref[...], dc_ref[...][:, :, :1])
        _, vjp = jax.vjp(lambda *a: _ret_head(*a, *consts), *_ret_pieces(u_ref), st_ref[...].astype(F32),
                         _heads(gain_ref, 0, RET_DV))
        dq1, dq2, dk1, dk2, dv, dg, dstate, dgain = vjp((_heads(don_ref, 0, RET_DV), dstate_ref[...]))
        dstate_ref[...] = dstate
        for hd in range(HEADS):
            dgain_ref[:, RET_DV * hd:RET_DV * (hd + 1)] += dgain[hd]
        _put_heads(du_ref, 0, dq1, mask, RET_DK)
        _put_heads(du_ref, hk, dq2, mask, RET_DK)
        _put_heads(du_ref, D, dk1, mask, RET_DK)
        _put_heads(du_ref, D + hk, dk2, mask, RET_DK)
        _put_heads(du_ref, _RET_V0, dv, mask)
        _put_heads(du_ref, _RET_G0, dg, mask)

    return pl.pallas_call(
        body, name=name, grid=(nch,),
        in_specs=[pl.BlockSpec((RET_CHUNK, 6 * D), lambda n: (rev(n), 0)),
                  pl.BlockSpec((1, HEADS * RET_DV), lambda n: (0, 0)),
                  pl.BlockSpec((None, HEADS, RET_DK, RET_DV), lambda n: (rev(n), 0, 0, 0)),
                  pl.BlockSpec((RET_CHUNK, HEADS * RET_DV), lambda n: (rev(n), 0))] + _ret_const_specs(rev),
        out_specs=[pl.BlockSpec((RET_CHUNK, 6 * D), lambda n: (rev(n), 0)),
                   pl.BlockSpec((1, HEADS * RET_DV), lambda n: (0, 0))],
        out_shape=[jax.ShapeDtypeStruct((tp, 6 * D), BF16), jax.ShapeDtypeStruct((1, HEADS * RET_DV), F32)],
        scratch_shapes=[pltpu.VMEM((HEADS, RET_DK, RET_DV), F32)],
        compiler_params=_cp(1))(u, gain, states, d_on, cos, sin, dmat, dq, dk, dc)


_GLA_K0, _GLA_V0, _GLA_G0, _GLA_Z0 = 512, 1024, 2048, 3072


def _gla_head(q, k, v, g, z, state_t, wg, bg, gain, mask, lo, lo_t, loc, loc_t):
    ga = _nn(jnp.broadcast_to(z, wg.shape[:-2] + z.shape), wg) + bg
    log_a = (jnp.minimum(ga, 0.0) - jnp.log(1.0 + jnp.exp(-jnp.abs(ga)))) * (mask * (1.0 / GLA_TAU))
    bcum = _cum(lo, lo_t, log_a)
    bmid = _cum(loc, loc_t, log_a)
    btot = jnp.sum(log_a, axis=-2, keepdims=True)
    qs = q * (GLA_DK ** -0.5)
    causal = lax.broadcasted_iota(jnp.int32, (CHUNK, CHUNK), 0) >= lax.broadcasted_iota(jnp.int32, (CHUNK, CHUNK), 1)
    scores = jnp.where(causal, _nt(qs * jnp.exp(bmid), k * jnp.exp(-bmid)), 0.0)
    o = _nn(scores, v) + _nt(qs * jnp.exp(bcum), state_t)
    new_state_t = state_t * jnp.exp(btot) + _tn(v, k * jnp.exp(btot - bcum))
    return _gated_headnorm(o, g, gain), new_state_t


def _cum_mats():
    r = lax.broadcasted_iota(jnp.int32, (CHUNK, CHUNK), 0)
    c = lax.broadcasted_iota(jnp.int32, (CHUNK, CHUNK), 1)
    mid = CHUNK // 2
    low = lambda a, b: (a >= b).astype(F32)
    lo, lo_t = low(r, c), low(c, r)
    loc = lo - (c <= mid).astype(F32)
    loc_t = lo_t - (r <= mid).astype(F32)
    return tuple(m.astype(BF16) for m in (lo, lo_t, loc, loc_t))


def _gla_pieces(u_ref):
    return (_heads(u_ref, 0, GLA_DK), _heads(u_ref, _GLA_K0, GLA_DK), _heads(u_ref, _GLA_V0, GLA_DV),
            _heads(u_ref, _GLA_G0, GLA_DV), u_ref[:, _GLA_Z0:_GLA_Z0 + 128].astype(F32))


def _gla_fwd(u, wg, bg, gain, name):
    tp = u.shape[0]
    nch = tp // CHUNK

    def body(u_ref, wg_ref, bg_ref, gain_ref, on_ref, st_ref, state_ref):
        @pl.when(pl.program_id(0) == 0)
        def _():
            state_ref[...] = jnp.zeros_like(state_ref)

        state = state_ref[...]
        st_ref[...] = state.astype(BF16)
        on, new_state = _gla_head(*_gla_pieces(u_ref), state, _heads(wg_ref, 0, GLA_DK), _heads(bg_ref, 0, GLA_DK),
                                  _heads(gain_ref, 0, GLA_DV), _row_mask(pl.program_id(0)), *_cum_mats())
        state_ref[...] = new_state
        _put_heads(on_ref, 0, on, 1.0)

    rows = lambda width: pl.BlockSpec((CHUNK, width), lambda n: (n, 0))
    full = lambda r, c: pl.BlockSpec((r, c), lambda n: (0, 0))
    return pl.pallas_call(
        body, name=name, grid=(nch,),
        in_specs=[rows(GLA_U), full(128, HEADS * GLA_DK), full(1, HEADS * GLA_DK), full(1, HEADS * GLA_DV)],
        out_specs=[rows(HEADS * GLA_DV), pl.BlockSpec((None, HEADS, GLA_DV, GLA_DK), lambda n: (n, 0, 0, 0))],
        out_shape=[jax.ShapeDtypeStruct((tp, HEADS * GLA_DV), BF16),
                   jax.ShapeDtypeStruct((nch, HEADS, GLA_DV, GLA_DK), BF16)],
        scratch_shapes=[pltpu.VMEM((HEADS, GLA_DV, GLA_DK), F32)],
        compiler_params=_cp(1))(u, wg, bg, gain)


def _gla_bwd(u, wg, bg, gain, states, d_on, name):
    tp = u.shape[0]
    nch = tp // CHUNK
    rev = lambda n: nch - 1 - n

    def body(u_ref, wg_ref, bg_ref, gain_ref, st_ref, don_ref, du_ref, dwg_ref, dbg_ref, dgain_ref, dstate_ref):
        @pl.when(pl.program_id(0) == 0)
        def _():
            dstate_ref[...] = jnp.zeros_like(dstate_ref)
            dwg_ref[...] = jnp.zeros_like(dwg_ref)
            dbg_ref[...] = jnp.zeros_like(dbg_ref)
            dgain_ref[...] = jnp.zeros_like(dgain_ref)

        mask = _row_mask(rev(pl.program_id(0)))
        mats = _cum_mats()
        _, vjp = jax.vjp(lambda *a: _gla_head(*a, mask, *mats), *_gla_pieces(u_ref), st_ref[...].astype(F32),
                         _heads(wg_ref, 0, GLA_DK), _heads(bg_ref, 0, GLA_DK), _heads(gain_ref, 0, GLA_DV))
        dq, dk, dv, dg, dz, dstate, dwg, dbg, dgain = vjp((_heads(don_ref, 0, GLA_DV), dstate_ref[...]))
        dstate_ref[...] = dstate
        for hd in range(HEADS):
            dwg_ref[:, GLA_DK * hd:GLA_DK * (hd + 1)] += dwg[hd]
            dbg_ref[:, GLA_DK * hd:GLA_DK * (hd + 1)] += dbg[hd]
            dgain_ref[:, GLA_DV * hd:GLA_DV * (hd + 1)] += dgain[hd]
        _put_heads(du_ref, 0, dq, mask)
        _put_heads(du_ref, _GLA_K0, dk, mask)
        _put_heads(du_ref, _GLA_V0, dv, mask)
        _put_heads(du_ref, _GLA_G0, dg, mask)
        du_ref[:, _GLA_Z0:_GLA_Z0 + 128] = dz.astype(BF16)
        du_ref[:, _GLA_Z0 + 128:] = jnp.zeros((CHUNK, GLA_U - _GLA_Z0 - 128), BF16)

    full = lambda r, c: pl.BlockSpec((r, c), lambda n: (0, 0))
    return pl.pallas_call(
        body, name=name, grid=(nch,),
        in_specs=[pl.BlockSpec((CHUNK, GLA_U), lambda n: (rev(n), 0)), full(128, HEADS * GLA_DK),
                  full(1, HEADS * GLA_DK), full(1, HEADS * GLA_DV),
                  pl.BlockSpec((None, HEADS, GLA_DV, GLA_DK), lambda n: (rev(n), 0, 0, 0)),
                  pl.BlockSpec((CHUNK, HEADS * GLA_DV), lambda n: (rev(n), 0))],
        out_specs=[pl.BlockSpec((CHUNK, GLA_U), lambda n: (rev(n), 0)), full(128, HEADS * GLA_DK),
                   full(1, HEADS * GLA_DK), full(1, HEADS * GLA_DV)],
        out_shape=[jax.ShapeDtypeStruct((tp, GLA_U), BF16), jax.ShapeDtypeStruct((128, HEADS * GLA_DK), F32),
                   jax.ShapeDtypeStruct((1, HEADS * GLA_DK), F32), jax.ShapeDtypeStruct((1, HEADS * GLA_DV), F32)],
        scratch_shapes=[pltpu.VMEM((HEADS, GLA_DV, GLA_DK), F32)],
        compiler_params=_cp(1))(u, wg, bg, gain, states, d_on)


def _ffn_fwd(h, gain, w_in, w_out, tag):
    hn, ug, uu, act = _norm_ffn_in(h, gain, w_in, f"{tag}_in")
    if callable(w_out):
        w_out = w_out(act)
    return _out_proj(act, w_out, h, 0.5, f"{tag}_out"), (h, hn, ug, uu, act), w_out


def _ffn_dgrad(dh, w_out, w_in, act_dg, act_du, h, gain, name, split_front=False):
    tp, d = dh.shape
    ff = w_out.shape[0]
    tm = TM_SMALL
    nt = (((1,), (1,)), ((), ()))

    def body(dh_ref, wo_ref, wi_ref, dg_ref, du_ref, h_ref, g_ref, o_ref, *out_refs):
        dhi_ref, dgain_ref = out_refs[-2:]

        @pl.when(pl.program_id(0) == 0)
        def _():
            dgain_ref[...] = jnp.zeros_like(dgain_ref)

        dho = dh_ref[...]
        dact = lax.dot_general((0.5 * dho).astype(BF16), wo_ref[...], nt, preferred_element_type=F32)
        d_gate = (dact * dg_ref[...].astype(F32)).astype(BF16)
        d_up = (dact * du_ref[...].astype(F32)).astype(BF16)
        o_ref[:, :ff] = d_gate
        o_ref[:, ff:] = d_up
        dhn = (lax.dot_general(d_gate, wi_ref[:, :ff], nt, preferred_element_type=F32)
               + lax.dot_general(d_up, wi_ref[:, ff:], nt, preferred_element_type=F32))
        dx, xhat = _rmsnorm_bwd(dhn, h_ref[...], g_ref[...])
        dgain_ref[...] += jnp.sum(dhn * xhat, axis=0, keepdims=True)
        dhi_ref[...] = dho + dx
        if split_front:
            @pl.when(pl.program_id(0) == 0)
            def _():
                out_refs[0][...] = dho + dx

    rows = lambda width: pl.BlockSpec((tm, width), lambda i: (i, 0))
    if split_front:
        assert tm == FRONT
        dhi_specs = [pl.BlockSpec((tm, d), lambda i: (0, 0)), pl.BlockSpec((tm, d), lambda i: (jnp.maximum(i - 1, 0), 0))]
        dhi_shapes = [jax.ShapeDtypeStruct((FRONT, d), F32), jax.ShapeDtypeStruct((tp - FRONT, d), F32)]
    else:
        dhi_specs, dhi_shapes = [rows(d)], [jax.ShapeDtypeStruct((tp, d), F32)]
    out = pl.pallas_call(
        body, name=name, grid=(tp // tm,),
        in_specs=[rows(d), _resident(w_out.shape, 1), _resident(w_in.shape, 1), rows(ff), rows(ff), rows(d),
                  pl.BlockSpec((1, d), lambda i: (0, 0))],
        out_specs=[rows(2 * ff), *dhi_specs, pl.BlockSpec((1, d), lambda i: (0, 0))],
        out_shape=[jax.ShapeDtypeStruct((tp, 2 * ff), BF16), *dhi_shapes, jax.ShapeDtypeStruct((1, d), F32)],
        compiler_params=_cp(1))(dh, w_out, w_in, act_dg, act_du, h, gain)
    return (out[0], tuple(out[1:3]), out[3]) if split_front else tuple(out)


def _ffn_bwd(dh, saved, gain, w_in, w_out, tag, push, split_front=False):
    h, hn, act_dg, act_du, act = saved
    du, dh_in, d_gain = _ffn_dgrad(dh, w_out, w_in, act_dg, act_du, h, gain, f"{tag}_dgrad", split_front)
    d_w_out = _wgrad(act, dh, bm=D_FF // 2, bn=D, scale=0.5, sharded=False, name=f"{tag}_dwout")
    d_w_in = _wgrad(hn, du, bm=D, bn=D_FF, scale=1.0, sharded=False, name=f"{tag}_dwin")
    return dh_in, d_gain, push([("cols", d_w_in), d_w_out])


def _sequence_grads(x, target, p, weights, grads):
    row = lambda v, token: v.reshape(1, -1) + token[0, 0]
    gains = {}

    tok = weights.start(1, weights.start(0, None))
    weights.pin = tok
    h = jnp.concatenate([jnp.zeros((FRONT, D), F32), x], axis=0) + tok[0, 0]
    rope = _rope_tables(h.shape[0])
    w = weights.wait(0, [tok, h, *rope, *weights.later_shards(2)])
    tok = weights.start(2, w["l0_ffn1_in"])
    h = lax.dynamic_update_slice(h, w["meta"], (FRONT - N_META, 0))
    gains["l0_ffn1"] = row(p["norm_ffn1"][0], tok)
    h, s1, w["l0_ffn1_out"] = _ffn_fwd(h, gains["l0_ffn1"], w["l0_ffn1_in"],
                                       lambda act: weights.wait(1, act)["l0_ffn1_out"], "l0_ffn1")
    w.update(weights.wait(2, h))
    tok = weights.start(4, weights.start(3, w["ret_in"]))
    gains["ret"] = row(p["norm_mix"][0], tok)
    hn, u = _norm_proj(h, gains["ret"], w["ret_in"], "ret_in")
    w.update(weights.wait(3, u))
    on, states, h_mix = _ret_fwd(u, w["ret_gain"], rope, h, w["ret_out"], "ret_fwd")
    s2 = (h, hn, u, on, states)
    w.update(weights.wait(4, h_mix))
    tok = weights.start(5, w["l0_ffn2_in"])
    gains["l0_ffn2"] = row(p["norm_ffn2"][0], tok)
    h, s3, _ = _ffn_fwd(h_mix, gains["l0_ffn2"], w["l0_ffn2_in"], w["l0_ffn2_out"], "l0_ffn2")
    saved = [(s1, s2, s3)]

    w.update(weights.wait(5, h))
    tok = weights.start(6, w["l1_ffn1_in"])
    gains["l1_ffn1"] = row(p["norm_ffn1"][1], tok)
    h, s1, _ = _ffn_fwd(h, gains["l1_ffn1"], w["l1_ffn1_in"], w["l1_ffn1_out"], "l1_ffn1")
    w.update(weights.wait(6, h))
    tok = weights.start(7, w["gla_out"])
    gains["gla"] = row(p["norm_mix"][1], tok)
    hn, u = _norm_proj(h, gains["gla"], w["gla_in"], "gla_in")
    on, states = _gla_fwd(u, w["gla_wg"], w["gla_bg"], w["gla_gain"], "gla_fwd")
    h_mix = _out_proj(on, w["gla_out"], h, 1.0, "gla_out")
    s2 = (h, hn, u, on, states)
    w.update(weights.wait(7, h_mix))
    gains["l1_ffn2"] = p["norm_ffn2"][1].reshape(1, -1)
    h, s3, _ = _ffn_fwd(h_mix, gains["l1_ffn2"], w["l1_ffn2_in"], w["l1_ffn2_out"], "l1_ffn2")
    saved.append((s1, s2, s3))

    dh, d_final, loss = _loss_head(h, p["final_norm"].reshape(1, -1), target, "loss_head")
    small = {"final_norm": d_final, "norm_ffn1": [None, None], "norm_mix": [None, None], "norm_ffn2": [None, None]}
    pusher = lambda k: functools.partial(grads.push, k)

    s1, s2, s3 = saved[1]
    dh, small["norm_ffn2"][1], tok = _ffn_bwd(dh, s3, gains["l1_ffn2"], w["l1_ffn2_in"], w["l1_ffn2_out"], "l1_ffn2",
                                              pusher(0))
    h_in, hn, u, on, states = s2
    d_on = _dgrad(dh, w["gla_out"], "gla_don")
    d_out = _wgrad(on, dh, bm=D, bn=D, scale=1.0, sharded=False, name="gla_dwout")
    du, small["gla_wg"], small["gla_bg"], small["gla_gain"] = _gla_bwd(
        u, w["gla_wg"], w["gla_bg"], w["gla_gain"] + tok[0, 0], states, d_on, "gla_bwd")
    d_in = _wgrad(hn, du, bm=D, bn=GLA_U, scale=1.0, sharded=False, name="gla_dwin")
    d_in = jnp.moveaxis(d_in[:, :GLA_IN].reshape(D, N_CHIPS, -1), 1, 0)
    tok = grads.push(1, [d_in, d_out])
    dh, small["norm_mix"][1] = _dgrad_norm(du, w["gla_in"], h_in, gains["gla"] + tok[0, 0], dh, "gla_dnorm")
    dh, small["norm_ffn1"][1], tok = _ffn_bwd(dh, s1, gains["l1_ffn1"], w["l1_ffn1_in"], w["l1_ffn1_out"], "l1_ffn1",
                                              pusher(2))

    s1, s2, s3 = saved[0]
    dh, small["norm_ffn2"][0], tok = _ffn_bwd(dh, s3, gains["l0_ffn2"] + tok[0, 0], w["l0_ffn2_in"],
                                              w["l0_ffn2_out"], "l0_ffn2", pusher(3))
    h_in, hn, u, on, states = s2
    d_on = _dgrad(dh, w["ret_out"], "ret_don")
    d_out = _wgrad(on, dh, bm=D, bn=D, scale=1.0, sharded=False, name="ret_dwout")
    du, small["ret_gain"] = _ret_bwd(u, w["ret_gain"] + tok[0, 0], rope, states, d_on, "ret_bwd")
    d_in = _wgrad(hn, du, bm=D, bn=w["ret_in"].shape[2], scale=1.0, sharded=True, name="ret_dwin")
    tok = grads.push(4, [d_in, d_out])
    dh, small["norm_mix"][0] = _dgrad_norm(du, w["ret_in"], h_in, gains["ret"] + tok[0, 0], dh, "ret_dnorm")
    (d_front, d_x), small["norm_ffn1"][0], tok = _ffn_bwd(dh, s1, gains["l0_ffn1"], w["l0_ffn1_in"], w["l0_ffn1_out"],
                                                          "l0_ffn1", pusher(5), split_front=True)
    grads.push(6, [], [d_front[FRONT - N_META:], *small["norm_ffn1"], *small["norm_mix"], *small["norm_ffn2"],
                       small["final_norm"], small["ret_gain"], small["gla_wg"][:GLA_RANK], small["gla_bg"],
                       small["gla_gain"], loss[:, :1] + tok[0, 0]])
    return d_x


_HBM = pl.BlockSpec(memory_space=pl.ANY)


def _place():
    return lax.axis_index("x"), lax.axis_index("y"), lax.axis_index("c")


def _flip(v, bit):
    return 1 - v if bit else v


DMA_CHUNK_BYTES = 128 * 1024


def _row_chunks(ref):
    rows, cols = ref.shape
    step = _row_tile(rows, max(16, DMA_CHUNK_BYTES // (cols * ref.dtype.itemsize)))
    return [pl.ds(a, step) for a in range(0, rows, step)]


def _whole(src, dst, send_sem, recv_sem, peer):
    return pltpu.make_async_remote_copy(src_ref=src, dst_ref=dst, send_sem=send_sem, recv_sem=recv_sem,
                                        device_id=peer, device_id_type=MESH)


def _send(src, dst, send_sem, recv_sem, peer):
    for rows in _row_chunks(src):
        _whole(src.at[rows], dst.at[rows], send_sem, recv_sem, peer).start()
    return _whole(src, dst, send_sem, recv_sem, peer)


_HBM_ONLY = pl.BlockSpec(memory_space=pltpu.HBM)
_SEMS = pl.BlockSpec(memory_space=pltpu.SEMAPHORE)
_SIDE_EFFECT = pltpu.CompilerParams(has_side_effects=pltpu.SideEffectType.DATAFLOW_SIDE_EFFECTING)
_GATHER_FLIPS = [(1, 0, 0), (0, 1, 0), (1, 1, 0), (0, 0, 1)]
_PEER_FLIPS = [(fx, fy, fc) for fx in (0, 1) for fy in (0, 1) for fc in (0, 1)][1:]


def _zero_token():
    return jnp.zeros((8, 128), F32)


def _exchange_start(srcs, lands, route, flips, after, name):
    n = len(srcs)

    def body(*refs):
        src, land = refs[:n], refs[n:2 * n]
        send_sems, recv_sems, token = refs[2 * n + 1], refs[2 * n + 2], refs[-1]
        me = _place()
        for t in range(n):
            for j, flip in enumerate(flips):
                peer = tuple(_flip(v, f) for v, f in zip(me, flip))
                s, d = route(t, src[t], land[t], me, peer)
                _send(s, d, send_sems.at[t * len(flips) + j], recv_sems.at[t * len(flips) + j], peer)
        token[...] = jnp.zeros_like(token)

    hbm = lambda a: pltpu.HBM(a.shape, a.dtype)
    sems = pltpu.SemaphoreType.DMA((n * len(flips),))
    operands = [pltpu.with_memory_space_constraint(a, pltpu.HBM) for a in list(srcs) + list(lands)]
    out = pl.pallas_call(
        body, name=name, in_specs=[_HBM_ONLY] * (2 * n) + [_HBM],
        out_shape=(sems, sems, *[hbm(a) for a in operands], jax.ShapeDtypeStruct((8, 128), F32)),
        out_specs=(_SEMS, _SEMS, *[_HBM_ONLY] * (2 * n), pl.BlockSpec(memory_space=pltpu.VMEM)),
        input_output_aliases={i: 2 + i for i in range(2 * n)}, compiler_params=_SIDE_EFFECT,
    )(*operands, _zero_token() if after is None else after)
    return (out[0], out[1], out[2:2 + n], out[2 + n:2 + 2 * n]), out[-1]


def _exchange_wait(started, route, flips, after, name):
    send_sems, recv_sems, srcs, lands = started
    n = len(srcs)

    def body(*refs):
        src, land = refs[:n], refs[n:2 * n]
        send_sems, recv_sems = refs[2 * n], refs[2 * n + 1]
        me = _place()
        for t in range(n):
            for j, flip in enumerate(flips):
                peer = tuple(_flip(v, f) for v, f in zip(me, flip))
                s, d = route(t, src[t], land[t], me, peer)
                cp = _whole(s, d, send_sems.at[t * len(flips) + j], recv_sems.at[t * len(flips) + j], peer)
                cp.wait_send()
                cp.wait_recv()

    hbm = lambda a: pltpu.HBM(a.shape, a.dtype)
    after = list(after) if isinstance(after, (list, tuple)) else [after]
    out = pl.pallas_call(
        body, name=name, in_specs=[_HBM_ONLY] * (2 * n) + [_SEMS, _SEMS] + [_HBM] * len(after),
        out_shape=tuple(hbm(a) for a in list(srcs) + list(lands)), out_specs=tuple([_HBM_ONLY] * (2 * n)),
        input_output_aliases={i: i for i in range(2 * n)}, compiler_params=_SIDE_EFFECT,
    )(*srcs, *lands, send_sems, recv_sems, *after)
    return out[:n], out[n:]


def _gather_route(t, src, land, me, peer):
    mine = 2 * me[0] + me[1]
    if land.ndim == 3:
        return src, land.at[mine]
    cols = src.shape[1]
    return src, land.at[:, pl.ds(pl.multiple_of(mine * cols, 128), cols)]


def _scatter_route(n_pieces):
    def route(t, src, land, me, peer):
        chip = 2 * peer[0] + peer[1]
        if t >= n_pieces:
            part = src
        elif src.ndim == 4:
            part = src.at[chip, peer[2]]
        else:
            rows, cols = land.shape[1:]
            part = src.at[pl.ds(pl.multiple_of(peer[2] * rows, 16), rows), pl.ds(pl.multiple_of(chip * cols, 128), cols)]
        return part, land.at[4 * me[0] + 2 * me[1] + me[2]]

    return route


def _swap_cores(halves, name):
    n = len(halves)

    def body(*refs):
        src, dst = refs[:n], refs[n:2 * n]
        send_sems, recv_sems = refs[2 * n:]
        x, y, c = _place()
        copies = [_send(src[t], dst[t], send_sems.at[t], recv_sems.at[t], (x, y, 1 - c)) for t in range(n)]
        for cp in copies:
            cp.wait()

    got = pl.pallas_call(
        body, name=name, in_specs=[_HBM] * n, out_specs=[_HBM] * n,
        out_shape=[jax.ShapeDtypeStruct(a.shape, a.dtype) for a in halves],
        scratch_shapes=[pltpu.SemaphoreType.DMA((n,)), pltpu.SemaphoreType.DMA((n,))],
    )(*halves)
    south = lax.axis_index("c") == 0
    return [jnp.stack([jnp.where(south, a, b), jnp.where(south, b, a)]) for a, b in zip(halves, got)]


def _row_tile(rows, cap):
    fits = [t for t in range(16, cap + 1, 16) if rows % t == 0]
    return fits[-1] if fits else rows


def _sum_slots(a, name):
    _, r, c = a.shape
    tr = _row_tile(r, 384)

    def body(a_ref, o_ref):
        s = a_ref[0].astype(F32)
        for k in range(1, N_DEV):
            s = s + a_ref[k].astype(F32)
        o_ref[...] = s

    return pl.pallas_call(
        body, name=name, grid=(r // tr,),
        in_specs=[pl.BlockSpec((N_DEV, tr, c), lambda i: (0, i, 0))],
        out_specs=pl.BlockSpec((tr, c), lambda i: (i, 0)),
        out_shape=jax.ShapeDtypeStruct((r, c), F32),
        compiler_params=_cp(1))(a)


def _adamw(w, g, m, v, name):
    layers, r, c = w.shape
    tr = _row_tile(r, 256)

    def body(w_ref, g_ref, m_ref, v_ref, d_ref, nm_ref, nv_ref):
        gv = g_ref[...]
        nm = ADAM_B1 * m_ref[...] + (1.0 - ADAM_B1) * gv
        nv = ADAM_B2 * v_ref[...] + (1.0 - ADAM_B2) * (gv * gv)
        m_hat = nm / (1.0 - ADAM_B1 ** ADAM_STEP)
        v_hat = nv / (1.0 - ADAM_B2 ** ADAM_STEP)
        d_ref[...] = -ADAM_LR * (m_hat / (jnp.sqrt(v_hat) + ADAM_EPS) + ADAM_WD * w_ref[...])
        nm_ref[...] = nm
        nv_ref[...] = nv

    spec = pl.BlockSpec((None, tr, c), lambda a, i: (a, i, 0))
    return pl.pallas_call(
        body, name=name, grid=(layers, r // tr), in_specs=[spec] * 4, out_specs=[spec] * 3,
        out_shape=[jax.ShapeDtypeStruct((layers, r, c), F32)] * 3,
        compiler_params=_cp(2))(*[pltpu.with_memory_space_constraint(a, pltpu.HBM) for a in (w, g, m, v)])


_SMALL = ["meta_tokens", "ret_head_norm", "gla_w_gate", "gla_b_gate", "gla_head_norm"]
_LOCAL_SMALL = ["meta_tokens", "norm_ffn1", "norm_mix", "norm_ffn2", "ret_head_norm", "gla_w_gate", "gla_b_gate",
                "gla_head_norm", "final_norm"]
_WEIGHTS = ["meta_tokens", "norm_ffn1", "ffn1_w_in", "ffn1_w_out", "norm_mix", "norm_ffn2", "ffn2_w_in", "ffn2_w_out",
            "ret_w_in", "ret_head_norm", "ret_w_out", "gla_w_in", "gla_w_gate", "gla_b_gate", "gla_head_norm",
            "gla_w_out", "final_norm"]


def _pack_rows(arrays, width):
    flat = jnp.concatenate([a.reshape(-1) for a in arrays])
    pad = -flat.shape[0] % (8 * width)
    return jnp.pad(flat, (0, pad)).reshape(-1, width)


def _unpack_rows(packed, shapes):
    flat, out, at = packed.reshape(-1), [], 0
    for s in shapes:
        size = 1
        for dim in s:
            size *= dim
        out.append(flat[at:at + size].reshape(s))
        at += size
    return out


class _WeightGather:
    GROUPS = [("small", "l0_ffn1_in"), ("l0_ffn1_out",), ("ret_in",), ("ret_out",), ("l0_ffn2_in", "l0_ffn2_out"),
              ("l1_ffn1_in", "l1_ffn1_out"), ("gla_in", "gla_out"), ("l1_ffn2_in", "l1_ffn2_out")]

    def __init__(self, p):
        self.small_shapes = [p[name].shape for name in _SMALL]
        self.f32 = {"small": _pack_rows([p[name] for name in _SMALL], 128), "ret_in": p["ret_w_in"][0],
                    "ret_out": p["ret_w_out"][0], "gla_in": p["gla_w_in"][0], "gla_out": p["gla_w_out"][0]}
        for layer in range(2):
            for name in ("ffn1", "ffn2"):
                self.f32[f"l{layer}_{name}_in"] = p[f"{name}_w_in"][layer]
                self.f32[f"l{layer}_{name}_out"] = p[f"{name}_w_out"][layer]
        self.shards = {}
        self.started = {}
        self.pin = None

    def shard(self, name):
        if name not in self.shards:
            a = self.f32[name]
            if name != "small":
                a = (a if self.pin is None else a + self.pin[0, 0]).astype(BF16)
            self.shards[name] = a
        return self.shards[name]

    def later_shards(self, k):
        return [self.shard(name) for group in self.GROUPS[k:] for name in group]

    def start(self, k, after):
        shards = [self.shard(name) for name in self.GROUPS[k]]
        lands = []
        for name, s in zip(self.GROUPS[k], shards):
            if "ffn" in name and name.endswith("_in"):
                lands.append(lax.empty((s.shape[0], N_CHIPS * s.shape[1]), s.dtype))
            else:
                lands.append(lax.empty((N_CHIPS,) + s.shape, s.dtype))
        self.started[k], token = _exchange_start(shards, lands, _gather_route, _GATHER_FLIPS, after, f"gather{k}_start")
        return token

    def wait(self, k, after):
        _, got = _exchange_wait(self.started[k], _gather_route, _GATHER_FLIPS, after, f"gather{k}_wait")
        w = {}
        for name, g in zip(self.GROUPS[k], got):
            if name == "small":
                parts = zip(*[_unpack_rows(g[chip], self.small_shapes) for chip in range(N_CHIPS)])
                cat = lambda a: jnp.moveaxis(a, 0, -2).reshape(a.shape[1:-1] + (-1,))
                meta, ret_gain, wg, bg, gla_gain = [cat(jnp.stack(part)) for part in parts]
                w.update(meta=meta, ret_gain=ret_gain.reshape(1, -1), gla_bg=bg.reshape(1, -1),
                         gla_gain=gla_gain.reshape(1, -1),
                         gla_wg=jnp.pad(wg[0], ((0, 128 - GLA_RANK), (0, 0))).astype(BF16))
            elif name == "gla_in":
                full = jnp.moveaxis(g, 0, 1).reshape(D, -1)
                w[name] = jnp.pad(full, ((0, 0), (0, GLA_U - GLA_IN)))[None]
            elif name.endswith("_out"):
                w[name] = g.reshape(-1, g.shape[-1])
            else:
                w[name] = g
        return w


class _GradExchange:
    def __init__(self):
        self.started = []
        self.token = None
        self.small_shapes = None

    def push(self, k, arrays, small=None):
        srcs, lands = [], []
        for a in arrays:
            if isinstance(a, tuple):
                a = a[1]
                piece = (a.shape[0] // 2, a.shape[1] // N_CHIPS)
            else:
                a = a.reshape(N_CHIPS, 2, -1, a.shape[-1])
                piece = a.shape[2:]
            srcs.append(a)
            lands.append(lax.empty((N_DEV,) + piece, a.dtype))
        if small is not None:
            self.small_shapes = [a.shape for a in small]
            srcs.append(_pack_rows(small, D))
            lands.append(lax.empty((N_DEV,) + srcs[-1].shape, F32))
        started, self.token = _exchange_start(srcs, lands, _scatter_route(len(arrays)), _PEER_FLIPS, None,
                                              f"scatter{k}_start")
        self.started.append((started, len(arrays)))
        return self.token

    def collect(self, groups, after=None):
        x, y, c = _place()
        after, sums = self.token if after is None else after, []
        for k in groups:
            started, n_pieces = self.started[k]
            srcs, got = _exchange_wait(started, _scatter_route(n_pieces), _PEER_FLIPS, after, f"scatter{k}_wait")
            own = []
            for t, (a, g) in enumerate(zip(srcs, got)):
                if t >= n_pieces:
                    own.append(a)
                elif a.ndim == 4:
                    own.append(a[2 * x + y, c])
                else:
                    rows, cols = g.shape[1:]
                    own.append(lax.dynamic_slice(a, (c * rows, (2 * x + y) * cols), (rows, cols)))
            got = [lax.dynamic_update_index_in_dim(g, a, 4 * x + 2 * y + c, 0) for g, a in zip(got, own)]
            sums.append([_sum_slots(a, f"sum{k}_{i}") for i, a in enumerate(got)])
            after = sums[-1][0]
        return sums


def kernel(x, meta_tokens, norm_ffn1, ffn1_w_in, ffn1_w_out, norm_mix, norm_ffn2, ffn2_w_in, ffn2_w_out, ret_w_in, ret_head_norm, ret_w_out, gla_w_in, gla_w_gate, gla_b_gate, gla_head_norm, gla_w_out, final_norm, loss_target, m_meta_tokens, m_norm_ffn1, m_ffn1_w_in, m_ffn1_w_out, m_norm_mix, m_norm_ffn2, m_ffn2_w_in, m_ffn2_w_out, m_ret_w_in, m_ret_head_norm, m_ret_w_out, m_gla_w_in, m_gla_w_gate, m_gla_b_gate, m_gla_head_norm, m_gla_w_out, m_final_norm, v_meta_tokens, v_norm_ffn1, v_ffn1_w_in, v_ffn1_w_out, v_norm_mix, v_norm_ffn2, v_ffn2_w_in, v_ffn2_w_out, v_ret_w_in, v_ret_head_norm, v_ret_w_out, v_gla_w_in, v_gla_w_gate, v_gla_b_gate, v_gla_head_norm, v_gla_w_out, v_final_norm):
    p = dict(meta_tokens=meta_tokens, norm_ffn1=norm_ffn1, ffn1_w_in=ffn1_w_in, ffn1_w_out=ffn1_w_out, norm_mix=norm_mix,
             norm_ffn2=norm_ffn2, ffn2_w_in=ffn2_w_in, ffn2_w_out=ffn2_w_out, ret_w_in=ret_w_in,
             ret_head_norm=ret_head_norm, ret_w_out=ret_w_out, gla_w_in=gla_w_in, gla_w_gate=gla_w_gate,
             gla_b_gate=gla_b_gate, gla_head_norm=gla_head_norm, gla_w_out=gla_w_out, final_norm=final_norm)
    m = dict(zip(_WEIGHTS, (m_meta_tokens, m_norm_ffn1, m_ffn1_w_in, m_ffn1_w_out, m_norm_mix, m_norm_ffn2, m_ffn2_w_in,
                            m_ffn2_w_out, m_ret_w_in, m_ret_head_norm, m_ret_w_out, m_gla_w_in, m_gla_w_gate,
                            m_gla_b_gate, m_gla_head_norm, m_gla_w_out, m_final_norm)))
    v = dict(zip(_WEIGHTS, (v_meta_tokens, v_norm_ffn1, v_ffn1_w_in, v_ffn1_w_out, v_norm_mix, v_norm_ffn2, v_ffn2_w_in,
                            v_ffn2_w_out, v_ret_w_in, v_ret_head_norm, v_ret_w_out, v_gla_w_in, v_gla_w_gate,
                            v_gla_b_gate, v_gla_head_norm, v_gla_w_out, v_final_norm)))

    exchange = _GradExchange()
    d_x = _sequence_grads(x[0], loss_target[0], p, _WeightGather(p), exchange)
    names = [("ffn2_w_in", 1), ("ffn2_w_out", 1), ("gla_w_in", 0), ("gla_w_out", 0), ("ffn1_w_in", 1), ("ffn1_w_out", 1),
             ("ffn2_w_in", 0), ("ffn2_w_out", 0), ("ret_w_in", 0), ("ret_w_out", 0), ("ffn1_w_in", 0), ("ffn1_w_out", 0)]
    shard, grads, delta, new_m, new_v = {}, {}, {}, {}, {}

    def swap(sums, keys, name):
        for key, a in zip(keys, _swap_cores(sums, name)):
            shard[key] = a.reshape(-1, a.shape[-1])

    def update(name):
        layers = p[name].shape[0]
        grads[name] = jnp.stack([shard[name, layer] for layer in range(layers)])
        delta[name], new_m[name], new_v[name] = _adamw(p[name], grads[name], m[name], v[name], f"adamw_{name}")

    swap([a for group in exchange.collect(range(5)) for a in group], names[:10], "swap_first")
    for name in ("ffn2_w_in", "ffn2_w_out", "ret_w_in", "ret_w_out", "gla_w_in", "gla_w_out"):
        update(name)
    last, (small_sum,) = exchange.collect([5, 6], after=list(delta.values()))
    swap(last, names[10:], "swap_last")
    for name in ("ffn1_w_in", "ffn1_w_out"):
        update(name)

    chip = 2 * lax.axis_index("x") + lax.axis_index("y")
    cols = lambda a, n: lax.dynamic_slice_in_dim(a, chip * n, n, axis=a.ndim - 1)
    (s_meta, s_n1a, s_n1b, s_nma, s_nmb, s_n2a, s_n2b, s_final, s_ret_gain, s_wg, s_bg, s_gla_gain,
     s_loss) = _unpack_rows(small_sum, exchange.small_shapes)
    grads.update({
        "meta_tokens": cols(s_meta, 256), "norm_ffn1": jnp.concatenate([s_n1a, s_n1b]),
        "norm_mix": jnp.concatenate([s_nma, s_nmb]), "norm_ffn2": jnp.concatenate([s_n2a, s_n2b]),
        "final_norm": s_final.reshape(D),
        "ret_head_norm": cols(s_ret_gain.reshape(1, HEADS, RET_DV), RET_DV // N_CHIPS),
        "gla_w_gate": cols(s_wg, GLA_DK)[None], "gla_b_gate": cols(s_bg, GLA_DK),
        "gla_head_norm": cols(s_gla_gain.reshape(1, HEADS, GLA_DV), GLA_DV // N_CHIPS),
    })
    for name in _LOCAL_SMALL:
        shape = p[name].shape
        as3d = lambda a: a.reshape((1,) * (3 - len(shape)) + shape)
        out = _adamw(as3d(p[name]), as3d(grads[name]), as3d(m[name]), as3d(v[name]), f"adamw_{name}")
        delta[name], new_m[name], new_v[name] = [a.reshape(shape) for a in out]

    return (s_loss.reshape(()), d_x[None], *[grads[n] for n in _WEIGHTS], *[delta[n] for n in _WEIGHTS],
            *[new_m[n] for n in _WEIGHTS], *[new_v[n] for n in _WEIGHTS])
```

```python
import functools

import jax
import jax.numpy as jnp
from jax import lax
from jax.experimental import pallas as pl
from jax.experimental.pallas import tpu as pltpu

F32, BF16 = jnp.float32, jnp.bfloat16
MESH = pl.DeviceIdType.MESH

D = 1024
N_META = 16
CHUNK = 64
RET_CHUNK = 256
FRONT = 256
D_FF = 2816
EPS = 1e-6
HEADS = 4
RET_DK, RET_DV = 256, 512
GLA_DK, GLA_DV = 128, 256
GLA_RANK = 16
GLA_TAU = 16.0
GLA_IN = 2 * HEADS * GLA_DK + 2 * HEADS * GLA_DV + GLA_RANK
GLA_U = 3328
ROPE_BASE = 10000.0
N_CHIPS = 4
N_DEV = 8

ADAM_LR, ADAM_B1, ADAM_B2, ADAM_EPS, ADAM_WD, ADAM_STEP = 0.001, 0.9, 0.999, 1e-08, 0.01, 10

VMEM_LIMIT_BYTES = 56 * 1024 * 1024
TM = 768
TM_SMALL = 256


TM_RESIDENT = 384
MXU_TILE = 256


def _cp(n_axes):
    return pltpu.CompilerParams(dimension_semantics=("arbitrary",) * n_axes, vmem_limit_bytes=VMEM_LIMIT_BYTES)


def _resident(shape, n_axes):
    zeros = (0,) * len(shape)
    index = (lambda i: zeros) if n_axes == 1 else (lambda i, j: zeros)
    return pl.BlockSpec(shape, index, pipeline_mode=pl.Buffered(1))


def _dg(a, b, ca, cb):
    nb = a.ndim - 2
    dims = (((ca + nb,), (cb + nb,)), (tuple(range(nb)), tuple(range(nb))))
    return lax.dot_general(a.astype(BF16), b.astype(BF16), dims, preferred_element_type=F32)


@jax.custom_vjp
def _nn(a, b):
    return _dg(a, b, 1, 0)


@jax.custom_vjp
def _nt(a, b):
    return _dg(a, b, 1, 1)


@jax.custom_vjp
def _tn(a, b):
    return _dg(a, b, 0, 0)


def _dot_vjp(fn, ca, cb, da, db):
    def fwd(a, b):
        a, b = a.astype(BF16), b.astype(BF16)
        return _dg(a, b, ca, cb), (a, b)

    def bwd(res, g):
        a, b = res
        g = g.astype(BF16)
        grad = lambda other, dims, g_first: _dg(g, other, *dims) if g_first else _dg(other, g, *dims)
        return grad(b, *da), grad(a, *db)

    fn.defvjp(fwd, bwd)


_dot_vjp(_nn, 1, 0, ((1, 1), True), ((0, 0), False))
_dot_vjp(_nt, 1, 1, ((1, 0), True), ((0, 0), True))
_dot_vjp(_tn, 0, 0, ((1, 1), False), ((1, 0), False))


def _split3_dot(m, a):
    a1 = a.astype(BF16)
    r1 = a - a1.astype(F32)
    a2 = r1.astype(BF16)
    a3 = (r1 - a2.astype(F32)).astype(BF16)
    mb = jnp.broadcast_to(m, a.shape[:-2] + m.shape)
    return _dg(mb, a1, 1, 0) + _dg(mb, a2, 1, 0) + _dg(mb, a3, 1, 0)


@jax.custom_vjp
def _cum(m, mt, a):
    return _split3_dot(m, a)


_cum.defvjp(lambda m, mt, a: (_split3_dot(m, a), (m, mt)),
            lambda res, g: (jnp.zeros_like(res[0]), jnp.zeros_like(res[1]), _split3_dot(res[1], g)))


def _sigmoid(x):
    return 1.0 / (1.0 + jnp.exp(-x))


def _rms(x):
    return lax.rsqrt(jnp.mean(x * x, axis=-1, keepdims=True) + EPS)


def _rmsnorm_bwd(dy, x, gain):
    r = _rms(x)
    xhat = x * r
    dxh = dy * gain
    return r * (dxh - xhat * jnp.mean(dxh * xhat, axis=-1, keepdims=True)), xhat


def _norm_proj(h, gain, w, name):
    tp, d = h.shape
    s, _, ns = w.shape

    tm = TM_RESIDENT

    def body(h_ref, g_ref, w_ref, hn_ref, u_ref):
        x = h_ref[...]
        a = (x * _rms(x) * g_ref[...]).astype(BF16)
        hn_ref[...] = a
        for k in range(s):
            u_ref[:, ns * k:ns * (k + 1)] = jnp.dot(a, w_ref[k], preferred_element_type=F32).astype(BF16)

    return pl.pallas_call(
        body, name=name, grid=(tp // tm,),
        in_specs=[pl.BlockSpec((tm, d), lambda i: (i, 0)), pl.BlockSpec((1, d), lambda i: (0, 0)), _resident(w.shape, 1)],
        out_specs=[pl.BlockSpec((tm, d), lambda i: (i, 0)), pl.BlockSpec((tm, s * ns), lambda i: (i, 0))],
        out_shape=[jax.ShapeDtypeStruct((tp, d), BF16), jax.ShapeDtypeStruct((tp, s * ns), BF16)],
        compiler_params=_cp(1))(h, gain, w)


def _norm_ffn_in(h, gain, w, name):
    tp, d = h.shape
    ff = w.shape[1] // 2
    tm = TM_RESIDENT
    blocks = [(c, min(c + 6 * MXU_TILE, ff)) for c in range(0, ff, 6 * MXU_TILE)]

    def body(h_ref, g_ref, w_ref, hn_ref, dg_ref, du_ref, act_ref):
        x = h_ref[...]
        a = (x * _rms(x) * g_ref[...]).astype(BF16)
        hn_ref[...] = a
        for c0, c1 in blocks:
            g = jnp.dot(a, w_ref[:, c0:c1], preferred_element_type=F32)
            u = jnp.dot(a, w_ref[:, ff + c0:ff + c1], preferred_element_type=F32)
            sg = _sigmoid(g)
            silu = g * sg
            dg_ref[:, c0:c1] = (u * (sg + silu * (1.0 - sg))).astype(BF16)
            du_ref[:, c0:c1] = silu.astype(BF16)
            act_ref[:, c0:c1] = (silu * u).astype(BF16)

    wide = jax.ShapeDtypeStruct((tp, ff), BF16)
    return pl.pallas_call(
        body, name=name, grid=(tp // tm,),
        in_specs=[pl.BlockSpec((tm, d), lambda i: (i, 0)), pl.BlockSpec((1, d), lambda i: (0, 0)),
                  _resident(w.shape, 1)],
        out_specs=[pl.BlockSpec((tm, d), lambda i: (i, 0))] + [pl.BlockSpec((tm, ff), lambda i: (i, 0))] * 3,
        out_shape=[jax.ShapeDtypeStruct((tp, d), BF16), wide, wide, wide],
        compiler_params=_cp(1))(h, gain, w)


def _out_proj(a, w, h, scale, name):
    tp, k = a.shape
    d = w.shape[1]

    def body(a_ref, w_ref, h_ref, o_ref):
        o_ref[...] = h_ref[...] + scale * jnp.dot(a_ref[...], w_ref[...], preferred_element_type=F32)

    return pl.pallas_call(
        body, name=name, grid=(tp // TM,),
        in_specs=[pl.BlockSpec((TM, k), lambda i: (i, 0)), pl.BlockSpec((k, d), lambda i: (0, 0)),
                  pl.BlockSpec((TM, d), lambda i: (i, 0))],
        out_specs=pl.BlockSpec((TM, d), lambda i: (i, 0)),
        out_shape=jax.ShapeDtypeStruct((tp, d), F32),
        compiler_params=_cp(1))(a, w, h)


def _dgrad(dh, w, name):
    tp, d = dh.shape
    k = w.shape[0]

    def body(dh_ref, w_ref, o_ref):
        o_ref[...] = lax.dot_general(dh_ref[...].astype(BF16), w_ref[...], (((1,), (1,)), ((), ())),
                                     preferred_element_type=F32).astype(BF16)

    return pl.pallas_call(
        body, name=name, grid=(tp // TM,),
        in_specs=[pl.BlockSpec((TM, d), lambda i: (i, 0)), pl.BlockSpec((k, d), lambda i: (0, 0))],
        out_specs=pl.BlockSpec((TM, k), lambda i: (i, 0)),
        out_shape=jax.ShapeDtypeStruct((tp, k), BF16),
        compiler_params=_cp(1))(dh, w)


def _wgrad(a, b, *, bm, bn, scale, sharded, name):
    tp, m = a.shape
    n = b.shape[1]
    nk = tp // TM

    def body(a_ref, b_ref, o_ref, acc_ref):
        k = pl.program_id(2)

        @pl.when(k == 0)
        def _():
            acc_ref[...] = jnp.zeros_like(acc_ref)

        bb = b_ref[...]
        if scale != 1.0:
            bb = scale * bb
        acc_ref[...] += lax.dot_general(a_ref[...], bb.astype(BF16), (((0,), (0,)), ((), ())),
                                        preferred_element_type=F32)

        @pl.when(k == nk - 1)
        def _():
            o_ref[...] = acc_ref[...].astype(BF16)

    if sharded:
        assert m == bm
        out_spec = pl.BlockSpec((None, bm, bn), lambda i, j, k: (j, 0, 0))
        out_shape = jax.ShapeDtypeStruct((n // bn, m, bn), BF16)
    else:
        out_spec = pl.BlockSpec((bm, bn), lambda i, j, k: (i, j))
        out_shape = jax.ShapeDtypeStruct((m, n), BF16)
    return pl.pallas_call(
        body, name=name, grid=(m // bm, n // bn, nk),
        in_specs=[pl.BlockSpec((TM, bm), lambda i, j, k: (k, i)), pl.BlockSpec((TM, bn), lambda i, j, k: (k, j))],
        out_specs=out_spec, out_shape=out_shape,
        scratch_shapes=[pltpu.VMEM((bm, bn), F32)],
        compiler_params=_cp(3))(a, b)


def _dgrad_norm(du, w, h, gain, dh_out, name):
    tp, d = h.shape
    s, _, ns = w.shape
    tm = TM_RESIDENT

    def body(du_ref, w_ref, h_ref, g_ref, dho_ref, dhi_ref, dg_ref):
        @pl.when(pl.program_id(0) == 0)
        def _():
            dg_ref[...] = jnp.zeros_like(dg_ref)

        dhn = None
        for k in range(s):
            part = lax.dot_general(du_ref[:, ns * k:ns * (k + 1)], w_ref[k], (((1,), (1,)), ((), ())),
                                   preferred_element_type=F32)
            dhn = part if dhn is None else dhn + part
        dx, xhat = _rmsnorm_bwd(dhn, h_ref[...], g_ref[...])
        dg_ref[...] += jnp.sum(dhn * xhat, axis=0, keepdims=True)
        dhi_ref[...] = dho_ref[...] + dx

    return pl.pallas_call(
        body, name=name, grid=(tp // tm,),
        in_specs=[pl.BlockSpec((tm, s * ns), lambda i: (i, 0)), _resident(w.shape, 1),
                  pl.BlockSpec((tm, d), lambda i: (i, 0)), pl.BlockSpec((1, d), lambda i: (0, 0)),
                  pl.BlockSpec((tm, d), lambda i: (i, 0))],
        out_specs=[pl.BlockSpec((tm, d), lambda i: (i, 0)), pl.BlockSpec((1, d), lambda i: (0, 0))],
        out_shape=[jax.ShapeDtypeStruct((tp, d), F32), jax.ShapeDtypeStruct((1, d), F32)],
        compiler_params=_cp(1))(du, w, h, gain, dh_out)


def _loss_head(h, gain, target, name):
    tp, d = h.shape
    tm = TM_SMALL
    front_tiles = FRONT // tm

    def body(h_ref, g_ref, t_ref, dh_ref, dg_ref, loss_ref):
        i = pl.program_id(0)

        @pl.when(i == 0)
        def _():
            dg_ref[...] = jnp.zeros_like(dg_ref)
            loss_ref[...] = jnp.zeros_like(loss_ref)

        x = h_ref[...]
        gain_v = g_ref[...]
        y = x * _rms(x) * gain_v
        err = jnp.where(i >= front_tiles, y - t_ref[...], 0.0)
        loss_ref[...] += 0.5 * jnp.sum(jnp.mean(err * err, axis=-1, keepdims=True), axis=0, keepdims=True)
        dy = err * (1.0 / d)
        dx, xhat = _rmsnorm_bwd(dy, x, gain_v)
        dg_ref[...] += jnp.sum(dy * xhat, axis=0, keepdims=True)
        dh_ref[...] = dx

    return pl.pallas_call(
        body, name=name, grid=(tp // tm,),
        in_specs=[pl.BlockSpec((tm, d), lambda i: (i, 0)), pl.BlockSpec((1, d), lambda i: (0, 0)),
                  pl.BlockSpec((tm, d), lambda i: (jnp.maximum(i - front_tiles, 0), 0))],
        out_specs=[pl.BlockSpec((tm, d), lambda i: (i, 0)), pl.BlockSpec((1, d), lambda i: (0, 0)),
                   pl.BlockSpec((1, 128), lambda i: (0, 0))],
        out_shape=[jax.ShapeDtypeStruct((tp, d), F32), jax.ShapeDtypeStruct((1, d), F32),
                   jax.ShapeDtypeStruct((1, 128), F32)],
        compiler_params=_cp(1))(h, gain, target)


def _gated_headnorm(o, g, gain):
    return o * _rms(o) * gain * (g * _sigmoid(g))


def _row_mask(chunk, size=CHUNK):
    rows = chunk * size + lax.broadcasted_iota(jnp.int32, (size, 1), 0)
    return (rows >= FRONT - N_META).astype(F32)


def _ret_head(q1, q2, k1, k2, v, g, state, gain, cos, sin, dmat, dq, dk, dc):
    q = jnp.concatenate([q1 * cos - q2 * sin, q1 * sin + q2 * cos], axis=-1)
    k = jnp.concatenate([k1 * cos - k2 * sin, k1 * sin + k2 * cos], axis=-1) * (RET_DK ** -0.5)
    scores = _nt(q, k) * dmat
    o = _nn(scores, v) + _nn(q * dq, state)
    new_state = state * dc + _tn(k * dk, v)
    return _gated_headnorm(o, g, gain), new_state


def _ret_consts():
    log_gamma = jnp.log1p(-2.0 ** (-5.0 - jnp.arange(HEADS, dtype=F32)))
    idx = jnp.arange(RET_CHUNK, dtype=F32)
    rel = idx[:, None] - idx[None, :]
    dmat = jnp.where(rel >= 0, jnp.exp(log_gamma[:, None, None] * jnp.maximum(rel, 0.0)), 0.0)
    dq = jnp.exp(log_gamma[:, None] * (idx + 1.0))[..., None]
    dk = jnp.exp(log_gamma[:, None] * (RET_CHUNK - 1.0 - idx))[..., None]
    dc = jnp.broadcast_to(jnp.exp(log_gamma * RET_CHUNK)[:, None, None], (HEADS, 1, 128))
    return dmat, dq, dk, dc


def _rope_tables(tp):
    half = RET_DK // 2
    inv = 1.0 / (ROPE_BASE ** jnp.linspace(0.0, 1.0, half, dtype=F32))
    pos = (jnp.arange(tp) - (FRONT - N_META)).astype(F32)
    ang = pos[:, None] * inv[None, :]
    return jnp.cos(ang), jnp.sin(ang)


_RET_V0, _RET_G0 = 2 * D, 4 * D


def _heads(ref, start, width, stride=None, rows=slice(None)):
    stride = width if stride is None else stride
    return jnp.stack([ref[rows, start + stride * h:start + stride * h + width].astype(F32) for h in range(HEADS)])


def _put_heads(ref, start, value, mask, stride=None, rows=slice(None)):
    width = value.shape[-1]
    stride = width if stride is None else stride
    for h in range(HEADS):
        ref[rows, start + stride * h:start + stride * h + width] = (value[h] * mask).astype(ref.dtype)


def _ret_pieces(u_ref):
    hk = RET_DK // 2
    return (_heads(u_ref, 0, hk, RET_DK), _heads(u_ref, hk, hk, RET_DK), _heads(u_ref, D, hk, RET_DK),
            _heads(u_ref, D + hk, hk, RET_DK), _heads(u_ref, _RET_V0, RET_DV), _heads(u_ref, _RET_G0, RET_DV))


def _ret_const_specs(rev=None):
    c = (lambda n: (rev(n), 0)) if rev else (lambda n: (n, 0))
    z3 = lambda n: (0, 0, 0)
    return [pl.BlockSpec((RET_CHUNK, RET_DK // 2), c), pl.BlockSpec((RET_CHUNK, RET_DK // 2), c),
            pl.BlockSpec((HEADS, RET_CHUNK, RET_CHUNK), z3), pl.BlockSpec((HEADS, RET_CHUNK, 1), z3),
            pl.BlockSpec((HEADS, RET_CHUNK, 1), z3), pl.BlockSpec((HEADS, 1, 128), z3)]


def _ret_fwd(u, gain, rope, h, w_out, name):
    tp = u.shape[0]
    nch = tp // RET_CHUNK
    cos, sin = rope
    dmat, dq, dk, dc = _ret_consts()

    def body(u_ref, gain_ref, h_ref, w_ref, cos_ref, sin_ref, dmat_ref, dq_ref, dk_ref, dc_ref,
             on_ref, st_ref, hmix_ref, state_ref):
        @pl.when(pl.program_id(0) == 0)
        def _():
            state_ref[...] = jnp.zeros_like(state_ref)

        state = state_ref[...]
        st_ref[...] = state.astype(BF16)
        on, new_state = _ret_head(*_ret_pieces(u_ref), state, _heads(gain_ref, 0, RET_DV), cos_ref[...], sin_ref[...],
                                  dmat_ref[...], dq_ref[...], dk_ref[...], dc_ref[...][:, :, :1])
        state_ref[...] = new_state
        _put_heads(on_ref, 0, on, 1.0)
        hmix_ref[...] = h_ref[...] + jnp.dot(on_ref[...], w_ref[...], preferred_element_type=F32)

    rows = lambda width: pl.BlockSpec((RET_CHUNK, width), lambda n: (n, 0))
    return pl.pallas_call(
        body, name=name, grid=(nch,),
        in_specs=[rows(6 * D), pl.BlockSpec((1, HEADS * RET_DV), lambda n: (0, 0)), rows(D),
                  _resident(w_out.shape, 1)] + _ret_const_specs(),
        out_specs=[rows(HEADS * RET_DV), pl.BlockSpec((None, HEADS, RET_DK, RET_DV), lambda n: (n, 0, 0, 0)), rows(D)],
        out_shape=[jax.ShapeDtypeStruct((tp, HEADS * RET_DV), BF16),
                   jax.ShapeDtypeStruct((nch, HEADS, RET_DK, RET_DV), BF16), jax.ShapeDtypeStruct((tp, D), F32)],
        scratch_shapes=[pltpu.VMEM((HEADS, RET_DK, RET_DV), F32)],
        compiler_params=_cp(1))(u, gain, h, w_out, cos, sin, dmat, dq, dk, dc)


def _ret_bwd(u, gain, rope, states, d_on, name):
    tp = u.shape[0]
    nch = tp // RET_CHUNK
    cos, sin = rope
    dmat, dq, dk, dc = _ret_consts()
    rev = lambda n: nch - 1 - n
    hk = RET_DK // 2

    def body(u_ref, gain_ref, st_ref, don_ref, cos_ref, sin_ref, dmat_ref, dq_ref, dk_ref, dc_ref,
             du_ref, dgain_ref, dstate_ref):
        @pl.when(pl.program_id(0) == 0)
        def _():
            dstate_ref[...] = jnp.zeros_like(dstate_ref)
            dgain_ref[...] = jnp.zeros_like(dgain_ref)

        mask = _row_mask(rev(pl.program_id(0)), RET_CHUNK)
        consts = (cos_ref[...], sin_ref[...], dmat_ref[...], dq_ref[...], dk_ref[...], dc_ref[...][:, :, :1])
        _, vjp = jax.vjp(lambda *a: _ret_head(*a, *consts), *_ret_pieces(u_ref), st_ref[...].astype(F32),
                         _heads(gain_ref, 0, RET_DV))
        dq1, dq2, dk1, dk2, dv, dg, dstate, dgain = vjp((_heads(don_ref, 0, RET_DV), dstate_ref[...]))
        dstate_ref[...] = dstate
        for hd in range(HEADS):
            dgain_ref[:, RET_DV * hd:RET_DV * (hd + 1)] += dgain[hd]
        _put_heads(du_ref, 0, dq1, mask, RET_DK)
        _put_heads(du_ref, hk, dq2, mask, RET_DK)
        _put_heads(du_ref, D, dk1, mask, RET_DK)
        _put_heads(du_ref, D + hk, dk2, mask, RET_DK)
        _put_heads(du_ref, _RET_V0, dv, mask)
        _put_heads(du_ref, _RET_G0, dg, mask)

    return pl.pallas_call(
        body, name=name, grid=(nch,),
        in_specs=[pl.BlockSpec((RET_CHUNK, 6 * D), lambda n: (rev(n), 0)),
                  pl.BlockSpec((1, HEADS * RET_DV), lambda n: (0, 0)),
                  pl.BlockSpec((None, HEADS, RET_DK, RET_DV), lambda n: (rev(n), 0, 0, 0)),
                  pl.BlockSpec((RET_CHUNK, HEADS * RET_DV), lambda n: (rev(n), 0))] + _ret_const_specs(rev),
        out_specs=[pl.BlockSpec((RET_CHUNK, 6 * D), lambda n: (rev(n), 0)),
                   pl.BlockSpec((1, HEADS * RET_DV), lambda n: (0, 0))],
        out_shape=[jax.ShapeDtypeStruct((tp, 6 * D), BF16), jax.ShapeDtypeStruct((1, HEADS * RET_DV), F32)],
        scratch_shapes=[pltpu.VMEM((HEADS, RET_DK, RET_DV), F32)],
        compiler_params=_cp(1))(u, gain, states, d_on, cos, sin, dmat, dq, dk, dc)


_GLA_K0, _GLA_V0, _GLA_G0, _GLA_Z0 = 512, 1024, 2048, 3072


def _gla_head(q, k, v, g, z, state_t, wg, bg, gain, mask, lo, lo_t, loc, loc_t):
    ga = _nn(jnp.broadcast_to(z, wg.shape[:-2] + z.shape), wg) + bg
    log_a = (jnp.minimum(ga, 0.0) - jnp.log(1.0 + jnp.exp(-jnp.abs(ga)))) * (mask * (1.0 / GLA_TAU))
    bcum = _cum(lo, lo_t, log_a)
    bmid = _cum(loc, loc_t, log_a)
    btot = jnp.sum(log_a, axis=-2, keepdims=True)
    qs = q * (GLA_DK ** -0.5)
    causal = lax.broadcasted_iota(jnp.int32, (CHUNK, CHUNK), 0) >= lax.broadcasted_iota(jnp.int32, (CHUNK, CHUNK), 1)
    scores = jnp.where(causal, _nt(qs * jnp.exp(bmid), k * jnp.exp(-bmid)), 0.0)
    o = _nn(scores, v) + _nt(qs * jnp.exp(bcum), state_t)
    new_state_t = state_t * jnp.exp(btot) + _tn(v, k * jnp.exp(btot - bcum))
    return _gated_headnorm(o, g, gain), new_state_t


def _cum_mats():
    r = lax.broadcasted_iota(jnp.int32, (CHUNK, CHUNK), 0)
    c = lax.broadcasted_iota(jnp.int32, (CHUNK, CHUNK), 1)
    mid = CHUNK // 2
    low = lambda a, b: (a >= b).astype(F32)
    lo, lo_t = low(r, c), low(c, r)
    loc = lo - (c <= mid).astype(F32)
    loc_t = lo_t - (r <= mid).astype(F32)
    return tuple(m.astype(BF16) for m in (lo, lo_t, loc, loc_t))


GLA_STEP_CHUNKS = 4


def _gla_pieces(u_ref, rows):
    return (_heads(u_ref, 0, GLA_DK, rows=rows), _heads(u_ref, _GLA_K0, GLA_DK, rows=rows),
            _heads(u_ref, _GLA_V0, GLA_DV, rows=rows), _heads(u_ref, _GLA_G0, GLA_DV, rows=rows),
            u_ref[rows, _GLA_Z0:_GLA_Z0 + 128].astype(F32))


def _gla_fwd(u, wg, bg, gain, name):
    tp = u.shape[0]
    nch = tp // CHUNK
    per = GLA_STEP_CHUNKS

    def body(u_ref, wg_ref, bg_ref, gain_ref, on_ref, st_ref, state_ref):
        @pl.when(pl.program_id(0) == 0)
        def _():
            state_ref[...] = jnp.zeros_like(state_ref)

        params = (_heads(wg_ref, 0, GLA_DK), _heads(bg_ref, 0, GLA_DK), _heads(gain_ref, 0, GLA_DV))
        mats = _cum_mats()
        state = state_ref[...]
        for c in range(per):
            rows = slice(CHUNK * c, CHUNK * (c + 1))
            st_ref[c] = state.astype(BF16)
            on, state = _gla_head(*_gla_pieces(u_ref, rows), state, *params, _row_mask(pl.program_id(0) * per + c), *mats)
            _put_heads(on_ref, 0, on, 1.0, rows=rows)
        state_ref[...] = state

    rows_spec = lambda width: pl.BlockSpec((per * CHUNK, width), lambda n: (n, 0))
    full = lambda r, c: pl.BlockSpec((r, c), lambda n: (0, 0))
    return pl.pallas_call(
        body, name=name, grid=(nch // per,),
        in_specs=[rows_spec(GLA_U), full(128, HEADS * GLA_DK), full(1, HEADS * GLA_DK), full(1, HEADS * GLA_DV)],
        out_specs=[rows_spec(HEADS * GLA_DV), pl.BlockSpec((per, HEADS, GLA_DV, GLA_DK), lambda n: (n, 0, 0, 0))],
        out_shape=[jax.ShapeDtypeStruct((tp, HEADS * GLA_DV), BF16),
                   jax.ShapeDtypeStruct((nch, HEADS, GLA_DV, GLA_DK), BF16)],
        scratch_shapes=[pltpu.VMEM((HEADS, GLA_DV, GLA_DK), F32)],
        compiler_params=_cp(1))(u, wg, bg, gain)


def _gla_bwd(u, wg, bg, gain, states, d_on, name):
    tp = u.shape[0]
    per = GLA_STEP_CHUNKS
    steps = tp // (per * CHUNK)
    rev = lambda n: steps - 1 - n

    def body(u_ref, wg_ref, bg_ref, gain_ref, st_ref, don_ref, du_ref, dwg_ref, dbg_ref, dgain_ref, dstate_ref):
        @pl.when(pl.program_id(0) == 0)
        def _():
            dstate_ref[...] = jnp.zeros_like(dstate_ref)
            dwg_ref[...] = jnp.zeros_like(dwg_ref)
            dbg_ref[...] = jnp.zeros_like(dbg_ref)
            dgain_ref[...] = jnp.zeros_like(dgain_ref)

        params = (_heads(wg_ref, 0, GLA_DK), _heads(bg_ref, 0, GLA_DK), _heads(gain_ref, 0, GLA_DV))
        mats = _cum_mats()
        dstate = dstate_ref[...]
        for c in reversed(range(per)):
            rows = slice(CHUNK * c, CHUNK * (c + 1))
            mask = _row_mask(rev(pl.program_id(0)) * per + c)
            _, vjp = jax.vjp(lambda *a: _gla_head(*a, mask, *mats), *_gla_pieces(u_ref, rows),
                             st_ref[c].astype(F32), *params)
            dq, dk, dv, dg, dz, dstate, dwg, dbg, dgain = vjp((_heads(don_ref, 0, GLA_DV, rows=rows), dstate))
            for hd in range(HEADS):
                dwg_ref[:, GLA_DK * hd:GLA_DK * (hd + 1)] += dwg[hd]
                dbg_ref[:, GLA_DK * hd:GLA_DK * (hd + 1)] += dbg[hd]
                dgain_ref[:, GLA_DV * hd:GLA_DV * (hd + 1)] += dgain[hd]
            _put_heads(du_ref, 0, dq, mask, rows=rows)
            _put_heads(du_ref, _GLA_K0, dk, mask, rows=rows)
            _put_heads(du_ref, _GLA_V0, dv, mask, rows=rows)
            _put_heads(du_ref, _GLA_G0, dg, mask, rows=rows)
            du_ref[rows, _GLA_Z0:_GLA_Z0 + 128] = dz.astype(BF16)
            du_ref[rows, _GLA_Z0 + 128:] = jnp.zeros((CHUNK, GLA_U - _GLA_Z0 - 128), BF16)
        dstate_ref[...] = dstate

    full = lambda r, c: pl.BlockSpec((r, c), lambda n: (0, 0))
    return pl.pallas_call(
        body, name=name, grid=(steps,),
        in_specs=[pl.BlockSpec((per * CHUNK, GLA_U), lambda n: (rev(n), 0)), full(128, HEADS * GLA_DK),
                  full(1, HEADS * GLA_DK), full(1, HEADS * GLA_DV),
                  pl.BlockSpec((per, HEADS, GLA_DV, GLA_DK), lambda n: (rev(n), 0, 0, 0)),
                  pl.BlockSpec((per * CHUNK, HEADS * GLA_DV), lambda n: (rev(n), 0))],
        out_specs=[pl.BlockSpec((per * CHUNK, GLA_U), lambda n: (rev(n), 0)), full(128, HEADS * GLA_DK),
                   full(1, HEADS * GLA_DK), full(1, HEADS * GLA_DV)],
        out_shape=[jax.ShapeDtypeStruct((tp, GLA_U), BF16), jax.ShapeDtypeStruct((128, HEADS * GLA_DK), F32),
                   jax.ShapeDtypeStruct((1, HEADS * GLA_DK), F32), jax.ShapeDtypeStruct((1, HEADS * GLA_DV), F32)],
        scratch_shapes=[pltpu.VMEM((HEADS, GLA_DV, GLA_DK), F32)],
        compiler_params=_cp(1))(u, wg, bg, gain, states, d_on)


def _ffn_fwd(h, gain, w_in, w_out, tag):
    hn, ug, uu, act = _norm_ffn_in(h, gain, w_in, f"{tag}_in")
    if callable(w_out):
        w_out = w_out(act)
    return _out_proj(act, w_out, h, 0.5, f"{tag}_out"), (h, hn, ug, uu, act), w_out


def _ffn_dgrad(dh, w_out, w_in, act_dg, act_du, h, gain, name, split_front=False):
    tp, d = dh.shape
    ff = w_out.shape[0]
    tm = TM_SMALL
    nt = (((1,), (1,)), ((), ()))

    def body(dh_ref, wo_ref, wi_ref, dg_ref, du_ref, h_ref, g_ref, o_ref, *out_refs):
        dhi_ref, dgain_ref = out_refs[-2:]

        @pl.when(pl.program_id(0) == 0)
        def _():
            dgain_ref[...] = jnp.zeros_like(dgain_ref)

        dho = dh_ref[...]
        dact = lax.dot_general((0.5 * dho).astype(BF16), wo_ref[...], nt, preferred_element_type=F32)
        d_gate = (dact * dg_ref[...].astype(F32)).astype(BF16)
        d_up = (dact * du_ref[...].astype(F32)).astype(BF16)
        o_ref[:, :ff] = d_gate
        o_ref[:, ff:] = d_up
        dhn = (lax.dot_general(d_gate, wi_ref[:, :ff], nt, preferred_element_type=F32)
               + lax.dot_general(d_up, wi_ref[:, ff:], nt, preferred_element_type=F32))
        dx, xhat = _rmsnorm_bwd(dhn, h_ref[...], g_ref[...])
        dgain_ref[...] += jnp.sum(dhn * xhat, axis=0, keepdims=True)
        dhi_ref[...] = dho + dx
        if split_front:
            @pl.when(pl.program_id(0) == 0)
            def _():
                out_refs[0][...] = dho + dx

    rows = lambda width: pl.BlockSpec((tm, width), lambda i: (i, 0))
    if split_front:
        assert tm == FRONT
        dhi_specs = [pl.BlockSpec((tm, d), lambda i: (0, 0)), pl.BlockSpec((tm, d), lambda i: (jnp.maximum(i - 1, 0), 0))]
        dhi_shapes = [jax.ShapeDtypeStruct((FRONT, d), F32), jax.ShapeDtypeStruct((tp - FRONT, d), F32)]
    else:
        dhi_specs, dhi_shapes = [rows(d)], [jax.ShapeDtypeStruct((tp, d), F32)]
    out = pl.pallas_call(
        body, name=name, grid=(tp // tm,),
        in_specs=[rows(d), _resident(w_out.shape, 1), _resident(w_in.shape, 1), rows(ff), rows(ff), rows(d),
                  pl.BlockSpec((1, d), lambda i: (0, 0))],
        out_specs=[rows(2 * ff), *dhi_specs, pl.BlockSpec((1, d), lambda i: (0, 0))],
        out_shape=[jax.ShapeDtypeStruct((tp, 2 * ff), BF16), *dhi_shapes, jax.ShapeDtypeStruct((1, d), F32)],
        compiler_params=_cp(1))(dh, w_out, w_in, act_dg, act_du, h, gain)
    return (out[0], tuple(out[1:3]), out[3]) if split_front else tuple(out)


def _ffn_bwd(dh, saved, gain, w_in, w_out, tag, push, split_front=False):
    h, hn, act_dg, act_du, act = saved
    du, dh_in, d_gain = _ffn_dgrad(dh, w_out, w_in, act_dg, act_du, h, gain, f"{tag}_dgrad", split_front)
    d_w_out = _wgrad(act, dh, bm=D_FF // 2, bn=D, scale=0.5, sharded=False, name=f"{tag}_dwout")
    d_w_in = _wgrad(hn, du, bm=D, bn=D_FF, scale=1.0, sharded=False, name=f"{tag}_dwin")
    return dh_in, d_gain, push([("cols", d_w_in), d_w_out])


def _sequence_grads(x, target, p, weights, grads):
    row = lambda v, token: v.reshape(1, -1) + token[0, 0]
    gains = {}

    tok = weights.start(1, weights.start(0, None))
    weights.pin = tok
    h = jnp.concatenate([jnp.zeros((FRONT, D), F32), x], axis=0) + tok[0, 0]
    rope = _rope_tables(h.shape[0])
    w = weights.wait(0, [tok, h, *rope, *weights.later_shards(2)])
    tok = weights.start(2, w["l0_ffn1_in"])
    h = lax.dynamic_update_slice(h, w["meta"], (FRONT - N_META, 0))
    gains["l0_ffn1"] = row(p["norm_ffn1"][0], tok)
    h, s1, w["l0_ffn1_out"] = _ffn_fwd(h, gains["l0_ffn1"], w["l0_ffn1_in"],
                                       lambda act: weights.wait(1, act)["l0_ffn1_out"], "l0_ffn1")
    w.update(weights.wait(2, h))
    tok = weights.start(4, weights.start(3, w["ret_in"]))
    gains["ret"] = row(p["norm_mix"][0], tok)
    hn, u = _norm_proj(h, gains["ret"], w["ret_in"], "ret_in")
    w.update(weights.wait(3, u))
    on, states, h_mix = _ret_fwd(u, w["ret_gain"], rope, h, w["ret_out"], "ret_fwd")
    s2 = (h, hn, u, on, states)
    w.update(weights.wait(4, h_mix))
    tok = weights.start(5, w["l0_ffn2_in"])
    gains["l0_ffn2"] = row(p["norm_ffn2"][0], tok)
    h, s3, _ = _ffn_fwd(h_mix, gains["l0_ffn2"], w["l0_ffn2_in"], w["l0_ffn2_out"], "l0_ffn2")
    saved = [(s1, s2, s3)]

    w.update(weights.wait(5, h))
    tok = weights.start(6, w["l1_ffn1_in"])
    gains["l1_ffn1"] = row(p["norm_ffn1"][1], tok)
    h, s1, _ = _ffn_fwd(h, gains["l1_ffn1"], w["l1_ffn1_in"], w["l1_ffn1_out"], "l1_ffn1")
    w.update(weights.wait(6, h))
    tok = weights.start(7, w["gla_out"])
    gains["gla"] = row(p["norm_mix"][1], tok)
    hn, u = _norm_proj(h, gains["gla"], w["gla_in"], "gla_in")
    on, states = _gla_fwd(u, w["gla_wg"], w["gla_bg"], w["gla_gain"], "gla_fwd")
    h_mix = _out_proj(on, w["gla_out"], h, 1.0, "gla_out")
    s2 = (h, hn, u, on, states)
    w.update(weights.wait(7, h_mix))
    gains["l1_ffn2"] = p["norm_ffn2"][1].reshape(1, -1)
    h, s3, _ = _ffn_fwd(h_mix, gains["l1_ffn2"], w["l1_ffn2_in"], w["l1_ffn2_out"], "l1_ffn2")
    saved.append((s1, s2, s3))

    dh, d_final, loss = _loss_head(h, p["final_norm"].reshape(1, -1), target, "loss_head")
    small = {"final_norm": d_final, "norm_ffn1": [None, None], "norm_mix": [None, None], "norm_ffn2": [None, None]}
    pusher = lambda k: functools.partial(grads.push, k)

    s1, s2, s3 = saved[1]
    dh, small["norm_ffn2"][1], tok = _ffn_bwd(dh, s3, gains["l1_ffn2"], w["l1_ffn2_in"], w["l1_ffn2_out"], "l1_ffn2",
                                              pusher(0))
    h_in, hn, u, on, states = s2
    d_on = _dgrad(dh, w["gla_out"], "gla_don")
    d_out = _wgrad(on, dh, bm=D, bn=D, scale=1.0, sharded=False, name="gla_dwout")
    du, small["gla_wg"], small["gla_bg"], small["gla_gain"] = _gla_bwd(
        u, w["gla_wg"], w["gla_bg"], w["gla_gain"] + tok[0, 0], states, d_on, "gla_bwd")
    d_in = _wgrad(hn, du, bm=D, bn=GLA_U, scale=1.0, sharded=False, name="gla_dwin")
    d_in = jnp.moveaxis(d_in[:, :GLA_IN].reshape(D, N_CHIPS, -1), 1, 0)
    tok = grads.push(1, [d_in, d_out])
    dh, small["norm_mix"][1] = _dgrad_norm(du, w["gla_in"], h_in, gains["gla"] + tok[0, 0], dh, "gla_dnorm")
    dh, small["norm_ffn1"][1], tok = _ffn_bwd(dh, s1, gains["l1_ffn1"], w["l1_ffn1_in"], w["l1_ffn1_out"], "l1_ffn1",
                                              pusher(2))

    s1, s2, s3 = saved[0]
    dh, small["norm_ffn2"][0], tok = _ffn_bwd(dh, s3, gains["l0_ffn2"] + tok[0, 0], w["l0_ffn2_in"],
                                              w["l0_ffn2_out"], "l0_ffn2", pusher(3))
    h_in, hn, u, on, states = s2
    d_on = _dgrad(dh, w["ret_out"], "ret_don")
    d_out = _wgrad(on, dh, bm=D, bn=D, scale=1.0, sharded=False, name="ret_dwout")
    du, small["ret_gain"] = _ret_bwd(u, w["ret_gain"] + tok[0, 0], rope, states, d_on, "ret_bwd")
    d_in = _wgrad(hn, du, bm=D, bn=w["ret_in"].shape[2], scale=1.0, sharded=True, name="ret_dwin")
    tok = grads.push(4, [d_in, d_out])
    dh, small["norm_mix"][0] = _dgrad_norm(du, w["ret_in"], h_in, gains["ret"] + tok[0, 0], dh, "ret_dnorm")
    (d_front, d_x), small["norm_ffn1"][0], tok = _ffn_bwd(dh, s1, gains["l0_ffn1"], w["l0_ffn1_in"], w["l0_ffn1_out"],
                                                          "l0_ffn1", pusher(5), split_front=True)
    grads.push(6, [], [d_front[FRONT - N_META:], *small["norm_ffn1"], *small["norm_mix"], *small["norm_ffn2"],
                       small["final_norm"], small["ret_gain"], small["gla_wg"][:GLA_RANK], small["gla_bg"],
                       small["gla_gain"], loss[:, :1] + tok[0, 0]])
    return d_x


_HBM = pl.BlockSpec(memory_space=pl.ANY)


def _place():
    return lax.axis_index("x"), lax.axis_index("y"), lax.axis_index("c")


def _flip(v, bit):
    return 1 - v if bit else v


DMA_CHUNK_BYTES = 128 * 1024


def _row_chunks(ref):
    rows, cols = ref.shape
    step = _row_tile(rows, max(16, DMA_CHUNK_BYTES // (cols * ref.dtype.itemsize)))
    return [pl.ds(a, step) for a in range(0, rows, step)]


def _whole(src, dst, send_sem, recv_sem, peer):
    return pltpu.make_async_remote_copy(src_ref=src, dst_ref=dst, send_sem=send_sem, recv_sem=recv_sem,
                                        device_id=peer, device_id_type=MESH)


def _send(src, dst, send_sem, recv_sem, peer):
    for rows in _row_chunks(src):
        _whole(src.at[rows], dst.at[rows], send_sem, recv_sem, peer).start()
    return _whole(src, dst, send_sem, recv_sem, peer)


_HBM_ONLY = pl.BlockSpec(memory_space=pltpu.HBM)
_SEMS = pl.BlockSpec(memory_space=pltpu.SEMAPHORE)
_SIDE_EFFECT = pltpu.CompilerParams(has_side_effects=pltpu.SideEffectType.DATAFLOW_SIDE_EFFECTING)
_GATHER_FLIPS = [(1, 0, 0), (0, 1, 0), (1, 1, 0), (0, 0, 1)]
_PEER_FLIPS = [(fx, fy, fc) for fx in (0, 1) for fy in (0, 1) for fc in (0, 1)][1:]


def _zero_token():
    return jnp.zeros((8, 128), F32)


def _exchange_start(srcs, lands, route, flips, after, name):
    n = len(srcs)

    def body(*refs):
        src, land = refs[:n], refs[n:2 * n]
        send_sems, recv_sems, token = refs[2 * n + 1], refs[2 * n + 2], refs[-1]
        me = _place()
        for t in range(n):
            for j, flip in enumerate(flips):
                peer = tuple(_flip(v, f) for v, f in zip(me, flip))
                s, d = route(t, src[t], land[t], me, peer)
                _send(s, d, send_sems.at[t * len(flips) + j], recv_sems.at[t * len(flips) + j], peer)
        token[...] = jnp.zeros_like(token)

    hbm = lambda a: pltpu.HBM(a.shape, a.dtype)
    sems = pltpu.SemaphoreType.DMA((n * len(flips),))
    operands = [pltpu.with_memory_space_constraint(a, pltpu.HBM) for a in list(srcs) + list(lands)]
    out = pl.pallas_call(
        body, name=name, in_specs=[_HBM_ONLY] * (2 * n) + [_HBM],
        out_shape=(sems, sems, *[hbm(a) for a in operands], jax.ShapeDtypeStruct((8, 128), F32)),
        out_specs=(_SEMS, _SEMS, *[_HBM_ONLY] * (2 * n), pl.BlockSpec(memory_space=pltpu.VMEM)),
        input_output_aliases={i: 2 + i for i in range(2 * n)}, compiler_params=_SIDE_EFFECT,
    )(*operands, _zero_token() if after is None else after)
    return (out[0], out[1], out[2:2 + n], out[2 + n:2 + 2 * n]), out[-1]


def _exchange_wait(started, route, flips, after, name):
    send_sems, recv_sems, srcs, lands = started
    n = len(srcs)

    def body(*refs):
        src, land = refs[:n], refs[n:2 * n]
        send_sems, recv_sems = refs[2 * n], refs[2 * n + 1]
        me = _place()
        for t in range(n):
            for j, flip in enumerate(flips):
                peer = tuple(_flip(v, f) for v, f in zip(me, flip))
                s, d = route(t, src[t], land[t], me, peer)
                cp = _whole(s, d, send_sems.at[t * len(flips) + j], recv_sems.at[t * len(flips) + j], peer)
                cp.wait_send()
                cp.wait_recv()

    hbm = lambda a: pltpu.HBM(a.shape, a.dtype)
    after = list(after) if isinstance(after, (list, tuple)) else [after]
    out = pl.pallas_call(
        body, name=name, in_specs=[_HBM_ONLY] * (2 * n) + [_SEMS, _SEMS] + [_HBM] * len(after),
        out_shape=tuple(hbm(a) for a in list(srcs) + list(lands)), out_specs=tuple([_HBM_ONLY] * (2 * n)),
        input_output_aliases={i: i for i in range(2 * n)}, compiler_params=_SIDE_EFFECT,
    )(*srcs, *lands, send_sems, recv_sems, *after)
    return out[:n], out[n:]


def _gather_route(t, src, land, me, peer):
    mine = 2 * me[0] + me[1]
    if land.ndim == 3:
        return src, land.at[mine]
    cols = src.shape[1]
    return src, land.at[:, pl.ds(pl.multiple_of(mine * cols, 128), cols)]


def _scatter_route(n_pieces):
    def route(t, src, land, me, peer):
        chip = 2 * peer[0] + peer[1]
        if t >= n_pieces:
            part = src
        elif src.ndim == 4:
            part = src.at[chip, peer[2]]
        else:
            rows, cols = land.shape[1:]
            part = src.at[pl.ds(pl.multiple_of(peer[2] * rows, 16), rows), pl.ds(pl.multiple_of(chip * cols, 128), cols)]
        return part, land.at[4 * me[0] + 2 * me[1] + me[2]]

    return route


def _swap_cores(halves, name):
    n = len(halves)

    def body(*refs):
        src, dst = refs[:n], refs[n:2 * n]
        send_sems, recv_sems = refs[2 * n:]
        x, y, c = _place()
        copies = [_send(src[t], dst[t], send_sems.at[t], recv_sems.at[t], (x, y, 1 - c)) for t in range(n)]
        for cp in copies:
            cp.wait()

    got = pl.pallas_call(
        body, name=name, in_specs=[_HBM] * n, out_specs=[_HBM] * n,
        out_shape=[jax.ShapeDtypeStruct(a.shape, a.dtype) for a in halves],
        scratch_shapes=[pltpu.SemaphoreType.DMA((n,)), pltpu.SemaphoreType.DMA((n,))],
    )(*halves)
    south = lax.axis_index("c") == 0
    return [jnp.stack([jnp.where(south, a, b), jnp.where(south, b, a)]) for a, b in zip(halves, got)]


def _row_tile(rows, cap):
    fits = [t for t in range(16, cap + 1, 16) if rows % t == 0]
    return fits[-1] if fits else rows


def _sum_slots(a, name):
    _, r, c = a.shape
    tr = _row_tile(r, 384)

    def body(a_ref, o_ref):
        s = a_ref[0].astype(F32)
        for k in range(1, N_DEV):
            s = s + a_ref[k].astype(F32)
        o_ref[...] = s

    return pl.pallas_call(
        body, name=name, grid=(r // tr,),
        in_specs=[pl.BlockSpec((N_DEV, tr, c), lambda i: (0, i, 0))],
        out_specs=pl.BlockSpec((tr, c), lambda i: (i, 0)),
        out_shape=jax.ShapeDtypeStruct((r, c), F32),
        compiler_params=_cp(1))(a)


def _adamw(w, g, m, v, name):
    layers, r, c = w.shape
    tr = _row_tile(r, 256)

    def body(w_ref, g_ref, m_ref, v_ref, d_ref, nm_ref, nv_ref):
        gv = g_ref[...]
        nm = ADAM_B1 * m_ref[...] + (1.0 - ADAM_B1) * gv
        nv = ADAM_B2 * v_ref[...] + (1.0 - ADAM_B2) * (gv * gv)
        m_hat = nm / (1.0 - ADAM_B1 ** ADAM_STEP)
        v_hat = nv / (1.0 - ADAM_B2 ** ADAM_STEP)
        d_ref[...] = -ADAM_LR * (m_hat / (jnp.sqrt(v_hat) + ADAM_EPS) + ADAM_WD * w_ref[...])
        nm_ref[...] = nm
        nv_ref[...] = nv

    spec = pl.BlockSpec((None, tr, c), lambda a, i: (a, i, 0))
    return pl.pallas_call(
        body, name=name, grid=(layers, r // tr), in_specs=[spec] * 4, out_specs=[spec] * 3,
        out_shape=[jax.ShapeDtypeStruct((layers, r, c), F32)] * 3,
        compiler_params=_cp(2))(*[pltpu.with_memory_space_constraint(a, pltpu.HBM) for a in (w, g, m, v)])


_SMALL = ["meta_tokens", "ret_head_norm", "gla_w_gate", "gla_b_gate", "gla_head_norm"]
_LOCAL_SMALL = ["meta_tokens", "norm_ffn1", "norm_mix", "norm_ffn2", "ret_head_norm", "gla_w_gate", "gla_b_gate",
                "gla_head_norm", "final_norm"]
_WEIGHTS = ["meta_tokens", "norm_ffn1", "ffn1_w_in", "ffn1_w_out", "norm_mix", "norm_ffn2", "ffn2_w_in", "ffn2_w_out",
            "ret_w_in", "ret_head_norm", "ret_w_out", "gla_w_in", "gla_w_gate", "gla_b_gate", "gla_head_norm",
            "gla_w_out", "final_norm"]


def _pack_rows(arrays, width):
    flat = jnp.concatenate([a.reshape(-1) for a in arrays])
    pad = -flat.shape[0] % (8 * width)
    return jnp.pad(flat, (0, pad)).reshape(-1, width)


def _unpack_rows(packed, shapes):
    flat, out, at = packed.reshape(-1), [], 0
    for s in shapes:
        size = 1
        for dim in s:
            size *= dim
        out.append(flat[at:at + size].reshape(s))
        at += size
    return out


class _WeightGather:
    GROUPS = [("small", "l0_ffn1_in"), ("l0_ffn1_out",), ("ret_in",), ("ret_out",), ("l0_ffn2_in", "l0_ffn2_out"),
              ("l1_ffn1_in", "l1_ffn1_out"), ("gla_in", "gla_out"), ("l1_ffn2_in", "l1_ffn2_out")]

    def __init__(self, p):
        self.small_shapes = [p[name].shape for name in _SMALL]
        self.f32 = {"small": _pack_rows([p[name] for name in _SMALL], 128), "ret_in": p["ret_w_in"][0],
                    "ret_out": p["ret_w_out"][0], "gla_in": p["gla_w_in"][0], "gla_out": p["gla_w_out"][0]}
        for layer in range(2):
            for name in ("ffn1", "ffn2"):
                self.f32[f"l{layer}_{name}_in"] = p[f"{name}_w_in"][layer]
                self.f32[f"l{layer}_{name}_out"] = p[f"{name}_w_out"][layer]
        self.shards = {}
        self.started = {}
        self.pin = None

    def shard(self, name):
        if name not in self.shards:
            a = self.f32[name]
            if name != "small":
                a = (a if self.pin is None else a + self.pin[0, 0]).astype(BF16)
            self.shards[name] = a
        return self.shards[name]

    def later_shards(self, k):
        return [self.shard(name) for group in self.GROUPS[k:] for name in group]

    def start(self, k, after):
        shards = [self.shard(name) for name in self.GROUPS[k]]
        lands = []
        for name, s in zip(self.GROUPS[k], shards):
            if "ffn" in name and name.endswith("_in"):
                lands.append(lax.empty((s.shape[0], N_CHIPS * s.shape[1]), s.dtype))
            else:
                lands.append(lax.empty((N_CHIPS,) + s.shape, s.dtype))
        self.started[k], token = _exchange_start(shards, lands, _gather_route, _GATHER_FLIPS, after, f"gather{k}_start")
        return token

    def wait(self, k, after):
        _, got = _exchange_wait(self.started[k], _gather_route, _GATHER_FLIPS, after, f"gather{k}_wait")
        w = {}
        for name, g in zip(self.GROUPS[k], got):
            if name == "small":
                parts = zip(*[_unpack_rows(g[chip], self.small_shapes) for chip in range(N_CHIPS)])
                cat = lambda a: jnp.moveaxis(a, 0, -2).reshape(a.shape[1:-1] + (-1,))
                meta, ret_gain, wg, bg, gla_gain = [cat(jnp.stack(part)) for part in parts]
                w.update(meta=meta, ret_gain=ret_gain.reshape(1, -1), gla_bg=bg.reshape(1, -1),
                         gla_gain=gla_gain.reshape(1, -1),
                         gla_wg=jnp.pad(wg[0], ((0, 128 - GLA_RANK), (0, 0))).astype(BF16))
            elif name == "gla_in":
                full = jnp.moveaxis(g, 0, 1).reshape(D, -1)
                w[name] = jnp.pad(full, ((0, 0), (0, GLA_U - GLA_IN)))[None]
            elif name.endswith("_out"):
                w[name] = g.reshape(-1, g.shape[-1])
            else:
                w[name] = g
        return w


class _GradExchange:
    def __init__(self):
        self.started = []
        self.token = None
        self.small_shapes = None

    def push(self, k, arrays, small=None):
        srcs, lands = [], []
        for a in arrays:
            if isinstance(a, tuple):
                a = a[1]
                piece = (a.shape[0] // 2, a.shape[1] // N_CHIPS)
            else:
                a = a.reshape(N_CHIPS, 2, -1, a.shape[-1])
                piece = a.shape[2:]
            srcs.append(a)
            lands.append(lax.empty((N_DEV,) + piece, a.dtype))
        if small is not None:
            self.small_shapes = [a.shape for a in small]
            srcs.append(_pack_rows(small, D))
            lands.append(lax.empty((N_DEV,) + srcs[-1].shape, F32))
        started, self.token = _exchange_start(srcs, lands, _scatter_route(len(arrays)), _PEER_FLIPS, None,
                                              f"scatter{k}_start")
        self.started.append((started, len(arrays)))
        return self.token

    def collect(self, groups, after=None):
        x, y, c = _place()
        after, sums = self.token if after is None else after, []
        for k in groups:
            started, n_pieces = self.started[k]
            srcs, got = _exchange_wait(started, _scatter_route(n_pieces), _PEER_FLIPS, after, f"scatter{k}_wait")
            own = []
            for t, (a, g) in enumerate(zip(srcs, got)):
                if t >= n_pieces:
                    own.append(a)
                elif a.ndim == 4:
                    own.append(a[2 * x + y, c])
                else:
                    rows, cols = g.shape[1:]
                    own.append(lax.dynamic_slice(a, (c * rows, (2 * x + y) * cols), (rows, cols)))
            got = [lax.dynamic_update_index_in_dim(g, a, 4 * x + 2 * y + c, 0) for g, a in zip(got, own)]
            sums.append([_sum_slots(a, f"sum{k}_{i}") for i, a in enumerate(got)])
            after = sums[-1][0]
        return sums


def kernel(x, meta_tokens, norm_ffn1, ffn1_w_in, ffn1_w_out, norm_mix, norm_ffn2, ffn2_w_in, ffn2_w_out, ret_w_in, ret_head_norm, ret_w_out, gla_w_in, gla_w_gate, gla_b_gate, gla_head_norm, gla_w_out, final_norm, loss_target, m_meta_tokens, m_norm_ffn1, m_ffn1_w_in, m_ffn1_w_out, m_norm_mix, m_norm_ffn2, m_ffn2_w_in, m_ffn2_w_out, m_ret_w_in, m_ret_head_norm, m_ret_w_out, m_gla_w_in, m_gla_w_gate, m_gla_b_gate, m_gla_head_norm, m_gla_w_out, m_final_norm, v_meta_tokens, v_norm_ffn1, v_ffn1_w_in, v_ffn1_w_out, v_norm_mix, v_norm_ffn2, v_ffn2_w_in, v_ffn2_w_out, v_ret_w_in, v_ret_head_norm, v_ret_w_out, v_gla_w_in, v_gla_w_gate, v_gla_b_gate, v_gla_head_norm, v_gla_w_out, v_final_norm):
    p = dict(meta_tokens=meta_tokens, norm_ffn1=norm_ffn1, ffn1_w_in=ffn1_w_in, ffn1_w_out=ffn1_w_out, norm_mix=norm_mix,
             norm_ffn2=norm_ffn2, ffn2_w_in=ffn2_w_in, ffn2_w_out=ffn2_w_out, ret_w_in=ret_w_in,
             ret_head_norm=ret_head_norm, ret_w_out=ret_w_out, gla_w_in=gla_w_in, gla_w_gate=gla_w_gate,
             gla_b_gate=gla_b_gate, gla_head_norm=gla_head_norm, gla_w_out=gla_w_out, final_norm=final_norm)
    m = dict(zip(_WEIGHTS, (m_meta_tokens, m_norm_ffn1, m_ffn1_w_in, m_ffn1_w_out, m_norm_mix, m_norm_ffn2, m_ffn2_w_in,
                            m_ffn2_w_out, m_ret_w_in, m_ret_head_norm, m_ret_w_out, m_gla_w_in, m_gla_w_gate,
                            m_gla_b_gate, m_gla_head_norm, m_gla_w_out, m_final_norm)))
    v = dict(zip(_WEIGHTS, (v_meta_tokens, v_norm_ffn1, v_ffn1_w_in, v_ffn1_w_out, v_norm_mix, v_norm_ffn2, v_ffn2_w_in,
                            v_ffn2_w_out, v_ret_w_in, v_ret_head_norm, v_ret_w_out, v_gla_w_in, v_gla_w_gate,
                            v_gla_b_gate, v_gla_head_norm, v_gla_w_out, v_final_norm)))

    exchange = _GradExchange()
    d_x = _sequence_grads(x[0], loss_target[0], p, _WeightGather(p), exchange)
    names = [("ffn2_w_in", 1), ("ffn2_w_out", 1), ("gla_w_in", 0), ("gla_w_out", 0), ("ffn1_w_in", 1), ("ffn1_w_out", 1),
             ("ffn2_w_in", 0), ("ffn2_w_out", 0), ("ret_w_in", 0), ("ret_w_out", 0), ("ffn1_w_in", 0), ("ffn1_w_out", 0)]
    shard, grads, delta, new_m, new_v = {}, {}, {}, {}, {}

    def swap(sums, keys, name):
        for key, a in zip(keys, _swap_cores(sums, name)):
            shard[key] = a.reshape(-1, a.shape[-1])

    def update(name):
        layers = p[name].shape[0]
        grads[name] = jnp.stack([shard[name, layer] for layer in range(layers)])
        delta[name], new_m[name], new_v[name] = _adamw(p[name], grads[name], m[name], v[name], f"adamw_{name}")

    swap([a for group in exchange.collect(range(5)) for a in group], names[:10], "swap_first")
    for name in ("ffn2_w_in", "ffn2_w_out", "ret_w_in", "ret_w_out", "gla_w_in", "gla_w_out"):
        update(name)
    last, (small_sum,) = exchange.collect([5, 6], after=list(delta.values()))
    swap(last, names[10:], "swap_last")
    for name in ("ffn1_w_in", "ffn1_w_out"):
        update(name)

    chip = 2 * lax.axis_index("x") + lax.axis_index("y")
    cols = lambda a, n: lax.dynamic_slice_in_dim(a, chip * n, n, axis=a.ndim - 1)
    (s_meta, s_n1a, s_n1b, s_nma, s_nmb, s_n2a, s_n2b, s_final, s_ret_gain, s_wg, s_bg, s_gla_gain,
     s_loss) = _unpack_rows(small_sum, exchange.small_shapes)
    grads.update({
        "meta_tokens": cols(s_meta, 256), "norm_ffn1": jnp.concatenate([s_n1a, s_n1b]),
        "norm_mix": jnp.concatenate([s_nma, s_nmb]), "norm_ffn2": jnp.concatenate([s_n2a, s_n2b]),
        "final_norm": s_final.reshape(D),
        "ret_head_norm": cols(s_ret_gain.reshape(1, HEADS, RET_DV), RET_DV // N_CHIPS),
        "gla_w_gate": cols(s_wg, GLA_DK)[None], "gla_b_gate": cols(s_bg, GLA_DK),
        "gla_head_norm": cols(s_gla_gain.reshape(1, HEADS, GLA_DV), GLA_DV // N_CHIPS),
    })
    for name in _LOCAL_SMALL:
        shape = p[name].shape
        as3d = lambda a: a.reshape((1,) * (3 - len(shape)) + shape)
        out = _adamw(as3d(p[name]), as3d(grads[name]), as3d(m[name]), as3d(v[name]), f"adamw_{name}")
        delta[name], new_m[name], new_v[name] = [a.reshape(shape) for a in out]

    return (s_loss.reshape(()), d_x[None], *[grads[n] for n in _WEIGHTS], *[delta[n] for n in _WEIGHTS],
            *[new_m[n] for n in _WEIGHTS], *[new_v[n] for n in _WEIGHTS])
```

```python
import functools

import jax
import numpy as np
import jax.numpy as jnp
from jax import lax
from jax.experimental import pallas as pl
from jax.experimental.pallas import tpu as pltpu

F32, BF16 = jnp.float32, jnp.bfloat16
MESH = pl.DeviceIdType.MESH

D = 1024
N_META = 16
CHUNK = 64
RET_CHUNK = 256
FRONT = 256
D_FF = 2816
EPS = 1e-6
HEADS = 4
RET_DK, RET_DV = 256, 512
GLA_DK, GLA_DV = 128, 256
GLA_RANK = 16
GLA_TAU = 16.0
GLA_IN = 2 * HEADS * GLA_DK + 2 * HEADS * GLA_DV + GLA_RANK
GLA_U = 3328
ROPE_BASE = 10000.0
N_CHIPS = 4
N_DEV = 8

ADAM_LR, ADAM_B1, ADAM_B2, ADAM_EPS, ADAM_WD, ADAM_STEP = 0.001, 0.9, 0.999, 1e-08, 0.01, 10

VMEM_LIMIT_BYTES = 56 * 1024 * 1024
TM = 768
TM_SMALL = 256


TM_RESIDENT = 384
MXU_TILE = 256


def _cp(n_axes):
    return pltpu.CompilerParams(dimension_semantics=("arbitrary",) * n_axes, vmem_limit_bytes=VMEM_LIMIT_BYTES)


def _resident(shape, n_axes):
    zeros = (0,) * len(shape)
    index = (lambda i: zeros) if n_axes == 1 else (lambda i, j: zeros)
    return pl.BlockSpec(shape, index, pipeline_mode=pl.Buffered(1))


def _dg(a, b, ca, cb):
    nb = a.ndim - 2
    dims = (((ca + nb,), (cb + nb,)), (tuple(range(nb)), tuple(range(nb))))
    return lax.dot_general(a.astype(BF16), b.astype(BF16), dims, preferred_element_type=F32)


@jax.custom_vjp
def _nn(a, b):
    return _dg(a, b, 1, 0)


@jax.custom_vjp
def _nt(a, b):
    return _dg(a, b, 1, 1)


@jax.custom_vjp
def _tn(a, b):
    return _dg(a, b, 0, 0)


def _dot_vjp(fn, ca, cb, da, db):
    def fwd(a, b):
        a, b = a.astype(BF16), b.astype(BF16)
        return _dg(a, b, ca, cb), (a, b)

    def bwd(res, g):
        a, b = res
        g = g.astype(BF16)
        grad = lambda other, dims, g_first: _dg(g, other, *dims) if g_first else _dg(other, g, *dims)
        return grad(b, *da), grad(a, *db)

    fn.defvjp(fwd, bwd)


_dot_vjp(_nn, 1, 0, ((1, 1), True), ((0, 0), False))
_dot_vjp(_nt, 1, 1, ((1, 0), True), ((0, 0), True))
_dot_vjp(_tn, 0, 0, ((1, 1), False), ((1, 0), False))


def _split3_dot(m, a):
    a1 = a.astype(BF16)
    r1 = a - a1.astype(F32)
    a2 = r1.astype(BF16)
    a3 = (r1 - a2.astype(F32)).astype(BF16)
    mb = jnp.broadcast_to(m, a.shape[:-2] + m.shape)
    return _dg(mb, a1, 1, 0) + _dg(mb, a2, 1, 0) + _dg(mb, a3, 1, 0)


@jax.custom_vjp
def _cum(m, mt, a):
    return _split3_dot(m, a)


_cum.defvjp(lambda m, mt, a: (_split3_dot(m, a), (m, mt)),
            lambda res, g: (jnp.zeros_like(res[0]), jnp.zeros_like(res[1]), _split3_dot(res[1], g)))


def _sigmoid(x):
    return 1.0 / (1.0 + jnp.exp(-x))


def _rms(x):
    return lax.rsqrt(jnp.mean(x * x, axis=-1, keepdims=True) + EPS)


def _rmsnorm_bwd(dy, x, gain):
    r = _rms(x)
    xhat = x * r
    dxh = dy * gain
    return r * (dxh - xhat * jnp.mean(dxh * xhat, axis=-1, keepdims=True)), xhat


def _norm_proj(h, gain, w, name):
    tp, d = h.shape
    s, _, ns = w.shape

    tm = TM_RESIDENT

    def body(h_ref, g_ref, w_ref, hn_ref, u_ref):
        x = h_ref[...]
        a = (x * _rms(x) * g_ref[...]).astype(BF16)
        hn_ref[...] = a
        for k in range(s):
            u_ref[:, ns * k:ns * (k + 1)] = jnp.dot(a, w_ref[k], preferred_element_type=F32).astype(BF16)

    return pl.pallas_call(
        body, name=name, grid=(tp // tm,),
        in_specs=[pl.BlockSpec((tm, d), lambda i: (i, 0)), pl.BlockSpec((1, d), lambda i: (0, 0)), _resident(w.shape, 1)],
        out_specs=[pl.BlockSpec((tm, d), lambda i: (i, 0)), pl.BlockSpec((tm, s * ns), lambda i: (i, 0))],
        out_shape=[jax.ShapeDtypeStruct((tp, d), BF16), jax.ShapeDtypeStruct((tp, s * ns), BF16)],
        compiler_params=_cp(1))(h, gain, w)


def _norm_ffn_in(h, gain, w, name):
    tp, d = h.shape
    ff = w.shape[1] // 2
    tm = TM_RESIDENT
    blocks = [(c, min(c + 6 * MXU_TILE, ff)) for c in range(0, ff, 6 * MXU_TILE)]

    def body(h_ref, g_ref, w_ref, hn_ref, dg_ref, du_ref, act_ref):
        x = h_ref[...]
        a = (x * _rms(x) * g_ref[...]).astype(BF16)
        hn_ref[...] = a
        for c0, c1 in blocks:
            g = jnp.dot(a, w_ref[:, c0:c1], preferred_element_type=F32)
            u = jnp.dot(a, w_ref[:, ff + c0:ff + c1], preferred_element_type=F32)
            sg = _sigmoid(g)
            silu = g * sg
            dg_ref[:, c0:c1] = (u * (sg + silu * (1.0 - sg))).astype(BF16)
            du_ref[:, c0:c1] = silu.astype(BF16)
            act_ref[:, c0:c1] = (silu * u).astype(BF16)

    wide = jax.ShapeDtypeStruct((tp, ff), BF16)
    return pl.pallas_call(
        body, name=name, grid=(tp // tm,),
        in_specs=[pl.BlockSpec((tm, d), lambda i: (i, 0)), pl.BlockSpec((1, d), lambda i: (0, 0)),
                  _resident(w.shape, 1)],
        out_specs=[pl.BlockSpec((tm, d), lambda i: (i, 0))] + [pl.BlockSpec((tm, ff), lambda i: (i, 0))] * 3,
        out_shape=[jax.ShapeDtypeStruct((tp, d), BF16), wide, wide, wide],
        compiler_params=_cp(1))(h, gain, w)


def _out_proj(a, w, h, scale, name):
    tp, k = a.shape
    d = w.shape[1]

    def body(a_ref, w_ref, h_ref, o_ref):
        o_ref[...] = h_ref[...] + scale * jnp.dot(a_ref[...], w_ref[...], preferred_element_type=F32)

    return pl.pallas_call(
        body, name=name, grid=(tp // TM,),
        in_specs=[pl.BlockSpec((TM, k), lambda i: (i, 0)), pl.BlockSpec((k, d), lambda i: (0, 0)),
                  pl.BlockSpec((TM, d), lambda i: (i, 0))],
        out_specs=pl.BlockSpec((TM, d), lambda i: (i, 0)),
        out_shape=jax.ShapeDtypeStruct((tp, d), F32),
        compiler_params=_cp(1))(a, w, h)


def _dgrad(dh, w, name):
    tp, d = dh.shape
    k = w.shape[0]

    def body(dh_ref, w_ref, o_ref):
        o_ref[...] = lax.dot_general(dh_ref[...].astype(BF16), w_ref[...], (((1,), (1,)), ((), ())),
                                     preferred_element_type=F32).astype(BF16)

    return pl.pallas_call(
        body, name=name, grid=(tp // TM,),
        in_specs=[pl.BlockSpec((TM, d), lambda i: (i, 0)), pl.BlockSpec((k, d), lambda i: (0, 0))],
        out_specs=pl.BlockSpec((TM, k), lambda i: (i, 0)),
        out_shape=jax.ShapeDtypeStruct((tp, k), BF16),
        compiler_params=_cp(1))(dh, w)


def _wgrad(a, b, *, bm, bn, scale, sharded, name):
    tp, m = a.shape
    n = b.shape[1]
    nk = tp // TM

    def body(a_ref, b_ref, o_ref, acc_ref):
        k = pl.program_id(2)

        @pl.when(k == 0)
        def _():
            acc_ref[...] = jnp.zeros_like(acc_ref)

        bb = b_ref[...]
        if scale != 1.0:
            bb = scale * bb
        acc_ref[...] += lax.dot_general(a_ref[...], bb.astype(BF16), (((0,), (0,)), ((), ())),
                                        preferred_element_type=F32)

        @pl.when(k == nk - 1)
        def _():
            o_ref[...] = acc_ref[...].astype(BF16)

    if sharded:
        assert m == bm
        out_spec = pl.BlockSpec((None, bm, bn), lambda i, j, k: (j, 0, 0))
        out_shape = jax.ShapeDtypeStruct((n // bn, m, bn), BF16)
    else:
        out_spec = pl.BlockSpec((bm, bn), lambda i, j, k: (i, j))
        out_shape = jax.ShapeDtypeStruct((m, n), BF16)
    return pl.pallas_call(
        body, name=name, grid=(m // bm, n // bn, nk),
        in_specs=[pl.BlockSpec((TM, bm), lambda i, j, k: (k, i)), pl.BlockSpec((TM, bn), lambda i, j, k: (k, j))],
        out_specs=out_spec, out_shape=out_shape,
        scratch_shapes=[pltpu.VMEM((bm, bn), F32)],
        compiler_params=_cp(3))(a, b)


def _dgrad_norm(du, w, h, gain, dh_out, name):
    tp, d = h.shape
    s, _, ns = w.shape
    tm = TM_RESIDENT

    def body(du_ref, w_ref, h_ref, g_ref, dho_ref, dhi_ref, dg_ref):
        @pl.when(pl.program_id(0) == 0)
        def _():
            dg_ref[...] = jnp.zeros_like(dg_ref)

        dhn = None
        for k in range(s):
            part = lax.dot_general(du_ref[:, ns * k:ns * (k + 1)], w_ref[k], (((1,), (1,)), ((), ())),
                                   preferred_element_type=F32)
            dhn = part if dhn is None else dhn + part
        dx, xhat = _rmsnorm_bwd(dhn, h_ref[...], g_ref[...])
        dg_ref[...] += jnp.sum(dhn * xhat, axis=0, keepdims=True)
        dhi_ref[...] = dho_ref[...] + dx

    return pl.pallas_call(
        body, name=name, grid=(tp // tm,),
        in_specs=[pl.BlockSpec((tm, s * ns), lambda i: (i, 0)), _resident(w.shape, 1),
                  pl.BlockSpec((tm, d), lambda i: (i, 0)), pl.BlockSpec((1, d), lambda i: (0, 0)),
                  pl.BlockSpec((tm, d), lambda i: (i, 0))],
        out_specs=[pl.BlockSpec((tm, d), lambda i: (i, 0)), pl.BlockSpec((1, d), lambda i: (0, 0))],
        out_shape=[jax.ShapeDtypeStruct((tp, d), F32), jax.ShapeDtypeStruct((1, d), F32)],
        compiler_params=_cp(1))(du, w, h, gain, dh_out)


def _loss_head(h, gain, target, name):
    tp, d = h.shape
    tm = TM_SMALL
    front_tiles = FRONT // tm

    def body(h_ref, g_ref, t_ref, dh_ref, dg_ref, loss_ref):
        i = pl.program_id(0)

        @pl.when(i == 0)
        def _():
            dg_ref[...] = jnp.zeros_like(dg_ref)
            loss_ref[...] = jnp.zeros_like(loss_ref)

        x = h_ref[...]
        gain_v = g_ref[...]
        y = x * _rms(x) * gain_v
        err = jnp.where(i >= front_tiles, y - t_ref[...], 0.0)
        loss_ref[...] += 0.5 * jnp.sum(jnp.mean(err * err, axis=-1, keepdims=True), axis=0, keepdims=True)
        dy = err * (1.0 / d)
        dx, xhat = _rmsnorm_bwd(dy, x, gain_v)
        dg_ref[...] += jnp.sum(dy * xhat, axis=0, keepdims=True)
        dh_ref[...] = dx

    return pl.pallas_call(
        body, name=name, grid=(tp // tm,),
        in_specs=[pl.BlockSpec((tm, d), lambda i: (i, 0)), pl.BlockSpec((1, d), lambda i: (0, 0)),
                  pl.BlockSpec((tm, d), lambda i: (jnp.maximum(i - front_tiles, 0), 0))],
        out_specs=[pl.BlockSpec((tm, d), lambda i: (i, 0)), pl.BlockSpec((1, d), lambda i: (0, 0)),
                   pl.BlockSpec((1, 128), lambda i: (0, 0))],
        out_shape=[jax.ShapeDtypeStruct((tp, d), F32), jax.ShapeDtypeStruct((1, d), F32),
                   jax.ShapeDtypeStruct((1, 128), F32)],
        compiler_params=_cp(1))(h, gain, target)


def _gated_headnorm(o, g, gain):
    return o * _rms(o) * gain * (g * _sigmoid(g))


def _row_mask(chunk, size=CHUNK):
    rows = chunk * size + lax.broadcasted_iota(jnp.int32, (size, 1), 0)
    return (rows >= FRONT - N_META).astype(F32)


def _ret_head(q1, q2, k1, k2, v, g, state, gain, cos, sin, dmat, dq, dk, dc):
    q = jnp.concatenate([q1 * cos - q2 * sin, q1 * sin + q2 * cos], axis=-1)
    k = jnp.concatenate([k1 * cos - k2 * sin, k1 * sin + k2 * cos], axis=-1) * (RET_DK ** -0.5)
    scores = _nt(q, k) * dmat
    o = _nn(scores, v) + _nn(q * dq, state)
    new_state = state * dc + _tn(k * dk, v)
    return _gated_headnorm(o, g, gain), new_state


def _ret_consts():
    log_gamma = jnp.log1p(-2.0 ** (-5.0 - jnp.arange(HEADS, dtype=F32)))
    idx = jnp.arange(RET_CHUNK, dtype=F32)
    rel = idx[:, None] - idx[None, :]
    dmat = jnp.where(rel >= 0, jnp.exp(log_gamma[:, None, None] * jnp.maximum(rel, 0.0)), 0.0)
    dq = jnp.exp(log_gamma[:, None] * (idx + 1.0))[..., None]
    dk = jnp.exp(log_gamma[:, None] * (RET_CHUNK - 1.0 - idx))[..., None]
    dc = jnp.broadcast_to(jnp.exp(log_gamma * RET_CHUNK)[:, None, None], (HEADS, 1, 128))
    return dmat, dq, dk, dc


def _rope_tables(tp):
    half = RET_DK // 2
    inv = 1.0 / (ROPE_BASE ** jnp.linspace(0.0, 1.0, half, dtype=F32))
    pos = (jnp.arange(tp) - (FRONT - N_META)).astype(F32)
    ang = pos[:, None] * inv[None, :]
    return jnp.cos(ang), jnp.sin(ang)


_RET_V0, _RET_G0 = 2 * D, 4 * D


def _heads(ref, start, width, stride=None, rows=slice(None)):
    stride = width if stride is None else stride
    return jnp.stack([ref[rows, start + stride * h:start + stride * h + width].astype(F32) for h in range(HEADS)])


def _put_heads(ref, start, value, mask, stride=None, rows=slice(None)):
    width = value.shape[-1]
    stride = width if stride is None else stride
    for h in range(HEADS):
        ref[rows, start + stride * h:start + stride * h + width] = (value[h] * mask).astype(ref.dtype)


def _ret_pieces(u_ref):
    hk = RET_DK // 2
    return (_heads(u_ref, 0, hk, RET_DK), _heads(u_ref, hk, hk, RET_DK), _heads(u_ref, D, hk, RET_DK),
            _heads(u_ref, D + hk, hk, RET_DK), _heads(u_ref, _RET_V0, RET_DV), _heads(u_ref, _RET_G0, RET_DV))


def _ret_const_specs(rev=None):
    c = (lambda n: (rev(n), 0)) if rev else (lambda n: (n, 0))
    z3 = lambda n: (0, 0, 0)
    return [pl.BlockSpec((RET_CHUNK, RET_DK // 2), c), pl.BlockSpec((RET_CHUNK, RET_DK // 2), c),
            pl.BlockSpec((HEADS, RET_CHUNK, RET_CHUNK), z3), pl.BlockSpec((HEADS, RET_CHUNK, 1), z3),
            pl.BlockSpec((HEADS, RET_CHUNK, 1), z3), pl.BlockSpec((HEADS, 1, 128), z3)]


def _ret_fwd(u, gain, rope, h, w_out, name):
    tp = u.shape[0]
    nch = tp // RET_CHUNK
    cos, sin = rope
    dmat, dq, dk, dc = _ret_consts()

    def body(u_ref, gain_ref, h_ref, w_ref, cos_ref, sin_ref, dmat_ref, dq_ref, dk_ref, dc_ref,
             on_ref, st_ref, hmix_ref, state_ref):
        @pl.when(pl.program_id(0) == 0)
        def _():
            state_ref[...] = jnp.zeros_like(state_ref)

        state = state_ref[...]
        st_ref[...] = state.astype(BF16)
        on, new_state = _ret_head(*_ret_pieces(u_ref), state, _heads(gain_ref, 0, RET_DV), cos_ref[...], sin_ref[...],
                                  dmat_ref[...], dq_ref[...], dk_ref[...], dc_ref[...][:, :, :1])
        state_ref[...] = new_state
        _put_heads(on_ref, 0, on, 1.0)
        hmix_ref[...] = h_ref[...] + jnp.dot(on_ref[...], w_ref[...], preferred_element_type=F32)

    rows = lambda width: pl.BlockSpec((RET_CHUNK, width), lambda n: (n, 0))
    return pl.pallas_call(
        body, name=name, grid=(nch,),
        in_specs=[rows(6 * D), pl.BlockSpec((1, HEADS * RET_DV), lambda n: (0, 0)), rows(D),
                  _resident(w_out.shape, 1)] + _ret_const_specs(),
        out_specs=[rows(HEADS * RET_DV), pl.BlockSpec((None, HEADS, RET_DK, RET_DV), lambda n: (n, 0, 0, 0)), rows(D)],
        out_shape=[jax.ShapeDtypeStruct((tp, HEADS * RET_DV), BF16),
                   jax.ShapeDtypeStruct((nch, HEADS, RET_DK, RET_DV), BF16), jax.ShapeDtypeStruct((tp, D), F32)],
        scratch_shapes=[pltpu.VMEM((HEADS, RET_DK, RET_DV), F32)],
        compiler_params=_cp(1))(u, gain, h, w_out, cos, sin, dmat, dq, dk, dc)


def _ret_bwd(u, gain, rope, states, d_on, name):
    tp = u.shape[0]
    nch = tp // RET_CHUNK
    cos, sin = rope
    dmat, dq, dk, dc = _ret_consts()
    rev = lambda n: nch - 1 - n
    hk = RET_DK // 2

    def body(u_ref, gain_ref, st_ref, don_ref, cos_ref, sin_ref, dmat_ref, dq_ref, dk_ref, dc_ref,
             du_ref, dgain_ref, dstate_ref):
        @pl.when(pl.program_id(0) == 0)
        def _():
            dstate_ref[...] = jnp.zeros_like(dstate_ref)
            dgain_ref[...] = jnp.zeros_like(dgain_ref)

        mask = _row_mask(rev(pl.program_id(0)), RET_CHUNK)
        consts = (cos_ref[...], sin_ref[...], dmat_ref[...], dq_ref[...], dk_ref[...], dc_ref[...][:, :, :1])
        _, vjp = jax.vjp(lambda *a: _ret_head(*a, *consts), *_ret_pieces(u_ref), st_ref[...].astype(F32),
                         _heads(gain_ref, 0, RET_DV))
        dq1, dq2, dk1, dk2, dv, dg, dstate, dgain = vjp((_heads(don_ref, 0, RET_DV), dstate_ref[...]))
        dstate_ref[...] = dstate
        for hd in range(HEADS):
            dgain_ref[:, RET_DV * hd:RET_DV * (hd + 1)] += dgain[hd]
        _put_heads(du_ref, 0, dq1, mask, RET_DK)
        _put_heads(du_ref, hk, dq2, mask, RET_DK)
        _put_heads(du_ref, D, dk1, mask, RET_DK)
        _put_heads(du_ref, D + hk, dk2, mask, RET_DK)
        _put_heads(du_ref, _RET_V0, dv, mask)
        _put_heads(du_ref, _RET_G0, dg, mask)

    return pl.pallas_call(
        body, name=name, grid=(nch,),
        in_specs=[pl.BlockSpec((RET_CHUNK, 6 * D), lambda n: (rev(n), 0)),
                  pl.BlockSpec((1, HEADS * RET_DV), lambda n: (0, 0)),
                  pl.BlockSpec((None, HEADS, RET_DK, RET_DV), lambda n: (rev(n), 0, 0, 0)),
                  pl.BlockSpec((RET_CHUNK, HEADS * RET_DV), lambda n: (rev(n), 0))] + _ret_const_specs(rev),
        out_specs=[pl.BlockSpec((RET_CHUNK, 6 * D), lambda n: (rev(n), 0)),
                   pl.BlockSpec((1, HEADS * RET_DV), lambda n: (0, 0))],
        out_shape=[jax.ShapeDtypeStruct((tp, 6 * D), BF16), jax.ShapeDtypeStruct((1, HEADS * RET_DV), F32)],
        scratch_shapes=[pltpu.VMEM((HEADS, RET_DK, RET_DV), F32)],
        compiler_params=_cp(1))(u, gain, states, d_on, cos, sin, dmat, dq, dk, dc)


_GLA_K0, _GLA_V0, _GLA_G0, _GLA_Z0 = 512, 1024, 2048, 3072


def _gla_head(q, k, v, g, z, state_t, wg, bg, gain, mask, lo, lo_t, aq, aq_t, ak, ak_t, in_second, in_first, same):
    ga = _nn(jnp.broadcast_to(z, wg.shape[:-2] + z.shape), wg) + bg
    log_a = (jnp.minimum(ga, 0.0) - jnp.log(1.0 + jnp.exp(-jnp.abs(ga)))) * (mask * (1.0 / GLA_TAU))
    bcum = _cum(lo, lo_t, log_a)
    btot = jnp.sum(log_a, axis=-2, keepdims=True)
    qs = q * (GLA_DK ** -0.5)
    heads, levels = q.shape[0], same.shape[0]
    by_level = lambda e: jnp.exp(e.reshape(heads, levels, CHUNK, GLA_DK))
    ql = qs[:, None] * (by_level(_cum(aq, aq_t, log_a)) * in_second)
    kl = k[:, None] * (by_level(_cum(ak, ak_t, log_a)) * in_first)
    pairs = _nt(ql.reshape(heads * levels, CHUNK, GLA_DK), kl.reshape(heads * levels, CHUNK, GLA_DK))
    rows = lax.broadcasted_iota(jnp.int32, (CHUNK, CHUNK), 0)
    cols = lax.broadcasted_iota(jnp.int32, (CHUNK, CHUNK), 1)
    scores = (jnp.where(rows == cols, _nt(qs, k), 0.0)
              + jnp.sum(pairs.reshape(heads, levels, CHUNK, CHUNK) * same, axis=1))
    o = _nn(scores, v) + _nt(qs * jnp.exp(bcum), state_t)
    new_state_t = state_t * jnp.exp(btot) + _tn(v, k * jnp.exp(btot - bcum))
    return _gated_headnorm(o, g, gain), new_state_t


def _gla_consts():
    r, c = np.meshgrid(np.arange(CHUNK), np.arange(CHUNK), indexing="ij")
    aq, ak, second, first, same = [], [], [], [], []
    block = 2
    while block <= CHUNK:
        mid = (r // block) * block + block // 2
        aq.append((r >= mid) & (c > mid) & (c <= r))
        ak.append((r < mid) & (c > r) & (c <= mid))
        second.append((r >= mid)[:, :1])
        first.append((r < mid)[:, :1])
        same.append(r // block == c // block)
        block *= 2
    aq, ak = np.concatenate(aq), np.concatenate(ak)
    bf = lambda m: jnp.asarray(m, F32).astype(BF16)
    f32 = lambda ms: jnp.asarray(np.stack(ms), F32)
    return (bf(r >= c), bf(c >= r), bf(aq), bf(aq.T), bf(ak), bf(ak.T), f32(second), f32(first), f32(same))


def _gla_const_specs(consts):
    return [pl.BlockSpec(a.shape, functools.partial(lambda nd, n: (0,) * nd, a.ndim)) for a in consts]


GLA_STEP_CHUNKS = 4


def _gla_pieces(u_ref, rows):
    return (_heads(u_ref, 0, GLA_DK, rows=rows), _heads(u_ref, _GLA_K0, GLA_DK, rows=rows),
            _heads(u_ref, _GLA_V0, GLA_DV, rows=rows), _heads(u_ref, _GLA_G0, GLA_DV, rows=rows),
            u_ref[rows, _GLA_Z0:_GLA_Z0 + 128].astype(F32))


def _gla_fwd(u, wg, bg, gain, name):
    tp = u.shape[0]
    nch = tp // CHUNK
    per = GLA_STEP_CHUNKS
    consts = _gla_consts()

    def body(u_ref, wg_ref, bg_ref, gain_ref, *refs):
        const_refs, (on_ref, st_ref, state_ref) = refs[:len(consts)], refs[len(consts):]

        @pl.when(pl.program_id(0) == 0)
        def _():
            state_ref[...] = jnp.zeros_like(state_ref)

        params = (_heads(wg_ref, 0, GLA_DK), _heads(bg_ref, 0, GLA_DK), _heads(gain_ref, 0, GLA_DV))
        mats = [ref[...] for ref in const_refs]
        state = state_ref[...]
        for c in range(per):
            rows = slice(CHUNK * c, CHUNK * (c + 1))
            st_ref[c] = state.astype(BF16)
            on, state = _gla_head(*_gla_pieces(u_ref, rows), state, *params, _row_mask(pl.program_id(0) * per + c), *mats)
            _put_heads(on_ref, 0, on, 1.0, rows=rows)
        state_ref[...] = state

    rows_spec = lambda width: pl.BlockSpec((per * CHUNK, width), lambda n: (n, 0))
    full = lambda r, c: pl.BlockSpec((r, c), lambda n: (0, 0))
    return pl.pallas_call(
        body, name=name, grid=(nch // per,),
        in_specs=[rows_spec(GLA_U), full(128, HEADS * GLA_DK), full(1, HEADS * GLA_DK), full(1, HEADS * GLA_DV)]
                 + _gla_const_specs(consts),
        out_specs=[rows_spec(HEADS * GLA_DV), pl.BlockSpec((per, HEADS, GLA_DV, GLA_DK), lambda n: (n, 0, 0, 0))],
        out_shape=[jax.ShapeDtypeStruct((tp, HEADS * GLA_DV), BF16),
                   jax.ShapeDtypeStruct((nch, HEADS, GLA_DV, GLA_DK), BF16)],
        scratch_shapes=[pltpu.VMEM((HEADS, GLA_DV, GLA_DK), F32)],
        compiler_params=_cp(1))(u, wg, bg, gain, *consts)


def _gla_bwd(u, wg, bg, gain, states, d_on, name):
    tp = u.shape[0]
    per = GLA_STEP_CHUNKS
    steps = tp // (per * CHUNK)
    rev = lambda n: steps - 1 - n
    consts = _gla_consts()

    def body(u_ref, wg_ref, bg_ref, gain_ref, st_ref, don_ref, *refs):
        const_refs, (du_ref, dwg_ref, dbg_ref, dgain_ref, dstate_ref) = refs[:len(consts)], refs[len(consts):]

        @pl.when(pl.program_id(0) == 0)
        def _():
            dstate_ref[...] = jnp.zeros_like(dstate_ref)
            dwg_ref[...] = jnp.zeros_like(dwg_ref)
            dbg_ref[...] = jnp.zeros_like(dbg_ref)
            dgain_ref[...] = jnp.zeros_like(dgain_ref)

        params = (_heads(wg_ref, 0, GLA_DK), _heads(bg_ref, 0, GLA_DK), _heads(gain_ref, 0, GLA_DV))
        mats = [ref[...] for ref in const_refs]
        dstate = dstate_ref[...]
        for c in reversed(range(per)):
            rows = slice(CHUNK * c, CHUNK * (c + 1))
            mask = _row_mask(rev(pl.program_id(0)) * per + c)
            _, vjp = jax.vjp(lambda *a: _gla_head(*a, mask, *mats), *_gla_pieces(u_ref, rows),
                             st_ref[c].astype(F32), *params)
            dq, dk, dv, dg, dz, dstate, dwg, dbg, dgain = vjp((_heads(don_ref, 0, GLA_DV, rows=rows), dstate))
            for hd in range(HEADS):
                dwg_ref[:, GLA_DK * hd:GLA_DK * (hd + 1)] += dwg[hd]
                dbg_ref[:, GLA_DK * hd:GLA_DK * (hd + 1)] += dbg[hd]
                dgain_ref[:, GLA_DV * hd:GLA_DV * (hd + 1)] += dgain[hd]
            _put_heads(du_ref, 0, dq, mask, rows=rows)
            _put_heads(du_ref, _GLA_K0, dk, mask, rows=rows)
            _put_heads(du_ref, _GLA_V0, dv, mask, rows=rows)
            _put_heads(du_ref, _GLA_G0, dg, mask, rows=rows)
            du_ref[rows, _GLA_Z0:_GLA_Z0 + 128] = dz.astype(BF16)
            du_ref[rows, _GLA_Z0 + 128:] = jnp.zeros((CHUNK, GLA_U - _GLA_Z0 - 128), BF16)
        dstate_ref[...] = dstate

    full = lambda r, c: pl.BlockSpec((r, c), lambda n: (0, 0))
    return pl.pallas_call(
        body, name=name, grid=(steps,),
        in_specs=[pl.BlockSpec((per * CHUNK, GLA_U), lambda n: (rev(n), 0)), full(128, HEADS * GLA_DK),
                  full(1, HEADS * GLA_DK), full(1, HEADS * GLA_DV),
                  pl.BlockSpec((per, HEADS, GLA_DV, GLA_DK), lambda n: (rev(n), 0, 0, 0)),
                  pl.BlockSpec((per * CHUNK, HEADS * GLA_DV), lambda n: (rev(n), 0))] + _gla_const_specs(consts),
        out_specs=[pl.BlockSpec((per * CHUNK, GLA_U), lambda n: (rev(n), 0)), full(128, HEADS * GLA_DK),
                   full(1, HEADS * GLA_DK), full(1, HEADS * GLA_DV)],
        out_shape=[jax.ShapeDtypeStruct((tp, GLA_U), BF16), jax.ShapeDtypeStruct((128, HEADS * GLA_DK), F32),
                   jax.ShapeDtypeStruct((1, HEADS * GLA_DK), F32), jax.ShapeDtypeStruct((1, HEADS * GLA_DV), F32)],
        scratch_shapes=[pltpu.VMEM((HEADS, GLA_DV, GLA_DK), F32)],
        compiler_params=_cp(1))(u, wg, bg, gain, states, d_on, *consts)


def _ffn_fwd(h, gain, w_in, w_out, tag):
    hn, ug, uu, act = _norm_ffn_in(h, gain, w_in, f"{tag}_in")
    if callable(w_out):
        w_out = w_out(act)
    return _out_proj(act, w_out, h, 0.5, f"{tag}_out"), (h, hn, ug, uu, act), w_out


def _ffn_dgrad(dh, w_out, w_in, act_dg, act_du, h, gain, name, split_front=False):
    tp, d = dh.shape
    ff = w_out.shape[0]
    tm = TM_SMALL
    nt = (((1,), (1,)), ((), ()))

    def body(dh_ref, wo_ref, wi_ref, dg_ref, du_ref, h_ref, g_ref, o_ref, *out_refs):
        dhi_ref, dgain_ref = out_refs[-2:]

        @pl.when(pl.program_id(0) == 0)
        def _():
            dgain_ref[...] = jnp.zeros_like(dgain_ref)

        dho = dh_ref[...]
        dact = lax.dot_general((0.5 * dho).astype(BF16), wo_ref[...], nt, preferred_element_type=F32)
        d_gate = (dact * dg_ref[...].astype(F32)).astype(BF16)
        d_up = (dact * du_ref[...].astype(F32)).astype(BF16)
        o_ref[:, :ff] = d_gate
        o_ref[:, ff:] = d_up
        dhn = (lax.dot_general(d_gate, wi_ref[:, :ff], nt, preferred_element_type=F32)
               + lax.dot_general(d_up, wi_ref[:, ff:], nt, preferred_element_type=F32))
        dx, xhat = _rmsnorm_bwd(dhn, h_ref[...], g_ref[...])
        dgain_ref[...] += jnp.sum(dhn * xhat, axis=0, keepdims=True)
        dhi_ref[...] = dho + dx
        if split_front:
            @pl.when(pl.program_id(0) == 0)
            def _():
                out_refs[0][...] = dho + dx

    rows = lambda width: pl.BlockSpec((tm, width), lambda i: (i, 0))
    if split_front:
        assert tm == FRONT
        dhi_specs = [pl.BlockSpec((tm, d), lambda i: (0, 0)), pl.BlockSpec((tm, d), lambda i: (jnp.maximum(i - 1, 0), 0))]
        dhi_shapes = [jax.ShapeDtypeStruct((FRONT, d), F32), jax.ShapeDtypeStruct((tp - FRONT, d), F32)]
    else:
        dhi_specs, dhi_shapes = [rows(d)], [jax.ShapeDtypeStruct((tp, d), F32)]
    out = pl.pallas_call(
        body, name=name, grid=(tp // tm,),
        in_specs=[rows(d), _resident(w_out.shape, 1), _resident(w_in.shape, 1), rows(ff), rows(ff), rows(d),
                  pl.BlockSpec((1, d), lambda i: (0, 0))],
        out_specs=[rows(2 * ff), *dhi_specs, pl.BlockSpec((1, d), lambda i: (0, 0))],
        out_shape=[jax.ShapeDtypeStruct((tp, 2 * ff), BF16), *dhi_shapes, jax.ShapeDtypeStruct((1, d), F32)],
        compiler_params=_cp(1))(dh, w_out, w_in, act_dg, act_du, h, gain)
    return (out[0], tuple(out[1:3]), out[3]) if split_front else tuple(out)


def _ffn_bwd(dh, saved, gain, w_in, w_out, tag, push, split_front=False):
    h, hn, act_dg, act_du, act = saved
    du, dh_in, d_gain = _ffn_dgrad(dh, w_out, w_in, act_dg, act_du, h, gain, f"{tag}_dgrad", split_front)
    d_w_out = _wgrad(act, dh, bm=D_FF // 2, bn=D, scale=0.5, sharded=False, name=f"{tag}_dwout")
    d_w_in = _wgrad(hn, du, bm=D, bn=D_FF, scale=1.0, sharded=False, name=f"{tag}_dwin")
    return dh_in, d_gain, push([("cols", d_w_in), d_w_out])


def _sequence_grads(x, target, p, weights, grads):
    row = lambda v, token: v.reshape(1, -1) + token[0, 0]
    gains = {}

    tok = weights.start(1, weights.start(0, None))
    weights.pin = tok
    h = jnp.concatenate([jnp.zeros((FRONT, D), F32), x], axis=0) + tok[0, 0]
    rope = _rope_tables(h.shape[0])
    w = weights.wait(0, [tok, h, *rope, *weights.later_shards(2)])
    tok = weights.start(2, w["l0_ffn1_in"])
    h = lax.dynamic_update_slice(h, w["meta"], (FRONT - N_META, 0))
    gains["l0_ffn1"] = row(p["norm_ffn1"][0], tok)
    h, s1, w["l0_ffn1_out"] = _ffn_fwd(h, gains["l0_ffn1"], w["l0_ffn1_in"],
                                       lambda act: weights.wait(1, act)["l0_ffn1_out"], "l0_ffn1")
    w.update(weights.wait(2, h))
    tok = weights.start(4, weights.start(3, w["ret_in"]))
    gains["ret"] = row(p["norm_mix"][0], tok)
    hn, u = _norm_proj(h, gains["ret"], w["ret_in"], "ret_in")
    w.update(weights.wait(3, u))
    on, states, h_mix = _ret_fwd(u, w["ret_gain"], rope, h, w["ret_out"], "ret_fwd")
    s2 = (h, hn, u, on, states)
    w.update(weights.wait(4, h_mix))
    tok = weights.start(5, w["l0_ffn2_in"])
    gains["l0_ffn2"] = row(p["norm_ffn2"][0], tok)
    h, s3, _ = _ffn_fwd(h_mix, gains["l0_ffn2"], w["l0_ffn2_in"], w["l0_ffn2_out"], "l0_ffn2")
    saved = [(s1, s2, s3)]

    w.update(weights.wait(5, h))
    tok = weights.start(6, w["l1_ffn1_in"])
    gains["l1_ffn1"] = row(p["norm_ffn1"][1], tok)
    h, s1, _ = _ffn_fwd(h, gains["l1_ffn1"], w["l1_ffn1_in"], w["l1_ffn1_out"], "l1_ffn1")
    w.update(weights.wait(6, h))
    tok = weights.start(7, w["gla_out"])
    gains["gla"] = row(p["norm_mix"][1], tok)
    hn, u = _norm_proj(h, gains["gla"], w["gla_in"], "gla_in")
    on, states = _gla_fwd(u, w["gla_wg"], w["gla_bg"], w["gla_gain"], "gla_fwd")
    h_mix = _out_proj(on, w["gla_out"], h, 1.0, "gla_out")
    s2 = (h, hn, u, on, states)
    w.update(weights.wait(7, h_mix))
    gains["l1_ffn2"] = p["norm_ffn2"][1].reshape(1, -1)
    h, s3, _ = _ffn_fwd(h_mix, gains["l1_ffn2"], w["l1_ffn2_in"], w["l1_ffn2_out"], "l1_ffn2")
    saved.append((s1, s2, s3))

    dh, d_final, loss = _loss_head(h, p["final_norm"].reshape(1, -1), target, "loss_head")
    small = {"final_norm": d_final, "norm_ffn1": [None, None], "norm_mix": [None, None], "norm_ffn2": [None, None]}
    pusher = lambda k: functools.partial(grads.push, k)

    s1, s2, s3 = saved[1]
    dh, small["norm_ffn2"][1], tok = _ffn_bwd(dh, s3, gains["l1_ffn2"], w["l1_ffn2_in"], w["l1_ffn2_out"], "l1_ffn2",
                                              pusher(0))
    h_in, hn, u, on, states = s2
    d_on = _dgrad(dh, w["gla_out"], "gla_don")
    d_out = _wgrad(on, dh, bm=D, bn=D, scale=1.0, sharded=False, name="gla_dwout")
    du, small["gla_wg"], small["gla_bg"], small["gla_gain"] = _gla_bwd(
        u, w["gla_wg"], w["gla_bg"], w["gla_gain"] + tok[0, 0], states, d_on, "gla_bwd")
    d_in = _wgrad(hn, du, bm=D, bn=GLA_U, scale=1.0, sharded=False, name="gla_dwin")
    d_in = jnp.moveaxis(d_in[:, :GLA_IN].reshape(D, N_CHIPS, -1), 1, 0)
    tok = grads.push(1, [d_in, d_out])
    dh, small["norm_mix"][1] = _dgrad_norm(du, w["gla_in"], h_in, gains["gla"] + tok[0, 0], dh, "gla_dnorm")
    dh, small["norm_ffn1"][1], tok = _ffn_bwd(dh, s1, gains["l1_ffn1"], w["l1_ffn1_in"], w["l1_ffn1_out"], "l1_ffn1",
                                              pusher(2))

    s1, s2, s3 = saved[0]
    dh, small["norm_ffn2"][0], tok = _ffn_bwd(dh, s3, gains["l0_ffn2"] + tok[0, 0], w["l0_ffn2_in"],
                                              w["l0_ffn2_out"], "l0_ffn2", pusher(3))
    h_in, hn, u, on, states = s2
    d_on = _dgrad(dh, w["ret_out"], "ret_don")
    d_out = _wgrad(on, dh, bm=D, bn=D, scale=1.0, sharded=False, name="ret_dwout")
    du, small["ret_gain"] = _ret_bwd(u, w["ret_gain"] + tok[0, 0], rope, states, d_on, "ret_bwd")
    d_in = _wgrad(hn, du, bm=D, bn=w["ret_in"].shape[2], scale=1.0, sharded=True, name="ret_dwin")
    tok = grads.push(4, [d_in, d_out])
    dh, small["norm_mix"][0] = _dgrad_norm(du, w["ret_in"], h_in, gains["ret"] + tok[0, 0], dh, "ret_dnorm")
    (d_front, d_x), small["norm_ffn1"][0], tok = _ffn_bwd(dh, s1, gains["l0_ffn1"], w["l0_ffn1_in"], w["l0_ffn1_out"],
                                                          "l0_ffn1", pusher(5), split_front=True)
    grads.push(6, [], [d_front[FRONT - N_META:], *small["norm_ffn1"], *small["norm_mix"], *small["norm_ffn2"],
                       small["final_norm"], small["ret_gain"], small["gla_wg"][:GLA_RANK], small["gla_bg"],
                       small["gla_gain"], loss[:, :1] + tok[0, 0]])
    return d_x


_HBM = pl.BlockSpec(memory_space=pl.ANY)


def _place():
    return lax.axis_index("x"), lax.axis_index("y"), lax.axis_index("c")


def _flip(v, bit):
    return 1 - v if bit else v


DMA_CHUNK_BYTES = 128 * 1024


def _row_chunks(ref):
    rows, cols = ref.shape
    step = _row_tile(rows, max(16, DMA_CHUNK_BYTES // (cols * ref.dtype.itemsize)))
    return [pl.ds(a, step) for a in range(0, rows, step)]


def _whole(src, dst, send_sem, recv_sem, peer):
    return pltpu.make_async_remote_copy(src_ref=src, dst_ref=dst, send_sem=send_sem, recv_sem=recv_sem,
                                        device_id=peer, device_id_type=MESH)


def _send(src, dst, send_sem, recv_sem, peer):
    for rows in _row_chunks(src):
        _whole(src.at[rows], dst.at[rows], send_sem, recv_sem, peer).start()
    return _whole(src, dst, send_sem, recv_sem, peer)


_HBM_ONLY = pl.BlockSpec(memory_space=pltpu.HBM)
_SEMS = pl.BlockSpec(memory_space=pltpu.SEMAPHORE)
_SIDE_EFFECT = pltpu.CompilerParams(has_side_effects=pltpu.SideEffectType.DATAFLOW_SIDE_EFFECTING)
_GATHER_FLIPS = [(1, 0, 0), (0, 1, 0), (1, 1, 0), (0, 0, 1)]
_PEER_FLIPS = [(fx, fy, fc) for fx in (0, 1) for fy in (0, 1) for fc in (0, 1)][1:]


def _zero_token():
    return jnp.zeros((8, 128), F32)


def _exchange_start(srcs, lands, route, flips, after, name):
    n = len(srcs)

    def body(*refs):
        src, land = refs[:n], refs[n:2 * n]
        send_sems, recv_sems, token = refs[2 * n + 1], refs[2 * n + 2], refs[-1]
        me = _place()
        for t in range(n):
            for j, flip in enumerate(flips):
                peer = tuple(_flip(v, f) for v, f in zip(me, flip))
                s, d = route(t, src[t], land[t], me, peer)
                _send(s, d, send_sems.at[t * len(flips) + j], recv_sems.at[t * len(flips) + j], peer)
        token[...] = jnp.zeros_like(token)

    hbm = lambda a: pltpu.HBM(a.shape, a.dtype)
    sems = pltpu.SemaphoreType.DMA((n * len(flips),))
    operands = [pltpu.with_memory_space_constraint(a, pltpu.HBM) for a in list(srcs) + list(lands)]
    out = pl.pallas_call(
        body, name=name, in_specs=[_HBM_ONLY] * (2 * n) + [_HBM],
        out_shape=(sems, sems, *[hbm(a) for a in operands], jax.ShapeDtypeStruct((8, 128), F32)),
        out_specs=(_SEMS, _SEMS, *[_HBM_ONLY] * (2 * n), pl.BlockSpec(memory_space=pltpu.VMEM)),
        input_output_aliases={i: 2 + i for i in range(2 * n)}, compiler_params=_SIDE_EFFECT,
    )(*operands, _zero_token() if after is None else after)
    return (out[0], out[1], out[2:2 + n], out[2 + n:2 + 2 * n]), out[-1]


def _exchange_wait(started, route, flips, after, name):
    send_sems, recv_sems, srcs, lands = started
    n = len(srcs)

    def body(*refs):
        src, land = refs[:n], refs[n:2 * n]
        send_sems, recv_sems = refs[2 * n], refs[2 * n + 1]
        me = _place()
        for t in range(n):
            for j, flip in enumerate(flips):
                peer = tuple(_flip(v, f) for v, f in zip(me, flip))
                s, d = route(t, src[t], land[t], me, peer)
                cp = _whole(s, d, send_sems.at[t * len(flips) + j], recv_sems.at[t * len(flips) + j], peer)
                cp.wait_send()
                cp.wait_recv()

    hbm = lambda a: pltpu.HBM(a.shape, a.dtype)
    after = list(after) if isinstance(after, (list, tuple)) else [after]
    out = pl.pallas_call(
        body, name=name, in_specs=[_HBM_ONLY] * (2 * n) + [_SEMS, _SEMS] + [_HBM] * len(after),
        out_shape=tuple(hbm(a) for a in list(srcs) + list(lands)), out_specs=tuple([_HBM_ONLY] * (2 * n)),
        input_output_aliases={i: i for i in range(2 * n)}, compiler_params=_SIDE_EFFECT,
    )(*srcs, *lands, send_sems, recv_sems, *after)
    return out[:n], out[n:]


def _gather_route(t, src, land, me, peer):
    mine = 2 * me[0] + me[1]
    if land.ndim == 3:
        return src, land.at[mine]
    cols = src.shape[1]
    return src, land.at[:, pl.ds(pl.multiple_of(mine * cols, 128), cols)]


def _scatter_route(n_pieces):
    def route(t, src, land, me, peer):
        chip = 2 * peer[0] + peer[1]
        if t >= n_pieces:
            part = src
        elif src.ndim == 4:
            part = src.at[chip, peer[2]]
        else:
            rows, cols = land.shape[1:]
            part = src.at[pl.ds(pl.multiple_of(peer[2] * rows, 16), rows), pl.ds(pl.multiple_of(chip * cols, 128), cols)]
        return part, land.at[4 * me[0] + 2 * me[1] + me[2]]

    return route


def _swap_cores(halves, name):
    n = len(halves)

    def body(*refs):
        src, dst = refs[:n], refs[n:2 * n]
        send_sems, recv_sems = refs[2 * n:]
        x, y, c = _place()
        copies = [_send(src[t], dst[t], send_sems.at[t], recv_sems.at[t], (x, y, 1 - c)) for t in range(n)]
        for cp in copies:
            cp.wait()

    got = pl.pallas_call(
        body, name=name, in_specs=[_HBM] * n, out_specs=[_HBM] * n,
        out_shape=[jax.ShapeDtypeStruct(a.shape, a.dtype) for a in halves],
        scratch_shapes=[pltpu.SemaphoreType.DMA((n,)), pltpu.SemaphoreType.DMA((n,))],
    )(*halves)
    south = lax.axis_index("c") == 0
    return [jnp.stack([jnp.where(south, a, b), jnp.where(south, b, a)]) for a, b in zip(halves, got)]


def _row_tile(rows, cap):
    fits = [t for t in range(16, cap + 1, 16) if rows % t == 0]
    return fits[-1] if fits else rows


def _sum_slots(a, name):
    _, r, c = a.shape
    tr = _row_tile(r, 384)

    def body(a_ref, o_ref):
        s = a_ref[0].astype(F32)
        for k in range(1, N_DEV):
            s = s + a_ref[k].astype(F32)
        o_ref[...] = s

    return pl.pallas_call(
        body, name=name, grid=(r // tr,),
        in_specs=[pl.BlockSpec((N_DEV, tr, c), lambda i: (0, i, 0))],
        out_specs=pl.BlockSpec((tr, c), lambda i: (i, 0)),
        out_shape=jax.ShapeDtypeStruct((r, c), F32),
        compiler_params=_cp(1))(a)


def _adamw(w, g, m, v, name):
    layers, r, c = w.shape
    tr = _row_tile(r, 256)

    def body(w_ref, g_ref, m_ref, v_ref, d_ref, nm_ref, nv_ref):
        gv = g_ref[...]
        nm = ADAM_B1 * m_ref[...] + (1.0 - ADAM_B1) * gv
        nv = ADAM_B2 * v_ref[...] + (1.0 - ADAM_B2) * (gv * gv)
        m_hat = nm / (1.0 - ADAM_B1 ** ADAM_STEP)
        v_hat = nv / (1.0 - ADAM_B2 ** ADAM_STEP)
        d_ref[...] = -ADAM_LR * (m_hat / (jnp.sqrt(v_hat) + ADAM_EPS) + ADAM_WD * w_ref[...])
        nm_ref[...] = nm
        nv_ref[...] = nv

    spec = pl.BlockSpec((None, tr, c), lambda a, i: (a, i, 0))
    return pl.pallas_call(
        body, name=name, grid=(layers, r // tr), in_specs=[spec] * 4, out_specs=[spec] * 3,
        out_shape=[jax.ShapeDtypeStruct((layers, r, c), F32)] * 3,
        compiler_params=_cp(2))(*[pltpu.with_memory_space_constraint(a, pltpu.HBM) for a in (w, g, m, v)])


_SMALL = ["meta_tokens", "ret_head_norm", "gla_w_gate", "gla_b_gate", "gla_head_norm"]
_LOCAL_SMALL = ["meta_tokens", "norm_ffn1", "norm_mix", "norm_ffn2", "ret_head_norm", "gla_w_gate", "gla_b_gate",
                "gla_head_norm", "final_norm"]
_WEIGHTS = ["meta_tokens", "norm_ffn1", "ffn1_w_in", "ffn1_w_out", "norm_mix", "norm_ffn2", "ffn2_w_in", "ffn2_w_out",
            "ret_w_in", "ret_head_norm", "ret_w_out", "gla_w_in", "gla_w_gate", "gla_b_gate", "gla_head_norm",
            "gla_w_out", "final_norm"]


def _pack_rows(arrays, width):
    flat = jnp.concatenate([a.reshape(-1) for a in arrays])
    pad = -flat.shape[0] % (8 * width)
    return jnp.pad(flat, (0, pad)).reshape(-1, width)


def _unpack_rows(packed, shapes):
    flat, out, at = packed.reshape(-1), [], 0
    for s in shapes:
        size = 1
        for dim in s:
            size *= dim
        out.append(flat[at:at + size].reshape(s))
        at += size
    return out


class _WeightGather:
    GROUPS = [("small", "l0_ffn1_in"), ("l0_ffn1_out",), ("ret_in",), ("ret_out",), ("l0_ffn2_in", "l0_ffn2_out"),
              ("l1_ffn1_in", "l1_ffn1_out"), ("gla_in", "gla_out"), ("l1_ffn2_in", "l1_ffn2_out")]

    def __init__(self, p):
        self.small_shapes = [p[name].shape for name in _SMALL]
        self.f32 = {"small": _pack_rows([p[name] for name in _SMALL], 128), "ret_in": p["ret_w_in"][0],
                    "ret_out": p["ret_w_out"][0], "gla_in": p["gla_w_in"][0], "gla_out": p["gla_w_out"][0]}
        for layer in range(2):
            for name in ("ffn1", "ffn2"):
                self.f32[f"l{layer}_{name}_in"] = p[f"{name}_w_in"][layer]
                self.f32[f"l{layer}_{name}_out"] = p[f"{name}_w_out"][layer]
        self.shards = {}
        self.started = {}
        self.pin = None

    def shard(self, name):
        if name not in self.shards:
            a = self.f32[name]
            if name != "small":
                a = (a if self.pin is None else a + self.pin[0, 0]).astype(BF16)
            self.shards[name] = a
        return self.shards[name]

    def later_shards(self, k):
        return [self.shard(name) for group in self.GROUPS[k:] for name in group]

    def start(self, k, after):
        shards = [self.shard(name) for name in self.GROUPS[k]]
        lands = []
        for name, s in zip(self.GROUPS[k], shards):
            if "ffn" in name and name.endswith("_in"):
                lands.append(lax.empty((s.shape[0], N_CHIPS * s.shape[1]), s.dtype))
            else:
                lands.append(lax.empty((N_CHIPS,) + s.shape, s.dtype))
        self.started[k], token = _exchange_start(shards, lands, _gather_route, _GATHER_FLIPS, after, f"gather{k}_start")
        return token

    def wait(self, k, after):
        _, got = _exchange_wait(self.started[k], _gather_route, _GATHER_FLIPS, after, f"gather{k}_wait")
        w = {}
        for name, g in zip(self.GROUPS[k], got):
            if name == "small":
                parts = zip(*[_unpack_rows(g[chip], self.small_shapes) for chip in range(N_CHIPS)])
                cat = lambda a: jnp.moveaxis(a, 0, -2).reshape(a.shape[1:-1] + (-1,))
                meta, ret_gain, wg, bg, gla_gain = [cat(jnp.stack(part)) for part in parts]
                w.update(meta=meta, ret_gain=ret_gain.reshape(1, -1), gla_bg=bg.reshape(1, -1),
                         gla_gain=gla_gain.reshape(1, -1),
                         gla_wg=jnp.pad(wg[0], ((0, 128 - GLA_RANK), (0, 0))).astype(BF16))
            elif name == "gla_in":
                full = jnp.moveaxis(g, 0, 1).reshape(D, -1)
                w[name] = jnp.pad(full, ((0, 0), (0, GLA_U - GLA_IN)))[None]
            elif name.endswith("_out"):
                w[name] = g.reshape(-1, g.shape[-1])
            else:
                w[name] = g
        return w


class _GradExchange:
    def __init__(self):
        self.started = []
        self.token = None
        self.small_shapes = None

    def push(self, k, arrays, small=None):
        srcs, lands = [], []
        for a in arrays:
            if isinstance(a, tuple):
                a = a[1]
                piece = (a.shape[0] // 2, a.shape[1] // N_CHIPS)
            else:
                a = a.reshape(N_CHIPS, 2, -1, a.shape[-1])
                piece = a.shape[2:]
            srcs.append(a)
            lands.append(lax.empty((N_DEV,) + piece, a.dtype))
        if small is not None:
            self.small_shapes = [a.shape for a in small]
            srcs.append(_pack_rows(small, D))
            lands.append(lax.empty((N_DEV,) + srcs[-1].shape, F32))
        started, self.token = _exchange_start(srcs, lands, _scatter_route(len(arrays)), _PEER_FLIPS, None,
                                              f"scatter{k}_start")
        self.started.append((started, len(arrays)))
        return self.token

    def collect(self, groups, after=None):
        x, y, c = _place()
        after, sums = self.token if after is None else after, []
        for k in groups:
            started, n_pieces = self.started[k]
            srcs, got = _exchange_wait(started, _scatter_route(n_pieces), _PEER_FLIPS, after, f"scatter{k}_wait")
            own = []
            for t, (a, g) in enumerate(zip(srcs, got)):
                if t >= n_pieces:
                    own.append(a)
                elif a.ndim == 4:
                    own.append(a[2 * x + y, c])
                else:
                    rows, cols = g.shape[1:]
                    own.append(lax.dynamic_slice(a, (c * rows, (2 * x + y) * cols), (rows, cols)))
            got = [lax.dynamic_update_index_in_dim(g, a, 4 * x + 2 * y + c, 0) for g, a in zip(got, own)]
            sums.append([_sum_slots(a, f"sum{k}_{i}") for i, a in enumerate(got)])
            after = sums[-1][0]
        return sums


def kernel(x, meta_tokens, norm_ffn1, ffn1_w_in, ffn1_w_out, norm_mix, norm_ffn2, ffn2_w_in, ffn2_w_out, ret_w_in, ret_head_norm, ret_w_out, gla_w_in, gla_w_gate, gla_b_gate, gla_head_norm, gla_w_out, final_norm, loss_target, m_meta_tokens, m_norm_ffn1, m_ffn1_w_in, m_ffn1_w_out, m_norm_mix, m_norm_ffn2, m_ffn2_w_in, m_ffn2_w_out, m_ret_w_in, m_ret_head_norm, m_ret_w_out, m_gla_w_in, m_gla_w_gate, m_gla_b_gate, m_gla_head_norm, m_gla_w_out, m_final_norm, v_meta_tokens, v_norm_ffn1, v_ffn1_w_in, v_ffn1_w_out, v_norm_mix, v_norm_ffn2, v_ffn2_w_in, v_ffn2_w_out, v_ret_w_in, v_ret_head_norm, v_ret_w_out, v_gla_w_in, v_gla_w_gate, v_gla_b_gate, v_gla_head_norm, v_gla_w_out, v_final_norm):
    p = dict(meta_tokens=meta_tokens, norm_ffn1=norm_ffn1, ffn1_w_in=ffn1_w_in, ffn1_w_out=ffn1_w_out, norm_mix=norm_mix,
             norm_ffn2=norm_ffn2, ffn2_w_in=ffn2_w_in, ffn2_w_out=ffn2_w_out, ret_w_in=ret_w_in,
             ret_head_norm=ret_head_norm, ret_w_out=ret_w_out, gla_w_in=gla_w_in, gla_w_gate=gla_w_gate,
             gla_b_gate=gla_b_gate, gla_head_norm=gla_head_norm, gla_w_out=gla_w_out, final_norm=final_norm)
    m = dict(zip(_WEIGHTS, (m_meta_tokens, m_norm_ffn1, m_ffn1_w_in, m_ffn1_w_out, m_norm_mix, m_norm_ffn2, m_ffn2_w_in,
                            m_ffn2_w_out, m_ret_w_in, m_ret_head_norm, m_ret_w_out, m_gla_w_in, m_gla_w_gate,
                            m_gla_b_gate, m_gla_head_norm, m_gla_w_out, m_final_norm)))
    v = dict(zip(_WEIGHTS, (v_meta_tokens, v_norm_ffn1, v_ffn1_w_in, v_ffn1_w_out, v_norm_mix, v_norm_ffn2, v_ffn2_w_in,
                            v_ffn2_w_out, v_ret_w_in, v_ret_head_norm, v_ret_w_out, v_gla_w_in, v_gla_w_gate,
                            v_gla_b_gate, v_gla_head_norm, v_gla_w_out, v_final_norm)))

    exchange = _GradExchange()
    d_x = _sequence_grads(x[0], loss_target[0], p, _WeightGather(p), exchange)
    names = [("ffn2_w_in", 1), ("ffn2_w_out", 1), ("gla_w_in", 0), ("gla_w_out", 0), ("ffn1_w_in", 1), ("ffn1_w_out", 1),
             ("ffn2_w_in", 0), ("ffn2_w_out", 0), ("ret_w_in", 0), ("ret_w_out", 0), ("ffn1_w_in", 0), ("ffn1_w_out", 0)]
    shard, grads, delta, new_m, new_v = {}, {}, {}, {}, {}

    def swap(sums, keys, name):
        for key, a in zip(keys, _swap_cores(sums, name)):
            shard[key] = a.reshape(-1, a.shape[-1])

    def update(name):
        layers = p[name].shape[0]
        grads[name] = jnp.stack([shard[name, layer] for layer in range(layers)])
        delta[name], new_m[name], new_v[name] = _adamw(p[name], grads[name], m[name], v[name], f"adamw_{name}")

    swap([a for group in exchange.collect(range(5)) for a in group], names[:10], "swap_first")
    for name in ("ffn2_w_in", "ffn2_w_out", "ret_w_in", "ret_w_out", "gla_w_in", "gla_w_out"):
        update(name)
    last, (small_sum,) = exchange.collect([5, 6], after=list(delta.values()))
    swap(last, names[10:], "swap_last")
    for name in ("ffn1_w_in", "ffn1_w_out"):
        update(name)

    chip = 2 * lax.axis_index("x") + lax.axis_index("y")
    cols = lambda a, n: lax.dynamic_slice_in_dim(a, chip * n, n, axis=a.ndim - 1)
    (s_meta, s_n1a, s_n1b, s_nma, s_nmb, s_n2a, s_n2b, s_final, s_ret_gain, s_wg, s_bg, s_gla_gain,
     s_loss) = _unpack_rows(small_sum, exchange.small_shapes)
    grads.update({
        "meta_tokens": cols(s_meta, 256), "norm_ffn1": jnp.concatenate([s_n1a, s_n1b]),
        "norm_mix": jnp.concatenate([s_nma, s_nmb]), "norm_ffn2": jnp.concatenate([s_n2a, s_n2b]),
        "final_norm": s_final.reshape(D),
        "ret_head_norm": cols(s_ret_gain.reshape(1, HEADS, RET_DV), RET_DV // N_CHIPS),
        "gla_w_gate": cols(s_wg, GLA_DK)[None], "gla_b_gate": cols(s_bg, GLA_DK),
        "gla_head_norm": cols(s_gla_gain.reshape(1, HEADS, GLA_DV), GLA_DV // N_CHIPS),
    })
    for name in _LOCAL_SMALL:
        shape = p[name].shape
        as3d = lambda a: a.reshape((1,) * (3 - len(shape)) + shape)
        out = _adamw(as3d(p[name]), as3d(grads[name]), as3d(m[name]), as3d(v[name]), f"adamw_{name}")
        delta[name], new_m[name], new_v[name] = [a.reshape(shape) for a in out]

    return (s_loss.reshape(()), d_x[None], *[grads[n] for n in _WEIGHTS], *[delta[n] for n in _WEIGHTS],
            *[new_m[n] for n in _WEIGHTS], *[new_v[n] for n in _WEIGHTS])
```

```python
import functools

import jax
import numpy as np
import jax.numpy as jnp
from jax import lax
from jax.experimental import pallas as pl
from jax.experimental.pallas import tpu as pltpu

F32, BF16 = jnp.float32, jnp.bfloat16
MESH = pl.DeviceIdType.MESH

D = 1024
N_META = 16
CHUNK = 64
RET_CHUNK = 256
FRONT = 256
D_FF = 2816
EPS = 1e-6
HEADS = 4
RET_DK, RET_DV = 256, 512
GLA_DK, GLA_DV = 128, 256
GLA_RANK = 16
GLA_TAU = 16.0
GLA_IN = 2 * HEADS * GLA_DK + 2 * HEADS * GLA_DV + GLA_RANK
GLA_U = 3328
ROPE_BASE = 10000.0
N_CHIPS = 4
N_DEV = 8

ADAM_LR, ADAM_B1, ADAM_B2, ADAM_EPS, ADAM_WD, ADAM_STEP = 0.001, 0.9, 0.999, 1e-08, 0.01, 10

VMEM_LIMIT_BYTES = 56 * 1024 * 1024
TM = 768
TM_SMALL = 256


TM_RESIDENT = 384
MXU_TILE = 256


def _cp(n_axes):
    return pltpu.CompilerParams(dimension_semantics=("arbitrary",) * n_axes, vmem_limit_bytes=VMEM_LIMIT_BYTES)


def _resident(shape, n_axes):
    zeros = (0,) * len(shape)
    index = (lambda i: zeros) if n_axes == 1 else (lambda i, j: zeros)
    return pl.BlockSpec(shape, index, pipeline_mode=pl.Buffered(1))


def _dg(a, b, ca, cb):
    nb = a.ndim - 2
    dims = (((ca + nb,), (cb + nb,)), (tuple(range(nb)), tuple(range(nb))))
    return lax.dot_general(a.astype(BF16), b.astype(BF16), dims, preferred_element_type=F32)


@jax.custom_vjp
def _nn(a, b):
    return _dg(a, b, 1, 0)


@jax.custom_vjp
def _nt(a, b):
    return _dg(a, b, 1, 1)


@jax.custom_vjp
def _tn(a, b):
    return _dg(a, b, 0, 0)


def _dot_vjp(fn, ca, cb, da, db):
    def fwd(a, b):
        a, b = a.astype(BF16), b.astype(BF16)
        return _dg(a, b, ca, cb), (a, b)

    def bwd(res, g):
        a, b = res
        g = g.astype(BF16)
        grad = lambda other, dims, g_first: _dg(g, other, *dims) if g_first else _dg(other, g, *dims)
        return grad(b, *da), grad(a, *db)

    fn.defvjp(fwd, bwd)


_dot_vjp(_nn, 1, 0, ((1, 1), True), ((0, 0), False))
_dot_vjp(_nt, 1, 1, ((1, 0), True), ((0, 0), True))
_dot_vjp(_tn, 0, 0, ((1, 1), False), ((1, 0), False))


def _split_dot(m, a, parts):
    mb = jnp.broadcast_to(m, a.shape[:-2] + m.shape)
    total, rest = None, a
    for _ in range(parts):
        term = rest.astype(BF16)
        rest = rest - term.astype(F32)
        product = _dg(mb, term, 1, 0)
        total = product if total is None else total + product
    return total


def _make_cum(parts):
    @jax.custom_vjp
    def cum(m, mt, a):
        return _split_dot(m, a, parts)

    cum.defvjp(lambda m, mt, a: (_split_dot(m, a, parts), (m, mt)),
               lambda res, g: (jnp.zeros_like(res[0]), jnp.zeros_like(res[1]), _split_dot(res[1], g, parts)))
    return cum


_cum = _make_cum(3)
_cum16 = _make_cum(2)


def _sigmoid(x):
    return 1.0 / (1.0 + jnp.exp(-x))


def _rms(x):
    return lax.rsqrt(jnp.mean(x * x, axis=-1, keepdims=True) + EPS)


def _rmsnorm_bwd(dy, x, gain):
    r = _rms(x)
    xhat = x * r
    dxh = dy * gain
    return r * (dxh - xhat * jnp.mean(dxh * xhat, axis=-1, keepdims=True)), xhat


def _norm_proj(h, gain, w, name):
    tp, d = h.shape
    s, _, ns = w.shape

    tm = TM_RESIDENT

    def body(h_ref, g_ref, w_ref, hn_ref, u_ref):
        x = h_ref[...]
        a = (x * _rms(x) * g_ref[...]).astype(BF16)
        hn_ref[...] = a
        for k in range(s):
            u_ref[:, ns * k:ns * (k + 1)] = jnp.dot(a, w_ref[k], preferred_element_type=F32).astype(BF16)

    return pl.pallas_call(
        body, name=name, grid=(tp // tm,),
        in_specs=[pl.BlockSpec((tm, d), lambda i: (i, 0)), pl.BlockSpec((1, d), lambda i: (0, 0)), _resident(w.shape, 1)],
        out_specs=[pl.BlockSpec((tm, d), lambda i: (i, 0)), pl.BlockSpec((tm, s * ns), lambda i: (i, 0))],
        out_shape=[jax.ShapeDtypeStruct((tp, d), BF16), jax.ShapeDtypeStruct((tp, s * ns), BF16)],
        compiler_params=_cp(1))(h, gain, w)


def _norm_ffn_in(h, gain, w, name):
    tp, d = h.shape
    ff = w.shape[1] // 2
    tm = TM_RESIDENT
    blocks = [(c, min(c + 6 * MXU_TILE, ff)) for c in range(0, ff, 6 * MXU_TILE)]

    def body(h_ref, g_ref, w_ref, hn_ref, dg_ref, du_ref, act_ref):
        x = h_ref[...]
        a = (x * _rms(x) * g_ref[...]).astype(BF16)
        hn_ref[...] = a
        for c0, c1 in blocks:
            g = jnp.dot(a, w_ref[:, c0:c1], preferred_element_type=F32)
            u = jnp.dot(a, w_ref[:, ff + c0:ff + c1], preferred_element_type=F32)
            sg = _sigmoid(g)
            silu = g * sg
            dg_ref[:, c0:c1] = (u * (sg + silu * (1.0 - sg))).astype(BF16)
            du_ref[:, c0:c1] = silu.astype(BF16)
            act_ref[:, c0:c1] = (silu * u).astype(BF16)

    wide = jax.ShapeDtypeStruct((tp, ff), BF16)
    return pl.pallas_call(
        body, name=name, grid=(tp // tm,),
        in_specs=[pl.BlockSpec((tm, d), lambda i: (i, 0)), pl.BlockSpec((1, d), lambda i: (0, 0)),
                  _resident(w.shape, 1)],
        out_specs=[pl.BlockSpec((tm, d), lambda i: (i, 0))] + [pl.BlockSpec((tm, ff), lambda i: (i, 0))] * 3,
        out_shape=[jax.ShapeDtypeStruct((tp, d), BF16), wide, wide, wide],
        compiler_params=_cp(1))(h, gain, w)


def _out_proj(a, w, h, scale, name):
    tp, k = a.shape
    d = w.shape[1]

    def body(a_ref, w_ref, h_ref, o_ref):
        o_ref[...] = h_ref[...] + scale * jnp.dot(a_ref[...], w_ref[...], preferred_element_type=F32)

    return pl.pallas_call(
        body, name=name, grid=(tp // TM,),
        in_specs=[pl.BlockSpec((TM, k), lambda i: (i, 0)), pl.BlockSpec((k, d), lambda i: (0, 0)),
                  pl.BlockSpec((TM, d), lambda i: (i, 0))],
        out_specs=pl.BlockSpec((TM, d), lambda i: (i, 0)),
        out_shape=jax.ShapeDtypeStruct((tp, d), F32),
        compiler_params=_cp(1))(a, w, h)


def _dgrad(dh, w, name):
    tp, d = dh.shape
    k = w.shape[0]

    def body(dh_ref, w_ref, o_ref):
        o_ref[...] = lax.dot_general(dh_ref[...].astype(BF16), w_ref[...], (((1,), (1,)), ((), ())),
                                     preferred_element_type=F32).astype(BF16)

    return pl.pallas_call(
        body, name=name, grid=(tp // TM,),
        in_specs=[pl.BlockSpec((TM, d), lambda i: (i, 0)), pl.BlockSpec((k, d), lambda i: (0, 0))],
        out_specs=pl.BlockSpec((TM, k), lambda i: (i, 0)),
        out_shape=jax.ShapeDtypeStruct((tp, k), BF16),
        compiler_params=_cp(1))(dh, w)


def _wgrad(a, b, *, bm, bn, scale, sharded, name):
    tp, m = a.shape
    n = b.shape[1]
    nk = tp // TM

    def body(a_ref, b_ref, o_ref, acc_ref):
        k = pl.program_id(2)

        @pl.when(k == 0)
        def _():
            acc_ref[...] = jnp.zeros_like(acc_ref)

        bb = b_ref[...]
        if scale != 1.0:
            bb = scale * bb
        acc_ref[...] += lax.dot_general(a_ref[...], bb.astype(BF16), (((0,), (0,)), ((), ())),
                                        preferred_element_type=F32)

        @pl.when(k == nk - 1)
        def _():
            o_ref[...] = acc_ref[...].astype(BF16)

    if sharded:
        assert m == bm
        out_spec = pl.BlockSpec((None, bm, bn), lambda i, j, k: (j, 0, 0))
        out_shape = jax.ShapeDtypeStruct((n // bn, m, bn), BF16)
    else:
        out_spec = pl.BlockSpec((bm, bn), lambda i, j, k: (i, j))
        out_shape = jax.ShapeDtypeStruct((m, n), BF16)
    return pl.pallas_call(
        body, name=name, grid=(m // bm, n // bn, nk),
        in_specs=[pl.BlockSpec((TM, bm), lambda i, j, k: (k, i)), pl.BlockSpec((TM, bn), lambda i, j, k: (k, j))],
        out_specs=out_spec, out_shape=out_shape,
        scratch_shapes=[pltpu.VMEM((bm, bn), F32)],
        compiler_params=_cp(3))(a, b)


def _dgrad_norm(du, w, h, gain, dh_out, name):
    tp, d = h.shape
    s, _, ns = w.shape
    tm = TM_RESIDENT

    def body(du_ref, w_ref, h_ref, g_ref, dho_ref, dhi_ref, dg_ref):
        @pl.when(pl.program_id(0) == 0)
        def _():
            dg_ref[...] = jnp.zeros_like(dg_ref)

        dhn = None
        for k in range(s):
            part = lax.dot_general(du_ref[:, ns * k:ns * (k + 1)], w_ref[k], (((1,), (1,)), ((), ())),
                                   preferred_element_type=F32)
            dhn = part if dhn is None else dhn + part
        dx, xhat = _rmsnorm_bwd(dhn, h_ref[...], g_ref[...])
        dg_ref[...] += jnp.sum(dhn * xhat, axis=0, keepdims=True)
        dhi_ref[...] = dho_ref[...] + dx

    return pl.pallas_call(
        body, name=name, grid=(tp // tm,),
        in_specs=[pl.BlockSpec((tm, s * ns), lambda i: (i, 0)), _resident(w.shape, 1),
                  pl.BlockSpec((tm, d), lambda i: (i, 0)), pl.BlockSpec((1, d), lambda i: (0, 0)),
                  pl.BlockSpec((tm, d), lambda i: (i, 0))],
        out_specs=[pl.BlockSpec((tm, d), lambda i: (i, 0)), pl.BlockSpec((1, d), lambda i: (0, 0))],
        out_shape=[jax.ShapeDtypeStruct((tp, d), F32), jax.ShapeDtypeStruct((1, d), F32)],
        compiler_params=_cp(1))(du, w, h, gain, dh_out)


def _loss_head(h, gain, target, name):
    tp, d = h.shape
    tm = TM_SMALL
    front_tiles = FRONT // tm

    def body(h_ref, g_ref, t_ref, dh_ref, dg_ref, loss_ref):
        i = pl.program_id(0)

        @pl.when(i == 0)
        def _():
            dg_ref[...] = jnp.zeros_like(dg_ref)
            loss_ref[...] = jnp.zeros_like(loss_ref)

        x = h_ref[...]
        gain_v = g_ref[...]
        y = x * _rms(x) * gain_v
        err = jnp.where(i >= front_tiles, y - t_ref[...], 0.0)
        loss_ref[...] += 0.5 * jnp.sum(jnp.mean(err * err, axis=-1, keepdims=True), axis=0, keepdims=True)
        dy = err * (1.0 / d)
        dx, xhat = _rmsnorm_bwd(dy, x, gain_v)
        dg_ref[...] += jnp.sum(dy * xhat, axis=0, keepdims=True)
        dh_ref[...] = dx

    return pl.pallas_call(
        body, name=name, grid=(tp // tm,),
        in_specs=[pl.BlockSpec((tm, d), lambda i: (i, 0)), pl.BlockSpec((1, d), lambda i: (0, 0)),
                  pl.BlockSpec((tm, d), lambda i: (jnp.maximum(i - front_tiles, 0), 0))],
        out_specs=[pl.BlockSpec((tm, d), lambda i: (i, 0)), pl.BlockSpec((1, d), lambda i: (0, 0)),
                   pl.BlockSpec((1, 128), lambda i: (0, 0))],
        out_shape=[jax.ShapeDtypeStruct((tp, d), F32), jax.ShapeDtypeStruct((1, d), F32),
                   jax.ShapeDtypeStruct((1, 128), F32)],
        compiler_params=_cp(1))(h, gain, target)


def _gated_headnorm(o, g, gain):
    return o * _rms(o) * gain * (g * _sigmoid(g))


def _row_mask(chunk, size=CHUNK):
    rows = chunk * size + lax.broadcasted_iota(jnp.int32, (size, 1), 0)
    return (rows >= FRONT - N_META).astype(F32)


def _ret_head(q1, q2, k1, k2, v, g, state, gain, cos, sin, dmat, dq, dk, dc):
    q = jnp.concatenate([q1 * cos - q2 * sin, q1 * sin + q2 * cos], axis=-1)
    k = jnp.concatenate([k1 * cos - k2 * sin, k1 * sin + k2 * cos], axis=-1) * (RET_DK ** -0.5)
    scores = _nt(q, k) * dmat
    o = _nn(scores, v) + _nn(q * dq, state)
    new_state = state * dc + _tn(k * dk, v)
    return _gated_headnorm(o, g, gain), new_state


def _ret_consts():
    log_gamma = jnp.log1p(-2.0 ** (-5.0 - jnp.arange(HEADS, dtype=F32)))
    idx = jnp.arange(RET_CHUNK, dtype=F32)
    rel = idx[:, None] - idx[None, :]
    dmat = jnp.where(rel >= 0, jnp.exp(log_gamma[:, None, None] * jnp.maximum(rel, 0.0)), 0.0)
    dq = jnp.exp(log_gamma[:, None] * (idx + 1.0))[..., None]
    dk = jnp.exp(log_gamma[:, None] * (RET_CHUNK - 1.0 - idx))[..., None]
    dc = jnp.broadcast_to(jnp.exp(log_gamma * RET_CHUNK)[:, None, None], (HEADS, 1, 128))
    return dmat, dq, dk, dc


def _rope_tables(tp):
    half = RET_DK // 2
    inv = 1.0 / (ROPE_BASE ** jnp.linspace(0.0, 1.0, half, dtype=F32))
    pos = (jnp.arange(tp) - (FRONT - N_META)).astype(F32)
    ang = pos[:, None] * inv[None, :]
    return jnp.cos(ang), jnp.sin(ang)


_RET_V0, _RET_G0 = 2 * D, 4 * D


def _heads(ref, start, width, stride=None, rows=slice(None)):
    stride = width if stride is None else stride
    return jnp.stack([ref[rows, start + stride * h:start + stride * h + width].astype(F32) for h in range(HEADS)])


def _put_heads(ref, start, value, mask, stride=None, rows=slice(None)):
    width = value.shape[-1]
    stride = width if stride is None else stride
    for h in range(HEADS):
        ref[rows, start + stride * h:start + stride * h + width] = (value[h] * mask).astype(ref.dtype)


def _ret_pieces(u_ref):
    hk = RET_DK // 2
    return (_heads(u_ref, 0, hk, RET_DK), _heads(u_ref, hk, hk, RET_DK), _heads(u_ref, D, hk, RET_DK),
            _heads(u_ref, D + hk, hk, RET_DK), _heads(u_ref, _RET_V0, RET_DV), _heads(u_ref, _RET_G0, RET_DV))


def _ret_const_specs(rev=None):
    c = (lambda n: (rev(n), 0)) if rev else (lambda n: (n, 0))
    z3 = lambda n: (0, 0, 0)
    return [pl.BlockSpec((RET_CHUNK, RET_DK // 2), c), pl.BlockSpec((RET_CHUNK, RET_DK // 2), c),
            pl.BlockSpec((HEADS, RET_CHUNK, RET_CHUNK), z3), pl.BlockSpec((HEADS, RET_CHUNK, 1), z3),
            pl.BlockSpec((HEADS, RET_CHUNK, 1), z3), pl.BlockSpec((HEADS, 1, 128), z3)]


def _ret_fwd(u, gain, rope, h, w_out, name):
    tp = u.shape[0]
    nch = tp // RET_CHUNK
    cos, sin = rope
    dmat, dq, dk, dc = _ret_consts()

    def body(u_ref, gain_ref, h_ref, w_ref, cos_ref, sin_ref, dmat_ref, dq_ref, dk_ref, dc_ref,
             on_ref, st_ref, hmix_ref, state_ref):
        @pl.when(pl.program_id(0) == 0)
        def _():
            state_ref[...] = jnp.zeros_like(state_ref)

        state = state_ref[...]
        st_ref[...] = state.astype(BF16)
        on, new_state = _ret_head(*_ret_pieces(u_ref), state, _heads(gain_ref, 0, RET_DV), cos_ref[...], sin_ref[...],
                                  dmat_ref[...], dq_ref[...], dk_ref[...], dc_ref[...][:, :, :1])
        state_ref[...] = new_state
        _put_heads(on_ref, 0, on, 1.0)
        hmix_ref[...] = h_ref[...] + jnp.dot(on_ref[...], w_ref[...], preferred_element_type=F32)

    rows = lambda width: pl.BlockSpec((RET_CHUNK, width), lambda n: (n, 0))
    return pl.pallas_call(
        body, name=name, grid=(nch,),
        in_specs=[rows(6 * D), pl.BlockSpec((1, HEADS * RET_DV), lambda n: (0, 0)), rows(D),
                  _resident(w_out.shape, 1)] + _ret_const_specs(),
        out_specs=[rows(HEADS * RET_DV), pl.BlockSpec((None, HEADS, RET_DK, RET_DV), lambda n: (n, 0, 0, 0)), rows(D)],
        out_shape=[jax.ShapeDtypeStruct((tp, HEADS * RET_DV), BF16),
                   jax.ShapeDtypeStruct((nch, HEADS, RET_DK, RET_DV), BF16), jax.ShapeDtypeStruct((tp, D), F32)],
        scratch_shapes=[pltpu.VMEM((HEADS, RET_DK, RET_DV), F32)],
        compiler_params=_cp(1))(u, gain, h, w_out, cos, sin, dmat, dq, dk, dc)


def _ret_bwd(u, gain, rope, states, d_on, name):
    tp = u.shape[0]
    nch = tp // RET_CHUNK
    cos, sin = rope
    dmat, dq, dk, dc = _ret_consts()
    rev = lambda n: nch - 1 - n
    hk = RET_DK // 2

    def body(u_ref, gain_ref, st_ref, don_ref, cos_ref, sin_ref, dmat_ref, dq_ref, dk_ref, dc_ref,
             du_ref, dgain_ref, dstate_ref):
        @pl.when(pl.program_id(0) == 0)
        def _():
            dstate_ref[...] = jnp.zeros_like(dstate_ref)
            dgain_ref[...] = jnp.zeros_like(dgain_ref)

        mask = _row_mask(rev(pl.program_id(0)), RET_CHUNK)
        consts = (cos_ref[...], sin_ref[...], dmat_ref[...], dq_ref[...], dk_ref[...], dc_ref[...][:, :, :1])
        _, vjp = jax.vjp(lambda *a: _ret_head(*a, *consts), *_ret_pieces(u_ref), st_ref[...].astype(F32),
                         _heads(gain_ref, 0, RET_DV))
        dq1, dq2, dk1, dk2, dv, dg, dstate, dgain = vjp((_heads(don_ref, 0, RET_DV), dstate_ref[...]))
        dstate_ref[...] = dstate
        for hd in range(HEADS):
            dgain_ref[:, RET_DV * hd:RET_DV * (hd + 1)] += dgain[hd]
        _put_heads(du_ref, 0, dq1, mask, RET_DK)
        _put_heads(du_ref, hk, dq2, mask, RET_DK)
        _put_heads(du_ref, D, dk1, mask, RET_DK)
        _put_heads(du_ref, D + hk, dk2, mask, RET_DK)
        _put_heads(du_ref, _RET_V0, dv, mask)
        _put_heads(du_ref, _RET_G0, dg, mask)

    return pl.pallas_call(
        body, name=name, grid=(nch,),
        in_specs=[pl.BlockSpec((RET_CHUNK, 6 * D), lambda n: (rev(n), 0)),
                  pl.BlockSpec((1, HEADS * RET_DV), lambda n: (0, 0)),
                  pl.BlockSpec((None, HEADS, RET_DK, RET_DV), lambda n: (rev(n), 0, 0, 0)),
                  pl.BlockSpec((RET_CHUNK, HEADS * RET_DV), lambda n: (rev(n), 0))] + _ret_const_specs(rev),
        out_specs=[pl.BlockSpec((RET_CHUNK, 6 * D), lambda n: (rev(n), 0)),
                   pl.BlockSpec((1, HEADS * RET_DV), lambda n: (0, 0))],
        out_shape=[jax.ShapeDtypeStruct((tp, 6 * D), BF16), jax.ShapeDtypeStruct((1, HEADS * RET_DV), F32)],
        scratch_shapes=[pltpu.VMEM((HEADS, RET_DK, RET_DV), F32)],
        compiler_params=_cp(1))(u, gain, states, d_on, cos, sin, dmat, dq, dk, dc)


_GLA_K0, _GLA_V0, _GLA_G0, _GLA_Z0 = 512, 1024, 2048, 3072


def _gla_head(q, k, v, g, z, state_t, wg, bg, gain, mask, lo, lo_t, to_mid, to_mid_t, in_second, pair):
    ga = _nn(jnp.broadcast_to(z, wg.shape[:-2] + z.shape), wg) + bg
    log_a = (jnp.minimum(ga, 0.0) - jnp.log(1.0 + jnp.exp(-jnp.abs(ga)))) * (mask * (1.0 / GLA_TAU))
    bcum = _cum(lo, lo_t, log_a)
    btot = jnp.sum(log_a, axis=-2, keepdims=True)
    qs = q * (GLA_DK ** -0.5)
    heads, levels = q.shape[0], pair.shape[0]
    decay = jnp.exp(_cum16(to_mid, to_mid_t, log_a).reshape(heads, levels, CHUNK, GLA_DK))
    qk = (jnp.where(in_second > 0.0, qs[:, None], k[:, None]) * decay).reshape(heads * levels, CHUNK, GLA_DK)
    rows = lax.broadcasted_iota(jnp.int32, (CHUNK, CHUNK), 0)
    cols = lax.broadcasted_iota(jnp.int32, (CHUNK, CHUNK), 1)
    scores = (jnp.where(rows == cols, _nt(qs, k), 0.0)
              + jnp.sum(_nt(qk, qk).reshape(heads, levels, CHUNK, CHUNK) * pair, axis=1))
    o = _nn(scores, v) + _nt(qs * jnp.exp(bcum), state_t)
    new_state_t = state_t * jnp.exp(btot) + _tn(v, k * jnp.exp(btot - bcum))
    return _gated_headnorm(o, g, gain), new_state_t


def _gla_consts():
    r, c = np.meshgrid(np.arange(CHUNK), np.arange(CHUNK), indexing="ij")
    to_mid, second, pair = [], [], []
    block = 2
    while block <= CHUNK:
        mid = (r // block) * block + block // 2
        to_mid.append(((r >= mid) & (c > mid) & (c <= r)) | ((r < mid) & (c > r) & (c <= mid)))
        second.append((r >= mid)[:, :1])
        pair.append((r // block == c // block) & (r >= mid) & (c < mid))
        block *= 2
    to_mid = np.concatenate(to_mid)
    bf = lambda m: jnp.asarray(m, F32).astype(BF16)
    f32 = lambda ms: jnp.asarray(np.stack(ms), F32)
    return bf(r >= c), bf(c >= r), bf(to_mid), bf(to_mid.T), f32(second), f32(pair)


def _gla_const_specs(consts):
    return [pl.BlockSpec(a.shape, functools.partial(lambda nd, n: (0,) * nd, a.ndim)) for a in consts]


GLA_STEP_CHUNKS = 4


def _gla_pieces(u_ref, rows):
    return (_heads(u_ref, 0, GLA_DK, rows=rows), _heads(u_ref, _GLA_K0, GLA_DK, rows=rows),
            _heads(u_ref, _GLA_V0, GLA_DV, rows=rows), _heads(u_ref, _GLA_G0, GLA_DV, rows=rows),
            u_ref[rows, _GLA_Z0:_GLA_Z0 + 128].astype(F32))


def _gla_fwd(u, wg, bg, gain, name):
    tp = u.shape[0]
    nch = tp // CHUNK
    per = GLA_STEP_CHUNKS
    consts = _gla_consts()

    def body(u_ref, wg_ref, bg_ref, gain_ref, *refs):
        const_refs, (on_ref, st_ref, state_ref) = refs[:len(consts)], refs[len(consts):]

        @pl.when(pl.program_id(0) == 0)
        def _():
            state_ref[...] = jnp.zeros_like(state_ref)

        params = (_heads(wg_ref, 0, GLA_DK), _heads(bg_ref, 0, GLA_DK), _heads(gain_ref, 0, GLA_DV))
        mats = [ref[...] for ref in const_refs]
        state = state_ref[...]
        for c in range(per):
            rows = slice(CHUNK * c, CHUNK * (c + 1))
            st_ref[c] = state.astype(BF16)
            on, state = _gla_head(*_gla_pieces(u_ref, rows), state, *params, _row_mask(pl.program_id(0) * per + c), *mats)
            _put_heads(on_ref, 0, on, 1.0, rows=rows)
        state_ref[...] = state

    rows_spec = lambda width: pl.BlockSpec((per * CHUNK, width), lambda n: (n, 0))
    full = lambda r, c: pl.BlockSpec((r, c), lambda n: (0, 0))
    return pl.pallas_call(
        body, name=name, grid=(nch // per,),
        in_specs=[rows_spec(GLA_U), full(128, HEADS * GLA_DK), full(1, HEADS * GLA_DK), full(1, HEADS * GLA_DV)]
                 + _gla_const_specs(consts),
        out_specs=[rows_spec(HEADS * GLA_DV), pl.BlockSpec((per, HEADS, GLA_DV, GLA_DK), lambda n: (n, 0, 0, 0))],
        out_shape=[jax.ShapeDtypeStruct((tp, HEADS * GLA_DV), BF16),
                   jax.ShapeDtypeStruct((nch, HEADS, GLA_DV, GLA_DK), BF16)],
        scratch_shapes=[pltpu.VMEM((HEADS, GLA_DV, GLA_DK), F32)],
        compiler_params=_cp(1))(u, wg, bg, gain, *consts)


def _gla_bwd(u, wg, bg, gain, states, d_on, name):
    tp = u.shape[0]
    per = GLA_STEP_CHUNKS
    steps = tp // (per * CHUNK)
    rev = lambda n: steps - 1 - n
    consts = _gla_consts()

    def body(u_ref, wg_ref, bg_ref, gain_ref, st_ref, don_ref, *refs):
        const_refs, (du_ref, dwg_ref, dbg_ref, dgain_ref, dstate_ref) = refs[:len(consts)], refs[len(consts):]

        @pl.when(pl.program_id(0) == 0)
        def _():
            dstate_ref[...] = jnp.zeros_like(dstate_ref)
            dwg_ref[...] = jnp.zeros_like(dwg_ref)
            dbg_ref[...] = jnp.zeros_like(dbg_ref)
            dgain_ref[...] = jnp.zeros_like(dgain_ref)

        params = (_heads(wg_ref, 0, GLA_DK), _heads(bg_ref, 0, GLA_DK), _heads(gain_ref, 0, GLA_DV))
        mats = [ref[...] for ref in const_refs]
        dstate = dstate_ref[...]
        for c in reversed(range(per)):
            rows = slice(CHUNK * c, CHUNK * (c + 1))
            mask = _row_mask(rev(pl.program_id(0)) * per + c)
            _, vjp = jax.vjp(lambda *a: _gla_head(*a, mask, *mats), *_gla_pieces(u_ref, rows),
                             st_ref[c].astype(F32), *params)
            dq, dk, dv, dg, dz, dstate, dwg, dbg, dgain = vjp((_heads(don_ref, 0, GLA_DV, rows=rows), dstate))
            for hd in range(HEADS):
                dwg_ref[:, GLA_DK * hd:GLA_DK * (hd + 1)] += dwg[hd]
                dbg_ref[:, GLA_DK * hd:GLA_DK * (hd + 1)] += dbg[hd]
                dgain_ref[:, GLA_DV * hd:GLA_DV * (hd + 1)] += dgain[hd]
            _put_heads(du_ref, 0, dq, mask, rows=rows)
            _put_heads(du_ref, _GLA_K0, dk, mask, rows=rows)
            _put_heads(du_ref, _GLA_V0, dv, mask, rows=rows)
            _put_heads(du_ref, _GLA_G0, dg, mask, rows=rows)
            du_ref[rows, _GLA_Z0:_GLA_Z0 + 128] = dz.astype(BF16)
            du_ref[rows, _GLA_Z0 + 128:] = jnp.zeros((CHUNK, GLA_U - _GLA_Z0 - 128), BF16)
        dstate_ref[...] = dstate

    full = lambda r, c: pl.BlockSpec((r, c), lambda n: (0, 0))
    return pl.pallas_call(
        body, name=name, grid=(steps,),
        in_specs=[pl.BlockSpec((per * CHUNK, GLA_U), lambda n: (rev(n), 0)), full(128, HEADS * GLA_DK),
                  full(1, HEADS * GLA_DK), full(1, HEADS * GLA_DV),
                  pl.BlockSpec((per, HEADS, GLA_DV, GLA_DK), lambda n: (rev(n), 0, 0, 0)),
                  pl.BlockSpec((per * CHUNK, HEADS * GLA_DV), lambda n: (rev(n), 0))] + _gla_const_specs(consts),
        out_specs=[pl.BlockSpec((per * CHUNK, GLA_U), lambda n: (rev(n), 0)), full(128, HEADS * GLA_DK),
                   full(1, HEADS * GLA_DK), full(1, HEADS * GLA_DV)],
        out_shape=[jax.ShapeDtypeStruct((tp, GLA_U), BF16), jax.ShapeDtypeStruct((128, HEADS * GLA_DK), F32),
                   jax.ShapeDtypeStruct((1, HEADS * GLA_DK), F32), jax.ShapeDtypeStruct((1, HEADS * GLA_DV), F32)],
        scratch_shapes=[pltpu.VMEM((HEADS, GLA_DV, GLA_DK), F32)],
        compiler_params=_cp(1))(u, wg, bg, gain, states, d_on, *consts)


def _ffn_fwd(h, gain, w_in, w_out, tag):
    hn, ug, uu, act = _norm_ffn_in(h, gain, w_in, f"{tag}_in")
    if callable(w_out):
        w_out = w_out(act)
    return _out_proj(act, w_out, h, 0.5, f"{tag}_out"), (h, hn, ug, uu, act), w_out


def _ffn_dgrad(dh, w_out, w_in, act_dg, act_du, h, gain, name, split_front=False):
    tp, d = dh.shape
    ff = w_out.shape[0]
    tm = TM_SMALL
    nt = (((1,), (1,)), ((), ()))

    def body(dh_ref, wo_ref, wi_ref, dg_ref, du_ref, h_ref, g_ref, o_ref, *out_refs):
        dhi_ref, dgain_ref = out_refs[-2:]

        @pl.when(pl.program_id(0) == 0)
        def _():
            dgain_ref[...] = jnp.zeros_like(dgain_ref)

        dho = dh_ref[...]
        dact = lax.dot_general((0.5 * dho).astype(BF16), wo_ref[...], nt, preferred_element_type=F32)
        d_gate = (dact * dg_ref[...].astype(F32)).astype(BF16)
        d_up = (dact * du_ref[...].astype(F32)).astype(BF16)
        o_ref[:, :ff] = d_gate
        o_ref[:, ff:] = d_up
        dhn = (lax.dot_general(d_gate, wi_ref[:, :ff], nt, preferred_element_type=F32)
               + lax.dot_general(d_up, wi_ref[:, ff:], nt, preferred_element_type=F32))
        dx, xhat = _rmsnorm_bwd(dhn, h_ref[...], g_ref[...])
        dgain_ref[...] += jnp.sum(dhn * xhat, axis=0, keepdims=True)
        dhi_ref[...] = dho + dx
        if split_front:
            @pl.when(pl.program_id(0) == 0)
            def _():
                out_refs[0][...] = dho + dx

    rows = lambda width: pl.BlockSpec((tm, width), lambda i: (i, 0))
    if split_front:
        assert tm == FRONT
        dhi_specs = [pl.BlockSpec((tm, d), lambda i: (0, 0)), pl.BlockSpec((tm, d), lambda i: (jnp.maximum(i - 1, 0), 0))]
        dhi_shapes = [jax.ShapeDtypeStruct((FRONT, d), F32), jax.ShapeDtypeStruct((tp - FRONT, d), F32)]
    else:
        dhi_specs, dhi_shapes = [rows(d)], [jax.ShapeDtypeStruct((tp, d), F32)]
    out = pl.pallas_call(
        body, name=name, grid=(tp // tm,),
        in_specs=[rows(d), _resident(w_out.shape, 1), _resident(w_in.shape, 1), rows(ff), rows(ff), rows(d),
                  pl.BlockSpec((1, d), lambda i: (0, 0))],
        out_specs=[rows(2 * ff), *dhi_specs, pl.BlockSpec((1, d), lambda i: (0, 0))],
        out_shape=[jax.ShapeDtypeStruct((tp, 2 * ff), BF16), *dhi_shapes, jax.ShapeDtypeStruct((1, d), F32)],
        compiler_params=_cp(1))(dh, w_out, w_in, act_dg, act_du, h, gain)
    return (out[0], tuple(out[1:3]), out[3]) if split_front else tuple(out)


def _ffn_bwd(dh, saved, gain, w_in, w_out, tag, push, split_front=False):
    h, hn, act_dg, act_du, act = saved
    du, dh_in, d_gain = _ffn_dgrad(dh, w_out, w_in, act_dg, act_du, h, gain, f"{tag}_dgrad", split_front)
    d_w_out = _wgrad(act, dh, bm=D_FF // 2, bn=D, scale=0.5, sharded=False, name=f"{tag}_dwout")
    d_w_in = _wgrad(hn, du, bm=D, bn=D_FF, scale=1.0, sharded=False, name=f"{tag}_dwin")
    return dh_in, d_gain, push([("cols", d_w_in), d_w_out])


def _sequence_grads(x, target, p, weights, grads):
    row = lambda v, token: v.reshape(1, -1) + token[0, 0]
    gains = {}

    tok = weights.start(1, weights.start(0, None))
    weights.pin = tok
    h = jnp.concatenate([jnp.zeros((FRONT, D), F32), x], axis=0) + tok[0, 0]
    rope = _rope_tables(h.shape[0])
    w = weights.wait(0, [tok, h, *rope, *weights.later_shards(2)])
    tok = weights.start(2, w["l0_ffn1_in"])
    h = lax.dynamic_update_slice(h, w["meta"], (FRONT - N_META, 0))
    gains["l0_ffn1"] = row(p["norm_ffn1"][0], tok)
    h, s1, w["l0_ffn1_out"] = _ffn_fwd(h, gains["l0_ffn1"], w["l0_ffn1_in"],
                                       lambda act: weights.wait(1, act)["l0_ffn1_out"], "l0_ffn1")
    w.update(weights.wait(2, h))
    tok = weights.start(4, weights.start(3, w["ret_in"]))
    gains["ret"] = row(p["norm_mix"][0], tok)
    hn, u = _norm_proj(h, gains["ret"], w["ret_in"], "ret_in")
    w.update(weights.wait(3, u))
    on, states, h_mix = _ret_fwd(u, w["ret_gain"], rope, h, w["ret_out"], "ret_fwd")
    s2 = (h, hn, u, on, states)
    w.update(weights.wait(4, h_mix))
    tok = weights.start(5, w["l0_ffn2_in"])
    gains["l0_ffn2"] = row(p["norm_ffn2"][0], tok)
    h, s3, _ = _ffn_fwd(h_mix, gains["l0_ffn2"], w["l0_ffn2_in"], w["l0_ffn2_out"], "l0_ffn2")
    saved = [(s1, s2, s3)]

    w.update(weights.wait(5, h))
    tok = weights.start(6, w["l1_ffn1_in"])
    gains["l1_ffn1"] = row(p["norm_ffn1"][1], tok)
    h, s1, _ = _ffn_fwd(h, gains["l1_ffn1"], w["l1_ffn1_in"], w["l1_ffn1_out"], "l1_ffn1")
    w.update(weights.wait(6, h))
    tok = weights.start(7, w["gla_out"])
    gains["gla"] = row(p["norm_mix"][1], tok)
    hn, u = _norm_proj(h, gains["gla"], w["gla_in"], "gla_in")
    on, states = _gla_fwd(u, w["gla_wg"], w["gla_bg"], w["gla_gain"], "gla_fwd")
    h_mix = _out_proj(on, w["gla_out"], h, 1.0, "gla_out")
    s2 = (h, hn, u, on, states)
    w.update(weights.wait(7, h_mix))
    gains["l1_ffn2"] = p["norm_ffn2"][1].reshape(1, -1)
    h, s3, _ = _ffn_fwd(h_mix, gains["l1_ffn2"], w["l1_ffn2_in"], w["l1_ffn2_out"], "l1_ffn2")
    saved.append((s1, s2, s3))

    dh, d_final, loss = _loss_head(h, p["final_norm"].reshape(1, -1), target, "loss_head")
    small = {"final_norm": d_final, "norm_ffn1": [None, None], "norm_mix": [None, None], "norm_ffn2": [None, None]}
    pusher = lambda k: functools.partial(grads.push, k)

    s1, s2, s3 = saved[1]
    dh, small["norm_ffn2"][1], tok = _ffn_bwd(dh, s3, gains["l1_ffn2"], w["l1_ffn2_in"], w["l1_ffn2_out"], "l1_ffn2",
                                              pusher(0))
    h_in, hn, u, on, states = s2
    d_on = _dgrad(dh, w["gla_out"], "gla_don")
    d_out = _wgrad(on, dh, bm=D, bn=D, scale=1.0, sharded=False, name="gla_dwout")
    du, small["gla_wg"], small["gla_bg"], small["gla_gain"] = _gla_bwd(
        u, w["gla_wg"], w["gla_bg"], w["gla_gain"] + tok[0, 0], states, d_on, "gla_bwd")
    d_in = _wgrad(hn, du, bm=D, bn=GLA_U, scale=1.0, sharded=False, name="gla_dwin")
    d_in = jnp.moveaxis(d_in[:, :GLA_IN].reshape(D, N_CHIPS, -1), 1, 0)
    tok = grads.push(1, [d_in, d_out])
    dh, small["norm_mix"][1] = _dgrad_norm(du, w["gla_in"], h_in, gains["gla"] + tok[0, 0], dh, "gla_dnorm")
    dh, small["norm_ffn1"][1], tok = _ffn_bwd(dh, s1, gains["l1_ffn1"], w["l1_ffn1_in"], w["l1_ffn1_out"], "l1_ffn1",
                                              pusher(2))

    s1, s2, s3 = saved[0]
    dh, small["norm_ffn2"][0], tok = _ffn_bwd(dh, s3, gains["l0_ffn2"] + tok[0, 0], w["l0_ffn2_in"],
                                              w["l0_ffn2_out"], "l0_ffn2", pusher(3))
    h_in, hn, u, on, states = s2
    d_on = _dgrad(dh, w["ret_out"], "ret_don")
    d_out = _wgrad(on, dh, bm=D, bn=D, scale=1.0, sharded=False, name="ret_dwout")
    du, small["ret_gain"] = _ret_bwd(u, w["ret_gain"] + tok[0, 0], rope, states, d_on, "ret_bwd")
    d_in = _wgrad(hn, du, bm=D, bn=w["ret_in"].shape[2], scale=1.0, sharded=True, name="ret_dwin")
    tok = grads.push(4, [d_in, d_out])
    dh, small["norm_mix"][0] = _dgrad_norm(du, w["ret_in"], h_in, gains["ret"] + tok[0, 0], dh, "ret_dnorm")
    (d_front, d_x), small["norm_ffn1"][0], tok = _ffn_bwd(dh, s1, gains["l0_ffn1"], w["l0_ffn1_in"], w["l0_ffn1_out"],
                                                          "l0_ffn1", pusher(5), split_front=True)
    grads.push(6, [], [d_front[FRONT - N_META:], *small["norm_ffn1"], *small["norm_mix"], *small["norm_ffn2"],
                       small["final_norm"], small["ret_gain"], small["gla_wg"][:GLA_RANK], small["gla_bg"],
                       small["gla_gain"], loss[:, :1] + tok[0, 0]])
    return d_x


_HBM = pl.BlockSpec(memory_space=pl.ANY)


def _place():
    return lax.axis_index("x"), lax.axis_index("y"), lax.axis_index("c")


def _flip(v, bit):
    return 1 - v if bit else v


DMA_CHUNK_BYTES = 128 * 1024


def _row_chunks(ref):
    rows, cols = ref.shape
    step = _row_tile(rows, max(16, DMA_CHUNK_BYTES // (cols * ref.dtype.itemsize)))
    return [pl.ds(a, step) for a in range(0, rows, step)]


def _whole(src, dst, send_sem, recv_sem, peer):
    return pltpu.make_async_remote_copy(src_ref=src, dst_ref=dst, send_sem=send_sem, recv_sem=recv_sem,
                                        device_id=peer, device_id_type=MESH)


def _send(src, dst, send_sem, recv_sem, peer):
    for rows in _row_chunks(src):
        _whole(src.at[rows], dst.at[rows], send_sem, recv_sem, peer).start()
    return _whole(src, dst, send_sem, recv_sem, peer)


_HBM_ONLY = pl.BlockSpec(memory_space=pltpu.HBM)
_SEMS = pl.BlockSpec(memory_space=pltpu.SEMAPHORE)
_SIDE_EFFECT = pltpu.CompilerParams(has_side_effects=pltpu.SideEffectType.DATAFLOW_SIDE_EFFECTING)
_GATHER_FLIPS = [(1, 0, 0), (0, 1, 0), (1, 1, 0), (0, 0, 1)]
_PEER_FLIPS = [(fx, fy, fc) for fx in (0, 1) for fy in (0, 1) for fc in (0, 1)][1:]


def _zero_token():
    return jnp.zeros((8, 128), F32)


def _exchange_start(srcs, lands, route, flips, after, name):
    n = len(srcs)

    def body(*refs):
        src, land = refs[:n], refs[n:2 * n]
        send_sems, recv_sems, token = refs[2 * n + 1], refs[2 * n + 2], refs[-1]
        me = _place()
        for t in range(n):
            for j, flip in enumerate(flips):
                peer = tuple(_flip(v, f) for v, f in zip(me, flip))
                s, d = route(t, src[t], land[t], me, peer)
                _send(s, d, send_sems.at[t * len(flips) + j], recv_sems.at[t * len(flips) + j], peer)
        token[...] = jnp.zeros_like(token)

    hbm = lambda a: pltpu.HBM(a.shape, a.dtype)
    sems = pltpu.SemaphoreType.DMA((n * len(flips),))
    operands = [pltpu.with_memory_space_constraint(a, pltpu.HBM) for a in list(srcs) + list(lands)]
    out = pl.pallas_call(
        body, name=name, in_specs=[_HBM_ONLY] * (2 * n) + [_HBM],
        out_shape=(sems, sems, *[hbm(a) for a in operands], jax.ShapeDtypeStruct((8, 128), F32)),
        out_specs=(_SEMS, _SEMS, *[_HBM_ONLY] * (2 * n), pl.BlockSpec(memory_space=pltpu.VMEM)),
        input_output_aliases={i: 2 + i for i in range(2 * n)}, compiler_params=_SIDE_EFFECT,
    )(*operands, _zero_token() if after is None else after)
    return (out[0], out[1], out[2:2 + n], out[2 + n:2 + 2 * n]), out[-1]


def _exchange_wait(started, route, flips, after, name):
    send_sems, recv_sems, srcs, lands = started
    n = len(srcs)

    def body(*refs):
        src, land = refs[:n], refs[n:2 * n]
        send_sems, recv_sems = refs[2 * n], refs[2 * n + 1]
        me = _place()
        for t in range(n):
            for j, flip in enumerate(flips):
                peer = tuple(_flip(v, f) for v, f in zip(me, flip))
                s, d = route(t, src[t], land[t], me, peer)
                cp = _whole(s, d, send_sems.at[t * len(flips) + j], recv_sems.at[t * len(flips) + j], peer)
                cp.wait_send()
                cp.wait_recv()

    hbm = lambda a: pltpu.HBM(a.shape, a.dtype)
    after = list(after) if isinstance(after, (list, tuple)) else [after]
    out = pl.pallas_call(
        body, name=name, in_specs=[_HBM_ONLY] * (2 * n) + [_SEMS, _SEMS] + [_HBM] * len(after),
        out_shape=tuple(hbm(a) for a in list(srcs) + list(lands)), out_specs=tuple([_HBM_ONLY] * (2 * n)),
        input_output_aliases={i: i for i in range(2 * n)}, compiler_params=_SIDE_EFFECT,
    )(*srcs, *lands, send_sems, recv_sems, *after)
    return out[:n], out[n:]


def _gather_route(t, src, land, me, peer):
    mine = 2 * me[0] + me[1]
    if land.ndim == 3:
        return src, land.at[mine]
    cols = src.shape[1]
    return src, land.at[:, pl.ds(pl.multiple_of(mine * cols, 128), cols)]


def _scatter_route(n_pieces):
    def route(t, src, land, me, peer):
        chip = 2 * peer[0] + peer[1]
        if t >= n_pieces:
            part = src
        elif src.ndim == 4:
            part = src.at[chip, peer[2]]
        else:
            rows, cols = land.shape[1:]
            part = src.at[pl.ds(pl.multiple_of(peer[2] * rows, 16), rows), pl.ds(pl.multiple_of(chip * cols, 128), cols)]
        return part, land.at[4 * me[0] + 2 * me[1] + me[2]]

    return route


def _swap_cores(halves, name):
    n = len(halves)

    def body(*refs):
        src, dst = refs[:n], refs[n:2 * n]
        send_sems, recv_sems = refs[2 * n:]
        x, y, c = _place()
        copies = [_send(src[t], dst[t], send_sems.at[t], recv_sems.at[t], (x, y, 1 - c)) for t in range(n)]
        for cp in copies:
            cp.wait()

    got = pl.pallas_call(
        body, name=name, in_specs=[_HBM] * n, out_specs=[_HBM] * n,
        out_shape=[jax.ShapeDtypeStruct(a.shape, a.dtype) for a in halves],
        scratch_shapes=[pltpu.SemaphoreType.DMA((n,)), pltpu.SemaphoreType.DMA((n,))],
    )(*halves)
    south = lax.axis_index("c") == 0
    return [jnp.stack([jnp.where(south, a, b), jnp.where(south, b, a)]) for a, b in zip(halves, got)]


def _row_tile(rows, cap):
    fits = [t for t in range(16, cap + 1, 16) if rows % t == 0]
    return fits[-1] if fits else rows


def _sum_slots(a, name):
    _, r, c = a.shape
    tr = _row_tile(r, 384)

    def body(a_ref, o_ref):
        s = a_ref[0].astype(F32)
        for k in range(1, N_DEV):
            s = s + a_ref[k].astype(F32)
        o_ref[...] = s

    return pl.pallas_call(
        body, name=name, grid=(r // tr,),
        in_specs=[pl.BlockSpec((N_DEV, tr, c), lambda i: (0, i, 0))],
        out_specs=pl.BlockSpec((tr, c), lambda i: (i, 0)),
        out_shape=jax.ShapeDtypeStruct((r, c), F32),
        compiler_params=_cp(1))(a)


def _adamw(w, g, m, v, name):
    layers, r, c = w.shape
    tr = _row_tile(r, 256)

    def body(w_ref, g_ref, m_ref, v_ref, d_ref, nm_ref, nv_ref):
        gv = g_ref[...]
        nm = ADAM_B1 * m_ref[...] + (1.0 - ADAM_B1) * gv
        nv = ADAM_B2 * v_ref[...] + (1.0 - ADAM_B2) * (gv * gv)
        m_hat = nm / (1.0 - ADAM_B1 ** ADAM_STEP)
        v_hat = nv / (1.0 - ADAM_B2 ** ADAM_STEP)
        d_ref[...] = -ADAM_LR * (m_hat / (jnp.sqrt(v_hat) + ADAM_EPS) + ADAM_WD * w_ref[...])
        nm_ref[...] = nm
        nv_ref[...] = nv

    spec = pl.BlockSpec((None, tr, c), lambda a, i: (a, i, 0))
    return pl.pallas_call(
        body, name=name, grid=(layers, r // tr), in_specs=[spec] * 4, out_specs=[spec] * 3,
        out_shape=[jax.ShapeDtypeStruct((layers, r, c), F32)] * 3,
        compiler_params=_cp(2))(*[pltpu.with_memory_space_constraint(a, pltpu.HBM) for a in (w, g, m, v)])


_SMALL = ["meta_tokens", "ret_head_norm", "gla_w_gate", "gla_b_gate", "gla_head_norm"]
_LOCAL_SMALL = ["meta_tokens", "norm_ffn1", "norm_mix", "norm_ffn2", "ret_head_norm", "gla_w_gate", "gla_b_gate",
                "gla_head_norm", "final_norm"]
_WEIGHTS = ["meta_tokens", "norm_ffn1", "ffn1_w_in", "ffn1_w_out", "norm_mix", "norm_ffn2", "ffn2_w_in", "ffn2_w_out",
            "ret_w_in", "ret_head_norm", "ret_w_out", "gla_w_in", "gla_w_gate", "gla_b_gate", "gla_head_norm",
            "gla_w_out", "final_norm"]


def _pack_rows(arrays, width):
    flat = jnp.concatenate([a.reshape(-1) for a in arrays])
    pad = -flat.shape[0] % (8 * width)
    return jnp.pad(flat, (0, pad)).reshape(-1, width)


def _unpack_rows(packed, shapes):
    flat, out, at = packed.reshape(-1), [], 0
    for s in shapes:
        size = 1
        for dim in s:
            size *= dim
        out.append(flat[at:at + size].reshape(s))
        at += size
    return out


class _WeightGather:
    GROUPS = [("small", "l0_ffn1_in"), ("l0_ffn1_out",), ("ret_in",), ("ret_out",), ("l0_ffn2_in", "l0_ffn2_out"),
              ("l1_ffn1_in", "l1_ffn1_out"), ("gla_in", "gla_out"), ("l1_ffn2_in", "l1_ffn2_out")]

    def __init__(self, p):
        self.small_shapes = [p[name].shape for name in _SMALL]
        self.f32 = {"small": _pack_rows([p[name] for name in _SMALL], 128), "ret_in": p["ret_w_in"][0],
                    "ret_out": p["ret_w_out"][0], "gla_in": p["gla_w_in"][0], "gla_out": p["gla_w_out"][0]}
        for layer in range(2):
            for name in ("ffn1", "ffn2"):
                self.f32[f"l{layer}_{name}_in"] = p[f"{name}_w_in"][layer]
                self.f32[f"l{layer}_{name}_out"] = p[f"{name}_w_out"][layer]
        self.shards = {}
        self.started = {}
        self.pin = None

    def shard(self, name):
        if name not in self.shards:
            a = self.f32[name]
            if name != "small":
                a = (a if self.pin is None else a + self.pin[0, 0]).astype(BF16)
            self.shards[name] = a
        return self.shards[name]

    def later_shards(self, k):
        return [self.shard(name) for group in self.GROUPS[k:] for name in group]

    def start(self, k, after):
        shards = [self.shard(name) for name in self.GROUPS[k]]
        lands = []
        for name, s in zip(self.GROUPS[k], shards):
            if "ffn" in name and name.endswith("_in"):
                lands.append(lax.empty((s.shape[0], N_CHIPS * s.shape[1]), s.dtype))
            else:
                lands.append(lax.empty((N_CHIPS,) + s.shape, s.dtype))
        self.started[k], token = _exchange_start(shards, lands, _gather_route, _GATHER_FLIPS, after, f"gather{k}_start")
        return token

    def wait(self, k, after):
        _, got = _exchange_wait(self.started[k], _gather_route, _GATHER_FLIPS, after, f"gather{k}_wait")
        w = {}
        for name, g in zip(self.GROUPS[k], got):
            if name == "small":
                parts = zip(*[_unpack_rows(g[chip], self.small_shapes) for chip in range(N_CHIPS)])
                cat = lambda a: jnp.moveaxis(a, 0, -2).reshape(a.shape[1:-1] + (-1,))
                meta, ret_gain, wg, bg, gla_gain = [cat(jnp.stack(part)) for part in parts]
                w.update(meta=meta, ret_gain=ret_gain.reshape(1, -1), gla_bg=bg.reshape(1, -1),
                         gla_gain=gla_gain.reshape(1, -1),
                         gla_wg=jnp.pad(wg[0], ((0, 128 - GLA_RANK), (0, 0))).astype(BF16))
            elif name == "gla_in":
                full = jnp.moveaxis(g, 0, 1).reshape(D, -1)
                w[name] = jnp.pad(full, ((0, 0), (0, GLA_U - GLA_IN)))[None]
            elif name.endswith("_out"):
                w[name] = g.reshape(-1, g.shape[-1])
            else:
                w[name] = g
        return w


class _GradExchange:
    def __init__(self):
        self.started = []
        self.token = None
        self.small_shapes = None

    def push(self, k, arrays, small=None):
        srcs, lands = [], []
        for a in arrays:
            if isinstance(a, tuple):
                a = a[1]
                piece = (a.shape[0] // 2, a.shape[1] // N_CHIPS)
            else:
                a = a.reshape(N_CHIPS, 2, -1, a.shape[-1])
                piece = a.shape[2:]
            srcs.append(a)
            lands.append(lax.empty((N_DEV,) + piece, a.dtype))
        if small is not None:
            self.small_shapes = [a.shape for a in small]
            srcs.append(_pack_rows(small, D))
            lands.append(lax.empty((N_DEV,) + srcs[-1].shape, F32))
        started, self.token = _exchange_start(srcs, lands, _scatter_route(len(arrays)), _PEER_FLIPS, None,
                                              f"scatter{k}_start")
        self.started.append((started, len(arrays)))
        return self.token

    def collect(self, groups, after=None):
        x, y, c = _place()
        after, sums = self.token if after is None else after, []
        for k in groups:
            started, n_pieces = self.started[k]
            srcs, got = _exchange_wait(started, _scatter_route(n_pieces), _PEER_FLIPS, after, f"scatter{k}_wait")
            own = []
            for t, (a, g) in enumerate(zip(srcs, got)):
                if t >= n_pieces:
                    own.append(a)
                elif a.ndim == 4:
                    own.append(a[2 * x + y, c])
                else:
                    rows, cols = g.shape[1:]
                    own.append(lax.dynamic_slice(a, (c * rows, (2 * x + y) * cols), (rows, cols)))
            got = [lax.dynamic_update_index_in_dim(g, a, 4 * x + 2 * y + c, 0) for g, a in zip(got, own)]
            sums.append([_sum_slots(a, f"sum{k}_{i}") for i, a in enumerate(got)])
            after = sums[-1][0]
        return sums


def kernel(x, meta_tokens, norm_ffn1, ffn1_w_in, ffn1_w_out, norm_mix, norm_ffn2, ffn2_w_in, ffn2_w_out, ret_w_in, ret_head_norm, ret_w_out, gla_w_in, gla_w_gate, gla_b_gate, gla_head_norm, gla_w_out, final_norm, loss_target, m_meta_tokens, m_norm_ffn1, m_ffn1_w_in, m_ffn1_w_out, m_norm_mix, m_norm_ffn2, m_ffn2_w_in, m_ffn2_w_out, m_ret_w_in, m_ret_head_norm, m_ret_w_out, m_gla_w_in, m_gla_w_gate, m_gla_b_gate, m_gla_head_norm, m_gla_w_out, m_final_norm, v_meta_tokens, v_norm_ffn1, v_ffn1_w_in, v_ffn1_w_out, v_norm_mix, v_norm_ffn2, v_ffn2_w_in, v_ffn2_w_out, v_ret_w_in, v_ret_head_norm, v_ret_w_out, v_gla_w_in, v_gla_w_gate, v_gla_b_gate, v_gla_head_norm, v_gla_w_out, v_final_norm):
    p = dict(meta_tokens=meta_tokens, norm_ffn1=norm_ffn1, ffn1_w_in=ffn1_w_in, ffn1_w_out=ffn1_w_out, norm_mix=norm_mix,
             norm_ffn2=norm_ffn2, ffn2_w_in=ffn2_w_in, ffn2_w_out=ffn2_w_out, ret_w_in=ret_w_in,
             ret_head_norm=ret_head_norm, ret_w_out=ret_w_out, gla_w_in=gla_w_in, gla_w_gate=gla_w_gate,
             gla_b_gate=gla_b_gate, gla_head_norm=gla_head_norm, gla_w_out=gla_w_out, final_norm=final_norm)
    m = dict(zip(_WEIGHTS, (m_meta_tokens, m_norm_ffn1, m_ffn1_w_in, m_ffn1_w_out, m_norm_mix, m_norm_ffn2, m_ffn2_w_in,
                            m_ffn2_w_out, m_ret_w_in, m_ret_head_norm, m_ret_w_out, m_gla_w_in, m_gla_w_gate,
                            m_gla_b_gate, m_gla_head_norm, m_gla_w_out, m_final_norm)))
    v = dict(zip(_WEIGHTS, (v_meta_tokens, v_norm_ffn1, v_ffn1_w_in, v_ffn1_w_out, v_norm_mix, v_norm_ffn2, v_ffn2_w_in,
                            v_ffn2_w_out, v_ret_w_in, v_ret_head_norm, v_ret_w_out, v_gla_w_in, v_gla_w_gate,
                            v_gla_b_gate, v_gla_head_norm, v_gla_w_out, v_final_norm)))

    exchange = _GradExchange()
    d_x = _sequence_grads(x[0], loss_target[0], p, _WeightGather(p), exchange)
    names = [("ffn2_w_in", 1), ("ffn2_w_out", 1), ("gla_w_in", 0), ("gla_w_out", 0), ("ffn1_w_in", 1), ("ffn1_w_out", 1),
             ("ffn2_w_in", 0), ("ffn2_w_out", 0), ("ret_w_in", 0), ("ret_w_out", 0), ("ffn1_w_in", 0), ("ffn1_w_out", 0)]
    shard, grads, delta, new_m, new_v = {}, {}, {}, {}, {}

    def swap(sums, keys, name):
        for key, a in zip(keys, _swap_cores(sums, name)):
            shard[key] = a.reshape(-1, a.shape[-1])

    def update(name):
        layers = p[name].shape[0]
        grads[name] = jnp.stack([shard[name, layer] for layer in range(layers)])
        delta[name], new_m[name], new_v[name] = _adamw(p[name], grads[name], m[name], v[name], f"adamw_{name}")

    swap([a for group in exchange.collect(range(5)) for a in group], names[:10], "swap_first")
    for name in ("ffn2_w_in", "ffn2_w_out", "ret_w_in", "ret_w_out", "gla_w_in", "gla_w_out"):
        update(name)
    last, (small_sum,) = exchange.collect([5, 6], after=list(delta.values()))
    swap(last, names[10:], "swap_last")
    for name in ("ffn1_w_in", "ffn1_w_out"):
        update(name)

    chip = 2 * lax.axis_index("x") + lax.axis_index("y")
    cols = lambda a, n: lax.dynamic_slice_in_dim(a, chip * n, n, axis=a.ndim - 1)
    (s_meta, s_n1a, s_n1b, s_nma, s_nmb, s_n2a, s_n2b, s_final, s_ret_gain, s_wg, s_bg, s_gla_gain,
     s_loss) = _unpack_rows(small_sum, exchange.small_shapes)
    grads.update({
        "meta_tokens": cols(s_meta, 256), "norm_ffn1": jnp.concatenate([s_n1a, s_n1b]),
        "norm_mix": jnp.concatenate([s_nma, s_nmb]), "norm_ffn2": jnp.concatenate([s_n2a, s_n2b]),
        "final_norm": s_final.reshape(D),
        "ret_head_norm": cols(s_ret_gain.reshape(1, HEADS, RET_DV), RET_DV // N_CHIPS),
        "gla_w_gate": cols(s_wg, GLA_DK)[None], "gla_b_gate": cols(s_bg, GLA_DK),
        "gla_head_norm": cols(s_gla_gain.reshape(1, HEADS, GLA_DV), GLA_DV // N_CHIPS),
    })
    for name in _LOCAL_SMALL:
        shape = p[name].shape
        as3d = lambda a: a.reshape((1,) * (3 - len(shape)) + shape)
        out = _adamw(as3d(p[name]), as3d(grads[name]), as3d(m[name]), as3d(v[name]), f"adamw_{name}")
        delta[name], new_m[name], new_v[name] = [a.reshape(shape) for a in out]

    return (s_loss.reshape(()), d_x[None], *[grads[n] for n in _WEIGHTS], *[delta[n] for n in _WEIGHTS],
            *[new_m[n] for n in _WEIGHTS], *[new_v[n] for n in _WEIGHTS])
```

```python
import functools

import jax
import numpy as np
import jax.numpy as jnp
from jax import lax
from jax.experimental import pallas as pl
from jax.experimental.pallas import tpu as pltpu

F32, BF16 = jnp.float32, jnp.bfloat16
MESH = pl.DeviceIdType.MESH

D = 1024
N_META = 16
CHUNK = 64
RET_CHUNK = 256
FRONT = 256
D_FF = 2816
EPS = 1e-6
HEADS = 4
RET_DK, RET_DV = 256, 512
GLA_DK, GLA_DV = 128, 256
GLA_RANK = 16
GLA_TAU = 16.0
GLA_IN = 2 * HEADS * GLA_DK + 2 * HEADS * GLA_DV + GLA_RANK
GLA_U = 3328
ROPE_BASE = 10000.0
N_CHIPS = 4
N_DEV = 8

ADAM_LR, ADAM_B1, ADAM_B2, ADAM_EPS, ADAM_WD, ADAM_STEP = 0.001, 0.9, 0.999, 1e-08, 0.01, 10

VMEM_LIMIT_BYTES = 56 * 1024 * 1024
TM = 768
TM_SMALL = 256


TM_RESIDENT = 384
MXU_TILE = 256


def _cp(n_axes):
    return pltpu.CompilerParams(dimension_semantics=("arbitrary",) * n_axes, vmem_limit_bytes=VMEM_LIMIT_BYTES)


def _resident(shape, n_axes):
    zeros = (0,) * len(shape)
    index = (lambda i: zeros) if n_axes == 1 else (lambda i, j: zeros)
    return pl.BlockSpec(shape, index, pipeline_mode=pl.Buffered(1))


def _dg(a, b, ca, cb):
    nb = a.ndim - 2
    dims = (((ca + nb,), (cb + nb,)), (tuple(range(nb)), tuple(range(nb))))
    return lax.dot_general(a.astype(BF16), b.astype(BF16), dims, preferred_element_type=F32)


@jax.custom_vjp
def _nn(a, b):
    return _dg(a, b, 1, 0)


@jax.custom_vjp
def _nt(a, b):
    return _dg(a, b, 1, 1)


@jax.custom_vjp
def _tn(a, b):
    return _dg(a, b, 0, 0)


def _dot_vjp(fn, ca, cb, da, db):
    def fwd(a, b):
        a, b = a.astype(BF16), b.astype(BF16)
        return _dg(a, b, ca, cb), (a, b)

    def bwd(res, g):
        a, b = res
        g = g.astype(BF16)
        grad = lambda other, dims, g_first: _dg(g, other, *dims) if g_first else _dg(other, g, *dims)
        return grad(b, *da), grad(a, *db)

    fn.defvjp(fwd, bwd)


_dot_vjp(_nn, 1, 0, ((1, 1), True), ((0, 0), False))
_dot_vjp(_nt, 1, 1, ((1, 0), True), ((0, 0), True))
_dot_vjp(_tn, 0, 0, ((1, 1), False), ((1, 0), False))


def _split_dot(m, a, parts):
    mb = jnp.broadcast_to(m, a.shape[:-2] + m.shape)
    total, rest = None, a
    for _ in range(parts):
        term = rest.astype(BF16)
        rest = rest - term.astype(F32)
        product = _dg(mb, term, 1, 0)
        total = product if total is None else total + product
    return total


def _make_cum(parts):
    @jax.custom_vjp
    def cum(m, mt, a):
        return _split_dot(m, a, parts)

    cum.defvjp(lambda m, mt, a: (_split_dot(m, a, parts), (m, mt)),
               lambda res, g: (jnp.zeros_like(res[0]), jnp.zeros_like(res[1]), _split_dot(res[1], g, parts)))
    return cum


_cum = _make_cum(3)
_cum16 = _make_cum(2)


def _sigmoid(x):
    return 1.0 / (1.0 + jnp.exp(-x))


def _rms(x):
    return lax.rsqrt(jnp.mean(x * x, axis=-1, keepdims=True) + EPS)


def _rmsnorm_bwd(dy, x, gain):
    r = _rms(x)
    xhat = x * r
    dxh = dy * gain
    return r * (dxh - xhat * jnp.mean(dxh * xhat, axis=-1, keepdims=True)), xhat


def _norm_proj(h, gain, w, name):
    tp, d = h.shape
    s, _, ns = w.shape

    tm = TM_RESIDENT

    def body(h_ref, g_ref, w_ref, hn_ref, u_ref):
        x = h_ref[...]
        a = (x * _rms(x) * g_ref[...]).astype(BF16)
        hn_ref[...] = a
        for k in range(s):
            u_ref[:, ns * k:ns * (k + 1)] = jnp.dot(a, w_ref[k], preferred_element_type=F32).astype(BF16)

    return pl.pallas_call(
        body, name=name, grid=(tp // tm,),
        in_specs=[pl.BlockSpec((tm, d), lambda i: (i, 0)), pl.BlockSpec((1, d), lambda i: (0, 0)), _resident(w.shape, 1)],
        out_specs=[pl.BlockSpec((tm, d), lambda i: (i, 0)), pl.BlockSpec((tm, s * ns), lambda i: (i, 0))],
        out_shape=[jax.ShapeDtypeStruct((tp, d), BF16), jax.ShapeDtypeStruct((tp, s * ns), BF16)],
        compiler_params=_cp(1))(h, gain, w)


def _norm_ffn_in(h, gain, w, name):
    tp, d = h.shape
    ff = w.shape[1] // 2
    tm = TM_RESIDENT
    blocks = [(c, min(c + 6 * MXU_TILE, ff)) for c in range(0, ff, 6 * MXU_TILE)]

    def body(h_ref, g_ref, w_ref, hn_ref, dg_ref, du_ref, act_ref):
        x = h_ref[...]
        a = (x * _rms(x) * g_ref[...]).astype(BF16)
        hn_ref[...] = a
        for c0, c1 in blocks:
            g = jnp.dot(a, w_ref[:, c0:c1], preferred_element_type=F32)
            u = jnp.dot(a, w_ref[:, ff + c0:ff + c1], preferred_element_type=F32)
            sg = _sigmoid(g)
            silu = g * sg
            dg_ref[:, c0:c1] = (u * (sg + silu * (1.0 - sg))).astype(BF16)
            du_ref[:, c0:c1] = silu.astype(BF16)
            act_ref[:, c0:c1] = (silu * u).astype(BF16)

    wide = jax.ShapeDtypeStruct((tp, ff), BF16)
    return pl.pallas_call(
        body, name=name, grid=(tp // tm,),
        in_specs=[pl.BlockSpec((tm, d), lambda i: (i, 0)), pl.BlockSpec((1, d), lambda i: (0, 0)),
                  _resident(w.shape, 1)],
        out_specs=[pl.BlockSpec((tm, d), lambda i: (i, 0))] + [pl.BlockSpec((tm, ff), lambda i: (i, 0))] * 3,
        out_shape=[jax.ShapeDtypeStruct((tp, d), BF16), wide, wide, wide],
        compiler_params=_cp(1))(h, gain, w)


def _out_proj(a, w, h, scale, name):
    tp, k = a.shape
    d = w.shape[1]

    def body(a_ref, w_ref, h_ref, o_ref):
        o_ref[...] = h_ref[...] + scale * jnp.dot(a_ref[...], w_ref[...], preferred_element_type=F32)

    return pl.pallas_call(
        body, name=name, grid=(tp // TM,),
        in_specs=[pl.BlockSpec((TM, k), lambda i: (i, 0)), pl.BlockSpec((k, d), lambda i: (0, 0)),
                  pl.BlockSpec((TM, d), lambda i: (i, 0))],
        out_specs=pl.BlockSpec((TM, d), lambda i: (i, 0)),
        out_shape=jax.ShapeDtypeStruct((tp, d), F32),
        compiler_params=_cp(1))(a, w, h)


def _dgrad(dh, w, name):
    tp, d = dh.shape
    k = w.shape[0]

    def body(dh_ref, w_ref, o_ref):
        o_ref[...] = lax.dot_general(dh_ref[...].astype(BF16), w_ref[...], (((1,), (1,)), ((), ())),
                                     preferred_element_type=F32).astype(BF16)

    return pl.pallas_call(
        body, name=name, grid=(tp // TM,),
        in_specs=[pl.BlockSpec((TM, d), lambda i: (i, 0)), pl.BlockSpec((k, d), lambda i: (0, 0))],
        out_specs=pl.BlockSpec((TM, k), lambda i: (i, 0)),
        out_shape=jax.ShapeDtypeStruct((tp, k), BF16),
        compiler_params=_cp(1))(dh, w)


def _wgrad(a, b, *, bm, bn, scale, sharded, name):
    tp, m = a.shape
    n = b.shape[1]
    nk = tp // TM

    def body(a_ref, b_ref, o_ref, acc_ref):
        k = pl.program_id(2)

        @pl.when(k == 0)
        def _():
            acc_ref[...] = jnp.zeros_like(acc_ref)

        bb = b_ref[...]
        if scale != 1.0:
            bb = scale * bb
        acc_ref[...] += lax.dot_general(a_ref[...], bb.astype(BF16), (((0,), (0,)), ((), ())),
                                        preferred_element_type=F32)

        @pl.when(k == nk - 1)
        def _():
            o_ref[...] = acc_ref[...].astype(BF16)

    if sharded:
        assert m == bm
        out_spec = pl.BlockSpec((None, bm, bn), lambda i, j, k: (j, 0, 0))
        out_shape = jax.ShapeDtypeStruct((n // bn, m, bn), BF16)
    else:
        out_spec = pl.BlockSpec((bm, bn), lambda i, j, k: (i, j))
        out_shape = jax.ShapeDtypeStruct((m, n), BF16)
    return pl.pallas_call(
        body, name=name, grid=(m // bm, n // bn, nk),
        in_specs=[pl.BlockSpec((TM, bm), lambda i, j, k: (k, i)), pl.BlockSpec((TM, bn), lambda i, j, k: (k, j))],
        out_specs=out_spec, out_shape=out_shape,
        scratch_shapes=[pltpu.VMEM((bm, bn), F32)],
        compiler_params=_cp(3))(a, b)


def _dgrad_norm(du, w, h, gain, dh_out, name):
    tp, d = h.shape
    s, _, ns = w.shape
    tm = TM_RESIDENT

    def body(du_ref, w_ref, h_ref, g_ref, dho_ref, dhi_ref, dg_ref):
        @pl.when(pl.program_id(0) == 0)
        def _():
            dg_ref[...] = jnp.zeros_like(dg_ref)

        dhn = None
        for k in range(s):
            part = lax.dot_general(du_ref[:, ns * k:ns * (k + 1)], w_ref[k], (((1,), (1,)), ((), ())),
                                   preferred_element_type=F32)
            dhn = part if dhn is None else dhn + part
        dx, xhat = _rmsnorm_bwd(dhn, h_ref[...], g_ref[...])
        dg_ref[...] += jnp.sum(dhn * xhat, axis=0, keepdims=True)
        dhi_ref[...] = dho_ref[...] + dx

    return pl.pallas_call(
        body, name=name, grid=(tp // tm,),
        in_specs=[pl.BlockSpec((tm, s * ns), lambda i: (i, 0)), _resident(w.shape, 1),
                  pl.BlockSpec((tm, d), lambda i: (i, 0)), pl.BlockSpec((1, d), lambda i: (0, 0)),
                  pl.BlockSpec((tm, d), lambda i: (i, 0))],
        out_specs=[pl.BlockSpec((tm, d), lambda i: (i, 0)), pl.BlockSpec((1, d), lambda i: (0, 0))],
        out_shape=[jax.ShapeDtypeStruct((tp, d), F32), jax.ShapeDtypeStruct((1, d), F32)],
        compiler_params=_cp(1))(du, w, h, gain, dh_out)


def _loss_head(h, gain, target, name):
    tp, d = h.shape
    tm = TM_SMALL
    front_tiles = FRONT // tm

    def body(h_ref, g_ref, t_ref, dh_ref, dg_ref, loss_ref):
        i = pl.program_id(0)

        @pl.when(i == 0)
        def _():
            dg_ref[...] = jnp.zeros_like(dg_ref)
            loss_ref[...] = jnp.zeros_like(loss_ref)

        x = h_ref[...]
        gain_v = g_ref[...]
        y = x * _rms(x) * gain_v
        err = jnp.where(i >= front_tiles, y - t_ref[...], 0.0)
        loss_ref[...] += 0.5 * jnp.sum(jnp.mean(err * err, axis=-1, keepdims=True), axis=0, keepdims=True)
        dy = err * (1.0 / d)
        dx, xhat = _rmsnorm_bwd(dy, x, gain_v)
        dg_ref[...] += jnp.sum(dy * xhat, axis=0, keepdims=True)
        dh_ref[...] = dx

    return pl.pallas_call(
        body, name=name, grid=(tp // tm,),
        in_specs=[pl.BlockSpec((tm, d), lambda i: (i, 0)), pl.BlockSpec((1, d), lambda i: (0, 0)),
                  pl.BlockSpec((tm, d), lambda i: (jnp.maximum(i - front_tiles, 0), 0))],
        out_specs=[pl.BlockSpec((tm, d), lambda i: (i, 0)), pl.BlockSpec((1, d), lambda i: (0, 0)),
                   pl.BlockSpec((1, 128), lambda i: (0, 0))],
        out_shape=[jax.ShapeDtypeStruct((tp, d), F32), jax.ShapeDtypeStruct((1, d), F32),
                   jax.ShapeDtypeStruct((1, 128), F32)],
        compiler_params=_cp(1))(h, gain, target)


def _gated_headnorm(o, g, gain):
    return o * _rms(o) * gain * (g * _sigmoid(g))


def _row_mask(chunk, size=CHUNK):
    rows = chunk * size + lax.broadcasted_iota(jnp.int32, (size, 1), 0)
    return (rows >= FRONT - N_META).astype(F32)


def _ret_head(q1, q2, k1, k2, v, g, state, gain, cos, sin, dmat, dq, dk, dc):
    q = jnp.concatenate([q1 * cos - q2 * sin, q1 * sin + q2 * cos], axis=-1)
    k = jnp.concatenate([k1 * cos - k2 * sin, k1 * sin + k2 * cos], axis=-1) * (RET_DK ** -0.5)
    scores = _nt(q, k) * dmat
    o = _nn(scores, v) + _nn(q * dq, state)
    new_state = state * dc + _tn(k * dk, v)
    return _gated_headnorm(o, g, gain), new_state


def _ret_consts():
    log_gamma = jnp.log1p(-2.0 ** (-5.0 - jnp.arange(HEADS, dtype=F32)))
    idx = jnp.arange(RET_CHUNK, dtype=F32)
    rel = idx[:, None] - idx[None, :]
    dmat = jnp.where(rel >= 0, jnp.exp(log_gamma[:, None, None] * jnp.maximum(rel, 0.0)), 0.0)
    dq = jnp.exp(log_gamma[:, None] * (idx + 1.0))[..., None]
    dk = jnp.exp(log_gamma[:, None] * (RET_CHUNK - 1.0 - idx))[..., None]
    dc = jnp.broadcast_to(jnp.exp(log_gamma * RET_CHUNK)[:, None, None], (HEADS, 1, 128))
    return dmat, dq, dk, dc


def _rope_tables(tp):
    half = RET_DK // 2
    inv = 1.0 / (ROPE_BASE ** jnp.linspace(0.0, 1.0, half, dtype=F32))
    pos = (jnp.arange(tp) - (FRONT - N_META)).astype(F32)
    ang = pos[:, None] * inv[None, :]
    return jnp.cos(ang), jnp.sin(ang)


_RET_V0, _RET_G0 = 2 * D, 4 * D


def _heads(ref, start, width, stride=None, rows=slice(None)):
    stride = width if stride is None else stride
    return jnp.stack([ref[rows, start + stride * h:start + stride * h + width].astype(F32) for h in range(HEADS)])


def _put_heads(ref, start, value, mask, stride=None, rows=slice(None)):
    width = value.shape[-1]
    stride = width if stride is None else stride
    for h in range(HEADS):
        ref[rows, start + stride * h:start + stride * h + width] = (value[h] * mask).astype(ref.dtype)


def _ret_pieces(u_ref):
    hk = RET_DK // 2
    return (_heads(u_ref, 0, hk, RET_DK), _heads(u_ref, hk, hk, RET_DK), _heads(u_ref, D, hk, RET_DK),
            _heads(u_ref, D + hk, hk, RET_DK), _heads(u_ref, _RET_V0, RET_DV), _heads(u_ref, _RET_G0, RET_DV))


def _ret_const_specs(rev=None):
    c = (lambda n: (rev(n), 0)) if rev else (lambda n: (n, 0))
    z3 = lambda n: (0, 0, 0)
    return [pl.BlockSpec((RET_CHUNK, RET_DK // 2), c), pl.BlockSpec((RET_CHUNK, RET_DK // 2), c),
            pl.BlockSpec((HEADS, RET_CHUNK, RET_CHUNK), z3), pl.BlockSpec((HEADS, RET_CHUNK, 1), z3),
            pl.BlockSpec((HEADS, RET_CHUNK, 1), z3), pl.BlockSpec((HEADS, 1, 128), z3)]


def _ret_fwd(u, gain, rope, h, w_out, name):
    tp = u.shape[0]
    nch = tp // RET_CHUNK
    cos, sin = rope
    dmat, dq, dk, dc = _ret_consts()

    def body(u_ref, gain_ref, h_ref, w_ref, cos_ref, sin_ref, dmat_ref, dq_ref, dk_ref, dc_ref,
             on_ref, st_ref, hmix_ref, state_ref):
        @pl.when(pl.program_id(0) == 0)
        def _():
            state_ref[...] = jnp.zeros_like(state_ref)

        state = state_ref[...]
        st_ref[...] = state.astype(BF16)
        on, new_state = _ret_head(*_ret_pieces(u_ref), state, _heads(gain_ref, 0, RET_DV), cos_ref[...], sin_ref[...],
                                  dmat_ref[...], dq_ref[...], dk_ref[...], dc_ref[...][:, :, :1])
        state_ref[...] = new_state
        _put_heads(on_ref, 0, on, 1.0)
        hmix_ref[...] = h_ref[...] + jnp.dot(on_ref[...], w_ref[...], preferred_element_type=F32)

    rows = lambda width: pl.BlockSpec((RET_CHUNK, width), lambda n: (n, 0))
    return pl.pallas_call(
        body, name=name, grid=(nch,),
        in_specs=[rows(6 * D), pl.BlockSpec((1, HEADS * RET_DV), lambda n: (0, 0)), rows(D),
                  _resident(w_out.shape, 1)] + _ret_const_specs(),
        out_specs=[rows(HEADS * RET_DV), pl.BlockSpec((None, HEADS, RET_DK, RET_DV), lambda n: (n, 0, 0, 0)), rows(D)],
        out_shape=[jax.ShapeDtypeStruct((tp, HEADS * RET_DV), BF16),
                   jax.ShapeDtypeStruct((nch, HEADS, RET_DK, RET_DV), BF16), jax.ShapeDtypeStruct((tp, D), F32)],
        scratch_shapes=[pltpu.VMEM((HEADS, RET_DK, RET_DV), F32)],
        compiler_params=_cp(1))(u, gain, h, w_out, cos, sin, dmat, dq, dk, dc)


def _ret_bwd(u, gain, rope, states, d_on, name):
    tp = u.shape[0]
    nch = tp // RET_CHUNK
    cos, sin = rope
    dmat, dq, dk, dc = _ret_consts()
    rev = lambda n: nch - 1 - n
    hk = RET_DK // 2

    def body(u_ref, gain_ref, st_ref, don_ref, cos_ref, sin_ref, dmat_ref, dq_ref, dk_ref, dc_ref,
             du_ref, dgain_ref, dstate_ref):
        @pl.when(pl.program_id(0) == 0)
        def _():
            dstate_ref[...] = jnp.zeros_like(dstate_ref)
            dgain_ref[...] = jnp.zeros_like(dgain_ref)

        mask = _row_mask(rev(pl.program_id(0)), RET_CHUNK)
        consts = (cos_ref[...], sin_ref[...], dmat_ref[...], dq_ref[...], dk_ref[...], dc_ref[...][:, :, :1])
        _, vjp = jax.vjp(lambda *a: _ret_head(*a, *consts), *_ret_pieces(u_ref), st_ref[...].astype(F32),
                         _heads(gain_ref, 0, RET_DV))
        dq1, dq2, dk1, dk2, dv, dg, dstate, dgain = vjp((_heads(don_ref, 0, RET_DV), dstate_ref[...]))
        dstate_ref[...] = dstate
        for hd in range(HEADS):
            dgain_ref[:, RET_DV * hd:RET_DV * (hd + 1)] += dgain[hd]
        _put_heads(du_ref, 0, dq1, mask, RET_DK)
        _put_heads(du_ref, hk, dq2, mask, RET_DK)
        _put_heads(du_ref, D, dk1, mask, RET_DK)
        _put_heads(du_ref, D + hk, dk2, mask, RET_DK)
        _put_heads(du_ref, _RET_V0, dv, mask)
        _put_heads(du_ref, _RET_G0, dg, mask)

    return pl.pallas_call(
        body, name=name, grid=(nch,),
        in_specs=[pl.BlockSpec((RET_CHUNK, 6 * D), lambda n: (rev(n), 0)),
                  pl.BlockSpec((1, HEADS * RET_DV), lambda n: (0, 0)),
                  pl.BlockSpec((None, HEADS, RET_DK, RET_DV), lambda n: (rev(n), 0, 0, 0)),
                  pl.BlockSpec((RET_CHUNK, HEADS * RET_DV), lambda n: (rev(n), 0))] + _ret_const_specs(rev),
        out_specs=[pl.BlockSpec((RET_CHUNK, 6 * D), lambda n: (rev(n), 0)),
                   pl.BlockSpec((1, HEADS * RET_DV), lambda n: (0, 0))],
        out_shape=[jax.ShapeDtypeStruct((tp, 6 * D), BF16), jax.ShapeDtypeStruct((1, HEADS * RET_DV), F32)],
        scratch_shapes=[pltpu.VMEM((HEADS, RET_DK, RET_DV), F32)],
        compiler_params=_cp(1))(u, gain, states, d_on, cos, sin, dmat, dq, dk, dc)


_GLA_K0, _GLA_V0, _GLA_G0, _GLA_Z0 = 512, 1024, 2048, 3072


def _gla_head(q, k, v, g, z, state_t, wg, bg, gain, mask, lo, lo_t, to_mid, to_mid_t, in_second, pair):
    ga = _nn(jnp.broadcast_to(z, wg.shape[:-2] + z.shape), wg) + bg
    log_a = (jnp.minimum(ga, 0.0) - jnp.log(1.0 + jnp.exp(-jnp.abs(ga)))) * (mask * (1.0 / GLA_TAU))
    bcum = _cum(lo, lo_t, log_a)
    btot = jnp.sum(log_a, axis=-2, keepdims=True)
    qs = q * (GLA_DK ** -0.5)
    heads, levels = q.shape[0], pair.shape[0]
    decay = jnp.exp(_cum16(to_mid, to_mid_t, log_a).reshape(heads, levels, CHUNK, GLA_DK))
    qk = (jnp.where(in_second > 0.0, qs[:, None], k[:, None]) * decay).reshape(heads * levels, CHUNK, GLA_DK)
    rows = lax.broadcasted_iota(jnp.int32, (CHUNK, CHUNK), 0)
    cols = lax.broadcasted_iota(jnp.int32, (CHUNK, CHUNK), 1)
    scores = (jnp.where(rows == cols, _nt(qs, k), 0.0)
              + jnp.sum(_nt(qk, qk).reshape(heads, levels, CHUNK, CHUNK) * pair, axis=1))
    o = _nn(scores, v) + _nt(qs * jnp.exp(bcum), state_t)
    new_state_t = state_t * jnp.exp(btot) + _tn(v, k * jnp.exp(btot - bcum))
    return _gated_headnorm(o, g, gain), new_state_t


def _gla_consts():
    r, c = np.meshgrid(np.arange(CHUNK), np.arange(CHUNK), indexing="ij")
    to_mid, second, pair = [], [], []
    block = 2
    while block <= CHUNK:
        mid = (r // block) * block + block // 2
        to_mid.append(((r >= mid) & (c > mid) & (c <= r)) | ((r < mid) & (c > r) & (c <= mid)))
        second.append((r >= mid)[:, :1])
        pair.append((r // block == c // block) & (r >= mid) & (c < mid))
        block *= 2
    to_mid = np.concatenate(to_mid)
    bf = lambda m: jnp.asarray(m, F32).astype(BF16)
    f32 = lambda ms: jnp.asarray(np.stack(ms), F32)
    return bf(r >= c), bf(c >= r), bf(to_mid), bf(to_mid.T), f32(second), f32(pair)


def _gla_const_specs(consts):
    return [pl.BlockSpec(a.shape, functools.partial(lambda nd, n: (0,) * nd, a.ndim)) for a in consts]


GLA_STEP_CHUNKS = 6


def _gla_pieces(u_ref, rows):
    return (_heads(u_ref, 0, GLA_DK, rows=rows), _heads(u_ref, _GLA_K0, GLA_DK, rows=rows),
            _heads(u_ref, _GLA_V0, GLA_DV, rows=rows), _heads(u_ref, _GLA_G0, GLA_DV, rows=rows),
            u_ref[rows, _GLA_Z0:_GLA_Z0 + 128].astype(F32))


def _gla_fwd(u, wg, bg, gain, name):
    tp = u.shape[0]
    nch = tp // CHUNK
    per = GLA_STEP_CHUNKS
    consts = _gla_consts()

    def body(u_ref, wg_ref, bg_ref, gain_ref, *refs):
        const_refs, (on_ref, st_ref, state_ref) = refs[:len(consts)], refs[len(consts):]

        @pl.when(pl.program_id(0) == 0)
        def _():
            state_ref[...] = jnp.zeros_like(state_ref)

        params = (_heads(wg_ref, 0, GLA_DK), _heads(bg_ref, 0, GLA_DK), _heads(gain_ref, 0, GLA_DV))
        mats = [ref[...] for ref in const_refs]
        state = state_ref[...]
        for c in range(per):
            rows = slice(CHUNK * c, CHUNK * (c + 1))
            st_ref[c] = state.astype(BF16)
            on, state = _gla_head(*_gla_pieces(u_ref, rows), state, *params, _row_mask(pl.program_id(0) * per + c), *mats)
            _put_heads(on_ref, 0, on, 1.0, rows=rows)
        state_ref[...] = state

    rows_spec = lambda width: pl.BlockSpec((per * CHUNK, width), lambda n: (n, 0))
    full = lambda r, c: pl.BlockSpec((r, c), lambda n: (0, 0))
    return pl.pallas_call(
        body, name=name, grid=(nch // per,),
        in_specs=[rows_spec(GLA_U), full(128, HEADS * GLA_DK), full(1, HEADS * GLA_DK), full(1, HEADS * GLA_DV)]
                 + _gla_const_specs(consts),
        out_specs=[rows_spec(HEADS * GLA_DV), pl.BlockSpec((per, HEADS, GLA_DV, GLA_DK), lambda n: (n, 0, 0, 0))],
        out_shape=[jax.ShapeDtypeStruct((tp, HEADS * GLA_DV), BF16),
                   jax.ShapeDtypeStruct((nch, HEADS, GLA_DV, GLA_DK), BF16)],
        scratch_shapes=[pltpu.VMEM((HEADS, GLA_DV, GLA_DK), F32)],
        compiler_params=_cp(1))(u, wg, bg, gain, *consts)


def _gla_bwd(u, wg, bg, gain, states, d_on, name):
    tp = u.shape[0]
    per = GLA_STEP_CHUNKS
    steps = tp // (per * CHUNK)
    rev = lambda n: steps - 1 - n
    consts = _gla_consts()

    def body(u_ref, wg_ref, bg_ref, gain_ref, st_ref, don_ref, *refs):
        const_refs, (du_ref, dwg_ref, dbg_ref, dgain_ref, dstate_ref) = refs[:len(consts)], refs[len(consts):]

        @pl.when(pl.program_id(0) == 0)
        def _():
            dstate_ref[...] = jnp.zeros_like(dstate_ref)
            dwg_ref[...] = jnp.zeros_like(dwg_ref)
            dbg_ref[...] = jnp.zeros_like(dbg_ref)
            dgain_ref[...] = jnp.zeros_like(dgain_ref)

        params = (_heads(wg_ref, 0, GLA_DK), _heads(bg_ref, 0, GLA_DK), _heads(gain_ref, 0, GLA_DV))
        mats = [ref[...] for ref in const_refs]
        dstate = dstate_ref[...]
        for c in reversed(range(per)):
            rows = slice(CHUNK * c, CHUNK * (c + 1))
            mask = _row_mask(rev(pl.program_id(0)) * per + c)
            _, vjp = jax.vjp(lambda *a: _gla_head(*a, mask, *mats), *_gla_pieces(u_ref, rows),
                             st_ref[c].astype(F32), *params)
            dq, dk, dv, dg, dz, dstate, dwg, dbg, dgain = vjp((_heads(don_ref, 0, GLA_DV, rows=rows), dstate))
            for hd in range(HEADS):
                dwg_ref[:, GLA_DK * hd:GLA_DK * (hd + 1)] += dwg[hd]
                dbg_ref[:, GLA_DK * hd:GLA_DK * (hd + 1)] += dbg[hd]
                dgain_ref[:, GLA_DV * hd:GLA_DV * (hd + 1)] += dgain[hd]
            _put_heads(du_ref, 0, dq, mask, rows=rows)
            _put_heads(du_ref, _GLA_K0, dk, mask, rows=rows)
            _put_heads(du_ref, _GLA_V0, dv, mask, rows=rows)
            _put_heads(du_ref, _GLA_G0, dg, mask, rows=rows)
            du_ref[rows, _GLA_Z0:_GLA_Z0 + 128] = dz.astype(BF16)
            du_ref[rows, _GLA_Z0 + 128:] = jnp.zeros((CHUNK, GLA_U - _GLA_Z0 - 128), BF16)
        dstate_ref[...] = dstate

    full = lambda r, c: pl.BlockSpec((r, c), lambda n: (0, 0))
    return pl.pallas_call(
        body, name=name, grid=(steps,),
        in_specs=[pl.BlockSpec((per * CHUNK, GLA_U), lambda n: (rev(n), 0)), full(128, HEADS * GLA_DK),
                  full(1, HEADS * GLA_DK), full(1, HEADS * GLA_DV),
                  pl.BlockSpec((per, HEADS, GLA_DV, GLA_DK), lambda n: (rev(n), 0, 0, 0)),
                  pl.BlockSpec((per * CHUNK, HEADS * GLA_DV), lambda n: (rev(n), 0))] + _gla_const_specs(consts),
        out_specs=[pl.BlockSpec((per * CHUNK, GLA_U), lambda n: (rev(n), 0)), full(128, HEADS * GLA_DK),
                   full(1, HEADS * GLA_DK), full(1, HEADS * GLA_DV)],
        out_shape=[jax.ShapeDtypeStruct((tp, GLA_U), BF16), jax.ShapeDtypeStruct((128, HEADS * GLA_DK), F32),
                   jax.ShapeDtypeStruct((1, HEADS * GLA_DK), F32), jax.ShapeDtypeStruct((1, HEADS * GLA_DV), F32)],
        scratch_shapes=[pltpu.VMEM((HEADS, GLA_DV, GLA_DK), F32)],
        compiler_params=_cp(1))(u, wg, bg, gain, states, d_on, *consts)


def _ffn_fwd(h, gain, w_in, w_out, tag):
    hn, ug, uu, act = _norm_ffn_in(h, gain, w_in, f"{tag}_in")
    if callable(w_out):
        w_out = w_out(act)
    return _out_proj(act, w_out, h, 0.5, f"{tag}_out"), (h, hn, ug, uu, act), w_out


def _ffn_dgrad(dh, w_out, w_in, act_dg, act_du, h, gain, name, split_front=False):
    tp, d = dh.shape
    ff = w_out.shape[0]
    tm = TM_SMALL
    nt = (((1,), (1,)), ((), ()))

    def body(dh_ref, wo_ref, wi_ref, dg_ref, du_ref, h_ref, g_ref, o_ref, *out_refs):
        dhi_ref, dgain_ref = out_refs[-2:]

        @pl.when(pl.program_id(0) == 0)
        def _():
            dgain_ref[...] = jnp.zeros_like(dgain_ref)

        dho = dh_ref[...]
        dact = lax.dot_general((0.5 * dho).astype(BF16), wo_ref[...], nt, preferred_element_type=F32)
        d_gate = (dact * dg_ref[...].astype(F32)).astype(BF16)
        d_up = (dact * du_ref[...].astype(F32)).astype(BF16)
        o_ref[:, :ff] = d_gate
        o_ref[:, ff:] = d_up
        dhn = (lax.dot_general(d_gate, wi_ref[:, :ff], nt, preferred_element_type=F32)
               + lax.dot_general(d_up, wi_ref[:, ff:], nt, preferred_element_type=F32))
        dx, xhat = _rmsnorm_bwd(dhn, h_ref[...], g_ref[...])
        dgain_ref[...] += jnp.sum(dhn * xhat, axis=0, keepdims=True)
        dhi_ref[...] = dho + dx
        if split_front:
            @pl.when(pl.program_id(0) == 0)
            def _():
                out_refs[0][...] = dho + dx

    rows = lambda width: pl.BlockSpec((tm, width), lambda i: (i, 0))
    if split_front:
        assert tm == FRONT
        dhi_specs = [pl.BlockSpec((tm, d), lambda i: (0, 0)), pl.BlockSpec((tm, d), lambda i: (jnp.maximum(i - 1, 0), 0))]
        dhi_shapes = [jax.ShapeDtypeStruct((FRONT, d), F32), jax.ShapeDtypeStruct((tp - FRONT, d), F32)]
    else:
        dhi_specs, dhi_shapes = [rows(d)], [jax.ShapeDtypeStruct((tp, d), F32)]
    out = pl.pallas_call(
        body, name=name, grid=(tp // tm,),
        in_specs=[rows(d), _resident(w_out.shape, 1), _resident(w_in.shape, 1), rows(ff), rows(ff), rows(d),
                  pl.BlockSpec((1, d), lambda i: (0, 0))],
        out_specs=[rows(2 * ff), *dhi_specs, pl.BlockSpec((1, d), lambda i: (0, 0))],
        out_shape=[jax.ShapeDtypeStruct((tp, 2 * ff), BF16), *dhi_shapes, jax.ShapeDtypeStruct((1, d), F32)],
        compiler_params=_cp(1))(dh, w_out, w_in, act_dg, act_du, h, gain)
    return (out[0], tuple(out[1:3]), out[3]) if split_front else tuple(out)


def _ffn_bwd(dh, saved, gain, w_in, w_out, tag, push, split_front=False):
    h, hn, act_dg, act_du, act = saved
    du, dh_in, d_gain = _ffn_dgrad(dh, w_out, w_in, act_dg, act_du, h, gain, f"{tag}_dgrad", split_front)
    d_w_out = _wgrad(act, dh, bm=D_FF // 2, bn=D, scale=0.5, sharded=False, name=f"{tag}_dwout")
    d_w_in = _wgrad(hn, du, bm=D, bn=D_FF, scale=1.0, sharded=False, name=f"{tag}_dwin")
    return dh_in, d_gain, push([("cols", d_w_in), d_w_out])


def _sequence_grads(x, target, p, weights, grads):
    row = lambda v, token: v.reshape(1, -1) + token[0, 0]
    gains = {}

    tok = weights.start(1, weights.start(0, None))
    weights.pin = tok
    h = jnp.concatenate([jnp.zeros((FRONT, D), F32), x], axis=0) + tok[0, 0]
    rope = _rope_tables(h.shape[0])
    w = weights.wait(0, [tok, h, *rope, *weights.later_shards(2)])
    tok = weights.start(2, w["l0_ffn1_in"])
    h = lax.dynamic_update_slice(h, w["meta"], (FRONT - N_META, 0))
    gains["l0_ffn1"] = row(p["norm_ffn1"][0], tok)
    h, s1, w["l0_ffn1_out"] = _ffn_fwd(h, gains["l0_ffn1"], w["l0_ffn1_in"],
                                       lambda act: weights.wait(1, act)["l0_ffn1_out"], "l0_ffn1")
    w.update(weights.wait(2, h))
    tok = weights.start(4, weights.start(3, w["ret_in"]))
    gains["ret"] = row(p["norm_mix"][0], tok)
    hn, u = _norm_proj(h, gains["ret"], w["ret_in"], "ret_in")
    w.update(weights.wait(3, u))
    on, states, h_mix = _ret_fwd(u, w["ret_gain"], rope, h, w["ret_out"], "ret_fwd")
    s2 = (h, hn, u, on, states)
    w.update(weights.wait(4, h_mix))
    tok = weights.start(5, w["l0_ffn2_in"])
    gains["l0_ffn2"] = row(p["norm_ffn2"][0], tok)
    h, s3, _ = _ffn_fwd(h_mix, gains["l0_ffn2"], w["l0_ffn2_in"], w["l0_ffn2_out"], "l0_ffn2")
    saved = [(s1, s2, s3)]

    w.update(weights.wait(5, h))
    tok = weights.start(6, w["l1_ffn1_in"])
    gains["l1_ffn1"] = row(p["norm_ffn1"][1], tok)
    h, s1, _ = _ffn_fwd(h, gains["l1_ffn1"], w["l1_ffn1_in"], w["l1_ffn1_out"], "l1_ffn1")
    w.update(weights.wait(6, h))
    tok = weights.start(7, w["gla_out"])
    gains["gla"] = row(p["norm_mix"][1], tok)
    hn, u = _norm_proj(h, gains["gla"], w["gla_in"], "gla_in")
    on, states = _gla_fwd(u, w["gla_wg"], w["gla_bg"], w["gla_gain"], "gla_fwd")
    h_mix = _out_proj(on, w["gla_out"], h, 1.0, "gla_out")
    s2 = (h, hn, u, on, states)
    w.update(weights.wait(7, h_mix))
    gains["l1_ffn2"] = p["norm_ffn2"][1].reshape(1, -1)
    h, s3, _ = _ffn_fwd(h_mix, gains["l1_ffn2"], w["l1_ffn2_in"], w["l1_ffn2_out"], "l1_ffn2")
    saved.append((s1, s2, s3))

    dh, d_final, loss = _loss_head(h, p["final_norm"].reshape(1, -1), target, "loss_head")
    small = {"final_norm": d_final, "norm_ffn1": [None, None], "norm_mix": [None, None], "norm_ffn2": [None, None]}
    pusher = lambda k: functools.partial(grads.push, k)

    s1, s2, s3 = saved[1]
    dh, small["norm_ffn2"][1], tok = _ffn_bwd(dh, s3, gains["l1_ffn2"], w["l1_ffn2_in"], w["l1_ffn2_out"], "l1_ffn2",
                                              pusher(0))
    h_in, hn, u, on, states = s2
    d_on = _dgrad(dh, w["gla_out"], "gla_don")
    d_out = _wgrad(on, dh, bm=D, bn=D, scale=1.0, sharded=False, name="gla_dwout")
    du, small["gla_wg"], small["gla_bg"], small["gla_gain"] = _gla_bwd(
        u, w["gla_wg"], w["gla_bg"], w["gla_gain"] + tok[0, 0], states, d_on, "gla_bwd")
    d_in = _wgrad(hn, du, bm=D, bn=GLA_U, scale=1.0, sharded=False, name="gla_dwin")
    d_in = jnp.moveaxis(d_in[:, :GLA_IN].reshape(D, N_CHIPS, -1), 1, 0)
    tok = grads.push(1, [d_in, d_out])
    dh, small["norm_mix"][1] = _dgrad_norm(du, w["gla_in"], h_in, gains["gla"] + tok[0, 0], dh, "gla_dnorm")
    dh, small["norm_ffn1"][1], tok = _ffn_bwd(dh, s1, gains["l1_ffn1"], w["l1_ffn1_in"], w["l1_ffn1_out"], "l1_ffn1",
                                              pusher(2))

    s1, s2, s3 = saved[0]
    dh, small["norm_ffn2"][0], tok = _ffn_bwd(dh, s3, gains["l0_ffn2"] + tok[0, 0], w["l0_ffn2_in"],
                                              w["l0_ffn2_out"], "l0_ffn2", pusher(3))
    h_in, hn, u, on, states = s2
    d_on = _dgrad(dh, w["ret_out"], "ret_don")
    d_out = _wgrad(on, dh, bm=D, bn=D, scale=1.0, sharded=False, name="ret_dwout")
    du, small["ret_gain"] = _ret_bwd(u, w["ret_gain"] + tok[0, 0], rope, states, d_on, "ret_bwd")
    d_in = _wgrad(hn, du, bm=D, bn=w["ret_in"].shape[2], scale=1.0, sharded=True, name="ret_dwin")
    tok = grads.push(4, [d_in, d_out])
    dh, small["norm_mix"][0] = _dgrad_norm(du, w["ret_in"], h_in, gains["ret"] + tok[0, 0], dh, "ret_dnorm")
    (d_front, d_x), small["norm_ffn1"][0], tok = _ffn_bwd(dh, s1, gains["l0_ffn1"], w["l0_ffn1_in"], w["l0_ffn1_out"],
                                                          "l0_ffn1", pusher(5), split_front=True)
    grads.push(6, [], [d_front[FRONT - N_META:], *small["norm_ffn1"], *small["norm_mix"], *small["norm_ffn2"],
                       small["final_norm"], small["ret_gain"], small["gla_wg"][:GLA_RANK], small["gla_bg"],
                       small["gla_gain"], loss[:, :1] + tok[0, 0]])
    return d_x


_HBM = pl.BlockSpec(memory_space=pl.ANY)


def _place():
    return lax.axis_index("x"), lax.axis_index("y"), lax.axis_index("c")


def _flip(v, bit):
    return 1 - v if bit else v


DMA_CHUNK_BYTES = 128 * 1024


def _row_chunks(ref):
    rows, cols = ref.shape
    step = _row_tile(rows, max(16, DMA_CHUNK_BYTES // (cols * ref.dtype.itemsize)))
    return [pl.ds(a, step) for a in range(0, rows, step)]


def _whole(src, dst, send_sem, recv_sem, peer):
    return pltpu.make_async_remote_copy(src_ref=src, dst_ref=dst, send_sem=send_sem, recv_sem=recv_sem,
                                        device_id=peer, device_id_type=MESH)


def _send(src, dst, send_sem, recv_sem, peer):
    for rows in _row_chunks(src):
        _whole(src.at[rows], dst.at[rows], send_sem, recv_sem, peer).start()
    return _whole(src, dst, send_sem, recv_sem, peer)


_HBM_ONLY = pl.BlockSpec(memory_space=pltpu.HBM)
_SEMS = pl.BlockSpec(memory_space=pltpu.SEMAPHORE)
_SIDE_EFFECT = pltpu.CompilerParams(has_side_effects=pltpu.SideEffectType.DATAFLOW_SIDE_EFFECTING)
_GATHER_FLIPS = [(1, 0, 0), (0, 1, 0), (1, 1, 0), (0, 0, 1)]
_PEER_FLIPS = [(fx, fy, fc) for fx in (0, 1) for fy in (0, 1) for fc in (0, 1)][1:]


def _zero_token():
    return jnp.zeros((8, 128), F32)


def _exchange_start(srcs, lands, route, flips, after, name):
    n = len(srcs)

    def body(*refs):
        src, land = refs[:n], refs[n:2 * n]
        send_sems, recv_sems, token = refs[2 * n + 1], refs[2 * n + 2], refs[-1]
        me = _place()
        for t in range(n):
            for j, flip in enumerate(flips):
                peer = tuple(_flip(v, f) for v, f in zip(me, flip))
                s, d = route(t, src[t], land[t], me, peer)
                _send(s, d, send_sems.at[t * len(flips) + j], recv_sems.at[t * len(flips) + j], peer)
        token[...] = jnp.zeros_like(token)

    hbm = lambda a: pltpu.HBM(a.shape, a.dtype)
    sems = pltpu.SemaphoreType.DMA((n * len(flips),))
    operands = [pltpu.with_memory_space_constraint(a, pltpu.HBM) for a in list(srcs) + list(lands)]
    out = pl.pallas_call(
        body, name=name, in_specs=[_HBM_ONLY] * (2 * n) + [_HBM],
        out_shape=(sems, sems, *[hbm(a) for a in operands], jax.ShapeDtypeStruct((8, 128), F32)),
        out_specs=(_SEMS, _SEMS, *[_HBM_ONLY] * (2 * n), pl.BlockSpec(memory_space=pltpu.VMEM)),
        input_output_aliases={i: 2 + i for i in range(2 * n)}, compiler_params=_SIDE_EFFECT,
    )(*operands, _zero_token() if after is None else after)
    return (out[0], out[1], out[2:2 + n], out[2 + n:2 + 2 * n]), out[-1]


def _exchange_wait(started, route, flips, after, name):
    send_sems, recv_sems, srcs, lands = started
    n = len(srcs)

    def body(*refs):
        src, land = refs[:n], refs[n:2 * n]
        send_sems, recv_sems = refs[2 * n], refs[2 * n + 1]
        me = _place()
        for t in range(n):
            for j, flip in enumerate(flips):
                peer = tuple(_flip(v, f) for v, f in zip(me, flip))
                s, d = route(t, src[t], land[t], me, peer)
                cp = _whole(s, d, send_sems.at[t * len(flips) + j], recv_sems.at[t * len(flips) + j], peer)
                cp.wait_send()
                cp.wait_recv()

    hbm = lambda a: pltpu.HBM(a.shape, a.dtype)
    after = list(after) if isinstance(after, (list, tuple)) else [after]
    out = pl.pallas_call(
        body, name=name, in_specs=[_HBM_ONLY] * (2 * n) + [_SEMS, _SEMS] + [_HBM] * len(after),
        out_shape=tuple(hbm(a) for a in list(srcs) + list(lands)), out_specs=tuple([_HBM_ONLY] * (2 * n)),
        input_output_aliases={i: i for i in range(2 * n)}, compiler_params=_SIDE_EFFECT,
    )(*srcs, *lands, send_sems, recv_sems, *after)
    return out[:n], out[n:]


def _gather_route(t, src, land, me, peer):
    mine = 2 * me[0] + me[1]
    if land.ndim == 3:
        return src, land.at[mine]
    cols = src.shape[1]
    return src, land.at[:, pl.ds(pl.multiple_of(mine * cols, 128), cols)]


def _scatter_route(n_pieces):
    def route(t, src, land, me, peer):
        chip = 2 * peer[0] + peer[1]
        if t >= n_pieces:
            part = src
        elif src.ndim == 4:
            part = src.at[chip, peer[2]]
        else:
            rows, cols = land.shape[1:]
            part = src.at[pl.ds(pl.multiple_of(peer[2] * rows, 16), rows), pl.ds(pl.multiple_of(chip * cols, 128), cols)]
        return part, land.at[4 * me[0] + 2 * me[1] + me[2]]

    return route


def _swap_cores(halves, name):
    n = len(halves)

    def body(*refs):
        src, dst = refs[:n], refs[n:2 * n]
        send_sems, recv_sems = refs[2 * n:]
        x, y, c = _place()
        copies = [_send(src[t], dst[t], send_sems.at[t], recv_sems.at[t], (x, y, 1 - c)) for t in range(n)]
        for cp in copies:
            cp.wait()

    got = pl.pallas_call(
        body, name=name, in_specs=[_HBM] * n, out_specs=[_HBM] * n,
        out_shape=[jax.ShapeDtypeStruct(a.shape, a.dtype) for a in halves],
        scratch_shapes=[pltpu.SemaphoreType.DMA((n,)), pltpu.SemaphoreType.DMA((n,))],
    )(*halves)
    south = lax.axis_index("c") == 0
    return [jnp.stack([jnp.where(south, a, b), jnp.where(south, b, a)]) for a, b in zip(halves, got)]


def _row_tile(rows, cap):
    fits = [t for t in range(16, cap + 1, 16) if rows % t == 0]
    return fits[-1] if fits else rows


def _sum_slots(a, name):
    _, r, c = a.shape
    tr = _row_tile(r, 384)

    def body(a_ref, o_ref):
        s = a_ref[0].astype(F32)
        for k in range(1, N_DEV):
            s = s + a_ref[k].astype(F32)
        o_ref[...] = s

    return pl.pallas_call(
        body, name=name, grid=(r // tr,),
        in_specs=[pl.BlockSpec((N_DEV, tr, c), lambda i: (0, i, 0))],
        out_specs=pl.BlockSpec((tr, c), lambda i: (i, 0)),
        out_shape=jax.ShapeDtypeStruct((r, c), F32),
        compiler_params=_cp(1))(a)


def _adamw(w, g, m, v, name):
    layers, r, c = w.shape
    tr = _row_tile(r, 256)

    def body(w_ref, g_ref, m_ref, v_ref, d_ref, nm_ref, nv_ref):
        gv = g_ref[...]
        nm = ADAM_B1 * m_ref[...] + (1.0 - ADAM_B1) * gv
        nv = ADAM_B2 * v_ref[...] + (1.0 - ADAM_B2) * (gv * gv)
        m_hat = nm / (1.0 - ADAM_B1 ** ADAM_STEP)
        v_hat = nv / (1.0 - ADAM_B2 ** ADAM_STEP)
        d_ref[...] = -ADAM_LR * (m_hat / (jnp.sqrt(v_hat) + ADAM_EPS) + ADAM_WD * w_ref[...])
        nm_ref[...] = nm
        nv_ref[...] = nv

    spec = pl.BlockSpec((None, tr, c), lambda a, i: (a, i, 0))
    return pl.pallas_call(
        body, name=name, grid=(layers, r // tr), in_specs=[spec] * 4, out_specs=[spec] * 3,
        out_shape=[jax.ShapeDtypeStruct((layers, r, c), F32)] * 3,
        compiler_params=_cp(2))(*[pltpu.with_memory_space_constraint(a, pltpu.HBM) for a in (w, g, m, v)])


_SMALL = ["meta_tokens", "ret_head_norm", "gla_w_gate", "gla_b_gate", "gla_head_norm"]
_LOCAL_SMALL = ["meta_tokens", "norm_ffn1", "norm_mix", "norm_ffn2", "ret_head_norm", "gla_w_gate", "gla_b_gate",
                "gla_head_norm", "final_norm"]
_WEIGHTS = ["meta_tokens", "norm_ffn1", "ffn1_w_in", "ffn1_w_out", "norm_mix", "norm_ffn2", "ffn2_w_in", "ffn2_w_out",
            "ret_w_in", "ret_head_norm", "ret_w_out", "gla_w_in", "gla_w_gate", "gla_b_gate", "gla_head_norm",
            "gla_w_out", "final_norm"]


def _pack_rows(arrays, width):
    flat = jnp.concatenate([a.reshape(-1) for a in arrays])
    pad = -flat.shape[0] % (8 * width)
    return jnp.pad(flat, (0, pad)).reshape(-1, width)


def _unpack_rows(packed, shapes):
    flat, out, at = packed.reshape(-1), [], 0
    for s in shapes:
        size = 1
        for dim in s:
            size *= dim
        out.append(flat[at:at + size].reshape(s))
        at += size
    return out


class _WeightGather:
    GROUPS = [("small", "l0_ffn1_in"), ("l0_ffn1_out",), ("ret_in",), ("ret_out",), ("l0_ffn2_in", "l0_ffn2_out"),
              ("l1_ffn1_in", "l1_ffn1_out"), ("gla_in", "gla_out"), ("l1_ffn2_in", "l1_ffn2_out")]

    def __init__(self, p):
        self.small_shapes = [p[name].shape for name in _SMALL]
        self.f32 = {"small": _pack_rows([p[name] for name in _SMALL], 128), "ret_in": p["ret_w_in"][0],
                    "ret_out": p["ret_w_out"][0], "gla_in": p["gla_w_in"][0], "gla_out": p["gla_w_out"][0]}
        for layer in range(2):
            for name in ("ffn1", "ffn2"):
                self.f32[f"l{layer}_{name}_in"] = p[f"{name}_w_in"][layer]
                self.f32[f"l{layer}_{name}_out"] = p[f"{name}_w_out"][layer]
        self.shards = {}
        self.started = {}
        self.pin = None

    def shard(self, name):
        if name not in self.shards:
            a = self.f32[name]
            if name != "small":
                a = (a if self.pin is None else a + self.pin[0, 0]).astype(BF16)
            self.shards[name] = a
        return self.shards[name]

    def later_shards(self, k):
        return [self.shard(name) for group in self.GROUPS[k:] for name in group]

    def start(self, k, after):
        shards = [self.shard(name) for name in self.GROUPS[k]]
        lands = []
        for name, s in zip(self.GROUPS[k], shards):
            if "ffn" in name and name.endswith("_in"):
                lands.append(lax.empty((s.shape[0], N_CHIPS * s.shape[1]), s.dtype))
            else:
                lands.append(lax.empty((N_CHIPS,) + s.shape, s.dtype))
        self.started[k], token = _exchange_start(shards, lands, _gather_route, _GATHER_FLIPS, after, f"gather{k}_start")
        return token

    def wait(self, k, after):
        _, got = _exchange_wait(self.started[k], _gather_route, _GATHER_FLIPS, after, f"gather{k}_wait")
        w = {}
        for name, g in zip(self.GROUPS[k], got):
            if name == "small":
                parts = zip(*[_unpack_rows(g[chip], self.small_shapes) for chip in range(N_CHIPS)])
                cat = lambda a: jnp.moveaxis(a, 0, -2).reshape(a.shape[1:-1] + (-1,))
                meta, ret_gain, wg, bg, gla_gain = [cat(jnp.stack(part)) for part in parts]
                w.update(meta=meta, ret_gain=ret_gain.reshape(1, -1), gla_bg=bg.reshape(1, -1),
                         gla_gain=gla_gain.reshape(1, -1),
                         gla_wg=jnp.pad(wg[0], ((0, 128 - GLA_RANK), (0, 0))).astype(BF16))
            elif name == "gla_in":
                full = jnp.moveaxis(g, 0, 1).reshape(D, -1)
                w[name] = jnp.pad(full, ((0, 0), (0, GLA_U - GLA_IN)))[None]
            elif name.endswith("_out"):
                w[name] = g.reshape(-1, g.shape[-1])
            else:
                w[name] = g
        return w


class _GradExchange:
    def __init__(self):
        self.started = []
        self.token = None
        self.small_shapes = None

    def push(self, k, arrays, small=None):
        srcs, lands = [], []
        for a in arrays:
            if isinstance(a, tuple):
                a = a[1]
                piece = (a.shape[0] // 2, a.shape[1] // N_CHIPS)
            else:
                a = a.reshape(N_CHIPS, 2, -1, a.shape[-1])
                piece = a.shape[2:]
            srcs.append(a)
            lands.append(lax.empty((N_DEV,) + piece, a.dtype))
        if small is not None:
            self.small_shapes = [a.shape for a in small]
            srcs.append(_pack_rows(small, D))
            lands.append(lax.empty((N_DEV,) + srcs[-1].shape, F32))
        started, self.token = _exchange_start(srcs, lands, _scatter_route(len(arrays)), _PEER_FLIPS, None,
                                              f"scatter{k}_start")
        self.started.append((started, len(arrays)))
        return self.token

    def collect(self, groups, after=None):
        x, y, c = _place()
        after, sums = self.token if after is None else after, []
        for k in groups:
            started, n_pieces = self.started[k]
            srcs, got = _exchange_wait(started, _scatter_route(n_pieces), _PEER_FLIPS, after, f"scatter{k}_wait")
            own = []
            for t, (a, g) in enumerate(zip(srcs, got)):
                if t >= n_pieces:
                    own.append(a)
                elif a.ndim == 4:
                    own.append(a[2 * x + y, c])
                else:
                    rows, cols = g.shape[1:]
                    own.append(lax.dynamic_slice(a, (c * rows, (2 * x + y) * cols), (rows, cols)))
            got = [lax.dynamic_update_index_in_dim(g, a, 4 * x + 2 * y + c, 0) for g, a in zip(got, own)]
            sums.append([_sum_slots(a, f"sum{k}_{i}") for i, a in enumerate(got)])
            after = sums[-1][0]
        return sums


def kernel(x, meta_tokens, norm_ffn1, ffn1_w_in, ffn1_w_out, norm_mix, norm_ffn2, ffn2_w_in, ffn2_w_out, ret_w_in, ret_head_norm, ret_w_out, gla_w_in, gla_w_gate, gla_b_gate, gla_head_norm, gla_w_out, final_norm, loss_target, m_meta_tokens, m_norm_ffn1, m_ffn1_w_in, m_ffn1_w_out, m_norm_mix, m_norm_ffn2, m_ffn2_w_in, m_ffn2_w_out, m_ret_w_in, m_ret_head_norm, m_ret_w_out, m_gla_w_in, m_gla_w_gate, m_gla_b_gate, m_gla_head_norm, m_gla_w_out, m_final_norm, v_meta_tokens, v_norm_ffn1, v_ffn1_w_in, v_ffn1_w_out, v_norm_mix, v_norm_ffn2, v_ffn2_w_in, v_ffn2_w_out, v_ret_w_in, v_ret_head_norm, v_ret_w_out, v_gla_w_in, v_gla_w_gate, v_gla_b_gate, v_gla_head_norm, v_gla_w_out, v_final_norm):
    p = dict(meta_tokens=meta_tokens, norm_ffn1=norm_ffn1, ffn1_w_in=ffn1_w_in, ffn1_w_out=ffn1_w_out, norm_mix=norm_mix,
             norm_ffn2=norm_ffn2, ffn2_w_in=ffn2_w_in, ffn2_w_out=ffn2_w_out, ret_w_in=ret_w_in,
             ret_head_norm=ret_head_norm, ret_w_out=ret_w_out, gla_w_in=gla_w_in, gla_w_gate=gla_w_gate,
             gla_b_gate=gla_b_gate, gla_head_norm=gla_head_norm, gla_w_out=gla_w_out, final_norm=final_norm)
    m = dict(zip(_WEIGHTS, (m_meta_tokens, m_norm_ffn1, m_ffn1_w_in, m_ffn1_w_out, m_norm_mix, m_norm_ffn2, m_ffn2_w_in,
                            m_ffn2_w_out, m_ret_w_in, m_ret_head_norm, m_ret_w_out, m_gla_w_in, m_gla_w_gate,
                            m_gla_b_gate, m_gla_head_norm, m_gla_w_out, m_final_norm)))
    v = dict(zip(_WEIGHTS, (v_meta_tokens, v_norm_ffn1, v_ffn1_w_in, v_ffn1_w_out, v_norm_mix, v_norm_ffn2, v_ffn2_w_in,
                            v_ffn2_w_out, v_ret_w_in, v_ret_head_norm, v_ret_w_out, v_gla_w_in, v_gla_w_gate,
                            v_gla_b_gate, v_gla_head_norm, v_gla_w_out, v_final_norm)))

    exchange = _GradExchange()
    d_x = _sequence_grads(x[0], loss_target[0], p, _WeightGather(p), exchange)
    names = [("ffn2_w_in", 1), ("ffn2_w_out", 1), ("gla_w_in", 0), ("gla_w_out", 0), ("ffn1_w_in", 1), ("ffn1_w_out", 1),
             ("ffn2_w_in", 0), ("ffn2_w_out", 0), ("ret_w_in", 0), ("ret_w_out", 0), ("ffn1_w_in", 0), ("ffn1_w_out", 0)]
    shard, grads, delta, new_m, new_v = {}, {}, {}, {}, {}

    def swap(sums, keys, name):
        for key, a in zip(keys, _swap_cores(sums, name)):
            shard[key] = a.reshape(-1, a.shape[-1])

    def update(name):
        layers = p[name].shape[0]
        grads[name] = jnp.stack([shard[name, layer] for layer in range(layers)])
        delta[name], new_m[name], new_v[name] = _adamw(p[name], grads[name], m[name], v[name], f"adamw_{name}")

    swap([a for group in exchange.collect(range(5)) for a in group], names[:10], "swap_first")
    for name in ("ffn2_w_in", "ffn2_w_out", "ret_w_in", "ret_w_out", "gla_w_in", "gla_w_out"):
        update(name)
    last, (small_sum,) = exchange.collect([5, 6], after=list(delta.values()))
    swap(last, names[10:], "swap_last")
    for name in ("ffn1_w_in", "ffn1_w_out"):
        update(name)

    chip = 2 * lax.axis_index("x") + lax.axis_index("y")
    cols = lambda a, n: lax.dynamic_slice_in_dim(a, chip * n, n, axis=a.ndim - 1)
    (s_meta, s_n1a, s_n1b, s_nma, s_nmb, s_n2a, s_n2b, s_final, s_ret_gain, s_wg, s_bg, s_gla_gain,
     s_loss) = _unpack_rows(small_sum, exchange.small_shapes)
    grads.update({
        "meta_tokens": cols(s_meta, 256), "norm_ffn1": jnp.concatenate([s_n1a, s_n1b]),
        "norm_mix": jnp.concatenate([s_nma, s_nmb]), "norm_ffn2": jnp.concatenate([s_n2a, s_n2b]),
        "final_norm": s_final.reshape(D),
        "ret_head_norm": cols(s_ret_gain.reshape(1, HEADS, RET_DV), RET_DV // N_CHIPS),
        "gla_w_gate": cols(s_wg, GLA_DK)[None], "gla_b_gate": cols(s_bg, GLA_DK),
        "gla_head_norm": cols(s_gla_gain.reshape(1, HEADS, GLA_DV), GLA_DV // N_CHIPS),
    })
    for name in _LOCAL_SMALL:
        shape = p[name].shape
        as3d = lambda a: a.reshape((1,) * (3 - len(shape)) + shape)
        out = _adamw(as3d(p[name]), as3d(grads[name]), as3d(m[name]), as3d(v[name]), f"adamw_{name}")
        delta[name], new_m[name], new_v[name] = [a.reshape(shape) for a in out]

    return (s_loss.reshape(()), d_x[None], *[grads[n] for n in _WEIGHTS], *[delta[n] for n in _WEIGHTS],
            *[new_m[n] for n in _WEIGHTS], *[new_v[n] for n in _WEIGHTS])
```

```python
import functools

import jax
import numpy as np
import jax.numpy as jnp
from jax import lax
from jax.experimental import pallas as pl
from jax.experimental.pallas import tpu as pltpu

F32, BF16 = jnp.float32, jnp.bfloat16
MESH = pl.DeviceIdType.MESH

D = 1024
N_META = 16
CHUNK = 64
RET_CHUNK = 256
FRONT = 256
D_FF = 2816
EPS = 1e-6
HEADS = 4
RET_DK, RET_DV = 256, 512
GLA_DK, GLA_DV = 128, 256
GLA_RANK = 16
GLA_TAU = 16.0
GLA_IN = 2 * HEADS * GLA_DK + 2 * HEADS * GLA_DV + GLA_RANK
GLA_U = 3328
ROPE_BASE = 10000.0
N_CHIPS = 4
N_DEV = 8

ADAM_LR, ADAM_B1, ADAM_B2, ADAM_EPS, ADAM_WD, ADAM_STEP = 0.001, 0.9, 0.999, 1e-08, 0.01, 10

VMEM_LIMIT_BYTES = 56 * 1024 * 1024
TM = 768
TM_SMALL = 256


TM_RESIDENT = 384
MXU_TILE = 256


def _cp(n_axes):
    return pltpu.CompilerParams(dimension_semantics=("arbitrary",) * n_axes, vmem_limit_bytes=VMEM_LIMIT_BYTES)


def _resident(shape, n_axes):
    zeros = (0,) * len(shape)
    index = (lambda i: zeros) if n_axes == 1 else (lambda i, j: zeros)
    return pl.BlockSpec(shape, index, pipeline_mode=pl.Buffered(1))


def _dg(a, b, ca, cb):
    nb = a.ndim - 2
    dims = (((ca + nb,), (cb + nb,)), (tuple(range(nb)), tuple(range(nb))))
    return lax.dot_general(a.astype(BF16), b.astype(BF16), dims, preferred_element_type=F32)


@jax.custom_vjp
def _nn(a, b):
    return _dg(a, b, 1, 0)


@jax.custom_vjp
def _nt(a, b):
    return _dg(a, b, 1, 1)


@jax.custom_vjp
def _tn(a, b):
    return _dg(a, b, 0, 0)


def _dot_vjp(fn, ca, cb, da, db):
    def fwd(a, b):
        a, b = a.astype(BF16), b.astype(BF16)
        return _dg(a, b, ca, cb), (a, b)

    def bwd(res, g):
        a, b = res
        g = g.astype(BF16)
        grad = lambda other, dims, g_first: _dg(g, other, *dims) if g_first else _dg(other, g, *dims)
        return grad(b, *da), grad(a, *db)

    fn.defvjp(fwd, bwd)


_dot_vjp(_nn, 1, 0, ((1, 1), True), ((0, 0), False))
_dot_vjp(_nt, 1, 1, ((1, 0), True), ((0, 0), True))
_dot_vjp(_tn, 0, 0, ((1, 1), False), ((1, 0), False))


def _split_dot(m, a, parts):
    mb = jnp.broadcast_to(m, a.shape[:-2] + m.shape)
    total, rest = None, a
    for _ in range(parts):
        term = rest.astype(BF16)
        rest = rest - term.astype(F32)
        product = _dg(mb, term, 1, 0)
        total = product if total is None else total + product
    return total


def _make_cum(parts):
    @jax.custom_vjp
    def cum(m, mt, a):
        return _split_dot(m, a, parts)

    cum.defvjp(lambda m, mt, a: (_split_dot(m, a, parts), (m, mt)),
               lambda res, g: (jnp.zeros_like(res[0]), jnp.zeros_like(res[1]), _split_dot(res[1], g, parts)))
    return cum


_cum = _make_cum(3)
_cum16 = _make_cum(2)


def _sigmoid(x):
    return 1.0 / (1.0 + jnp.exp(-x))


def _rms(x):
    return lax.rsqrt(jnp.mean(x * x, axis=-1, keepdims=True) + EPS)


def _rmsnorm_bwd(dy, x, gain):
    r = _rms(x)
    xhat = x * r
    dxh = dy * gain
    return r * (dxh - xhat * jnp.mean(dxh * xhat, axis=-1, keepdims=True)), xhat


def _norm_proj(h, gain, w, name):
    tp, d = h.shape
    s, _, ns = w.shape

    tm = TM_RESIDENT

    def body(h_ref, g_ref, w_ref, hn_ref, u_ref):
        x = h_ref[...]
        a = (x * _rms(x) * g_ref[...]).astype(BF16)
        hn_ref[...] = a
        for k in range(s):
            u_ref[:, ns * k:ns * (k + 1)] = jnp.dot(a, w_ref[k], preferred_element_type=F32).astype(BF16)

    return pl.pallas_call(
        body, name=name, grid=(tp // tm,),
        in_specs=[pl.BlockSpec((tm, d), lambda i: (i, 0)), pl.BlockSpec((1, d), lambda i: (0, 0)), _resident(w.shape, 1)],
        out_specs=[pl.BlockSpec((tm, d), lambda i: (i, 0)), pl.BlockSpec((tm, s * ns), lambda i: (i, 0))],
        out_shape=[jax.ShapeDtypeStruct((tp, d), BF16), jax.ShapeDtypeStruct((tp, s * ns), BF16)],
        compiler_params=_cp(1))(h, gain, w)


def _norm_ffn_in(h, gain, w, name):
    tp, d = h.shape
    ff = w.shape[1] // 2
    tm = TM_RESIDENT
    blocks = [(c, min(c + 6 * MXU_TILE, ff)) for c in range(0, ff, 6 * MXU_TILE)]

    def body(h_ref, g_ref, w_ref, hn_ref, dg_ref, du_ref, act_ref):
        x = h_ref[...]
        a = (x * _rms(x) * g_ref[...]).astype(BF16)
        hn_ref[...] = a
        for c0, c1 in blocks:
            g = jnp.dot(a, w_ref[:, c0:c1], preferred_element_type=F32)
            u = jnp.dot(a, w_ref[:, ff + c0:ff + c1], preferred_element_type=F32)
            sg = _sigmoid(g)
            silu = g * sg
            dg_ref[:, c0:c1] = (u * (sg + silu * (1.0 - sg))).astype(BF16)
            du_ref[:, c0:c1] = silu.astype(BF16)
            act_ref[:, c0:c1] = (silu * u).astype(BF16)

    wide = jax.ShapeDtypeStruct((tp, ff), BF16)
    return pl.pallas_call(
        body, name=name, grid=(tp // tm,),
        in_specs=[pl.BlockSpec((tm, d), lambda i: (i, 0)), pl.BlockSpec((1, d), lambda i: (0, 0)),
                  _resident(w.shape, 1)],
        out_specs=[pl.BlockSpec((tm, d), lambda i: (i, 0))] + [pl.BlockSpec((tm, ff), lambda i: (i, 0))] * 3,
        out_shape=[jax.ShapeDtypeStruct((tp, d), BF16), wide, wide, wide],
        compiler_params=_cp(1))(h, gain, w)


def _out_proj(a, w, h, scale, name):
    tp, k = a.shape
    d = w.shape[1]

    def body(a_ref, w_ref, h_ref, o_ref):
        o_ref[...] = h_ref[...] + scale * jnp.dot(a_ref[...], w_ref[...], preferred_element_type=F32)

    return pl.pallas_call(
        body, name=name, grid=(tp // TM,),
        in_specs=[pl.BlockSpec((TM, k), lambda i: (i, 0)), pl.BlockSpec((k, d), lambda i: (0, 0)),
                  pl.BlockSpec((TM, d), lambda i: (i, 0))],
        out_specs=pl.BlockSpec((TM, d), lambda i: (i, 0)),
        out_shape=jax.ShapeDtypeStruct((tp, d), F32),
        compiler_params=_cp(1))(a, w, h)


def _dgrad(dh, w, name):
    tp, d = dh.shape
    k = w.shape[0]

    def body(dh_ref, w_ref, o_ref):
        o_ref[...] = lax.dot_general(dh_ref[...].astype(BF16), w_ref[...], (((1,), (1,)), ((), ())),
                                     preferred_element_type=F32).astype(BF16)

    return pl.pallas_call(
        body, name=name, grid=(tp // TM,),
        in_specs=[pl.BlockSpec((TM, d), lambda i: (i, 0)), pl.BlockSpec((k, d), lambda i: (0, 0))],
        out_specs=pl.BlockSpec((TM, k), lambda i: (i, 0)),
        out_shape=jax.ShapeDtypeStruct((tp, k), BF16),
        compiler_params=_cp(1))(dh, w)


def _wgrad(a, b, *, bm, bn, scale, sharded, name):
    tp, m = a.shape
    n = b.shape[1]
    nk = tp // TM

    def body(a_ref, b_ref, o_ref, acc_ref):
        k = pl.program_id(2)

        @pl.when(k == 0)
        def _():
            acc_ref[...] = jnp.zeros_like(acc_ref)

        bb = b_ref[...]
        if scale != 1.0:
            bb = scale * bb
        acc_ref[...] += lax.dot_general(a_ref[...], bb.astype(BF16), (((0,), (0,)), ((), ())),
                                        preferred_element_type=F32)

        @pl.when(k == nk - 1)
        def _():
            o_ref[...] = acc_ref[...].astype(BF16)

    if sharded:
        assert m == bm
        out_spec = pl.BlockSpec((None, bm, bn), lambda i, j, k: (j, 0, 0))
        out_shape = jax.ShapeDtypeStruct((n // bn, m, bn), BF16)
    else:
        out_spec = pl.BlockSpec((bm, bn), lambda i, j, k: (i, j))
        out_shape = jax.ShapeDtypeStruct((m, n), BF16)
    return pl.pallas_call(
        body, name=name, grid=(m // bm, n // bn, nk),
        in_specs=[pl.BlockSpec((TM, bm), lambda i, j, k: (k, i)), pl.BlockSpec((TM, bn), lambda i, j, k: (k, j))],
        out_specs=out_spec, out_shape=out_shape,
        scratch_shapes=[pltpu.VMEM((bm, bn), F32)],
        compiler_params=_cp(3))(a, b)


def _dgrad_norm(du, w, h, gain, dh_out, name):
    tp, d = h.shape
    s, _, ns = w.shape
    tm = TM_RESIDENT

    def body(du_ref, w_ref, h_ref, g_ref, dho_ref, dhi_ref, dg_ref):
        @pl.when(pl.program_id(0) == 0)
        def _():
            dg_ref[...] = jnp.zeros_like(dg_ref)

        dhn = None
        for k in range(s):
            part = lax.dot_general(du_ref[:, ns * k:ns * (k + 1)], w_ref[k], (((1,), (1,)), ((), ())),
                                   preferred_element_type=F32)
            dhn = part if dhn is None else dhn + part
        dx, xhat = _rmsnorm_bwd(dhn, h_ref[...], g_ref[...])
        dg_ref[...] += jnp.sum(dhn * xhat, axis=0, keepdims=True)
        dhi_ref[...] = dho_ref[...] + dx

    return pl.pallas_call(
        body, name=name, grid=(tp // tm,),
        in_specs=[pl.BlockSpec((tm, s * ns), lambda i: (i, 0)), _resident(w.shape, 1),
                  pl.BlockSpec((tm, d), lambda i: (i, 0)), pl.BlockSpec((1, d), lambda i: (0, 0)),
                  pl.BlockSpec((tm, d), lambda i: (i, 0))],
        out_specs=[pl.BlockSpec((tm, d), lambda i: (i, 0)), pl.BlockSpec((1, d), lambda i: (0, 0))],
        out_shape=[jax.ShapeDtypeStruct((tp, d), F32), jax.ShapeDtypeStruct((1, d), F32)],
        compiler_params=_cp(1))(du, w, h, gain, dh_out)


def _loss_head(act, w_out, h, gain, target, name):
    tp, d = h.shape
    ff = act.shape[1]
    tm = TM_SMALL
    front_tiles = FRONT // tm

    def body(a_ref, w_ref, h_ref, g_ref, t_ref, dh_ref, dg_ref, loss_ref):
        i = pl.program_id(0)

        @pl.when(i == 0)
        def _():
            dg_ref[...] = jnp.zeros_like(dg_ref)
            loss_ref[...] = jnp.zeros_like(loss_ref)

        x = h_ref[...] + 0.5 * jnp.dot(a_ref[...], w_ref[...], preferred_element_type=F32)
        gain_v = g_ref[...]
        y = x * _rms(x) * gain_v
        err = jnp.where(i >= front_tiles, y - t_ref[...], 0.0)
        loss_ref[...] += 0.5 * jnp.sum(jnp.mean(err * err, axis=-1, keepdims=True), axis=0, keepdims=True)
        dy = err * (1.0 / d)
        dx, xhat = _rmsnorm_bwd(dy, x, gain_v)
        dg_ref[...] += jnp.sum(dy * xhat, axis=0, keepdims=True)
        dh_ref[...] = dx

    return pl.pallas_call(
        body, name=name, grid=(tp // tm,),
        in_specs=[pl.BlockSpec((tm, ff), lambda i: (i, 0)), _resident(w_out.shape, 1),
                  pl.BlockSpec((tm, d), lambda i: (i, 0)), pl.BlockSpec((1, d), lambda i: (0, 0)),
                  pl.BlockSpec((tm, d), lambda i: (jnp.maximum(i - front_tiles, 0), 0))],
        out_specs=[pl.BlockSpec((tm, d), lambda i: (i, 0)), pl.BlockSpec((1, d), lambda i: (0, 0)),
                   pl.BlockSpec((1, 128), lambda i: (0, 0))],
        out_shape=[jax.ShapeDtypeStruct((tp, d), F32), jax.ShapeDtypeStruct((1, d), F32),
                   jax.ShapeDtypeStruct((1, 128), F32)],
        compiler_params=_cp(1))(act, w_out, h, gain, target)


def _gated_headnorm(o, g, gain):
    return o * _rms(o) * gain * (g * _sigmoid(g))


def _row_mask(chunk, size=CHUNK):
    rows = chunk * size + lax.broadcasted_iota(jnp.int32, (size, 1), 0)
    return (rows >= FRONT - N_META).astype(F32)


def _ret_head(q1, q2, k1, k2, v, g, state, gain, cos, sin, dmat, dq, dk, dc):
    q = jnp.concatenate([q1 * cos - q2 * sin, q1 * sin + q2 * cos], axis=-1)
    k = jnp.concatenate([k1 * cos - k2 * sin, k1 * sin + k2 * cos], axis=-1) * (RET_DK ** -0.5)
    scores = _nt(q, k) * dmat
    o = _nn(scores, v) + _nn(q * dq, state)
    new_state = state * dc + _tn(k * dk, v)
    return _gated_headnorm(o, g, gain), new_state


def _ret_consts():
    log_gamma = jnp.log1p(-2.0 ** (-5.0 - jnp.arange(HEADS, dtype=F32)))
    idx = jnp.arange(RET_CHUNK, dtype=F32)
    rel = idx[:, None] - idx[None, :]
    dmat = jnp.where(rel >= 0, jnp.exp(log_gamma[:, None, None] * jnp.maximum(rel, 0.0)), 0.0)
    dq = jnp.exp(log_gamma[:, None] * (idx + 1.0))[..., None]
    dk = jnp.exp(log_gamma[:, None] * (RET_CHUNK - 1.0 - idx))[..., None]
    dc = jnp.broadcast_to(jnp.exp(log_gamma * RET_CHUNK)[:, None, None], (HEADS, 1, 128))
    return dmat, dq, dk, dc


def _rope_tables(tp):
    half = RET_DK // 2
    inv = 1.0 / (ROPE_BASE ** jnp.linspace(0.0, 1.0, half, dtype=F32))
    pos = (jnp.arange(tp) - (FRONT - N_META)).astype(F32)
    ang = pos[:, None] * inv[None, :]
    return jnp.cos(ang), jnp.sin(ang)


_RET_V0, _RET_G0 = 2 * D, 4 * D


def _heads(ref, start, width, stride=None, rows=slice(None)):
    stride = width if stride is None else stride
    return jnp.stack([ref[rows, start + stride * h:start + stride * h + width].astype(F32) for h in range(HEADS)])


def _put_heads(ref, start, value, mask, stride=None, rows=slice(None)):
    width = value.shape[-1]
    stride = width if stride is None else stride
    for h in range(HEADS):
        ref[rows, start + stride * h:start + stride * h + width] = (value[h] * mask).astype(ref.dtype)


def _ret_pieces(u_ref):
    hk = RET_DK // 2
    return (_heads(u_ref, 0, hk, RET_DK), _heads(u_ref, hk, hk, RET_DK), _heads(u_ref, D, hk, RET_DK),
            _heads(u_ref, D + hk, hk, RET_DK), _heads(u_ref, _RET_V0, RET_DV), _heads(u_ref, _RET_G0, RET_DV))


def _ret_const_specs(rev=None):
    c = (lambda n: (rev(n), 0)) if rev else (lambda n: (n, 0))
    z3 = lambda n: (0, 0, 0)
    return [pl.BlockSpec((RET_CHUNK, RET_DK // 2), c), pl.BlockSpec((RET_CHUNK, RET_DK // 2), c),
            pl.BlockSpec((HEADS, RET_CHUNK, RET_CHUNK), z3), pl.BlockSpec((HEADS, RET_CHUNK, 1), z3),
            pl.BlockSpec((HEADS, RET_CHUNK, 1), z3), pl.BlockSpec((HEADS, 1, 128), z3)]


def _ret_fwd(u, gain, rope, h, w_out, name):
    tp = u.shape[0]
    nch = tp // RET_CHUNK
    cos, sin = rope
    dmat, dq, dk, dc = _ret_consts()

    def body(u_ref, gain_ref, h_ref, w_ref, cos_ref, sin_ref, dmat_ref, dq_ref, dk_ref, dc_ref,
             on_ref, st_ref, hmix_ref, state_ref):
        @pl.when(pl.program_id(0) == 0)
        def _():
            state_ref[...] = jnp.zeros_like(state_ref)

        state = state_ref[...]
        st_ref[...] = state.astype(BF16)
        on, new_state = _ret_head(*_ret_pieces(u_ref), state, _heads(gain_ref, 0, RET_DV), cos_ref[...], sin_ref[...],
                                  dmat_ref[...], dq_ref[...], dk_ref[...], dc_ref[...][:, :, :1])
        state_ref[...] = new_state
        _put_heads(on_ref, 0, on, 1.0)
        hmix_ref[...] = h_ref[...] + jnp.dot(on_ref[...], w_ref[...], preferred_element_type=F32)

    rows = lambda width: pl.BlockSpec((RET_CHUNK, width), lambda n: (n, 0))
    return pl.pallas_call(
        body, name=name, grid=(nch,),
        in_specs=[rows(6 * D), pl.BlockSpec((1, HEADS * RET_DV), lambda n: (0, 0)), rows(D),
                  _resident(w_out.shape, 1)] + _ret_const_specs(),
        out_specs=[rows(HEADS * RET_DV), pl.BlockSpec((None, HEADS, RET_DK, RET_DV), lambda n: (n, 0, 0, 0)), rows(D)],
        out_shape=[jax.ShapeDtypeStruct((tp, HEADS * RET_DV), BF16),
                   jax.ShapeDtypeStruct((nch, HEADS, RET_DK, RET_DV), BF16), jax.ShapeDtypeStruct((tp, D), F32)],
        scratch_shapes=[pltpu.VMEM((HEADS, RET_DK, RET_DV), F32)],
        compiler_params=_cp(1))(u, gain, h, w_out, cos, sin, dmat, dq, dk, dc)


def _ret_bwd(u, gain, rope, states, d_on, name):
    tp = u.shape[0]
    nch = tp // RET_CHUNK
    cos, sin = rope
    dmat, dq, dk, dc = _ret_consts()
    rev = lambda n: nch - 1 - n
    hk = RET_DK // 2

    def body(u_ref, gain_ref, st_ref, don_ref, cos_ref, sin_ref, dmat_ref, dq_ref, dk_ref, dc_ref,
             du_ref, dgain_ref, dstate_ref):
        @pl.when(pl.program_id(0) == 0)
        def _():
            dstate_ref[...] = jnp.zeros_like(dstate_ref)
            dgain_ref[...] = jnp.zeros_like(dgain_ref)

        mask = _row_mask(rev(pl.program_id(0)), RET_CHUNK)
        consts = (cos_ref[...], sin_ref[...], dmat_ref[...], dq_ref[...], dk_ref[...], dc_ref[...][:, :, :1])
        _, vjp = jax.vjp(lambda *a: _ret_head(*a, *consts), *_ret_pieces(u_ref), st_ref[...].astype(F32),
                         _heads(gain_ref, 0, RET_DV))
        dq1, dq2, dk1, dk2, dv, dg, dstate, dgain = vjp((_heads(don_ref, 0, RET_DV), dstate_ref[...]))
        dstate_ref[...] = dstate
        for hd in range(HEADS):
            dgain_ref[:, RET_DV * hd:RET_DV * (hd + 1)] += dgain[hd]
        _put_heads(du_ref, 0, dq1, mask, RET_DK)
        _put_heads(du_ref, hk, dq2, mask, RET_DK)
        _put_heads(du_ref, D, dk1, mask, RET_DK)
        _put_heads(du_ref, D + hk, dk2, mask, RET_DK)
        _put_heads(du_ref, _RET_V0, dv, mask)
        _put_heads(du_ref, _RET_G0, dg, mask)

    return pl.pallas_call(
        body, name=name, grid=(nch,),
        in_specs=[pl.BlockSpec((RET_CHUNK, 6 * D), lambda n: (rev(n), 0)),
                  pl.BlockSpec((1, HEADS * RET_DV), lambda n: (0, 0)),
                  pl.BlockSpec((None, HEADS, RET_DK, RET_DV), lambda n: (rev(n), 0, 0, 0)),
                  pl.BlockSpec((RET_CHUNK, HEADS * RET_DV), lambda n: (rev(n), 0))] + _ret_const_specs(rev),
        out_specs=[pl.BlockSpec((RET_CHUNK, 6 * D), lambda n: (rev(n), 0)),
                   pl.BlockSpec((1, HEADS * RET_DV), lambda n: (0, 0))],
        out_shape=[jax.ShapeDtypeStruct((tp, 6 * D), BF16), jax.ShapeDtypeStruct((1, HEADS * RET_DV), F32)],
        scratch_shapes=[pltpu.VMEM((HEADS, RET_DK, RET_DV), F32)],
        compiler_params=_cp(1))(u, gain, states, d_on, cos, sin, dmat, dq, dk, dc)


_GLA_K0, _GLA_V0, _GLA_G0, _GLA_Z0 = 512, 1024, 2048, 3072


def _gla_head(q, k, v, g, z, state_t, wg, bg, gain, mask, lo, lo_t, to_mid, to_mid_t, in_second, pair):
    ga = _nn(jnp.broadcast_to(z, wg.shape[:-2] + z.shape), wg) + bg
    log_a = (jnp.minimum(ga, 0.0) - jnp.log(1.0 + jnp.exp(-jnp.abs(ga)))) * (mask * (1.0 / GLA_TAU))
    bcum = _cum(lo, lo_t, log_a)
    btot = jnp.sum(log_a, axis=-2, keepdims=True)
    qs = q * (GLA_DK ** -0.5)
    heads, levels = q.shape[0], pair.shape[0]
    decay = jnp.exp(_cum16(to_mid, to_mid_t, log_a).reshape(heads, levels, CHUNK, GLA_DK))
    qk = (jnp.where(in_second > 0.0, qs[:, None], k[:, None]) * decay).reshape(heads * levels, CHUNK, GLA_DK)
    rows = lax.broadcasted_iota(jnp.int32, (CHUNK, CHUNK), 0)
    cols = lax.broadcasted_iota(jnp.int32, (CHUNK, CHUNK), 1)
    scores = (jnp.where(rows == cols, _nt(qs, k), 0.0)
              + jnp.sum(_nt(qk, qk).reshape(heads, levels, CHUNK, CHUNK) * pair, axis=1))
    o = _nn(scores, v) + _nt(qs * jnp.exp(bcum), state_t)
    new_state_t = state_t * jnp.exp(btot) + _tn(v, k * jnp.exp(btot - bcum))
    return _gated_headnorm(o, g, gain), new_state_t


def _gla_consts():
    r, c = np.meshgrid(np.arange(CHUNK), np.arange(CHUNK), indexing="ij")
    to_mid, second, pair = [], [], []
    block = 2
    while block <= CHUNK:
        mid = (r // block) * block + block // 2
        to_mid.append(((r >= mid) & (c > mid) & (c <= r)) | ((r < mid) & (c > r) & (c <= mid)))
        second.append((r >= mid)[:, :1])
        pair.append((r // block == c // block) & (r >= mid) & (c < mid))
        block *= 2
    to_mid = np.concatenate(to_mid)
    bf = lambda m: jnp.asarray(m, F32).astype(BF16)
    f32 = lambda ms: jnp.asarray(np.stack(ms), F32)
    return bf(r >= c), bf(c >= r), bf(to_mid), bf(to_mid.T), f32(second), f32(pair)


def _gla_const_specs(consts):
    return [pl.BlockSpec(a.shape, functools.partial(lambda nd, n: (0,) * nd, a.ndim)) for a in consts]


GLA_STEP_CHUNKS = 4


def _gla_pieces(u_ref, rows):
    return (_heads(u_ref, 0, GLA_DK, rows=rows), _heads(u_ref, _GLA_K0, GLA_DK, rows=rows),
            _heads(u_ref, _GLA_V0, GLA_DV, rows=rows), _heads(u_ref, _GLA_G0, GLA_DV, rows=rows),
            u_ref[rows, _GLA_Z0:_GLA_Z0 + 128].astype(F32))


def _gla_fwd(u, wg, bg, gain, name):
    tp = u.shape[0]
    nch = tp // CHUNK
    per = GLA_STEP_CHUNKS
    consts = _gla_consts()

    def body(u_ref, wg_ref, bg_ref, gain_ref, *refs):
        const_refs, (on_ref, st_ref, state_ref) = refs[:len(consts)], refs[len(consts):]

        @pl.when(pl.program_id(0) == 0)
        def _():
            state_ref[...] = jnp.zeros_like(state_ref)

        params = (_heads(wg_ref, 0, GLA_DK), _heads(bg_ref, 0, GLA_DK), _heads(gain_ref, 0, GLA_DV))
        mats = [ref[...] for ref in const_refs]
        state = state_ref[...]
        for c in range(per):
            rows = slice(CHUNK * c, CHUNK * (c + 1))
            st_ref[c] = state.astype(BF16)
            on, state = _gla_head(*_gla_pieces(u_ref, rows), state, *params, _row_mask(pl.program_id(0) * per + c), *mats)
            _put_heads(on_ref, 0, on, 1.0, rows=rows)
        state_ref[...] = state

    rows_spec = lambda width: pl.BlockSpec((per * CHUNK, width), lambda n: (n, 0))
    full = lambda r, c: pl.BlockSpec((r, c), lambda n: (0, 0))
    return pl.pallas_call(
        body, name=name, grid=(nch // per,),
        in_specs=[rows_spec(GLA_U), full(128, HEADS * GLA_DK), full(1, HEADS * GLA_DK), full(1, HEADS * GLA_DV)]
                 + _gla_const_specs(consts),
        out_specs=[rows_spec(HEADS * GLA_DV), pl.BlockSpec((per, HEADS, GLA_DV, GLA_DK), lambda n: (n, 0, 0, 0))],
        out_shape=[jax.ShapeDtypeStruct((tp, HEADS * GLA_DV), BF16),
                   jax.ShapeDtypeStruct((nch, HEADS, GLA_DV, GLA_DK), BF16)],
        scratch_shapes=[pltpu.VMEM((HEADS, GLA_DV, GLA_DK), F32)],
        compiler_params=_cp(1))(u, wg, bg, gain, *consts)


def _gla_bwd(u, wg, bg, gain, states, d_on, name):
    tp = u.shape[0]
    per = GLA_STEP_CHUNKS
    steps = tp // (per * CHUNK)
    rev = lambda n: steps - 1 - n
    consts = _gla_consts()

    def body(u_ref, wg_ref, bg_ref, gain_ref, st_ref, don_ref, *refs):
        const_refs, (du_ref, dwg_ref, dbg_ref, dgain_ref, dstate_ref) = refs[:len(consts)], refs[len(consts):]

        @pl.when(pl.program_id(0) == 0)
        def _():
            dstate_ref[...] = jnp.zeros_like(dstate_ref)
            dwg_ref[...] = jnp.zeros_like(dwg_ref)
            dbg_ref[...] = jnp.zeros_like(dbg_ref)
            dgain_ref[...] = jnp.zeros_like(dgain_ref)

        params = (_heads(wg_ref, 0, GLA_DK), _heads(bg_ref, 0, GLA_DK), _heads(gain_ref, 0, GLA_DV))
        mats = [ref[...] for ref in const_refs]
        dstate = dstate_ref[...]
        for c in reversed(range(per)):
            rows = slice(CHUNK * c, CHUNK * (c + 1))
            mask = _row_mask(rev(pl.program_id(0)) * per + c)
            _, vjp = jax.vjp(lambda *a: _gla_head(*a, mask, *mats), *_gla_pieces(u_ref, rows),
                             st_ref[c].astype(F32), *params)
            dq, dk, dv, dg, dz, dstate, dwg, dbg, dgain = vjp((_heads(don_ref, 0, GLA_DV, rows=rows), dstate))
            for hd in range(HEADS):
                dwg_ref[:, GLA_DK * hd:GLA_DK * (hd + 1)] += dwg[hd]
                dbg_ref[:, GLA_DK * hd:GLA_DK * (hd + 1)] += dbg[hd]
                dgain_ref[:, GLA_DV * hd:GLA_DV * (hd + 1)] += dgain[hd]
            _put_heads(du_ref, 0, dq, mask, rows=rows)
            _put_heads(du_ref, _GLA_K0, dk, mask, rows=rows)
            _put_heads(du_ref, _GLA_V0, dv, mask, rows=rows)
            _put_heads(du_ref, _GLA_G0, dg, mask, rows=rows)
            du_ref[rows, _GLA_Z0:_GLA_Z0 + 128] = dz.astype(BF16)
            du_ref[rows, _GLA_Z0 + 128:] = jnp.zeros((CHUNK, GLA_U - _GLA_Z0 - 128), BF16)
        dstate_ref[...] = dstate

    full = lambda r, c: pl.BlockSpec((r, c), lambda n: (0, 0))
    return pl.pallas_call(
        body, name=name, grid=(steps,),
        in_specs=[pl.BlockSpec((per * CHUNK, GLA_U), lambda n: (rev(n), 0)), full(128, HEADS * GLA_DK),
                  full(1, HEADS * GLA_DK), full(1, HEADS * GLA_DV),
                  pl.BlockSpec((per, HEADS, GLA_DV, GLA_DK), lambda n: (rev(n), 0, 0, 0)),
                  pl.BlockSpec((per * CHUNK, HEADS * GLA_DV), lambda n: (rev(n), 0))] + _gla_const_specs(consts),
        out_specs=[pl.BlockSpec((per * CHUNK, GLA_U), lambda n: (rev(n), 0)), full(128, HEADS * GLA_DK),
                   full(1, HEADS * GLA_DK), full(1, HEADS * GLA_DV)],
        out_shape=[jax.ShapeDtypeStruct((tp, GLA_U), BF16), jax.ShapeDtypeStruct((128, HEADS * GLA_DK), F32),
                   jax.ShapeDtypeStruct((1, HEADS * GLA_DK), F32), jax.ShapeDtypeStruct((1, HEADS * GLA_DV), F32)],
        scratch_shapes=[pltpu.VMEM((HEADS, GLA_DV, GLA_DK), F32)],
        compiler_params=_cp(1))(u, wg, bg, gain, states, d_on, *consts)


def _ffn_fwd(h, gain, w_in, w_out, tag):
    hn, ug, uu, act = _norm_ffn_in(h, gain, w_in, f"{tag}_in")
    if callable(w_out):
        w_out = w_out(act)
    return _out_proj(act, w_out, h, 0.5, f"{tag}_out"), (h, hn, ug, uu, act), w_out


def _ffn_dgrad(dh, w_out, w_in, act_dg, act_du, h, gain, name, split_front=False):
    tp, d = dh.shape
    ff = w_out.shape[0]
    tm = TM_SMALL
    nt = (((1,), (1,)), ((), ()))

    def body(dh_ref, wo_ref, wi_ref, dg_ref, du_ref, h_ref, g_ref, o_ref, *out_refs):
        dhi_ref, dgain_ref = out_refs[-2:]

        @pl.when(pl.program_id(0) == 0)
        def _():
            dgain_ref[...] = jnp.zeros_like(dgain_ref)

        dho = dh_ref[...]
        dact = lax.dot_general((0.5 * dho).astype(BF16), wo_ref[...], nt, preferred_element_type=F32)
        d_gate = (dact * dg_ref[...].astype(F32)).astype(BF16)
        d_up = (dact * du_ref[...].astype(F32)).astype(BF16)
        o_ref[:, :ff] = d_gate
        o_ref[:, ff:] = d_up
        dhn = (lax.dot_general(d_gate, wi_ref[:, :ff], nt, preferred_element_type=F32)
               + lax.dot_general(d_up, wi_ref[:, ff:], nt, preferred_element_type=F32))
        dx, xhat = _rmsnorm_bwd(dhn, h_ref[...], g_ref[...])
        dgain_ref[...] += jnp.sum(dhn * xhat, axis=0, keepdims=True)
        dhi_ref[...] = dho + dx
        if split_front:
            @pl.when(pl.program_id(0) == 0)
            def _():
                out_refs[0][...] = dho + dx

    rows = lambda width: pl.BlockSpec((tm, width), lambda i: (i, 0))
    if split_front:
        assert tm == FRONT
        dhi_specs = [pl.BlockSpec((tm, d), lambda i: (0, 0)), pl.BlockSpec((tm, d), lambda i: (jnp.maximum(i - 1, 0), 0))]
        dhi_shapes = [jax.ShapeDtypeStruct((FRONT, d), F32), jax.ShapeDtypeStruct((tp - FRONT, d), F32)]
    else:
        dhi_specs, dhi_shapes = [rows(d)], [jax.ShapeDtypeStruct((tp, d), F32)]
    out = pl.pallas_call(
        body, name=name, grid=(tp // tm,),
        in_specs=[rows(d), _resident(w_out.shape, 1), _resident(w_in.shape, 1), rows(ff), rows(ff), rows(d),
                  pl.BlockSpec((1, d), lambda i: (0, 0))],
        out_specs=[rows(2 * ff), *dhi_specs, pl.BlockSpec((1, d), lambda i: (0, 0))],
        out_shape=[jax.ShapeDtypeStruct((tp, 2 * ff), BF16), *dhi_shapes, jax.ShapeDtypeStruct((1, d), F32)],
        compiler_params=_cp(1))(dh, w_out, w_in, act_dg, act_du, h, gain)
    return (out[0], tuple(out[1:3]), out[3]) if split_front else tuple(out)


def _ffn_bwd(dh, saved, gain, w_in, w_out, tag, push, split_front=False):
    h, hn, act_dg, act_du, act = saved
    du, dh_in, d_gain = _ffn_dgrad(dh, w_out, w_in, act_dg, act_du, h, gain, f"{tag}_dgrad", split_front)
    d_w_out = _wgrad(act, dh, bm=D_FF // 2, bn=D, scale=0.5, sharded=False, name=f"{tag}_dwout")
    d_w_in = _wgrad(hn, du, bm=D, bn=D_FF, scale=1.0, sharded=False, name=f"{tag}_dwin")
    return dh_in, d_gain, push([("cols", d_w_in), d_w_out])


def _sequence_grads(x, target, p, weights, grads):
    row = lambda v, token: v.reshape(1, -1) + token[0, 0]
    gains = {}

    tok = weights.start(1, weights.start(0, None))
    weights.pin = tok
    h = jnp.concatenate([jnp.zeros((FRONT, D), F32), x], axis=0) + tok[0, 0]
    rope = _rope_tables(h.shape[0])
    w = weights.wait(0, [tok, h, *rope, *weights.later_shards(2)])
    tok = weights.start(2, w["l0_ffn1_in"])
    h = lax.dynamic_update_slice(h, w["meta"], (FRONT - N_META, 0))
    gains["l0_ffn1"] = row(p["norm_ffn1"][0], tok)
    h, s1, w["l0_ffn1_out"] = _ffn_fwd(h, gains["l0_ffn1"], w["l0_ffn1_in"],
                                       lambda act: weights.wait(1, act)["l0_ffn1_out"], "l0_ffn1")
    w.update(weights.wait(2, h))
    tok = weights.start(4, weights.start(3, w["ret_in"]))
    gains["ret"] = row(p["norm_mix"][0], tok)
    hn, u = _norm_proj(h, gains["ret"], w["ret_in"], "ret_in")
    w.update(weights.wait(3, u))
    on, states, h_mix = _ret_fwd(u, w["ret_gain"], rope, h, w["ret_out"], "ret_fwd")
    s2 = (h, hn, u, on, states)
    w.update(weights.wait(4, h_mix))
    tok = weights.start(5, w["l0_ffn2_in"])
    gains["l0_ffn2"] = row(p["norm_ffn2"][0], tok)
    h, s3, _ = _ffn_fwd(h_mix, gains["l0_ffn2"], w["l0_ffn2_in"], w["l0_ffn2_out"], "l0_ffn2")
    saved = [(s1, s2, s3)]

    w.update(weights.wait(5, h))
    tok = weights.start(6, w["l1_ffn1_in"])
    gains["l1_ffn1"] = row(p["norm_ffn1"][1], tok)
    h, s1, _ = _ffn_fwd(h, gains["l1_ffn1"], w["l1_ffn1_in"], w["l1_ffn1_out"], "l1_ffn1")
    w.update(weights.wait(6, h))
    tok = weights.start(7, w["gla_out"])
    gains["gla"] = row(p["norm_mix"][1], tok)
    hn, u = _norm_proj(h, gains["gla"], w["gla_in"], "gla_in")
    on, states = _gla_fwd(u, w["gla_wg"], w["gla_bg"], w["gla_gain"], "gla_fwd")
    h_mix = _out_proj(on, w["gla_out"], h, 1.0, "gla_out")
    s2 = (h, hn, u, on, states)
    w.update(weights.wait(7, h_mix))
    gains["l1_ffn2"] = p["norm_ffn2"][1].reshape(1, -1)
    s3 = (h_mix, *_norm_ffn_in(h_mix, gains["l1_ffn2"], w["l1_ffn2_in"], "l1_ffn2_in"))
    saved.append((s1, s2, s3))

    dh, d_final, loss = _loss_head(s3[-1], w["l1_ffn2_out"], h_mix, p["final_norm"].reshape(1, -1), target,
                                   "l1_ffn2_out_loss")
    small = {"final_norm": d_final, "norm_ffn1": [None, None], "norm_mix": [None, None], "norm_ffn2": [None, None]}
    pusher = lambda k: functools.partial(grads.push, k)

    s1, s2, s3 = saved[1]
    dh, small["norm_ffn2"][1], tok = _ffn_bwd(dh, s3, gains["l1_ffn2"], w["l1_ffn2_in"], w["l1_ffn2_out"], "l1_ffn2",
                                              pusher(0))
    h_in, hn, u, on, states = s2
    d_on = _dgrad(dh, w["gla_out"], "gla_don")
    d_out = _wgrad(on, dh, bm=D, bn=D, scale=1.0, sharded=False, name="gla_dwout")
    du, small["gla_wg"], small["gla_bg"], small["gla_gain"] = _gla_bwd(
        u, w["gla_wg"], w["gla_bg"], w["gla_gain"] + tok[0, 0], states, d_on, "gla_bwd")
    d_in = _wgrad(hn, du, bm=D, bn=GLA_U, scale=1.0, sharded=False, name="gla_dwin")
    d_in = jnp.moveaxis(d_in[:, :GLA_IN].reshape(D, N_CHIPS, -1), 1, 0)
    tok = grads.push(1, [d_in, d_out])
    dh, small["norm_mix"][1] = _dgrad_norm(du, w["gla_in"], h_in, gains["gla"] + tok[0, 0], dh, "gla_dnorm")
    dh, small["norm_ffn1"][1], tok = _ffn_bwd(dh, s1, gains["l1_ffn1"], w["l1_ffn1_in"], w["l1_ffn1_out"], "l1_ffn1",
                                              pusher(2))

    s1, s2, s3 = saved[0]
    dh, small["norm_ffn2"][0], tok = _ffn_bwd(dh, s3, gains["l0_ffn2"] + tok[0, 0], w["l0_ffn2_in"],
                                              w["l0_ffn2_out"], "l0_ffn2", pusher(3))
    h_in, hn, u, on, states = s2
    d_on = _dgrad(dh, w["ret_out"], "ret_don")
    d_out = _wgrad(on, dh, bm=D, bn=D, scale=1.0, sharded=False, name="ret_dwout")
    du, small["ret_gain"] = _ret_bwd(u, w["ret_gain"] + tok[0, 0], rope, states, d_on, "ret_bwd")
    d_in = _wgrad(hn, du, bm=D, bn=w["ret_in"].shape[2], scale=1.0, sharded=True, name="ret_dwin")
    tok = grads.push(4, [d_in, d_out])
    dh, small["norm_mix"][0] = _dgrad_norm(du, w["ret_in"], h_in, gains["ret"] + tok[0, 0], dh, "ret_dnorm")
    (d_front, d_x), small["norm_ffn1"][0], tok = _ffn_bwd(dh, s1, gains["l0_ffn1"], w["l0_ffn1_in"], w["l0_ffn1_out"],
                                                          "l0_ffn1", pusher(5), split_front=True)
    grads.push(6, [], [d_front[FRONT - N_META:], *small["norm_ffn1"], *small["norm_mix"], *small["norm_ffn2"],
                       small["final_norm"], small["ret_gain"], small["gla_wg"][:GLA_RANK], small["gla_bg"],
                       small["gla_gain"], loss[:, :1] + tok[0, 0]])
    return d_x


_HBM = pl.BlockSpec(memory_space=pl.ANY)


def _place():
    return lax.axis_index("x"), lax.axis_index("y"), lax.axis_index("c")


def _flip(v, bit):
    return 1 - v if bit else v


DMA_CHUNK_BYTES = 128 * 1024


def _row_chunks(ref):
    rows, cols = ref.shape
    step = _row_tile(rows, max(16, DMA_CHUNK_BYTES // (cols * ref.dtype.itemsize)))
    return [pl.ds(a, step) for a in range(0, rows, step)]


def _whole(src, dst, send_sem, recv_sem, peer):
    return pltpu.make_async_remote_copy(src_ref=src, dst_ref=dst, send_sem=send_sem, recv_sem=recv_sem,
                                        device_id=peer, device_id_type=MESH)


def _send(src, dst, send_sem, recv_sem, peer):
    for rows in _row_chunks(src):
        _whole(src.at[rows], dst.at[rows], send_sem, recv_sem, peer).start()
    return _whole(src, dst, send_sem, recv_sem, peer)


_HBM_ONLY = pl.BlockSpec(memory_space=pltpu.HBM)
_SEMS = pl.BlockSpec(memory_space=pltpu.SEMAPHORE)
_SIDE_EFFECT = pltpu.CompilerParams(has_side_effects=pltpu.SideEffectType.DATAFLOW_SIDE_EFFECTING)
_GATHER_FLIPS = [(1, 0, 0), (0, 1, 0), (1, 1, 0), (0, 0, 1)]
_PEER_FLIPS = [(fx, fy, fc) for fx in (0, 1) for fy in (0, 1) for fc in (0, 1)][1:]


def _zero_token():
    return jnp.zeros((8, 128), F32)


def _exchange_start(srcs, lands, route, flips, after, name):
    n = len(srcs)

    def body(*refs):
        src, land = refs[:n], refs[n:2 * n]
        send_sems, recv_sems, token = refs[2 * n + 1], refs[2 * n + 2], refs[-1]
        me = _place()
        for t in range(n):
            for j, flip in enumerate(flips):
                peer = tuple(_flip(v, f) for v, f in zip(me, flip))
                s, d = route(t, src[t], land[t], me, peer)
                _send(s, d, send_sems.at[t * len(flips) + j], recv_sems.at[t * len(flips) + j], peer)
        token[...] = jnp.zeros_like(token)

    hbm = lambda a: pltpu.HBM(a.shape, a.dtype)
    sems = pltpu.SemaphoreType.DMA((n * len(flips),))
    operands = [pltpu.with_memory_space_constraint(a, pltpu.HBM) for a in list(srcs) + list(lands)]
    out = pl.pallas_call(
        body, name=name, in_specs=[_HBM_ONLY] * (2 * n) + [_HBM],
        out_shape=(sems, sems, *[hbm(a) for a in operands], jax.ShapeDtypeStruct((8, 128), F32)),
        out_specs=(_SEMS, _SEMS, *[_HBM_ONLY] * (2 * n), pl.BlockSpec(memory_space=pltpu.VMEM)),
        input_output_aliases={i: 2 + i for i in range(2 * n)}, compiler_params=_SIDE_EFFECT,
    )(*operands, _zero_token() if after is None else after)
    return (out[0], out[1], out[2:2 + n], out[2 + n:2 + 2 * n]), out[-1]


def _exchange_wait(started, route, flips, after, name):
    send_sems, recv_sems, srcs, lands = started
    n = len(srcs)

    def body(*refs):
        src, land = refs[:n], refs[n:2 * n]
        send_sems, recv_sems = refs[2 * n], refs[2 * n + 1]
        me = _place()
        for t in range(n):
            for j, flip in enumerate(flips):
                peer = tuple(_flip(v, f) for v, f in zip(me, flip))
                s, d = route(t, src[t], land[t], me, peer)
                cp = _whole(s, d, send_sems.at[t * len(flips) + j], recv_sems.at[t * len(flips) + j], peer)
                cp.wait_send()
                cp.wait_recv()

    hbm = lambda a: pltpu.HBM(a.shape, a.dtype)
    after = list(after) if isinstance(after, (list, tuple)) else [after]
    out = pl.pallas_call(
        body, name=name, in_specs=[_HBM_ONLY] * (2 * n) + [_SEMS, _SEMS] + [_HBM] * len(after),
        out_shape=tuple(hbm(a) for a in list(srcs) + list(lands)), out_specs=tuple([_HBM_ONLY] * (2 * n)),
        input_output_aliases={i: i for i in range(2 * n)}, compiler_params=_SIDE_EFFECT,
    )(*srcs, *lands, send_sems, recv_sems, *after)
    return out[:n], out[n:]


def _gather_route(t, src, land, me, peer):
    mine = 2 * me[0] + me[1]
    if land.ndim == 3:
        return src, land.at[mine]
    cols = src.shape[1]
    return src, land.at[:, pl.ds(pl.multiple_of(mine * cols, 128), cols)]


def _scatter_route(n_pieces):
    def route(t, src, land, me, peer):
        chip = 2 * peer[0] + peer[1]
        if t >= n_pieces:
            part = src
        elif src.ndim == 4:
            part = src.at[chip, peer[2]]
        else:
            rows, cols = land.shape[1:]
            part = src.at[pl.ds(pl.multiple_of(peer[2] * rows, 16), rows), pl.ds(pl.multiple_of(chip * cols, 128), cols)]
        return part, land.at[4 * me[0] + 2 * me[1] + me[2]]

    return route


def _swap_cores(halves, name):
    n = len(halves)

    def body(*refs):
        src, dst = refs[:n], refs[n:2 * n]
        send_sems, recv_sems = refs[2 * n:]
        x, y, c = _place()
        copies = [_send(src[t], dst[t], send_sems.at[t], recv_sems.at[t], (x, y, 1 - c)) for t in range(n)]
        for cp in copies:
            cp.wait()

    got = pl.pallas_call(
        body, name=name, in_specs=[_HBM] * n, out_specs=[_HBM] * n,
        out_shape=[jax.ShapeDtypeStruct(a.shape, a.dtype) for a in halves],
        scratch_shapes=[pltpu.SemaphoreType.DMA((n,)), pltpu.SemaphoreType.DMA((n,))],
    )(*halves)
    south = lax.axis_index("c") == 0
    return [jnp.stack([jnp.where(south, a, b), jnp.where(south, b, a)]) for a, b in zip(halves, got)]


def _row_tile(rows, cap):
    fits = [t for t in range(16, cap + 1, 16) if rows % t == 0]
    return fits[-1] if fits else rows


def _sum_slots(a, name):
    _, r, c = a.shape
    tr = _row_tile(r, 384)

    def body(a_ref, o_ref):
        s = a_ref[0].astype(F32)
        for k in range(1, N_DEV):
            s = s + a_ref[k].astype(F32)
        o_ref[...] = s

    return pl.pallas_call(
        body, name=name, grid=(r // tr,),
        in_specs=[pl.BlockSpec((N_DEV, tr, c), lambda i: (0, i, 0))],
        out_specs=pl.BlockSpec((tr, c), lambda i: (i, 0)),
        out_shape=jax.ShapeDtypeStruct((r, c), F32),
        compiler_params=_cp(1))(a)


def _adamw(w, g, m, v, name):
    layers, r, c = w.shape
    tr = _row_tile(r, 256)

    def body(w_ref, g_ref, m_ref, v_ref, d_ref, nm_ref, nv_ref):
        gv = g_ref[...]
        nm = ADAM_B1 * m_ref[...] + (1.0 - ADAM_B1) * gv
        nv = ADAM_B2 * v_ref[...] + (1.0 - ADAM_B2) * (gv * gv)
        m_hat = nm / (1.0 - ADAM_B1 ** ADAM_STEP)
        v_hat = nv / (1.0 - ADAM_B2 ** ADAM_STEP)
        d_ref[...] = -ADAM_LR * (m_hat / (jnp.sqrt(v_hat) + ADAM_EPS) + ADAM_WD * w_ref[...])
        nm_ref[...] = nm
        nv_ref[...] = nv

    spec = pl.BlockSpec((None, tr, c), lambda a, i: (a, i, 0))
    return pl.pallas_call(
        body, name=name, grid=(layers, r // tr), in_specs=[spec] * 4, out_specs=[spec] * 3,
        out_shape=[jax.ShapeDtypeStruct((layers, r, c), F32)] * 3,
        compiler_params=_cp(2))(*[pltpu.with_memory_space_constraint(a, pltpu.HBM) for a in (w, g, m, v)])


_SMALL = ["meta_tokens", "ret_head_norm", "gla_w_gate", "gla_b_gate", "gla_head_norm"]
_LOCAL_SMALL = ["meta_tokens", "norm_ffn1", "norm_mix", "norm_ffn2", "ret_head_norm", "gla_w_gate", "gla_b_gate",
                "gla_head_norm", "final_norm"]
_WEIGHTS = ["meta_tokens", "norm_ffn1", "ffn1_w_in", "ffn1_w_out", "norm_mix", "norm_ffn2", "ffn2_w_in", "ffn2_w_out",
            "ret_w_in", "ret_head_norm", "ret_w_out", "gla_w_in", "gla_w_gate", "gla_b_gate", "gla_head_norm",
            "gla_w_out", "final_norm"]


def _pack_rows(arrays, width):
    flat = jnp.concatenate([a.reshape(-1) for a in arrays])
    pad = -flat.shape[0] % (8 * width)
    return jnp.pad(flat, (0, pad)).reshape(-1, width)


def _unpack_rows(packed, shapes):
    flat, out, at = packed.reshape(-1), [], 0
    for s in shapes:
        size = 1
        for dim in s:
            size *= dim
        out.append(flat[at:at + size].reshape(s))
        at += size
    return out


class _WeightGather:
    GROUPS = [("small", "l0_ffn1_in"), ("l0_ffn1_out",), ("ret_in",), ("ret_out",), ("l0_ffn2_in", "l0_ffn2_out"),
              ("l1_ffn1_in", "l1_ffn1_out"), ("gla_in", "gla_out"), ("l1_ffn2_in", "l1_ffn2_out")]

    def __init__(self, p):
        self.small_shapes = [p[name].shape for name in _SMALL]
        self.f32 = {"small": _pack_rows([p[name] for name in _SMALL], 128), "ret_in": p["ret_w_in"][0],
                    "ret_out": p["ret_w_out"][0], "gla_in": p["gla_w_in"][0], "gla_out": p["gla_w_out"][0]}
        for layer in range(2):
            for name in ("ffn1", "ffn2"):
                self.f32[f"l{layer}_{name}_in"] = p[f"{name}_w_in"][layer]
                self.f32[f"l{layer}_{name}_out"] = p[f"{name}_w_out"][layer]
        self.shards = {}
        self.started = {}
        self.pin = None

    def shard(self, name):
        if name not in self.shards:
            a = self.f32[name]
            if name != "small":
                a = (a if self.pin is None else a + self.pin[0, 0]).astype(BF16)
            self.shards[name] = a
        return self.shards[name]

    def later_shards(self, k):
        return [self.shard(name) for group in self.GROUPS[k:] for name in group]

    def start(self, k, after):
        shards = [self.shard(name) for name in self.GROUPS[k]]
        lands = []
        for name, s in zip(self.GROUPS[k], shards):
            if "ffn" in name and name.endswith("_in"):
                lands.append(lax.empty((s.shape[0], N_CHIPS * s.shape[1]), s.dtype))
            else:
                lands.append(lax.empty((N_CHIPS,) + s.shape, s.dtype))
        self.started[k], token = _exchange_start(shards, lands, _gather_route, _GATHER_FLIPS, after, f"gather{k}_start")
        return token

    def wait(self, k, after):
        _, got = _exchange_wait(self.started[k], _gather_route, _GATHER_FLIPS, after, f"gather{k}_wait")
        w = {}
        for name, g in zip(self.GROUPS[k], got):
            if name == "small":
                parts = zip(*[_unpack_rows(g[chip], self.small_shapes) for chip in range(N_CHIPS)])
                cat = lambda a: jnp.moveaxis(a, 0, -2).reshape(a.shape[1:-1] + (-1,))
                meta, ret_gain, wg, bg, gla_gain = [cat(jnp.stack(part)) for part in parts]
                w.update(meta=meta, ret_gain=ret_gain.reshape(1, -1), gla_bg=bg.reshape(1, -1),
                         gla_gain=gla_gain.reshape(1, -1),
                         gla_wg=jnp.pad(wg[0], ((0, 128 - GLA_RANK), (0, 0))).astype(BF16))
            elif name == "gla_in":
                full = jnp.moveaxis(g, 0, 1).reshape(D, -1)
                w[name] = jnp.pad(full, ((0, 0), (0, GLA_U - GLA_IN)))[None]
            elif name.endswith("_out"):
                w[name] = g.reshape(-1, g.shape[-1])
            else:
                w[name] = g
        return w


class _GradExchange:
    def __init__(self):
        self.started = []
        self.token = None
        self.small_shapes = None

    def push(self, k, arrays, small=None):
        srcs, lands = [], []
        for a in arrays:
            if isinstance(a, tuple):
                a = a[1]
                piece = (a.shape[0] // 2, a.shape[1] // N_CHIPS)
            else:
                a = a.reshape(N_CHIPS, 2, -1, a.shape[-1])
                piece = a.shape[2:]
            srcs.append(a)
            lands.append(lax.empty((N_DEV,) + piece, a.dtype))
        if small is not None:
            self.small_shapes = [a.shape for a in small]
            srcs.append(_pack_rows(small, D))
            lands.append(lax.empty((N_DEV,) + srcs[-1].shape, F32))
        started, self.token = _exchange_start(srcs, lands, _scatter_route(len(arrays)), _PEER_FLIPS, None,
                                              f"scatter{k}_start")
        self.started.append((started, len(arrays)))
        return self.token

    def collect(self, groups, after=None):
        x, y, c = _place()
        after, sums = self.token if after is None else after, []
        for k in groups:
            started, n_pieces = self.started[k]
            srcs, got = _exchange_wait(started, _scatter_route(n_pieces), _PEER_FLIPS, after, f"scatter{k}_wait")
            own = []
            for t, (a, g) in enumerate(zip(srcs, got)):
                if t >= n_pieces:
                    own.append(a)
                elif a.ndim == 4:
                    own.append(a[2 * x + y, c])
                else:
                    rows, cols = g.shape[1:]
                    own.append(lax.dynamic_slice(a, (c * rows, (2 * x + y) * cols), (rows, cols)))
            got = [lax.dynamic_update_index_in_dim(g, a, 4 * x + 2 * y + c, 0) for g, a in zip(got, own)]
            sums.append([_sum_slots(a, f"sum{k}_{i}") for i, a in enumerate(got)])
            after = sums[-1][0]
        return sums


def kernel(x, meta_tokens, norm_ffn1, ffn1_w_in, ffn1_w_out, norm_mix, norm_ffn2, ffn2_w_in, ffn2_w_out, ret_w_in, ret_head_norm, ret_w_out, gla_w_in, gla_w_gate, gla_b_gate, gla_head_norm, gla_w_out, final_norm, loss_target, m_meta_tokens, m_norm_ffn1, m_ffn1_w_in, m_ffn1_w_out, m_norm_mix, m_norm_ffn2, m_ffn2_w_in, m_ffn2_w_out, m_ret_w_in, m_ret_head_norm, m_ret_w_out, m_gla_w_in, m_gla_w_gate, m_gla_b_gate, m_gla_head_norm, m_gla_w_out, m_final_norm, v_meta_tokens, v_norm_ffn1, v_ffn1_w_in, v_ffn1_w_out, v_norm_mix, v_norm_ffn2, v_ffn2_w_in, v_ffn2_w_out, v_ret_w_in, v_ret_head_norm, v_ret_w_out, v_gla_w_in, v_gla_w_gate, v_gla_b_gate, v_gla_head_norm, v_gla_w_out, v_final_norm):
    p = dict(meta_tokens=meta_tokens, norm_ffn1=norm_ffn1, ffn1_w_in=ffn1_w_in, ffn1_w_out=ffn1_w_out, norm_mix=norm_mix,
             norm_ffn2=norm_ffn2, ffn2_w_in=ffn2_w_in, ffn2_w_out=ffn2_w_out, ret_w_in=ret_w_in,
             ret_head_norm=ret_head_norm, ret_w_out=ret_w_out, gla_w_in=gla_w_in, gla_w_gate=gla_w_gate,
             gla_b_gate=gla_b_gate, gla_head_norm=gla_head_norm, gla_w_out=gla_w_out, final_norm=final_norm)
    m = dict(zip(_WEIGHTS, (m_meta_tokens, m_norm_ffn1, m_ffn1_w_in, m_ffn1_w_out, m_norm_mix, m_norm_ffn2, m_ffn2_w_in,
                            m_ffn2_w_out, m_ret_w_in, m_ret_head_norm, m_ret_w_out, m_gla_w_in, m_gla_w_gate,
                            m_gla_b_gate, m_gla_head_norm, m_gla_w_out, m_final_norm)))
    v = dict(zip(_WEIGHTS, (v_meta_tokens, v_norm_ffn1, v_ffn1_w_in, v_ffn1_w_out, v_norm_mix, v_norm_ffn2, v_ffn2_w_in,
                            v_ffn2_w_out, v_ret_w_in, v_ret_head_norm, v_ret_w_out, v_gla_w_in, v_gla_w_gate,
                            v_gla_b_gate, v_gla_head_norm, v_gla_w_out, v_final_norm)))

    exchange = _GradExchange()
    d_x = _sequence_grads(x[0], loss_target[0], p, _WeightGather(p), exchange)
    names = [("ffn2_w_in", 1), ("ffn2_w_out", 1), ("gla_w_in", 0), ("gla_w_out", 0), ("ffn1_w_in", 1), ("ffn1_w_out", 1),
             ("ffn2_w_in", 0), ("ffn2_w_out", 0), ("ret_w_in", 0), ("ret_w_out", 0), ("ffn1_w_in", 0), ("ffn1_w_out", 0)]
    shard, grads, delta, new_m, new_v = {}, {}, {}, {}, {}

    def swap(sums, keys, name):
        for key, a in zip(keys, _swap_cores(sums, name)):
            shard[key] = a.reshape(-1, a.shape[-1])

    def update(name):
        layers = p[name].shape[0]
        grads[name] = jnp.stack([shard[name, layer] for layer in range(layers)])
        delta[name], new_m[name], new_v[name] = _adamw(p[name], grads[name], m[name], v[name], f"adamw_{name}")

    swap([a for group in exchange.collect(range(5)) for a in group], names[:10], "swap_first")
    for name in ("ffn2_w_in", "ffn2_w_out", "ret_w_in", "ret_w_out", "gla_w_in", "gla_w_out"):
        update(name)
    last, (small_sum,) = exchange.collect([5, 6], after=list(delta.values()))
    swap(last, names[10:], "swap_last")
    for name in ("ffn1_w_in", "ffn1_w_out"):
        update(name)

    chip = 2 * lax.axis_index("x") + lax.axis_index("y")
    cols = lambda a, n: lax.dynamic_slice_in_dim(a, chip * n, n, axis=a.ndim - 1)
    (s_meta, s_n1a, s_n1b, s_nma, s_nmb, s_n2a, s_n2b, s_final, s_ret_gain, s_wg, s_bg, s_gla_gain,
     s_loss) = _unpack_rows(small_sum, exchange.small_shapes)
    grads.update({
        "meta_tokens": cols(s_meta, 256), "norm_ffn1": jnp.concatenate([s_n1a, s_n1b]),
        "norm_mix": jnp.concatenate([s_nma, s_nmb]), "norm_ffn2": jnp.concatenate([s_n2a, s_n2b]),
        "final_norm": s_final.reshape(D),
        "ret_head_norm": cols(s_ret_gain.reshape(1, HEADS, RET_DV), RET_DV // N_CHIPS),
        "gla_w_gate": cols(s_wg, GLA_DK)[None], "gla_b_gate": cols(s_bg, GLA_DK),
        "gla_head_norm": cols(s_gla_gain.reshape(1, HEADS, GLA_DV), GLA_DV // N_CHIPS),
    })
    for name in _LOCAL_SMALL:
        shape = p[name].shape
        as3d = lambda a: a.reshape((1,) * (3 - len(shape)) + shape)
        out = _adamw(as3d(p[name]), as3d(grads[name]), as3d(m[name]), as3d(v[name]), f"adamw_{name}")
        delta[name], new_m[name], new_v[name] = [a.reshape(shape) for a in out]

    return (s_loss.reshape(()), d_x[None], *[grads[n] for n in _WEIGHTS], *[delta[n] for n in _WEIGHTS],
            *[new_m[n] for n in _WEIGHTS], *[new_v[n] for n in _WEIGHTS])
```

```python
import functools

import jax
import numpy as np
import jax.numpy as jnp
from jax import lax
from jax.experimental import pallas as pl
from jax.experimental.pallas import tpu as pltpu

F32, BF16 = jnp.float32, jnp.bfloat16
MESH = pl.DeviceIdType.MESH

D = 1024
N_META = 16
CHUNK = 64
RET_CHUNK = 256
FRONT = 256
D_FF = 2816
EPS = 1e-6
HEADS = 4
RET_DK, RET_DV = 256, 512
GLA_DK, GLA_DV = 128, 256
GLA_RANK = 16
GLA_TAU = 16.0
GLA_IN = 2 * HEADS * GLA_DK + 2 * HEADS * GLA_DV + GLA_RANK
GLA_U = 3328
ROPE_BASE = 10000.0
N_CHIPS = 4
N_DEV = 8

ADAM_LR, ADAM_B1, ADAM_B2, ADAM_EPS, ADAM_WD, ADAM_STEP = 0.001, 0.9, 0.999, 1e-08, 0.01, 10

VMEM_LIMIT_BYTES = 56 * 1024 * 1024
TM = 768
TM_SMALL = 256


TM_RESIDENT = 384
MXU_TILE = 256


def _cp(n_axes):
    return pltpu.CompilerParams(dimension_semantics=("arbitrary",) * n_axes, vmem_limit_bytes=VMEM_LIMIT_BYTES)


def _resident(shape, n_axes):
    zeros = (0,) * len(shape)
    index = (lambda i: zeros) if n_axes == 1 else (lambda i, j: zeros)
    return pl.BlockSpec(shape, index, pipeline_mode=pl.Buffered(1))


def _dg(a, b, ca, cb):
    nb = a.ndim - 2
    dims = (((ca + nb,), (cb + nb,)), (tuple(range(nb)), tuple(range(nb))))
    return lax.dot_general(a.astype(BF16), b.astype(BF16), dims, preferred_element_type=F32)


@jax.custom_vjp
def _nn(a, b):
    return _dg(a, b, 1, 0)


@jax.custom_vjp
def _nt(a, b):
    return _dg(a, b, 1, 1)


@jax.custom_vjp
def _tn(a, b):
    return _dg(a, b, 0, 0)


def _dot_vjp(fn, ca, cb, da, db):
    def fwd(a, b):
        a, b = a.astype(BF16), b.astype(BF16)
        return _dg(a, b, ca, cb), (a, b)

    def bwd(res, g):
        a, b = res
        g = g.astype(BF16)
        grad = lambda other, dims, g_first: _dg(g, other, *dims) if g_first else _dg(other, g, *dims)
        return grad(b, *da), grad(a, *db)

    fn.defvjp(fwd, bwd)


_dot_vjp(_nn, 1, 0, ((1, 1), True), ((0, 0), False))
_dot_vjp(_nt, 1, 1, ((1, 0), True), ((0, 0), True))
_dot_vjp(_tn, 0, 0, ((1, 1), False), ((1, 0), False))


def _split_dot(m, a, parts):
    mb = jnp.broadcast_to(m, a.shape[:-2] + m.shape)
    total, rest = None, a
    for _ in range(parts):
        term = rest.astype(BF16)
        rest = rest - term.astype(F32)
        product = _dg(mb, term, 1, 0)
        total = product if total is None else total + product
    return total


def _make_cum(parts):
    @jax.custom_vjp
    def cum(m, mt, a):
        return _split_dot(m, a, parts)

    cum.defvjp(lambda m, mt, a: (_split_dot(m, a, parts), (m, mt)),
               lambda res, g: (jnp.zeros_like(res[0]), jnp.zeros_like(res[1]), _split_dot(res[1], g, parts)))
    return cum


_cum = _make_cum(3)
_cum16 = _make_cum(2)


def _sigmoid(x):
    return 1.0 / (1.0 + jnp.exp(-x))


def _rms(x):
    return lax.rsqrt(jnp.mean(x * x, axis=-1, keepdims=True) + EPS)


def _rmsnorm_bwd(dy, x, gain):
    r = _rms(x)
    xhat = x * r
    dxh = dy * gain
    return r * (dxh - xhat * jnp.mean(dxh * xhat, axis=-1, keepdims=True)), xhat


def _norm_proj(h, gain, w, name):
    tp, d = h.shape
    s, _, ns = w.shape

    tm = TM_RESIDENT

    def body(h_ref, g_ref, w_ref, hn_ref, u_ref):
        x = h_ref[...]
        a = (x * _rms(x) * g_ref[...]).astype(BF16)
        hn_ref[...] = a
        for k in range(s):
            u_ref[:, ns * k:ns * (k + 1)] = jnp.dot(a, w_ref[k], preferred_element_type=F32).astype(BF16)

    return pl.pallas_call(
        body, name=name, grid=(tp // tm,),
        in_specs=[pl.BlockSpec((tm, d), lambda i: (i, 0)), pl.BlockSpec((1, d), lambda i: (0, 0)), _resident(w.shape, 1)],
        out_specs=[pl.BlockSpec((tm, d), lambda i: (i, 0)), pl.BlockSpec((tm, s * ns), lambda i: (i, 0))],
        out_shape=[jax.ShapeDtypeStruct((tp, d), BF16), jax.ShapeDtypeStruct((tp, s * ns), BF16)],
        compiler_params=_cp(1))(h, gain, w)


def _norm_ffn_in(h, gain, w, name):
    tp, d = h.shape
    ff = w.shape[1] // 2
    tm = TM_RESIDENT
    blocks = [(c, min(c + 6 * MXU_TILE, ff)) for c in range(0, ff, 6 * MXU_TILE)]

    def body(h_ref, g_ref, w_ref, hn_ref, dg_ref, du_ref, act_ref):
        x = h_ref[...]
        a = (x * _rms(x) * g_ref[...]).astype(BF16)
        hn_ref[...] = a
        for c0, c1 in blocks:
            g = jnp.dot(a, w_ref[:, c0:c1], preferred_element_type=F32)
            u = jnp.dot(a, w_ref[:, ff + c0:ff + c1], preferred_element_type=F32)
            sg = _sigmoid(g)
            silu = g * sg
            dg_ref[:, c0:c1] = (u * (sg + silu * (1.0 - sg))).astype(BF16)
            du_ref[:, c0:c1] = silu.astype(BF16)
            act_ref[:, c0:c1] = (silu * u).astype(BF16)

    wide = jax.ShapeDtypeStruct((tp, ff), BF16)
    return pl.pallas_call(
        body, name=name, grid=(tp // tm,),
        in_specs=[pl.BlockSpec((tm, d), lambda i: (i, 0)), pl.BlockSpec((1, d), lambda i: (0, 0)),
                  _resident(w.shape, 1)],
        out_specs=[pl.BlockSpec((tm, d), lambda i: (i, 0))] + [pl.BlockSpec((tm, ff), lambda i: (i, 0))] * 3,
        out_shape=[jax.ShapeDtypeStruct((tp, d), BF16), wide, wide, wide],
        compiler_params=_cp(1))(h, gain, w)


def _out_proj(a, w, h, scale, name):
    tp, k = a.shape
    d = w.shape[1]

    def body(a_ref, w_ref, h_ref, o_ref):
        o_ref[...] = h_ref[...] + scale * jnp.dot(a_ref[...], w_ref[...], preferred_element_type=F32)

    return pl.pallas_call(
        body, name=name, grid=(tp // TM,),
        in_specs=[pl.BlockSpec((TM, k), lambda i: (i, 0)), pl.BlockSpec((k, d), lambda i: (0, 0)),
                  pl.BlockSpec((TM, d), lambda i: (i, 0))],
        out_specs=pl.BlockSpec((TM, d), lambda i: (i, 0)),
        out_shape=jax.ShapeDtypeStruct((tp, d), F32),
        compiler_params=_cp(1))(a, w, h)


def _out_proj_bwd(dh, w, on, name):
    tp, d = dh.shape
    k = w.shape[0]
    steps = tp // TM

    def body(dh_ref, w_ref, on_ref, don_ref, dw_ref, acc_ref):
        i = pl.program_id(0)

        @pl.when(i == 0)
        def _():
            acc_ref[...] = jnp.zeros_like(acc_ref)

        g = dh_ref[...].astype(BF16)
        don_ref[...] = lax.dot_general(g, w_ref[...], (((1,), (1,)), ((), ())), preferred_element_type=F32).astype(BF16)
        acc_ref[...] += lax.dot_general(on_ref[...], g, (((0,), (0,)), ((), ())), preferred_element_type=F32)

        @pl.when(i == steps - 1)
        def _():
            dw_ref[...] = acc_ref[...].astype(BF16)

    return pl.pallas_call(
        body, name=name, grid=(steps,),
        in_specs=[pl.BlockSpec((TM, d), lambda i: (i, 0)), _resident(w.shape, 1), pl.BlockSpec((TM, k), lambda i: (i, 0))],
        out_specs=[pl.BlockSpec((TM, k), lambda i: (i, 0)), pl.BlockSpec((k, d), lambda i: (0, 0))],
        out_shape=[jax.ShapeDtypeStruct((tp, k), BF16), jax.ShapeDtypeStruct((k, d), BF16)],
        scratch_shapes=[pltpu.VMEM((k, d), F32)],
        compiler_params=_cp(1))(dh, w, on)


def _wgrad(a, b, *, bm, bn, scale, sharded, name):
    tp, m = a.shape
    n = b.shape[1]
    nk = tp // TM

    def body(a_ref, b_ref, o_ref, acc_ref):
        k = pl.program_id(2)

        @pl.when(k == 0)
        def _():
            acc_ref[...] = jnp.zeros_like(acc_ref)

        bb = b_ref[...]
        if scale != 1.0:
            bb = scale * bb
        acc_ref[...] += lax.dot_general(a_ref[...], bb.astype(BF16), (((0,), (0,)), ((), ())),
                                        preferred_element_type=F32)

        @pl.when(k == nk - 1)
        def _():
            o_ref[...] = acc_ref[...].astype(BF16)

    if sharded:
        assert m == bm
        out_spec = pl.BlockSpec((None, bm, bn), lambda i, j, k: (j, 0, 0))
        out_shape = jax.ShapeDtypeStruct((n // bn, m, bn), BF16)
    else:
        out_spec = pl.BlockSpec((bm, bn), lambda i, j, k: (i, j))
        out_shape = jax.ShapeDtypeStruct((m, n), BF16)
    return pl.pallas_call(
        body, name=name, grid=(m // bm, n // bn, nk),
        in_specs=[pl.BlockSpec((TM, bm), lambda i, j, k: (k, i)), pl.BlockSpec((TM, bn), lambda i, j, k: (k, j))],
        out_specs=out_spec, out_shape=out_shape,
        scratch_shapes=[pltpu.VMEM((bm, bn), F32)],
        compiler_params=_cp(3))(a, b)


def _dgrad_norm(du, w, h, gain, dh_out, name):
    tp, d = h.shape
    s, _, ns = w.shape
    tm = TM_RESIDENT

    def body(du_ref, w_ref, h_ref, g_ref, dho_ref, dhi_ref, dg_ref):
        @pl.when(pl.program_id(0) == 0)
        def _():
            dg_ref[...] = jnp.zeros_like(dg_ref)

        dhn = None
        for k in range(s):
            part = lax.dot_general(du_ref[:, ns * k:ns * (k + 1)], w_ref[k], (((1,), (1,)), ((), ())),
                                   preferred_element_type=F32)
            dhn = part if dhn is None else dhn + part
        dx, xhat = _rmsnorm_bwd(dhn, h_ref[...], g_ref[...])
        dg_ref[...] += jnp.sum(dhn * xhat, axis=0, keepdims=True)
        dhi_ref[...] = dho_ref[...] + dx

    return pl.pallas_call(
        body, name=name, grid=(tp // tm,),
        in_specs=[pl.BlockSpec((tm, s * ns), lambda i: (i, 0)), _resident(w.shape, 1),
                  pl.BlockSpec((tm, d), lambda i: (i, 0)), pl.BlockSpec((1, d), lambda i: (0, 0)),
                  pl.BlockSpec((tm, d), lambda i: (i, 0))],
        out_specs=[pl.BlockSpec((tm, d), lambda i: (i, 0)), pl.BlockSpec((1, d), lambda i: (0, 0))],
        out_shape=[jax.ShapeDtypeStruct((tp, d), F32), jax.ShapeDtypeStruct((1, d), F32)],
        compiler_params=_cp(1))(du, w, h, gain, dh_out)


def _loss_head(act, w_out, h, gain, target, name):
    tp, d = h.shape
    ff = act.shape[1]
    tm = TM_SMALL
    front_tiles = FRONT // tm

    def body(a_ref, w_ref, h_ref, g_ref, t_ref, dh_ref, dg_ref, loss_ref):
        i = pl.program_id(0)

        @pl.when(i == 0)
        def _():
            dg_ref[...] = jnp.zeros_like(dg_ref)
            loss_ref[...] = jnp.zeros_like(loss_ref)

        x = h_ref[...] + 0.5 * jnp.dot(a_ref[...], w_ref[...], preferred_element_type=F32)
        gain_v = g_ref[...]
        y = x * _rms(x) * gain_v
        err = jnp.where(i >= front_tiles, y - t_ref[...], 0.0)
        loss_ref[...] += 0.5 * jnp.sum(jnp.mean(err * err, axis=-1, keepdims=True), axis=0, keepdims=True)
        dy = err * (1.0 / d)
        dx, xhat = _rmsnorm_bwd(dy, x, gain_v)
        dg_ref[...] += jnp.sum(dy * xhat, axis=0, keepdims=True)
        dh_ref[...] = dx

    return pl.pallas_call(
        body, name=name, grid=(tp // tm,),
        in_specs=[pl.BlockSpec((tm, ff), lambda i: (i, 0)), _resident(w_out.shape, 1),
                  pl.BlockSpec((tm, d), lambda i: (i, 0)), pl.BlockSpec((1, d), lambda i: (0, 0)),
                  pl.BlockSpec((tm, d), lambda i: (jnp.maximum(i - front_tiles, 0), 0))],
        out_specs=[pl.BlockSpec((tm, d), lambda i: (i, 0)), pl.BlockSpec((1, d), lambda i: (0, 0)),
                   pl.BlockSpec((1, 128), lambda i: (0, 0))],
        out_shape=[jax.ShapeDtypeStruct((tp, d), F32), jax.ShapeDtypeStruct((1, d), F32),
                   jax.ShapeDtypeStruct((1, 128), F32)],
        compiler_params=_cp(1))(act, w_out, h, gain, target)


def _gated_headnorm(o, g, gain):
    return o * _rms(o) * gain * (g * _sigmoid(g))


def _row_mask(chunk, size=CHUNK):
    rows = chunk * size + lax.broadcasted_iota(jnp.int32, (size, 1), 0)
    return (rows >= FRONT - N_META).astype(F32)


def _ret_head(q1, q2, k1, k2, v, g, state, gain, cos, sin, dmat, dq, dk, dc):
    q = jnp.concatenate([q1 * cos - q2 * sin, q1 * sin + q2 * cos], axis=-1)
    k = jnp.concatenate([k1 * cos - k2 * sin, k1 * sin + k2 * cos], axis=-1) * (RET_DK ** -0.5)
    scores = _nt(q, k) * dmat
    o = _nn(scores, v) + _nn(q * dq, state)
    new_state = state * dc + _tn(k * dk, v)
    return _gated_headnorm(o, g, gain), new_state


def _ret_consts():
    log_gamma = jnp.log1p(-2.0 ** (-5.0 - jnp.arange(HEADS, dtype=F32)))
    idx = jnp.arange(RET_CHUNK, dtype=F32)
    rel = idx[:, None] - idx[None, :]
    dmat = jnp.where(rel >= 0, jnp.exp(log_gamma[:, None, None] * jnp.maximum(rel, 0.0)), 0.0)
    dq = jnp.exp(log_gamma[:, None] * (idx + 1.0))[..., None]
    dk = jnp.exp(log_gamma[:, None] * (RET_CHUNK - 1.0 - idx))[..., None]
    dc = jnp.broadcast_to(jnp.exp(log_gamma * RET_CHUNK)[:, None, None], (HEADS, 1, 128))
    return dmat, dq, dk, dc


def _rope_tables(tp):
    half = RET_DK // 2
    inv = 1.0 / (ROPE_BASE ** jnp.linspace(0.0, 1.0, half, dtype=F32))
    pos = (jnp.arange(tp) - (FRONT - N_META)).astype(F32)
    ang = pos[:, None] * inv[None, :]
    return jnp.cos(ang), jnp.sin(ang)


_RET_V0, _RET_G0 = 2 * D, 4 * D


def _heads(ref, start, width, stride=None, rows=slice(None)):
    stride = width if stride is None else stride
    return jnp.stack([ref[rows, start + stride * h:start + stride * h + width].astype(F32) for h in range(HEADS)])


def _put_heads(ref, start, value, mask, stride=None, rows=slice(None)):
    width = value.shape[-1]
    stride = width if stride is None else stride
    for h in range(HEADS):
        ref[rows, start + stride * h:start + stride * h + width] = (value[h] * mask).astype(ref.dtype)


def _ret_pieces(u_ref):
    hk = RET_DK // 2
    return (_heads(u_ref, 0, hk, RET_DK), _heads(u_ref, hk, hk, RET_DK), _heads(u_ref, D, hk, RET_DK),
            _heads(u_ref, D + hk, hk, RET_DK), _heads(u_ref, _RET_V0, RET_DV), _heads(u_ref, _RET_G0, RET_DV))


def _ret_const_specs(rev=None):
    c = (lambda n: (rev(n), 0)) if rev else (lambda n: (n, 0))
    z3 = lambda n: (0, 0, 0)
    return [pl.BlockSpec((RET_CHUNK, RET_DK // 2), c), pl.BlockSpec((RET_CHUNK, RET_DK // 2), c),
            pl.BlockSpec((HEADS, RET_CHUNK, RET_CHUNK), z3), pl.BlockSpec((HEADS, RET_CHUNK, 1), z3),
            pl.BlockSpec((HEADS, RET_CHUNK, 1), z3), pl.BlockSpec((HEADS, 1, 128), z3)]


def _ret_fwd(u, gain, rope, h, w_out, name):
    tp = u.shape[0]
    nch = tp // RET_CHUNK
    cos, sin = rope
    dmat, dq, dk, dc = _ret_consts()

    def body(u_ref, gain_ref, h_ref, w_ref, cos_ref, sin_ref, dmat_ref, dq_ref, dk_ref, dc_ref,
             on_ref, st_ref, hmix_ref, state_ref):
        @pl.when(pl.program_id(0) == 0)
        def _():
            state_ref[...] = jnp.zeros_like(state_ref)

        state = state_ref[...]
        st_ref[...] = state.astype(BF16)
        on, new_state = _ret_head(*_ret_pieces(u_ref), state, _heads(gain_ref, 0, RET_DV), cos_ref[...], sin_ref[...],
                                  dmat_ref[...], dq_ref[...], dk_ref[...], dc_ref[...][:, :, :1])
        state_ref[...] = new_state
        _put_heads(on_ref, 0, on, 1.0)
        hmix_ref[...] = h_ref[...] + jnp.dot(on_ref[...], w_ref[...], preferred_element_type=F32)

    rows = lambda width: pl.BlockSpec((RET_CHUNK, width), lambda n: (n, 0))
    return pl.pallas_call(
        body, name=name, grid=(nch,),
        in_specs=[rows(6 * D), pl.BlockSpec((1, HEADS * RET_DV), lambda n: (0, 0)), rows(D),
                  _resident(w_out.shape, 1)] + _ret_const_specs(),
        out_specs=[rows(HEADS * RET_DV), pl.BlockSpec((None, HEADS, RET_DK, RET_DV), lambda n: (n, 0, 0, 0)), rows(D)],
        out_shape=[jax.ShapeDtypeStruct((tp, HEADS * RET_DV), BF16),
                   jax.ShapeDtypeStruct((nch, HEADS, RET_DK, RET_DV), BF16), jax.ShapeDtypeStruct((tp, D), F32)],
        scratch_shapes=[pltpu.VMEM((HEADS, RET_DK, RET_DV), F32)],
        compiler_params=_cp(1))(u, gain, h, w_out, cos, sin, dmat, dq, dk, dc)


def _ret_bwd(u, gain, rope, states, d_on, name):
    tp = u.shape[0]
    nch = tp // RET_CHUNK
    cos, sin = rope
    dmat, dq, dk, dc = _ret_consts()
    rev = lambda n: nch - 1 - n
    hk = RET_DK // 2

    def body(u_ref, gain_ref, st_ref, don_ref, cos_ref, sin_ref, dmat_ref, dq_ref, dk_ref, dc_ref,
             du_ref, dgain_ref, dstate_ref):
        @pl.when(pl.program_id(0) == 0)
        def _():
            dstate_ref[...] = jnp.zeros_like(dstate_ref)
            dgain_ref[...] = jnp.zeros_like(dgain_ref)

        mask = _row_mask(rev(pl.program_id(0)), RET_CHUNK)
        consts = (cos_ref[...], sin_ref[...], dmat_ref[...], dq_ref[...], dk_ref[...], dc_ref[...][:, :, :1])
        _, vjp = jax.vjp(lambda *a: _ret_head(*a, *consts), *_ret_pieces(u_ref), st_ref[...].astype(F32),
                         _heads(gain_ref, 0, RET_DV))
        dq1, dq2, dk1, dk2, dv, dg, dstate, dgain = vjp((_heads(don_ref, 0, RET_DV), dstate_ref[...]))
        dstate_ref[...] = dstate
        for hd in range(HEADS):
            dgain_ref[:, RET_DV * hd:RET_DV * (hd + 1)] += dgain[hd]
        _put_heads(du_ref, 0, dq1, mask, RET_DK)
        _put_heads(du_ref, hk, dq2, mask, RET_DK)
        _put_heads(du_ref, D, dk1, mask, RET_DK)
        _put_heads(du_ref, D + hk, dk2, mask, RET_DK)
        _put_heads(du_ref, _RET_V0, dv, mask)
        _put_heads(du_ref, _RET_G0, dg, mask)

    return pl.pallas_call(
        body, name=name, grid=(nch,),
        in_specs=[pl.BlockSpec((RET_CHUNK, 6 * D), lambda n: (rev(n), 0)),
                  pl.BlockSpec((1, HEADS * RET_DV), lambda n: (0, 0)),
                  pl.BlockSpec((None, HEADS, RET_DK, RET_DV), lambda n: (rev(n), 0, 0, 0)),
                  pl.BlockSpec((RET_CHUNK, HEADS * RET_DV), lambda n: (rev(n), 0))] + _ret_const_specs(rev),
        out_specs=[pl.BlockSpec((RET_CHUNK, 6 * D), lambda n: (rev(n), 0)),
                   pl.BlockSpec((1, HEADS * RET_DV), lambda n: (0, 0))],
        out_shape=[jax.ShapeDtypeStruct((tp, 6 * D), BF16), jax.ShapeDtypeStruct((1, HEADS * RET_DV), F32)],
        scratch_shapes=[pltpu.VMEM((HEADS, RET_DK, RET_DV), F32)],
        compiler_params=_cp(1))(u, gain, states, d_on, cos, sin, dmat, dq, dk, dc)


_GLA_K0, _GLA_V0, _GLA_G0, _GLA_Z0 = 512, 1024, 2048, 3072


def _gla_head(q, k, v, g, z, state_t, wg, bg, gain, mask, lo, lo_t, to_mid, to_mid_t, in_second, pair):
    ga = _nn(jnp.broadcast_to(z, wg.shape[:-2] + z.shape), wg) + bg
    log_a = (jnp.minimum(ga, 0.0) - jnp.log(1.0 + jnp.exp(-jnp.abs(ga)))) * (mask * (1.0 / GLA_TAU))
    bcum = _cum(lo, lo_t, log_a)
    btot = jnp.sum(log_a, axis=-2, keepdims=True)
    qs = q * (GLA_DK ** -0.5)
    heads, levels = q.shape[0], pair.shape[0]
    decay = jnp.exp(_cum16(to_mid, to_mid_t, log_a).reshape(heads, levels, CHUNK, GLA_DK))
    qk = (jnp.where(in_second > 0.0, qs[:, None], k[:, None]) * decay).reshape(heads * levels, CHUNK, GLA_DK)
    rows = lax.broadcasted_iota(jnp.int32, (CHUNK, CHUNK), 0)
    cols = lax.broadcasted_iota(jnp.int32, (CHUNK, CHUNK), 1)
    scores = (jnp.where(rows == cols, _nt(qs, k), 0.0)
              + jnp.sum(_nt(qk, qk).reshape(heads, levels, CHUNK, CHUNK) * pair, axis=1))
    o = _nn(scores, v) + _nt(qs * jnp.exp(bcum), state_t)
    new_state_t = state_t * jnp.exp(btot) + _tn(v, k * jnp.exp(btot - bcum))
    return _gated_headnorm(o, g, gain), new_state_t


def _gla_consts():
    r, c = np.meshgrid(np.arange(CHUNK), np.arange(CHUNK), indexing="ij")
    to_mid, second, pair = [], [], []
    block = 2
    while block <= CHUNK:
        mid = (r // block) * block + block // 2
        to_mid.append(((r >= mid) & (c > mid) & (c <= r)) | ((r < mid) & (c > r) & (c <= mid)))
        second.append((r >= mid)[:, :1])
        pair.append((r // block == c // block) & (r >= mid) & (c < mid))
        block *= 2
    to_mid = np.concatenate(to_mid)
    bf = lambda m: jnp.asarray(m, F32).astype(BF16)
    f32 = lambda ms: jnp.asarray(np.stack(ms), F32)
    return bf(r >= c), bf(c >= r), bf(to_mid), bf(to_mid.T), f32(second), f32(pair)


def _gla_const_specs(consts):
    return [pl.BlockSpec(a.shape, functools.partial(lambda nd, n: (0,) * nd, a.ndim)) for a in consts]


GLA_STEP_CHUNKS = 4


def _gla_pieces(u_ref, rows):
    return (_heads(u_ref, 0, GLA_DK, rows=rows), _heads(u_ref, _GLA_K0, GLA_DK, rows=rows),
            _heads(u_ref, _GLA_V0, GLA_DV, rows=rows), _heads(u_ref, _GLA_G0, GLA_DV, rows=rows),
            u_ref[rows, _GLA_Z0:_GLA_Z0 + 128].astype(F32))


def _gla_fwd(u, wg, bg, gain, name):
    tp = u.shape[0]
    nch = tp // CHUNK
    per = GLA_STEP_CHUNKS
    consts = _gla_consts()

    def body(u_ref, wg_ref, bg_ref, gain_ref, *refs):
        const_refs, (on_ref, st_ref, state_ref) = refs[:len(consts)], refs[len(consts):]

        @pl.when(pl.program_id(0) == 0)
        def _():
            state_ref[...] = jnp.zeros_like(state_ref)

        params = (_heads(wg_ref, 0, GLA_DK), _heads(bg_ref, 0, GLA_DK), _heads(gain_ref, 0, GLA_DV))
        mats = [ref[...] for ref in const_refs]
        state = state_ref[...]
        for c in range(per):
            rows = slice(CHUNK * c, CHUNK * (c + 1))
            st_ref[c] = state.astype(BF16)
            on, state = _gla_head(*_gla_pieces(u_ref, rows), state, *params, _row_mask(pl.program_id(0) * per + c), *mats)
            _put_heads(on_ref, 0, on, 1.0, rows=rows)
        state_ref[...] = state

    rows_spec = lambda width: pl.BlockSpec((per * CHUNK, width), lambda n: (n, 0))
    full = lambda r, c: pl.BlockSpec((r, c), lambda n: (0, 0))
    return pl.pallas_call(
        body, name=name, grid=(nch // per,),
        in_specs=[rows_spec(GLA_U), full(128, HEADS * GLA_DK), full(1, HEADS * GLA_DK), full(1, HEADS * GLA_DV)]
                 + _gla_const_specs(consts),
        out_specs=[rows_spec(HEADS * GLA_DV), pl.BlockSpec((per, HEADS, GLA_DV, GLA_DK), lambda n: (n, 0, 0, 0))],
        out_shape=[jax.ShapeDtypeStruct((tp, HEADS * GLA_DV), BF16),
                   jax.ShapeDtypeStruct((nch, HEADS, GLA_DV, GLA_DK), BF16)],
        scratch_shapes=[pltpu.VMEM((HEADS, GLA_DV, GLA_DK), F32)],
        compiler_params=_cp(1))(u, wg, bg, gain, *consts)


def _gla_bwd(u, wg, bg, gain, states, d_on, name):
    tp = u.shape[0]
    per = GLA_STEP_CHUNKS
    steps = tp // (per * CHUNK)
    rev = lambda n: steps - 1 - n
    consts = _gla_consts()

    def body(u_ref, wg_ref, bg_ref, gain_ref, st_ref, don_ref, *refs):
        const_refs, (du_ref, dwg_ref, dbg_ref, dgain_ref, dstate_ref) = refs[:len(consts)], refs[len(consts):]

        @pl.when(pl.program_id(0) == 0)
        def _():
            dstate_ref[...] = jnp.zeros_like(dstate_ref)
            dwg_ref[...] = jnp.zeros_like(dwg_ref)
            dbg_ref[...] = jnp.zeros_like(dbg_ref)
            dgain_ref[...] = jnp.zeros_like(dgain_ref)

        params = (_heads(wg_ref, 0, GLA_DK), _heads(bg_ref, 0, GLA_DK), _heads(gain_ref, 0, GLA_DV))
        mats = [ref[...] for ref in const_refs]
        dstate = dstate_ref[...]
        for c in reversed(range(per)):
            rows = slice(CHUNK * c, CHUNK * (c + 1))
            mask = _row_mask(rev(pl.program_id(0)) * per + c)
            _, vjp = jax.vjp(lambda *a: _gla_head(*a, mask, *mats), *_gla_pieces(u_ref, rows),
                             st_ref[c].astype(F32), *params)
            dq, dk, dv, dg, dz, dstate, dwg, dbg, dgain = vjp((_heads(don_ref, 0, GLA_DV, rows=rows), dstate))
            for hd in range(HEADS):
                dwg_ref[:, GLA_DK * hd:GLA_DK * (hd + 1)] += dwg[hd]
                dbg_ref[:, GLA_DK * hd:GLA_DK * (hd + 1)] += dbg[hd]
                dgain_ref[:, GLA_DV * hd:GLA_DV * (hd + 1)] += dgain[hd]
            _put_heads(du_ref, 0, dq, mask, rows=rows)
            _put_heads(du_ref, _GLA_K0, dk, mask, rows=rows)
            _put_heads(du_ref, _GLA_V0, dv, mask, rows=rows)
            _put_heads(du_ref, _GLA_G0, dg, mask, rows=rows)
            du_ref[rows, _GLA_Z0:_GLA_Z0 + 128] = dz.astype(BF16)
            du_ref[rows, _GLA_Z0 + 128:] = jnp.zeros((CHUNK, GLA_U - _GLA_Z0 - 128), BF16)
        dstate_ref[...] = dstate

    full = lambda r, c: pl.BlockSpec((r, c), lambda n: (0, 0))
    return pl.pallas_call(
        body, name=name, grid=(steps,),
        in_specs=[pl.BlockSpec((per * CHUNK, GLA_U), lambda n: (rev(n), 0)), full(128, HEADS * GLA_DK),
                  full(1, HEADS * GLA_DK), full(1, HEADS * GLA_DV),
                  pl.BlockSpec((per, HEADS, GLA_DV, GLA_DK), lambda n: (rev(n), 0, 0, 0)),
                  pl.BlockSpec((per * CHUNK, HEADS * GLA_DV), lambda n: (rev(n), 0))] + _gla_const_specs(consts),
        out_specs=[pl.BlockSpec((per * CHUNK, GLA_U), lambda n: (rev(n), 0)), full(128, HEADS * GLA_DK),
                   full(1, HEADS * GLA_DK), full(1, HEADS * GLA_DV)],
        out_shape=[jax.ShapeDtypeStruct((tp, GLA_U), BF16), jax.ShapeDtypeStruct((128, HEADS * GLA_DK), F32),
                   jax.ShapeDtypeStruct((1, HEADS * GLA_DK), F32), jax.ShapeDtypeStruct((1, HEADS * GLA_DV), F32)],
        scratch_shapes=[pltpu.VMEM((HEADS, GLA_DV, GLA_DK), F32)],
        compiler_params=_cp(1))(u, wg, bg, gain, states, d_on, *consts)


def _ffn_fwd(h, gain, w_in, w_out, tag):
    hn, ug, uu, act = _norm_ffn_in(h, gain, w_in, f"{tag}_in")
    if callable(w_out):
        w_out = w_out(act)
    return _out_proj(act, w_out, h, 0.5, f"{tag}_out"), (h, hn, ug, uu, act), w_out


def _ffn_dgrad(dh, w_out, w_in, act_dg, act_du, h, gain, name, split_front=False):
    tp, d = dh.shape
    ff = w_out.shape[0]
    tm = TM_SMALL
    nt = (((1,), (1,)), ((), ()))

    def body(dh_ref, wo_ref, wi_ref, dg_ref, du_ref, h_ref, g_ref, o_ref, *out_refs):
        dhi_ref, dgain_ref = out_refs[-2:]

        @pl.when(pl.program_id(0) == 0)
        def _():
            dgain_ref[...] = jnp.zeros_like(dgain_ref)

        dho = dh_ref[...]
        dact = lax.dot_general((0.5 * dho).astype(BF16), wo_ref[...], nt, preferred_element_type=F32)
        d_gate = (dact * dg_ref[...].astype(F32)).astype(BF16)
        d_up = (dact * du_ref[...].astype(F32)).astype(BF16)
        o_ref[:, :ff] = d_gate
        o_ref[:, ff:] = d_up
        dhn = (lax.dot_general(d_gate, wi_ref[:, :ff], nt, preferred_element_type=F32)
               + lax.dot_general(d_up, wi_ref[:, ff:], nt, preferred_element_type=F32))
        dx, xhat = _rmsnorm_bwd(dhn, h_ref[...], g_ref[...])
        dgain_ref[...] += jnp.sum(dhn * xhat, axis=0, keepdims=True)
        dhi_ref[...] = dho + dx
        if split_front:
            @pl.when(pl.program_id(0) == 0)
            def _():
                out_refs[0][...] = dho + dx

    rows = lambda width: pl.BlockSpec((tm, width), lambda i: (i, 0))
    if split_front:
        assert tm == FRONT
        dhi_specs = [pl.BlockSpec((tm, d), lambda i: (0, 0)), pl.BlockSpec((tm, d), lambda i: (jnp.maximum(i - 1, 0), 0))]
        dhi_shapes = [jax.ShapeDtypeStruct((FRONT, d), F32), jax.ShapeDtypeStruct((tp - FRONT, d), F32)]
    else:
        dhi_specs, dhi_shapes = [rows(d)], [jax.ShapeDtypeStruct((tp, d), F32)]
    out = pl.pallas_call(
        body, name=name, grid=(tp // tm,),
        in_specs=[rows(d), _resident(w_out.shape, 1), _resident(w_in.shape, 1), rows(ff), rows(ff), rows(d),
                  pl.BlockSpec((1, d), lambda i: (0, 0))],
        out_specs=[rows(2 * ff), *dhi_specs, pl.BlockSpec((1, d), lambda i: (0, 0))],
        out_shape=[jax.ShapeDtypeStruct((tp, 2 * ff), BF16), *dhi_shapes, jax.ShapeDtypeStruct((1, d), F32)],
        compiler_params=_cp(1))(dh, w_out, w_in, act_dg, act_du, h, gain)
    return (out[0], tuple(out[1:3]), out[3]) if split_front else tuple(out)


def _ffn_bwd(dh, saved, gain, w_in, w_out, tag, push, split_front=False):
    h, hn, act_dg, act_du, act = saved
    du, dh_in, d_gain = _ffn_dgrad(dh, w_out, w_in, act_dg, act_du, h, gain, f"{tag}_dgrad", split_front)
    d_w_out = _wgrad(act, dh, bm=D_FF // 2, bn=D, scale=0.5, sharded=False, name=f"{tag}_dwout")
    d_w_in = _wgrad(hn, du, bm=D, bn=D_FF, scale=1.0, sharded=False, name=f"{tag}_dwin")
    return dh_in, d_gain, push([("cols", d_w_in), d_w_out])


def _sequence_grads(x, target, p, weights, grads):
    row = lambda v, token: v.reshape(1, -1) + token[0, 0]
    gains = {}

    tok = weights.start(1, weights.start(0, None))
    weights.pin = tok
    h = jnp.concatenate([jnp.zeros((FRONT, D), F32), x], axis=0) + tok[0, 0]
    rope = _rope_tables(h.shape[0])
    w = weights.wait(0, [tok, h, *rope, *weights.later_shards(2)])
    tok = weights.start(2, w["l0_ffn1_in"])
    h = lax.dynamic_update_slice(h, w["meta"], (FRONT - N_META, 0))
    gains["l0_ffn1"] = row(p["norm_ffn1"][0], tok)
    h, s1, w["l0_ffn1_out"] = _ffn_fwd(h, gains["l0_ffn1"], w["l0_ffn1_in"],
                                       lambda act: weights.wait(1, act)["l0_ffn1_out"], "l0_ffn1")
    w.update(weights.wait(2, h))
    tok = weights.start(4, weights.start(3, w["ret_in"]))
    gains["ret"] = row(p["norm_mix"][0], tok)
    hn, u = _norm_proj(h, gains["ret"], w["ret_in"], "ret_in")
    w.update(weights.wait(3, u))
    on, states, h_mix = _ret_fwd(u, w["ret_gain"], rope, h, w["ret_out"], "ret_fwd")
    s2 = (h, hn, u, on, states)
    w.update(weights.wait(4, h_mix))
    tok = weights.start(5, w["l0_ffn2_in"])
    gains["l0_ffn2"] = row(p["norm_ffn2"][0], tok)
    h, s3, _ = _ffn_fwd(h_mix, gains["l0_ffn2"], w["l0_ffn2_in"], w["l0_ffn2_out"], "l0_ffn2")
    saved = [(s1, s2, s3)]

    w.update(weights.wait(5, h))
    tok = weights.start(6, w["l1_ffn1_in"])
    gains["l1_ffn1"] = row(p["norm_ffn1"][1], tok)
    h, s1, _ = _ffn_fwd(h, gains["l1_ffn1"], w["l1_ffn1_in"], w["l1_ffn1_out"], "l1_ffn1")
    w.update(weights.wait(6, h))
    tok = weights.start(7, w["gla_out"])
    gains["gla"] = row(p["norm_mix"][1], tok)
    hn, u = _norm_proj(h, gains["gla"], w["gla_in"], "gla_in")
    on, states = _gla_fwd(u, w["gla_wg"], w["gla_bg"], w["gla_gain"], "gla_fwd")
    h_mix = _out_proj(on, w["gla_out"], h, 1.0, "gla_out")
    s2 = (h, hn, u, on, states)
    w.update(weights.wait(7, h_mix))
    gains["l1_ffn2"] = p["norm_ffn2"][1].reshape(1, -1)
    s3 = (h_mix, *_norm_ffn_in(h_mix, gains["l1_ffn2"], w["l1_ffn2_in"], "l1_ffn2_in"))
    saved.append((s1, s2, s3))

    dh, d_final, loss = _loss_head(s3[-1], w["l1_ffn2_out"], h_mix, p["final_norm"].reshape(1, -1), target,
                                   "l1_ffn2_out_loss")
    small = {"final_norm": d_final, "norm_ffn1": [None, None], "norm_mix": [None, None], "norm_ffn2": [None, None]}
    pusher = lambda k: functools.partial(grads.push, k)

    s1, s2, s3 = saved[1]
    dh, small["norm_ffn2"][1], tok = _ffn_bwd(dh, s3, gains["l1_ffn2"], w["l1_ffn2_in"], w["l1_ffn2_out"], "l1_ffn2",
                                              pusher(0))
    h_in, hn, u, on, states = s2
    d_on, d_out = _out_proj_bwd(dh, w["gla_out"], on, "gla_out_bwd")
    du, small["gla_wg"], small["gla_bg"], small["gla_gain"] = _gla_bwd(
        u, w["gla_wg"], w["gla_bg"], w["gla_gain"] + tok[0, 0], states, d_on, "gla_bwd")
    d_in = _wgrad(hn, du, bm=D, bn=GLA_U, scale=1.0, sharded=False, name="gla_dwin")
    d_in = jnp.moveaxis(d_in[:, :GLA_IN].reshape(D, N_CHIPS, -1), 1, 0)
    tok = grads.push(1, [d_in, d_out])
    dh, small["norm_mix"][1] = _dgrad_norm(du, w["gla_in"], h_in, gains["gla"] + tok[0, 0], dh, "gla_dnorm")
    dh, small["norm_ffn1"][1], tok = _ffn_bwd(dh, s1, gains["l1_ffn1"], w["l1_ffn1_in"], w["l1_ffn1_out"], "l1_ffn1",
                                              pusher(2))

    s1, s2, s3 = saved[0]
    dh, small["norm_ffn2"][0], tok = _ffn_bwd(dh, s3, gains["l0_ffn2"] + tok[0, 0], w["l0_ffn2_in"],
                                              w["l0_ffn2_out"], "l0_ffn2", pusher(3))
    h_in, hn, u, on, states = s2
    d_on, d_out = _out_proj_bwd(dh, w["ret_out"], on, "ret_out_bwd")
    du, small["ret_gain"] = _ret_bwd(u, w["ret_gain"] + tok[0, 0], rope, states, d_on, "ret_bwd")
    d_in = _wgrad(hn, du, bm=D, bn=w["ret_in"].shape[2], scale=1.0, sharded=True, name="ret_dwin")
    tok = grads.push(4, [d_in, d_out])
    dh, small["norm_mix"][0] = _dgrad_norm(du, w["ret_in"], h_in, gains["ret"] + tok[0, 0], dh, "ret_dnorm")
    (d_front, d_x), small["norm_ffn1"][0], tok = _ffn_bwd(dh, s1, gains["l0_ffn1"], w["l0_ffn1_in"], w["l0_ffn1_out"],
                                                          "l0_ffn1", pusher(5), split_front=True)
    grads.push(6, [], [d_front[FRONT - N_META:], *small["norm_ffn1"], *small["norm_mix"], *small["norm_ffn2"],
                       small["final_norm"], small["ret_gain"], small["gla_wg"][:GLA_RANK], small["gla_bg"],
                       small["gla_gain"], loss[:, :1] + tok[0, 0]])
    return d_x


_HBM = pl.BlockSpec(memory_space=pl.ANY)


def _place():
    return lax.axis_index("x"), lax.axis_index("y"), lax.axis_index("c")


def _flip(v, bit):
    return 1 - v if bit else v


DMA_CHUNK_BYTES = 128 * 1024


def _row_chunks(ref):
    rows, cols = ref.shape
    step = _row_tile(rows, max(16, DMA_CHUNK_BYTES // (cols * ref.dtype.itemsize)))
    return [pl.ds(a, step) for a in range(0, rows, step)]


def _whole(src, dst, send_sem, recv_sem, peer):
    return pltpu.make_async_remote_copy(src_ref=src, dst_ref=dst, send_sem=send_sem, recv_sem=recv_sem,
                                        device_id=peer, device_id_type=MESH)


def _send(src, dst, send_sem, recv_sem, peer):
    for rows in _row_chunks(src):
        _whole(src.at[rows], dst.at[rows], send_sem, recv_sem, peer).start()
    return _whole(src, dst, send_sem, recv_sem, peer)


_HBM_ONLY = pl.BlockSpec(memory_space=pltpu.HBM)
_SEMS = pl.BlockSpec(memory_space=pltpu.SEMAPHORE)
_SIDE_EFFECT = pltpu.CompilerParams(has_side_effects=pltpu.SideEffectType.DATAFLOW_SIDE_EFFECTING)
_GATHER_FLIPS = [(1, 0, 0), (0, 1, 0), (1, 1, 0), (0, 0, 1)]
_PEER_FLIPS = [(fx, fy, fc) for fx in (0, 1) for fy in (0, 1) for fc in (0, 1)][1:]


def _zero_token():
    return jnp.zeros((8, 128), F32)


def _exchange_start(srcs, lands, route, flips, after, name):
    n = len(srcs)

    def body(*refs):
        src, land = refs[:n], refs[n:2 * n]
        send_sems, recv_sems, token = refs[2 * n + 1], refs[2 * n + 2], refs[-1]
        me = _place()
        for t in range(n):
            for j, flip in enumerate(flips):
                peer = tuple(_flip(v, f) for v, f in zip(me, flip))
                s, d = route(t, src[t], land[t], me, peer)
                _send(s, d, send_sems.at[t * len(flips) + j], recv_sems.at[t * len(flips) + j], peer)
        token[...] = jnp.zeros_like(token)

    hbm = lambda a: pltpu.HBM(a.shape, a.dtype)
    sems = pltpu.SemaphoreType.DMA((n * len(flips),))
    operands = [pltpu.with_memory_space_constraint(a, pltpu.HBM) for a in list(srcs) + list(lands)]
    out = pl.pallas_call(
        body, name=name, in_specs=[_HBM_ONLY] * (2 * n) + [_HBM],
        out_shape=(sems, sems, *[hbm(a) for a in operands], jax.ShapeDtypeStruct((8, 128), F32)),
        out_specs=(_SEMS, _SEMS, *[_HBM_ONLY] * (2 * n), pl.BlockSpec(memory_space=pltpu.VMEM)),
        input_output_aliases={i: 2 + i for i in range(2 * n)}, compiler_params=_SIDE_EFFECT,
    )(*operands, _zero_token() if after is None else after)
    return (out[0], out[1], out[2:2 + n], out[2 + n:2 + 2 * n]), out[-1]


def _exchange_wait(started, route, flips, after, name):
    send_sems, recv_sems, srcs, lands = started
    n = len(srcs)

    def body(*refs):
        src, land = refs[:n], refs[n:2 * n]
        send_sems, recv_sems = refs[2 * n], refs[2 * n + 1]
        me = _place()
        for t in range(n):
            for j, flip in enumerate(flips):
                peer = tuple(_flip(v, f) for v, f in zip(me, flip))
                s, d = route(t, src[t], land[t], me, peer)
                cp = _whole(s, d, send_sems.at[t * len(flips) + j], recv_sems.at[t * len(flips) + j], peer)
                cp.wait_send()
                cp.wait_recv()

    hbm = lambda a: pltpu.HBM(a.shape, a.dtype)
    after = list(after) if isinstance(after, (list, tuple)) else [after]
    out = pl.pallas_call(
        body, name=name, in_specs=[_HBM_ONLY] * (2 * n) + [_SEMS, _SEMS] + [_HBM] * len(after),
        out_shape=tuple(hbm(a) for a in list(srcs) + list(lands)), out_specs=tuple([_HBM_ONLY] * (2 * n)),
        input_output_aliases={i: i for i in range(2 * n)}, compiler_params=_SIDE_EFFECT,
    )(*srcs, *lands, send_sems, recv_sems, *after)
    return out[:n], out[n:]


def _gather_route(t, src, land, me, peer):
    mine = 2 * me[0] + me[1]
    if land.ndim == 3:
        return src, land.at[mine]
    cols = src.shape[1]
    return src, land.at[:, pl.ds(pl.multiple_of(mine * cols, 128), cols)]


def _scatter_route(n_pieces):
    def route(t, src, land, me, peer):
        chip = 2 * peer[0] + peer[1]
        if t >= n_pieces:
            part = src
        elif src.ndim == 4:
            part = src.at[chip, peer[2]]
        else:
            rows, cols = land.shape[1:]
            part = src.at[pl.ds(pl.multiple_of(peer[2] * rows, 16), rows), pl.ds(pl.multiple_of(chip * cols, 128), cols)]
        return part, land.at[4 * me[0] + 2 * me[1] + me[2]]

    return route


def _swap_cores(halves, name):
    n = len(halves)

    def body(*refs):
        src, dst = refs[:n], refs[n:2 * n]
        send_sems, recv_sems = refs[2 * n:]
        x, y, c = _place()
        copies = [_send(src[t], dst[t], send_sems.at[t], recv_sems.at[t], (x, y, 1 - c)) for t in range(n)]
        for cp in copies:
            cp.wait()

    got = pl.pallas_call(
        body, name=name, in_specs=[_HBM] * n, out_specs=[_HBM] * n,
        out_shape=[jax.ShapeDtypeStruct(a.shape, a.dtype) for a in halves],
        scratch_shapes=[pltpu.SemaphoreType.DMA((n,)), pltpu.SemaphoreType.DMA((n,))],
    )(*halves)
    south = lax.axis_index("c") == 0
    return [jnp.stack([jnp.where(south, a, b), jnp.where(south, b, a)]) for a, b in zip(halves, got)]


def _row_tile(rows, cap):
    fits = [t for t in range(16, cap + 1, 16) if rows % t == 0]
    return fits[-1] if fits else rows


def _sum_slots(a, name):
    _, r, c = a.shape
    tr = _row_tile(r, 384)

    def body(a_ref, o_ref):
        s = a_ref[0].astype(F32)
        for k in range(1, N_DEV):
            s = s + a_ref[k].astype(F32)
        o_ref[...] = s

    return pl.pallas_call(
        body, name=name, grid=(r // tr,),
        in_specs=[pl.BlockSpec((N_DEV, tr, c), lambda i: (0, i, 0))],
        out_specs=pl.BlockSpec((tr, c), lambda i: (i, 0)),
        out_shape=jax.ShapeDtypeStruct((r, c), F32),
        compiler_params=_cp(1))(a)


def _adamw(w, g, m, v, name):
    layers, r, c = w.shape
    tr = _row_tile(r, 256)

    def body(w_ref, g_ref, m_ref, v_ref, d_ref, nm_ref, nv_ref):
        gv = g_ref[...]
        nm = ADAM_B1 * m_ref[...] + (1.0 - ADAM_B1) * gv
        nv = ADAM_B2 * v_ref[...] + (1.0 - ADAM_B2) * (gv * gv)
        m_hat = nm / (1.0 - ADAM_B1 ** ADAM_STEP)
        v_hat = nv / (1.0 - ADAM_B2 ** ADAM_STEP)
        d_ref[...] = -ADAM_LR * (m_hat / (jnp.sqrt(v_hat) + ADAM_EPS) + ADAM_WD * w_ref[...])
        nm_ref[...] = nm
        nv_ref[...] = nv

    spec = pl.BlockSpec((None, tr, c), lambda a, i: (a, i, 0))
    return pl.pallas_call(
        body, name=name, grid=(layers, r // tr), in_specs=[spec] * 4, out_specs=[spec] * 3,
        out_shape=[jax.ShapeDtypeStruct((layers, r, c), F32)] * 3,
        compiler_params=_cp(2))(*[pltpu.with_memory_space_constraint(a, pltpu.HBM) for a in (w, g, m, v)])


_SMALL = ["meta_tokens", "ret_head_norm", "gla_w_gate", "gla_b_gate", "gla_head_norm"]
_LOCAL_SMALL = ["meta_tokens", "norm_ffn1", "norm_mix", "norm_ffn2", "ret_head_norm", "gla_w_gate", "gla_b_gate",
                "gla_head_norm", "final_norm"]
_WEIGHTS = ["meta_tokens", "norm_ffn1", "ffn1_w_in", "ffn1_w_out", "norm_mix", "norm_ffn2", "ffn2_w_in", "ffn2_w_out",
            "ret_w_in", "ret_head_norm", "ret_w_out", "gla_w_in", "gla_w_gate", "gla_b_gate", "gla_head_norm",
            "gla_w_out", "final_norm"]


def _pack_rows(arrays, width):
    flat = jnp.concatenate([a.reshape(-1) for a in arrays])
    pad = -flat.shape[0] % (8 * width)
    return jnp.pad(flat, (0, pad)).reshape(-1, width)


def _unpack_rows(packed, shapes):
    flat, out, at = packed.reshape(-1), [], 0
    for s in shapes:
        size = 1
        for dim in s:
            size *= dim
        out.append(flat[at:at + size].reshape(s))
        at += size
    return out


class _WeightGather:
    GROUPS = [("small", "l0_ffn1_in"), ("l0_ffn1_out",), ("ret_in",), ("ret_out",), ("l0_ffn2_in", "l0_ffn2_out"),
              ("l1_ffn1_in", "l1_ffn1_out"), ("gla_in", "gla_out"), ("l1_ffn2_in", "l1_ffn2_out")]

    def __init__(self, p):
        self.small_shapes = [p[name].shape for name in _SMALL]
        self.f32 = {"small": _pack_rows([p[name] for name in _SMALL], 128), "ret_in": p["ret_w_in"][0],
                    "ret_out": p["ret_w_out"][0], "gla_in": p["gla_w_in"][0], "gla_out": p["gla_w_out"][0]}
        for layer in range(2):
            for name in ("ffn1", "ffn2"):
                self.f32[f"l{layer}_{name}_in"] = p[f"{name}_w_in"][layer]
                self.f32[f"l{layer}_{name}_out"] = p[f"{name}_w_out"][layer]
        self.shards = {}
        self.started = {}
        self.pin = None

    def shard(self, name):
        if name not in self.shards:
            a = self.f32[name]
            if name != "small":
                a = (a if self.pin is None else a + self.pin[0, 0]).astype(BF16)
            self.shards[name] = a
        return self.shards[name]

    def later_shards(self, k):
        return [self.shard(name) for group in self.GROUPS[k:] for name in group]

    def start(self, k, after):
        shards = [self.shard(name) for name in self.GROUPS[k]]
        lands = []
        for name, s in zip(self.GROUPS[k], shards):
            if "ffn" in name and name.endswith("_in"):
                lands.append(lax.empty((s.shape[0], N_CHIPS * s.shape[1]), s.dtype))
            else:
                lands.append(lax.empty((N_CHIPS,) + s.shape, s.dtype))
        self.started[k], token = _exchange_start(shards, lands, _gather_route, _GATHER_FLIPS, after, f"gather{k}_start")
        return token

    def wait(self, k, after):
        _, got = _exchange_wait(self.started[k], _gather_route, _GATHER_FLIPS, after, f"gather{k}_wait")
        w = {}
        for name, g in zip(self.GROUPS[k], got):
            if name == "small":
                parts = zip(*[_unpack_rows(g[chip], self.small_shapes) for chip in range(N_CHIPS)])
                cat = lambda a: jnp.moveaxis(a, 0, -2).reshape(a.shape[1:-1] + (-1,))
                meta, ret_gain, wg, bg, gla_gain = [cat(jnp.stack(part)) for part in parts]
                w.update(meta=meta, ret_gain=ret_gain.reshape(1, -1), gla_bg=bg.reshape(1, -1),
                         gla_gain=gla_gain.reshape(1, -1),
                         gla_wg=jnp.pad(wg[0], ((0, 128 - GLA_RANK), (0, 0))).astype(BF16))
            elif name == "gla_in":
                full = jnp.moveaxis(g, 0, 1).reshape(D, -1)
                w[name] = jnp.pad(full, ((0, 0), (0, GLA_U - GLA_IN)))[None]
            elif name.endswith("_out"):
                w[name] = g.reshape(-1, g.shape[-1])
            else:
                w[name] = g
        return w


class _GradExchange:
    def __init__(self):
        self.started = []
        self.token = None
        self.small_shapes = None

    def push(self, k, arrays, small=None):
        srcs, lands = [], []
        for a in arrays:
            if isinstance(a, tuple):
                a = a[1]
                piece = (a.shape[0] // 2, a.shape[1] // N_CHIPS)
            else:
                a = a.reshape(N_CHIPS, 2, -1, a.shape[-1])
                piece = a.shape[2:]
            srcs.append(a)
            lands.append(lax.empty((N_DEV,) + piece, a.dtype))
        if small is not None:
            self.small_shapes = [a.shape for a in small]
            srcs.append(_pack_rows(small, D))
            lands.append(lax.empty((N_DEV,) + srcs[-1].shape, F32))
        started, self.token = _exchange_start(srcs, lands, _scatter_route(len(arrays)), _PEER_FLIPS, None,
                                              f"scatter{k}_start")
        self.started.append((started, len(arrays)))
        return self.token

    def collect(self, groups, after=None):
        x, y, c = _place()
        after, sums = self.token if after is None else after, []
        for k in groups:
            started, n_pieces = self.started[k]
            srcs, got = _exchange_wait(started, _scatter_route(n_pieces), _PEER_FLIPS, after, f"scatter{k}_wait")
            own = []
            for t, (a, g) in enumerate(zip(srcs, got)):
                if t >= n_pieces:
                    own.append(a)
                elif a.ndim == 4:
                    own.append(a[2 * x + y, c])
                else:
                    rows, cols = g.shape[1:]
                    own.append(lax.dynamic_slice(a, (c * rows, (2 * x + y) * cols), (rows, cols)))
            got = [lax.dynamic_update_index_in_dim(g, a, 4 * x + 2 * y + c, 0) for g, a in zip(got, own)]
            sums.append([_sum_slots(a, f"sum{k}_{i}") for i, a in enumerate(got)])
            after = sums[-1][0]
        return sums


def kernel(x, meta_tokens, norm_ffn1, ffn1_w_in, ffn1_w_out, norm_mix, norm_ffn2, ffn2_w_in, ffn2_w_out, ret_w_in, ret_head_norm, ret_w_out, gla_w_in, gla_w_gate, gla_b_gate, gla_head_norm, gla_w_out, final_norm, loss_target, m_meta_tokens, m_norm_ffn1, m_ffn1_w_in, m_ffn1_w_out, m_norm_mix, m_norm_ffn2, m_ffn2_w_in, m_ffn2_w_out, m_ret_w_in, m_ret_head_norm, m_ret_w_out, m_gla_w_in, m_gla_w_gate, m_gla_b_gate, m_gla_head_norm, m_gla_w_out, m_final_norm, v_meta_tokens, v_norm_ffn1, v_ffn1_w_in, v_ffn1_w_out, v_norm_mix, v_norm_ffn2, v_ffn2_w_in, v_ffn2_w_out, v_ret_w_in, v_ret_head_norm, v_ret_w_out, v_gla_w_in, v_gla_w_gate, v_gla_b_gate, v_gla_head_norm, v_gla_w_out, v_final_norm):
    p = dict(meta_tokens=meta_tokens, norm_ffn1=norm_ffn1, ffn1_w_in=ffn1_w_in, ffn1_w_out=ffn1_w_out, norm_mix=norm_mix,
             norm_ffn2=norm_ffn2, ffn2_w_in=ffn2_w_in, ffn2_w_out=ffn2_w_out, ret_w_in=ret_w_in,
             ret_head_norm=ret_head_norm, ret_w_out=ret_w_out, gla_w_in=gla_w_in, gla_w_gate=gla_w_gate,
             gla_b_gate=gla_b_gate, gla_head_norm=gla_head_norm, gla_w_out=gla_w_out, final_norm=final_norm)
    m = dict(zip(_WEIGHTS, (m_meta_tokens, m_norm_ffn1, m_ffn1_w_in, m_ffn1_w_out, m_norm_mix, m_norm_ffn2, m_ffn2_w_in,
                            m_ffn2_w_out, m_ret_w_in, m_ret_head_norm, m_ret_w_out, m_gla_w_in, m_gla_w_gate,
                            m_gla_b_gate, m_gla_head_norm, m_gla_w_out, m_final_norm)))
    v = dict(zip(_WEIGHTS, (v_meta_tokens, v_norm_ffn1, v_ffn1_w_in, v_ffn1_w_out, v_norm_mix, v_norm_ffn2, v_ffn2_w_in,
                            v_ffn2_w_out, v_ret_w_in, v_ret_head_norm, v_ret_w_out, v_gla_w_in, v_gla_w_gate,
                            v_gla_b_gate, v_gla_head_norm, v_gla_w_out, v_final_norm)))

    exchange = _GradExchange()
    d_x = _sequence_grads(x[0], loss_target[0], p, _WeightGather(p), exchange)
    names = [("ffn2_w_in", 1), ("ffn2_w_out", 1), ("gla_w_in", 0), ("gla_w_out", 0), ("ffn1_w_in", 1), ("ffn1_w_out", 1),
             ("ffn2_w_in", 0), ("ffn2_w_out", 0), ("ret_w_in", 0), ("ret_w_out", 0), ("ffn1_w_in", 0), ("ffn1_w_out", 0)]
    shard, grads, delta, new_m, new_v = {}, {}, {}, {}, {}

    def swap(sums, keys, name):
        for key, a in zip(keys, _swap_cores(sums, name)):
            shard[key] = a.reshape(-1, a.shape[-1])

    def update(name):
        layers = p[name].shape[0]
        grads[name] = jnp.stack([shard[name, layer] for layer in range(layers)])
        delta[name], new_m[name], new_v[name] = _adamw(p[name], grads[name], m[name], v[name], f"adamw_{name}")

    swap([a for group in exchange.collect(range(5)) for a in group], names[:10], "swap_first")
    for name in ("ffn2_w_in", "ffn2_w_out", "ret_w_in", "ret_w_out", "gla_w_in", "gla_w_out"):
        update(name)
    last, (small_sum,) = exchange.collect([5, 6], after=list(delta.values()))
    swap(last, names[10:], "swap_last")
    for name in ("ffn1_w_in", "ffn1_w_out"):
        update(name)

    chip = 2 * lax.axis_index("x") + lax.axis_index("y")
    cols = lambda a, n: lax.dynamic_slice_in_dim(a, chip * n, n, axis=a.ndim - 1)
    (s_meta, s_n1a, s_n1b, s_nma, s_nmb, s_n2a, s_n2b, s_final, s_ret_gain, s_wg, s_bg, s_gla_gain,
     s_loss) = _unpack_rows(small_sum, exchange.small_shapes)
    grads.update({
        "meta_tokens": cols(s_meta, 256), "norm_ffn1": jnp.concatenate([s_n1a, s_n1b]),
        "norm_mix": jnp.concatenate([s_nma, s_nmb]), "norm_ffn2": jnp.concatenate([s_n2a, s_n2b]),
        "final_norm": s_final.reshape(D),
        "ret_head_norm": cols(s_ret_gain.reshape(1, HEADS, RET_DV), RET_DV // N_CHIPS),
        "gla_w_gate": cols(s_wg, GLA_DK)[None], "gla_b_gate": cols(s_bg, GLA_DK),
        "gla_head_norm": cols(s_gla_gain.reshape(1, HEADS, GLA_DV), GLA_DV // N_CHIPS),
    })
    for name in _LOCAL_SMALL:
        shape = p[name].shape
        as3d = lambda a: a.reshape((1,) * (3 - len(shape)) + shape)
        out = _adamw(as3d(p[name]), as3d(grads[name]), as3d(m[name]), as3d(v[name]), f"adamw_{name}")
        delta[name], new_m[name], new_v[name] = [a.reshape(shape) for a in out]

    return (s_loss.reshape(()), d_x[None], *[grads[n] for n in _WEIGHTS], *[delta[n] for n in _WEIGHTS],
            *[new_m[n] for n in _WEIGHTS], *[new_v[n] for n in _WEIGHTS])
```

```python
import functools

import jax
import numpy as np
import jax.numpy as jnp
from jax import lax
from jax.experimental import pallas as pl
from jax.experimental.pallas import tpu as pltpu

F32, BF16 = jnp.float32, jnp.bfloat16
MESH = pl.DeviceIdType.MESH

D = 1024
N_META = 16
CHUNK = 64
RET_CHUNK = 256
FRONT = 256
D_FF = 2816
EPS = 1e-6
HEADS = 4
RET_DK, RET_DV = 256, 512
GLA_DK, GLA_DV = 128, 256
GLA_RANK = 16
GLA_TAU = 16.0
GLA_IN = 2 * HEADS * GLA_DK + 2 * HEADS * GLA_DV + GLA_RANK
GLA_U = 3328
ROPE_BASE = 10000.0
N_CHIPS = 4
N_DEV = 8

ADAM_LR, ADAM_B1, ADAM_B2, ADAM_EPS, ADAM_WD, ADAM_STEP = 0.001, 0.9, 0.999, 1e-08, 0.01, 10

VMEM_LIMIT_BYTES = 56 * 1024 * 1024
TM = 768
TM_SMALL = 256


TM_RESIDENT = 384
MXU_TILE = 256


def _cp(n_axes):
    return pltpu.CompilerParams(dimension_semantics=("arbitrary",) * n_axes, vmem_limit_bytes=VMEM_LIMIT_BYTES)


def _resident(shape, n_axes):
    zeros = (0,) * len(shape)
    index = (lambda i: zeros) if n_axes == 1 else (lambda i, j: zeros)
    return pl.BlockSpec(shape, index, pipeline_mode=pl.Buffered(1))


def _dg(a, b, ca, cb):
    nb = a.ndim - 2
    dims = (((ca + nb,), (cb + nb,)), (tuple(range(nb)), tuple(range(nb))))
    return lax.dot_general(a.astype(BF16), b.astype(BF16), dims, preferred_element_type=F32)


@jax.custom_vjp
def _nn(a, b):
    return _dg(a, b, 1, 0)


@jax.custom_vjp
def _nt(a, b):
    return _dg(a, b, 1, 1)


@jax.custom_vjp
def _tn(a, b):
    return _dg(a, b, 0, 0)


@jax.custom_vjp
def _nt16(a, b):
    return _dg(a, b, 1, 1)


def _dot_vjp(fn, ca, cb, da, db, operand_dtype=F32):
    def fwd(a, b):
        a, b = a.astype(BF16), b.astype(BF16)
        return _dg(a, b, ca, cb), (a, b)

    def bwd(res, g):
        a, b = res
        g = g.astype(BF16)
        grad = lambda other, dims, g_first: _dg(g, other, *dims) if g_first else _dg(other, g, *dims)
        return grad(b, *da).astype(operand_dtype), grad(a, *db).astype(operand_dtype)

    fn.defvjp(fwd, bwd)


_dot_vjp(_nn, 1, 0, ((1, 1), True), ((0, 0), False))
_dot_vjp(_nt, 1, 1, ((1, 0), True), ((0, 0), True))
_dot_vjp(_tn, 0, 0, ((1, 1), False), ((1, 0), False))
_dot_vjp(_nt16, 1, 1, ((1, 0), True), ((0, 0), True), BF16)


def _split_dot(m, a, parts):
    mb = jnp.broadcast_to(m, a.shape[:-2] + m.shape)
    total, rest = None, a
    for _ in range(parts):
        term = rest.astype(BF16)
        rest = rest - term.astype(F32)
        product = _dg(mb, term, 1, 0)
        total = product if total is None else total + product
    return total


def _make_cum(parts):
    @jax.custom_vjp
    def cum(m, mt, a):
        return _split_dot(m, a, parts)

    cum.defvjp(lambda m, mt, a: (_split_dot(m, a, parts), (m, mt)),
               lambda res, g: (jnp.zeros_like(res[0]), jnp.zeros_like(res[1]), _split_dot(res[1], g, parts)))
    return cum


_cum = _make_cum(3)
_cum16 = _make_cum(2)


def _sigmoid(x):
    return 1.0 / (1.0 + jnp.exp(-x))


def _rms(x):
    return lax.rsqrt(jnp.mean(x * x, axis=-1, keepdims=True) + EPS)


def _rmsnorm_bwd(dy, x, gain):
    r = _rms(x)
    xhat = x * r
    dxh = dy * gain
    return r * (dxh - xhat * jnp.mean(dxh * xhat, axis=-1, keepdims=True)), xhat


def _norm_proj(h, gain, w, name):
    tp, d = h.shape
    s, _, ns = w.shape

    tm = TM_RESIDENT

    def body(h_ref, g_ref, w_ref, hn_ref, u_ref):
        x = h_ref[...]
        a = (x * _rms(x) * g_ref[...]).astype(BF16)
        hn_ref[...] = a
        for k in range(s):
            u_ref[:, ns * k:ns * (k + 1)] = jnp.dot(a, w_ref[k], preferred_element_type=F32).astype(BF16)

    return pl.pallas_call(
        body, name=name, grid=(tp // tm,),
        in_specs=[pl.BlockSpec((tm, d), lambda i: (i, 0)), pl.BlockSpec((1, d), lambda i: (0, 0)), _resident(w.shape, 1)],
        out_specs=[pl.BlockSpec((tm, d), lambda i: (i, 0)), pl.BlockSpec((tm, s * ns), lambda i: (i, 0))],
        out_shape=[jax.ShapeDtypeStruct((tp, d), BF16), jax.ShapeDtypeStruct((tp, s * ns), BF16)],
        compiler_params=_cp(1))(h, gain, w)


def _norm_ffn_in(h, gain, w, name):
    tp, d = h.shape
    ff = w.shape[1] // 2
    tm = TM_RESIDENT
    blocks = [(c, min(c + 6 * MXU_TILE, ff)) for c in range(0, ff, 6 * MXU_TILE)]

    def body(h_ref, g_ref, w_ref, hn_ref, dg_ref, du_ref, act_ref):
        x = h_ref[...]
        a = (x * _rms(x) * g_ref[...]).astype(BF16)
        hn_ref[...] = a
        for c0, c1 in blocks:
            g = jnp.dot(a, w_ref[:, c0:c1], preferred_element_type=F32)
            u = jnp.dot(a, w_ref[:, ff + c0:ff + c1], preferred_element_type=F32)
            sg = _sigmoid(g)
            silu = g * sg
            dg_ref[:, c0:c1] = (u * (sg + silu * (1.0 - sg))).astype(BF16)
            du_ref[:, c0:c1] = silu.astype(BF16)
            act_ref[:, c0:c1] = (silu * u).astype(BF16)

    wide = jax.ShapeDtypeStruct((tp, ff), BF16)
    return pl.pallas_call(
        body, name=name, grid=(tp // tm,),
        in_specs=[pl.BlockSpec((tm, d), lambda i: (i, 0)), pl.BlockSpec((1, d), lambda i: (0, 0)),
                  _resident(w.shape, 1)],
        out_specs=[pl.BlockSpec((tm, d), lambda i: (i, 0))] + [pl.BlockSpec((tm, ff), lambda i: (i, 0))] * 3,
        out_shape=[jax.ShapeDtypeStruct((tp, d), BF16), wide, wide, wide],
        compiler_params=_cp(1))(h, gain, w)


def _out_proj(a, w, h, scale, name):
    tp, k = a.shape
    d = w.shape[1]

    def body(a_ref, w_ref, h_ref, o_ref):
        o_ref[...] = h_ref[...] + scale * jnp.dot(a_ref[...], w_ref[...], preferred_element_type=F32)

    return pl.pallas_call(
        body, name=name, grid=(tp // TM,),
        in_specs=[pl.BlockSpec((TM, k), lambda i: (i, 0)), pl.BlockSpec((k, d), lambda i: (0, 0)),
                  pl.BlockSpec((TM, d), lambda i: (i, 0))],
        out_specs=pl.BlockSpec((TM, d), lambda i: (i, 0)),
        out_shape=jax.ShapeDtypeStruct((tp, d), F32),
        compiler_params=_cp(1))(a, w, h)


def _out_proj_bwd(dh, w, on, name):
    tp, d = dh.shape
    k = w.shape[0]
    steps = tp // TM

    def body(dh_ref, w_ref, on_ref, don_ref, dw_ref, acc_ref):
        i = pl.program_id(0)

        @pl.when(i == 0)
        def _():
            acc_ref[...] = jnp.zeros_like(acc_ref)

        g = dh_ref[...].astype(BF16)
        don_ref[...] = lax.dot_general(g, w_ref[...], (((1,), (1,)), ((), ())), preferred_element_type=F32).astype(BF16)
        acc_ref[...] += lax.dot_general(on_ref[...], g, (((0,), (0,)), ((), ())), preferred_element_type=F32)

        @pl.when(i == steps - 1)
        def _():
            dw_ref[...] = acc_ref[...].astype(BF16)

    return pl.pallas_call(
        body, name=name, grid=(steps,),
        in_specs=[pl.BlockSpec((TM, d), lambda i: (i, 0)), _resident(w.shape, 1), pl.BlockSpec((TM, k), lambda i: (i, 0))],
        out_specs=[pl.BlockSpec((TM, k), lambda i: (i, 0)), pl.BlockSpec((k, d), lambda i: (0, 0))],
        out_shape=[jax.ShapeDtypeStruct((tp, k), BF16), jax.ShapeDtypeStruct((k, d), BF16)],
        scratch_shapes=[pltpu.VMEM((k, d), F32)],
        compiler_params=_cp(1))(dh, w, on)


def _wgrad(a, b, *, bm, bn, scale, sharded, name):
    tp, m = a.shape
    n = b.shape[1]
    nk = tp // TM

    def body(a_ref, b_ref, o_ref, acc_ref):
        k = pl.program_id(2)

        @pl.when(k == 0)
        def _():
            acc_ref[...] = jnp.zeros_like(acc_ref)

        bb = b_ref[...]
        if scale != 1.0:
            bb = scale * bb
        acc_ref[...] += lax.dot_general(a_ref[...], bb.astype(BF16), (((0,), (0,)), ((), ())),
                                        preferred_element_type=F32)

        @pl.when(k == nk - 1)
        def _():
            o_ref[...] = acc_ref[...].astype(BF16)

    if sharded:
        assert m == bm
        out_spec = pl.BlockSpec((None, bm, bn), lambda i, j, k: (j, 0, 0))
        out_shape = jax.ShapeDtypeStruct((n // bn, m, bn), BF16)
    else:
        out_spec = pl.BlockSpec((bm, bn), lambda i, j, k: (i, j))
        out_shape = jax.ShapeDtypeStruct((m, n), BF16)
    return pl.pallas_call(
        body, name=name, grid=(m // bm, n // bn, nk),
        in_specs=[pl.BlockSpec((TM, bm), lambda i, j, k: (k, i)), pl.BlockSpec((TM, bn), lambda i, j, k: (k, j))],
        out_specs=out_spec, out_shape=out_shape,
        scratch_shapes=[pltpu.VMEM((bm, bn), F32)],
        compiler_params=_cp(3))(a, b)


def _dgrad_norm(du, w, h, gain, dh_out, name):
    tp, d = h.shape
    s, _, ns = w.shape
    tm = TM_RESIDENT

    def body(du_ref, w_ref, h_ref, g_ref, dho_ref, dhi_ref, dg_ref):
        @pl.when(pl.program_id(0) == 0)
        def _():
            dg_ref[...] = jnp.zeros_like(dg_ref)

        dhn = None
        for k in range(s):
            part = lax.dot_general(du_ref[:, ns * k:ns * (k + 1)], w_ref[k], (((1,), (1,)), ((), ())),
                                   preferred_element_type=F32)
            dhn = part if dhn is None else dhn + part
        dx, xhat = _rmsnorm_bwd(dhn, h_ref[...], g_ref[...])
        dg_ref[...] += jnp.sum(dhn * xhat, axis=0, keepdims=True)
        dhi_ref[...] = dho_ref[...] + dx

    return pl.pallas_call(
        body, name=name, grid=(tp // tm,),
        in_specs=[pl.BlockSpec((tm, s * ns), lambda i: (i, 0)), _resident(w.shape, 1),
                  pl.BlockSpec((tm, d), lambda i: (i, 0)), pl.BlockSpec((1, d), lambda i: (0, 0)),
                  pl.BlockSpec((tm, d), lambda i: (i, 0))],
        out_specs=[pl.BlockSpec((tm, d), lambda i: (i, 0)), pl.BlockSpec((1, d), lambda i: (0, 0))],
        out_shape=[jax.ShapeDtypeStruct((tp, d), F32), jax.ShapeDtypeStruct((1, d), F32)],
        compiler_params=_cp(1))(du, w, h, gain, dh_out)


def _loss_head(act, w_out, h, gain, target, name):
    tp, d = h.shape
    ff = act.shape[1]
    tm = TM_SMALL
    front_tiles = FRONT // tm

    def body(a_ref, w_ref, h_ref, g_ref, t_ref, dh_ref, dg_ref, loss_ref):
        i = pl.program_id(0)

        @pl.when(i == 0)
        def _():
            dg_ref[...] = jnp.zeros_like(dg_ref)
            loss_ref[...] = jnp.zeros_like(loss_ref)

        x = h_ref[...] + 0.5 * jnp.dot(a_ref[...], w_ref[...], preferred_element_type=F32)
        gain_v = g_ref[...]
        y = x * _rms(x) * gain_v
        err = jnp.where(i >= front_tiles, y - t_ref[...], 0.0)
        loss_ref[...] += 0.5 * jnp.sum(jnp.mean(err * err, axis=-1, keepdims=True), axis=0, keepdims=True)
        dy = err * (1.0 / d)
        dx, xhat = _rmsnorm_bwd(dy, x, gain_v)
        dg_ref[...] += jnp.sum(dy * xhat, axis=0, keepdims=True)
        dh_ref[...] = dx

    return pl.pallas_call(
        body, name=name, grid=(tp // tm,),
        in_specs=[pl.BlockSpec((tm, ff), lambda i: (i, 0)), _resident(w_out.shape, 1),
                  pl.BlockSpec((tm, d), lambda i: (i, 0)), pl.BlockSpec((1, d), lambda i: (0, 0)),
                  pl.BlockSpec((tm, d), lambda i: (jnp.maximum(i - front_tiles, 0), 0))],
        out_specs=[pl.BlockSpec((tm, d), lambda i: (i, 0)), pl.BlockSpec((1, d), lambda i: (0, 0)),
                   pl.BlockSpec((1, 128), lambda i: (0, 0))],
        out_shape=[jax.ShapeDtypeStruct((tp, d), F32), jax.ShapeDtypeStruct((1, d), F32),
                   jax.ShapeDtypeStruct((1, 128), F32)],
        compiler_params=_cp(1))(act, w_out, h, gain, target)


def _gated_headnorm(o, g, gain):
    return o * _rms(o) * gain * (g * _sigmoid(g))


def _row_mask(chunk, size=CHUNK):
    rows = chunk * size + lax.broadcasted_iota(jnp.int32, (size, 1), 0)
    return (rows >= FRONT - N_META).astype(F32)


def _ret_head(q1, q2, k1, k2, v, g, state, gain, cos, sin, dmat, dq, dk, dc):
    q = jnp.concatenate([q1 * cos - q2 * sin, q1 * sin + q2 * cos], axis=-1)
    k = jnp.concatenate([k1 * cos - k2 * sin, k1 * sin + k2 * cos], axis=-1) * (RET_DK ** -0.5)
    scores = _nt(q, k) * dmat
    o = _nn(scores, v) + _nn(q * dq, state)
    new_state = state * dc + _tn(k * dk, v)
    return _gated_headnorm(o, g, gain), new_state


def _ret_consts():
    log_gamma = jnp.log1p(-2.0 ** (-5.0 - jnp.arange(HEADS, dtype=F32)))
    idx = jnp.arange(RET_CHUNK, dtype=F32)
    rel = idx[:, None] - idx[None, :]
    dmat = jnp.where(rel >= 0, jnp.exp(log_gamma[:, None, None] * jnp.maximum(rel, 0.0)), 0.0)
    dq = jnp.exp(log_gamma[:, None] * (idx + 1.0))[..., None]
    dk = jnp.exp(log_gamma[:, None] * (RET_CHUNK - 1.0 - idx))[..., None]
    dc = jnp.broadcast_to(jnp.exp(log_gamma * RET_CHUNK)[:, None, None], (HEADS, 1, 128))
    return dmat, dq, dk, dc


def _rope_tables(tp):
    half = RET_DK // 2
    inv = 1.0 / (ROPE_BASE ** jnp.linspace(0.0, 1.0, half, dtype=F32))
    pos = (jnp.arange(tp) - (FRONT - N_META)).astype(F32)
    ang = pos[:, None] * inv[None, :]
    return jnp.cos(ang), jnp.sin(ang)


_RET_V0, _RET_G0 = 2 * D, 4 * D


def _heads(ref, start, width, stride=None, rows=slice(None)):
    stride = width if stride is None else stride
    return jnp.stack([ref[rows, start + stride * h:start + stride * h + width].astype(F32) for h in range(HEADS)])


def _put_heads(ref, start, value, mask, stride=None, rows=slice(None)):
    width = value.shape[-1]
    stride = width if stride is None else stride
    for h in range(HEADS):
        ref[rows, start + stride * h:start + stride * h + width] = (value[h] * mask).astype(ref.dtype)


def _ret_pieces(u_ref):
    hk = RET_DK // 2
    return (_heads(u_ref, 0, hk, RET_DK), _heads(u_ref, hk, hk, RET_DK), _heads(u_ref, D, hk, RET_DK),
            _heads(u_ref, D + hk, hk, RET_DK), _heads(u_ref, _RET_V0, RET_DV), _heads(u_ref, _RET_G0, RET_DV))


def _ret_const_specs(rev=None):
    c = (lambda n: (rev(n), 0)) if rev else (lambda n: (n, 0))
    z3 = lambda n: (0, 0, 0)
    return [pl.BlockSpec((RET_CHUNK, RET_DK // 2), c), pl.BlockSpec((RET_CHUNK, RET_DK // 2), c),
            pl.BlockSpec((HEADS, RET_CHUNK, RET_CHUNK), z3), pl.BlockSpec((HEADS, RET_CHUNK, 1), z3),
            pl.BlockSpec((HEADS, RET_CHUNK, 1), z3), pl.BlockSpec((HEADS, 1, 128), z3)]


def _ret_fwd(u, gain, rope, h, w_out, name):
    tp = u.shape[0]
    nch = tp // RET_CHUNK
    cos, sin = rope
    dmat, dq, dk, dc = _ret_consts()

    def body(u_ref, gain_ref, h_ref, w_ref, cos_ref, sin_ref, dmat_ref, dq_ref, dk_ref, dc_ref,
             on_ref, st_ref, hmix_ref, state_ref):
        @pl.when(pl.program_id(0) == 0)
        def _():
            state_ref[...] = jnp.zeros_like(state_ref)

        state = state_ref[...]
        st_ref[...] = state.astype(BF16)
        on, new_state = _ret_head(*_ret_pieces(u_ref), state, _heads(gain_ref, 0, RET_DV), cos_ref[...], sin_ref[...],
                                  dmat_ref[...], dq_ref[...], dk_ref[...], dc_ref[...][:, :, :1])
        state_ref[...] = new_state
        _put_heads(on_ref, 0, on, 1.0)
        hmix_ref[...] = h_ref[...] + jnp.dot(on_ref[...], w_ref[...], preferred_element_type=F32)

    rows = lambda width: pl.BlockSpec((RET_CHUNK, width), lambda n: (n, 0))
    return pl.pallas_call(
        body, name=name, grid=(nch,),
        in_specs=[rows(6 * D), pl.BlockSpec((1, HEADS * RET_DV), lambda n: (0, 0)), rows(D),
                  _resident(w_out.shape, 1)] + _ret_const_specs(),
        out_specs=[rows(HEADS * RET_DV), pl.BlockSpec((None, HEADS, RET_DK, RET_DV), lambda n: (n, 0, 0, 0)), rows(D)],
        out_shape=[jax.ShapeDtypeStruct((tp, HEADS * RET_DV), BF16),
                   jax.ShapeDtypeStruct((nch, HEADS, RET_DK, RET_DV), BF16), jax.ShapeDtypeStruct((tp, D), F32)],
        scratch_shapes=[pltpu.VMEM((HEADS, RET_DK, RET_DV), F32)],
        compiler_params=_cp(1))(u, gain, h, w_out, cos, sin, dmat, dq, dk, dc)


def _ret_bwd(u, gain, rope, states, d_on, name):
    tp = u.shape[0]
    nch = tp // RET_CHUNK
    cos, sin = rope
    dmat, dq, dk, dc = _ret_consts()
    rev = lambda n: nch - 1 - n
    hk = RET_DK // 2

    def body(u_ref, gain_ref, st_ref, don_ref, cos_ref, sin_ref, dmat_ref, dq_ref, dk_ref, dc_ref,
             du_ref, dgain_ref, dstate_ref):
        @pl.when(pl.program_id(0) == 0)
        def _():
            dstate_ref[...] = jnp.zeros_like(dstate_ref)
            dgain_ref[...] = jnp.zeros_like(dgain_ref)

        mask = _row_mask(rev(pl.program_id(0)), RET_CHUNK)
        consts = (cos_ref[...], sin_ref[...], dmat_ref[...], dq_ref[...], dk_ref[...], dc_ref[...][:, :, :1])
        _, vjp = jax.vjp(lambda *a: _ret_head(*a, *consts), *_ret_pieces(u_ref), st_ref[...].astype(F32),
                         _heads(gain_ref, 0, RET_DV))
        dq1, dq2, dk1, dk2, dv, dg, dstate, dgain = vjp((_heads(don_ref, 0, RET_DV), dstate_ref[...]))
        dstate_ref[...] = dstate
        for hd in range(HEADS):
            dgain_ref[:, RET_DV * hd:RET_DV * (hd + 1)] += dgain[hd]
        _put_heads(du_ref, 0, dq1, mask, RET_DK)
        _put_heads(du_ref, hk, dq2, mask, RET_DK)
        _put_heads(du_ref, D, dk1, mask, RET_DK)
        _put_heads(du_ref, D + hk, dk2, mask, RET_DK)
        _put_heads(du_ref, _RET_V0, dv, mask)
        _put_heads(du_ref, _RET_G0, dg, mask)

    return pl.pallas_call(
        body, name=name, grid=(nch,),
        in_specs=[pl.BlockSpec((RET_CHUNK, 6 * D), lambda n: (rev(n), 0)),
                  pl.BlockSpec((1, HEADS * RET_DV), lambda n: (0, 0)),
                  pl.BlockSpec((None, HEADS, RET_DK, RET_DV), lambda n: (rev(n), 0, 0, 0)),
                  pl.BlockSpec((RET_CHUNK, HEADS * RET_DV), lambda n: (rev(n), 0))] + _ret_const_specs(rev),
        out_specs=[pl.BlockSpec((RET_CHUNK, 6 * D), lambda n: (rev(n), 0)),
                   pl.BlockSpec((1, HEADS * RET_DV), lambda n: (0, 0))],
        out_shape=[jax.ShapeDtypeStruct((tp, 6 * D), BF16), jax.ShapeDtypeStruct((1, HEADS * RET_DV), F32)],
        scratch_shapes=[pltpu.VMEM((HEADS, RET_DK, RET_DV), F32)],
        compiler_params=_cp(1))(u, gain, states, d_on, cos, sin, dmat, dq, dk, dc)


_GLA_K0, _GLA_V0, _GLA_G0, _GLA_Z0 = 512, 1024, 2048, 3072


def _gla_head(q, k, v, g, z, state_t, wg, bg, gain, mask, lo, lo_t, to_mid, to_mid_t, in_second, pair):
    ga = _nn(jnp.broadcast_to(z, wg.shape[:-2] + z.shape), wg) + bg
    log_a = (jnp.minimum(ga, 0.0) - jnp.log(1.0 + jnp.exp(-jnp.abs(ga)))) * (mask * (1.0 / GLA_TAU))
    bcum = _cum(lo, lo_t, log_a)
    btot = jnp.sum(log_a, axis=-2, keepdims=True)
    qs = q * (GLA_DK ** -0.5)
    heads, levels = q.shape[0], pair.shape[0]
    decay = jnp.exp(_cum16(to_mid, to_mid_t, log_a).reshape(heads, levels, CHUNK, GLA_DK))
    qk = jnp.where(in_second > 0.0, qs.astype(BF16)[:, None], k.astype(BF16)[:, None]) * decay.astype(BF16)
    qk = qk.reshape(heads * levels, CHUNK, GLA_DK)
    rows = lax.broadcasted_iota(jnp.int32, (CHUNK, CHUNK), 0)
    cols = lax.broadcasted_iota(jnp.int32, (CHUNK, CHUNK), 1)
    scores = (jnp.where(rows == cols, _nt(qs, k), 0.0)
              + jnp.sum(_nt16(qk, qk).reshape(heads, levels, CHUNK, CHUNK) * pair, axis=1))
    o = _nn(scores, v) + _nt(qs * jnp.exp(bcum), state_t)
    new_state_t = state_t * jnp.exp(btot) + _tn(v, k * jnp.exp(btot - bcum))
    return _gated_headnorm(o, g, gain), new_state_t


def _gla_consts():
    r, c = np.meshgrid(np.arange(CHUNK), np.arange(CHUNK), indexing="ij")
    to_mid, second, pair = [], [], []
    block = 2
    while block <= CHUNK:
        mid = (r // block) * block + block // 2
        to_mid.append(((r >= mid) & (c > mid) & (c <= r)) | ((r < mid) & (c > r) & (c <= mid)))
        second.append((r >= mid)[:, :1])
        pair.append((r // block == c // block) & (r >= mid) & (c < mid))
        block *= 2
    to_mid = np.concatenate(to_mid)
    bf = lambda m: jnp.asarray(m, F32).astype(BF16)
    f32 = lambda ms: jnp.asarray(np.stack(ms), F32)
    return bf(r >= c), bf(c >= r), bf(to_mid), bf(to_mid.T), f32(second), f32(pair)


def _gla_const_specs(consts):
    return [pl.BlockSpec(a.shape, functools.partial(lambda nd, n: (0,) * nd, a.ndim)) for a in consts]


GLA_STEP_CHUNKS = 4


def _gla_pieces(u_ref, rows):
    return (_heads(u_ref, 0, GLA_DK, rows=rows), _heads(u_ref, _GLA_K0, GLA_DK, rows=rows),
            _heads(u_ref, _GLA_V0, GLA_DV, rows=rows), _heads(u_ref, _GLA_G0, GLA_DV, rows=rows),
            u_ref[rows, _GLA_Z0:_GLA_Z0 + 128].astype(F32))


def _gla_fwd(u, wg, bg, gain, name):
    tp = u.shape[0]
    nch = tp // CHUNK
    per = GLA_STEP_CHUNKS
    consts = _gla_consts()

    def body(u_ref, wg_ref, bg_ref, gain_ref, *refs):
        const_refs, (on_ref, st_ref, state_ref) = refs[:len(consts)], refs[len(consts):]

        @pl.when(pl.program_id(0) == 0)
        def _():
            state_ref[...] = jnp.zeros_like(state_ref)

        params = (_heads(wg_ref, 0, GLA_DK), _heads(bg_ref, 0, GLA_DK), _heads(gain_ref, 0, GLA_DV))
        mats = [ref[...] for ref in const_refs]
        state = state_ref[...]
        for c in range(per):
            rows = slice(CHUNK * c, CHUNK * (c + 1))
            st_ref[c] = state.astype(BF16)
            on, state = _gla_head(*_gla_pieces(u_ref, rows), state, *params, _row_mask(pl.program_id(0) * per + c), *mats)
            _put_heads(on_ref, 0, on, 1.0, rows=rows)
        state_ref[...] = state

    rows_spec = lambda width: pl.BlockSpec((per * CHUNK, width), lambda n: (n, 0))
    full = lambda r, c: pl.BlockSpec((r, c), lambda n: (0, 0))
    return pl.pallas_call(
        body, name=name, grid=(nch // per,),
        in_specs=[rows_spec(GLA_U), full(128, HEADS * GLA_DK), full(1, HEADS * GLA_DK), full(1, HEADS * GLA_DV)]
                 + _gla_const_specs(consts),
        out_specs=[rows_spec(HEADS * GLA_DV), pl.BlockSpec((per, HEADS, GLA_DV, GLA_DK), lambda n: (n, 0, 0, 0))],
        out_shape=[jax.ShapeDtypeStruct((tp, HEADS * GLA_DV), BF16),
                   jax.ShapeDtypeStruct((nch, HEADS, GLA_DV, GLA_DK), BF16)],
        scratch_shapes=[pltpu.VMEM((HEADS, GLA_DV, GLA_DK), F32)],
        compiler_params=_cp(1))(u, wg, bg, gain, *consts)


def _gla_bwd(u, wg, bg, gain, states, d_on, name):
    tp = u.shape[0]
    per = GLA_STEP_CHUNKS
    steps = tp // (per * CHUNK)
    rev = lambda n: steps - 1 - n
    consts = _gla_consts()

    def body(u_ref, wg_ref, bg_ref, gain_ref, st_ref, don_ref, *refs):
        const_refs, (du_ref, dwg_ref, dbg_ref, dgain_ref, dstate_ref) = refs[:len(consts)], refs[len(consts):]

        @pl.when(pl.program_id(0) == 0)
        def _():
            dstate_ref[...] = jnp.zeros_like(dstate_ref)
            dwg_ref[...] = jnp.zeros_like(dwg_ref)
            dbg_ref[...] = jnp.zeros_like(dbg_ref)
            dgain_ref[...] = jnp.zeros_like(dgain_ref)

        params = (_heads(wg_ref, 0, GLA_DK), _heads(bg_ref, 0, GLA_DK), _heads(gain_ref, 0, GLA_DV))
        mats = [ref[...] for ref in const_refs]
        dstate = dstate_ref[...]
        for c in reversed(range(per)):
            rows = slice(CHUNK * c, CHUNK * (c + 1))
            mask = _row_mask(rev(pl.program_id(0)) * per + c)
            _, vjp = jax.vjp(lambda *a: _gla_head(*a, mask, *mats), *_gla_pieces(u_ref, rows),
                             st_ref[c].astype(F32), *params)
            dq, dk, dv, dg, dz, dstate, dwg, dbg, dgain = vjp((_heads(don_ref, 0, GLA_DV, rows=rows), dstate))
            for hd in range(HEADS):
                dwg_ref[:, GLA_DK * hd:GLA_DK * (hd + 1)] += dwg[hd]
                dbg_ref[:, GLA_DK * hd:GLA_DK * (hd + 1)] += dbg[hd]
                dgain_ref[:, GLA_DV * hd:GLA_DV * (hd + 1)] += dgain[hd]
            _put_heads(du_ref, 0, dq, mask, rows=rows)
            _put_heads(du_ref, _GLA_K0, dk, mask, rows=rows)
            _put_heads(du_ref, _GLA_V0, dv, mask, rows=rows)
            _put_heads(du_ref, _GLA_G0, dg, mask, rows=rows)
            du_ref[rows, _GLA_Z0:_GLA_Z0 + 128] = dz.astype(BF16)
            du_ref[rows, _GLA_Z0 + 128:] = jnp.zeros((CHUNK, GLA_U - _GLA_Z0 - 128), BF16)
        dstate_ref[...] = dstate

    full = lambda r, c: pl.BlockSpec((r, c), lambda n: (0, 0))
    return pl.pallas_call(
        body, name=name, grid=(steps,),
        in_specs=[pl.BlockSpec((per * CHUNK, GLA_U), lambda n: (rev(n), 0)), full(128, HEADS * GLA_DK),
                  full(1, HEADS * GLA_DK), full(1, HEADS * GLA_DV),
                  pl.BlockSpec((per, HEADS, GLA_DV, GLA_DK), lambda n: (rev(n), 0, 0, 0)),
                  pl.BlockSpec((per * CHUNK, HEADS * GLA_DV), lambda n: (rev(n), 0))] + _gla_const_specs(consts),
        out_specs=[pl.BlockSpec((per * CHUNK, GLA_U), lambda n: (rev(n), 0)), full(128, HEADS * GLA_DK),
                   full(1, HEADS * GLA_DK), full(1, HEADS * GLA_DV)],
        out_shape=[jax.ShapeDtypeStruct((tp, GLA_U), BF16), jax.ShapeDtypeStruct((128, HEADS * GLA_DK), F32),
                   jax.ShapeDtypeStruct((1, HEADS * GLA_DK), F32), jax.ShapeDtypeStruct((1, HEADS * GLA_DV), F32)],
        scratch_shapes=[pltpu.VMEM((HEADS, GLA_DV, GLA_DK), F32)],
        compiler_params=_cp(1))(u, wg, bg, gain, states, d_on, *consts)


def _ffn_fwd(h, gain, w_in, w_out, tag):
    hn, ug, uu, act = _norm_ffn_in(h, gain, w_in, f"{tag}_in")
    if callable(w_out):
        w_out = w_out(act)
    return _out_proj(act, w_out, h, 0.5, f"{tag}_out"), (h, hn, ug, uu, act), w_out


def _ffn_dgrad(dh, w_out, w_in, act_dg, act_du, h, gain, name, split_front=False):
    tp, d = dh.shape
    ff = w_out.shape[0]
    tm = TM_SMALL
    nt = (((1,), (1,)), ((), ()))

    def body(dh_ref, wo_ref, wi_ref, dg_ref, du_ref, h_ref, g_ref, o_ref, *out_refs):
        dhi_ref, dgain_ref = out_refs[-2:]

        @pl.when(pl.program_id(0) == 0)
        def _():
            dgain_ref[...] = jnp.zeros_like(dgain_ref)

        dho = dh_ref[...]
        dact = lax.dot_general((0.5 * dho).astype(BF16), wo_ref[...], nt, preferred_element_type=F32)
        d_gate = (dact * dg_ref[...].astype(F32)).astype(BF16)
        d_up = (dact * du_ref[...].astype(F32)).astype(BF16)
        o_ref[:, :ff] = d_gate
        o_ref[:, ff:] = d_up
        dhn = (lax.dot_general(d_gate, wi_ref[:, :ff], nt, preferred_element_type=F32)
               + lax.dot_general(d_up, wi_ref[:, ff:], nt, preferred_element_type=F32))
        dx, xhat = _rmsnorm_bwd(dhn, h_ref[...], g_ref[...])
        dgain_ref[...] += jnp.sum(dhn * xhat, axis=0, keepdims=True)
        dhi_ref[...] = dho + dx
        if split_front:
            @pl.when(pl.program_id(0) == 0)
            def _():
                out_refs[0][...] = dho + dx

    rows = lambda width: pl.BlockSpec((tm, width), lambda i: (i, 0))
    if split_front:
        assert tm == FRONT
        dhi_specs = [pl.BlockSpec((tm, d), lambda i: (0, 0)), pl.BlockSpec((tm, d), lambda i: (jnp.maximum(i - 1, 0), 0))]
        dhi_shapes = [jax.ShapeDtypeStruct((FRONT, d), F32), jax.ShapeDtypeStruct((tp - FRONT, d), F32)]
    else:
        dhi_specs, dhi_shapes = [rows(d)], [jax.ShapeDtypeStruct((tp, d), F32)]
    out = pl.pallas_call(
        body, name=name, grid=(tp // tm,),
        in_specs=[rows(d), _resident(w_out.shape, 1), _resident(w_in.shape, 1), rows(ff), rows(ff), rows(d),
                  pl.BlockSpec((1, d), lambda i: (0, 0))],
        out_specs=[rows(2 * ff), *dhi_specs, pl.BlockSpec((1, d), lambda i: (0, 0))],
        out_shape=[jax.ShapeDtypeStruct((tp, 2 * ff), BF16), *dhi_shapes, jax.ShapeDtypeStruct((1, d), F32)],
        compiler_params=_cp(1))(dh, w_out, w_in, act_dg, act_du, h, gain)
    return (out[0], tuple(out[1:3]), out[3]) if split_front else tuple(out)


def _ffn_bwd(dh, saved, gain, w_in, w_out, tag, push, split_front=False):
    h, hn, act_dg, act_du, act = saved
    du, dh_in, d_gain = _ffn_dgrad(dh, w_out, w_in, act_dg, act_du, h, gain, f"{tag}_dgrad", split_front)
    d_w_out = _wgrad(act, dh, bm=D_FF // 2, bn=D, scale=0.5, sharded=False, name=f"{tag}_dwout")
    d_w_in = _wgrad(hn, du, bm=D, bn=D_FF, scale=1.0, sharded=False, name=f"{tag}_dwin")
    return dh_in, d_gain, push([("cols", d_w_in), d_w_out])


def _sequence_grads(x, target, p, weights, grads):
    row = lambda v, token: v.reshape(1, -1) + token[0, 0]
    gains = {}

    tok = weights.start(1, weights.start(0, None))
    weights.pin = tok
    h = jnp.concatenate([jnp.zeros((FRONT, D), F32), x], axis=0) + tok[0, 0]
    rope = _rope_tables(h.shape[0])
    w = weights.wait(0, [tok, h, *rope, *weights.later_shards(2)])
    tok = weights.start(2, w["l0_ffn1_in"])
    h = lax.dynamic_update_slice(h, w["meta"], (FRONT - N_META, 0))
    gains["l0_ffn1"] = row(p["norm_ffn1"][0], tok)
    h, s1, w["l0_ffn1_out"] = _ffn_fwd(h, gains["l0_ffn1"], w["l0_ffn1_in"],
                                       lambda act: weights.wait(1, act)["l0_ffn1_out"], "l0_ffn1")
    w.update(weights.wait(2, h))
    tok = weights.start(4, weights.start(3, w["ret_in"]))
    gains["ret"] = row(p["norm_mix"][0], tok)
    hn, u = _norm_proj(h, gains["ret"], w["ret_in"], "ret_in")
    w.update(weights.wait(3, u))
    on, states, h_mix = _ret_fwd(u, w["ret_gain"], rope, h, w["ret_out"], "ret_fwd")
    s2 = (h, hn, u, on, states)
    w.update(weights.wait(4, h_mix))
    tok = weights.start(5, w["l0_ffn2_in"])
    gains["l0_ffn2"] = row(p["norm_ffn2"][0], tok)
    h, s3, _ = _ffn_fwd(h_mix, gains["l0_ffn2"], w["l0_ffn2_in"], w["l0_ffn2_out"], "l0_ffn2")
    saved = [(s1, s2, s3)]

    w.update(weights.wait(5, h))
    tok = weights.start(6, w["l1_ffn1_in"])
    gains["l1_ffn1"] = row(p["norm_ffn1"][1], tok)
    h, s1, _ = _ffn_fwd(h, gains["l1_ffn1"], w["l1_ffn1_in"], w["l1_ffn1_out"], "l1_ffn1")
    w.update(weights.wait(6, h))
    tok = weights.start(7, w["gla_out"])
    gains["gla"] = row(p["norm_mix"][1], tok)
    hn, u = _norm_proj(h, gains["gla"], w["gla_in"], "gla_in")
    on, states = _gla_fwd(u, w["gla_wg"], w["gla_bg"], w["gla_gain"], "gla_fwd")
    h_mix = _out_proj(on, w["gla_out"], h, 1.0, "gla_out")
    s2 = (h, hn, u, on, states)
    w.update(weights.wait(7, h_mix))
    gains["l1_ffn2"] = p["norm_ffn2"][1].reshape(1, -1)
    s3 = (h_mix, *_norm_ffn_in(h_mix, gains["l1_ffn2"], w["l1_ffn2_in"], "l1_ffn2_in"))
    saved.append((s1, s2, s3))

    dh, d_final, loss = _loss_head(s3[-1], w["l1_ffn2_out"], h_mix, p["final_norm"].reshape(1, -1), target,
                                   "l1_ffn2_out_loss")
    small = {"final_norm": d_final, "norm_ffn1": [None, None], "norm_mix": [None, None], "norm_ffn2": [None, None]}
    pusher = lambda k: functools.partial(grads.push, k)

    s1, s2, s3 = saved[1]
    dh, small["norm_ffn2"][1], tok = _ffn_bwd(dh, s3, gains["l1_ffn2"], w["l1_ffn2_in"], w["l1_ffn2_out"], "l1_ffn2",
                                              pusher(0))
    h_in, hn, u, on, states = s2
    d_on, d_out = _out_proj_bwd(dh, w["gla_out"], on, "gla_out_bwd")
    du, small["gla_wg"], small["gla_bg"], small["gla_gain"] = _gla_bwd(
        u, w["gla_wg"], w["gla_bg"], w["gla_gain"] + tok[0, 0], states, d_on, "gla_bwd")
    d_in = _wgrad(hn, du, bm=D, bn=GLA_U, scale=1.0, sharded=False, name="gla_dwin")
    d_in = jnp.moveaxis(d_in[:, :GLA_IN].reshape(D, N_CHIPS, -1), 1, 0)
    tok = grads.push(1, [d_in, d_out])
    dh, small["norm_mix"][1] = _dgrad_norm(du, w["gla_in"], h_in, gains["gla"] + tok[0, 0], dh, "gla_dnorm")
    dh, small["norm_ffn1"][1], tok = _ffn_bwd(dh, s1, gains["l1_ffn1"], w["l1_ffn1_in"], w["l1_ffn1_out"], "l1_ffn1",
                                              pusher(2))

    s1, s2, s3 = saved[0]
    dh, small["norm_ffn2"][0], tok = _ffn_bwd(dh, s3, gains["l0_ffn2"] + tok[0, 0], w["l0_ffn2_in"],
                                              w["l0_ffn2_out"], "l0_ffn2", pusher(3))
    h_in, hn, u, on, states = s2
    d_on, d_out = _out_proj_bwd(dh, w["ret_out"], on, "ret_out_bwd")
    du, small["ret_gain"] = _ret_bwd(u, w["ret_gain"] + tok[0, 0], rope, states, d_on, "ret_bwd")
    d_in = _wgrad(hn, du, bm=D, bn=w["ret_in"].shape[2], scale=1.0, sharded=True, name="ret_dwin")
    tok = grads.push(4, [d_in, d_out])
    dh, small["norm_mix"][0] = _dgrad_norm(du, w["ret_in"], h_in, gains["ret"] + tok[0, 0], dh, "ret_dnorm")
    (d_front, d_x), small["norm_ffn1"][0], tok = _ffn_bwd(dh, s1, gains["l0_ffn1"], w["l0_ffn1_in"], w["l0_ffn1_out"],
                                                          "l0_ffn1", pusher(5), split_front=True)
    grads.push(6, [], [d_front[FRONT - N_META:], *small["norm_ffn1"], *small["norm_mix"], *small["norm_ffn2"],
                       small["final_norm"], small["ret_gain"], small["gla_wg"][:GLA_RANK], small["gla_bg"],
                       small["gla_gain"], loss[:, :1] + tok[0, 0]])
    return d_x


_HBM = pl.BlockSpec(memory_space=pl.ANY)


def _place():
    return lax.axis_index("x"), lax.axis_index("y"), lax.axis_index("c")


def _flip(v, bit):
    return 1 - v if bit else v


DMA_CHUNK_BYTES = 128 * 1024


def _row_chunks(ref):
    rows, cols = ref.shape
    step = _row_tile(rows, max(16, DMA_CHUNK_BYTES // (cols * ref.dtype.itemsize)))
    return [pl.ds(a, step) for a in range(0, rows, step)]


def _whole(src, dst, send_sem, recv_sem, peer):
    return pltpu.make_async_remote_copy(src_ref=src, dst_ref=dst, send_sem=send_sem, recv_sem=recv_sem,
                                        device_id=peer, device_id_type=MESH)


def _send(src, dst, send_sem, recv_sem, peer):
    for rows in _row_chunks(src):
        _whole(src.at[rows], dst.at[rows], send_sem, recv_sem, peer).start()
    return _whole(src, dst, send_sem, recv_sem, peer)


_HBM_ONLY = pl.BlockSpec(memory_space=pltpu.HBM)
_SEMS = pl.BlockSpec(memory_space=pltpu.SEMAPHORE)
_SIDE_EFFECT = pltpu.CompilerParams(has_side_effects=pltpu.SideEffectType.DATAFLOW_SIDE_EFFECTING)
_GATHER_FLIPS = [(1, 0, 0), (0, 1, 0), (1, 1, 0), (0, 0, 1)]
_PEER_FLIPS = [(fx, fy, fc) for fx in (0, 1) for fy in (0, 1) for fc in (0, 1)][1:]


def _zero_token():
    return jnp.zeros((8, 128), F32)


def _exchange_start(srcs, lands, route, flips, after, name):
    n = len(srcs)

    def body(*refs):
        src, land = refs[:n], refs[n:2 * n]
        send_sems, recv_sems, token = refs[2 * n + 1], refs[2 * n + 2], refs[-1]
        me = _place()
        for t in range(n):
            for j, flip in enumerate(flips):
                peer = tuple(_flip(v, f) for v, f in zip(me, flip))
                s, d = route(t, src[t], land[t], me, peer)
                _send(s, d, send_sems.at[t * len(flips) + j], recv_sems.at[t * len(flips) + j], peer)
        token[...] = jnp.zeros_like(token)

    hbm = lambda a: pltpu.HBM(a.shape, a.dtype)
    sems = pltpu.SemaphoreType.DMA((n * len(flips),))
    operands = [pltpu.with_memory_space_constraint(a, pltpu.HBM) for a in list(srcs) + list(lands)]
    out = pl.pallas_call(
        body, name=name, in_specs=[_HBM_ONLY] * (2 * n) + [_HBM],
        out_shape=(sems, sems, *[hbm(a) for a in operands], jax.ShapeDtypeStruct((8, 128), F32)),
        out_specs=(_SEMS, _SEMS, *[_HBM_ONLY] * (2 * n), pl.BlockSpec(memory_space=pltpu.VMEM)),
        input_output_aliases={i: 2 + i for i in range(2 * n)}, compiler_params=_SIDE_EFFECT,
    )(*operands, _zero_token() if after is None else after)
    return (out[0], out[1], out[2:2 + n], out[2 + n:2 + 2 * n]), out[-1]


def _exchange_wait(started, route, flips, after, name):
    send_sems, recv_sems, srcs, lands = started
    n = len(srcs)

    def body(*refs):
        src, land = refs[:n], refs[n:2 * n]
        send_sems, recv_sems = refs[2 * n], refs[2 * n + 1]
        me = _place()
        for t in range(n):
            for j, flip in enumerate(flips):
                peer = tuple(_flip(v, f) for v, f in zip(me, flip))
                s, d = route(t, src[t], land[t], me, peer)
                cp = _whole(s, d, send_sems.at[t * len(flips) + j], recv_sems.at[t * len(flips) + j], peer)
                cp.wait_send()
                cp.wait_recv()

    hbm = lambda a: pltpu.HBM(a.shape, a.dtype)
    after = list(after) if isinstance(after, (list, tuple)) else [after]
    out = pl.pallas_call(
        body, name=name, in_specs=[_HBM_ONLY] * (2 * n) + [_SEMS, _SEMS] + [_HBM] * len(after),
        out_shape=tuple(hbm(a) for a in list(srcs) + list(lands)), out_specs=tuple([_HBM_ONLY] * (2 * n)),
        input_output_aliases={i: i for i in range(2 * n)}, compiler_params=_SIDE_EFFECT,
    )(*srcs, *lands, send_sems, recv_sems, *after)
    return out[:n], out[n:]


def _gather_route(t, src, land, me, peer):
    mine = 2 * me[0] + me[1]
    if land.ndim == 3:
        return src, land.at[mine]
    cols = src.shape[1]
    return src, land.at[:, pl.ds(pl.multiple_of(mine * cols, 128), cols)]


def _scatter_route(n_pieces):
    def route(t, src, land, me, peer):
        chip = 2 * peer[0] + peer[1]
        if t >= n_pieces:
            part = src
        elif src.ndim == 4:
            part = src.at[chip, peer[2]]
        else:
            rows, cols = land.shape[1:]
            part = src.at[pl.ds(pl.multiple_of(peer[2] * rows, 16), rows), pl.ds(pl.multiple_of(chip * cols, 128), cols)]
        return part, land.at[4 * me[0] + 2 * me[1] + me[2]]

    return route


def _swap_cores(halves, name):
    n = len(halves)

    def body(*refs):
        src, dst = refs[:n], refs[n:2 * n]
        send_sems, recv_sems = refs[2 * n:]
        x, y, c = _place()
        copies = [_send(src[t], dst[t], send_sems.at[t], recv_sems.at[t], (x, y, 1 - c)) for t in range(n)]
        for cp in copies:
            cp.wait()

    got = pl.pallas_call(
        body, name=name, in_specs=[_HBM] * n, out_specs=[_HBM] * n,
        out_shape=[jax.ShapeDtypeStruct(a.shape, a.dtype) for a in halves],
        scratch_shapes=[pltpu.SemaphoreType.DMA((n,)), pltpu.SemaphoreType.DMA((n,))],
    )(*halves)
    south = lax.axis_index("c") == 0
    return [jnp.stack([jnp.where(south, a, b), jnp.where(south, b, a)]) for a, b in zip(halves, got)]


def _row_tile(rows, cap):
    fits = [t for t in range(16, cap + 1, 16) if rows % t == 0]
    return fits[-1] if fits else rows


def _sum_slots(a, name):
    _, r, c = a.shape
    tr = _row_tile(r, 384)

    def body(a_ref, o_ref):
        s = a_ref[0].astype(F32)
        for k in range(1, N_DEV):
            s = s + a_ref[k].astype(F32)
        o_ref[...] = s

    return pl.pallas_call(
        body, name=name, grid=(r // tr,),
        in_specs=[pl.BlockSpec((N_DEV, tr, c), lambda i: (0, i, 0))],
        out_specs=pl.BlockSpec((tr, c), lambda i: (i, 0)),
        out_shape=jax.ShapeDtypeStruct((r, c), F32),
        compiler_params=_cp(1))(a)


def _adamw(w, g, m, v, name):
    layers, r, c = w.shape
    tr = _row_tile(r, 256)

    def body(w_ref, g_ref, m_ref, v_ref, d_ref, nm_ref, nv_ref):
        gv = g_ref[...]
        nm = ADAM_B1 * m_ref[...] + (1.0 - ADAM_B1) * gv
        nv = ADAM_B2 * v_ref[...] + (1.0 - ADAM_B2) * (gv * gv)
        m_hat = nm / (1.0 - ADAM_B1 ** ADAM_STEP)
        v_hat = nv / (1.0 - ADAM_B2 ** ADAM_STEP)
        d_ref[...] = -ADAM_LR * (m_hat / (jnp.sqrt(v_hat) + ADAM_EPS) + ADAM_WD * w_ref[...])
        nm_ref[...] = nm
        nv_ref[...] = nv

    spec = pl.BlockSpec((None, tr, c), lambda a, i: (a, i, 0))
    return pl.pallas_call(
        body, name=name, grid=(layers, r // tr), in_specs=[spec] * 4, out_specs=[spec] * 3,
        out_shape=[jax.ShapeDtypeStruct((layers, r, c), F32)] * 3,
        compiler_params=_cp(2))(*[pltpu.with_memory_space_constraint(a, pltpu.HBM) for a in (w, g, m, v)])


_SMALL = ["meta_tokens", "ret_head_norm", "gla_w_gate", "gla_b_gate", "gla_head_norm"]
_LOCAL_SMALL = ["meta_tokens", "norm_ffn1", "norm_mix", "norm_ffn2", "ret_head_norm", "gla_w_gate", "gla_b_gate",
                "gla_head_norm", "final_norm"]
_WEIGHTS = ["meta_tokens", "norm_ffn1", "ffn1_w_in", "ffn1_w_out", "norm_mix", "norm_ffn2", "ffn2_w_in", "ffn2_w_out",
            "ret_w_in", "ret_head_norm", "ret_w_out", "gla_w_in", "gla_w_gate", "gla_b_gate", "gla_head_norm",
            "gla_w_out", "final_norm"]


def _pack_rows(arrays, width):
    flat = jnp.concatenate([a.reshape(-1) for a in arrays])
    pad = -flat.shape[0] % (8 * width)
    return jnp.pad(flat, (0, pad)).reshape(-1, width)


def _unpack_rows(packed, shapes):
    flat, out, at = packed.reshape(-1), [], 0
    for s in shapes:
        size = 1
        for dim in s:
            size *= dim
        out.append(flat[at:at + size].reshape(s))
        at += size
    return out


class _WeightGather:
    GROUPS = [("small", "l0_ffn1_in"), ("l0_ffn1_out",), ("ret_in",), ("ret_out",), ("l0_ffn2_in", "l0_ffn2_out"),
              ("l1_ffn1_in", "l1_ffn1_out"), ("gla_in", "gla_out"), ("l1_ffn2_in", "l1_ffn2_out")]

    def __init__(self, p):
        self.small_shapes = [p[name].shape for name in _SMALL]
        self.f32 = {"small": _pack_rows([p[name] for name in _SMALL], 128), "ret_in": p["ret_w_in"][0],
                    "ret_out": p["ret_w_out"][0], "gla_in": p["gla_w_in"][0], "gla_out": p["gla_w_out"][0]}
        for layer in range(2):
            for name in ("ffn1", "ffn2"):
                self.f32[f"l{layer}_{name}_in"] = p[f"{name}_w_in"][layer]
                self.f32[f"l{layer}_{name}_out"] = p[f"{name}_w_out"][layer]
        self.shards = {}
        self.started = {}
        self.pin = None

    def shard(self, name):
        if name not in self.shards:
            a = self.f32[name]
            if name != "small":
                a = (a if self.pin is None else a + self.pin[0, 0]).astype(BF16)
            self.shards[name] = a
        return self.shards[name]

    def later_shards(self, k):
        return [self.shard(name) for group in self.GROUPS[k:] for name in group]

    def start(self, k, after):
        shards = [self.shard(name) for name in self.GROUPS[k]]
        lands = []
        for name, s in zip(self.GROUPS[k], shards):
            if "ffn" in name and name.endswith("_in"):
                lands.append(lax.empty((s.shape[0], N_CHIPS * s.shape[1]), s.dtype))
            else:
                lands.append(lax.empty((N_CHIPS,) + s.shape, s.dtype))
        self.started[k], token = _exchange_start(shards, lands, _gather_route, _GATHER_FLIPS, after, f"gather{k}_start")
        return token

    def wait(self, k, after):
        _, got = _exchange_wait(self.started[k], _gather_route, _GATHER_FLIPS, after, f"gather{k}_wait")
        w = {}
        for name, g in zip(self.GROUPS[k], got):
            if name == "small":
                parts = zip(*[_unpack_rows(g[chip], self.small_shapes) for chip in range(N_CHIPS)])
                cat = lambda a: jnp.moveaxis(a, 0, -2).reshape(a.shape[1:-1] + (-1,))
                meta, ret_gain, wg, bg, gla_gain = [cat(jnp.stack(part)) for part in parts]
                w.update(meta=meta, ret_gain=ret_gain.reshape(1, -1), gla_bg=bg.reshape(1, -1),
                         gla_gain=gla_gain.reshape(1, -1),
                         gla_wg=jnp.pad(wg[0], ((0, 128 - GLA_RANK), (0, 0))).astype(BF16))
            elif name == "gla_in":
                full = jnp.moveaxis(g, 0, 1).reshape(D, -1)
                w[name] = jnp.pad(full, ((0, 0), (0, GLA_U - GLA_IN)))[None]
            elif name.endswith("_out"):
                w[name] = g.reshape(-1, g.shape[-1])
            else:
                w[name] = g
        return w


class _GradExchange:
    def __init__(self):
        self.started = []
        self.token = None
        self.small_shapes = None

    def push(self, k, arrays, small=None):
        srcs, lands = [], []
        for a in arrays:
            if isinstance(a, tuple):
                a = a[1]
                piece = (a.shape[0] // 2, a.shape[1] // N_CHIPS)
            else:
                a = a.reshape(N_CHIPS, 2, -1, a.shape[-1])
                piece = a.shape[2:]
            srcs.append(a)
            lands.append(lax.empty((N_DEV,) + piece, a.dtype))
        if small is not None:
            self.small_shapes = [a.shape for a in small]
            srcs.append(_pack_rows(small, D))
            lands.append(lax.empty((N_DEV,) + srcs[-1].shape, F32))
        started, self.token = _exchange_start(srcs, lands, _scatter_route(len(arrays)), _PEER_FLIPS, None,
                                              f"scatter{k}_start")
        self.started.append((started, len(arrays)))
        return self.token

    def collect(self, groups, after=None):
        x, y, c = _place()
        after, sums = self.token if after is None else after, []
        for k in groups:
            started, n_pieces = self.started[k]
            srcs, got = _exchange_wait(started, _scatter_route(n_pieces), _PEER_FLIPS, after, f"scatter{k}_wait")
            own = []
            for t, (a, g) in enumerate(zip(srcs, got)):
                if t >= n_pieces:
                    own.append(a)
                elif a.ndim == 4:
                    own.append(a[2 * x + y, c])
                else:
                    rows, cols = g.shape[1:]
                    own.append(lax.dynamic_slice(a, (c * rows, (2 * x + y) * cols), (rows, cols)))
            got = [lax.dynamic_update_index_in_dim(g, a, 4 * x + 2 * y + c, 0) for g, a in zip(got, own)]
            sums.append([_sum_slots(a, f"sum{k}_{i}") for i, a in enumerate(got)])
            after = sums[-1][0]
        return sums


def kernel(x, meta_tokens, norm_ffn1, ffn1_w_in, ffn1_w_out, norm_mix, norm_ffn2, ffn2_w_in, ffn2_w_out, ret_w_in, ret_head_norm, ret_w_out, gla_w_in, gla_w_gate, gla_b_gate, gla_head_norm, gla_w_out, final_norm, loss_target, m_meta_tokens, m_norm_ffn1, m_ffn1_w_in, m_ffn1_w_out, m_norm_mix, m_norm_ffn2, m_ffn2_w_in, m_ffn2_w_out, m_ret_w_in, m_ret_head_norm, m_ret_w_out, m_gla_w_in, m_gla_w_gate, m_gla_b_gate, m_gla_head_norm, m_gla_w_out, m_final_norm, v_meta_tokens, v_norm_ffn1, v_ffn1_w_in, v_ffn1_w_out, v_norm_mix, v_norm_ffn2, v_ffn2_w_in, v_ffn2_w_out, v_ret_w_in, v_ret_head_norm, v_ret_w_out, v_gla_w_in, v_gla_w_gate, v_gla_b_gate, v_gla_head_norm, v_gla_w_out, v_final_norm):
    p = dict(meta_tokens=meta_tokens, norm_ffn1=norm_ffn1, ffn1_w_in=ffn1_w_in, ffn1_w_out=ffn1_w_out, norm_mix=norm_mix,
             norm_ffn2=norm_ffn2, ffn2_w_in=ffn2_w_in, ffn2_w_out=ffn2_w_out, ret_w_in=ret_w_in,
             ret_head_norm=ret_head_norm, ret_w_out=ret_w_out, gla_w_in=gla_w_in, gla_w_gate=gla_w_gate,
             gla_b_gate=gla_b_gate, gla_head_norm=gla_head_norm, gla_w_out=gla_w_out, final_norm=final_norm)
    m = dict(zip(_WEIGHTS, (m_meta_tokens, m_norm_ffn1, m_ffn1_w_in, m_ffn1_w_out, m_norm_mix, m_norm_ffn2, m_ffn2_w_in,
                            m_ffn2_w_out, m_ret_w_in, m_ret_head_norm, m_ret_w_out, m_gla_w_in, m_gla_w_gate,
                            m_gla_b_gate, m_gla_head_norm, m_gla_w_out, m_final_norm)))
    v = dict(zip(_WEIGHTS, (v_meta_tokens, v_norm_ffn1, v_ffn1_w_in, v_ffn1_w_out, v_norm_mix, v_norm_ffn2, v_ffn2_w_in,
                            v_ffn2_w_out, v_ret_w_in, v_ret_head_norm, v_ret_w_out, v_gla_w_in, v_gla_w_gate,
                            v_gla_b_gate, v_gla_head_norm, v_gla_w_out, v_final_norm)))

    exchange = _GradExchange()
    d_x = _sequence_grads(x[0], loss_target[0], p, _WeightGather(p), exchange)
    names = [("ffn2_w_in", 1), ("ffn2_w_out", 1), ("gla_w_in", 0), ("gla_w_out", 0), ("ffn1_w_in", 1), ("ffn1_w_out", 1),
             ("ffn2_w_in", 0), ("ffn2_w_out", 0), ("ret_w_in", 0), ("ret_w_out", 0), ("ffn1_w_in", 0), ("ffn1_w_out", 0)]
    shard, grads, delta, new_m, new_v = {}, {}, {}, {}, {}

    def swap(sums, keys, name):
        for key, a in zip(keys, _swap_cores(sums, name)):
            shard[key] = a.reshape(-1, a.shape[-1])

    def update(name):
        layers = p[name].shape[0]
        grads[name] = jnp.stack([shard[name, layer] for layer in range(layers)])
        delta[name], new_m[name], new_v[name] = _adamw(p[name], grads[name], m[name], v[name], f"adamw_{name}")

    swap([a for group in exchange.collect(range(5)) for a in group], names[:10], "swap_first")
    for name in ("ffn2_w_in", "ffn2_w_out", "ret_w_in", "ret_w_out", "gla_w_in", "gla_w_out"):
        update(name)
    last, (small_sum,) = exchange.collect([5, 6], after=list(delta.values()))
    swap(last, names[10:], "swap_last")
    for name in ("ffn1_w_in", "ffn1_w_out"):
        update(name)

    chip = 2 * lax.axis_index("x") + lax.axis_index("y")
    cols = lambda a, n: lax.dynamic_slice_in_dim(a, chip * n, n, axis=a.ndim - 1)
    (s_meta, s_n1a, s_n1b, s_nma, s_nmb, s_n2a, s_n2b, s_final, s_ret_gain, s_wg, s_bg, s_gla_gain,
     s_loss) = _unpack_rows(small_sum, exchange.small_shapes)
    grads.update({
        "meta_tokens": cols(s_meta, 256), "norm_ffn1": jnp.concatenate([s_n1a, s_n1b]),
        "norm_mix": jnp.concatenate([s_nma, s_nmb]), "norm_ffn2": jnp.concatenate([s_n2a, s_n2b]),
        "final_norm": s_final.reshape(D),
        "ret_head_norm": cols(s_ret_gain.reshape(1, HEADS, RET_DV), RET_DV // N_CHIPS),
        "gla_w_gate": cols(s_wg, GLA_DK)[None], "gla_b_gate": cols(s_bg, GLA_DK),
        "gla_head_norm": cols(s_gla_gain.reshape(1, HEADS, GLA_DV), GLA_DV // N_CHIPS),
    })
    for name in _LOCAL_SMALL:
        shape = p[name].shape
        as3d = lambda a: a.reshape((1,) * (3 - len(shape)) + shape)
        out = _adamw(as3d(p[name]), as3d(grads[name]), as3d(m[name]), as3d(v[name]), f"adamw_{name}")
        delta[name], new_m[name], new_v[name] = [a.reshape(shape) for a in out]

    return (s_loss.reshape(()), d_x[None], *[grads[n] for n in _WEIGHTS], *[delta[n] for n in _WEIGHTS],
            *[new_m[n] for n in _WEIGHTS], *[new_v[n] for n in _WEIGHTS])
```

```python
import functools

import jax
import numpy as np
import jax.numpy as jnp
from jax import lax
from jax.experimental import pallas as pl
from jax.experimental.pallas import tpu as pltpu

F32, BF16 = jnp.float32, jnp.bfloat16
MESH = pl.DeviceIdType.MESH

D = 1024
N_META = 16
CHUNK = 64
RET_CHUNK = 256
FRONT = 256
D_FF = 2816
EPS = 1e-6
HEADS = 4
RET_DK, RET_DV = 256, 512
GLA_DK, GLA_DV = 128, 256
GLA_RANK = 16
GLA_TAU = 16.0
GLA_IN = 2 * HEADS * GLA_DK + 2 * HEADS * GLA_DV + GLA_RANK
GLA_U = 3328
ROPE_BASE = 10000.0
N_CHIPS = 4
N_DEV = 8

ADAM_LR, ADAM_B1, ADAM_B2, ADAM_EPS, ADAM_WD, ADAM_STEP = 0.001, 0.9, 0.999, 1e-08, 0.01, 10

VMEM_LIMIT_BYTES = 56 * 1024 * 1024
TM = 768
TM_SMALL = 256


TM_RESIDENT = 384
MXU_TILE = 256


def _cp(n_axes):
    return pltpu.CompilerParams(dimension_semantics=("arbitrary",) * n_axes, vmem_limit_bytes=VMEM_LIMIT_BYTES)


def _resident(shape, n_axes):
    zeros = (0,) * len(shape)
    index = (lambda i: zeros) if n_axes == 1 else (lambda i, j: zeros)
    return pl.BlockSpec(shape, index, pipeline_mode=pl.Buffered(1))


def _dg(a, b, ca, cb):
    nb = a.ndim - 2
    dims = (((ca + nb,), (cb + nb,)), (tuple(range(nb)), tuple(range(nb))))
    return lax.dot_general(a.astype(BF16), b.astype(BF16), dims, preferred_element_type=F32)


@jax.custom_vjp
def _nn(a, b):
    return _dg(a, b, 1, 0)


@jax.custom_vjp
def _nt(a, b):
    return _dg(a, b, 1, 1)


@jax.custom_vjp
def _tn(a, b):
    return _dg(a, b, 0, 0)


@jax.custom_vjp
def _nt16(a, b):
    return _dg(a, b, 1, 1)


def _dot_vjp(fn, ca, cb, da, db, operand_dtype=F32):
    def fwd(a, b):
        a, b = a.astype(BF16), b.astype(BF16)
        return _dg(a, b, ca, cb), (a, b)

    def bwd(res, g):
        a, b = res
        g = g.astype(BF16)
        grad = lambda other, dims, g_first: _dg(g, other, *dims) if g_first else _dg(other, g, *dims)
        return grad(b, *da).astype(operand_dtype), grad(a, *db).astype(operand_dtype)

    fn.defvjp(fwd, bwd)


_dot_vjp(_nn, 1, 0, ((1, 1), True), ((0, 0), False))
_dot_vjp(_nt, 1, 1, ((1, 0), True), ((0, 0), True))
_dot_vjp(_tn, 0, 0, ((1, 1), False), ((1, 0), False))
_dot_vjp(_nt16, 1, 1, ((1, 0), True), ((0, 0), True), BF16)


def _split_dot(m, a, parts):
    mb = jnp.broadcast_to(m, a.shape[:-2] + m.shape)
    total, rest = None, a
    for _ in range(parts):
        term = rest.astype(BF16)
        rest = rest - term.astype(F32)
        product = _dg(mb, term, 1, 0)
        total = product if total is None else total + product
    return total


def _make_cum(parts):
    @jax.custom_vjp
    def cum(m, mt, a):
        return _split_dot(m, a, parts)

    cum.defvjp(lambda m, mt, a: (_split_dot(m, a, parts), (m, mt)),
               lambda res, g: (jnp.zeros_like(res[0]), jnp.zeros_like(res[1]), _split_dot(res[1], g, parts)))
    return cum


_cum = _make_cum(3)
_cum16 = _make_cum(2)


def _sigmoid(x):
    return 1.0 / (1.0 + jnp.exp(-x))


def _rms(x):
    return lax.rsqrt(jnp.mean(x * x, axis=-1, keepdims=True) + EPS)


def _rmsnorm_bwd(dy, x, gain):
    r = _rms(x)
    xhat = x * r
    dxh = dy * gain
    return r * (dxh - xhat * jnp.mean(dxh * xhat, axis=-1, keepdims=True)), xhat


def _norm_proj(h, gain, w, name):
    tp, d = h.shape
    s, _, ns = w.shape

    tm = TM_RESIDENT

    def body(h_ref, g_ref, w_ref, hn_ref, u_ref):
        x = h_ref[...]
        a = (x * _rms(x) * g_ref[...]).astype(BF16)
        hn_ref[...] = a
        for k in range(s):
            u_ref[:, ns * k:ns * (k + 1)] = jnp.dot(a, w_ref[k], preferred_element_type=F32).astype(BF16)

    return pl.pallas_call(
        body, name=name, grid=(tp // tm,),
        in_specs=[pl.BlockSpec((tm, d), lambda i: (i, 0)), pl.BlockSpec((1, d), lambda i: (0, 0)), _resident(w.shape, 1)],
        out_specs=[pl.BlockSpec((tm, d), lambda i: (i, 0)), pl.BlockSpec((tm, s * ns), lambda i: (i, 0))],
        out_shape=[jax.ShapeDtypeStruct((tp, d), BF16), jax.ShapeDtypeStruct((tp, s * ns), BF16)],
        compiler_params=_cp(1))(h, gain, w)


def _norm_ffn_in(h, gain, w, name):
    tp, d = h.shape
    ff = w.shape[1] // 2
    tm = TM_RESIDENT
    blocks = [(c, min(c + 6 * MXU_TILE, ff)) for c in range(0, ff, 6 * MXU_TILE)]

    def body(h_ref, g_ref, w_ref, hn_ref, dg_ref, du_ref, act_ref):
        x = h_ref[...]
        a = (x * _rms(x) * g_ref[...]).astype(BF16)
        hn_ref[...] = a
        for c0, c1 in blocks:
            g = jnp.dot(a, w_ref[:, c0:c1], preferred_element_type=F32)
            u = jnp.dot(a, w_ref[:, ff + c0:ff + c1], preferred_element_type=F32)
            sg = _sigmoid(g)
            silu = g * sg
            dg_ref[:, c0:c1] = (u * (sg + silu * (1.0 - sg))).astype(BF16)
            du_ref[:, c0:c1] = silu.astype(BF16)
            act_ref[:, c0:c1] = (silu * u).astype(BF16)

    wide = jax.ShapeDtypeStruct((tp, ff), BF16)
    return pl.pallas_call(
        body, name=name, grid=(tp // tm,),
        in_specs=[pl.BlockSpec((tm, d), lambda i: (i, 0)), pl.BlockSpec((1, d), lambda i: (0, 0)),
                  _resident(w.shape, 1)],
        out_specs=[pl.BlockSpec((tm, d), lambda i: (i, 0))] + [pl.BlockSpec((tm, ff), lambda i: (i, 0))] * 3,
        out_shape=[jax.ShapeDtypeStruct((tp, d), BF16), wide, wide, wide],
        compiler_params=_cp(1))(h, gain, w)


def _out_proj(a, w, h, scale, name):
    tp, k = a.shape
    d = w.shape[1]

    def body(a_ref, w_ref, h_ref, o_ref):
        o_ref[...] = h_ref[...] + scale * jnp.dot(a_ref[...], w_ref[...], preferred_element_type=F32)

    return pl.pallas_call(
        body, name=name, grid=(tp // TM,),
        in_specs=[pl.BlockSpec((TM, k), lambda i: (i, 0)), pl.BlockSpec((k, d), lambda i: (0, 0)),
                  pl.BlockSpec((TM, d), lambda i: (i, 0))],
        out_specs=pl.BlockSpec((TM, d), lambda i: (i, 0)),
        out_shape=jax.ShapeDtypeStruct((tp, d), F32),
        compiler_params=_cp(1))(a, w, h)


def _out_proj_bwd(dh, w, on, name):
    tp, d = dh.shape
    k = w.shape[0]
    steps = tp // TM

    def body(dh_ref, w_ref, on_ref, don_ref, dw_ref, acc_ref):
        i = pl.program_id(0)

        @pl.when(i == 0)
        def _():
            acc_ref[...] = jnp.zeros_like(acc_ref)

        g = dh_ref[...].astype(BF16)
        don_ref[...] = lax.dot_general(g, w_ref[...], (((1,), (1,)), ((), ())), preferred_element_type=F32).astype(BF16)
        acc_ref[...] += lax.dot_general(on_ref[...], g, (((0,), (0,)), ((), ())), preferred_element_type=F32)

        @pl.when(i == steps - 1)
        def _():
            dw_ref[...] = acc_ref[...].astype(BF16)

    return pl.pallas_call(
        body, name=name, grid=(steps,),
        in_specs=[pl.BlockSpec((TM, d), lambda i: (i, 0)), _resident(w.shape, 1), pl.BlockSpec((TM, k), lambda i: (i, 0))],
        out_specs=[pl.BlockSpec((TM, k), lambda i: (i, 0)), pl.BlockSpec((k, d), lambda i: (0, 0))],
        out_shape=[jax.ShapeDtypeStruct((tp, k), BF16), jax.ShapeDtypeStruct((k, d), BF16)],
        scratch_shapes=[pltpu.VMEM((k, d), F32)],
        compiler_params=_cp(1))(dh, w, on)


def _wgrad(a, b, *, bm, bn, scale, sharded, name):
    tp, m = a.shape
    n = b.shape[1]
    nk = tp // TM

    def body(a_ref, b_ref, o_ref, acc_ref):
        k = pl.program_id(2)

        @pl.when(k == 0)
        def _():
            acc_ref[...] = jnp.zeros_like(acc_ref)

        bb = b_ref[...]
        if scale != 1.0:
            bb = scale * bb
        acc_ref[...] += lax.dot_general(a_ref[...], bb.astype(BF16), (((0,), (0,)), ((), ())),
                                        preferred_element_type=F32)

        @pl.when(k == nk - 1)
        def _():
            o_ref[...] = acc_ref[...].astype(BF16)

    if sharded:
        assert m == bm
        out_spec = pl.BlockSpec((None, bm, bn), lambda i, j, k: (j, 0, 0))
        out_shape = jax.ShapeDtypeStruct((n // bn, m, bn), BF16)
    else:
        out_spec = pl.BlockSpec((bm, bn), lambda i, j, k: (i, j))
        out_shape = jax.ShapeDtypeStruct((m, n), BF16)
    return pl.pallas_call(
        body, name=name, grid=(m // bm, n // bn, nk),
        in_specs=[pl.BlockSpec((TM, bm), lambda i, j, k: (k, i)), pl.BlockSpec((TM, bn), lambda i, j, k: (k, j))],
        out_specs=out_spec, out_shape=out_shape,
        scratch_shapes=[pltpu.VMEM((bm, bn), F32)],
        compiler_params=_cp(3))(a, b)


def _dgrad_norm(du, w, h, gain, dh_out, name):
    tp, d = h.shape
    s, _, ns = w.shape
    tm = TM_RESIDENT

    def body(du_ref, w_ref, h_ref, g_ref, dho_ref, dhi_ref, dg_ref):
        @pl.when(pl.program_id(0) == 0)
        def _():
            dg_ref[...] = jnp.zeros_like(dg_ref)

        dhn = None
        for k in range(s):
            part = lax.dot_general(du_ref[:, ns * k:ns * (k + 1)], w_ref[k], (((1,), (1,)), ((), ())),
                                   preferred_element_type=F32)
            dhn = part if dhn is None else dhn + part
        dx, xhat = _rmsnorm_bwd(dhn, h_ref[...], g_ref[...])
        dg_ref[...] += jnp.sum(dhn * xhat, axis=0, keepdims=True)
        dhi_ref[...] = dho_ref[...] + dx

    return pl.pallas_call(
        body, name=name, grid=(tp // tm,),
        in_specs=[pl.BlockSpec((tm, s * ns), lambda i: (i, 0)), _resident(w.shape, 1),
                  pl.BlockSpec((tm, d), lambda i: (i, 0)), pl.BlockSpec((1, d), lambda i: (0, 0)),
                  pl.BlockSpec((tm, d), lambda i: (i, 0))],
        out_specs=[pl.BlockSpec((tm, d), lambda i: (i, 0)), pl.BlockSpec((1, d), lambda i: (0, 0))],
        out_shape=[jax.ShapeDtypeStruct((tp, d), F32), jax.ShapeDtypeStruct((1, d), F32)],
        compiler_params=_cp(1))(du, w, h, gain, dh_out)


def _loss_head(act, w_out, h, gain, target, name):
    tp, d = h.shape
    ff = act.shape[1]
    tm = TM_SMALL
    front_tiles = FRONT // tm

    def body(a_ref, w_ref, h_ref, g_ref, t_ref, dh_ref, dg_ref, loss_ref):
        i = pl.program_id(0)

        @pl.when(i == 0)
        def _():
            dg_ref[...] = jnp.zeros_like(dg_ref)
            loss_ref[...] = jnp.zeros_like(loss_ref)

        x = h_ref[...] + 0.5 * jnp.dot(a_ref[...], w_ref[...], preferred_element_type=F32)
        gain_v = g_ref[...]
        y = x * _rms(x) * gain_v
        err = jnp.where(i >= front_tiles, y - t_ref[...], 0.0)
        loss_ref[...] += 0.5 * jnp.sum(jnp.mean(err * err, axis=-1, keepdims=True), axis=0, keepdims=True)
        dy = err * (1.0 / d)
        dx, xhat = _rmsnorm_bwd(dy, x, gain_v)
        dg_ref[...] += jnp.sum(dy * xhat, axis=0, keepdims=True)
        dh_ref[...] = dx

    return pl.pallas_call(
        body, name=name, grid=(tp // tm,),
        in_specs=[pl.BlockSpec((tm, ff), lambda i: (i, 0)), _resident(w_out.shape, 1),
                  pl.BlockSpec((tm, d), lambda i: (i, 0)), pl.BlockSpec((1, d), lambda i: (0, 0)),
                  pl.BlockSpec((tm, d), lambda i: (jnp.maximum(i - front_tiles, 0), 0))],
        out_specs=[pl.BlockSpec((tm, d), lambda i: (i, 0)), pl.BlockSpec((1, d), lambda i: (0, 0)),
                   pl.BlockSpec((1, 128), lambda i: (0, 0))],
        out_shape=[jax.ShapeDtypeStruct((tp, d), F32), jax.ShapeDtypeStruct((1, d), F32),
                   jax.ShapeDtypeStruct((1, 128), F32)],
        compiler_params=_cp(1))(act, w_out, h, gain, target)


@jax.custom_vjp
def _gated_headnorm(o, g, gain):
    return o * _rms(o) * gain * (g * _sigmoid(g))


def _gated_headnorm_bwd(res, dy):
    o, g, gain = res
    r = _rms(o)
    ohat = o * r
    sg = _sigmoid(g)
    silu = g * sg
    dy_ohat = dy * ohat
    d_ohat = dy * gain * silu
    d_o = r * (d_ohat - ohat * jnp.mean(d_ohat * ohat, axis=-1, keepdims=True))
    d_g = dy_ohat * gain * (sg + silu * (1.0 - sg))
    d_gain = jnp.sum(dy_ohat * silu, axis=-2, keepdims=True)
    return d_o, d_g, d_gain


_gated_headnorm.defvjp(lambda o, g, gain: (_gated_headnorm(o, g, gain), (o, g, gain)), _gated_headnorm_bwd)


def _row_mask(chunk, size=CHUNK):
    rows = chunk * size + lax.broadcasted_iota(jnp.int32, (size, 1), 0)
    return (rows >= FRONT - N_META).astype(F32)


def _ret_head(q1, q2, k1, k2, v, g, state, gain, cos, sin, dmat, dq, dk, dc):
    q = jnp.concatenate([q1 * cos - q2 * sin, q1 * sin + q2 * cos], axis=-1)
    k = jnp.concatenate([k1 * cos - k2 * sin, k1 * sin + k2 * cos], axis=-1) * (RET_DK ** -0.5)
    scores = _nt(q, k) * dmat
    o = _nn(scores, v) + _nn(q * dq, state)
    new_state = state * dc + _tn(k * dk, v)
    return _gated_headnorm(o, g, gain), new_state


def _ret_consts():
    log_gamma = jnp.log1p(-2.0 ** (-5.0 - jnp.arange(HEADS, dtype=F32)))
    idx = jnp.arange(RET_CHUNK, dtype=F32)
    rel = idx[:, None] - idx[None, :]
    dmat = jnp.where(rel >= 0, jnp.exp(log_gamma[:, None, None] * jnp.maximum(rel, 0.0)), 0.0)
    dq = jnp.exp(log_gamma[:, None] * (idx + 1.0))[..., None]
    dk = jnp.exp(log_gamma[:, None] * (RET_CHUNK - 1.0 - idx))[..., None]
    dc = jnp.broadcast_to(jnp.exp(log_gamma * RET_CHUNK)[:, None, None], (HEADS, 1, 128))
    return dmat, dq, dk, dc


def _rope_tables(tp):
    half = RET_DK // 2
    inv = 1.0 / (ROPE_BASE ** jnp.linspace(0.0, 1.0, half, dtype=F32))
    pos = (jnp.arange(tp) - (FRONT - N_META)).astype(F32)
    ang = pos[:, None] * inv[None, :]
    return jnp.cos(ang), jnp.sin(ang)


_RET_V0, _RET_G0 = 2 * D, 4 * D


def _heads(ref, start, width, stride=None, rows=slice(None)):
    stride = width if stride is None else stride
    return jnp.stack([ref[rows, start + stride * h:start + stride * h + width].astype(F32) for h in range(HEADS)])


def _put_heads(ref, start, value, mask, stride=None, rows=slice(None)):
    width = value.shape[-1]
    stride = width if stride is None else stride
    for h in range(HEADS):
        ref[rows, start + stride * h:start + stride * h + width] = (value[h] * mask).astype(ref.dtype)


def _ret_pieces(u_ref):
    hk = RET_DK // 2
    return (_heads(u_ref, 0, hk, RET_DK), _heads(u_ref, hk, hk, RET_DK), _heads(u_ref, D, hk, RET_DK),
            _heads(u_ref, D + hk, hk, RET_DK), _heads(u_ref, _RET_V0, RET_DV), _heads(u_ref, _RET_G0, RET_DV))


def _ret_const_specs(rev=None):
    c = (lambda n: (rev(n), 0)) if rev else (lambda n: (n, 0))
    z3 = lambda n: (0, 0, 0)
    return [pl.BlockSpec((RET_CHUNK, RET_DK // 2), c), pl.BlockSpec((RET_CHUNK, RET_DK // 2), c),
            pl.BlockSpec((HEADS, RET_CHUNK, RET_CHUNK), z3), pl.BlockSpec((HEADS, RET_CHUNK, 1), z3),
            pl.BlockSpec((HEADS, RET_CHUNK, 1), z3), pl.BlockSpec((HEADS, 1, 128), z3)]


def _ret_fwd(u, gain, rope, h, w_out, name):
    tp = u.shape[0]
    nch = tp // RET_CHUNK
    cos, sin = rope
    dmat, dq, dk, dc = _ret_consts()

    def body(u_ref, gain_ref, h_ref, w_ref, cos_ref, sin_ref, dmat_ref, dq_ref, dk_ref, dc_ref,
             on_ref, st_ref, hmix_ref, state_ref):
        @pl.when(pl.program_id(0) == 0)
        def _():
            state_ref[...] = jnp.zeros_like(state_ref)

        state = state_ref[...]
        st_ref[...] = state.astype(BF16)
        on, new_state = _ret_head(*_ret_pieces(u_ref), state, _heads(gain_ref, 0, RET_DV), cos_ref[...], sin_ref[...],
                                  dmat_ref[...], dq_ref[...], dk_ref[...], dc_ref[...][:, :, :1])
        state_ref[...] = new_state
        _put_heads(on_ref, 0, on, 1.0)
        hmix_ref[...] = h_ref[...] + jnp.dot(on_ref[...], w_ref[...], preferred_element_type=F32)

    rows = lambda width: pl.BlockSpec((RET_CHUNK, width), lambda n: (n, 0))
    return pl.pallas_call(
        body, name=name, grid=(nch,),
        in_specs=[rows(6 * D), pl.BlockSpec((1, HEADS * RET_DV), lambda n: (0, 0)), rows(D),
                  _resident(w_out.shape, 1)] + _ret_const_specs(),
        out_specs=[rows(HEADS * RET_DV), pl.BlockSpec((None, HEADS, RET_DK, RET_DV), lambda n: (n, 0, 0, 0)), rows(D)],
        out_shape=[jax.ShapeDtypeStruct((tp, HEADS * RET_DV), BF16),
                   jax.ShapeDtypeStruct((nch, HEADS, RET_DK, RET_DV), BF16), jax.ShapeDtypeStruct((tp, D), F32)],
        scratch_shapes=[pltpu.VMEM((HEADS, RET_DK, RET_DV), F32)],
        compiler_params=_cp(1))(u, gain, h, w_out, cos, sin, dmat, dq, dk, dc)


def _ret_bwd(u, gain, rope, states, d_on, name):
    tp = u.shape[0]
    nch = tp // RET_CHUNK
    cos, sin = rope
    dmat, dq, dk, dc = _ret_consts()
    rev = lambda n: nch - 1 - n
    hk = RET_DK // 2

    def body(u_ref, gain_ref, st_ref, don_ref, cos_ref, sin_ref, dmat_ref, dq_ref, dk_ref, dc_ref,
             du_ref, dgain_ref, dstate_ref):
        @pl.when(pl.program_id(0) == 0)
        def _():
            dstate_ref[...] = jnp.zeros_like(dstate_ref)
            dgain_ref[...] = jnp.zeros_like(dgain_ref)

        mask = _row_mask(rev(pl.program_id(0)), RET_CHUNK)
        consts = (cos_ref[...], sin_ref[...], dmat_ref[...], dq_ref[...], dk_ref[...], dc_ref[...][:, :, :1])
        _, vjp = jax.vjp(lambda *a: _ret_head(*a, *consts), *_ret_pieces(u_ref), st_ref[...].astype(F32),
                         _heads(gain_ref, 0, RET_DV))
        dq1, dq2, dk1, dk2, dv, dg, dstate, dgain = vjp((_heads(don_ref, 0, RET_DV), dstate_ref[...]))
        dstate_ref[...] = dstate
        for hd in range(HEADS):
            dgain_ref[:, RET_DV * hd:RET_DV * (hd + 1)] += dgain[hd]
        _put_heads(du_ref, 0, dq1, mask, RET_DK)
        _put_heads(du_ref, hk, dq2, mask, RET_DK)
        _put_heads(du_ref, D, dk1, mask, RET_DK)
        _put_heads(du_ref, D + hk, dk2, mask, RET_DK)
        _put_heads(du_ref, _RET_V0, dv, mask)
        _put_heads(du_ref, _RET_G0, dg, mask)

    return pl.pallas_call(
        body, name=name, grid=(nch,),
        in_specs=[pl.BlockSpec((RET_CHUNK, 6 * D), lambda n: (rev(n), 0)),
                  pl.BlockSpec((1, HEADS * RET_DV), lambda n: (0, 0)),
                  pl.BlockSpec((None, HEADS, RET_DK, RET_DV), lambda n: (rev(n), 0, 0, 0)),
                  pl.BlockSpec((RET_CHUNK, HEADS * RET_DV), lambda n: (rev(n), 0))] + _ret_const_specs(rev),
        out_specs=[pl.BlockSpec((RET_CHUNK, 6 * D), lambda n: (rev(n), 0)),
                   pl.BlockSpec((1, HEADS * RET_DV), lambda n: (0, 0))],
        out_shape=[jax.ShapeDtypeStruct((tp, 6 * D), BF16), jax.ShapeDtypeStruct((1, HEADS * RET_DV), F32)],
        scratch_shapes=[pltpu.VMEM((HEADS, RET_DK, RET_DV), F32)],
        compiler_params=_cp(1))(u, gain, states, d_on, cos, sin, dmat, dq, dk, dc)


_GLA_K0, _GLA_V0, _GLA_G0, _GLA_Z0 = 512, 1024, 2048, 3072


def _gla_head(q, k, v, g, z, state_t, wg, bg, gain, mask, lo, lo_t, to_mid, to_mid_t, in_second, pair):
    ga = _nn(jnp.broadcast_to(z, wg.shape[:-2] + z.shape), wg) + bg
    log_a = (jnp.minimum(ga, 0.0) - jnp.log(1.0 + jnp.exp(-jnp.abs(ga)))) * (mask * (1.0 / GLA_TAU))
    bcum = _cum(lo, lo_t, log_a)
    btot = jnp.sum(log_a, axis=-2, keepdims=True)
    qs = q * (GLA_DK ** -0.5)
    heads, levels = q.shape[0], pair.shape[0]
    decay = jnp.exp(_cum16(to_mid, to_mid_t, log_a).reshape(heads, levels, CHUNK, GLA_DK))
    qk = jnp.where(in_second > 0.0, qs.astype(BF16)[:, None], k.astype(BF16)[:, None]) * decay.astype(BF16)
    qk = qk.reshape(heads * levels, CHUNK, GLA_DK)
    rows = lax.broadcasted_iota(jnp.int32, (CHUNK, CHUNK), 0)
    cols = lax.broadcasted_iota(jnp.int32, (CHUNK, CHUNK), 1)
    scores = (jnp.where(rows == cols, _nt(qs, k), 0.0)
              + jnp.sum(_nt16(qk, qk).reshape(heads, levels, CHUNK, CHUNK) * pair, axis=1))
    o = _nn(scores, v) + _nt(qs * jnp.exp(bcum), state_t)
    new_state_t = state_t * jnp.exp(btot) + _tn(v, k * jnp.exp(btot - bcum))
    return _gated_headnorm(o, g, gain), new_state_t


def _gla_consts():
    r, c = np.meshgrid(np.arange(CHUNK), np.arange(CHUNK), indexing="ij")
    to_mid, second, pair = [], [], []
    block = 2
    while block <= CHUNK:
        mid = (r // block) * block + block // 2
        to_mid.append(((r >= mid) & (c > mid) & (c <= r)) | ((r < mid) & (c > r) & (c <= mid)))
        second.append((r >= mid)[:, :1])
        pair.append((r // block == c // block) & (r >= mid) & (c < mid))
        block *= 2
    to_mid = np.concatenate(to_mid)
    bf = lambda m: jnp.asarray(m, F32).astype(BF16)
    f32 = lambda ms: jnp.asarray(np.stack(ms), F32)
    return bf(r >= c), bf(c >= r), bf(to_mid), bf(to_mid.T), f32(second), f32(pair)


def _gla_const_specs(consts):
    return [pl.BlockSpec(a.shape, functools.partial(lambda nd, n: (0,) * nd, a.ndim)) for a in consts]


GLA_STEP_CHUNKS = 4


def _gla_pieces(u_ref, rows):
    return (_heads(u_ref, 0, GLA_DK, rows=rows), _heads(u_ref, _GLA_K0, GLA_DK, rows=rows),
            _heads(u_ref, _GLA_V0, GLA_DV, rows=rows), _heads(u_ref, _GLA_G0, GLA_DV, rows=rows),
            u_ref[rows, _GLA_Z0:_GLA_Z0 + 128].astype(F32))


def _gla_fwd(u, wg, bg, gain, name):
    tp = u.shape[0]
    nch = tp // CHUNK
    per = GLA_STEP_CHUNKS
    consts = _gla_consts()

    def body(u_ref, wg_ref, bg_ref, gain_ref, *refs):
        const_refs, (on_ref, st_ref, state_ref) = refs[:len(consts)], refs[len(consts):]

        @pl.when(pl.program_id(0) == 0)
        def _():
            state_ref[...] = jnp.zeros_like(state_ref)

        params = (_heads(wg_ref, 0, GLA_DK), _heads(bg_ref, 0, GLA_DK), _heads(gain_ref, 0, GLA_DV))
        mats = [ref[...] for ref in const_refs]
        state = state_ref[...]
        for c in range(per):
            rows = slice(CHUNK * c, CHUNK * (c + 1))
            st_ref[c] = state.astype(BF16)
            on, state = _gla_head(*_gla_pieces(u_ref, rows), state, *params, _row_mask(pl.program_id(0) * per + c), *mats)
            _put_heads(on_ref, 0, on, 1.0, rows=rows)
        state_ref[...] = state

    rows_spec = lambda width: pl.BlockSpec((per * CHUNK, width), lambda n: (n, 0))
    full = lambda r, c: pl.BlockSpec((r, c), lambda n: (0, 0))
    return pl.pallas_call(
        body, name=name, grid=(nch // per,),
        in_specs=[rows_spec(GLA_U), full(128, HEADS * GLA_DK), full(1, HEADS * GLA_DK), full(1, HEADS * GLA_DV)]
                 + _gla_const_specs(consts),
        out_specs=[rows_spec(HEADS * GLA_DV), pl.BlockSpec((per, HEADS, GLA_DV, GLA_DK), lambda n: (n, 0, 0, 0))],
        out_shape=[jax.ShapeDtypeStruct((tp, HEADS * GLA_DV), BF16),
                   jax.ShapeDtypeStruct((nch, HEADS, GLA_DV, GLA_DK), BF16)],
        scratch_shapes=[pltpu.VMEM((HEADS, GLA_DV, GLA_DK), F32)],
        compiler_params=_cp(1))(u, wg, bg, gain, *consts)


def _gla_bwd(u, wg, bg, gain, states, d_on, name):
    tp = u.shape[0]
    per = GLA_STEP_CHUNKS
    steps = tp // (per * CHUNK)
    rev = lambda n: steps - 1 - n
    consts = _gla_consts()

    def body(u_ref, wg_ref, bg_ref, gain_ref, st_ref, don_ref, *refs):
        const_refs, (du_ref, dwg_ref, dbg_ref, dgain_ref, dstate_ref) = refs[:len(consts)], refs[len(consts):]

        @pl.when(pl.program_id(0) == 0)
        def _():
            dstate_ref[...] = jnp.zeros_like(dstate_ref)
            dwg_ref[...] = jnp.zeros_like(dwg_ref)
            dbg_ref[...] = jnp.zeros_like(dbg_ref)
            dgain_ref[...] = jnp.zeros_like(dgain_ref)

        params = (_heads(wg_ref, 0, GLA_DK), _heads(bg_ref, 0, GLA_DK), _heads(gain_ref, 0, GLA_DV))
        mats = [ref[...] for ref in const_refs]
        dstate = dstate_ref[...]
        for c in reversed(range(per)):
            rows = slice(CHUNK * c, CHUNK * (c + 1))
            mask = _row_mask(rev(pl.program_id(0)) * per + c)
            _, vjp = jax.vjp(lambda *a: _gla_head(*a, mask, *mats), *_gla_pieces(u_ref, rows),
                             st_ref[c].astype(F32), *params)
            dq, dk, dv, dg, dz, dstate, dwg, dbg, dgain = vjp((_heads(don_ref, 0, GLA_DV, rows=rows), dstate))
            for hd in range(HEADS):
                dwg_ref[:, GLA_DK * hd:GLA_DK * (hd + 1)] += dwg[hd]
                dbg_ref[:, GLA_DK * hd:GLA_DK * (hd + 1)] += dbg[hd]
                dgain_ref[:, GLA_DV * hd:GLA_DV * (hd + 1)] += dgain[hd]
            _put_heads(du_ref, 0, dq, mask, rows=rows)
            _put_heads(du_ref, _GLA_K0, dk, mask, rows=rows)
            _put_heads(du_ref, _GLA_V0, dv, mask, rows=rows)
            _put_heads(du_ref, _GLA_G0, dg, mask, rows=rows)
            du_ref[rows, _GLA_Z0:_GLA_Z0 + 128] = dz.astype(BF16)
            du_ref[rows, _GLA_Z0 + 128:] = jnp.zeros((CHUNK, GLA_U - _GLA_Z0 - 128), BF16)
        dstate_ref[...] = dstate

    full = lambda r, c: pl.BlockSpec((r, c), lambda n: (0, 0))
    return pl.pallas_call(
        body, name=name, grid=(steps,),
        in_specs=[pl.BlockSpec((per * CHUNK, GLA_U), lambda n: (rev(n), 0)), full(128, HEADS * GLA_DK),
                  full(1, HEADS * GLA_DK), full(1, HEADS * GLA_DV),
                  pl.BlockSpec((per, HEADS, GLA_DV, GLA_DK), lambda n: (rev(n), 0, 0, 0)),
                  pl.BlockSpec((per * CHUNK, HEADS * GLA_DV), lambda n: (rev(n), 0))] + _gla_const_specs(consts),
        out_specs=[pl.BlockSpec((per * CHUNK, GLA_U), lambda n: (rev(n), 0)), full(128, HEADS * GLA_DK),
                   full(1, HEADS * GLA_DK), full(1, HEADS * GLA_DV)],
        out_shape=[jax.ShapeDtypeStruct((tp, GLA_U), BF16), jax.ShapeDtypeStruct((128, HEADS * GLA_DK), F32),
                   jax.ShapeDtypeStruct((1, HEADS * GLA_DK), F32), jax.ShapeDtypeStruct((1, HEADS * GLA_DV), F32)],
        scratch_shapes=[pltpu.VMEM((HEADS, GLA_DV, GLA_DK), F32)],
        compiler_params=_cp(1))(u, wg, bg, gain, states, d_on, *consts)


def _ffn_fwd(h, gain, w_in, w_out, tag):
    hn, ug, uu, act = _norm_ffn_in(h, gain, w_in, f"{tag}_in")
    if callable(w_out):
        w_out = w_out(act)
    return _out_proj(act, w_out, h, 0.5, f"{tag}_out"), (h, hn, ug, uu, act), w_out


def _ffn_dgrad(dh, w_out, w_in, act_dg, act_du, h, gain, name, split_front=False):
    tp, d = dh.shape
    ff = w_out.shape[0]
    tm = TM_SMALL
    nt = (((1,), (1,)), ((), ()))

    def body(dh_ref, wo_ref, wi_ref, dg_ref, du_ref, h_ref, g_ref, o_ref, *out_refs):
        dhi_ref, dgain_ref = out_refs[-2:]

        @pl.when(pl.program_id(0) == 0)
        def _():
            dgain_ref[...] = jnp.zeros_like(dgain_ref)

        dho = dh_ref[...]
        dact = lax.dot_general((0.5 * dho).astype(BF16), wo_ref[...], nt, preferred_element_type=F32)
        d_gate = (dact * dg_ref[...].astype(F32)).astype(BF16)
        d_up = (dact * du_ref[...].astype(F32)).astype(BF16)
        o_ref[:, :ff] = d_gate
        o_ref[:, ff:] = d_up
        dhn = (lax.dot_general(d_gate, wi_ref[:, :ff], nt, preferred_element_type=F32)
               + lax.dot_general(d_up, wi_ref[:, ff:], nt, preferred_element_type=F32))
        dx, xhat = _rmsnorm_bwd(dhn, h_ref[...], g_ref[...])
        dgain_ref[...] += jnp.sum(dhn * xhat, axis=0, keepdims=True)
        dhi_ref[...] = dho + dx
        if split_front:
            @pl.when(pl.program_id(0) == 0)
            def _():
                out_refs[0][...] = dho + dx

    rows = lambda width: pl.BlockSpec((tm, width), lambda i: (i, 0))
    if split_front:
        assert tm == FRONT
        dhi_specs = [pl.BlockSpec((tm, d), lambda i: (0, 0)), pl.BlockSpec((tm, d), lambda i: (jnp.maximum(i - 1, 0), 0))]
        dhi_shapes = [jax.ShapeDtypeStruct((FRONT, d), F32), jax.ShapeDtypeStruct((tp - FRONT, d), F32)]
    else:
        dhi_specs, dhi_shapes = [rows(d)], [jax.ShapeDtypeStruct((tp, d), F32)]
    out = pl.pallas_call(
        body, name=name, grid=(tp // tm,),
        in_specs=[rows(d), _resident(w_out.shape, 1), _resident(w_in.shape, 1), rows(ff), rows(ff), rows(d),
                  pl.BlockSpec((1, d), lambda i: (0, 0))],
        out_specs=[rows(2 * ff), *dhi_specs, pl.BlockSpec((1, d), lambda i: (0, 0))],
        out_shape=[jax.ShapeDtypeStruct((tp, 2 * ff), BF16), *dhi_shapes, jax.ShapeDtypeStruct((1, d), F32)],
        compiler_params=_cp(1))(dh, w_out, w_in, act_dg, act_du, h, gain)
    return (out[0], tuple(out[1:3]), out[3]) if split_front else tuple(out)


def _ffn_bwd(dh, saved, gain, w_in, w_out, tag, push, split_front=False):
    h, hn, act_dg, act_du, act = saved
    du, dh_in, d_gain = _ffn_dgrad(dh, w_out, w_in, act_dg, act_du, h, gain, f"{tag}_dgrad", split_front)
    d_w_out = _wgrad(act, dh, bm=D_FF // 2, bn=D, scale=0.5, sharded=False, name=f"{tag}_dwout")
    d_w_in = _wgrad(hn, du, bm=D, bn=D_FF, scale=1.0, sharded=False, name=f"{tag}_dwin")
    return dh_in, d_gain, push([("cols", d_w_in), d_w_out])


def _sequence_grads(x, target, p, weights, grads):
    row = lambda v, token: v.reshape(1, -1) + token[0, 0]
    gains = {}

    tok = weights.start(1, weights.start(0, None))
    weights.pin = tok
    h = jnp.concatenate([jnp.zeros((FRONT, D), F32), x], axis=0) + tok[0, 0]
    rope = _rope_tables(h.shape[0])
    w = weights.wait(0, [tok, h, *rope, *weights.later_shards(2)])
    tok = weights.start(2, w["l0_ffn1_in"])
    h = lax.dynamic_update_slice(h, w["meta"], (FRONT - N_META, 0))
    gains["l0_ffn1"] = row(p["norm_ffn1"][0], tok)
    h, s1, w["l0_ffn1_out"] = _ffn_fwd(h, gains["l0_ffn1"], w["l0_ffn1_in"],
                                       lambda act: weights.wait(1, act)["l0_ffn1_out"], "l0_ffn1")
    w.update(weights.wait(2, h))
    tok = weights.start(4, weights.start(3, w["ret_in"]))
    gains["ret"] = row(p["norm_mix"][0], tok)
    hn, u = _norm_proj(h, gains["ret"], w["ret_in"], "ret_in")
    w.update(weights.wait(3, u))
    on, states, h_mix = _ret_fwd(u, w["ret_gain"], rope, h, w["ret_out"], "ret_fwd")
    s2 = (h, hn, u, on, states)
    w.update(weights.wait(4, h_mix))
    tok = weights.start(5, w["l0_ffn2_in"])
    gains["l0_ffn2"] = row(p["norm_ffn2"][0], tok)
    h, s3, _ = _ffn_fwd(h_mix, gains["l0_ffn2"], w["l0_ffn2_in"], w["l0_ffn2_out"], "l0_ffn2")
    saved = [(s1, s2, s3)]

    w.update(weights.wait(5, h))
    tok = weights.start(6, w["l1_ffn1_in"])
    gains["l1_ffn1"] = row(p["norm_ffn1"][1], tok)
    h, s1, _ = _ffn_fwd(h, gains["l1_ffn1"], w["l1_ffn1_in"], w["l1_ffn1_out"], "l1_ffn1")
    w.update(weights.wait(6, h))
    tok = weights.start(7, w["gla_out"])
    gains["gla"] = row(p["norm_mix"][1], tok)
    hn, u = _norm_proj(h, gains["gla"], w["gla_in"], "gla_in")
    on, states = _gla_fwd(u, w["gla_wg"], w["gla_bg"], w["gla_gain"], "gla_fwd")
    h_mix = _out_proj(on, w["gla_out"], h, 1.0, "gla_out")
    s2 = (h, hn, u, on, states)
    w.update(weights.wait(7, h_mix))
    gains["l1_ffn2"] = p["norm_ffn2"][1].reshape(1, -1)
    s3 = (h_mix, *_norm_ffn_in(h_mix, gains["l1_ffn2"], w["l1_ffn2_in"], "l1_ffn2_in"))
    saved.append((s1, s2, s3))

    dh, d_final, loss = _loss_head(s3[-1], w["l1_ffn2_out"], h_mix, p["final_norm"].reshape(1, -1), target,
                                   "l1_ffn2_out_loss")
    small = {"final_norm": d_final, "norm_ffn1": [None, None], "norm_mix": [None, None], "norm_ffn2": [None, None]}
    pusher = lambda k: functools.partial(grads.push, k)

    s1, s2, s3 = saved[1]
    dh, small["norm_ffn2"][1], tok = _ffn_bwd(dh, s3, gains["l1_ffn2"], w["l1_ffn2_in"], w["l1_ffn2_out"], "l1_ffn2",
                                              pusher(0))
    h_in, hn, u, on, states = s2
    d_on, d_out = _out_proj_bwd(dh, w["gla_out"], on, "gla_out_bwd")
    du, small["gla_wg"], small["gla_bg"], small["gla_gain"] = _gla_bwd(
        u, w["gla_wg"], w["gla_bg"], w["gla_gain"] + tok[0, 0], states, d_on, "gla_bwd")
    d_in = _wgrad(hn, du, bm=D, bn=GLA_U, scale=1.0, sharded=False, name="gla_dwin")
    d_in = jnp.moveaxis(d_in[:, :GLA_IN].reshape(D, N_CHIPS, -1), 1, 0)
    tok = grads.push(1, [d_in, d_out])
    dh, small["norm_mix"][1] = _dgrad_norm(du, w["gla_in"], h_in, gains["gla"] + tok[0, 0], dh, "gla_dnorm")
    dh, small["norm_ffn1"][1], tok = _ffn_bwd(dh, s1, gains["l1_ffn1"], w["l1_ffn1_in"], w["l1_ffn1_out"], "l1_ffn1",
                                              pusher(2))

    s1, s2, s3 = saved[0]
    dh, small["norm_ffn2"][0], tok = _ffn_bwd(dh, s3, gains["l0_ffn2"] + tok[0, 0], w["l0_ffn2_in"],
                                              w["l0_ffn2_out"], "l0_ffn2", pusher(3))
    h_in, hn, u, on, states = s2
    d_on, d_out = _out_proj_bwd(dh, w["ret_out"], on, "ret_out_bwd")
    du, small["ret_gain"] = _ret_bwd(u, w["ret_gain"] + tok[0, 0], rope, states, d_on, "ret_bwd")
    d_in = _wgrad(hn, du, bm=D, bn=w["ret_in"].shape[2], scale=1.0, sharded=True, name="ret_dwin")
    tok = grads.push(4, [d_in, d_out])
    dh, small["norm_mix"][0] = _dgrad_norm(du, w["ret_in"], h_in, gains["ret"] + tok[0, 0], dh, "ret_dnorm")
    (d_front, d_x), small["norm_ffn1"][0], tok = _ffn_bwd(dh, s1, gains["l0_ffn1"], w["l0_ffn1_in"], w["l0_ffn1_out"],
                                                          "l0_ffn1", pusher(5), split_front=True)
    grads.push(6, [], [d_front[FRONT - N_META:], *small["norm_ffn1"], *small["norm_mix"], *small["norm_ffn2"],
                       small["final_norm"], small["ret_gain"], small["gla_wg"][:GLA_RANK], small["gla_bg"],
                       small["gla_gain"], loss[:, :1] + tok[0, 0]])
    return d_x


_HBM = pl.BlockSpec(memory_space=pl.ANY)


def _place():
    return lax.axis_index("x"), lax.axis_index("y"), lax.axis_index("c")


def _flip(v, bit):
    return 1 - v if bit else v


DMA_CHUNK_BYTES = 128 * 1024


def _row_chunks(ref):
    rows, cols = ref.shape
    step = _row_tile(rows, max(16, DMA_CHUNK_BYTES // (cols * ref.dtype.itemsize)))
    return [pl.ds(a, step) for a in range(0, rows, step)]


def _whole(src, dst, send_sem, recv_sem, peer):
    return pltpu.make_async_remote_copy(src_ref=src, dst_ref=dst, send_sem=send_sem, recv_sem=recv_sem,
                                        device_id=peer, device_id_type=MESH)


def _send(src, dst, send_sem, recv_sem, peer):
    for rows in _row_chunks(src):
        _whole(src.at[rows], dst.at[rows], send_sem, recv_sem, peer).start()
    return _whole(src, dst, send_sem, recv_sem, peer)


_HBM_ONLY = pl.BlockSpec(memory_space=pltpu.HBM)
_SEMS = pl.BlockSpec(memory_space=pltpu.SEMAPHORE)
_SIDE_EFFECT = pltpu.CompilerParams(has_side_effects=pltpu.SideEffectType.DATAFLOW_SIDE_EFFECTING)
_GATHER_FLIPS = [(1, 0, 0), (0, 1, 0), (1, 1, 0), (0, 0, 1)]
_PEER_FLIPS = [(fx, fy, fc) for fx in (0, 1) for fy in (0, 1) for fc in (0, 1)][1:]


def _zero_token():
    return jnp.zeros((8, 128), F32)


def _exchange_start(srcs, lands, route, flips, after, name):
    n = len(srcs)

    def body(*refs):
        src, land = refs[:n], refs[n:2 * n]
        send_sems, recv_sems, token = refs[2 * n + 1], refs[2 * n + 2], refs[-1]
        me = _place()
        for t in range(n):
            for j, flip in enumerate(flips):
                peer = tuple(_flip(v, f) for v, f in zip(me, flip))
                s, d = route(t, src[t], land[t], me, peer)
                _send(s, d, send_sems.at[t * len(flips) + j], recv_sems.at[t * len(flips) + j], peer)
        token[...] = jnp.zeros_like(token)

    hbm = lambda a: pltpu.HBM(a.shape, a.dtype)
    sems = pltpu.SemaphoreType.DMA((n * len(flips),))
    operands = [pltpu.with_memory_space_constraint(a, pltpu.HBM) for a in list(srcs) + list(lands)]
    out = pl.pallas_call(
        body, name=name, in_specs=[_HBM_ONLY] * (2 * n) + [_HBM],
        out_shape=(sems, sems, *[hbm(a) for a in operands], jax.ShapeDtypeStruct((8, 128), F32)),
        out_specs=(_SEMS, _SEMS, *[_HBM_ONLY] * (2 * n), pl.BlockSpec(memory_space=pltpu.VMEM)),
        input_output_aliases={i: 2 + i for i in range(2 * n)}, compiler_params=_SIDE_EFFECT,
    )(*operands, _zero_token() if after is None else after)
    return (out[0], out[1], out[2:2 + n], out[2 + n:2 + 2 * n]), out[-1]


def _exchange_wait(started, route, flips, after, name):
    send_sems, recv_sems, srcs, lands = started
    n = len(srcs)

    def body(*refs):
        src, land = refs[:n], refs[n:2 * n]
        send_sems, recv_sems = refs[2 * n], refs[2 * n + 1]
        me = _place()
        for t in range(n):
            for j, flip in enumerate(flips):
                peer = tuple(_flip(v, f) for v, f in zip(me, flip))
                s, d = route(t, src[t], land[t], me, peer)
                cp = _whole(s, d, send_sems.at[t * len(flips) + j], recv_sems.at[t * len(flips) + j], peer)
                cp.wait_send()
                cp.wait_recv()

    hbm = lambda a: pltpu.HBM(a.shape, a.dtype)
    after = list(after) if isinstance(after, (list, tuple)) else [after]
    out = pl.pallas_call(
        body, name=name, in_specs=[_HBM_ONLY] * (2 * n) + [_SEMS, _SEMS] + [_HBM] * len(after),
        out_shape=tuple(hbm(a) for a in list(srcs) + list(lands)), out_specs=tuple([_HBM_ONLY] * (2 * n)),
        input_output_aliases={i: i for i in range(2 * n)}, compiler_params=_SIDE_EFFECT,
    )(*srcs, *lands, send_sems, recv_sems, *after)
    return out[:n], out[n:]


def _gather_route(t, src, land, me, peer):
    mine = 2 * me[0] + me[1]
    if land.ndim == 3:
        return src, land.at[mine]
    cols = src.shape[1]
    return src, land.at[:, pl.ds(pl.multiple_of(mine * cols, 128), cols)]


def _scatter_route(n_pieces):
    def route(t, src, land, me, peer):
        chip = 2 * peer[0] + peer[1]
        if t >= n_pieces:
            part = src
        elif src.ndim == 4:
            part = src.at[chip, peer[2]]
        else:
            rows, cols = land.shape[1:]
            part = src.at[pl.ds(pl.multiple_of(peer[2] * rows, 16), rows), pl.ds(pl.multiple_of(chip * cols, 128), cols)]
        return part, land.at[4 * me[0] + 2 * me[1] + me[2]]

    return route


def _swap_cores(halves, name):
    n = len(halves)

    def body(*refs):
        src, dst = refs[:n], refs[n:2 * n]
        send_sems, recv_sems = refs[2 * n:]
        x, y, c = _place()
        copies = [_send(src[t], dst[t], send_sems.at[t], recv_sems.at[t], (x, y, 1 - c)) for t in range(n)]
        for cp in copies:
            cp.wait()

    got = pl.pallas_call(
        body, name=name, in_specs=[_HBM] * n, out_specs=[_HBM] * n,
        out_shape=[jax.ShapeDtypeStruct(a.shape, a.dtype) for a in halves],
        scratch_shapes=[pltpu.SemaphoreType.DMA((n,)), pltpu.SemaphoreType.DMA((n,))],
    )(*halves)
    south = lax.axis_index("c") == 0
    return [jnp.stack([jnp.where(south, a, b), jnp.where(south, b, a)]) for a, b in zip(halves, got)]


def _row_tile(rows, cap):
    fits = [t for t in range(16, cap + 1, 16) if rows % t == 0]
    return fits[-1] if fits else rows


def _sum_slots(a, name):
    _, r, c = a.shape
    tr = _row_tile(r, 384)

    def body(a_ref, o_ref):
        s = a_ref[0].astype(F32)
        for k in range(1, N_DEV):
            s = s + a_ref[k].astype(F32)
        o_ref[...] = s

    return pl.pallas_call(
        body, name=name, grid=(r // tr,),
        in_specs=[pl.BlockSpec((N_DEV, tr, c), lambda i: (0, i, 0))],
        out_specs=pl.BlockSpec((tr, c), lambda i: (i, 0)),
        out_shape=jax.ShapeDtypeStruct((r, c), F32),
        compiler_params=_cp(1))(a)


def _adamw(w, g, m, v, name):
    layers, r, c = w.shape
    tr = _row_tile(r, 256)

    def body(w_ref, g_ref, m_ref, v_ref, d_ref, nm_ref, nv_ref):
        gv = g_ref[...]
        nm = ADAM_B1 * m_ref[...] + (1.0 - ADAM_B1) * gv
        nv = ADAM_B2 * v_ref[...] + (1.0 - ADAM_B2) * (gv * gv)
        m_hat = nm / (1.0 - ADAM_B1 ** ADAM_STEP)
        v_hat = nv / (1.0 - ADAM_B2 ** ADAM_STEP)
        d_ref[...] = -ADAM_LR * (m_hat / (jnp.sqrt(v_hat) + ADAM_EPS) + ADAM_WD * w_ref[...])
        nm_ref[...] = nm
        nv_ref[...] = nv

    spec = pl.BlockSpec((None, tr, c), lambda a, i: (a, i, 0))
    return pl.pallas_call(
        body, name=name, grid=(layers, r // tr), in_specs=[spec] * 4, out_specs=[spec] * 3,
        out_shape=[jax.ShapeDtypeStruct((layers, r, c), F32)] * 3,
        compiler_params=_cp(2))(*[pltpu.with_memory_space_constraint(a, pltpu.HBM) for a in (w, g, m, v)])


_SMALL = ["meta_tokens", "ret_head_norm", "gla_w_gate", "gla_b_gate", "gla_head_norm"]
_LOCAL_SMALL = ["meta_tokens", "norm_ffn1", "norm_mix", "norm_ffn2", "ret_head_norm", "gla_w_gate", "gla_b_gate",
                "gla_head_norm", "final_norm"]
_WEIGHTS = ["meta_tokens", "norm_ffn1", "ffn1_w_in", "ffn1_w_out", "norm_mix", "norm_ffn2", "ffn2_w_in", "ffn2_w_out",
            "ret_w_in", "ret_head_norm", "ret_w_out", "gla_w_in", "gla_w_gate", "gla_b_gate", "gla_head_norm",
            "gla_w_out", "final_norm"]


def _pack_rows(arrays, width):
    flat = jnp.concatenate([a.reshape(-1) for a in arrays])
    pad = -flat.shape[0] % (8 * width)
    return jnp.pad(flat, (0, pad)).reshape(-1, width)


def _unpack_rows(packed, shapes):
    flat, out, at = packed.reshape(-1), [], 0
    for s in shapes:
        size = 1
        for dim in s:
            size *= dim
        out.append(flat[at:at + size].reshape(s))
        at += size
    return out


class _WeightGather:
    GROUPS = [("small", "l0_ffn1_in"), ("l0_ffn1_out",), ("ret_in",), ("ret_out",), ("l0_ffn2_in", "l0_ffn2_out"),
              ("l1_ffn1_in", "l1_ffn1_out"), ("gla_in", "gla_out"), ("l1_ffn2_in", "l1_ffn2_out")]

    def __init__(self, p):
        self.small_shapes = [p[name].shape for name in _SMALL]
        self.f32 = {"small": _pack_rows([p[name] for name in _SMALL], 128), "ret_in": p["ret_w_in"][0],
                    "ret_out": p["ret_w_out"][0], "gla_in": p["gla_w_in"][0], "gla_out": p["gla_w_out"][0]}
        for layer in range(2):
            for name in ("ffn1", "ffn2"):
                self.f32[f"l{layer}_{name}_in"] = p[f"{name}_w_in"][layer]
                self.f32[f"l{layer}_{name}_out"] = p[f"{name}_w_out"][layer]
        self.shards = {}
        self.started = {}
        self.pin = None

    def shard(self, name):
        if name not in self.shards:
            a = self.f32[name]
            if name != "small":
                a = (a if self.pin is None else a + self.pin[0, 0]).astype(BF16)
            self.shards[name] = a
        return self.shards[name]

    def later_shards(self, k):
        return [self.shard(name) for group in self.GROUPS[k:] for name in group]

    def start(self, k, after):
        shards = [self.shard(name) for name in self.GROUPS[k]]
        lands = []
        for name, s in zip(self.GROUPS[k], shards):
            if "ffn" in name and name.endswith("_in"):
                lands.append(lax.empty((s.shape[0], N_CHIPS * s.shape[1]), s.dtype))
            else:
                lands.append(lax.empty((N_CHIPS,) + s.shape, s.dtype))
        self.started[k], token = _exchange_start(shards, lands, _gather_route, _GATHER_FLIPS, after, f"gather{k}_start")
        return token

    def wait(self, k, after):
        _, got = _exchange_wait(self.started[k], _gather_route, _GATHER_FLIPS, after, f"gather{k}_wait")
        w = {}
        for name, g in zip(self.GROUPS[k], got):
            if name == "small":
                parts = zip(*[_unpack_rows(g[chip], self.small_shapes) for chip in range(N_CHIPS)])
                cat = lambda a: jnp.moveaxis(a, 0, -2).reshape(a.shape[1:-1] + (-1,))
                meta, ret_gain, wg, bg, gla_gain = [cat(jnp.stack(part)) for part in parts]
                w.update(meta=meta, ret_gain=ret_gain.reshape(1, -1), gla_bg=bg.reshape(1, -1),
                         gla_gain=gla_gain.reshape(1, -1),
                         gla_wg=jnp.pad(wg[0], ((0, 128 - GLA_RANK), (0, 0))).astype(BF16))
            elif name == "gla_in":
                full = jnp.moveaxis(g, 0, 1).reshape(D, -1)
                w[name] = jnp.pad(full, ((0, 0), (0, GLA_U - GLA_IN)))[None]
            elif name.endswith("_out"):
                w[name] = g.reshape(-1, g.shape[-1])
            else:
                w[name] = g
        return w


class _GradExchange:
    def __init__(self):
        self.started = []
        self.token = None
        self.small_shapes = None

    def push(self, k, arrays, small=None):
        srcs, lands = [], []
        for a in arrays:
            if isinstance(a, tuple):
                a = a[1]
                piece = (a.shape[0] // 2, a.shape[1] // N_CHIPS)
            else:
                a = a.reshape(N_CHIPS, 2, -1, a.shape[-1])
                piece = a.shape[2:]
            srcs.append(a)
            lands.append(lax.empty((N_DEV,) + piece, a.dtype))
        if small is not None:
            self.small_shapes = [a.shape for a in small]
            srcs.append(_pack_rows(small, D))
            lands.append(lax.empty((N_DEV,) + srcs[-1].shape, F32))
        started, self.token = _exchange_start(srcs, lands, _scatter_route(len(arrays)), _PEER_FLIPS, None,
                                              f"scatter{k}_start")
        self.started.append((started, len(arrays)))
        return self.token

    def collect(self, groups, after=None):
        x, y, c = _place()
        after, sums = self.token if after is None else after, []
        for k in groups:
            started, n_pieces = self.started[k]
            srcs, got = _exchange_wait(started, _scatter_route(n_pieces), _PEER_FLIPS, after, f"scatter{k}_wait")
            own = []
            for t, (a, g) in enumerate(zip(srcs, got)):
                if t >= n_pieces:
                    own.append(a)
                elif a.ndim == 4:
                    own.append(a[2 * x + y, c])
                else:
                    rows, cols = g.shape[1:]
                    own.append(lax.dynamic_slice(a, (c * rows, (2 * x + y) * cols), (rows, cols)))
            got = [lax.dynamic_update_index_in_dim(g, a, 4 * x + 2 * y + c, 0) for g, a in zip(got, own)]
            sums.append([_sum_slots(a, f"sum{k}_{i}") for i, a in enumerate(got)])
            after = sums[-1][0]
        return sums


def kernel(x, meta_tokens, norm_ffn1, ffn1_w_in, ffn1_w_out, norm_mix, norm_ffn2, ffn2_w_in, ffn2_w_out, ret_w_in, ret_head_norm, ret_w_out, gla_w_in, gla_w_gate, gla_b_gate, gla_head_norm, gla_w_out, final_norm, loss_target, m_meta_tokens, m_norm_ffn1, m_ffn1_w_in, m_ffn1_w_out, m_norm_mix, m_norm_ffn2, m_ffn2_w_in, m_ffn2_w_out, m_ret_w_in, m_ret_head_norm, m_ret_w_out, m_gla_w_in, m_gla_w_gate, m_gla_b_gate, m_gla_head_norm, m_gla_w_out, m_final_norm, v_meta_tokens, v_norm_ffn1, v_ffn1_w_in, v_ffn1_w_out, v_norm_mix, v_norm_ffn2, v_ffn2_w_in, v_ffn2_w_out, v_ret_w_in, v_ret_head_norm, v_ret_w_out, v_gla_w_in, v_gla_w_gate, v_gla_b_gate, v_gla_head_norm, v_gla_w_out, v_final_norm):
    p = dict(meta_tokens=meta_tokens, norm_ffn1=norm_ffn1, ffn1_w_in=ffn1_w_in, ffn1_w_out=ffn1_w_out, norm_mix=norm_mix,
             norm_ffn2=norm_ffn2, ffn2_w_in=ffn2_w_in, ffn2_w_out=ffn2_w_out, ret_w_in=ret_w_in,
             ret_head_norm=ret_head_norm, ret_w_out=ret_w_out, gla_w_in=gla_w_in, gla_w_gate=gla_w_gate,
             gla_b_gate=gla_b_gate, gla_head_norm=gla_head_norm, gla_w_out=gla_w_out, final_norm=final_norm)
    m = dict(zip(_WEIGHTS, (m_meta_tokens, m_norm_ffn1, m_ffn1_w_in, m_ffn1_w_out, m_norm_mix, m_norm_ffn2, m_ffn2_w_in,
                            m_ffn2_w_out, m_ret_w_in, m_ret_head_norm, m_ret_w_out, m_gla_w_in, m_gla_w_gate,
                            m_gla_b_gate, m_gla_head_norm, m_gla_w_out, m_final_norm)))
    v = dict(zip(_WEIGHTS, (v_meta_tokens, v_norm_ffn1, v_ffn1_w_in, v_ffn1_w_out, v_norm_mix, v_norm_ffn2, v_ffn2_w_in,
                            v_ffn2_w_out, v_ret_w_in, v_ret_head_norm, v_ret_w_out, v_gla_w_in, v_gla_w_gate,
                            v_gla_b_gate, v_gla_head_norm, v_gla_w_out, v_final_norm)))

    exchange = _GradExchange()
    d_x = _sequence_grads(x[0], loss_target[0], p, _WeightGather(p), exchange)
    names = [("ffn2_w_in", 1), ("ffn2_w_out", 1), ("gla_w_in", 0), ("gla_w_out", 0), ("ffn1_w_in", 1), ("ffn1_w_out", 1),
             ("ffn2_w_in", 0), ("ffn2_w_out", 0), ("ret_w_in", 0), ("ret_w_out", 0), ("ffn1_w_in", 0), ("ffn1_w_out", 0)]
    shard, grads, delta, new_m, new_v = {}, {}, {}, {}, {}

    def swap(sums, keys, name):
        for key, a in zip(keys, _swap_cores(sums, name)):
            shard[key] = a.reshape(-1, a.shape[-1])

    def update(name):
        layers = p[name].shape[0]
        grads[name] = jnp.stack([shard[name, layer] for layer in range(layers)])
        delta[name], new_m[name], new_v[name] = _adamw(p[name], grads[name], m[name], v[name], f"adamw_{name}")

    swap([a for group in exchange.collect(range(5)) for a in group], names[:10], "swap_first")
    for name in ("ffn2_w_in", "ffn2_w_out", "ret_w_in", "ret_w_out", "gla_w_in", "gla_w_out"):
        update(name)
    last, (small_sum,) = exchange.collect([5, 6], after=list(delta.values()))
    swap(last, names[10:], "swap_last")
    for name in ("ffn1_w_in", "ffn1_w_out"):
        update(name)

    chip = 2 * lax.axis_index("x") + lax.axis_index("y")
    cols = lambda a, n: lax.dynamic_slice_in_dim(a, chip * n, n, axis=a.ndim - 1)
    (s_meta, s_n1a, s_n1b, s_nma, s_nmb, s_n2a, s_n2b, s_final, s_ret_gain, s_wg, s_bg, s_gla_gain,
     s_loss) = _unpack_rows(small_sum, exchange.small_shapes)
    grads.update({
        "meta_tokens": cols(s_meta, 256), "norm_ffn1": jnp.concatenate([s_n1a, s_n1b]),
        "norm_mix": jnp.concatenate([s_nma, s_nmb]), "norm_ffn2": jnp.concatenate([s_n2a, s_n2b]),
        "final_norm": s_final.reshape(D),
        "ret_head_norm": cols(s_ret_gain.reshape(1, HEADS, RET_DV), RET_DV // N_CHIPS),
        "gla_w_gate": cols(s_wg, GLA_DK)[None], "gla_b_gate": cols(s_bg, GLA_DK),
        "gla_head_norm": cols(s_gla_gain.reshape(1, HEADS, GLA_DV), GLA_DV // N_CHIPS),
    })
    for name in _LOCAL_SMALL:
        shape = p[name].shape
        as3d = lambda a: a.reshape((1,) * (3 - len(shape)) + shape)
        out = _adamw(as3d(p[name]), as3d(grads[name]), as3d(m[name]), as3d(v[name]), f"adamw_{name}")
        delta[name], new_m[name], new_v[name] = [a.reshape(shape) for a in out]

    return (s_loss.reshape(()), d_x[None], *[grads[n] for n in _WEIGHTS], *[delta[n] for n in _WEIGHTS],
            *[new_m[n] for n in _WEIGHTS], *[new_v[n] for n in _WEIGHTS])
```

```python
import functools

import jax
import numpy as np
import jax.numpy as jnp
from jax import lax
from jax.experimental import pallas as pl
from jax.experimental.pallas import tpu as pltpu

F32, BF16 = jnp.float32, jnp.bfloat16
MESH = pl.DeviceIdType.MESH

D = 1024
N_META = 16
CHUNK = 64
RET_CHUNK = 256
FRONT = 256
D_FF = 2816
EPS = 1e-6
HEADS = 4
RET_DK, RET_DV = 256, 512
GLA_DK, GLA_DV = 128, 256
GLA_RANK = 16
GLA_TAU = 16.0
GLA_IN = 2 * HEADS * GLA_DK + 2 * HEADS * GLA_DV + GLA_RANK
GLA_U = 3328
ROPE_BASE = 10000.0
N_CHIPS = 4
N_DEV = 8

ADAM_LR, ADAM_B1, ADAM_B2, ADAM_EPS, ADAM_WD, ADAM_STEP = 0.001, 0.9, 0.999, 1e-08, 0.01, 10

VMEM_LIMIT_BYTES = 56 * 1024 * 1024
TM = 768
TM_SMALL = 256


TM_RESIDENT = 384
MXU_TILE = 256


def _cp(n_axes):
    return pltpu.CompilerParams(dimension_semantics=("arbitrary",) * n_axes, vmem_limit_bytes=VMEM_LIMIT_BYTES)


def _resident(shape, n_axes):
    zeros = (0,) * len(shape)
    index = (lambda i: zeros) if n_axes == 1 else (lambda i, j: zeros)
    return pl.BlockSpec(shape, index, pipeline_mode=pl.Buffered(1))


def _dg(a, b, ca, cb):
    nb = a.ndim - 2
    dims = (((ca + nb,), (cb + nb,)), (tuple(range(nb)), tuple(range(nb))))
    return lax.dot_general(a.astype(BF16), b.astype(BF16), dims, preferred_element_type=F32)


@jax.custom_vjp
def _nn(a, b):
    return _dg(a, b, 1, 0)


@jax.custom_vjp
def _nt(a, b):
    return _dg(a, b, 1, 1)


@jax.custom_vjp
def _tn(a, b):
    return _dg(a, b, 0, 0)


@jax.custom_vjp
def _nt16(a, b):
    return _dg(a, b, 1, 1)


def _dot_vjp(fn, ca, cb, da, db, operand_dtype=F32):
    def fwd(a, b):
        a, b = a.astype(BF16), b.astype(BF16)
        return _dg(a, b, ca, cb), (a, b)

    def bwd(res, g):
        a, b = res
        g = g.astype(BF16)
        grad = lambda other, dims, g_first: _dg(g, other, *dims) if g_first else _dg(other, g, *dims)
        return grad(b, *da).astype(operand_dtype), grad(a, *db).astype(operand_dtype)

    fn.defvjp(fwd, bwd)


_dot_vjp(_nn, 1, 0, ((1, 1), True), ((0, 0), False))
_dot_vjp(_nt, 1, 1, ((1, 0), True), ((0, 0), True))
_dot_vjp(_tn, 0, 0, ((1, 1), False), ((1, 0), False))
_dot_vjp(_nt16, 1, 1, ((1, 0), True), ((0, 0), True), BF16)


def _split_dot(m, a, parts):
    mb = jnp.broadcast_to(m, a.shape[:-2] + m.shape)
    total, rest = None, a
    for _ in range(parts):
        term = rest.astype(BF16)
        rest = rest - term.astype(F32)
        product = _dg(mb, term, 1, 0)
        total = product if total is None else total + product
    return total


def _make_cum(parts):
    @jax.custom_vjp
    def cum(m, mt, a):
        return _split_dot(m, a, parts)

    cum.defvjp(lambda m, mt, a: (_split_dot(m, a, parts), (m, mt)),
               lambda res, g: (jnp.zeros_like(res[0]), jnp.zeros_like(res[1]), _split_dot(res[1], g, parts)))
    return cum


_cum = _make_cum(3)
_cum16 = _make_cum(2)


def _sigmoid(x):
    return 1.0 / (1.0 + jnp.exp(-x))


def _rms(x):
    return lax.rsqrt(jnp.mean(x * x, axis=-1, keepdims=True) + EPS)


def _rmsnorm_bwd(dy, x, gain):
    r = _rms(x)
    xhat = x * r
    dxh = dy * gain
    return r * (dxh - xhat * jnp.mean(dxh * xhat, axis=-1, keepdims=True)), xhat


def _norm_proj(h, gain, w, name):
    tp, d = h.shape
    s, _, ns = w.shape

    tm = TM_RESIDENT

    def body(h_ref, g_ref, w_ref, hn_ref, u_ref):
        x = h_ref[...]
        a = (x * _rms(x) * g_ref[...]).astype(BF16)
        hn_ref[...] = a
        for k in range(s):
            u_ref[:, ns * k:ns * (k + 1)] = jnp.dot(a, w_ref[k], preferred_element_type=F32).astype(BF16)

    return pl.pallas_call(
        body, name=name, grid=(tp // tm,),
        in_specs=[pl.BlockSpec((tm, d), lambda i: (i, 0)), pl.BlockSpec((1, d), lambda i: (0, 0)), _resident(w.shape, 1)],
        out_specs=[pl.BlockSpec((tm, d), lambda i: (i, 0)), pl.BlockSpec((tm, s * ns), lambda i: (i, 0))],
        out_shape=[jax.ShapeDtypeStruct((tp, d), BF16), jax.ShapeDtypeStruct((tp, s * ns), BF16)],
        compiler_params=_cp(1))(h, gain, w)


def _norm_ffn_in(h, gain, w, name):
    tp, d = h.shape
    ff = w.shape[1] // 2
    tm = TM_RESIDENT
    blocks = [(c, min(c + 6 * MXU_TILE, ff)) for c in range(0, ff, 6 * MXU_TILE)]

    def body(h_ref, g_ref, w_ref, hn_ref, dg_ref, du_ref, act_ref):
        x = h_ref[...]
        a = (x * _rms(x) * g_ref[...]).astype(BF16)
        hn_ref[...] = a
        for c0, c1 in blocks:
            g = jnp.dot(a, w_ref[:, c0:c1], preferred_element_type=F32)
            u = jnp.dot(a, w_ref[:, ff + c0:ff + c1], preferred_element_type=F32)
            sg = _sigmoid(g)
            silu = g * sg
            dg_ref[:, c0:c1] = (u * (sg + silu * (1.0 - sg))).astype(BF16)
            du_ref[:, c0:c1] = silu.astype(BF16)
            act_ref[:, c0:c1] = (silu * u).astype(BF16)

    wide = jax.ShapeDtypeStruct((tp, ff), BF16)
    return pl.pallas_call(
        body, name=name, grid=(tp // tm,),
        in_specs=[pl.BlockSpec((tm, d), lambda i: (i, 0)), pl.BlockSpec((1, d), lambda i: (0, 0)),
                  _resident(w.shape, 1)],
        out_specs=[pl.BlockSpec((tm, d), lambda i: (i, 0))] + [pl.BlockSpec((tm, ff), lambda i: (i, 0))] * 3,
        out_shape=[jax.ShapeDtypeStruct((tp, d), BF16), wide, wide, wide],
        compiler_params=_cp(1))(h, gain, w)


def _out_proj(a, w, h, scale, name):
    tp, k = a.shape
    d = w.shape[1]

    def body(a_ref, w_ref, h_ref, o_ref):
        o_ref[...] = h_ref[...] + scale * jnp.dot(a_ref[...], w_ref[...], preferred_element_type=F32)

    return pl.pallas_call(
        body, name=name, grid=(tp // TM,),
        in_specs=[pl.BlockSpec((TM, k), lambda i: (i, 0)), pl.BlockSpec((k, d), lambda i: (0, 0)),
                  pl.BlockSpec((TM, d), lambda i: (i, 0))],
        out_specs=pl.BlockSpec((TM, d), lambda i: (i, 0)),
        out_shape=jax.ShapeDtypeStruct((tp, d), F32),
        compiler_params=_cp(1))(a, w, h)


def _out_proj_bwd(dh, w, on, name):
    tp, d = dh.shape
    k = w.shape[0]
    steps = tp // TM

    def body(dh_ref, w_ref, on_ref, don_ref, dw_ref, acc_ref):
        i = pl.program_id(0)

        @pl.when(i == 0)
        def _():
            acc_ref[...] = jnp.zeros_like(acc_ref)

        g = dh_ref[...].astype(BF16)
        don_ref[...] = lax.dot_general(g, w_ref[...], (((1,), (1,)), ((), ())), preferred_element_type=F32).astype(BF16)
        acc_ref[...] += lax.dot_general(on_ref[...], g, (((0,), (0,)), ((), ())), preferred_element_type=F32)

        @pl.when(i == steps - 1)
        def _():
            dw_ref[...] = acc_ref[...].astype(BF16)

    return pl.pallas_call(
        body, name=name, grid=(steps,),
        in_specs=[pl.BlockSpec((TM, d), lambda i: (i, 0)), _resident(w.shape, 1), pl.BlockSpec((TM, k), lambda i: (i, 0))],
        out_specs=[pl.BlockSpec((TM, k), lambda i: (i, 0)), pl.BlockSpec((k, d), lambda i: (0, 0))],
        out_shape=[jax.ShapeDtypeStruct((tp, k), BF16), jax.ShapeDtypeStruct((k, d), BF16)],
        scratch_shapes=[pltpu.VMEM((k, d), F32)],
        compiler_params=_cp(1))(dh, w, on)


def _wgrad(a, b, *, bm, bn, scale, sharded, name):
    tp, m = a.shape
    n = b.shape[1]
    nk = tp // TM

    def body(a_ref, b_ref, o_ref, acc_ref):
        k = pl.program_id(2)

        @pl.when(k == 0)
        def _():
            acc_ref[...] = jnp.zeros_like(acc_ref)

        bb = b_ref[...]
        if scale != 1.0:
            bb = scale * bb
        acc_ref[...] += lax.dot_general(a_ref[...], bb.astype(BF16), (((0,), (0,)), ((), ())),
                                        preferred_element_type=F32)

        @pl.when(k == nk - 1)
        def _():
            o_ref[...] = acc_ref[...].astype(BF16)

    if sharded:
        assert m == bm
        out_spec = pl.BlockSpec((None, bm, bn), lambda i, j, k: (j, 0, 0))
        out_shape = jax.ShapeDtypeStruct((n // bn, m, bn), BF16)
    else:
        out_spec = pl.BlockSpec((bm, bn), lambda i, j, k: (i, j))
        out_shape = jax.ShapeDtypeStruct((m, n), BF16)
    return pl.pallas_call(
        body, name=name, grid=(m // bm, n // bn, nk),
        in_specs=[pl.BlockSpec((TM, bm), lambda i, j, k: (k, i)), pl.BlockSpec((TM, bn), lambda i, j, k: (k, j))],
        out_specs=out_spec, out_shape=out_shape,
        scratch_shapes=[pltpu.VMEM((bm, bn), F32)],
        compiler_params=_cp(3))(a, b)


def _dgrad_norm(du, w, h, gain, dh_out, name):
    tp, d = h.shape
    s, _, ns = w.shape
    tm = TM_RESIDENT

    def body(du_ref, w_ref, h_ref, g_ref, dho_ref, dhi_ref, dg_ref):
        @pl.when(pl.program_id(0) == 0)
        def _():
            dg_ref[...] = jnp.zeros_like(dg_ref)

        dhn = None
        for k in range(s):
            part = lax.dot_general(du_ref[:, ns * k:ns * (k + 1)], w_ref[k], (((1,), (1,)), ((), ())),
                                   preferred_element_type=F32)
            dhn = part if dhn is None else dhn + part
        dx, xhat = _rmsnorm_bwd(dhn, h_ref[...], g_ref[...])
        dg_ref[...] += jnp.sum(dhn * xhat, axis=0, keepdims=True)
        dhi_ref[...] = dho_ref[...] + dx

    return pl.pallas_call(
        body, name=name, grid=(tp // tm,),
        in_specs=[pl.BlockSpec((tm, s * ns), lambda i: (i, 0)), _resident(w.shape, 1),
                  pl.BlockSpec((tm, d), lambda i: (i, 0)), pl.BlockSpec((1, d), lambda i: (0, 0)),
                  pl.BlockSpec((tm, d), lambda i: (i, 0))],
        out_specs=[pl.BlockSpec((tm, d), lambda i: (i, 0)), pl.BlockSpec((1, d), lambda i: (0, 0))],
        out_shape=[jax.ShapeDtypeStruct((tp, d), F32), jax.ShapeDtypeStruct((1, d), F32)],
        compiler_params=_cp(1))(du, w, h, gain, dh_out)


def _loss_head(act, w_out, h, gain, target, name):
    tp, d = h.shape
    ff = act.shape[1]
    tm = TM_SMALL
    front_tiles = FRONT // tm

    def body(a_ref, w_ref, h_ref, g_ref, t_ref, dh_ref, dg_ref, loss_ref):
        i = pl.program_id(0)

        @pl.when(i == 0)
        def _():
            dg_ref[...] = jnp.zeros_like(dg_ref)
            loss_ref[...] = jnp.zeros_like(loss_ref)

        x = h_ref[...] + 0.5 * jnp.dot(a_ref[...], w_ref[...], preferred_element_type=F32)
        gain_v = g_ref[...]
        y = x * _rms(x) * gain_v
        err = jnp.where(i >= front_tiles, y - t_ref[...], 0.0)
        loss_ref[...] += 0.5 * jnp.sum(jnp.mean(err * err, axis=-1, keepdims=True), axis=0, keepdims=True)
        dy = err * (1.0 / d)
        dx, xhat = _rmsnorm_bwd(dy, x, gain_v)
        dg_ref[...] += jnp.sum(dy * xhat, axis=0, keepdims=True)
        dh_ref[...] = dx

    return pl.pallas_call(
        body, name=name, grid=(tp // tm,),
        in_specs=[pl.BlockSpec((tm, ff), lambda i: (i, 0)), _resident(w_out.shape, 1),
                  pl.BlockSpec((tm, d), lambda i: (i, 0)), pl.BlockSpec((1, d), lambda i: (0, 0)),
                  pl.BlockSpec((tm, d), lambda i: (jnp.maximum(i - front_tiles, 0), 0))],
        out_specs=[pl.BlockSpec((tm, d), lambda i: (i, 0)), pl.BlockSpec((1, d), lambda i: (0, 0)),
                   pl.BlockSpec((1, 128), lambda i: (0, 0))],
        out_shape=[jax.ShapeDtypeStruct((tp, d), F32), jax.ShapeDtypeStruct((1, d), F32),
                   jax.ShapeDtypeStruct((1, 128), F32)],
        compiler_params=_cp(1))(act, w_out, h, gain, target)


@jax.custom_vjp
def _gated_headnorm(o, g, gain):
    return o * _rms(o) * gain * (g * _sigmoid(g))


def _gated_headnorm_bwd(res, dy):
    o, g, gain = res
    r = _rms(o)
    ohat = o * r
    sg = _sigmoid(g)
    silu = g * sg
    dy_ohat = dy * ohat
    d_ohat = dy * gain * silu
    d_o = r * (d_ohat - ohat * jnp.mean(d_ohat * ohat, axis=-1, keepdims=True))
    d_g = dy_ohat * gain * (sg + silu * (1.0 - sg))
    d_gain = jnp.sum(dy_ohat * silu, axis=-2, keepdims=True)
    return d_o, d_g, d_gain


_gated_headnorm.defvjp(lambda o, g, gain: (_gated_headnorm(o, g, gain), (o, g, gain)), _gated_headnorm_bwd)


def _row_mask(chunk, size=CHUNK):
    rows = chunk * size + lax.broadcasted_iota(jnp.int32, (size, 1), 0)
    return (rows >= FRONT - N_META).astype(F32)


def _ret_head(q1, q2, k1, k2, v, g, state, gain, cos, sin, dmat, dq, dk, dc):
    q = jnp.concatenate([q1 * cos - q2 * sin, q1 * sin + q2 * cos], axis=-1)
    k = jnp.concatenate([k1 * cos - k2 * sin, k1 * sin + k2 * cos], axis=-1) * (RET_DK ** -0.5)
    scores = _nt(q, k) * dmat
    o = _nn(scores, v) + _nn(q * dq, state)
    new_state = state * dc + _tn(k * dk, v)
    return _gated_headnorm(o, g, gain), new_state


def _ret_consts():
    log_gamma = jnp.log1p(-2.0 ** (-5.0 - jnp.arange(HEADS, dtype=F32)))
    idx = jnp.arange(RET_CHUNK, dtype=F32)
    rel = idx[:, None] - idx[None, :]
    dmat = jnp.where(rel >= 0, jnp.exp(log_gamma[:, None, None] * jnp.maximum(rel, 0.0)), 0.0)
    dq = jnp.exp(log_gamma[:, None] * (idx + 1.0))[..., None]
    dk = jnp.exp(log_gamma[:, None] * (RET_CHUNK - 1.0 - idx))[..., None]
    dc = jnp.broadcast_to(jnp.exp(log_gamma * RET_CHUNK)[:, None, None], (HEADS, 1, 128))
    return dmat, dq, dk, dc


def _rope_tables(tp):
    half = RET_DK // 2
    inv = 1.0 / (ROPE_BASE ** jnp.linspace(0.0, 1.0, half, dtype=F32))
    pos = (jnp.arange(tp) - (FRONT - N_META)).astype(F32)
    ang = pos[:, None] * inv[None, :]
    return jnp.cos(ang), jnp.sin(ang)


_RET_V0, _RET_G0 = 2 * D, 4 * D


def _heads(ref, start, width, stride=None, rows=slice(None)):
    stride = width if stride is None else stride
    return jnp.stack([ref[rows, start + stride * h:start + stride * h + width].astype(F32) for h in range(HEADS)])


def _put_heads(ref, start, value, mask, stride=None, rows=slice(None)):
    width = value.shape[-1]
    stride = width if stride is None else stride
    for h in range(HEADS):
        ref[rows, start + stride * h:start + stride * h + width] = (value[h] * mask).astype(ref.dtype)


def _ret_pieces(u_ref):
    hk = RET_DK // 2
    return (_heads(u_ref, 0, hk, RET_DK), _heads(u_ref, hk, hk, RET_DK), _heads(u_ref, D, hk, RET_DK),
            _heads(u_ref, D + hk, hk, RET_DK), _heads(u_ref, _RET_V0, RET_DV), _heads(u_ref, _RET_G0, RET_DV))


def _ret_const_specs(rev=None):
    c = (lambda n: (rev(n), 0)) if rev else (lambda n: (n, 0))
    z3 = lambda n: (0, 0, 0)
    return [pl.BlockSpec((RET_CHUNK, RET_DK // 2), c), pl.BlockSpec((RET_CHUNK, RET_DK // 2), c),
            pl.BlockSpec((HEADS, RET_CHUNK, RET_CHUNK), z3), pl.BlockSpec((HEADS, RET_CHUNK, 1), z3),
            pl.BlockSpec((HEADS, RET_CHUNK, 1), z3), pl.BlockSpec((HEADS, 1, 128), z3)]


def _ret_fwd(u, gain, rope, h, w_out, name):
    tp = u.shape[0]
    nch = tp // RET_CHUNK
    cos, sin = rope
    dmat, dq, dk, dc = _ret_consts()

    def body(u_ref, gain_ref, h_ref, w_ref, cos_ref, sin_ref, dmat_ref, dq_ref, dk_ref, dc_ref,
             on_ref, st_ref, hmix_ref, state_ref):
        @pl.when(pl.program_id(0) == 0)
        def _():
            state_ref[...] = jnp.zeros_like(state_ref)

        state = state_ref[...]
        st_ref[...] = state.astype(BF16)
        on, new_state = _ret_head(*_ret_pieces(u_ref), state, _heads(gain_ref, 0, RET_DV), cos_ref[...], sin_ref[...],
                                  dmat_ref[...], dq_ref[...], dk_ref[...], dc_ref[...][:, :, :1])
        state_ref[...] = new_state
        _put_heads(on_ref, 0, on, 1.0)
        hmix_ref[...] = h_ref[...] + jnp.dot(on_ref[...], w_ref[...], preferred_element_type=F32)

    rows = lambda width: pl.BlockSpec((RET_CHUNK, width), lambda n: (n, 0))
    return pl.pallas_call(
        body, name=name, grid=(nch,),
        in_specs=[rows(6 * D), pl.BlockSpec((1, HEADS * RET_DV), lambda n: (0, 0)), rows(D),
                  _resident(w_out.shape, 1)] + _ret_const_specs(),
        out_specs=[rows(HEADS * RET_DV), pl.BlockSpec((None, HEADS, RET_DK, RET_DV), lambda n: (n, 0, 0, 0)), rows(D)],
        out_shape=[jax.ShapeDtypeStruct((tp, HEADS * RET_DV), BF16),
                   jax.ShapeDtypeStruct((nch, HEADS, RET_DK, RET_DV), BF16), jax.ShapeDtypeStruct((tp, D), F32)],
        scratch_shapes=[pltpu.VMEM((HEADS, RET_DK, RET_DV), F32)],
        compiler_params=_cp(1))(u, gain, h, w_out, cos, sin, dmat, dq, dk, dc)


def _ret_bwd(u, gain, rope, states, d_on, name):
    tp = u.shape[0]
    nch = tp // RET_CHUNK
    cos, sin = rope
    dmat, dq, dk, dc = _ret_consts()
    rev = lambda n: nch - 1 - n
    hk = RET_DK // 2

    def body(u_ref, gain_ref, st_ref, don_ref, cos_ref, sin_ref, dmat_ref, dq_ref, dk_ref, dc_ref,
             du_ref, dgain_ref, dstate_ref):
        @pl.when(pl.program_id(0) == 0)
        def _():
            dstate_ref[...] = jnp.zeros_like(dstate_ref)
            dgain_ref[...] = jnp.zeros_like(dgain_ref)

        mask = _row_mask(rev(pl.program_id(0)), RET_CHUNK)
        consts = (cos_ref[...], sin_ref[...], dmat_ref[...], dq_ref[...], dk_ref[...], dc_ref[...][:, :, :1])
        _, vjp = jax.vjp(lambda *a: _ret_head(*a, *consts), *_ret_pieces(u_ref), st_ref[...].astype(F32),
                         _heads(gain_ref, 0, RET_DV))
        dq1, dq2, dk1, dk2, dv, dg, dstate, dgain = vjp((_heads(don_ref, 0, RET_DV), dstate_ref[...]))
        dstate_ref[...] = dstate
        for hd in range(HEADS):
            dgain_ref[:, RET_DV * hd:RET_DV * (hd + 1)] += dgain[hd]
        _put_heads(du_ref, 0, dq1, mask, RET_DK)
        _put_heads(du_ref, hk, dq2, mask, RET_DK)
        _put_heads(du_ref, D, dk1, mask, RET_DK)
        _put_heads(du_ref, D + hk, dk2, mask, RET_DK)
        _put_heads(du_ref, _RET_V0, dv, mask)
        _put_heads(du_ref, _RET_G0, dg, mask)

    return pl.pallas_call(
        body, name=name, grid=(nch,),
        in_specs=[pl.BlockSpec((RET_CHUNK, 6 * D), lambda n: (rev(n), 0)),
                  pl.BlockSpec((1, HEADS * RET_DV), lambda n: (0, 0)),
                  pl.BlockSpec((None, HEADS, RET_DK, RET_DV), lambda n: (rev(n), 0, 0, 0)),
                  pl.BlockSpec((RET_CHUNK, HEADS * RET_DV), lambda n: (rev(n), 0))] + _ret_const_specs(rev),
        out_specs=[pl.BlockSpec((RET_CHUNK, 6 * D), lambda n: (rev(n), 0)),
                   pl.BlockSpec((1, HEADS * RET_DV), lambda n: (0, 0))],
        out_shape=[jax.ShapeDtypeStruct((tp, 6 * D), BF16), jax.ShapeDtypeStruct((1, HEADS * RET_DV), F32)],
        scratch_shapes=[pltpu.VMEM((HEADS, RET_DK, RET_DV), F32)],
        compiler_params=_cp(1))(u, gain, states, d_on, cos, sin, dmat, dq, dk, dc)


_GLA_K0, _GLA_V0, _GLA_G0, _GLA_Z0 = 512, 1024, 2048, 3072


def _gla_head(q, k, v, g, z, state_t, wg, bg, gain, mask, lo, lo_t, to_mid, to_mid_t, in_second, pair):
    ga = _nn(jnp.broadcast_to(z, wg.shape[:-2] + z.shape), wg) + bg
    log_a = (jnp.minimum(ga, 0.0) - jnp.log(1.0 + jnp.exp(-jnp.abs(ga)))) * (mask * (1.0 / GLA_TAU))
    bcum = _cum(lo, lo_t, log_a)
    btot = jnp.sum(log_a, axis=-2, keepdims=True)
    qs = q * (GLA_DK ** -0.5)
    heads, levels = q.shape[0], pair.shape[0]
    decay = jnp.exp(_cum16(to_mid, to_mid_t, log_a).reshape(heads, levels, CHUNK, GLA_DK))
    qk = jnp.where(in_second > 0.0, qs.astype(BF16)[:, None], k.astype(BF16)[:, None]) * decay.astype(BF16)
    qk = qk.reshape(heads * levels, CHUNK, GLA_DK)
    rows = lax.broadcasted_iota(jnp.int32, (CHUNK, CHUNK), 0)
    cols = lax.broadcasted_iota(jnp.int32, (CHUNK, CHUNK), 1)
    scores = (jnp.where(rows == cols, _nt(qs, k), 0.0)
              + jnp.sum(_nt16(qk, qk).reshape(heads, levels, CHUNK, CHUNK) * pair, axis=1))
    o = _nn(scores, v) + _nt(qs * jnp.exp(bcum), state_t)
    new_state_t = state_t * jnp.exp(btot) + _tn(v, k * jnp.exp(btot - bcum))
    return o * _rms(o) * gain * (g * _sigmoid(g)), new_state_t


def _gla_consts():
    r, c = np.meshgrid(np.arange(CHUNK), np.arange(CHUNK), indexing="ij")
    to_mid, second, pair = [], [], []
    block = 2
    while block <= CHUNK:
        mid = (r // block) * block + block // 2
        to_mid.append(((r >= mid) & (c > mid) & (c <= r)) | ((r < mid) & (c > r) & (c <= mid)))
        second.append((r >= mid)[:, :1])
        pair.append((r // block == c // block) & (r >= mid) & (c < mid))
        block *= 2
    to_mid = np.concatenate(to_mid)
    bf = lambda m: jnp.asarray(m, F32).astype(BF16)
    f32 = lambda ms: jnp.asarray(np.stack(ms), F32)
    return bf(r >= c), bf(c >= r), bf(to_mid), bf(to_mid.T), f32(second), f32(pair)


def _gla_const_specs(consts):
    return [pl.BlockSpec(a.shape, functools.partial(lambda nd, n: (0,) * nd, a.ndim)) for a in consts]


GLA_STEP_CHUNKS = 4


def _gla_pieces(u_ref, rows):
    return (_heads(u_ref, 0, GLA_DK, rows=rows), _heads(u_ref, _GLA_K0, GLA_DK, rows=rows),
            _heads(u_ref, _GLA_V0, GLA_DV, rows=rows), _heads(u_ref, _GLA_G0, GLA_DV, rows=rows),
            u_ref[rows, _GLA_Z0:_GLA_Z0 + 128].astype(F32))


def _gla_fwd(u, wg, bg, gain, name):
    tp = u.shape[0]
    nch = tp // CHUNK
    per = GLA_STEP_CHUNKS
    consts = _gla_consts()

    def body(u_ref, wg_ref, bg_ref, gain_ref, *refs):
        const_refs, (on_ref, st_ref, state_ref) = refs[:len(consts)], refs[len(consts):]

        @pl.when(pl.program_id(0) == 0)
        def _():
            state_ref[...] = jnp.zeros_like(state_ref)

        params = (_heads(wg_ref, 0, GLA_DK), _heads(bg_ref, 0, GLA_DK), _heads(gain_ref, 0, GLA_DV))
        mats = [ref[...] for ref in const_refs]
        state = state_ref[...]
        for c in range(per):
            rows = slice(CHUNK * c, CHUNK * (c + 1))
            st_ref[c] = state.astype(BF16)
            on, state = _gla_head(*_gla_pieces(u_ref, rows), state, *params, _row_mask(pl.program_id(0) * per + c), *mats)
            _put_heads(on_ref, 0, on, 1.0, rows=rows)
        state_ref[...] = state

    rows_spec = lambda width: pl.BlockSpec((per * CHUNK, width), lambda n: (n, 0))
    full = lambda r, c: pl.BlockSpec((r, c), lambda n: (0, 0))
    return pl.pallas_call(
        body, name=name, grid=(nch // per,),
        in_specs=[rows_spec(GLA_U), full(128, HEADS * GLA_DK), full(1, HEADS * GLA_DK), full(1, HEADS * GLA_DV)]
                 + _gla_const_specs(consts),
        out_specs=[rows_spec(HEADS * GLA_DV), pl.BlockSpec((per, HEADS, GLA_DV, GLA_DK), lambda n: (n, 0, 0, 0))],
        out_shape=[jax.ShapeDtypeStruct((tp, HEADS * GLA_DV), BF16),
                   jax.ShapeDtypeStruct((nch, HEADS, GLA_DV, GLA_DK), BF16)],
        scratch_shapes=[pltpu.VMEM((HEADS, GLA_DV, GLA_DK), F32)],
        compiler_params=_cp(1))(u, wg, bg, gain, *consts)


def _gla_bwd(u, wg, bg, gain, states, d_on, name):
    tp = u.shape[0]
    per = GLA_STEP_CHUNKS
    steps = tp // (per * CHUNK)
    rev = lambda n: steps - 1 - n
    consts = _gla_consts()

    def body(u_ref, wg_ref, bg_ref, gain_ref, st_ref, don_ref, *refs):
        const_refs, (du_ref, dwg_ref, dbg_ref, dgain_ref, dstate_ref) = refs[:len(consts)], refs[len(consts):]

        @pl.when(pl.program_id(0) == 0)
        def _():
            dstate_ref[...] = jnp.zeros_like(dstate_ref)
            dwg_ref[...] = jnp.zeros_like(dwg_ref)
            dbg_ref[...] = jnp.zeros_like(dbg_ref)
            dgain_ref[...] = jnp.zeros_like(dgain_ref)

        params = (_heads(wg_ref, 0, GLA_DK), _heads(bg_ref, 0, GLA_DK), _heads(gain_ref, 0, GLA_DV))
        mats = [ref[...] for ref in const_refs]
        dstate = dstate_ref[...]
        for c in reversed(range(per)):
            rows = slice(CHUNK * c, CHUNK * (c + 1))
            mask = _row_mask(rev(pl.program_id(0)) * per + c)
            _, vjp = jax.vjp(lambda *a: _gla_head(*a, mask, *mats), *_gla_pieces(u_ref, rows),
                             st_ref[c].astype(F32), *params)
            dq, dk, dv, dg, dz, dstate, dwg, dbg, dgain = vjp((_heads(don_ref, 0, GLA_DV, rows=rows), dstate))
            for hd in range(HEADS):
                dwg_ref[:, GLA_DK * hd:GLA_DK * (hd + 1)] += dwg[hd]
                dbg_ref[:, GLA_DK * hd:GLA_DK * (hd + 1)] += dbg[hd]
                dgain_ref[:, GLA_DV * hd:GLA_DV * (hd + 1)] += dgain[hd]
            _put_heads(du_ref, 0, dq, mask, rows=rows)
            _put_heads(du_ref, _GLA_K0, dk, mask, rows=rows)
            _put_heads(du_ref, _GLA_V0, dv, mask, rows=rows)
            _put_heads(du_ref, _GLA_G0, dg, mask, rows=rows)
            du_ref[rows, _GLA_Z0:_GLA_Z0 + 128] = dz.astype(BF16)
            du_ref[rows, _GLA_Z0 + 128:] = jnp.zeros((CHUNK, GLA_U - _GLA_Z0 - 128), BF16)
        dstate_ref[...] = dstate

    full = lambda r, c: pl.BlockSpec((r, c), lambda n: (0, 0))
    return pl.pallas_call(
        body, name=name, grid=(steps,),
        in_specs=[pl.BlockSpec((per * CHUNK, GLA_U), lambda n: (rev(n), 0)), full(128, HEADS * GLA_DK),
                  full(1, HEADS * GLA_DK), full(1, HEADS * GLA_DV),
                  pl.BlockSpec((per, HEADS, GLA_DV, GLA_DK), lambda n: (rev(n), 0, 0, 0)),
                  pl.BlockSpec((per * CHUNK, HEADS * GLA_DV), lambda n: (rev(n), 0))] + _gla_const_specs(consts),
        out_specs=[pl.BlockSpec((per * CHUNK, GLA_U), lambda n: (rev(n), 0)), full(128, HEADS * GLA_DK),
                   full(1, HEADS * GLA_DK), full(1, HEADS * GLA_DV)],
        out_shape=[jax.ShapeDtypeStruct((tp, GLA_U), BF16), jax.ShapeDtypeStruct((128, HEADS * GLA_DK), F32),
                   jax.ShapeDtypeStruct((1, HEADS * GLA_DK), F32), jax.ShapeDtypeStruct((1, HEADS * GLA_DV), F32)],
        scratch_shapes=[pltpu.VMEM((HEADS, GLA_DV, GLA_DK), F32)],
        compiler_params=_cp(1))(u, wg, bg, gain, states, d_on, *consts)


def _ffn_fwd(h, gain, w_in, w_out, tag):
    hn, ug, uu, act = _norm_ffn_in(h, gain, w_in, f"{tag}_in")
    if callable(w_out):
        w_out = w_out(act)
    return _out_proj(act, w_out, h, 0.5, f"{tag}_out"), (h, hn, ug, uu, act), w_out


def _ffn_dgrad(dh, w_out, w_in, act_dg, act_du, h, gain, name, split_front=False):
    tp, d = dh.shape
    ff = w_out.shape[0]
    tm = TM_SMALL
    nt = (((1,), (1,)), ((), ()))

    def body(dh_ref, wo_ref, wi_ref, dg_ref, du_ref, h_ref, g_ref, o_ref, *out_refs):
        dhi_ref, dgain_ref = out_refs[-2:]

        @pl.when(pl.program_id(0) == 0)
        def _():
            dgain_ref[...] = jnp.zeros_like(dgain_ref)

        dho = dh_ref[...]
        dact = lax.dot_general((0.5 * dho).astype(BF16), wo_ref[...], nt, preferred_element_type=F32)
        d_gate = (dact * dg_ref[...].astype(F32)).astype(BF16)
        d_up = (dact * du_ref[...].astype(F32)).astype(BF16)
        o_ref[:, :ff] = d_gate
        o_ref[:, ff:] = d_up
        dhn = (lax.dot_general(d_gate, wi_ref[:, :ff], nt, preferred_element_type=F32)
               + lax.dot_general(d_up, wi_ref[:, ff:], nt, preferred_element_type=F32))
        dx, xhat = _rmsnorm_bwd(dhn, h_ref[...], g_ref[...])
        dgain_ref[...] += jnp.sum(dhn * xhat, axis=0, keepdims=True)
        dhi_ref[...] = dho + dx
        if split_front:
            @pl.when(pl.program_id(0) == 0)
            def _():
                out_refs[0][...] = dho + dx

    rows = lambda width: pl.BlockSpec((tm, width), lambda i: (i, 0))
    if split_front:
        assert tm == FRONT
        dhi_specs = [pl.BlockSpec((tm, d), lambda i: (0, 0)), pl.BlockSpec((tm, d), lambda i: (jnp.maximum(i - 1, 0), 0))]
        dhi_shapes = [jax.ShapeDtypeStruct((FRONT, d), F32), jax.ShapeDtypeStruct((tp - FRONT, d), F32)]
    else:
        dhi_specs, dhi_shapes = [rows(d)], [jax.ShapeDtypeStruct((tp, d), F32)]
    out = pl.pallas_call(
        body, name=name, grid=(tp // tm,),
        in_specs=[rows(d), _resident(w_out.shape, 1), _resident(w_in.shape, 1), rows(ff), rows(ff), rows(d),
                  pl.BlockSpec((1, d), lambda i: (0, 0))],
        out_specs=[rows(2 * ff), *dhi_specs, pl.BlockSpec((1, d), lambda i: (0, 0))],
        out_shape=[jax.ShapeDtypeStruct((tp, 2 * ff), BF16), *dhi_shapes, jax.ShapeDtypeStruct((1, d), F32)],
        compiler_params=_cp(1))(dh, w_out, w_in, act_dg, act_du, h, gain)
    return (out[0], tuple(out[1:3]), out[3]) if split_front else tuple(out)


def _ffn_bwd(dh, saved, gain, w_in, w_out, tag, push, split_front=False):
    h, hn, act_dg, act_du, act = saved
    du, dh_in, d_gain = _ffn_dgrad(dh, w_out, w_in, act_dg, act_du, h, gain, f"{tag}_dgrad", split_front)
    d_w_out = _wgrad(act, dh, bm=D_FF // 2, bn=D, scale=0.5, sharded=False, name=f"{tag}_dwout")
    d_w_in = _wgrad(hn, du, bm=D, bn=D_FF, scale=1.0, sharded=False, name=f"{tag}_dwin")
    return dh_in, d_gain, push([("cols", d_w_in), d_w_out])


def _sequence_grads(x, target, p, weights, grads):
    row = lambda v, token: v.reshape(1, -1) + token[0, 0]
    gains = {}

    tok = weights.start(1, weights.start(0, None))
    weights.pin = tok
    h = jnp.concatenate([jnp.zeros((FRONT, D), F32), x], axis=0) + tok[0, 0]
    rope = _rope_tables(h.shape[0])
    w = weights.wait(0, [tok, h, *rope, *weights.later_shards(2)])
    tok = weights.start(2, w["l0_ffn1_in"])
    h = lax.dynamic_update_slice(h, w["meta"], (FRONT - N_META, 0))
    gains["l0_ffn1"] = row(p["norm_ffn1"][0], tok)
    h, s1, w["l0_ffn1_out"] = _ffn_fwd(h, gains["l0_ffn1"], w["l0_ffn1_in"],
                                       lambda act: weights.wait(1, act)["l0_ffn1_out"], "l0_ffn1")
    w.update(weights.wait(2, h))
    tok = weights.start(4, weights.start(3, w["ret_in"]))
    gains["ret"] = row(p["norm_mix"][0], tok)
    hn, u = _norm_proj(h, gains["ret"], w["ret_in"], "ret_in")
    w.update(weights.wait(3, u))
    on, states, h_mix = _ret_fwd(u, w["ret_gain"], rope, h, w["ret_out"], "ret_fwd")
    s2 = (h, hn, u, on, states)
    w.update(weights.wait(4, h_mix))
    tok = weights.start(5, w["l0_ffn2_in"])
    gains["l0_ffn2"] = row(p["norm_ffn2"][0], tok)
    h, s3, _ = _ffn_fwd(h_mix, gains["l0_ffn2"], w["l0_ffn2_in"], w["l0_ffn2_out"], "l0_ffn2")
    saved = [(s1, s2, s3)]

    w.update(weights.wait(5, h))
    tok = weights.start(6, w["l1_ffn1_in"])
    gains["l1_ffn1"] = row(p["norm_ffn1"][1], tok)
    h, s1, _ = _ffn_fwd(h, gains["l1_ffn1"], w["l1_ffn1_in"], w["l1_ffn1_out"], "l1_ffn1")
    w.update(weights.wait(6, h))
    tok = weights.start(7, w["gla_out"])
    gains["gla"] = row(p["norm_mix"][1], tok)
    hn, u = _norm_proj(h, gains["gla"], w["gla_in"], "gla_in")
    on, states = _gla_fwd(u, w["gla_wg"], w["gla_bg"], w["gla_gain"], "gla_fwd")
    h_mix = _out_proj(on, w["gla_out"], h, 1.0, "gla_out")
    s2 = (h, hn, u, on, states)
    w.update(weights.wait(7, h_mix))
    gains["l1_ffn2"] = p["norm_ffn2"][1].reshape(1, -1)
    s3 = (h_mix, *_norm_ffn_in(h_mix, gains["l1_ffn2"], w["l1_ffn2_in"], "l1_ffn2_in"))
    saved.append((s1, s2, s3))

    dh, d_final, loss = _loss_head(s3[-1], w["l1_ffn2_out"], h_mix, p["final_norm"].reshape(1, -1), target,
                                   "l1_ffn2_out_loss")
    small = {"final_norm": d_final, "norm_ffn1": [None, None], "norm_mix": [None, None], "norm_ffn2": [None, None]}
    pusher = lambda k: functools.partial(grads.push, k)

    s1, s2, s3 = saved[1]
    dh, small["norm_ffn2"][1], tok = _ffn_bwd(dh, s3, gains["l1_ffn2"], w["l1_ffn2_in"], w["l1_ffn2_out"], "l1_ffn2",
                                              pusher(0))
    h_in, hn, u, on, states = s2
    d_on, d_out = _out_proj_bwd(dh, w["gla_out"], on, "gla_out_bwd")
    du, small["gla_wg"], small["gla_bg"], small["gla_gain"] = _gla_bwd(
        u, w["gla_wg"], w["gla_bg"], w["gla_gain"] + tok[0, 0], states, d_on, "gla_bwd")
    d_in = _wgrad(hn, du, bm=D, bn=GLA_U, scale=1.0, sharded=False, name="gla_dwin")
    d_in = jnp.moveaxis(d_in[:, :GLA_IN].reshape(D, N_CHIPS, -1), 1, 0)
    tok = grads.push(1, [d_in, d_out])
    dh, small["norm_mix"][1] = _dgrad_norm(du, w["gla_in"], h_in, gains["gla"] + tok[0, 0], dh, "gla_dnorm")
    dh, small["norm_ffn1"][1], tok = _ffn_bwd(dh, s1, gains["l1_ffn1"], w["l1_ffn1_in"], w["l1_ffn1_out"], "l1_ffn1",
                                              pusher(2))

    s1, s2, s3 = saved[0]
    dh, small["norm_ffn2"][0], tok = _ffn_bwd(dh, s3, gains["l0_ffn2"] + tok[0, 0], w["l0_ffn2_in"],
                                              w["l0_ffn2_out"], "l0_ffn2", pusher(3))
    h_in, hn, u, on, states = s2
    d_on, d_out = _out_proj_bwd(dh, w["ret_out"], on, "ret_out_bwd")
    du, small["ret_gain"] = _ret_bwd(u, w["ret_gain"] + tok[0, 0], rope, states, d_on, "ret_bwd")
    d_in = _wgrad(hn, du, bm=D, bn=w["ret_in"].shape[2], scale=1.0, sharded=True, name="ret_dwin")
    tok = grads.push(4, [d_in, d_out])
    dh, small["norm_mix"][0] = _dgrad_norm(du, w["ret_in"], h_in, gains["ret"] + tok[0, 0], dh, "ret_dnorm")
    (d_front, d_x), small["norm_ffn1"][0], tok = _ffn_bwd(dh, s1, gains["l0_ffn1"], w["l0_ffn1_in"], w["l0_ffn1_out"],
                                                          "l0_ffn1", pusher(5), split_front=True)
    grads.push(6, [], [d_front[FRONT - N_META:], *small["norm_ffn1"], *small["norm_mix"], *small["norm_ffn2"],
                       small["final_norm"], small["ret_gain"], small["gla_wg"][:GLA_RANK], small["gla_bg"],
                       small["gla_gain"], loss[:, :1] + tok[0, 0]])
    return d_x


_HBM = pl.BlockSpec(memory_space=pl.ANY)


def _place():
    return lax.axis_index("x"), lax.axis_index("y"), lax.axis_index("c")


def _flip(v, bit):
    return 1 - v if bit else v


DMA_CHUNK_BYTES = 128 * 1024


def _row_chunks(ref):
    rows, cols = ref.shape
    step = _row_tile(rows, max(16, DMA_CHUNK_BYTES // (cols * ref.dtype.itemsize)))
    return [pl.ds(a, step) for a in range(0, rows, step)]


def _whole(src, dst, send_sem, recv_sem, peer):
    return pltpu.make_async_remote_copy(src_ref=src, dst_ref=dst, send_sem=send_sem, recv_sem=recv_sem,
                                        device_id=peer, device_id_type=MESH)


def _send(src, dst, send_sem, recv_sem, peer):
    for rows in _row_chunks(src):
        _whole(src.at[rows], dst.at[rows], send_sem, recv_sem, peer).start()
    return _whole(src, dst, send_sem, recv_sem, peer)


_HBM_ONLY = pl.BlockSpec(memory_space=pltpu.HBM)
_SEMS = pl.BlockSpec(memory_space=pltpu.SEMAPHORE)
_SIDE_EFFECT = pltpu.CompilerParams(has_side_effects=pltpu.SideEffectType.DATAFLOW_SIDE_EFFECTING)
_GATHER_FLIPS = [(1, 0, 0), (0, 1, 0), (1, 1, 0), (0, 0, 1)]
_PEER_FLIPS = [(fx, fy, fc) for fx in (0, 1) for fy in (0, 1) for fc in (0, 1)][1:]


def _zero_token():
    return jnp.zeros((8, 128), F32)


def _exchange_start(srcs, lands, route, flips, after, name):
    n = len(srcs)

    def body(*refs):
        src, land = refs[:n], refs[n:2 * n]
        send_sems, recv_sems, token = refs[2 * n + 1], refs[2 * n + 2], refs[-1]
        me = _place()
        for t in range(n):
            for j, flip in enumerate(flips):
                peer = tuple(_flip(v, f) for v, f in zip(me, flip))
                s, d = route(t, src[t], land[t], me, peer)
                _send(s, d, send_sems.at[t * len(flips) + j], recv_sems.at[t * len(flips) + j], peer)
        token[...] = jnp.zeros_like(token)

    hbm = lambda a: pltpu.HBM(a.shape, a.dtype)
    sems = pltpu.SemaphoreType.DMA((n * len(flips),))
    operands = [pltpu.with_memory_space_constraint(a, pltpu.HBM) for a in list(srcs) + list(lands)]
    out = pl.pallas_call(
        body, name=name, in_specs=[_HBM_ONLY] * (2 * n) + [_HBM],
        out_shape=(sems, sems, *[hbm(a) for a in operands], jax.ShapeDtypeStruct((8, 128), F32)),
        out_specs=(_SEMS, _SEMS, *[_HBM_ONLY] * (2 * n), pl.BlockSpec(memory_space=pltpu.VMEM)),
        input_output_aliases={i: 2 + i for i in range(2 * n)}, compiler_params=_SIDE_EFFECT,
    )(*operands, _zero_token() if after is None else after)
    return (out[0], out[1], out[2:2 + n], out[2 + n:2 + 2 * n]), out[-1]


def _exchange_wait(started, route, flips, after, name):
    send_sems, recv_sems, srcs, lands = started
    n = len(srcs)

    def body(*refs):
        src, land = refs[:n], refs[n:2 * n]
        send_sems, recv_sems = refs[2 * n], refs[2 * n + 1]
        me = _place()
        for t in range(n):
            for j, flip in enumerate(flips):
                peer = tuple(_flip(v, f) for v, f in zip(me, flip))
                s, d = route(t, src[t], land[t], me, peer)
                cp = _whole(s, d, send_sems.at[t * len(flips) + j], recv_sems.at[t * len(flips) + j], peer)
                cp.wait_send()
                cp.wait_recv()

    hbm = lambda a: pltpu.HBM(a.shape, a.dtype)
    after = list(after) if isinstance(after, (list, tuple)) else [after]
    out = pl.pallas_call(
        body, name=name, in_specs=[_HBM_ONLY] * (2 * n) + [_SEMS, _SEMS] + [_HBM] * len(after),
        out_shape=tuple(hbm(a) for a in list(srcs) + list(lands)), out_specs=tuple([_HBM_ONLY] * (2 * n)),
        input_output_aliases={i: i for i in range(2 * n)}, compiler_params=_SIDE_EFFECT,
    )(*srcs, *lands, send_sems, recv_sems, *after)
    return out[:n], out[n:]


def _gather_route(t, src, land, me, peer):
    mine = 2 * me[0] + me[1]
    if land.ndim == 3:
        return src, land.at[mine]
    cols = src.shape[1]
    return src, land.at[:, pl.ds(pl.multiple_of(mine * cols, 128), cols)]


def _scatter_route(n_pieces):
    def route(t, src, land, me, peer):
        chip = 2 * peer[0] + peer[1]
        if t >= n_pieces:
            part = src
        elif src.ndim == 4:
            part = src.at[chip, peer[2]]
        else:
            rows, cols = land.shape[1:]
            part = src.at[pl.ds(pl.multiple_of(peer[2] * rows, 16), rows), pl.ds(pl.multiple_of(chip * cols, 128), cols)]
        return part, land.at[4 * me[0] + 2 * me[1] + me[2]]

    return route


def _swap_cores(halves, name):
    n = len(halves)

    def body(*refs):
        src, dst = refs[:n], refs[n:2 * n]
        send_sems, recv_sems = refs[2 * n:]
        x, y, c = _place()
        copies = [_send(src[t], dst[t], send_sems.at[t], recv_sems.at[t], (x, y, 1 - c)) for t in range(n)]
        for cp in copies:
            cp.wait()

    got = pl.pallas_call(
        body, name=name, in_specs=[_HBM] * n, out_specs=[_HBM] * n,
        out_shape=[jax.ShapeDtypeStruct(a.shape, a.dtype) for a in halves],
        scratch_shapes=[pltpu.SemaphoreType.DMA((n,)), pltpu.SemaphoreType.DMA((n,))],
    )(*halves)
    south = lax.axis_index("c") == 0
    return [jnp.stack([jnp.where(south, a, b), jnp.where(south, b, a)]) for a, b in zip(halves, got)]


def _row_tile(rows, cap):
    fits = [t for t in range(16, cap + 1, 16) if rows % t == 0]
    return fits[-1] if fits else rows


def _sum_slots(a, name):
    _, r, c = a.shape
    tr = _row_tile(r, 384)

    def body(a_ref, o_ref):
        s = a_ref[0].astype(F32)
        for k in range(1, N_DEV):
            s = s + a_ref[k].astype(F32)
        o_ref[...] = s

    return pl.pallas_call(
        body, name=name, grid=(r // tr,),
        in_specs=[pl.BlockSpec((N_DEV, tr, c), lambda i: (0, i, 0))],
        out_specs=pl.BlockSpec((tr, c), lambda i: (i, 0)),
        out_shape=jax.ShapeDtypeStruct((r, c), F32),
        compiler_params=_cp(1))(a)


def _adamw(w, g, m, v, name):
    layers, r, c = w.shape
    tr = _row_tile(r, 256)

    def body(w_ref, g_ref, m_ref, v_ref, d_ref, nm_ref, nv_ref):
        gv = g_ref[...]
        nm = ADAM_B1 * m_ref[...] + (1.0 - ADAM_B1) * gv
        nv = ADAM_B2 * v_ref[...] + (1.0 - ADAM_B2) * (gv * gv)
        m_hat = nm / (1.0 - ADAM_B1 ** ADAM_STEP)
        v_hat = nv / (1.0 - ADAM_B2 ** ADAM_STEP)
        d_ref[...] = -ADAM_LR * (m_hat / (jnp.sqrt(v_hat) + ADAM_EPS) + ADAM_WD * w_ref[...])
        nm_ref[...] = nm
        nv_ref[...] = nv

    spec = pl.BlockSpec((None, tr, c), lambda a, i: (a, i, 0))
    return pl.pallas_call(
        body, name=name, grid=(layers, r // tr), in_specs=[spec] * 4, out_specs=[spec] * 3,
        out_shape=[jax.ShapeDtypeStruct((layers, r, c), F32)] * 3,
        compiler_params=_cp(2))(*[pltpu.with_memory_space_constraint(a, pltpu.HBM) for a in (w, g, m, v)])


_SMALL = ["meta_tokens", "ret_head_norm", "gla_w_gate", "gla_b_gate", "gla_head_norm"]
_LOCAL_SMALL = ["meta_tokens", "norm_ffn1", "norm_mix", "norm_ffn2", "ret_head_norm", "gla_w_gate", "gla_b_gate",
                "gla_head_norm", "final_norm"]
_WEIGHTS = ["meta_tokens", "norm_ffn1", "ffn1_w_in", "ffn1_w_out", "norm_mix", "norm_ffn2", "ffn2_w_in", "ffn2_w_out",
            "ret_w_in", "ret_head_norm", "ret_w_out", "gla_w_in", "gla_w_gate", "gla_b_gate", "gla_head_norm",
            "gla_w_out", "final_norm"]


def _pack_rows(arrays, width):
    flat = jnp.concatenate([a.reshape(-1) for a in arrays])
    pad = -flat.shape[0] % (8 * width)
    return jnp.pad(flat, (0, pad)).reshape(-1, width)


def _unpack_rows(packed, shapes):
    flat, out, at = packed.reshape(-1), [], 0
    for s in shapes:
        size = 1
        for dim in s:
            size *= dim
        out.append(flat[at:at + size].reshape(s))
        at += size
    return out


class _WeightGather:
    GROUPS = [("small", "l0_ffn1_in"), ("l0_ffn1_out",), ("ret_in",), ("ret_out",), ("l0_ffn2_in", "l0_ffn2_out"),
              ("l1_ffn1_in", "l1_ffn1_out"), ("gla_in", "gla_out"), ("l1_ffn2_in", "l1_ffn2_out")]

    def __init__(self, p):
        self.small_shapes = [p[name].shape for name in _SMALL]
        self.f32 = {"small": _pack_rows([p[name] for name in _SMALL], 128), "ret_in": p["ret_w_in"][0],
                    "ret_out": p["ret_w_out"][0], "gla_in": p["gla_w_in"][0], "gla_out": p["gla_w_out"][0]}
        for layer in range(2):
            for name in ("ffn1", "ffn2"):
                self.f32[f"l{layer}_{name}_in"] = p[f"{name}_w_in"][layer]
                self.f32[f"l{layer}_{name}_out"] = p[f"{name}_w_out"][layer]
        self.shards = {}
        self.started = {}
        self.pin = None

    def shard(self, name):
        if name not in self.shards:
            a = self.f32[name]
            if name != "small":
                a = (a if self.pin is None else a + self.pin[0, 0]).astype(BF16)
            self.shards[name] = a
        return self.shards[name]

    def later_shards(self, k):
        return [self.shard(name) for group in self.GROUPS[k:] for name in group]

    def start(self, k, after):
        shards = [self.shard(name) for name in self.GROUPS[k]]
        lands = []
        for name, s in zip(self.GROUPS[k], shards):
            if "ffn" in name and name.endswith("_in"):
                lands.append(lax.empty((s.shape[0], N_CHIPS * s.shape[1]), s.dtype))
            else:
                lands.append(lax.empty((N_CHIPS,) + s.shape, s.dtype))
        self.started[k], token = _exchange_start(shards, lands, _gather_route, _GATHER_FLIPS, after, f"gather{k}_start")
        return token

    def wait(self, k, after):
        _, got = _exchange_wait(self.started[k], _gather_route, _GATHER_FLIPS, after, f"gather{k}_wait")
        w = {}
        for name, g in zip(self.GROUPS[k], got):
            if name == "small":
                parts = zip(*[_unpack_rows(g[chip], self.small_shapes) for chip in range(N_CHIPS)])
                cat = lambda a: jnp.moveaxis(a, 0, -2).reshape(a.shape[1:-1] + (-1,))
                meta, ret_gain, wg, bg, gla_gain = [cat(jnp.stack(part)) for part in parts]
                w.update(meta=meta, ret_gain=ret_gain.reshape(1, -1), gla_bg=bg.reshape(1, -1),
                         gla_gain=gla_gain.reshape(1, -1),
                         gla_wg=jnp.pad(wg[0], ((0, 128 - GLA_RANK), (0, 0))).astype(BF16))
            elif name == "gla_in":
                full = jnp.moveaxis(g, 0, 1).reshape(D, -1)
                w[name] = jnp.pad(full, ((0, 0), (0, GLA_U - GLA_IN)))[None]
            elif name.endswith("_out"):
                w[name] = g.reshape(-1, g.shape[-1])
            else:
                w[name] = g
        return w


class _GradExchange:
    def __init__(self):
        self.started = []
        self.token = None
        self.small_shapes = None

    def push(self, k, arrays, small=None):
        srcs, lands = [], []
        for a in arrays:
            if isinstance(a, tuple):
                a = a[1]
                piece = (a.shape[0] // 2, a.shape[1] // N_CHIPS)
            else:
                a = a.reshape(N_CHIPS, 2, -1, a.shape[-1])
                piece = a.shape[2:]
            srcs.append(a)
            lands.append(lax.empty((N_DEV,) + piece, a.dtype))
        if small is not None:
            self.small_shapes = [a.shape for a in small]
            srcs.append(_pack_rows(small, D))
            lands.append(lax.empty((N_DEV,) + srcs[-1].shape, F32))
        started, self.token = _exchange_start(srcs, lands, _scatter_route(len(arrays)), _PEER_FLIPS, None,
                                              f"scatter{k}_start")
        self.started.append((started, len(arrays)))
        return self.token

    def collect(self, groups, after=None):
        x, y, c = _place()
        after, sums = self.token if after is None else after, []
        for k in groups:
            started, n_pieces = self.started[k]
            srcs, got = _exchange_wait(started, _scatter_route(n_pieces), _PEER_FLIPS, after, f"scatter{k}_wait")
            own = []
            for t, (a, g) in enumerate(zip(srcs, got)):
                if t >= n_pieces:
                    own.append(a)
                elif a.ndim == 4:
                    own.append(a[2 * x + y, c])
                else:
                    rows, cols = g.shape[1:]
                    own.append(lax.dynamic_slice(a, (c * rows, (2 * x + y) * cols), (rows, cols)))
            got = [lax.dynamic_update_index_in_dim(g, a, 4 * x + 2 * y + c, 0) for g, a in zip(got, own)]
            sums.append([_sum_slots(a, f"sum{k}_{i}") for i, a in enumerate(got)])
            after = sums[-1][0]
        return sums


def kernel(x, meta_tokens, norm_ffn1, ffn1_w_in, ffn1_w_out, norm_mix, norm_ffn2, ffn2_w_in, ffn2_w_out, ret_w_in, ret_head_norm, ret_w_out, gla_w_in, gla_w_gate, gla_b_gate, gla_head_norm, gla_w_out, final_norm, loss_target, m_meta_tokens, m_norm_ffn1, m_ffn1_w_in, m_ffn1_w_out, m_norm_mix, m_norm_ffn2, m_ffn2_w_in, m_ffn2_w_out, m_ret_w_in, m_ret_head_norm, m_ret_w_out, m_gla_w_in, m_gla_w_gate, m_gla_b_gate, m_gla_head_norm, m_gla_w_out, m_final_norm, v_meta_tokens, v_norm_ffn1, v_ffn1_w_in, v_ffn1_w_out, v_norm_mix, v_norm_ffn2, v_ffn2_w_in, v_ffn2_w_out, v_ret_w_in, v_ret_head_norm, v_ret_w_out, v_gla_w_in, v_gla_w_gate, v_gla_b_gate, v_gla_head_norm, v_gla_w_out, v_final_norm):
    p = dict(meta_tokens=meta_tokens, norm_ffn1=norm_ffn1, ffn1_w_in=ffn1_w_in, ffn1_w_out=ffn1_w_out, norm_mix=norm_mix,
             norm_ffn2=norm_ffn2, ffn2_w_in=ffn2_w_in, ffn2_w_out=ffn2_w_out, ret_w_in=ret_w_in,
             ret_head_norm=ret_head_norm, ret_w_out=ret_w_out, gla_w_in=gla_w_in, gla_w_gate=gla_w_gate,
             gla_b_gate=gla_b_gate, gla_head_norm=gla_head_norm, gla_w_out=gla_w_out, final_norm=final_norm)
    m = dict(zip(_WEIGHTS, (m_meta_tokens, m_norm_ffn1, m_ffn1_w_in, m_ffn1_w_out, m_norm_mix, m_norm_ffn2, m_ffn2_w_in,
                            m_ffn2_w_out, m_ret_w_in, m_ret_head_norm, m_ret_w_out, m_gla_w_in, m_gla_w_gate,
                            m_gla_b_gate, m_gla_head_norm, m_gla_w_out, m_final_norm)))
    v = dict(zip(_WEIGHTS, (v_meta_tokens, v_norm_ffn1, v_ffn1_w_in, v_ffn1_w_out, v_norm_mix, v_norm_ffn2, v_ffn2_w_in,
                            v_ffn2_w_out, v_ret_w_in, v_ret_head_norm, v_ret_w_out, v_gla_w_in, v_gla_w_gate,
                            v_gla_b_gate, v_gla_head_norm, v_gla_w_out, v_final_norm)))

    exchange = _GradExchange()
    d_x = _sequence_grads(x[0], loss_target[0], p, _WeightGather(p), exchange)
    names = [("ffn2_w_in", 1), ("ffn2_w_out", 1), ("gla_w_in", 0), ("gla_w_out", 0), ("ffn1_w_in", 1), ("ffn1_w_out", 1),
             ("ffn2_w_in", 0), ("ffn2_w_out", 0), ("ret_w_in", 0), ("ret_w_out", 0), ("ffn1_w_in", 0), ("ffn1_w_out", 0)]
    shard, grads, delta, new_m, new_v = {}, {}, {}, {}, {}

    def swap(sums, keys, name):
        for key, a in zip(keys, _swap_cores(sums, name)):
            shard[key] = a.reshape(-1, a.shape[-1])

    def update(name):
        layers = p[name].shape[0]
        grads[name] = jnp.stack([shard[name, layer] for layer in range(layers)])
        delta[name], new_m[name], new_v[name] = _adamw(p[name], grads[name], m[name], v[name], f"adamw_{name}")

    swap([a for group in exchange.collect(range(5)) for a in group], names[:10], "swap_first")
    for name in ("ffn2_w_in", "ffn2_w_out", "ret_w_in", "ret_w_out", "gla_w_in", "gla_w_out"):
        update(name)
    last, (small_sum,) = exchange.collect([5, 6], after=list(delta.values()))
    swap(last, names[10:], "swap_last")
    for name in ("ffn1_w_in", "ffn1_w_out"):
        update(name)

    chip = 2 * lax.axis_index("x") + lax.axis_index("y")
    cols = lambda a, n: lax.dynamic_slice_in_dim(a, chip * n, n, axis=a.ndim - 1)
    (s_meta, s_n1a, s_n1b, s_nma, s_nmb, s_n2a, s_n2b, s_final, s_ret_gain, s_wg, s_bg, s_gla_gain,
     s_loss) = _unpack_rows(small_sum, exchange.small_shapes)
    grads.update({
        "meta_tokens": cols(s_meta, 256), "norm_ffn1": jnp.concatenate([s_n1a, s_n1b]),
        "norm_mix": jnp.concatenate([s_nma, s_nmb]), "norm_ffn2": jnp.concatenate([s_n2a, s_n2b]),
        "final_norm": s_final.reshape(D),
        "ret_head_norm": cols(s_ret_gain.reshape(1, HEADS, RET_DV), RET_DV // N_CHIPS),
        "gla_w_gate": cols(s_wg, GLA_DK)[None], "gla_b_gate": cols(s_bg, GLA_DK),
        "gla_head_norm": cols(s_gla_gain.reshape(1, HEADS, GLA_DV), GLA_DV // N_CHIPS),
    })
    for name in _LOCAL_SMALL:
        shape = p[name].shape
        as3d = lambda a: a.reshape((1,) * (3 - len(shape)) + shape)
        out = _adamw(as3d(p[name]), as3d(grads[name]), as3d(m[name]), as3d(v[name]), f"adamw_{name}")
        delta[name], new_m[name], new_v[name] = [a.reshape(shape) for a in out]

    return (s_loss.reshape(()), d_x[None], *[grads[n] for n in _WEIGHTS], *[delta[n] for n in _WEIGHTS],
            *[new_m[n] for n in _WEIGHTS], *[new_v[n] for n in _WEIGHTS])
```

```python
import functools

import jax
import numpy as np
import jax.numpy as jnp
from jax import lax
from jax.experimental import pallas as pl
from jax.experimental.pallas import tpu as pltpu

F32, BF16 = jnp.float32, jnp.bfloat16
MESH = pl.DeviceIdType.MESH

D = 1024
N_META = 16
CHUNK = 64
RET_CHUNK = 256
FRONT = 256
D_FF = 2816
EPS = 1e-6
HEADS = 4
RET_DK, RET_DV = 256, 512
GLA_DK, GLA_DV = 128, 256
GLA_RANK = 16
GLA_TAU = 16.0
GLA_IN = 2 * HEADS * GLA_DK + 2 * HEADS * GLA_DV + GLA_RANK
GLA_U = 3328
ROPE_BASE = 10000.0
N_CHIPS = 4
N_DEV = 8

ADAM_LR, ADAM_B1, ADAM_B2, ADAM_EPS, ADAM_WD, ADAM_STEP = 0.001, 0.9, 0.999, 1e-08, 0.01, 10

VMEM_LIMIT_BYTES = 56 * 1024 * 1024
TM = 768
TM_SMALL = 256


TM_RESIDENT = 384
MXU_TILE = 256


def _cp(n_axes):
    return pltpu.CompilerParams(dimension_semantics=("arbitrary",) * n_axes, vmem_limit_bytes=VMEM_LIMIT_BYTES)


def _resident(shape, n_axes):
    zeros = (0,) * len(shape)
    index = (lambda i: zeros) if n_axes == 1 else (lambda i, j: zeros)
    return pl.BlockSpec(shape, index, pipeline_mode=pl.Buffered(1))


def _dg(a, b, ca, cb):
    nb = a.ndim - 2
    dims = (((ca + nb,), (cb + nb,)), (tuple(range(nb)), tuple(range(nb))))
    return lax.dot_general(a.astype(BF16), b.astype(BF16), dims, preferred_element_type=F32)


@jax.custom_vjp
def _nn(a, b):
    return _dg(a, b, 1, 0)


@jax.custom_vjp
def _nt(a, b):
    return _dg(a, b, 1, 1)


@jax.custom_vjp
def _tn(a, b):
    return _dg(a, b, 0, 0)


@jax.custom_vjp
def _nt16(a, b):
    return _dg(a, b, 1, 1)


def _dot_vjp(fn, ca, cb, da, db, operand_dtype=F32):
    def fwd(a, b):
        a, b = a.astype(BF16), b.astype(BF16)
        return _dg(a, b, ca, cb), (a, b)

    def bwd(res, g):
        a, b = res
        g = g.astype(BF16)
        grad = lambda other, dims, g_first: _dg(g, other, *dims) if g_first else _dg(other, g, *dims)
        return grad(b, *da).astype(operand_dtype), grad(a, *db).astype(operand_dtype)

    fn.defvjp(fwd, bwd)


_dot_vjp(_nn, 1, 0, ((1, 1), True), ((0, 0), False))
_dot_vjp(_nt, 1, 1, ((1, 0), True), ((0, 0), True))
_dot_vjp(_tn, 0, 0, ((1, 1), False), ((1, 0), False))
_dot_vjp(_nt16, 1, 1, ((1, 0), True), ((0, 0), True), BF16)


def _split_dot(m, a, parts):
    mb = jnp.broadcast_to(m, a.shape[:-2] + m.shape)
    total, rest = None, a
    for _ in range(parts):
        term = rest.astype(BF16)
        rest = rest - term.astype(F32)
        product = _dg(mb, term, 1, 0)
        total = product if total is None else total + product
    return total


def _make_cum(parts):
    @jax.custom_vjp
    def cum(m, mt, a):
        return _split_dot(m, a, parts)

    cum.defvjp(lambda m, mt, a: (_split_dot(m, a, parts), (m, mt)),
               lambda res, g: (jnp.zeros_like(res[0]), jnp.zeros_like(res[1]), _split_dot(res[1], g, parts)))
    return cum


_cum = _make_cum(3)
_cum16 = _make_cum(2)


def _sigmoid(x):
    return 1.0 / (1.0 + jnp.exp(-x))


@jax.custom_vjp
def _log_sigmoid(x):
    return jnp.minimum(x, 0.0) - jnp.log(1.0 + jnp.exp(-jnp.abs(x)))


_log_sigmoid.defvjp(lambda x: (_log_sigmoid(x), x), lambda x, g: (g * _sigmoid(-x),))


def _rms(x):
    return lax.rsqrt(jnp.mean(x * x, axis=-1, keepdims=True) + EPS)


def _rmsnorm_bwd(dy, x, gain):
    r = _rms(x)
    xhat = x * r
    dxh = dy * gain
    return r * (dxh - xhat * jnp.mean(dxh * xhat, axis=-1, keepdims=True)), xhat


def _norm_proj(h, gain, w, name):
    tp, d = h.shape
    s, _, ns = w.shape

    tm = TM_RESIDENT

    def body(h_ref, g_ref, w_ref, hn_ref, u_ref):
        x = h_ref[...]
        a = (x * _rms(x) * g_ref[...]).astype(BF16)
        hn_ref[...] = a
        for k in range(s):
            u_ref[:, ns * k:ns * (k + 1)] = jnp.dot(a, w_ref[k], preferred_element_type=F32).astype(BF16)

    return pl.pallas_call(
        body, name=name, grid=(tp // tm,),
        in_specs=[pl.BlockSpec((tm, d), lambda i: (i, 0)), pl.BlockSpec((1, d), lambda i: (0, 0)), _resident(w.shape, 1)],
        out_specs=[pl.BlockSpec((tm, d), lambda i: (i, 0)), pl.BlockSpec((tm, s * ns), lambda i: (i, 0))],
        out_shape=[jax.ShapeDtypeStruct((tp, d), BF16), jax.ShapeDtypeStruct((tp, s * ns), BF16)],
        compiler_params=_cp(1))(h, gain, w)


def _norm_ffn_in(h, gain, w, name):
    tp, d = h.shape
    ff = w.shape[1] // 2
    tm = TM_RESIDENT
    blocks = [(c, min(c + 6 * MXU_TILE, ff)) for c in range(0, ff, 6 * MXU_TILE)]

    def body(h_ref, g_ref, w_ref, hn_ref, dg_ref, du_ref, act_ref):
        x = h_ref[...]
        a = (x * _rms(x) * g_ref[...]).astype(BF16)
        hn_ref[...] = a
        for c0, c1 in blocks:
            g = jnp.dot(a, w_ref[:, c0:c1], preferred_element_type=F32)
            u = jnp.dot(a, w_ref[:, ff + c0:ff + c1], preferred_element_type=F32)
            sg = _sigmoid(g)
            silu = g * sg
            dg_ref[:, c0:c1] = (u * (sg + silu * (1.0 - sg))).astype(BF16)
            du_ref[:, c0:c1] = silu.astype(BF16)
            act_ref[:, c0:c1] = (silu * u).astype(BF16)

    wide = jax.ShapeDtypeStruct((tp, ff), BF16)
    return pl.pallas_call(
        body, name=name, grid=(tp // tm,),
        in_specs=[pl.BlockSpec((tm, d), lambda i: (i, 0)), pl.BlockSpec((1, d), lambda i: (0, 0)),
                  _resident(w.shape, 1)],
        out_specs=[pl.BlockSpec((tm, d), lambda i: (i, 0))] + [pl.BlockSpec((tm, ff), lambda i: (i, 0))] * 3,
        out_shape=[jax.ShapeDtypeStruct((tp, d), BF16), wide, wide, wide],
        compiler_params=_cp(1))(h, gain, w)


def _out_proj(a, w, h, scale, name):
    tp, k = a.shape
    d = w.shape[1]

    def body(a_ref, w_ref, h_ref, o_ref):
        o_ref[...] = h_ref[...] + scale * jnp.dot(a_ref[...], w_ref[...], preferred_element_type=F32)

    return pl.pallas_call(
        body, name=name, grid=(tp // TM,),
        in_specs=[pl.BlockSpec((TM, k), lambda i: (i, 0)), pl.BlockSpec((k, d), lambda i: (0, 0)),
                  pl.BlockSpec((TM, d), lambda i: (i, 0))],
        out_specs=pl.BlockSpec((TM, d), lambda i: (i, 0)),
        out_shape=jax.ShapeDtypeStruct((tp, d), F32),
        compiler_params=_cp(1))(a, w, h)


def _out_proj_bwd(dh, w, on, name):
    tp, d = dh.shape
    k = w.shape[0]
    steps = tp // TM

    def body(dh_ref, w_ref, on_ref, don_ref, dw_ref, acc_ref):
        i = pl.program_id(0)

        @pl.when(i == 0)
        def _():
            acc_ref[...] = jnp.zeros_like(acc_ref)

        g = dh_ref[...].astype(BF16)
        don_ref[...] = lax.dot_general(g, w_ref[...], (((1,), (1,)), ((), ())), preferred_element_type=F32).astype(BF16)
        acc_ref[...] += lax.dot_general(on_ref[...], g, (((0,), (0,)), ((), ())), preferred_element_type=F32)

        @pl.when(i == steps - 1)
        def _():
            dw_ref[...] = acc_ref[...].astype(BF16)

    return pl.pallas_call(
        body, name=name, grid=(steps,),
        in_specs=[pl.BlockSpec((TM, d), lambda i: (i, 0)), _resident(w.shape, 1), pl.BlockSpec((TM, k), lambda i: (i, 0))],
        out_specs=[pl.BlockSpec((TM, k), lambda i: (i, 0)), pl.BlockSpec((k, d), lambda i: (0, 0))],
        out_shape=[jax.ShapeDtypeStruct((tp, k), BF16), jax.ShapeDtypeStruct((k, d), BF16)],
        scratch_shapes=[pltpu.VMEM((k, d), F32)],
        compiler_params=_cp(1))(dh, w, on)


def _wgrad(a, b, *, bm, bn, scale, sharded, name):
    tp, m = a.shape
    n = b.shape[1]
    nk = tp // TM

    def body(a_ref, b_ref, o_ref, acc_ref):
        k = pl.program_id(2)

        @pl.when(k == 0)
        def _():
            acc_ref[...] = jnp.zeros_like(acc_ref)

        bb = b_ref[...]
        if scale != 1.0:
            bb = scale * bb
        acc_ref[...] += lax.dot_general(a_ref[...], bb.astype(BF16), (((0,), (0,)), ((), ())),
                                        preferred_element_type=F32)

        @pl.when(k == nk - 1)
        def _():
            o_ref[...] = acc_ref[...].astype(BF16)

    if sharded:
        assert m == bm
        out_spec = pl.BlockSpec((None, bm, bn), lambda i, j, k: (j, 0, 0))
        out_shape = jax.ShapeDtypeStruct((n // bn, m, bn), BF16)
    else:
        out_spec = pl.BlockSpec((bm, bn), lambda i, j, k: (i, j))
        out_shape = jax.ShapeDtypeStruct((m, n), BF16)
    return pl.pallas_call(
        body, name=name, grid=(m // bm, n // bn, nk),
        in_specs=[pl.BlockSpec((TM, bm), lambda i, j, k: (k, i)), pl.BlockSpec((TM, bn), lambda i, j, k: (k, j))],
        out_specs=out_spec, out_shape=out_shape,
        scratch_shapes=[pltpu.VMEM((bm, bn), F32)],
        compiler_params=_cp(3))(a, b)


def _dgrad_norm(du, w, h, gain, dh_out, name):
    tp, d = h.shape
    s, _, ns = w.shape
    tm = TM_RESIDENT

    def body(du_ref, w_ref, h_ref, g_ref, dho_ref, dhi_ref, dg_ref):
        @pl.when(pl.program_id(0) == 0)
        def _():
            dg_ref[...] = jnp.zeros_like(dg_ref)

        dhn = None
        for k in range(s):
            part = lax.dot_general(du_ref[:, ns * k:ns * (k + 1)], w_ref[k], (((1,), (1,)), ((), ())),
                                   preferred_element_type=F32)
            dhn = part if dhn is None else dhn + part
        dx, xhat = _rmsnorm_bwd(dhn, h_ref[...], g_ref[...])
        dg_ref[...] += jnp.sum(dhn * xhat, axis=0, keepdims=True)
        dhi_ref[...] = dho_ref[...] + dx

    return pl.pallas_call(
        body, name=name, grid=(tp // tm,),
        in_specs=[pl.BlockSpec((tm, s * ns), lambda i: (i, 0)), _resident(w.shape, 1),
                  pl.BlockSpec((tm, d), lambda i: (i, 0)), pl.BlockSpec((1, d), lambda i: (0, 0)),
                  pl.BlockSpec((tm, d), lambda i: (i, 0))],
        out_specs=[pl.BlockSpec((tm, d), lambda i: (i, 0)), pl.BlockSpec((1, d), lambda i: (0, 0))],
        out_shape=[jax.ShapeDtypeStruct((tp, d), F32), jax.ShapeDtypeStruct((1, d), F32)],
        compiler_params=_cp(1))(du, w, h, gain, dh_out)


def _loss_head(act, w_out, h, gain, target, name):
    tp, d = h.shape
    ff = act.shape[1]
    tm = TM_SMALL
    front_tiles = FRONT // tm

    def body(a_ref, w_ref, h_ref, g_ref, t_ref, dh_ref, dg_ref, loss_ref):
        i = pl.program_id(0)

        @pl.when(i == 0)
        def _():
            dg_ref[...] = jnp.zeros_like(dg_ref)
            loss_ref[...] = jnp.zeros_like(loss_ref)

        x = h_ref[...] + 0.5 * jnp.dot(a_ref[...], w_ref[...], preferred_element_type=F32)
        gain_v = g_ref[...]
        y = x * _rms(x) * gain_v
        err = jnp.where(i >= front_tiles, y - t_ref[...], 0.0)
        loss_ref[...] += 0.5 * jnp.sum(jnp.mean(err * err, axis=-1, keepdims=True), axis=0, keepdims=True)
        dy = err * (1.0 / d)
        dx, xhat = _rmsnorm_bwd(dy, x, gain_v)
        dg_ref[...] += jnp.sum(dy * xhat, axis=0, keepdims=True)
        dh_ref[...] = dx

    return pl.pallas_call(
        body, name=name, grid=(tp // tm,),
        in_specs=[pl.BlockSpec((tm, ff), lambda i: (i, 0)), _resident(w_out.shape, 1),
                  pl.BlockSpec((tm, d), lambda i: (i, 0)), pl.BlockSpec((1, d), lambda i: (0, 0)),
                  pl.BlockSpec((tm, d), lambda i: (jnp.maximum(i - front_tiles, 0), 0))],
        out_specs=[pl.BlockSpec((tm, d), lambda i: (i, 0)), pl.BlockSpec((1, d), lambda i: (0, 0)),
                   pl.BlockSpec((1, 128), lambda i: (0, 0))],
        out_shape=[jax.ShapeDtypeStruct((tp, d), F32), jax.ShapeDtypeStruct((1, d), F32),
                   jax.ShapeDtypeStruct((1, 128), F32)],
        compiler_params=_cp(1))(act, w_out, h, gain, target)


@jax.custom_vjp
def _gated_headnorm(o, g, gain):
    return o * _rms(o) * gain * (g * _sigmoid(g))


def _gated_headnorm_bwd(res, dy):
    o, g, gain = res
    r = _rms(o)
    ohat = o * r
    sg = _sigmoid(g)
    silu = g * sg
    dy_ohat = dy * ohat
    d_ohat = dy * gain * silu
    d_o = r * (d_ohat - ohat * jnp.mean(d_ohat * ohat, axis=-1, keepdims=True))
    d_g = dy_ohat * gain * (sg + silu * (1.0 - sg))
    d_gain = jnp.sum(dy_ohat * silu, axis=-2, keepdims=True)
    return d_o, d_g, d_gain


_gated_headnorm.defvjp(lambda o, g, gain: (_gated_headnorm(o, g, gain), (o, g, gain)), _gated_headnorm_bwd)


def _row_mask(chunk, size=CHUNK):
    rows = chunk * size + lax.broadcasted_iota(jnp.int32, (size, 1), 0)
    return (rows >= FRONT - N_META).astype(F32)


def _ret_head(q1, q2, k1, k2, v, g, state, gain, cos, sin, dmat, dq, dk, dc):
    q = jnp.concatenate([q1 * cos - q2 * sin, q1 * sin + q2 * cos], axis=-1)
    k = jnp.concatenate([k1 * cos - k2 * sin, k1 * sin + k2 * cos], axis=-1) * (RET_DK ** -0.5)
    scores = _nt(q, k) * dmat
    o = _nn(scores, v) + _nn(q * dq, state)
    new_state = state * dc + _tn(k * dk, v)
    return _gated_headnorm(o, g, gain), new_state


def _ret_consts():
    log_gamma = jnp.log1p(-2.0 ** (-5.0 - jnp.arange(HEADS, dtype=F32)))
    idx = jnp.arange(RET_CHUNK, dtype=F32)
    rel = idx[:, None] - idx[None, :]
    dmat = jnp.where(rel >= 0, jnp.exp(log_gamma[:, None, None] * jnp.maximum(rel, 0.0)), 0.0)
    dq = jnp.exp(log_gamma[:, None] * (idx + 1.0))[..., None]
    dk = jnp.exp(log_gamma[:, None] * (RET_CHUNK - 1.0 - idx))[..., None]
    dc = jnp.broadcast_to(jnp.exp(log_gamma * RET_CHUNK)[:, None, None], (HEADS, 1, 128))
    return dmat, dq, dk, dc


def _rope_tables(tp):
    half = RET_DK // 2
    inv = 1.0 / (ROPE_BASE ** jnp.linspace(0.0, 1.0, half, dtype=F32))
    pos = (jnp.arange(tp) - (FRONT - N_META)).astype(F32)
    ang = pos[:, None] * inv[None, :]
    return jnp.cos(ang), jnp.sin(ang)


_RET_V0, _RET_G0 = 2 * D, 4 * D


def _heads(ref, start, width, stride=None, rows=slice(None)):
    stride = width if stride is None else stride
    return jnp.stack([ref[rows, start + stride * h:start + stride * h + width].astype(F32) for h in range(HEADS)])


def _put_heads(ref, start, value, mask, stride=None, rows=slice(None)):
    width = value.shape[-1]
    stride = width if stride is None else stride
    for h in range(HEADS):
        ref[rows, start + stride * h:start + stride * h + width] = (value[h] * mask).astype(ref.dtype)


def _ret_pieces(u_ref):
    hk = RET_DK // 2
    return (_heads(u_ref, 0, hk, RET_DK), _heads(u_ref, hk, hk, RET_DK), _heads(u_ref, D, hk, RET_DK),
            _heads(u_ref, D + hk, hk, RET_DK), _heads(u_ref, _RET_V0, RET_DV), _heads(u_ref, _RET_G0, RET_DV))


def _ret_const_specs(rev=None):
    c = (lambda n: (rev(n), 0)) if rev else (lambda n: (n, 0))
    z3 = lambda n: (0, 0, 0)
    return [pl.BlockSpec((RET_CHUNK, RET_DK // 2), c), pl.BlockSpec((RET_CHUNK, RET_DK // 2), c),
            pl.BlockSpec((HEADS, RET_CHUNK, RET_CHUNK), z3), pl.BlockSpec((HEADS, RET_CHUNK, 1), z3),
            pl.BlockSpec((HEADS, RET_CHUNK, 1), z3), pl.BlockSpec((HEADS, 1, 128), z3)]


def _ret_fwd(u, gain, rope, h, w_out, name):
    tp = u.shape[0]
    nch = tp // RET_CHUNK
    cos, sin = rope
    dmat, dq, dk, dc = _ret_consts()

    def body(u_ref, gain_ref, h_ref, w_ref, cos_ref, sin_ref, dmat_ref, dq_ref, dk_ref, dc_ref,
             on_ref, st_ref, hmix_ref, state_ref):
        @pl.when(pl.program_id(0) == 0)
        def _():
            state_ref[...] = jnp.zeros_like(state_ref)

        state = state_ref[...]
        st_ref[...] = state.astype(BF16)
        on, new_state = _ret_head(*_ret_pieces(u_ref), state, _heads(gain_ref, 0, RET_DV), cos_ref[...], sin_ref[...],
                                  dmat_ref[...], dq_ref[...], dk_ref[...], dc_ref[...][:, :, :1])
        state_ref[...] = new_state
        _put_heads(on_ref, 0, on, 1.0)
        hmix_ref[...] = h_ref[...] + jnp.dot(on_ref[...], w_ref[...], preferred_element_type=F32)

    rows = lambda width: pl.BlockSpec((RET_CHUNK, width), lambda n: (n, 0))
    return pl.pallas_call(
        body, name=name, grid=(nch,),
        in_specs=[rows(6 * D), pl.BlockSpec((1, HEADS * RET_DV), lambda n: (0, 0)), rows(D),
                  _resident(w_out.shape, 1)] + _ret_const_specs(),
        out_specs=[rows(HEADS * RET_DV), pl.BlockSpec((None, HEADS, RET_DK, RET_DV), lambda n: (n, 0, 0, 0)), rows(D)],
        out_shape=[jax.ShapeDtypeStruct((tp, HEADS * RET_DV), BF16),
                   jax.ShapeDtypeStruct((nch, HEADS, RET_DK, RET_DV), BF16), jax.ShapeDtypeStruct((tp, D), F32)],
        scratch_shapes=[pltpu.VMEM((HEADS, RET_DK, RET_DV), F32)],
        compiler_params=_cp(1))(u, gain, h, w_out, cos, sin, dmat, dq, dk, dc)


def _ret_bwd(u, gain, rope, states, d_on, name):
    tp = u.shape[0]
    nch = tp // RET_CHUNK
    cos, sin = rope
    dmat, dq, dk, dc = _ret_consts()
    rev = lambda n: nch - 1 - n
    hk = RET_DK // 2

    def body(u_ref, gain_ref, st_ref, don_ref, cos_ref, sin_ref, dmat_ref, dq_ref, dk_ref, dc_ref,
             du_ref, dgain_ref, dstate_ref):
        @pl.when(pl.program_id(0) == 0)
        def _():
            dstate_ref[...] = jnp.zeros_like(dstate_ref)
            dgain_ref[...] = jnp.zeros_like(dgain_ref)

        mask = _row_mask(rev(pl.program_id(0)), RET_CHUNK)
        consts = (cos_ref[...], sin_ref[...], dmat_ref[...], dq_ref[...], dk_ref[...], dc_ref[...][:, :, :1])
        _, vjp = jax.vjp(lambda *a: _ret_head(*a, *consts), *_ret_pieces(u_ref), st_ref[...].astype(F32),
                         _heads(gain_ref, 0, RET_DV))
        dq1, dq2, dk1, dk2, dv, dg, dstate, dgain = vjp((_heads(don_ref, 0, RET_DV), dstate_ref[...]))
        dstate_ref[...] = dstate
        for hd in range(HEADS):
            dgain_ref[:, RET_DV * hd:RET_DV * (hd + 1)] += dgain[hd]
        _put_heads(du_ref, 0, dq1, mask, RET_DK)
        _put_heads(du_ref, hk, dq2, mask, RET_DK)
        _put_heads(du_ref, D, dk1, mask, RET_DK)
        _put_heads(du_ref, D + hk, dk2, mask, RET_DK)
        _put_heads(du_ref, _RET_V0, dv, mask)
        _put_heads(du_ref, _RET_G0, dg, mask)

    return pl.pallas_call(
        body, name=name, grid=(nch,),
        in_specs=[pl.BlockSpec((RET_CHUNK, 6 * D), lambda n: (rev(n), 0)),
                  pl.BlockSpec((1, HEADS * RET_DV), lambda n: (0, 0)),
                  pl.BlockSpec((None, HEADS, RET_DK, RET_DV), lambda n: (rev(n), 0, 0, 0)),
                  pl.BlockSpec((RET_CHUNK, HEADS * RET_DV), lambda n: (rev(n), 0))] + _ret_const_specs(rev),
        out_specs=[pl.BlockSpec((RET_CHUNK, 6 * D), lambda n: (rev(n), 0)),
                   pl.BlockSpec((1, HEADS * RET_DV), lambda n: (0, 0))],
        out_shape=[jax.ShapeDtypeStruct((tp, 6 * D), BF16), jax.ShapeDtypeStruct((1, HEADS * RET_DV), F32)],
        scratch_shapes=[pltpu.VMEM((HEADS, RET_DK, RET_DV), F32)],
        compiler_params=_cp(1))(u, gain, states, d_on, cos, sin, dmat, dq, dk, dc)


_GLA_K0, _GLA_V0, _GLA_G0, _GLA_Z0 = 512, 1024, 2048, 3072


def _gla_head(q, k, v, g, z, state_t, wg, bg, gain, mask, lo, lo_t, to_mid, to_mid_t, in_second, pair):
    ga = _nn(jnp.broadcast_to(z, wg.shape[:-2] + z.shape), wg) + bg
    log_a = _log_sigmoid(ga) * (mask * (1.0 / GLA_TAU))
    bcum = _cum(lo, lo_t, log_a)
    btot = jnp.sum(log_a, axis=-2, keepdims=True)
    qs = q * (GLA_DK ** -0.5)
    heads, levels = q.shape[0], pair.shape[0]
    decay = jnp.exp(_cum16(to_mid, to_mid_t, log_a).reshape(heads, levels, CHUNK, GLA_DK))
    qk = jnp.where(in_second > 0.0, qs.astype(BF16)[:, None], k.astype(BF16)[:, None]) * decay.astype(BF16)
    qk = qk.reshape(heads * levels, CHUNK, GLA_DK)
    rows = lax.broadcasted_iota(jnp.int32, (CHUNK, CHUNK), 0)
    cols = lax.broadcasted_iota(jnp.int32, (CHUNK, CHUNK), 1)
    scores = (jnp.where(rows == cols, _nt(qs, k), 0.0)
              + jnp.sum(_nt16(qk, qk).reshape(heads, levels, CHUNK, CHUNK) * pair, axis=1))
    o = _nn(scores, v) + _nt(qs * jnp.exp(bcum), state_t)
    new_state_t = state_t * jnp.exp(btot) + _tn(v, k * jnp.exp(btot - bcum))
    return o * _rms(o) * gain * (g * _sigmoid(g)), new_state_t


def _gla_consts():
    r, c = np.meshgrid(np.arange(CHUNK), np.arange(CHUNK), indexing="ij")
    to_mid, second, pair = [], [], []
    block = 2
    while block <= CHUNK:
        mid = (r // block) * block + block // 2
        to_mid.append(((r >= mid) & (c > mid) & (c <= r)) | ((r < mid) & (c > r) & (c <= mid)))
        second.append((r >= mid)[:, :1])
        pair.append((r // block == c // block) & (r >= mid) & (c < mid))
        block *= 2
    to_mid = np.concatenate(to_mid)
    bf = lambda m: jnp.asarray(m, F32).astype(BF16)
    f32 = lambda ms: jnp.asarray(np.stack(ms), F32)
    return bf(r >= c), bf(c >= r), bf(to_mid), bf(to_mid.T), f32(second), f32(pair)


def _gla_const_specs(consts):
    return [pl.BlockSpec(a.shape, functools.partial(lambda nd, n: (0,) * nd, a.ndim)) for a in consts]


GLA_STEP_CHUNKS = 4


def _gla_pieces(u_ref, rows):
    return (_heads(u_ref, 0, GLA_DK, rows=rows), _heads(u_ref, _GLA_K0, GLA_DK, rows=rows),
            _heads(u_ref, _GLA_V0, GLA_DV, rows=rows), _heads(u_ref, _GLA_G0, GLA_DV, rows=rows),
            u_ref[rows, _GLA_Z0:_GLA_Z0 + 128].astype(F32))


def _gla_fwd(u, wg, bg, gain, name):
    tp = u.shape[0]
    nch = tp // CHUNK
    per = GLA_STEP_CHUNKS
    consts = _gla_consts()

    def body(u_ref, wg_ref, bg_ref, gain_ref, *refs):
        const_refs, (on_ref, st_ref, state_ref) = refs[:len(consts)], refs[len(consts):]

        @pl.when(pl.program_id(0) == 0)
        def _():
            state_ref[...] = jnp.zeros_like(state_ref)

        params = (_heads(wg_ref, 0, GLA_DK), _heads(bg_ref, 0, GLA_DK), _heads(gain_ref, 0, GLA_DV))
        mats = [ref[...] for ref in const_refs]
        state = state_ref[...]
        for c in range(per):
            rows = slice(CHUNK * c, CHUNK * (c + 1))
            st_ref[c] = state.astype(BF16)
            on, state = _gla_head(*_gla_pieces(u_ref, rows), state, *params, _row_mask(pl.program_id(0) * per + c), *mats)
            _put_heads(on_ref, 0, on, 1.0, rows=rows)
        state_ref[...] = state

    rows_spec = lambda width: pl.BlockSpec((per * CHUNK, width), lambda n: (n, 0))
    full = lambda r, c: pl.BlockSpec((r, c), lambda n: (0, 0))
    return pl.pallas_call(
        body, name=name, grid=(nch // per,),
        in_specs=[rows_spec(GLA_U), full(128, HEADS * GLA_DK), full(1, HEADS * GLA_DK), full(1, HEADS * GLA_DV)]
                 + _gla_const_specs(consts),
        out_specs=[rows_spec(HEADS * GLA_DV), pl.BlockSpec((per, HEADS, GLA_DV, GLA_DK), lambda n: (n, 0, 0, 0))],
        out_shape=[jax.ShapeDtypeStruct((tp, HEADS * GLA_DV), BF16),
                   jax.ShapeDtypeStruct((nch, HEADS, GLA_DV, GLA_DK), BF16)],
        scratch_shapes=[pltpu.VMEM((HEADS, GLA_DV, GLA_DK), F32)],
        compiler_params=_cp(1))(u, wg, bg, gain, *consts)


def _gla_bwd(u, wg, bg, gain, states, d_on, name):
    tp = u.shape[0]
    per = GLA_STEP_CHUNKS
    steps = tp // (per * CHUNK)
    rev = lambda n: steps - 1 - n
    consts = _gla_consts()

    def body(u_ref, wg_ref, bg_ref, gain_ref, st_ref, don_ref, *refs):
        const_refs, (du_ref, dwg_ref, dbg_ref, dgain_ref, dstate_ref) = refs[:len(consts)], refs[len(consts):]

        @pl.when(pl.program_id(0) == 0)
        def _():
            dstate_ref[...] = jnp.zeros_like(dstate_ref)
            dwg_ref[...] = jnp.zeros_like(dwg_ref)
            dbg_ref[...] = jnp.zeros_like(dbg_ref)
            dgain_ref[...] = jnp.zeros_like(dgain_ref)

        params = (_heads(wg_ref, 0, GLA_DK), _heads(bg_ref, 0, GLA_DK), _heads(gain_ref, 0, GLA_DV))
        mats = [ref[...] for ref in const_refs]
        dstate = dstate_ref[...]
        for c in reversed(range(per)):
            rows = slice(CHUNK * c, CHUNK * (c + 1))
            mask = _row_mask(rev(pl.program_id(0)) * per + c)
            _, vjp = jax.vjp(lambda *a: _gla_head(*a, mask, *mats), *_gla_pieces(u_ref, rows),
                             st_ref[c].astype(F32), *params)
            dq, dk, dv, dg, dz, dstate, dwg, dbg, dgain = vjp((_heads(don_ref, 0, GLA_DV, rows=rows), dstate))
            for hd in range(HEADS):
                dwg_ref[:, GLA_DK * hd:GLA_DK * (hd + 1)] += dwg[hd]
                dbg_ref[:, GLA_DK * hd:GLA_DK * (hd + 1)] += dbg[hd]
                dgain_ref[:, GLA_DV * hd:GLA_DV * (hd + 1)] += dgain[hd]
            _put_heads(du_ref, 0, dq, mask, rows=rows)
            _put_heads(du_ref, _GLA_K0, dk, mask, rows=rows)
            _put_heads(du_ref, _GLA_V0, dv, mask, rows=rows)
            _put_heads(du_ref, _GLA_G0, dg, mask, rows=rows)
            du_ref[rows, _GLA_Z0:_GLA_Z0 + 128] = dz.astype(BF16)
            du_ref[rows, _GLA_Z0 + 128:] = jnp.zeros((CHUNK, GLA_U - _GLA_Z0 - 128), BF16)
        dstate_ref[...] = dstate

    full = lambda r, c: pl.BlockSpec((r, c), lambda n: (0, 0))
    return pl.pallas_call(
        body, name=name, grid=(steps,),
        in_specs=[pl.BlockSpec((per * CHUNK, GLA_U), lambda n: (rev(n), 0)), full(128, HEADS * GLA_DK),
                  full(1, HEADS * GLA_DK), full(1, HEADS * GLA_DV),
                  pl.BlockSpec((per, HEADS, GLA_DV, GLA_DK), lambda n: (rev(n), 0, 0, 0)),
                  pl.BlockSpec((per * CHUNK, HEADS * GLA_DV), lambda n: (rev(n), 0))] + _gla_const_specs(consts),
        out_specs=[pl.BlockSpec((per * CHUNK, GLA_U), lambda n: (rev(n), 0)), full(128, HEADS * GLA_DK),
                   full(1, HEADS * GLA_DK), full(1, HEADS * GLA_DV)],
        out_shape=[jax.ShapeDtypeStruct((tp, GLA_U), BF16), jax.ShapeDtypeStruct((128, HEADS * GLA_DK), F32),
                   jax.ShapeDtypeStruct((1, HEADS * GLA_DK), F32), jax.ShapeDtypeStruct((1, HEADS * GLA_DV), F32)],
        scratch_shapes=[pltpu.VMEM((HEADS, GLA_DV, GLA_DK), F32)],
        compiler_params=_cp(1))(u, wg, bg, gain, states, d_on, *consts)


def _ffn_fwd(h, gain, w_in, w_out, tag):
    hn, ug, uu, act = _norm_ffn_in(h, gain, w_in, f"{tag}_in")
    if callable(w_out):
        w_out = w_out(act)
    return _out_proj(act, w_out, h, 0.5, f"{tag}_out"), (h, hn, ug, uu, act), w_out


def _ffn_dgrad(dh, w_out, w_in, act_dg, act_du, h, gain, name, split_front=False):
    tp, d = dh.shape
    ff = w_out.shape[0]
    tm = TM_SMALL
    nt = (((1,), (1,)), ((), ()))

    def body(dh_ref, wo_ref, wi_ref, dg_ref, du_ref, h_ref, g_ref, o_ref, *out_refs):
        dhi_ref, dgain_ref = out_refs[-2:]

        @pl.when(pl.program_id(0) == 0)
        def _():
            dgain_ref[...] = jnp.zeros_like(dgain_ref)

        dho = dh_ref[...]
        dact = lax.dot_general((0.5 * dho).astype(BF16), wo_ref[...], nt, preferred_element_type=F32)
        d_gate = (dact * dg_ref[...].astype(F32)).astype(BF16)
        d_up = (dact * du_ref[...].astype(F32)).astype(BF16)
        o_ref[:, :ff] = d_gate
        o_ref[:, ff:] = d_up
        dhn = (lax.dot_general(d_gate, wi_ref[:, :ff], nt, preferred_element_type=F32)
               + lax.dot_general(d_up, wi_ref[:, ff:], nt, preferred_element_type=F32))
        dx, xhat = _rmsnorm_bwd(dhn, h_ref[...], g_ref[...])
        dgain_ref[...] += jnp.sum(dhn * xhat, axis=0, keepdims=True)
        dhi_ref[...] = dho + dx
        if split_front:
            @pl.when(pl.program_id(0) == 0)
            def _():
                out_refs[0][...] = dho + dx

    rows = lambda width: pl.BlockSpec((tm, width), lambda i: (i, 0))
    if split_front:
        assert tm == FRONT
        dhi_specs = [pl.BlockSpec((tm, d), lambda i: (0, 0)), pl.BlockSpec((tm, d), lambda i: (jnp.maximum(i - 1, 0), 0))]
        dhi_shapes = [jax.ShapeDtypeStruct((FRONT, d), F32), jax.ShapeDtypeStruct((tp - FRONT, d), F32)]
    else:
        dhi_specs, dhi_shapes = [rows(d)], [jax.ShapeDtypeStruct((tp, d), F32)]
    out = pl.pallas_call(
        body, name=name, grid=(tp // tm,),
        in_specs=[rows(d), _resident(w_out.shape, 1), _resident(w_in.shape, 1), rows(ff), rows(ff), rows(d),
                  pl.BlockSpec((1, d), lambda i: (0, 0))],
        out_specs=[rows(2 * ff), *dhi_specs, pl.BlockSpec((1, d), lambda i: (0, 0))],
        out_shape=[jax.ShapeDtypeStruct((tp, 2 * ff), BF16), *dhi_shapes, jax.ShapeDtypeStruct((1, d), F32)],
        compiler_params=_cp(1))(dh, w_out, w_in, act_dg, act_du, h, gain)
    return (out[0], tuple(out[1:3]), out[3]) if split_front else tuple(out)


def _ffn_bwd(dh, saved, gain, w_in, w_out, tag, push, split_front=False):
    h, hn, act_dg, act_du, act = saved
    du, dh_in, d_gain = _ffn_dgrad(dh, w_out, w_in, act_dg, act_du, h, gain, f"{tag}_dgrad", split_front)
    d_w_out = _wgrad(act, dh, bm=D_FF // 2, bn=D, scale=0.5, sharded=False, name=f"{tag}_dwout")
    d_w_in = _wgrad(hn, du, bm=D, bn=D_FF, scale=1.0, sharded=False, name=f"{tag}_dwin")
    return dh_in, d_gain, push([("cols", d_w_in), d_w_out])


def _sequence_grads(x, target, p, weights, grads):
    row = lambda v, token: v.reshape(1, -1) + token[0, 0]
    gains = {}

    tok = weights.start(1, weights.start(0, None))
    weights.pin = tok
    h = jnp.concatenate([jnp.zeros((FRONT, D), F32), x], axis=0) + tok[0, 0]
    rope = _rope_tables(h.shape[0])
    w = weights.wait(0, [tok, h, *rope, *weights.later_shards(2)])
    tok = weights.start(2, w["l0_ffn1_in"])
    h = lax.dynamic_update_slice(h, w["meta"], (FRONT - N_META, 0))
    gains["l0_ffn1"] = row(p["norm_ffn1"][0], tok)
    h, s1, w["l0_ffn1_out"] = _ffn_fwd(h, gains["l0_ffn1"], w["l0_ffn1_in"],
                                       lambda act: weights.wait(1, act)["l0_ffn1_out"], "l0_ffn1")
    w.update(weights.wait(2, h))
    tok = weights.start(4, weights.start(3, w["ret_in"]))
    gains["ret"] = row(p["norm_mix"][0], tok)
    hn, u = _norm_proj(h, gains["ret"], w["ret_in"], "ret_in")
    w.update(weights.wait(3, u))
    on, states, h_mix = _ret_fwd(u, w["ret_gain"], rope, h, w["ret_out"], "ret_fwd")
    s2 = (h, hn, u, on, states)
    w.update(weights.wait(4, h_mix))
    tok = weights.start(5, w["l0_ffn2_in"])
    gains["l0_ffn2"] = row(p["norm_ffn2"][0], tok)
    h, s3, _ = _ffn_fwd(h_mix, gains["l0_ffn2"], w["l0_ffn2_in"], w["l0_ffn2_out"], "l0_ffn2")
    saved = [(s1, s2, s3)]

    w.update(weights.wait(5, h))
    tok = weights.start(6, w["l1_ffn1_in"])
    gains["l1_ffn1"] = row(p["norm_ffn1"][1], tok)
    h, s1, _ = _ffn_fwd(h, gains["l1_ffn1"], w["l1_ffn1_in"], w["l1_ffn1_out"], "l1_ffn1")
    w.update(weights.wait(6, h))
    tok = weights.start(7, w["gla_out"])
    gains["gla"] = row(p["norm_mix"][1], tok)
    hn, u = _norm_proj(h, gains["gla"], w["gla_in"], "gla_in")
    on, states = _gla_fwd(u, w["gla_wg"], w["gla_bg"], w["gla_gain"], "gla_fwd")
    h_mix = _out_proj(on, w["gla_out"], h, 1.0, "gla_out")
    s2 = (h, hn, u, on, states)
    w.update(weights.wait(7, h_mix))
    gains["l1_ffn2"] = p["norm_ffn2"][1].reshape(1, -1)
    s3 = (h_mix, *_norm_ffn_in(h_mix, gains["l1_ffn2"], w["l1_ffn2_in"], "l1_ffn2_in"))
    saved.append((s1, s2, s3))

    dh, d_final, loss = _loss_head(s3[-1], w["l1_ffn2_out"], h_mix, p["final_norm"].reshape(1, -1), target,
                                   "l1_ffn2_out_loss")
    small = {"final_norm": d_final, "norm_ffn1": [None, None], "norm_mix": [None, None], "norm_ffn2": [None, None]}
    pusher = lambda k: functools.partial(grads.push, k)

    s1, s2, s3 = saved[1]
    dh, small["norm_ffn2"][1], tok = _ffn_bwd(dh, s3, gains["l1_ffn2"], w["l1_ffn2_in"], w["l1_ffn2_out"], "l1_ffn2",
                                              pusher(0))
    h_in, hn, u, on, states = s2
    d_on, d_out = _out_proj_bwd(dh, w["gla_out"], on, "gla_out_bwd")
    du, small["gla_wg"], small["gla_bg"], small["gla_gain"] = _gla_bwd(
        u, w["gla_wg"], w["gla_bg"], w["gla_gain"] + tok[0, 0], states, d_on, "gla_bwd")
    d_in = _wgrad(hn, du, bm=D, bn=GLA_U, scale=1.0, sharded=False, name="gla_dwin")
    d_in = jnp.moveaxis(d_in[:, :GLA_IN].reshape(D, N_CHIPS, -1), 1, 0)
    tok = grads.push(1, [d_in, d_out])
    dh, small["norm_mix"][1] = _dgrad_norm(du, w["gla_in"], h_in, gains["gla"] + tok[0, 0], dh, "gla_dnorm")
    dh, small["norm_ffn1"][1], tok = _ffn_bwd(dh, s1, gains["l1_ffn1"], w["l1_ffn1_in"], w["l1_ffn1_out"], "l1_ffn1",
                                              pusher(2))

    s1, s2, s3 = saved[0]
    dh, small["norm_ffn2"][0], tok = _ffn_bwd(dh, s3, gains["l0_ffn2"] + tok[0, 0], w["l0_ffn2_in"],
                                              w["l0_ffn2_out"], "l0_ffn2", pusher(3))
    h_in, hn, u, on, states = s2
    d_on, d_out = _out_proj_bwd(dh, w["ret_out"], on, "ret_out_bwd")
    du, small["ret_gain"] = _ret_bwd(u, w["ret_gain"] + tok[0, 0], rope, states, d_on, "ret_bwd")
    d_in = _wgrad(hn, du, bm=D, bn=w["ret_in"].shape[2], scale=1.0, sharded=True, name="ret_dwin")
    tok = grads.push(4, [d_in, d_out])
    dh, small["norm_mix"][0] = _dgrad_norm(du, w["ret_in"], h_in, gains["ret"] + tok[0, 0], dh, "ret_dnorm")
    (d_front, d_x), small["norm_ffn1"][0], tok = _ffn_bwd(dh, s1, gains["l0_ffn1"], w["l0_ffn1_in"], w["l0_ffn1_out"],
                                                          "l0_ffn1", pusher(5), split_front=True)
    grads.push(6, [], [d_front[FRONT - N_META:], *small["norm_ffn1"], *small["norm_mix"], *small["norm_ffn2"],
                       small["final_norm"], small["ret_gain"], small["gla_wg"][:GLA_RANK], small["gla_bg"],
                       small["gla_gain"], loss[:, :1] + tok[0, 0]])
    return d_x


_HBM = pl.BlockSpec(memory_space=pl.ANY)


def _place():
    return lax.axis_index("x"), lax.axis_index("y"), lax.axis_index("c")


def _flip(v, bit):
    return 1 - v if bit else v


DMA_CHUNK_BYTES = 128 * 1024


def _row_chunks(ref):
    rows, cols = ref.shape
    step = _row_tile(rows, max(16, DMA_CHUNK_BYTES // (cols * ref.dtype.itemsize)))
    return [pl.ds(a, step) for a in range(0, rows, step)]


def _whole(src, dst, send_sem, recv_sem, peer):
    return pltpu.make_async_remote_copy(src_ref=src, dst_ref=dst, send_sem=send_sem, recv_sem=recv_sem,
                                        device_id=peer, device_id_type=MESH)


def _send(src, dst, send_sem, recv_sem, peer):
    for rows in _row_chunks(src):
        _whole(src.at[rows], dst.at[rows], send_sem, recv_sem, peer).start()
    return _whole(src, dst, send_sem, recv_sem, peer)


_HBM_ONLY = pl.BlockSpec(memory_space=pltpu.HBM)
_SEMS = pl.BlockSpec(memory_space=pltpu.SEMAPHORE)
_SIDE_EFFECT = pltpu.CompilerParams(has_side_effects=pltpu.SideEffectType.DATAFLOW_SIDE_EFFECTING)
_GATHER_FLIPS = [(1, 0, 0), (0, 1, 0), (1, 1, 0), (0, 0, 1)]
_PEER_FLIPS = [(fx, fy, fc) for fx in (0, 1) for fy in (0, 1) for fc in (0, 1)][1:]


def _zero_token():
    return jnp.zeros((8, 128), F32)


def _exchange_start(srcs, lands, route, flips, after, name):
    n = len(srcs)

    def body(*refs):
        src, land = refs[:n], refs[n:2 * n]
        send_sems, recv_sems, token = refs[2 * n + 1], refs[2 * n + 2], refs[-1]
        me = _place()
        for t in range(n):
            for j, flip in enumerate(flips):
                peer = tuple(_flip(v, f) for v, f in zip(me, flip))
                s, d = route(t, src[t], land[t], me, peer)
                _send(s, d, send_sems.at[t * len(flips) + j], recv_sems.at[t * len(flips) + j], peer)
        token[...] = jnp.zeros_like(token)

    hbm = lambda a: pltpu.HBM(a.shape, a.dtype)
    sems = pltpu.SemaphoreType.DMA((n * len(flips),))
    operands = [pltpu.with_memory_space_constraint(a, pltpu.HBM) for a in list(srcs) + list(lands)]
    out = pl.pallas_call(
        body, name=name, in_specs=[_HBM_ONLY] * (2 * n) + [_HBM],
        out_shape=(sems, sems, *[hbm(a) for a in operands], jax.ShapeDtypeStruct((8, 128), F32)),
        out_specs=(_SEMS, _SEMS, *[_HBM_ONLY] * (2 * n), pl.BlockSpec(memory_space=pltpu.VMEM)),
        input_output_aliases={i: 2 + i for i in range(2 * n)}, compiler_params=_SIDE_EFFECT,
    )(*operands, _zero_token() if after is None else after)
    return (out[0], out[1], out[2:2 + n], out[2 + n:2 + 2 * n]), out[-1]


def _exchange_wait(started, route, flips, after, name):
    send_sems, recv_sems, srcs, lands = started
    n = len(srcs)

    def body(*refs):
        src, land = refs[:n], refs[n:2 * n]
        send_sems, recv_sems = refs[2 * n], refs[2 * n + 1]
        me = _place()
        for t in range(n):
            for j, flip in enumerate(flips):
                peer = tuple(_flip(v, f) for v, f in zip(me, flip))
                s, d = route(t, src[t], land[t], me, peer)
                cp = _whole(s, d, send_sems.at[t * len(flips) + j], recv_sems.at[t * len(flips) + j], peer)
                cp.wait_send()
                cp.wait_recv()

    hbm = lambda a: pltpu.HBM(a.shape, a.dtype)
    after = list(after) if isinstance(after, (list, tuple)) else [after]
    out = pl.pallas_call(
        body, name=name, in_specs=[_HBM_ONLY] * (2 * n) + [_SEMS, _SEMS] + [_HBM] * len(after),
        out_shape=tuple(hbm(a) for a in list(srcs) + list(lands)), out_specs=tuple([_HBM_ONLY] * (2 * n)),
        input_output_aliases={i: i for i in range(2 * n)}, compiler_params=_SIDE_EFFECT,
    )(*srcs, *lands, send_sems, recv_sems, *after)
    return out[:n], out[n:]


def _gather_route(t, src, land, me, peer):
    mine = 2 * me[0] + me[1]
    if land.ndim == 3:
        return src, land.at[mine]
    cols = src.shape[1]
    return src, land.at[:, pl.ds(pl.multiple_of(mine * cols, 128), cols)]


def _scatter_route(n_pieces):
    def route(t, src, land, me, peer):
        chip = 2 * peer[0] + peer[1]
        if t >= n_pieces:
            part = src
        elif src.ndim == 4:
            part = src.at[chip, peer[2]]
        else:
            rows, cols = land.shape[1:]
            part = src.at[pl.ds(pl.multiple_of(peer[2] * rows, 16), rows), pl.ds(pl.multiple_of(chip * cols, 128), cols)]
        return part, land.at[4 * me[0] + 2 * me[1] + me[2]]

    return route


def _swap_cores(halves, name):
    n = len(halves)

    def body(*refs):
        src, dst = refs[:n], refs[n:2 * n]
        send_sems, recv_sems = refs[2 * n:]
        x, y, c = _place()
        copies = [_send(src[t], dst[t], send_sems.at[t], recv_sems.at[t], (x, y, 1 - c)) for t in range(n)]
        for cp in copies:
            cp.wait()

    got = pl.pallas_call(
        body, name=name, in_specs=[_HBM] * n, out_specs=[_HBM] * n,
        out_shape=[jax.ShapeDtypeStruct(a.shape, a.dtype) for a in halves],
        scratch_shapes=[pltpu.SemaphoreType.DMA((n,)), pltpu.SemaphoreType.DMA((n,))],
    )(*halves)
    south = lax.axis_index("c") == 0
    return [jnp.stack([jnp.where(south, a, b), jnp.where(south, b, a)]) for a, b in zip(halves, got)]


def _row_tile(rows, cap):
    fits = [t for t in range(16, cap + 1, 16) if rows % t == 0]
    return fits[-1] if fits else rows


def _sum_slots(a, name):
    _, r, c = a.shape
    tr = _row_tile(r, 384)

    def body(a_ref, o_ref):
        s = a_ref[0].astype(F32)
        for k in range(1, N_DEV):
            s = s + a_ref[k].astype(F32)
        o_ref[...] = s

    return pl.pallas_call(
        body, name=name, grid=(r // tr,),
        in_specs=[pl.BlockSpec((N_DEV, tr, c), lambda i: (0, i, 0))],
        out_specs=pl.BlockSpec((tr, c), lambda i: (i, 0)),
        out_shape=jax.ShapeDtypeStruct((r, c), F32),
        compiler_params=_cp(1))(a)


def _adamw(w, g, m, v, name):
    layers, r, c = w.shape
    tr = _row_tile(r, 256)

    def body(w_ref, g_ref, m_ref, v_ref, d_ref, nm_ref, nv_ref):
        gv = g_ref[...]
        nm = ADAM_B1 * m_ref[...] + (1.0 - ADAM_B1) * gv
        nv = ADAM_B2 * v_ref[...] + (1.0 - ADAM_B2) * (gv * gv)
        m_hat = nm / (1.0 - ADAM_B1 ** ADAM_STEP)
        v_hat = nv / (1.0 - ADAM_B2 ** ADAM_STEP)
        d_ref[...] = -ADAM_LR * (m_hat / (jnp.sqrt(v_hat) + ADAM_EPS) + ADAM_WD * w_ref[...])
        nm_ref[...] = nm
        nv_ref[...] = nv

    spec = pl.BlockSpec((None, tr, c), lambda a, i: (a, i, 0))
    return pl.pallas_call(
        body, name=name, grid=(layers, r // tr), in_specs=[spec] * 4, out_specs=[spec] * 3,
        out_shape=[jax.ShapeDtypeStruct((layers, r, c), F32)] * 3,
        compiler_params=_cp(2))(*[pltpu.with_memory_space_constraint(a, pltpu.HBM) for a in (w, g, m, v)])


_SMALL = ["meta_tokens", "ret_head_norm", "gla_w_gate", "gla_b_gate", "gla_head_norm"]
_LOCAL_SMALL = ["meta_tokens", "norm_ffn1", "norm_mix", "norm_ffn2", "ret_head_norm", "gla_w_gate", "gla_b_gate",
                "gla_head_norm", "final_norm"]
_WEIGHTS = ["meta_tokens", "norm_ffn1", "ffn1_w_in", "ffn1_w_out", "norm_mix", "norm_ffn2", "ffn2_w_in", "ffn2_w_out",
            "ret_w_in", "ret_head_norm", "ret_w_out", "gla_w_in", "gla_w_gate", "gla_b_gate", "gla_head_norm",
            "gla_w_out", "final_norm"]


def _pack_rows(arrays, width):
    flat = jnp.concatenate([a.reshape(-1) for a in arrays])
    pad = -flat.shape[0] % (8 * width)
    return jnp.pad(flat, (0, pad)).reshape(-1, width)


def _unpack_rows(packed, shapes):
    flat, out, at = packed.reshape(-1), [], 0
    for s in shapes:
        size = 1
        for dim in s:
            size *= dim
        out.append(flat[at:at + size].reshape(s))
        at += size
    return out


class _WeightGather:
    GROUPS = [("small", "l0_ffn1_in"), ("l0_ffn1_out",), ("ret_in",), ("ret_out",), ("l0_ffn2_in", "l0_ffn2_out"),
              ("l1_ffn1_in", "l1_ffn1_out"), ("gla_in", "gla_out"), ("l1_ffn2_in", "l1_ffn2_out")]

    def __init__(self, p):
        self.small_shapes = [p[name].shape for name in _SMALL]
        self.f32 = {"small": _pack_rows([p[name] for name in _SMALL], 128), "ret_in": p["ret_w_in"][0],
                    "ret_out": p["ret_w_out"][0], "gla_in": p["gla_w_in"][0], "gla_out": p["gla_w_out"][0]}
        for layer in range(2):
            for name in ("ffn1", "ffn2"):
                self.f32[f"l{layer}_{name}_in"] = p[f"{name}_w_in"][layer]
                self.f32[f"l{layer}_{name}_out"] = p[f"{name}_w_out"][layer]
        self.shards = {}
        self.started = {}
        self.pin = None

    def shard(self, name):
        if name not in self.shards:
            a = self.f32[name]
            if name != "small":
                a = (a if self.pin is None else a + self.pin[0, 0]).astype(BF16)
            self.shards[name] = a
        return self.shards[name]

    def later_shards(self, k):
        return [self.shard(name) for group in self.GROUPS[k:] for name in group]

    def start(self, k, after):
        shards = [self.shard(name) for name in self.GROUPS[k]]
        lands = []
        for name, s in zip(self.GROUPS[k], shards):
            if "ffn" in name and name.endswith("_in"):
                lands.append(lax.empty((s.shape[0], N_CHIPS * s.shape[1]), s.dtype))
            else:
                lands.append(lax.empty((N_CHIPS,) + s.shape, s.dtype))
        self.started[k], token = _exchange_start(shards, lands, _gather_route, _GATHER_FLIPS, after, f"gather{k}_start")
        return token

    def wait(self, k, after):
        _, got = _exchange_wait(self.started[k], _gather_route, _GATHER_FLIPS, after, f"gather{k}_wait")
        w = {}
        for name, g in zip(self.GROUPS[k], got):
            if name == "small":
                parts = zip(*[_unpack_rows(g[chip], self.small_shapes) for chip in range(N_CHIPS)])
                cat = lambda a: jnp.moveaxis(a, 0, -2).reshape(a.shape[1:-1] + (-1,))
                meta, ret_gain, wg, bg, gla_gain = [cat(jnp.stack(part)) for part in parts]
                w.update(meta=meta, ret_gain=ret_gain.reshape(1, -1), gla_bg=bg.reshape(1, -1),
                         gla_gain=gla_gain.reshape(1, -1),
                         gla_wg=jnp.pad(wg[0], ((0, 128 - GLA_RANK), (0, 0))).astype(BF16))
            elif name == "gla_in":
                full = jnp.moveaxis(g, 0, 1).reshape(D, -1)
                w[name] = jnp.pad(full, ((0, 0), (0, GLA_U - GLA_IN)))[None]
            elif name.endswith("_out"):
                w[name] = g.reshape(-1, g.shape[-1])
            else:
                w[name] = g
        return w


class _GradExchange:
    def __init__(self):
        self.started = []
        self.token = None
        self.small_shapes = None

    def push(self, k, arrays, small=None):
        srcs, lands = [], []
        for a in arrays:
            if isinstance(a, tuple):
                a = a[1]
                piece = (a.shape[0] // 2, a.shape[1] // N_CHIPS)
            else:
                a = a.reshape(N_CHIPS, 2, -1, a.shape[-1])
                piece = a.shape[2:]
            srcs.append(a)
            lands.append(lax.empty((N_DEV,) + piece, a.dtype))
        if small is not None:
            self.small_shapes = [a.shape for a in small]
            srcs.append(_pack_rows(small, D))
            lands.append(lax.empty((N_DEV,) + srcs[-1].shape, F32))
        started, self.token = _exchange_start(srcs, lands, _scatter_route(len(arrays)), _PEER_FLIPS, None,
                                              f"scatter{k}_start")
        self.started.append((started, len(arrays)))
        return self.token

    def collect(self, groups, after=None):
        x, y, c = _place()
        after, sums = self.token if after is None else after, []
        for k in groups:
            started, n_pieces = self.started[k]
            srcs, got = _exchange_wait(started, _scatter_route(n_pieces), _PEER_FLIPS, after, f"scatter{k}_wait")
            own = []
            for t, (a, g) in enumerate(zip(srcs, got)):
                if t >= n_pieces:
                    own.append(a)
                elif a.ndim == 4:
                    own.append(a[2 * x + y, c])
                else:
                    rows, cols = g.shape[1:]
                    own.append(lax.dynamic_slice(a, (c * rows, (2 * x + y) * cols), (rows, cols)))
            got = [lax.dynamic_update_index_in_dim(g, a, 4 * x + 2 * y + c, 0) for g, a in zip(got, own)]
            sums.append([_sum_slots(a, f"sum{k}_{i}") for i, a in enumerate(got)])
            after = sums[-1][0]
        return sums


def kernel(x, meta_tokens, norm_ffn1, ffn1_w_in, ffn1_w_out, norm_mix, norm_ffn2, ffn2_w_in, ffn2_w_out, ret_w_in, ret_head_norm, ret_w_out, gla_w_in, gla_w_gate, gla_b_gate, gla_head_norm, gla_w_out, final_norm, loss_target, m_meta_tokens, m_norm_ffn1, m_ffn1_w_in, m_ffn1_w_out, m_norm_mix, m_norm_ffn2, m_ffn2_w_in, m_ffn2_w_out, m_ret_w_in, m_ret_head_norm, m_ret_w_out, m_gla_w_in, m_gla_w_gate, m_gla_b_gate, m_gla_head_norm, m_gla_w_out, m_final_norm, v_meta_tokens, v_norm_ffn1, v_ffn1_w_in, v_ffn1_w_out, v_norm_mix, v_norm_ffn2, v_ffn2_w_in, v_ffn2_w_out, v_ret_w_in, v_ret_head_norm, v_ret_w_out, v_gla_w_in, v_gla_w_gate, v_gla_b_gate, v_gla_head_norm, v_gla_w_out, v_final_norm):
    p = dict(meta_tokens=meta_tokens, norm_ffn1=norm_ffn1, ffn1_w_in=ffn1_w_in, ffn1_w_out=ffn1_w_out, norm_mix=norm_mix,
             norm_ffn2=norm_ffn2, ffn2_w_in=ffn2_w_in, ffn2_w_out=ffn2_w_out, ret_w_in=ret_w_in,
             ret_head_norm=ret_head_norm, ret_w_out=ret_w_out, gla_w_in=gla_w_in, gla_w_gate=gla_w_gate,
             gla_b_gate=gla_b_gate, gla_head_norm=gla_head_norm, gla_w_out=gla_w_out, final_norm=final_norm)
    m = dict(zip(_WEIGHTS, (m_meta_tokens, m_norm_ffn1, m_ffn1_w_in, m_ffn1_w_out, m_norm_mix, m_norm_ffn2, m_ffn2_w_in,
                            m_ffn2_w_out, m_ret_w_in, m_ret_head_norm, m_ret_w_out, m_gla_w_in, m_gla_w_gate,
                            m_gla_b_gate, m_gla_head_norm, m_gla_w_out, m_final_norm)))
    v = dict(zip(_WEIGHTS, (v_meta_tokens, v_norm_ffn1, v_ffn1_w_in, v_ffn1_w_out, v_norm_mix, v_norm_ffn2, v_ffn2_w_in,
                            v_ffn2_w_out, v_ret_w_in, v_ret_head_norm, v_ret_w_out, v_gla_w_in, v_gla_w_gate,
                            v_gla_b_gate, v_gla_head_norm, v_gla_w_out, v_final_norm)))

    exchange = _GradExchange()
    d_x = _sequence_grads(x[0], loss_target[0], p, _WeightGather(p), exchange)
    names = [("ffn2_w_in", 1), ("ffn2_w_out", 1), ("gla_w_in", 0), ("gla_w_out", 0), ("ffn1_w_in", 1), ("ffn1_w_out", 1),
             ("ffn2_w_in", 0), ("ffn2_w_out", 0), ("ret_w_in", 0), ("ret_w_out", 0), ("ffn1_w_in", 0), ("ffn1_w_out", 0)]
    shard, grads, delta, new_m, new_v = {}, {}, {}, {}, {}

    def swap(sums, keys, name):
        for key, a in zip(keys, _swap_cores(sums, name)):
            shard[key] = a.reshape(-1, a.shape[-1])

    def update(name):
        layers = p[name].shape[0]
        grads[name] = jnp.stack([shard[name, layer] for layer in range(layers)])
        delta[name], new_m[name], new_v[name] = _adamw(p[name], grads[name], m[name], v[name], f"adamw_{name}")

    swap([a for group in exchange.collect(range(5)) for a in group], names[:10], "swap_first")
    for name in ("ffn2_w_in", "ffn2_w_out", "ret_w_in", "ret_w_out", "gla_w_in", "gla_w_out"):
        update(name)
    last, (small_sum,) = exchange.collect([5, 6], after=list(delta.values()))
    swap(last, names[10:], "swap_last")
    for name in ("ffn1_w_in", "ffn1_w_out"):
        update(name)

    chip = 2 * lax.axis_index("x") + lax.axis_index("y")
    cols = lambda a, n: lax.dynamic_slice_in_dim(a, chip * n, n, axis=a.ndim - 1)
    (s_meta, s_n1a, s_n1b, s_nma, s_nmb, s_n2a, s_n2b, s_final, s_ret_gain, s_wg, s_bg, s_gla_gain,
     s_loss) = _unpack_rows(small_sum, exchange.small_shapes)
    grads.update({
        "meta_tokens": cols(s_meta, 256), "norm_ffn1": jnp.concatenate([s_n1a, s_n1b]),
        "norm_mix": jnp.concatenate([s_nma, s_nmb]), "norm_ffn2": jnp.concatenate([s_n2a, s_n2b]),
        "final_norm": s_final.reshape(D),
        "ret_head_norm": cols(s_ret_gain.reshape(1, HEADS, RET_DV), RET_DV // N_CHIPS),
        "gla_w_gate": cols(s_wg, GLA_DK)[None], "gla_b_gate": cols(s_bg, GLA_DK),
        "gla_head_norm": cols(s_gla_gain.reshape(1, HEADS, GLA_DV), GLA_DV // N_CHIPS),
    })
    for name in _LOCAL_SMALL:
        shape = p[name].shape
        as3d = lambda a: a.reshape((1,) * (3 - len(shape)) + shape)
        out = _adamw(as3d(p[name]), as3d(grads[name]), as3d(m[name]), as3d(v[name]), f"adamw_{name}")
        delta[name], new_m[name], new_v[name] = [a.reshape(shape) for a in out]

    return (s_loss.reshape(()), d_x[None], *[grads[n] for n in _WEIGHTS], *[delta[n] for n in _WEIGHTS],
            *[new_m[n] for n in _WEIGHTS], *[new_v[n] for n in _WEIGHTS])
```
